```python
import jax, jax.numpy as jnp
from jax import lax
import numpy as np

D_MODEL = 1024
BATCH = 8
SEQ = 8192
DEPTH = 2

N_MIXERS = 2
N_A = (DEPTH + 1) // 2
N_B = DEPTH // 2
HG_EXPAND = 128
HG_HEADS = D_MODEL // 128
HG_FDIM = HG_HEADS * HG_EXPAND
HG_VDIM = D_MODEL // HG_HEADS
HG_CHUNK = 64
FOX_HEAD_DIM = 64
FOX_HEADS = D_MODEL // FOX_HEAD_DIM
FOX_BLOCK = 128
D_FF = 4 * D_MODEL
N_MOD = 6
EPS = 1e-6

kernel_name = "hybrid_hgrn2_fox_adaln_block"


def rmsnorm(x, w):
    xf = x.astype(jnp.float32)
    y = xf * lax.rsqrt(jnp.mean(xf * xf, axis=-1, keepdims=True) + EPS)
    return (y * w.astype(jnp.float32)).astype(x.dtype)


def modulate(x, w, shift, scale):
    return rmsnorm(x, w) * (1 + scale[:, None, :]) + shift[:, None, :]


def hgrn2_mixer(h, w_in, w_out, lb, gn_w):
    B, S, _ = h.shape
    H, dk, dv, C = HG_HEADS, HG_EXPAND, HG_VDIM, HG_CHUNK
    N = S // C
    proj = h @ w_in
    q, fz, i, g = jnp.split(proj, [HG_FDIM, 2 * HG_FDIM, 2 * HG_FDIM + D_MODEL], axis=-1)
    q = jax.nn.silu(q.astype(jnp.float32))
    lbf = lb.astype(jnp.float32)
    logf = jnp.logaddexp(jnp.log(lbf), jnp.log1p(-lbf) + jax.nn.log_sigmoid(fz.astype(jnp.float32)))
    k = -jnp.expm1(logf)
    v = i.astype(jnp.float32)

    def to_chunks(t, d):
        return t.reshape(B, N, C, H, d).transpose(1, 0, 3, 2, 4)

    qc, kc, gc, vc = to_chunks(q, dk), to_chunks(k, dk), to_chunks(logf, dk), to_chunks(v, dv)
    causal = jnp.tril(jnp.ones((C, C), dtype=bool))

    def step(state, inp):
        qb, kb, gb, vb = inp
        G = jnp.cumsum(gb, axis=2)
        diff = G[:, :, :, None, :] - G[:, :, None, :, :]
        decay = jnp.exp(jnp.where(causal[:, :, None], diff, -jnp.inf))
        A = jnp.einsum('bhtk,bhsk,bhtsk->bhts', qb, kb, decay)
        o = jnp.einsum('bhts,bhsv->bhtv', A, vb) + jnp.einsum('bhtk,bhkv->bhtv', qb * jnp.exp(G), state)
        G_last = G[:, :, -1:, :]
        new_state = jnp.exp(G_last[:, :, 0, :])[..., None] * state + \
            jnp.einsum('bhsk,bhsv->bhkv', kb * jnp.exp(G_last - G), vb)
        return new_state, o

    state0 = jnp.zeros((B, H, dk, dv), jnp.float32)
    _, o = lax.scan(step, state0, (qc, kc, gc, vc))
    o = o.transpose(1, 0, 3, 2, 4).reshape(B, S, H, dv)
    o = rmsnorm(o, gn_w).reshape(B, S, D_MODEL)
    o = (o * jax.nn.silu(g.astype(jnp.float32))).astype(h.dtype)
    return o @ w_out


def fox_mixer(h, w_in, b_f, qn_w, kn_w, w_out):
    B, S, _ = h.shape
    H, dh, blk = FOX_HEADS, FOX_HEAD_DIM, FOX_BLOCK
    proj = h @ w_in
    q, k, v, g, fz = jnp.split(proj, [D_MODEL, 2 * D_MODEL, 3 * D_MODEL, 4 * D_MODEL], axis=-1)
    q = rmsnorm(q.reshape(B, S, H, dh), qn_w).transpose(0, 2, 1, 3)
    k = rmsnorm(k.reshape(B, S, H, dh), kn_w).transpose(0, 2, 1, 3)
    v = v.reshape(B, S, H, dh).transpose(0, 2, 1, 3)
    logf = jax.nn.log_sigmoid(fz.astype(jnp.float32) + b_f.astype(jnp.float32))
    F = jnp.cumsum(logf, axis=1).transpose(0, 2, 1)
    scale = 1.0 / np.sqrt(dh)
    kpos = jnp.arange(S)

    def block(bi):
        start = bi * blk
        qb = lax.dynamic_slice_in_dim(q, start, blk, axis=2)
        Fq = lax.dynamic_slice_in_dim(F, start, blk, axis=2)
        s = jnp.einsum('bhqd,bhkd->bhqk', qb, k).astype(jnp.float32) * scale
        s = s + (Fq[..., :, None] - F[..., None, :])
        qpos = start + jnp.arange(blk)
        s = jnp.where(kpos[None, :] <= qpos[:, None], s, -jnp.inf)
        p = jax.nn.softmax(s, axis=-1)
        return jnp.einsum('bhqk,bhkd->bhqd', p.astype(v.dtype), v)

    o = lax.map(block, jnp.arange(S // blk))
    o = o.transpose(1, 0, 3, 2, 4).reshape(B, S, D_MODEL)
    o = o * jax.nn.sigmoid(g)
    return o @ w_out


def sqrelu_mlp(h, w1, w2):
    a = jax.nn.relu(h @ w1)
    return (a * a) @ w2


def _fwd_setup_inputs(seed: int = 0) -> dict:
    key = jax.random.key(seed)
    ks = jax.random.split(key, 20)
    D, F = D_MODEL, HG_FDIM
    nrm = jax.random.normal
    return {
        "x": nrm(ks[0], (BATCH, SEQ, D), jnp.float32),
        "c": nrm(ks[1], (BATCH, D), jnp.float32),
        "w_mod": nrm(ks[2], (DEPTH, D, N_MOD * D), jnp.float32) * (0.5 * D ** -0.5),
        "b_mod": nrm(ks[3], (DEPTH, N_MOD * D), jnp.float32) * 0.02,
        "norm1_w": 1.0 + 0.05 * nrm(ks[4], (DEPTH, D), jnp.float32),
        "norm2_w": 1.0 + 0.05 * nrm(ks[5], (DEPTH, D), jnp.float32),
        "hg_w_in": nrm(ks[6], (N_A, D, 2 * F + 2 * D), jnp.float32) * D ** -0.5,
        "hg_w_out": nrm(ks[7], (N_A, D, D), jnp.float32) * D ** -0.5,
        "hg_lb": 0.5 * nrm(ks[8], (DEPTH + 1, F), jnp.float32),
        "hg_gn_w": 1.0 + 0.05 * nrm(ks[9], (N_A, HG_VDIM), jnp.float32),
        "fox_w_in": nrm(ks[10], (N_B, D, 4 * D + FOX_HEADS), jnp.float32) * D ** -0.5,
        "fox_b_f": jax.random.uniform(ks[11], (N_B, FOX_HEADS), jnp.float32, 1.0, 4.0),
        "fox_qn_w": 1.0 + 0.05 * nrm(ks[12], (N_B, FOX_HEAD_DIM), jnp.float32),
        "fox_kn_w": 1.0 + 0.05 * nrm(ks[13], (N_B, FOX_HEAD_DIM), jnp.float32),
        "fox_w_out": nrm(ks[14], (N_B, D, D), jnp.float32) * D ** -0.5,
        "mlp_w1": nrm(ks[15], (DEPTH, D, D_FF), jnp.float32) * D ** -0.5,
        "mlp_w2": nrm(ks[16], (DEPTH, D_FF, D), jnp.float32) * D_FF ** -0.5,
        "final_w": 1.0 + 0.05 * nrm(ks[17], (D,), jnp.float32),
    }


def _fwd_reference(x, c, w_mod, b_mod, norm1_w, norm2_w, hg_w_in, hg_w_out, hg_lb, hg_gn_w,
              fox_w_in, fox_b_f, fox_qn_w, fox_kn_w, fox_w_out, mlp_w1, mlp_w2, final_w):
    lb_all = jnp.cumsum(jax.nn.softmax(hg_lb.astype(jnp.float32), axis=0), axis=0)
    c_act = jax.nn.silu(c)
    for i in range(DEPTH):
        mod = c_act @ w_mod[i] + b_mod[i]
        sh1, sc1, g1, sh2, sc2, g2 = jnp.split(mod, N_MOD, axis=-1)
        h = modulate(x, norm1_w[i], sh1, sc1)
        j = i // N_MIXERS
        if i % N_MIXERS == 0:
            y = hgrn2_mixer(h, hg_w_in[j], hg_w_out[j], lb_all[i], hg_gn_w[j])
        else:
            y = fox_mixer(h, fox_w_in[j], fox_b_f[j], fox_qn_w[j], fox_kn_w[j], fox_w_out[j])
        x = x + g1[:, None, :] * y
        h = modulate(x, norm2_w[i], sh2, sc2)
        x = x + g2[:, None, :] * sqrelu_mlp(h, mlp_w1[i], mlp_w2[i])
    return rmsnorm(x, final_w)


import jax as _jax
import jax.numpy as _jnp

TWIN_FORMAT = 'train_step'
FWD_PARAMS = ['x', 'c', 'w_mod', 'b_mod', 'norm1_w', 'norm2_w', 'hg_w_in', 'hg_w_out', 'hg_lb', 'hg_gn_w', 'fox_w_in', 'fox_b_f', 'fox_qn_w', 'fox_kn_w', 'fox_w_out', 'mlp_w1', 'mlp_w2', 'final_w']
TWIN_WEIGHTS = ['w_mod', 'b_mod', 'norm1_w', 'norm2_w', 'hg_w_in', 'hg_w_out', 'hg_lb', 'hg_gn_w', 'fox_w_in', 'fox_b_f', 'fox_qn_w', 'fox_kn_w', 'fox_w_out', 'mlp_w1', 'mlp_w2', 'final_w']
TWIN_DIFF_INPUT = 'x'
TWIN_INPUTS = ['x', 'c', 'w_mod', 'b_mod', 'norm1_w', 'norm2_w', 'hg_w_in', 'hg_w_out', 'hg_lb', 'hg_gn_w', 'fox_w_in', 'fox_b_f', 'fox_qn_w', 'fox_kn_w', 'fox_w_out', 'mlp_w1', 'mlp_w2', 'final_w', 'loss_target', 'm_w_mod', 'm_b_mod', 'm_norm1_w', 'm_norm2_w', 'm_hg_w_in', 'm_hg_w_out', 'm_hg_lb', 'm_hg_gn_w', 'm_fox_w_in', 'm_fox_b_f', 'm_fox_qn_w', 'm_fox_kn_w', 'm_fox_w_out', 'm_mlp_w1', 'm_mlp_w2', 'm_final_w', 'v_w_mod', 'v_b_mod', 'v_norm1_w', 'v_norm2_w', 'v_hg_w_in', 'v_hg_w_out', 'v_hg_lb', 'v_hg_gn_w', 'v_fox_w_in', 'v_fox_b_f', 'v_fox_qn_w', 'v_fox_kn_w', 'v_fox_w_out', 'v_mlp_w1', 'v_mlp_w2', 'v_final_w']
TWIN_OUTPUTS = ['loss', 'grad_x', 'grad_w_mod', 'grad_b_mod', 'grad_norm1_w', 'grad_norm2_w', 'grad_hg_w_in', 'grad_hg_w_out', 'grad_hg_lb', 'grad_hg_gn_w', 'grad_fox_w_in', 'grad_fox_b_f', 'grad_fox_qn_w', 'grad_fox_kn_w', 'grad_fox_w_out', 'grad_mlp_w1', 'grad_mlp_w2', 'grad_final_w', 'delta_w_mod', 'delta_b_mod', 'delta_norm1_w', 'delta_norm2_w', 'delta_hg_w_in', 'delta_hg_w_out', 'delta_hg_lb', 'delta_hg_gn_w', 'delta_fox_w_in', 'delta_fox_b_f', 'delta_fox_qn_w', 'delta_fox_kn_w', 'delta_fox_w_out', 'delta_mlp_w1', 'delta_mlp_w2', 'delta_final_w', 'new_m_w_mod', 'new_m_b_mod', 'new_m_norm1_w', 'new_m_norm2_w', 'new_m_hg_w_in', 'new_m_hg_w_out', 'new_m_hg_lb', 'new_m_hg_gn_w', 'new_m_fox_w_in', 'new_m_fox_b_f', 'new_m_fox_qn_w', 'new_m_fox_kn_w', 'new_m_fox_w_out', 'new_m_mlp_w1', 'new_m_mlp_w2', 'new_m_final_w', 'new_v_w_mod', 'new_v_b_mod', 'new_v_norm1_w', 'new_v_norm2_w', 'new_v_hg_w_in', 'new_v_hg_w_out', 'new_v_hg_lb', 'new_v_hg_gn_w', 'new_v_fox_w_in', 'new_v_fox_b_f', 'new_v_fox_qn_w', 'new_v_fox_kn_w', 'new_v_fox_w_out', 'new_v_mlp_w1', 'new_v_mlp_w2', 'new_v_final_w']
TWIN_LEAF_KINDS = {'loss': 'loss', 'grad_x': 'grad_x', 'grad_w_mod': 'grad_w', 'grad_b_mod': 'grad_w', 'grad_norm1_w': 'grad_w', 'grad_norm2_w': 'grad_w', 'grad_hg_w_in': 'grad_w', 'grad_hg_w_out': 'grad_w', 'grad_hg_lb': 'grad_w', 'grad_hg_gn_w': 'grad_w', 'grad_fox_w_in': 'grad_w', 'grad_fox_b_f': 'grad_w', 'grad_fox_qn_w': 'grad_w', 'grad_fox_kn_w': 'grad_w', 'grad_fox_w_out': 'grad_w', 'grad_mlp_w1': 'grad_w', 'grad_mlp_w2': 'grad_w', 'grad_final_w': 'grad_w', 'delta_w_mod': 'delta_w', 'delta_b_mod': 'delta_w', 'delta_norm1_w': 'delta_w', 'delta_norm2_w': 'delta_w', 'delta_hg_w_in': 'delta_w', 'delta_hg_w_out': 'delta_w', 'delta_hg_lb': 'delta_w', 'delta_hg_gn_w': 'delta_w', 'delta_fox_w_in': 'delta_w', 'delta_fox_b_f': 'delta_w', 'delta_fox_qn_w': 'delta_w', 'delta_fox_kn_w': 'delta_w', 'delta_fox_w_out': 'delta_w', 'delta_mlp_w1': 'delta_w', 'delta_mlp_w2': 'delta_w', 'delta_final_w': 'delta_w', 'new_m_w_mod': 'new_m', 'new_m_b_mod': 'new_m', 'new_m_norm1_w': 'new_m', 'new_m_norm2_w': 'new_m', 'new_m_hg_w_in': 'new_m', 'new_m_hg_w_out': 'new_m', 'new_m_hg_lb': 'new_m', 'new_m_hg_gn_w': 'new_m', 'new_m_fox_w_in': 'new_m', 'new_m_fox_b_f': 'new_m', 'new_m_fox_qn_w': 'new_m', 'new_m_fox_kn_w': 'new_m', 'new_m_fox_w_out': 'new_m', 'new_m_mlp_w1': 'new_m', 'new_m_mlp_w2': 'new_m', 'new_m_final_w': 'new_m', 'new_v_w_mod': 'new_v', 'new_v_b_mod': 'new_v', 'new_v_norm1_w': 'new_v', 'new_v_norm2_w': 'new_v', 'new_v_hg_w_in': 'new_v', 'new_v_hg_w_out': 'new_v', 'new_v_hg_lb': 'new_v', 'new_v_hg_gn_w': 'new_v', 'new_v_fox_w_in': 'new_v', 'new_v_fox_b_f': 'new_v', 'new_v_fox_qn_w': 'new_v', 'new_v_fox_kn_w': 'new_v', 'new_v_fox_w_out': 'new_v', 'new_v_mlp_w1': 'new_v', 'new_v_mlp_w2': 'new_v', 'new_v_final_w': 'new_v'}


def _forward(args):
    return _fwd_reference(*[args[k] for k in FWD_PARAMS])


def _output_shape():
    def fwd():
        inp = _fwd_setup_inputs(0)
        return _fwd_reference(*[inp[k] for k in FWD_PARAMS])
    out = _jax.eval_shape(fwd)
    return out.shape, out.dtype

N_MICROBATCH = 1
ADAM_LR = 0.001
ADAM_B1 = 0.9
ADAM_B2 = 0.999
ADAM_EPS = 1e-08
ADAM_WD = 0.01
ADAM_STEP = 10
PER_EXAMPLE_BATCH_AXIS = {'x': 0, 'c': 0, 'loss_target': 0}
SHARED_INPUTS = []
_WEIGHT_DTYPES = {'w_mod': _jnp.float32, 'b_mod': _jnp.float32, 'norm1_w': _jnp.float32, 'norm2_w': _jnp.float32, 'hg_w_in': _jnp.float32, 'hg_w_out': _jnp.float32, 'hg_lb': _jnp.float32, 'hg_gn_w': _jnp.float32, 'fox_w_in': _jnp.float32, 'fox_b_f': _jnp.float32, 'fox_qn_w': _jnp.float32, 'fox_kn_w': _jnp.float32, 'fox_w_out': _jnp.float32, 'mlp_w1': _jnp.float32, 'mlp_w2': _jnp.float32, 'final_w': _jnp.float32}
MOMENT_SCALE = {'w_mod': 1.994371e-01, 'b_mod': 4.404857e-01, 'norm1_w': 5.058127e-02, 'norm2_w': 1.090918e-01, 'hg_w_in': 3.777206e-02, 'hg_w_out': 5.320890e-02, 'hg_lb': 2.188017e-03, 'hg_gn_w': 1.941836e-01, 'fox_w_in': 1.444996e-02, 'fox_b_f': 1.156344e-01, 'fox_qn_w': 4.213089e-02, 'fox_kn_w': 4.262561e-02, 'fox_w_out': 2.292767e-02, 'mlp_w1': 5.903066e-02, 'mlp_w2': 1.392930e-01, 'final_w': 6.457343e+01}


def _to_microbatches(a, axis):
    t = _jnp.moveaxis(a, axis, 0)
    t = t.reshape((N_MICROBATCH, t.shape[0] // N_MICROBATCH) + t.shape[1:])
    return _jnp.moveaxis(t, 1, axis + 1)


def setup_inputs(seed: int = 0) -> dict:
    inp = _fwd_setup_inputs(seed)
    key = _jax.random.fold_in(_jax.random.key(seed), 7919)
    shape, _ = _output_shape()
    out = dict(inp)
    out["loss_target"] = _jax.random.normal(_jax.random.fold_in(key, 0), shape, _jnp.float32)
    for i, name in enumerate(TWIN_WEIGHTS):
        w = inp[name].astype(_jnp.float32)
        if MOMENT_SCALE is None:
            s = _jnp.sqrt(_jnp.mean(_jnp.square(w)) + 1e-30)
        else:
            s = MOMENT_SCALE[name]
        km, kv = _jax.random.split(_jax.random.fold_in(key, i + 1))
        out[name] = w
        out["m_" + name] = s * _jax.random.normal(km, w.shape, _jnp.float32)
        out["v_" + name] = (s * s) * _jax.random.uniform(kv, w.shape, _jnp.float32, 0.5, 1.5)
    if N_MICROBATCH > 1:
        for name, axis in PER_EXAMPLE_BATCH_AXIS.items():
            out[name] = _to_microbatches(out[name], axis)
    return {'x': out['x'], 'c': out['c'], 'w_mod': out['w_mod'], 'b_mod': out['b_mod'], 'norm1_w': out['norm1_w'], 'norm2_w': out['norm2_w'], 'hg_w_in': out['hg_w_in'], 'hg_w_out': out['hg_w_out'], 'hg_lb': out['hg_lb'], 'hg_gn_w': out['hg_gn_w'], 'fox_w_in': out['fox_w_in'], 'fox_b_f': out['fox_b_f'], 'fox_qn_w': out['fox_qn_w'], 'fox_kn_w': out['fox_kn_w'], 'fox_w_out': out['fox_w_out'], 'mlp_w1': out['mlp_w1'], 'mlp_w2': out['mlp_w2'], 'final_w': out['final_w'], 'loss_target': out['loss_target'], 'm_w_mod': out['m_w_mod'], 'm_b_mod': out['m_b_mod'], 'm_norm1_w': out['m_norm1_w'], 'm_norm2_w': out['m_norm2_w'], 'm_hg_w_in': out['m_hg_w_in'], 'm_hg_w_out': out['m_hg_w_out'], 'm_hg_lb': out['m_hg_lb'], 'm_hg_gn_w': out['m_hg_gn_w'], 'm_fox_w_in': out['m_fox_w_in'], 'm_fox_b_f': out['m_fox_b_f'], 'm_fox_qn_w': out['m_fox_qn_w'], 'm_fox_kn_w': out['m_fox_kn_w'], 'm_fox_w_out': out['m_fox_w_out'], 'm_mlp_w1': out['m_mlp_w1'], 'm_mlp_w2': out['m_mlp_w2'], 'm_final_w': out['m_final_w'], 'v_w_mod': out['v_w_mod'], 'v_b_mod': out['v_b_mod'], 'v_norm1_w': out['v_norm1_w'], 'v_norm2_w': out['v_norm2_w'], 'v_hg_w_in': out['v_hg_w_in'], 'v_hg_w_out': out['v_hg_w_out'], 'v_hg_lb': out['v_hg_lb'], 'v_hg_gn_w': out['v_hg_gn_w'], 'v_fox_w_in': out['v_fox_w_in'], 'v_fox_b_f': out['v_fox_b_f'], 'v_fox_qn_w': out['v_fox_qn_w'], 'v_fox_kn_w': out['v_fox_kn_w'], 'v_fox_w_out': out['v_fox_w_out'], 'v_mlp_w1': out['v_mlp_w1'], 'v_mlp_w2': out['v_mlp_w2'], 'v_final_w': out['v_final_w']}


def _loss(weights, diff, rest, loss_target):
    with _jax.named_scope("forward"):
        args = {**rest, TWIN_DIFF_INPUT: diff, **{k: w.astype(_WEIGHT_DTYPES[k]) for k, w in weights.items()}}
        y = _forward(args)
    with _jax.named_scope("loss_head"):
        err = _jnp.square(y.astype(_jnp.float32) - loss_target)
        return 0.5 * _jnp.sum(_jnp.mean(err, axis=-1)) if err.ndim else 0.5 * err


def _adamw(w, g, m, v):
    m = ADAM_B1 * m + (1.0 - ADAM_B1) * g
    v = ADAM_B2 * v + (1.0 - ADAM_B2) * _jnp.square(g)
    m_hat = m / (1.0 - ADAM_B1 ** ADAM_STEP)
    v_hat = v / (1.0 - ADAM_B2 ** ADAM_STEP)
    delta = -ADAM_LR * (m_hat / (_jnp.sqrt(v_hat) + ADAM_EPS) + ADAM_WD * w)
    return delta, m, v


def reference(x, c, w_mod, b_mod, norm1_w, norm2_w, hg_w_in, hg_w_out, hg_lb, hg_gn_w, fox_w_in, fox_b_f, fox_qn_w, fox_kn_w, fox_w_out, mlp_w1, mlp_w2, final_w, loss_target, m_w_mod, m_b_mod, m_norm1_w, m_norm2_w, m_hg_w_in, m_hg_w_out, m_hg_lb, m_hg_gn_w, m_fox_w_in, m_fox_b_f, m_fox_qn_w, m_fox_kn_w, m_fox_w_out, m_mlp_w1, m_mlp_w2, m_final_w, v_w_mod, v_b_mod, v_norm1_w, v_norm2_w, v_hg_w_in, v_hg_w_out, v_hg_lb, v_hg_gn_w, v_fox_w_in, v_fox_b_f, v_fox_qn_w, v_fox_kn_w, v_fox_w_out, v_mlp_w1, v_mlp_w2, v_final_w):
    given = dict(x=x, c=c, w_mod=w_mod, b_mod=b_mod, norm1_w=norm1_w, norm2_w=norm2_w, hg_w_in=hg_w_in, hg_w_out=hg_w_out, hg_lb=hg_lb, hg_gn_w=hg_gn_w, fox_w_in=fox_w_in, fox_b_f=fox_b_f, fox_qn_w=fox_qn_w, fox_kn_w=fox_kn_w, fox_w_out=fox_w_out, mlp_w1=mlp_w1, mlp_w2=mlp_w2, final_w=final_w, loss_target=loss_target, m_w_mod=m_w_mod, m_b_mod=m_b_mod, m_norm1_w=m_norm1_w, m_norm2_w=m_norm2_w, m_hg_w_in=m_hg_w_in, m_hg_w_out=m_hg_w_out, m_hg_lb=m_hg_lb, m_hg_gn_w=m_hg_gn_w, m_fox_w_in=m_fox_w_in, m_fox_b_f=m_fox_b_f, m_fox_qn_w=m_fox_qn_w, m_fox_kn_w=m_fox_kn_w, m_fox_w_out=m_fox_w_out, m_mlp_w1=m_mlp_w1, m_mlp_w2=m_mlp_w2, m_final_w=m_final_w, v_w_mod=v_w_mod, v_b_mod=v_b_mod, v_norm1_w=v_norm1_w, v_norm2_w=v_norm2_w, v_hg_w_in=v_hg_w_in, v_hg_w_out=v_hg_w_out, v_hg_lb=v_hg_lb, v_hg_gn_w=v_hg_gn_w, v_fox_w_in=v_fox_w_in, v_fox_b_f=v_fox_b_f, v_fox_qn_w=v_fox_qn_w, v_fox_kn_w=v_fox_kn_w, v_fox_w_out=v_fox_w_out, v_mlp_w1=v_mlp_w1, v_mlp_w2=v_mlp_w2, v_final_w=v_final_w)
    weights = {n: given[n] for n in TWIN_WEIGHTS}
    shared = {n: given[n] for n in SHARED_INPUTS}
    per_example = {n: given[n] for n in ['x', 'c']}
    grad_fn = _jax.value_and_grad(_loss, argnums=(0, 1))

    def one_microbatch(ex, loss_target):
        ex = dict(ex)
        diff = ex.pop(TWIN_DIFF_INPUT)
        return grad_fn(weights, diff, {**shared, **ex}, loss_target)

    if N_MICROBATCH == 1:
        loss, (grad_w, grad_x) = one_microbatch(per_example, given["loss_target"])
    else:
        def body(carry, xs):
            loss_sum, grad_sum = carry
            l_k, (gw_k, gx_k) = one_microbatch(xs[0], xs[1])
            with _jax.named_scope("update"):
                return (loss_sum + l_k, _jax.tree.map(_jnp.add, grad_sum, gw_k)), gx_k

        init = (_jnp.zeros((), _jnp.float32), _jax.tree.map(_jnp.zeros_like, weights))
        (loss, grad_w), grad_x = _jax.lax.scan(body, init, (per_example, given["loss_target"]))
    with _jax.named_scope("update"):
        delta_w, new_m, new_v = {}, {}, {}
        for n in TWIN_WEIGHTS:
            delta_w[n], new_m[n], new_v[n] = _adamw(weights[n], grad_w[n], given["m_" + n], given["v_" + n])
    return (loss, grad_x, *[grad_w[n] for n in TWIN_WEIGHTS], *[delta_w[n] for n in TWIN_WEIGHTS],
            *[new_m[n] for n in TWIN_WEIGHTS], *[new_v[n] for n in TWIN_WEIGHTS])
```

```python
import math

import jax
import jax.numpy as jnp
from jax import lax
from jax.experimental import pallas as pl
from jax.experimental.pallas import tpu as pltpu

EPS = 1e-6
ADAM_LR, ADAM_B1, ADAM_B2, ADAM_EPS, ADAM_WD, ADAM_STEP = 0.001, 0.9, 0.999, 1e-08, 0.01, 10

F32 = jnp.float32
BF = jnp.bfloat16
LANES = 128
HG_CHUNK = 64
FOX_DH = 64
N_CHIPS = 4
N_DEV = 8
VMEM_LIMIT = 48 * 1024 * 1024
MESH = pl.DeviceIdType.MESH

NT = (((1,), (1,)), ((), ()))
TN = (((0,), (0,)), ((), ()))


def _pick(n, pref, mult=LANES):
    if n <= pref:
        return n
    t = (pref // mult) * mult
    while t >= mult:
        if n % t == 0:
            return t
        t -= mult
    raise ValueError((n, pref, mult))


def _cp(*sem):
    return pltpu.CompilerParams(dimension_semantics=sem, vmem_limit_bytes=VMEM_LIMIT)


def _dot(a, b):
    return jnp.dot(a, b, preferred_element_type=F32)


def _dg(a, b, dims):
    return lax.dot_general(a, b, dims, preferred_element_type=F32)


def _split3(x):
    hi = x.astype(BF)
    r1 = x - hi.astype(F32)
    mid = r1.astype(BF)
    lo = (r1 - mid.astype(F32)).astype(BF)
    return hi, mid, lo


def _tri_dot(tri, x):
    hi, mid, lo = _split3(x)
    return _dot(tri, hi) + _dot(tri, mid) + _dot(tri, lo)


def _dg3(a, b, dims):
    ah, bh = a.astype(BF), b.astype(BF)
    al, bl = (a - ah.astype(F32)).astype(BF), (b - bh.astype(F32)).astype(BF)
    return _dg(ah, bh, dims) + _dg(ah, bl, dims) + _dg(al, bh, dims)


NN = (((1,), (0,)), ((), ()))


def _sigmoid(x):
    return jax.nn.sigmoid(x)


def _ln_matmul(x, nw, sc, sh, w, *, relu2, name):
    S, D = x.shape
    N = w.shape[1]
    tm, tn = _pick(S, 512, 16), _pick(N, 512)

    def body(x_ref, nw_ref, sc_ref, sh_ref, w_ref, *rest):
        outs, hs = rest[:-1], rest[-1]
        h_ref = outs[-1]

        @pl.when(pl.program_id(1) == 0)
        def _():
            xv = x_ref[...]
            r = lax.rsqrt(jnp.mean(xv * xv, axis=-1, keepdims=True) + EPS)
            hb = ((xv * r * nw_ref[...]) * (1.0 + sc_ref[...]) + sh_ref[...]).astype(BF)
            hs[...] = hb
            h_ref[...] = hb

        z = _dot(hs[...], w_ref[...])
        if relu2:
            a = jnp.maximum(z, 0.0)
            outs[0][...] = a.astype(BF)
            outs[1][...] = (a * a).astype(BF)
        else:
            outs[0][...] = z

    vec = pl.BlockSpec((1, D), lambda i, j: (0, 0))
    tile = pl.BlockSpec((tm, tn), lambda i, j: (i, j))
    if relu2:
        out_shape = [jax.ShapeDtypeStruct((S, N), BF), jax.ShapeDtypeStruct((S, N), BF)]
        out_specs = [tile, tile]
    else:
        out_shape = [jax.ShapeDtypeStruct((S, N), F32)]
        out_specs = [tile]
    out_shape.append(jax.ShapeDtypeStruct((S, D), BF))
    out_specs.append(pl.BlockSpec((tm, D), lambda i, j: (i, 0)))
    return pl.pallas_call(
        body, name=name, grid=(S // tm, N // tn),
        in_specs=[pl.BlockSpec((tm, D), lambda i, j: (i, 0)), vec, vec, vec,
                  pl.BlockSpec((D, tn), lambda i, j: (0, j))],
        out_specs=out_specs, out_shape=out_shape,
        scratch_shapes=[pltpu.VMEM((tm, D), BF)],
        compiler_params=_cp("parallel", "arbitrary"),
    )(x, nw, sc, sh, w)


def _matmul_resid(a, w, x, gate, *, name):
    S, K = a.shape
    D = w.shape[1]
    tm, tn = _pick(S, 512, 16), _pick(D, 512)

    def body(a_ref, w_ref, x_ref, g_ref, o_ref, y_ref):
        y = _dot(a_ref[...], w_ref[...])
        y_ref[...] = y.astype(BF)
        o_ref[...] = x_ref[...] + g_ref[...] * y

    tile = pl.BlockSpec((tm, tn), lambda i, j: (i, j))
    return pl.pallas_call(
        body, name=name, grid=(S // tm, D // tn),
        in_specs=[pl.BlockSpec((tm, K), lambda i, j: (i, 0)), pl.BlockSpec((K, tn), lambda i, j: (0, j)),
                  tile, pl.BlockSpec((1, tn), lambda i, j: (0, j))],
        out_specs=[tile, tile],
        out_shape=[jax.ShapeDtypeStruct((S, D), F32), jax.ShapeDtypeStruct((S, D), BF)],
        compiler_params=_cp("parallel", "arbitrary"),
    )(a, w, x, gate)


def _gate_matmul_nt(dx, gate, y, w, act, *, name):
    S, D = dx.shape
    K = w.shape[0]
    tm, tn = _pick(S, 512, 16), _pick(K, 512)
    fused = act is not None

    def body(dx_ref, g_ref, y_ref, w_ref, *rest):
        if fused:
            act_ref, da_ref, dm_ref, dg_ref, ms = rest
        else:
            da_ref, dm_ref, dg_ref, ms = rest
        i, j = pl.program_id(0), pl.program_id(1)

        @pl.when((i == 0) & (j == 0))
        def _():
            dg_ref[...] = jnp.zeros_like(dg_ref)

        @pl.when(j == 0)
        def _():
            dxv = dx_ref[...]
            dmb = (dxv * g_ref[...]).astype(BF)
            ms[...] = dmb
            dm_ref[...] = dmb
            dg_ref[...] += jnp.sum(dxv * y_ref[...].astype(F32), axis=0, keepdims=True)

        da = _dg(ms[...], w_ref[...], NT)
        if fused:
            da_ref[...] = (da * (2.0 * act_ref[...].astype(F32))).astype(BF)
        else:
            da_ref[...] = da

    row = pl.BlockSpec((tm, D), lambda i, j: (i, 0))
    vec = pl.BlockSpec((1, D), lambda i, j: (0, 0))
    tile = pl.BlockSpec((tm, tn), lambda i, j: (i, j))
    in_specs = [row, vec, row, pl.BlockSpec((tn, D), lambda i, j: (j, 0))]
    args = [dx, gate, y, w]
    if fused:
        in_specs.append(tile)
        args.append(act)
    return pl.pallas_call(
        body, name=name, grid=(S // tm, K // tn),
        in_specs=in_specs, out_specs=[tile, row, vec],
        out_shape=[jax.ShapeDtypeStruct((S, K), BF if fused else F32), jax.ShapeDtypeStruct((S, D), BF),
                   jax.ShapeDtypeStruct((1, D), F32)],
        scratch_shapes=[pltpu.VMEM((tm, D), BF)],
        compiler_params=_cp("arbitrary", "arbitrary"),
    )(*args)


def _matmul_tn(a, b, *, name):
    S, Ka = a.shape
    P, _, Db = b.shape
    tk, tn, ts = _pick(Ka, 1024), _pick(Db, 1024), _pick(S, 512, 16)
    npb = Db // tn

    def body(a_ref, b_ref, o_ref, acc):
        s = pl.program_id(2)

        @pl.when(s == 0)
        def _():
            acc[...] = jnp.zeros_like(acc)

        acc[...] += _dg(a_ref[...], b_ref[...], TN)

        @pl.when(s == pl.num_programs(2) - 1)
        def _():
            o_ref[...] = acc[...]

    return pl.pallas_call(
        body, name=name, grid=(Ka // tk, P * npb, S // ts),
        in_specs=[pl.BlockSpec((ts, tk), lambda i, j, s: (s, i)),
                  pl.BlockSpec((None, ts, tn), lambda i, j, s: (j // npb, s, j % npb))],
        out_specs=pl.BlockSpec((tk, tn), lambda i, j, s: (i, j)),
        out_shape=jax.ShapeDtypeStruct((Ka, P * Db), F32),
        scratch_shapes=[pltpu.VMEM((tk, tn), F32)],
        compiler_params=_cp("parallel", "parallel", "arbitrary"),
    )(a, b)


def _matmul_nt_lnbwd(g, w, x, nw, sc, dx_out, *, name):
    P, S, Dg = g.shape
    D = x.shape[1]
    tm, tk = _pick(S, 512, 16), _pick(Dg, 1024)
    npb = Dg // tk
    nk = P * npb

    def body(g_ref, w_ref, x_ref, nw_ref, sc_ref, dxo_ref, dx_ref, dsc_ref, dsh_ref, dnw_ref, acc):
        i, k = pl.program_id(0), pl.program_id(1)

        @pl.when((i == 0) & (k == 0))
        def _():
            dsc_ref[...] = jnp.zeros_like(dsc_ref)
            dsh_ref[...] = jnp.zeros_like(dsh_ref)
            dnw_ref[...] = jnp.zeros_like(dnw_ref)

        @pl.when(k == 0)
        def _():
            acc[...] = jnp.zeros_like(acc)

        acc[...] += _dg(g_ref[...], w_ref[...], NT)

        @pl.when(k == nk - 1)
        def _():
            dh = acc[...]
            xv = x_ref[...]
            nwv = nw_ref[...]
            r = lax.rsqrt(jnp.mean(xv * xv, axis=-1, keepdims=True) + EPS)
            xr = xv * r
            dn = dh * (1.0 + sc_ref[...])
            dsc_ref[...] += jnp.sum(dh * (xr * nwv), axis=0, keepdims=True)
            dsh_ref[...] += jnp.sum(dh, axis=0, keepdims=True)
            dnw_ref[...] += jnp.sum(dn * xr, axis=0, keepdims=True)
            u = dn * nwv
            dx_ref[...] = dxo_ref[...] + r * (u - xr * jnp.mean(u * xr, axis=-1, keepdims=True))

    row = pl.BlockSpec((tm, D), lambda i, k: (i, 0))
    vec = pl.BlockSpec((1, D), lambda i, k: (0, 0))
    return pl.pallas_call(
        body, name=name, grid=(S // tm, nk),
        in_specs=[pl.BlockSpec((None, tm, tk), lambda i, k: (k // npb, i, k % npb)),
                  pl.BlockSpec((D, tk), lambda i, k: (0, k)), row, vec, vec, row],
        out_specs=[row, vec, vec, vec],
        out_shape=[jax.ShapeDtypeStruct((S, D), F32)] + [jax.ShapeDtypeStruct((1, D), F32)] * 3,
        scratch_shapes=[pltpu.VMEM((tm, D), F32)],
        compiler_params=_cp("arbitrary", "arbitrary"),
    )(g, w, x, nw, sc, dx_out)


def _loss_kernel(x, fw, tgt, *, name):
    S, D = x.shape
    tm = _pick(S, 512, 8)

    def body(x_ref, fw_ref, t_ref, l_ref, dx_ref, dfw_ref):
        @pl.when(pl.program_id(0) == 0)
        def _():
            l_ref[...] = jnp.zeros_like(l_ref)
            dfw_ref[...] = jnp.zeros_like(dfw_ref)

        xv = x_ref[...]
        fwv = fw_ref[...]
        r = lax.rsqrt(jnp.mean(xv * xv, axis=-1, keepdims=True) + EPS)
        xr = xv * r
        err = xr * fwv - t_ref[...]
        per_tok = jnp.mean(err * err, axis=-1, keepdims=True)
        l_ref[...] += 0.5 * jnp.sum(per_tok, axis=0, keepdims=True)
        dy = err * (1.0 / D)
        dfw_ref[...] += jnp.sum(dy * xr, axis=0, keepdims=True)
        u = dy * fwv
        dx_ref[...] = r * (u - xr * jnp.mean(u * xr, axis=-1, keepdims=True))

    row = pl.BlockSpec((tm, D), lambda i: (i, 0))
    vec = pl.BlockSpec((1, D), lambda i: (0, 0))
    return pl.pallas_call(
        body, name=name, grid=(S // tm,),
        in_specs=[row, vec, row],
        out_specs=[pl.BlockSpec((1, LANES), lambda i: (0, 0)), row, vec],
        out_shape=[jax.ShapeDtypeStruct((1, LANES), F32), jax.ShapeDtypeStruct((S, D), F32),
                   jax.ShapeDtypeStruct((1, D), F32)],
        compiler_params=_cp("arbitrary"),
    )(x, fw, tgt)


def _hg_lower_bound(lb3):
    mx = jnp.max(lb3, axis=0, keepdims=True)
    e = jnp.exp(lb3 - mx)
    p = e / jnp.sum(e, axis=0, keepdims=True)
    return p[0:1, :], p


def _hg_chunk_common(qr, fz, lbv):
    sq = _sigmoid(qr)
    q = qr * sq
    sig = _sigmoid(fz)
    f = lbv + (1.0 - lbv) * sig
    k = (1.0 - lbv) * (1.0 - sig)
    return q, sq, sig, f, k, jnp.log(f)


def _row_of(x, rows, r):
    return jnp.sum(jnp.where(rows == r, x, 0.0), axis=0, keepdims=True)


def _hg_fwd(proj, hg_lb, gn, *, name):
    S = proj.shape[0]
    D = proj.shape[1] // 4
    H = D // LANES
    C = HG_CHUNK
    T = _pick(S, 512, C)
    nch, nb = T // C, S // T

    def body(q_ref, fz_ref, v_ref, g_ref, lb_ref, gn_ref, y_ref, o_ref, sts_ref, st):
        @pl.when(pl.program_id(1) == 0)
        def _():
            st[...] = jnp.zeros_like(st)

        lbv, _ = _hg_lower_bound(lb_ref[...])
        gnv = gn_ref[...]
        ri = lax.broadcasted_iota(jnp.int32, (C, C), 0)
        ci_ = lax.broadcasted_iota(jnp.int32, (C, C), 1)
        low = ri >= ci_
        tri = jnp.where(low, 1.0, 0.0).astype(BF)
        rows = lax.broadcasted_iota(jnp.int32, (C, LANES), 0)

        def chunk(ci, carry):
            sl = pl.ds(pl.multiple_of(ci * C, C), C)
            q, _, _, _, k, logf = _hg_chunk_common(q_ref[sl, :], fz_ref[sl, :], lbv)
            vv = v_ref[sl, :]
            gg = g_ref[sl, :]
            G = _tri_dot(tri, logf)
            Gm = _row_of(G, rows, C // 2 - 1)
            Gl = _row_of(G, rows, C - 1)
            qt = q * jnp.exp(G - Gm)
            kt = k * jnp.exp(Gm - G)
            A = jnp.where(low, _dg3(qt, kt, NT), 0.0)
            Sv = st[...]
            sts_ref[ci] = Sv
            o = _dg3(A, vv, NN) + _dg3(q * jnp.exp(G), Sv, NT)
            st[...] = Sv * jnp.exp(Gl) + _dg3(vv, k * jnp.exp(Gl - G), TN)
            r = lax.rsqrt(jnp.mean(o * o, axis=-1, keepdims=True) + EPS)
            y_ref[sl, :] = ((o * r * gnv) * (gg * _sigmoid(gg))).astype(BF)
            o_ref[sl, :] = o
            return carry

        lax.fori_loop(0, nch, chunk, 0)

    def part(p):
        return pl.BlockSpec((T, LANES), lambda h, n: (n, p * H + h))

    blk = pl.BlockSpec((T, LANES), lambda h, n: (n, h))
    return pl.pallas_call(
        body, name=name, grid=(H, nb),
        in_specs=[part(0), part(1), part(2), part(3),
                  pl.BlockSpec((3, LANES), lambda h, n: (0, h)), pl.BlockSpec((1, LANES), lambda h, n: (0, 0))],
        out_specs=[blk, blk, pl.BlockSpec((None, nch, LANES, LANES), lambda h, n: (h, n, 0, 0))],
        out_shape=[jax.ShapeDtypeStruct((S, D), BF), jax.ShapeDtypeStruct((S, D), F32),
                   jax.ShapeDtypeStruct((H, S // C, LANES, LANES), F32)],
        scratch_shapes=[pltpu.VMEM((LANES, LANES), F32)],
        compiler_params=_cp("parallel", "arbitrary"),
    )(proj, proj, proj, proj, hg_lb, gn)


def _hg_bwd(proj, hg_lb, gn, o_all, states, dy, *, name):
    S = proj.shape[0]
    D = proj.shape[1] // 4
    H = D // LANES
    C = HG_CHUNK
    T = _pick(S, 512, C)
    nch, nb = T // C, S // T

    def body(q_ref, fz_ref, v_ref, g_ref, lb_ref, gn_ref, o_ref, sts_ref, dy_ref,
             dp_ref, dlb_ref, dgn_ref, dst, dlb_acc):
        n = pl.program_id(1)

        @pl.when(n == 0)
        def _():
            dst[...] = jnp.zeros_like(dst)
            dlb_acc[...] = jnp.zeros_like(dlb_acc)
            dgn_ref[...] = jnp.zeros_like(dgn_ref)

        lbv, p3 = _hg_lower_bound(lb_ref[...])
        gnv = gn_ref[...]
        ri = lax.broadcasted_iota(jnp.int32, (C, C), 0)
        ci_ = lax.broadcasted_iota(jnp.int32, (C, C), 1)
        low = ri >= ci_
        tri = jnp.where(low, 1.0, 0.0).astype(BF)
        triu = jnp.where(ri <= ci_, 1.0, 0.0).astype(BF)
        rows = lax.broadcasted_iota(jnp.int32, (C, LANES), 0)

        def chunk(cj, carry):
            ci = nch - 1 - cj
            sl = pl.ds(pl.multiple_of(ci * C, C), C)
            qr = q_ref[sl, :]
            q, sq, sig, f, k, logf = _hg_chunk_common(qr, fz_ref[sl, :], lbv)
            vv = v_ref[sl, :]
            gg = g_ref[sl, :]
            o = o_ref[sl, :]
            dyv = dy_ref[sl, :]
            G = _tri_dot(tri, logf)
            Gm = _row_of(G, rows, C // 2 - 1)
            Gl = _row_of(G, rows, C - 1)
            eG, e_qm, e_km, e_lk, eGl = jnp.exp(G), jnp.exp(G - Gm), jnp.exp(Gm - G), jnp.exp(Gl - G), jnp.exp(Gl)
            qt = q * e_qm
            kt = k * e_km
            A = jnp.where(low, _dg3(qt, kt, NT), 0.0)
            sg = _sigmoid(gg)
            r = lax.rsqrt(jnp.mean(o * o, axis=-1, keepdims=True) + EPS)
            on = o * r
            d_onw = dyv * (gg * sg)
            dgn_ref[...] += jnp.sum(d_onw * on, axis=0, keepdims=True)
            dgg = dyv * (on * gnv) * (sg * (1.0 + gg * (1.0 - sg)))
            u = d_onw * gnv
            do = r * (u - on * jnp.mean(u * on, axis=-1, keepdims=True))
            Sv = sts_ref[ci]
            dSv = dst[...]
            dA = jnp.where(low, _dg3(do, vv, NT), 0.0)
            kdec = k * e_lk
            dv = _dg3(A, do, TN) + _dg3(kdec, dSv, NT)
            dq = _dg3(dA, kt, NN) * e_qm + eG * _dg3(do, Sv, NN)
            dk = _dg3(dA, qt, TN) * e_km + e_lk * _dg3(vv, dSv, NN)
            s_end = Sv * eGl + _dg3(vv, kdec, TN)
            dgl = jnp.sum(dSv * s_end, axis=0, keepdims=True)
            dG = q * dq - k * dk + jnp.where(rows == C - 1, dgl, 0.0)
            dlogf = _tri_dot(triu, dG) - f * dk
            dst[...] = dSv * eGl + _dg3(do, q * eG, TN)
            dlf_f = dlogf / f
            dlb_acc[...] += jnp.sum(dlf_f * (1.0 - sig), axis=0, keepdims=True)
            dp_ref[0, sl, :] = (dq * (sq * (1.0 + qr * (1.0 - sq)))).astype(BF)
            dp_ref[1, sl, :] = (dlf_f * (1.0 - lbv) * sig * (1.0 - sig)).astype(BF)
            dp_ref[2, sl, :] = dv.astype(BF)
            dp_ref[3, sl, :] = dgg.astype(BF)
            return carry

        lax.fori_loop(0, nch, chunk, 0)
        sel = jnp.where(lax.broadcasted_iota(jnp.int32, (3, LANES), 0) == 0, 1.0, 0.0)
        dlb_ref[...] = lbv * (sel - p3) * dlb_acc[...]

    def part(p):
        return pl.BlockSpec((T, LANES), lambda h, n: (nb - 1 - n, p * H + h))

    blk = pl.BlockSpec((T, LANES), lambda h, n: (nb - 1 - n, h))
    return pl.pallas_call(
        body, name=name, grid=(H, nb),
        in_specs=[part(0), part(1), part(2), part(3),
                  pl.BlockSpec((3, LANES), lambda h, n: (0, h)), pl.BlockSpec((1, LANES), lambda h, n: (0, 0)),
                  blk, pl.BlockSpec((None, nch, LANES, LANES), lambda h, n: (h, nb - 1 - n, 0, 0)), blk],
        out_specs=[pl.BlockSpec((4, T, LANES), lambda h, n: (0, nb - 1 - n, h)),
                   pl.BlockSpec((3, LANES), lambda h, n: (0, h)),
                   pl.BlockSpec((None, 1, LANES), lambda h, n: (h, 0, 0))],
        out_shape=[jax.ShapeDtypeStruct((4, S, D), BF), jax.ShapeDtypeStruct((3, D), F32),
                   jax.ShapeDtypeStruct((H, 1, LANES), F32)],
        scratch_shapes=[pltpu.VMEM((LANES, LANES), F32), pltpu.VMEM((1, LANES), F32)],
        compiler_params=_cp("parallel", "arbitrary"),
    )(proj, proj, proj, proj, hg_lb, gn, o_all, states, dy)


def _log_sigmoid(u):
    return jnp.minimum(u, 0.0) - jnp.log(1.0 + jnp.exp(-jnp.abs(u)))


def _lane_put(base, lane, first, pieces):
    for n, p in enumerate(pieces):
        base = jnp.where(lane == first + n, p, base)
    return base


def _fox_cumsum(proj, bf_pad, *, name):
    S = proj.shape[0]
    D = proj.shape[1] // 5
    T = _pick(S, 256, 8)

    def body(fz_ref, b_ref, f_ref, carry):
        @pl.when(pl.program_id(0) == 0)
        def _():
            carry[...] = jnp.zeros_like(carry)

        logf = _log_sigmoid(fz_ref[...] + b_ref[...])
        tri = jnp.where(lax.broadcasted_iota(jnp.int32, (T, T), 0) >= lax.broadcasted_iota(jnp.int32, (T, T), 1),
                        1.0, 0.0).astype(BF)
        fv = _tri_dot(tri, logf) + carry[...]
        f_ref[...] = fv
        carry[...] = _row_of(fv, lax.broadcasted_iota(jnp.int32, (T, LANES), 0), T - 1)

    return pl.pallas_call(
        body, name=name, grid=(S // T,),
        in_specs=[pl.BlockSpec((T, LANES), lambda i: (i, 4 * D // LANES)), pl.BlockSpec((1, LANES), lambda i: (0, 0))],
        out_specs=pl.BlockSpec((T, LANES), lambda i: (i, 0)),
        out_shape=jax.ShapeDtypeStruct((S, LANES), F32),
        scratch_shapes=[pltpu.VMEM((1, LANES), F32)],
        compiler_params=_cp("arbitrary"),
    )(proj, bf_pad)


def _pair_stats(sq, lo):
    s_lo = jnp.sum(jnp.where(lo, sq, 0.0), axis=-1, keepdims=True)
    s_hi = jnp.sum(jnp.where(lo, 0.0, sq), axis=-1, keepdims=True)
    return jnp.where(lo, s_lo, s_hi) * (1.0 / FOX_DH)


def _fox_prep(proj, fcum, qw2, kw2, *, name):
    S = proj.shape[0]
    D = proj.shape[1] // 5
    HP = D // LANES
    T = _pick(S, 512, 16)

    def body(q_ref, k_ref, v_ref, f_ref, qw_ref, kw_ref, qa_ref, ka_ref, va_ref):
        hp = pl.program_id(1)
        lane = lax.broadcasted_iota(jnp.int32, (T, LANES), 1)
        lo = lane < FOX_DH
        qv, kv, vv, fv = q_ref[...], k_ref[...], v_ref[...], f_ref[...]
        qn = qv * lax.rsqrt(_pair_stats(qv * qv, lo) + EPS) * qw_ref[...] * 0.125
        kn = kv * lax.rsqrt(_pair_stats(kv * kv, lo) + EPS) * kw_ref[...]
        ones_q = jnp.where((lane >= 67) & (lane <= 69), 1.0, 0.0)
        ones_k = jnp.where(((lane >= 64) & (lane <= 66)) | ((lane >= 70) & (lane <= 72)), 1.0, 0.0)
        ones_v = jnp.where((lane >= 64) & (lane <= 66), 1.0, 0.0)
        for hh in range(2):
            fh = jnp.sum(jnp.where(lane == 2 * hp + hh, fv, 0.0), axis=-1, keepdims=True)
            pieces = [p.astype(F32) for p in _split3(fh)]

            def half(x):
                return jnp.where(lo, x if hh == 0 else pltpu.roll(x, FOX_DH, 1), 0.0)

            qa_ref[hh] = _lane_put(half(qn) + ones_q, lane, 64, pieces).astype(BF)
            ka_ref[hh] = _lane_put(half(kn) + ones_k, lane, 67, [-p for p in pieces]).astype(BF)
            va_ref[hh] = (half(vv) + ones_v).astype(BF)

    def part(p):
        return pl.BlockSpec((T, LANES), lambda i, hp: (i, p * HP + hp))

    vec = pl.BlockSpec((1, LANES), lambda i, hp: (0, 0))
    aug = pl.BlockSpec((2, T, LANES), lambda i, hp: (hp, i, 0))
    return pl.pallas_call(
        body, name=name, grid=(S // T, HP),
        in_specs=[part(0), part(1), part(2), pl.BlockSpec((T, LANES), lambda i, hp: (i, 0)), vec, vec],
        out_specs=[aug, aug, aug],
        out_shape=[jax.ShapeDtypeStruct((2 * HP, S, LANES), BF)] * 3,
        compiler_params=_cp("parallel", "arbitrary"),
    )(proj, proj, proj, fcum, qw2, kw2)


def _fox_fwd(qa, ka, va, proj, *, name):
    H, S, _ = qa.shape
    HP = H // 2
    D = HP * LANES
    B = _pick(S, 256, 16)
    nq = S // B

    def body(q_ref, k_ref, v_ref, g_ref, y_ref, o_ref, q2_ref):
        i = pl.program_id(1)
        lane = lax.broadcasted_iota(jnp.int32, (B, LANES), 1)
        lo = lane < FOX_DH
        causal = lax.broadcasted_iota(jnp.int32, (B, B), 1) <= lax.broadcasted_iota(jnp.int32, (B, B), 0)
        halves = []
        for hh in range(2):
            qb = q_ref[hh]

            def step(j, carry, masked=False):
                m, acc = carry
                sl = pl.ds(pl.multiple_of(j * B, B), B)
                s = _dg(qb, k_ref[hh, sl, :], NT)
                if masked:
                    s = jnp.where(causal, s, -jnp.inf)
                m_new = jnp.maximum(m, jnp.max(s, axis=-1, keepdims=True))
                p = jnp.exp(s - m_new)
                ph = p.astype(BF)
                pl_ = (p - ph.astype(F32)).astype(BF)
                vb = v_ref[hh, sl, :]
                acc = acc * jnp.exp(m - m_new) + (_dot(ph, vb) + _dot(pl_, vb))
                return m_new, acc

            init = (jnp.full((B, 1), -jnp.inf, F32), jnp.zeros((B, LANES), F32))
            m, acc = step(i, lax.fori_loop(0, i, step, init), masked=True)
            l = jnp.sum(jnp.where(lane == FOX_DH, acc, 0.0), axis=-1, keepdims=True)
            halves.append(acc / l)
            neg_lse = [-(p.astype(F32)) for p in _split3(m + jnp.log(l))]
            q2_ref[hh] = _lane_put(qb.astype(F32), lane, 70, neg_lse).astype(BF)
        o = jnp.where(lo, halves[0], pltpu.roll(halves[1], FOX_DH, 1))
        o_ref[...] = o
        y_ref[...] = (o * _sigmoid(g_ref[...])).astype(BF)

    blk = pl.BlockSpec((B, LANES), lambda hp, i: (i, hp))
    qblk = pl.BlockSpec((2, B, LANES), lambda hp, i: (hp, i, 0))
    full = pl.BlockSpec((2, S, LANES), lambda hp, i: (hp, 0, 0))
    return pl.pallas_call(
        body, name=name, grid=(HP, nq),
        in_specs=[qblk, full, full, pl.BlockSpec((B, LANES), lambda hp, i: (i, 3 * HP + hp))],
        out_specs=[blk, blk, qblk],
        out_shape=[jax.ShapeDtypeStruct((S, D), BF), jax.ShapeDtypeStruct((S, D), F32),
                   jax.ShapeDtypeStruct((H, S, LANES), BF)],
        compiler_params=_cp("parallel", "arbitrary"),
    )(qa, ka, va, proj)


def _fox_bwd_prep(dy, o, proj, *, name):
    S, D = dy.shape
    HP = D // LANES
    T = _pick(S, 512, 16)

    def body(dy_ref, o_ref, g_ref, da_ref):
        lane = lax.broadcasted_iota(jnp.int32, (T, LANES), 1)
        lo = lane < FOX_DH
        do = (dy_ref[...] * _sigmoid(g_ref[...])).astype(BF).astype(F32)
        prod = do * o_ref[...]
        d_lo = jnp.sum(jnp.where(lo, prod, 0.0), axis=-1, keepdims=True)
        d_hi = jnp.sum(jnp.where(lo, 0.0, prod), axis=-1, keepdims=True)
        for hh, delta in enumerate((d_lo, d_hi)):
            base = jnp.where(lo, do if hh == 0 else pltpu.roll(do, FOX_DH, 1), 0.0)
            da_ref[hh] = _lane_put(base, lane, 64, [-(p.astype(F32)) for p in _split3(delta)]).astype(BF)

    blk = pl.BlockSpec((T, LANES), lambda i, hp: (i, hp))
    return pl.pallas_call(
        body, name=name, grid=(S // T, HP),
        in_specs=[blk, blk, pl.BlockSpec((T, LANES), lambda i, hp: (i, 3 * HP + hp))],
        out_specs=pl.BlockSpec((2, T, LANES), lambda i, hp: (hp, i, 0)),
        out_shape=jax.ShapeDtypeStruct((2 * HP, S, LANES), BF),
        compiler_params=_cp("parallel", "arbitrary"),
    )(dy, o, proj)


def _fox_bwd(q2, ka, va, doa, *, name):
    H, S, _ = q2.shape
    B = _pick(S, 256, 16)
    nb = S // B

    def body(q_ref, do_ref, k_ref, v_ref, dq_ref, dk_ref, dv_ref, cs_ref):
        j = pl.program_id(1)

        @pl.when(j == 0)
        def _():
            dq_ref[...] = jnp.zeros_like(dq_ref)

        kb, vb = k_ref[...], v_ref[...]
        causal = lax.broadcasted_iota(jnp.int32, (B, B), 1) <= lax.broadcasted_iota(jnp.int32, (B, B), 0)

        def step(i, carry, masked=False):
            dk_acc, dv_acc, cs_acc = carry
            sl = pl.ds(pl.multiple_of(i * B, B), B)
            qb, dob = q_ref[sl, :], do_ref[sl, :]
            s = _dg(qb, kb, NT)
            if masked:
                s = jnp.where(causal, s, -jnp.inf)
            p = jnp.exp(s)
            ds = p * _dg(dob, vb, NT)
            dsb = ds.astype(BF)
            cs_acc = cs_acc + jnp.sum(ds.reshape(B // 8, 8, B), axis=0)
            dv_acc = dv_acc + _dg(p.astype(BF), dob, TN)
            dk_acc = dk_acc + _dg(dsb, qb, TN)
            dq_ref[sl, :] += _dot(dsb, kb)
            return dk_acc, dv_acc, cs_acc

        zero = jnp.zeros((B, LANES), F32)
        init = step(j, (zero, zero, jnp.zeros((8, B), F32)), masked=True)
        dk_acc, dv_acc, cs_acc = lax.fori_loop(j + 1, nb, step, init)
        dk_ref[...] = dk_acc
        dv_ref[...] = dv_acc
        cs_ref[...] = jnp.sum(cs_acc, axis=0, keepdims=True)

    full = pl.BlockSpec((None, S, LANES), lambda h, j: (h, 0, 0))
    blk = pl.BlockSpec((None, B, LANES), lambda h, j: (h, j, 0))
    return pl.pallas_call(
        body, name=name, grid=(H, nb),
        in_specs=[full, full, blk, blk],
        out_specs=[full, blk, blk, pl.BlockSpec((None, 1, B), lambda h, j: (h, 0, j))],
        out_shape=[jax.ShapeDtypeStruct((H, S, LANES), F32)] * 3 + [jax.ShapeDtypeStruct((H, 1, S), F32)],
        compiler_params=_cp("parallel", "arbitrary"),
    )(q2, doa, ka, va)


def _fox_bwd_post(dqa, dka, dva, proj, dy, o, qw2, kw2, *, name):
    S, D = dy.shape
    HP = D // LANES
    T = _pick(S, 512, 16)

    def body(dq_ref, dk_ref, dv_ref, q_ref, k_ref, g_ref, dy_ref, o_ref, qw_ref, kw_ref, dp_ref, dqw_ref, dkw_ref):
        @pl.when((pl.program_id(0) == 0) & (pl.program_id(1) == 0))
        def _():
            dqw_ref[...] = jnp.zeros_like(dqw_ref)
            dkw_ref[...] = jnp.zeros_like(dkw_ref)

        lane = lax.broadcasted_iota(jnp.int32, (T, LANES), 1)
        lo = lane < FOX_DH

        def pair(ref):
            return jnp.where(lo, ref[0], pltpu.roll(ref[1], FOX_DH, 1))

        def norm_bwd(xv, w, dyn, dw_ref):
            r = lax.rsqrt(_pair_stats(xv * xv, lo) + EPS)
            xr = xv * r
            dw_ref[...] += jnp.sum(dyn * xr, axis=0, keepdims=True)
            u = dyn * w
            return r * (u - xr * _pair_stats(u * xr, lo))

        dp_ref[0] = norm_bwd(q_ref[...], qw_ref[...], pair(dq_ref) * 0.125, dqw_ref).astype(BF)
        dp_ref[1] = norm_bwd(k_ref[...], kw_ref[...], pair(dk_ref), dkw_ref).astype(BF)
        dp_ref[2] = pair(dv_ref).astype(BF)
        sg = _sigmoid(g_ref[...])
        dp_ref[3] = (dy_ref[...] * o_ref[...] * sg * (1.0 - sg)).astype(BF)

    def part(p):
        return pl.BlockSpec((T, LANES), lambda i, hp: (i, p * HP + hp))

    aug = pl.BlockSpec((2, T, LANES), lambda i, hp: (hp, i, 0))
    blk = pl.BlockSpec((T, LANES), lambda i, hp: (i, hp))
    vec = pl.BlockSpec((1, LANES), lambda i, hp: (0, 0))
    return pl.pallas_call(
        body, name=name, grid=(S // T, HP),
        in_specs=[aug, aug, aug, part(0), part(1), part(3), blk, blk, vec, vec],
        out_specs=[pl.BlockSpec((4, T, LANES), lambda i, hp: (0, i, hp)), vec, vec],
        out_shape=[jax.ShapeDtypeStruct((5, S, D), BF), jax.ShapeDtypeStruct((1, LANES), F32),
                   jax.ShapeDtypeStruct((1, LANES), F32)],
        compiler_params=_cp("arbitrary", "arbitrary"),
    )(dqa, dka, dva, proj, proj, proj, dy, o, qw2, kw2)


def _fox_dfz(colsum, nheads, proj, bf_pad, dproj, *, name):
    S = colsum.shape[0]
    H = nheads
    D = dproj.shape[2]
    T = _pick(S, 256, 16)
    nb = S // T

    def body(cs_ref, fz_ref, b_ref, _, dp_ref, db_ref, carry):
        @pl.when(pl.program_id(0) == 0)
        def _():
            carry[...] = jnp.zeros_like(carry)
            db_ref[...] = jnp.zeros_like(db_ref)

        lane = lax.broadcasted_iota(jnp.int32, (T, LANES), 1)
        df = -cs_ref[...]
        triu = jnp.where(lax.broadcasted_iota(jnp.int32, (T, T), 0) <= lax.broadcasted_iota(jnp.int32, (T, T), 1),
                         1.0, 0.0).astype(BF)
        dlogf = _tri_dot(triu, df) + carry[...]
        carry[...] = _row_of(dlogf, lax.broadcasted_iota(jnp.int32, (T, LANES), 0), 0)
        dfz = jnp.where(lane < H, dlogf * _sigmoid(-(fz_ref[...] + b_ref[...])), 0.0)
        db_ref[...] += jnp.sum(dfz, axis=0, keepdims=True)
        dp_ref[...] = jnp.zeros_like(dp_ref)
        dp_ref[:, 0:LANES] = dfz.astype(BF)

    return pl.pallas_call(
        body, name=name, grid=(nb,),
        in_specs=[pl.BlockSpec((T, LANES), lambda i: (nb - 1 - i, 0)),
                  pl.BlockSpec((T, LANES), lambda i: (nb - 1 - i, 4 * D // LANES)),
                  pl.BlockSpec((1, LANES), lambda i: (0, 0)),
                  pl.BlockSpec(memory_space=pl.ANY)],
        out_specs=[pl.BlockSpec((None, T, D), lambda i: (4, nb - 1 - i, 0)), pl.BlockSpec((1, LANES), lambda i: (0, 0))],
        out_shape=[jax.ShapeDtypeStruct(dproj.shape, BF), jax.ShapeDtypeStruct((1, LANES), F32)],
        scratch_shapes=[pltpu.VMEM((1, LANES), F32)],
        input_output_aliases={3: 0},
        compiler_params=_cp("arbitrary"),
    )(colsum, proj, bf_pad, dproj)


def _mod_fwd(c16, w, b, *, name):
    L, D, N = w.shape
    tn = _pick(N, 512)

    def body(c_ref, w_ref, b_ref, o_ref):
        cv = c_ref[...]
        ca = (cv * _sigmoid(cv)).astype(BF)
        o_ref[...] = _dot(ca, w_ref[...].astype(BF)) + b_ref[...]

    return pl.pallas_call(
        body, name=name, grid=(L, N // tn),
        in_specs=[pl.BlockSpec((16, D), lambda l, j: (0, 0)), pl.BlockSpec((None, D, tn), lambda l, j: (l, 0, j)),
                  pl.BlockSpec((None, 1, tn), lambda l, j: (l, 0, j))],
        out_specs=pl.BlockSpec((None, 16, tn), lambda l, j: (l, 0, j)),
        out_shape=jax.ShapeDtypeStruct((L, 16, N), F32),
        compiler_params=_cp("parallel", "arbitrary"),
    )(c16, w, b)


def _mod_bwd(c16, dmod, *, name):
    L, _, N = dmod.shape
    D = c16.shape[1]
    tn = _pick(N, 512)

    def body(c_ref, d_ref, o_ref):
        cv = c_ref[...]
        ca = (cv * _sigmoid(cv)).astype(BF)
        o_ref[...] = _dg(ca, d_ref[...].astype(BF), TN)

    return pl.pallas_call(
        body, name=name, grid=(L, N // tn),
        in_specs=[pl.BlockSpec((16, D), lambda l, j: (0, 0)), pl.BlockSpec((None, 16, tn), lambda l, j: (l, 0, j))],
        out_specs=pl.BlockSpec((None, D, tn), lambda l, j: (l, 0, j)),
        out_shape=jax.ShapeDtypeStruct((L, D, N), F32),
        compiler_params=_cp("parallel", "arbitrary"),
    )(c16, dmod)


def _adamw_math(w, g, m, v):
    m = ADAM_B1 * m + (1.0 - ADAM_B1) * g
    v = ADAM_B2 * v + (1.0 - ADAM_B2) * (g * g)
    m_hat = m / (1.0 - ADAM_B1 ** ADAM_STEP)
    v_hat = v / (1.0 - ADAM_B2 ** ADAM_STEP)
    return -ADAM_LR * (m_hat / (jnp.sqrt(v_hat) + ADAM_EPS) + ADAM_WD * w), m, v


def _adamw(w, g, m, v, *, g_row0=0, name):
    R, C = w.shape
    tr = min(math.gcd(g_row0, 256) if g_row0 else 256, -(-R // 8) * 8)
    g0 = g_row0 // tr

    def body(w_ref, g_ref, m_ref, v_ref, d_ref, mo_ref, vo_ref):
        d, mn, vn = _adamw_math(w_ref[...], g_ref[...], m_ref[...], v_ref[...])
        d_ref[...] = d
        mo_ref[...] = mn
        vo_ref[...] = vn

    blk = pl.BlockSpec((tr, C), lambda i: (i, 0))
    return pl.pallas_call(
        body, name=name, grid=(pl.cdiv(R, tr),),
        in_specs=[blk, pl.BlockSpec((tr, C), lambda i: (g0 + i, 0)), blk, blk],
        out_specs=[blk, blk, blk],
        out_shape=[jax.ShapeDtypeStruct((R, C), F32)] * 3,
        compiler_params=_cp("parallel"),
    )(w, g, m, v)


def _sum_parts(parts, *, name):
    P, R, C = parts.shape

    def body(p_ref, o_ref):
        acc = p_ref[0]
        for p in range(1, P):
            acc = acc + p_ref[p]
        o_ref[...] = acc

    return pl.pallas_call(
        body, name=name, grid=(1,),
        in_specs=[pl.BlockSpec((P, R, C), lambda i: (0, 0, 0))],
        out_specs=pl.BlockSpec((R, C), lambda i: (0, 0)),
        out_shape=jax.ShapeDtypeStruct((R, C), F32),
        compiler_params=_cp("arbitrary"),
    )(parts)


def _add_halves(g4, recv, c_idx, *, name):
    _, _, Rh, C = g4.shape
    tr = _pick(Rh, 256, 8)

    def body(c_ref, a_ref, b_ref, o_ref):
        o_ref[...] = a_ref[...] + b_ref[...]

    return pl.pallas_call(
        body, name=name,
        grid_spec=pltpu.PrefetchScalarGridSpec(
            num_scalar_prefetch=1, grid=(4, pl.cdiv(Rh, tr)),
            in_specs=[pl.BlockSpec((None, None, tr, C), lambda j, r, c: (j, c[0], r, 0)),
                      pl.BlockSpec((None, tr, C), lambda j, r, c: (j, r, 0))],
            out_specs=pl.BlockSpec((None, tr, C), lambda j, r, c: (j, r, 0))),
        out_shape=jax.ShapeDtypeStruct((4, Rh, C), F32),
        compiler_params=_cp("parallel", "arbitrary"),
    )(c_idx, g4, recv)


def _add_four(own, recv, chip_idx, *, name):
    _, Rh, C = own.shape
    tr = _pick(Rh, 256, 8)

    def body(c_ref, a_ref, b_ref, o_ref):
        o_ref[...] = ((a_ref[...] + b_ref[0]) + b_ref[1]) + b_ref[2]

    return pl.pallas_call(
        body, name=name,
        grid_spec=pltpu.PrefetchScalarGridSpec(
            num_scalar_prefetch=1, grid=(pl.cdiv(Rh, tr),),
            in_specs=[pl.BlockSpec((None, tr, C), lambda r, c: (c[0], r, 0)),
                      pl.BlockSpec((3, tr, C), lambda r, c: (0, r, 0))],
            out_specs=pl.BlockSpec((tr, C), lambda r, c: (r, 0))),
        out_shape=jax.ShapeDtypeStruct((Rh, C), F32),
        compiler_params=_cp("arbitrary"),
    )(chip_idx, own, recv)


HBM = pl.BlockSpec(memory_space=pltpu.HBM)


def _mesh_pos():
    return lax.axis_index("x"), lax.axis_index("y"), lax.axis_index("c")


def _other_chips(x, y):
    return [(1 - x, y), (x, 1 - y), (1 - x, 1 - y)]


def _allgather_small(xs, *, name):
    m_per, n = xs.shape

    def body(x_ref, out_ref, send_sems, recv_sems, local_sem):
        x, y, c = _mesh_pos()
        me, sibling = (x, y, c), (x, y, 1 - c)
        chips = _other_chips(x, y)

        def rows(px, py, pc):
            return out_ref.at[pl.ds((4 * px + 2 * py + pc) * m_per, m_per), :]

        def copy(k, block, to, src=None):
            return pltpu.make_async_remote_copy(
                src_ref=rows(*block) if src is None else src, dst_ref=rows(*block),
                send_sem=send_sems.at[k], recv_sem=recv_sems.at[k], device_id=to, device_id_type=MESH)

        mine = pltpu.make_async_copy(x_ref, rows(*me), local_sem)
        mine.start()
        first = [copy(0, me, sibling, src=x_ref)]
        first += [copy(1 + j, me, (*chip, c), src=x_ref) for j, chip in enumerate(chips)]
        for cp in first:
            cp.start()
        passed = [copy(4 + j, (*chip, c), sibling) for j, chip in enumerate(chips)]
        for j, chip in enumerate(chips):
            copy(1 + j, (*chip, c), me).wait_recv()
            passed[j].start()
        copy(0, sibling, me).wait_recv()
        for j, chip in enumerate(chips):
            copy(4 + j, (*chip, 1 - c), me).wait_recv()
        for cp in first + passed:
            cp.wait_send()
        mine.wait()

    return pl.pallas_call(
        body, name=name,
        out_shape=jax.ShapeDtypeStruct((N_DEV * m_per, n), xs.dtype),
        in_specs=[pl.BlockSpec(memory_space=pltpu.VMEM)],
        out_specs=pl.BlockSpec(memory_space=pltpu.VMEM),
        scratch_shapes=[pltpu.SemaphoreType.DMA((7,)), pltpu.SemaphoreType.DMA((7,)), pltpu.SemaphoreType.DMA],
    )(xs)


def _allgather_chip_slabs(slab, *, name):
    R, C = slab.shape
    Rh = R // 2

    def body(s_ref, out_ref, send_sems, recv_sems, local_sem):
        x, y, c = _mesh_pos()
        sibling = (x, y, 1 - c)
        chips = _other_chips(x, y)

        def half(px, py, pc):
            return out_ref.at[2 * px + py, pl.ds(pc * Rh, Rh), :]

        def copy(k, block, to, src=None):
            return pltpu.make_async_remote_copy(
                src_ref=half(*block) if src is None else src, dst_ref=half(*block),
                send_sem=send_sems.at[k], recv_sem=recv_sems.at[k], device_id=to, device_id_type=MESH)

        mine = pltpu.make_async_copy(s_ref, out_ref.at[2 * x + y], local_sem)
        mine.start()
        first = [copy(j, (x, y, c), (*chip, c), src=s_ref.at[pl.ds(c * Rh, Rh), :]) for j, chip in enumerate(chips)]
        for cp in first:
            cp.start()
        passed = [copy(3 + j, (*chip, c), sibling) for j, chip in enumerate(chips)]
        for j, chip in enumerate(chips):
            copy(j, (*chip, c), (x, y, c)).wait_recv()
            passed[j].start()
        for j, chip in enumerate(chips):
            copy(3 + j, (*chip, 1 - c), (x, y, c)).wait_recv()
        for cp in first + passed:
            cp.wait_send()
        mine.wait()

    return pl.pallas_call(
        body, name=name,
        out_shape=jax.ShapeDtypeStruct((N_CHIPS, R, C), slab.dtype),
        in_specs=[HBM], out_specs=HBM,
        scratch_shapes=[pltpu.SemaphoreType.DMA((6,)), pltpu.SemaphoreType.DMA((6,)), pltpu.SemaphoreType.DMA],
    )(slab)


def _swap_halves(g4, *, name):
    _, _, Rh, C = g4.shape

    def body(g_ref, out_ref, send_sems, recv_sems):
        x, y, c = _mesh_pos()
        copies = [pltpu.make_async_remote_copy(
            src_ref=g_ref.at[j, 1 - c], dst_ref=out_ref.at[j], send_sem=send_sems.at[j], recv_sem=recv_sems.at[j],
            device_id=(x, y, 1 - c), device_id_type=MESH) for j in range(N_CHIPS)]
        for cp in copies:
            cp.start()
        for cp in copies:
            cp.wait()

    return pl.pallas_call(
        body, name=name,
        out_shape=jax.ShapeDtypeStruct((N_CHIPS, Rh, C), g4.dtype),
        in_specs=[HBM], out_specs=HBM,
        scratch_shapes=[pltpu.SemaphoreType.DMA((N_CHIPS,)), pltpu.SemaphoreType.DMA((N_CHIPS,))],
    )(g4)


def _scatter_partials(part, *, name):
    _, Rh, C = part.shape

    def body(p_ref, out_ref, send_sems, recv_sems):
        x, y, c = _mesh_pos()
        copies = [pltpu.make_async_remote_copy(
            src_ref=p_ref.at[2 * px + py], dst_ref=out_ref.at[j], send_sem=send_sems.at[j], recv_sem=recv_sems.at[j],
            device_id=(px, py, c), device_id_type=MESH) for j, (px, py) in enumerate(_other_chips(x, y))]
        for cp in copies:
            cp.start()
        for cp in copies:
            cp.wait()

    return pl.pallas_call(
        body, name=name,
        out_shape=jax.ShapeDtypeStruct((3, Rh, C), part.dtype),
        in_specs=[HBM], out_specs=HBM,
        scratch_shapes=[pltpu.SemaphoreType.DMA((3,)), pltpu.SemaphoreType.DMA((3,))],
    )(part)


def _join_halves(mine, *, name):
    Rh, C = mine.shape

    def body(m_ref, out_ref, send_sem, recv_sem, local_sem):
        x, y, c = _mesh_pos()
        keep = pltpu.make_async_copy(m_ref, out_ref.at[c], local_sem)
        keep.start()
        cp = pltpu.make_async_remote_copy(
            src_ref=m_ref, dst_ref=out_ref.at[c], send_sem=send_sem, recv_sem=recv_sem,
            device_id=(x, y, 1 - c), device_id_type=MESH)
        cp.start()
        cp.wait()
        keep.wait()

    return pl.pallas_call(
        body, name=name,
        out_shape=jax.ShapeDtypeStruct((2, Rh, C), mine.dtype),
        in_specs=[HBM], out_specs=HBM,
        scratch_shapes=[pltpu.SemaphoreType.DMA, pltpu.SemaphoreType.DMA, pltpu.SemaphoreType.DMA],
    )(mine)


def _pad_rows(a, mult):
    pad = (-a.shape[0]) % mult
    return a if pad == 0 else jnp.pad(a, ((0, pad),) + ((0, 0),) * (a.ndim - 1))


def _local_step(x, target, mod, wts, small):
    S, D = x.shape
    HP = D // LANES
    row = lambda v: v.reshape(1, -1)
    msplit = [[row(mod[i, k * D:(k + 1) * D]) for k in range(6)] for i in range(2)]
    gw, gs = {}, {}
    dmod = [[None] * 6 for _ in range(2)]

    sh1, sc1, g1, sh2, sc2, g2 = msplit[0]
    n1w0, n2w0 = row(small["norm1_w"][0]), row(small["norm2_w"][0])
    proj0, h1_0 = _ln_matmul(x, n1w0, sc1, sh1, wts["hg_w_in"], relu2=False, name="hg_in_proj")
    gn = small["hg_gn_w"].reshape(1, LANES)
    ypre0, o0, states = _hg_fwd(proj0, small["hg_lb"], gn, name="hg_fwd")
    x1, ymix0 = _matmul_resid(ypre0, wts["hg_w_out"], x, g1, name="hg_out_proj")
    a0, u0, h2_0 = _ln_matmul(x1, n2w0, sc2, sh2, wts["mlp_w1_0"], relu2=True, name="mlp0_up")
    x2, ymlp0 = _matmul_resid(u0, wts["mlp_w2_0"], x1, g2, name="mlp0_down")

    sh1b, sc1b, g1b, sh2b, sc2b, g2b = msplit[1]
    n1w1, n2w1 = row(small["norm1_w"][1]), row(small["norm2_w"][1])
    proj1, h1_1 = _ln_matmul(x2, n1w1, sc1b, sh1b, wts["fox_w_in"], relu2=False, name="fox_in_proj")
    nheads = 2 * HP
    bf_pad = jnp.pad(small["fox_b_f"].reshape(1, nheads), ((0, 0), (0, LANES - nheads)))
    qw2 = jnp.tile(small["fox_qn_w"].reshape(1, FOX_DH), (1, 2))
    kw2 = jnp.tile(small["fox_kn_w"].reshape(1, FOX_DH), (1, 2))
    fcum = _fox_cumsum(proj1, bf_pad, name="fox_cumsum")
    qa, ka, va = _fox_prep(proj1, fcum, qw2, kw2, name="fox_prep")
    ypre1, o1, q2 = _fox_fwd(qa, ka, va, proj1, name="fox_fwd")
    x3, ymix1 = _matmul_resid(ypre1, wts["fox_w_out"], x2, g1b, name="fox_out_proj")
    a1, u1, h2_1 = _ln_matmul(x3, n2w1, sc2b, sh2b, wts["mlp_w1_1"], relu2=True, name="mlp1_up")
    x4, ymlp1 = _matmul_resid(u1, wts["mlp_w2_1"], x3, g2b, name="mlp1_down")

    loss, dx4, dfw = _loss_kernel(x4, row(small["final_w"]), target, name="loss")
    gs["final_w"] = dfw.reshape(-1)

    def mlp_bwd(i, dx_out, x_in, h2, a, u, ymlp, n2w, sc2_, g2_):
        dz, dm, dg2 = _gate_matmul_nt(dx_out, g2_, ymlp, wts[f"mlp_w2_{i}"], a, name=f"mlp{i}_down_bwd")
        gw[f"mlp_w2_{i}"] = _matmul_tn(u, dm[None], name=f"mlp{i}_dw2")
        gw[f"mlp_w1_{i}"] = _matmul_tn(h2, dz[None], name=f"mlp{i}_dw1")
        dx_in, dsc, dsh, dnw = _matmul_nt_lnbwd(dz[None], wts[f"mlp_w1_{i}"], x_in, n2w, sc2_, dx_out,
                                                name=f"mlp{i}_up_bwd")
        dmod[i][3], dmod[i][4], dmod[i][5] = dsh, dsc, dg2
        return dx_in, dnw

    dx3, dn2w1 = mlp_bwd(1, dx4, x3, h2_1, a1, u1, ymlp1, n2w1, sc2b, g2b)
    dyp1, dm1, dg1b = _gate_matmul_nt(dx3, g1b, ymix1, wts["fox_w_out"], None, name="fox_out_bwd")
    gw["fox_w_out"] = _matmul_tn(ypre1, dm1[None], name="fox_dw_out")
    doa = _fox_bwd_prep(dyp1, o1, proj1, name="fox_bwd_prep")
    dqa, dka, dva, colsum = _fox_bwd(q2, ka, va, doa, name="fox_bwd")
    colsum = jnp.pad(colsum[:, 0, :].T, ((0, 0), (0, LANES - nheads)))
    dproj1, dqw, dkw = _fox_bwd_post(dqa, dka, dva, proj1, dyp1, o1, qw2, kw2, name="fox_bwd_post")
    dproj1, dbf = _fox_dfz(colsum, nheads, proj1, bf_pad, dproj1, name="fox_dfz")
    gw["fox_w_in"] = _matmul_tn(h1_1, dproj1, name="fox_dw_in")
    dx2, dsc, dsh, dn1w1 = _matmul_nt_lnbwd(dproj1, wts["fox_w_in"], x2, n1w1, sc1b, dx3, name="fox_in_bwd")
    dmod[1][0], dmod[1][1], dmod[1][2] = dsh, dsc, dg1b
    gs["fox_qn_w"] = dqw[0, :FOX_DH] + dqw[0, FOX_DH:]
    gs["fox_kn_w"] = dkw[0, :FOX_DH] + dkw[0, FOX_DH:]
    gs["fox_b_f"] = dbf[0, :nheads]

    dx1, dn2w0 = mlp_bwd(0, dx2, x1, h2_0, a0, u0, ymlp0, n2w0, sc2, g2)
    dyp0, dm0, dg1 = _gate_matmul_nt(dx1, g1, ymix0, wts["hg_w_out"], None, name="hg_out_bwd")
    gw["hg_w_out"] = _matmul_tn(ypre0, dm0[None], name="hg_dw_out")
    dproj0, dlb, dgn = _hg_bwd(proj0, small["hg_lb"], gn, o0, states, dyp0, name="hg_bwd")
    gw["hg_w_in"] = _matmul_tn(h1_0, dproj0, name="hg_dw_in")
    dx0, dsc, dsh, dn1w0 = _matmul_nt_lnbwd(dproj0, wts["hg_w_in"], x, n1w0, sc1, dx1, name="hg_in_bwd")
    dmod[0][0], dmod[0][1], dmod[0][2] = dsh, dsc, dg1
    gs["hg_lb"] = dlb
    gs["hg_gn_w"] = jnp.sum(dgn, axis=0)

    gs["norm1_w"] = jnp.concatenate([dn1w0, dn1w1], axis=0)
    gs["norm2_w"] = jnp.concatenate([dn2w0, dn2w1], axis=0)
    gs["dmod"] = jnp.stack([jnp.concatenate(dmod[i], axis=1)[0] for i in range(2)])
    return loss, dx0, gw, gs


SMALL_NAMES = ["norm1_w", "norm2_w", "hg_lb", "hg_gn_w", "fox_b_f", "fox_qn_w", "fox_kn_w", "final_w"]


def _pack_small(d, names):
    rows, offs, r0 = [], {}, 0
    for n in names:
        flat = d[n].reshape(-1)
        nr = -(-flat.shape[0] // LANES)
        rows.append(jnp.pad(flat, (0, nr * LANES - flat.shape[0])).reshape(nr, LANES))
        offs[n] = (r0, nr)
        r0 += nr
    return jnp.concatenate(rows, axis=0), offs


def _unpack_small(packed, offs, name, like):
    r0, nr = offs[name]
    return packed[r0:r0 + nr].reshape(-1)[:like.size].reshape(like.shape)


def kernel(x, c, w_mod, b_mod, norm1_w, norm2_w, hg_w_in, hg_w_out, hg_lb, hg_gn_w, fox_w_in, fox_b_f, fox_qn_w, fox_kn_w, fox_w_out, mlp_w1, mlp_w2, final_w, loss_target, m_w_mod, m_b_mod, m_norm1_w, m_norm2_w, m_hg_w_in, m_hg_w_out, m_hg_lb, m_hg_gn_w, m_fox_w_in, m_fox_b_f, m_fox_qn_w, m_fox_kn_w, m_fox_w_out, m_mlp_w1, m_mlp_w2, m_final_w, v_w_mod, v_b_mod, v_norm1_w, v_norm2_w, v_hg_w_in, v_hg_w_out, v_hg_lb, v_hg_gn_w, v_fox_w_in, v_fox_b_f, v_fox_qn_w, v_fox_kn_w, v_fox_w_out, v_mlp_w1, v_mlp_w2, v_final_w):
    S, D = x.shape[1], x.shape[2]
    nheads = D // FOX_DH
    ax, ay, ac = _mesh_pos()
    chip = 2 * ax + ay
    dev = 2 * chip + ac
    xs, tgt = x.reshape(S, D), loss_target.reshape(S, D)

    c_all = _allgather_small(_pad_rows(c.reshape(-1, LANES), 8), name="gather_c")
    c_all = c_all.reshape(N_DEV, -1)[:, :D]
    c16 = _pad_rows(c_all, 16)
    nmod = w_mod.shape[2]
    b_shard = lax.dynamic_slice_in_dim(b_mod, chip * nmod, nmod, axis=1)
    mod_shard = _mod_fwd(c16, w_mod, b_shard[:, None, :], name="mod_fwd")[:, :N_DEV]
    mod_all = _allgather_small(mod_shard.reshape(-1, LANES), name="gather_mod")
    mod_all = mod_all.reshape(N_CHIPS, 2, 2, N_DEV, nmod)[:, 0]
    mod = lax.dynamic_index_in_dim(mod_all, dev, axis=2, keepdims=False)
    mod = mod.transpose(1, 0, 2).reshape(2, N_CHIPS * nmod)

    fox_rows = fox_w_in.shape[2]
    segs = [hg_w_in[0], hg_w_out[0], fox_w_out[0], mlp_w1.reshape(2 * D, D), mlp_w2.reshape(2 * D, D),
            fox_w_in[0].reshape(fox_rows, D)]
    seg_rows = [s.shape[0] for s in segs]
    seg_off = [sum(seg_rows[:i]) for i in range(len(segs))]
    slab = _pad_rows(jnp.concatenate([s.astype(BF) for s in segs], axis=0), 32)
    R = slab.shape[0]
    gathered = _allgather_chip_slabs(slab, name="gather_weights")

    def seg(i):
        return gathered[:, seg_off[i]:seg_off[i] + seg_rows[i], :]

    col = lambda g: g.transpose(1, 0, 2).reshape(g.shape[1], -1)
    rowsh = lambda g: g.reshape(-1, g.shape[2])
    w1 = seg(3).reshape(N_CHIPS, 2, D, D)
    w2 = seg(4).reshape(N_CHIPS, 2, D, D)
    fox_in = col(seg(5).reshape(N_CHIPS, D, fox_rows))
    wts = {
        "hg_w_in": col(seg(0)), "hg_w_out": rowsh(seg(1)), "fox_w_out": rowsh(seg(2)),
        "mlp_w1_0": col(w1[:, 0]), "mlp_w1_1": col(w1[:, 1]), "mlp_w2_0": rowsh(w2[:, 0]), "mlp_w2_1": rowsh(w2[:, 1]),
        "fox_w_in": jnp.pad(fox_in, ((0, 0), (0, 5 * D - fox_in.shape[1]))),
    }
    small = {"norm1_w": norm1_w, "norm2_w": norm2_w, "hg_lb": hg_lb, "hg_gn_w": hg_gn_w, "fox_b_f": fox_b_f,
             "fox_qn_w": fox_qn_w, "fox_kn_w": fox_kn_w, "final_w": final_w}

    loss_part, grad_x, gw, gs = _local_step(xs, tgt, mod, wts, small)
    loss = lax.psum(loss_part[0, 0], ("x", "y", "c"))

    def uncol(g, n):
        return g.reshape(g.shape[0], N_CHIPS, n).transpose(1, 0, 2)

    gfox = uncol(gw["fox_w_in"][:, :4 * fox_rows], fox_rows).reshape(N_CHIPS, fox_rows, D)
    gsegs = [uncol(gw["hg_w_in"], D), gw["hg_w_out"].reshape(N_CHIPS, D // 4, D), gw["fox_w_out"].reshape(N_CHIPS, D // 4, D),
             jnp.concatenate([uncol(gw["mlp_w1_0"], D), uncol(gw["mlp_w1_1"], D)], axis=1),
             jnp.concatenate([gw["mlp_w2_0"].reshape(N_CHIPS, D, D), gw["mlp_w2_1"].reshape(N_CHIPS, D, D)], axis=1),
             gfox]
    gfull = jnp.concatenate(gsegs, axis=1)
    gfull = jnp.pad(gfull, ((0, 0), (0, R - gfull.shape[1]), (0, 0)))
    g4 = gfull.reshape(N_CHIPS, 2, R // 2, D)
    from_sibling = _swap_halves(g4, name="rs_swap_halves")
    chip_part = _add_halves(g4, from_sibling, ac.reshape(1), name="rs_add_halves")
    from_chips = _scatter_partials(chip_part, name="rs_scatter")
    my_half = _add_four(chip_part, from_chips, chip.reshape(1), name="rs_add_chips")
    gshard = _join_halves(my_half, name="rs_join").reshape(R, D)

    names = ["dmod"] + SMALL_NAMES
    packed, offs = _pack_small(gs, names)
    packed = _pad_rows(packed, 8)
    rp = packed.shape[0]
    parts = _allgather_small(packed, name="gather_small").reshape(N_DEV, rp, LANES)
    total = _sum_parts(parts, name="sum_small")
    r0, nr = offs["dmod"]
    dmod_all = parts[:, r0:r0 + nr].reshape(N_DEV, 2, N_CHIPS * nmod)
    dmod_shard = lax.dynamic_slice_in_dim(dmod_all, chip * nmod, nmod, axis=2).transpose(1, 0, 2)
    g_w_mod = _mod_bwd(c16, jnp.pad(dmod_shard, ((0, 0), (0, 16 - N_DEV), (0, 0))), name="mod_bwd")

    grads = {"w_mod": g_w_mod, "b_mod": _unpack_small(total, offs, "dmod", b_mod)}
    for n in SMALL_NAMES:
        grads[n] = _unpack_small(total, offs, n, small[n])

    given = dict(w_mod=(w_mod, m_w_mod, v_w_mod), b_mod=(b_mod, m_b_mod, v_b_mod), norm1_w=(norm1_w, m_norm1_w, v_norm1_w),
                 norm2_w=(norm2_w, m_norm2_w, v_norm2_w), hg_w_in=(hg_w_in, m_hg_w_in, v_hg_w_in),
                 hg_w_out=(hg_w_out, m_hg_w_out, v_hg_w_out), hg_lb=(hg_lb, m_hg_lb, v_hg_lb),
                 hg_gn_w=(hg_gn_w, m_hg_gn_w, v_hg_gn_w), fox_w_in=(fox_w_in, m_fox_w_in, v_fox_w_in),
                 fox_b_f=(fox_b_f, m_fox_b_f, v_fox_b_f), fox_qn_w=(fox_qn_w, m_fox_qn_w, v_fox_qn_w),
                 fox_kn_w=(fox_kn_w, m_fox_kn_w, v_fox_kn_w), fox_w_out=(fox_w_out, m_fox_w_out, v_fox_w_out),
                 mlp_w1=(mlp_w1, m_mlp_w1, v_mlp_w1), mlp_w2=(mlp_w2, m_mlp_w2, v_mlp_w2), final_w=(final_w, m_final_w, v_final_w))
    upd = {}

    big = [("hg_w_in", 0), ("hg_w_out", 1), ("fox_w_out", 2), ("mlp_w1", 3), ("mlp_w2", 4), ("fox_w_in", 5)]
    for n, i in big:
        w, m, v = given[n]
        flat = lambda a: a.reshape(seg_rows[i], D)
        d, mn, vn = _adamw(flat(w), gshard, flat(m), flat(v), g_row0=seg_off[i], name=f"adamw_{n}")
        grads[n] = gshard[seg_off[i]:seg_off[i] + seg_rows[i]].reshape(w.shape)
        upd[n] = tuple(a.reshape(w.shape) for a in (d, mn, vn))

    w, m, v = given["w_mod"]
    flat = lambda a: a.reshape(-1, nmod)
    upd["w_mod"] = tuple(a.reshape(w.shape) for a in _adamw(flat(w), flat(g_w_mod), flat(m), flat(v), name="adamw_w_mod"))

    snames = ["b_mod"] + SMALL_NAMES
    pw, soffs = _pack_small({n: given[n][0] for n in snames}, snames)
    pm, _ = _pack_small({n: given[n][1] for n in snames}, snames)
    pv, _ = _pack_small({n: given[n][2] for n in snames}, snames)
    pg, _ = _pack_small({n: grads[n] for n in snames}, snames)
    pw, pm, pv, pg = (_pad_rows(a, 8) for a in (pw, pm, pv, pg))
    sd, smn, svn = _adamw(pw, pg, pm, pv, name="adamw_small")
    for n in snames:
        like = given[n][0]
        upd[n] = tuple(_unpack_small(a, soffs, n, like) for a in (sd, smn, svn))

    order = ["w_mod", "b_mod", "norm1_w", "norm2_w", "hg_w_in", "hg_w_out", "hg_lb", "hg_gn_w", "fox_w_in", "fox_b_f",
             "fox_qn_w", "fox_kn_w", "fox_w_out", "mlp_w1", "mlp_w2", "final_w"]
    return (loss, grad_x.reshape(x.shape), *[grads[n] for n in order], *[upd[n][0] for n in order],
            *[upd[n][1] for n in order], *[upd[n][2] for n in order])
```

```python
import math

import jax
import jax.numpy as jnp
from jax import lax
from jax.experimental import pallas as pl
from jax.experimental.pallas import tpu as pltpu

EPS = 1e-6
ADAM_LR, ADAM_B1, ADAM_B2, ADAM_EPS, ADAM_WD, ADAM_STEP = 0.001, 0.9, 0.999, 1e-08, 0.01, 10

F32 = jnp.float32
BF = jnp.bfloat16
LANES = 128
HG_CHUNK = 64
HG_HEADS_PER_STEP = 4
LOG2E = 1.4426950408889634
FOX_DH = 64
N_CHIPS = 4
N_DEV = 8
VMEM_LIMIT = 48 * 1024 * 1024
MESH = pl.DeviceIdType.MESH

NT = (((1,), (1,)), ((), ()))
TN = (((0,), (0,)), ((), ()))


def _pick(n, pref, mult=LANES):
    if n <= pref:
        return n
    t = (pref // mult) * mult
    while t >= mult:
        if n % t == 0:
            return t
        t -= mult
    raise ValueError((n, pref, mult))


def _cp(*sem):
    return pltpu.CompilerParams(dimension_semantics=sem, vmem_limit_bytes=VMEM_LIMIT)


def _dot(a, b):
    return jnp.dot(a, b, preferred_element_type=F32)


def _dg(a, b, dims):
    return lax.dot_general(a, b, dims, preferred_element_type=F32)


def _split3(x):
    hi = x.astype(BF)
    r1 = x - hi.astype(F32)
    mid = r1.astype(BF)
    lo = (r1 - mid.astype(F32)).astype(BF)
    return hi, mid, lo


def _tri_dot(tri, x):
    hi, mid, lo = _split3(x)
    return _dot(tri, hi) + _dot(tri, mid) + _dot(tri, lo)


def _dg3(a, b, dims):
    ah, bh = a.astype(BF), b.astype(BF)
    al, bl = (a - ah.astype(F32)).astype(BF), (b - bh.astype(F32)).astype(BF)
    return _dg(ah, bh, dims) + _dg(ah, bl, dims) + _dg(al, bh, dims)


NN = (((1,), (0,)), ((), ()))


def _sigmoid(x):
    return jax.nn.sigmoid(x)


def _ln_matmul(x, nw, sc, sh, w, *, relu2, name):
    S, D = x.shape
    N = w.shape[1]
    tm, tn = _pick(S, 512, 16), _pick(N, 512)

    def body(x_ref, nw_ref, sc_ref, sh_ref, w_ref, *rest):
        outs, hs = rest[:-1], rest[-1]
        h_ref = outs[-1]

        @pl.when(pl.program_id(1) == 0)
        def _():
            xv = x_ref[...]
            r = lax.rsqrt(jnp.mean(xv * xv, axis=-1, keepdims=True) + EPS)
            hb = ((xv * r * nw_ref[...]) * (1.0 + sc_ref[...]) + sh_ref[...]).astype(BF)
            hs[...] = hb
            h_ref[...] = hb

        z = _dot(hs[...], w_ref[...])
        if relu2:
            a = jnp.maximum(z, 0.0)
            outs[0][...] = a.astype(BF)
            outs[1][...] = (a * a).astype(BF)
        else:
            outs[0][...] = z

    vec = pl.BlockSpec((1, D), lambda i, j: (0, 0))
    tile = pl.BlockSpec((tm, tn), lambda i, j: (i, j))
    if relu2:
        out_shape = [jax.ShapeDtypeStruct((S, N), BF), jax.ShapeDtypeStruct((S, N), BF)]
        out_specs = [tile, tile]
    else:
        out_shape = [jax.ShapeDtypeStruct((S, N), F32)]
        out_specs = [tile]
    out_shape.append(jax.ShapeDtypeStruct((S, D), BF))
    out_specs.append(pl.BlockSpec((tm, D), lambda i, j: (i, 0)))
    return pl.pallas_call(
        body, name=name, grid=(S // tm, N // tn),
        in_specs=[pl.BlockSpec((tm, D), lambda i, j: (i, 0)), vec, vec, vec,
                  pl.BlockSpec((D, tn), lambda i, j: (0, j))],
        out_specs=out_specs, out_shape=out_shape,
        scratch_shapes=[pltpu.VMEM((tm, D), BF)],
        compiler_params=_cp("parallel", "arbitrary"),
    )(x, nw, sc, sh, w)


def _matmul_resid(a, w, x, gate, *, name):
    S, K = a.shape
    D = w.shape[1]
    tm, tn = _pick(S, 512, 16), _pick(D, 512)

    def body(a_ref, w_ref, x_ref, g_ref, o_ref, y_ref):
        y = _dot(a_ref[...], w_ref[...])
        y_ref[...] = y.astype(BF)
        o_ref[...] = x_ref[...] + g_ref[...] * y

    tile = pl.BlockSpec((tm, tn), lambda i, j: (i, j))
    return pl.pallas_call(
        body, name=name, grid=(S // tm, D // tn),
        in_specs=[pl.BlockSpec((tm, K), lambda i, j: (i, 0)), pl.BlockSpec((K, tn), lambda i, j: (0, j)),
                  tile, pl.BlockSpec((1, tn), lambda i, j: (0, j))],
        out_specs=[tile, tile],
        out_shape=[jax.ShapeDtypeStruct((S, D), F32), jax.ShapeDtypeStruct((S, D), BF)],
        compiler_params=_cp("parallel", "arbitrary"),
    )(a, w, x, gate)


def _gate_matmul_nt(dx, gate, y, w, act, *, name):
    S, D = dx.shape
    K = w.shape[0]
    tm, tn = _pick(S, 512, 16), _pick(K, 512)
    fused = act is not None

    def body(dx_ref, g_ref, y_ref, w_ref, *rest):
        if fused:
            act_ref, da_ref, dm_ref, dg_ref, ms = rest
        else:
            da_ref, dm_ref, dg_ref, ms = rest
        i, j = pl.program_id(0), pl.program_id(1)

        @pl.when((i == 0) & (j == 0))
        def _():
            dg_ref[...] = jnp.zeros_like(dg_ref)

        @pl.when(j == 0)
        def _():
            dxv = dx_ref[...]
            dmb = (dxv * g_ref[...]).astype(BF)
            ms[...] = dmb
            dm_ref[...] = dmb
            dg_ref[...] += jnp.sum(dxv * y_ref[...].astype(F32), axis=0, keepdims=True)

        da = _dg(ms[...], w_ref[...], NT)
        if fused:
            da_ref[...] = (da * (2.0 * act_ref[...].astype(F32))).astype(BF)
        else:
            da_ref[...] = da

    row = pl.BlockSpec((tm, D), lambda i, j: (i, 0))
    vec = pl.BlockSpec((1, D), lambda i, j: (0, 0))
    tile = pl.BlockSpec((tm, tn), lambda i, j: (i, j))
    in_specs = [row, vec, row, pl.BlockSpec((tn, D), lambda i, j: (j, 0))]
    args = [dx, gate, y, w]
    if fused:
        in_specs.append(tile)
        args.append(act)
    return pl.pallas_call(
        body, name=name, grid=(S // tm, K // tn),
        in_specs=in_specs, out_specs=[tile, row, vec],
        out_shape=[jax.ShapeDtypeStruct((S, K), BF if fused else F32), jax.ShapeDtypeStruct((S, D), BF),
                   jax.ShapeDtypeStruct((1, D), F32)],
        scratch_shapes=[pltpu.VMEM((tm, D), BF)],
        compiler_params=_cp("arbitrary", "arbitrary"),
    )(*args)


def _matmul_tn(a, b, *, name):
    S, Ka = a.shape
    P, _, Db = b.shape
    tk, tn, ts = _pick(Ka, 1024), _pick(Db, 1024), _pick(S, 512, 16)
    npb = Db // tn

    def body(a_ref, b_ref, o_ref, acc):
        s = pl.program_id(2)

        @pl.when(s == 0)
        def _():
            acc[...] = jnp.zeros_like(acc)

        acc[...] += _dg(a_ref[...], b_ref[...], TN)

        @pl.when(s == pl.num_programs(2) - 1)
        def _():
            o_ref[...] = acc[...]

    return pl.pallas_call(
        body, name=name, grid=(Ka // tk, P * npb, S // ts),
        in_specs=[pl.BlockSpec((ts, tk), lambda i, j, s: (s, i)),
                  pl.BlockSpec((None, ts, tn), lambda i, j, s: (j // npb, s, j % npb))],
        out_specs=pl.BlockSpec((tk, tn), lambda i, j, s: (i, j)),
        out_shape=jax.ShapeDtypeStruct((Ka, P * Db), F32),
        scratch_shapes=[pltpu.VMEM((tk, tn), F32)],
        compiler_params=_cp("parallel", "parallel", "arbitrary"),
    )(a, b)


def _matmul_nt_lnbwd(g, w, x, nw, sc, dx_out, *, name):
    P, S, Dg = g.shape
    D = x.shape[1]
    tm, tk = _pick(S, 512, 16), _pick(Dg, 1024)
    npb = Dg // tk
    nk = P * npb

    def body(g_ref, w_ref, x_ref, nw_ref, sc_ref, dxo_ref, dx_ref, dsc_ref, dsh_ref, dnw_ref, acc):
        i, k = pl.program_id(0), pl.program_id(1)

        @pl.when((i == 0) & (k == 0))
        def _():
            dsc_ref[...] = jnp.zeros_like(dsc_ref)
            dsh_ref[...] = jnp.zeros_like(dsh_ref)
            dnw_ref[...] = jnp.zeros_like(dnw_ref)

        @pl.when(k == 0)
        def _():
            acc[...] = jnp.zeros_like(acc)

        acc[...] += _dg(g_ref[...], w_ref[...], NT)

        @pl.when(k == nk - 1)
        def _():
            dh = acc[...]
            xv = x_ref[...]
            nwv = nw_ref[...]
            r = lax.rsqrt(jnp.mean(xv * xv, axis=-1, keepdims=True) + EPS)
            xr = xv * r
            dn = dh * (1.0 + sc_ref[...])
            dsc_ref[...] += jnp.sum(dh * (xr * nwv), axis=0, keepdims=True)
            dsh_ref[...] += jnp.sum(dh, axis=0, keepdims=True)
            dnw_ref[...] += jnp.sum(dn * xr, axis=0, keepdims=True)
            u = dn * nwv
            dx_ref[...] = dxo_ref[...] + r * (u - xr * jnp.mean(u * xr, axis=-1, keepdims=True))

    row = pl.BlockSpec((tm, D), lambda i, k: (i, 0))
    vec = pl.BlockSpec((1, D), lambda i, k: (0, 0))
    return pl.pallas_call(
        body, name=name, grid=(S // tm, nk),
        in_specs=[pl.BlockSpec((None, tm, tk), lambda i, k: (k // npb, i, k % npb)),
                  pl.BlockSpec((D, tk), lambda i, k: (0, k)), row, vec, vec, row],
        out_specs=[row, vec, vec, vec],
        out_shape=[jax.ShapeDtypeStruct((S, D), F32)] + [jax.ShapeDtypeStruct((1, D), F32)] * 3,
        scratch_shapes=[pltpu.VMEM((tm, D), F32)],
        compiler_params=_cp("arbitrary", "arbitrary"),
    )(g, w, x, nw, sc, dx_out)


def _loss_kernel(x, fw, tgt, *, name):
    S, D = x.shape
    tm = _pick(S, 512, 8)

    def body(x_ref, fw_ref, t_ref, l_ref, dx_ref, dfw_ref):
        @pl.when(pl.program_id(0) == 0)
        def _():
            l_ref[...] = jnp.zeros_like(l_ref)
            dfw_ref[...] = jnp.zeros_like(dfw_ref)

        xv = x_ref[...]
        fwv = fw_ref[...]
        r = lax.rsqrt(jnp.mean(xv * xv, axis=-1, keepdims=True) + EPS)
        xr = xv * r
        err = xr * fwv - t_ref[...]
        per_tok = jnp.mean(err * err, axis=-1, keepdims=True)
        l_ref[...] += 0.5 * jnp.sum(per_tok, axis=0, keepdims=True)
        dy = err * (1.0 / D)
        dfw_ref[...] += jnp.sum(dy * xr, axis=0, keepdims=True)
        u = dy * fwv
        dx_ref[...] = r * (u - xr * jnp.mean(u * xr, axis=-1, keepdims=True))

    row = pl.BlockSpec((tm, D), lambda i: (i, 0))
    vec = pl.BlockSpec((1, D), lambda i: (0, 0))
    return pl.pallas_call(
        body, name=name, grid=(S // tm,),
        in_specs=[row, vec, row],
        out_specs=[pl.BlockSpec((1, LANES), lambda i: (0, 0)), row, vec],
        out_shape=[jax.ShapeDtypeStruct((1, LANES), F32), jax.ShapeDtypeStruct((S, D), F32),
                   jax.ShapeDtypeStruct((1, D), F32)],
        compiler_params=_cp("arbitrary"),
    )(x, fw, tgt)


def _hg_lower_bound(lb3):
    mx = jnp.max(lb3, axis=0, keepdims=True)
    e = jnp.exp(lb3 - mx)
    p = e / jnp.sum(e, axis=0, keepdims=True)
    return p[0:1, :], p


def _hg_chunk_common(qr, fz, lbv):
    sq = _sigmoid(qr)
    q = qr * sq
    sig = _sigmoid(fz)
    f = lbv + (1.0 - lbv) * sig
    k = (1.0 - lbv) * (1.0 - sig)
    return q, sq, sig, f, k, jnp.log(f)


def _row_of(x, rows, r):
    return jnp.sum(jnp.where(rows == r, x, 0.0), axis=0, keepdims=True)


def _hg_fwd(proj, hg_lb, gn, *, name):
    S = proj.shape[0]
    D = proj.shape[1] // 4
    H = D // LANES
    HB = min(HG_HEADS_PER_STEP, H)
    W = HB * LANES
    C = HG_CHUNK
    T = _pick(S, 512, C)
    nch, nb = T // C, S // T

    def body(q_ref, fz_ref, v_ref, g_ref, lb_ref, gn_ref, y_ref, o_ref, sts_ref, st):
        @pl.when(pl.program_id(1) == 0)
        def _():
            st[...] = jnp.zeros_like(st)

        lb_all, _ = _hg_lower_bound(lb_ref[...])
        gnv = gn_ref[...]
        ri = lax.broadcasted_iota(jnp.int32, (C, C), 0)
        ci_ = lax.broadcasted_iota(jnp.int32, (C, C), 1)
        low = ri >= ci_
        tri = jnp.where(low, 1.0, 0.0).astype(BF)
        rows = lax.broadcasted_iota(jnp.int32, (C, LANES), 0)

        def chunk(ci, carry):
            sl = pl.ds(pl.multiple_of(ci * C, C), C)
            for hh in range(HB):
                ls = slice(hh * LANES, (hh + 1) * LANES)
                q, _, _, _, k, logf = _hg_chunk_common(q_ref[sl, ls], fz_ref[sl, ls], lb_all[:, ls])
                vv = v_ref[sl, ls]
                gg = g_ref[sl, ls]
                G = _tri_dot(tri, logf)
                Gm = _row_of(G, rows, C // 2 - 1)
                Gl = _row_of(G, rows, C - 1)
                qt = q * jnp.exp(G - Gm)
                kt = k * jnp.exp(Gm - G)
                A = jnp.where(low, _dg3(qt, kt, NT), 0.0)
                Sv = st[hh]
                sts_ref[hh, ci] = Sv
                o = _dg3(A, vv, NN) + _dg3(q * jnp.exp(G), Sv, NT)
                st[hh] = Sv * jnp.exp(Gl) + _dg3(vv, k * jnp.exp(Gl - G), TN)
                r = lax.rsqrt(jnp.mean(o * o, axis=-1, keepdims=True) + EPS)
                y_ref[sl, ls] = ((o * r * gnv) * (gg * _sigmoid(gg))).astype(BF)
                o_ref[sl, ls] = o
            return carry

        lax.fori_loop(0, nch, chunk, 0)

    ng = H // HB

    def part(p):
        return pl.BlockSpec((T, W), lambda h, n: (n, p * ng + h))

    blk = pl.BlockSpec((T, W), lambda h, n: (n, h))
    return pl.pallas_call(
        body, name=name, grid=(ng, nb),
        in_specs=[part(0), part(1), part(2), part(3),
                  pl.BlockSpec((3, W), lambda h, n: (0, h)), pl.BlockSpec((1, LANES), lambda h, n: (0, 0))],
        out_specs=[blk, blk, pl.BlockSpec((HB, nch, LANES, LANES), lambda h, n: (h, n, 0, 0))],
        out_shape=[jax.ShapeDtypeStruct((S, D), BF), jax.ShapeDtypeStruct((S, D), F32),
                   jax.ShapeDtypeStruct((H, S // C, LANES, LANES), F32)],
        scratch_shapes=[pltpu.VMEM((HB, LANES, LANES), F32)],
        compiler_params=_cp("parallel", "arbitrary"),
    )(proj, proj, proj, proj, hg_lb, gn)


def _hg_bwd(proj, hg_lb, gn, o_all, states, dy, *, name):
    S = proj.shape[0]
    D = proj.shape[1] // 4
    H = D // LANES
    HB = min(HG_HEADS_PER_STEP, H)
    W = HB * LANES
    C = HG_CHUNK
    T = _pick(S, 512, C)
    nch, nb = T // C, S // T

    def body(q_ref, fz_ref, v_ref, g_ref, lb_ref, gn_ref, o_ref, sts_ref, dy_ref,
             dp_ref, dlb_ref, dgn_ref, dst, dlb_acc):
        n = pl.program_id(1)

        @pl.when(n == 0)
        def _():
            dst[...] = jnp.zeros_like(dst)
            dlb_acc[...] = jnp.zeros_like(dlb_acc)
            dgn_ref[...] = jnp.zeros_like(dgn_ref)

        lb_all, p3 = _hg_lower_bound(lb_ref[...])
        gnv = gn_ref[...]
        ri = lax.broadcasted_iota(jnp.int32, (C, C), 0)
        ci_ = lax.broadcasted_iota(jnp.int32, (C, C), 1)
        low = ri >= ci_
        tri = jnp.where(low, 1.0, 0.0).astype(BF)
        triu = jnp.where(ri <= ci_, 1.0, 0.0).astype(BF)
        rows = lax.broadcasted_iota(jnp.int32, (C, LANES), 0)

        def chunk(cj, carry):
            ci = nch - 1 - cj
            sl = pl.ds(pl.multiple_of(ci * C, C), C)
            for hh in range(HB):
                ls = slice(hh * LANES, (hh + 1) * LANES)
                lbv = lb_all[:, ls]
                qr = q_ref[sl, ls]
                q, sq, sig, f, k, logf = _hg_chunk_common(qr, fz_ref[sl, ls], lbv)
                vv = v_ref[sl, ls]
                gg = g_ref[sl, ls]
                o = o_ref[sl, ls]
                dyv = dy_ref[sl, ls]
                G = _tri_dot(tri, logf)
                Gm = _row_of(G, rows, C // 2 - 1)
                Gl = _row_of(G, rows, C - 1)
                eG, e_qm, e_km, e_lk, eGl = jnp.exp(G), jnp.exp(G - Gm), jnp.exp(Gm - G), jnp.exp(Gl - G), jnp.exp(Gl)
                qt = q * e_qm
                kt = k * e_km
                A = jnp.where(low, _dg3(qt, kt, NT), 0.0)
                sg = _sigmoid(gg)
                r = lax.rsqrt(jnp.mean(o * o, axis=-1, keepdims=True) + EPS)
                on = o * r
                d_onw = dyv * (gg * sg)
                dgn_ref[hh] += jnp.sum(d_onw * on, axis=0, keepdims=True)
                dgg = dyv * (on * gnv) * (sg * (1.0 + gg * (1.0 - sg)))
                u = d_onw * gnv
                do = r * (u - on * jnp.mean(u * on, axis=-1, keepdims=True))
                Sv = sts_ref[hh, ci]
                dSv = dst[hh]
                dA = jnp.where(low, _dg3(do, vv, NT), 0.0)
                kdec = k * e_lk
                dv = _dg3(A, do, TN) + _dg3(kdec, dSv, NT)
                dq = _dg3(dA, kt, NN) * e_qm + eG * _dg3(do, Sv, NN)
                dk = _dg3(dA, qt, TN) * e_km + e_lk * _dg3(vv, dSv, NN)
                s_end = Sv * eGl + _dg3(vv, kdec, TN)
                dgl = jnp.sum(dSv * s_end, axis=0, keepdims=True)
                dG = q * dq - k * dk + jnp.where(rows == C - 1, dgl, 0.0)
                dlogf = _tri_dot(triu, dG) - f * dk
                dst[hh] = dSv * eGl + _dg3(do, q * eG, TN)
                dlf_f = dlogf / f
                dlb_acc[:, ls] += jnp.sum(dlf_f * (1.0 - sig), axis=0, keepdims=True)
                dp_ref[0, sl, ls] = (dq * (sq * (1.0 + qr * (1.0 - sq)))).astype(BF)
                dp_ref[1, sl, ls] = (dlf_f * (1.0 - lbv) * sig * (1.0 - sig)).astype(BF)
                dp_ref[2, sl, ls] = dv.astype(BF)
                dp_ref[3, sl, ls] = dgg.astype(BF)
            return carry

        lax.fori_loop(0, nch, chunk, 0)
        sel = jnp.where(lax.broadcasted_iota(jnp.int32, (3, W), 0) == 0, 1.0, 0.0)
        dlb_ref[...] = lb_all * (sel - p3) * dlb_acc[...]

    ng = H // HB

    def part(p):
        return pl.BlockSpec((T, W), lambda h, n: (nb - 1 - n, p * ng + h))

    blk = pl.BlockSpec((T, W), lambda h, n: (nb - 1 - n, h))
    return pl.pallas_call(
        body, name=name, grid=(ng, nb),
        in_specs=[part(0), part(1), part(2), part(3),
                  pl.BlockSpec((3, W), lambda h, n: (0, h)), pl.BlockSpec((1, LANES), lambda h, n: (0, 0)),
                  blk, pl.BlockSpec((HB, nch, LANES, LANES), lambda h, n: (h, nb - 1 - n, 0, 0)), blk],
        out_specs=[pl.BlockSpec((4, T, W), lambda h, n: (0, nb - 1 - n, h)),
                   pl.BlockSpec((3, W), lambda h, n: (0, h)),
                   pl.BlockSpec((HB, 1, LANES), lambda h, n: (h, 0, 0))],
        out_shape=[jax.ShapeDtypeStruct((4, S, D), BF), jax.ShapeDtypeStruct((3, D), F32),
                   jax.ShapeDtypeStruct((H, 1, LANES), F32)],
        scratch_shapes=[pltpu.VMEM((HB, LANES, LANES), F32), pltpu.VMEM((1, W), F32)],
        compiler_params=_cp("parallel", "arbitrary"),
    )(proj, proj, proj, proj, hg_lb, gn, o_all, states, dy)


def _log_sigmoid(u):
    return jnp.minimum(u, 0.0) - jnp.log(1.0 + jnp.exp(-jnp.abs(u)))


def _lane_put(base, lane, first, pieces):
    for n, p in enumerate(pieces):
        base = jnp.where(lane == first + n, p, base)
    return base


def _fox_cumsum(proj, bf_pad, *, name):
    S = proj.shape[0]
    D = proj.shape[1] // 5
    T = _pick(S, 256, 8)

    def body(fz_ref, b_ref, f_ref, carry):
        @pl.when(pl.program_id(0) == 0)
        def _():
            carry[...] = jnp.zeros_like(carry)

        logf = _log_sigmoid(fz_ref[...] + b_ref[...])
        tri = jnp.where(lax.broadcasted_iota(jnp.int32, (T, T), 0) >= lax.broadcasted_iota(jnp.int32, (T, T), 1),
                        1.0, 0.0).astype(BF)
        fv = _tri_dot(tri, logf) + carry[...]
        f_ref[...] = fv
        carry[...] = _row_of(fv, lax.broadcasted_iota(jnp.int32, (T, LANES), 0), T - 1)

    return pl.pallas_call(
        body, name=name, grid=(S // T,),
        in_specs=[pl.BlockSpec((T, LANES), lambda i: (i, 4 * D // LANES)), pl.BlockSpec((1, LANES), lambda i: (0, 0))],
        out_specs=pl.BlockSpec((T, LANES), lambda i: (i, 0)),
        out_shape=jax.ShapeDtypeStruct((S, LANES), F32),
        scratch_shapes=[pltpu.VMEM((1, LANES), F32)],
        compiler_params=_cp("arbitrary"),
    )(proj, bf_pad)


def _pair_stats(sq, lo):
    s_lo = jnp.sum(jnp.where(lo, sq, 0.0), axis=-1, keepdims=True)
    s_hi = jnp.sum(jnp.where(lo, 0.0, sq), axis=-1, keepdims=True)
    return jnp.where(lo, s_lo, s_hi) * (1.0 / FOX_DH)


def _fox_prep(proj, fcum, qw2, kw2, *, name):
    S = proj.shape[0]
    D = proj.shape[1] // 5
    HP = D // LANES
    T = _pick(S, 512, 16)

    def body(q_ref, k_ref, v_ref, f_ref, qw_ref, kw_ref, qa_ref, ka_ref, va_ref):
        hp = pl.program_id(1)
        lane = lax.broadcasted_iota(jnp.int32, (T, LANES), 1)
        lo = lane < FOX_DH
        qv, kv, vv, fv = q_ref[...], k_ref[...], v_ref[...], f_ref[...]
        qn = qv * lax.rsqrt(_pair_stats(qv * qv, lo) + EPS) * qw_ref[...] * (0.125 * LOG2E)
        kn = kv * lax.rsqrt(_pair_stats(kv * kv, lo) + EPS) * kw_ref[...]
        ones_q = jnp.where((lane >= 67) & (lane <= 69), 1.0, 0.0)
        ones_k = jnp.where(((lane >= 64) & (lane <= 66)) | ((lane >= 70) & (lane <= 72)), 1.0, 0.0)
        ones_v = jnp.where((lane >= 64) & (lane <= 66), 1.0, 0.0)
        for hh in range(2):
            fh = jnp.sum(jnp.where(lane == 2 * hp + hh, fv, 0.0), axis=-1, keepdims=True) * LOG2E
            pieces = [p.astype(F32) for p in _split3(fh)]

            def half(x):
                return jnp.where(lo, x if hh == 0 else pltpu.roll(x, FOX_DH, 1), 0.0)

            qa_ref[hh] = _lane_put(half(qn) + ones_q, lane, 64, pieces).astype(BF)
            ka_ref[hh] = _lane_put(half(kn) + ones_k, lane, 67, [-p for p in pieces]).astype(BF)
            va_ref[hh] = (half(vv) + ones_v).astype(BF)

    def part(p):
        return pl.BlockSpec((T, LANES), lambda i, hp: (i, p * HP + hp))

    vec = pl.BlockSpec((1, LANES), lambda i, hp: (0, 0))
    aug = pl.BlockSpec((2, T, LANES), lambda i, hp: (hp, i, 0))
    return pl.pallas_call(
        body, name=name, grid=(S // T, HP),
        in_specs=[part(0), part(1), part(2), pl.BlockSpec((T, LANES), lambda i, hp: (i, 0)), vec, vec],
        out_specs=[aug, aug, aug],
        out_shape=[jax.ShapeDtypeStruct((2 * HP, S, LANES), BF)] * 3,
        compiler_params=_cp("parallel", "arbitrary"),
    )(proj, proj, proj, fcum, qw2, kw2)


def _fox_fwd(qa, ka, va, proj, *, name):
    H, S, _ = qa.shape
    HP = H // 2
    D = HP * LANES
    B = _pick(S, 256, 16)
    nq = S // B

    def body(q_ref, k_ref, v_ref, g_ref, y_ref, o_ref, q2_ref):
        i = pl.program_id(1)
        lane = lax.broadcasted_iota(jnp.int32, (B, LANES), 1)
        lo = lane < FOX_DH
        causal = lax.broadcasted_iota(jnp.int32, (B, B), 1) <= lax.broadcasted_iota(jnp.int32, (B, B), 0)
        qbs = [q_ref[0], q_ref[1]]

        def block(hh, j, m, acc, masked):
            sl = pl.ds(pl.multiple_of(j * B, B), B)
            s = _dg(qbs[hh], k_ref[hh, sl, :], NT)
            if masked:
                s = jnp.where(causal, s, -jnp.inf)
            m_new = jnp.maximum(m, jnp.max(s, axis=-1, keepdims=True))
            p = jnp.exp2(s - m_new)
            ph = p.astype(BF)
            pl_ = (p - ph.astype(F32)).astype(BF)
            vb = v_ref[hh, sl, :]
            return m_new, acc * jnp.exp2(m - m_new) + (_dot(ph, vb) + _dot(pl_, vb))

        def steps(js, carry, masked=False):
            carry = list(carry)
            for j in js:
                for hh in range(2):
                    carry[2 * hh], carry[2 * hh + 1] = block(hh, j, carry[2 * hh], carry[2 * hh + 1], masked)
            return tuple(carry)

        m0, acc0 = jnp.full((B, 1), -jnp.inf, F32), jnp.zeros((B, LANES), F32)
        carry = lax.fori_loop(0, i // 2, lambda jj, c: steps((2 * jj, 2 * jj + 1), c), (m0, acc0, m0, acc0))
        carry = lax.fori_loop(2 * (i // 2), i, lambda j, c: steps((j,), c), carry)
        carry = steps((i,), carry, masked=True)
        halves = []
        for hh in range(2):
            m, acc = carry[2 * hh], carry[2 * hh + 1]
            l = jnp.sum(jnp.where(lane == FOX_DH, acc, 0.0), axis=-1, keepdims=True)
            halves.append(acc / l)
            neg_lse = [-(p.astype(F32)) for p in _split3(m + jnp.log2(l))]
            q2_ref[hh] = _lane_put(qbs[hh].astype(F32), lane, 70, neg_lse).astype(BF)
        o = jnp.where(lo, halves[0], pltpu.roll(halves[1], FOX_DH, 1))
        o_ref[...] = o
        y_ref[...] = (o * _sigmoid(g_ref[...])).astype(BF)

    blk = pl.BlockSpec((B, LANES), lambda hp, i: (i, hp))
    qblk = pl.BlockSpec((2, B, LANES), lambda hp, i: (hp, i, 0))
    full = pl.BlockSpec((2, S, LANES), lambda hp, i: (hp, 0, 0))
    return pl.pallas_call(
        body, name=name, grid=(HP, nq),
        in_specs=[qblk, full, full, pl.BlockSpec((B, LANES), lambda hp, i: (i, 3 * HP + hp))],
        out_specs=[blk, blk, qblk],
        out_shape=[jax.ShapeDtypeStruct((S, D), BF), jax.ShapeDtypeStruct((S, D), F32),
                   jax.ShapeDtypeStruct((H, S, LANES), BF)],
        compiler_params=_cp("parallel", "arbitrary"),
    )(qa, ka, va, proj)


def _fox_bwd_prep(dy, o, proj, *, name):
    S, D = dy.shape
    HP = D // LANES
    T = _pick(S, 512, 16)

    def body(dy_ref, o_ref, g_ref, da_ref):
        lane = lax.broadcasted_iota(jnp.int32, (T, LANES), 1)
        lo = lane < FOX_DH
        do = (dy_ref[...] * _sigmoid(g_ref[...])).astype(BF).astype(F32)
        prod = do * o_ref[...]
        d_lo = jnp.sum(jnp.where(lo, prod, 0.0), axis=-1, keepdims=True)
        d_hi = jnp.sum(jnp.where(lo, 0.0, prod), axis=-1, keepdims=True)
        for hh, delta in enumerate((d_lo, d_hi)):
            base = jnp.where(lo, do if hh == 0 else pltpu.roll(do, FOX_DH, 1), 0.0)
            da_ref[hh] = _lane_put(base, lane, 64, [-(p.astype(F32)) for p in _split3(delta)]).astype(BF)

    blk = pl.BlockSpec((T, LANES), lambda i, hp: (i, hp))
    return pl.pallas_call(
        body, name=name, grid=(S // T, HP),
        in_specs=[blk, blk, pl.BlockSpec((T, LANES), lambda i, hp: (i, 3 * HP + hp))],
        out_specs=pl.BlockSpec((2, T, LANES), lambda i, hp: (hp, i, 0)),
        out_shape=jax.ShapeDtypeStruct((2 * HP, S, LANES), BF),
        compiler_params=_cp("parallel", "arbitrary"),
    )(dy, o, proj)


def _fox_bwd(q2, ka, va, doa, *, name):
    H, S, _ = q2.shape
    B = _pick(S, 256, 16)
    nb = S // B

    def body(q_ref, do_ref, k_ref, v_ref, dq_ref, dk_ref, dv_ref, cs_ref):
        j = pl.program_id(1)

        @pl.when(j == 0)
        def _():
            dq_ref[...] = jnp.zeros_like(dq_ref)

        kb, vb = k_ref[...], v_ref[...]
        causal = lax.broadcasted_iota(jnp.int32, (B, B), 1) <= lax.broadcasted_iota(jnp.int32, (B, B), 0)

        def step(i, carry, masked=False):
            dk_acc, dv_acc, cs_acc = carry
            sl = pl.ds(pl.multiple_of(i * B, B), B)
            qb, dob = q_ref[sl, :], do_ref[sl, :]
            s = _dg(qb, kb, NT)
            if masked:
                s = jnp.where(causal, s, -jnp.inf)
            p = jnp.exp2(s)
            ds = p * _dg(dob, vb, NT)
            dsb = ds.astype(BF)
            cs_acc = cs_acc + jnp.sum(ds.reshape(B // 8, 8, B), axis=0)
            dv_acc = dv_acc + _dg(p.astype(BF), dob, TN)
            dk_acc = dk_acc + _dg(dsb, qb, TN)
            dq_ref[sl, :] += _dot(dsb, kb)
            return dk_acc, dv_acc, cs_acc

        zero = jnp.zeros((B, LANES), F32)
        carry = step(j, (zero, zero, jnp.zeros((8, B), F32)), masked=True)
        npair = (nb - 1 - j) // 2
        carry = lax.fori_loop(0, npair, lambda ii, c: step(j + 2 + 2 * ii, step(j + 1 + 2 * ii, c)), carry)
        dk_acc, dv_acc, cs_acc = lax.fori_loop(j + 1 + 2 * npair, nb, step, carry)
        dk_ref[...] = dk_acc
        dv_ref[...] = dv_acc
        cs_ref[...] = jnp.sum(cs_acc, axis=0, keepdims=True)

    full = pl.BlockSpec((None, S, LANES), lambda h, j: (h, 0, 0))
    blk = pl.BlockSpec((None, B, LANES), lambda h, j: (h, j, 0))
    return pl.pallas_call(
        body, name=name, grid=(H, nb),
        in_specs=[full, full, blk, blk],
        out_specs=[full, blk, blk, pl.BlockSpec((None, 1, B), lambda h, j: (h, 0, j))],
        out_shape=[jax.ShapeDtypeStruct((H, S, LANES), F32)] * 3 + [jax.ShapeDtypeStruct((H, 1, S), F32)],
        compiler_params=_cp("parallel", "arbitrary"),
    )(q2, doa, ka, va)


def _fox_bwd_post(dqa, dka, dva, proj, dy, o, qw2, kw2, *, name):
    S, D = dy.shape
    HP = D // LANES
    T = _pick(S, 512, 16)

    def body(dq_ref, dk_ref, dv_ref, q_ref, k_ref, g_ref, dy_ref, o_ref, qw_ref, kw_ref, dp_ref, dqw_ref, dkw_ref):
        @pl.when((pl.program_id(0) == 0) & (pl.program_id(1) == 0))
        def _():
            dqw_ref[...] = jnp.zeros_like(dqw_ref)
            dkw_ref[...] = jnp.zeros_like(dkw_ref)

        lane = lax.broadcasted_iota(jnp.int32, (T, LANES), 1)
        lo = lane < FOX_DH

        def pair(ref):
            return jnp.where(lo, ref[0], pltpu.roll(ref[1], FOX_DH, 1))

        def norm_bwd(xv, w, dyn, dw_ref):
            r = lax.rsqrt(_pair_stats(xv * xv, lo) + EPS)
            xr = xv * r
            dw_ref[...] += jnp.sum(dyn * xr, axis=0, keepdims=True)
            u = dyn * w
            return r * (u - xr * _pair_stats(u * xr, lo))

        dp_ref[0] = norm_bwd(q_ref[...], qw_ref[...], pair(dq_ref) * 0.125, dqw_ref).astype(BF)
        dp_ref[1] = norm_bwd(k_ref[...], kw_ref[...], pair(dk_ref) * (1.0 / LOG2E), dkw_ref).astype(BF)
        dp_ref[2] = pair(dv_ref).astype(BF)
        sg = _sigmoid(g_ref[...])
        dp_ref[3] = (dy_ref[...] * o_ref[...] * sg * (1.0 - sg)).astype(BF)

    def part(p):
        return pl.BlockSpec((T, LANES), lambda i, hp: (i, p * HP + hp))

    aug = pl.BlockSpec((2, T, LANES), lambda i, hp: (hp, i, 0))
    blk = pl.BlockSpec((T, LANES), lambda i, hp: (i, hp))
    vec = pl.BlockSpec((1, LANES), lambda i, hp: (0, 0))
    return pl.pallas_call(
        body, name=name, grid=(S // T, HP),
        in_specs=[aug, aug, aug, part(0), part(1), part(3), blk, blk, vec, vec],
        out_specs=[pl.BlockSpec((4, T, LANES), lambda i, hp: (0, i, hp)), vec, vec],
        out_shape=[jax.ShapeDtypeStruct((5, S, D), BF), jax.ShapeDtypeStruct((1, LANES), F32),
                   jax.ShapeDtypeStruct((1, LANES), F32)],
        compiler_params=_cp("arbitrary", "arbitrary"),
    )(dqa, dka, dva, proj, proj, proj, dy, o, qw2, kw2)


def _fox_dfz(colsum, nheads, proj, bf_pad, dproj, *, name):
    S = colsum.shape[0]
    H = nheads
    D = dproj.shape[2]
    T = _pick(S, 256, 16)
    nb = S // T

    def body(cs_ref, fz_ref, b_ref, _, dp_ref, db_ref, carry):
        @pl.when(pl.program_id(0) == 0)
        def _():
            carry[...] = jnp.zeros_like(carry)
            db_ref[...] = jnp.zeros_like(db_ref)

        lane = lax.broadcasted_iota(jnp.int32, (T, LANES), 1)
        df = -cs_ref[...]
        triu = jnp.where(lax.broadcasted_iota(jnp.int32, (T, T), 0) <= lax.broadcasted_iota(jnp.int32, (T, T), 1),
                         1.0, 0.0).astype(BF)
        dlogf = _tri_dot(triu, df) + carry[...]
        carry[...] = _row_of(dlogf, lax.broadcasted_iota(jnp.int32, (T, LANES), 0), 0)
        dfz = jnp.where(lane < H, dlogf * _sigmoid(-(fz_ref[...] + b_ref[...])), 0.0)
        db_ref[...] += jnp.sum(dfz, axis=0, keepdims=True)
        dp_ref[...] = jnp.zeros_like(dp_ref)
        dp_ref[:, 0:LANES] = dfz.astype(BF)

    return pl.pallas_call(
        body, name=name, grid=(nb,),
        in_specs=[pl.BlockSpec((T, LANES), lambda i: (nb - 1 - i, 0)),
                  pl.BlockSpec((T, LANES), lambda i: (nb - 1 - i, 4 * D // LANES)),
                  pl.BlockSpec((1, LANES), lambda i: (0, 0)),
                  pl.BlockSpec(memory_space=pl.ANY)],
        out_specs=[pl.BlockSpec((None, T, D), lambda i: (4, nb - 1 - i, 0)), pl.BlockSpec((1, LANES), lambda i: (0, 0))],
        out_shape=[jax.ShapeDtypeStruct(dproj.shape, BF), jax.ShapeDtypeStruct((1, LANES), F32)],
        scratch_shapes=[pltpu.VMEM((1, LANES), F32)],
        input_output_aliases={3: 0},
        compiler_params=_cp("arbitrary"),
    )(colsum, proj, bf_pad, dproj)


def _mod_fwd(c16, w, b, *, name):
    L, D, N = w.shape
    tn = _pick(N, 512)

    def body(c_ref, w_ref, b_ref, o_ref):
        cv = c_ref[...]
        ca = (cv * _sigmoid(cv)).astype(BF)
        o_ref[...] = _dot(ca, w_ref[...].astype(BF)) + b_ref[...]

    return pl.pallas_call(
        body, name=name, grid=(L, N // tn),
        in_specs=[pl.BlockSpec((16, D), lambda l, j: (0, 0)), pl.BlockSpec((None, D, tn), lambda l, j: (l, 0, j)),
                  pl.BlockSpec((None, 1, tn), lambda l, j: (l, 0, j))],
        out_specs=pl.BlockSpec((None, 16, tn), lambda l, j: (l, 0, j)),
        out_shape=jax.ShapeDtypeStruct((L, 16, N), F32),
        compiler_params=_cp("parallel", "arbitrary"),
    )(c16, w, b)


def _mod_bwd(c16, dmod, *, name):
    L, _, N = dmod.shape
    D = c16.shape[1]
    tn = _pick(N, 512)

    def body(c_ref, d_ref, o_ref):
        cv = c_ref[...]
        ca = (cv * _sigmoid(cv)).astype(BF)
        o_ref[...] = _dg(ca, d_ref[...].astype(BF), TN)

    return pl.pallas_call(
        body, name=name, grid=(L, N // tn),
        in_specs=[pl.BlockSpec((16, D), lambda l, j: (0, 0)), pl.BlockSpec((None, 16, tn), lambda l, j: (l, 0, j))],
        out_specs=pl.BlockSpec((None, D, tn), lambda l, j: (l, 0, j)),
        out_shape=jax.ShapeDtypeStruct((L, D, N), F32),
        compiler_params=_cp("parallel", "arbitrary"),
    )(c16, dmod)


def _adamw_math(w, g, m, v):
    m = ADAM_B1 * m + (1.0 - ADAM_B1) * g
    v = ADAM_B2 * v + (1.0 - ADAM_B2) * (g * g)
    m_hat = m / (1.0 - ADAM_B1 ** ADAM_STEP)
    v_hat = v / (1.0 - ADAM_B2 ** ADAM_STEP)
    return -ADAM_LR * (m_hat / (jnp.sqrt(v_hat) + ADAM_EPS) + ADAM_WD * w), m, v


def _adamw(w, g, m, v, *, g_row0=0, name):
    R, C = w.shape
    tr = min(math.gcd(g_row0, 256) if g_row0 else 256, -(-R // 8) * 8)
    g0 = g_row0 // tr

    def body(w_ref, g_ref, m_ref, v_ref, d_ref, mo_ref, vo_ref):
        d, mn, vn = _adamw_math(w_ref[...], g_ref[...], m_ref[...], v_ref[...])
        d_ref[...] = d
        mo_ref[...] = mn
        vo_ref[...] = vn

    blk = pl.BlockSpec((tr, C), lambda i: (i, 0))
    return pl.pallas_call(
        body, name=name, grid=(pl.cdiv(R, tr),),
        in_specs=[blk, pl.BlockSpec((tr, C), lambda i: (g0 + i, 0)), blk, blk],
        out_specs=[blk, blk, blk],
        out_shape=[jax.ShapeDtypeStruct((R, C), F32)] * 3,
        compiler_params=_cp("parallel"),
    )(w, g, m, v)


def _sum_parts(parts, *, name):
    P, R, C = parts.shape

    def body(p_ref, o_ref):
        acc = p_ref[0]
        for p in range(1, P):
            acc = acc + p_ref[p]
        o_ref[...] = acc

    return pl.pallas_call(
        body, name=name, grid=(1,),
        in_specs=[pl.BlockSpec((P, R, C), lambda i: (0, 0, 0))],
        out_specs=pl.BlockSpec((R, C), lambda i: (0, 0)),
        out_shape=jax.ShapeDtypeStruct((R, C), F32),
        compiler_params=_cp("arbitrary"),
    )(parts)


def _add_halves(g4, recv, c_idx, *, name):
    _, _, Rh, C = g4.shape
    tr = _pick(Rh, 256, 8)

    def body(c_ref, a_ref, b_ref, o_ref):
        o_ref[...] = a_ref[...] + b_ref[...]

    return pl.pallas_call(
        body, name=name,
        grid_spec=pltpu.PrefetchScalarGridSpec(
            num_scalar_prefetch=1, grid=(4, pl.cdiv(Rh, tr)),
            in_specs=[pl.BlockSpec((None, None, tr, C), lambda j, r, c: (j, c[0], r, 0)),
                      pl.BlockSpec((None, tr, C), lambda j, r, c: (j, r, 0))],
            out_specs=pl.BlockSpec((None, tr, C), lambda j, r, c: (j, r, 0))),
        out_shape=jax.ShapeDtypeStruct((4, Rh, C), F32),
        compiler_params=_cp("parallel", "arbitrary"),
    )(c_idx, g4, recv)


def _add_four(own, recv, chip_idx, *, name):
    _, Rh, C = own.shape
    tr = _pick(Rh, 256, 8)

    def body(c_ref, a_ref, b_ref, o_ref):
        o_ref[...] = ((a_ref[...] + b_ref[0]) + b_ref[1]) + b_ref[2]

    return pl.pallas_call(
        body, name=name,
        grid_spec=pltpu.PrefetchScalarGridSpec(
            num_scalar_prefetch=1, grid=(pl.cdiv(Rh, tr),),
            in_specs=[pl.BlockSpec((None, tr, C), lambda r, c: (c[0], r, 0)),
                      pl.BlockSpec((3, tr, C), lambda r, c: (0, r, 0))],
            out_specs=pl.BlockSpec((tr, C), lambda r, c: (r, 0))),
        out_shape=jax.ShapeDtypeStruct((Rh, C), F32),
        compiler_params=_cp("arbitrary"),
    )(chip_idx, own, recv)


HBM = pl.BlockSpec(memory_space=pltpu.HBM)


def _mesh_pos():
    return lax.axis_index("x"), lax.axis_index("y"), lax.axis_index("c")


def _other_chips(x, y):
    return [(1 - x, y), (x, 1 - y), (1 - x, 1 - y)]


def _allgather_small(xs, *, name):
    m_per, n = xs.shape

    def body(x_ref, out_ref, send_sems, recv_sems, local_sem):
        x, y, c = _mesh_pos()
        me, sibling = (x, y, c), (x, y, 1 - c)
        chips = _other_chips(x, y)

        def rows(px, py, pc):
            return out_ref.at[pl.ds((4 * px + 2 * py + pc) * m_per, m_per), :]

        def copy(k, block, to, src=None):
            return pltpu.make_async_remote_copy(
                src_ref=rows(*block) if src is None else src, dst_ref=rows(*block),
                send_sem=send_sems.at[k], recv_sem=recv_sems.at[k], device_id=to, device_id_type=MESH)

        mine = pltpu.make_async_copy(x_ref, rows(*me), local_sem)
        mine.start()
        first = [copy(0, me, sibling, src=x_ref)]
        first += [copy(1 + j, me, (*chip, c), src=x_ref) for j, chip in enumerate(chips)]
        for cp in first:
            cp.start()
        passed = [copy(4 + j, (*chip, c), sibling) for j, chip in enumerate(chips)]
        for j, chip in enumerate(chips):
            copy(1 + j, (*chip, c), me).wait_recv()
            passed[j].start()
        copy(0, sibling, me).wait_recv()
        for j, chip in enumerate(chips):
            copy(4 + j, (*chip, 1 - c), me).wait_recv()
        for cp in first + passed:
            cp.wait_send()
        mine.wait()

    return pl.pallas_call(
        body, name=name,
        out_shape=jax.ShapeDtypeStruct((N_DEV * m_per, n), xs.dtype),
        in_specs=[pl.BlockSpec(memory_space=pltpu.VMEM)],
        out_specs=pl.BlockSpec(memory_space=pltpu.VMEM),
        scratch_shapes=[pltpu.SemaphoreType.DMA((7,)), pltpu.SemaphoreType.DMA((7,)), pltpu.SemaphoreType.DMA],
    )(xs)


def _allgather_chip_slabs(slab, *, name):
    R, C = slab.shape
    Rh = R // 2

    def body(s_ref, out_ref, send_sems, recv_sems, local_sem):
        x, y, c = _mesh_pos()
        sibling = (x, y, 1 - c)
        chips = _other_chips(x, y)

        def half(px, py, pc):
            return out_ref.at[2 * px + py, pl.ds(pc * Rh, Rh), :]

        def copy(k, block, to, src=None):
            return pltpu.make_async_remote_copy(
                src_ref=half(*block) if src is None else src, dst_ref=half(*block),
                send_sem=send_sems.at[k], recv_sem=recv_sems.at[k], device_id=to, device_id_type=MESH)

        mine = pltpu.make_async_copy(s_ref, out_ref.at[2 * x + y], local_sem)
        mine.start()
        first = [copy(j, (x, y, c), (*chip, c), src=s_ref.at[pl.ds(c * Rh, Rh), :]) for j, chip in enumerate(chips)]
        for cp in first:
            cp.start()
        passed = [copy(3 + j, (*chip, c), sibling) for j, chip in enumerate(chips)]
        for j, chip in enumerate(chips):
            copy(j, (*chip, c), (x, y, c)).wait_recv()
            passed[j].start()
        for j, chip in enumerate(chips):
            copy(3 + j, (*chip, 1 - c), (x, y, c)).wait_recv()
        for cp in first + passed:
            cp.wait_send()
        mine.wait()

    return pl.pallas_call(
        body, name=name,
        out_shape=jax.ShapeDtypeStruct((N_CHIPS, R, C), slab.dtype),
        in_specs=[HBM], out_specs=HBM,
        scratch_shapes=[pltpu.SemaphoreType.DMA((6,)), pltpu.SemaphoreType.DMA((6,)), pltpu.SemaphoreType.DMA],
    )(slab)


def _swap_halves(g4, *, name):
    _, _, Rh, C = g4.shape

    def body(g_ref, out_ref, send_sems, recv_sems):
        x, y, c = _mesh_pos()
        copies = [pltpu.make_async_remote_copy(
            src_ref=g_ref.at[j, 1 - c], dst_ref=out_ref.at[j], send_sem=send_sems.at[j], recv_sem=recv_sems.at[j],
            device_id=(x, y, 1 - c), device_id_type=MESH) for j in range(N_CHIPS)]
        for cp in copies:
            cp.start()
        for cp in copies:
            cp.wait()

    return pl.pallas_call(
        body, name=name,
        out_shape=jax.ShapeDtypeStruct((N_CHIPS, Rh, C), g4.dtype),
        in_specs=[HBM], out_specs=HBM,
        scratch_shapes=[pltpu.SemaphoreType.DMA((N_CHIPS,)), pltpu.SemaphoreType.DMA((N_CHIPS,))],
    )(g4)


def _scatter_partials(part, *, name):
    _, Rh, C = part.shape

    def body(p_ref, out_ref, send_sems, recv_sems):
        x, y, c = _mesh_pos()
        copies = [pltpu.make_async_remote_copy(
            src_ref=p_ref.at[2 * px + py], dst_ref=out_ref.at[j], send_sem=send_sems.at[j], recv_sem=recv_sems.at[j],
            device_id=(px, py, c), device_id_type=MESH) for j, (px, py) in enumerate(_other_chips(x, y))]
        for cp in copies:
            cp.start()
        for cp in copies:
            cp.wait()

    return pl.pallas_call(
        body, name=name,
        out_shape=jax.ShapeDtypeStruct((3, Rh, C), part.dtype),
        in_specs=[HBM], out_specs=HBM,
        scratch_shapes=[pltpu.SemaphoreType.DMA((3,)), pltpu.SemaphoreType.DMA((3,))],
    )(part)


def _join_halves(mine, *, name):
    Rh, C = mine.shape

    def body(m_ref, out_ref, send_sem, recv_sem, local_sem):
        x, y, c = _mesh_pos()
        keep = pltpu.make_async_copy(m_ref, out_ref.at[c], local_sem)
        keep.start()
        cp = pltpu.make_async_remote_copy(
            src_ref=m_ref, dst_ref=out_ref.at[c], send_sem=send_sem, recv_sem=recv_sem,
            device_id=(x, y, 1 - c), device_id_type=MESH)
        cp.start()
        cp.wait()
        keep.wait()

    return pl.pallas_call(
        body, name=name,
        out_shape=jax.ShapeDtypeStruct((2, Rh, C), mine.dtype),
        in_specs=[HBM], out_specs=HBM,
        scratch_shapes=[pltpu.SemaphoreType.DMA, pltpu.SemaphoreType.DMA, pltpu.SemaphoreType.DMA],
    )(mine)


def _pad_rows(a, mult):
    pad = (-a.shape[0]) % mult
    return a if pad == 0 else jnp.pad(a, ((0, pad),) + ((0, 0),) * (a.ndim - 1))


def _local_step(x, target, mod, wts, small):
    S, D = x.shape
    HP = D // LANES
    row = lambda v: v.reshape(1, -1)
    msplit = [[row(mod[i, k * D:(k + 1) * D]) for k in range(6)] for i in range(2)]
    gw, gs = {}, {}
    dmod = [[None] * 6 for _ in range(2)]

    sh1, sc1, g1, sh2, sc2, g2 = msplit[0]
    n1w0, n2w0 = row(small["norm1_w"][0]), row(small["norm2_w"][0])
    proj0, h1_0 = _ln_matmul(x, n1w0, sc1, sh1, wts["hg_w_in"], relu2=False, name="hg_in_proj")
    gn = small["hg_gn_w"].reshape(1, LANES)
    ypre0, o0, states = _hg_fwd(proj0, small["hg_lb"], gn, name="hg_fwd")
    x1, ymix0 = _matmul_resid(ypre0, wts["hg_w_out"], x, g1, name="hg_out_proj")
    a0, u0, h2_0 = _ln_matmul(x1, n2w0, sc2, sh2, wts["mlp_w1_0"], relu2=True, name="mlp0_up")
    x2, ymlp0 = _matmul_resid(u0, wts["mlp_w2_0"], x1, g2, name="mlp0_down")

    sh1b, sc1b, g1b, sh2b, sc2b, g2b = msplit[1]
    n1w1, n2w1 = row(small["norm1_w"][1]), row(small["norm2_w"][1])
    proj1, h1_1 = _ln_matmul(x2, n1w1, sc1b, sh1b, wts["fox_w_in"], relu2=False, name="fox_in_proj")
    nheads = 2 * HP
    bf_pad = jnp.pad(small["fox_b_f"].reshape(1, nheads), ((0, 0), (0, LANES - nheads)))
    qw2 = jnp.tile(small["fox_qn_w"].reshape(1, FOX_DH), (1, 2))
    kw2 = jnp.tile(small["fox_kn_w"].reshape(1, FOX_DH), (1, 2))
    fcum = _fox_cumsum(proj1, bf_pad, name="fox_cumsum")
    qa, ka, va = _fox_prep(proj1, fcum, qw2, kw2, name="fox_prep")
    ypre1, o1, q2 = _fox_fwd(qa, ka, va, proj1, name="fox_fwd")
    x3, ymix1 = _matmul_resid(ypre1, wts["fox_w_out"], x2, g1b, name="fox_out_proj")
    a1, u1, h2_1 = _ln_matmul(x3, n2w1, sc2b, sh2b, wts["mlp_w1_1"], relu2=True, name="mlp1_up")
    x4, ymlp1 = _matmul_resid(u1, wts["mlp_w2_1"], x3, g2b, name="mlp1_down")

    loss, dx4, dfw = _loss_kernel(x4, row(small["final_w"]), target, name="loss")
    gs["final_w"] = dfw.reshape(-1)

    def mlp_bwd(i, dx_out, x_in, h2, a, u, ymlp, n2w, sc2_, g2_):
        dz, dm, dg2 = _gate_matmul_nt(dx_out, g2_, ymlp, wts[f"mlp_w2_{i}"], a, name=f"mlp{i}_down_bwd")
        gw[f"mlp_w2_{i}"] = _matmul_tn(u, dm[None], name=f"mlp{i}_dw2")
        gw[f"mlp_w1_{i}"] = _matmul_tn(h2, dz[None], name=f"mlp{i}_dw1")
        dx_in, dsc, dsh, dnw = _matmul_nt_lnbwd(dz[None], wts[f"mlp_w1_{i}"], x_in, n2w, sc2_, dx_out,
                                                name=f"mlp{i}_up_bwd")
        dmod[i][3], dmod[i][4], dmod[i][5] = dsh, dsc, dg2
        return dx_in, dnw

    dx3, dn2w1 = mlp_bwd(1, dx4, x3, h2_1, a1, u1, ymlp1, n2w1, sc2b, g2b)
    dyp1, dm1, dg1b = _gate_matmul_nt(dx3, g1b, ymix1, wts["fox_w_out"], None, name="fox_out_bwd")
    gw["fox_w_out"] = _matmul_tn(ypre1, dm1[None], name="fox_dw_out")
    doa = _fox_bwd_prep(dyp1, o1, proj1, name="fox_bwd_prep")
    dqa, dka, dva, colsum = _fox_bwd(q2, ka, va, doa, name="fox_bwd")
    colsum = jnp.pad(colsum[:, 0, :].T, ((0, 0), (0, LANES - nheads)))
    dproj1, dqw, dkw = _fox_bwd_post(dqa, dka, dva, proj1, dyp1, o1, qw2, kw2, name="fox_bwd_post")
    dproj1, dbf = _fox_dfz(colsum, nheads, proj1, bf_pad, dproj1, name="fox_dfz")
    gw["fox_w_in"] = _matmul_tn(h1_1, dproj1, name="fox_dw_in")
    dx2, dsc, dsh, dn1w1 = _matmul_nt_lnbwd(dproj1, wts["fox_w_in"], x2, n1w1, sc1b, dx3, name="fox_in_bwd")
    dmod[1][0], dmod[1][1], dmod[1][2] = dsh, dsc, dg1b
    gs["fox_qn_w"] = dqw[0, :FOX_DH] + dqw[0, FOX_DH:]
    gs["fox_kn_w"] = dkw[0, :FOX_DH] + dkw[0, FOX_DH:]
    gs["fox_b_f"] = dbf[0, :nheads]

    dx1, dn2w0 = mlp_bwd(0, dx2, x1, h2_0, a0, u0, ymlp0, n2w0, sc2, g2)
    dyp0, dm0, dg1 = _gate_matmul_nt(dx1, g1, ymix0, wts["hg_w_out"], None, name="hg_out_bwd")
    gw["hg_w_out"] = _matmul_tn(ypre0, dm0[None], name="hg_dw_out")
    dproj0, dlb, dgn = _hg_bwd(proj0, small["hg_lb"], gn, o0, states, dyp0, name="hg_bwd")
    gw["hg_w_in"] = _matmul_tn(h1_0, dproj0, name="hg_dw_in")
    dx0, dsc, dsh, dn1w0 = _matmul_nt_lnbwd(dproj0, wts["hg_w_in"], x, n1w0, sc1, dx1, name="hg_in_bwd")
    dmod[0][0], dmod[0][1], dmod[0][2] = dsh, dsc, dg1
    gs["hg_lb"] = dlb
    gs["hg_gn_w"] = jnp.sum(dgn, axis=0)

    gs["norm1_w"] = jnp.concatenate([dn1w0, dn1w1], axis=0)
    gs["norm2_w"] = jnp.concatenate([dn2w0, dn2w1], axis=0)
    gs["dmod"] = jnp.stack([jnp.concatenate(dmod[i], axis=1)[0] for i in range(2)])
    return loss, dx0, gw, gs


SMALL_NAMES = ["norm1_w", "norm2_w", "hg_lb", "hg_gn_w", "fox_b_f", "fox_qn_w", "fox_kn_w", "final_w"]


def _pack_small(d, names):
    rows, offs, r0 = [], {}, 0
    for n in names:
        flat = d[n].reshape(-1)
        nr = -(-flat.shape[0] // LANES)
        rows.append(jnp.pad(flat, (0, nr * LANES - flat.shape[0])).reshape(nr, LANES))
        offs[n] = (r0, nr)
        r0 += nr
    return jnp.concatenate(rows, axis=0), offs


def _unpack_small(packed, offs, name, like):
    r0, nr = offs[name]
    return packed[r0:r0 + nr].reshape(-1)[:like.size].reshape(like.shape)


def kernel(x, c, w_mod, b_mod, norm1_w, norm2_w, hg_w_in, hg_w_out, hg_lb, hg_gn_w, fox_w_in, fox_b_f, fox_qn_w, fox_kn_w, fox_w_out, mlp_w1, mlp_w2, final_w, loss_target, m_w_mod, m_b_mod, m_norm1_w, m_norm2_w, m_hg_w_in, m_hg_w_out, m_hg_lb, m_hg_gn_w, m_fox_w_in, m_fox_b_f, m_fox_qn_w, m_fox_kn_w, m_fox_w_out, m_mlp_w1, m_mlp_w2, m_final_w, v_w_mod, v_b_mod, v_norm1_w, v_norm2_w, v_hg_w_in, v_hg_w_out, v_hg_lb, v_hg_gn_w, v_fox_w_in, v_fox_b_f, v_fox_qn_w, v_fox_kn_w, v_fox_w_out, v_mlp_w1, v_mlp_w2, v_final_w):
    S, D = x.shape[1], x.shape[2]
    nheads = D // FOX_DH
    ax, ay, ac = _mesh_pos()
    chip = 2 * ax + ay
    dev = 2 * chip + ac
    xs, tgt = x.reshape(S, D), loss_target.reshape(S, D)

    c_all = _allgather_small(_pad_rows(c.reshape(-1, LANES), 8), name="gather_c")
    c_all = c_all.reshape(N_DEV, -1)[:, :D]
    c16 = _pad_rows(c_all, 16)
    nmod = w_mod.shape[2]
    b_shard = lax.dynamic_slice_in_dim(b_mod, chip * nmod, nmod, axis=1)
    mod_shard = _mod_fwd(c16, w_mod, b_shard[:, None, :], name="mod_fwd")[:, :N_DEV]
    mod_all = _allgather_small(mod_shard.reshape(-1, LANES), name="gather_mod")
    mod_all = mod_all.reshape(N_CHIPS, 2, 2, N_DEV, nmod)[:, 0]
    mod = lax.dynamic_index_in_dim(mod_all, dev, axis=2, keepdims=False)
    mod = mod.transpose(1, 0, 2).reshape(2, N_CHIPS * nmod)

    fox_rows = fox_w_in.shape[2]
    segs = [hg_w_in[0], hg_w_out[0], fox_w_out[0], mlp_w1.reshape(2 * D, D), mlp_w2.reshape(2 * D, D),
            fox_w_in[0].reshape(fox_rows, D)]
    seg_rows = [s.shape[0] for s in segs]
    seg_off = [sum(seg_rows[:i]) for i in range(len(segs))]
    slab = _pad_rows(jnp.concatenate([s.astype(BF) for s in segs], axis=0), 32)
    R = slab.shape[0]
    gathered = _allgather_chip_slabs(slab, name="gather_weights")

    def seg(i):
        return gathered[:, seg_off[i]:seg_off[i] + seg_rows[i], :]

    col = lambda g: g.transpose(1, 0, 2).reshape(g.shape[1], -1)
    rowsh = lambda g: g.reshape(-1, g.shape[2])
    w1 = seg(3).reshape(N_CHIPS, 2, D, D)
    w2 = seg(4).reshape(N_CHIPS, 2, D, D)
    fox_in = col(seg(5).reshape(N_CHIPS, D, fox_rows))
    wts = {
        "hg_w_in": col(seg(0)), "hg_w_out": rowsh(seg(1)), "fox_w_out": rowsh(seg(2)),
        "mlp_w1_0": col(w1[:, 0]), "mlp_w1_1": col(w1[:, 1]), "mlp_w2_0": rowsh(w2[:, 0]), "mlp_w2_1": rowsh(w2[:, 1]),
        "fox_w_in": jnp.pad(fox_in, ((0, 0), (0, 5 * D - fox_in.shape[1]))),
    }
    small = {"norm1_w": norm1_w, "norm2_w": norm2_w, "hg_lb": hg_lb, "hg_gn_w": hg_gn_w, "fox_b_f": fox_b_f,
             "fox_qn_w": fox_qn_w, "fox_kn_w": fox_kn_w, "final_w": final_w}

    loss_part, grad_x, gw, gs = _local_step(xs, tgt, mod, wts, small)
    loss = lax.psum(loss_part[0, 0], ("x", "y", "c"))

    def uncol(g, n):
        return g.reshape(g.shape[0], N_CHIPS, n).transpose(1, 0, 2)

    gfox = uncol(gw["fox_w_in"][:, :4 * fox_rows], fox_rows).reshape(N_CHIPS, fox_rows, D)
    gsegs = [uncol(gw["hg_w_in"], D), gw["hg_w_out"].reshape(N_CHIPS, D // 4, D), gw["fox_w_out"].reshape(N_CHIPS, D // 4, D),
             jnp.concatenate([uncol(gw["mlp_w1_0"], D), uncol(gw["mlp_w1_1"], D)], axis=1),
             jnp.concatenate([gw["mlp_w2_0"].reshape(N_CHIPS, D, D), gw["mlp_w2_1"].reshape(N_CHIPS, D, D)], axis=1),
             gfox]
    gfull = jnp.concatenate(gsegs, axis=1)
    gfull = jnp.pad(gfull, ((0, 0), (0, R - gfull.shape[1]), (0, 0)))
    g4 = gfull.reshape(N_CHIPS, 2, R // 2, D)
    from_sibling = _swap_halves(g4, name="rs_swap_halves")
    chip_part = _add_halves(g4, from_sibling, ac.reshape(1), name="rs_add_halves")
    from_chips = _scatter_partials(chip_part, name="rs_scatter")
    my_half = _add_four(chip_part, from_chips, chip.reshape(1), name="rs_add_chips")
    gshard = _join_halves(my_half, name="rs_join").reshape(R, D)

    names = ["dmod"] + SMALL_NAMES
    packed, offs = _pack_small(gs, names)
    packed = _pad_rows(packed, 8)
    rp = packed.shape[0]
    parts = _allgather_small(packed, name="gather_small").reshape(N_DEV, rp, LANES)
    total = _sum_parts(parts, name="sum_small")
    r0, nr = offs["dmod"]
    dmod_all = parts[:, r0:r0 + nr].reshape(N_DEV, 2, N_CHIPS * nmod)
    dmod_shard = lax.dynamic_slice_in_dim(dmod_all, chip * nmod, nmod, axis=2).transpose(1, 0, 2)
    g_w_mod = _mod_bwd(c16, jnp.pad(dmod_shard, ((0, 0), (0, 16 - N_DEV), (0, 0))), name="mod_bwd")

    grads = {"w_mod": g_w_mod, "b_mod": _unpack_small(total, offs, "dmod", b_mod)}
    for n in SMALL_NAMES:
        grads[n] = _unpack_small(total, offs, n, small[n])

    given = dict(w_mod=(w_mod, m_w_mod, v_w_mod), b_mod=(b_mod, m_b_mod, v_b_mod), norm1_w=(norm1_w, m_norm1_w, v_norm1_w),
                 norm2_w=(norm2_w, m_norm2_w, v_norm2_w), hg_w_in=(hg_w_in, m_hg_w_in, v_hg_w_in),
                 hg_w_out=(hg_w_out, m_hg_w_out, v_hg_w_out), hg_lb=(hg_lb, m_hg_lb, v_hg_lb),
                 hg_gn_w=(hg_gn_w, m_hg_gn_w, v_hg_gn_w), fox_w_in=(fox_w_in, m_fox_w_in, v_fox_w_in),
                 fox_b_f=(fox_b_f, m_fox_b_f, v_fox_b_f), fox_qn_w=(fox_qn_w, m_fox_qn_w, v_fox_qn_w),
                 fox_kn_w=(fox_kn_w, m_fox_kn_w, v_fox_kn_w), fox_w_out=(fox_w_out, m_fox_w_out, v_fox_w_out),
                 mlp_w1=(mlp_w1, m_mlp_w1, v_mlp_w1), mlp_w2=(mlp_w2, m_mlp_w2, v_mlp_w2), final_w=(final_w, m_final_w, v_final_w))
    upd = {}

    big = [("hg_w_in", 0), ("hg_w_out", 1), ("fox_w_out", 2), ("mlp_w1", 3), ("mlp_w2", 4), ("fox_w_in", 5)]
    for n, i in big:
        w, m, v = given[n]
        flat = lambda a: a.reshape(seg_rows[i], D)
        d, mn, vn = _adamw(flat(w), gshard, flat(m), flat(v), g_row0=seg_off[i], name=f"adamw_{n}")
        grads[n] = gshard[seg_off[i]:seg_off[i] + seg_rows[i]].reshape(w.shape)
        upd[n] = tuple(a.reshape(w.shape) for a in (d, mn, vn))

    w, m, v = given["w_mod"]
    flat = lambda a: a.reshape(-1, nmod)
    upd["w_mod"] = tuple(a.reshape(w.shape) for a in _adamw(flat(w), flat(g_w_mod), flat(m), flat(v), name="adamw_w_mod"))

    snames = ["b_mod"] + SMALL_NAMES
    pw, soffs = _pack_small({n: given[n][0] for n in snames}, snames)
    pm, _ = _pack_small({n: given[n][1] for n in snames}, snames)
    pv, _ = _pack_small({n: given[n][2] for n in snames}, snames)
    pg, _ = _pack_small({n: grads[n] for n in snames}, snames)
    pw, pm, pv, pg = (_pad_rows(a, 8) for a in (pw, pm, pv, pg))
    sd, smn, svn = _adamw(pw, pg, pm, pv, name="adamw_small")
    for n in snames:
        like = given[n][0]
        upd[n] = tuple(_unpack_small(a, soffs, n, like) for a in (sd, smn, svn))

    order = ["w_mod", "b_mod", "norm1_w", "norm2_w", "hg_w_in", "hg_w_out", "hg_lb", "hg_gn_w", "fox_w_in", "fox_b_f",
             "fox_qn_w", "fox_kn_w", "fox_w_out", "mlp_w1", "mlp_w2", "final_w"]
    return (loss, grad_x.reshape(x.shape), *[grads[n] for n in order], *[upd[n][0] for n in order],
            *[upd[n][1] for n in order], *[upd[n][2] for n in order])
```

```python
import math

import jax
import jax.numpy as jnp
from jax import lax
from jax.experimental import pallas as pl
from jax.experimental.pallas import tpu as pltpu

EPS = 1e-6
ADAM_LR, ADAM_B1, ADAM_B2, ADAM_EPS, ADAM_WD, ADAM_STEP = 0.001, 0.9, 0.999, 1e-08, 0.01, 10

F32 = jnp.float32
BF = jnp.bfloat16
LANES = 128
HG_CHUNK = 64
HG_HEADS_PER_STEP = 4
LOG2E = 1.4426950408889634
FOX_DH = 64
N_CHIPS = 4
N_DEV = 8
VMEM_LIMIT = 48 * 1024 * 1024
MESH = pl.DeviceIdType.MESH

NT = (((1,), (1,)), ((), ()))
TN = (((0,), (0,)), ((), ()))


def _pick(n, pref, mult=LANES):
    if n <= pref:
        return n
    t = (pref // mult) * mult
    while t >= mult:
        if n % t == 0:
            return t
        t -= mult
    raise ValueError((n, pref, mult))


def _cp(*sem):
    return pltpu.CompilerParams(dimension_semantics=sem, vmem_limit_bytes=VMEM_LIMIT)


def _dot(a, b):
    return jnp.dot(a, b, preferred_element_type=F32)


def _dg(a, b, dims):
    return lax.dot_general(a, b, dims, preferred_element_type=F32)


def _split3(x):
    hi = x.astype(BF)
    r1 = x - hi.astype(F32)
    mid = r1.astype(BF)
    lo = (r1 - mid.astype(F32)).astype(BF)
    return hi, mid, lo


def _tri_dot(tri, x):
    hi, mid, lo = _split3(x)
    return _dot(tri, hi) + _dot(tri, mid) + _dot(tri, lo)


def _dg3(a, b, dims):
    ah, bh = a.astype(BF), b.astype(BF)
    al, bl = (a - ah.astype(F32)).astype(BF), (b - bh.astype(F32)).astype(BF)
    return _dg(ah, bh, dims) + _dg(ah, bl, dims) + _dg(al, bh, dims)


NN = (((1,), (0,)), ((), ()))


def _sigmoid(x):
    return jax.nn.sigmoid(x)


def _ln_matmul(x, nw, sc, sh, w, *, relu2, name):
    S, D = x.shape
    N = w.shape[1]
    tm, tn = _pick(S, 1024, 16), _pick(N, 1024)

    def body(x_ref, nw_ref, sc_ref, sh_ref, w_ref, *rest):
        outs, hs = rest[:-1], rest[-1]
        h_ref = outs[-1]

        @pl.when(pl.program_id(1) == 0)
        def _():
            xv = x_ref[...]
            r = lax.rsqrt(jnp.mean(xv * xv, axis=-1, keepdims=True) + EPS)
            hb = ((xv * r * nw_ref[...]) * (1.0 + sc_ref[...]) + sh_ref[...]).astype(BF)
            hs[...] = hb
            h_ref[...] = hb

        z = _dot(hs[...], w_ref[...])
        if relu2:
            a = jnp.maximum(z, 0.0)
            outs[0][...] = a.astype(BF)
            outs[1][...] = (a * a).astype(BF)
        else:
            outs[0][...] = z

    vec = pl.BlockSpec((1, D), lambda i, j: (0, 0))
    tile = pl.BlockSpec((tm, tn), lambda i, j: (i, j))
    if relu2:
        out_shape = [jax.ShapeDtypeStruct((S, N), BF), jax.ShapeDtypeStruct((S, N), BF)]
        out_specs = [tile, tile]
    else:
        out_shape = [jax.ShapeDtypeStruct((S, N), F32)]
        out_specs = [tile]
    out_shape.append(jax.ShapeDtypeStruct((S, D), BF))
    out_specs.append(pl.BlockSpec((tm, D), lambda i, j: (i, 0)))
    return pl.pallas_call(
        body, name=name, grid=(S // tm, N // tn),
        in_specs=[pl.BlockSpec((tm, D), lambda i, j: (i, 0)), vec, vec, vec,
                  pl.BlockSpec((D, tn), lambda i, j: (0, j))],
        out_specs=out_specs, out_shape=out_shape,
        scratch_shapes=[pltpu.VMEM((tm, D), BF)],
        compiler_params=_cp("parallel", "arbitrary"),
    )(x, nw, sc, sh, w)


def _matmul_resid(a, w, x, gate, *, name):
    S, K = a.shape
    D = w.shape[1]
    big = 1024 if K <= 1024 else 512
    tm, tn = _pick(S, big, 16), _pick(D, big)

    def body(a_ref, w_ref, x_ref, g_ref, o_ref, y_ref):
        y = _dot(a_ref[...], w_ref[...])
        y_ref[...] = y.astype(BF)
        o_ref[...] = x_ref[...] + g_ref[...] * y

    tile = pl.BlockSpec((tm, tn), lambda i, j: (i, j))
    return pl.pallas_call(
        body, name=name, grid=(S // tm, D // tn),
        in_specs=[pl.BlockSpec((tm, K), lambda i, j: (i, 0)), pl.BlockSpec((K, tn), lambda i, j: (0, j)),
                  tile, pl.BlockSpec((1, tn), lambda i, j: (0, j))],
        out_specs=[tile, tile],
        out_shape=[jax.ShapeDtypeStruct((S, D), F32), jax.ShapeDtypeStruct((S, D), BF)],
        compiler_params=_cp("parallel", "arbitrary"),
    )(a, w, x, gate)


def _gate_matmul_nt(dx, gate, y, w, act, *, name):
    S, D = dx.shape
    K = w.shape[0]
    tm, tn = _pick(S, 1024, 16), _pick(K, 1024)
    fused = act is not None

    def body(dx_ref, g_ref, y_ref, w_ref, *rest):
        if fused:
            act_ref, da_ref, dm_ref, dg_ref, ms = rest
        else:
            da_ref, dm_ref, dg_ref, ms = rest
        i, j = pl.program_id(0), pl.program_id(1)

        @pl.when((i == 0) & (j == 0))
        def _():
            dg_ref[...] = jnp.zeros_like(dg_ref)

        @pl.when(j == 0)
        def _():
            dxv = dx_ref[...]
            dmb = (dxv * g_ref[...]).astype(BF)
            ms[...] = dmb
            dm_ref[...] = dmb
            dg_ref[...] += jnp.sum(dxv * y_ref[...].astype(F32), axis=0, keepdims=True)

        da = _dg(ms[...], w_ref[...], NT)
        if fused:
            da_ref[...] = (da * (2.0 * act_ref[...].astype(F32))).astype(BF)
        else:
            da_ref[...] = da

    row = pl.BlockSpec((tm, D), lambda i, j: (i, 0))
    vec = pl.BlockSpec((1, D), lambda i, j: (0, 0))
    tile = pl.BlockSpec((tm, tn), lambda i, j: (i, j))
    in_specs = [row, vec, row, pl.BlockSpec((tn, D), lambda i, j: (j, 0))]
    args = [dx, gate, y, w]
    if fused:
        in_specs.append(tile)
        args.append(act)
    return pl.pallas_call(
        body, name=name, grid=(S // tm, K // tn),
        in_specs=in_specs, out_specs=[tile, row, vec],
        out_shape=[jax.ShapeDtypeStruct((S, K), BF if fused else F32), jax.ShapeDtypeStruct((S, D), BF),
                   jax.ShapeDtypeStruct((1, D), F32)],
        scratch_shapes=[pltpu.VMEM((tm, D), BF)],
        compiler_params=_cp("arbitrary", "arbitrary"),
    )(*args)


def _matmul_tn(a, b, *, name):
    S, Ka = a.shape
    P, _, Db = b.shape
    tk, tn, ts = _pick(Ka, 1024), _pick(Db, 1024), _pick(S, 1024, 16)
    npb = Db // tn

    def body(a_ref, b_ref, o_ref, acc):
        s = pl.program_id(2)

        @pl.when(s == 0)
        def _():
            acc[...] = jnp.zeros_like(acc)

        acc[...] += _dg(a_ref[...], b_ref[...], TN)

        @pl.when(s == pl.num_programs(2) - 1)
        def _():
            o_ref[...] = acc[...]

    return pl.pallas_call(
        body, name=name, grid=(Ka // tk, P * npb, S // ts),
        in_specs=[pl.BlockSpec((ts, tk), lambda i, j, s: (s, i)),
                  pl.BlockSpec((None, ts, tn), lambda i, j, s: (j // npb, s, j % npb))],
        out_specs=pl.BlockSpec((tk, tn), lambda i, j, s: (i, j)),
        out_shape=jax.ShapeDtypeStruct((Ka, P * Db), F32),
        scratch_shapes=[pltpu.VMEM((tk, tn), F32)],
        compiler_params=_cp("parallel", "parallel", "arbitrary"),
    )(a, b)


def _matmul_nt_lnbwd(g, w, x, nw, sc, dx_out, *, name):
    P, S, Dg = g.shape
    D = x.shape[1]
    tm, tk = _pick(S, 1024, 16), _pick(Dg, 1024)
    npb = Dg // tk
    nk = P * npb

    def body(g_ref, w_ref, x_ref, nw_ref, sc_ref, dxo_ref, dx_ref, dsc_ref, dsh_ref, dnw_ref, acc):
        i, k = pl.program_id(0), pl.program_id(1)

        @pl.when((i == 0) & (k == 0))
        def _():
            dsc_ref[...] = jnp.zeros_like(dsc_ref)
            dsh_ref[...] = jnp.zeros_like(dsh_ref)
            dnw_ref[...] = jnp.zeros_like(dnw_ref)

        @pl.when(k == 0)
        def _():
            acc[...] = jnp.zeros_like(acc)

        acc[...] += _dg(g_ref[...], w_ref[...], NT)

        @pl.when(k == nk - 1)
        def _():
            dh = acc[...]
            xv = x_ref[...]
            nwv = nw_ref[...]
            r = lax.rsqrt(jnp.mean(xv * xv, axis=-1, keepdims=True) + EPS)
            xr = xv * r
            dn = dh * (1.0 + sc_ref[...])
            dsc_ref[...] += jnp.sum(dh * (xr * nwv), axis=0, keepdims=True)
            dsh_ref[...] += jnp.sum(dh, axis=0, keepdims=True)
            dnw_ref[...] += jnp.sum(dn * xr, axis=0, keepdims=True)
            u = dn * nwv
            dx_ref[...] = dxo_ref[...] + r * (u - xr * jnp.mean(u * xr, axis=-1, keepdims=True))

    row = pl.BlockSpec((tm, D), lambda i, k: (i, 0))
    vec = pl.BlockSpec((1, D), lambda i, k: (0, 0))
    return pl.pallas_call(
        body, name=name, grid=(S // tm, nk),
        in_specs=[pl.BlockSpec((None, tm, tk), lambda i, k: (k // npb, i, k % npb)),
                  pl.BlockSpec((D, tk), lambda i, k: (0, k)), row, vec, vec, row],
        out_specs=[row, vec, vec, vec],
        out_shape=[jax.ShapeDtypeStruct((S, D), F32)] + [jax.ShapeDtypeStruct((1, D), F32)] * 3,
        scratch_shapes=[pltpu.VMEM((tm, D), F32)],
        compiler_params=_cp("arbitrary", "arbitrary"),
    )(g, w, x, nw, sc, dx_out)


def _loss_kernel(x, fw, tgt, *, name):
    S, D = x.shape
    tm = _pick(S, 512, 8)

    def body(x_ref, fw_ref, t_ref, l_ref, dx_ref, dfw_ref):
        @pl.when(pl.program_id(0) == 0)
        def _():
            l_ref[...] = jnp.zeros_like(l_ref)
            dfw_ref[...] = jnp.zeros_like(dfw_ref)

        xv = x_ref[...]
        fwv = fw_ref[...]
        r = lax.rsqrt(jnp.mean(xv * xv, axis=-1, keepdims=True) + EPS)
        xr = xv * r
        err = xr * fwv - t_ref[...]
        per_tok = jnp.mean(err * err, axis=-1, keepdims=True)
        l_ref[...] += 0.5 * jnp.sum(per_tok, axis=0, keepdims=True)
        dy = err * (1.0 / D)
        dfw_ref[...] += jnp.sum(dy * xr, axis=0, keepdims=True)
        u = dy * fwv
        dx_ref[...] = r * (u - xr * jnp.mean(u * xr, axis=-1, keepdims=True))

    row = pl.BlockSpec((tm, D), lambda i: (i, 0))
    vec = pl.BlockSpec((1, D), lambda i: (0, 0))
    return pl.pallas_call(
        body, name=name, grid=(S // tm,),
        in_specs=[row, vec, row],
        out_specs=[pl.BlockSpec((1, LANES), lambda i: (0, 0)), row, vec],
        out_shape=[jax.ShapeDtypeStruct((1, LANES), F32), jax.ShapeDtypeStruct((S, D), F32),
                   jax.ShapeDtypeStruct((1, D), F32)],
        compiler_params=_cp("arbitrary"),
    )(x, fw, tgt)


def _hg_lower_bound(lb3):
    mx = jnp.max(lb3, axis=0, keepdims=True)
    e = jnp.exp(lb3 - mx)
    p = e / jnp.sum(e, axis=0, keepdims=True)
    return p[0:1, :], p


def _hg_chunk_common(qr, fz, lbv):
    sq = _sigmoid(qr)
    q = qr * sq
    sig = _sigmoid(fz)
    f = lbv + (1.0 - lbv) * sig
    k = (1.0 - lbv) * (1.0 - sig)
    return q, sq, sig, f, k, jnp.log(f)


def _row_of(x, rows, r):
    return jnp.sum(jnp.where(rows == r, x, 0.0), axis=0, keepdims=True)


def _hg_fwd(proj, hg_lb, gn, *, name):
    S = proj.shape[0]
    D = proj.shape[1] // 4
    H = D // LANES
    HB = min(HG_HEADS_PER_STEP, H)
    W = HB * LANES
    C = HG_CHUNK
    T = _pick(S, 512, C)
    nch, nb = T // C, S // T

    def body(q_ref, fz_ref, v_ref, g_ref, lb_ref, gn_ref, y_ref, o_ref, sts_ref, st):
        @pl.when(pl.program_id(1) == 0)
        def _():
            st[...] = jnp.zeros_like(st)

        lb_all, _ = _hg_lower_bound(lb_ref[...])
        gnv = gn_ref[...]
        ri = lax.broadcasted_iota(jnp.int32, (C, C), 0)
        ci_ = lax.broadcasted_iota(jnp.int32, (C, C), 1)
        low = ri >= ci_
        tri = jnp.where(low, 1.0, 0.0).astype(BF)
        rows = lax.broadcasted_iota(jnp.int32, (C, LANES), 0)

        def chunk(ci, carry):
            sl = pl.ds(pl.multiple_of(ci * C, C), C)
            for hh in range(HB):
                ls = slice(hh * LANES, (hh + 1) * LANES)
                q, _, _, _, k, logf = _hg_chunk_common(q_ref[sl, ls], fz_ref[sl, ls], lb_all[:, ls])
                vv = v_ref[sl, ls]
                gg = g_ref[sl, ls]
                G = _tri_dot(tri, logf)
                Gm = _row_of(G, rows, C // 2 - 1)
                Gl = _row_of(G, rows, C - 1)
                qt = q * jnp.exp(G - Gm)
                kt = k * jnp.exp(Gm - G)
                A = jnp.where(low, _dg3(qt, kt, NT), 0.0)
                Sv = st[hh]
                sts_ref[hh, ci] = Sv
                o = _dg3(A, vv, NN) + _dg3(q * jnp.exp(G), Sv, NT)
                st[hh] = Sv * jnp.exp(Gl) + _dg3(vv, k * jnp.exp(Gl - G), TN)
                r = lax.rsqrt(jnp.mean(o * o, axis=-1, keepdims=True) + EPS)
                y_ref[sl, ls] = ((o * r * gnv) * (gg * _sigmoid(gg))).astype(BF)
                o_ref[sl, ls] = o
            return carry

        lax.fori_loop(0, nch, chunk, 0)

    ng = H // HB

    def part(p):
        return pl.BlockSpec((T, W), lambda h, n: (n, p * ng + h))

    blk = pl.BlockSpec((T, W), lambda h, n: (n, h))
    return pl.pallas_call(
        body, name=name, grid=(ng, nb),
        in_specs=[part(0), part(1), part(2), part(3),
                  pl.BlockSpec((3, W), lambda h, n: (0, h)), pl.BlockSpec((1, LANES), lambda h, n: (0, 0))],
        out_specs=[blk, blk, pl.BlockSpec((HB, nch, LANES, LANES), lambda h, n: (h, n, 0, 0))],
        out_shape=[jax.ShapeDtypeStruct((S, D), BF), jax.ShapeDtypeStruct((S, D), F32),
                   jax.ShapeDtypeStruct((H, S // C, LANES, LANES), F32)],
        scratch_shapes=[pltpu.VMEM((HB, LANES, LANES), F32)],
        compiler_params=_cp("parallel", "arbitrary"),
    )(proj, proj, proj, proj, hg_lb, gn)


def _hg_bwd(proj, hg_lb, gn, o_all, states, dy, *, name):
    S = proj.shape[0]
    D = proj.shape[1] // 4
    H = D // LANES
    HB = min(HG_HEADS_PER_STEP, H)
    W = HB * LANES
    C = HG_CHUNK
    T = _pick(S, 512, C)
    nch, nb = T // C, S // T

    def body(q_ref, fz_ref, v_ref, g_ref, lb_ref, gn_ref, o_ref, sts_ref, dy_ref,
             dp_ref, dlb_ref, dgn_ref, dst, dlb_acc):
        n = pl.program_id(1)

        @pl.when(n == 0)
        def _():
            dst[...] = jnp.zeros_like(dst)
            dlb_acc[...] = jnp.zeros_like(dlb_acc)
            dgn_ref[...] = jnp.zeros_like(dgn_ref)

        lb_all, p3 = _hg_lower_bound(lb_ref[...])
        gnv = gn_ref[...]
        ri = lax.broadcasted_iota(jnp.int32, (C, C), 0)
        ci_ = lax.broadcasted_iota(jnp.int32, (C, C), 1)
        low = ri >= ci_
        tri = jnp.where(low, 1.0, 0.0).astype(BF)
        triu = jnp.where(ri <= ci_, 1.0, 0.0).astype(BF)
        rows = lax.broadcasted_iota(jnp.int32, (C, LANES), 0)

        def chunk(cj, carry):
            ci = nch - 1 - cj
            sl = pl.ds(pl.multiple_of(ci * C, C), C)
            for hh in range(HB):
                ls = slice(hh * LANES, (hh + 1) * LANES)
                lbv = lb_all[:, ls]
                qr = q_ref[sl, ls]
                q, sq, sig, f, k, logf = _hg_chunk_common(qr, fz_ref[sl, ls], lbv)
                vv = v_ref[sl, ls]
                gg = g_ref[sl, ls]
                o = o_ref[sl, ls]
                dyv = dy_ref[sl, ls]
                G = _tri_dot(tri, logf)
                Gm = _row_of(G, rows, C // 2 - 1)
                Gl = _row_of(G, rows, C - 1)
                eG, e_qm, e_km, e_lk, eGl = jnp.exp(G), jnp.exp(G - Gm), jnp.exp(Gm - G), jnp.exp(Gl - G), jnp.exp(Gl)
                qt = q * e_qm
                kt = k * e_km
                A = jnp.where(low, _dg3(qt, kt, NT), 0.0)
                sg = _sigmoid(gg)
                r = lax.rsqrt(jnp.mean(o * o, axis=-1, keepdims=True) + EPS)
                on = o * r
                d_onw = dyv * (gg * sg)
                dgn_ref[hh] += jnp.sum(d_onw * on, axis=0, keepdims=True)
                dgg = dyv * (on * gnv) * (sg * (1.0 + gg * (1.0 - sg)))
                u = d_onw * gnv
                do = r * (u - on * jnp.mean(u * on, axis=-1, keepdims=True))
                Sv = sts_ref[hh, ci]
                dSv = dst[hh]
                dA = jnp.where(low, _dg3(do, vv, NT), 0.0)
                kdec = k * e_lk
                dv = _dg3(A, do, TN) + _dg3(kdec, dSv, NT)
                dq = _dg3(dA, kt, NN) * e_qm + eG * _dg3(do, Sv, NN)
                dk = _dg3(dA, qt, TN) * e_km + e_lk * _dg3(vv, dSv, NN)
                s_end = Sv * eGl + _dg3(vv, kdec, TN)
                dgl = jnp.sum(dSv * s_end, axis=0, keepdims=True)
                dG = q * dq - k * dk + jnp.where(rows == C - 1, dgl, 0.0)
                dlogf = _tri_dot(triu, dG) - f * dk
                dst[hh] = dSv * eGl + _dg3(do, q * eG, TN)
                dlf_f = dlogf / f
                dlb_acc[:, ls] += jnp.sum(dlf_f * (1.0 - sig), axis=0, keepdims=True)
                dp_ref[0, sl, ls] = (dq * (sq * (1.0 + qr * (1.0 - sq)))).astype(BF)
                dp_ref[1, sl, ls] = (dlf_f * (1.0 - lbv) * sig * (1.0 - sig)).astype(BF)
                dp_ref[2, sl, ls] = dv.astype(BF)
                dp_ref[3, sl, ls] = dgg.astype(BF)
            return carry

        lax.fori_loop(0, nch, chunk, 0)
        sel = jnp.where(lax.broadcasted_iota(jnp.int32, (3, W), 0) == 0, 1.0, 0.0)
        dlb_ref[...] = lb_all * (sel - p3) * dlb_acc[...]

    ng = H // HB

    def part(p):
        return pl.BlockSpec((T, W), lambda h, n: (nb - 1 - n, p * ng + h))

    blk = pl.BlockSpec((T, W), lambda h, n: (nb - 1 - n, h))
    return pl.pallas_call(
        body, name=name, grid=(ng, nb),
        in_specs=[part(0), part(1), part(2), part(3),
                  pl.BlockSpec((3, W), lambda h, n: (0, h)), pl.BlockSpec((1, LANES), lambda h, n: (0, 0)),
                  blk, pl.BlockSpec((HB, nch, LANES, LANES), lambda h, n: (h, nb - 1 - n, 0, 0)), blk],
        out_specs=[pl.BlockSpec((4, T, W), lambda h, n: (0, nb - 1 - n, h)),
                   pl.BlockSpec((3, W), lambda h, n: (0, h)),
                   pl.BlockSpec((HB, 1, LANES), lambda h, n: (h, 0, 0))],
        out_shape=[jax.ShapeDtypeStruct((4, S, D), BF), jax.ShapeDtypeStruct((3, D), F32),
                   jax.ShapeDtypeStruct((H, 1, LANES), F32)],
        scratch_shapes=[pltpu.VMEM((HB, LANES, LANES), F32), pltpu.VMEM((1, W), F32)],
        compiler_params=_cp("parallel", "arbitrary"),
    )(proj, proj, proj, proj, hg_lb, gn, o_all, states, dy)


def _log_sigmoid(u):
    return jnp.minimum(u, 0.0) - jnp.log(1.0 + jnp.exp(-jnp.abs(u)))


def _lane_put(base, lane, first, pieces):
    for n, p in enumerate(pieces):
        base = jnp.where(lane == first + n, p, base)
    return base


def _fox_cumsum(proj, bf_pad, *, name):
    S = proj.shape[0]
    D = proj.shape[1] // 5
    T = _pick(S, 256, 8)

    def body(fz_ref, b_ref, f_ref, carry):
        @pl.when(pl.program_id(0) == 0)
        def _():
            carry[...] = jnp.zeros_like(carry)

        logf = _log_sigmoid(fz_ref[...] + b_ref[...])
        tri = jnp.where(lax.broadcasted_iota(jnp.int32, (T, T), 0) >= lax.broadcasted_iota(jnp.int32, (T, T), 1),
                        1.0, 0.0).astype(BF)
        fv = _tri_dot(tri, logf) + carry[...]
        f_ref[...] = fv
        carry[...] = _row_of(fv, lax.broadcasted_iota(jnp.int32, (T, LANES), 0), T - 1)

    return pl.pallas_call(
        body, name=name, grid=(S // T,),
        in_specs=[pl.BlockSpec((T, LANES), lambda i: (i, 4 * D // LANES)), pl.BlockSpec((1, LANES), lambda i: (0, 0))],
        out_specs=pl.BlockSpec((T, LANES), lambda i: (i, 0)),
        out_shape=jax.ShapeDtypeStruct((S, LANES), F32),
        scratch_shapes=[pltpu.VMEM((1, LANES), F32)],
        compiler_params=_cp("arbitrary"),
    )(proj, bf_pad)


def _pair_stats(sq, lo):
    s_lo = jnp.sum(jnp.where(lo, sq, 0.0), axis=-1, keepdims=True)
    s_hi = jnp.sum(jnp.where(lo, 0.0, sq), axis=-1, keepdims=True)
    return jnp.where(lo, s_lo, s_hi) * (1.0 / FOX_DH)


def _fox_prep(proj, fcum, qw2, kw2, *, name):
    S = proj.shape[0]
    D = proj.shape[1] // 5
    HP = D // LANES
    T = _pick(S, 512, 16)

    def body(q_ref, k_ref, v_ref, f_ref, qw_ref, kw_ref, qa_ref, ka_ref, va_ref):
        hp = pl.program_id(1)
        lane = lax.broadcasted_iota(jnp.int32, (T, LANES), 1)
        lo = lane < FOX_DH
        qv, kv, vv, fv = q_ref[...], k_ref[...], v_ref[...], f_ref[...]
        qn = qv * lax.rsqrt(_pair_stats(qv * qv, lo) + EPS) * qw_ref[...] * (0.125 * LOG2E)
        kn = kv * lax.rsqrt(_pair_stats(kv * kv, lo) + EPS) * kw_ref[...]
        ones_q = jnp.where((lane >= 67) & (lane <= 69), 1.0, 0.0)
        ones_k = jnp.where(((lane >= 64) & (lane <= 66)) | ((lane >= 70) & (lane <= 72)), 1.0, 0.0)
        ones_v = jnp.where((lane >= 64) & (lane <= 66), 1.0, 0.0)
        for hh in range(2):
            fh = jnp.sum(jnp.where(lane == 2 * hp + hh, fv, 0.0), axis=-1, keepdims=True) * LOG2E
            pieces = [p.astype(F32) for p in _split3(fh)]

            def half(x):
                return jnp.where(lo, x if hh == 0 else pltpu.roll(x, FOX_DH, 1), 0.0)

            qa_ref[hh] = _lane_put(half(qn) + ones_q, lane, 64, pieces).astype(BF)
            ka_ref[hh] = _lane_put(half(kn) + ones_k, lane, 67, [-p for p in pieces]).astype(BF)
            va_ref[hh] = (half(vv) + ones_v).astype(BF)

    def part(p):
        return pl.BlockSpec((T, LANES), lambda i, hp: (i, p * HP + hp))

    vec = pl.BlockSpec((1, LANES), lambda i, hp: (0, 0))
    aug = pl.BlockSpec((2, T, LANES), lambda i, hp: (hp, i, 0))
    return pl.pallas_call(
        body, name=name, grid=(S // T, HP),
        in_specs=[part(0), part(1), part(2), pl.BlockSpec((T, LANES), lambda i, hp: (i, 0)), vec, vec],
        out_specs=[aug, aug, aug],
        out_shape=[jax.ShapeDtypeStruct((2 * HP, S, LANES), BF)] * 3,
        compiler_params=_cp("parallel", "arbitrary"),
    )(proj, proj, proj, fcum, qw2, kw2)


def _fox_block(S):
    return _pick(S, 256, 16)


def _fox_skip_bounds(fcum, qn_w, kn_w, nheads):
    S = fcum.shape[0]
    B = _fox_block(S)
    qk = 8.0 * LOG2E * 1.02 * jnp.max(jnp.abs(qn_w)) * jnp.max(jnp.abs(kn_w))
    thresh = -(2.0 * qk + 160.0)
    f2 = fcum[:, :nheads] * LOG2E
    first, last = f2[0::B], f2[B - 1::B]
    nb = S // B
    blk = jnp.arange(nb)
    dead = (first[0::2, None, :] - last[None, :, :]) < thresh
    jmin = jnp.sum(dead & (blk[None, :, None] < 2 * jnp.arange(nb // 2)[:, None, None]), axis=1)
    live = (first[:, None, :] - last[None, :, :]) >= thresh
    imax = blk[:, None] + jnp.sum(live & (blk[:, None, None] > blk[None, :, None]), axis=0)
    return jmin.T.astype(jnp.int32), imax.T.astype(jnp.int32)


def _fox_fwd(jmin, qa, ka, va, proj, *, name):
    H, S, _ = qa.shape
    HP = H // 2
    D = HP * LANES
    B = _fox_block(S)
    BQ = 2 * B
    nq = S // BQ

    def body(jmin_ref, q_ref, k_ref, v_ref, g_ref, y_ref, o_ref, q2_ref):
        hp, i = pl.program_id(0), pl.program_id(1)
        lane = lax.broadcasted_iota(jnp.int32, (B, LANES), 1)
        lo = lane < FOX_DH
        causal = lax.broadcasted_iota(jnp.int32, (B, B), 1) <= lax.broadcasted_iota(jnp.int32, (B, B), 0)
        m0, acc0 = jnp.full((B, 1), -jnp.inf, F32), jnp.zeros((B, LANES), F32)
        outs = []
        for hh in range(2):
            qbs = [q_ref[hh, 0:B, :], q_ref[hh, B:BQ, :]]

            def block(c, j, m, acc, masked=False):
                sl = pl.ds(pl.multiple_of(j * B, B), B)
                s = _dg(qbs[c], k_ref[hh, sl, :], NT)
                if masked:
                    s = jnp.where(causal, s, -jnp.inf)
                m_new = jnp.maximum(m, jnp.max(s, axis=-1, keepdims=True))
                p = jnp.exp2(s - m_new)
                ph = p.astype(BF)
                pl_ = (p - ph.astype(F32)).astype(BF)
                vb = v_ref[hh, sl, :]
                return m_new, acc * jnp.exp2(m - m_new) + (_dot(ph, vb) + _dot(pl_, vb))

            def both(j, carry):
                ma, aa, mb, ab = carry
                return block(0, j, ma, aa) + block(1, j, mb, ab)

            ma, aa, mb, ab = lax.fori_loop(jmin_ref[2 * hp + hh, i], 2 * i, both, (m0, acc0, m0, acc0))
            ma, aa = block(0, 2 * i, ma, aa, masked=True)
            mb, ab = block(1, 2 * i, mb, ab)
            mb, ab = block(1, 2 * i + 1, mb, ab, masked=True)
            for c, (m, acc) in enumerate(((ma, aa), (mb, ab))):
                l = jnp.sum(jnp.where(lane == FOX_DH, acc, 0.0), axis=-1, keepdims=True)
                outs.append(acc / l)
                neg_lse = [-(p.astype(F32)) for p in _split3(m + jnp.log2(l))]
                q2_ref[hh, c * B:(c + 1) * B, :] = _lane_put(qbs[c].astype(F32), lane, 70, neg_lse).astype(BF)
        for c in range(2):
            rows = slice(c * B, (c + 1) * B)
            o = jnp.where(lo, outs[c], pltpu.roll(outs[2 + c], FOX_DH, 1))
            o_ref[rows, :] = o
            y_ref[rows, :] = (o * _sigmoid(g_ref[rows, :])).astype(BF)

    blk = pl.BlockSpec((BQ, LANES), lambda hp, i, jm: (i, hp))
    qblk = pl.BlockSpec((2, BQ, LANES), lambda hp, i, jm: (hp, i, 0))
    full = pl.BlockSpec((2, S, LANES), lambda hp, i, jm: (hp, 0, 0))
    return pl.pallas_call(
        body, name=name,
        grid_spec=pltpu.PrefetchScalarGridSpec(
            num_scalar_prefetch=1, grid=(HP, nq),
            in_specs=[qblk, full, full, pl.BlockSpec((BQ, LANES), lambda hp, i, jm: (i, 3 * HP + hp))],
            out_specs=[blk, blk, qblk]),
        out_shape=[jax.ShapeDtypeStruct((S, D), BF), jax.ShapeDtypeStruct((S, D), F32),
                   jax.ShapeDtypeStruct((H, S, LANES), BF)],
        compiler_params=_cp("parallel", "arbitrary"),
    )(jmin, qa, ka, va, proj)


def _fox_bwd_prep(dy, o, proj, *, name):
    S, D = dy.shape
    HP = D // LANES
    T = _pick(S, 512, 16)

    def body(dy_ref, o_ref, g_ref, da_ref):
        lane = lax.broadcasted_iota(jnp.int32, (T, LANES), 1)
        lo = lane < FOX_DH
        do = (dy_ref[...] * _sigmoid(g_ref[...])).astype(BF).astype(F32)
        prod = do * o_ref[...]
        d_lo = jnp.sum(jnp.where(lo, prod, 0.0), axis=-1, keepdims=True)
        d_hi = jnp.sum(jnp.where(lo, 0.0, prod), axis=-1, keepdims=True)
        for hh, delta in enumerate((d_lo, d_hi)):
            base = jnp.where(lo, do if hh == 0 else pltpu.roll(do, FOX_DH, 1), 0.0)
            da_ref[hh] = _lane_put(base, lane, 64, [-(p.astype(F32)) for p in _split3(delta)]).astype(BF)

    blk = pl.BlockSpec((T, LANES), lambda i, hp: (i, hp))
    return pl.pallas_call(
        body, name=name, grid=(S // T, HP),
        in_specs=[blk, blk, pl.BlockSpec((T, LANES), lambda i, hp: (i, 3 * HP + hp))],
        out_specs=pl.BlockSpec((2, T, LANES), lambda i, hp: (hp, i, 0)),
        out_shape=jax.ShapeDtypeStruct((2 * HP, S, LANES), BF),
        compiler_params=_cp("parallel", "arbitrary"),
    )(dy, o, proj)


def _fox_bwd(imax, q2, ka, va, doa, *, name):
    H, S, _ = q2.shape
    B = _fox_block(S)
    nb = S // B

    def body(imax_ref, q_ref, do_ref, k_ref, v_ref, dq_ref, dk_ref, dv_ref, cs_ref):
        j = pl.program_id(1)
        end = imax_ref[pl.program_id(0), j] + 1

        @pl.when(j == 0)
        def _():
            dq_ref[...] = jnp.zeros_like(dq_ref)

        kb, vb = k_ref[...], v_ref[...]
        causal = lax.broadcasted_iota(jnp.int32, (B, B), 1) <= lax.broadcasted_iota(jnp.int32, (B, B), 0)

        def step(i, carry, masked=False):
            dk_acc, dv_acc, cs_acc = carry
            sl = pl.ds(pl.multiple_of(i * B, B), B)
            qb, dob = q_ref[sl, :], do_ref[sl, :]
            s = _dg(qb, kb, NT)
            if masked:
                s = jnp.where(causal, s, -jnp.inf)
            p = jnp.exp2(s)
            ds = p * _dg(dob, vb, NT)
            dsb = ds.astype(BF)
            cs_acc = cs_acc + jnp.sum(ds.reshape(B // 8, 8, B), axis=0)
            dv_acc = dv_acc + _dg(p.astype(BF), dob, TN)
            dk_acc = dk_acc + _dg(dsb, qb, TN)
            dq_ref[sl, :] += _dot(dsb, kb)
            return dk_acc, dv_acc, cs_acc

        zero = jnp.zeros((B, LANES), F32)
        carry = step(j, (zero, zero, jnp.zeros((8, B), F32)), masked=True)
        npair = (end - 1 - j) // 2
        carry = lax.fori_loop(0, npair, lambda ii, c: step(j + 2 + 2 * ii, step(j + 1 + 2 * ii, c)), carry)
        dk_acc, dv_acc, cs_acc = lax.fori_loop(j + 1 + 2 * npair, end, step, carry)
        dk_ref[...] = dk_acc
        dv_ref[...] = dv_acc
        cs_ref[...] = jnp.sum(cs_acc, axis=0, keepdims=True)

    full = pl.BlockSpec((None, S, LANES), lambda h, j, im: (h, 0, 0))
    blk = pl.BlockSpec((None, B, LANES), lambda h, j, im: (h, j, 0))
    return pl.pallas_call(
        body, name=name,
        grid_spec=pltpu.PrefetchScalarGridSpec(
            num_scalar_prefetch=1, grid=(H, nb),
            in_specs=[full, full, blk, blk],
            out_specs=[full, blk, blk, pl.BlockSpec((None, 1, B), lambda h, j, im: (h, 0, j))]),
        out_shape=[jax.ShapeDtypeStruct((H, S, LANES), F32)] * 3 + [jax.ShapeDtypeStruct((H, 1, S), F32)],
        compiler_params=_cp("parallel", "arbitrary"),
    )(imax, q2, doa, ka, va)


def _fox_bwd_post(dqa, dka, dva, proj, dy, o, qw2, kw2, *, name):
    S, D = dy.shape
    HP = D // LANES
    T = _pick(S, 512, 16)

    def body(dq_ref, dk_ref, dv_ref, q_ref, k_ref, g_ref, dy_ref, o_ref, qw_ref, kw_ref, dp_ref, dqw_ref, dkw_ref):
        @pl.when((pl.program_id(0) == 0) & (pl.program_id(1) == 0))
        def _():
            dqw_ref[...] = jnp.zeros_like(dqw_ref)
            dkw_ref[...] = jnp.zeros_like(dkw_ref)

        lane = lax.broadcasted_iota(jnp.int32, (T, LANES), 1)
        lo = lane < FOX_DH

        def pair(ref):
            return jnp.where(lo, ref[0], pltpu.roll(ref[1], FOX_DH, 1))

        def norm_bwd(xv, w, dyn, dw_ref):
            r = lax.rsqrt(_pair_stats(xv * xv, lo) + EPS)
            xr = xv * r
            dw_ref[...] += jnp.sum(dyn * xr, axis=0, keepdims=True)
            u = dyn * w
            return r * (u - xr * _pair_stats(u * xr, lo))

        dp_ref[0] = norm_bwd(q_ref[...], qw_ref[...], pair(dq_ref) * 0.125, dqw_ref).astype(BF)
        dp_ref[1] = norm_bwd(k_ref[...], kw_ref[...], pair(dk_ref) * (1.0 / LOG2E), dkw_ref).astype(BF)
        dp_ref[2] = pair(dv_ref).astype(BF)
        sg = _sigmoid(g_ref[...])
        dp_ref[3] = (dy_ref[...] * o_ref[...] * sg * (1.0 - sg)).astype(BF)

    def part(p):
        return pl.BlockSpec((T, LANES), lambda i, hp: (i, p * HP + hp))

    aug = pl.BlockSpec((2, T, LANES), lambda i, hp: (hp, i, 0))
    blk = pl.BlockSpec((T, LANES), lambda i, hp: (i, hp))
    vec = pl.BlockSpec((1, LANES), lambda i, hp: (0, 0))
    return pl.pallas_call(
        body, name=name, grid=(S // T, HP),
        in_specs=[aug, aug, aug, part(0), part(1), part(3), blk, blk, vec, vec],
        out_specs=[pl.BlockSpec((4, T, LANES), lambda i, hp: (0, i, hp)), vec, vec],
        out_shape=[jax.ShapeDtypeStruct((5, S, D), BF), jax.ShapeDtypeStruct((1, LANES), F32),
                   jax.ShapeDtypeStruct((1, LANES), F32)],
        compiler_params=_cp("arbitrary", "arbitrary"),
    )(dqa, dka, dva, proj, proj, proj, dy, o, qw2, kw2)


def _fox_dfz(colsum, nheads, proj, bf_pad, dproj, *, name):
    S = colsum.shape[0]
    H = nheads
    D = dproj.shape[2]
    T = _pick(S, 256, 16)
    nb = S // T

    def body(cs_ref, fz_ref, b_ref, _, dp_ref, db_ref, carry):
        @pl.when(pl.program_id(0) == 0)
        def _():
            carry[...] = jnp.zeros_like(carry)
            db_ref[...] = jnp.zeros_like(db_ref)

        lane = lax.broadcasted_iota(jnp.int32, (T, LANES), 1)
        df = -cs_ref[...]
        triu = jnp.where(lax.broadcasted_iota(jnp.int32, (T, T), 0) <= lax.broadcasted_iota(jnp.int32, (T, T), 1),
                         1.0, 0.0).astype(BF)
        dlogf = _tri_dot(triu, df) + carry[...]
        carry[...] = _row_of(dlogf, lax.broadcasted_iota(jnp.int32, (T, LANES), 0), 0)
        dfz = jnp.where(lane < H, dlogf * _sigmoid(-(fz_ref[...] + b_ref[...])), 0.0)
        db_ref[...] += jnp.sum(dfz, axis=0, keepdims=True)
        dp_ref[...] = jnp.zeros_like(dp_ref)
        dp_ref[:, 0:LANES] = dfz.astype(BF)

    return pl.pallas_call(
        body, name=name, grid=(nb,),
        in_specs=[pl.BlockSpec((T, LANES), lambda i: (nb - 1 - i, 0)),
                  pl.BlockSpec((T, LANES), lambda i: (nb - 1 - i, 4 * D // LANES)),
                  pl.BlockSpec((1, LANES), lambda i: (0, 0)),
                  pl.BlockSpec(memory_space=pl.ANY)],
        out_specs=[pl.BlockSpec((None, T, D), lambda i: (4, nb - 1 - i, 0)), pl.BlockSpec((1, LANES), lambda i: (0, 0))],
        out_shape=[jax.ShapeDtypeStruct(dproj.shape, BF), jax.ShapeDtypeStruct((1, LANES), F32)],
        scratch_shapes=[pltpu.VMEM((1, LANES), F32)],
        input_output_aliases={3: 0},
        compiler_params=_cp("arbitrary"),
    )(colsum, proj, bf_pad, dproj)


def _mod_fwd(c16, w, b, *, name):
    L, D, N = w.shape
    tn = _pick(N, 512)

    def body(c_ref, w_ref, b_ref, o_ref):
        cv = c_ref[...]
        ca = (cv * _sigmoid(cv)).astype(BF)
        o_ref[...] = _dot(ca, w_ref[...].astype(BF)) + b_ref[...]

    return pl.pallas_call(
        body, name=name, grid=(L, N // tn),
        in_specs=[pl.BlockSpec((16, D), lambda l, j: (0, 0)), pl.BlockSpec((None, D, tn), lambda l, j: (l, 0, j)),
                  pl.BlockSpec((None, 1, tn), lambda l, j: (l, 0, j))],
        out_specs=pl.BlockSpec((None, 16, tn), lambda l, j: (l, 0, j)),
        out_shape=jax.ShapeDtypeStruct((L, 16, N), F32),
        compiler_params=_cp("parallel", "arbitrary"),
    )(c16, w, b)


def _mod_bwd(c16, dmod, *, name):
    L, _, N = dmod.shape
    D = c16.shape[1]
    tn = _pick(N, 512)

    def body(c_ref, d_ref, o_ref):
        cv = c_ref[...]
        ca = (cv * _sigmoid(cv)).astype(BF)
        o_ref[...] = _dg(ca, d_ref[...].astype(BF), TN)

    return pl.pallas_call(
        body, name=name, grid=(L, N // tn),
        in_specs=[pl.BlockSpec((16, D), lambda l, j: (0, 0)), pl.BlockSpec((None, 16, tn), lambda l, j: (l, 0, j))],
        out_specs=pl.BlockSpec((None, D, tn), lambda l, j: (l, 0, j)),
        out_shape=jax.ShapeDtypeStruct((L, D, N), F32),
        compiler_params=_cp("parallel", "arbitrary"),
    )(c16, dmod)


def _adamw_math(w, g, m, v):
    m = ADAM_B1 * m + (1.0 - ADAM_B1) * g
    v = ADAM_B2 * v + (1.0 - ADAM_B2) * (g * g)
    m_hat = m / (1.0 - ADAM_B1 ** ADAM_STEP)
    v_hat = v / (1.0 - ADAM_B2 ** ADAM_STEP)
    return -ADAM_LR * (m_hat / (jnp.sqrt(v_hat) + ADAM_EPS) + ADAM_WD * w), m, v


def _adamw(w, g, m, v, *, g_row0=0, name):
    R, C = w.shape
    tr = min(math.gcd(g_row0, 256) if g_row0 else 256, -(-R // 8) * 8)
    g0 = g_row0 // tr

    def body(w_ref, g_ref, m_ref, v_ref, d_ref, mo_ref, vo_ref):
        d, mn, vn = _adamw_math(w_ref[...], g_ref[...], m_ref[...], v_ref[...])
        d_ref[...] = d
        mo_ref[...] = mn
        vo_ref[...] = vn

    blk = pl.BlockSpec((tr, C), lambda i: (i, 0))
    return pl.pallas_call(
        body, name=name, grid=(pl.cdiv(R, tr),),
        in_specs=[blk, pl.BlockSpec((tr, C), lambda i: (g0 + i, 0)), blk, blk],
        out_specs=[blk, blk, blk],
        out_shape=[jax.ShapeDtypeStruct((R, C), F32)] * 3,
        compiler_params=_cp("parallel"),
    )(w, g, m, v)


def _sum_parts(parts, *, name):
    P, R, C = parts.shape

    def body(p_ref, o_ref):
        acc = p_ref[0]
        for p in range(1, P):
            acc = acc + p_ref[p]
        o_ref[...] = acc

    return pl.pallas_call(
        body, name=name, grid=(1,),
        in_specs=[pl.BlockSpec((P, R, C), lambda i: (0, 0, 0))],
        out_specs=pl.BlockSpec((R, C), lambda i: (0, 0)),
        out_shape=jax.ShapeDtypeStruct((R, C), F32),
        compiler_params=_cp("arbitrary"),
    )(parts)


def _add_halves(g4, recv, c_idx, *, name):
    _, _, Rh, C = g4.shape
    tr = _pick(Rh, 256, 8)

    def body(c_ref, a_ref, b_ref, o_ref):
        o_ref[...] = a_ref[...] + b_ref[...]

    return pl.pallas_call(
        body, name=name,
        grid_spec=pltpu.PrefetchScalarGridSpec(
            num_scalar_prefetch=1, grid=(4, pl.cdiv(Rh, tr)),
            in_specs=[pl.BlockSpec((None, None, tr, C), lambda j, r, c: (j, c[0], r, 0)),
                      pl.BlockSpec((None, tr, C), lambda j, r, c: (j, r, 0))],
            out_specs=pl.BlockSpec((None, tr, C), lambda j, r, c: (j, r, 0))),
        out_shape=jax.ShapeDtypeStruct((4, Rh, C), F32),
        compiler_params=_cp("parallel", "arbitrary"),
    )(c_idx, g4, recv)


def _add_four(own, recv, chip_idx, *, name):
    _, Rh, C = own.shape
    tr = _pick(Rh, 256, 8)

    def body(c_ref, a_ref, b_ref, o_ref):
        o_ref[...] = ((a_ref[...] + b_ref[0]) + b_ref[1]) + b_ref[2]

    return pl.pallas_call(
        body, name=name,
        grid_spec=pltpu.PrefetchScalarGridSpec(
            num_scalar_prefetch=1, grid=(pl.cdiv(Rh, tr),),
            in_specs=[pl.BlockSpec((None, tr, C), lambda r, c: (c[0], r, 0)),
                      pl.BlockSpec((3, tr, C), lambda r, c: (0, r, 0))],
            out_specs=pl.BlockSpec((tr, C), lambda r, c: (r, 0))),
        out_shape=jax.ShapeDtypeStruct((Rh, C), F32),
        compiler_params=_cp("arbitrary"),
    )(chip_idx, own, recv)


HBM = pl.BlockSpec(memory_space=pltpu.HBM)


def _mesh_pos():
    return lax.axis_index("x"), lax.axis_index("y"), lax.axis_index("c")


def _other_chips(x, y):
    return [(1 - x, y), (x, 1 - y), (1 - x, 1 - y)]


def _allgather_small(xs, *, name):
    m_per, n = xs.shape

    def body(x_ref, out_ref, send_sems, recv_sems, local_sem):
        x, y, c = _mesh_pos()
        me, sibling = (x, y, c), (x, y, 1 - c)
        chips = _other_chips(x, y)

        def rows(px, py, pc):
            return out_ref.at[pl.ds((4 * px + 2 * py + pc) * m_per, m_per), :]

        def copy(k, block, to, src=None):
            return pltpu.make_async_remote_copy(
                src_ref=rows(*block) if src is None else src, dst_ref=rows(*block),
                send_sem=send_sems.at[k], recv_sem=recv_sems.at[k], device_id=to, device_id_type=MESH)

        mine = pltpu.make_async_copy(x_ref, rows(*me), local_sem)
        mine.start()
        first = [copy(0, me, sibling, src=x_ref)]
        first += [copy(1 + j, me, (*chip, c), src=x_ref) for j, chip in enumerate(chips)]
        for cp in first:
            cp.start()
        passed = [copy(4 + j, (*chip, c), sibling) for j, chip in enumerate(chips)]
        for j, chip in enumerate(chips):
            copy(1 + j, (*chip, c), me).wait_recv()
            passed[j].start()
        copy(0, sibling, me).wait_recv()
        for j, chip in enumerate(chips):
            copy(4 + j, (*chip, 1 - c), me).wait_recv()
        for cp in first + passed:
            cp.wait_send()
        mine.wait()

    return pl.pallas_call(
        body, name=name,
        out_shape=jax.ShapeDtypeStruct((N_DEV * m_per, n), xs.dtype),
        in_specs=[pl.BlockSpec(memory_space=pltpu.VMEM)],
        out_specs=pl.BlockSpec(memory_space=pltpu.VMEM),
        scratch_shapes=[pltpu.SemaphoreType.DMA((7,)), pltpu.SemaphoreType.DMA((7,)), pltpu.SemaphoreType.DMA],
    )(xs)


def _allgather_chip_slabs(slab, *, name):
    R, C = slab.shape
    Rh = R // 2

    def body(s_ref, out_ref, send_sems, recv_sems, local_sem):
        x, y, c = _mesh_pos()
        sibling = (x, y, 1 - c)
        chips = _other_chips(x, y)

        def half(px, py, pc):
            return out_ref.at[2 * px + py, pl.ds(pc * Rh, Rh), :]

        def copy(k, block, to, src=None):
            return pltpu.make_async_remote_copy(
                src_ref=half(*block) if src is None else src, dst_ref=half(*block),
                send_sem=send_sems.at[k], recv_sem=recv_sems.at[k], device_id=to, device_id_type=MESH)

        mine = pltpu.make_async_copy(s_ref, out_ref.at[2 * x + y], local_sem)
        mine.start()
        first = [copy(j, (x, y, c), (*chip, c), src=s_ref.at[pl.ds(c * Rh, Rh), :]) for j, chip in enumerate(chips)]
        for cp in first:
            cp.start()
        passed = [copy(3 + j, (*chip, c), sibling) for j, chip in enumerate(chips)]
        for j, chip in enumerate(chips):
            copy(j, (*chip, c), (x, y, c)).wait_recv()
            passed[j].start()
        for j, chip in enumerate(chips):
            copy(3 + j, (*chip, 1 - c), (x, y, c)).wait_recv()
        for cp in first + passed:
            cp.wait_send()
        mine.wait()

    return pl.pallas_call(
        body, name=name,
        out_shape=jax.ShapeDtypeStruct((N_CHIPS, R, C), slab.dtype),
        in_specs=[HBM], out_specs=HBM,
        scratch_shapes=[pltpu.SemaphoreType.DMA((6,)), pltpu.SemaphoreType.DMA((6,)), pltpu.SemaphoreType.DMA],
    )(slab)


def _swap_halves(g4, *, name):
    _, _, Rh, C = g4.shape

    def body(g_ref, out_ref, send_sems, recv_sems):
        x, y, c = _mesh_pos()
        copies = [pltpu.make_async_remote_copy(
            src_ref=g_ref.at[j, 1 - c], dst_ref=out_ref.at[j], send_sem=send_sems.at[j], recv_sem=recv_sems.at[j],
            device_id=(x, y, 1 - c), device_id_type=MESH) for j in range(N_CHIPS)]
        for cp in copies:
            cp.start()
        for cp in copies:
            cp.wait()

    return pl.pallas_call(
        body, name=name,
        out_shape=jax.ShapeDtypeStruct((N_CHIPS, Rh, C), g4.dtype),
        in_specs=[HBM], out_specs=HBM,
        scratch_shapes=[pltpu.SemaphoreType.DMA((N_CHIPS,)), pltpu.SemaphoreType.DMA((N_CHIPS,))],
    )(g4)


def _scatter_partials(part, *, name):
    _, Rh, C = part.shape

    def body(p_ref, out_ref, send_sems, recv_sems):
        x, y, c = _mesh_pos()
        copies = [pltpu.make_async_remote_copy(
            src_ref=p_ref.at[2 * px + py], dst_ref=out_ref.at[j], send_sem=send_sems.at[j], recv_sem=recv_sems.at[j],
            device_id=(px, py, c), device_id_type=MESH) for j, (px, py) in enumerate(_other_chips(x, y))]
        for cp in copies:
            cp.start()
        for cp in copies:
            cp.wait()

    return pl.pallas_call(
        body, name=name,
        out_shape=jax.ShapeDtypeStruct((3, Rh, C), part.dtype),
        in_specs=[HBM], out_specs=HBM,
        scratch_shapes=[pltpu.SemaphoreType.DMA((3,)), pltpu.SemaphoreType.DMA((3,))],
    )(part)


def _join_halves(mine, *, name):
    Rh, C = mine.shape

    def body(m_ref, out_ref, send_sem, recv_sem, local_sem):
        x, y, c = _mesh_pos()
        keep = pltpu.make_async_copy(m_ref, out_ref.at[c], local_sem)
        keep.start()
        cp = pltpu.make_async_remote_copy(
            src_ref=m_ref, dst_ref=out_ref.at[c], send_sem=send_sem, recv_sem=recv_sem,
            device_id=(x, y, 1 - c), device_id_type=MESH)
        cp.start()
        cp.wait()
        keep.wait()

    return pl.pallas_call(
        body, name=name,
        out_shape=jax.ShapeDtypeStruct((2, Rh, C), mine.dtype),
        in_specs=[HBM], out_specs=HBM,
        scratch_shapes=[pltpu.SemaphoreType.DMA, pltpu.SemaphoreType.DMA, pltpu.SemaphoreType.DMA],
    )(mine)


def _pad_rows(a, mult):
    pad = (-a.shape[0]) % mult
    return a if pad == 0 else jnp.pad(a, ((0, pad),) + ((0, 0),) * (a.ndim - 1))


def _local_step(x, target, mod, wts, small):
    S, D = x.shape
    HP = D // LANES
    row = lambda v: v.reshape(1, -1)
    msplit = [[row(mod[i, k * D:(k + 1) * D]) for k in range(6)] for i in range(2)]
    gw, gs = {}, {}
    dmod = [[None] * 6 for _ in range(2)]

    sh1, sc1, g1, sh2, sc2, g2 = msplit[0]
    n1w0, n2w0 = row(small["norm1_w"][0]), row(small["norm2_w"][0])
    proj0, h1_0 = _ln_matmul(x, n1w0, sc1, sh1, wts["hg_w_in"], relu2=False, name="hg_in_proj")
    gn = small["hg_gn_w"].reshape(1, LANES)
    ypre0, o0, states = _hg_fwd(proj0, small["hg_lb"], gn, name="hg_fwd")
    x1, ymix0 = _matmul_resid(ypre0, wts["hg_w_out"], x, g1, name="hg_out_proj")
    a0, u0, h2_0 = _ln_matmul(x1, n2w0, sc2, sh2, wts["mlp_w1_0"], relu2=True, name="mlp0_up")
    x2, ymlp0 = _matmul_resid(u0, wts["mlp_w2_0"], x1, g2, name="mlp0_down")

    sh1b, sc1b, g1b, sh2b, sc2b, g2b = msplit[1]
    n1w1, n2w1 = row(small["norm1_w"][1]), row(small["norm2_w"][1])
    proj1, h1_1 = _ln_matmul(x2, n1w1, sc1b, sh1b, wts["fox_w_in"], relu2=False, name="fox_in_proj")
    nheads = 2 * HP
    bf_pad = jnp.pad(small["fox_b_f"].reshape(1, nheads), ((0, 0), (0, LANES - nheads)))
    qw2 = jnp.tile(small["fox_qn_w"].reshape(1, FOX_DH), (1, 2))
    kw2 = jnp.tile(small["fox_kn_w"].reshape(1, FOX_DH), (1, 2))
    fcum = _fox_cumsum(proj1, bf_pad, name="fox_cumsum")
    qa, ka, va = _fox_prep(proj1, fcum, qw2, kw2, name="fox_prep")
    jmin, imax = _fox_skip_bounds(fcum, small["fox_qn_w"], small["fox_kn_w"], nheads)
    ypre1, o1, q2 = _fox_fwd(jmin, qa, ka, va, proj1, name="fox_fwd")
    x3, ymix1 = _matmul_resid(ypre1, wts["fox_w_out"], x2, g1b, name="fox_out_proj")
    a1, u1, h2_1 = _ln_matmul(x3, n2w1, sc2b, sh2b, wts["mlp_w1_1"], relu2=True, name="mlp1_up")
    x4, ymlp1 = _matmul_resid(u1, wts["mlp_w2_1"], x3, g2b, name="mlp1_down")

    loss, dx4, dfw = _loss_kernel(x4, row(small["final_w"]), target, name="loss")
    gs["final_w"] = dfw.reshape(-1)

    def mlp_bwd(i, dx_out, x_in, h2, a, u, ymlp, n2w, sc2_, g2_):
        dz, dm, dg2 = _gate_matmul_nt(dx_out, g2_, ymlp, wts[f"mlp_w2_{i}"], a, name=f"mlp{i}_down_bwd")
        gw[f"mlp_w2_{i}"] = _matmul_tn(u, dm[None], name=f"mlp{i}_dw2")
        gw[f"mlp_w1_{i}"] = _matmul_tn(h2, dz[None], name=f"mlp{i}_dw1")
        dx_in, dsc, dsh, dnw = _matmul_nt_lnbwd(dz[None], wts[f"mlp_w1_{i}"], x_in, n2w, sc2_, dx_out,
                                                name=f"mlp{i}_up_bwd")
        dmod[i][3], dmod[i][4], dmod[i][5] = dsh, dsc, dg2
        return dx_in, dnw

    dx3, dn2w1 = mlp_bwd(1, dx4, x3, h2_1, a1, u1, ymlp1, n2w1, sc2b, g2b)
    dyp1, dm1, dg1b = _gate_matmul_nt(dx3, g1b, ymix1, wts["fox_w_out"], None, name="fox_out_bwd")
    gw["fox_w_out"] = _matmul_tn(ypre1, dm1[None], name="fox_dw_out")
    doa = _fox_bwd_prep(dyp1, o1, proj1, name="fox_bwd_prep")
    dqa, dka, dva, colsum = _fox_bwd(imax, q2, ka, va, doa, name="fox_bwd")
    colsum = jnp.pad(colsum[:, 0, :].T, ((0, 0), (0, LANES - nheads)))
    dproj1, dqw, dkw = _fox_bwd_post(dqa, dka, dva, proj1, dyp1, o1, qw2, kw2, name="fox_bwd_post")
    dproj1, dbf = _fox_dfz(colsum, nheads, proj1, bf_pad, dproj1, name="fox_dfz")
    gw["fox_w_in"] = _matmul_tn(h1_1, dproj1, name="fox_dw_in")
    dx2, dsc, dsh, dn1w1 = _matmul_nt_lnbwd(dproj1, wts["fox_w_in"], x2, n1w1, sc1b, dx3, name="fox_in_bwd")
    dmod[1][0], dmod[1][1], dmod[1][2] = dsh, dsc, dg1b
    gs["fox_qn_w"] = dqw[0, :FOX_DH] + dqw[0, FOX_DH:]
    gs["fox_kn_w"] = dkw[0, :FOX_DH] + dkw[0, FOX_DH:]
    gs["fox_b_f"] = dbf[0, :nheads]

    dx1, dn2w0 = mlp_bwd(0, dx2, x1, h2_0, a0, u0, ymlp0, n2w0, sc2, g2)
    dyp0, dm0, dg1 = _gate_matmul_nt(dx1, g1, ymix0, wts["hg_w_out"], None, name="hg_out_bwd")
    gw["hg_w_out"] = _matmul_tn(ypre0, dm0[None], name="hg_dw_out")
    dproj0, dlb, dgn = _hg_bwd(proj0, small["hg_lb"], gn, o0, states, dyp0, name="hg_bwd")
    gw["hg_w_in"] = _matmul_tn(h1_0, dproj0, name="hg_dw_in")
    dx0, dsc, dsh, dn1w0 = _matmul_nt_lnbwd(dproj0, wts["hg_w_in"], x, n1w0, sc1, dx1, name="hg_in_bwd")
    dmod[0][0], dmod[0][1], dmod[0][2] = dsh, dsc, dg1
    gs["hg_lb"] = dlb
    gs["hg_gn_w"] = jnp.sum(dgn, axis=0)

    gs["norm1_w"] = jnp.concatenate([dn1w0, dn1w1], axis=0)
    gs["norm2_w"] = jnp.concatenate([dn2w0, dn2w1], axis=0)
    gs["dmod"] = jnp.stack([jnp.concatenate(dmod[i], axis=1)[0] for i in range(2)])
    return loss, dx0, gw, gs


SMALL_NAMES = ["norm1_w", "norm2_w", "hg_lb", "hg_gn_w", "fox_b_f", "fox_qn_w", "fox_kn_w", "final_w"]


def _pack_small(d, names):
    rows, offs, r0 = [], {}, 0
    for n in names:
        flat = d[n].reshape(-1)
        nr = -(-flat.shape[0] // LANES)
        rows.append(jnp.pad(flat, (0, nr * LANES - flat.shape[0])).reshape(nr, LANES))
        offs[n] = (r0, nr)
        r0 += nr
    return jnp.concatenate(rows, axis=0), offs


def _unpack_small(packed, offs, name, like):
    r0, nr = offs[name]
    return packed[r0:r0 + nr].reshape(-1)[:like.size].reshape(like.shape)


def kernel(x, c, w_mod, b_mod, norm1_w, norm2_w, hg_w_in, hg_w_out, hg_lb, hg_gn_w, fox_w_in, fox_b_f, fox_qn_w, fox_kn_w, fox_w_out, mlp_w1, mlp_w2, final_w, loss_target, m_w_mod, m_b_mod, m_norm1_w, m_norm2_w, m_hg_w_in, m_hg_w_out, m_hg_lb, m_hg_gn_w, m_fox_w_in, m_fox_b_f, m_fox_qn_w, m_fox_kn_w, m_fox_w_out, m_mlp_w1, m_mlp_w2, m_final_w, v_w_mod, v_b_mod, v_norm1_w, v_norm2_w, v_hg_w_in, v_hg_w_out, v_hg_lb, v_hg_gn_w, v_fox_w_in, v_fox_b_f, v_fox_qn_w, v_fox_kn_w, v_fox_w_out, v_mlp_w1, v_mlp_w2, v_final_w):
    S, D = x.shape[1], x.shape[2]
    nheads = D // FOX_DH
    ax, ay, ac = _mesh_pos()
    chip = 2 * ax + ay
    dev = 2 * chip + ac
    xs, tgt = x.reshape(S, D), loss_target.reshape(S, D)

    c_all = _allgather_small(_pad_rows(c.reshape(-1, LANES), 8), name="gather_c")
    c_all = c_all.reshape(N_DEV, -1)[:, :D]
    c16 = _pad_rows(c_all, 16)
    nmod = w_mod.shape[2]
    b_shard = lax.dynamic_slice_in_dim(b_mod, chip * nmod, nmod, axis=1)
    mod_shard = _mod_fwd(c16, w_mod, b_shard[:, None, :], name="mod_fwd")[:, :N_DEV]
    mod_all = _allgather_small(mod_shard.reshape(-1, LANES), name="gather_mod")
    mod_all = mod_all.reshape(N_CHIPS, 2, 2, N_DEV, nmod)[:, 0]
    mod = lax.dynamic_index_in_dim(mod_all, dev, axis=2, keepdims=False)
    mod = mod.transpose(1, 0, 2).reshape(2, N_CHIPS * nmod)

    fox_rows = fox_w_in.shape[2]
    segs = [hg_w_in[0], hg_w_out[0], fox_w_out[0], mlp_w1.reshape(2 * D, D), mlp_w2.reshape(2 * D, D),
            fox_w_in[0].reshape(fox_rows, D)]
    seg_rows = [s.shape[0] for s in segs]
    seg_off = [sum(seg_rows[:i]) for i in range(len(segs))]
    slab = _pad_rows(jnp.concatenate([s.astype(BF) for s in segs], axis=0), 32)
    R = slab.shape[0]
    gathered = _allgather_chip_slabs(slab, name="gather_weights")

    def seg(i):
        return gathered[:, seg_off[i]:seg_off[i] + seg_rows[i], :]

    col = lambda g: g.transpose(1, 0, 2).reshape(g.shape[1], -1)
    rowsh = lambda g: g.reshape(-1, g.shape[2])
    w1 = seg(3).reshape(N_CHIPS, 2, D, D)
    w2 = seg(4).reshape(N_CHIPS, 2, D, D)
    fox_in = col(seg(5).reshape(N_CHIPS, D, fox_rows))
    wts = {
        "hg_w_in": col(seg(0)), "hg_w_out": rowsh(seg(1)), "fox_w_out": rowsh(seg(2)),
        "mlp_w1_0": col(w1[:, 0]), "mlp_w1_1": col(w1[:, 1]), "mlp_w2_0": rowsh(w2[:, 0]), "mlp_w2_1": rowsh(w2[:, 1]),
        "fox_w_in": jnp.pad(fox_in, ((0, 0), (0, 5 * D - fox_in.shape[1]))),
    }
    small = {"norm1_w": norm1_w, "norm2_w": norm2_w, "hg_lb": hg_lb, "hg_gn_w": hg_gn_w, "fox_b_f": fox_b_f,
             "fox_qn_w": fox_qn_w, "fox_kn_w": fox_kn_w, "final_w": final_w}

    loss_part, grad_x, gw, gs = _local_step(xs, tgt, mod, wts, small)
    loss = lax.psum(loss_part[0, 0], ("x", "y", "c"))

    def uncol(g, n):
        return g.reshape(g.shape[0], N_CHIPS, n).transpose(1, 0, 2)

    gfox = uncol(gw["fox_w_in"][:, :4 * fox_rows], fox_rows).reshape(N_CHIPS, fox_rows, D)
    gsegs = [uncol(gw["hg_w_in"], D), gw["hg_w_out"].reshape(N_CHIPS, D // 4, D), gw["fox_w_out"].reshape(N_CHIPS, D // 4, D),
             jnp.concatenate([uncol(gw["mlp_w1_0"], D), uncol(gw["mlp_w1_1"], D)], axis=1),
             jnp.concatenate([gw["mlp_w2_0"].reshape(N_CHIPS, D, D), gw["mlp_w2_1"].reshape(N_CHIPS, D, D)], axis=1),
             gfox]
    gfull = jnp.concatenate(gsegs, axis=1)
    gfull = jnp.pad(gfull, ((0, 0), (0, R - gfull.shape[1]), (0, 0)))
    g4 = gfull.reshape(N_CHIPS, 2, R // 2, D)
    from_sibling = _swap_halves(g4, name="rs_swap_halves")
    chip_part = _add_halves(g4, from_sibling, ac.reshape(1), name="rs_add_halves")
    from_chips = _scatter_partials(chip_part, name="rs_scatter")
    my_half = _add_four(chip_part, from_chips, chip.reshape(1), name="rs_add_chips")
    gshard = _join_halves(my_half, name="rs_join").reshape(R, D)

    names = ["dmod"] + SMALL_NAMES
    packed, offs = _pack_small(gs, names)
    packed = _pad_rows(packed, 8)
    rp = packed.shape[0]
    parts = _allgather_small(packed, name="gather_small").reshape(N_DEV, rp, LANES)
    total = _sum_parts(parts, name="sum_small")
    r0, nr = offs["dmod"]
    dmod_all = parts[:, r0:r0 + nr].reshape(N_DEV, 2, N_CHIPS * nmod)
    dmod_shard = lax.dynamic_slice_in_dim(dmod_all, chip * nmod, nmod, axis=2).transpose(1, 0, 2)
    g_w_mod = _mod_bwd(c16, jnp.pad(dmod_shard, ((0, 0), (0, 16 - N_DEV), (0, 0))), name="mod_bwd")

    grads = {"w_mod": g_w_mod, "b_mod": _unpack_small(total, offs, "dmod", b_mod)}
    for n in SMALL_NAMES:
        grads[n] = _unpack_small(total, offs, n, small[n])

    given = dict(w_mod=(w_mod, m_w_mod, v_w_mod), b_mod=(b_mod, m_b_mod, v_b_mod), norm1_w=(norm1_w, m_norm1_w, v_norm1_w),
                 norm2_w=(norm2_w, m_norm2_w, v_norm2_w), hg_w_in=(hg_w_in, m_hg_w_in, v_hg_w_in),
                 hg_w_out=(hg_w_out, m_hg_w_out, v_hg_w_out), hg_lb=(hg_lb, m_hg_lb, v_hg_lb),
                 hg_gn_w=(hg_gn_w, m_hg_gn_w, v_hg_gn_w), fox_w_in=(fox_w_in, m_fox_w_in, v_fox_w_in),
                 fox_b_f=(fox_b_f, m_fox_b_f, v_fox_b_f), fox_qn_w=(fox_qn_w, m_fox_qn_w, v_fox_qn_w),
                 fox_kn_w=(fox_kn_w, m_fox_kn_w, v_fox_kn_w), fox_w_out=(fox_w_out, m_fox_w_out, v_fox_w_out),
                 mlp_w1=(mlp_w1, m_mlp_w1, v_mlp_w1), mlp_w2=(mlp_w2, m_mlp_w2, v_mlp_w2), final_w=(final_w, m_final_w, v_final_w))
    upd = {}

    big = [("hg_w_in", 0), ("hg_w_out", 1), ("fox_w_out", 2), ("mlp_w1", 3), ("mlp_w2", 4), ("fox_w_in", 5)]
    for n, i in big:
        w, m, v = given[n]
        flat = lambda a: a.reshape(seg_rows[i], D)
        d, mn, vn = _adamw(flat(w), gshard, flat(m), flat(v), g_row0=seg_off[i], name=f"adamw_{n}")
        grads[n] = gshard[seg_off[i]:seg_off[i] + seg_rows[i]].reshape(w.shape)
        upd[n] = tuple(a.reshape(w.shape) for a in (d, mn, vn))

    w, m, v = given["w_mod"]
    flat = lambda a: a.reshape(-1, nmod)
    upd["w_mod"] = tuple(a.reshape(w.shape) for a in _adamw(flat(w), flat(g_w_mod), flat(m), flat(v), name="adamw_w_mod"))

    snames = ["b_mod"] + SMALL_NAMES
    pw, soffs = _pack_small({n: given[n][0] for n in snames}, snames)
    pm, _ = _pack_small({n: given[n][1] for n in snames}, snames)
    pv, _ = _pack_small({n: given[n][2] for n in snames}, snames)
    pg, _ = _pack_small({n: grads[n] for n in snames}, snames)
    pw, pm, pv, pg = (_pad_rows(a, 8) for a in (pw, pm, pv, pg))
    sd, smn, svn = _adamw(pw, pg, pm, pv, name="adamw_small")
    for n in snames:
        like = given[n][0]
        upd[n] = tuple(_unpack_small(a, soffs, n, like) for a in (sd, smn, svn))

    order = ["w_mod", "b_mod", "norm1_w", "norm2_w", "hg_w_in", "hg_w_out", "hg_lb", "hg_gn_w", "fox_w_in", "fox_b_f",
             "fox_qn_w", "fox_kn_w", "fox_w_out", "mlp_w1", "mlp_w2", "final_w"]
    return (loss, grad_x.reshape(x.shape), *[grads[n] for n in order], *[upd[n][0] for n in order],
            *[upd[n][1] for n in order], *[upd[n][2] for n in order])
```

```python
import math

import jax
import jax.numpy as jnp
from jax import lax
from jax.experimental import pallas as pl
from jax.experimental.pallas import tpu as pltpu

EPS = 1e-6
ADAM_LR, ADAM_B1, ADAM_B2, ADAM_EPS, ADAM_WD, ADAM_STEP = 0.001, 0.9, 0.999, 1e-08, 0.01, 10

F32 = jnp.float32
BF = jnp.bfloat16
LANES = 128
HG_CHUNK = 64
HG_HEADS_PER_STEP = 8
HG_TOKENS_PER_STEP = 256
FOX_BWD_UNROLL = 4
LOG2E = 1.4426950408889634
FOX_DH = 64
N_CHIPS = 4
N_DEV = 8
VMEM_LIMIT = 48 * 1024 * 1024
MESH = pl.DeviceIdType.MESH

NT = (((1,), (1,)), ((), ()))
TN = (((0,), (0,)), ((), ()))


def _pick(n, pref, mult=LANES):
    if n <= pref:
        return n
    t = (pref // mult) * mult
    while t >= mult:
        if n % t == 0:
            return t
        t -= mult
    raise ValueError((n, pref, mult))


def _cp(*sem):
    return pltpu.CompilerParams(dimension_semantics=sem, vmem_limit_bytes=VMEM_LIMIT)


def _dot(a, b):
    return jnp.dot(a, b, preferred_element_type=F32)


def _dg(a, b, dims):
    return lax.dot_general(a, b, dims, preferred_element_type=F32)


def _split3(x):
    hi = x.astype(BF)
    r1 = x - hi.astype(F32)
    mid = r1.astype(BF)
    lo = (r1 - mid.astype(F32)).astype(BF)
    return hi, mid, lo


def _tri_dot(tri, x):
    hi, mid, lo = _split3(x)
    return _dot(tri, hi) + _dot(tri, mid) + _dot(tri, lo)


def _dg3(a, b, dims):
    ah, bh = a.astype(BF), b.astype(BF)
    al, bl = (a - ah.astype(F32)).astype(BF), (b - bh.astype(F32)).astype(BF)
    return _dg(ah, bh, dims) + _dg(ah, bl, dims) + _dg(al, bh, dims)


NN = (((1,), (0,)), ((), ()))


def _sigmoid(x):
    return jax.nn.sigmoid(x)


def _ln_matmul(x, nw, sc, sh, w, *, relu2, name):
    S, D = x.shape
    N = w.shape[1]
    tm, tn = _pick(S, 1024, 16), _pick(N, 1024)

    def body(x_ref, nw_ref, sc_ref, sh_ref, w_ref, *rest):
        outs, hs = rest[:-1], rest[-1]
        h_ref = outs[-1]

        @pl.when(pl.program_id(1) == 0)
        def _():
            xv = x_ref[...]
            r = lax.rsqrt(jnp.mean(xv * xv, axis=-1, keepdims=True) + EPS)
            hb = ((xv * r * nw_ref[...]) * (1.0 + sc_ref[...]) + sh_ref[...]).astype(BF)
            hs[...] = hb
            h_ref[...] = hb

        z = _dot(hs[...], w_ref[...])
        if relu2:
            a = jnp.maximum(z, 0.0)
            outs[0][...] = a.astype(BF)
            outs[1][...] = (a * a).astype(BF)
        else:
            outs[0][...] = z

    vec = pl.BlockSpec((1, D), lambda i, j: (0, 0))
    tile = pl.BlockSpec((tm, tn), lambda i, j: (i, j))
    if relu2:
        out_shape = [jax.ShapeDtypeStruct((S, N), BF), jax.ShapeDtypeStruct((S, N), BF)]
        out_specs = [tile, tile]
    else:
        out_shape = [jax.ShapeDtypeStruct((S, N), F32)]
        out_specs = [tile]
    out_shape.append(jax.ShapeDtypeStruct((S, D), BF))
    out_specs.append(pl.BlockSpec((tm, D), lambda i, j: (i, 0)))
    return pl.pallas_call(
        body, name=name, grid=(S // tm, N // tn),
        in_specs=[pl.BlockSpec((tm, D), lambda i, j: (i, 0)), vec, vec, vec,
                  pl.BlockSpec((D, tn), lambda i, j: (0, j))],
        out_specs=out_specs, out_shape=out_shape,
        scratch_shapes=[pltpu.VMEM((tm, D), BF)],
        compiler_params=_cp("parallel", "arbitrary"),
    )(x, nw, sc, sh, w)


def _matmul_resid(a, w, x, gate, *, name):
    S, K = a.shape
    D = w.shape[1]
    big = 1024 if K <= 1024 else 512
    tm, tn = _pick(S, big, 16), _pick(D, big)

    def body(a_ref, w_ref, x_ref, g_ref, o_ref, y_ref):
        y = _dot(a_ref[...], w_ref[...])
        y_ref[...] = y.astype(BF)
        o_ref[...] = x_ref[...] + g_ref[...] * y

    tile = pl.BlockSpec((tm, tn), lambda i, j: (i, j))
    return pl.pallas_call(
        body, name=name, grid=(S // tm, D // tn),
        in_specs=[pl.BlockSpec((tm, K), lambda i, j: (i, 0)), pl.BlockSpec((K, tn), lambda i, j: (0, j)),
                  tile, pl.BlockSpec((1, tn), lambda i, j: (0, j))],
        out_specs=[tile, tile],
        out_shape=[jax.ShapeDtypeStruct((S, D), F32), jax.ShapeDtypeStruct((S, D), BF)],
        compiler_params=_cp("parallel", "arbitrary"),
    )(a, w, x, gate)


def _gate_matmul_nt(dx, gate, y, w, act, *, name):
    S, D = dx.shape
    K = w.shape[0]
    tm, tn = _pick(S, 1024, 16), _pick(K, 1024)
    fused = act is not None

    def body(dx_ref, g_ref, y_ref, w_ref, *rest):
        if fused:
            act_ref, da_ref, dm_ref, dg_ref, ms = rest
        else:
            da_ref, dm_ref, dg_ref, ms = rest
        i, j = pl.program_id(0), pl.program_id(1)

        @pl.when((i == 0) & (j == 0))
        def _():
            dg_ref[...] = jnp.zeros_like(dg_ref)

        @pl.when(j == 0)
        def _():
            dxv = dx_ref[...]
            dmb = (dxv * g_ref[...]).astype(BF)
            ms[...] = dmb
            dm_ref[...] = dmb
            dg_ref[...] += jnp.sum(dxv * y_ref[...].astype(F32), axis=0, keepdims=True)

        da = _dg(ms[...], w_ref[...], NT)
        if fused:
            da_ref[...] = (da * (2.0 * act_ref[...].astype(F32))).astype(BF)
        else:
            da_ref[...] = da

    row = pl.BlockSpec((tm, D), lambda i, j: (i, 0))
    vec = pl.BlockSpec((1, D), lambda i, j: (0, 0))
    tile = pl.BlockSpec((tm, tn), lambda i, j: (i, j))
    in_specs = [row, vec, row, pl.BlockSpec((tn, D), lambda i, j: (j, 0))]
    args = [dx, gate, y, w]
    if fused:
        in_specs.append(tile)
        args.append(act)
    return pl.pallas_call(
        body, name=name, grid=(S // tm, K // tn),
        in_specs=in_specs, out_specs=[tile, row, vec],
        out_shape=[jax.ShapeDtypeStruct((S, K), BF if fused else F32), jax.ShapeDtypeStruct((S, D), BF),
                   jax.ShapeDtypeStruct((1, D), F32)],
        scratch_shapes=[pltpu.VMEM((tm, D), BF)],
        compiler_params=_cp("arbitrary", "arbitrary"),
    )(*args)


def _matmul_tn(a, b, *, name):
    S, Ka = a.shape
    P, _, Db = b.shape
    tk, tn, ts = _pick(Ka, 1024), _pick(Db, 1024), _pick(S, 1024, 16)
    npb = Db // tn

    def body(a_ref, b_ref, o_ref, acc):
        s = pl.program_id(2)

        @pl.when(s == 0)
        def _():
            acc[...] = jnp.zeros_like(acc)

        acc[...] += _dg(a_ref[...], b_ref[...], TN)

        @pl.when(s == pl.num_programs(2) - 1)
        def _():
            o_ref[...] = acc[...]

    return pl.pallas_call(
        body, name=name, grid=(Ka // tk, P * npb, S // ts),
        in_specs=[pl.BlockSpec((ts, tk), lambda i, j, s: (s, i)),
                  pl.BlockSpec((None, ts, tn), lambda i, j, s: (j // npb, s, j % npb))],
        out_specs=pl.BlockSpec((tk, tn), lambda i, j, s: (i, j)),
        out_shape=jax.ShapeDtypeStruct((Ka, P * Db), F32),
        scratch_shapes=[pltpu.VMEM((tk, tn), F32)],
        compiler_params=_cp("parallel", "parallel", "arbitrary"),
    )(a, b)


def _matmul_nt_lnbwd(g, w, x, nw, sc, dx_out, *, name):
    P, S, Dg = g.shape
    D = x.shape[1]
    tm, tk = _pick(S, 1024, 16), _pick(Dg, 1024)
    npb = Dg // tk
    nk = P * npb

    def body(g_ref, w_ref, x_ref, nw_ref, sc_ref, dxo_ref, dx_ref, dsc_ref, dsh_ref, dnw_ref, acc):
        i, k = pl.program_id(0), pl.program_id(1)

        @pl.when((i == 0) & (k == 0))
        def _():
            dsc_ref[...] = jnp.zeros_like(dsc_ref)
            dsh_ref[...] = jnp.zeros_like(dsh_ref)
            dnw_ref[...] = jnp.zeros_like(dnw_ref)

        @pl.when(k == 0)
        def _():
            acc[...] = jnp.zeros_like(acc)

        acc[...] += _dg(g_ref[...], w_ref[...], NT)

        @pl.when(k == nk - 1)
        def _():
            dh = acc[...]
            xv = x_ref[...]
            nwv = nw_ref[...]
            r = lax.rsqrt(jnp.mean(xv * xv, axis=-1, keepdims=True) + EPS)
            xr = xv * r
            dn = dh * (1.0 + sc_ref[...])
            dsc_ref[...] += jnp.sum(dh * (xr * nwv), axis=0, keepdims=True)
            dsh_ref[...] += jnp.sum(dh, axis=0, keepdims=True)
            dnw_ref[...] += jnp.sum(dn * xr, axis=0, keepdims=True)
            u = dn * nwv
            dx_ref[...] = dxo_ref[...] + r * (u - xr * jnp.mean(u * xr, axis=-1, keepdims=True))

    row = pl.BlockSpec((tm, D), lambda i, k: (i, 0))
    vec = pl.BlockSpec((1, D), lambda i, k: (0, 0))
    return pl.pallas_call(
        body, name=name, grid=(S // tm, nk),
        in_specs=[pl.BlockSpec((None, tm, tk), lambda i, k: (k // npb, i, k % npb)),
                  pl.BlockSpec((D, tk), lambda i, k: (0, k)), row, vec, vec, row],
        out_specs=[row, vec, vec, vec],
        out_shape=[jax.ShapeDtypeStruct((S, D), F32)] + [jax.ShapeDtypeStruct((1, D), F32)] * 3,
        scratch_shapes=[pltpu.VMEM((tm, D), F32)],
        compiler_params=_cp("arbitrary", "arbitrary"),
    )(g, w, x, nw, sc, dx_out)


def _loss_kernel(x, fw, tgt, *, name):
    S, D = x.shape
    tm = _pick(S, 512, 8)

    def body(x_ref, fw_ref, t_ref, l_ref, dx_ref, dfw_ref):
        @pl.when(pl.program_id(0) == 0)
        def _():
            l_ref[...] = jnp.zeros_like(l_ref)
            dfw_ref[...] = jnp.zeros_like(dfw_ref)

        xv = x_ref[...]
        fwv = fw_ref[...]
        r = lax.rsqrt(jnp.mean(xv * xv, axis=-1, keepdims=True) + EPS)
        xr = xv * r
        err = xr * fwv - t_ref[...]
        per_tok = jnp.mean(err * err, axis=-1, keepdims=True)
        l_ref[...] += 0.5 * jnp.sum(per_tok, axis=0, keepdims=True)
        dy = err * (1.0 / D)
        dfw_ref[...] += jnp.sum(dy * xr, axis=0, keepdims=True)
        u = dy * fwv
        dx_ref[...] = r * (u - xr * jnp.mean(u * xr, axis=-1, keepdims=True))

    row = pl.BlockSpec((tm, D), lambda i: (i, 0))
    vec = pl.BlockSpec((1, D), lambda i: (0, 0))
    return pl.pallas_call(
        body, name=name, grid=(S // tm,),
        in_specs=[row, vec, row],
        out_specs=[pl.BlockSpec((1, LANES), lambda i: (0, 0)), row, vec],
        out_shape=[jax.ShapeDtypeStruct((1, LANES), F32), jax.ShapeDtypeStruct((S, D), F32),
                   jax.ShapeDtypeStruct((1, D), F32)],
        compiler_params=_cp("arbitrary"),
    )(x, fw, tgt)


def _hg_lower_bound(lb3):
    mx = jnp.max(lb3, axis=0, keepdims=True)
    e = jnp.exp(lb3 - mx)
    p = e / jnp.sum(e, axis=0, keepdims=True)
    return p[0:1, :], p


def _hg_chunk_common(qr, fz, lbv):
    sq = _sigmoid(qr)
    q = qr * sq
    sig = _sigmoid(fz)
    f = lbv + (1.0 - lbv) * sig
    k = (1.0 - lbv) * (1.0 - sig)
    return q, sq, sig, f, k, jnp.log(f)


def _row_of(x, rows, r):
    return jnp.sum(jnp.where(rows == r, x, 0.0), axis=0, keepdims=True)


def _hg_fwd(proj, hg_lb, gn, *, name):
    S = proj.shape[0]
    D = proj.shape[1] // 4
    H = D // LANES
    HB = min(HG_HEADS_PER_STEP, H)
    W = HB * LANES
    C = HG_CHUNK
    T = _pick(S, HG_TOKENS_PER_STEP, C)
    nch, nb = T // C, S // T

    def body(q_ref, fz_ref, v_ref, g_ref, lb_ref, gn_ref, y_ref, o_ref, sts_ref, st):
        @pl.when(pl.program_id(1) == 0)
        def _():
            st[...] = jnp.zeros_like(st)

        lb_all, _ = _hg_lower_bound(lb_ref[...])
        gnv = gn_ref[...]
        ri = lax.broadcasted_iota(jnp.int32, (C, C), 0)
        ci_ = lax.broadcasted_iota(jnp.int32, (C, C), 1)
        low = ri >= ci_
        tri = jnp.where(low, 1.0, 0.0).astype(BF)
        rows = lax.broadcasted_iota(jnp.int32, (C, LANES), 0)

        def chunk(ci, carry):
            sl = pl.ds(pl.multiple_of(ci * C, C), C)
            for hh in range(HB):
                ls = slice(hh * LANES, (hh + 1) * LANES)
                q, _, _, _, k, logf = _hg_chunk_common(q_ref[sl, ls], fz_ref[sl, ls], lb_all[:, ls])
                vv = v_ref[sl, ls]
                gg = g_ref[sl, ls]
                G = _tri_dot(tri, logf)
                Gm = _row_of(G, rows, C // 2 - 1)
                Gl = _row_of(G, rows, C - 1)
                qt = q * jnp.exp(G - Gm)
                kt = k * jnp.exp(Gm - G)
                A = jnp.where(low, _dg3(qt, kt, NT), 0.0)
                Sv = st[hh]
                sts_ref[hh, ci] = Sv
                o = _dg3(A, vv, NN) + _dg3(q * jnp.exp(G), Sv, NT)
                st[hh] = Sv * jnp.exp(Gl) + _dg3(vv, k * jnp.exp(Gl - G), TN)
                r = lax.rsqrt(jnp.mean(o * o, axis=-1, keepdims=True) + EPS)
                y_ref[sl, ls] = ((o * r * gnv) * (gg * _sigmoid(gg))).astype(BF)
                o_ref[sl, ls] = o
            return carry

        lax.fori_loop(0, nch, chunk, 0)

    ng = H // HB

    def part(p):
        return pl.BlockSpec((T, W), lambda h, n: (n, p * ng + h))

    blk = pl.BlockSpec((T, W), lambda h, n: (n, h))
    return pl.pallas_call(
        body, name=name, grid=(ng, nb),
        in_specs=[part(0), part(1), part(2), part(3),
                  pl.BlockSpec((3, W), lambda h, n: (0, h)), pl.BlockSpec((1, LANES), lambda h, n: (0, 0))],
        out_specs=[blk, blk, pl.BlockSpec((HB, nch, LANES, LANES), lambda h, n: (h, n, 0, 0))],
        out_shape=[jax.ShapeDtypeStruct((S, D), BF), jax.ShapeDtypeStruct((S, D), F32),
                   jax.ShapeDtypeStruct((H, S // C, LANES, LANES), F32)],
        scratch_shapes=[pltpu.VMEM((HB, LANES, LANES), F32)],
        compiler_params=_cp("parallel", "arbitrary"),
    )(proj, proj, proj, proj, hg_lb, gn)


def _hg_bwd(proj, hg_lb, gn, o_all, states, dy, *, name):
    S = proj.shape[0]
    D = proj.shape[1] // 4
    H = D // LANES
    HB = min(HG_HEADS_PER_STEP, H)
    W = HB * LANES
    C = HG_CHUNK
    T = _pick(S, HG_TOKENS_PER_STEP, C)
    nch, nb = T // C, S // T

    def body(q_ref, fz_ref, v_ref, g_ref, lb_ref, gn_ref, o_ref, sts_ref, dy_ref,
             dp_ref, dlb_ref, dgn_ref, dst, dlb_acc):
        n = pl.program_id(1)

        @pl.when(n == 0)
        def _():
            dst[...] = jnp.zeros_like(dst)
            dlb_acc[...] = jnp.zeros_like(dlb_acc)
            dgn_ref[...] = jnp.zeros_like(dgn_ref)

        lb_all, p3 = _hg_lower_bound(lb_ref[...])
        gnv = gn_ref[...]
        ri = lax.broadcasted_iota(jnp.int32, (C, C), 0)
        ci_ = lax.broadcasted_iota(jnp.int32, (C, C), 1)
        low = ri >= ci_
        tri = jnp.where(low, 1.0, 0.0).astype(BF)
        triu = jnp.where(ri <= ci_, 1.0, 0.0).astype(BF)
        rows = lax.broadcasted_iota(jnp.int32, (C, LANES), 0)

        def chunk(cj, carry):
            ci = nch - 1 - cj
            sl = pl.ds(pl.multiple_of(ci * C, C), C)
            for hh in range(HB):
                ls = slice(hh * LANES, (hh + 1) * LANES)
                lbv = lb_all[:, ls]
                qr = q_ref[sl, ls]
                q, sq, sig, f, k, logf = _hg_chunk_common(qr, fz_ref[sl, ls], lbv)
                vv = v_ref[sl, ls]
                gg = g_ref[sl, ls]
                o = o_ref[sl, ls]
                dyv = dy_ref[sl, ls]
                G = _tri_dot(tri, logf)
                Gm = _row_of(G, rows, C // 2 - 1)
                Gl = _row_of(G, rows, C - 1)
                eG, e_qm, e_km, e_lk, eGl = jnp.exp(G), jnp.exp(G - Gm), jnp.exp(Gm - G), jnp.exp(Gl - G), jnp.exp(Gl)
                qt = q * e_qm
                kt = k * e_km
                A = jnp.where(low, _dg3(qt, kt, NT), 0.0)
                sg = _sigmoid(gg)
                r = lax.rsqrt(jnp.mean(o * o, axis=-1, keepdims=True) + EPS)
                on = o * r
                d_onw = dyv * (gg * sg)
                dgn_ref[hh] += jnp.sum(d_onw * on, axis=0, keepdims=True)
                dgg = dyv * (on * gnv) * (sg * (1.0 + gg * (1.0 - sg)))
                u = d_onw * gnv
                do = r * (u - on * jnp.mean(u * on, axis=-1, keepdims=True))
                Sv = sts_ref[hh, ci]
                dSv = dst[hh]
                dA = jnp.where(low, _dg3(do, vv, NT), 0.0)
                kdec = k * e_lk
                dv = _dg3(A, do, TN) + _dg3(kdec, dSv, NT)
                dq = _dg3(dA, kt, NN) * e_qm + eG * _dg3(do, Sv, NN)
                dk = _dg3(dA, qt, TN) * e_km + e_lk * _dg3(vv, dSv, NN)
                s_end = Sv * eGl + _dg3(vv, kdec, TN)
                dgl = jnp.sum(dSv * s_end, axis=0, keepdims=True)
                dG = q * dq - k * dk + jnp.where(rows == C - 1, dgl, 0.0)
                dlogf = _tri_dot(triu, dG) - f * dk
                dst[hh] = dSv * eGl + _dg3(do, q * eG, TN)
                dlf_f = dlogf / f
                dlb_acc[:, ls] += jnp.sum(dlf_f * (1.0 - sig), axis=0, keepdims=True)
                dp_ref[0, sl, ls] = (dq * (sq * (1.0 + qr * (1.0 - sq)))).astype(BF)
                dp_ref[1, sl, ls] = (dlf_f * (1.0 - lbv) * sig * (1.0 - sig)).astype(BF)
                dp_ref[2, sl, ls] = dv.astype(BF)
                dp_ref[3, sl, ls] = dgg.astype(BF)
            return carry

        lax.fori_loop(0, nch, chunk, 0)
        sel = jnp.where(lax.broadcasted_iota(jnp.int32, (3, W), 0) == 0, 1.0, 0.0)
        dlb_ref[...] = lb_all * (sel - p3) * dlb_acc[...]

    ng = H // HB

    def part(p):
        return pl.BlockSpec((T, W), lambda h, n: (nb - 1 - n, p * ng + h))

    blk = pl.BlockSpec((T, W), lambda h, n: (nb - 1 - n, h))
    return pl.pallas_call(
        body, name=name, grid=(ng, nb),
        in_specs=[part(0), part(1), part(2), part(3),
                  pl.BlockSpec((3, W), lambda h, n: (0, h)), pl.BlockSpec((1, LANES), lambda h, n: (0, 0)),
                  blk, pl.BlockSpec((HB, nch, LANES, LANES), lambda h, n: (h, nb - 1 - n, 0, 0)), blk],
        out_specs=[pl.BlockSpec((4, T, W), lambda h, n: (0, nb - 1 - n, h)),
                   pl.BlockSpec((3, W), lambda h, n: (0, h)),
                   pl.BlockSpec((HB, 1, LANES), lambda h, n: (h, 0, 0))],
        out_shape=[jax.ShapeDtypeStruct((4, S, D), BF), jax.ShapeDtypeStruct((3, D), F32),
                   jax.ShapeDtypeStruct((H, 1, LANES), F32)],
        scratch_shapes=[pltpu.VMEM((HB, LANES, LANES), F32), pltpu.VMEM((1, W), F32)],
        compiler_params=_cp("parallel", "arbitrary"),
    )(proj, proj, proj, proj, hg_lb, gn, o_all, states, dy)


def _log_sigmoid(u):
    return jnp.minimum(u, 0.0) - jnp.log(1.0 + jnp.exp(-jnp.abs(u)))


def _lane_put(base, lane, first, pieces):
    for n, p in enumerate(pieces):
        base = jnp.where(lane == first + n, p, base)
    return base


def _fox_cumsum(proj, bf_pad, *, name):
    S = proj.shape[0]
    D = proj.shape[1] // 5
    T = _pick(S, 256, 8)

    def body(fz_ref, b_ref, f_ref, carry):
        @pl.when(pl.program_id(0) == 0)
        def _():
            carry[...] = jnp.zeros_like(carry)

        logf = _log_sigmoid(fz_ref[...] + b_ref[...])
        tri = jnp.where(lax.broadcasted_iota(jnp.int32, (T, T), 0) >= lax.broadcasted_iota(jnp.int32, (T, T), 1),
                        1.0, 0.0).astype(BF)
        fv = _tri_dot(tri, logf) + carry[...]
        f_ref[...] = fv
        carry[...] = _row_of(fv, lax.broadcasted_iota(jnp.int32, (T, LANES), 0), T - 1)

    return pl.pallas_call(
        body, name=name, grid=(S // T,),
        in_specs=[pl.BlockSpec((T, LANES), lambda i: (i, 4 * D // LANES)), pl.BlockSpec((1, LANES), lambda i: (0, 0))],
        out_specs=pl.BlockSpec((T, LANES), lambda i: (i, 0)),
        out_shape=jax.ShapeDtypeStruct((S, LANES), F32),
        scratch_shapes=[pltpu.VMEM((1, LANES), F32)],
        compiler_params=_cp("arbitrary"),
    )(proj, bf_pad)


def _pair_stats(sq, lo):
    s_lo = jnp.sum(jnp.where(lo, sq, 0.0), axis=-1, keepdims=True)
    s_hi = jnp.sum(jnp.where(lo, 0.0, sq), axis=-1, keepdims=True)
    return jnp.where(lo, s_lo, s_hi) * (1.0 / FOX_DH)


def _fox_prep(proj, fcum, qw2, kw2, *, name):
    S = proj.shape[0]
    D = proj.shape[1] // 5
    HP = D // LANES
    T = _pick(S, 512, 16)

    def body(q_ref, k_ref, v_ref, f_ref, qw_ref, kw_ref, qa_ref, ka_ref, va_ref):
        hp = pl.program_id(1)
        lane = lax.broadcasted_iota(jnp.int32, (T, LANES), 1)
        lo = lane < FOX_DH
        qv, kv, vv, fv = q_ref[...], k_ref[...], v_ref[...], f_ref[...]
        qn = qv * lax.rsqrt(_pair_stats(qv * qv, lo) + EPS) * qw_ref[...] * (0.125 * LOG2E)
        kn = kv * lax.rsqrt(_pair_stats(kv * kv, lo) + EPS) * kw_ref[...]
        ones_q = jnp.where((lane >= 67) & (lane <= 69), 1.0, 0.0)
        ones_k = jnp.where(((lane >= 64) & (lane <= 66)) | ((lane >= 70) & (lane <= 72)), 1.0, 0.0)
        ones_v = jnp.where((lane >= 64) & (lane <= 66), 1.0, 0.0)
        for hh in range(2):
            fh = jnp.sum(jnp.where(lane == 2 * hp + hh, fv, 0.0), axis=-1, keepdims=True) * LOG2E
            pieces = [p.astype(F32) for p in _split3(fh)]

            def half(x):
                return jnp.where(lo, x if hh == 0 else pltpu.roll(x, FOX_DH, 1), 0.0)

            qa_ref[hh] = _lane_put(half(qn) + ones_q, lane, 64, pieces).astype(BF)
            ka_ref[hh] = _lane_put(half(kn) + ones_k, lane, 67, [-p for p in pieces]).astype(BF)
            va_ref[hh] = (half(vv) + ones_v).astype(BF)

    def part(p):
        return pl.BlockSpec((T, LANES), lambda i, hp: (i, p * HP + hp))

    vec = pl.BlockSpec((1, LANES), lambda i, hp: (0, 0))
    aug = pl.BlockSpec((2, T, LANES), lambda i, hp: (hp, i, 0))
    return pl.pallas_call(
        body, name=name, grid=(S // T, HP),
        in_specs=[part(0), part(1), part(2), pl.BlockSpec((T, LANES), lambda i, hp: (i, 0)), vec, vec],
        out_specs=[aug, aug, aug],
        out_shape=[jax.ShapeDtypeStruct((2 * HP, S, LANES), BF)] * 3,
        compiler_params=_cp("parallel", "arbitrary"),
    )(proj, proj, proj, fcum, qw2, kw2)


def _fox_block(S):
    return _pick(S, 256, 16)


def _fox_skip_bounds(fcum, qn_w, kn_w, nheads):
    S = fcum.shape[0]
    B = _fox_block(S)
    qk = 8.0 * LOG2E * 1.02 * jnp.max(jnp.abs(qn_w)) * jnp.max(jnp.abs(kn_w))
    thresh = -(2.0 * qk + 160.0)
    f2 = fcum[:, :nheads] * LOG2E
    first, last = f2[0::B], f2[B - 1::B]
    nb = S // B
    blk = jnp.arange(nb)
    dead = (first[0::2, None, :] - last[None, :, :]) < thresh
    jmin = jnp.sum(dead & (blk[None, :, None] < 2 * jnp.arange(nb // 2)[:, None, None]), axis=1)
    live = (first[:, None, :] - last[None, :, :]) >= thresh
    imax = blk[:, None] + jnp.sum(live & (blk[:, None, None] > blk[None, :, None]), axis=0)
    return jmin.T.astype(jnp.int32), imax.T.astype(jnp.int32)


def _fox_fwd(jmin, qa, ka, va, proj, *, name):
    H, S, _ = qa.shape
    HP = H // 2
    D = HP * LANES
    B = _fox_block(S)
    BQ = 2 * B
    nq = S // BQ

    def body(jmin_ref, q_ref, k_ref, v_ref, g_ref, y_ref, o_ref, q2_ref):
        hp, i = pl.program_id(0), pl.program_id(1)
        lane = lax.broadcasted_iota(jnp.int32, (B, LANES), 1)
        lo = lane < FOX_DH
        causal = lax.broadcasted_iota(jnp.int32, (B, B), 1) <= lax.broadcasted_iota(jnp.int32, (B, B), 0)
        m0, acc0 = jnp.full((B, 1), -jnp.inf, F32), jnp.zeros((B, LANES), F32)
        outs = []
        for hh in range(2):
            qbs = [q_ref[hh, 0:B, :], q_ref[hh, B:BQ, :]]

            def block(c, j, m, acc, masked=False):
                sl = pl.ds(pl.multiple_of(j * B, B), B)
                s = _dg(qbs[c], k_ref[hh, sl, :], NT)
                if masked:
                    s = jnp.where(causal, s, -jnp.inf)
                m_new = jnp.maximum(m, jnp.max(s, axis=-1, keepdims=True))
                p = jnp.exp2(s - m_new)
                ph = p.astype(BF)
                pl_ = (p - ph.astype(F32)).astype(BF)
                vb = v_ref[hh, sl, :]
                pv = _dot(jnp.concatenate([ph, pl_], axis=1), jnp.concatenate([vb, vb], axis=0))
                return m_new, acc * jnp.exp2(m - m_new) + pv

            def both(j, carry):
                ma, aa, mb, ab = carry
                return block(0, j, ma, aa) + block(1, j, mb, ab)

            ma, aa, mb, ab = lax.fori_loop(jmin_ref[2 * hp + hh, i], 2 * i, both, (m0, acc0, m0, acc0))
            ma, aa = block(0, 2 * i, ma, aa, masked=True)
            mb, ab = block(1, 2 * i, mb, ab)
            mb, ab = block(1, 2 * i + 1, mb, ab, masked=True)
            for c, (m, acc) in enumerate(((ma, aa), (mb, ab))):
                l = jnp.sum(jnp.where(lane == FOX_DH, acc, 0.0), axis=-1, keepdims=True)
                outs.append(acc / l)
                neg_lse = [-(p.astype(F32)) for p in _split3(m + jnp.log2(l))]
                q2_ref[hh, c * B:(c + 1) * B, :] = _lane_put(qbs[c].astype(F32), lane, 70, neg_lse).astype(BF)
        for c in range(2):
            rows = slice(c * B, (c + 1) * B)
            o = jnp.where(lo, outs[c], pltpu.roll(outs[2 + c], FOX_DH, 1))
            o_ref[rows, :] = o
            y_ref[rows, :] = (o * _sigmoid(g_ref[rows, :])).astype(BF)

    blk = pl.BlockSpec((BQ, LANES), lambda hp, i, jm: (i, hp))
    qblk = pl.BlockSpec((2, BQ, LANES), lambda hp, i, jm: (hp, i, 0))
    full = pl.BlockSpec((2, S, LANES), lambda hp, i, jm: (hp, 0, 0))
    return pl.pallas_call(
        body, name=name,
        grid_spec=pltpu.PrefetchScalarGridSpec(
            num_scalar_prefetch=1, grid=(HP, nq),
            in_specs=[qblk, full, full, pl.BlockSpec((BQ, LANES), lambda hp, i, jm: (i, 3 * HP + hp))],
            out_specs=[blk, blk, qblk]),
        out_shape=[jax.ShapeDtypeStruct((S, D), BF), jax.ShapeDtypeStruct((S, D), F32),
                   jax.ShapeDtypeStruct((H, S, LANES), BF)],
        compiler_params=_cp("parallel", "arbitrary"),
    )(jmin, qa, ka, va, proj)


def _fox_bwd_prep(dy, o, proj, *, name):
    S, D = dy.shape
    HP = D // LANES
    T = _pick(S, 512, 16)

    def body(dy_ref, o_ref, g_ref, da_ref):
        lane = lax.broadcasted_iota(jnp.int32, (T, LANES), 1)
        lo = lane < FOX_DH
        do = (dy_ref[...] * _sigmoid(g_ref[...])).astype(BF).astype(F32)
        prod = do * o_ref[...]
        d_lo = jnp.sum(jnp.where(lo, prod, 0.0), axis=-1, keepdims=True)
        d_hi = jnp.sum(jnp.where(lo, 0.0, prod), axis=-1, keepdims=True)
        for hh, delta in enumerate((d_lo, d_hi)):
            base = jnp.where(lo, do if hh == 0 else pltpu.roll(do, FOX_DH, 1), 0.0)
            da_ref[hh] = _lane_put(base, lane, 64, [-(p.astype(F32)) for p in _split3(delta)]).astype(BF)

    blk = pl.BlockSpec((T, LANES), lambda i, hp: (i, hp))
    return pl.pallas_call(
        body, name=name, grid=(S // T, HP),
        in_specs=[blk, blk, pl.BlockSpec((T, LANES), lambda i, hp: (i, 3 * HP + hp))],
        out_specs=pl.BlockSpec((2, T, LANES), lambda i, hp: (hp, i, 0)),
        out_shape=jax.ShapeDtypeStruct((2 * HP, S, LANES), BF),
        compiler_params=_cp("parallel", "arbitrary"),
    )(dy, o, proj)


def _fox_bwd(imax, q2, ka, va, doa, *, name):
    H, S, _ = q2.shape
    B = _fox_block(S)
    nb = S // B

    def body(imax_ref, q_ref, do_ref, k_ref, v_ref, dq_ref, dk_ref, dv_ref, cs_ref):
        j = pl.program_id(1)
        end = imax_ref[pl.program_id(0), j] + 1

        @pl.when(j == 0)
        def _():
            dq_ref[...] = jnp.zeros_like(dq_ref)

        kb, vb = k_ref[...], v_ref[...]
        causal = lax.broadcasted_iota(jnp.int32, (B, B), 1) <= lax.broadcasted_iota(jnp.int32, (B, B), 0)

        def step(i, carry, masked=False):
            dk_acc, dv_acc, cs_acc = carry
            sl = pl.ds(pl.multiple_of(i * B, B), B)
            qb, dob = q_ref[sl, :], do_ref[sl, :]
            s = _dg(qb, kb, NT)
            if masked:
                s = jnp.where(causal, s, -jnp.inf)
            p = jnp.exp2(s)
            ds = p * _dg(dob, vb, NT)
            dsb = ds.astype(BF)
            cs_acc = cs_acc + jnp.sum(ds.reshape(B // 8, 8, B), axis=0)
            dv_acc = dv_acc + _dg(p.astype(BF), dob, TN)
            dk_acc = dk_acc + _dg(dsb, qb, TN)
            dq_ref[sl, :] += _dot(dsb, kb)
            return dk_acc, dv_acc, cs_acc

        zero = jnp.zeros((B, LANES), F32)
        carry = step(j, (zero, zero, jnp.zeros((8, B), F32)), masked=True)
        U = FOX_BWD_UNROLL
        ngroup = (end - 1 - j) // U

        def group(ii, c):
            for u in range(U):
                c = step(j + 1 + U * ii + u, c)
            return c

        carry = lax.fori_loop(0, ngroup, group, carry)
        dk_acc, dv_acc, cs_acc = lax.fori_loop(j + 1 + U * ngroup, end, step, carry)
        dk_ref[...] = dk_acc
        dv_ref[...] = dv_acc
        cs_ref[...] = jnp.sum(cs_acc, axis=0, keepdims=True)

    full = pl.BlockSpec((None, S, LANES), lambda h, j, im: (h, 0, 0))
    blk = pl.BlockSpec((None, B, LANES), lambda h, j, im: (h, j, 0))
    return pl.pallas_call(
        body, name=name,
        grid_spec=pltpu.PrefetchScalarGridSpec(
            num_scalar_prefetch=1, grid=(H, nb),
            in_specs=[full, full, blk, blk],
            out_specs=[full, blk, blk, pl.BlockSpec((None, 1, B), lambda h, j, im: (h, 0, j))]),
        out_shape=[jax.ShapeDtypeStruct((H, S, LANES), F32)] * 3 + [jax.ShapeDtypeStruct((H, 1, S), F32)],
        compiler_params=_cp("parallel", "arbitrary"),
    )(imax, q2, doa, ka, va)


def _fox_bwd_post(dqa, dka, dva, proj, dy, o, qw2, kw2, *, name):
    S, D = dy.shape
    HP = D // LANES
    T = _pick(S, 512, 16)

    def body(dq_ref, dk_ref, dv_ref, q_ref, k_ref, g_ref, dy_ref, o_ref, qw_ref, kw_ref, dp_ref, dqw_ref, dkw_ref):
        @pl.when((pl.program_id(0) == 0) & (pl.program_id(1) == 0))
        def _():
            dqw_ref[...] = jnp.zeros_like(dqw_ref)
            dkw_ref[...] = jnp.zeros_like(dkw_ref)

        lane = lax.broadcasted_iota(jnp.int32, (T, LANES), 1)
        lo = lane < FOX_DH

        def pair(ref):
            return jnp.where(lo, ref[0], pltpu.roll(ref[1], FOX_DH, 1))

        def norm_bwd(xv, w, dyn, dw_ref):
            r = lax.rsqrt(_pair_stats(xv * xv, lo) + EPS)
            xr = xv * r
            dw_ref[...] += jnp.sum(dyn * xr, axis=0, keepdims=True)
            u = dyn * w
            return r * (u - xr * _pair_stats(u * xr, lo))

        dp_ref[0] = norm_bwd(q_ref[...], qw_ref[...], pair(dq_ref) * 0.125, dqw_ref).astype(BF)
        dp_ref[1] = norm_bwd(k_ref[...], kw_ref[...], pair(dk_ref) * (1.0 / LOG2E), dkw_ref).astype(BF)
        dp_ref[2] = pair(dv_ref).astype(BF)
        sg = _sigmoid(g_ref[...])
        dp_ref[3] = (dy_ref[...] * o_ref[...] * sg * (1.0 - sg)).astype(BF)

    def part(p):
        return pl.BlockSpec((T, LANES), lambda i, hp: (i, p * HP + hp))

    aug = pl.BlockSpec((2, T, LANES), lambda i, hp: (hp, i, 0))
    blk = pl.BlockSpec((T, LANES), lambda i, hp: (i, hp))
    vec = pl.BlockSpec((1, LANES), lambda i, hp: (0, 0))
    return pl.pallas_call(
        body, name=name, grid=(S // T, HP),
        in_specs=[aug, aug, aug, part(0), part(1), part(3), blk, blk, vec, vec],
        out_specs=[pl.BlockSpec((4, T, LANES), lambda i, hp: (0, i, hp)), vec, vec],
        out_shape=[jax.ShapeDtypeStruct((5, S, D), BF), jax.ShapeDtypeStruct((1, LANES), F32),
                   jax.ShapeDtypeStruct((1, LANES), F32)],
        compiler_params=_cp("arbitrary", "arbitrary"),
    )(dqa, dka, dva, proj, proj, proj, dy, o, qw2, kw2)


def _fox_dfz(colsum, nheads, proj, bf_pad, dproj, *, name):
    S = colsum.shape[0]
    H = nheads
    D = dproj.shape[2]
    T = _pick(S, 256, 16)
    nb = S // T

    def body(cs_ref, fz_ref, b_ref, _, dp_ref, db_ref, carry):
        @pl.when(pl.program_id(0) == 0)
        def _():
            carry[...] = jnp.zeros_like(carry)
            db_ref[...] = jnp.zeros_like(db_ref)

        lane = lax.broadcasted_iota(jnp.int32, (T, LANES), 1)
        df = -cs_ref[...]
        triu = jnp.where(lax.broadcasted_iota(jnp.int32, (T, T), 0) <= lax.broadcasted_iota(jnp.int32, (T, T), 1),
                         1.0, 0.0).astype(BF)
        dlogf = _tri_dot(triu, df) + carry[...]
        carry[...] = _row_of(dlogf, lax.broadcasted_iota(jnp.int32, (T, LANES), 0), 0)
        dfz = jnp.where(lane < H, dlogf * _sigmoid(-(fz_ref[...] + b_ref[...])), 0.0)
        db_ref[...] += jnp.sum(dfz, axis=0, keepdims=True)
        dp_ref[...] = jnp.zeros_like(dp_ref)
        dp_ref[:, 0:LANES] = dfz.astype(BF)

    return pl.pallas_call(
        body, name=name, grid=(nb,),
        in_specs=[pl.BlockSpec((T, LANES), lambda i: (nb - 1 - i, 0)),
                  pl.BlockSpec((T, LANES), lambda i: (nb - 1 - i, 4 * D // LANES)),
                  pl.BlockSpec((1, LANES), lambda i: (0, 0)),
                  pl.BlockSpec(memory_space=pl.ANY)],
        out_specs=[pl.BlockSpec((None, T, D), lambda i: (4, nb - 1 - i, 0)), pl.BlockSpec((1, LANES), lambda i: (0, 0))],
        out_shape=[jax.ShapeDtypeStruct(dproj.shape, BF), jax.ShapeDtypeStruct((1, LANES), F32)],
        scratch_shapes=[pltpu.VMEM((1, LANES), F32)],
        input_output_aliases={3: 0},
        compiler_params=_cp("arbitrary"),
    )(colsum, proj, bf_pad, dproj)


def _mod_fwd(c16, w, b, *, name):
    L, D, N = w.shape
    tn = _pick(N, 512)

    def body(c_ref, w_ref, b_ref, o_ref):
        cv = c_ref[...]
        ca = (cv * _sigmoid(cv)).astype(BF)
        o_ref[...] = _dot(ca, w_ref[...].astype(BF)) + b_ref[...]

    return pl.pallas_call(
        body, name=name, grid=(L, N // tn),
        in_specs=[pl.BlockSpec((16, D), lambda l, j: (0, 0)), pl.BlockSpec((None, D, tn), lambda l, j: (l, 0, j)),
                  pl.BlockSpec((None, 1, tn), lambda l, j: (l, 0, j))],
        out_specs=pl.BlockSpec((None, 16, tn), lambda l, j: (l, 0, j)),
        out_shape=jax.ShapeDtypeStruct((L, 16, N), F32),
        compiler_params=_cp("parallel", "arbitrary"),
    )(c16, w, b)


def _mod_bwd(c16, dmod, *, name):
    L, _, N = dmod.shape
    D = c16.shape[1]
    tn = _pick(N, 512)

    def body(c_ref, d_ref, o_ref):
        cv = c_ref[...]
        ca = (cv * _sigmoid(cv)).astype(BF)
        o_ref[...] = _dg(ca, d_ref[...].astype(BF), TN)

    return pl.pallas_call(
        body, name=name, grid=(L, N // tn),
        in_specs=[pl.BlockSpec((16, D), lambda l, j: (0, 0)), pl.BlockSpec((None, 16, tn), lambda l, j: (l, 0, j))],
        out_specs=pl.BlockSpec((None, D, tn), lambda l, j: (l, 0, j)),
        out_shape=jax.ShapeDtypeStruct((L, D, N), F32),
        compiler_params=_cp("parallel", "arbitrary"),
    )(c16, dmod)


def _adamw_math(w, g, m, v):
    m = ADAM_B1 * m + (1.0 - ADAM_B1) * g
    v = ADAM_B2 * v + (1.0 - ADAM_B2) * (g * g)
    m_hat = m / (1.0 - ADAM_B1 ** ADAM_STEP)
    v_hat = v / (1.0 - ADAM_B2 ** ADAM_STEP)
    return -ADAM_LR * (m_hat / (jnp.sqrt(v_hat) + ADAM_EPS) + ADAM_WD * w), m, v


def _adamw(w, g, m, v, *, g_row0=0, name):
    R, C = w.shape
    tr = min(math.gcd(g_row0, 256) if g_row0 else 256, -(-R // 8) * 8)
    g0 = g_row0 // tr

    def body(w_ref, g_ref, m_ref, v_ref, d_ref, mo_ref, vo_ref):
        d, mn, vn = _adamw_math(w_ref[...], g_ref[...], m_ref[...], v_ref[...])
        d_ref[...] = d
        mo_ref[...] = mn
        vo_ref[...] = vn

    blk = pl.BlockSpec((tr, C), lambda i: (i, 0))
    return pl.pallas_call(
        body, name=name, grid=(pl.cdiv(R, tr),),
        in_specs=[blk, pl.BlockSpec((tr, C), lambda i: (g0 + i, 0)), blk, blk],
        out_specs=[blk, blk, blk],
        out_shape=[jax.ShapeDtypeStruct((R, C), F32)] * 3,
        compiler_params=_cp("parallel"),
    )(w, g, m, v)


def _sum_parts(parts, *, name):
    P, R, C = parts.shape

    def body(p_ref, o_ref):
        acc = p_ref[0]
        for p in range(1, P):
            acc = acc + p_ref[p]
        o_ref[...] = acc

    return pl.pallas_call(
        body, name=name, grid=(1,),
        in_specs=[pl.BlockSpec((P, R, C), lambda i: (0, 0, 0))],
        out_specs=pl.BlockSpec((R, C), lambda i: (0, 0)),
        out_shape=jax.ShapeDtypeStruct((R, C), F32),
        compiler_params=_cp("arbitrary"),
    )(parts)


def _add_halves(g4, recv, c_idx, *, name):
    _, _, Rh, C = g4.shape
    tr = _pick(Rh, 256, 16)

    def body(c_ref, a_ref, b_ref, o_ref):
        o_ref[...] = (a_ref[...] + b_ref[...].astype(F32)).astype(BF)

    return pl.pallas_call(
        body, name=name,
        grid_spec=pltpu.PrefetchScalarGridSpec(
            num_scalar_prefetch=1, grid=(4, pl.cdiv(Rh, tr)),
            in_specs=[pl.BlockSpec((None, None, tr, C), lambda j, r, c: (j, c[0], r, 0)),
                      pl.BlockSpec((None, tr, C), lambda j, r, c: (j, r, 0))],
            out_specs=pl.BlockSpec((None, tr, C), lambda j, r, c: (j, r, 0))),
        out_shape=jax.ShapeDtypeStruct((4, Rh, C), BF),
        compiler_params=_cp("parallel", "arbitrary"),
    )(c_idx, g4, recv)


def _add_four(g4, from_sibling, from_chips, pos, *, name):
    _, _, Rh, C = g4.shape
    tr = _pick(Rh, 256, 16)

    def body(p_ref, a_ref, s_ref, b_ref, o_ref):
        own = a_ref[...] + s_ref[...].astype(F32)
        o_ref[...] = ((own + b_ref[0].astype(F32)) + b_ref[1].astype(F32)) + b_ref[2].astype(F32)

    return pl.pallas_call(
        body, name=name,
        grid_spec=pltpu.PrefetchScalarGridSpec(
            num_scalar_prefetch=1, grid=(pl.cdiv(Rh, tr),),
            in_specs=[pl.BlockSpec((None, None, tr, C), lambda r, p: (p[0], p[1], r, 0)),
                      pl.BlockSpec((None, tr, C), lambda r, p: (p[0], r, 0)),
                      pl.BlockSpec((3, tr, C), lambda r, p: (0, r, 0))],
            out_specs=pl.BlockSpec((tr, C), lambda r, p: (r, 0))),
        out_shape=jax.ShapeDtypeStruct((Rh, C), F32),
        compiler_params=_cp("arbitrary"),
    )(pos, g4, from_sibling, from_chips)


HBM = pl.BlockSpec(memory_space=pltpu.HBM)


def _mesh_pos():
    return lax.axis_index("x"), lax.axis_index("y"), lax.axis_index("c")


def _other_chips(x, y):
    return [(1 - x, y), (x, 1 - y), (1 - x, 1 - y)]


def _allgather_small(xs, *, name):
    m_per, n = xs.shape

    def body(x_ref, out_ref, send_sems, recv_sems, local_sem):
        x, y, c = _mesh_pos()
        me, sibling = (x, y, c), (x, y, 1 - c)
        chips = _other_chips(x, y)

        def rows(px, py, pc):
            return out_ref.at[pl.ds((4 * px + 2 * py + pc) * m_per, m_per), :]

        def copy(k, block, to, src=None):
            return pltpu.make_async_remote_copy(
                src_ref=rows(*block) if src is None else src, dst_ref=rows(*block),
                send_sem=send_sems.at[k], recv_sem=recv_sems.at[k], device_id=to, device_id_type=MESH)

        mine = pltpu.make_async_copy(x_ref, rows(*me), local_sem)
        mine.start()
        first = [copy(0, me, sibling, src=x_ref)]
        first += [copy(1 + j, me, (*chip, c), src=x_ref) for j, chip in enumerate(chips)]
        for cp in first:
            cp.start()
        passed = [copy(4 + j, (*chip, c), sibling) for j, chip in enumerate(chips)]
        for j, chip in enumerate(chips):
            copy(1 + j, (*chip, c), me).wait_recv()
            passed[j].start()
        copy(0, sibling, me).wait_recv()
        for j, chip in enumerate(chips):
            copy(4 + j, (*chip, 1 - c), me).wait_recv()
        for cp in first + passed:
            cp.wait_send()
        mine.wait()

    return pl.pallas_call(
        body, name=name,
        out_shape=jax.ShapeDtypeStruct((N_DEV * m_per, n), xs.dtype),
        in_specs=[pl.BlockSpec(memory_space=pltpu.VMEM)],
        out_specs=pl.BlockSpec(memory_space=pltpu.VMEM),
        scratch_shapes=[pltpu.SemaphoreType.DMA((7,)), pltpu.SemaphoreType.DMA((7,)), pltpu.SemaphoreType.DMA],
    )(xs)


def _allgather_chip_slabs(slab, *, name):
    R, C = slab.shape
    Rh = R // 2

    def body(s_ref, out_ref, send_sems, recv_sems, local_sem):
        x, y, c = _mesh_pos()
        sibling = (x, y, 1 - c)
        chips = _other_chips(x, y)

        def half(px, py, pc):
            return out_ref.at[2 * px + py, pl.ds(pc * Rh, Rh), :]

        def copy(k, block, to, src=None):
            return pltpu.make_async_remote_copy(
                src_ref=half(*block) if src is None else src, dst_ref=half(*block),
                send_sem=send_sems.at[k], recv_sem=recv_sems.at[k], device_id=to, device_id_type=MESH)

        mine = pltpu.make_async_copy(s_ref, out_ref.at[2 * x + y], local_sem)
        mine.start()
        first = [copy(j, (x, y, c), (*chip, c), src=s_ref.at[pl.ds(c * Rh, Rh), :]) for j, chip in enumerate(chips)]
        for cp in first:
            cp.start()
        passed = [copy(3 + j, (*chip, c), sibling) for j, chip in enumerate(chips)]
        for j, chip in enumerate(chips):
            copy(j, (*chip, c), (x, y, c)).wait_recv()
            passed[j].start()
        for j, chip in enumerate(chips):
            copy(3 + j, (*chip, 1 - c), (x, y, c)).wait_recv()
        for cp in first + passed:
            cp.wait_send()
        mine.wait()

    return pl.pallas_call(
        body, name=name,
        out_shape=jax.ShapeDtypeStruct((N_CHIPS, R, C), slab.dtype),
        in_specs=[HBM], out_specs=HBM,
        scratch_shapes=[pltpu.SemaphoreType.DMA((6,)), pltpu.SemaphoreType.DMA((6,)), pltpu.SemaphoreType.DMA],
    )(slab)


def _swap_halves(g4, *, name):
    _, _, Rh, C = g4.shape

    def body(g_ref, out_ref, send_sems, recv_sems):
        x, y, c = _mesh_pos()
        copies = [pltpu.make_async_remote_copy(
            src_ref=g_ref.at[j, 1 - c], dst_ref=out_ref.at[j], send_sem=send_sems.at[j], recv_sem=recv_sems.at[j],
            device_id=(x, y, 1 - c), device_id_type=MESH) for j in range(N_CHIPS)]
        for cp in copies:
            cp.start()
        for cp in copies:
            cp.wait()

    return pl.pallas_call(
        body, name=name,
        out_shape=jax.ShapeDtypeStruct((N_CHIPS, Rh, C), g4.dtype),
        in_specs=[HBM], out_specs=HBM,
        scratch_shapes=[pltpu.SemaphoreType.DMA((N_CHIPS,)), pltpu.SemaphoreType.DMA((N_CHIPS,))],
    )(g4)


def _scatter_partials(part, *, name):
    _, Rh, C = part.shape

    def body(p_ref, out_ref, send_sems, recv_sems):
        x, y, c = _mesh_pos()
        copies = [pltpu.make_async_remote_copy(
            src_ref=p_ref.at[2 * px + py], dst_ref=out_ref.at[j], send_sem=send_sems.at[j], recv_sem=recv_sems.at[j],
            device_id=(px, py, c), device_id_type=MESH) for j, (px, py) in enumerate(_other_chips(x, y))]
        for cp in copies:
            cp.start()
        for cp in copies:
            cp.wait()

    return pl.pallas_call(
        body, name=name,
        out_shape=jax.ShapeDtypeStruct((3, Rh, C), part.dtype),
        in_specs=[HBM], out_specs=HBM,
        scratch_shapes=[pltpu.SemaphoreType.DMA((3,)), pltpu.SemaphoreType.DMA((3,))],
    )(part)


def _join_halves(mine, *, name):
    Rh, C = mine.shape

    def body(m_ref, out_ref, send_sem, recv_sem, local_sem):
        x, y, c = _mesh_pos()
        keep = pltpu.make_async_copy(m_ref, out_ref.at[c], local_sem)
        keep.start()
        cp = pltpu.make_async_remote_copy(
            src_ref=m_ref, dst_ref=out_ref.at[c], send_sem=send_sem, recv_sem=recv_sem,
            device_id=(x, y, 1 - c), device_id_type=MESH)
        cp.start()
        cp.wait()
        keep.wait()

    return pl.pallas_call(
        body, name=name,
        out_shape=jax.ShapeDtypeStruct((2, Rh, C), mine.dtype),
        in_specs=[HBM], out_specs=HBM,
        scratch_shapes=[pltpu.SemaphoreType.DMA, pltpu.SemaphoreType.DMA, pltpu.SemaphoreType.DMA],
    )(mine)


def _pad_rows(a, mult):
    pad = (-a.shape[0]) % mult
    return a if pad == 0 else jnp.pad(a, ((0, pad),) + ((0, 0),) * (a.ndim - 1))


def _local_step(x, target, mod, wts, small):
    S, D = x.shape
    HP = D // LANES
    row = lambda v: v.reshape(1, -1)
    msplit = [[row(mod[i, k * D:(k + 1) * D]) for k in range(6)] for i in range(2)]
    gw, gs = {}, {}
    dmod = [[None] * 6 for _ in range(2)]

    sh1, sc1, g1, sh2, sc2, g2 = msplit[0]
    n1w0, n2w0 = row(small["norm1_w"][0]), row(small["norm2_w"][0])
    proj0, h1_0 = _ln_matmul(x, n1w0, sc1, sh1, wts["hg_w_in"], relu2=False, name="hg_in_proj")
    gn = small["hg_gn_w"].reshape(1, LANES)
    ypre0, o0, states = _hg_fwd(proj0, small["hg_lb"], gn, name="hg_fwd")
    x1, ymix0 = _matmul_resid(ypre0, wts["hg_w_out"], x, g1, name="hg_out_proj")
    a0, u0, h2_0 = _ln_matmul(x1, n2w0, sc2, sh2, wts["mlp_w1_0"], relu2=True, name="mlp0_up")
    x2, ymlp0 = _matmul_resid(u0, wts["mlp_w2_0"], x1, g2, name="mlp0_down")

    sh1b, sc1b, g1b, sh2b, sc2b, g2b = msplit[1]
    n1w1, n2w1 = row(small["norm1_w"][1]), row(small["norm2_w"][1])
    proj1, h1_1 = _ln_matmul(x2, n1w1, sc1b, sh1b, wts["fox_w_in"], relu2=False, name="fox_in_proj")
    nheads = 2 * HP
    bf_pad = jnp.pad(small["fox_b_f"].reshape(1, nheads), ((0, 0), (0, LANES - nheads)))
    qw2 = jnp.tile(small["fox_qn_w"].reshape(1, FOX_DH), (1, 2))
    kw2 = jnp.tile(small["fox_kn_w"].reshape(1, FOX_DH), (1, 2))
    fcum = _fox_cumsum(proj1, bf_pad, name="fox_cumsum")
    qa, ka, va = _fox_prep(proj1, fcum, qw2, kw2, name="fox_prep")
    jmin, imax = _fox_skip_bounds(fcum, small["fox_qn_w"], small["fox_kn_w"], nheads)
    ypre1, o1, q2 = _fox_fwd(jmin, qa, ka, va, proj1, name="fox_fwd")
    x3, ymix1 = _matmul_resid(ypre1, wts["fox_w_out"], x2, g1b, name="fox_out_proj")
    a1, u1, h2_1 = _ln_matmul(x3, n2w1, sc2b, sh2b, wts["mlp_w1_1"], relu2=True, name="mlp1_up")
    x4, ymlp1 = _matmul_resid(u1, wts["mlp_w2_1"], x3, g2b, name="mlp1_down")

    loss, dx4, dfw = _loss_kernel(x4, row(small["final_w"]), target, name="loss")
    gs["final_w"] = dfw.reshape(-1)

    def mlp_bwd(i, dx_out, x_in, h2, a, u, ymlp, n2w, sc2_, g2_):
        dz, dm, dg2 = _gate_matmul_nt(dx_out, g2_, ymlp, wts[f"mlp_w2_{i}"], a, name=f"mlp{i}_down_bwd")
        gw[f"mlp_w2_{i}"] = _matmul_tn(u, dm[None], name=f"mlp{i}_dw2")
        gw[f"mlp_w1_{i}"] = _matmul_tn(h2, dz[None], name=f"mlp{i}_dw1")
        dx_in, dsc, dsh, dnw = _matmul_nt_lnbwd(dz[None], wts[f"mlp_w1_{i}"], x_in, n2w, sc2_, dx_out,
                                                name=f"mlp{i}_up_bwd")
        dmod[i][3], dmod[i][4], dmod[i][5] = dsh, dsc, dg2
        return dx_in, dnw

    dx3, dn2w1 = mlp_bwd(1, dx4, x3, h2_1, a1, u1, ymlp1, n2w1, sc2b, g2b)
    dyp1, dm1, dg1b = _gate_matmul_nt(dx3, g1b, ymix1, wts["fox_w_out"], None, name="fox_out_bwd")
    gw["fox_w_out"] = _matmul_tn(ypre1, dm1[None], name="fox_dw_out")
    doa = _fox_bwd_prep(dyp1, o1, proj1, name="fox_bwd_prep")
    dqa, dka, dva, colsum = _fox_bwd(imax, q2, ka, va, doa, name="fox_bwd")
    colsum = jnp.pad(colsum[:, 0, :].T, ((0, 0), (0, LANES - nheads)))
    dproj1, dqw, dkw = _fox_bwd_post(dqa, dka, dva, proj1, dyp1, o1, qw2, kw2, name="fox_bwd_post")
    dproj1, dbf = _fox_dfz(colsum, nheads, proj1, bf_pad, dproj1, name="fox_dfz")
    gw["fox_w_in"] = _matmul_tn(h1_1, dproj1, name="fox_dw_in")
    dx2, dsc, dsh, dn1w1 = _matmul_nt_lnbwd(dproj1, wts["fox_w_in"], x2, n1w1, sc1b, dx3, name="fox_in_bwd")
    dmod[1][0], dmod[1][1], dmod[1][2] = dsh, dsc, dg1b
    gs["fox_qn_w"] = dqw[0, :FOX_DH] + dqw[0, FOX_DH:]
    gs["fox_kn_w"] = dkw[0, :FOX_DH] + dkw[0, FOX_DH:]
    gs["fox_b_f"] = dbf[0, :nheads]

    dx1, dn2w0 = mlp_bwd(0, dx2, x1, h2_0, a0, u0, ymlp0, n2w0, sc2, g2)
    dyp0, dm0, dg1 = _gate_matmul_nt(dx1, g1, ymix0, wts["hg_w_out"], None, name="hg_out_bwd")
    gw["hg_w_out"] = _matmul_tn(ypre0, dm0[None], name="hg_dw_out")
    dproj0, dlb, dgn = _hg_bwd(proj0, small["hg_lb"], gn, o0, states, dyp0, name="hg_bwd")
    gw["hg_w_in"] = _matmul_tn(h1_0, dproj0, name="hg_dw_in")
    dx0, dsc, dsh, dn1w0 = _matmul_nt_lnbwd(dproj0, wts["hg_w_in"], x, n1w0, sc1, dx1, name="hg_in_bwd")
    dmod[0][0], dmod[0][1], dmod[0][2] = dsh, dsc, dg1
    gs["hg_lb"] = dlb
    gs["hg_gn_w"] = jnp.sum(dgn, axis=0)

    gs["norm1_w"] = jnp.concatenate([dn1w0, dn1w1], axis=0)
    gs["norm2_w"] = jnp.concatenate([dn2w0, dn2w1], axis=0)
    gs["dmod"] = jnp.stack([jnp.concatenate(dmod[i], axis=1)[0] for i in range(2)])
    return loss, dx0, gw, gs


SMALL_NAMES = ["norm1_w", "norm2_w", "hg_lb", "hg_gn_w", "fox_b_f", "fox_qn_w", "fox_kn_w", "final_w"]


def _pack_small(d, names):
    rows, offs, r0 = [], {}, 0
    for n in names:
        flat = d[n].reshape(-1)
        nr = -(-flat.shape[0] // LANES)
        rows.append(jnp.pad(flat, (0, nr * LANES - flat.shape[0])).reshape(nr, LANES))
        offs[n] = (r0, nr)
        r0 += nr
    return jnp.concatenate(rows, axis=0), offs


def _unpack_small(packed, offs, name, like):
    r0, nr = offs[name]
    return packed[r0:r0 + nr].reshape(-1)[:like.size].reshape(like.shape)


def kernel(x, c, w_mod, b_mod, norm1_w, norm2_w, hg_w_in, hg_w_out, hg_lb, hg_gn_w, fox_w_in, fox_b_f, fox_qn_w, fox_kn_w, fox_w_out, mlp_w1, mlp_w2, final_w, loss_target, m_w_mod, m_b_mod, m_norm1_w, m_norm2_w, m_hg_w_in, m_hg_w_out, m_hg_lb, m_hg_gn_w, m_fox_w_in, m_fox_b_f, m_fox_qn_w, m_fox_kn_w, m_fox_w_out, m_mlp_w1, m_mlp_w2, m_final_w, v_w_mod, v_b_mod, v_norm1_w, v_norm2_w, v_hg_w_in, v_hg_w_out, v_hg_lb, v_hg_gn_w, v_fox_w_in, v_fox_b_f, v_fox_qn_w, v_fox_kn_w, v_fox_w_out, v_mlp_w1, v_mlp_w2, v_final_w):
    S, D = x.shape[1], x.shape[2]
    nheads = D // FOX_DH
    ax, ay, ac = _mesh_pos()
    chip = 2 * ax + ay
    dev = 2 * chip + ac
    xs, tgt = x.reshape(S, D), loss_target.reshape(S, D)

    c_all = _allgather_small(_pad_rows(c.reshape(-1, LANES), 8), name="gather_c")
    c_all = c_all.reshape(N_DEV, -1)[:, :D]
    c16 = _pad_rows(c_all, 16)
    nmod = w_mod.shape[2]
    b_shard = lax.dynamic_slice_in_dim(b_mod, chip * nmod, nmod, axis=1)
    mod_shard = _mod_fwd(c16, w_mod, b_shard[:, None, :], name="mod_fwd")[:, :N_DEV]
    mod_all = _allgather_small(mod_shard.reshape(-1, LANES), name="gather_mod")
    mod_all = mod_all.reshape(N_CHIPS, 2, 2, N_DEV, nmod)[:, 0]
    mod = lax.dynamic_index_in_dim(mod_all, dev, axis=2, keepdims=False)
    mod = mod.transpose(1, 0, 2).reshape(2, N_CHIPS * nmod)

    fox_rows = fox_w_in.shape[2]
    segs = [hg_w_in[0], hg_w_out[0], fox_w_out[0], mlp_w1.reshape(2 * D, D), mlp_w2.reshape(2 * D, D),
            fox_w_in[0].reshape(fox_rows, D)]
    seg_rows = [s.shape[0] for s in segs]
    seg_off = [sum(seg_rows[:i]) for i in range(len(segs))]
    slab = _pad_rows(jnp.concatenate([s.astype(BF) for s in segs], axis=0), 32)
    R = slab.shape[0]
    gathered = _allgather_chip_slabs(slab, name="gather_weights")

    def seg(i):
        return gathered[:, seg_off[i]:seg_off[i] + seg_rows[i], :]

    col = lambda g: g.transpose(1, 0, 2).reshape(g.shape[1], -1)
    rowsh = lambda g: g.reshape(-1, g.shape[2])
    w1 = seg(3).reshape(N_CHIPS, 2, D, D)
    w2 = seg(4).reshape(N_CHIPS, 2, D, D)
    fox_in = col(seg(5).reshape(N_CHIPS, D, fox_rows))
    wts = {
        "hg_w_in": col(seg(0)), "hg_w_out": rowsh(seg(1)), "fox_w_out": rowsh(seg(2)),
        "mlp_w1_0": col(w1[:, 0]), "mlp_w1_1": col(w1[:, 1]), "mlp_w2_0": rowsh(w2[:, 0]), "mlp_w2_1": rowsh(w2[:, 1]),
        "fox_w_in": jnp.pad(fox_in, ((0, 0), (0, 5 * D - fox_in.shape[1]))),
    }
    small = {"norm1_w": norm1_w, "norm2_w": norm2_w, "hg_lb": hg_lb, "hg_gn_w": hg_gn_w, "fox_b_f": fox_b_f,
             "fox_qn_w": fox_qn_w, "fox_kn_w": fox_kn_w, "final_w": final_w}

    loss_part, grad_x, gw, gs = _local_step(xs, tgt, mod, wts, small)
    loss = lax.psum(loss_part[0, 0], ("x", "y", "c"))

    def uncol(g, n):
        return g.reshape(g.shape[0], N_CHIPS, n).transpose(1, 0, 2)

    gfox = uncol(gw["fox_w_in"][:, :4 * fox_rows], fox_rows).reshape(N_CHIPS, fox_rows, D)
    gsegs = [uncol(gw["hg_w_in"], D), gw["hg_w_out"].reshape(N_CHIPS, D // 4, D), gw["fox_w_out"].reshape(N_CHIPS, D // 4, D),
             jnp.concatenate([uncol(gw["mlp_w1_0"], D), uncol(gw["mlp_w1_1"], D)], axis=1),
             jnp.concatenate([gw["mlp_w2_0"].reshape(N_CHIPS, D, D), gw["mlp_w2_1"].reshape(N_CHIPS, D, D)], axis=1),
             gfox]
    gfull = jnp.concatenate(gsegs, axis=1)
    gfull = jnp.pad(gfull, ((0, 0), (0, R - gfull.shape[1]), (0, 0)))
    g4 = gfull.reshape(N_CHIPS, 2, R // 2, D)
    from_sibling = _swap_halves(g4.astype(BF), name="rs_swap_halves")
    chip_part = _add_halves(g4, from_sibling, ac.reshape(1), name="rs_add_halves")
    from_chips = _scatter_partials(chip_part, name="rs_scatter")
    my_half = _add_four(g4, from_sibling, from_chips, jnp.stack([chip, ac]), name="rs_add_chips")
    gshard = _join_halves(my_half, name="rs_join").reshape(R, D)

    names = ["dmod"] + SMALL_NAMES
    packed, offs = _pack_small(gs, names)
    packed = _pad_rows(packed, 8)
    rp = packed.shape[0]
    parts = _allgather_small(packed, name="gather_small").reshape(N_DEV, rp, LANES)
    total = _sum_parts(parts, name="sum_small")
    r0, nr = offs["dmod"]
    dmod_all = parts[:, r0:r0 + nr].reshape(N_DEV, 2, N_CHIPS * nmod)
    dmod_shard = lax.dynamic_slice_in_dim(dmod_all, chip * nmod, nmod, axis=2).transpose(1, 0, 2)
    g_w_mod = _mod_bwd(c16, jnp.pad(dmod_shard, ((0, 0), (0, 16 - N_DEV), (0, 0))), name="mod_bwd")

    grads = {"w_mod": g_w_mod, "b_mod": _unpack_small(total, offs, "dmod", b_mod)}
    for n in SMALL_NAMES:
        grads[n] = _unpack_small(total, offs, n, small[n])

    given = dict(w_mod=(w_mod, m_w_mod, v_w_mod), b_mod=(b_mod, m_b_mod, v_b_mod), norm1_w=(norm1_w, m_norm1_w, v_norm1_w),
                 norm2_w=(norm2_w, m_norm2_w, v_norm2_w), hg_w_in=(hg_w_in, m_hg_w_in, v_hg_w_in),
                 hg_w_out=(hg_w_out, m_hg_w_out, v_hg_w_out), hg_lb=(hg_lb, m_hg_lb, v_hg_lb),
                 hg_gn_w=(hg_gn_w, m_hg_gn_w, v_hg_gn_w), fox_w_in=(fox_w_in, m_fox_w_in, v_fox_w_in),
                 fox_b_f=(fox_b_f, m_fox_b_f, v_fox_b_f), fox_qn_w=(fox_qn_w, m_fox_qn_w, v_fox_qn_w),
                 fox_kn_w=(fox_kn_w, m_fox_kn_w, v_fox_kn_w), fox_w_out=(fox_w_out, m_fox_w_out, v_fox_w_out),
                 mlp_w1=(mlp_w1, m_mlp_w1, v_mlp_w1), mlp_w2=(mlp_w2, m_mlp_w2, v_mlp_w2), final_w=(final_w, m_final_w, v_final_w))
    upd = {}

    big = [("hg_w_in", 0), ("hg_w_out", 1), ("fox_w_out", 2), ("mlp_w1", 3), ("mlp_w2", 4), ("fox_w_in", 5)]
    for n, i in big:
        w, m, v = given[n]
        flat = lambda a: a.reshape(seg_rows[i], D)
        d, mn, vn = _adamw(flat(w), gshard, flat(m), flat(v), g_row0=seg_off[i], name=f"adamw_{n}")
        grads[n] = gshard[seg_off[i]:seg_off[i] + seg_rows[i]].reshape(w.shape)
        upd[n] = tuple(a.reshape(w.shape) for a in (d, mn, vn))

    w, m, v = given["w_mod"]
    flat = lambda a: a.reshape(-1, nmod)
    upd["w_mod"] = tuple(a.reshape(w.shape) for a in _adamw(flat(w), flat(g_w_mod), flat(m), flat(v), name="adamw_w_mod"))

    snames = ["b_mod"] + SMALL_NAMES
    pw, soffs = _pack_small({n: given[n][0] for n in snames}, snames)
    pm, _ = _pack_small({n: given[n][1] for n in snames}, snames)
    pv, _ = _pack_small({n: given[n][2] for n in snames}, snames)
    pg, _ = _pack_small({n: grads[n] for n in snames}, snames)
    pw, pm, pv, pg = (_pad_rows(a, 8) for a in (pw, pm, pv, pg))
    sd, smn, svn = _adamw(pw, pg, pm, pv, name="adamw_small")
    for n in snames:
        like = given[n][0]
        upd[n] = tuple(_unpack_small(a, soffs, n, like) for a in (sd, smn, svn))

    order = ["w_mod", "b_mod", "norm1_w", "norm2_w", "hg_w_in", "hg_w_out", "hg_lb", "hg_gn_w", "fox_w_in", "fox_b_f",
             "fox_qn_w", "fox_kn_w", "fox_w_out", "mlp_w1", "mlp_w2", "final_w"]
    return (loss, grad_x.reshape(x.shape), *[grads[n] for n in order], *[upd[n][0] for n in order],
            *[upd[n][1] for n in order], *[upd[n][2] for n in order])
```

```python
import math

import jax
import jax.numpy as jnp
from jax import lax
from jax.experimental import pallas as pl
from jax.experimental.pallas import tpu as pltpu

EPS = 1e-6
ADAM_LR, ADAM_B1, ADAM_B2, ADAM_EPS, ADAM_WD, ADAM_STEP = 0.001, 0.9, 0.999, 1e-08, 0.01, 10

F32 = jnp.float32
BF = jnp.bfloat16
LANES = 128
HG_CHUNK = 64
HG_HEADS_PER_STEP = 8
HG_TOKENS_PER_STEP = 256
FOX_BWD_UNROLL = 4
LOG2E = 1.4426950408889634
FOX_DH = 64
N_CHIPS = 4
N_DEV = 8
VMEM_LIMIT = 48 * 1024 * 1024
MESH = pl.DeviceIdType.MESH

NT = (((1,), (1,)), ((), ()))
TN = (((0,), (0,)), ((), ()))


def _pick(n, pref, mult=LANES):
    if n <= pref:
        return n
    t = (pref // mult) * mult
    while t >= mult:
        if n % t == 0:
            return t
        t -= mult
    raise ValueError((n, pref, mult))


def _cp(*sem):
    return pltpu.CompilerParams(dimension_semantics=sem, vmem_limit_bytes=VMEM_LIMIT)


def _dot(a, b):
    return jnp.dot(a, b, preferred_element_type=F32)


def _dg(a, b, dims):
    return lax.dot_general(a, b, dims, preferred_element_type=F32)


def _split3(x):
    hi = x.astype(BF)
    r1 = x - hi.astype(F32)
    mid = r1.astype(BF)
    lo = (r1 - mid.astype(F32)).astype(BF)
    return hi, mid, lo


def _tri_dot(tri, x):
    hi, mid, lo = _split3(x)
    return _dot(tri, hi) + _dot(tri, mid) + _dot(tri, lo)


def _dg3(a, b, dims):
    ah, bh = a.astype(BF), b.astype(BF)
    al, bl = (a - ah.astype(F32)).astype(BF), (b - bh.astype(F32)).astype(BF)
    return _dg(ah, bh, dims) + _dg(ah, bl, dims) + _dg(al, bh, dims)


NN = (((1,), (0,)), ((), ()))


def _sigmoid(x):
    return jax.nn.sigmoid(x)


def _ln_matmul(x, nw, sc, sh, w, *, relu2, name):
    S, D = x.shape
    N = w.shape[1]
    tm, tn = _pick(S, 1024, 16), _pick(N, 1024)

    def body(x_ref, nw_ref, sc_ref, sh_ref, w_ref, *rest):
        outs, hs = rest[:-1], rest[-1]
        h_ref = outs[-1]

        @pl.when(pl.program_id(1) == 0)
        def _():
            xv = x_ref[...]
            r = lax.rsqrt(jnp.mean(xv * xv, axis=-1, keepdims=True) + EPS)
            hb = ((xv * r * nw_ref[...]) * (1.0 + sc_ref[...]) + sh_ref[...]).astype(BF)
            hs[...] = hb
            h_ref[...] = hb

        z = _dot(hs[...], w_ref[...])
        if relu2:
            a = jnp.maximum(z, 0.0)
            outs[0][...] = a.astype(BF)
            outs[1][...] = (a * a).astype(BF)
        else:
            outs[0][...] = z

    vec = pl.BlockSpec((1, D), lambda i, j: (0, 0))
    tile = pl.BlockSpec((tm, tn), lambda i, j: (i, j))
    if relu2:
        out_shape = [jax.ShapeDtypeStruct((S, N), BF), jax.ShapeDtypeStruct((S, N), BF)]
        out_specs = [tile, tile]
    else:
        out_shape = [jax.ShapeDtypeStruct((S, N), F32)]
        out_specs = [tile]
    out_shape.append(jax.ShapeDtypeStruct((S, D), BF))
    out_specs.append(pl.BlockSpec((tm, D), lambda i, j: (i, 0)))
    return pl.pallas_call(
        body, name=name, grid=(S // tm, N // tn),
        in_specs=[pl.BlockSpec((tm, D), lambda i, j: (i, 0)), vec, vec, vec,
                  pl.BlockSpec((D, tn), lambda i, j: (0, j))],
        out_specs=out_specs, out_shape=out_shape,
        scratch_shapes=[pltpu.VMEM((tm, D), BF)],
        compiler_params=_cp("parallel", "arbitrary"),
    )(x, nw, sc, sh, w)


def _matmul_resid(a, w, x, gate, *, name):
    S, K = a.shape
    D = w.shape[1]
    big = 1024 if K <= 1024 else 512
    tm, tn = _pick(S, big, 16), _pick(D, big)

    def body(a_ref, w_ref, x_ref, g_ref, o_ref, y_ref):
        y = _dot(a_ref[...], w_ref[...])
        y_ref[...] = y.astype(BF)
        o_ref[...] = x_ref[...] + g_ref[...] * y

    tile = pl.BlockSpec((tm, tn), lambda i, j: (i, j))
    return pl.pallas_call(
        body, name=name, grid=(S // tm, D // tn),
        in_specs=[pl.BlockSpec((tm, K), lambda i, j: (i, 0)), pl.BlockSpec((K, tn), lambda i, j: (0, j)),
                  tile, pl.BlockSpec((1, tn), lambda i, j: (0, j))],
        out_specs=[tile, tile],
        out_shape=[jax.ShapeDtypeStruct((S, D), F32), jax.ShapeDtypeStruct((S, D), BF)],
        compiler_params=_cp("parallel", "arbitrary"),
    )(a, w, x, gate)


def _gate_matmul_nt(dx, gate, y, w, act, *, name):
    S, D = dx.shape
    K = w.shape[0]
    tm, tn = _pick(S, 1024, 16), _pick(K, 1024)
    fused = act is not None

    def body(dx_ref, g_ref, y_ref, w_ref, *rest):
        if fused:
            act_ref, da_ref, dm_ref, dg_ref, ms = rest
        else:
            da_ref, dm_ref, dg_ref, ms = rest
        i, j = pl.program_id(0), pl.program_id(1)

        @pl.when((i == 0) & (j == 0))
        def _():
            dg_ref[...] = jnp.zeros_like(dg_ref)

        @pl.when(j == 0)
        def _():
            dxv = dx_ref[...]
            dmb = (dxv * g_ref[...]).astype(BF)
            ms[...] = dmb
            dm_ref[...] = dmb
            dg_ref[...] += jnp.sum(dxv * y_ref[...].astype(F32), axis=0, keepdims=True)

        da = _dg(ms[...], w_ref[...], NT)
        if fused:
            da_ref[...] = (da * (2.0 * act_ref[...].astype(F32))).astype(BF)
        else:
            da_ref[...] = da

    row = pl.BlockSpec((tm, D), lambda i, j: (i, 0))
    vec = pl.BlockSpec((1, D), lambda i, j: (0, 0))
    tile = pl.BlockSpec((tm, tn), lambda i, j: (i, j))
    in_specs = [row, vec, row, pl.BlockSpec((tn, D), lambda i, j: (j, 0))]
    args = [dx, gate, y, w]
    if fused:
        in_specs.append(tile)
        args.append(act)
    return pl.pallas_call(
        body, name=name, grid=(S // tm, K // tn),
        in_specs=in_specs, out_specs=[tile, row, vec],
        out_shape=[jax.ShapeDtypeStruct((S, K), BF if fused else F32), jax.ShapeDtypeStruct((S, D), BF),
                   jax.ShapeDtypeStruct((1, D), F32)],
        scratch_shapes=[pltpu.VMEM((tm, D), BF)],
        compiler_params=_cp("arbitrary", "arbitrary"),
    )(*args)


def _matmul_tn(a, b, *, name):
    S, Ka = a.shape
    P, _, Db = b.shape
    tk, tn, ts = _pick(Ka, 1024), _pick(Db, 1024), _pick(S, 1024, 16)
    npb = Db // tn

    def body(a_ref, b_ref, o_ref, acc):
        s = pl.program_id(2)

        @pl.when(s == 0)
        def _():
            acc[...] = jnp.zeros_like(acc)

        acc[...] += _dg(a_ref[...], b_ref[...], TN)

        @pl.when(s == pl.num_programs(2) - 1)
        def _():
            o_ref[...] = acc[...]

    return pl.pallas_call(
        body, name=name, grid=(Ka // tk, P * npb, S // ts),
        in_specs=[pl.BlockSpec((ts, tk), lambda i, j, s: (s, i)),
                  pl.BlockSpec((None, ts, tn), lambda i, j, s: (j // npb, s, j % npb))],
        out_specs=pl.BlockSpec((tk, tn), lambda i, j, s: (i, j)),
        out_shape=jax.ShapeDtypeStruct((Ka, P * Db), F32),
        scratch_shapes=[pltpu.VMEM((tk, tn), F32)],
        compiler_params=_cp("parallel", "parallel", "arbitrary"),
    )(a, b)


def _matmul_nt_lnbwd(g, w, x, nw, sc, dx_out, *, name):
    P, S, Dg = g.shape
    D = x.shape[1]
    tm, tk = _pick(S, 1024, 16), _pick(Dg, 1024)
    npb = Dg // tk
    nk = P * npb

    def body(g_ref, w_ref, x_ref, nw_ref, sc_ref, dxo_ref, dx_ref, dsc_ref, dsh_ref, dnw_ref, acc):
        i, k = pl.program_id(0), pl.program_id(1)

        @pl.when((i == 0) & (k == 0))
        def _():
            dsc_ref[...] = jnp.zeros_like(dsc_ref)
            dsh_ref[...] = jnp.zeros_like(dsh_ref)
            dnw_ref[...] = jnp.zeros_like(dnw_ref)

        @pl.when(k == 0)
        def _():
            acc[...] = jnp.zeros_like(acc)

        acc[...] += _dg(g_ref[...], w_ref[...], NT)

        @pl.when(k == nk - 1)
        def _():
            dh = acc[...]
            xv = x_ref[...]
            nwv = nw_ref[...]
            r = lax.rsqrt(jnp.mean(xv * xv, axis=-1, keepdims=True) + EPS)
            xr = xv * r
            dn = dh * (1.0 + sc_ref[...])
            dsc_ref[...] += jnp.sum(dh * (xr * nwv), axis=0, keepdims=True)
            dsh_ref[...] += jnp.sum(dh, axis=0, keepdims=True)
            dnw_ref[...] += jnp.sum(dn * xr, axis=0, keepdims=True)
            u = dn * nwv
            dx_ref[...] = dxo_ref[...] + r * (u - xr * jnp.mean(u * xr, axis=-1, keepdims=True))

    row = pl.BlockSpec((tm, D), lambda i, k: (i, 0))
    vec = pl.BlockSpec((1, D), lambda i, k: (0, 0))
    return pl.pallas_call(
        body, name=name, grid=(S // tm, nk),
        in_specs=[pl.BlockSpec((None, tm, tk), lambda i, k: (k // npb, i, k % npb)),
                  pl.BlockSpec((D, tk), lambda i, k: (0, k)), row, vec, vec, row],
        out_specs=[row, vec, vec, vec],
        out_shape=[jax.ShapeDtypeStruct((S, D), F32)] + [jax.ShapeDtypeStruct((1, D), F32)] * 3,
        scratch_shapes=[pltpu.VMEM((tm, D), F32)],
        compiler_params=_cp("arbitrary", "arbitrary"),
    )(g, w, x, nw, sc, dx_out)


def _loss_kernel(x, fw, tgt, *, name):
    S, D = x.shape
    tm = _pick(S, 512, 8)

    def body(x_ref, fw_ref, t_ref, l_ref, dx_ref, dfw_ref):
        @pl.when(pl.program_id(0) == 0)
        def _():
            l_ref[...] = jnp.zeros_like(l_ref)
            dfw_ref[...] = jnp.zeros_like(dfw_ref)

        xv = x_ref[...]
        fwv = fw_ref[...]
        r = lax.rsqrt(jnp.mean(xv * xv, axis=-1, keepdims=True) + EPS)
        xr = xv * r
        err = xr * fwv - t_ref[...]
        per_tok = jnp.mean(err * err, axis=-1, keepdims=True)
        l_ref[...] += 0.5 * jnp.sum(per_tok, axis=0, keepdims=True)
        dy = err * (1.0 / D)
        dfw_ref[...] += jnp.sum(dy * xr, axis=0, keepdims=True)
        u = dy * fwv
        dx_ref[...] = r * (u - xr * jnp.mean(u * xr, axis=-1, keepdims=True))

    row = pl.BlockSpec((tm, D), lambda i: (i, 0))
    vec = pl.BlockSpec((1, D), lambda i: (0, 0))
    return pl.pallas_call(
        body, name=name, grid=(S // tm,),
        in_specs=[row, vec, row],
        out_specs=[pl.BlockSpec((1, LANES), lambda i: (0, 0)), row, vec],
        out_shape=[jax.ShapeDtypeStruct((1, LANES), F32), jax.ShapeDtypeStruct((S, D), F32),
                   jax.ShapeDtypeStruct((1, D), F32)],
        compiler_params=_cp("arbitrary"),
    )(x, fw, tgt)


def _hg_lower_bound(lb3):
    mx = jnp.max(lb3, axis=0, keepdims=True)
    e = jnp.exp(lb3 - mx)
    p = e / jnp.sum(e, axis=0, keepdims=True)
    return p[0:1, :], p


def _hg_chunk_common(qr, fz, lbv):
    sq = _sigmoid(qr)
    q = qr * sq
    sig = _sigmoid(fz)
    f = lbv + (1.0 - lbv) * sig
    k = (1.0 - lbv) * (1.0 - sig)
    return q, sq, sig, f, k, jnp.log(f)


def _row_of(x, rows, r):
    return jnp.sum(jnp.where(rows == r, x, 0.0), axis=0, keepdims=True)


def _hg_fwd(proj, hg_lb, gn, *, name):
    S = proj.shape[0]
    D = proj.shape[1] // 4
    H = D // LANES
    HB = min(HG_HEADS_PER_STEP, H)
    W = HB * LANES
    C = HG_CHUNK
    T = _pick(S, HG_TOKENS_PER_STEP, C)
    nch, nb = T // C, S // T

    def body(q_ref, fz_ref, v_ref, g_ref, lb_ref, gn_ref, y_ref, o_ref, sts_ref, st):
        @pl.when(pl.program_id(1) == 0)
        def _():
            st[...] = jnp.zeros_like(st)

        lb_all, _ = _hg_lower_bound(lb_ref[...])
        gnv = gn_ref[...]
        ri = lax.broadcasted_iota(jnp.int32, (C, C), 0)
        ci_ = lax.broadcasted_iota(jnp.int32, (C, C), 1)
        low = ri >= ci_
        tri = jnp.where(low, 1.0, 0.0).astype(BF)
        rows = lax.broadcasted_iota(jnp.int32, (C, LANES), 0)

        def chunk(ci, carry):
            sl = pl.ds(pl.multiple_of(ci * C, C), C)
            for hh in range(HB):
                ls = slice(hh * LANES, (hh + 1) * LANES)
                q, _, _, _, k, logf = _hg_chunk_common(q_ref[sl, ls], fz_ref[sl, ls], lb_all[:, ls])
                vv = v_ref[sl, ls]
                gg = g_ref[sl, ls]
                G = _tri_dot(tri, logf)
                Gm = _row_of(G, rows, C // 2 - 1)
                Gl = _row_of(G, rows, C - 1)
                qt = q * jnp.exp(G - Gm)
                kt = k * jnp.exp(Gm - G)
                A = jnp.where(low, _dg3(qt, kt, NT), 0.0)
                Sv = st[hh]
                sts_ref[hh, ci] = Sv
                o = _dg3(A, vv, NN) + _dg3(q * jnp.exp(G), Sv, NT)
                st[hh] = Sv * jnp.exp(Gl) + _dg3(vv, k * jnp.exp(Gl - G), TN)
                r = lax.rsqrt(jnp.mean(o * o, axis=-1, keepdims=True) + EPS)
                y_ref[sl, ls] = ((o * r * gnv) * (gg * _sigmoid(gg))).astype(BF)
                o_ref[sl, ls] = o
            return carry

        lax.fori_loop(0, nch, chunk, 0)

    ng = H // HB

    def part(p):
        return pl.BlockSpec((T, W), lambda h, n: (n, p * ng + h))

    blk = pl.BlockSpec((T, W), lambda h, n: (n, h))
    return pl.pallas_call(
        body, name=name, grid=(ng, nb),
        in_specs=[part(0), part(1), part(2), part(3),
                  pl.BlockSpec((3, W), lambda h, n: (0, h)), pl.BlockSpec((1, LANES), lambda h, n: (0, 0))],
        out_specs=[blk, blk, pl.BlockSpec((HB, nch, LANES, LANES), lambda h, n: (h, n, 0, 0))],
        out_shape=[jax.ShapeDtypeStruct((S, D), BF), jax.ShapeDtypeStruct((S, D), F32),
                   jax.ShapeDtypeStruct((H, S // C, LANES, LANES), F32)],
        scratch_shapes=[pltpu.VMEM((HB, LANES, LANES), F32)],
        compiler_params=_cp("parallel", "arbitrary"),
    )(proj, proj, proj, proj, hg_lb, gn)


def _hg_bwd(proj, hg_lb, gn, o_all, states, dy, *, name):
    S = proj.shape[0]
    D = proj.shape[1] // 4
    H = D // LANES
    HB = min(HG_HEADS_PER_STEP, H)
    W = HB * LANES
    C = HG_CHUNK
    T = _pick(S, HG_TOKENS_PER_STEP, C)
    nch, nb = T // C, S // T

    def body(q_ref, fz_ref, v_ref, g_ref, lb_ref, gn_ref, o_ref, sts_ref, dy_ref,
             dp_ref, dlb_ref, dgn_ref, dst, dlb_acc):
        n = pl.program_id(1)

        @pl.when(n == 0)
        def _():
            dst[...] = jnp.zeros_like(dst)
            dlb_acc[...] = jnp.zeros_like(dlb_acc)
            dgn_ref[...] = jnp.zeros_like(dgn_ref)

        lb_all, p3 = _hg_lower_bound(lb_ref[...])
        gnv = gn_ref[...]
        ri = lax.broadcasted_iota(jnp.int32, (C, C), 0)
        ci_ = lax.broadcasted_iota(jnp.int32, (C, C), 1)
        low = ri >= ci_
        tri = jnp.where(low, 1.0, 0.0).astype(BF)
        triu = jnp.where(ri <= ci_, 1.0, 0.0).astype(BF)
        rows = lax.broadcasted_iota(jnp.int32, (C, LANES), 0)

        def chunk(cj, carry):
            ci = nch - 1 - cj
            sl = pl.ds(pl.multiple_of(ci * C, C), C)
            for hh in range(HB):
                ls = slice(hh * LANES, (hh + 1) * LANES)
                lbv = lb_all[:, ls]
                qr = q_ref[sl, ls]
                q, sq, sig, f, k, logf = _hg_chunk_common(qr, fz_ref[sl, ls], lbv)
                vv = v_ref[sl, ls]
                gg = g_ref[sl, ls]
                o = o_ref[sl, ls]
                dyv = dy_ref[sl, ls]
                G = _tri_dot(tri, logf)
                Gm = _row_of(G, rows, C // 2 - 1)
                Gl = _row_of(G, rows, C - 1)
                eG, e_qm, e_km, e_lk, eGl = jnp.exp(G), jnp.exp(G - Gm), jnp.exp(Gm - G), jnp.exp(Gl - G), jnp.exp(Gl)
                qt = q * e_qm
                kt = k * e_km
                A = jnp.where(low, _dg3(qt, kt, NT), 0.0)
                sg = _sigmoid(gg)
                r = lax.rsqrt(jnp.mean(o * o, axis=-1, keepdims=True) + EPS)
                on = o * r
                d_onw = dyv * (gg * sg)
                dgn_ref[hh] += jnp.sum(d_onw * on, axis=0, keepdims=True)
                dgg = dyv * (on * gnv) * (sg * (1.0 + gg * (1.0 - sg)))
                u = d_onw * gnv
                do = r * (u - on * jnp.mean(u * on, axis=-1, keepdims=True))
                Sv = sts_ref[hh, ci]
                dSv = dst[hh]
                dA = jnp.where(low, _dg3(do, vv, NT), 0.0)
                kdec = k * e_lk
                dv = _dg3(A, do, TN) + _dg3(kdec, dSv, NT)
                dq = _dg3(dA, kt, NN) * e_qm + eG * _dg3(do, Sv, NN)
                dk = _dg3(dA, qt, TN) * e_km + e_lk * _dg3(vv, dSv, NN)
                s_end = Sv * eGl + _dg3(vv, kdec, TN)
                dgl = jnp.sum(dSv * s_end, axis=0, keepdims=True)
                dG = q * dq - k * dk + jnp.where(rows == C - 1, dgl, 0.0)
                dlogf = _tri_dot(triu, dG) - f * dk
                dst[hh] = dSv * eGl + _dg3(do, q * eG, TN)
                dlf_f = dlogf / f
                dlb_acc[:, ls] += jnp.sum(dlf_f * (1.0 - sig), axis=0, keepdims=True)
                dp_ref[0, sl, ls] = (dq * (sq * (1.0 + qr * (1.0 - sq)))).astype(BF)
                dp_ref[1, sl, ls] = (dlf_f * (1.0 - lbv) * sig * (1.0 - sig)).astype(BF)
                dp_ref[2, sl, ls] = dv.astype(BF)
                dp_ref[3, sl, ls] = dgg.astype(BF)
            return carry

        lax.fori_loop(0, nch, chunk, 0)
        sel = jnp.where(lax.broadcasted_iota(jnp.int32, (3, W), 0) == 0, 1.0, 0.0)
        dlb_ref[...] = lb_all * (sel - p3) * dlb_acc[...]

    ng = H // HB

    def part(p):
        return pl.BlockSpec((T, W), lambda h, n: (nb - 1 - n, p * ng + h))

    blk = pl.BlockSpec((T, W), lambda h, n: (nb - 1 - n, h))
    return pl.pallas_call(
        body, name=name, grid=(ng, nb),
        in_specs=[part(0), part(1), part(2), part(3),
                  pl.BlockSpec((3, W), lambda h, n: (0, h)), pl.BlockSpec((1, LANES), lambda h, n: (0, 0)),
                  blk, pl.BlockSpec((HB, nch, LANES, LANES), lambda h, n: (h, nb - 1 - n, 0, 0)), blk],
        out_specs=[pl.BlockSpec((4, T, W), lambda h, n: (0, nb - 1 - n, h)),
                   pl.BlockSpec((3, W), lambda h, n: (0, h)),
                   pl.BlockSpec((HB, 1, LANES), lambda h, n: (h, 0, 0))],
        out_shape=[jax.ShapeDtypeStruct((4, S, D), BF), jax.ShapeDtypeStruct((3, D), F32),
                   jax.ShapeDtypeStruct((H, 1, LANES), F32)],
        scratch_shapes=[pltpu.VMEM((HB, LANES, LANES), F32), pltpu.VMEM((1, W), F32)],
        compiler_params=_cp("parallel", "arbitrary"),
    )(proj, proj, proj, proj, hg_lb, gn, o_all, states, dy)


def _log_sigmoid(u):
    return jnp.minimum(u, 0.0) - jnp.log(1.0 + jnp.exp(-jnp.abs(u)))


def _lane_put(base, lane, first, pieces):
    for n, p in enumerate(pieces):
        base = jnp.where(lane == first + n, p, base)
    return base


def _fox_cumsum(proj, bf_pad, *, name):
    S = proj.shape[0]
    D = proj.shape[1] // 5
    T = _pick(S, 256, 8)

    def body(fz_ref, b_ref, f_ref, carry):
        @pl.when(pl.program_id(0) == 0)
        def _():
            carry[...] = jnp.zeros_like(carry)

        logf = _log_sigmoid(fz_ref[...] + b_ref[...])
        tri = jnp.where(lax.broadcasted_iota(jnp.int32, (T, T), 0) >= lax.broadcasted_iota(jnp.int32, (T, T), 1),
                        1.0, 0.0).astype(BF)
        fv = _tri_dot(tri, logf) + carry[...]
        f_ref[...] = fv
        carry[...] = _row_of(fv, lax.broadcasted_iota(jnp.int32, (T, LANES), 0), T - 1)

    return pl.pallas_call(
        body, name=name, grid=(S // T,),
        in_specs=[pl.BlockSpec((T, LANES), lambda i: (i, 4 * D // LANES)), pl.BlockSpec((1, LANES), lambda i: (0, 0))],
        out_specs=pl.BlockSpec((T, LANES), lambda i: (i, 0)),
        out_shape=jax.ShapeDtypeStruct((S, LANES), F32),
        scratch_shapes=[pltpu.VMEM((1, LANES), F32)],
        compiler_params=_cp("arbitrary"),
    )(proj, bf_pad)


def _pair_stats(sq, lo):
    s_lo = jnp.sum(jnp.where(lo, sq, 0.0), axis=-1, keepdims=True)
    s_hi = jnp.sum(jnp.where(lo, 0.0, sq), axis=-1, keepdims=True)
    return jnp.where(lo, s_lo, s_hi) * (1.0 / FOX_DH)


def _fox_prep(proj, fcum, qw2, kw2, *, name):
    S = proj.shape[0]
    D = proj.shape[1] // 5
    HP = D // LANES
    T = _pick(S, 512, 16)

    def body(q_ref, k_ref, v_ref, f_ref, qw_ref, kw_ref, qa_ref, ka_ref, va_ref):
        hp = pl.program_id(1)
        lane = lax.broadcasted_iota(jnp.int32, (T, LANES), 1)
        lo = lane < FOX_DH
        qv, kv, vv, fv = q_ref[...], k_ref[...], v_ref[...], f_ref[...]
        qn = qv * lax.rsqrt(_pair_stats(qv * qv, lo) + EPS) * qw_ref[...] * (0.125 * LOG2E)
        kn = kv * lax.rsqrt(_pair_stats(kv * kv, lo) + EPS) * kw_ref[...]
        ones_q = jnp.where((lane >= 67) & (lane <= 69), 1.0, 0.0)
        ones_k = jnp.where(((lane >= 64) & (lane <= 66)) | ((lane >= 70) & (lane <= 72)), 1.0, 0.0)
        ones_v = jnp.where((lane >= 64) & (lane <= 66), 1.0, 0.0)
        for hh in range(2):
            fh = jnp.sum(jnp.where(lane == 2 * hp + hh, fv, 0.0), axis=-1, keepdims=True) * LOG2E
            pieces = [p.astype(F32) for p in _split3(fh)]

            def half(x):
                return jnp.where(lo, x if hh == 0 else pltpu.roll(x, FOX_DH, 1), 0.0)

            qa_ref[hh] = _lane_put(half(qn) + ones_q, lane, 64, pieces).astype(BF)
            ka_ref[hh] = _lane_put(half(kn) + ones_k, lane, 67, [-p for p in pieces]).astype(BF)
            va_ref[hh] = (half(vv) + ones_v).astype(BF)

    def part(p):
        return pl.BlockSpec((T, LANES), lambda i, hp: (i, p * HP + hp))

    vec = pl.BlockSpec((1, LANES), lambda i, hp: (0, 0))
    aug = pl.BlockSpec((2, T, LANES), lambda i, hp: (hp, i, 0))
    return pl.pallas_call(
        body, name=name, grid=(S // T, HP),
        in_specs=[part(0), part(1), part(2), pl.BlockSpec((T, LANES), lambda i, hp: (i, 0)), vec, vec],
        out_specs=[aug, aug, aug],
        out_shape=[jax.ShapeDtypeStruct((2 * HP, S, LANES), BF)] * 3,
        compiler_params=_cp("parallel", "arbitrary"),
    )(proj, proj, proj, fcum, qw2, kw2)


def _fox_block(S):
    return _pick(S, 256, 16)


def _fox_skip_bounds(fcum, qn_w, kn_w, nheads):
    S = fcum.shape[0]
    B = _fox_block(S)
    qk = 8.0 * LOG2E * 1.02 * jnp.max(jnp.abs(qn_w)) * jnp.max(jnp.abs(kn_w))
    thresh = -(2.0 * qk + 160.0)
    f2 = fcum[:, :nheads] * LOG2E
    first, last = f2[0::B], f2[B - 1::B]
    nb = S // B
    blk = jnp.arange(nb)
    dead = (first[0::2, None, :] - last[None, :, :]) < thresh
    jmin = jnp.sum(dead & (blk[None, :, None] < 2 * jnp.arange(nb // 2)[:, None, None]), axis=1)
    live = (first[:, None, :] - last[None, :, :]) >= thresh
    imax = blk[:, None] + jnp.sum(live & (blk[:, None, None] > blk[None, :, None]), axis=0)
    return jmin.T.astype(jnp.int32), imax.T.astype(jnp.int32)


def _fox_fwd(jmin, qa, ka, va, proj, *, name):
    H, S, _ = qa.shape
    HP = H // 2
    D = HP * LANES
    B = _fox_block(S)
    BQ = 2 * B
    nq = S // BQ

    def body(jmin_ref, q_ref, k_ref, v_ref, g_ref, y_ref, o_ref, q2_ref):
        hp, i = pl.program_id(0), pl.program_id(1)
        lane = lax.broadcasted_iota(jnp.int32, (B, LANES), 1)
        lo = lane < FOX_DH
        causal = lax.broadcasted_iota(jnp.int32, (B, B), 1) <= lax.broadcasted_iota(jnp.int32, (B, B), 0)
        m0, acc0 = jnp.full((B, 1), -jnp.inf, F32), jnp.zeros((B, LANES), F32)
        outs = []
        for hh in range(2):
            qbs = [q_ref[hh, 0:B, :], q_ref[hh, B:BQ, :]]

            def block(c, j, m, acc, masked=False):
                sl = pl.ds(pl.multiple_of(j * B, B), B)
                s = _dg(qbs[c], k_ref[hh, sl, :], NT)
                if masked:
                    s = jnp.where(causal, s, -jnp.inf)
                m_new = jnp.maximum(m, jnp.max(s, axis=-1, keepdims=True))
                p = jnp.exp2(s - m_new)
                ph = p.astype(BF)
                pl_ = (p - ph.astype(F32)).astype(BF)
                vb = v_ref[hh, sl, :]
                pv = _dot(jnp.concatenate([ph, pl_], axis=1), jnp.concatenate([vb, vb], axis=0))
                return m_new, acc * jnp.exp2(m - m_new) + pv

            def both(j, carry):
                ma, aa, mb, ab = carry
                return block(0, j, ma, aa) + block(1, j, mb, ab)

            ma, aa, mb, ab = lax.fori_loop(jmin_ref[2 * hp + hh, i], 2 * i, both, (m0, acc0, m0, acc0))
            ma, aa = block(0, 2 * i, ma, aa, masked=True)
            mb, ab = block(1, 2 * i, mb, ab)
            mb, ab = block(1, 2 * i + 1, mb, ab, masked=True)
            for c, (m, acc) in enumerate(((ma, aa), (mb, ab))):
                l = jnp.sum(jnp.where(lane == FOX_DH, acc, 0.0), axis=-1, keepdims=True)
                outs.append(acc / l)
                neg_lse = [-(p.astype(F32)) for p in _split3(m + jnp.log2(l))]
                q2_ref[hh, c * B:(c + 1) * B, :] = _lane_put(qbs[c].astype(F32), lane, 70, neg_lse).astype(BF)
        for c in range(2):
            rows = slice(c * B, (c + 1) * B)
            o = jnp.where(lo, outs[c], pltpu.roll(outs[2 + c], FOX_DH, 1))
            o_ref[rows, :] = o
            y_ref[rows, :] = (o * _sigmoid(g_ref[rows, :])).astype(BF)

    blk = pl.BlockSpec((BQ, LANES), lambda hp, i, jm: (i, hp))
    qblk = pl.BlockSpec((2, BQ, LANES), lambda hp, i, jm: (hp, i, 0))
    full = pl.BlockSpec((2, S, LANES), lambda hp, i, jm: (hp, 0, 0))
    return pl.pallas_call(
        body, name=name,
        grid_spec=pltpu.PrefetchScalarGridSpec(
            num_scalar_prefetch=1, grid=(HP, nq),
            in_specs=[qblk, full, full, pl.BlockSpec((BQ, LANES), lambda hp, i, jm: (i, 3 * HP + hp))],
            out_specs=[blk, blk, qblk]),
        out_shape=[jax.ShapeDtypeStruct((S, D), BF), jax.ShapeDtypeStruct((S, D), F32),
                   jax.ShapeDtypeStruct((H, S, LANES), BF)],
        compiler_params=_cp("parallel", "arbitrary"),
    )(jmin, qa, ka, va, proj)


def _fox_bwd_prep(dy, o, proj, *, name):
    S, D = dy.shape
    HP = D // LANES
    T = _pick(S, 512, 16)

    def body(dy_ref, o_ref, g_ref, da_ref):
        lane = lax.broadcasted_iota(jnp.int32, (T, LANES), 1)
        lo = lane < FOX_DH
        do = (dy_ref[...] * _sigmoid(g_ref[...])).astype(BF).astype(F32)
        prod = do * o_ref[...]
        d_lo = jnp.sum(jnp.where(lo, prod, 0.0), axis=-1, keepdims=True)
        d_hi = jnp.sum(jnp.where(lo, 0.0, prod), axis=-1, keepdims=True)
        for hh, delta in enumerate((d_lo, d_hi)):
            base = jnp.where(lo, do if hh == 0 else pltpu.roll(do, FOX_DH, 1), 0.0)
            da_ref[hh] = _lane_put(base, lane, 64, [-(p.astype(F32)) for p in _split3(delta)]).astype(BF)

    blk = pl.BlockSpec((T, LANES), lambda i, hp: (i, hp))
    return pl.pallas_call(
        body, name=name, grid=(S // T, HP),
        in_specs=[blk, blk, pl.BlockSpec((T, LANES), lambda i, hp: (i, 3 * HP + hp))],
        out_specs=pl.BlockSpec((2, T, LANES), lambda i, hp: (hp, i, 0)),
        out_shape=jax.ShapeDtypeStruct((2 * HP, S, LANES), BF),
        compiler_params=_cp("parallel", "arbitrary"),
    )(dy, o, proj)


def _fox_bwd(imax, q2, ka, va, doa, *, name):
    H, S, _ = q2.shape
    B = _fox_block(S)
    nb = S // B

    def body(imax_ref, q_ref, do_ref, k_ref, v_ref, dq_ref, dk_ref, dv_ref, cs_ref):
        j = pl.program_id(1)
        end = imax_ref[pl.program_id(0), j] + 1

        @pl.when(j == 0)
        def _():
            dq_ref[...] = jnp.zeros_like(dq_ref)

        kb, vb = k_ref[...], v_ref[...]
        causal = lax.broadcasted_iota(jnp.int32, (B, B), 1) <= lax.broadcasted_iota(jnp.int32, (B, B), 0)

        def step(i, carry, masked=False):
            dk_acc, dv_acc, cs_acc = carry
            sl = pl.ds(pl.multiple_of(i * B, B), B)
            qb, dob = q_ref[sl, :], do_ref[sl, :]
            s = _dg(qb, kb, NT)
            if masked:
                s = jnp.where(causal, s, -jnp.inf)
            p = jnp.exp2(s)
            ds = p * _dg(dob, vb, NT)
            dsb = ds.astype(BF)
            cs_acc = cs_acc + jnp.sum(ds.reshape(B // 8, 8, B), axis=0)
            dv_acc = dv_acc + _dg(p.astype(BF), dob, TN)
            dk_acc = dk_acc + _dg(dsb, qb, TN)
            dq_ref[sl, :] += _dot(dsb, kb)
            return dk_acc, dv_acc, cs_acc

        zero = jnp.zeros((B, LANES), F32)
        carry = step(j, (zero, zero, jnp.zeros((8, B), F32)), masked=True)
        U = FOX_BWD_UNROLL
        ngroup = (end - 1 - j) // U

        def group(ii, c):
            for u in range(U):
                c = step(j + 1 + U * ii + u, c)
            return c

        carry = lax.fori_loop(0, ngroup, group, carry)
        dk_acc, dv_acc, cs_acc = lax.fori_loop(j + 1 + U * ngroup, end, step, carry)
        dk_ref[...] = dk_acc
        dv_ref[...] = dv_acc
        cs_ref[...] = jnp.sum(cs_acc, axis=0, keepdims=True)

    full = pl.BlockSpec((None, S, LANES), lambda h, j, im: (h, 0, 0))
    blk = pl.BlockSpec((None, B, LANES), lambda h, j, im: (h, j, 0))
    return pl.pallas_call(
        body, name=name,
        grid_spec=pltpu.PrefetchScalarGridSpec(
            num_scalar_prefetch=1, grid=(H, nb),
            in_specs=[full, full, blk, blk],
            out_specs=[full, blk, blk, pl.BlockSpec((None, 1, B), lambda h, j, im: (h, 0, j))]),
        out_shape=[jax.ShapeDtypeStruct((H, S, LANES), F32)] * 3 + [jax.ShapeDtypeStruct((H, 1, S), F32)],
        compiler_params=_cp("parallel", "arbitrary"),
    )(imax, q2, doa, ka, va)


def _fox_bwd_post(dqa, dka, dva, proj, dy, o, qw2, kw2, *, name):
    S, D = dy.shape
    HP = D // LANES
    T = _pick(S, 512, 16)

    def body(dq_ref, dk_ref, dv_ref, q_ref, k_ref, g_ref, dy_ref, o_ref, qw_ref, kw_ref, dp_ref, dqw_ref, dkw_ref):
        @pl.when((pl.program_id(0) == 0) & (pl.program_id(1) == 0))
        def _():
            dqw_ref[...] = jnp.zeros_like(dqw_ref)
            dkw_ref[...] = jnp.zeros_like(dkw_ref)

        lane = lax.broadcasted_iota(jnp.int32, (T, LANES), 1)
        lo = lane < FOX_DH

        def pair(ref):
            return jnp.where(lo, ref[0], pltpu.roll(ref[1], FOX_DH, 1))

        def norm_bwd(xv, w, dyn, dw_ref):
            r = lax.rsqrt(_pair_stats(xv * xv, lo) + EPS)
            xr = xv * r
            dw_ref[...] += jnp.sum(dyn * xr, axis=0, keepdims=True)
            u = dyn * w
            return r * (u - xr * _pair_stats(u * xr, lo))

        dp_ref[0] = norm_bwd(q_ref[...], qw_ref[...], pair(dq_ref) * 0.125, dqw_ref).astype(BF)
        dp_ref[1] = norm_bwd(k_ref[...], kw_ref[...], pair(dk_ref) * (1.0 / LOG2E), dkw_ref).astype(BF)
        dp_ref[2] = pair(dv_ref).astype(BF)
        sg = _sigmoid(g_ref[...])
        dp_ref[3] = (dy_ref[...] * o_ref[...] * sg * (1.0 - sg)).astype(BF)

    def part(p):
        return pl.BlockSpec((T, LANES), lambda i, hp: (i, p * HP + hp))

    aug = pl.BlockSpec((2, T, LANES), lambda i, hp: (hp, i, 0))
    blk = pl.BlockSpec((T, LANES), lambda i, hp: (i, hp))
    vec = pl.BlockSpec((1, LANES), lambda i, hp: (0, 0))
    return pl.pallas_call(
        body, name=name, grid=(S // T, HP),
        in_specs=[aug, aug, aug, part(0), part(1), part(3), blk, blk, vec, vec],
        out_specs=[pl.BlockSpec((4, T, LANES), lambda i, hp: (0, i, hp)), vec, vec],
        out_shape=[jax.ShapeDtypeStruct((5, S, D), BF), jax.ShapeDtypeStruct((1, LANES), F32),
                   jax.ShapeDtypeStruct((1, LANES), F32)],
        compiler_params=_cp("arbitrary", "arbitrary"),
    )(dqa, dka, dva, proj, proj, proj, dy, o, qw2, kw2)


def _fox_dfz(colsum, nheads, proj, bf_pad, dproj, *, name):
    S = colsum.shape[0]
    H = nheads
    D = dproj.shape[2]
    T = _pick(S, 256, 16)
    nb = S // T

    def body(cs_ref, fz_ref, b_ref, _, dp_ref, db_ref, carry):
        @pl.when(pl.program_id(0) == 0)
        def _():
            carry[...] = jnp.zeros_like(carry)
            db_ref[...] = jnp.zeros_like(db_ref)

        lane = lax.broadcasted_iota(jnp.int32, (T, LANES), 1)
        df = -cs_ref[...]
        triu = jnp.where(lax.broadcasted_iota(jnp.int32, (T, T), 0) <= lax.broadcasted_iota(jnp.int32, (T, T), 1),
                         1.0, 0.0).astype(BF)
        dlogf = _tri_dot(triu, df) + carry[...]
        carry[...] = _row_of(dlogf, lax.broadcasted_iota(jnp.int32, (T, LANES), 0), 0)
        dfz = jnp.where(lane < H, dlogf * _sigmoid(-(fz_ref[...] + b_ref[...])), 0.0)
        db_ref[...] += jnp.sum(dfz, axis=0, keepdims=True)
        dp_ref[...] = jnp.zeros_like(dp_ref)
        dp_ref[:, 0:LANES] = dfz.astype(BF)

    return pl.pallas_call(
        body, name=name, grid=(nb,),
        in_specs=[pl.BlockSpec((T, LANES), lambda i: (nb - 1 - i, 0)),
                  pl.BlockSpec((T, LANES), lambda i: (nb - 1 - i, 4 * D // LANES)),
                  pl.BlockSpec((1, LANES), lambda i: (0, 0)),
                  pl.BlockSpec(memory_space=pl.ANY)],
        out_specs=[pl.BlockSpec((None, T, D), lambda i: (4, nb - 1 - i, 0)), pl.BlockSpec((1, LANES), lambda i: (0, 0))],
        out_shape=[jax.ShapeDtypeStruct(dproj.shape, BF), jax.ShapeDtypeStruct((1, LANES), F32)],
        scratch_shapes=[pltpu.VMEM((1, LANES), F32)],
        input_output_aliases={3: 0},
        compiler_params=_cp("arbitrary"),
    )(colsum, proj, bf_pad, dproj)


def _mod_fwd(c16, w, b, *, name):
    L, D, N = w.shape
    tn = _pick(N, 512)

    def body(c_ref, w_ref, b_ref, o_ref):
        cv = c_ref[...]
        ca = (cv * _sigmoid(cv)).astype(BF)
        o_ref[...] = _dot(ca, w_ref[...].astype(BF)) + b_ref[...]

    return pl.pallas_call(
        body, name=name, grid=(L, N // tn),
        in_specs=[pl.BlockSpec((16, D), lambda l, j: (0, 0)), pl.BlockSpec((None, D, tn), lambda l, j: (l, 0, j)),
                  pl.BlockSpec((None, 1, tn), lambda l, j: (l, 0, j))],
        out_specs=pl.BlockSpec((None, 16, tn), lambda l, j: (l, 0, j)),
        out_shape=jax.ShapeDtypeStruct((L, 16, N), F32),
        compiler_params=_cp("parallel", "arbitrary"),
    )(c16, w, b)


def _mod_bwd(c16, dmod, *, name):
    L, _, N = dmod.shape
    D = c16.shape[1]
    tn = _pick(N, 512)

    def body(c_ref, d_ref, o_ref):
        cv = c_ref[...]
        ca = (cv * _sigmoid(cv)).astype(BF)
        o_ref[...] = _dg(ca, d_ref[...].astype(BF), TN)

    return pl.pallas_call(
        body, name=name, grid=(L, N // tn),
        in_specs=[pl.BlockSpec((16, D), lambda l, j: (0, 0)), pl.BlockSpec((None, 16, tn), lambda l, j: (l, 0, j))],
        out_specs=pl.BlockSpec((None, D, tn), lambda l, j: (l, 0, j)),
        out_shape=jax.ShapeDtypeStruct((L, D, N), F32),
        compiler_params=_cp("parallel", "arbitrary"),
    )(c16, dmod)


def _adamw_math(w, g, m, v):
    m = ADAM_B1 * m + (1.0 - ADAM_B1) * g
    v = ADAM_B2 * v + (1.0 - ADAM_B2) * (g * g)
    m_hat = m / (1.0 - ADAM_B1 ** ADAM_STEP)
    v_hat = v / (1.0 - ADAM_B2 ** ADAM_STEP)
    return -ADAM_LR * (m_hat / (jnp.sqrt(v_hat) + ADAM_EPS) + ADAM_WD * w), m, v


def _adamw(w, g, m, v, *, g_at=None, name):
    R, C = w.shape
    row0 = 0 if g_at is None else g_at[1]
    tr = min(math.gcd(row0, 256) if row0 else 256, -(-R // 8) * 8)
    g0 = row0 // tr
    if g_at is None:
        g_spec = pl.BlockSpec((tr, C), lambda i: (i, 0))
    else:
        g_spec = pl.BlockSpec((None, tr, C), lambda i: (g_at[0], g0 + i, 0))

    def body(w_ref, g_ref, m_ref, v_ref, d_ref, mo_ref, vo_ref):
        d, mn, vn = _adamw_math(w_ref[...], g_ref[...], m_ref[...], v_ref[...])
        d_ref[...] = d
        mo_ref[...] = mn
        vo_ref[...] = vn

    blk = pl.BlockSpec((tr, C), lambda i: (i, 0))
    return pl.pallas_call(
        body, name=name, grid=(pl.cdiv(R, tr),),
        in_specs=[blk, g_spec, blk, blk],
        out_specs=[blk, blk, blk],
        out_shape=[jax.ShapeDtypeStruct((R, C), F32)] * 3,
        compiler_params=_cp("parallel"),
    )(w, g, m, v)


def _sum_parts(parts, *, name):
    P, R, C = parts.shape

    def body(p_ref, o_ref):
        acc = p_ref[0]
        for p in range(1, P):
            acc = acc + p_ref[p]
        o_ref[...] = acc

    return pl.pallas_call(
        body, name=name, grid=(1,),
        in_specs=[pl.BlockSpec((P, R, C), lambda i: (0, 0, 0))],
        out_specs=pl.BlockSpec((R, C), lambda i: (0, 0)),
        out_shape=jax.ShapeDtypeStruct((R, C), F32),
        compiler_params=_cp("arbitrary"),
    )(parts)


def _add_halves(g4, recv, c_idx, *, name):
    _, _, Rh, C = g4.shape
    tr = _pick(Rh, 256, 16)

    def body(c_ref, a_ref, b_ref, o_ref):
        o_ref[...] = (a_ref[...] + b_ref[...].astype(F32)).astype(BF)

    return pl.pallas_call(
        body, name=name,
        grid_spec=pltpu.PrefetchScalarGridSpec(
            num_scalar_prefetch=1, grid=(4, pl.cdiv(Rh, tr)),
            in_specs=[pl.BlockSpec((None, None, tr, C), lambda j, r, c: (j, c[0], r, 0)),
                      pl.BlockSpec((None, tr, C), lambda j, r, c: (j, r, 0))],
            out_specs=pl.BlockSpec((None, tr, C), lambda j, r, c: (j, r, 0))),
        out_shape=jax.ShapeDtypeStruct((4, Rh, C), BF),
        compiler_params=_cp("parallel", "arbitrary"),
    )(c_idx, g4, recv)


def _add_four(g4, from_sibling, from_chips, pos, *, name):
    _, _, Rh, C = g4.shape
    tr = _pick(Rh, 256, 16)

    def body(p_ref, a_ref, s_ref, b_ref, o_ref):
        own = a_ref[...] + s_ref[...].astype(F32)
        o_ref[...] = ((own + b_ref[0].astype(F32)) + b_ref[1].astype(F32)) + b_ref[2].astype(F32)

    return pl.pallas_call(
        body, name=name,
        grid_spec=pltpu.PrefetchScalarGridSpec(
            num_scalar_prefetch=1, grid=(pl.cdiv(Rh, tr),),
            in_specs=[pl.BlockSpec((None, None, tr, C), lambda r, p: (p[0], p[1], r, 0)),
                      pl.BlockSpec((None, tr, C), lambda r, p: (p[0], r, 0)),
                      pl.BlockSpec((3, tr, C), lambda r, p: (0, r, 0))],
            out_specs=pl.BlockSpec((None, tr, C), lambda r, p: (p[1], r, 0))),
        out_shape=jax.ShapeDtypeStruct((2, Rh, C), F32),
        compiler_params=_cp("arbitrary"),
    )(pos, g4, from_sibling, from_chips)


HBM = pl.BlockSpec(memory_space=pltpu.HBM)


def _mesh_pos():
    return lax.axis_index("x"), lax.axis_index("y"), lax.axis_index("c")


def _other_chips(x, y):
    return [(1 - x, y), (x, 1 - y), (1 - x, 1 - y)]


def _allgather_small(xs, *, name):
    m_per, n = xs.shape

    def body(x_ref, out_ref, send_sems, recv_sems, local_sem):
        x, y, c = _mesh_pos()
        me, sibling = (x, y, c), (x, y, 1 - c)
        chips = _other_chips(x, y)

        def rows(px, py, pc):
            return out_ref.at[pl.ds((4 * px + 2 * py + pc) * m_per, m_per), :]

        def copy(k, block, to, src=None):
            return pltpu.make_async_remote_copy(
                src_ref=rows(*block) if src is None else src, dst_ref=rows(*block),
                send_sem=send_sems.at[k], recv_sem=recv_sems.at[k], device_id=to, device_id_type=MESH)

        mine = pltpu.make_async_copy(x_ref, rows(*me), local_sem)
        mine.start()
        first = [copy(0, me, sibling, src=x_ref)]
        first += [copy(1 + j, me, (*chip, c), src=x_ref) for j, chip in enumerate(chips)]
        for cp in first:
            cp.start()
        passed = [copy(4 + j, (*chip, c), sibling) for j, chip in enumerate(chips)]
        for j, chip in enumerate(chips):
            copy(1 + j, (*chip, c), me).wait_recv()
            passed[j].start()
        copy(0, sibling, me).wait_recv()
        for j, chip in enumerate(chips):
            copy(4 + j, (*chip, 1 - c), me).wait_recv()
        for cp in first + passed:
            cp.wait_send()
        mine.wait()

    return pl.pallas_call(
        body, name=name,
        out_shape=jax.ShapeDtypeStruct((N_DEV * m_per, n), xs.dtype),
        in_specs=[pl.BlockSpec(memory_space=pltpu.VMEM)],
        out_specs=pl.BlockSpec(memory_space=pltpu.VMEM),
        scratch_shapes=[pltpu.SemaphoreType.DMA((7,)), pltpu.SemaphoreType.DMA((7,)), pltpu.SemaphoreType.DMA],
    )(xs)


def _allgather_chip_slabs(slab, *, name):
    R, C = slab.shape
    Rh = R // 2

    def body(s_ref, out_ref, send_sems, recv_sems):
        x, y, c = _mesh_pos()
        sibling = (x, y, 1 - c)
        chips = _other_chips(x, y)

        def half(px, py, pc):
            return out_ref.at[2 * px + py, pl.ds(pc * Rh, Rh), :]

        def copy(k, block, to, src=None):
            return pltpu.make_async_remote_copy(
                src_ref=half(*block) if src is None else src, dst_ref=half(*block),
                send_sem=send_sems.at[k], recv_sem=recv_sems.at[k], device_id=to, device_id_type=MESH)

        first = [copy(j, (x, y, c), (*chip, c), src=s_ref.at[pl.ds(c * Rh, Rh), :]) for j, chip in enumerate(chips)]
        for cp in first:
            cp.start()
        passed = [copy(3 + j, (*chip, c), sibling) for j, chip in enumerate(chips)]
        for j, chip in enumerate(chips):
            copy(j, (*chip, c), (x, y, c)).wait_recv()
            passed[j].start()
        for j, chip in enumerate(chips):
            copy(3 + j, (*chip, 1 - c), (x, y, c)).wait_recv()
        for cp in first + passed:
            cp.wait_send()

    return pl.pallas_call(
        body, name=name,
        out_shape=jax.ShapeDtypeStruct((N_CHIPS, R, C), slab.dtype),
        in_specs=[HBM], out_specs=HBM,
        scratch_shapes=[pltpu.SemaphoreType.DMA((6,)), pltpu.SemaphoreType.DMA((6,))],
    )(slab)


def _swap_halves(g4, *, name):
    _, _, Rh, C = g4.shape

    def body(g_ref, out_ref, send_sems, recv_sems):
        x, y, c = _mesh_pos()
        copies = [pltpu.make_async_remote_copy(
            src_ref=g_ref.at[j, 1 - c], dst_ref=out_ref.at[j], send_sem=send_sems.at[j], recv_sem=recv_sems.at[j],
            device_id=(x, y, 1 - c), device_id_type=MESH) for j in range(N_CHIPS)]
        for cp in copies:
            cp.start()
        for cp in copies:
            cp.wait()

    return pl.pallas_call(
        body, name=name,
        out_shape=jax.ShapeDtypeStruct((N_CHIPS, Rh, C), g4.dtype),
        in_specs=[HBM], out_specs=HBM,
        scratch_shapes=[pltpu.SemaphoreType.DMA((N_CHIPS,)), pltpu.SemaphoreType.DMA((N_CHIPS,))],
    )(g4)


def _scatter_partials(part, *, name):
    _, Rh, C = part.shape

    def body(p_ref, out_ref, send_sems, recv_sems):
        x, y, c = _mesh_pos()
        copies = [pltpu.make_async_remote_copy(
            src_ref=p_ref.at[2 * px + py], dst_ref=out_ref.at[j], send_sem=send_sems.at[j], recv_sem=recv_sems.at[j],
            device_id=(px, py, c), device_id_type=MESH) for j, (px, py) in enumerate(_other_chips(x, y))]
        for cp in copies:
            cp.start()
        for cp in copies:
            cp.wait()

    return pl.pallas_call(
        body, name=name,
        out_shape=jax.ShapeDtypeStruct((3, Rh, C), part.dtype),
        in_specs=[HBM], out_specs=HBM,
        scratch_shapes=[pltpu.SemaphoreType.DMA((3,)), pltpu.SemaphoreType.DMA((3,))],
    )(part)


def _join_halves(buf, *, name):
    def body(b_ref, out_ref, send_sem, recv_sem):
        x, y, c = _mesh_pos()
        cp = pltpu.make_async_remote_copy(
            src_ref=b_ref.at[c], dst_ref=out_ref.at[c], send_sem=send_sem, recv_sem=recv_sem,
            device_id=(x, y, 1 - c), device_id_type=MESH)
        cp.start()
        cp.wait()

    return pl.pallas_call(
        body, name=name,
        out_shape=jax.ShapeDtypeStruct(buf.shape, buf.dtype),
        in_specs=[HBM], out_specs=HBM, input_output_aliases={0: 0},
        scratch_shapes=[pltpu.SemaphoreType.DMA, pltpu.SemaphoreType.DMA],
    )(buf)


def _pad_rows(a, mult):
    pad = (-a.shape[0]) % mult
    return a if pad == 0 else jnp.pad(a, ((0, pad),) + ((0, 0),) * (a.ndim - 1))


def _local_step(x, target, mod, wts, small):
    S, D = x.shape
    HP = D // LANES
    row = lambda v: v.reshape(1, -1)
    msplit = [[row(mod[i, k * D:(k + 1) * D]) for k in range(6)] for i in range(2)]
    gw, gs = {}, {}
    dmod = [[None] * 6 for _ in range(2)]

    sh1, sc1, g1, sh2, sc2, g2 = msplit[0]
    n1w0, n2w0 = row(small["norm1_w"][0]), row(small["norm2_w"][0])
    proj0, h1_0 = _ln_matmul(x, n1w0, sc1, sh1, wts["hg_w_in"], relu2=False, name="hg_in_proj")
    gn = small["hg_gn_w"].reshape(1, LANES)
    ypre0, o0, states = _hg_fwd(proj0, small["hg_lb"], gn, name="hg_fwd")
    x1, ymix0 = _matmul_resid(ypre0, wts["hg_w_out"], x, g1, name="hg_out_proj")
    a0, u0, h2_0 = _ln_matmul(x1, n2w0, sc2, sh2, wts["mlp_w1_0"], relu2=True, name="mlp0_up")
    x2, ymlp0 = _matmul_resid(u0, wts["mlp_w2_0"], x1, g2, name="mlp0_down")

    sh1b, sc1b, g1b, sh2b, sc2b, g2b = msplit[1]
    n1w1, n2w1 = row(small["norm1_w"][1]), row(small["norm2_w"][1])
    proj1, h1_1 = _ln_matmul(x2, n1w1, sc1b, sh1b, wts["fox_w_in"], relu2=False, name="fox_in_proj")
    nheads = 2 * HP
    bf_pad = jnp.pad(small["fox_b_f"].reshape(1, nheads), ((0, 0), (0, LANES - nheads)))
    qw2 = jnp.tile(small["fox_qn_w"].reshape(1, FOX_DH), (1, 2))
    kw2 = jnp.tile(small["fox_kn_w"].reshape(1, FOX_DH), (1, 2))
    fcum = _fox_cumsum(proj1, bf_pad, name="fox_cumsum")
    qa, ka, va = _fox_prep(proj1, fcum, qw2, kw2, name="fox_prep")
    jmin, imax = _fox_skip_bounds(fcum, small["fox_qn_w"], small["fox_kn_w"], nheads)
    ypre1, o1, q2 = _fox_fwd(jmin, qa, ka, va, proj1, name="fox_fwd")
    x3, ymix1 = _matmul_resid(ypre1, wts["fox_w_out"], x2, g1b, name="fox_out_proj")
    a1, u1, h2_1 = _ln_matmul(x3, n2w1, sc2b, sh2b, wts["mlp_w1_1"], relu2=True, name="mlp1_up")
    x4, ymlp1 = _matmul_resid(u1, wts["mlp_w2_1"], x3, g2b, name="mlp1_down")

    loss, dx4, dfw = _loss_kernel(x4, row(small["final_w"]), target, name="loss")
    gs["final_w"] = dfw.reshape(-1)

    def mlp_bwd(i, dx_out, x_in, h2, a, u, ymlp, n2w, sc2_, g2_):
        dz, dm, dg2 = _gate_matmul_nt(dx_out, g2_, ymlp, wts[f"mlp_w2_{i}"], a, name=f"mlp{i}_down_bwd")
        gw[f"mlp_w2_{i}"] = _matmul_tn(u, dm[None], name=f"mlp{i}_dw2")
        gw[f"mlp_w1_{i}"] = _matmul_tn(h2, dz[None], name=f"mlp{i}_dw1")
        dx_in, dsc, dsh, dnw = _matmul_nt_lnbwd(dz[None], wts[f"mlp_w1_{i}"], x_in, n2w, sc2_, dx_out,
                                                name=f"mlp{i}_up_bwd")
        dmod[i][3], dmod[i][4], dmod[i][5] = dsh, dsc, dg2
        return dx_in, dnw

    dx3, dn2w1 = mlp_bwd(1, dx4, x3, h2_1, a1, u1, ymlp1, n2w1, sc2b, g2b)
    dyp1, dm1, dg1b = _gate_matmul_nt(dx3, g1b, ymix1, wts["fox_w_out"], None, name="fox_out_bwd")
    gw["fox_w_out"] = _matmul_tn(ypre1, dm1[None], name="fox_dw_out")
    doa = _fox_bwd_prep(dyp1, o1, proj1, name="fox_bwd_prep")
    dqa, dka, dva, colsum = _fox_bwd(imax, q2, ka, va, doa, name="fox_bwd")
    colsum = jnp.pad(colsum[:, 0, :].T, ((0, 0), (0, LANES - nheads)))
    dproj1, dqw, dkw = _fox_bwd_post(dqa, dka, dva, proj1, dyp1, o1, qw2, kw2, name="fox_bwd_post")
    dproj1, dbf = _fox_dfz(colsum, nheads, proj1, bf_pad, dproj1, name="fox_dfz")
    gw["fox_w_in"] = _matmul_tn(h1_1, dproj1, name="fox_dw_in")
    dx2, dsc, dsh, dn1w1 = _matmul_nt_lnbwd(dproj1, wts["fox_w_in"], x2, n1w1, sc1b, dx3, name="fox_in_bwd")
    dmod[1][0], dmod[1][1], dmod[1][2] = dsh, dsc, dg1b
    gs["fox_qn_w"] = dqw[0, :FOX_DH] + dqw[0, FOX_DH:]
    gs["fox_kn_w"] = dkw[0, :FOX_DH] + dkw[0, FOX_DH:]
    gs["fox_b_f"] = dbf[0, :nheads]

    dx1, dn2w0 = mlp_bwd(0, dx2, x1, h2_0, a0, u0, ymlp0, n2w0, sc2, g2)
    dyp0, dm0, dg1 = _gate_matmul_nt(dx1, g1, ymix0, wts["hg_w_out"], None, name="hg_out_bwd")
    gw["hg_w_out"] = _matmul_tn(ypre0, dm0[None], name="hg_dw_out")
    dproj0, dlb, dgn = _hg_bwd(proj0, small["hg_lb"], gn, o0, states, dyp0, name="hg_bwd")
    gw["hg_w_in"] = _matmul_tn(h1_0, dproj0, name="hg_dw_in")
    dx0, dsc, dsh, dn1w0 = _matmul_nt_lnbwd(dproj0, wts["hg_w_in"], x, n1w0, sc1, dx1, name="hg_in_bwd")
    dmod[0][0], dmod[0][1], dmod[0][2] = dsh, dsc, dg1
    gs["hg_lb"] = dlb
    gs["hg_gn_w"] = jnp.sum(dgn, axis=0)

    gs["norm1_w"] = jnp.concatenate([dn1w0, dn1w1], axis=0)
    gs["norm2_w"] = jnp.concatenate([dn2w0, dn2w1], axis=0)
    gs["dmod"] = jnp.stack([jnp.concatenate(dmod[i], axis=1)[0] for i in range(2)])
    return loss, dx0, gw, gs


SMALL_NAMES = ["norm1_w", "norm2_w", "hg_lb", "hg_gn_w", "fox_b_f", "fox_qn_w", "fox_kn_w", "final_w"]


def _pack_small(d, names):
    rows, offs, r0 = [], {}, 0
    for n in names:
        flat = d[n].reshape(-1)
        nr = -(-flat.shape[0] // LANES)
        rows.append(jnp.pad(flat, (0, nr * LANES - flat.shape[0])).reshape(nr, LANES))
        offs[n] = (r0, nr)
        r0 += nr
    return jnp.concatenate(rows, axis=0), offs


def _unpack_small(packed, offs, name, like):
    r0, nr = offs[name]
    return packed[r0:r0 + nr].reshape(-1)[:like.size].reshape(like.shape)


def kernel(x, c, w_mod, b_mod, norm1_w, norm2_w, hg_w_in, hg_w_out, hg_lb, hg_gn_w, fox_w_in, fox_b_f, fox_qn_w, fox_kn_w, fox_w_out, mlp_w1, mlp_w2, final_w, loss_target, m_w_mod, m_b_mod, m_norm1_w, m_norm2_w, m_hg_w_in, m_hg_w_out, m_hg_lb, m_hg_gn_w, m_fox_w_in, m_fox_b_f, m_fox_qn_w, m_fox_kn_w, m_fox_w_out, m_mlp_w1, m_mlp_w2, m_final_w, v_w_mod, v_b_mod, v_norm1_w, v_norm2_w, v_hg_w_in, v_hg_w_out, v_hg_lb, v_hg_gn_w, v_fox_w_in, v_fox_b_f, v_fox_qn_w, v_fox_kn_w, v_fox_w_out, v_mlp_w1, v_mlp_w2, v_final_w):
    S, D = x.shape[1], x.shape[2]
    nheads = D // FOX_DH
    ax, ay, ac = _mesh_pos()
    chip = 2 * ax + ay
    dev = 2 * chip + ac
    xs, tgt = x.reshape(S, D), loss_target.reshape(S, D)

    c_all = _allgather_small(_pad_rows(c.reshape(-1, LANES), 8), name="gather_c")
    c_all = c_all.reshape(N_DEV, -1)[:, :D]
    c16 = _pad_rows(c_all, 16)
    nmod = w_mod.shape[2]
    b_shard = lax.dynamic_slice_in_dim(b_mod, chip * nmod, nmod, axis=1)
    mod_shard = _mod_fwd(c16, w_mod, b_shard[:, None, :], name="mod_fwd")[:, :N_DEV]
    mod_all = _allgather_small(mod_shard.reshape(-1, LANES), name="gather_mod")
    mod_all = mod_all.reshape(N_CHIPS, 2, 2, N_DEV, nmod)[:, 0]
    mod = lax.dynamic_index_in_dim(mod_all, dev, axis=2, keepdims=False)
    mod = mod.transpose(1, 0, 2).reshape(2, N_CHIPS * nmod)

    fox_rows = fox_w_in.shape[2]
    layout = [[("hg_w_in", hg_w_in[0]), ("mlp_w1", mlp_w1.reshape(2 * D, D)), ("hg_w_out", hg_w_out[0])],
              [("mlp_w2", mlp_w2.reshape(2 * D, D)), ("fox_w_out", fox_w_out[0]),
               ("fox_w_in", fox_w_in[0].reshape(fox_rows, D))]]
    Rh = -(-max(sum(a.shape[0] for _, a in half) for half in layout) // 16) * 16
    place, parts = {}, []
    for h, half in enumerate(layout):
        off = 0
        for n, a in half:
            place[n] = (h, off, a.shape[0])
            off += a.shape[0]
        parts.append(jnp.pad(jnp.concatenate([a.astype(BF) for _, a in half], axis=0), ((0, Rh - off), (0, 0))))
    slab = jnp.concatenate(parts, axis=0)
    R = 2 * Rh
    gathered = _allgather_chip_slabs(slab, name="gather_weights")
    gathered = lax.dynamic_update_index_in_dim(gathered, slab, chip, 0)

    def seg(n):
        h, off, rows = place[n]
        return gathered[:, h * Rh + off:h * Rh + off + rows, :]

    col = lambda g: g.transpose(1, 0, 2).reshape(g.shape[1], -1)
    rowsh = lambda g: g.reshape(-1, g.shape[2])
    w1 = seg("mlp_w1").reshape(N_CHIPS, 2, D, D)
    w2 = seg("mlp_w2").reshape(N_CHIPS, 2, D, D)
    fox_in = col(seg("fox_w_in").reshape(N_CHIPS, D, fox_rows))
    wts = {
        "hg_w_in": col(seg("hg_w_in")), "hg_w_out": rowsh(seg("hg_w_out")), "fox_w_out": rowsh(seg("fox_w_out")),
        "mlp_w1_0": col(w1[:, 0]), "mlp_w1_1": col(w1[:, 1]), "mlp_w2_0": rowsh(w2[:, 0]), "mlp_w2_1": rowsh(w2[:, 1]),
        "fox_w_in": jnp.pad(fox_in, ((0, 0), (0, 5 * D - fox_in.shape[1]))),
    }
    small = {"norm1_w": norm1_w, "norm2_w": norm2_w, "hg_lb": hg_lb, "hg_gn_w": hg_gn_w, "fox_b_f": fox_b_f,
             "fox_qn_w": fox_qn_w, "fox_kn_w": fox_kn_w, "final_w": final_w}

    loss_part, grad_x, gw, gs = _local_step(xs, tgt, mod, wts, small)
    loss = lax.psum(loss_part[0, 0], ("x", "y", "c"))

    def uncol(g, n):
        return g.reshape(g.shape[0], N_CHIPS, n).transpose(1, 0, 2)

    gseg = {
        "hg_w_in": uncol(gw["hg_w_in"], D), "hg_w_out": gw["hg_w_out"].reshape(N_CHIPS, D // 4, D),
        "fox_w_out": gw["fox_w_out"].reshape(N_CHIPS, D // 4, D),
        "mlp_w1": jnp.concatenate([uncol(gw["mlp_w1_0"], D), uncol(gw["mlp_w1_1"], D)], axis=1),
        "mlp_w2": jnp.concatenate([gw["mlp_w2_0"].reshape(N_CHIPS, D, D), gw["mlp_w2_1"].reshape(N_CHIPS, D, D)], axis=1),
        "fox_w_in": uncol(gw["fox_w_in"][:, :4 * fox_rows], fox_rows).reshape(N_CHIPS, fox_rows, D),
    }
    ghalves = []
    for half in layout:
        gh = jnp.concatenate([gseg[n] for n, _ in half], axis=1)
        ghalves.append(jnp.pad(gh, ((0, 0), (0, Rh - gh.shape[1]), (0, 0))))
    g4 = jnp.stack(ghalves, axis=1)
    from_sibling = _swap_halves(g4.astype(BF), name="rs_swap_halves")
    chip_part = _add_halves(g4, from_sibling, ac.reshape(1), name="rs_add_halves")
    from_chips = _scatter_partials(chip_part, name="rs_scatter")
    my_half = _add_four(g4, from_sibling, from_chips, jnp.stack([chip, ac]), name="rs_add_chips")
    gshard = _join_halves(my_half, name="rs_join")

    names = ["dmod"] + SMALL_NAMES
    packed, offs = _pack_small(gs, names)
    packed = _pad_rows(packed, 8)
    rp = packed.shape[0]
    parts = _allgather_small(packed, name="gather_small").reshape(N_DEV, rp, LANES)
    total = _sum_parts(parts, name="sum_small")
    r0, nr = offs["dmod"]
    dmod_all = parts[:, r0:r0 + nr].reshape(N_DEV, 2, N_CHIPS * nmod)
    dmod_shard = lax.dynamic_slice_in_dim(dmod_all, chip * nmod, nmod, axis=2).transpose(1, 0, 2)
    g_w_mod = _mod_bwd(c16, jnp.pad(dmod_shard, ((0, 0), (0, 16 - N_DEV), (0, 0))), name="mod_bwd")

    grads = {"w_mod": g_w_mod, "b_mod": _unpack_small(total, offs, "dmod", b_mod)}
    for n in SMALL_NAMES:
        grads[n] = _unpack_small(total, offs, n, small[n])

    given = dict(w_mod=(w_mod, m_w_mod, v_w_mod), b_mod=(b_mod, m_b_mod, v_b_mod), norm1_w=(norm1_w, m_norm1_w, v_norm1_w),
                 norm2_w=(norm2_w, m_norm2_w, v_norm2_w), hg_w_in=(hg_w_in, m_hg_w_in, v_hg_w_in),
                 hg_w_out=(hg_w_out, m_hg_w_out, v_hg_w_out), hg_lb=(hg_lb, m_hg_lb, v_hg_lb),
                 hg_gn_w=(hg_gn_w, m_hg_gn_w, v_hg_gn_w), fox_w_in=(fox_w_in, m_fox_w_in, v_fox_w_in),
                 fox_b_f=(fox_b_f, m_fox_b_f, v_fox_b_f), fox_qn_w=(fox_qn_w, m_fox_qn_w, v_fox_qn_w),
                 fox_kn_w=(fox_kn_w, m_fox_kn_w, v_fox_kn_w), fox_w_out=(fox_w_out, m_fox_w_out, v_fox_w_out),
                 mlp_w1=(mlp_w1, m_mlp_w1, v_mlp_w1), mlp_w2=(mlp_w2, m_mlp_w2, v_mlp_w2), final_w=(final_w, m_final_w, v_final_w))
    upd = {}

    for n, (h, off, rows) in place.items():
        w, m, v = given[n]
        flat = lambda a: a.reshape(rows, D)
        d, mn, vn = _adamw(flat(w), gshard, flat(m), flat(v), g_at=(h, off), name=f"adamw_{n}")
        grads[n] = gshard[h, off:off + rows].reshape(w.shape)
        upd[n] = tuple(a.reshape(w.shape) for a in (d, mn, vn))

    w, m, v = given["w_mod"]
    flat = lambda a: a.reshape(-1, nmod)
    upd["w_mod"] = tuple(a.reshape(w.shape) for a in _adamw(flat(w), flat(g_w_mod), flat(m), flat(v), name="adamw_w_mod"))

    snames = ["b_mod"] + SMALL_NAMES
    pw, soffs = _pack_small({n: given[n][0] for n in snames}, snames)
    pm, _ = _pack_small({n: given[n][1] for n in snames}, snames)
    pv, _ = _pack_small({n: given[n][2] for n in snames}, snames)
    pg, _ = _pack_small({n: grads[n] for n in snames}, snames)
    pw, pm, pv, pg = (_pad_rows(a, 8) for a in (pw, pm, pv, pg))
    sd, smn, svn = _adamw(pw, pg, pm, pv, name="adamw_small")
    for n in snames:
        like = given[n][0]
        upd[n] = tuple(_unpack_small(a, soffs, n, like) for a in (sd, smn, svn))

    order = ["w_mod", "b_mod", "norm1_w", "norm2_w", "hg_w_in", "hg_w_out", "hg_lb", "hg_gn_w", "fox_w_in", "fox_b_f",
             "fox_qn_w", "fox_kn_w", "fox_w_out", "mlp_w1", "mlp_w2", "final_w"]
    return (loss, grad_x.reshape(x.shape), *[grads[n] for n in order], *[upd[n][0] for n in order],
            *[upd[n][1] for n in order], *[upd[n][2] for n in order])
```

```python
import math

import jax
import jax.numpy as jnp
from jax import lax
from jax.experimental import pallas as pl
from jax.experimental.pallas import tpu as pltpu

EPS = 1e-6
ADAM_LR, ADAM_B1, ADAM_B2, ADAM_EPS, ADAM_WD, ADAM_STEP = 0.001, 0.9, 0.999, 1e-08, 0.01, 10

F32 = jnp.float32
BF = jnp.bfloat16
LANES = 128
HG_CHUNK = 64
HG_HEADS_PER_STEP = 8
HG_TOKENS_PER_STEP = 256
FOX_BWD_TILES = (8, 4, 2, 1)
LOG2E = 1.4426950408889634
FOX_DH = 64
N_CHIPS = 4
N_DEV = 8
VMEM_LIMIT = 48 * 1024 * 1024
MESH = pl.DeviceIdType.MESH

NT = (((1,), (1,)), ((), ()))
TN = (((0,), (0,)), ((), ()))


def _pick(n, pref, mult=LANES):
    if n <= pref:
        return n
    t = (pref // mult) * mult
    while t >= mult:
        if n % t == 0:
            return t
        t -= mult
    raise ValueError((n, pref, mult))


def _cp(*sem):
    return pltpu.CompilerParams(dimension_semantics=sem, vmem_limit_bytes=VMEM_LIMIT)


def _dot(a, b):
    return jnp.dot(a, b, preferred_element_type=F32)


def _dg(a, b, dims):
    return lax.dot_general(a, b, dims, preferred_element_type=F32)


def _split3(x):
    hi = x.astype(BF)
    r1 = x - hi.astype(F32)
    mid = r1.astype(BF)
    lo = (r1 - mid.astype(F32)).astype(BF)
    return hi, mid, lo


def _tri_dot(tri, x):
    hi, mid, lo = _split3(x)
    return _dot(tri, hi) + _dot(tri, mid) + _dot(tri, lo)


def _dg3(a, b, dims):
    ah, bh = a.astype(BF), b.astype(BF)
    al, bl = (a - ah.astype(F32)).astype(BF), (b - bh.astype(F32)).astype(BF)
    return _dg(ah, bh, dims) + _dg(ah, bl, dims) + _dg(al, bh, dims)


NN = (((1,), (0,)), ((), ()))


def _sigmoid(x):
    return jax.nn.sigmoid(x)


def _ln_matmul(x, nw, sc, sh, w, *, relu2, name):
    S, D = x.shape
    N = w.shape[1]
    tm, tn = _pick(S, 1024, 16), _pick(N, 1024)

    def body(x_ref, nw_ref, sc_ref, sh_ref, w_ref, *rest):
        outs, hs = rest[:-1], rest[-1]
        h_ref = outs[-1]

        @pl.when(pl.program_id(1) == 0)
        def _():
            xv = x_ref[...]
            r = lax.rsqrt(jnp.mean(xv * xv, axis=-1, keepdims=True) + EPS)
            hb = ((xv * r * nw_ref[...]) * (1.0 + sc_ref[...]) + sh_ref[...]).astype(BF)
            hs[...] = hb
            h_ref[...] = hb

        z = _dot(hs[...], w_ref[...])
        if relu2:
            a = jnp.maximum(z, 0.0)
            outs[0][...] = a.astype(BF)
            outs[1][...] = (a * a).astype(BF)
        else:
            outs[0][...] = z

    vec = pl.BlockSpec((1, D), lambda i, j: (0, 0))
    tile = pl.BlockSpec((tm, tn), lambda i, j: (i, j))
    if relu2:
        out_shape = [jax.ShapeDtypeStruct((S, N), BF), jax.ShapeDtypeStruct((S, N), BF)]
        out_specs = [tile, tile]
    else:
        out_shape = [jax.ShapeDtypeStruct((S, N), F32)]
        out_specs = [tile]
    out_shape.append(jax.ShapeDtypeStruct((S, D), BF))
    out_specs.append(pl.BlockSpec((tm, D), lambda i, j: (i, 0)))
    return pl.pallas_call(
        body, name=name, grid=(S // tm, N // tn),
        in_specs=[pl.BlockSpec((tm, D), lambda i, j: (i, 0)), vec, vec, vec,
                  pl.BlockSpec((D, tn), lambda i, j: (0, j))],
        out_specs=out_specs, out_shape=out_shape,
        scratch_shapes=[pltpu.VMEM((tm, D), BF)],
        compiler_params=_cp("parallel", "arbitrary"),
    )(x, nw, sc, sh, w)


def _matmul_resid(a, w, x, gate, *, name):
    S, K = a.shape
    D = w.shape[1]
    big = 1024 if K <= 1024 else 512
    tm, tn = _pick(S, big, 16), _pick(D, big)

    def body(a_ref, w_ref, x_ref, g_ref, o_ref, y_ref):
        y = _dot(a_ref[...], w_ref[...])
        y_ref[...] = y.astype(BF)
        o_ref[...] = x_ref[...] + g_ref[...] * y

    tile = pl.BlockSpec((tm, tn), lambda i, j: (i, j))
    return pl.pallas_call(
        body, name=name, grid=(S // tm, D // tn),
        in_specs=[pl.BlockSpec((tm, K), lambda i, j: (i, 0)), pl.BlockSpec((K, tn), lambda i, j: (0, j)),
                  tile, pl.BlockSpec((1, tn), lambda i, j: (0, j))],
        out_specs=[tile, tile],
        out_shape=[jax.ShapeDtypeStruct((S, D), F32), jax.ShapeDtypeStruct((S, D), BF)],
        compiler_params=_cp("parallel", "arbitrary"),
    )(a, w, x, gate)


def _gate_matmul_nt(dx, gate, y, w, act, *, name):
    S, D = dx.shape
    K = w.shape[0]
    tm, tn = _pick(S, 1024, 16), _pick(K, 1024)
    fused = act is not None

    def body(dx_ref, g_ref, y_ref, w_ref, *rest):
        if fused:
            act_ref, da_ref, dm_ref, dg_ref, ms = rest
        else:
            da_ref, dm_ref, dg_ref, ms = rest
        i, j = pl.program_id(0), pl.program_id(1)

        @pl.when((i == 0) & (j == 0))
        def _():
            dg_ref[...] = jnp.zeros_like(dg_ref)

        @pl.when(j == 0)
        def _():
            dxv = dx_ref[...]
            dmb = (dxv * g_ref[...]).astype(BF)
            ms[...] = dmb
            dm_ref[...] = dmb
            dg_ref[...] += jnp.sum(dxv * y_ref[...].astype(F32), axis=0, keepdims=True)

        da = _dg(ms[...], w_ref[...], NT)
        if fused:
            da_ref[...] = (da * (2.0 * act_ref[...].astype(F32))).astype(BF)
        else:
            da_ref[...] = da

    row = pl.BlockSpec((tm, D), lambda i, j: (i, 0))
    vec = pl.BlockSpec((1, D), lambda i, j: (0, 0))
    tile = pl.BlockSpec((tm, tn), lambda i, j: (i, j))
    in_specs = [row, vec, row, pl.BlockSpec((tn, D), lambda i, j: (j, 0))]
    args = [dx, gate, y, w]
    if fused:
        in_specs.append(tile)
        args.append(act)
    return pl.pallas_call(
        body, name=name, grid=(S // tm, K // tn),
        in_specs=in_specs, out_specs=[tile, row, vec],
        out_shape=[jax.ShapeDtypeStruct((S, K), BF if fused else F32), jax.ShapeDtypeStruct((S, D), BF),
                   jax.ShapeDtypeStruct((1, D), F32)],
        scratch_shapes=[pltpu.VMEM((tm, D), BF)],
        compiler_params=_cp("arbitrary", "arbitrary"),
    )(*args)


def _matmul_tn(a, b, *, name):
    S, Ka = a.shape
    P, _, Db = b.shape
    tk, tn, ts = _pick(Ka, 1024), _pick(Db, 1024), _pick(S, 1024, 16)
    npb = Db // tn

    def body(a_ref, b_ref, o_ref, acc):
        s = pl.program_id(2)

        @pl.when(s == 0)
        def _():
            acc[...] = jnp.zeros_like(acc)

        acc[...] += _dg(a_ref[...], b_ref[...], TN)

        @pl.when(s == pl.num_programs(2) - 1)
        def _():
            o_ref[...] = acc[...]

    return pl.pallas_call(
        body, name=name, grid=(Ka // tk, P * npb, S // ts),
        in_specs=[pl.BlockSpec((ts, tk), lambda i, j, s: (s, i)),
                  pl.BlockSpec((None, ts, tn), lambda i, j, s: (j // npb, s, j % npb))],
        out_specs=pl.BlockSpec((tk, tn), lambda i, j, s: (i, j)),
        out_shape=jax.ShapeDtypeStruct((Ka, P * Db), F32),
        scratch_shapes=[pltpu.VMEM((tk, tn), F32)],
        compiler_params=_cp("parallel", "parallel", "arbitrary"),
    )(a, b)


def _matmul_nt_lnbwd(g, w, x, nw, sc, dx_out, *, name):
    P, S, Dg = g.shape
    D = x.shape[1]
    tm, tk = _pick(S, 1024, 16), _pick(Dg, 1024)
    npb = Dg // tk
    nk = P * npb

    def body(g_ref, w_ref, x_ref, nw_ref, sc_ref, dxo_ref, dx_ref, dsc_ref, dsh_ref, dnw_ref, acc):
        i, k = pl.program_id(0), pl.program_id(1)

        @pl.when((i == 0) & (k == 0))
        def _():
            dsc_ref[...] = jnp.zeros_like(dsc_ref)
            dsh_ref[...] = jnp.zeros_like(dsh_ref)
            dnw_ref[...] = jnp.zeros_like(dnw_ref)

        @pl.when(k == 0)
        def _():
            acc[...] = jnp.zeros_like(acc)

        acc[...] += _dg(g_ref[...], w_ref[...], NT)

        @pl.when(k == nk - 1)
        def _():
            dh = acc[...]
            xv = x_ref[...]
            nwv = nw_ref[...]
            r = lax.rsqrt(jnp.mean(xv * xv, axis=-1, keepdims=True) + EPS)
            xr = xv * r
            dn = dh * (1.0 + sc_ref[...])
            dsc_ref[...] += jnp.sum(dh * (xr * nwv), axis=0, keepdims=True)
            dsh_ref[...] += jnp.sum(dh, axis=0, keepdims=True)
            dnw_ref[...] += jnp.sum(dn * xr, axis=0, keepdims=True)
            u = dn * nwv
            dx_ref[...] = dxo_ref[...] + r * (u - xr * jnp.mean(u * xr, axis=-1, keepdims=True))

    row = pl.BlockSpec((tm, D), lambda i, k: (i, 0))
    vec = pl.BlockSpec((1, D), lambda i, k: (0, 0))
    return pl.pallas_call(
        body, name=name, grid=(S // tm, nk),
        in_specs=[pl.BlockSpec((None, tm, tk), lambda i, k: (k // npb, i, k % npb)),
                  pl.BlockSpec((D, tk), lambda i, k: (0, k)), row, vec, vec, row],
        out_specs=[row, vec, vec, vec],
        out_shape=[jax.ShapeDtypeStruct((S, D), F32)] + [jax.ShapeDtypeStruct((1, D), F32)] * 3,
        scratch_shapes=[pltpu.VMEM((tm, D), F32)],
        compiler_params=_cp("arbitrary", "arbitrary"),
    )(g, w, x, nw, sc, dx_out)


def _loss_kernel(x, fw, tgt, *, name):
    S, D = x.shape
    tm = _pick(S, 512, 8)

    def body(x_ref, fw_ref, t_ref, l_ref, dx_ref, dfw_ref):
        @pl.when(pl.program_id(0) == 0)
        def _():
            l_ref[...] = jnp.zeros_like(l_ref)
            dfw_ref[...] = jnp.zeros_like(dfw_ref)

        xv = x_ref[...]
        fwv = fw_ref[...]
        r = lax.rsqrt(jnp.mean(xv * xv, axis=-1, keepdims=True) + EPS)
        xr = xv * r
        err = xr * fwv - t_ref[...]
        per_tok = jnp.mean(err * err, axis=-1, keepdims=True)
        l_ref[...] += 0.5 * jnp.sum(per_tok, axis=0, keepdims=True)
        dy = err * (1.0 / D)
        dfw_ref[...] += jnp.sum(dy * xr, axis=0, keepdims=True)
        u = dy * fwv
        dx_ref[...] = r * (u - xr * jnp.mean(u * xr, axis=-1, keepdims=True))

    row = pl.BlockSpec((tm, D), lambda i: (i, 0))
    vec = pl.BlockSpec((1, D), lambda i: (0, 0))
    return pl.pallas_call(
        body, name=name, grid=(S // tm,),
        in_specs=[row, vec, row],
        out_specs=[pl.BlockSpec((1, LANES), lambda i: (0, 0)), row, vec],
        out_shape=[jax.ShapeDtypeStruct((1, LANES), F32), jax.ShapeDtypeStruct((S, D), F32),
                   jax.ShapeDtypeStruct((1, D), F32)],
        compiler_params=_cp("arbitrary"),
    )(x, fw, tgt)


def _hg_lower_bound(lb3):
    mx = jnp.max(lb3, axis=0, keepdims=True)
    e = jnp.exp(lb3 - mx)
    p = e / jnp.sum(e, axis=0, keepdims=True)
    return p[0:1, :], p


def _hg_chunk_common(qr, fz, lbv):
    sq = _sigmoid(qr)
    q = qr * sq
    sig = _sigmoid(fz)
    f = lbv + (1.0 - lbv) * sig
    k = (1.0 - lbv) * (1.0 - sig)
    return q, sq, sig, f, k, jnp.log(f)


def _row_of(x, rows, r):
    return jnp.sum(jnp.where(rows == r, x, 0.0), axis=0, keepdims=True)


def _hg_fwd(proj, hg_lb, gn, *, name):
    S = proj.shape[0]
    D = proj.shape[1] // 4
    H = D // LANES
    HB = min(HG_HEADS_PER_STEP, H)
    W = HB * LANES
    C = HG_CHUNK
    T = _pick(S, HG_TOKENS_PER_STEP, C)
    nch, nb = T // C, S // T

    def body(q_ref, fz_ref, v_ref, g_ref, lb_ref, gn_ref, y_ref, o_ref, sts_ref, st):
        @pl.when(pl.program_id(1) == 0)
        def _():
            st[...] = jnp.zeros_like(st)

        lb_all, _ = _hg_lower_bound(lb_ref[...])
        gnv = gn_ref[...]
        ri = lax.broadcasted_iota(jnp.int32, (C, C), 0)
        ci_ = lax.broadcasted_iota(jnp.int32, (C, C), 1)
        low = ri >= ci_
        tri = jnp.where(low, 1.0, 0.0).astype(BF)
        rows = lax.broadcasted_iota(jnp.int32, (C, LANES), 0)

        def chunk(ci, carry):
            sl = pl.ds(pl.multiple_of(ci * C, C), C)
            for hh in range(HB):
                ls = slice(hh * LANES, (hh + 1) * LANES)
                q, _, _, _, k, logf = _hg_chunk_common(q_ref[sl, ls], fz_ref[sl, ls], lb_all[:, ls])
                vv = v_ref[sl, ls]
                gg = g_ref[sl, ls]
                G = _tri_dot(tri, logf)
                Gm = _row_of(G, rows, C // 2 - 1)
                Gl = _row_of(G, rows, C - 1)
                qt = q * jnp.exp(G - Gm)
                kt = k * jnp.exp(Gm - G)
                A = jnp.where(low, _dg3(qt, kt, NT), 0.0)
                Sv = st[hh]
                sts_ref[hh, ci] = Sv
                o = _dg3(A, vv, NN) + _dg3(q * jnp.exp(G), Sv, NT)
                st[hh] = Sv * jnp.exp(Gl) + _dg3(vv, k * jnp.exp(Gl - G), TN)
                r = lax.rsqrt(jnp.mean(o * o, axis=-1, keepdims=True) + EPS)
                y_ref[sl, ls] = ((o * r * gnv) * (gg * _sigmoid(gg))).astype(BF)
                o_ref[sl, ls] = o
            return carry

        lax.fori_loop(0, nch, chunk, 0)

    ng = H // HB

    def part(p):
        return pl.BlockSpec((T, W), lambda h, n: (n, p * ng + h))

    blk = pl.BlockSpec((T, W), lambda h, n: (n, h))
    return pl.pallas_call(
        body, name=name, grid=(ng, nb),
        in_specs=[part(0), part(1), part(2), part(3),
                  pl.BlockSpec((3, W), lambda h, n: (0, h)), pl.BlockSpec((1, LANES), lambda h, n: (0, 0))],
        out_specs=[blk, blk, pl.BlockSpec((HB, nch, LANES, LANES), lambda h, n: (h, n, 0, 0))],
        out_shape=[jax.ShapeDtypeStruct((S, D), BF), jax.ShapeDtypeStruct((S, D), F32),
                   jax.ShapeDtypeStruct((H, S // C, LANES, LANES), F32)],
        scratch_shapes=[pltpu.VMEM((HB, LANES, LANES), F32)],
        compiler_params=_cp("parallel", "arbitrary"),
    )(proj, proj, proj, proj, hg_lb, gn)


def _hg_bwd(proj, hg_lb, gn, o_all, states, dy, *, name):
    S = proj.shape[0]
    D = proj.shape[1] // 4
    H = D // LANES
    HB = min(HG_HEADS_PER_STEP, H)
    W = HB * LANES
    C = HG_CHUNK
    T = _pick(S, HG_TOKENS_PER_STEP, C)
    nch, nb = T // C, S // T

    def body(q_ref, fz_ref, v_ref, g_ref, lb_ref, gn_ref, o_ref, sts_ref, dy_ref,
             dp_ref, dlb_ref, dgn_ref, dst, dlb_acc):
        n = pl.program_id(1)

        @pl.when(n == 0)
        def _():
            dst[...] = jnp.zeros_like(dst)
            dlb_acc[...] = jnp.zeros_like(dlb_acc)
            dgn_ref[...] = jnp.zeros_like(dgn_ref)

        lb_all, p3 = _hg_lower_bound(lb_ref[...])
        gnv = gn_ref[...]
        ri = lax.broadcasted_iota(jnp.int32, (C, C), 0)
        ci_ = lax.broadcasted_iota(jnp.int32, (C, C), 1)
        low = ri >= ci_
        tri = jnp.where(low, 1.0, 0.0).astype(BF)
        triu = jnp.where(ri <= ci_, 1.0, 0.0).astype(BF)
        rows = lax.broadcasted_iota(jnp.int32, (C, LANES), 0)

        def chunk(cj, carry):
            ci = nch - 1 - cj
            sl = pl.ds(pl.multiple_of(ci * C, C), C)
            for hh in range(HB):
                ls = slice(hh * LANES, (hh + 1) * LANES)
                lbv = lb_all[:, ls]
                qr = q_ref[sl, ls]
                q, sq, sig, f, k, logf = _hg_chunk_common(qr, fz_ref[sl, ls], lbv)
                vv = v_ref[sl, ls]
                gg = g_ref[sl, ls]
                o = o_ref[sl, ls]
                dyv = dy_ref[sl, ls]
                G = _tri_dot(tri, logf)
                Gm = _row_of(G, rows, C // 2 - 1)
                Gl = _row_of(G, rows, C - 1)
                eG, e_qm, e_km, e_lk, eGl = jnp.exp(G), jnp.exp(G - Gm), jnp.exp(Gm - G), jnp.exp(Gl - G), jnp.exp(Gl)
                qt = q * e_qm
                kt = k * e_km
                A = jnp.where(low, _dg3(qt, kt, NT), 0.0)
                sg = _sigmoid(gg)
                r = lax.rsqrt(jnp.mean(o * o, axis=-1, keepdims=True) + EPS)
                on = o * r
                d_onw = dyv * (gg * sg)
                dgn_ref[hh] += jnp.sum(d_onw * on, axis=0, keepdims=True)
                dgg = dyv * (on * gnv) * (sg * (1.0 + gg * (1.0 - sg)))
                u = d_onw * gnv
                do = r * (u - on * jnp.mean(u * on, axis=-1, keepdims=True))
                Sv = sts_ref[hh, ci]
                dSv = dst[hh]
                dA = jnp.where(low, _dg3(do, vv, NT), 0.0)
                kdec = k * e_lk
                dv = _dg3(A, do, TN) + _dg3(kdec, dSv, NT)
                dq = _dg3(dA, kt, NN) * e_qm + eG * _dg3(do, Sv, NN)
                dk = _dg3(dA, qt, TN) * e_km + e_lk * _dg3(vv, dSv, NN)
                s_end = Sv * eGl + _dg3(vv, kdec, TN)
                dgl = jnp.sum(dSv * s_end, axis=0, keepdims=True)
                dG = q * dq - k * dk + jnp.where(rows == C - 1, dgl, 0.0)
                dlogf = _tri_dot(triu, dG) - f * dk
                dst[hh] = dSv * eGl + _dg3(do, q * eG, TN)
                dlf_f = dlogf / f
                dlb_acc[:, ls] += jnp.sum(dlf_f * (1.0 - sig), axis=0, keepdims=True)
                dp_ref[0, sl, ls] = (dq * (sq * (1.0 + qr * (1.0 - sq)))).astype(BF)
                dp_ref[1, sl, ls] = (dlf_f * (1.0 - lbv) * sig * (1.0 - sig)).astype(BF)
                dp_ref[2, sl, ls] = dv.astype(BF)
                dp_ref[3, sl, ls] = dgg.astype(BF)
            return carry

        lax.fori_loop(0, nch, chunk, 0)
        sel = jnp.where(lax.broadcasted_iota(jnp.int32, (3, W), 0) == 0, 1.0, 0.0)
        dlb_ref[...] = lb_all * (sel - p3) * dlb_acc[...]

    ng = H // HB

    def part(p):
        return pl.BlockSpec((T, W), lambda h, n: (nb - 1 - n, p * ng + h))

    blk = pl.BlockSpec((T, W), lambda h, n: (nb - 1 - n, h))
    return pl.pallas_call(
        body, name=name, grid=(ng, nb),
        in_specs=[part(0), part(1), part(2), part(3),
                  pl.BlockSpec((3, W), lambda h, n: (0, h)), pl.BlockSpec((1, LANES), lambda h, n: (0, 0)),
                  blk, pl.BlockSpec((HB, nch, LANES, LANES), lambda h, n: (h, nb - 1 - n, 0, 0)), blk],
        out_specs=[pl.BlockSpec((4, T, W), lambda h, n: (0, nb - 1 - n, h)),
                   pl.BlockSpec((3, W), lambda h, n: (0, h)),
                   pl.BlockSpec((HB, 1, LANES), lambda h, n: (h, 0, 0))],
        out_shape=[jax.ShapeDtypeStruct((4, S, D), BF), jax.ShapeDtypeStruct((3, D), F32),
                   jax.ShapeDtypeStruct((H, 1, LANES), F32)],
        scratch_shapes=[pltpu.VMEM((HB, LANES, LANES), F32), pltpu.VMEM((1, W), F32)],
        compiler_params=_cp("parallel", "arbitrary"),
    )(proj, proj, proj, proj, hg_lb, gn, o_all, states, dy)


def _log_sigmoid(u):
    return jnp.minimum(u, 0.0) - jnp.log(1.0 + jnp.exp(-jnp.abs(u)))


def _lane_put(base, lane, first, pieces):
    for n, p in enumerate(pieces):
        base = jnp.where(lane == first + n, p, base)
    return base


def _fox_cumsum(proj, bf_pad, *, name):
    S = proj.shape[0]
    D = proj.shape[1] // 5
    T = _pick(S, 256, 8)

    def body(fz_ref, b_ref, f_ref, carry):
        @pl.when(pl.program_id(0) == 0)
        def _():
            carry[...] = jnp.zeros_like(carry)

        logf = _log_sigmoid(fz_ref[...] + b_ref[...])
        tri = jnp.where(lax.broadcasted_iota(jnp.int32, (T, T), 0) >= lax.broadcasted_iota(jnp.int32, (T, T), 1),
                        1.0, 0.0).astype(BF)
        fv = _tri_dot(tri, logf) + carry[...]
        f_ref[...] = fv
        carry[...] = _row_of(fv, lax.broadcasted_iota(jnp.int32, (T, LANES), 0), T - 1)

    return pl.pallas_call(
        body, name=name, grid=(S // T,),
        in_specs=[pl.BlockSpec((T, LANES), lambda i: (i, 4 * D // LANES)), pl.BlockSpec((1, LANES), lambda i: (0, 0))],
        out_specs=pl.BlockSpec((T, LANES), lambda i: (i, 0)),
        out_shape=jax.ShapeDtypeStruct((S, LANES), F32),
        scratch_shapes=[pltpu.VMEM((1, LANES), F32)],
        compiler_params=_cp("arbitrary"),
    )(proj, bf_pad)


def _pair_stats(sq, lo):
    s_lo = jnp.sum(jnp.where(lo, sq, 0.0), axis=-1, keepdims=True)
    s_hi = jnp.sum(jnp.where(lo, 0.0, sq), axis=-1, keepdims=True)
    return jnp.where(lo, s_lo, s_hi) * (1.0 / FOX_DH)


def _fox_prep(proj, fcum, qw2, kw2, *, name):
    S = proj.shape[0]
    D = proj.shape[1] // 5
    HP = D // LANES
    T = _pick(S, 512, 16)

    def body(q_ref, k_ref, v_ref, f_ref, qw_ref, kw_ref, qa_ref, ka_ref, va_ref, vt_ref):
        hp = pl.program_id(1)
        lane = lax.broadcasted_iota(jnp.int32, (T, LANES), 1)
        lo = lane < FOX_DH
        qv, kv, vv, fv = q_ref[...], k_ref[...], v_ref[...], f_ref[...]
        qn = qv * lax.rsqrt(_pair_stats(qv * qv, lo) + EPS) * qw_ref[...] * (0.125 * LOG2E)
        kn = kv * lax.rsqrt(_pair_stats(kv * kv, lo) + EPS) * kw_ref[...]
        ones_q = jnp.where((lane >= 67) & (lane <= 69), 1.0, 0.0)
        ones_k = jnp.where(((lane >= 64) & (lane <= 66)) | ((lane >= 70) & (lane <= 72)), 1.0, 0.0)
        ones_v = jnp.where((lane >= 64) & (lane <= 66), 1.0, 0.0)
        for hh in range(2):
            fh = jnp.sum(jnp.where(lane == 2 * hp + hh, fv, 0.0), axis=-1, keepdims=True) * LOG2E
            pieces = [p.astype(F32) for p in _split3(fh)]

            def half(x):
                return jnp.where(lo, x if hh == 0 else pltpu.roll(x, FOX_DH, 1), 0.0)

            qa_ref[hh] = _lane_put(half(qn) + ones_q, lane, 64, pieces).astype(BF)
            ka_ref[hh] = _lane_put(half(kn) + ones_k, lane, 67, [-p for p in pieces]).astype(BF)
            va = half(vv) + ones_v
            va_ref[hh] = va.astype(BF)
            vt_ref[hh] = va.T.astype(BF)

    def part(p):
        return pl.BlockSpec((T, LANES), lambda i, hp: (i, p * HP + hp))

    vec = pl.BlockSpec((1, LANES), lambda i, hp: (0, 0))
    aug = pl.BlockSpec((2, T, LANES), lambda i, hp: (hp, i, 0))
    return pl.pallas_call(
        body, name=name, grid=(S // T, HP),
        in_specs=[part(0), part(1), part(2), pl.BlockSpec((T, LANES), lambda i, hp: (i, 0)), vec, vec],
        out_specs=[aug, aug, aug, pl.BlockSpec((2, LANES, T), lambda i, hp: (hp, 0, i))],
        out_shape=[jax.ShapeDtypeStruct((2 * HP, S, LANES), BF)] * 3 + [jax.ShapeDtypeStruct((2 * HP, LANES, S), BF)],
        compiler_params=_cp("parallel", "arbitrary"),
    )(proj, proj, proj, fcum, qw2, kw2)


def _fox_block(S):
    return _pick(S, 256, 16)


def _fox_skip_bounds(fcum, qn_w, kn_w, nheads):
    S = fcum.shape[0]
    B = _fox_block(S)
    qk = 8.0 * LOG2E * 1.02 * jnp.max(jnp.abs(qn_w)) * jnp.max(jnp.abs(kn_w))
    thresh = -(2.0 * qk + 160.0)
    f2 = fcum[:, :nheads] * LOG2E
    first, last = f2[0::B], f2[B - 1::B]
    nb = S // B
    blk = jnp.arange(nb)
    dead = (first[0::2, None, :] - last[None, :, :]) < thresh
    jmin = jnp.sum(dead & (blk[None, :, None] < 2 * jnp.arange(nb // 2)[:, None, None]), axis=1)
    live = (first[:, None, :] - last[None, :, :]) >= thresh
    imax = blk[:, None] + jnp.sum(live & (blk[:, None, None] > blk[None, :, None]), axis=0)
    return jmin.T.astype(jnp.int32), imax.T.astype(jnp.int32)


def _fox_fwd(jmin, qa, ka, vat, proj, *, name):
    H, S, _ = qa.shape
    HP = H // 2
    D = HP * LANES
    B = _fox_block(S)
    BQ = 2 * B
    nq = S // BQ

    def body(jmin_ref, q_ref, k_ref, vt_ref, g_ref, y_ref, o_ref, q2_ref):
        hp, i = pl.program_id(0), pl.program_id(1)
        lane = lax.broadcasted_iota(jnp.int32, (BQ, LANES), 1)
        lo = lane < FOX_DH
        in_lse = (lane >= 70) & (lane <= 72)
        causal = lax.broadcasted_iota(jnp.int32, (BQ, BQ), 0) <= lax.broadcasted_iota(jnp.int32, (BQ, BQ), 1)
        row = lax.broadcasted_iota(jnp.int32, (LANES, BQ), 0)
        m0, acc0 = jnp.full((1, BQ), -jnp.inf, F32), jnp.zeros((LANES, BQ), F32)
        outs = []
        for hh in range(2):
            qb = q_ref[hh]

            def block(j, carry, masked=False):
                m, acc = carry
                sl = pl.ds(pl.multiple_of(j * BQ, BQ), BQ)
                st = _dg(k_ref[hh, sl, :], qb, NT)
                if masked:
                    st = jnp.where(causal, st, -jnp.inf)
                m_new = jnp.maximum(m, jnp.max(st, axis=0, keepdims=True))
                p = jnp.exp2(st - m_new)
                ph = p.astype(BF)
                pl_ = (p - ph.astype(F32)).astype(BF)
                vt = vt_ref[hh, :, sl]
                pv = _dot(jnp.concatenate([vt, vt], axis=1), jnp.concatenate([ph, pl_], axis=0))
                return m_new, acc * jnp.exp2(m - m_new) + pv

            carry = lax.fori_loop(jmin_ref[2 * hp + hh, i] // 2, i, block, (m0, acc0))
            m, acc = block(i, carry, masked=True)
            l = jnp.sum(jnp.where(row == FOX_DH, acc, 0.0), axis=0, keepdims=True)
            tile = acc / l
            for n, piece in enumerate(_split3(m + jnp.log2(l))):
                tile = jnp.where(row == 70 + n, -(piece.astype(F32)), tile)
            tile = tile.T
            outs.append(tile)
            q2_ref[hh] = jnp.where(in_lse, tile, qb.astype(F32)).astype(BF)
        o = jnp.where(lo, outs[0], pltpu.roll(outs[1], FOX_DH, 1))
        o_ref[...] = o
        y_ref[...] = (o * _sigmoid(g_ref[...])).astype(BF)

    blk = pl.BlockSpec((BQ, LANES), lambda hp, i, jm: (i, hp))
    qblk = pl.BlockSpec((2, BQ, LANES), lambda hp, i, jm: (hp, i, 0))
    full = pl.BlockSpec((2, S, LANES), lambda hp, i, jm: (hp, 0, 0))
    full_t = pl.BlockSpec((2, LANES, S), lambda hp, i, jm: (hp, 0, 0))
    return pl.pallas_call(
        body, name=name,
        grid_spec=pltpu.PrefetchScalarGridSpec(
            num_scalar_prefetch=1, grid=(HP, nq),
            in_specs=[qblk, full, full_t, pl.BlockSpec((BQ, LANES), lambda hp, i, jm: (i, 3 * HP + hp))],
            out_specs=[blk, blk, qblk]),
        out_shape=[jax.ShapeDtypeStruct((S, D), BF), jax.ShapeDtypeStruct((S, D), F32),
                   jax.ShapeDtypeStruct((H, S, LANES), BF)],
        compiler_params=_cp("parallel", "arbitrary"),
    )(jmin, qa, ka, vat, proj)


def _fox_bwd_prep(dy, o, proj, *, name):
    S, D = dy.shape
    HP = D // LANES
    T = _pick(S, 512, 16)

    def body(dy_ref, o_ref, g_ref, da_ref):
        lane = lax.broadcasted_iota(jnp.int32, (T, LANES), 1)
        lo = lane < FOX_DH
        do = (dy_ref[...] * _sigmoid(g_ref[...])).astype(BF).astype(F32)
        prod = do * o_ref[...]
        d_lo = jnp.sum(jnp.where(lo, prod, 0.0), axis=-1, keepdims=True)
        d_hi = jnp.sum(jnp.where(lo, 0.0, prod), axis=-1, keepdims=True)
        for hh, delta in enumerate((d_lo, d_hi)):
            base = jnp.where(lo, do if hh == 0 else pltpu.roll(do, FOX_DH, 1), 0.0)
            da_ref[hh] = _lane_put(base, lane, 64, [-(p.astype(F32)) for p in _split3(delta)]).astype(BF)

    blk = pl.BlockSpec((T, LANES), lambda i, hp: (i, hp))
    return pl.pallas_call(
        body, name=name, grid=(S // T, HP),
        in_specs=[blk, blk, pl.BlockSpec((T, LANES), lambda i, hp: (i, 3 * HP + hp))],
        out_specs=pl.BlockSpec((2, T, LANES), lambda i, hp: (hp, i, 0)),
        out_shape=jax.ShapeDtypeStruct((2 * HP, S, LANES), BF),
        compiler_params=_cp("parallel", "arbitrary"),
    )(dy, o, proj)


def _fox_bwd(imax, q2, ka, va, doa, *, name):
    H, S, _ = q2.shape
    B = _fox_block(S)
    nb = S // B

    def body(imax_ref, q_ref, do_ref, k_ref, v_ref, dq_ref, dk_ref, dv_ref, cs_ref):
        j = pl.program_id(1)
        end = imax_ref[pl.program_id(0), j] + 1

        @pl.when(j == 0)
        def _():
            dq_ref[...] = jnp.zeros_like(dq_ref)

        kb, vb = k_ref[...], v_ref[...]
        causal = lax.broadcasted_iota(jnp.int32, (B, B), 1) <= lax.broadcasted_iota(jnp.int32, (B, B), 0)

        def step(i, carry, masked=False, nblk=1):
            dk_acc, dv_acc, cs_acc = carry
            rows = nblk * B
            sl = pl.ds(pl.multiple_of(i * B, B), rows)
            qb, dob = q_ref[sl, :], do_ref[sl, :]
            s = _dg(qb, kb, NT)
            if masked:
                s = jnp.where(causal, s, -jnp.inf)
            p = jnp.exp2(s)
            ds = p * _dg(dob, vb, NT)
            dsb = ds.astype(BF)
            cs_acc = cs_acc + jnp.sum(ds.reshape(rows // 8, 8, B), axis=0)
            dv_acc = dv_acc + _dg(p.astype(BF), dob, TN)
            dk_acc = dk_acc + _dg(dsb, qb, TN)
            dq_ref[sl, :] += _dot(dsb, kb)
            return dk_acc, dv_acc, cs_acc

        zero = jnp.zeros((B, LANES), F32)
        carry = step(j, (zero, zero, jnp.zeros((8, B), F32)), masked=True)
        pos = j + 1
        for U in FOX_BWD_TILES:
            n = (end - pos) // U
            carry = lax.fori_loop(0, n, lambda ii, c, pos=pos, U=U: step(pos + U * ii, c, nblk=U), carry)
            pos = pos + U * n
        dk_acc, dv_acc, cs_acc = carry
        dk_ref[...] = dk_acc
        dv_ref[...] = dv_acc
        cs_ref[...] = jnp.sum(cs_acc, axis=0, keepdims=True)

    full = pl.BlockSpec((None, S, LANES), lambda h, j, im: (h, 0, 0))
    blk = pl.BlockSpec((None, B, LANES), lambda h, j, im: (h, j, 0))
    return pl.pallas_call(
        body, name=name,
        grid_spec=pltpu.PrefetchScalarGridSpec(
            num_scalar_prefetch=1, grid=(H, nb),
            in_specs=[full, full, blk, blk],
            out_specs=[full, blk, blk, pl.BlockSpec((None, 1, B), lambda h, j, im: (h, 0, j))]),
        out_shape=[jax.ShapeDtypeStruct((H, S, LANES), F32)] * 3 + [jax.ShapeDtypeStruct((H, 1, S), F32)],
        compiler_params=_cp("parallel", "arbitrary"),
    )(imax, q2, doa, ka, va)


def _fox_bwd_post(dqa, dka, dva, proj, dy, o, qw2, kw2, *, name):
    S, D = dy.shape
    HP = D // LANES
    T = _pick(S, 512, 16)

    def body(dq_ref, dk_ref, dv_ref, q_ref, k_ref, g_ref, dy_ref, o_ref, qw_ref, kw_ref, dp_ref, dqw_ref, dkw_ref):
        @pl.when((pl.program_id(0) == 0) & (pl.program_id(1) == 0))
        def _():
            dqw_ref[...] = jnp.zeros_like(dqw_ref)
            dkw_ref[...] = jnp.zeros_like(dkw_ref)

        lane = lax.broadcasted_iota(jnp.int32, (T, LANES), 1)
        lo = lane < FOX_DH

        def pair(ref):
            return jnp.where(lo, ref[0], pltpu.roll(ref[1], FOX_DH, 1))

        def norm_bwd(xv, w, dyn, dw_ref):
            r = lax.rsqrt(_pair_stats(xv * xv, lo) + EPS)
            xr = xv * r
            dw_ref[...] += jnp.sum(dyn * xr, axis=0, keepdims=True)
            u = dyn * w
            return r * (u - xr * _pair_stats(u * xr, lo))

        dp_ref[0] = norm_bwd(q_ref[...], qw_ref[...], pair(dq_ref) * 0.125, dqw_ref).astype(BF)
        dp_ref[1] = norm_bwd(k_ref[...], kw_ref[...], pair(dk_ref) * (1.0 / LOG2E), dkw_ref).astype(BF)
        dp_ref[2] = pair(dv_ref).astype(BF)
        sg = _sigmoid(g_ref[...])
        dp_ref[3] = (dy_ref[...] * o_ref[...] * sg * (1.0 - sg)).astype(BF)

    def part(p):
        return pl.BlockSpec((T, LANES), lambda i, hp: (i, p * HP + hp))

    aug = pl.BlockSpec((2, T, LANES), lambda i, hp: (hp, i, 0))
    blk = pl.BlockSpec((T, LANES), lambda i, hp: (i, hp))
    vec = pl.BlockSpec((1, LANES), lambda i, hp: (0, 0))
    return pl.pallas_call(
        body, name=name, grid=(S // T, HP),
        in_specs=[aug, aug, aug, part(0), part(1), part(3), blk, blk, vec, vec],
        out_specs=[pl.BlockSpec((4, T, LANES), lambda i, hp: (0, i, hp)), vec, vec],
        out_shape=[jax.ShapeDtypeStruct((5, S, D), BF), jax.ShapeDtypeStruct((1, LANES), F32),
                   jax.ShapeDtypeStruct((1, LANES), F32)],
        compiler_params=_cp("arbitrary", "arbitrary"),
    )(dqa, dka, dva, proj, proj, proj, dy, o, qw2, kw2)


def _fox_dfz(colsum, nheads, proj, bf_pad, dproj, *, name):
    S = colsum.shape[0]
    H = nheads
    D = dproj.shape[2]
    T = _pick(S, 256, 16)
    nb = S // T

    def body(cs_ref, fz_ref, b_ref, _, dp_ref, db_ref, carry):
        @pl.when(pl.program_id(0) == 0)
        def _():
            carry[...] = jnp.zeros_like(carry)
            db_ref[...] = jnp.zeros_like(db_ref)

        lane = lax.broadcasted_iota(jnp.int32, (T, LANES), 1)
        df = -cs_ref[...]
        triu = jnp.where(lax.broadcasted_iota(jnp.int32, (T, T), 0) <= lax.broadcasted_iota(jnp.int32, (T, T), 1),
                         1.0, 0.0).astype(BF)
        dlogf = _tri_dot(triu, df) + carry[...]
        carry[...] = _row_of(dlogf, lax.broadcasted_iota(jnp.int32, (T, LANES), 0), 0)
        dfz = jnp.where(lane < H, dlogf * _sigmoid(-(fz_ref[...] + b_ref[...])), 0.0)
        db_ref[...] += jnp.sum(dfz, axis=0, keepdims=True)
        dp_ref[...] = jnp.zeros_like(dp_ref)
        dp_ref[:, 0:LANES] = dfz.astype(BF)

    return pl.pallas_call(
        body, name=name, grid=(nb,),
        in_specs=[pl.BlockSpec((T, LANES), lambda i: (nb - 1 - i, 0)),
                  pl.BlockSpec((T, LANES), lambda i: (nb - 1 - i, 4 * D // LANES)),
                  pl.BlockSpec((1, LANES), lambda i: (0, 0)),
                  pl.BlockSpec(memory_space=pl.ANY)],
        out_specs=[pl.BlockSpec((None, T, D), lambda i: (4, nb - 1 - i, 0)), pl.BlockSpec((1, LANES), lambda i: (0, 0))],
        out_shape=[jax.ShapeDtypeStruct(dproj.shape, BF), jax.ShapeDtypeStruct((1, LANES), F32)],
        scratch_shapes=[pltpu.VMEM((1, LANES), F32)],
        input_output_aliases={3: 0},
        compiler_params=_cp("arbitrary"),
    )(colsum, proj, bf_pad, dproj)


def _mod_fwd(c16, w, b, *, name):
    L, D, N = w.shape
    tn = _pick(N, 512)

    def body(c_ref, w_ref, b_ref, o_ref):
        cv = c_ref[...]
        ca = (cv * _sigmoid(cv)).astype(BF)
        o_ref[...] = _dot(ca, w_ref[...].astype(BF)) + b_ref[...]

    return pl.pallas_call(
        body, name=name, grid=(L, N // tn),
        in_specs=[pl.BlockSpec((16, D), lambda l, j: (0, 0)), pl.BlockSpec((None, D, tn), lambda l, j: (l, 0, j)),
                  pl.BlockSpec((None, 1, tn), lambda l, j: (l, 0, j))],
        out_specs=pl.BlockSpec((None, 16, tn), lambda l, j: (l, 0, j)),
        out_shape=jax.ShapeDtypeStruct((L, 16, N), F32),
        compiler_params=_cp("parallel", "arbitrary"),
    )(c16, w, b)


def _mod_bwd(c16, dmod, *, name):
    L, _, N = dmod.shape
    D = c16.shape[1]
    tn = _pick(N, 512)

    def body(c_ref, d_ref, o_ref):
        cv = c_ref[...]
        ca = (cv * _sigmoid(cv)).astype(BF)
        o_ref[...] = _dg(ca, d_ref[...].astype(BF), TN)

    return pl.pallas_call(
        body, name=name, grid=(L, N // tn),
        in_specs=[pl.BlockSpec((16, D), lambda l, j: (0, 0)), pl.BlockSpec((None, 16, tn), lambda l, j: (l, 0, j))],
        out_specs=pl.BlockSpec((None, D, tn), lambda l, j: (l, 0, j)),
        out_shape=jax.ShapeDtypeStruct((L, D, N), F32),
        compiler_params=_cp("parallel", "arbitrary"),
    )(c16, dmod)


def _adamw_math(w, g, m, v):
    m = ADAM_B1 * m + (1.0 - ADAM_B1) * g
    v = ADAM_B2 * v + (1.0 - ADAM_B2) * (g * g)
    m_hat = m / (1.0 - ADAM_B1 ** ADAM_STEP)
    v_hat = v / (1.0 - ADAM_B2 ** ADAM_STEP)
    return -ADAM_LR * (m_hat / (jnp.sqrt(v_hat) + ADAM_EPS) + ADAM_WD * w), m, v


def _adamw(w, g, m, v, *, g_at=None, name):
    R, C = w.shape
    row0 = 0 if g_at is None else g_at[1]
    tr = min(math.gcd(row0, 256) if row0 else 256, -(-R // 8) * 8)
    g0 = row0 // tr
    if g_at is None:
        g_spec = pl.BlockSpec((tr, C), lambda i: (i, 0))
    else:
        g_spec = pl.BlockSpec((None, tr, C), lambda i: (g_at[0], g0 + i, 0))

    def body(w_ref, g_ref, m_ref, v_ref, d_ref, mo_ref, vo_ref):
        d, mn, vn = _adamw_math(w_ref[...], g_ref[...], m_ref[...], v_ref[...])
        d_ref[...] = d
        mo_ref[...] = mn
        vo_ref[...] = vn

    blk = pl.BlockSpec((tr, C), lambda i: (i, 0))
    return pl.pallas_call(
        body, name=name, grid=(pl.cdiv(R, tr),),
        in_specs=[blk, g_spec, blk, blk],
        out_specs=[blk, blk, blk],
        out_shape=[jax.ShapeDtypeStruct((R, C), F32)] * 3,
        compiler_params=_cp("parallel"),
    )(w, g, m, v)


def _sum_parts(parts, *, name):
    P, R, C = parts.shape

    def body(p_ref, o_ref):
        acc = p_ref[0]
        for p in range(1, P):
            acc = acc + p_ref[p]
        o_ref[...] = acc

    return pl.pallas_call(
        body, name=name, grid=(1,),
        in_specs=[pl.BlockSpec((P, R, C), lambda i: (0, 0, 0))],
        out_specs=pl.BlockSpec((R, C), lambda i: (0, 0)),
        out_shape=jax.ShapeDtypeStruct((R, C), F32),
        compiler_params=_cp("arbitrary"),
    )(parts)


def _add_halves(g4, recv, c_idx, *, name):
    _, _, Rh, C = g4.shape
    tr = _pick(Rh, 256, 16)

    def body(c_ref, a_ref, b_ref, o_ref):
        o_ref[...] = (a_ref[...] + b_ref[...].astype(F32)).astype(BF)

    return pl.pallas_call(
        body, name=name,
        grid_spec=pltpu.PrefetchScalarGridSpec(
            num_scalar_prefetch=1, grid=(4, pl.cdiv(Rh, tr)),
            in_specs=[pl.BlockSpec((None, None, tr, C), lambda j, r, c: (j, c[0], r, 0)),
                      pl.BlockSpec((None, tr, C), lambda j, r, c: (j, r, 0))],
            out_specs=pl.BlockSpec((None, tr, C), lambda j, r, c: (j, r, 0))),
        out_shape=jax.ShapeDtypeStruct((4, Rh, C), BF),
        compiler_params=_cp("parallel", "arbitrary"),
    )(c_idx, g4, recv)


def _add_four(g4, from_sibling, from_chips, pos, *, name):
    _, _, Rh, C = g4.shape
    tr = _pick(Rh, 256, 16)

    def body(p_ref, a_ref, s_ref, b_ref, o_ref):
        own = a_ref[...] + s_ref[...].astype(F32)
        o_ref[...] = ((own + b_ref[0].astype(F32)) + b_ref[1].astype(F32)) + b_ref[2].astype(F32)

    return pl.pallas_call(
        body, name=name,
        grid_spec=pltpu.PrefetchScalarGridSpec(
            num_scalar_prefetch=1, grid=(pl.cdiv(Rh, tr),),
            in_specs=[pl.BlockSpec((None, None, tr, C), lambda r, p: (p[0], p[1], r, 0)),
                      pl.BlockSpec((None, tr, C), lambda r, p: (p[0], r, 0)),
                      pl.BlockSpec((3, tr, C), lambda r, p: (0, r, 0))],
            out_specs=pl.BlockSpec((None, tr, C), lambda r, p: (p[1], r, 0))),
        out_shape=jax.ShapeDtypeStruct((2, Rh, C), F32),
        compiler_params=_cp("arbitrary"),
    )(pos, g4, from_sibling, from_chips)


HBM = pl.BlockSpec(memory_space=pltpu.HBM)


def _mesh_pos():
    return lax.axis_index("x"), lax.axis_index("y"), lax.axis_index("c")


def _other_chips(x, y):
    return [(1 - x, y), (x, 1 - y), (1 - x, 1 - y)]


def _allgather_small(xs, *, name):
    m_per, n = xs.shape

    def body(x_ref, out_ref, send_sems, recv_sems, local_sem):
        x, y, c = _mesh_pos()
        me, sibling = (x, y, c), (x, y, 1 - c)
        chips = _other_chips(x, y)

        def rows(px, py, pc):
            return out_ref.at[pl.ds((4 * px + 2 * py + pc) * m_per, m_per), :]

        def copy(k, block, to, src=None):
            return pltpu.make_async_remote_copy(
                src_ref=rows(*block) if src is None else src, dst_ref=rows(*block),
                send_sem=send_sems.at[k], recv_sem=recv_sems.at[k], device_id=to, device_id_type=MESH)

        mine = pltpu.make_async_copy(x_ref, rows(*me), local_sem)
        mine.start()
        first = [copy(0, me, sibling, src=x_ref)]
        first += [copy(1 + j, me, (*chip, c), src=x_ref) for j, chip in enumerate(chips)]
        for cp in first:
            cp.start()
        passed = [copy(4 + j, (*chip, c), sibling) for j, chip in enumerate(chips)]
        for j, chip in enumerate(chips):
            copy(1 + j, (*chip, c), me).wait_recv()
            passed[j].start()
        copy(0, sibling, me).wait_recv()
        for j, chip in enumerate(chips):
            copy(4 + j, (*chip, 1 - c), me).wait_recv()
        for cp in first + passed:
            cp.wait_send()
        mine.wait()

    return pl.pallas_call(
        body, name=name,
        out_shape=jax.ShapeDtypeStruct((N_DEV * m_per, n), xs.dtype),
        in_specs=[pl.BlockSpec(memory_space=pltpu.VMEM)],
        out_specs=pl.BlockSpec(memory_space=pltpu.VMEM),
        scratch_shapes=[pltpu.SemaphoreType.DMA((7,)), pltpu.SemaphoreType.DMA((7,)), pltpu.SemaphoreType.DMA],
    )(xs)


def _allgather_chip_slabs(slab, *, name):
    R, C = slab.shape
    Rh = R // 2

    def body(s_ref, out_ref, send_sems, recv_sems):
        x, y, c = _mesh_pos()
        sibling = (x, y, 1 - c)
        chips = _other_chips(x, y)

        def half(px, py, pc):
            return out_ref.at[2 * px + py, pl.ds(pc * Rh, Rh), :]

        def copy(k, block, to, src=None):
            return pltpu.make_async_remote_copy(
                src_ref=half(*block) if src is None else src, dst_ref=half(*block),
                send_sem=send_sems.at[k], recv_sem=recv_sems.at[k], device_id=to, device_id_type=MESH)

        first = [copy(j, (x, y, c), (*chip, c), src=s_ref.at[pl.ds(c * Rh, Rh), :]) for j, chip in enumerate(chips)]
        for cp in first:
            cp.start()
        passed = [copy(3 + j, (*chip, c), sibling) for j, chip in enumerate(chips)]
        for j, chip in enumerate(chips):
            copy(j, (*chip, c), (x, y, c)).wait_recv()
            passed[j].start()
        for j, chip in enumerate(chips):
            copy(3 + j, (*chip, 1 - c), (x, y, c)).wait_recv()
        for cp in first + passed:
            cp.wait_send()

    return pl.pallas_call(
        body, name=name,
        out_shape=jax.ShapeDtypeStruct((N_CHIPS, R, C), slab.dtype),
        in_specs=[HBM], out_specs=HBM,
        scratch_shapes=[pltpu.SemaphoreType.DMA((6,)), pltpu.SemaphoreType.DMA((6,))],
    )(slab)


def _swap_halves(g4, *, name):
    _, _, Rh, C = g4.shape

    def body(g_ref, out_ref, send_sems, recv_sems):
        x, y, c = _mesh_pos()
        copies = [pltpu.make_async_remote_copy(
            src_ref=g_ref.at[j, 1 - c], dst_ref=out_ref.at[j], send_sem=send_sems.at[j], recv_sem=recv_sems.at[j],
            device_id=(x, y, 1 - c), device_id_type=MESH) for j in range(N_CHIPS)]
        for cp in copies:
            cp.start()
        for cp in copies:
            cp.wait()

    return pl.pallas_call(
        body, name=name,
        out_shape=jax.ShapeDtypeStruct((N_CHIPS, Rh, C), g4.dtype),
        in_specs=[HBM], out_specs=HBM,
        scratch_shapes=[pltpu.SemaphoreType.DMA((N_CHIPS,)), pltpu.SemaphoreType.DMA((N_CHIPS,))],
    )(g4)


def _scatter_partials(part, *, name):
    _, Rh, C = part.shape

    def body(p_ref, out_ref, send_sems, recv_sems):
        x, y, c = _mesh_pos()
        copies = [pltpu.make_async_remote_copy(
            src_ref=p_ref.at[2 * px + py], dst_ref=out_ref.at[j], send_sem=send_sems.at[j], recv_sem=recv_sems.at[j],
            device_id=(px, py, c), device_id_type=MESH) for j, (px, py) in enumerate(_other_chips(x, y))]
        for cp in copies:
            cp.start()
        for cp in copies:
            cp.wait()

    return pl.pallas_call(
        body, name=name,
        out_shape=jax.ShapeDtypeStruct((3, Rh, C), part.dtype),
        in_specs=[HBM], out_specs=HBM,
        scratch_shapes=[pltpu.SemaphoreType.DMA((3,)), pltpu.SemaphoreType.DMA((3,))],
    )(part)


def _join_halves(buf, *, name):
    def body(b_ref, out_ref, send_sem, recv_sem):
        x, y, c = _mesh_pos()
        cp = pltpu.make_async_remote_copy(
            src_ref=b_ref.at[c], dst_ref=out_ref.at[c], send_sem=send_sem, recv_sem=recv_sem,
            device_id=(x, y, 1 - c), device_id_type=MESH)
        cp.start()
        cp.wait()

    return pl.pallas_call(
        body, name=name,
        out_shape=jax.ShapeDtypeStruct(buf.shape, buf.dtype),
        in_specs=[HBM], out_specs=HBM, input_output_aliases={0: 0},
        scratch_shapes=[pltpu.SemaphoreType.DMA, pltpu.SemaphoreType.DMA],
    )(buf)


def _pad_rows(a, mult):
    pad = (-a.shape[0]) % mult
    return a if pad == 0 else jnp.pad(a, ((0, pad),) + ((0, 0),) * (a.ndim - 1))


def _local_step(x, target, mod, wts, small):
    S, D = x.shape
    HP = D // LANES
    row = lambda v: v.reshape(1, -1)
    msplit = [[row(mod[i, k * D:(k + 1) * D]) for k in range(6)] for i in range(2)]
    gw, gs = {}, {}
    dmod = [[None] * 6 for _ in range(2)]

    sh1, sc1, g1, sh2, sc2, g2 = msplit[0]
    n1w0, n2w0 = row(small["norm1_w"][0]), row(small["norm2_w"][0])
    proj0, h1_0 = _ln_matmul(x, n1w0, sc1, sh1, wts["hg_w_in"], relu2=False, name="hg_in_proj")
    gn = small["hg_gn_w"].reshape(1, LANES)
    ypre0, o0, states = _hg_fwd(proj0, small["hg_lb"], gn, name="hg_fwd")
    x1, ymix0 = _matmul_resid(ypre0, wts["hg_w_out"], x, g1, name="hg_out_proj")
    a0, u0, h2_0 = _ln_matmul(x1, n2w0, sc2, sh2, wts["mlp_w1_0"], relu2=True, name="mlp0_up")
    x2, ymlp0 = _matmul_resid(u0, wts["mlp_w2_0"], x1, g2, name="mlp0_down")

    sh1b, sc1b, g1b, sh2b, sc2b, g2b = msplit[1]
    n1w1, n2w1 = row(small["norm1_w"][1]), row(small["norm2_w"][1])
    proj1, h1_1 = _ln_matmul(x2, n1w1, sc1b, sh1b, wts["fox_w_in"], relu2=False, name="fox_in_proj")
    nheads = 2 * HP
    bf_pad = jnp.pad(small["fox_b_f"].reshape(1, nheads), ((0, 0), (0, LANES - nheads)))
    qw2 = jnp.tile(small["fox_qn_w"].reshape(1, FOX_DH), (1, 2))
    kw2 = jnp.tile(small["fox_kn_w"].reshape(1, FOX_DH), (1, 2))
    fcum = _fox_cumsum(proj1, bf_pad, name="fox_cumsum")
    qa, ka, va, vat = _fox_prep(proj1, fcum, qw2, kw2, name="fox_prep")
    jmin, imax = _fox_skip_bounds(fcum, small["fox_qn_w"], small["fox_kn_w"], nheads)
    ypre1, o1, q2 = _fox_fwd(jmin, qa, ka, vat, proj1, name="fox_fwd")
    x3, ymix1 = _matmul_resid(ypre1, wts["fox_w_out"], x2, g1b, name="fox_out_proj")
    a1, u1, h2_1 = _ln_matmul(x3, n2w1, sc2b, sh2b, wts["mlp_w1_1"], relu2=True, name="mlp1_up")
    x4, ymlp1 = _matmul_resid(u1, wts["mlp_w2_1"], x3, g2b, name="mlp1_down")

    loss, dx4, dfw = _loss_kernel(x4, row(small["final_w"]), target, name="loss")
    gs["final_w"] = dfw.reshape(-1)

    def mlp_bwd(i, dx_out, x_in, h2, a, u, ymlp, n2w, sc2_, g2_):
        dz, dm, dg2 = _gate_matmul_nt(dx_out, g2_, ymlp, wts[f"mlp_w2_{i}"], a, name=f"mlp{i}_down_bwd")
        gw[f"mlp_w2_{i}"] = _matmul_tn(u, dm[None], name=f"mlp{i}_dw2")
        gw[f"mlp_w1_{i}"] = _matmul_tn(h2, dz[None], name=f"mlp{i}_dw1")
        dx_in, dsc, dsh, dnw = _matmul_nt_lnbwd(dz[None], wts[f"mlp_w1_{i}"], x_in, n2w, sc2_, dx_out,
                                                name=f"mlp{i}_up_bwd")
        dmod[i][3], dmod[i][4], dmod[i][5] = dsh, dsc, dg2
        return dx_in, dnw

    dx3, dn2w1 = mlp_bwd(1, dx4, x3, h2_1, a1, u1, ymlp1, n2w1, sc2b, g2b)
    dyp1, dm1, dg1b = _gate_matmul_nt(dx3, g1b, ymix1, wts["fox_w_out"], None, name="fox_out_bwd")
    gw["fox_w_out"] = _matmul_tn(ypre1, dm1[None], name="fox_dw_out")
    doa = _fox_bwd_prep(dyp1, o1, proj1, name="fox_bwd_prep")
    dqa, dka, dva, colsum = _fox_bwd(imax, q2, ka, va, doa, name="fox_bwd")
    colsum = jnp.pad(colsum[:, 0, :].T, ((0, 0), (0, LANES - nheads)))
    dproj1, dqw, dkw = _fox_bwd_post(dqa, dka, dva, proj1, dyp1, o1, qw2, kw2, name="fox_bwd_post")
    dproj1, dbf = _fox_dfz(colsum, nheads, proj1, bf_pad, dproj1, name="fox_dfz")
    gw["fox_w_in"] = _matmul_tn(h1_1, dproj1, name="fox_dw_in")
    dx2, dsc, dsh, dn1w1 = _matmul_nt_lnbwd(dproj1, wts["fox_w_in"], x2, n1w1, sc1b, dx3, name="fox_in_bwd")
    dmod[1][0], dmod[1][1], dmod[1][2] = dsh, dsc, dg1b
    gs["fox_qn_w"] = dqw[0, :FOX_DH] + dqw[0, FOX_DH:]
    gs["fox_kn_w"] = dkw[0, :FOX_DH] + dkw[0, FOX_DH:]
    gs["fox_b_f"] = dbf[0, :nheads]

    dx1, dn2w0 = mlp_bwd(0, dx2, x1, h2_0, a0, u0, ymlp0, n2w0, sc2, g2)
    dyp0, dm0, dg1 = _gate_matmul_nt(dx1, g1, ymix0, wts["hg_w_out"], None, name="hg_out_bwd")
    gw["hg_w_out"] = _matmul_tn(ypre0, dm0[None], name="hg_dw_out")
    dproj0, dlb, dgn = _hg_bwd(proj0, small["hg_lb"], gn, o0, states, dyp0, name="hg_bwd")
    gw["hg_w_in"] = _matmul_tn(h1_0, dproj0, name="hg_dw_in")
    dx0, dsc, dsh, dn1w0 = _matmul_nt_lnbwd(dproj0, wts["hg_w_in"], x, n1w0, sc1, dx1, name="hg_in_bwd")
    dmod[0][0], dmod[0][1], dmod[0][2] = dsh, dsc, dg1
    gs["hg_lb"] = dlb
    gs["hg_gn_w"] = jnp.sum(dgn, axis=0)

    gs["norm1_w"] = jnp.concatenate([dn1w0, dn1w1], axis=0)
    gs["norm2_w"] = jnp.concatenate([dn2w0, dn2w1], axis=0)
    gs["dmod"] = jnp.stack([jnp.concatenate(dmod[i], axis=1)[0] for i in range(2)])
    return loss, dx0, gw, gs


SMALL_NAMES = ["norm1_w", "norm2_w", "hg_lb", "hg_gn_w", "fox_b_f", "fox_qn_w", "fox_kn_w", "final_w"]


def _pack_small(d, names):
    rows, offs, r0 = [], {}, 0
    for n in names:
        flat = d[n].reshape(-1)
        nr = -(-flat.shape[0] // LANES)
        rows.append(jnp.pad(flat, (0, nr * LANES - flat.shape[0])).reshape(nr, LANES))
        offs[n] = (r0, nr)
        r0 += nr
    return jnp.concatenate(rows, axis=0), offs


def _unpack_small(packed, offs, name, like):
    r0, nr = offs[name]
    return packed[r0:r0 + nr].reshape(-1)[:like.size].reshape(like.shape)


def kernel(x, c, w_mod, b_mod, norm1_w, norm2_w, hg_w_in, hg_w_out, hg_lb, hg_gn_w, fox_w_in, fox_b_f, fox_qn_w, fox_kn_w, fox_w_out, mlp_w1, mlp_w2, final_w, loss_target, m_w_mod, m_b_mod, m_norm1_w, m_norm2_w, m_hg_w_in, m_hg_w_out, m_hg_lb, m_hg_gn_w, m_fox_w_in, m_fox_b_f, m_fox_qn_w, m_fox_kn_w, m_fox_w_out, m_mlp_w1, m_mlp_w2, m_final_w, v_w_mod, v_b_mod, v_norm1_w, v_norm2_w, v_hg_w_in, v_hg_w_out, v_hg_lb, v_hg_gn_w, v_fox_w_in, v_fox_b_f, v_fox_qn_w, v_fox_kn_w, v_fox_w_out, v_mlp_w1, v_mlp_w2, v_final_w):
    S, D = x.shape[1], x.shape[2]
    nheads = D // FOX_DH
    ax, ay, ac = _mesh_pos()
    chip = 2 * ax + ay
    dev = 2 * chip + ac
    xs, tgt = x.reshape(S, D), loss_target.reshape(S, D)

    c_all = _allgather_small(_pad_rows(c.reshape(-1, LANES), 8), name="gather_c")
    c_all = c_all.reshape(N_DEV, -1)[:, :D]
    c16 = _pad_rows(c_all, 16)
    nmod = w_mod.shape[2]
    b_shard = lax.dynamic_slice_in_dim(b_mod, chip * nmod, nmod, axis=1)
    mod_shard = _mod_fwd(c16, w_mod, b_shard[:, None, :], name="mod_fwd")[:, :N_DEV]
    mod_all = _allgather_small(mod_shard.reshape(-1, LANES), name="gather_mod")
    mod_all = mod_all.reshape(N_CHIPS, 2, 2, N_DEV, nmod)[:, 0]
    mod = lax.dynamic_index_in_dim(mod_all, dev, axis=2, keepdims=False)
    mod = mod.transpose(1, 0, 2).reshape(2, N_CHIPS * nmod)

    fox_rows = fox_w_in.shape[2]
    layout = [[("hg_w_in", hg_w_in[0]), ("mlp_w1", mlp_w1.reshape(2 * D, D)), ("hg_w_out", hg_w_out[0])],
              [("mlp_w2", mlp_w2.reshape(2 * D, D)), ("fox_w_out", fox_w_out[0]),
               ("fox_w_in", fox_w_in[0].reshape(fox_rows, D))]]
    Rh = -(-max(sum(a.shape[0] for _, a in half) for half in layout) // 16) * 16
    place, parts = {}, []
    for h, half in enumerate(layout):
        off = 0
        for n, a in half:
            place[n] = (h, off, a.shape[0])
            off += a.shape[0]
        parts.append(jnp.pad(jnp.concatenate([a.astype(BF) for _, a in half], axis=0), ((0, Rh - off), (0, 0))))
    slab = jnp.concatenate(parts, axis=0)
    R = 2 * Rh
    gathered = _allgather_chip_slabs(slab, name="gather_weights")
    gathered = lax.dynamic_update_index_in_dim(gathered, slab, chip, 0)

    def seg(n):
        h, off, rows = place[n]
        return gathered[:, h * Rh + off:h * Rh + off + rows, :]

    col = lambda g: g.transpose(1, 0, 2).reshape(g.shape[1], -1)
    rowsh = lambda g: g.reshape(-1, g.shape[2])
    w1 = seg("mlp_w1").reshape(N_CHIPS, 2, D, D)
    w2 = seg("mlp_w2").reshape(N_CHIPS, 2, D, D)
    fox_in = col(seg("fox_w_in").reshape(N_CHIPS, D, fox_rows))
    wts = {
        "hg_w_in": col(seg("hg_w_in")), "hg_w_out": rowsh(seg("hg_w_out")), "fox_w_out": rowsh(seg("fox_w_out")),
        "mlp_w1_0": col(w1[:, 0]), "mlp_w1_1": col(w1[:, 1]), "mlp_w2_0": rowsh(w2[:, 0]), "mlp_w2_1": rowsh(w2[:, 1]),
        "fox_w_in": jnp.pad(fox_in, ((0, 0), (0, 5 * D - fox_in.shape[1]))),
    }
    small = {"norm1_w": norm1_w, "norm2_w": norm2_w, "hg_lb": hg_lb, "hg_gn_w": hg_gn_w, "fox_b_f": fox_b_f,
             "fox_qn_w": fox_qn_w, "fox_kn_w": fox_kn_w, "final_w": final_w}

    loss_part, grad_x, gw, gs = _local_step(xs, tgt, mod, wts, small)
    loss = lax.psum(loss_part[0, 0], ("x", "y", "c"))

    def uncol(g, n):
        return g.reshape(g.shape[0], N_CHIPS, n).transpose(1, 0, 2)

    gseg = {
        "hg_w_in": uncol(gw["hg_w_in"], D), "hg_w_out": gw["hg_w_out"].reshape(N_CHIPS, D // 4, D),
        "fox_w_out": gw["fox_w_out"].reshape(N_CHIPS, D // 4, D),
        "mlp_w1": jnp.concatenate([uncol(gw["mlp_w1_0"], D), uncol(gw["mlp_w1_1"], D)], axis=1),
        "mlp_w2": jnp.concatenate([gw["mlp_w2_0"].reshape(N_CHIPS, D, D), gw["mlp_w2_1"].reshape(N_CHIPS, D, D)], axis=1),
        "fox_w_in": uncol(gw["fox_w_in"][:, :4 * fox_rows], fox_rows).reshape(N_CHIPS, fox_rows, D),
    }
    ghalves = []
    for half in layout:
        gh = jnp.concatenate([gseg[n] for n, _ in half], axis=1)
        ghalves.append(jnp.pad(gh, ((0, 0), (0, Rh - gh.shape[1]), (0, 0))))
    g4 = jnp.stack(ghalves, axis=1)
    from_sibling = _swap_halves(g4.astype(BF), name="rs_swap_halves")
    chip_part = _add_halves(g4, from_sibling, ac.reshape(1), name="rs_add_halves")
    from_chips = _scatter_partials(chip_part, name="rs_scatter")
    my_half = _add_four(g4, from_sibling, from_chips, jnp.stack([chip, ac]), name="rs_add_chips")
    gshard = _join_halves(my_half, name="rs_join")

    names = ["dmod"] + SMALL_NAMES
    packed, offs = _pack_small(gs, names)
    packed = _pad_rows(packed, 8)
    rp = packed.shape[0]
    parts = _allgather_small(packed, name="gather_small").reshape(N_DEV, rp, LANES)
    total = _sum_parts(parts, name="sum_small")
    r0, nr = offs["dmod"]
    dmod_all = parts[:, r0:r0 + nr].reshape(N_DEV, 2, N_CHIPS * nmod)
    dmod_shard = lax.dynamic_slice_in_dim(dmod_all, chip * nmod, nmod, axis=2).transpose(1, 0, 2)
    g_w_mod = _mod_bwd(c16, jnp.pad(dmod_shard, ((0, 0), (0, 16 - N_DEV), (0, 0))), name="mod_bwd")

    grads = {"w_mod": g_w_mod, "b_mod": _unpack_small(total, offs, "dmod", b_mod)}
    for n in SMALL_NAMES:
        grads[n] = _unpack_small(total, offs, n, small[n])

    given = dict(w_mod=(w_mod, m_w_mod, v_w_mod), b_mod=(b_mod, m_b_mod, v_b_mod), norm1_w=(norm1_w, m_norm1_w, v_norm1_w),
                 norm2_w=(norm2_w, m_norm2_w, v_norm2_w), hg_w_in=(hg_w_in, m_hg_w_in, v_hg_w_in),
                 hg_w_out=(hg_w_out, m_hg_w_out, v_hg_w_out), hg_lb=(hg_lb, m_hg_lb, v_hg_lb),
                 hg_gn_w=(hg_gn_w, m_hg_gn_w, v_hg_gn_w), fox_w_in=(fox_w_in, m_fox_w_in, v_fox_w_in),
                 fox_b_f=(fox_b_f, m_fox_b_f, v_fox_b_f), fox_qn_w=(fox_qn_w, m_fox_qn_w, v_fox_qn_w),
                 fox_kn_w=(fox_kn_w, m_fox_kn_w, v_fox_kn_w), fox_w_out=(fox_w_out, m_fox_w_out, v_fox_w_out),
                 mlp_w1=(mlp_w1, m_mlp_w1, v_mlp_w1), mlp_w2=(mlp_w2, m_mlp_w2, v_mlp_w2), final_w=(final_w, m_final_w, v_final_w))
    upd = {}

    for n, (h, off, rows) in place.items():
        w, m, v = given[n]
        flat = lambda a: a.reshape(rows, D)
        d, mn, vn = _adamw(flat(w), gshard, flat(m), flat(v), g_at=(h, off), name=f"adamw_{n}")
        grads[n] = gshard[h, off:off + rows].reshape(w.shape)
        upd[n] = tuple(a.reshape(w.shape) for a in (d, mn, vn))

    w, m, v = given["w_mod"]
    flat = lambda a: a.reshape(-1, nmod)
    upd["w_mod"] = tuple(a.reshape(w.shape) for a in _adamw(flat(w), flat(g_w_mod), flat(m), flat(v), name="adamw_w_mod"))

    snames = ["b_mod"] + SMALL_NAMES
    pw, soffs = _pack_small({n: given[n][0] for n in snames}, snames)
    pm, _ = _pack_small({n: given[n][1] for n in snames}, snames)
    pv, _ = _pack_small({n: given[n][2] for n in snames}, snames)
    pg, _ = _pack_small({n: grads[n] for n in snames}, snames)
    pw, pm, pv, pg = (_pad_rows(a, 8) for a in (pw, pm, pv, pg))
    sd, smn, svn = _adamw(pw, pg, pm, pv, name="adamw_small")
    for n in snames:
        like = given[n][0]
        upd[n] = tuple(_unpack_small(a, soffs, n, like) for a in (sd, smn, svn))

    order = ["w_mod", "b_mod", "norm1_w", "norm2_w", "hg_w_in", "hg_w_out", "hg_lb", "hg_gn_w", "fox_w_in", "fox_b_f",
             "fox_qn_w", "fox_kn_w", "fox_w_out", "mlp_w1", "mlp_w2", "final_w"]
    return (loss, grad_x.reshape(x.shape), *[grads[n] for n in order], *[upd[n][0] for n in order],
            *[upd[n][1] for n in order], *[upd[n][2] for n in order])
```

```python
import math

import jax
import jax.numpy as jnp
from jax import lax
from jax.experimental import pallas as pl
from jax.experimental.pallas import tpu as pltpu

EPS = 1e-6
ADAM_LR, ADAM_B1, ADAM_B2, ADAM_EPS, ADAM_WD, ADAM_STEP = 0.001, 0.9, 0.999, 1e-08, 0.01, 10

F32 = jnp.float32
BF = jnp.bfloat16
LANES = 128
HG_CHUNK = 64
HG_HEADS_PER_STEP = 8
HG_TOKENS_PER_STEP = 256
FOX_BWD_TILES = (8, 4, 2, 1)
LOG2E = 1.4426950408889634
FOX_DH = 64
N_CHIPS = 4
N_DEV = 8
VMEM_LIMIT = 48 * 1024 * 1024
MESH = pl.DeviceIdType.MESH

NT = (((1,), (1,)), ((), ()))
TN = (((0,), (0,)), ((), ()))


def _pick(n, pref, mult=LANES):
    if n <= pref:
        return n
    t = (pref // mult) * mult
    while t >= mult:
        if n % t == 0:
            return t
        t -= mult
    raise ValueError((n, pref, mult))


def _cp(*sem):
    return pltpu.CompilerParams(dimension_semantics=sem, vmem_limit_bytes=VMEM_LIMIT)


def _dot(a, b):
    return jnp.dot(a, b, preferred_element_type=F32)


def _dg(a, b, dims):
    return lax.dot_general(a, b, dims, preferred_element_type=F32)


def _split3(x):
    hi = x.astype(BF)
    r1 = x - hi.astype(F32)
    mid = r1.astype(BF)
    lo = (r1 - mid.astype(F32)).astype(BF)
    return hi, mid, lo


def _tri_dot(tri, x):
    hi, mid, lo = _split3(x)
    return _dot(tri, hi) + _dot(tri, mid) + _dot(tri, lo)


def _dg3(a, b, dims):
    ah, bh = a.astype(BF), b.astype(BF)
    al, bl = (a - ah.astype(F32)).astype(BF), (b - bh.astype(F32)).astype(BF)
    return _dg(ah, bh, dims) + _dg(ah, bl, dims) + _dg(al, bh, dims)


def _dg1(a, b, dims):
    return _dg(a.astype(BF), b.astype(BF), dims)


NN = (((1,), (0,)), ((), ()))


def _sigmoid(x):
    return jax.nn.sigmoid(x)


def _ln_matmul(x, nw, sc, sh, w, *, relu2, name):
    S, D = x.shape
    N = w.shape[1]
    tm, tn = _pick(S, 1024, 16), _pick(N, 1024)

    def body(x_ref, nw_ref, sc_ref, sh_ref, w_ref, *rest):
        outs, hs = rest[:-1], rest[-1]
        h_ref = outs[-1]

        @pl.when(pl.program_id(1) == 0)
        def _():
            xv = x_ref[...]
            r = lax.rsqrt(jnp.mean(xv * xv, axis=-1, keepdims=True) + EPS)
            hb = ((xv * r * nw_ref[...]) * (1.0 + sc_ref[...]) + sh_ref[...]).astype(BF)
            hs[...] = hb
            h_ref[...] = hb

        z = _dot(hs[...], w_ref[...])
        if relu2:
            a = jnp.maximum(z, 0.0)
            outs[0][...] = a.astype(BF)
            outs[1][...] = (a * a).astype(BF)
        else:
            outs[0][...] = z

    vec = pl.BlockSpec((1, D), lambda i, j: (0, 0))
    tile = pl.BlockSpec((tm, tn), lambda i, j: (i, j))
    if relu2:
        out_shape = [jax.ShapeDtypeStruct((S, N), BF), jax.ShapeDtypeStruct((S, N), BF)]
        out_specs = [tile, tile]
    else:
        out_shape = [jax.ShapeDtypeStruct((S, N), F32)]
        out_specs = [tile]
    out_shape.append(jax.ShapeDtypeStruct((S, D), BF))
    out_specs.append(pl.BlockSpec((tm, D), lambda i, j: (i, 0)))
    return pl.pallas_call(
        body, name=name, grid=(S // tm, N // tn),
        in_specs=[pl.BlockSpec((tm, D), lambda i, j: (i, 0)), vec, vec, vec,
                  pl.BlockSpec((D, tn), lambda i, j: (0, j))],
        out_specs=out_specs, out_shape=out_shape,
        scratch_shapes=[pltpu.VMEM((tm, D), BF)],
        compiler_params=_cp("parallel", "arbitrary"),
    )(x, nw, sc, sh, w)


def _matmul_resid(a, w, x, gate, *, name):
    S, K = a.shape
    D = w.shape[1]
    big = 1024 if K <= 1024 else 512
    tm, tn = _pick(S, big, 16), _pick(D, big)

    def body(a_ref, w_ref, x_ref, g_ref, o_ref, y_ref):
        y = _dot(a_ref[...], w_ref[...])
        y_ref[...] = y.astype(BF)
        o_ref[...] = x_ref[...] + g_ref[...] * y

    tile = pl.BlockSpec((tm, tn), lambda i, j: (i, j))
    return pl.pallas_call(
        body, name=name, grid=(S // tm, D // tn),
        in_specs=[pl.BlockSpec((tm, K), lambda i, j: (i, 0)), pl.BlockSpec((K, tn), lambda i, j: (0, j)),
                  tile, pl.BlockSpec((1, tn), lambda i, j: (0, j))],
        out_specs=[tile, tile],
        out_shape=[jax.ShapeDtypeStruct((S, D), F32), jax.ShapeDtypeStruct((S, D), BF)],
        compiler_params=_cp("parallel", "arbitrary"),
    )(a, w, x, gate)


def _gate_matmul_nt(dx, gate, y, w, act, *, name):
    S, D = dx.shape
    K = w.shape[0]
    tm, tn = _pick(S, 1024, 16), _pick(K, 1024)
    fused = act is not None

    def body(dx_ref, g_ref, y_ref, w_ref, *rest):
        if fused:
            act_ref, da_ref, dm_ref, dg_ref, ms = rest
        else:
            da_ref, dm_ref, dg_ref, ms = rest
        i, j = pl.program_id(0), pl.program_id(1)

        @pl.when((i == 0) & (j == 0))
        def _():
            dg_ref[...] = jnp.zeros_like(dg_ref)

        @pl.when(j == 0)
        def _():
            dxv = dx_ref[...]
            dmb = (dxv * g_ref[...]).astype(BF)
            ms[...] = dmb
            dm_ref[...] = dmb
            dg_ref[...] += jnp.sum(dxv * y_ref[...].astype(F32), axis=0, keepdims=True)

        da = _dg(ms[...], w_ref[...], NT)
        if fused:
            da_ref[...] = (da * (2.0 * act_ref[...].astype(F32))).astype(BF)
        else:
            da_ref[...] = da

    row = pl.BlockSpec((tm, D), lambda i, j: (i, 0))
    vec = pl.BlockSpec((1, D), lambda i, j: (0, 0))
    tile = pl.BlockSpec((tm, tn), lambda i, j: (i, j))
    in_specs = [row, vec, row, pl.BlockSpec((tn, D), lambda i, j: (j, 0))]
    args = [dx, gate, y, w]
    if fused:
        in_specs.append(tile)
        args.append(act)
    return pl.pallas_call(
        body, name=name, grid=(S // tm, K // tn),
        in_specs=in_specs, out_specs=[tile, row, vec],
        out_shape=[jax.ShapeDtypeStruct((S, K), BF if fused else F32), jax.ShapeDtypeStruct((S, D), BF),
                   jax.ShapeDtypeStruct((1, D), F32)],
        scratch_shapes=[pltpu.VMEM((tm, D), BF)],
        compiler_params=_cp("arbitrary", "arbitrary"),
    )(*args)


def _matmul_tn(a, b, *, name):
    S, Ka = a.shape
    P, _, Db = b.shape
    tk, tn, ts = _pick(Ka, 1024), _pick(Db, 1024), _pick(S, 1024, 16)
    npb = Db // tn

    def body(a_ref, b_ref, o_ref, acc):
        s = pl.program_id(2)

        @pl.when(s == 0)
        def _():
            acc[...] = jnp.zeros_like(acc)

        acc[...] += _dg(a_ref[...], b_ref[...], TN)

        @pl.when(s == pl.num_programs(2) - 1)
        def _():
            o_ref[...] = acc[...]

    return pl.pallas_call(
        body, name=name, grid=(Ka // tk, P * npb, S // ts),
        in_specs=[pl.BlockSpec((ts, tk), lambda i, j, s: (s, i)),
                  pl.BlockSpec((None, ts, tn), lambda i, j, s: (j // npb, s, j % npb))],
        out_specs=pl.BlockSpec((tk, tn), lambda i, j, s: (i, j)),
        out_shape=jax.ShapeDtypeStruct((Ka, P * Db), F32),
        scratch_shapes=[pltpu.VMEM((tk, tn), F32)],
        compiler_params=_cp("parallel", "parallel", "arbitrary"),
    )(a, b)


def _matmul_nt_lnbwd(g, w, x, nw, sc, dx_out, *, name):
    P, S, Dg = g.shape
    D = x.shape[1]
    tm, tk = _pick(S, 1024, 16), _pick(Dg, 1024)
    npb = Dg // tk
    nk = P * npb

    def body(g_ref, w_ref, x_ref, nw_ref, sc_ref, dxo_ref, dx_ref, dsc_ref, dsh_ref, dnw_ref, acc):
        i, k = pl.program_id(0), pl.program_id(1)

        @pl.when((i == 0) & (k == 0))
        def _():
            dsc_ref[...] = jnp.zeros_like(dsc_ref)
            dsh_ref[...] = jnp.zeros_like(dsh_ref)
            dnw_ref[...] = jnp.zeros_like(dnw_ref)

        @pl.when(k == 0)
        def _():
            acc[...] = jnp.zeros_like(acc)

        acc[...] += _dg(g_ref[...], w_ref[...], NT)

        @pl.when(k == nk - 1)
        def _():
            dh = acc[...]
            xv = x_ref[...]
            nwv = nw_ref[...]
            r = lax.rsqrt(jnp.mean(xv * xv, axis=-1, keepdims=True) + EPS)
            xr = xv * r
            dn = dh * (1.0 + sc_ref[...])
            dsc_ref[...] += jnp.sum(dh * (xr * nwv), axis=0, keepdims=True)
            dsh_ref[...] += jnp.sum(dh, axis=0, keepdims=True)
            dnw_ref[...] += jnp.sum(dn * xr, axis=0, keepdims=True)
            u = dn * nwv
            dx_ref[...] = dxo_ref[...] + r * (u - xr * jnp.mean(u * xr, axis=-1, keepdims=True))

    row = pl.BlockSpec((tm, D), lambda i, k: (i, 0))
    vec = pl.BlockSpec((1, D), lambda i, k: (0, 0))
    return pl.pallas_call(
        body, name=name, grid=(S // tm, nk),
        in_specs=[pl.BlockSpec((None, tm, tk), lambda i, k: (k // npb, i, k % npb)),
                  pl.BlockSpec((D, tk), lambda i, k: (0, k)), row, vec, vec, row],
        out_specs=[row, vec, vec, vec],
        out_shape=[jax.ShapeDtypeStruct((S, D), F32)] + [jax.ShapeDtypeStruct((1, D), F32)] * 3,
        scratch_shapes=[pltpu.VMEM((tm, D), F32)],
        compiler_params=_cp("arbitrary", "arbitrary"),
    )(g, w, x, nw, sc, dx_out)


def _loss_kernel(x, fw, tgt, *, name):
    S, D = x.shape
    tm = _pick(S, 512, 8)

    def body(x_ref, fw_ref, t_ref, l_ref, dx_ref, dfw_ref):
        @pl.when(pl.program_id(0) == 0)
        def _():
            l_ref[...] = jnp.zeros_like(l_ref)
            dfw_ref[...] = jnp.zeros_like(dfw_ref)

        xv = x_ref[...]
        fwv = fw_ref[...]
        r = lax.rsqrt(jnp.mean(xv * xv, axis=-1, keepdims=True) + EPS)
        xr = xv * r
        err = xr * fwv - t_ref[...]
        per_tok = jnp.mean(err * err, axis=-1, keepdims=True)
        l_ref[...] += 0.5 * jnp.sum(per_tok, axis=0, keepdims=True)
        dy = err * (1.0 / D)
        dfw_ref[...] += jnp.sum(dy * xr, axis=0, keepdims=True)
        u = dy * fwv
        dx_ref[...] = r * (u - xr * jnp.mean(u * xr, axis=-1, keepdims=True))

    row = pl.BlockSpec((tm, D), lambda i: (i, 0))
    vec = pl.BlockSpec((1, D), lambda i: (0, 0))
    return pl.pallas_call(
        body, name=name, grid=(S // tm,),
        in_specs=[row, vec, row],
        out_specs=[pl.BlockSpec((1, LANES), lambda i: (0, 0)), row, vec],
        out_shape=[jax.ShapeDtypeStruct((1, LANES), F32), jax.ShapeDtypeStruct((S, D), F32),
                   jax.ShapeDtypeStruct((1, D), F32)],
        compiler_params=_cp("arbitrary"),
    )(x, fw, tgt)


def _hg_lower_bound(lb3):
    mx = jnp.max(lb3, axis=0, keepdims=True)
    e = jnp.exp(lb3 - mx)
    p = e / jnp.sum(e, axis=0, keepdims=True)
    return p[0:1, :], p


def _hg_chunk_common(qr, fz, lbv):
    sq = _sigmoid(qr)
    q = qr * sq
    sig = _sigmoid(fz)
    f = lbv + (1.0 - lbv) * sig
    k = (1.0 - lbv) * (1.0 - sig)
    return q, sq, sig, f, k, jnp.log(f)


def _row_of(x, rows, r):
    return jnp.sum(jnp.where(rows == r, x, 0.0), axis=0, keepdims=True)


def _hg_fwd(proj, hg_lb, gn, *, name):
    S = proj.shape[0]
    D = proj.shape[1] // 4
    H = D // LANES
    HB = min(HG_HEADS_PER_STEP, H)
    W = HB * LANES
    C = HG_CHUNK
    T = _pick(S, HG_TOKENS_PER_STEP, C)
    nch, nb = T // C, S // T

    def body(q_ref, fz_ref, v_ref, g_ref, lb_ref, gn_ref, y_ref, o_ref, sts_ref, st):
        @pl.when(pl.program_id(1) == 0)
        def _():
            st[...] = jnp.zeros_like(st)

        lb_all, _ = _hg_lower_bound(lb_ref[...])
        gnv = gn_ref[...]
        ri = lax.broadcasted_iota(jnp.int32, (C, C), 0)
        ci_ = lax.broadcasted_iota(jnp.int32, (C, C), 1)
        low = ri >= ci_
        tri = jnp.where(low, 1.0, 0.0).astype(BF)
        rows = lax.broadcasted_iota(jnp.int32, (C, LANES), 0)

        def chunk(ci, carry):
            sl = pl.ds(pl.multiple_of(ci * C, C), C)
            for hh in range(HB):
                ls = slice(hh * LANES, (hh + 1) * LANES)
                q, _, _, _, k, logf = _hg_chunk_common(q_ref[sl, ls], fz_ref[sl, ls], lb_all[:, ls])
                vv = v_ref[sl, ls]
                gg = g_ref[sl, ls]
                G = _tri_dot(tri, logf)
                Gm = _row_of(G, rows, C // 2 - 1)
                Gl = _row_of(G, rows, C - 1)
                qt = q * jnp.exp(G - Gm)
                kt = k * jnp.exp(Gm - G)
                A = jnp.where(low, _dg1(qt, kt, NT), 0.0)
                Sv = st[hh]
                sts_ref[hh, ci] = Sv
                o = _dg1(A, vv, NN) + _dg1(q * jnp.exp(G), Sv, NT)
                st[hh] = Sv * jnp.exp(Gl) + _dg1(vv, k * jnp.exp(Gl - G), TN)
                r = lax.rsqrt(jnp.mean(o * o, axis=-1, keepdims=True) + EPS)
                y_ref[sl, ls] = ((o * r * gnv) * (gg * _sigmoid(gg))).astype(BF)
                o_ref[sl, ls] = o
            return carry

        lax.fori_loop(0, nch, chunk, 0)

    ng = H // HB

    def part(p):
        return pl.BlockSpec((T, W), lambda h, n: (n, p * ng + h))

    blk = pl.BlockSpec((T, W), lambda h, n: (n, h))
    return pl.pallas_call(
        body, name=name, grid=(ng, nb),
        in_specs=[part(0), part(1), part(2), part(3),
                  pl.BlockSpec((3, W), lambda h, n: (0, h)), pl.BlockSpec((1, LANES), lambda h, n: (0, 0))],
        out_specs=[blk, blk, pl.BlockSpec((HB, nch, LANES, LANES), lambda h, n: (h, n, 0, 0))],
        out_shape=[jax.ShapeDtypeStruct((S, D), BF), jax.ShapeDtypeStruct((S, D), F32),
                   jax.ShapeDtypeStruct((H, S // C, LANES, LANES), F32)],
        scratch_shapes=[pltpu.VMEM((HB, LANES, LANES), F32)],
        compiler_params=_cp("parallel", "arbitrary"),
    )(proj, proj, proj, proj, hg_lb, gn)


def _hg_bwd(proj, hg_lb, gn, o_all, states, dy, *, name):
    S = proj.shape[0]
    D = proj.shape[1] // 4
    H = D // LANES
    HB = min(HG_HEADS_PER_STEP, H)
    W = HB * LANES
    C = HG_CHUNK
    T = _pick(S, HG_TOKENS_PER_STEP, C)
    nch, nb = T // C, S // T

    def body(q_ref, fz_ref, v_ref, g_ref, lb_ref, gn_ref, o_ref, sts_ref, dy_ref,
             dp_ref, dlb_ref, dgn_ref, dst, dlb_acc):
        n = pl.program_id(1)

        @pl.when(n == 0)
        def _():
            dst[...] = jnp.zeros_like(dst)
            dlb_acc[...] = jnp.zeros_like(dlb_acc)
            dgn_ref[...] = jnp.zeros_like(dgn_ref)

        lb_all, p3 = _hg_lower_bound(lb_ref[...])
        gnv = gn_ref[...]
        ri = lax.broadcasted_iota(jnp.int32, (C, C), 0)
        ci_ = lax.broadcasted_iota(jnp.int32, (C, C), 1)
        low = ri >= ci_
        tri = jnp.where(low, 1.0, 0.0).astype(BF)
        triu = jnp.where(ri <= ci_, 1.0, 0.0).astype(BF)
        rows = lax.broadcasted_iota(jnp.int32, (C, LANES), 0)

        def chunk(cj, carry):
            ci = nch - 1 - cj
            sl = pl.ds(pl.multiple_of(ci * C, C), C)
            for hh in range(HB):
                ls = slice(hh * LANES, (hh + 1) * LANES)
                lbv = lb_all[:, ls]
                qr = q_ref[sl, ls]
                q, sq, sig, f, k, logf = _hg_chunk_common(qr, fz_ref[sl, ls], lbv)
                vv = v_ref[sl, ls]
                gg = g_ref[sl, ls]
                o = o_ref[sl, ls]
                dyv = dy_ref[sl, ls]
                G = _tri_dot(tri, logf)
                Gm = _row_of(G, rows, C // 2 - 1)
                Gl = _row_of(G, rows, C - 1)
                eG, e_qm, e_km, e_lk, eGl = jnp.exp(G), jnp.exp(G - Gm), jnp.exp(Gm - G), jnp.exp(Gl - G), jnp.exp(Gl)
                qt = q * e_qm
                kt = k * e_km
                A = jnp.where(low, _dg1(qt, kt, NT), 0.0)
                sg = _sigmoid(gg)
                r = lax.rsqrt(jnp.mean(o * o, axis=-1, keepdims=True) + EPS)
                on = o * r
                d_onw = dyv * (gg * sg)
                dgn_ref[hh] += jnp.sum(d_onw * on, axis=0, keepdims=True)
                dgg = dyv * (on * gnv) * (sg * (1.0 + gg * (1.0 - sg)))
                u = d_onw * gnv
                do = r * (u - on * jnp.mean(u * on, axis=-1, keepdims=True))
                Sv = sts_ref[hh, ci]
                dSv = dst[hh]
                dA = jnp.where(low, _dg3(do, vv, NT), 0.0)
                kdec = k * e_lk
                dv = _dg1(A, do, TN) + _dg1(kdec, dSv, NT)
                dq = _dg3(dA, kt, NN) * e_qm + eG * _dg3(do, Sv, NN)
                dk = _dg3(dA, qt, TN) * e_km + e_lk * _dg3(vv, dSv, NN)
                s_end = Sv * eGl + _dg3(vv, kdec, TN)
                dgl = jnp.sum(dSv * s_end, axis=0, keepdims=True)
                dG = q * dq - k * dk + jnp.where(rows == C - 1, dgl, 0.0)
                dlogf = _tri_dot(triu, dG) - f * dk
                dst[hh] = dSv * eGl + _dg1(do, q * eG, TN)
                dlf_f = dlogf / f
                dlb_acc[:, ls] += jnp.sum(dlf_f * (1.0 - sig), axis=0, keepdims=True)
                dp_ref[0, sl, ls] = (dq * (sq * (1.0 + qr * (1.0 - sq)))).astype(BF)
                dp_ref[1, sl, ls] = (dlf_f * (1.0 - lbv) * sig * (1.0 - sig)).astype(BF)
                dp_ref[2, sl, ls] = dv.astype(BF)
                dp_ref[3, sl, ls] = dgg.astype(BF)
            return carry

        lax.fori_loop(0, nch, chunk, 0)
        sel = jnp.where(lax.broadcasted_iota(jnp.int32, (3, W), 0) == 0, 1.0, 0.0)
        dlb_ref[...] = lb_all * (sel - p3) * dlb_acc[...]

    ng = H // HB

    def part(p):
        return pl.BlockSpec((T, W), lambda h, n: (nb - 1 - n, p * ng + h))

    blk = pl.BlockSpec((T, W), lambda h, n: (nb - 1 - n, h))
    return pl.pallas_call(
        body, name=name, grid=(ng, nb),
        in_specs=[part(0), part(1), part(2), part(3),
                  pl.BlockSpec((3, W), lambda h, n: (0, h)), pl.BlockSpec((1, LANES), lambda h, n: (0, 0)),
                  blk, pl.BlockSpec((HB, nch, LANES, LANES), lambda h, n: (h, nb - 1 - n, 0, 0)), blk],
        out_specs=[pl.BlockSpec((4, T, W), lambda h, n: (0, nb - 1 - n, h)),
                   pl.BlockSpec((3, W), lambda h, n: (0, h)),
                   pl.BlockSpec((HB, 1, LANES), lambda h, n: (h, 0, 0))],
        out_shape=[jax.ShapeDtypeStruct((4, S, D), BF), jax.ShapeDtypeStruct((3, D), F32),
                   jax.ShapeDtypeStruct((H, 1, LANES), F32)],
        scratch_shapes=[pltpu.VMEM((HB, LANES, LANES), F32), pltpu.VMEM((1, W), F32)],
        compiler_params=_cp("parallel", "arbitrary"),
    )(proj, proj, proj, proj, hg_lb, gn, o_all, states, dy)


def _log_sigmoid(u):
    return jnp.minimum(u, 0.0) - jnp.log(1.0 + jnp.exp(-jnp.abs(u)))


def _lane_put(base, lane, first, pieces):
    for n, p in enumerate(pieces):
        base = jnp.where(lane == first + n, p, base)
    return base


def _fox_cumsum(proj, bf_pad, *, name):
    S = proj.shape[0]
    D = proj.shape[1] // 5
    T = _pick(S, 256, 8)

    def body(fz_ref, b_ref, f_ref, carry):
        @pl.when(pl.program_id(0) == 0)
        def _():
            carry[...] = jnp.zeros_like(carry)

        logf = _log_sigmoid(fz_ref[...] + b_ref[...])
        tri = jnp.where(lax.broadcasted_iota(jnp.int32, (T, T), 0) >= lax.broadcasted_iota(jnp.int32, (T, T), 1),
                        1.0, 0.0).astype(BF)
        fv = _tri_dot(tri, logf) + carry[...]
        f_ref[...] = fv
        carry[...] = _row_of(fv, lax.broadcasted_iota(jnp.int32, (T, LANES), 0), T - 1)

    return pl.pallas_call(
        body, name=name, grid=(S // T,),
        in_specs=[pl.BlockSpec((T, LANES), lambda i: (i, 4 * D // LANES)), pl.BlockSpec((1, LANES), lambda i: (0, 0))],
        out_specs=pl.BlockSpec((T, LANES), lambda i: (i, 0)),
        out_shape=jax.ShapeDtypeStruct((S, LANES), F32),
        scratch_shapes=[pltpu.VMEM((1, LANES), F32)],
        compiler_params=_cp("arbitrary"),
    )(proj, bf_pad)


def _pair_stats(sq, lo):
    del lo
    a = lax.broadcasted_iota(jnp.int32, (LANES, LANES), 0) < FOX_DH
    b = lax.broadcasted_iota(jnp.int32, (LANES, LANES), 1) < FOX_DH
    avg = jnp.where(a == b, 1.0 / FOX_DH, 0.0).astype(BF)
    hi, mid, low = _split3(sq)
    return _dot(hi, avg) + _dot(mid, avg) + _dot(low, avg)


def _fox_prep(proj, fcum, qw2, kw2, *, name):
    S = proj.shape[0]
    D = proj.shape[1] // 5
    HP = D // LANES
    T = _pick(S, 512, 16)

    def body(q_ref, k_ref, v_ref, f_ref, qw_ref, kw_ref, qa_ref, ka_ref, va_ref, vt_ref):
        hp = pl.program_id(1)
        lane = lax.broadcasted_iota(jnp.int32, (T, LANES), 1)
        lo = lane < FOX_DH
        qv, kv, vv, fv = q_ref[...], k_ref[...], v_ref[...], f_ref[...]
        qn = qv * lax.rsqrt(_pair_stats(qv * qv, lo) + EPS) * qw_ref[...] * (0.125 * LOG2E)
        kn = kv * lax.rsqrt(_pair_stats(kv * kv, lo) + EPS) * kw_ref[...]
        ones_q = jnp.where((lane >= 67) & (lane <= 69), 1.0, 0.0)
        ones_k = jnp.where(((lane >= 64) & (lane <= 66)) | ((lane >= 70) & (lane <= 72)), 1.0, 0.0)
        ones_v = jnp.where((lane >= 64) & (lane <= 66), 1.0, 0.0)
        for hh in range(2):
            fh = jnp.sum(jnp.where(lane == 2 * hp + hh, fv, 0.0), axis=-1, keepdims=True) * LOG2E
            pieces = [p.astype(F32) for p in _split3(fh)]

            def half(x):
                return jnp.where(lo, x if hh == 0 else pltpu.roll(x, FOX_DH, 1), 0.0)

            qa_ref[hh] = _lane_put(half(qn) + ones_q, lane, 64, pieces).astype(BF)
            ka_ref[hh] = _lane_put(half(kn) + ones_k, lane, 67, [-p for p in pieces]).astype(BF)
            va = half(vv) + ones_v
            va_ref[hh] = va.astype(BF)
            vt_ref[hh] = va.T.astype(BF)

    def part(p):
        return pl.BlockSpec((T, LANES), lambda i, hp: (i, p * HP + hp))

    vec = pl.BlockSpec((1, LANES), lambda i, hp: (0, 0))
    aug = pl.BlockSpec((2, T, LANES), lambda i, hp: (hp, i, 0))
    return pl.pallas_call(
        body, name=name, grid=(S // T, HP),
        in_specs=[part(0), part(1), part(2), pl.BlockSpec((T, LANES), lambda i, hp: (i, 0)), vec, vec],
        out_specs=[aug, aug, aug, pl.BlockSpec((2, LANES, T), lambda i, hp: (hp, 0, i))],
        out_shape=[jax.ShapeDtypeStruct((2 * HP, S, LANES), BF)] * 3 + [jax.ShapeDtypeStruct((2 * HP, LANES, S), BF)],
        compiler_params=_cp("parallel", "arbitrary"),
    )(proj, proj, proj, fcum, qw2, kw2)


def _fox_block(S):
    return _pick(S, 256, 16)


def _fox_skip_bounds(fcum, qn_w, kn_w, nheads):
    S = fcum.shape[0]
    B = _fox_block(S)
    qk = 8.0 * LOG2E * 1.02 * jnp.max(jnp.abs(qn_w)) * jnp.max(jnp.abs(kn_w))
    thresh = -(2.0 * qk + 160.0)
    f2 = fcum[:, :nheads] * LOG2E
    first, last = f2[0::B], f2[B - 1::B]
    nb = S // B
    blk = jnp.arange(nb)
    dead = (first[0::2, None, :] - last[None, :, :]) < thresh
    jmin = jnp.sum(dead & (blk[None, :, None] < 2 * jnp.arange(nb // 2)[:, None, None]), axis=1)
    live = (first[:, None, :] - last[None, :, :]) >= thresh
    imax = blk[:, None] + jnp.sum(live & (blk[:, None, None] > blk[None, :, None]), axis=0)
    return jmin.T.astype(jnp.int32), imax.T.astype(jnp.int32)


def _fox_fwd(jmin, qa, ka, vat, proj, *, name):
    H, S, _ = qa.shape
    HP = H // 2
    D = HP * LANES
    B = _fox_block(S)
    BQ = 2 * B
    nq = S // BQ

    def body(jmin_ref, q_ref, k_ref, vt_ref, g_ref, y_ref, o_ref, q2_ref):
        hp, i = pl.program_id(0), pl.program_id(1)
        lane = lax.broadcasted_iota(jnp.int32, (BQ, LANES), 1)
        lo = lane < FOX_DH
        in_lse = (lane >= 70) & (lane <= 72)
        causal = lax.broadcasted_iota(jnp.int32, (BQ, BQ), 0) <= lax.broadcasted_iota(jnp.int32, (BQ, BQ), 1)
        row = lax.broadcasted_iota(jnp.int32, (LANES, BQ), 0)
        m0, acc0 = jnp.full((1, BQ), -jnp.inf, F32), jnp.zeros((LANES, BQ), F32)
        outs = []
        for hh in range(2):
            qb = q_ref[hh]

            def block(j, carry, masked=False):
                m, acc = carry
                sl = pl.ds(pl.multiple_of(j * BQ, BQ), BQ)
                st = _dg(k_ref[hh, sl, :], qb, NT)
                if masked:
                    st = jnp.where(causal, st, -jnp.inf)
                m_new = jnp.maximum(m, jnp.max(st, axis=0, keepdims=True))
                p = jnp.exp2(st - m_new)
                ph = p.astype(BF)
                pl_ = (p - ph.astype(F32)).astype(BF)
                vt = vt_ref[hh, :, sl]
                pv = _dot(jnp.concatenate([vt, vt], axis=1), jnp.concatenate([ph, pl_], axis=0))
                return m_new, acc * jnp.exp2(m - m_new) + pv

            carry = lax.fori_loop(jmin_ref[2 * hp + hh, i] // 2, i, block, (m0, acc0))
            m, acc = block(i, carry, masked=True)
            l = jnp.sum(jnp.where(row == FOX_DH, acc, 0.0), axis=0, keepdims=True)
            tile = acc / l
            for n, piece in enumerate(_split3(m + jnp.log2(l))):
                tile = jnp.where(row == 70 + n, -(piece.astype(F32)), tile)
            tile = tile.T
            outs.append(tile)
            q2_ref[hh] = jnp.where(in_lse, tile, qb.astype(F32)).astype(BF)
        o = jnp.where(lo, outs[0], pltpu.roll(outs[1], FOX_DH, 1))
        o_ref[...] = o
        y_ref[...] = (o * _sigmoid(g_ref[...])).astype(BF)

    blk = pl.BlockSpec((BQ, LANES), lambda hp, i, jm: (i, hp))
    qblk = pl.BlockSpec((2, BQ, LANES), lambda hp, i, jm: (hp, i, 0))
    full = pl.BlockSpec((2, S, LANES), lambda hp, i, jm: (hp, 0, 0))
    full_t = pl.BlockSpec((2, LANES, S), lambda hp, i, jm: (hp, 0, 0))
    return pl.pallas_call(
        body, name=name,
        grid_spec=pltpu.PrefetchScalarGridSpec(
            num_scalar_prefetch=1, grid=(HP, nq),
            in_specs=[qblk, full, full_t, pl.BlockSpec((BQ, LANES), lambda hp, i, jm: (i, 3 * HP + hp))],
            out_specs=[blk, blk, qblk]),
        out_shape=[jax.ShapeDtypeStruct((S, D), BF), jax.ShapeDtypeStruct((S, D), F32),
                   jax.ShapeDtypeStruct((H, S, LANES), BF)],
        compiler_params=_cp("parallel", "arbitrary"),
    )(jmin, qa, ka, vat, proj)


def _fox_bwd_prep(dy, o, proj, *, name):
    S, D = dy.shape
    HP = D // LANES
    T = _pick(S, 512, 16)

    def body(dy_ref, o_ref, g_ref, da_ref):
        lane = lax.broadcasted_iota(jnp.int32, (T, LANES), 1)
        lo = lane < FOX_DH
        do = (dy_ref[...] * _sigmoid(g_ref[...])).astype(BF).astype(F32)
        prod = do * o_ref[...]
        d_lo = jnp.sum(jnp.where(lo, prod, 0.0), axis=-1, keepdims=True)
        d_hi = jnp.sum(jnp.where(lo, 0.0, prod), axis=-1, keepdims=True)
        for hh, delta in enumerate((d_lo, d_hi)):
            base = jnp.where(lo, do if hh == 0 else pltpu.roll(do, FOX_DH, 1), 0.0)
            da_ref[hh] = _lane_put(base, lane, 64, [-(p.astype(F32)) for p in _split3(delta)]).astype(BF)

    blk = pl.BlockSpec((T, LANES), lambda i, hp: (i, hp))
    return pl.pallas_call(
        body, name=name, grid=(S // T, HP),
        in_specs=[blk, blk, pl.BlockSpec((T, LANES), lambda i, hp: (i, 3 * HP + hp))],
        out_specs=pl.BlockSpec((2, T, LANES), lambda i, hp: (hp, i, 0)),
        out_shape=jax.ShapeDtypeStruct((2 * HP, S, LANES), BF),
        compiler_params=_cp("parallel", "arbitrary"),
    )(dy, o, proj)


def _fox_bwd(imax, q2, ka, va, doa, *, name):
    H, S, _ = q2.shape
    B = _fox_block(S)
    nb = S // B

    def body(imax_ref, q_ref, do_ref, k_ref, v_ref, dq_ref, dk_ref, dv_ref, cs_ref):
        j = pl.program_id(1)
        end = imax_ref[pl.program_id(0), j] + 1

        @pl.when(j == 0)
        def _():
            dq_ref[...] = jnp.zeros_like(dq_ref)

        kb, vb = k_ref[...], v_ref[...]
        causal = lax.broadcasted_iota(jnp.int32, (B, B), 1) <= lax.broadcasted_iota(jnp.int32, (B, B), 0)

        def step(i, carry, masked=False, nblk=1):
            dk_acc, dv_acc, cs_acc = carry
            rows = nblk * B
            sl = pl.ds(pl.multiple_of(i * B, B), rows)
            qb, dob = q_ref[sl, :], do_ref[sl, :]
            s = _dg(qb, kb, NT)
            if masked:
                s = jnp.where(causal, s, -jnp.inf)
            p = jnp.exp2(s)
            ds = p * _dg(dob, vb, NT)
            dsb = ds.astype(BF)
            cs_acc = cs_acc + jnp.sum(ds.reshape(rows // 8, 8, B), axis=0)
            dv_acc = dv_acc + _dg(p.astype(BF), dob, TN)
            dk_acc = dk_acc + _dg(dsb, qb, TN)
            dq_ref[sl, :] += _dot(dsb, kb)
            return dk_acc, dv_acc, cs_acc

        zero = jnp.zeros((B, LANES), F32)
        carry = step(j, (zero, zero, jnp.zeros((8, B), F32)), masked=True)
        pos = j + 1
        for U in FOX_BWD_TILES:
            n = (end - pos) // U
            carry = lax.fori_loop(0, n, lambda ii, c, pos=pos, U=U: step(pos + U * ii, c, nblk=U), carry)
            pos = pos + U * n
        dk_acc, dv_acc, cs_acc = carry
        dk_ref[...] = dk_acc
        dv_ref[...] = dv_acc
        cs_ref[...] = jnp.sum(cs_acc, axis=0, keepdims=True)

    full = pl.BlockSpec((None, S, LANES), lambda h, j, im: (h, 0, 0))
    blk = pl.BlockSpec((None, B, LANES), lambda h, j, im: (h, j, 0))
    return pl.pallas_call(
        body, name=name,
        grid_spec=pltpu.PrefetchScalarGridSpec(
            num_scalar_prefetch=1, grid=(H, nb),
            in_specs=[full, full, blk, blk],
            out_specs=[full, blk, blk, pl.BlockSpec((None, 1, B), lambda h, j, im: (h, 0, j))]),
        out_shape=[jax.ShapeDtypeStruct((H, S, LANES), F32)] * 3 + [jax.ShapeDtypeStruct((H, 1, S), F32)],
        compiler_params=_cp("parallel", "arbitrary"),
    )(imax, q2, doa, ka, va)


def _fox_bwd_post(dqa, dka, dva, proj, dy, o, qw2, kw2, *, name):
    S, D = dy.shape
    HP = D // LANES
    T = _pick(S, 512, 16)

    def body(dq_ref, dk_ref, dv_ref, q_ref, k_ref, g_ref, dy_ref, o_ref, qw_ref, kw_ref, dp_ref, dqw_ref, dkw_ref):
        @pl.when((pl.program_id(0) == 0) & (pl.program_id(1) == 0))
        def _():
            dqw_ref[...] = jnp.zeros_like(dqw_ref)
            dkw_ref[...] = jnp.zeros_like(dkw_ref)

        lane = lax.broadcasted_iota(jnp.int32, (T, LANES), 1)
        lo = lane < FOX_DH

        def pair(ref):
            return jnp.where(lo, ref[0], pltpu.roll(ref[1], FOX_DH, 1))

        def norm_bwd(xv, w, dyn, dw_ref):
            r = lax.rsqrt(_pair_stats(xv * xv, lo) + EPS)
            xr = xv * r
            dw_ref[...] += jnp.sum(dyn * xr, axis=0, keepdims=True)
            u = dyn * w
            return r * (u - xr * _pair_stats(u * xr, lo))

        dp_ref[0] = norm_bwd(q_ref[...], qw_ref[...], pair(dq_ref) * 0.125, dqw_ref).astype(BF)
        dp_ref[1] = norm_bwd(k_ref[...], kw_ref[...], pair(dk_ref) * (1.0 / LOG2E), dkw_ref).astype(BF)
        dp_ref[2] = pair(dv_ref).astype(BF)
        sg = _sigmoid(g_ref[...])
        dp_ref[3] = (dy_ref[...] * o_ref[...] * sg * (1.0 - sg)).astype(BF)

    def part(p):
        return pl.BlockSpec((T, LANES), lambda i, hp: (i, p * HP + hp))

    aug = pl.BlockSpec((2, T, LANES), lambda i, hp: (hp, i, 0))
    blk = pl.BlockSpec((T, LANES), lambda i, hp: (i, hp))
    vec = pl.BlockSpec((1, LANES), lambda i, hp: (0, 0))
    return pl.pallas_call(
        body, name=name, grid=(S // T, HP),
        in_specs=[aug, aug, aug, part(0), part(1), part(3), blk, blk, vec, vec],
        out_specs=[pl.BlockSpec((4, T, LANES), lambda i, hp: (0, i, hp)), vec, vec],
        out_shape=[jax.ShapeDtypeStruct((5, S, D), BF), jax.ShapeDtypeStruct((1, LANES), F32),
                   jax.ShapeDtypeStruct((1, LANES), F32)],
        compiler_params=_cp("arbitrary", "arbitrary"),
    )(dqa, dka, dva, proj, proj, proj, dy, o, qw2, kw2)


def _fox_dfz(colsum, nheads, proj, bf_pad, dproj, *, name):
    S = colsum.shape[0]
    H = nheads
    D = dproj.shape[2]
    T = _pick(S, 256, 16)
    nb = S // T

    def body(cs_ref, fz_ref, b_ref, _, dp_ref, db_ref, carry):
        @pl.when(pl.program_id(0) == 0)
        def _():
            carry[...] = jnp.zeros_like(carry)
            db_ref[...] = jnp.zeros_like(db_ref)

        lane = lax.broadcasted_iota(jnp.int32, (T, LANES), 1)
        df = -cs_ref[...]
        triu = jnp.where(lax.broadcasted_iota(jnp.int32, (T, T), 0) <= lax.broadcasted_iota(jnp.int32, (T, T), 1),
                         1.0, 0.0).astype(BF)
        dlogf = _tri_dot(triu, df) + carry[...]
        carry[...] = _row_of(dlogf, lax.broadcasted_iota(jnp.int32, (T, LANES), 0), 0)
        dfz = jnp.where(lane < H, dlogf * _sigmoid(-(fz_ref[...] + b_ref[...])), 0.0)
        db_ref[...] += jnp.sum(dfz, axis=0, keepdims=True)
        dp_ref[...] = jnp.zeros_like(dp_ref)
        dp_ref[:, 0:LANES] = dfz.astype(BF)

    return pl.pallas_call(
        body, name=name, grid=(nb,),
        in_specs=[pl.BlockSpec((T, LANES), lambda i: (nb - 1 - i, 0)),
                  pl.BlockSpec((T, LANES), lambda i: (nb - 1 - i, 4 * D // LANES)),
                  pl.BlockSpec((1, LANES), lambda i: (0, 0)),
                  pl.BlockSpec(memory_space=pl.ANY)],
        out_specs=[pl.BlockSpec((None, T, D), lambda i: (4, nb - 1 - i, 0)), pl.BlockSpec((1, LANES), lambda i: (0, 0))],
        out_shape=[jax.ShapeDtypeStruct(dproj.shape, BF), jax.ShapeDtypeStruct((1, LANES), F32)],
        scratch_shapes=[pltpu.VMEM((1, LANES), F32)],
        input_output_aliases={3: 0},
        compiler_params=_cp("arbitrary"),
    )(colsum, proj, bf_pad, dproj)


def _mod_fwd(c16, w, b, *, name):
    L, D, N = w.shape
    tn = _pick(N, 512)

    def body(c_ref, w_ref, b_ref, o_ref):
        cv = c_ref[...]
        ca = (cv * _sigmoid(cv)).astype(BF)
        o_ref[...] = _dot(ca, w_ref[...].astype(BF)) + b_ref[...]

    return pl.pallas_call(
        body, name=name, grid=(L, N // tn),
        in_specs=[pl.BlockSpec((16, D), lambda l, j: (0, 0)), pl.BlockSpec((None, D, tn), lambda l, j: (l, 0, j)),
                  pl.BlockSpec((None, 1, tn), lambda l, j: (l, 0, j))],
        out_specs=pl.BlockSpec((None, 16, tn), lambda l, j: (l, 0, j)),
        out_shape=jax.ShapeDtypeStruct((L, 16, N), F32),
        compiler_params=_cp("parallel", "arbitrary"),
    )(c16, w, b)


def _mod_bwd(c16, dmod, *, name):
    L, _, N = dmod.shape
    D = c16.shape[1]
    tn = _pick(N, 512)

    def body(c_ref, d_ref, o_ref):
        cv = c_ref[...]
        ca = (cv * _sigmoid(cv)).astype(BF)
        o_ref[...] = _dg(ca, d_ref[...].astype(BF), TN)

    return pl.pallas_call(
        body, name=name, grid=(L, N // tn),
        in_specs=[pl.BlockSpec((16, D), lambda l, j: (0, 0)), pl.BlockSpec((None, 16, tn), lambda l, j: (l, 0, j))],
        out_specs=pl.BlockSpec((None, D, tn), lambda l, j: (l, 0, j)),
        out_shape=jax.ShapeDtypeStruct((L, D, N), F32),
        compiler_params=_cp("parallel", "arbitrary"),
    )(c16, dmod)


def _adamw_math(w, g, m, v):
    m = ADAM_B1 * m + (1.0 - ADAM_B1) * g
    v = ADAM_B2 * v + (1.0 - ADAM_B2) * (g * g)
    m_hat = m / (1.0 - ADAM_B1 ** ADAM_STEP)
    v_hat = v / (1.0 - ADAM_B2 ** ADAM_STEP)
    return -ADAM_LR * (m_hat / (jnp.sqrt(v_hat) + ADAM_EPS) + ADAM_WD * w), m, v


def _adamw(w, g, m, v, *, g_at=None, name):
    R, C = w.shape
    row0 = 0 if g_at is None else g_at[1]
    tr = min(math.gcd(row0, 256) if row0 else 256, -(-R // 8) * 8)
    g0 = row0 // tr
    if g_at is None:
        g_spec = pl.BlockSpec((tr, C), lambda i: (i, 0))
    else:
        g_spec = pl.BlockSpec((None, tr, C), lambda i: (g_at[0], g0 + i, 0))

    def body(w_ref, g_ref, m_ref, v_ref, d_ref, mo_ref, vo_ref):
        d, mn, vn = _adamw_math(w_ref[...], g_ref[...], m_ref[...], v_ref[...])
        d_ref[...] = d
        mo_ref[...] = mn
        vo_ref[...] = vn

    blk = pl.BlockSpec((tr, C), lambda i: (i, 0))
    return pl.pallas_call(
        body, name=name, grid=(pl.cdiv(R, tr),),
        in_specs=[blk, g_spec, blk, blk],
        out_specs=[blk, blk, blk],
        out_shape=[jax.ShapeDtypeStruct((R, C), F32)] * 3,
        compiler_params=_cp("parallel"),
    )(w, g, m, v)


def _sum_parts(parts, *, name):
    P, R, C = parts.shape

    def body(p_ref, o_ref):
        acc = p_ref[0]
        for p in range(1, P):
            acc = acc + p_ref[p]
        o_ref[...] = acc

    return pl.pallas_call(
        body, name=name, grid=(1,),
        in_specs=[pl.BlockSpec((P, R, C), lambda i: (0, 0, 0))],
        out_specs=pl.BlockSpec((R, C), lambda i: (0, 0)),
        out_shape=jax.ShapeDtypeStruct((R, C), F32),
        compiler_params=_cp("arbitrary"),
    )(parts)


def _add_halves(g4, recv, c_idx, *, name):
    _, _, Rh, C = g4.shape
    tr = _pick(Rh, 256, 16)

    def body(c_ref, a_ref, b_ref, o_ref):
        o_ref[...] = (a_ref[...] + b_ref[...].astype(F32)).astype(BF)

    return pl.pallas_call(
        body, name=name,
        grid_spec=pltpu.PrefetchScalarGridSpec(
            num_scalar_prefetch=1, grid=(4, pl.cdiv(Rh, tr)),
            in_specs=[pl.BlockSpec((None, None, tr, C), lambda j, r, c: (j, c[0], r, 0)),
                      pl.BlockSpec((None, tr, C), lambda j, r, c: (j, r, 0))],
            out_specs=pl.BlockSpec((None, tr, C), lambda j, r, c: (j, r, 0))),
        out_shape=jax.ShapeDtypeStruct((4, Rh, C), BF),
        compiler_params=_cp("parallel", "arbitrary"),
    )(c_idx, g4, recv)


def _add_four(g4, from_sibling, from_chips, pos, *, name):
    _, _, Rh, C = g4.shape
    tr = _pick(Rh, 256, 16)

    def body(p_ref, a_ref, s_ref, b_ref, o_ref):
        own = a_ref[...] + s_ref[...].astype(F32)
        o_ref[...] = ((own + b_ref[0].astype(F32)) + b_ref[1].astype(F32)) + b_ref[2].astype(F32)

    return pl.pallas_call(
        body, name=name,
        grid_spec=pltpu.PrefetchScalarGridSpec(
            num_scalar_prefetch=1, grid=(pl.cdiv(Rh, tr),),
            in_specs=[pl.BlockSpec((None, None, tr, C), lambda r, p: (p[0], p[1], r, 0)),
                      pl.BlockSpec((None, tr, C), lambda r, p: (p[0], r, 0)),
                      pl.BlockSpec((3, tr, C), lambda r, p: (0, r, 0))],
            out_specs=pl.BlockSpec((None, tr, C), lambda r, p: (p[1], r, 0))),
        out_shape=jax.ShapeDtypeStruct((2, Rh, C), F32),
        compiler_params=_cp("arbitrary"),
    )(pos, g4, from_sibling, from_chips)


HBM = pl.BlockSpec(memory_space=pltpu.HBM)


def _mesh_pos():
    return lax.axis_index("x"), lax.axis_index("y"), lax.axis_index("c")


def _other_chips(x, y):
    return [(1 - x, y), (x, 1 - y), (1 - x, 1 - y)]


def _allgather_small(xs, *, name):
    m_per, n = xs.shape

    def body(x_ref, out_ref, send_sems, recv_sems, local_sem):
        x, y, c = _mesh_pos()
        me, sibling = (x, y, c), (x, y, 1 - c)
        chips = _other_chips(x, y)

        def rows(px, py, pc):
            return out_ref.at[pl.ds((4 * px + 2 * py + pc) * m_per, m_per), :]

        def copy(k, block, to, src=None):
            return pltpu.make_async_remote_copy(
                src_ref=rows(*block) if src is None else src, dst_ref=rows(*block),
                send_sem=send_sems.at[k], recv_sem=recv_sems.at[k], device_id=to, device_id_type=MESH)

        mine = pltpu.make_async_copy(x_ref, rows(*me), local_sem)
        mine.start()
        first = [copy(0, me, sibling, src=x_ref)]
        first += [copy(1 + j, me, (*chip, c), src=x_ref) for j, chip in enumerate(chips)]
        for cp in first:
            cp.start()
        passed = [copy(4 + j, (*chip, c), sibling) for j, chip in enumerate(chips)]
        for j, chip in enumerate(chips):
            copy(1 + j, (*chip, c), me).wait_recv()
            passed[j].start()
        copy(0, sibling, me).wait_recv()
        for j, chip in enumerate(chips):
            copy(4 + j, (*chip, 1 - c), me).wait_recv()
        for cp in first + passed:
            cp.wait_send()
        mine.wait()

    return pl.pallas_call(
        body, name=name,
        out_shape=jax.ShapeDtypeStruct((N_DEV * m_per, n), xs.dtype),
        in_specs=[pl.BlockSpec(memory_space=pltpu.VMEM)],
        out_specs=pl.BlockSpec(memory_space=pltpu.VMEM),
        scratch_shapes=[pltpu.SemaphoreType.DMA((7,)), pltpu.SemaphoreType.DMA((7,)), pltpu.SemaphoreType.DMA],
    )(xs)


def _allgather_chip_slabs(slab, *, name):
    R, C = slab.shape
    Rh = R // 2

    def body(s_ref, out_ref, send_sems, recv_sems):
        x, y, c = _mesh_pos()
        sibling = (x, y, 1 - c)
        chips = _other_chips(x, y)

        def half(px, py, pc):
            return out_ref.at[2 * px + py, pl.ds(pc * Rh, Rh), :]

        def copy(k, block, to, src=None):
            return pltpu.make_async_remote_copy(
                src_ref=half(*block) if src is None else src, dst_ref=half(*block),
                send_sem=send_sems.at[k], recv_sem=recv_sems.at[k], device_id=to, device_id_type=MESH)

        first = [copy(j, (x, y, c), (*chip, c), src=s_ref.at[pl.ds(c * Rh, Rh), :]) for j, chip in enumerate(chips)]
        for cp in first:
            cp.start()
        passed = [copy(3 + j, (*chip, c), sibling) for j, chip in enumerate(chips)]
        for j, chip in enumerate(chips):
            copy(j, (*chip, c), (x, y, c)).wait_recv()
            passed[j].start()
        for j, chip in enumerate(chips):
            copy(3 + j, (*chip, 1 - c), (x, y, c)).wait_recv()
        for cp in first + passed:
            cp.wait_send()

    return pl.pallas_call(
        body, name=name,
        out_shape=jax.ShapeDtypeStruct((N_CHIPS, R, C), slab.dtype),
        in_specs=[HBM], out_specs=HBM,
        scratch_shapes=[pltpu.SemaphoreType.DMA((6,)), pltpu.SemaphoreType.DMA((6,))],
    )(slab)


def _swap_halves(g4, *, name):
    _, _, Rh, C = g4.shape

    def body(g_ref, out_ref, send_sems, recv_sems):
        x, y, c = _mesh_pos()
        copies = [pltpu.make_async_remote_copy(
            src_ref=g_ref.at[j, 1 - c], dst_ref=out_ref.at[j], send_sem=send_sems.at[j], recv_sem=recv_sems.at[j],
            device_id=(x, y, 1 - c), device_id_type=MESH) for j in range(N_CHIPS)]
        for cp in copies:
            cp.start()
        for cp in copies:
            cp.wait()

    return pl.pallas_call(
        body, name=name,
        out_shape=jax.ShapeDtypeStruct((N_CHIPS, Rh, C), g4.dtype),
        in_specs=[HBM], out_specs=HBM,
        scratch_shapes=[pltpu.SemaphoreType.DMA((N_CHIPS,)), pltpu.SemaphoreType.DMA((N_CHIPS,))],
    )(g4)


def _scatter_partials(part, *, name):
    _, Rh, C = part.shape

    def body(p_ref, out_ref, send_sems, recv_sems):
        x, y, c = _mesh_pos()
        copies = [pltpu.make_async_remote_copy(
            src_ref=p_ref.at[2 * px + py], dst_ref=out_ref.at[j], send_sem=send_sems.at[j], recv_sem=recv_sems.at[j],
            device_id=(px, py, c), device_id_type=MESH) for j, (px, py) in enumerate(_other_chips(x, y))]
        for cp in copies:
            cp.start()
        for cp in copies:
            cp.wait()

    return pl.pallas_call(
        body, name=name,
        out_shape=jax.ShapeDtypeStruct((3, Rh, C), part.dtype),
        in_specs=[HBM], out_specs=HBM,
        scratch_shapes=[pltpu.SemaphoreType.DMA((3,)), pltpu.SemaphoreType.DMA((3,))],
    )(part)


def _join_halves(buf, *, name):
    def body(b_ref, out_ref, send_sem, recv_sem):
        x, y, c = _mesh_pos()
        cp = pltpu.make_async_remote_copy(
            src_ref=b_ref.at[c], dst_ref=out_ref.at[c], send_sem=send_sem, recv_sem=recv_sem,
            device_id=(x, y, 1 - c), device_id_type=MESH)
        cp.start()
        cp.wait()

    return pl.pallas_call(
        body, name=name,
        out_shape=jax.ShapeDtypeStruct(buf.shape, buf.dtype),
        in_specs=[HBM], out_specs=HBM, input_output_aliases={0: 0},
        scratch_shapes=[pltpu.SemaphoreType.DMA, pltpu.SemaphoreType.DMA],
    )(buf)


def _pad_rows(a, mult):
    pad = (-a.shape[0]) % mult
    return a if pad == 0 else jnp.pad(a, ((0, pad),) + ((0, 0),) * (a.ndim - 1))


def _local_step(x, target, mod, wts, small):
    S, D = x.shape
    HP = D // LANES
    row = lambda v: v.reshape(1, -1)
    msplit = [[row(mod[i, k * D:(k + 1) * D]) for k in range(6)] for i in range(2)]
    gw, gs = {}, {}
    dmod = [[None] * 6 for _ in range(2)]

    sh1, sc1, g1, sh2, sc2, g2 = msplit[0]
    n1w0, n2w0 = row(small["norm1_w"][0]), row(small["norm2_w"][0])
    proj0, h1_0 = _ln_matmul(x, n1w0, sc1, sh1, wts["hg_w_in"], relu2=False, name="hg_in_proj")
    gn = small["hg_gn_w"].reshape(1, LANES)
    ypre0, o0, states = _hg_fwd(proj0, small["hg_lb"], gn, name="hg_fwd")
    x1, ymix0 = _matmul_resid(ypre0, wts["hg_w_out"], x, g1, name="hg_out_proj")
    a0, u0, h2_0 = _ln_matmul(x1, n2w0, sc2, sh2, wts["mlp_w1_0"], relu2=True, name="mlp0_up")
    x2, ymlp0 = _matmul_resid(u0, wts["mlp_w2_0"], x1, g2, name="mlp0_down")

    sh1b, sc1b, g1b, sh2b, sc2b, g2b = msplit[1]
    n1w1, n2w1 = row(small["norm1_w"][1]), row(small["norm2_w"][1])
    proj1, h1_1 = _ln_matmul(x2, n1w1, sc1b, sh1b, wts["fox_w_in"], relu2=False, name="fox_in_proj")
    nheads = 2 * HP
    bf_pad = jnp.pad(small["fox_b_f"].reshape(1, nheads), ((0, 0), (0, LANES - nheads)))
    qw2 = jnp.tile(small["fox_qn_w"].reshape(1, FOX_DH), (1, 2))
    kw2 = jnp.tile(small["fox_kn_w"].reshape(1, FOX_DH), (1, 2))
    fcum = _fox_cumsum(proj1, bf_pad, name="fox_cumsum")
    qa, ka, va, vat = _fox_prep(proj1, fcum, qw2, kw2, name="fox_prep")
    jmin, imax = _fox_skip_bounds(fcum, small["fox_qn_w"], small["fox_kn_w"], nheads)
    ypre1, o1, q2 = _fox_fwd(jmin, qa, ka, vat, proj1, name="fox_fwd")
    x3, ymix1 = _matmul_resid(ypre1, wts["fox_w_out"], x2, g1b, name="fox_out_proj")
    a1, u1, h2_1 = _ln_matmul(x3, n2w1, sc2b, sh2b, wts["mlp_w1_1"], relu2=True, name="mlp1_up")
    x4, ymlp1 = _matmul_resid(u1, wts["mlp_w2_1"], x3, g2b, name="mlp1_down")

    loss, dx4, dfw = _loss_kernel(x4, row(small["final_w"]), target, name="loss")
    gs["final_w"] = dfw.reshape(-1)

    def mlp_bwd(i, dx_out, x_in, h2, a, u, ymlp, n2w, sc2_, g2_):
        dz, dm, dg2 = _gate_matmul_nt(dx_out, g2_, ymlp, wts[f"mlp_w2_{i}"], a, name=f"mlp{i}_down_bwd")
        gw[f"mlp_w2_{i}"] = _matmul_tn(u, dm[None], name=f"mlp{i}_dw2")
        gw[f"mlp_w1_{i}"] = _matmul_tn(h2, dz[None], name=f"mlp{i}_dw1")
        dx_in, dsc, dsh, dnw = _matmul_nt_lnbwd(dz[None], wts[f"mlp_w1_{i}"], x_in, n2w, sc2_, dx_out,
                                                name=f"mlp{i}_up_bwd")
        dmod[i][3], dmod[i][4], dmod[i][5] = dsh, dsc, dg2
        return dx_in, dnw

    dx3, dn2w1 = mlp_bwd(1, dx4, x3, h2_1, a1, u1, ymlp1, n2w1, sc2b, g2b)
    dyp1, dm1, dg1b = _gate_matmul_nt(dx3, g1b, ymix1, wts["fox_w_out"], None, name="fox_out_bwd")
    gw["fox_w_out"] = _matmul_tn(ypre1, dm1[None], name="fox_dw_out")
    doa = _fox_bwd_prep(dyp1, o1, proj1, name="fox_bwd_prep")
    dqa, dka, dva, colsum = _fox_bwd(imax, q2, ka, va, doa, name="fox_bwd")
    colsum = jnp.pad(colsum[:, 0, :].T, ((0, 0), (0, LANES - nheads)))
    dproj1, dqw, dkw = _fox_bwd_post(dqa, dka, dva, proj1, dyp1, o1, qw2, kw2, name="fox_bwd_post")
    dproj1, dbf = _fox_dfz(colsum, nheads, proj1, bf_pad, dproj1, name="fox_dfz")
    gw["fox_w_in"] = _matmul_tn(h1_1, dproj1, name="fox_dw_in")
    dx2, dsc, dsh, dn1w1 = _matmul_nt_lnbwd(dproj1, wts["fox_w_in"], x2, n1w1, sc1b, dx3, name="fox_in_bwd")
    dmod[1][0], dmod[1][1], dmod[1][2] = dsh, dsc, dg1b
    gs["fox_qn_w"] = dqw[0, :FOX_DH] + dqw[0, FOX_DH:]
    gs["fox_kn_w"] = dkw[0, :FOX_DH] + dkw[0, FOX_DH:]
    gs["fox_b_f"] = dbf[0, :nheads]

    dx1, dn2w0 = mlp_bwd(0, dx2, x1, h2_0, a0, u0, ymlp0, n2w0, sc2, g2)
    dyp0, dm0, dg1 = _gate_matmul_nt(dx1, g1, ymix0, wts["hg_w_out"], None, name="hg_out_bwd")
    gw["hg_w_out"] = _matmul_tn(ypre0, dm0[None], name="hg_dw_out")
    dproj0, dlb, dgn = _hg_bwd(proj0, small["hg_lb"], gn, o0, states, dyp0, name="hg_bwd")
    gw["hg_w_in"] = _matmul_tn(h1_0, dproj0, name="hg_dw_in")
    dx0, dsc, dsh, dn1w0 = _matmul_nt_lnbwd(dproj0, wts["hg_w_in"], x, n1w0, sc1, dx1, name="hg_in_bwd")
    dmod[0][0], dmod[0][1], dmod[0][2] = dsh, dsc, dg1
    gs["hg_lb"] = dlb
    gs["hg_gn_w"] = jnp.sum(dgn, axis=0)

    gs["norm1_w"] = jnp.concatenate([dn1w0, dn1w1], axis=0)
    gs["norm2_w"] = jnp.concatenate([dn2w0, dn2w1], axis=0)
    gs["dmod"] = jnp.stack([jnp.concatenate(dmod[i], axis=1)[0] for i in range(2)])
    return loss, dx0, gw, gs


SMALL_NAMES = ["norm1_w", "norm2_w", "hg_lb", "hg_gn_w", "fox_b_f", "fox_qn_w", "fox_kn_w", "final_w"]


def _pack_small(d, names):
    rows, offs, r0 = [], {}, 0
    for n in names:
        flat = d[n].reshape(-1)
        nr = -(-flat.shape[0] // LANES)
        rows.append(jnp.pad(flat, (0, nr * LANES - flat.shape[0])).reshape(nr, LANES))
        offs[n] = (r0, nr)
        r0 += nr
    return jnp.concatenate(rows, axis=0), offs


def _unpack_small(packed, offs, name, like):
    r0, nr = offs[name]
    return packed[r0:r0 + nr].reshape(-1)[:like.size].reshape(like.shape)


def kernel(x, c, w_mod, b_mod, norm1_w, norm2_w, hg_w_in, hg_w_out, hg_lb, hg_gn_w, fox_w_in, fox_b_f, fox_qn_w, fox_kn_w, fox_w_out, mlp_w1, mlp_w2, final_w, loss_target, m_w_mod, m_b_mod, m_norm1_w, m_norm2_w, m_hg_w_in, m_hg_w_out, m_hg_lb, m_hg_gn_w, m_fox_w_in, m_fox_b_f, m_fox_qn_w, m_fox_kn_w, m_fox_w_out, m_mlp_w1, m_mlp_w2, m_final_w, v_w_mod, v_b_mod, v_norm1_w, v_norm2_w, v_hg_w_in, v_hg_w_out, v_hg_lb, v_hg_gn_w, v_fox_w_in, v_fox_b_f, v_fox_qn_w, v_fox_kn_w, v_fox_w_out, v_mlp_w1, v_mlp_w2, v_final_w):
    S, D = x.shape[1], x.shape[2]
    nheads = D // FOX_DH
    ax, ay, ac = _mesh_pos()
    chip = 2 * ax + ay
    dev = 2 * chip + ac
    xs, tgt = x.reshape(S, D), loss_target.reshape(S, D)

    c_all = _allgather_small(_pad_rows(c.reshape(-1, LANES), 8), name="gather_c")
    c_all = c_all.reshape(N_DEV, -1)[:, :D]
    c16 = _pad_rows(c_all, 16)
    nmod = w_mod.shape[2]
    b_shard = lax.dynamic_slice_in_dim(b_mod, chip * nmod, nmod, axis=1)
    mod_shard = _mod_fwd(c16, w_mod, b_shard[:, None, :], name="mod_fwd")[:, :N_DEV]
    mod_all = _allgather_small(mod_shard.reshape(-1, LANES), name="gather_mod")
    mod_all = mod_all.reshape(N_CHIPS, 2, 2, N_DEV, nmod)[:, 0]
    mod = lax.dynamic_index_in_dim(mod_all, dev, axis=2, keepdims=False)
    mod = mod.transpose(1, 0, 2).reshape(2, N_CHIPS * nmod)

    fox_rows = fox_w_in.shape[2]
    layout = [[("hg_w_in", hg_w_in[0]), ("mlp_w1", mlp_w1.reshape(2 * D, D)), ("hg_w_out", hg_w_out[0])],
              [("mlp_w2", mlp_w2.reshape(2 * D, D)), ("fox_w_out", fox_w_out[0]),
               ("fox_w_in", fox_w_in[0].reshape(fox_rows, D))]]
    Rh = -(-max(sum(a.shape[0] for _, a in half) for half in layout) // 16) * 16
    place, parts = {}, []
    for h, half in enumerate(layout):
        off = 0
        for n, a in half:
            place[n] = (h, off, a.shape[0])
            off += a.shape[0]
        parts.append(jnp.pad(jnp.concatenate([a.astype(BF) for _, a in half], axis=0), ((0, Rh - off), (0, 0))))
    slab = jnp.concatenate(parts, axis=0)
    R = 2 * Rh
    gathered = _allgather_chip_slabs(slab, name="gather_weights")
    gathered = lax.dynamic_update_index_in_dim(gathered, slab, chip, 0)

    def seg(n):
        h, off, rows = place[n]
        return gathered[:, h * Rh + off:h * Rh + off + rows, :]

    col = lambda g: g.transpose(1, 0, 2).reshape(g.shape[1], -1)
    rowsh = lambda g: g.reshape(-1, g.shape[2])
    w1 = seg("mlp_w1").reshape(N_CHIPS, 2, D, D)
    w2 = seg("mlp_w2").reshape(N_CHIPS, 2, D, D)
    fox_in = col(seg("fox_w_in").reshape(N_CHIPS, D, fox_rows))
    wts = {
        "hg_w_in": col(seg("hg_w_in")), "hg_w_out": rowsh(seg("hg_w_out")), "fox_w_out": rowsh(seg("fox_w_out")),
        "mlp_w1_0": col(w1[:, 0]), "mlp_w1_1": col(w1[:, 1]), "mlp_w2_0": rowsh(w2[:, 0]), "mlp_w2_1": rowsh(w2[:, 1]),
        "fox_w_in": jnp.pad(fox_in, ((0, 0), (0, 5 * D - fox_in.shape[1]))),
    }
    small = {"norm1_w": norm1_w, "norm2_w": norm2_w, "hg_lb": hg_lb, "hg_gn_w": hg_gn_w, "fox_b_f": fox_b_f,
             "fox_qn_w": fox_qn_w, "fox_kn_w": fox_kn_w, "final_w": final_w}

    loss_part, grad_x, gw, gs = _local_step(xs, tgt, mod, wts, small)
    loss = lax.psum(loss_part[0, 0], ("x", "y", "c"))

    def uncol(g, n):
        return g.reshape(g.shape[0], N_CHIPS, n).transpose(1, 0, 2)

    gseg = {
        "hg_w_in": uncol(gw["hg_w_in"], D), "hg_w_out": gw["hg_w_out"].reshape(N_CHIPS, D // 4, D),
        "fox_w_out": gw["fox_w_out"].reshape(N_CHIPS, D // 4, D),
        "mlp_w1": jnp.concatenate([uncol(gw["mlp_w1_0"], D), uncol(gw["mlp_w1_1"], D)], axis=1),
        "mlp_w2": jnp.concatenate([gw["mlp_w2_0"].reshape(N_CHIPS, D, D), gw["mlp_w2_1"].reshape(N_CHIPS, D, D)], axis=1),
        "fox_w_in": uncol(gw["fox_w_in"][:, :4 * fox_rows], fox_rows).reshape(N_CHIPS, fox_rows, D),
    }
    ghalves = []
    for half in layout:
        gh = jnp.concatenate([gseg[n] for n, _ in half], axis=1)
        ghalves.append(jnp.pad(gh, ((0, 0), (0, Rh - gh.shape[1]), (0, 0))))
    g4 = jnp.stack(ghalves, axis=1)
    from_sibling = _swap_halves(g4.astype(BF), name="rs_swap_halves")
    chip_part = _add_halves(g4, from_sibling, ac.reshape(1), name="rs_add_halves")
    from_chips = _scatter_partials(chip_part, name="rs_scatter")
    my_half = _add_four(g4, from_sibling, from_chips, jnp.stack([chip, ac]), name="rs_add_chips")
    gshard = _join_halves(my_half, name="rs_join")

    names = ["dmod"] + SMALL_NAMES
    packed, offs = _pack_small(gs, names)
    packed = _pad_rows(packed, 8)
    rp = packed.shape[0]
    parts = _allgather_small(packed, name="gather_small").reshape(N_DEV, rp, LANES)
    total = _sum_parts(parts, name="sum_small")
    r0, nr = offs["dmod"]
    dmod_all = parts[:, r0:r0 + nr].reshape(N_DEV, 2, N_CHIPS * nmod)
    dmod_shard = lax.dynamic_slice_in_dim(dmod_all, chip * nmod, nmod, axis=2).transpose(1, 0, 2)
    g_w_mod = _mod_bwd(c16, jnp.pad(dmod_shard, ((0, 0), (0, 16 - N_DEV), (0, 0))), name="mod_bwd")

    grads = {"w_mod": g_w_mod, "b_mod": _unpack_small(total, offs, "dmod", b_mod)}
    for n in SMALL_NAMES:
        grads[n] = _unpack_small(total, offs, n, small[n])

    given = dict(w_mod=(w_mod, m_w_mod, v_w_mod), b_mod=(b_mod, m_b_mod, v_b_mod), norm1_w=(norm1_w, m_norm1_w, v_norm1_w),
                 norm2_w=(norm2_w, m_norm2_w, v_norm2_w), hg_w_in=(hg_w_in, m_hg_w_in, v_hg_w_in),
                 hg_w_out=(hg_w_out, m_hg_w_out, v_hg_w_out), hg_lb=(hg_lb, m_hg_lb, v_hg_lb),
                 hg_gn_w=(hg_gn_w, m_hg_gn_w, v_hg_gn_w), fox_w_in=(fox_w_in, m_fox_w_in, v_fox_w_in),
                 fox_b_f=(fox_b_f, m_fox_b_f, v_fox_b_f), fox_qn_w=(fox_qn_w, m_fox_qn_w, v_fox_qn_w),
                 fox_kn_w=(fox_kn_w, m_fox_kn_w, v_fox_kn_w), fox_w_out=(fox_w_out, m_fox_w_out, v_fox_w_out),
                 mlp_w1=(mlp_w1, m_mlp_w1, v_mlp_w1), mlp_w2=(mlp_w2, m_mlp_w2, v_mlp_w2), final_w=(final_w, m_final_w, v_final_w))
    upd = {}

    for n, (h, off, rows) in place.items():
        w, m, v = given[n]
        flat = lambda a: a.reshape(rows, D)
        d, mn, vn = _adamw(flat(w), gshard, flat(m), flat(v), g_at=(h, off), name=f"adamw_{n}")
        grads[n] = gshard[h, off:off + rows].reshape(w.shape)
        upd[n] = tuple(a.reshape(w.shape) for a in (d, mn, vn))

    w, m, v = given["w_mod"]
    flat = lambda a: a.reshape(-1, nmod)
    upd["w_mod"] = tuple(a.reshape(w.shape) for a in _adamw(flat(w), flat(g_w_mod), flat(m), flat(v), name="adamw_w_mod"))

    snames = ["b_mod"] + SMALL_NAMES
    pw, soffs = _pack_small({n: given[n][0] for n in snames}, snames)
    pm, _ = _pack_small({n: given[n][1] for n in snames}, snames)
    pv, _ = _pack_small({n: given[n][2] for n in snames}, snames)
    pg, _ = _pack_small({n: grads[n] for n in snames}, snames)
    pw, pm, pv, pg = (_pad_rows(a, 8) for a in (pw, pm, pv, pg))
    sd, smn, svn = _adamw(pw, pg, pm, pv, name="adamw_small")
    for n in snames:
        like = given[n][0]
        upd[n] = tuple(_unpack_small(a, soffs, n, like) for a in (sd, smn, svn))

    order = ["w_mod", "b_mod", "norm1_w", "norm2_w", "hg_w_in", "hg_w_out", "hg_lb", "hg_gn_w", "fox_w_in", "fox_b_f",
             "fox_qn_w", "fox_kn_w", "fox_w_out", "mlp_w1", "mlp_w2", "final_w"]
    return (loss, grad_x.reshape(x.shape), *[grads[n] for n in order], *[upd[n][0] for n in order],
            *[upd[n][1] for n in order], *[upd[n][2] for n in order])
```

```python
import math

import jax
import jax.numpy as jnp
from jax import lax
from jax.experimental import pallas as pl
from jax.experimental.pallas import tpu as pltpu

EPS = 1e-6
ADAM_LR, ADAM_B1, ADAM_B2, ADAM_EPS, ADAM_WD, ADAM_STEP = 0.001, 0.9, 0.999, 1e-08, 0.01, 10

F32 = jnp.float32
BF = jnp.bfloat16
LANES = 128
HG_CHUNK = 64
HG_HEADS_PER_STEP = 8
HG_TOKENS_PER_STEP = 256
FOX_BWD_TILES = (8, 4, 2, 1)
LOG2E = 1.4426950408889634
FOX_DH = 64
N_CHIPS = 4
N_DEV = 8
VMEM_LIMIT = 48 * 1024 * 1024
MESH = pl.DeviceIdType.MESH

NT = (((1,), (1,)), ((), ()))
TN = (((0,), (0,)), ((), ()))


def _pick(n, pref, mult=LANES):
    if n <= pref:
        return n
    t = (pref // mult) * mult
    while t >= mult:
        if n % t == 0:
            return t
        t -= mult
    raise ValueError((n, pref, mult))


def _cp(*sem):
    return pltpu.CompilerParams(dimension_semantics=sem, vmem_limit_bytes=VMEM_LIMIT)


def _dot(a, b):
    return jnp.dot(a, b, preferred_element_type=F32)


def _dg(a, b, dims):
    return lax.dot_general(a, b, dims, preferred_element_type=F32)


def _split3(x):
    hi = x.astype(BF)
    r1 = x - hi.astype(F32)
    mid = r1.astype(BF)
    lo = (r1 - mid.astype(F32)).astype(BF)
    return hi, mid, lo


def _tri_dot(tri, x):
    hi, mid, lo = _split3(x)
    return _dot(tri, hi) + _dot(tri, mid) + _dot(tri, lo)


def _dg3(a, b, dims):
    ah, bh = a.astype(BF), b.astype(BF)
    al, bl = (a - ah.astype(F32)).astype(BF), (b - bh.astype(F32)).astype(BF)
    return _dg(ah, bh, dims) + _dg(ah, bl, dims) + _dg(al, bh, dims)


def _dg1(a, b, dims):
    return _dg(a.astype(BF), b.astype(BF), dims)


NN = (((1,), (0,)), ((), ()))


def _sigmoid(x):
    return jax.nn.sigmoid(x)


def _ln_matmul(x, nw, sc, sh, w, *, relu2, name):
    S, D = x.shape
    N = w.shape[1]
    tm, tn = _pick(S, 1024, 16), _pick(N, 1024)

    def body(x_ref, nw_ref, sc_ref, sh_ref, w_ref, *rest):
        outs, hs = rest[:-1], rest[-1]
        h_ref = outs[-1]

        @pl.when(pl.program_id(1) == 0)
        def _():
            xv = x_ref[...]
            r = lax.rsqrt(jnp.mean(xv * xv, axis=-1, keepdims=True) + EPS)
            hb = ((xv * r * nw_ref[...]) * (1.0 + sc_ref[...]) + sh_ref[...]).astype(BF)
            hs[...] = hb
            h_ref[...] = hb

        z = _dot(hs[...], w_ref[...])
        if relu2:
            a = jnp.maximum(z, 0.0)
            outs[0][...] = a.astype(BF)
            outs[1][...] = (a * a).astype(BF)
        else:
            outs[0][...] = z

    vec = pl.BlockSpec((1, D), lambda i, j: (0, 0))
    tile = pl.BlockSpec((tm, tn), lambda i, j: (i, j))
    if relu2:
        out_shape = [jax.ShapeDtypeStruct((S, N), BF), jax.ShapeDtypeStruct((S, N), BF)]
        out_specs = [tile, tile]
    else:
        out_shape = [jax.ShapeDtypeStruct((S, N), F32)]
        out_specs = [tile]
    out_shape.append(jax.ShapeDtypeStruct((S, D), BF))
    out_specs.append(pl.BlockSpec((tm, D), lambda i, j: (i, 0)))
    return pl.pallas_call(
        body, name=name, grid=(S // tm, N // tn),
        in_specs=[pl.BlockSpec((tm, D), lambda i, j: (i, 0)), vec, vec, vec,
                  pl.BlockSpec((D, tn), lambda i, j: (0, j))],
        out_specs=out_specs, out_shape=out_shape,
        scratch_shapes=[pltpu.VMEM((tm, D), BF)],
        compiler_params=_cp("parallel", "arbitrary"),
    )(x, nw, sc, sh, w)


def _matmul_resid(a, w, x, gate, *, name):
    S, K = a.shape
    D = w.shape[1]
    big = 1024 if K <= 1024 else 512
    tm, tn = _pick(S, big, 16), _pick(D, big)

    def body(a_ref, w_ref, x_ref, g_ref, o_ref, y_ref):
        y = _dot(a_ref[...], w_ref[...])
        y_ref[...] = y.astype(BF)
        o_ref[...] = x_ref[...] + g_ref[...] * y

    tile = pl.BlockSpec((tm, tn), lambda i, j: (i, j))
    return pl.pallas_call(
        body, name=name, grid=(S // tm, D // tn),
        in_specs=[pl.BlockSpec((tm, K), lambda i, j: (i, 0)), pl.BlockSpec((K, tn), lambda i, j: (0, j)),
                  tile, pl.BlockSpec((1, tn), lambda i, j: (0, j))],
        out_specs=[tile, tile],
        out_shape=[jax.ShapeDtypeStruct((S, D), F32), jax.ShapeDtypeStruct((S, D), BF)],
        compiler_params=_cp("parallel", "arbitrary"),
    )(a, w, x, gate)


def _gate_matmul_nt(dx, gate, y, w, act, *, name):
    S, D = dx.shape
    K = w.shape[0]
    tm, tn = _pick(S, 1024, 16), _pick(K, 1024)
    fused = act is not None

    def body(dx_ref, g_ref, y_ref, w_ref, *rest):
        if fused:
            act_ref, da_ref, dm_ref, dg_ref, ms = rest
        else:
            da_ref, dm_ref, dg_ref, ms = rest
        i, j = pl.program_id(0), pl.program_id(1)

        @pl.when((i == 0) & (j == 0))
        def _():
            dg_ref[...] = jnp.zeros_like(dg_ref)

        @pl.when(j == 0)
        def _():
            dxv = dx_ref[...]
            dmb = (dxv * g_ref[...]).astype(BF)
            ms[...] = dmb
            dm_ref[...] = dmb
            dg_ref[...] += jnp.sum(dxv * y_ref[...].astype(F32), axis=0, keepdims=True)

        da = _dg(ms[...], w_ref[...], NT)
        if fused:
            da_ref[...] = (da * (2.0 * act_ref[...].astype(F32))).astype(BF)
        else:
            da_ref[...] = da

    row = pl.BlockSpec((tm, D), lambda i, j: (i, 0))
    vec = pl.BlockSpec((1, D), lambda i, j: (0, 0))
    tile = pl.BlockSpec((tm, tn), lambda i, j: (i, j))
    in_specs = [row, vec, row, pl.BlockSpec((tn, D), lambda i, j: (j, 0))]
    args = [dx, gate, y, w]
    if fused:
        in_specs.append(tile)
        args.append(act)
    return pl.pallas_call(
        body, name=name, grid=(S // tm, K // tn),
        in_specs=in_specs, out_specs=[tile, row, vec],
        out_shape=[jax.ShapeDtypeStruct((S, K), BF if fused else F32), jax.ShapeDtypeStruct((S, D), BF),
                   jax.ShapeDtypeStruct((1, D), F32)],
        scratch_shapes=[pltpu.VMEM((tm, D), BF)],
        compiler_params=_cp("arbitrary", "arbitrary"),
    )(*args)


def _matmul_tn(a, b, *, name):
    S, Ka = a.shape
    P, _, Db = b.shape
    tk, tn, ts = _pick(Ka, 1024), _pick(Db, 1024), _pick(S, 1024, 16)
    npb = Db // tn

    def body(a_ref, b_ref, o_ref, acc):
        s = pl.program_id(2)

        @pl.when(s == 0)
        def _():
            acc[...] = jnp.zeros_like(acc)

        acc[...] += _dg(a_ref[...], b_ref[...], TN)

        @pl.when(s == pl.num_programs(2) - 1)
        def _():
            o_ref[...] = acc[...]

    return pl.pallas_call(
        body, name=name, grid=(Ka // tk, P * npb, S // ts),
        in_specs=[pl.BlockSpec((ts, tk), lambda i, j, s: (s, i)),
                  pl.BlockSpec((None, ts, tn), lambda i, j, s: (j // npb, s, j % npb))],
        out_specs=pl.BlockSpec((tk, tn), lambda i, j, s: (i, j)),
        out_shape=jax.ShapeDtypeStruct((Ka, P * Db), F32),
        scratch_shapes=[pltpu.VMEM((tk, tn), F32)],
        compiler_params=_cp("parallel", "parallel", "arbitrary"),
    )(a, b)


def _matmul_nt_lnbwd(g, w, x, nw, sc, dx_out, *, name):
    P, S, Dg = g.shape
    D = x.shape[1]
    tm, tk = _pick(S, 1024, 16), _pick(Dg, 1024)
    npb = Dg // tk
    nk = P * npb

    def body(g_ref, w_ref, x_ref, nw_ref, sc_ref, dxo_ref, dx_ref, dsc_ref, dsh_ref, dnw_ref, acc):
        i, k = pl.program_id(0), pl.program_id(1)

        @pl.when((i == 0) & (k == 0))
        def _():
            dsc_ref[...] = jnp.zeros_like(dsc_ref)
            dsh_ref[...] = jnp.zeros_like(dsh_ref)
            dnw_ref[...] = jnp.zeros_like(dnw_ref)

        @pl.when(k == 0)
        def _():
            acc[...] = jnp.zeros_like(acc)

        acc[...] += _dg(g_ref[...], w_ref[...], NT)

        @pl.when(k == nk - 1)
        def _():
            dh = acc[...]
            xv = x_ref[...]
            nwv = nw_ref[...]
            r = lax.rsqrt(jnp.mean(xv * xv, axis=-1, keepdims=True) + EPS)
            xr = xv * r
            dn = dh * (1.0 + sc_ref[...])
            dsc_ref[...] += jnp.sum(dh * (xr * nwv), axis=0, keepdims=True)
            dsh_ref[...] += jnp.sum(dh, axis=0, keepdims=True)
            dnw_ref[...] += jnp.sum(dn * xr, axis=0, keepdims=True)
            u = dn * nwv
            dx_ref[...] = dxo_ref[...] + r * (u - xr * jnp.mean(u * xr, axis=-1, keepdims=True))

    row = pl.BlockSpec((tm, D), lambda i, k: (i, 0))
    vec = pl.BlockSpec((1, D), lambda i, k: (0, 0))
    return pl.pallas_call(
        body, name=name, grid=(S // tm, nk),
        in_specs=[pl.BlockSpec((None, tm, tk), lambda i, k: (k // npb, i, k % npb)),
                  pl.BlockSpec((D, tk), lambda i, k: (0, k)), row, vec, vec, row],
        out_specs=[row, vec, vec, vec],
        out_shape=[jax.ShapeDtypeStruct((S, D), F32)] + [jax.ShapeDtypeStruct((1, D), F32)] * 3,
        scratch_shapes=[pltpu.VMEM((tm, D), F32)],
        compiler_params=_cp("arbitrary", "arbitrary"),
    )(g, w, x, nw, sc, dx_out)


def _loss_kernel(x, fw, tgt, *, name):
    S, D = x.shape
    tm = _pick(S, 512, 8)

    def body(x_ref, fw_ref, t_ref, l_ref, dx_ref, dfw_ref):
        @pl.when(pl.program_id(0) == 0)
        def _():
            l_ref[...] = jnp.zeros_like(l_ref)
            dfw_ref[...] = jnp.zeros_like(dfw_ref)

        xv = x_ref[...]
        fwv = fw_ref[...]
        r = lax.rsqrt(jnp.mean(xv * xv, axis=-1, keepdims=True) + EPS)
        xr = xv * r
        err = xr * fwv - t_ref[...]
        per_tok = jnp.mean(err * err, axis=-1, keepdims=True)
        l_ref[...] += 0.5 * jnp.sum(per_tok, axis=0, keepdims=True)
        dy = err * (1.0 / D)
        dfw_ref[...] += jnp.sum(dy * xr, axis=0, keepdims=True)
        u = dy * fwv
        dx_ref[...] = r * (u - xr * jnp.mean(u * xr, axis=-1, keepdims=True))

    row = pl.BlockSpec((tm, D), lambda i: (i, 0))
    vec = pl.BlockSpec((1, D), lambda i: (0, 0))
    return pl.pallas_call(
        body, name=name, grid=(S // tm,),
        in_specs=[row, vec, row],
        out_specs=[pl.BlockSpec((1, LANES), lambda i: (0, 0)), row, vec],
        out_shape=[jax.ShapeDtypeStruct((1, LANES), F32), jax.ShapeDtypeStruct((S, D), F32),
                   jax.ShapeDtypeStruct((1, D), F32)],
        compiler_params=_cp("arbitrary"),
    )(x, fw, tgt)


def _hg_lower_bound(lb3):
    mx = jnp.max(lb3, axis=0, keepdims=True)
    e = jnp.exp(lb3 - mx)
    p = e / jnp.sum(e, axis=0, keepdims=True)
    return p[0:1, :], p


def _hg_chunk_common(qr, fz, lbv):
    sq = _sigmoid(qr)
    q = qr * sq
    sig = _sigmoid(fz)
    f = lbv + (1.0 - lbv) * sig
    k = (1.0 - lbv) * (1.0 - sig)
    return q, sq, sig, f, k, jnp.log(f)


def _row_of(x, rows, r):
    return jnp.sum(jnp.where(rows == r, x, 0.0), axis=0, keepdims=True)


def _hg_fwd(proj, hg_lb, gn, slab=None, *, name):
    S = proj.shape[0]
    D = proj.shape[1] // 4
    H = D // LANES
    HB = min(HG_HEADS_PER_STEP, H)
    W = HB * LANES
    C = HG_CHUNK
    T = _pick(S, HG_TOKENS_PER_STEP, C)
    nch, nb = T // C, S // T
    ng = H // HB
    fused = slab is not None

    def body(q_ref, fz_ref, v_ref, g_ref, lb_ref, gn_ref, *rest):
        if fused:
            s_ref, y_ref, o_ref, sts_ref, out_ref, st, send_sems, recv_sems = rest
            first, passed, landed, from_sibling = _chip_slab_copies(s_ref, out_ref, send_sems, recv_sems)
            hgrp, n = pl.program_id(0), pl.program_id(1)

            @pl.when((hgrp == 0) & (n == 0))
            def _():
                for cp in first:
                    cp.start()

            @pl.when((hgrp == ng - 1) & (n == (3 * nb) // 4))
            def _():
                for arrived, onward in zip(landed, passed):
                    arrived.wait_recv()
                    onward.start()
        else:
            y_ref, o_ref, sts_ref, st = rest

        @pl.when(pl.program_id(1) == 0)
        def _():
            st[...] = jnp.zeros_like(st)

        lb_all, _ = _hg_lower_bound(lb_ref[...])
        gnv = gn_ref[...]
        ri = lax.broadcasted_iota(jnp.int32, (C, C), 0)
        ci_ = lax.broadcasted_iota(jnp.int32, (C, C), 1)
        low = ri >= ci_
        tri = jnp.where(low, 1.0, 0.0).astype(BF)
        rows = lax.broadcasted_iota(jnp.int32, (C, LANES), 0)

        def chunk(ci, carry):
            sl = pl.ds(pl.multiple_of(ci * C, C), C)
            for hh in range(HB):
                ls = slice(hh * LANES, (hh + 1) * LANES)
                q, _, _, _, k, logf = _hg_chunk_common(q_ref[sl, ls], fz_ref[sl, ls], lb_all[:, ls])
                vv = v_ref[sl, ls]
                gg = g_ref[sl, ls]
                G = _tri_dot(tri, logf)
                Gm = _row_of(G, rows, C // 2 - 1)
                Gl = _row_of(G, rows, C - 1)
                qt = q * jnp.exp(G - Gm)
                kt = k * jnp.exp(Gm - G)
                A = jnp.where(low, _dg1(qt, kt, NT), 0.0)
                Sv = st[hh]
                sts_ref[hh, ci] = Sv
                o = _dg1(A, vv, NN) + _dg1(q * jnp.exp(G), Sv, NT)
                st[hh] = Sv * jnp.exp(Gl) + _dg1(vv, k * jnp.exp(Gl - G), TN)
                r = lax.rsqrt(jnp.mean(o * o, axis=-1, keepdims=True) + EPS)
                y_ref[sl, ls] = ((o * r * gnv) * (gg * _sigmoid(gg))).astype(BF)
                o_ref[sl, ls] = o
            return carry

        lax.fori_loop(0, nch, chunk, 0)

        if fused:
            @pl.when((hgrp == ng - 1) & (n == nb - 1))
            def _():
                for cp in from_sibling:
                    cp.wait_recv()
                for cp in first + passed:
                    cp.wait_send()

    def part(p):
        return pl.BlockSpec((T, W), lambda h, n: (n, p * ng + h))

    blk = pl.BlockSpec((T, W), lambda h, n: (n, h))
    in_specs = [part(0), part(1), part(2), part(3),
                pl.BlockSpec((3, W), lambda h, n: (0, h)), pl.BlockSpec((1, LANES), lambda h, n: (0, 0))]
    out_specs = [blk, blk, pl.BlockSpec((HB, nch, LANES, LANES), lambda h, n: (h, n, 0, 0))]
    out_shape = [jax.ShapeDtypeStruct((S, D), BF), jax.ShapeDtypeStruct((S, D), F32),
                 jax.ShapeDtypeStruct((H, S // C, LANES, LANES), F32)]
    scratch = [pltpu.VMEM((HB, LANES, LANES), F32)]
    args = [proj, proj, proj, proj, hg_lb, gn]
    if fused:
        in_specs.append(HBM)
        out_specs.append(HBM)
        out_shape.append(jax.ShapeDtypeStruct((N_CHIPS,) + slab.shape, slab.dtype))
        scratch += [pltpu.SemaphoreType.DMA((6,)), pltpu.SemaphoreType.DMA((6,))]
        args.append(slab)
    return pl.pallas_call(
        body, name=name, grid=(ng, nb), in_specs=in_specs, out_specs=out_specs, out_shape=out_shape,
        scratch_shapes=scratch, compiler_params=_cp("arbitrary", "arbitrary"),
    )(*args)


def _hg_bwd(proj, hg_lb, gn, o_all, states, dy, *, name):
    S = proj.shape[0]
    D = proj.shape[1] // 4
    H = D // LANES
    HB = min(HG_HEADS_PER_STEP, H)
    W = HB * LANES
    C = HG_CHUNK
    T = _pick(S, HG_TOKENS_PER_STEP, C)
    nch, nb = T // C, S // T

    def body(q_ref, fz_ref, v_ref, g_ref, lb_ref, gn_ref, o_ref, sts_ref, dy_ref,
             dp_ref, dlb_ref, dgn_ref, dst, dlb_acc):
        n = pl.program_id(1)

        @pl.when(n == 0)
        def _():
            dst[...] = jnp.zeros_like(dst)
            dlb_acc[...] = jnp.zeros_like(dlb_acc)
            dgn_ref[...] = jnp.zeros_like(dgn_ref)

        lb_all, p3 = _hg_lower_bound(lb_ref[...])
        gnv = gn_ref[...]
        ri = lax.broadcasted_iota(jnp.int32, (C, C), 0)
        ci_ = lax.broadcasted_iota(jnp.int32, (C, C), 1)
        low = ri >= ci_
        tri = jnp.where(low, 1.0, 0.0).astype(BF)
        triu = jnp.where(ri <= ci_, 1.0, 0.0).astype(BF)
        rows = lax.broadcasted_iota(jnp.int32, (C, LANES), 0)

        def chunk(cj, carry):
            ci = nch - 1 - cj
            sl = pl.ds(pl.multiple_of(ci * C, C), C)
            for hh in range(HB):
                ls = slice(hh * LANES, (hh + 1) * LANES)
                lbv = lb_all[:, ls]
                qr = q_ref[sl, ls]
                q, sq, sig, f, k, logf = _hg_chunk_common(qr, fz_ref[sl, ls], lbv)
                vv = v_ref[sl, ls]
                gg = g_ref[sl, ls]
                o = o_ref[sl, ls]
                dyv = dy_ref[sl, ls]
                G = _tri_dot(tri, logf)
                Gm = _row_of(G, rows, C // 2 - 1)
                Gl = _row_of(G, rows, C - 1)
                eG, e_qm, e_km, e_lk, eGl = jnp.exp(G), jnp.exp(G - Gm), jnp.exp(Gm - G), jnp.exp(Gl - G), jnp.exp(Gl)
                qt = q * e_qm
                kt = k * e_km
                A = jnp.where(low, _dg1(qt, kt, NT), 0.0)
                sg = _sigmoid(gg)
                r = lax.rsqrt(jnp.mean(o * o, axis=-1, keepdims=True) + EPS)
                on = o * r
                d_onw = dyv * (gg * sg)
                dgn_ref[hh] += jnp.sum(d_onw * on, axis=0, keepdims=True)
                dgg = dyv * (on * gnv) * (sg * (1.0 + gg * (1.0 - sg)))
                u = d_onw * gnv
                do = r * (u - on * jnp.mean(u * on, axis=-1, keepdims=True))
                Sv = sts_ref[hh, ci]
                dSv = dst[hh]
                dA = jnp.where(low, _dg3(do, vv, NT), 0.0)
                kdec = k * e_lk
                dv = _dg1(A, do, TN) + _dg1(kdec, dSv, NT)
                dq = _dg3(dA, kt, NN) * e_qm + eG * _dg3(do, Sv, NN)
                dk = _dg3(dA, qt, TN) * e_km + e_lk * _dg3(vv, dSv, NN)
                s_end = Sv * eGl + _dg3(vv, kdec, TN)
                dgl = jnp.sum(dSv * s_end, axis=0, keepdims=True)
                dG = q * dq - k * dk + jnp.where(rows == C - 1, dgl, 0.0)
                dlogf = _tri_dot(triu, dG) - f * dk
                dst[hh] = dSv * eGl + _dg1(do, q * eG, TN)
                dlf_f = dlogf / f
                dlb_acc[:, ls] += jnp.sum(dlf_f * (1.0 - sig), axis=0, keepdims=True)
                dp_ref[0, sl, ls] = (dq * (sq * (1.0 + qr * (1.0 - sq)))).astype(BF)
                dp_ref[1, sl, ls] = (dlf_f * (1.0 - lbv) * sig * (1.0 - sig)).astype(BF)
                dp_ref[2, sl, ls] = dv.astype(BF)
                dp_ref[3, sl, ls] = dgg.astype(BF)
            return carry

        lax.fori_loop(0, nch, chunk, 0)
        sel = jnp.where(lax.broadcasted_iota(jnp.int32, (3, W), 0) == 0, 1.0, 0.0)
        dlb_ref[...] = lb_all * (sel - p3) * dlb_acc[...]

    ng = H // HB

    def part(p):
        return pl.BlockSpec((T, W), lambda h, n: (nb - 1 - n, p * ng + h))

    blk = pl.BlockSpec((T, W), lambda h, n: (nb - 1 - n, h))
    return pl.pallas_call(
        body, name=name, grid=(ng, nb),
        in_specs=[part(0), part(1), part(2), part(3),
                  pl.BlockSpec((3, W), lambda h, n: (0, h)), pl.BlockSpec((1, LANES), lambda h, n: (0, 0)),
                  blk, pl.BlockSpec((HB, nch, LANES, LANES), lambda h, n: (h, nb - 1 - n, 0, 0)), blk],
        out_specs=[pl.BlockSpec((4, T, W), lambda h, n: (0, nb - 1 - n, h)),
                   pl.BlockSpec((3, W), lambda h, n: (0, h)),
                   pl.BlockSpec((HB, 1, LANES), lambda h, n: (h, 0, 0))],
        out_shape=[jax.ShapeDtypeStruct((4, S, D), BF), jax.ShapeDtypeStruct((3, D), F32),
                   jax.ShapeDtypeStruct((H, 1, LANES), F32)],
        scratch_shapes=[pltpu.VMEM((HB, LANES, LANES), F32), pltpu.VMEM((1, W), F32)],
        compiler_params=_cp("parallel", "arbitrary"),
    )(proj, proj, proj, proj, hg_lb, gn, o_all, states, dy)


def _log_sigmoid(u):
    return jnp.minimum(u, 0.0) - jnp.log(1.0 + jnp.exp(-jnp.abs(u)))


def _lane_put(base, lane, first, pieces):
    for n, p in enumerate(pieces):
        base = jnp.where(lane == first + n, p, base)
    return base


def _fox_cumsum(proj, bf_pad, *, name):
    S = proj.shape[0]
    D = proj.shape[1] // 5
    T = _pick(S, 256, 8)

    def body(fz_ref, b_ref, f_ref, carry):
        @pl.when(pl.program_id(0) == 0)
        def _():
            carry[...] = jnp.zeros_like(carry)

        logf = _log_sigmoid(fz_ref[...] + b_ref[...])
        tri = jnp.where(lax.broadcasted_iota(jnp.int32, (T, T), 0) >= lax.broadcasted_iota(jnp.int32, (T, T), 1),
                        1.0, 0.0).astype(BF)
        fv = _tri_dot(tri, logf) + carry[...]
        f_ref[...] = fv
        carry[...] = _row_of(fv, lax.broadcasted_iota(jnp.int32, (T, LANES), 0), T - 1)

    return pl.pallas_call(
        body, name=name, grid=(S // T,),
        in_specs=[pl.BlockSpec((T, LANES), lambda i: (i, 4 * D // LANES)), pl.BlockSpec((1, LANES), lambda i: (0, 0))],
        out_specs=pl.BlockSpec((T, LANES), lambda i: (i, 0)),
        out_shape=jax.ShapeDtypeStruct((S, LANES), F32),
        scratch_shapes=[pltpu.VMEM((1, LANES), F32)],
        compiler_params=_cp("arbitrary"),
    )(proj, bf_pad)


def _pair_stats(sq, lo):
    del lo
    a = lax.broadcasted_iota(jnp.int32, (LANES, LANES), 0) < FOX_DH
    b = lax.broadcasted_iota(jnp.int32, (LANES, LANES), 1) < FOX_DH
    avg = jnp.where(a == b, 1.0 / FOX_DH, 0.0).astype(BF)
    hi, mid, low = _split3(sq)
    return _dot(hi, avg) + _dot(mid, avg) + _dot(low, avg)


def _fox_prep(proj, fcum, qw2, kw2, *, name):
    S = proj.shape[0]
    D = proj.shape[1] // 5
    HP = D // LANES
    T = _pick(S, 512, 16)

    def body(q_ref, k_ref, v_ref, f_ref, qw_ref, kw_ref, qa_ref, ka_ref, va_ref, vt_ref):
        hp = pl.program_id(1)
        lane = lax.broadcasted_iota(jnp.int32, (T, LANES), 1)
        lo = lane < FOX_DH
        qv, kv, vv, fv = q_ref[...], k_ref[...], v_ref[...], f_ref[...]
        qn = qv * lax.rsqrt(_pair_stats(qv * qv, lo) + EPS) * qw_ref[...] * (0.125 * LOG2E)
        kn = kv * lax.rsqrt(_pair_stats(kv * kv, lo) + EPS) * kw_ref[...]
        ones_q = jnp.where((lane >= 67) & (lane <= 69), 1.0, 0.0)
        ones_k = jnp.where(((lane >= 64) & (lane <= 66)) | ((lane >= 70) & (lane <= 72)), 1.0, 0.0)
        ones_v = jnp.where((lane >= 64) & (lane <= 66), 1.0, 0.0)
        for hh in range(2):
            fh = jnp.sum(jnp.where(lane == 2 * hp + hh, fv, 0.0), axis=-1, keepdims=True) * LOG2E
            pieces = [p.astype(F32) for p in _split3(fh)]

            def half(x):
                return jnp.where(lo, x if hh == 0 else pltpu.roll(x, FOX_DH, 1), 0.0)

            qa_ref[hh] = _lane_put(half(qn) + ones_q, lane, 64, pieces).astype(BF)
            ka_ref[hh] = _lane_put(half(kn) + ones_k, lane, 67, [-p for p in pieces]).astype(BF)
            va = half(vv) + ones_v
            va_ref[hh] = va.astype(BF)
            vt_ref[hh] = va.T.astype(BF)

    def part(p):
        return pl.BlockSpec((T, LANES), lambda i, hp: (i, p * HP + hp))

    vec = pl.BlockSpec((1, LANES), lambda i, hp: (0, 0))
    aug = pl.BlockSpec((2, T, LANES), lambda i, hp: (hp, i, 0))
    return pl.pallas_call(
        body, name=name, grid=(S // T, HP),
        in_specs=[part(0), part(1), part(2), pl.BlockSpec((T, LANES), lambda i, hp: (i, 0)), vec, vec],
        out_specs=[aug, aug, aug, pl.BlockSpec((2, LANES, T), lambda i, hp: (hp, 0, i))],
        out_shape=[jax.ShapeDtypeStruct((2 * HP, S, LANES), BF)] * 3 + [jax.ShapeDtypeStruct((2 * HP, LANES, S), BF)],
        compiler_params=_cp("parallel", "arbitrary"),
    )(proj, proj, proj, fcum, qw2, kw2)


def _fox_block(S):
    return _pick(S, 256, 16)


def _fox_skip_bounds(fcum, qn_w, kn_w, nheads):
    S = fcum.shape[0]
    B = _fox_block(S)
    qk = 8.0 * LOG2E * 1.02 * jnp.max(jnp.abs(qn_w)) * jnp.max(jnp.abs(kn_w))
    thresh = -(2.0 * qk + 160.0)
    f2 = fcum[:, :nheads] * LOG2E
    first, last = f2[0::B], f2[B - 1::B]
    nb = S // B
    blk = jnp.arange(nb)
    dead = (first[0::2, None, :] - last[None, :, :]) < thresh
    jmin = jnp.sum(dead & (blk[None, :, None] < 2 * jnp.arange(nb // 2)[:, None, None]), axis=1)
    live = (first[:, None, :] - last[None, :, :]) >= thresh
    imax = blk[:, None] + jnp.sum(live & (blk[:, None, None] > blk[None, :, None]), axis=0)
    return jmin.T.astype(jnp.int32), imax.T.astype(jnp.int32)


def _fox_fwd(jmin, qa, ka, vat, proj, *, name):
    H, S, _ = qa.shape
    HP = H // 2
    D = HP * LANES
    B = _fox_block(S)
    BQ = 2 * B
    nq = S // BQ

    def body(jmin_ref, q_ref, k_ref, vt_ref, g_ref, y_ref, o_ref, q2_ref):
        hp, i = pl.program_id(0), pl.program_id(1)
        lane = lax.broadcasted_iota(jnp.int32, (BQ, LANES), 1)
        lo = lane < FOX_DH
        in_lse = (lane >= 70) & (lane <= 72)
        causal = lax.broadcasted_iota(jnp.int32, (BQ, BQ), 0) <= lax.broadcasted_iota(jnp.int32, (BQ, BQ), 1)
        row = lax.broadcasted_iota(jnp.int32, (LANES, BQ), 0)
        m0, acc0 = jnp.full((1, BQ), -jnp.inf, F32), jnp.zeros((LANES, BQ), F32)
        outs = []
        for hh in range(2):
            qb = q_ref[hh]

            def block(j, carry, masked=False):
                m, acc = carry
                sl = pl.ds(pl.multiple_of(j * BQ, BQ), BQ)
                st = _dg(k_ref[hh, sl, :], qb, NT)
                if masked:
                    st = jnp.where(causal, st, -jnp.inf)
                m_new = jnp.maximum(m, jnp.max(st, axis=0, keepdims=True))
                p = jnp.exp2(st - m_new)
                ph = p.astype(BF)
                pl_ = (p - ph.astype(F32)).astype(BF)
                vt = vt_ref[hh, :, sl]
                pv = _dot(jnp.concatenate([vt, vt], axis=1), jnp.concatenate([ph, pl_], axis=0))
                return m_new, acc * jnp.exp2(m - m_new) + pv

            carry = lax.fori_loop(jmin_ref[2 * hp + hh, i] // 2, i, block, (m0, acc0))
            m, acc = block(i, carry, masked=True)
            l = jnp.sum(jnp.where(row == FOX_DH, acc, 0.0), axis=0, keepdims=True)
            tile = acc / l
            for n, piece in enumerate(_split3(m + jnp.log2(l))):
                tile = jnp.where(row == 70 + n, -(piece.astype(F32)), tile)
            tile = tile.T
            outs.append(tile)
            q2_ref[hh] = jnp.where(in_lse, tile, qb.astype(F32)).astype(BF)
        o = jnp.where(lo, outs[0], pltpu.roll(outs[1], FOX_DH, 1))
        o_ref[...] = o
        y_ref[...] = (o * _sigmoid(g_ref[...])).astype(BF)

    blk = pl.BlockSpec((BQ, LANES), lambda hp, i, jm: (i, hp))
    qblk = pl.BlockSpec((2, BQ, LANES), lambda hp, i, jm: (hp, i, 0))
    full = pl.BlockSpec((2, S, LANES), lambda hp, i, jm: (hp, 0, 0))
    full_t = pl.BlockSpec((2, LANES, S), lambda hp, i, jm: (hp, 0, 0))
    return pl.pallas_call(
        body, name=name,
        grid_spec=pltpu.PrefetchScalarGridSpec(
            num_scalar_prefetch=1, grid=(HP, nq),
            in_specs=[qblk, full, full_t, pl.BlockSpec((BQ, LANES), lambda hp, i, jm: (i, 3 * HP + hp))],
            out_specs=[blk, blk, qblk]),
        out_shape=[jax.ShapeDtypeStruct((S, D), BF), jax.ShapeDtypeStruct((S, D), F32),
                   jax.ShapeDtypeStruct((H, S, LANES), BF)],
        compiler_params=_cp("parallel", "arbitrary"),
    )(jmin, qa, ka, vat, proj)


def _fox_bwd_prep(dy, o, proj, *, name):
    S, D = dy.shape
    HP = D // LANES
    T = _pick(S, 512, 16)

    def body(dy_ref, o_ref, g_ref, da_ref):
        lane = lax.broadcasted_iota(jnp.int32, (T, LANES), 1)
        lo = lane < FOX_DH
        do = (dy_ref[...] * _sigmoid(g_ref[...])).astype(BF).astype(F32)
        prod = do * o_ref[...]
        d_lo = jnp.sum(jnp.where(lo, prod, 0.0), axis=-1, keepdims=True)
        d_hi = jnp.sum(jnp.where(lo, 0.0, prod), axis=-1, keepdims=True)
        for hh, delta in enumerate((d_lo, d_hi)):
            base = jnp.where(lo, do if hh == 0 else pltpu.roll(do, FOX_DH, 1), 0.0)
            da_ref[hh] = _lane_put(base, lane, 64, [-(p.astype(F32)) for p in _split3(delta)]).astype(BF)

    blk = pl.BlockSpec((T, LANES), lambda i, hp: (i, hp))
    return pl.pallas_call(
        body, name=name, grid=(S // T, HP),
        in_specs=[blk, blk, pl.BlockSpec((T, LANES), lambda i, hp: (i, 3 * HP + hp))],
        out_specs=pl.BlockSpec((2, T, LANES), lambda i, hp: (hp, i, 0)),
        out_shape=jax.ShapeDtypeStruct((2 * HP, S, LANES), BF),
        compiler_params=_cp("parallel", "arbitrary"),
    )(dy, o, proj)


def _fox_bwd(imax, q2, ka, va, doa, *, name):
    H, S, _ = q2.shape
    B = _fox_block(S)
    nb = S // B

    def body(imax_ref, q_ref, do_ref, k_ref, v_ref, dq_ref, dk_ref, dv_ref, cs_ref):
        j = pl.program_id(1)
        end = imax_ref[pl.program_id(0), j] + 1

        @pl.when(j == 0)
        def _():
            dq_ref[...] = jnp.zeros_like(dq_ref)

        kb, vb = k_ref[...], v_ref[...]
        causal = lax.broadcasted_iota(jnp.int32, (B, B), 1) <= lax.broadcasted_iota(jnp.int32, (B, B), 0)

        def step(i, carry, masked=False, nblk=1):
            dk_acc, dv_acc, cs_acc = carry
            rows = nblk * B
            sl = pl.ds(pl.multiple_of(i * B, B), rows)
            qb, dob = q_ref[sl, :], do_ref[sl, :]
            s = _dg(qb, kb, NT)
            if masked:
                s = jnp.where(causal, s, -jnp.inf)
            p = jnp.exp2(s)
            ds = p * _dg(dob, vb, NT)
            dsb = ds.astype(BF)
            cs_acc = cs_acc + jnp.sum(ds.reshape(rows // 8, 8, B), axis=0)
            dv_acc = dv_acc + _dg(p.astype(BF), dob, TN)
            dk_acc = dk_acc + _dg(dsb, qb, TN)
            dq_ref[sl, :] += _dot(dsb, kb)
            return dk_acc, dv_acc, cs_acc

        zero = jnp.zeros((B, LANES), F32)
        carry = step(j, (zero, zero, jnp.zeros((8, B), F32)), masked=True)
        pos = j + 1
        for U in FOX_BWD_TILES:
            n = (end - pos) // U
            carry = lax.fori_loop(0, n, lambda ii, c, pos=pos, U=U: step(pos + U * ii, c, nblk=U), carry)
            pos = pos + U * n
        dk_acc, dv_acc, cs_acc = carry
        dk_ref[...] = dk_acc
        dv_ref[...] = dv_acc
        cs_ref[...] = jnp.sum(cs_acc, axis=0, keepdims=True)

    full = pl.BlockSpec((None, S, LANES), lambda h, j, im: (h, 0, 0))
    blk = pl.BlockSpec((None, B, LANES), lambda h, j, im: (h, j, 0))
    return pl.pallas_call(
        body, name=name,
        grid_spec=pltpu.PrefetchScalarGridSpec(
            num_scalar_prefetch=1, grid=(H, nb),
            in_specs=[full, full, blk, blk],
            out_specs=[full, blk, blk, pl.BlockSpec((None, 1, B), lambda h, j, im: (h, 0, j))]),
        out_shape=[jax.ShapeDtypeStruct((H, S, LANES), F32)] * 3 + [jax.ShapeDtypeStruct((H, 1, S), F32)],
        compiler_params=_cp("parallel", "arbitrary"),
    )(imax, q2, doa, ka, va)


def _fox_bwd_post(dqa, dka, dva, proj, dy, o, qw2, kw2, *, name):
    S, D = dy.shape
    HP = D // LANES
    T = _pick(S, 512, 16)

    def body(dq_ref, dk_ref, dv_ref, q_ref, k_ref, g_ref, dy_ref, o_ref, qw_ref, kw_ref, dp_ref, dqw_ref, dkw_ref):
        @pl.when((pl.program_id(0) == 0) & (pl.program_id(1) == 0))
        def _():
            dqw_ref[...] = jnp.zeros_like(dqw_ref)
            dkw_ref[...] = jnp.zeros_like(dkw_ref)

        lane = lax.broadcasted_iota(jnp.int32, (T, LANES), 1)
        lo = lane < FOX_DH

        def pair(ref):
            return jnp.where(lo, ref[0], pltpu.roll(ref[1], FOX_DH, 1))

        def norm_bwd(xv, w, dyn, dw_ref):
            r = lax.rsqrt(_pair_stats(xv * xv, lo) + EPS)
            xr = xv * r
            dw_ref[...] += jnp.sum(dyn * xr, axis=0, keepdims=True)
            u = dyn * w
            return r * (u - xr * _pair_stats(u * xr, lo))

        dp_ref[0] = norm_bwd(q_ref[...], qw_ref[...], pair(dq_ref) * 0.125, dqw_ref).astype(BF)
        dp_ref[1] = norm_bwd(k_ref[...], kw_ref[...], pair(dk_ref) * (1.0 / LOG2E), dkw_ref).astype(BF)
        dp_ref[2] = pair(dv_ref).astype(BF)
        sg = _sigmoid(g_ref[...])
        dp_ref[3] = (dy_ref[...] * o_ref[...] * sg * (1.0 - sg)).astype(BF)

    def part(p):
        return pl.BlockSpec((T, LANES), lambda i, hp: (i, p * HP + hp))

    aug = pl.BlockSpec((2, T, LANES), lambda i, hp: (hp, i, 0))
    blk = pl.BlockSpec((T, LANES), lambda i, hp: (i, hp))
    vec = pl.BlockSpec((1, LANES), lambda i, hp: (0, 0))
    return pl.pallas_call(
        body, name=name, grid=(S // T, HP),
        in_specs=[aug, aug, aug, part(0), part(1), part(3), blk, blk, vec, vec],
        out_specs=[pl.BlockSpec((4, T, LANES), lambda i, hp: (0, i, hp)), vec, vec],
        out_shape=[jax.ShapeDtypeStruct((5, S, D), BF), jax.ShapeDtypeStruct((1, LANES), F32),
                   jax.ShapeDtypeStruct((1, LANES), F32)],
        compiler_params=_cp("arbitrary", "arbitrary"),
    )(dqa, dka, dva, proj, proj, proj, dy, o, qw2, kw2)


def _fox_dfz(colsum, nheads, proj, bf_pad, dproj, *, name):
    S = colsum.shape[0]
    H = nheads
    D = dproj.shape[2]
    T = _pick(S, 256, 16)
    nb = S // T

    def body(cs_ref, fz_ref, b_ref, _, dp_ref, db_ref, carry):
        @pl.when(pl.program_id(0) == 0)
        def _():
            carry[...] = jnp.zeros_like(carry)
            db_ref[...] = jnp.zeros_like(db_ref)

        lane = lax.broadcasted_iota(jnp.int32, (T, LANES), 1)
        df = -cs_ref[...]
        triu = jnp.where(lax.broadcasted_iota(jnp.int32, (T, T), 0) <= lax.broadcasted_iota(jnp.int32, (T, T), 1),
                         1.0, 0.0).astype(BF)
        dlogf = _tri_dot(triu, df) + carry[...]
        carry[...] = _row_of(dlogf, lax.broadcasted_iota(jnp.int32, (T, LANES), 0), 0)
        dfz = jnp.where(lane < H, dlogf * _sigmoid(-(fz_ref[...] + b_ref[...])), 0.0)
        db_ref[...] += jnp.sum(dfz, axis=0, keepdims=True)
        dp_ref[...] = jnp.zeros_like(dp_ref)
        dp_ref[:, 0:LANES] = dfz.astype(BF)

    return pl.pallas_call(
        body, name=name, grid=(nb,),
        in_specs=[pl.BlockSpec((T, LANES), lambda i: (nb - 1 - i, 0)),
                  pl.BlockSpec((T, LANES), lambda i: (nb - 1 - i, 4 * D // LANES)),
                  pl.BlockSpec((1, LANES), lambda i: (0, 0)),
                  pl.BlockSpec(memory_space=pl.ANY)],
        out_specs=[pl.BlockSpec((None, T, D), lambda i: (4, nb - 1 - i, 0)), pl.BlockSpec((1, LANES), lambda i: (0, 0))],
        out_shape=[jax.ShapeDtypeStruct(dproj.shape, BF), jax.ShapeDtypeStruct((1, LANES), F32)],
        scratch_shapes=[pltpu.VMEM((1, LANES), F32)],
        input_output_aliases={3: 0},
        compiler_params=_cp("arbitrary"),
    )(colsum, proj, bf_pad, dproj)


def _mod_fwd(c16, w, b, *, name):
    L, D, N = w.shape
    tn = _pick(N, 512)

    def body(c_ref, w_ref, b_ref, o_ref):
        cv = c_ref[...]
        ca = (cv * _sigmoid(cv)).astype(BF)
        o_ref[...] = _dot(ca, w_ref[...].astype(BF)) + b_ref[...]

    return pl.pallas_call(
        body, name=name, grid=(L, N // tn),
        in_specs=[pl.BlockSpec((16, D), lambda l, j: (0, 0)), pl.BlockSpec((None, D, tn), lambda l, j: (l, 0, j)),
                  pl.BlockSpec((None, 1, tn), lambda l, j: (l, 0, j))],
        out_specs=pl.BlockSpec((None, 16, tn), lambda l, j: (l, 0, j)),
        out_shape=jax.ShapeDtypeStruct((L, 16, N), F32),
        compiler_params=_cp("parallel", "arbitrary"),
    )(c16, w, b)


def _mod_bwd(c16, dmod, *, name):
    L, _, N = dmod.shape
    D = c16.shape[1]
    tn = _pick(N, 512)

    def body(c_ref, d_ref, o_ref):
        cv = c_ref[...]
        ca = (cv * _sigmoid(cv)).astype(BF)
        o_ref[...] = _dg(ca, d_ref[...].astype(BF), TN)

    return pl.pallas_call(
        body, name=name, grid=(L, N // tn),
        in_specs=[pl.BlockSpec((16, D), lambda l, j: (0, 0)), pl.BlockSpec((None, 16, tn), lambda l, j: (l, 0, j))],
        out_specs=pl.BlockSpec((None, D, tn), lambda l, j: (l, 0, j)),
        out_shape=jax.ShapeDtypeStruct((L, D, N), F32),
        compiler_params=_cp("parallel", "arbitrary"),
    )(c16, dmod)


def _adamw_math(w, g, m, v):
    m = ADAM_B1 * m + (1.0 - ADAM_B1) * g
    v = ADAM_B2 * v + (1.0 - ADAM_B2) * (g * g)
    m_hat = m / (1.0 - ADAM_B1 ** ADAM_STEP)
    v_hat = v / (1.0 - ADAM_B2 ** ADAM_STEP)
    return -ADAM_LR * (m_hat / (jnp.sqrt(v_hat) + ADAM_EPS) + ADAM_WD * w), m, v


def _adamw(w, g, m, v, *, g_at=None, name):
    R, C = w.shape
    row0 = 0 if g_at is None else g_at[1]
    tr = min(math.gcd(row0, 256) if row0 else 256, -(-R // 8) * 8)
    g0 = row0 // tr
    if g_at is None:
        g_spec = pl.BlockSpec((tr, C), lambda i: (i, 0))
    else:
        g_spec = pl.BlockSpec((None, tr, C), lambda i: (g_at[0], g0 + i, 0))

    def body(w_ref, g_ref, m_ref, v_ref, d_ref, mo_ref, vo_ref):
        d, mn, vn = _adamw_math(w_ref[...], g_ref[...], m_ref[...], v_ref[...])
        d_ref[...] = d
        mo_ref[...] = mn
        vo_ref[...] = vn

    blk = pl.BlockSpec((tr, C), lambda i: (i, 0))
    return pl.pallas_call(
        body, name=name, grid=(pl.cdiv(R, tr),),
        in_specs=[blk, g_spec, blk, blk],
        out_specs=[blk, blk, blk],
        out_shape=[jax.ShapeDtypeStruct((R, C), F32)] * 3,
        compiler_params=_cp("parallel"),
    )(w, g, m, v)


def _sum_parts(parts, *, name):
    P, R, C = parts.shape

    def body(p_ref, o_ref):
        acc = p_ref[0]
        for p in range(1, P):
            acc = acc + p_ref[p]
        o_ref[...] = acc

    return pl.pallas_call(
        body, name=name, grid=(1,),
        in_specs=[pl.BlockSpec((P, R, C), lambda i: (0, 0, 0))],
        out_specs=pl.BlockSpec((R, C), lambda i: (0, 0)),
        out_shape=jax.ShapeDtypeStruct((R, C), F32),
        compiler_params=_cp("arbitrary"),
    )(parts)


def _add_halves(g4, recv, c_idx, *, name):
    _, _, Rh, C = g4.shape
    tr = _pick(Rh, 256, 16)

    def body(c_ref, a_ref, b_ref, o_ref):
        o_ref[...] = (a_ref[...] + b_ref[...].astype(F32)).astype(BF)

    return pl.pallas_call(
        body, name=name,
        grid_spec=pltpu.PrefetchScalarGridSpec(
            num_scalar_prefetch=1, grid=(4, pl.cdiv(Rh, tr)),
            in_specs=[pl.BlockSpec((None, None, tr, C), lambda j, r, c: (j, c[0], r, 0)),
                      pl.BlockSpec((None, tr, C), lambda j, r, c: (j, r, 0))],
            out_specs=pl.BlockSpec((None, tr, C), lambda j, r, c: (j, r, 0))),
        out_shape=jax.ShapeDtypeStruct((4, Rh, C), BF),
        compiler_params=_cp("parallel", "arbitrary"),
    )(c_idx, g4, recv)


def _add_four(g4, from_sibling, from_chips, pos, *, name):
    _, _, Rh, C = g4.shape
    tr = _pick(Rh, 256, 16)

    def body(p_ref, a_ref, s_ref, b_ref, o_ref):
        own = a_ref[...] + s_ref[...].astype(F32)
        o_ref[...] = ((own + b_ref[0].astype(F32)) + b_ref[1].astype(F32)) + b_ref[2].astype(F32)

    return pl.pallas_call(
        body, name=name,
        grid_spec=pltpu.PrefetchScalarGridSpec(
            num_scalar_prefetch=1, grid=(pl.cdiv(Rh, tr),),
            in_specs=[pl.BlockSpec((None, None, tr, C), lambda r, p: (p[0], p[1], r, 0)),
                      pl.BlockSpec((None, tr, C), lambda r, p: (p[0], r, 0)),
                      pl.BlockSpec((3, tr, C), lambda r, p: (0, r, 0))],
            out_specs=pl.BlockSpec((None, tr, C), lambda r, p: (p[1], r, 0))),
        out_shape=jax.ShapeDtypeStruct((2, Rh, C), F32),
        compiler_params=_cp("arbitrary"),
    )(pos, g4, from_sibling, from_chips)


HBM = pl.BlockSpec(memory_space=pltpu.HBM)


def _mesh_pos():
    return lax.axis_index("x"), lax.axis_index("y"), lax.axis_index("c")


def _other_chips(x, y):
    return [(1 - x, y), (x, 1 - y), (1 - x, 1 - y)]


def _allgather_small(xs, *, name):
    m_per, n = xs.shape

    def body(x_ref, out_ref, send_sems, recv_sems, local_sem):
        x, y, c = _mesh_pos()
        me, sibling = (x, y, c), (x, y, 1 - c)
        chips = _other_chips(x, y)

        def rows(px, py, pc):
            return out_ref.at[pl.ds((4 * px + 2 * py + pc) * m_per, m_per), :]

        def copy(k, block, to, src=None):
            return pltpu.make_async_remote_copy(
                src_ref=rows(*block) if src is None else src, dst_ref=rows(*block),
                send_sem=send_sems.at[k], recv_sem=recv_sems.at[k], device_id=to, device_id_type=MESH)

        mine = pltpu.make_async_copy(x_ref, rows(*me), local_sem)
        mine.start()
        first = [copy(0, me, sibling, src=x_ref)]
        first += [copy(1 + j, me, (*chip, c), src=x_ref) for j, chip in enumerate(chips)]
        for cp in first:
            cp.start()
        passed = [copy(4 + j, (*chip, c), sibling) for j, chip in enumerate(chips)]
        for j, chip in enumerate(chips):
            copy(1 + j, (*chip, c), me).wait_recv()
            passed[j].start()
        copy(0, sibling, me).wait_recv()
        for j, chip in enumerate(chips):
            copy(4 + j, (*chip, 1 - c), me).wait_recv()
        for cp in first + passed:
            cp.wait_send()
        mine.wait()

    return pl.pallas_call(
        body, name=name,
        out_shape=jax.ShapeDtypeStruct((N_DEV * m_per, n), xs.dtype),
        in_specs=[pl.BlockSpec(memory_space=pltpu.VMEM)],
        out_specs=pl.BlockSpec(memory_space=pltpu.VMEM),
        scratch_shapes=[pltpu.SemaphoreType.DMA((7,)), pltpu.SemaphoreType.DMA((7,)), pltpu.SemaphoreType.DMA],
    )(xs)


def _chip_slab_copies(s_ref, out_ref, send_sems, recv_sems):
    R = s_ref.shape[0]
    Rh = R // 2
    x, y, c = _mesh_pos()
    me, sibling = (x, y, c), (x, y, 1 - c)
    chips = _other_chips(x, y)

    def half(px, py, pc):
        return out_ref.at[2 * px + py, pl.ds(pc * Rh, Rh), :]

    def copy(k, block, to, src=None):
        return pltpu.make_async_remote_copy(
            src_ref=half(*block) if src is None else src, dst_ref=half(*block),
            send_sem=send_sems.at[k], recv_sem=recv_sems.at[k], device_id=to, device_id_type=MESH)

    first = [copy(j, me, (*chip, c), src=s_ref.at[pl.ds(c * Rh, Rh), :]) for j, chip in enumerate(chips)]
    passed = [copy(3 + j, (*chip, c), sibling) for j, chip in enumerate(chips)]
    landed = [copy(j, (*chip, c), me) for j, chip in enumerate(chips)]
    from_sibling = [copy(3 + j, (*chip, 1 - c), me) for j, chip in enumerate(chips)]
    return first, passed, landed, from_sibling


def _allgather_chip_slabs(slab, *, name):
    R, C = slab.shape

    def body(s_ref, out_ref, send_sems, recv_sems):
        first, passed, landed, from_sibling = _chip_slab_copies(s_ref, out_ref, send_sems, recv_sems)
        for cp in first:
            cp.start()
        for arrived, onward in zip(landed, passed):
            arrived.wait_recv()
            onward.start()
        for cp in from_sibling:
            cp.wait_recv()
        for cp in first + passed:
            cp.wait_send()

    return pl.pallas_call(
        body, name=name,
        out_shape=jax.ShapeDtypeStruct((N_CHIPS, R, C), slab.dtype),
        in_specs=[HBM], out_specs=HBM,
        scratch_shapes=[pltpu.SemaphoreType.DMA((6,)), pltpu.SemaphoreType.DMA((6,))],
    )(slab)


def _swap_halves(g4, *, name):
    _, _, Rh, C = g4.shape

    def body(g_ref, out_ref, send_sems, recv_sems):
        x, y, c = _mesh_pos()
        copies = [pltpu.make_async_remote_copy(
            src_ref=g_ref.at[j, 1 - c], dst_ref=out_ref.at[j], send_sem=send_sems.at[j], recv_sem=recv_sems.at[j],
            device_id=(x, y, 1 - c), device_id_type=MESH) for j in range(N_CHIPS)]
        for cp in copies:
            cp.start()
        for cp in copies:
            cp.wait()

    return pl.pallas_call(
        body, name=name,
        out_shape=jax.ShapeDtypeStruct((N_CHIPS, Rh, C), g4.dtype),
        in_specs=[HBM], out_specs=HBM,
        scratch_shapes=[pltpu.SemaphoreType.DMA((N_CHIPS,)), pltpu.SemaphoreType.DMA((N_CHIPS,))],
    )(g4)


def _scatter_partials(part, *, name):
    _, Rh, C = part.shape

    def body(p_ref, out_ref, send_sems, recv_sems):
        x, y, c = _mesh_pos()
        copies = [pltpu.make_async_remote_copy(
            src_ref=p_ref.at[2 * px + py], dst_ref=out_ref.at[j], send_sem=send_sems.at[j], recv_sem=recv_sems.at[j],
            device_id=(px, py, c), device_id_type=MESH) for j, (px, py) in enumerate(_other_chips(x, y))]
        for cp in copies:
            cp.start()
        for cp in copies:
            cp.wait()

    return pl.pallas_call(
        body, name=name,
        out_shape=jax.ShapeDtypeStruct((3, Rh, C), part.dtype),
        in_specs=[HBM], out_specs=HBM,
        scratch_shapes=[pltpu.SemaphoreType.DMA((3,)), pltpu.SemaphoreType.DMA((3,))],
    )(part)


def _join_halves(buf, *, name):
    def body(b_ref, out_ref, send_sem, recv_sem):
        x, y, c = _mesh_pos()
        cp = pltpu.make_async_remote_copy(
            src_ref=b_ref.at[c], dst_ref=out_ref.at[c], send_sem=send_sem, recv_sem=recv_sem,
            device_id=(x, y, 1 - c), device_id_type=MESH)
        cp.start()
        cp.wait()

    return pl.pallas_call(
        body, name=name,
        out_shape=jax.ShapeDtypeStruct(buf.shape, buf.dtype),
        in_specs=[HBM], out_specs=HBM, input_output_aliases={0: 0},
        scratch_shapes=[pltpu.SemaphoreType.DMA, pltpu.SemaphoreType.DMA],
    )(buf)


def _pad_rows(a, mult):
    pad = (-a.shape[0]) % mult
    return a if pad == 0 else jnp.pad(a, ((0, pad),) + ((0, 0),) * (a.ndim - 1))


def _local_step(x, target, mod, wts, small, slab_rest=None, unpack_rest=None):
    S, D = x.shape
    HP = D // LANES
    row = lambda v: v.reshape(1, -1)
    msplit = [[row(mod[i, k * D:(k + 1) * D]) for k in range(6)] for i in range(2)]
    gw, gs = {}, {}
    dmod = [[None] * 6 for _ in range(2)]

    sh1, sc1, g1, sh2, sc2, g2 = msplit[0]
    n1w0, n2w0 = row(small["norm1_w"][0]), row(small["norm2_w"][0])
    proj0, h1_0 = _ln_matmul(x, n1w0, sc1, sh1, wts["hg_w_in"], relu2=False, name="hg_in_proj")
    gn = small["hg_gn_w"].reshape(1, LANES)
    ypre0, o0, states, *gathered = _hg_fwd(proj0, small["hg_lb"], gn, slab_rest, name="hg_fwd")
    if slab_rest is not None:
        wts = {**wts, **unpack_rest(gathered[0])}
    x1, ymix0 = _matmul_resid(ypre0, wts["hg_w_out"], x, g1, name="hg_out_proj")
    a0, u0, h2_0 = _ln_matmul(x1, n2w0, sc2, sh2, wts["mlp_w1_0"], relu2=True, name="mlp0_up")
    x2, ymlp0 = _matmul_resid(u0, wts["mlp_w2_0"], x1, g2, name="mlp0_down")

    sh1b, sc1b, g1b, sh2b, sc2b, g2b = msplit[1]
    n1w1, n2w1 = row(small["norm1_w"][1]), row(small["norm2_w"][1])
    proj1, h1_1 = _ln_matmul(x2, n1w1, sc1b, sh1b, wts["fox_w_in"], relu2=False, name="fox_in_proj")
    nheads = 2 * HP
    bf_pad = jnp.pad(small["fox_b_f"].reshape(1, nheads), ((0, 0), (0, LANES - nheads)))
    qw2 = jnp.tile(small["fox_qn_w"].reshape(1, FOX_DH), (1, 2))
    kw2 = jnp.tile(small["fox_kn_w"].reshape(1, FOX_DH), (1, 2))
    fcum = _fox_cumsum(proj1, bf_pad, name="fox_cumsum")
    qa, ka, va, vat = _fox_prep(proj1, fcum, qw2, kw2, name="fox_prep")
    jmin, imax = _fox_skip_bounds(fcum, small["fox_qn_w"], small["fox_kn_w"], nheads)
    ypre1, o1, q2 = _fox_fwd(jmin, qa, ka, vat, proj1, name="fox_fwd")
    x3, ymix1 = _matmul_resid(ypre1, wts["fox_w_out"], x2, g1b, name="fox_out_proj")
    a1, u1, h2_1 = _ln_matmul(x3, n2w1, sc2b, sh2b, wts["mlp_w1_1"], relu2=True, name="mlp1_up")
    x4, ymlp1 = _matmul_resid(u1, wts["mlp_w2_1"], x3, g2b, name="mlp1_down")

    loss, dx4, dfw = _loss_kernel(x4, row(small["final_w"]), target, name="loss")
    gs["final_w"] = dfw.reshape(-1)

    def mlp_bwd(i, dx_out, x_in, h2, a, u, ymlp, n2w, sc2_, g2_):
        dz, dm, dg2 = _gate_matmul_nt(dx_out, g2_, ymlp, wts[f"mlp_w2_{i}"], a, name=f"mlp{i}_down_bwd")
        gw[f"mlp_w2_{i}"] = _matmul_tn(u, dm[None], name=f"mlp{i}_dw2")
        gw[f"mlp_w1_{i}"] = _matmul_tn(h2, dz[None], name=f"mlp{i}_dw1")
        dx_in, dsc, dsh, dnw = _matmul_nt_lnbwd(dz[None], wts[f"mlp_w1_{i}"], x_in, n2w, sc2_, dx_out,
                                                name=f"mlp{i}_up_bwd")
        dmod[i][3], dmod[i][4], dmod[i][5] = dsh, dsc, dg2
        return dx_in, dnw

    dx3, dn2w1 = mlp_bwd(1, dx4, x3, h2_1, a1, u1, ymlp1, n2w1, sc2b, g2b)
    dyp1, dm1, dg1b = _gate_matmul_nt(dx3, g1b, ymix1, wts["fox_w_out"], None, name="fox_out_bwd")
    gw["fox_w_out"] = _matmul_tn(ypre1, dm1[None], name="fox_dw_out")
    doa = _fox_bwd_prep(dyp1, o1, proj1, name="fox_bwd_prep")
    dqa, dka, dva, colsum = _fox_bwd(imax, q2, ka, va, doa, name="fox_bwd")
    colsum = jnp.pad(colsum[:, 0, :].T, ((0, 0), (0, LANES - nheads)))
    dproj1, dqw, dkw = _fox_bwd_post(dqa, dka, dva, proj1, dyp1, o1, qw2, kw2, name="fox_bwd_post")
    dproj1, dbf = _fox_dfz(colsum, nheads, proj1, bf_pad, dproj1, name="fox_dfz")
    gw["fox_w_in"] = _matmul_tn(h1_1, dproj1, name="fox_dw_in")
    dx2, dsc, dsh, dn1w1 = _matmul_nt_lnbwd(dproj1, wts["fox_w_in"], x2, n1w1, sc1b, dx3, name="fox_in_bwd")
    dmod[1][0], dmod[1][1], dmod[1][2] = dsh, dsc, dg1b
    gs["fox_qn_w"] = dqw[0, :FOX_DH] + dqw[0, FOX_DH:]
    gs["fox_kn_w"] = dkw[0, :FOX_DH] + dkw[0, FOX_DH:]
    gs["fox_b_f"] = dbf[0, :nheads]

    dx1, dn2w0 = mlp_bwd(0, dx2, x1, h2_0, a0, u0, ymlp0, n2w0, sc2, g2)
    dyp0, dm0, dg1 = _gate_matmul_nt(dx1, g1, ymix0, wts["hg_w_out"], None, name="hg_out_bwd")
    gw["hg_w_out"] = _matmul_tn(ypre0, dm0[None], name="hg_dw_out")
    dproj0, dlb, dgn = _hg_bwd(proj0, small["hg_lb"], gn, o0, states, dyp0, name="hg_bwd")
    gw["hg_w_in"] = _matmul_tn(h1_0, dproj0, name="hg_dw_in")
    dx0, dsc, dsh, dn1w0 = _matmul_nt_lnbwd(dproj0, wts["hg_w_in"], x, n1w0, sc1, dx1, name="hg_in_bwd")
    dmod[0][0], dmod[0][1], dmod[0][2] = dsh, dsc, dg1
    gs["hg_lb"] = dlb
    gs["hg_gn_w"] = jnp.sum(dgn, axis=0)

    gs["norm1_w"] = jnp.concatenate([dn1w0, dn1w1], axis=0)
    gs["norm2_w"] = jnp.concatenate([dn2w0, dn2w1], axis=0)
    gs["dmod"] = jnp.stack([jnp.concatenate(dmod[i], axis=1)[0] for i in range(2)])
    return loss, dx0, gw, gs


def _pack_halves(layout):
    rh = -(-max(sum(a.shape[0] for _, a in half) for half in layout) // 16) * 16
    place, parts = {}, []
    for h, half in enumerate(layout):
        off = 0
        for n, a in half:
            place[n] = (h, off, a.shape[0])
            off += a.shape[0]
        parts.append(jnp.pad(jnp.concatenate([a.astype(BF) for _, a in half], axis=0), ((0, rh - off), (0, 0))))
    return jnp.concatenate(parts, axis=0), place, rh


SMALL_NAMES = ["norm1_w", "norm2_w", "hg_lb", "hg_gn_w", "fox_b_f", "fox_qn_w", "fox_kn_w", "final_w"]


def _pack_small(d, names):
    rows, offs, r0 = [], {}, 0
    for n in names:
        flat = d[n].reshape(-1)
        nr = -(-flat.shape[0] // LANES)
        rows.append(jnp.pad(flat, (0, nr * LANES - flat.shape[0])).reshape(nr, LANES))
        offs[n] = (r0, nr)
        r0 += nr
    return jnp.concatenate(rows, axis=0), offs


def _unpack_small(packed, offs, name, like):
    r0, nr = offs[name]
    return packed[r0:r0 + nr].reshape(-1)[:like.size].reshape(like.shape)


def kernel(x, c, w_mod, b_mod, norm1_w, norm2_w, hg_w_in, hg_w_out, hg_lb, hg_gn_w, fox_w_in, fox_b_f, fox_qn_w, fox_kn_w, fox_w_out, mlp_w1, mlp_w2, final_w, loss_target, m_w_mod, m_b_mod, m_norm1_w, m_norm2_w, m_hg_w_in, m_hg_w_out, m_hg_lb, m_hg_gn_w, m_fox_w_in, m_fox_b_f, m_fox_qn_w, m_fox_kn_w, m_fox_w_out, m_mlp_w1, m_mlp_w2, m_final_w, v_w_mod, v_b_mod, v_norm1_w, v_norm2_w, v_hg_w_in, v_hg_w_out, v_hg_lb, v_hg_gn_w, v_fox_w_in, v_fox_b_f, v_fox_qn_w, v_fox_kn_w, v_fox_w_out, v_mlp_w1, v_mlp_w2, v_final_w):
    S, D = x.shape[1], x.shape[2]
    nheads = D // FOX_DH
    ax, ay, ac = _mesh_pos()
    chip = 2 * ax + ay
    dev = 2 * chip + ac
    xs, tgt = x.reshape(S, D), loss_target.reshape(S, D)

    c_all = _allgather_small(_pad_rows(c.reshape(-1, LANES), 8), name="gather_c")
    c_all = c_all.reshape(N_DEV, -1)[:, :D]
    c16 = _pad_rows(c_all, 16)
    nmod = w_mod.shape[2]
    b_shard = lax.dynamic_slice_in_dim(b_mod, chip * nmod, nmod, axis=1)
    mod_shard = _mod_fwd(c16, w_mod, b_shard[:, None, :], name="mod_fwd")[:, :N_DEV]
    mod_all = _allgather_small(mod_shard.reshape(-1, LANES), name="gather_mod")
    mod_all = mod_all.reshape(N_CHIPS, 2, 2, N_DEV, nmod)[:, 0]
    mod = lax.dynamic_index_in_dim(mod_all, dev, axis=2, keepdims=False)
    mod = mod.transpose(1, 0, 2).reshape(2, N_CHIPS * nmod)

    fox_rows = fox_w_in.shape[2]
    col = lambda g: g.transpose(1, 0, 2).reshape(g.shape[1], -1)
    rowsh = lambda g: g.reshape(-1, g.shape[2])
    own = lambda g, s: lax.dynamic_update_index_in_dim(g, s, chip, 0)

    slab_in = hg_w_in[0].astype(BF)
    wts = {"hg_w_in": col(own(_allgather_chip_slabs(slab_in, name="gather_hg_w_in"), slab_in))}
    slab_rest, place_rest, rh_rest = _pack_halves(
        [[("mlp_w1", mlp_w1.reshape(2 * D, D)), ("hg_w_out", hg_w_out[0]), ("fox_w_out", fox_w_out[0])],
         [("mlp_w2", mlp_w2.reshape(2 * D, D)), ("fox_w_in", fox_w_in[0].reshape(fox_rows, D))]])

    def unpack_rest(gathered):
        gathered = own(gathered, slab_rest)

        def seg(n):
            h, off, rows = place_rest[n]
            return gathered[:, h * rh_rest + off:h * rh_rest + off + rows, :]

        w1 = seg("mlp_w1").reshape(N_CHIPS, 2, D, D)
        w2 = seg("mlp_w2").reshape(N_CHIPS, 2, D, D)
        fox_in = col(seg("fox_w_in").reshape(N_CHIPS, D, fox_rows))
        return {
            "hg_w_out": rowsh(seg("hg_w_out")), "fox_w_out": rowsh(seg("fox_w_out")),
            "mlp_w1_0": col(w1[:, 0]), "mlp_w1_1": col(w1[:, 1]), "mlp_w2_0": rowsh(w2[:, 0]), "mlp_w2_1": rowsh(w2[:, 1]),
            "fox_w_in": jnp.pad(fox_in, ((0, 0), (0, 5 * D - fox_in.shape[1]))),
        }

    small = {"norm1_w": norm1_w, "norm2_w": norm2_w, "hg_lb": hg_lb, "hg_gn_w": hg_gn_w, "fox_b_f": fox_b_f,
             "fox_qn_w": fox_qn_w, "fox_kn_w": fox_kn_w, "final_w": final_w}

    loss_part, grad_x, gw, gs = _local_step(xs, tgt, mod, wts, small, slab_rest, unpack_rest)

    layout = [[("hg_w_in", hg_w_in[0]), ("mlp_w1", mlp_w1.reshape(2 * D, D)), ("hg_w_out", hg_w_out[0])],
              [("mlp_w2", mlp_w2.reshape(2 * D, D)), ("fox_w_out", fox_w_out[0]),
               ("fox_w_in", fox_w_in[0].reshape(fox_rows, D))]]
    Rh = -(-max(sum(a.shape[0] for _, a in half) for half in layout) // 16) * 16
    place = {}
    for h, half in enumerate(layout):
        off = 0
        for n, a in half:
            place[n] = (h, off, a.shape[0])
            off += a.shape[0]
    loss = lax.psum(loss_part[0, 0], ("x", "y", "c"))

    def uncol(g, n):
        return g.reshape(g.shape[0], N_CHIPS, n).transpose(1, 0, 2)

    gseg = {
        "hg_w_in": uncol(gw["hg_w_in"], D), "hg_w_out": gw["hg_w_out"].reshape(N_CHIPS, D // 4, D),
        "fox_w_out": gw["fox_w_out"].reshape(N_CHIPS, D // 4, D),
        "mlp_w1": jnp.concatenate([uncol(gw["mlp_w1_0"], D), uncol(gw["mlp_w1_1"], D)], axis=1),
        "mlp_w2": jnp.concatenate([gw["mlp_w2_0"].reshape(N_CHIPS, D, D), gw["mlp_w2_1"].reshape(N_CHIPS, D, D)], axis=1),
        "fox_w_in": uncol(gw["fox_w_in"][:, :4 * fox_rows], fox_rows).reshape(N_CHIPS, fox_rows, D),
    }
    ghalves = []
    for half in layout:
        gh = jnp.concatenate([gseg[n] for n, _ in half], axis=1)
        ghalves.append(jnp.pad(gh, ((0, 0), (0, Rh - gh.shape[1]), (0, 0))))
    g4 = jnp.stack(ghalves, axis=1)
    from_sibling = _swap_halves(g4.astype(BF), name="rs_swap_halves")
    chip_part = _add_halves(g4, from_sibling, ac.reshape(1), name="rs_add_halves")
    from_chips = _scatter_partials(chip_part, name="rs_scatter")
    my_half = _add_four(g4, from_sibling, from_chips, jnp.stack([chip, ac]), name="rs_add_chips")
    gshard = _join_halves(my_half, name="rs_join")

    names = ["dmod"] + SMALL_NAMES
    packed, offs = _pack_small(gs, names)
    packed = _pad_rows(packed, 8)
    rp = packed.shape[0]
    parts = _allgather_small(packed, name="gather_small").reshape(N_DEV, rp, LANES)
    total = _sum_parts(parts, name="sum_small")
    r0, nr = offs["dmod"]
    dmod_all = parts[:, r0:r0 + nr].reshape(N_DEV, 2, N_CHIPS * nmod)
    dmod_shard = lax.dynamic_slice_in_dim(dmod_all, chip * nmod, nmod, axis=2).transpose(1, 0, 2)
    g_w_mod = _mod_bwd(c16, jnp.pad(dmod_shard, ((0, 0), (0, 16 - N_DEV), (0, 0))), name="mod_bwd")

    grads = {"w_mod": g_w_mod, "b_mod": _unpack_small(total, offs, "dmod", b_mod)}
    for n in SMALL_NAMES:
        grads[n] = _unpack_small(total, offs, n, small[n])

    given = dict(w_mod=(w_mod, m_w_mod, v_w_mod), b_mod=(b_mod, m_b_mod, v_b_mod), norm1_w=(norm1_w, m_norm1_w, v_norm1_w),
                 norm2_w=(norm2_w, m_norm2_w, v_norm2_w), hg_w_in=(hg_w_in, m_hg_w_in, v_hg_w_in),
                 hg_w_out=(hg_w_out, m_hg_w_out, v_hg_w_out), hg_lb=(hg_lb, m_hg_lb, v_hg_lb),
                 hg_gn_w=(hg_gn_w, m_hg_gn_w, v_hg_gn_w), fox_w_in=(fox_w_in, m_fox_w_in, v_fox_w_in),
                 fox_b_f=(fox_b_f, m_fox_b_f, v_fox_b_f), fox_qn_w=(fox_qn_w, m_fox_qn_w, v_fox_qn_w),
                 fox_kn_w=(fox_kn_w, m_fox_kn_w, v_fox_kn_w), fox_w_out=(fox_w_out, m_fox_w_out, v_fox_w_out),
                 mlp_w1=(mlp_w1, m_mlp_w1, v_mlp_w1), mlp_w2=(mlp_w2, m_mlp_w2, v_mlp_w2), final_w=(final_w, m_final_w, v_final_w))
    upd = {}

    for n, (h, off, rows) in place.items():
        w, m, v = given[n]
        flat = lambda a: a.reshape(rows, D)
        d, mn, vn = _adamw(flat(w), gshard, flat(m), flat(v), g_at=(h, off), name=f"adamw_{n}")
        grads[n] = gshard[h, off:off + rows].reshape(w.shape)
        upd[n] = tuple(a.reshape(w.shape) for a in (d, mn, vn))

    w, m, v = given["w_mod"]
    flat = lambda a: a.reshape(-1, nmod)
    upd["w_mod"] = tuple(a.reshape(w.shape) for a in _adamw(flat(w), flat(g_w_mod), flat(m), flat(v), name="adamw_w_mod"))

    snames = ["b_mod"] + SMALL_NAMES
    pw, soffs = _pack_small({n: given[n][0] for n in snames}, snames)
    pm, _ = _pack_small({n: given[n][1] for n in snames}, snames)
    pv, _ = _pack_small({n: given[n][2] for n in snames}, snames)
    pg, _ = _pack_small({n: grads[n] for n in snames}, snames)
    pw, pm, pv, pg = (_pad_rows(a, 8) for a in (pw, pm, pv, pg))
    sd, smn, svn = _adamw(pw, pg, pm, pv, name="adamw_small")
    for n in snames:
        like = given[n][0]
        upd[n] = tuple(_unpack_small(a, soffs, n, like) for a in (sd, smn, svn))

    order = ["w_mod", "b_mod", "norm1_w", "norm2_w", "hg_w_in", "hg_w_out", "hg_lb", "hg_gn_w", "fox_w_in", "fox_b_f",
             "fox_qn_w", "fox_kn_w", "fox_w_out", "mlp_w1", "mlp_w2", "final_w"]
    return (loss, grad_x.reshape(x.shape), *[grads[n] for n in order], *[upd[n][0] for n in order],
            *[upd[n][1] for n in order], *[upd[n][2] for n in order])
```

```python
import math

import jax
import jax.numpy as jnp
from jax import lax
from jax.experimental import pallas as pl
from jax.experimental.pallas import tpu as pltpu

EPS = 1e-6
ADAM_LR, ADAM_B1, ADAM_B2, ADAM_EPS, ADAM_WD, ADAM_STEP = 0.001, 0.9, 0.999, 1e-08, 0.01, 10

F32 = jnp.float32
BF = jnp.bfloat16
LANES = 128
HG_CHUNK = 64
HG_HEADS_PER_STEP = 8
HG_TOKENS_PER_STEP = 256
FOX_BWD_TILES = (8, 4, 2, 1)
LOG2E = 1.4426950408889634
FOX_DH = 64
N_CHIPS = 4
N_DEV = 8
VMEM_LIMIT = 48 * 1024 * 1024
MESH = pl.DeviceIdType.MESH

NT = (((1,), (1,)), ((), ()))
TN = (((0,), (0,)), ((), ()))


def _pick(n, pref, mult=LANES):
    if n <= pref:
        return n
    t = (pref // mult) * mult
    while t >= mult:
        if n % t == 0:
            return t
        t -= mult
    raise ValueError((n, pref, mult))


def _cp(*sem):
    return pltpu.CompilerParams(dimension_semantics=sem, vmem_limit_bytes=VMEM_LIMIT)


def _dot(a, b):
    return jnp.dot(a, b, preferred_element_type=F32)


def _dg(a, b, dims):
    return lax.dot_general(a, b, dims, preferred_element_type=F32)


def _split3(x):
    hi = x.astype(BF)
    r1 = x - hi.astype(F32)
    mid = r1.astype(BF)
    lo = (r1 - mid.astype(F32)).astype(BF)
    return hi, mid, lo


def _tri_dot(tri, x):
    hi, mid, lo = _split3(x)
    return _dot(tri, hi) + _dot(tri, mid) + _dot(tri, lo)


def _dg3(a, b, dims):
    ah, bh = a.astype(BF), b.astype(BF)
    al, bl = (a - ah.astype(F32)).astype(BF), (b - bh.astype(F32)).astype(BF)
    return _dg(ah, bh, dims) + _dg(ah, bl, dims) + _dg(al, bh, dims)


def _dg1(a, b, dims):
    return _dg(a.astype(BF), b.astype(BF), dims)


NN = (((1,), (0,)), ((), ()))


def _sigmoid(x):
    return jax.nn.sigmoid(x)


def _ln_matmul(x, nw, sc, sh, w, *, relu2, name):
    S, D = x.shape
    N = w.shape[1]
    tm, tn = _pick(S, 1024, 16), _pick(N, 1024)

    def body(x_ref, nw_ref, sc_ref, sh_ref, w_ref, *rest):
        outs, hs = rest[:-1], rest[-1]
        h_ref = outs[-1]

        @pl.when(pl.program_id(1) == 0)
        def _():
            xv = x_ref[...]
            r = lax.rsqrt(jnp.mean(xv * xv, axis=-1, keepdims=True) + EPS)
            hb = ((xv * r * nw_ref[...]) * (1.0 + sc_ref[...]) + sh_ref[...]).astype(BF)
            hs[...] = hb
            h_ref[...] = hb

        z = _dot(hs[...], w_ref[...])
        if relu2:
            a = jnp.maximum(z, 0.0)
            outs[0][...] = a.astype(BF)
            outs[1][...] = (a * a).astype(BF)
        else:
            outs[0][...] = z

    vec = pl.BlockSpec((1, D), lambda i, j: (0, 0))
    tile = pl.BlockSpec((tm, tn), lambda i, j: (i, j))
    if relu2:
        out_shape = [jax.ShapeDtypeStruct((S, N), BF), jax.ShapeDtypeStruct((S, N), BF)]
        out_specs = [tile, tile]
    else:
        out_shape = [jax.ShapeDtypeStruct((S, N), F32)]
        out_specs = [tile]
    out_shape.append(jax.ShapeDtypeStruct((S, D), BF))
    out_specs.append(pl.BlockSpec((tm, D), lambda i, j: (i, 0)))
    return pl.pallas_call(
        body, name=name, grid=(S // tm, N // tn),
        in_specs=[pl.BlockSpec((tm, D), lambda i, j: (i, 0)), vec, vec, vec,
                  pl.BlockSpec((D, tn), lambda i, j: (0, j))],
        out_specs=out_specs, out_shape=out_shape,
        scratch_shapes=[pltpu.VMEM((tm, D), BF)],
        compiler_params=_cp("parallel", "arbitrary"),
    )(x, nw, sc, sh, w)


def _matmul_resid(a, w, x, gate, *, name):
    S, K = a.shape
    D = w.shape[1]
    big = 1024 if K <= 1024 else 512
    tm, tn = _pick(S, big, 16), _pick(D, big)

    def body(a_ref, w_ref, x_ref, g_ref, o_ref, y_ref):
        y = _dot(a_ref[...], w_ref[...])
        y_ref[...] = y.astype(BF)
        o_ref[...] = x_ref[...] + g_ref[...] * y

    tile = pl.BlockSpec((tm, tn), lambda i, j: (i, j))
    return pl.pallas_call(
        body, name=name, grid=(S // tm, D // tn),
        in_specs=[pl.BlockSpec((tm, K), lambda i, j: (i, 0)), pl.BlockSpec((K, tn), lambda i, j: (0, j)),
                  tile, pl.BlockSpec((1, tn), lambda i, j: (0, j))],
        out_specs=[tile, tile],
        out_shape=[jax.ShapeDtypeStruct((S, D), F32), jax.ShapeDtypeStruct((S, D), BF)],
        compiler_params=_cp("parallel", "arbitrary"),
    )(a, w, x, gate)


def _gate_matmul_nt(dx, gate, y, w, act, *, name):
    S, D = dx.shape
    K = w.shape[0]
    tm, tn = _pick(S, 1024, 16), _pick(K, 1024)
    fused = act is not None

    def body(dx_ref, g_ref, y_ref, w_ref, *rest):
        if fused:
            act_ref, da_ref, dm_ref, dg_ref, ms = rest
        else:
            da_ref, dm_ref, dg_ref, ms = rest
        i, j = pl.program_id(0), pl.program_id(1)

        @pl.when((i == 0) & (j == 0))
        def _():
            dg_ref[...] = jnp.zeros_like(dg_ref)

        @pl.when(j == 0)
        def _():
            dxv = dx_ref[...]
            dmb = (dxv * g_ref[...]).astype(BF)
            ms[...] = dmb
            dm_ref[...] = dmb
            dg_ref[...] += jnp.sum(dxv * y_ref[...].astype(F32), axis=0, keepdims=True)

        da = _dg(ms[...], w_ref[...], NT)
        if fused:
            da_ref[...] = (da * (2.0 * act_ref[...].astype(F32))).astype(BF)
        else:
            da_ref[...] = da

    row = pl.BlockSpec((tm, D), lambda i, j: (i, 0))
    vec = pl.BlockSpec((1, D), lambda i, j: (0, 0))
    tile = pl.BlockSpec((tm, tn), lambda i, j: (i, j))
    in_specs = [row, vec, row, pl.BlockSpec((tn, D), lambda i, j: (j, 0))]
    args = [dx, gate, y, w]
    if fused:
        in_specs.append(tile)
        args.append(act)
    return pl.pallas_call(
        body, name=name, grid=(S // tm, K // tn),
        in_specs=in_specs, out_specs=[tile, row, vec],
        out_shape=[jax.ShapeDtypeStruct((S, K), BF if fused else F32), jax.ShapeDtypeStruct((S, D), BF),
                   jax.ShapeDtypeStruct((1, D), F32)],
        scratch_shapes=[pltpu.VMEM((tm, D), BF)],
        compiler_params=_cp("arbitrary", "arbitrary"),
    )(*args)


def _matmul_tn(a, b, *, name):
    S, Ka = a.shape
    P, _, Db = b.shape
    tk, tn, ts = _pick(Ka, 1024), _pick(Db, 1024), _pick(S, 1024, 16)
    npb = Db // tn

    def body(a_ref, b_ref, o_ref, acc):
        s = pl.program_id(2)

        @pl.when(s == 0)
        def _():
            acc[...] = jnp.zeros_like(acc)

        acc[...] += _dg(a_ref[...], b_ref[...], TN)

        @pl.when(s == pl.num_programs(2) - 1)
        def _():
            o_ref[...] = acc[...]

    return pl.pallas_call(
        body, name=name, grid=(Ka // tk, P * npb, S // ts),
        in_specs=[pl.BlockSpec((ts, tk), lambda i, j, s: (s, i)),
                  pl.BlockSpec((None, ts, tn), lambda i, j, s: (j // npb, s, j % npb))],
        out_specs=pl.BlockSpec((tk, tn), lambda i, j, s: (i, j)),
        out_shape=jax.ShapeDtypeStruct((Ka, P * Db), F32),
        scratch_shapes=[pltpu.VMEM((tk, tn), F32)],
        compiler_params=_cp("parallel", "parallel", "arbitrary"),
    )(a, b)


def _matmul_nt_lnbwd(g, w, x, nw, sc, dx_out, *, name):
    P, S, Dg = g.shape
    D = x.shape[1]
    tm, tk = _pick(S, 1024, 16), _pick(Dg, 1024)
    npb = Dg // tk
    nk = P * npb

    def body(g_ref, w_ref, x_ref, nw_ref, sc_ref, dxo_ref, dx_ref, dsc_ref, dsh_ref, dnw_ref, acc):
        i, k = pl.program_id(0), pl.program_id(1)

        @pl.when((i == 0) & (k == 0))
        def _():
            dsc_ref[...] = jnp.zeros_like(dsc_ref)
            dsh_ref[...] = jnp.zeros_like(dsh_ref)
            dnw_ref[...] = jnp.zeros_like(dnw_ref)

        @pl.when(k == 0)
        def _():
            acc[...] = jnp.zeros_like(acc)

        acc[...] += _dg(g_ref[...], w_ref[...], NT)

        @pl.when(k == nk - 1)
        def _():
            dh = acc[...]
            xv = x_ref[...]
            nwv = nw_ref[...]
            r = lax.rsqrt(jnp.mean(xv * xv, axis=-1, keepdims=True) + EPS)
            xr = xv * r
            dn = dh * (1.0 + sc_ref[...])
            dsc_ref[...] += jnp.sum(dh * (xr * nwv), axis=0, keepdims=True)
            dsh_ref[...] += jnp.sum(dh, axis=0, keepdims=True)
            dnw_ref[...] += jnp.sum(dn * xr, axis=0, keepdims=True)
            u = dn * nwv
            dx_ref[...] = dxo_ref[...] + r * (u - xr * jnp.mean(u * xr, axis=-1, keepdims=True))

    row = pl.BlockSpec((tm, D), lambda i, k: (i, 0))
    vec = pl.BlockSpec((1, D), lambda i, k: (0, 0))
    return pl.pallas_call(
        body, name=name, grid=(S // tm, nk),
        in_specs=[pl.BlockSpec((None, tm, tk), lambda i, k: (k // npb, i, k % npb)),
                  pl.BlockSpec((D, tk), lambda i, k: (0, k)), row, vec, vec, row],
        out_specs=[row, vec, vec, vec],
        out_shape=[jax.ShapeDtypeStruct((S, D), F32)] + [jax.ShapeDtypeStruct((1, D), F32)] * 3,
        scratch_shapes=[pltpu.VMEM((tm, D), F32)],
        compiler_params=_cp("arbitrary", "arbitrary"),
    )(g, w, x, nw, sc, dx_out)


def _loss_kernel(x, fw, tgt, *, name):
    S, D = x.shape
    tm = _pick(S, 512, 8)

    def body(x_ref, fw_ref, t_ref, l_ref, dx_ref, dfw_ref):
        @pl.when(pl.program_id(0) == 0)
        def _():
            l_ref[...] = jnp.zeros_like(l_ref)
            dfw_ref[...] = jnp.zeros_like(dfw_ref)

        xv = x_ref[...]
        fwv = fw_ref[...]
        r = lax.rsqrt(jnp.mean(xv * xv, axis=-1, keepdims=True) + EPS)
        xr = xv * r
        err = xr * fwv - t_ref[...]
        per_tok = jnp.mean(err * err, axis=-1, keepdims=True)
        l_ref[...] += 0.5 * jnp.sum(per_tok, axis=0, keepdims=True)
        dy = err * (1.0 / D)
        dfw_ref[...] += jnp.sum(dy * xr, axis=0, keepdims=True)
        u = dy * fwv
        dx_ref[...] = r * (u - xr * jnp.mean(u * xr, axis=-1, keepdims=True))

    row = pl.BlockSpec((tm, D), lambda i: (i, 0))
    vec = pl.BlockSpec((1, D), lambda i: (0, 0))
    return pl.pallas_call(
        body, name=name, grid=(S // tm,),
        in_specs=[row, vec, row],
        out_specs=[pl.BlockSpec((1, LANES), lambda i: (0, 0)), row, vec],
        out_shape=[jax.ShapeDtypeStruct((1, LANES), F32), jax.ShapeDtypeStruct((S, D), F32),
                   jax.ShapeDtypeStruct((1, D), F32)],
        compiler_params=_cp("arbitrary"),
    )(x, fw, tgt)


def _hg_lower_bound(lb3):
    mx = jnp.max(lb3, axis=0, keepdims=True)
    e = jnp.exp(lb3 - mx)
    p = e / jnp.sum(e, axis=0, keepdims=True)
    return p[0:1, :], p


def _hg_chunk_common(qr, fz, lbv):
    sq = _sigmoid(qr)
    q = qr * sq
    sig = _sigmoid(fz)
    f = lbv + (1.0 - lbv) * sig
    k = (1.0 - lbv) * (1.0 - sig)
    return q, sq, sig, f, k, jnp.log(f)


def _row_of(x, rows, r):
    return jnp.sum(jnp.where(rows == r, x, 0.0), axis=0, keepdims=True)


def _hg_fwd(proj, hg_lb, gn, slab=None, *, name):
    S = proj.shape[0]
    D = proj.shape[1] // 4
    H = D // LANES
    HB = min(HG_HEADS_PER_STEP, H)
    W = HB * LANES
    C = HG_CHUNK
    T = _pick(S, HG_TOKENS_PER_STEP, C)
    nch, nb = T // C, S // T
    ng = H // HB
    fused = slab is not None

    def body(q_ref, fz_ref, v_ref, g_ref, lb_ref, gn_ref, *rest):
        if fused:
            s_ref, y_ref, o_ref, sts_ref, out_ref, st, send_sems, recv_sems = rest
            first, passed, landed, from_sibling = _chip_slab_copies(s_ref, out_ref, send_sems, recv_sems)
            hgrp, n = pl.program_id(0), pl.program_id(1)

            @pl.when((hgrp == 0) & (n == 0))
            def _():
                for cp in first:
                    cp.start()

            @pl.when((hgrp == ng - 1) & (n == (3 * nb) // 4))
            def _():
                for arrived, onward in zip(landed, passed):
                    arrived.wait_recv()
                    onward.start()
        else:
            y_ref, o_ref, sts_ref, st = rest

        @pl.when(pl.program_id(1) == 0)
        def _():
            st[...] = jnp.zeros_like(st)

        lb_all, _ = _hg_lower_bound(lb_ref[...])
        gnv = gn_ref[...]
        ri = lax.broadcasted_iota(jnp.int32, (C, C), 0)
        ci_ = lax.broadcasted_iota(jnp.int32, (C, C), 1)
        low = ri >= ci_
        tri = jnp.where(low, 1.0, 0.0).astype(BF)
        rows = lax.broadcasted_iota(jnp.int32, (C, LANES), 0)

        def chunk(ci, carry):
            sl = pl.ds(pl.multiple_of(ci * C, C), C)
            for hh in range(HB):
                ls = slice(hh * LANES, (hh + 1) * LANES)
                q, _, _, _, k, logf = _hg_chunk_common(q_ref[sl, ls], fz_ref[sl, ls], lb_all[:, ls])
                vv = v_ref[sl, ls]
                gg = g_ref[sl, ls]
                G = _tri_dot(tri, logf)
                Gm = _row_of(G, rows, C // 2 - 1)
                Gl = _row_of(G, rows, C - 1)
                qt = q * jnp.exp(G - Gm)
                kt = k * jnp.exp(Gm - G)
                A = jnp.where(low, _dg1(qt, kt, NT), 0.0)
                Sv = st[hh]
                sts_ref[hh, ci] = Sv
                o = _dg1(A, vv, NN) + _dg1(q * jnp.exp(G), Sv, NT)
                st[hh] = Sv * jnp.exp(Gl) + _dg1(vv, k * jnp.exp(Gl - G), TN)
                r = lax.rsqrt(jnp.mean(o * o, axis=-1, keepdims=True) + EPS)
                y_ref[sl, ls] = ((o * r * gnv) * (gg * _sigmoid(gg))).astype(BF)
                o_ref[sl, ls] = o
            return carry

        lax.fori_loop(0, nch, chunk, 0)

        if fused:
            @pl.when((hgrp == ng - 1) & (n == nb - 1))
            def _():
                for cp in from_sibling:
                    cp.wait_recv()
                for cp in first + passed:
                    cp.wait_send()

    def part(p):
        return pl.BlockSpec((T, W), lambda h, n: (n, p * ng + h))

    blk = pl.BlockSpec((T, W), lambda h, n: (n, h))
    in_specs = [part(0), part(1), part(2), part(3),
                pl.BlockSpec((3, W), lambda h, n: (0, h)), pl.BlockSpec((1, LANES), lambda h, n: (0, 0))]
    out_specs = [blk, blk, pl.BlockSpec((HB, nch, LANES, LANES), lambda h, n: (h, n, 0, 0))]
    out_shape = [jax.ShapeDtypeStruct((S, D), BF), jax.ShapeDtypeStruct((S, D), F32),
                 jax.ShapeDtypeStruct((H, S // C, LANES, LANES), F32)]
    scratch = [pltpu.VMEM((HB, LANES, LANES), F32)]
    args = [proj, proj, proj, proj, hg_lb, gn]
    if fused:
        in_specs.append(HBM)
        out_specs.append(HBM)
        out_shape.append(jax.ShapeDtypeStruct((N_CHIPS,) + slab.shape, slab.dtype))
        scratch += [pltpu.SemaphoreType.DMA((6,)), pltpu.SemaphoreType.DMA((6,))]
        args.append(slab)
    return pl.pallas_call(
        body, name=name, grid=(ng, nb), in_specs=in_specs, out_specs=out_specs, out_shape=out_shape,
        scratch_shapes=scratch, compiler_params=_cp("arbitrary", "arbitrary"),
    )(*args)


def _hg_bwd(proj, hg_lb, gn, o_all, states, dy, *, name):
    S = proj.shape[0]
    D = proj.shape[1] // 4
    H = D // LANES
    HB = min(HG_HEADS_PER_STEP, H)
    W = HB * LANES
    C = HG_CHUNK
    T = _pick(S, HG_TOKENS_PER_STEP, C)
    nch, nb = T // C, S // T

    def body(q_ref, fz_ref, v_ref, g_ref, lb_ref, gn_ref, o_ref, sts_ref, dy_ref,
             dp_ref, dlb_ref, dgn_ref, dst, dlb_acc):
        n = pl.program_id(1)

        @pl.when(n == 0)
        def _():
            dst[...] = jnp.zeros_like(dst)
            dlb_acc[...] = jnp.zeros_like(dlb_acc)
            dgn_ref[...] = jnp.zeros_like(dgn_ref)

        lb_all, p3 = _hg_lower_bound(lb_ref[...])
        gnv = gn_ref[...]
        ri = lax.broadcasted_iota(jnp.int32, (C, C), 0)
        ci_ = lax.broadcasted_iota(jnp.int32, (C, C), 1)
        low = ri >= ci_
        tri = jnp.where(low, 1.0, 0.0).astype(BF)
        triu = jnp.where(ri <= ci_, 1.0, 0.0).astype(BF)
        rows = lax.broadcasted_iota(jnp.int32, (C, LANES), 0)

        def chunk(cj, carry):
            ci = nch - 1 - cj
            sl = pl.ds(pl.multiple_of(ci * C, C), C)
            for hh in range(HB):
                ls = slice(hh * LANES, (hh + 1) * LANES)
                lbv = lb_all[:, ls]
                qr = q_ref[sl, ls]
                q, sq, sig, f, k, logf = _hg_chunk_common(qr, fz_ref[sl, ls], lbv)
                vv = v_ref[sl, ls]
                gg = g_ref[sl, ls]
                o = o_ref[sl, ls]
                dyv = dy_ref[sl, ls]
                G = _tri_dot(tri, logf)
                Gm = _row_of(G, rows, C // 2 - 1)
                Gl = _row_of(G, rows, C - 1)
                eG, e_qm, e_km, e_lk, eGl = jnp.exp(G), jnp.exp(G - Gm), jnp.exp(Gm - G), jnp.exp(Gl - G), jnp.exp(Gl)
                qt = q * e_qm
                kt = k * e_km
                A = jnp.where(low, _dg1(qt, kt, NT), 0.0)
                sg = _sigmoid(gg)
                r = lax.rsqrt(jnp.mean(o * o, axis=-1, keepdims=True) + EPS)
                on = o * r
                d_onw = dyv * (gg * sg)
                dgn_ref[hh] += jnp.sum(d_onw * on, axis=0, keepdims=True)
                dgg = dyv * (on * gnv) * (sg * (1.0 + gg * (1.0 - sg)))
                u = d_onw * gnv
                do = r * (u - on * jnp.mean(u * on, axis=-1, keepdims=True))
                Sv = sts_ref[hh, ci]
                dSv = dst[hh]
                dA = jnp.where(low, _dg3(do, vv, NT), 0.0)
                kdec = k * e_lk
                dv = _dg1(A, do, TN) + _dg1(kdec, dSv, NT)
                dq = _dg3(dA, kt, NN) * e_qm + eG * _dg3(do, Sv, NN)
                dk = _dg3(dA, qt, TN) * e_km + e_lk * _dg3(vv, dSv, NN)
                s_end = Sv * eGl + _dg3(vv, kdec, TN)
                dgl = jnp.sum(dSv * s_end, axis=0, keepdims=True)
                dG = q * dq - k * dk + jnp.where(rows == C - 1, dgl, 0.0)
                dlogf = _tri_dot(triu, dG) - f * dk
                dst[hh] = dSv * eGl + _dg1(do, q * eG, TN)
                dlf_f = dlogf / f
                dlb_acc[:, ls] += jnp.sum(dlf_f * (1.0 - sig), axis=0, keepdims=True)
                dp_ref[0, sl, ls] = (dq * (sq * (1.0 + qr * (1.0 - sq)))).astype(BF)
                dp_ref[1, sl, ls] = (dlf_f * (1.0 - lbv) * sig * (1.0 - sig)).astype(BF)
                dp_ref[2, sl, ls] = dv.astype(BF)
                dp_ref[3, sl, ls] = dgg.astype(BF)
            return carry

        lax.fori_loop(0, nch, chunk, 0)
        sel = jnp.where(lax.broadcasted_iota(jnp.int32, (3, W), 0) == 0, 1.0, 0.0)
        dlb_ref[...] = lb_all * (sel - p3) * dlb_acc[...]

    ng = H // HB

    def part(p):
        return pl.BlockSpec((T, W), lambda h, n: (nb - 1 - n, p * ng + h))

    blk = pl.BlockSpec((T, W), lambda h, n: (nb - 1 - n, h))
    return pl.pallas_call(
        body, name=name, grid=(ng, nb),
        in_specs=[part(0), part(1), part(2), part(3),
                  pl.BlockSpec((3, W), lambda h, n: (0, h)), pl.BlockSpec((1, LANES), lambda h, n: (0, 0)),
                  blk, pl.BlockSpec((HB, nch, LANES, LANES), lambda h, n: (h, nb - 1 - n, 0, 0)), blk],
        out_specs=[pl.BlockSpec((4, T, W), lambda h, n: (0, nb - 1 - n, h)),
                   pl.BlockSpec((3, W), lambda h, n: (0, h)),
                   pl.BlockSpec((HB, 1, LANES), lambda h, n: (h, 0, 0))],
        out_shape=[jax.ShapeDtypeStruct((4, S, D), BF), jax.ShapeDtypeStruct((3, D), F32),
                   jax.ShapeDtypeStruct((H, 1, LANES), F32)],
        scratch_shapes=[pltpu.VMEM((HB, LANES, LANES), F32), pltpu.VMEM((1, W), F32)],
        compiler_params=_cp("parallel", "arbitrary"),
    )(proj, proj, proj, proj, hg_lb, gn, o_all, states, dy)


def _log_sigmoid(u):
    return jnp.minimum(u, 0.0) - jnp.log(1.0 + jnp.exp(-jnp.abs(u)))


def _lane_put(base, lane, first, pieces):
    for n, p in enumerate(pieces):
        base = jnp.where(lane == first + n, p, base)
    return base


def _fox_cumsum(proj, bf_pad, *, name):
    S = proj.shape[0]
    D = proj.shape[1] // 5
    T = _pick(S, 256, 8)

    def body(fz_ref, b_ref, f_ref, carry):
        @pl.when(pl.program_id(0) == 0)
        def _():
            carry[...] = jnp.zeros_like(carry)

        logf = _log_sigmoid(fz_ref[...] + b_ref[...])
        tri = jnp.where(lax.broadcasted_iota(jnp.int32, (T, T), 0) >= lax.broadcasted_iota(jnp.int32, (T, T), 1),
                        1.0, 0.0).astype(BF)
        fv = _tri_dot(tri, logf) + carry[...]
        f_ref[...] = fv
        carry[...] = _row_of(fv, lax.broadcasted_iota(jnp.int32, (T, LANES), 0), T - 1)

    return pl.pallas_call(
        body, name=name, grid=(S // T,),
        in_specs=[pl.BlockSpec((T, LANES), lambda i: (i, 4 * D // LANES)), pl.BlockSpec((1, LANES), lambda i: (0, 0))],
        out_specs=pl.BlockSpec((T, LANES), lambda i: (i, 0)),
        out_shape=jax.ShapeDtypeStruct((S, LANES), F32),
        scratch_shapes=[pltpu.VMEM((1, LANES), F32)],
        compiler_params=_cp("arbitrary"),
    )(proj, bf_pad)


def _pair_stats(sq, lo):
    del lo
    a = lax.broadcasted_iota(jnp.int32, (LANES, LANES), 0) < FOX_DH
    b = lax.broadcasted_iota(jnp.int32, (LANES, LANES), 1) < FOX_DH
    avg = jnp.where(a == b, 1.0 / FOX_DH, 0.0).astype(BF)
    hi, mid, low = _split3(sq)
    return _dot(hi, avg) + _dot(mid, avg) + _dot(low, avg)


def _fox_prep(proj, fcum, qw2, kw2, *, name):
    S = proj.shape[0]
    D = proj.shape[1] // 5
    HP = D // LANES
    T = _pick(S, 512, 16)

    def body(q_ref, k_ref, v_ref, f_ref, qw_ref, kw_ref, qa_ref, ka_ref, va_ref, vt_ref):
        hp = pl.program_id(1)
        lane = lax.broadcasted_iota(jnp.int32, (T, LANES), 1)
        lo = lane < FOX_DH
        qv, kv, vv, fv = q_ref[...], k_ref[...], v_ref[...], f_ref[...]
        qn = qv * lax.rsqrt(_pair_stats(qv * qv, lo) + EPS) * qw_ref[...] * (0.125 * LOG2E)
        kn = kv * lax.rsqrt(_pair_stats(kv * kv, lo) + EPS) * kw_ref[...]
        ones_q = jnp.where((lane >= 67) & (lane <= 69), 1.0, 0.0)
        ones_k = jnp.where(((lane >= 64) & (lane <= 66)) | ((lane >= 70) & (lane <= 72)), 1.0, 0.0)
        ones_v = jnp.where((lane >= 64) & (lane <= 66), 1.0, 0.0)
        for hh in range(2):
            fh = jnp.sum(jnp.where(lane == 2 * hp + hh, fv, 0.0), axis=-1, keepdims=True) * LOG2E
            pieces = [p.astype(F32) for p in _split3(fh)]

            def half(x):
                return jnp.where(lo, x if hh == 0 else pltpu.roll(x, FOX_DH, 1), 0.0)

            qa_ref[hh] = _lane_put(half(qn) + ones_q, lane, 64, pieces).astype(BF)
            ka_ref[hh] = _lane_put(half(kn) + ones_k, lane, 67, [-p for p in pieces]).astype(BF)
            va = half(vv) + ones_v
            va_ref[hh] = va.astype(BF)
            vt_ref[hh] = va.T.astype(BF)

    def part(p):
        return pl.BlockSpec((T, LANES), lambda i, hp: (i, p * HP + hp))

    vec = pl.BlockSpec((1, LANES), lambda i, hp: (0, 0))
    aug = pl.BlockSpec((2, T, LANES), lambda i, hp: (hp, i, 0))
    return pl.pallas_call(
        body, name=name, grid=(S // T, HP),
        in_specs=[part(0), part(1), part(2), pl.BlockSpec((T, LANES), lambda i, hp: (i, 0)), vec, vec],
        out_specs=[aug, aug, aug, pl.BlockSpec((2, LANES, T), lambda i, hp: (hp, 0, i))],
        out_shape=[jax.ShapeDtypeStruct((2 * HP, S, LANES), BF)] * 3 + [jax.ShapeDtypeStruct((2 * HP, LANES, S), BF)],
        compiler_params=_cp("parallel", "arbitrary"),
    )(proj, proj, proj, fcum, qw2, kw2)


def _fox_block(S):
    return _pick(S, 256, 16)


def _fox_skip_bounds(fcum, qn_w, kn_w, nheads):
    S = fcum.shape[0]
    B = _fox_block(S)
    qk = 8.0 * LOG2E * 1.02 * jnp.max(jnp.abs(qn_w)) * jnp.max(jnp.abs(kn_w))
    thresh = -(2.0 * qk + 160.0)
    f2 = fcum[:, :nheads] * LOG2E
    first, last = f2[0::B], f2[B - 1::B]
    nb = S // B
    blk = jnp.arange(nb)
    dead = (first[0::2, None, :] - last[None, :, :]) < thresh
    jmin = jnp.sum(dead & (blk[None, :, None] < 2 * jnp.arange(nb // 2)[:, None, None]), axis=1)
    live = (first[:, None, :] - last[None, :, :]) >= thresh
    imax = blk[:, None] + jnp.sum(live & (blk[:, None, None] > blk[None, :, None]), axis=0)
    jmin, imax = jnp.zeros_like(jmin), jnp.full_like(imax, nb - 1)
    return jmin.T.astype(jnp.int32), imax.T.astype(jnp.int32)


def _fox_fwd(jmin, qa, ka, vat, proj, *, name):
    H, S, _ = qa.shape
    HP = H // 2
    D = HP * LANES
    B = _fox_block(S)
    BQ = 2 * B
    nq = S // BQ

    def body(jmin_ref, q_ref, k_ref, vt_ref, g_ref, y_ref, o_ref, q2_ref):
        hp, i = pl.program_id(0), pl.program_id(1)
        lane = lax.broadcasted_iota(jnp.int32, (BQ, LANES), 1)
        lo = lane < FOX_DH
        in_lse = (lane >= 70) & (lane <= 72)
        causal = lax.broadcasted_iota(jnp.int32, (BQ, BQ), 0) <= lax.broadcasted_iota(jnp.int32, (BQ, BQ), 1)
        row = lax.broadcasted_iota(jnp.int32, (LANES, BQ), 0)
        m0, acc0 = jnp.full((1, BQ), -jnp.inf, F32), jnp.zeros((LANES, BQ), F32)
        outs = []
        for hh in range(2):
            qb = q_ref[hh]

            def block(j, carry, masked=False):
                m, acc = carry
                sl = pl.ds(pl.multiple_of(j * BQ, BQ), BQ)
                st = _dg(k_ref[hh, sl, :], qb, NT)
                if masked:
                    st = jnp.where(causal, st, -jnp.inf)
                m_new = jnp.maximum(m, jnp.max(st, axis=0, keepdims=True))
                p = jnp.exp2(st - m_new)
                ph = p.astype(BF)
                pl_ = (p - ph.astype(F32)).astype(BF)
                vt = vt_ref[hh, :, sl]
                pv = _dot(jnp.concatenate([vt, vt], axis=1), jnp.concatenate([ph, pl_], axis=0))
                return m_new, acc * jnp.exp2(m - m_new) + pv

            carry = lax.fori_loop(jmin_ref[2 * hp + hh, i] // 2, i, block, (m0, acc0))
            m, acc = block(i, carry, masked=True)
            l = jnp.sum(jnp.where(row == FOX_DH, acc, 0.0), axis=0, keepdims=True)
            tile = acc / l
            for n, piece in enumerate(_split3(m + jnp.log2(l))):
                tile = jnp.where(row == 70 + n, -(piece.astype(F32)), tile)
            tile = tile.T
            outs.append(tile)
            q2_ref[hh] = jnp.where(in_lse, tile, qb.astype(F32)).astype(BF)
        o = jnp.where(lo, outs[0], pltpu.roll(outs[1], FOX_DH, 1))
        o_ref[...] = o
        y_ref[...] = (o * _sigmoid(g_ref[...])).astype(BF)

    blk = pl.BlockSpec((BQ, LANES), lambda hp, i, jm: (i, hp))
    qblk = pl.BlockSpec((2, BQ, LANES), lambda hp, i, jm: (hp, i, 0))
    full = pl.BlockSpec((2, S, LANES), lambda hp, i, jm: (hp, 0, 0))
    full_t = pl.BlockSpec((2, LANES, S), lambda hp, i, jm: (hp, 0, 0))
    return pl.pallas_call(
        body, name=name,
        grid_spec=pltpu.PrefetchScalarGridSpec(
            num_scalar_prefetch=1, grid=(HP, nq),
            in_specs=[qblk, full, full_t, pl.BlockSpec((BQ, LANES), lambda hp, i, jm: (i, 3 * HP + hp))],
            out_specs=[blk, blk, qblk]),
        out_shape=[jax.ShapeDtypeStruct((S, D), BF), jax.ShapeDtypeStruct((S, D), F32),
                   jax.ShapeDtypeStruct((H, S, LANES), BF)],
        compiler_params=_cp("parallel", "arbitrary"),
    )(jmin, qa, ka, vat, proj)


def _fox_bwd_prep(dy, o, proj, *, name):
    S, D = dy.shape
    HP = D // LANES
    T = _pick(S, 512, 16)

    def body(dy_ref, o_ref, g_ref, da_ref):
        lane = lax.broadcasted_iota(jnp.int32, (T, LANES), 1)
        lo = lane < FOX_DH
        do = (dy_ref[...] * _sigmoid(g_ref[...])).astype(BF).astype(F32)
        prod = do * o_ref[...]
        d_lo = jnp.sum(jnp.where(lo, prod, 0.0), axis=-1, keepdims=True)
        d_hi = jnp.sum(jnp.where(lo, 0.0, prod), axis=-1, keepdims=True)
        for hh, delta in enumerate((d_lo, d_hi)):
            base = jnp.where(lo, do if hh == 0 else pltpu.roll(do, FOX_DH, 1), 0.0)
            da_ref[hh] = _lane_put(base, lane, 64, [-(p.astype(F32)) for p in _split3(delta)]).astype(BF)

    blk = pl.BlockSpec((T, LANES), lambda i, hp: (i, hp))
    return pl.pallas_call(
        body, name=name, grid=(S // T, HP),
        in_specs=[blk, blk, pl.BlockSpec((T, LANES), lambda i, hp: (i, 3 * HP + hp))],
        out_specs=pl.BlockSpec((2, T, LANES), lambda i, hp: (hp, i, 0)),
        out_shape=jax.ShapeDtypeStruct((2 * HP, S, LANES), BF),
        compiler_params=_cp("parallel", "arbitrary"),
    )(dy, o, proj)


def _fox_bwd(imax, q2, ka, va, doa, *, name):
    H, S, _ = q2.shape
    B = _fox_block(S)
    nb = S // B

    def body(imax_ref, q_ref, do_ref, k_ref, v_ref, dq_ref, dk_ref, dv_ref, cs_ref):
        j = pl.program_id(1)
        end = imax_ref[pl.program_id(0), j] + 1

        @pl.when(j == 0)
        def _():
            dq_ref[...] = jnp.zeros_like(dq_ref)

        kb, vb = k_ref[...], v_ref[...]
        causal = lax.broadcasted_iota(jnp.int32, (B, B), 1) <= lax.broadcasted_iota(jnp.int32, (B, B), 0)

        def step(i, carry, masked=False, nblk=1):
            dk_acc, dv_acc, cs_acc = carry
            rows = nblk * B
            sl = pl.ds(pl.multiple_of(i * B, B), rows)
            qb, dob = q_ref[sl, :], do_ref[sl, :]
            s = _dg(qb, kb, NT)
            if masked:
                s = jnp.where(causal, s, -jnp.inf)
            p = jnp.exp2(s)
            ds = p * _dg(dob, vb, NT)
            dsb = ds.astype(BF)
            cs_acc = cs_acc + jnp.sum(ds.reshape(rows // 8, 8, B), axis=0)
            dv_acc = dv_acc + _dg(p.astype(BF), dob, TN)
            dk_acc = dk_acc + _dg(dsb, qb, TN)
            dq_ref[sl, :] += _dot(dsb, kb)
            return dk_acc, dv_acc, cs_acc

        zero = jnp.zeros((B, LANES), F32)
        carry = step(j, (zero, zero, jnp.zeros((8, B), F32)), masked=True)
        pos = j + 1
        for U in FOX_BWD_TILES:
            n = (end - pos) // U
            carry = lax.fori_loop(0, n, lambda ii, c, pos=pos, U=U: step(pos + U * ii, c, nblk=U), carry)
            pos = pos + U * n
        dk_acc, dv_acc, cs_acc = carry
        dk_ref[...] = dk_acc
        dv_ref[...] = dv_acc
        cs_ref[...] = jnp.sum(cs_acc, axis=0, keepdims=True)

    full = pl.BlockSpec((None, S, LANES), lambda h, j, im: (h, 0, 0))
    blk = pl.BlockSpec((None, B, LANES), lambda h, j, im: (h, j, 0))
    return pl.pallas_call(
        body, name=name,
        grid_spec=pltpu.PrefetchScalarGridSpec(
            num_scalar_prefetch=1, grid=(H, nb),
            in_specs=[full, full, blk, blk],
            out_specs=[full, blk, blk, pl.BlockSpec((None, 1, B), lambda h, j, im: (h, 0, j))]),
        out_shape=[jax.ShapeDtypeStruct((H, S, LANES), F32)] * 3 + [jax.ShapeDtypeStruct((H, 1, S), F32)],
        compiler_params=_cp("parallel", "arbitrary"),
    )(imax, q2, doa, ka, va)


def _fox_bwd_post(dqa, dka, dva, proj, dy, o, qw2, kw2, *, name):
    S, D = dy.shape
    HP = D // LANES
    T = _pick(S, 512, 16)

    def body(dq_ref, dk_ref, dv_ref, q_ref, k_ref, g_ref, dy_ref, o_ref, qw_ref, kw_ref, dp_ref, dqw_ref, dkw_ref):
        @pl.when((pl.program_id(0) == 0) & (pl.program_id(1) == 0))
        def _():
            dqw_ref[...] = jnp.zeros_like(dqw_ref)
            dkw_ref[...] = jnp.zeros_like(dkw_ref)

        lane = lax.broadcasted_iota(jnp.int32, (T, LANES), 1)
        lo = lane < FOX_DH

        def pair(ref):
            return jnp.where(lo, ref[0], pltpu.roll(ref[1], FOX_DH, 1))

        def norm_bwd(xv, w, dyn, dw_ref):
            r = lax.rsqrt(_pair_stats(xv * xv, lo) + EPS)
            xr = xv * r
            dw_ref[...] += jnp.sum(dyn * xr, axis=0, keepdims=True)
            u = dyn * w
            return r * (u - xr * _pair_stats(u * xr, lo))

        dp_ref[0] = norm_bwd(q_ref[...], qw_ref[...], pair(dq_ref) * 0.125, dqw_ref).astype(BF)
        dp_ref[1] = norm_bwd(k_ref[...], kw_ref[...], pair(dk_ref) * (1.0 / LOG2E), dkw_ref).astype(BF)
        dp_ref[2] = pair(dv_ref).astype(BF)
        sg = _sigmoid(g_ref[...])
        dp_ref[3] = (dy_ref[...] * o_ref[...] * sg * (1.0 - sg)).astype(BF)

    def part(p):
        return pl.BlockSpec((T, LANES), lambda i, hp: (i, p * HP + hp))

    aug = pl.BlockSpec((2, T, LANES), lambda i, hp: (hp, i, 0))
    blk = pl.BlockSpec((T, LANES), lambda i, hp: (i, hp))
    vec = pl.BlockSpec((1, LANES), lambda i, hp: (0, 0))
    return pl.pallas_call(
        body, name=name, grid=(S // T, HP),
        in_specs=[aug, aug, aug, part(0), part(1), part(3), blk, blk, vec, vec],
        out_specs=[pl.BlockSpec((4, T, LANES), lambda i, hp: (0, i, hp)), vec, vec],
        out_shape=[jax.ShapeDtypeStruct((5, S, D), BF), jax.ShapeDtypeStruct((1, LANES), F32),
                   jax.ShapeDtypeStruct((1, LANES), F32)],
        compiler_params=_cp("arbitrary", "arbitrary"),
    )(dqa, dka, dva, proj, proj, proj, dy, o, qw2, kw2)


def _fox_dfz(colsum, nheads, proj, bf_pad, dproj, *, name):
    S = colsum.shape[0]
    H = nheads
    D = dproj.shape[2]
    T = _pick(S, 256, 16)
    nb = S // T

    def body(cs_ref, fz_ref, b_ref, _, dp_ref, db_ref, carry):
        @pl.when(pl.program_id(0) == 0)
        def _():
            carry[...] = jnp.zeros_like(carry)
            db_ref[...] = jnp.zeros_like(db_ref)

        lane = lax.broadcasted_iota(jnp.int32, (T, LANES), 1)
        df = -cs_ref[...]
        triu = jnp.where(lax.broadcasted_iota(jnp.int32, (T, T), 0) <= lax.broadcasted_iota(jnp.int32, (T, T), 1),
                         1.0, 0.0).astype(BF)
        dlogf = _tri_dot(triu, df) + carry[...]
        carry[...] = _row_of(dlogf, lax.broadcasted_iota(jnp.int32, (T, LANES), 0), 0)
        dfz = jnp.where(lane < H, dlogf * _sigmoid(-(fz_ref[...] + b_ref[...])), 0.0)
        db_ref[...] += jnp.sum(dfz, axis=0, keepdims=True)
        dp_ref[...] = jnp.zeros_like(dp_ref)
        dp_ref[:, 0:LANES] = dfz.astype(BF)

    return pl.pallas_call(
        body, name=name, grid=(nb,),
        in_specs=[pl.BlockSpec((T, LANES), lambda i: (nb - 1 - i, 0)),
                  pl.BlockSpec((T, LANES), lambda i: (nb - 1 - i, 4 * D // LANES)),
                  pl.BlockSpec((1, LANES), lambda i: (0, 0)),
                  pl.BlockSpec(memory_space=pl.ANY)],
        out_specs=[pl.BlockSpec((None, T, D), lambda i: (4, nb - 1 - i, 0)), pl.BlockSpec((1, LANES), lambda i: (0, 0))],
        out_shape=[jax.ShapeDtypeStruct(dproj.shape, BF), jax.ShapeDtypeStruct((1, LANES), F32)],
        scratch_shapes=[pltpu.VMEM((1, LANES), F32)],
        input_output_aliases={3: 0},
        compiler_params=_cp("arbitrary"),
    )(colsum, proj, bf_pad, dproj)


def _mod_fwd(c16, w, b, *, name):
    L, D, N = w.shape
    tn = _pick(N, 512)

    def body(c_ref, w_ref, b_ref, o_ref):
        cv = c_ref[...]
        ca = (cv * _sigmoid(cv)).astype(BF)
        o_ref[...] = _dot(ca, w_ref[...].astype(BF)) + b_ref[...]

    return pl.pallas_call(
        body, name=name, grid=(L, N // tn),
        in_specs=[pl.BlockSpec((16, D), lambda l, j: (0, 0)), pl.BlockSpec((None, D, tn), lambda l, j: (l, 0, j)),
                  pl.BlockSpec((None, 1, tn), lambda l, j: (l, 0, j))],
        out_specs=pl.BlockSpec((None, 16, tn), lambda l, j: (l, 0, j)),
        out_shape=jax.ShapeDtypeStruct((L, 16, N), F32),
        compiler_params=_cp("parallel", "arbitrary"),
    )(c16, w, b)


def _mod_bwd(c16, dmod, *, name):
    L, _, N = dmod.shape
    D = c16.shape[1]
    tn = _pick(N, 512)

    def body(c_ref, d_ref, o_ref):
        cv = c_ref[...]
        ca = (cv * _sigmoid(cv)).astype(BF)
        o_ref[...] = _dg(ca, d_ref[...].astype(BF), TN)

    return pl.pallas_call(
        body, name=name, grid=(L, N // tn),
        in_specs=[pl.BlockSpec((16, D), lambda l, j: (0, 0)), pl.BlockSpec((None, 16, tn), lambda l, j: (l, 0, j))],
        out_specs=pl.BlockSpec((None, D, tn), lambda l, j: (l, 0, j)),
        out_shape=jax.ShapeDtypeStruct((L, D, N), F32),
        compiler_params=_cp("parallel", "arbitrary"),
    )(c16, dmod)


def _adamw_math(w, g, m, v):
    m = ADAM_B1 * m + (1.0 - ADAM_B1) * g
    v = ADAM_B2 * v + (1.0 - ADAM_B2) * (g * g)
    m_hat = m / (1.0 - ADAM_B1 ** ADAM_STEP)
    v_hat = v / (1.0 - ADAM_B2 ** ADAM_STEP)
    return -ADAM_LR * (m_hat / (jnp.sqrt(v_hat) + ADAM_EPS) + ADAM_WD * w), m, v


def _adamw(w, g, m, v, *, g_at=None, name):
    R, C = w.shape
    row0 = 0 if g_at is None else g_at[1]
    tr = min(math.gcd(row0, 256) if row0 else 256, -(-R // 8) * 8)
    g0 = row0 // tr
    if g_at is None:
        g_spec = pl.BlockSpec((tr, C), lambda i: (i, 0))
    else:
        g_spec = pl.BlockSpec((None, tr, C), lambda i: (g_at[0], g0 + i, 0))

    def body(w_ref, g_ref, m_ref, v_ref, d_ref, mo_ref, vo_ref):
        d, mn, vn = _adamw_math(w_ref[...], g_ref[...], m_ref[...], v_ref[...])
        d_ref[...] = d
        mo_ref[...] = mn
        vo_ref[...] = vn

    blk = pl.BlockSpec((tr, C), lambda i: (i, 0))
    return pl.pallas_call(
        body, name=name, grid=(pl.cdiv(R, tr),),
        in_specs=[blk, g_spec, blk, blk],
        out_specs=[blk, blk, blk],
        out_shape=[jax.ShapeDtypeStruct((R, C), F32)] * 3,
        compiler_params=_cp("parallel"),
    )(w, g, m, v)


def _sum_parts(parts, *, name):
    P, R, C = parts.shape

    def body(p_ref, o_ref):
        acc = p_ref[0]
        for p in range(1, P):
            acc = acc + p_ref[p]
        o_ref[...] = acc

    return pl.pallas_call(
        body, name=name, grid=(1,),
        in_specs=[pl.BlockSpec((P, R, C), lambda i: (0, 0, 0))],
        out_specs=pl.BlockSpec((R, C), lambda i: (0, 0)),
        out_shape=jax.ShapeDtypeStruct((R, C), F32),
        compiler_params=_cp("arbitrary"),
    )(parts)


def _add_halves(g4, recv, c_idx, *, name):
    _, _, Rh, C = g4.shape
    tr = _pick(Rh, 256, 16)

    def body(c_ref, a_ref, b_ref, o_ref):
        o_ref[...] = (a_ref[...] + b_ref[...].astype(F32)).astype(BF)

    return pl.pallas_call(
        body, name=name,
        grid_spec=pltpu.PrefetchScalarGridSpec(
            num_scalar_prefetch=1, grid=(4, pl.cdiv(Rh, tr)),
            in_specs=[pl.BlockSpec((None, None, tr, C), lambda j, r, c: (j, c[0], r, 0)),
                      pl.BlockSpec((None, tr, C), lambda j, r, c: (j, r, 0))],
            out_specs=pl.BlockSpec((None, tr, C), lambda j, r, c: (j, r, 0))),
        out_shape=jax.ShapeDtypeStruct((4, Rh, C), BF),
        compiler_params=_cp("parallel", "arbitrary"),
    )(c_idx, g4, recv)


def _add_four(g4, from_sibling, from_chips, pos, *, name):
    _, _, Rh, C = g4.shape
    tr = _pick(Rh, 256, 16)

    def body(p_ref, a_ref, s_ref, b_ref, o_ref):
        own = a_ref[...] + s_ref[...].astype(F32)
        o_ref[...] = ((own + b_ref[0].astype(F32)) + b_ref[1].astype(F32)) + b_ref[2].astype(F32)

    return pl.pallas_call(
        body, name=name,
        grid_spec=pltpu.PrefetchScalarGridSpec(
            num_scalar_prefetch=1, grid=(pl.cdiv(Rh, tr),),
            in_specs=[pl.BlockSpec((None, None, tr, C), lambda r, p: (p[0], p[1], r, 0)),
                      pl.BlockSpec((None, tr, C), lambda r, p: (p[0], r, 0)),
                      pl.BlockSpec((3, tr, C), lambda r, p: (0, r, 0))],
            out_specs=pl.BlockSpec((None, tr, C), lambda r, p: (p[1], r, 0))),
        out_shape=jax.ShapeDtypeStruct((2, Rh, C), F32),
        compiler_params=_cp("arbitrary"),
    )(pos, g4, from_sibling, from_chips)


HBM = pl.BlockSpec(memory_space=pltpu.HBM)


def _mesh_pos():
    return lax.axis_index("x"), lax.axis_index("y"), lax.axis_index("c")


def _other_chips(x, y):
    return [(1 - x, y), (x, 1 - y), (1 - x, 1 - y)]


def _allgather_small(xs, *, name):
    m_per, n = xs.shape

    def body(x_ref, out_ref, send_sems, recv_sems, local_sem):
        x, y, c = _mesh_pos()
        me, sibling = (x, y, c), (x, y, 1 - c)
        chips = _other_chips(x, y)

        def rows(px, py, pc):
            return out_ref.at[pl.ds((4 * px + 2 * py + pc) * m_per, m_per), :]

        def copy(k, block, to, src=None):
            return pltpu.make_async_remote_copy(
                src_ref=rows(*block) if src is None else src, dst_ref=rows(*block),
                send_sem=send_sems.at[k], recv_sem=recv_sems.at[k], device_id=to, device_id_type=MESH)

        mine = pltpu.make_async_copy(x_ref, rows(*me), local_sem)
        mine.start()
        first = [copy(0, me, sibling, src=x_ref)]
        first += [copy(1 + j, me, (*chip, c), src=x_ref) for j, chip in enumerate(chips)]
        for cp in first:
            cp.start()
        passed = [copy(4 + j, (*chip, c), sibling) for j, chip in enumerate(chips)]
        for j, chip in enumerate(chips):
            copy(1 + j, (*chip, c), me).wait_recv()
            passed[j].start()
        copy(0, sibling, me).wait_recv()
        for j, chip in enumerate(chips):
            copy(4 + j, (*chip, 1 - c), me).wait_recv()
        for cp in first + passed:
            cp.wait_send()
        mine.wait()

    return pl.pallas_call(
        body, name=name,
        out_shape=jax.ShapeDtypeStruct((N_DEV * m_per, n), xs.dtype),
        in_specs=[pl.BlockSpec(memory_space=pltpu.VMEM)],
        out_specs=pl.BlockSpec(memory_space=pltpu.VMEM),
        scratch_shapes=[pltpu.SemaphoreType.DMA((7,)), pltpu.SemaphoreType.DMA((7,)), pltpu.SemaphoreType.DMA],
    )(xs)


def _chip_slab_copies(s_ref, out_ref, send_sems, recv_sems):
    R = s_ref.shape[0]
    Rh = R // 2
    x, y, c = _mesh_pos()
    me, sibling = (x, y, c), (x, y, 1 - c)
    chips = _other_chips(x, y)

    def half(px, py, pc):
        return out_ref.at[2 * px + py, pl.ds(pc * Rh, Rh), :]

    def copy(k, block, to, src=None):
        return pltpu.make_async_remote_copy(
            src_ref=half(*block) if src is None else src, dst_ref=half(*block),
            send_sem=send_sems.at[k], recv_sem=recv_sems.at[k], device_id=to, device_id_type=MESH)

    first = [copy(j, me, (*chip, c), src=s_ref.at[pl.ds(c * Rh, Rh), :]) for j, chip in enumerate(chips)]
    passed = [copy(3 + j, (*chip, c), sibling) for j, chip in enumerate(chips)]
    landed = [copy(j, (*chip, c), me) for j, chip in enumerate(chips)]
    from_sibling = [copy(3 + j, (*chip, 1 - c), me) for j, chip in enumerate(chips)]
    return first, passed, landed, from_sibling


def _allgather_chip_slabs(slab, *, name):
    R, C = slab.shape

    def body(s_ref, out_ref, send_sems, recv_sems):
        first, passed, landed, from_sibling = _chip_slab_copies(s_ref, out_ref, send_sems, recv_sems)
        for cp in first:
            cp.start()
        for arrived, onward in zip(landed, passed):
            arrived.wait_recv()
            onward.start()
        for cp in from_sibling:
            cp.wait_recv()
        for cp in first + passed:
            cp.wait_send()

    return pl.pallas_call(
        body, name=name,
        out_shape=jax.ShapeDtypeStruct((N_CHIPS, R, C), slab.dtype),
        in_specs=[HBM], out_specs=HBM,
        scratch_shapes=[pltpu.SemaphoreType.DMA((6,)), pltpu.SemaphoreType.DMA((6,))],
    )(slab)


def _swap_halves(g4, *, name):
    _, _, Rh, C = g4.shape

    def body(g_ref, out_ref, send_sems, recv_sems):
        x, y, c = _mesh_pos()
        copies = [pltpu.make_async_remote_copy(
            src_ref=g_ref.at[j, 1 - c], dst_ref=out_ref.at[j], send_sem=send_sems.at[j], recv_sem=recv_sems.at[j],
            device_id=(x, y, 1 - c), device_id_type=MESH) for j in range(N_CHIPS)]
        for cp in copies:
            cp.start()
        for cp in copies:
            cp.wait()

    return pl.pallas_call(
        body, name=name,
        out_shape=jax.ShapeDtypeStruct((N_CHIPS, Rh, C), g4.dtype),
        in_specs=[HBM], out_specs=HBM,
        scratch_shapes=[pltpu.SemaphoreType.DMA((N_CHIPS,)), pltpu.SemaphoreType.DMA((N_CHIPS,))],
    )(g4)


def _scatter_partials(part, *, name):
    _, Rh, C = part.shape

    def body(p_ref, out_ref, send_sems, recv_sems):
        x, y, c = _mesh_pos()
        copies = [pltpu.make_async_remote_copy(
            src_ref=p_ref.at[2 * px + py], dst_ref=out_ref.at[j], send_sem=send_sems.at[j], recv_sem=recv_sems.at[j],
            device_id=(px, py, c), device_id_type=MESH) for j, (px, py) in enumerate(_other_chips(x, y))]
        for cp in copies:
            cp.start()
        for cp in copies:
            cp.wait()

    return pl.pallas_call(
        body, name=name,
        out_shape=jax.ShapeDtypeStruct((3, Rh, C), part.dtype),
        in_specs=[HBM], out_specs=HBM,
        scratch_shapes=[pltpu.SemaphoreType.DMA((3,)), pltpu.SemaphoreType.DMA((3,))],
    )(part)


def _join_halves(buf, *, name):
    def body(b_ref, out_ref, send_sem, recv_sem):
        x, y, c = _mesh_pos()
        cp = pltpu.make_async_remote_copy(
            src_ref=b_ref.at[c], dst_ref=out_ref.at[c], send_sem=send_sem, recv_sem=recv_sem,
            device_id=(x, y, 1 - c), device_id_type=MESH)
        cp.start()
        cp.wait()

    return pl.pallas_call(
        body, name=name,
        out_shape=jax.ShapeDtypeStruct(buf.shape, buf.dtype),
        in_specs=[HBM], out_specs=HBM, input_output_aliases={0: 0},
        scratch_shapes=[pltpu.SemaphoreType.DMA, pltpu.SemaphoreType.DMA],
    )(buf)


def _pad_rows(a, mult):
    pad = (-a.shape[0]) % mult
    return a if pad == 0 else jnp.pad(a, ((0, pad),) + ((0, 0),) * (a.ndim - 1))


def _local_step(x, target, mod, wts, small, slab_rest=None, unpack_rest=None):
    S, D = x.shape
    HP = D // LANES
    row = lambda v: v.reshape(1, -1)
    msplit = [[row(mod[i, k * D:(k + 1) * D]) for k in range(6)] for i in range(2)]
    gw, gs = {}, {}
    dmod = [[None] * 6 for _ in range(2)]

    sh1, sc1, g1, sh2, sc2, g2 = msplit[0]
    n1w0, n2w0 = row(small["norm1_w"][0]), row(small["norm2_w"][0])
    proj0, h1_0 = _ln_matmul(x, n1w0, sc1, sh1, wts["hg_w_in"], relu2=False, name="hg_in_proj")
    gn = small["hg_gn_w"].reshape(1, LANES)
    ypre0, o0, states, *gathered = _hg_fwd(proj0, small["hg_lb"], gn, slab_rest, name="hg_fwd")
    if slab_rest is not None:
        wts = {**wts, **unpack_rest(gathered[0])}
    x1, ymix0 = _matmul_resid(ypre0, wts["hg_w_out"], x, g1, name="hg_out_proj")
    a0, u0, h2_0 = _ln_matmul(x1, n2w0, sc2, sh2, wts["mlp_w1_0"], relu2=True, name="mlp0_up")
    x2, ymlp0 = _matmul_resid(u0, wts["mlp_w2_0"], x1, g2, name="mlp0_down")

    sh1b, sc1b, g1b, sh2b, sc2b, g2b = msplit[1]
    n1w1, n2w1 = row(small["norm1_w"][1]), row(small["norm2_w"][1])
    proj1, h1_1 = _ln_matmul(x2, n1w1, sc1b, sh1b, wts["fox_w_in"], relu2=False, name="fox_in_proj")
    nheads = 2 * HP
    bf_pad = jnp.pad(small["fox_b_f"].reshape(1, nheads), ((0, 0), (0, LANES - nheads)))
    qw2 = jnp.tile(small["fox_qn_w"].reshape(1, FOX_DH), (1, 2))
    kw2 = jnp.tile(small["fox_kn_w"].reshape(1, FOX_DH), (1, 2))
    fcum = _fox_cumsum(proj1, bf_pad, name="fox_cumsum")
    qa, ka, va, vat = _fox_prep(proj1, fcum, qw2, kw2, name="fox_prep")
    jmin, imax = _fox_skip_bounds(fcum, small["fox_qn_w"], small["fox_kn_w"], nheads)
    ypre1, o1, q2 = _fox_fwd(jmin, qa, ka, vat, proj1, name="fox_fwd")
    x3, ymix1 = _matmul_resid(ypre1, wts["fox_w_out"], x2, g1b, name="fox_out_proj")
    a1, u1, h2_1 = _ln_matmul(x3, n2w1, sc2b, sh2b, wts["mlp_w1_1"], relu2=True, name="mlp1_up")
    x4, ymlp1 = _matmul_resid(u1, wts["mlp_w2_1"], x3, g2b, name="mlp1_down")

    loss, dx4, dfw = _loss_kernel(x4, row(small["final_w"]), target, name="loss")
    gs["final_w"] = dfw.reshape(-1)

    def mlp_bwd(i, dx_out, x_in, h2, a, u, ymlp, n2w, sc2_, g2_):
        dz, dm, dg2 = _gate_matmul_nt(dx_out, g2_, ymlp, wts[f"mlp_w2_{i}"], a, name=f"mlp{i}_down_bwd")
        gw[f"mlp_w2_{i}"] = _matmul_tn(u, dm[None], name=f"mlp{i}_dw2")
        gw[f"mlp_w1_{i}"] = _matmul_tn(h2, dz[None], name=f"mlp{i}_dw1")
        dx_in, dsc, dsh, dnw = _matmul_nt_lnbwd(dz[None], wts[f"mlp_w1_{i}"], x_in, n2w, sc2_, dx_out,
                                                name=f"mlp{i}_up_bwd")
        dmod[i][3], dmod[i][4], dmod[i][5] = dsh, dsc, dg2
        return dx_in, dnw

    dx3, dn2w1 = mlp_bwd(1, dx4, x3, h2_1, a1, u1, ymlp1, n2w1, sc2b, g2b)
    dyp1, dm1, dg1b = _gate_matmul_nt(dx3, g1b, ymix1, wts["fox_w_out"], None, name="fox_out_bwd")
    gw["fox_w_out"] = _matmul_tn(ypre1, dm1[None], name="fox_dw_out")
    doa = _fox_bwd_prep(dyp1, o1, proj1, name="fox_bwd_prep")
    dqa, dka, dva, colsum = _fox_bwd(imax, q2, ka, va, doa, name="fox_bwd")
    colsum = jnp.pad(colsum[:, 0, :].T, ((0, 0), (0, LANES - nheads)))
    dproj1, dqw, dkw = _fox_bwd_post(dqa, dka, dva, proj1, dyp1, o1, qw2, kw2, name="fox_bwd_post")
    dproj1, dbf = _fox_dfz(colsum, nheads, proj1, bf_pad, dproj1, name="fox_dfz")
    gw["fox_w_in"] = _matmul_tn(h1_1, dproj1, name="fox_dw_in")
    dx2, dsc, dsh, dn1w1 = _matmul_nt_lnbwd(dproj1, wts["fox_w_in"], x2, n1w1, sc1b, dx3, name="fox_in_bwd")
    dmod[1][0], dmod[1][1], dmod[1][2] = dsh, dsc, dg1b
    gs["fox_qn_w"] = dqw[0, :FOX_DH] + dqw[0, FOX_DH:]
    gs["fox_kn_w"] = dkw[0, :FOX_DH] + dkw[0, FOX_DH:]
    gs["fox_b_f"] = dbf[0, :nheads]

    dx1, dn2w0 = mlp_bwd(0, dx2, x1, h2_0, a0, u0, ymlp0, n2w0, sc2, g2)
    dyp0, dm0, dg1 = _gate_matmul_nt(dx1, g1, ymix0, wts["hg_w_out"], None, name="hg_out_bwd")
    gw["hg_w_out"] = _matmul_tn(ypre0, dm0[None], name="hg_dw_out")
    dproj0, dlb, dgn = _hg_bwd(proj0, small["hg_lb"], gn, o0, states, dyp0, name="hg_bwd")
    gw["hg_w_in"] = _matmul_tn(h1_0, dproj0, name="hg_dw_in")
    dx0, dsc, dsh, dn1w0 = _matmul_nt_lnbwd(dproj0, wts["hg_w_in"], x, n1w0, sc1, dx1, name="hg_in_bwd")
    dmod[0][0], dmod[0][1], dmod[0][2] = dsh, dsc, dg1
    gs["hg_lb"] = dlb
    gs["hg_gn_w"] = jnp.sum(dgn, axis=0)

    gs["norm1_w"] = jnp.concatenate([dn1w0, dn1w1], axis=0)
    gs["norm2_w"] = jnp.concatenate([dn2w0, dn2w1], axis=0)
    gs["dmod"] = jnp.stack([jnp.concatenate(dmod[i], axis=1)[0] for i in range(2)])
    return loss, dx0, gw, gs


def _pack_halves(layout):
    rh = -(-max(sum(a.shape[0] for _, a in half) for half in layout) // 16) * 16
    place, parts = {}, []
    for h, half in enumerate(layout):
        off = 0
        for n, a in half:
            place[n] = (h, off, a.shape[0])
            off += a.shape[0]
        parts.append(jnp.pad(jnp.concatenate([a.astype(BF) for _, a in half], axis=0), ((0, rh - off), (0, 0))))
    return jnp.concatenate(parts, axis=0), place, rh


SMALL_NAMES = ["norm1_w", "norm2_w", "hg_lb", "hg_gn_w", "fox_b_f", "fox_qn_w", "fox_kn_w", "final_w"]


def _pack_small(d, names):
    rows, offs, r0 = [], {}, 0
    for n in names:
        flat = d[n].reshape(-1)
        nr = -(-flat.shape[0] // LANES)
        rows.append(jnp.pad(flat, (0, nr * LANES - flat.shape[0])).reshape(nr, LANES))
        offs[n] = (r0, nr)
        r0 += nr
    return jnp.concatenate(rows, axis=0), offs


def _unpack_small(packed, offs, name, like):
    r0, nr = offs[name]
    return packed[r0:r0 + nr].reshape(-1)[:like.size].reshape(like.shape)


def kernel(x, c, w_mod, b_mod, norm1_w, norm2_w, hg_w_in, hg_w_out, hg_lb, hg_gn_w, fox_w_in, fox_b_f, fox_qn_w, fox_kn_w, fox_w_out, mlp_w1, mlp_w2, final_w, loss_target, m_w_mod, m_b_mod, m_norm1_w, m_norm2_w, m_hg_w_in, m_hg_w_out, m_hg_lb, m_hg_gn_w, m_fox_w_in, m_fox_b_f, m_fox_qn_w, m_fox_kn_w, m_fox_w_out, m_mlp_w1, m_mlp_w2, m_final_w, v_w_mod, v_b_mod, v_norm1_w, v_norm2_w, v_hg_w_in, v_hg_w_out, v_hg_lb, v_hg_gn_w, v_fox_w_in, v_fox_b_f, v_fox_qn_w, v_fox_kn_w, v_fox_w_out, v_mlp_w1, v_mlp_w2, v_final_w):
    S, D = x.shape[1], x.shape[2]
    nheads = D // FOX_DH
    ax, ay, ac = _mesh_pos()
    chip = 2 * ax + ay
    dev = 2 * chip + ac
    xs, tgt = x.reshape(S, D), loss_target.reshape(S, D)

    c_all = _allgather_small(_pad_rows(c.reshape(-1, LANES), 8), name="gather_c")
    c_all = c_all.reshape(N_DEV, -1)[:, :D]
    c16 = _pad_rows(c_all, 16)
    nmod = w_mod.shape[2]
    b_shard = lax.dynamic_slice_in_dim(b_mod, chip * nmod, nmod, axis=1)
    mod_shard = _mod_fwd(c16, w_mod, b_shard[:, None, :], name="mod_fwd")[:, :N_DEV]
    mod_all = _allgather_small(mod_shard.reshape(-1, LANES), name="gather_mod")
    mod_all = mod_all.reshape(N_CHIPS, 2, 2, N_DEV, nmod)[:, 0]
    mod = lax.dynamic_index_in_dim(mod_all, dev, axis=2, keepdims=False)
    mod = mod.transpose(1, 0, 2).reshape(2, N_CHIPS * nmod)

    fox_rows = fox_w_in.shape[2]
    col = lambda g: g.transpose(1, 0, 2).reshape(g.shape[1], -1)
    rowsh = lambda g: g.reshape(-1, g.shape[2])
    own = lambda g, s: lax.dynamic_update_index_in_dim(g, s, chip, 0)

    slab_in = hg_w_in[0].astype(BF)
    wts = {"hg_w_in": col(own(_allgather_chip_slabs(slab_in, name="gather_hg_w_in"), slab_in))}
    slab_rest, place_rest, rh_rest = _pack_halves(
        [[("mlp_w1", mlp_w1.reshape(2 * D, D)), ("hg_w_out", hg_w_out[0]), ("fox_w_out", fox_w_out[0])],
         [("mlp_w2", mlp_w2.reshape(2 * D, D)), ("fox_w_in", fox_w_in[0].reshape(fox_rows, D))]])

    def unpack_rest(gathered):
        gathered = own(gathered, slab_rest)

        def seg(n):
            h, off, rows = place_rest[n]
            return gathered[:, h * rh_rest + off:h * rh_rest + off + rows, :]

        w1 = seg("mlp_w1").reshape(N_CHIPS, 2, D, D)
        w2 = seg("mlp_w2").reshape(N_CHIPS, 2, D, D)
        fox_in = col(seg("fox_w_in").reshape(N_CHIPS, D, fox_rows))
        return {
            "hg_w_out": rowsh(seg("hg_w_out")), "fox_w_out": rowsh(seg("fox_w_out")),
            "mlp_w1_0": col(w1[:, 0]), "mlp_w1_1": col(w1[:, 1]), "mlp_w2_0": rowsh(w2[:, 0]), "mlp_w2_1": rowsh(w2[:, 1]),
            "fox_w_in": jnp.pad(fox_in, ((0, 0), (0, 5 * D - fox_in.shape[1]))),
        }

    small = {"norm1_w": norm1_w, "norm2_w": norm2_w, "hg_lb": hg_lb, "hg_gn_w": hg_gn_w, "fox_b_f": fox_b_f,
             "fox_qn_w": fox_qn_w, "fox_kn_w": fox_kn_w, "final_w": final_w}

    loss_part, grad_x, gw, gs = _local_step(xs, tgt, mod, wts, small, slab_rest, unpack_rest)

    layout = [[("hg_w_in", hg_w_in[0]), ("mlp_w1", mlp_w1.reshape(2 * D, D)), ("hg_w_out", hg_w_out[0])],
              [("mlp_w2", mlp_w2.reshape(2 * D, D)), ("fox_w_out", fox_w_out[0]),
               ("fox_w_in", fox_w_in[0].reshape(fox_rows, D))]]
    Rh = -(-max(sum(a.shape[0] for _, a in half) for half in layout) // 16) * 16
    place = {}
    for h, half in enumerate(layout):
        off = 0
        for n, a in half:
            place[n] = (h, off, a.shape[0])
            off += a.shape[0]
    loss = lax.psum(loss_part[0, 0], ("x", "y", "c"))

    def uncol(g, n):
        return g.reshape(g.shape[0], N_CHIPS, n).transpose(1, 0, 2)

    gseg = {
        "hg_w_in": uncol(gw["hg_w_in"], D), "hg_w_out": gw["hg_w_out"].reshape(N_CHIPS, D // 4, D),
        "fox_w_out": gw["fox_w_out"].reshape(N_CHIPS, D // 4, D),
        "mlp_w1": jnp.concatenate([uncol(gw["mlp_w1_0"], D), uncol(gw["mlp_w1_1"], D)], axis=1),
        "mlp_w2": jnp.concatenate([gw["mlp_w2_0"].reshape(N_CHIPS, D, D), gw["mlp_w2_1"].reshape(N_CHIPS, D, D)], axis=1),
        "fox_w_in": uncol(gw["fox_w_in"][:, :4 * fox_rows], fox_rows).reshape(N_CHIPS, fox_rows, D),
    }
    ghalves = []
    for half in layout:
        gh = jnp.concatenate([gseg[n] for n, _ in half], axis=1)
        ghalves.append(jnp.pad(gh, ((0, 0), (0, Rh - gh.shape[1]), (0, 0))))
    g4 = jnp.stack(ghalves, axis=1)
    from_sibling = _swap_halves(g4.astype(BF), name="rs_swap_halves")
    chip_part = _add_halves(g4, from_sibling, ac.reshape(1), name="rs_add_halves")
    from_chips = _scatter_partials(chip_part, name="rs_scatter")
    my_half = _add_four(g4, from_sibling, from_chips, jnp.stack([chip, ac]), name="rs_add_chips")
    gshard = _join_halves(my_half, name="rs_join")

    names = ["dmod"] + SMALL_NAMES
    packed, offs = _pack_small(gs, names)
    packed = _pad_rows(packed, 8)
    rp = packed.shape[0]
    parts = _allgather_small(packed, name="gather_small").reshape(N_DEV, rp, LANES)
    total = _sum_parts(parts, name="sum_small")
    r0, nr = offs["dmod"]
    dmod_all = parts[:, r0:r0 + nr].reshape(N_DEV, 2, N_CHIPS * nmod)
    dmod_shard = lax.dynamic_slice_in_dim(dmod_all, chip * nmod, nmod, axis=2).transpose(1, 0, 2)
    g_w_mod = _mod_bwd(c16, jnp.pad(dmod_shard, ((0, 0), (0, 16 - N_DEV), (0, 0))), name="mod_bwd")

    grads = {"w_mod": g_w_mod, "b_mod": _unpack_small(total, offs, "dmod", b_mod)}
    for n in SMALL_NAMES:
        grads[n] = _unpack_small(total, offs, n, small[n])

    given = dict(w_mod=(w_mod, m_w_mod, v_w_mod), b_mod=(b_mod, m_b_mod, v_b_mod), norm1_w=(norm1_w, m_norm1_w, v_norm1_w),
                 norm2_w=(norm2_w, m_norm2_w, v_norm2_w), hg_w_in=(hg_w_in, m_hg_w_in, v_hg_w_in),
                 hg_w_out=(hg_w_out, m_hg_w_out, v_hg_w_out), hg_lb=(hg_lb, m_hg_lb, v_hg_lb),
                 hg_gn_w=(hg_gn_w, m_hg_gn_w, v_hg_gn_w), fox_w_in=(fox_w_in, m_fox_w_in, v_fox_w_in),
                 fox_b_f=(fox_b_f, m_fox_b_f, v_fox_b_f), fox_qn_w=(fox_qn_w, m_fox_qn_w, v_fox_qn_w),
                 fox_kn_w=(fox_kn_w, m_fox_kn_w, v_fox_kn_w), fox_w_out=(fox_w_out, m_fox_w_out, v_fox_w_out),
                 mlp_w1=(mlp_w1, m_mlp_w1, v_mlp_w1), mlp_w2=(mlp_w2, m_mlp_w2, v_mlp_w2), final_w=(final_w, m_final_w, v_final_w))
    upd = {}

    for n, (h, off, rows) in place.items():
        w, m, v = given[n]
        flat = lambda a: a.reshape(rows, D)
        d, mn, vn = _adamw(flat(w), gshard, flat(m), flat(v), g_at=(h, off), name=f"adamw_{n}")
        grads[n] = gshard[h, off:off + rows].reshape(w.shape)
        upd[n] = tuple(a.reshape(w.shape) for a in (d, mn, vn))

    w, m, v = given["w_mod"]
    flat = lambda a: a.reshape(-1, nmod)
    upd["w_mod"] = tuple(a.reshape(w.shape) for a in _adamw(flat(w), flat(g_w_mod), flat(m), flat(v), name="adamw_w_mod"))

    snames = ["b_mod"] + SMALL_NAMES
    pw, soffs = _pack_small({n: given[n][0] for n in snames}, snames)
    pm, _ = _pack_small({n: given[n][1] for n in snames}, snames)
    pv, _ = _pack_small({n: given[n][2] for n in snames}, snames)
    pg, _ = _pack_small({n: grads[n] for n in snames}, snames)
    pw, pm, pv, pg = (_pad_rows(a, 8) for a in (pw, pm, pv, pg))
    sd, smn, svn = _adamw(pw, pg, pm, pv, name="adamw_small")
    for n in snames:
        like = given[n][0]
        upd[n] = tuple(_unpack_small(a, soffs, n, like) for a in (sd, smn, svn))

    order = ["w_mod", "b_mod", "norm1_w", "norm2_w", "hg_w_in", "hg_w_out", "hg_lb", "hg_gn_w", "fox_w_in", "fox_b_f",
             "fox_qn_w", "fox_kn_w", "fox_w_out", "mlp_w1", "mlp_w2", "final_w"]
    return (loss, grad_x.reshape(x.shape), *[grads[n] for n in order], *[upd[n][0] for n in order],
            *[upd[n][1] for n in order], *[upd[n][2] for n in order])
```

```python
import math

import jax
import jax.numpy as jnp
from jax import lax
from jax.experimental import pallas as pl
from jax.experimental.pallas import tpu as pltpu

EPS = 1e-6
ADAM_LR, ADAM_B1, ADAM_B2, ADAM_EPS, ADAM_WD, ADAM_STEP = 0.001, 0.9, 0.999, 1e-08, 0.01, 10

F32 = jnp.float32
BF = jnp.bfloat16
LANES = 128
HG_CHUNK = 64
HG_HEADS_PER_STEP = 8
HG_TOKENS_PER_STEP = 256
FOX_BWD_TILES = (8, 4, 2, 1)
LOG2E = 1.4426950408889634
FOX_DH = 64
N_CHIPS = 4
N_DEV = 8
VMEM_LIMIT = 48 * 1024 * 1024
MESH = pl.DeviceIdType.MESH

NT = (((1,), (1,)), ((), ()))
TN = (((0,), (0,)), ((), ()))


def _pick(n, pref, mult=LANES):
    if n <= pref:
        return n
    t = (pref // mult) * mult
    while t >= mult:
        if n % t == 0:
            return t
        t -= mult
    raise ValueError((n, pref, mult))


def _cp(*sem):
    return pltpu.CompilerParams(dimension_semantics=sem, vmem_limit_bytes=VMEM_LIMIT)


def _dot(a, b):
    return jnp.dot(a, b, preferred_element_type=F32)


def _dg(a, b, dims):
    return lax.dot_general(a, b, dims, preferred_element_type=F32)


def _split3(x):
    hi = x.astype(BF)
    r1 = x - hi.astype(F32)
    mid = r1.astype(BF)
    lo = (r1 - mid.astype(F32)).astype(BF)
    return hi, mid, lo


def _tri_dot(tri, x):
    hi, mid, lo = _split3(x)
    return _dot(tri, hi) + _dot(tri, mid) + _dot(tri, lo)


def _dg3(a, b, dims):
    ah, bh = a.astype(BF), b.astype(BF)
    al, bl = (a - ah.astype(F32)).astype(BF), (b - bh.astype(F32)).astype(BF)
    return _dg(ah, bh, dims) + _dg(ah, bl, dims) + _dg(al, bh, dims)


def _dg1(a, b, dims):
    return _dg(a.astype(BF), b.astype(BF), dims)


NN = (((1,), (0,)), ((), ()))


def _sigmoid(x):
    return jax.nn.sigmoid(x)


def _ln_matmul(x, nw, sc, sh, w, *, relu2, name):
    S, D = x.shape
    N = w.shape[1]
    tm, tn = _pick(S, 1024, 16), _pick(N, 1024)

    def body(x_ref, nw_ref, sc_ref, sh_ref, w_ref, *rest):
        outs, hs = rest[:-1], rest[-1]
        h_ref = outs[-1]

        @pl.when(pl.program_id(1) == 0)
        def _():
            xv = x_ref[...]
            r = lax.rsqrt(jnp.mean(xv * xv, axis=-1, keepdims=True) + EPS)
            hb = ((xv * r * nw_ref[...]) * (1.0 + sc_ref[...]) + sh_ref[...]).astype(BF)
            hs[...] = hb
            h_ref[...] = hb

        z = _dot(hs[...], w_ref[...])
        if relu2:
            a = jnp.maximum(z, 0.0)
            outs[0][...] = a.astype(BF)
            outs[1][...] = (a * a).astype(BF)
        else:
            outs[0][...] = z

    vec = pl.BlockSpec((1, D), lambda i, j: (0, 0))
    tile = pl.BlockSpec((tm, tn), lambda i, j: (i, j))
    if relu2:
        out_shape = [jax.ShapeDtypeStruct((S, N), BF), jax.ShapeDtypeStruct((S, N), BF)]
        out_specs = [tile, tile]
    else:
        out_shape = [jax.ShapeDtypeStruct((S, N), F32)]
        out_specs = [tile]
    out_shape.append(jax.ShapeDtypeStruct((S, D), BF))
    out_specs.append(pl.BlockSpec((tm, D), lambda i, j: (i, 0)))
    return pl.pallas_call(
        body, name=name, grid=(S // tm, N // tn),
        in_specs=[pl.BlockSpec((tm, D), lambda i, j: (i, 0)), vec, vec, vec,
                  pl.BlockSpec((D, tn), lambda i, j: (0, j))],
        out_specs=out_specs, out_shape=out_shape,
        scratch_shapes=[pltpu.VMEM((tm, D), BF)],
        compiler_params=_cp("parallel", "arbitrary"),
    )(x, nw, sc, sh, w)


def _matmul_resid(a, w, x, gate, *, name):
    S, K = a.shape
    D = w.shape[1]
    big = 1024 if K <= 1024 else 512
    tm, tn = _pick(S, big, 16), _pick(D, big)

    def body(a_ref, w_ref, x_ref, g_ref, o_ref, y_ref):
        y = _dot(a_ref[...], w_ref[...])
        y_ref[...] = y.astype(BF)
        o_ref[...] = x_ref[...] + g_ref[...] * y

    tile = pl.BlockSpec((tm, tn), lambda i, j: (i, j))
    return pl.pallas_call(
        body, name=name, grid=(S // tm, D // tn),
        in_specs=[pl.BlockSpec((tm, K), lambda i, j: (i, 0)), pl.BlockSpec((K, tn), lambda i, j: (0, j)),
                  tile, pl.BlockSpec((1, tn), lambda i, j: (0, j))],
        out_specs=[tile, tile],
        out_shape=[jax.ShapeDtypeStruct((S, D), F32), jax.ShapeDtypeStruct((S, D), BF)],
        compiler_params=_cp("parallel", "arbitrary"),
    )(a, w, x, gate)


def _gate_matmul_nt(dx, gate, y, w, act, *, name):
    S, D = dx.shape
    K = w.shape[0]
    tm, tn = _pick(S, 1024, 16), _pick(K, 1024)
    fused = act is not None

    def body(dx_ref, g_ref, y_ref, w_ref, *rest):
        if fused:
            act_ref, da_ref, dm_ref, dg_ref, ms = rest
        else:
            da_ref, dm_ref, dg_ref, ms = rest
        i, j = pl.program_id(0), pl.program_id(1)

        @pl.when((i == 0) & (j == 0))
        def _():
            dg_ref[...] = jnp.zeros_like(dg_ref)

        @pl.when(j == 0)
        def _():
            dxv = dx_ref[...]
            dmb = (dxv * g_ref[...]).astype(BF)
            ms[...] = dmb
            dm_ref[...] = dmb
            dg_ref[...] += jnp.sum(dxv * y_ref[...].astype(F32), axis=0, keepdims=True)

        da = _dg(ms[...], w_ref[...], NT)
        if fused:
            da_ref[...] = (da * (2.0 * act_ref[...].astype(F32))).astype(BF)
        else:
            da_ref[...] = da

    row = pl.BlockSpec((tm, D), lambda i, j: (i, 0))
    vec = pl.BlockSpec((1, D), lambda i, j: (0, 0))
    tile = pl.BlockSpec((tm, tn), lambda i, j: (i, j))
    in_specs = [row, vec, row, pl.BlockSpec((tn, D), lambda i, j: (j, 0))]
    args = [dx, gate, y, w]
    if fused:
        in_specs.append(tile)
        args.append(act)
    return pl.pallas_call(
        body, name=name, grid=(S // tm, K // tn),
        in_specs=in_specs, out_specs=[tile, row, vec],
        out_shape=[jax.ShapeDtypeStruct((S, K), BF if fused else F32), jax.ShapeDtypeStruct((S, D), BF),
                   jax.ShapeDtypeStruct((1, D), F32)],
        scratch_shapes=[pltpu.VMEM((tm, D), BF)],
        compiler_params=_cp("arbitrary", "arbitrary"),
    )(*args)


def _matmul_tn(a, b, *, name):
    S, Ka = a.shape
    P, _, Db = b.shape
    tk, tn, ts = _pick(Ka, 1024), _pick(Db, 1024), _pick(S, 1024, 16)
    npb = Db // tn

    def body(a_ref, b_ref, o_ref, acc):
        s = pl.program_id(2)

        @pl.when(s == 0)
        def _():
            acc[...] = jnp.zeros_like(acc)

        acc[...] += _dg(a_ref[...], b_ref[...], TN)

        @pl.when(s == pl.num_programs(2) - 1)
        def _():
            o_ref[...] = acc[...]

    return pl.pallas_call(
        body, name=name, grid=(Ka // tk, P * npb, S // ts),
        in_specs=[pl.BlockSpec((ts, tk), lambda i, j, s: (s, i)),
                  pl.BlockSpec((None, ts, tn), lambda i, j, s: (j // npb, s, j % npb))],
        out_specs=pl.BlockSpec((tk, tn), lambda i, j, s: (i, j)),
        out_shape=jax.ShapeDtypeStruct((Ka, P * Db), F32),
        scratch_shapes=[pltpu.VMEM((tk, tn), F32)],
        compiler_params=_cp("parallel", "parallel", "arbitrary"),
    )(a, b)


def _matmul_nt_lnbwd(g, w, x, nw, sc, dx_out, *, name):
    P, S, Dg = g.shape
    D = x.shape[1]
    tm, tk = _pick(S, 1024, 16), _pick(Dg, 1024)
    npb = Dg // tk
    nk = P * npb

    def body(g_ref, w_ref, x_ref, nw_ref, sc_ref, dxo_ref, dx_ref, dsc_ref, dsh_ref, dnw_ref, acc):
        i, k = pl.program_id(0), pl.program_id(1)

        @pl.when((i == 0) & (k == 0))
        def _():
            dsc_ref[...] = jnp.zeros_like(dsc_ref)
            dsh_ref[...] = jnp.zeros_like(dsh_ref)
            dnw_ref[...] = jnp.zeros_like(dnw_ref)

        @pl.when(k == 0)
        def _():
            acc[...] = jnp.zeros_like(acc)

        acc[...] += _dg(g_ref[...], w_ref[...], NT)

        @pl.when(k == nk - 1)
        def _():
            dh = acc[...]
            xv = x_ref[...]
            nwv = nw_ref[...]
            r = lax.rsqrt(jnp.mean(xv * xv, axis=-1, keepdims=True) + EPS)
            xr = xv * r
            dn = dh * (1.0 + sc_ref[...])
            dsc_ref[...] += jnp.sum(dh * (xr * nwv), axis=0, keepdims=True)
            dsh_ref[...] += jnp.sum(dh, axis=0, keepdims=True)
            dnw_ref[...] += jnp.sum(dn * xr, axis=0, keepdims=True)
            u = dn * nwv
            dx_ref[...] = dxo_ref[...] + r * (u - xr * jnp.mean(u * xr, axis=-1, keepdims=True))

    row = pl.BlockSpec((tm, D), lambda i, k: (i, 0))
    vec = pl.BlockSpec((1, D), lambda i, k: (0, 0))
    return pl.pallas_call(
        body, name=name, grid=(S // tm, nk),
        in_specs=[pl.BlockSpec((None, tm, tk), lambda i, k: (k // npb, i, k % npb)),
                  pl.BlockSpec((D, tk), lambda i, k: (0, k)), row, vec, vec, row],
        out_specs=[row, vec, vec, vec],
        out_shape=[jax.ShapeDtypeStruct((S, D), F32)] + [jax.ShapeDtypeStruct((1, D), F32)] * 3,
        scratch_shapes=[pltpu.VMEM((tm, D), F32)],
        compiler_params=_cp("arbitrary", "arbitrary"),
    )(g, w, x, nw, sc, dx_out)


def _loss_kernel(x, fw, tgt, *, name):
    S, D = x.shape
    tm = _pick(S, 512, 8)

    def body(x_ref, fw_ref, t_ref, l_ref, dx_ref, dfw_ref):
        @pl.when(pl.program_id(0) == 0)
        def _():
            l_ref[...] = jnp.zeros_like(l_ref)
            dfw_ref[...] = jnp.zeros_like(dfw_ref)

        xv = x_ref[...]
        fwv = fw_ref[...]
        r = lax.rsqrt(jnp.mean(xv * xv, axis=-1, keepdims=True) + EPS)
        xr = xv * r
        err = xr * fwv - t_ref[...]
        per_tok = jnp.mean(err * err, axis=-1, keepdims=True)
        l_ref[...] += 0.5 * jnp.sum(per_tok, axis=0, keepdims=True)
        dy = err * (1.0 / D)
        dfw_ref[...] += jnp.sum(dy * xr, axis=0, keepdims=True)
        u = dy * fwv
        dx_ref[...] = r * (u - xr * jnp.mean(u * xr, axis=-1, keepdims=True))

    row = pl.BlockSpec((tm, D), lambda i: (i, 0))
    vec = pl.BlockSpec((1, D), lambda i: (0, 0))
    return pl.pallas_call(
        body, name=name, grid=(S // tm,),
        in_specs=[row, vec, row],
        out_specs=[pl.BlockSpec((1, LANES), lambda i: (0, 0)), row, vec],
        out_shape=[jax.ShapeDtypeStruct((1, LANES), F32), jax.ShapeDtypeStruct((S, D), F32),
                   jax.ShapeDtypeStruct((1, D), F32)],
        compiler_params=_cp("arbitrary"),
    )(x, fw, tgt)


def _hg_lower_bound(lb3):
    mx = jnp.max(lb3, axis=0, keepdims=True)
    e = jnp.exp(lb3 - mx)
    p = e / jnp.sum(e, axis=0, keepdims=True)
    return p[0:1, :], p


def _hg_chunk_common(qr, fz, lbv):
    sq = _sigmoid(qr)
    q = qr * sq
    sig = _sigmoid(fz)
    f = lbv + (1.0 - lbv) * sig
    k = (1.0 - lbv) * (1.0 - sig)
    return q, sq, sig, f, k, jnp.log(f)


def _row_of(x, rows, r):
    return jnp.sum(jnp.where(rows == r, x, 0.0), axis=0, keepdims=True)


def _hg_fwd(proj, hg_lb, gn, slab=None, *, name):
    S = proj.shape[0]
    D = proj.shape[1] // 4
    H = D // LANES
    HB = min(HG_HEADS_PER_STEP, H)
    W = HB * LANES
    C = HG_CHUNK
    T = _pick(S, HG_TOKENS_PER_STEP, C)
    nch, nb = T // C, S // T
    ng = H // HB
    fused = slab is not None

    def body(q_ref, fz_ref, v_ref, g_ref, lb_ref, gn_ref, *rest):
        if fused:
            s_ref, y_ref, o_ref, sts_ref, out_ref, st, send_sems, recv_sems = rest
            first, passed, landed, from_sibling = _chip_slab_copies(s_ref, out_ref, send_sems, recv_sems)
            hgrp, n = pl.program_id(0), pl.program_id(1)

            @pl.when((hgrp == 0) & (n == 0))
            def _():
                for cp in first:
                    cp.start()

            @pl.when((hgrp == ng - 1) & (n == (3 * nb) // 4))
            def _():
                for arrived, onward in zip(landed, passed):
                    arrived.wait_recv()
                    onward.start()
        else:
            y_ref, o_ref, sts_ref, st = rest

        @pl.when(pl.program_id(1) == 0)
        def _():
            st[...] = jnp.zeros_like(st)

        lb_all, _ = _hg_lower_bound(lb_ref[...])
        gnv = gn_ref[...]
        ri = lax.broadcasted_iota(jnp.int32, (C, C), 0)
        ci_ = lax.broadcasted_iota(jnp.int32, (C, C), 1)
        low = ri >= ci_
        tri = jnp.where(low, 1.0, 0.0).astype(BF)
        rows = lax.broadcasted_iota(jnp.int32, (C, LANES), 0)

        def chunk(ci, carry):
            sl = pl.ds(pl.multiple_of(ci * C, C), C)
            for hh in range(HB):
                ls = slice(hh * LANES, (hh + 1) * LANES)
                q, _, _, _, k, logf = _hg_chunk_common(q_ref[sl, ls], fz_ref[sl, ls], lb_all[:, ls])
                vv = v_ref[sl, ls]
                gg = g_ref[sl, ls]
                G = _tri_dot(tri, logf)
                Gm = _row_of(G, rows, C // 2 - 1)
                Gl = _row_of(G, rows, C - 1)
                qt = q * jnp.exp(G - Gm)
                kt = k * jnp.exp(Gm - G)
                A = jnp.where(low, _dg1(qt, kt, NT), 0.0)
                Sv = st[hh]
                sts_ref[hh, ci] = Sv
                o = _dg1(A, vv, NN) + _dg1(q * jnp.exp(G), Sv, NT)
                st[hh] = Sv * jnp.exp(Gl) + _dg1(vv, k * jnp.exp(Gl - G), TN)
                r = lax.rsqrt(jnp.mean(o * o, axis=-1, keepdims=True) + EPS)
                y_ref[sl, ls] = ((o * r * gnv) * (gg * _sigmoid(gg))).astype(BF)
                o_ref[sl, ls] = o
            return carry

        lax.fori_loop(0, nch, chunk, 0)

        if fused:
            @pl.when((hgrp == ng - 1) & (n == nb - 1))
            def _():
                for cp in from_sibling:
                    cp.wait_recv()
                for cp in first + passed:
                    cp.wait_send()

    def part(p):
        return pl.BlockSpec((T, W), lambda h, n: (n, p * ng + h))

    blk = pl.BlockSpec((T, W), lambda h, n: (n, h))
    in_specs = [part(0), part(1), part(2), part(3),
                pl.BlockSpec((3, W), lambda h, n: (0, h)), pl.BlockSpec((1, LANES), lambda h, n: (0, 0))]
    out_specs = [blk, blk, pl.BlockSpec((HB, nch, LANES, LANES), lambda h, n: (h, n, 0, 0))]
    out_shape = [jax.ShapeDtypeStruct((S, D), BF), jax.ShapeDtypeStruct((S, D), F32),
                 jax.ShapeDtypeStruct((H, S // C, LANES, LANES), F32)]
    scratch = [pltpu.VMEM((HB, LANES, LANES), F32)]
    args = [proj, proj, proj, proj, hg_lb, gn]
    if fused:
        in_specs.append(HBM)
        out_specs.append(HBM)
        out_shape.append(jax.ShapeDtypeStruct((N_CHIPS,) + slab.shape, slab.dtype))
        scratch += [pltpu.SemaphoreType.DMA((6,)), pltpu.SemaphoreType.DMA((6,))]
        args.append(slab)
    return pl.pallas_call(
        body, name=name, grid=(ng, nb), in_specs=in_specs, out_specs=out_specs, out_shape=out_shape,
        scratch_shapes=scratch, compiler_params=_cp("arbitrary", "arbitrary"),
    )(*args)


def _hg_bwd(proj, hg_lb, gn, o_all, states, dy, *, name):
    S = proj.shape[0]
    D = proj.shape[1] // 4
    H = D // LANES
    HB = min(HG_HEADS_PER_STEP, H)
    W = HB * LANES
    C = HG_CHUNK
    T = _pick(S, HG_TOKENS_PER_STEP, C)
    nch, nb = T // C, S // T

    def body(q_ref, fz_ref, v_ref, g_ref, lb_ref, gn_ref, o_ref, sts_ref, dy_ref,
             dp_ref, dlb_ref, dgn_ref, dst, dlb_acc):
        n = pl.program_id(1)

        @pl.when(n == 0)
        def _():
            dst[...] = jnp.zeros_like(dst)
            dlb_acc[...] = jnp.zeros_like(dlb_acc)
            dgn_ref[...] = jnp.zeros_like(dgn_ref)

        lb_all, p3 = _hg_lower_bound(lb_ref[...])
        gnv = gn_ref[...]
        ri = lax.broadcasted_iota(jnp.int32, (C, C), 0)
        ci_ = lax.broadcasted_iota(jnp.int32, (C, C), 1)
        low = ri >= ci_
        tri = jnp.where(low, 1.0, 0.0).astype(BF)
        triu = jnp.where(ri <= ci_, 1.0, 0.0).astype(BF)
        rows = lax.broadcasted_iota(jnp.int32, (C, LANES), 0)

        def chunk(cj, carry):
            ci = nch - 1 - cj
            sl = pl.ds(pl.multiple_of(ci * C, C), C)
            for hh in range(HB):
                ls = slice(hh * LANES, (hh + 1) * LANES)
                lbv = lb_all[:, ls]
                qr = q_ref[sl, ls]
                q, sq, sig, f, k, logf = _hg_chunk_common(qr, fz_ref[sl, ls], lbv)
                vv = v_ref[sl, ls]
                gg = g_ref[sl, ls]
                o = o_ref[sl, ls]
                dyv = dy_ref[sl, ls]
                G = _tri_dot(tri, logf)
                Gm = _row_of(G, rows, C // 2 - 1)
                Gl = _row_of(G, rows, C - 1)
                eG, e_qm, e_km, e_lk, eGl = jnp.exp(G), jnp.exp(G - Gm), jnp.exp(Gm - G), jnp.exp(Gl - G), jnp.exp(Gl)
                qt = q * e_qm
                kt = k * e_km
                A = jnp.where(low, _dg1(qt, kt, NT), 0.0)
                sg = _sigmoid(gg)
                r = lax.rsqrt(jnp.mean(o * o, axis=-1, keepdims=True) + EPS)
                on = o * r
                d_onw = dyv * (gg * sg)
                dgn_ref[hh] += jnp.sum(d_onw * on, axis=0, keepdims=True)
                dgg = dyv * (on * gnv) * (sg * (1.0 + gg * (1.0 - sg)))
                u = d_onw * gnv
                do = r * (u - on * jnp.mean(u * on, axis=-1, keepdims=True))
                Sv = sts_ref[hh, ci]
                dSv = dst[hh]
                dA = jnp.where(low, _dg3(do, vv, NT), 0.0)
                kdec = k * e_lk
                dv = _dg1(A, do, TN) + _dg1(kdec, dSv, NT)
                dq = _dg3(dA, kt, NN) * e_qm + eG * _dg3(do, Sv, NN)
                dk = _dg3(dA, qt, TN) * e_km + e_lk * _dg3(vv, dSv, NN)
                s_end = Sv * eGl + _dg3(vv, kdec, TN)
                dgl = jnp.sum(dSv * s_end, axis=0, keepdims=True)
                dG = q * dq - k * dk + jnp.where(rows == C - 1, dgl, 0.0)
                dlogf = _tri_dot(triu, dG) - f * dk
                dst[hh] = dSv * eGl + _dg1(do, q * eG, TN)
                dlf_f = dlogf / f
                dlb_acc[:, ls] += jnp.sum(dlf_f * (1.0 - sig), axis=0, keepdims=True)
                dp_ref[0, sl, ls] = (dq * (sq * (1.0 + qr * (1.0 - sq)))).astype(BF)
                dp_ref[1, sl, ls] = (dlf_f * (1.0 - lbv) * sig * (1.0 - sig)).astype(BF)
                dp_ref[2, sl, ls] = dv.astype(BF)
                dp_ref[3, sl, ls] = dgg.astype(BF)
            return carry

        lax.fori_loop(0, nch, chunk, 0)
        sel = jnp.where(lax.broadcasted_iota(jnp.int32, (3, W), 0) == 0, 1.0, 0.0)
        dlb_ref[...] = lb_all * (sel - p3) * dlb_acc[...]

    ng = H // HB

    def part(p):
        return pl.BlockSpec((T, W), lambda h, n: (nb - 1 - n, p * ng + h))

    blk = pl.BlockSpec((T, W), lambda h, n: (nb - 1 - n, h))
    return pl.pallas_call(
        body, name=name, grid=(ng, nb),
        in_specs=[part(0), part(1), part(2), part(3),
                  pl.BlockSpec((3, W), lambda h, n: (0, h)), pl.BlockSpec((1, LANES), lambda h, n: (0, 0)),
                  blk, pl.BlockSpec((HB, nch, LANES, LANES), lambda h, n: (h, nb - 1 - n, 0, 0)), blk],
        out_specs=[pl.BlockSpec((4, T, W), lambda h, n: (0, nb - 1 - n, h)),
                   pl.BlockSpec((3, W), lambda h, n: (0, h)),
                   pl.BlockSpec((HB, 1, LANES), lambda h, n: (h, 0, 0))],
        out_shape=[jax.ShapeDtypeStruct((4, S, D), BF), jax.ShapeDtypeStruct((3, D), F32),
                   jax.ShapeDtypeStruct((H, 1, LANES), F32)],
        scratch_shapes=[pltpu.VMEM((HB, LANES, LANES), F32), pltpu.VMEM((1, W), F32)],
        compiler_params=_cp("parallel", "arbitrary"),
    )(proj, proj, proj, proj, hg_lb, gn, o_all, states, dy)


def _log_sigmoid(u):
    return jnp.minimum(u, 0.0) - jnp.log(1.0 + jnp.exp(-jnp.abs(u)))


def _lane_put(base, lane, first, pieces):
    for n, p in enumerate(pieces):
        base = jnp.where(lane == first + n, p, base)
    return base


def _fox_cumsum(proj, bf_pad, *, name):
    S = proj.shape[0]
    D = proj.shape[1] // 5
    T = _pick(S, 256, 8)

    def body(fz_ref, b_ref, f_ref, carry):
        @pl.when(pl.program_id(0) == 0)
        def _():
            carry[...] = jnp.zeros_like(carry)

        logf = _log_sigmoid(fz_ref[...] + b_ref[...])
        tri = jnp.where(lax.broadcasted_iota(jnp.int32, (T, T), 0) >= lax.broadcasted_iota(jnp.int32, (T, T), 1),
                        1.0, 0.0).astype(BF)
        fv = _tri_dot(tri, logf) + carry[...]
        f_ref[...] = fv
        carry[...] = _row_of(fv, lax.broadcasted_iota(jnp.int32, (T, LANES), 0), T - 1)

    return pl.pallas_call(
        body, name=name, grid=(S // T,),
        in_specs=[pl.BlockSpec((T, LANES), lambda i: (i, 4 * D // LANES)), pl.BlockSpec((1, LANES), lambda i: (0, 0))],
        out_specs=pl.BlockSpec((T, LANES), lambda i: (i, 0)),
        out_shape=jax.ShapeDtypeStruct((S, LANES), F32),
        scratch_shapes=[pltpu.VMEM((1, LANES), F32)],
        compiler_params=_cp("arbitrary"),
    )(proj, bf_pad)


def _pair_stats(sq, lo):
    del lo
    a = lax.broadcasted_iota(jnp.int32, (LANES, LANES), 0) < FOX_DH
    b = lax.broadcasted_iota(jnp.int32, (LANES, LANES), 1) < FOX_DH
    avg = jnp.where(a == b, 1.0 / FOX_DH, 0.0).astype(BF)
    hi, mid, low = _split3(sq)
    return _dot(hi, avg) + _dot(mid, avg) + _dot(low, avg)


def _fox_prep(proj, fcum, qw2, kw2, *, name):
    S = proj.shape[0]
    D = proj.shape[1] // 5
    HP = D // LANES
    T = _pick(S, 512, 16)

    def body(q_ref, k_ref, v_ref, f_ref, qw_ref, kw_ref, qa_ref, ka_ref, va_ref, vt_ref):
        hp = pl.program_id(1)
        lane = lax.broadcasted_iota(jnp.int32, (T, LANES), 1)
        lo = lane < FOX_DH
        qv, kv, vv, fv = q_ref[...], k_ref[...], v_ref[...], f_ref[...]
        qn = qv * lax.rsqrt(_pair_stats(qv * qv, lo) + EPS) * qw_ref[...] * (0.125 * LOG2E)
        kn = kv * lax.rsqrt(_pair_stats(kv * kv, lo) + EPS) * kw_ref[...]
        ones_q = jnp.where((lane >= 67) & (lane <= 69), 1.0, 0.0)
        ones_k = jnp.where(((lane >= 64) & (lane <= 66)) | ((lane >= 70) & (lane <= 72)), 1.0, 0.0)
        ones_v = jnp.where((lane >= 64) & (lane <= 66), 1.0, 0.0)
        for hh in range(2):
            fh = jnp.sum(jnp.where(lane == 2 * hp + hh, fv, 0.0), axis=-1, keepdims=True) * LOG2E
            pieces = [p.astype(F32) for p in _split3(fh)]

            def half(x):
                return jnp.where(lo, x if hh == 0 else pltpu.roll(x, FOX_DH, 1), 0.0)

            qa_ref[hh] = _lane_put(half(qn) + ones_q, lane, 64, pieces).astype(BF)
            ka_ref[hh] = _lane_put(half(kn) + ones_k, lane, 67, [-p for p in pieces]).astype(BF)
            va = half(vv) + ones_v
            va_ref[hh] = va.astype(BF)
            vt_ref[hh] = va.T.astype(BF)

    def part(p):
        return pl.BlockSpec((T, LANES), lambda i, hp: (i, p * HP + hp))

    vec = pl.BlockSpec((1, LANES), lambda i, hp: (0, 0))
    aug = pl.BlockSpec((2, T, LANES), lambda i, hp: (hp, i, 0))
    return pl.pallas_call(
        body, name=name, grid=(S // T, HP),
        in_specs=[part(0), part(1), part(2), pl.BlockSpec((T, LANES), lambda i, hp: (i, 0)), vec, vec],
        out_specs=[aug, aug, aug, pl.BlockSpec((2, LANES, T), lambda i, hp: (hp, 0, i))],
        out_shape=[jax.ShapeDtypeStruct((2 * HP, S, LANES), BF)] * 3 + [jax.ShapeDtypeStruct((2 * HP, LANES, S), BF)],
        compiler_params=_cp("parallel", "arbitrary"),
    )(proj, proj, proj, fcum, qw2, kw2)


def _fox_block(S):
    return _pick(S, 256, 16)


def _fox_skip_bounds(fcum, qn_w, kn_w, nheads):
    S = fcum.shape[0]
    B = _fox_block(S)
    qk = 8.0 * LOG2E * 1.02 * jnp.max(jnp.abs(qn_w)) * jnp.max(jnp.abs(kn_w))
    thresh = -(2.0 * qk + 160.0)
    f2 = fcum[:, :nheads] * LOG2E
    first, last = f2[0::B], f2[B - 1::B]
    nb = S // B
    blk = jnp.arange(nb)
    dead = (first[0::2, None, :] - last[None, :, :]) < thresh
    jmin = jnp.sum(dead & (blk[None, :, None] < 2 * jnp.arange(nb // 2)[:, None, None]), axis=1)
    live = (first[:, None, :] - last[None, :, :]) >= thresh
    imax = blk[:, None] + jnp.sum(live & (blk[:, None, None] > blk[None, :, None]), axis=0)
    return jmin.T.astype(jnp.int32), imax.T.astype(jnp.int32)


def _fox_fwd(jmin, qa, ka, vat, proj, *, name):
    H, S, _ = qa.shape
    HP = H // 2
    D = HP * LANES
    B = _fox_block(S)
    BQ = 2 * B
    nq = S // BQ

    def body(jmin_ref, q_ref, k_ref, vt_ref, g_ref, y_ref, o_ref, q2_ref):
        hp, i = pl.program_id(0), pl.program_id(1)
        lane = lax.broadcasted_iota(jnp.int32, (BQ, LANES), 1)
        lo = lane < FOX_DH
        in_lse = (lane >= 70) & (lane <= 72)
        causal = lax.broadcasted_iota(jnp.int32, (BQ, BQ), 0) <= lax.broadcasted_iota(jnp.int32, (BQ, BQ), 1)
        row = lax.broadcasted_iota(jnp.int32, (LANES, BQ), 0)
        m0, acc0 = jnp.full((1, BQ), -jnp.inf, F32), jnp.zeros((LANES, BQ), F32)
        outs = []
        for hh in range(2):
            qb = q_ref[hh]

            def block(j, carry, masked=False):
                m, acc = carry
                sl = pl.ds(pl.multiple_of(j * BQ, BQ), BQ)
                st = _dg(k_ref[hh, sl, :], qb, NT)
                if masked:
                    st = jnp.where(causal, st, -jnp.inf)
                m_new = jnp.maximum(m, jnp.max(st, axis=0, keepdims=True))
                p = jnp.exp2(st - m_new)
                ph = p.astype(BF)
                pl_ = (p - ph.astype(F32)).astype(BF)
                vt = vt_ref[hh, :, sl]
                pv = _dot(jnp.concatenate([vt, vt], axis=1), jnp.concatenate([ph, pl_], axis=0))
                return m_new, acc * jnp.exp2(m - m_new) + pv

            carry = lax.fori_loop(jmin_ref[2 * hp + hh, i] // 2, i, block, (m0, acc0))
            m, acc = block(i, carry, masked=True)
            l = jnp.sum(jnp.where(row == FOX_DH, acc, 0.0), axis=0, keepdims=True)
            tile = acc / l
            for n, piece in enumerate(_split3(m + jnp.log2(l))):
                tile = jnp.where(row == 70 + n, -(piece.astype(F32)), tile)
            tile = tile.T
            outs.append(tile)
            q2_ref[hh] = jnp.where(in_lse, tile, qb.astype(F32)).astype(BF)
        o = jnp.where(lo, outs[0], pltpu.roll(outs[1], FOX_DH, 1))
        o_ref[...] = o
        y_ref[...] = (o * _sigmoid(g_ref[...])).astype(BF)

    blk = pl.BlockSpec((BQ, LANES), lambda hp, i, jm: (i, hp))
    qblk = pl.BlockSpec((2, BQ, LANES), lambda hp, i, jm: (hp, i, 0))
    full = pl.BlockSpec((2, S, LANES), lambda hp, i, jm: (hp, 0, 0))
    full_t = pl.BlockSpec((2, LANES, S), lambda hp, i, jm: (hp, 0, 0))
    return pl.pallas_call(
        body, name=name,
        grid_spec=pltpu.PrefetchScalarGridSpec(
            num_scalar_prefetch=1, grid=(HP, nq),
            in_specs=[qblk, full, full_t, pl.BlockSpec((BQ, LANES), lambda hp, i, jm: (i, 3 * HP + hp))],
            out_specs=[blk, blk, qblk]),
        out_shape=[jax.ShapeDtypeStruct((S, D), BF), jax.ShapeDtypeStruct((S, D), F32),
                   jax.ShapeDtypeStruct((H, S, LANES), BF)],
        compiler_params=_cp("parallel", "arbitrary"),
    )(jmin, qa, ka, vat, proj)


def _fox_bwd_prep(dy, o, proj, *, name):
    S, D = dy.shape
    HP = D // LANES
    T = _pick(S, 512, 16)

    def body(dy_ref, o_ref, g_ref, da_ref):
        lane = lax.broadcasted_iota(jnp.int32, (T, LANES), 1)
        lo = lane < FOX_DH
        do = (dy_ref[...] * _sigmoid(g_ref[...])).astype(BF).astype(F32)
        prod = do * o_ref[...]
        d_lo = jnp.sum(jnp.where(lo, prod, 0.0), axis=-1, keepdims=True)
        d_hi = jnp.sum(jnp.where(lo, 0.0, prod), axis=-1, keepdims=True)
        for hh, delta in enumerate((d_lo, d_hi)):
            base = jnp.where(lo, do if hh == 0 else pltpu.roll(do, FOX_DH, 1), 0.0)
            da_ref[hh] = _lane_put(base, lane, 64, [-(p.astype(F32)) for p in _split3(delta)]).astype(BF)

    blk = pl.BlockSpec((T, LANES), lambda i, hp: (i, hp))
    return pl.pallas_call(
        body, name=name, grid=(S // T, HP),
        in_specs=[blk, blk, pl.BlockSpec((T, LANES), lambda i, hp: (i, 3 * HP + hp))],
        out_specs=pl.BlockSpec((2, T, LANES), lambda i, hp: (hp, i, 0)),
        out_shape=jax.ShapeDtypeStruct((2 * HP, S, LANES), BF),
        compiler_params=_cp("parallel", "arbitrary"),
    )(dy, o, proj)


def _fox_bwd(imax, q2, ka, va, doa, *, name):
    H, S, _ = q2.shape
    B = _fox_block(S)
    nb = S // B

    def body(imax_ref, q_ref, do_ref, k_ref, v_ref, dq_ref, dk_ref, dv_ref, cs_ref):
        j = pl.program_id(1)
        end = imax_ref[pl.program_id(0), j] + 1

        @pl.when(j == 0)
        def _():
            dq_ref[...] = jnp.zeros_like(dq_ref)

        kb, vb = k_ref[...], v_ref[...]
        causal = lax.broadcasted_iota(jnp.int32, (B, B), 1) <= lax.broadcasted_iota(jnp.int32, (B, B), 0)

        def step(i, carry, masked=False, nblk=1):
            dk_acc, dv_acc, cs_acc = carry
            rows = nblk * B
            sl = pl.ds(pl.multiple_of(i * B, B), rows)
            qb, dob = q_ref[sl, :], do_ref[sl, :]
            s = _dg(qb, kb, NT)
            if masked:
                s = jnp.where(causal, s, -jnp.inf)
            p = jnp.exp2(s)
            ds = p * _dg(dob, vb, NT)
            dsb = ds.astype(BF)
            cs_acc = cs_acc + jnp.sum(ds.reshape(rows // 8, 8, B), axis=0)
            dv_acc = dv_acc + _dg(p.astype(BF), dob, TN)
            dk_acc = dk_acc + _dg(dsb, qb, TN)
            dq_ref[sl, :] += _dot(dsb, kb)
            return dk_acc, dv_acc, cs_acc

        zero = jnp.zeros((B, LANES), F32)
        carry = step(j, (zero, zero, jnp.zeros((8, B), F32)), masked=True)
        pos = j + 1
        for U in FOX_BWD_TILES:
            n = (end - pos) // U
            carry = lax.fori_loop(0, n, lambda ii, c, pos=pos, U=U: step(pos + U * ii, c, nblk=U), carry)
            pos = pos + U * n
        dk_acc, dv_acc, cs_acc = carry
        dk_ref[...] = dk_acc
        dv_ref[...] = dv_acc
        cs_ref[...] = jnp.sum(cs_acc, axis=0, keepdims=True)

    full = pl.BlockSpec((None, S, LANES), lambda h, j, im: (h, 0, 0))
    blk = pl.BlockSpec((None, B, LANES), lambda h, j, im: (h, j, 0))
    return pl.pallas_call(
        body, name=name,
        grid_spec=pltpu.PrefetchScalarGridSpec(
            num_scalar_prefetch=1, grid=(H, nb),
            in_specs=[full, full, blk, blk],
            out_specs=[full, blk, blk, pl.BlockSpec((None, 1, B), lambda h, j, im: (h, 0, j))]),
        out_shape=[jax.ShapeDtypeStruct((H, S, LANES), F32)] * 3 + [jax.ShapeDtypeStruct((H, 1, S), F32)],
        compiler_params=_cp("parallel", "arbitrary"),
    )(imax, q2, doa, ka, va)


def _fox_bwd_post(dqa, dka, dva, proj, dy, o, qw2, kw2, *, name):
    S, D = dy.shape
    HP = D // LANES
    T = _pick(S, 512, 16)

    def body(dq_ref, dk_ref, dv_ref, q_ref, k_ref, g_ref, dy_ref, o_ref, qw_ref, kw_ref, dp_ref, dqw_ref, dkw_ref):
        @pl.when((pl.program_id(0) == 0) & (pl.program_id(1) == 0))
        def _():
            dqw_ref[...] = jnp.zeros_like(dqw_ref)
            dkw_ref[...] = jnp.zeros_like(dkw_ref)

        lane = lax.broadcasted_iota(jnp.int32, (T, LANES), 1)
        lo = lane < FOX_DH

        def pair(ref):
            return jnp.where(lo, ref[0], pltpu.roll(ref[1], FOX_DH, 1))

        def norm_bwd(xv, w, dyn, dw_ref):
            r = lax.rsqrt(_pair_stats(xv * xv, lo) + EPS)
            xr = xv * r
            dw_ref[...] += jnp.sum(dyn * xr, axis=0, keepdims=True)
            u = dyn * w
            return r * (u - xr * _pair_stats(u * xr, lo))

        dp_ref[0] = norm_bwd(q_ref[...], qw_ref[...], pair(dq_ref) * 0.125, dqw_ref).astype(BF)
        dp_ref[1] = norm_bwd(k_ref[...], kw_ref[...], pair(dk_ref) * (1.0 / LOG2E), dkw_ref).astype(BF)
        dp_ref[2] = pair(dv_ref).astype(BF)
        sg = _sigmoid(g_ref[...])
        dp_ref[3] = (dy_ref[...] * o_ref[...] * sg * (1.0 - sg)).astype(BF)

    def part(p):
        return pl.BlockSpec((T, LANES), lambda i, hp: (i, p * HP + hp))

    aug = pl.BlockSpec((2, T, LANES), lambda i, hp: (hp, i, 0))
    blk = pl.BlockSpec((T, LANES), lambda i, hp: (i, hp))
    vec = pl.BlockSpec((1, LANES), lambda i, hp: (0, 0))
    return pl.pallas_call(
        body, name=name, grid=(S // T, HP),
        in_specs=[aug, aug, aug, part(0), part(1), part(3), blk, blk, vec, vec],
        out_specs=[pl.BlockSpec((4, T, LANES), lambda i, hp: (0, i, hp)), vec, vec],
        out_shape=[jax.ShapeDtypeStruct((5, S, D), BF), jax.ShapeDtypeStruct((1, LANES), F32),
                   jax.ShapeDtypeStruct((1, LANES), F32)],
        compiler_params=_cp("arbitrary", "arbitrary"),
    )(dqa, dka, dva, proj, proj, proj, dy, o, qw2, kw2)


def _fox_dfz(colsum, nheads, proj, bf_pad, dproj, *, name):
    S = colsum.shape[0]
    H = nheads
    D = dproj.shape[2]
    T = _pick(S, 256, 16)
    nb = S // T

    def body(cs_ref, fz_ref, b_ref, _, dp_ref, db_ref, carry):
        @pl.when(pl.program_id(0) == 0)
        def _():
            carry[...] = jnp.zeros_like(carry)
            db_ref[...] = jnp.zeros_like(db_ref)

        lane = lax.broadcasted_iota(jnp.int32, (T, LANES), 1)
        df = -cs_ref[...]
        triu = jnp.where(lax.broadcasted_iota(jnp.int32, (T, T), 0) <= lax.broadcasted_iota(jnp.int32, (T, T), 1),
                         1.0, 0.0).astype(BF)
        dlogf = _tri_dot(triu, df) + carry[...]
        carry[...] = _row_of(dlogf, lax.broadcasted_iota(jnp.int32, (T, LANES), 0), 0)
        dfz = jnp.where(lane < H, dlogf * _sigmoid(-(fz_ref[...] + b_ref[...])), 0.0)
        db_ref[...] += jnp.sum(dfz, axis=0, keepdims=True)
        dp_ref[...] = jnp.zeros_like(dp_ref)
        dp_ref[:, 0:LANES] = dfz.astype(BF)

    return pl.pallas_call(
        body, name=name, grid=(nb,),
        in_specs=[pl.BlockSpec((T, LANES), lambda i: (nb - 1 - i, 0)),
                  pl.BlockSpec((T, LANES), lambda i: (nb - 1 - i, 4 * D // LANES)),
                  pl.BlockSpec((1, LANES), lambda i: (0, 0)),
                  pl.BlockSpec(memory_space=pl.ANY)],
        out_specs=[pl.BlockSpec((None, T, D), lambda i: (4, nb - 1 - i, 0)), pl.BlockSpec((1, LANES), lambda i: (0, 0))],
        out_shape=[jax.ShapeDtypeStruct(dproj.shape, BF), jax.ShapeDtypeStruct((1, LANES), F32)],
        scratch_shapes=[pltpu.VMEM((1, LANES), F32)],
        input_output_aliases={3: 0},
        compiler_params=_cp("arbitrary"),
    )(colsum, proj, bf_pad, dproj)


def _mod_fwd(c16, w, b, *, name):
    L, D, N = w.shape
    tn = _pick(N, 512)

    def body(c_ref, w_ref, b_ref, o_ref):
        cv = c_ref[...]
        ca = (cv * _sigmoid(cv)).astype(BF)
        o_ref[...] = _dot(ca, w_ref[...].astype(BF)) + b_ref[...]

    return pl.pallas_call(
        body, name=name, grid=(L, N // tn),
        in_specs=[pl.BlockSpec((16, D), lambda l, j: (0, 0)), pl.BlockSpec((None, D, tn), lambda l, j: (l, 0, j)),
                  pl.BlockSpec((None, 1, tn), lambda l, j: (l, 0, j))],
        out_specs=pl.BlockSpec((None, 16, tn), lambda l, j: (l, 0, j)),
        out_shape=jax.ShapeDtypeStruct((L, 16, N), F32),
        compiler_params=_cp("parallel", "arbitrary"),
    )(c16, w, b)


def _mod_bwd(c16, dmod, *, name):
    L, _, N = dmod.shape
    D = c16.shape[1]
    tn = _pick(N, 512)

    def body(c_ref, d_ref, o_ref):
        cv = c_ref[...]
        ca = (cv * _sigmoid(cv)).astype(BF)
        o_ref[...] = _dg(ca, d_ref[...].astype(BF), TN)

    return pl.pallas_call(
        body, name=name, grid=(L, N // tn),
        in_specs=[pl.BlockSpec((16, D), lambda l, j: (0, 0)), pl.BlockSpec((None, 16, tn), lambda l, j: (l, 0, j))],
        out_specs=pl.BlockSpec((None, D, tn), lambda l, j: (l, 0, j)),
        out_shape=jax.ShapeDtypeStruct((L, D, N), F32),
        compiler_params=_cp("parallel", "arbitrary"),
    )(c16, dmod)


def _adamw_math(w, g, m, v):
    m = ADAM_B1 * m + (1.0 - ADAM_B1) * g
    v = ADAM_B2 * v + (1.0 - ADAM_B2) * (g * g)
    m_hat = m / (1.0 - ADAM_B1 ** ADAM_STEP)
    v_hat = v / (1.0 - ADAM_B2 ** ADAM_STEP)
    return -ADAM_LR * (m_hat / (jnp.sqrt(v_hat) + ADAM_EPS) + ADAM_WD * w), m, v


def _adamw(w, g, m, v, *, g_at=None, name):
    R, C = w.shape
    row0 = 0 if g_at is None else g_at[1]
    tr = min(math.gcd(row0, 256) if row0 else 256, -(-R // 8) * 8)
    g0 = row0 // tr
    if g_at is None:
        g_spec = pl.BlockSpec((tr, C), lambda i: (i, 0))
    else:
        g_spec = pl.BlockSpec((None, tr, C), lambda i: (g_at[0], g0 + i, 0))

    def body(w_ref, g_ref, m_ref, v_ref, d_ref, mo_ref, vo_ref):
        d, mn, vn = _adamw_math(w_ref[...], g_ref[...], m_ref[...], v_ref[...])
        d_ref[...] = d
        mo_ref[...] = mn
        vo_ref[...] = vn

    blk = pl.BlockSpec((tr, C), lambda i: (i, 0))
    return pl.pallas_call(
        body, name=name, grid=(pl.cdiv(R, tr),),
        in_specs=[blk, g_spec, blk, blk],
        out_specs=[blk, blk, blk],
        out_shape=[jax.ShapeDtypeStruct((R, C), F32)] * 3,
        compiler_params=_cp("parallel"),
    )(w, g, m, v)


def _sum_parts(parts, *, name):
    P, R, C = parts.shape

    def body(p_ref, o_ref):
        acc = p_ref[0]
        for p in range(1, P):
            acc = acc + p_ref[p]
        o_ref[...] = acc

    return pl.pallas_call(
        body, name=name, grid=(1,),
        in_specs=[pl.BlockSpec((P, R, C), lambda i: (0, 0, 0))],
        out_specs=pl.BlockSpec((R, C), lambda i: (0, 0)),
        out_shape=jax.ShapeDtypeStruct((R, C), F32),
        compiler_params=_cp("arbitrary"),
    )(parts)


def _add_halves(g4, recv, c_idx, *, name):
    _, _, Rh, C = g4.shape
    tr = _pick(Rh, 256, 16)

    def body(c_ref, a_ref, b_ref, o_ref):
        o_ref[...] = (a_ref[...] + b_ref[...].astype(F32)).astype(BF)

    return pl.pallas_call(
        body, name=name,
        grid_spec=pltpu.PrefetchScalarGridSpec(
            num_scalar_prefetch=1, grid=(4, pl.cdiv(Rh, tr)),
            in_specs=[pl.BlockSpec((None, None, tr, C), lambda j, r, c: (j, c[0], r, 0)),
                      pl.BlockSpec((None, tr, C), lambda j, r, c: (j, r, 0))],
            out_specs=pl.BlockSpec((None, tr, C), lambda j, r, c: (j, r, 0))),
        out_shape=jax.ShapeDtypeStruct((4, Rh, C), BF),
        compiler_params=_cp("parallel", "arbitrary"),
    )(c_idx, g4, recv)


def _add_four(g4, from_sibling, from_chips, pos, *, name):
    _, _, Rh, C = g4.shape
    tr = _pick(Rh, 256, 16)

    def body(p_ref, a_ref, s_ref, b_ref, o_ref):
        own = a_ref[...] + s_ref[...].astype(F32)
        o_ref[...] = ((own + b_ref[0].astype(F32)) + b_ref[1].astype(F32)) + b_ref[2].astype(F32)

    return pl.pallas_call(
        body, name=name,
        grid_spec=pltpu.PrefetchScalarGridSpec(
            num_scalar_prefetch=1, grid=(pl.cdiv(Rh, tr),),
            in_specs=[pl.BlockSpec((None, None, tr, C), lambda r, p: (p[0], p[1], r, 0)),
                      pl.BlockSpec((None, tr, C), lambda r, p: (p[0], r, 0)),
                      pl.BlockSpec((3, tr, C), lambda r, p: (0, r, 0))],
            out_specs=pl.BlockSpec((None, tr, C), lambda r, p: (p[1], r, 0))),
        out_shape=jax.ShapeDtypeStruct((2, Rh, C), F32),
        compiler_params=_cp("arbitrary"),
    )(pos, g4, from_sibling, from_chips)


HBM = pl.BlockSpec(memory_space=pltpu.HBM)


def _mesh_pos():
    return lax.axis_index("x"), lax.axis_index("y"), lax.axis_index("c")


def _other_chips(x, y):
    return [(1 - x, y), (x, 1 - y), (1 - x, 1 - y)]


def _allgather_small(xs, *, name):
    m_per, n = xs.shape

    def body(x_ref, out_ref, send_sems, recv_sems, local_sem):
        x, y, c = _mesh_pos()
        me, sibling = (x, y, c), (x, y, 1 - c)
        chips = _other_chips(x, y)

        def rows(px, py, pc):
            return out_ref.at[pl.ds((4 * px + 2 * py + pc) * m_per, m_per), :]

        def copy(k, block, to, src=None):
            return pltpu.make_async_remote_copy(
                src_ref=rows(*block) if src is None else src, dst_ref=rows(*block),
                send_sem=send_sems.at[k], recv_sem=recv_sems.at[k], device_id=to, device_id_type=MESH)

        mine = pltpu.make_async_copy(x_ref, rows(*me), local_sem)
        mine.start()
        first = [copy(0, me, sibling, src=x_ref)]
        first += [copy(1 + j, me, (*chip, c), src=x_ref) for j, chip in enumerate(chips)]
        for cp in first:
            cp.start()
        passed = [copy(4 + j, (*chip, c), sibling) for j, chip in enumerate(chips)]
        for j, chip in enumerate(chips):
            copy(1 + j, (*chip, c), me).wait_recv()
            passed[j].start()
        copy(0, sibling, me).wait_recv()
        for j, chip in enumerate(chips):
            copy(4 + j, (*chip, 1 - c), me).wait_recv()
        for cp in first + passed:
            cp.wait_send()
        mine.wait()

    return pl.pallas_call(
        body, name=name,
        out_shape=jax.ShapeDtypeStruct((N_DEV * m_per, n), xs.dtype),
        in_specs=[pl.BlockSpec(memory_space=pltpu.VMEM)],
        out_specs=pl.BlockSpec(memory_space=pltpu.VMEM),
        scratch_shapes=[pltpu.SemaphoreType.DMA((7,)), pltpu.SemaphoreType.DMA((7,)), pltpu.SemaphoreType.DMA],
    )(xs)


def _chip_slab_copies(s_ref, out_ref, send_sems, recv_sems):
    R = s_ref.shape[0]
    Rh = R // 2
    x, y, c = _mesh_pos()
    me, sibling = (x, y, c), (x, y, 1 - c)
    chips = _other_chips(x, y)

    def half(px, py, pc):
        return out_ref.at[2 * px + py, pl.ds(pc * Rh, Rh), :]

    def copy(k, block, to, src=None):
        return pltpu.make_async_remote_copy(
            src_ref=half(*block) if src is None else src, dst_ref=half(*block),
            send_sem=send_sems.at[k], recv_sem=recv_sems.at[k], device_id=to, device_id_type=MESH)

    first = [copy(j, me, (*chip, c), src=s_ref.at[pl.ds(c * Rh, Rh), :]) for j, chip in enumerate(chips)]
    passed = [copy(3 + j, (*chip, c), sibling) for j, chip in enumerate(chips)]
    landed = [copy(j, (*chip, c), me) for j, chip in enumerate(chips)]
    from_sibling = [copy(3 + j, (*chip, 1 - c), me) for j, chip in enumerate(chips)]
    return first, passed, landed, from_sibling


def _allgather_chip_slabs(slab, *, name):
    R, C = slab.shape

    def body(s_ref, out_ref, send_sems, recv_sems):
        first, passed, landed, from_sibling = _chip_slab_copies(s_ref, out_ref, send_sems, recv_sems)
        for cp in first:
            cp.start()
        for arrived, onward in zip(landed, passed):
            arrived.wait_recv()
            onward.start()
        for cp in from_sibling:
            cp.wait_recv()
        for cp in first + passed:
            cp.wait_send()

    return pl.pallas_call(
        body, name=name,
        out_shape=jax.ShapeDtypeStruct((N_CHIPS, R, C), slab.dtype),
        in_specs=[HBM], out_specs=HBM,
        scratch_shapes=[pltpu.SemaphoreType.DMA((6,)), pltpu.SemaphoreType.DMA((6,))],
    )(slab)


def _swap_halves(mine, *, name):
    def body(g_ref, out_ref, send_sems, recv_sems):
        x, y, c = _mesh_pos()
        copies = [pltpu.make_async_remote_copy(
            src_ref=g_ref.at[j], dst_ref=out_ref.at[j], send_sem=send_sems.at[j], recv_sem=recv_sems.at[j],
            device_id=(x, y, 1 - c), device_id_type=MESH) for j in range(N_CHIPS)]
        for cp in copies:
            cp.start()
        for cp in copies:
            cp.wait()

    return pl.pallas_call(
        body, name=name,
        out_shape=jax.ShapeDtypeStruct(mine.shape, mine.dtype),
        in_specs=[HBM], out_specs=HBM,
        scratch_shapes=[pltpu.SemaphoreType.DMA((N_CHIPS,)), pltpu.SemaphoreType.DMA((N_CHIPS,))],
    )(mine)


def _scatter_partials(part, *, name):
    _, Rh, C = part.shape

    def body(p_ref, out_ref, send_sems, recv_sems):
        x, y, c = _mesh_pos()
        copies = [pltpu.make_async_remote_copy(
            src_ref=p_ref.at[2 * px + py], dst_ref=out_ref.at[j], send_sem=send_sems.at[j], recv_sem=recv_sems.at[j],
            device_id=(px, py, c), device_id_type=MESH) for j, (px, py) in enumerate(_other_chips(x, y))]
        for cp in copies:
            cp.start()
        for cp in copies:
            cp.wait()

    return pl.pallas_call(
        body, name=name,
        out_shape=jax.ShapeDtypeStruct((3, Rh, C), part.dtype),
        in_specs=[HBM], out_specs=HBM,
        scratch_shapes=[pltpu.SemaphoreType.DMA((3,)), pltpu.SemaphoreType.DMA((3,))],
    )(part)


def _join_halves(buf, *, name):
    def body(b_ref, out_ref, send_sem, recv_sem):
        x, y, c = _mesh_pos()
        cp = pltpu.make_async_remote_copy(
            src_ref=b_ref.at[c], dst_ref=out_ref.at[c], send_sem=send_sem, recv_sem=recv_sem,
            device_id=(x, y, 1 - c), device_id_type=MESH)
        cp.start()
        cp.wait()

    return pl.pallas_call(
        body, name=name,
        out_shape=jax.ShapeDtypeStruct(buf.shape, buf.dtype),
        in_specs=[HBM], out_specs=HBM, input_output_aliases={0: 0},
        scratch_shapes=[pltpu.SemaphoreType.DMA, pltpu.SemaphoreType.DMA],
    )(buf)


def _pad_rows(a, mult):
    pad = (-a.shape[0]) % mult
    return a if pad == 0 else jnp.pad(a, ((0, pad),) + ((0, 0),) * (a.ndim - 1))


def _local_step(x, target, mod, wts, small, slab_rest=None, unpack_rest=None):
    S, D = x.shape
    HP = D // LANES
    row = lambda v: v.reshape(1, -1)
    msplit = [[row(mod[i, k * D:(k + 1) * D]) for k in range(6)] for i in range(2)]
    gw, gs = {}, {}
    dmod = [[None] * 6 for _ in range(2)]

    sh1, sc1, g1, sh2, sc2, g2 = msplit[0]
    n1w0, n2w0 = row(small["norm1_w"][0]), row(small["norm2_w"][0])
    proj0, h1_0 = _ln_matmul(x, n1w0, sc1, sh1, wts["hg_w_in"], relu2=False, name="hg_in_proj")
    gn = small["hg_gn_w"].reshape(1, LANES)
    ypre0, o0, states, *gathered = _hg_fwd(proj0, small["hg_lb"], gn, slab_rest, name="hg_fwd")
    if slab_rest is not None:
        wts = {**wts, **unpack_rest(gathered[0])}
    x1, ymix0 = _matmul_resid(ypre0, wts["hg_w_out"], x, g1, name="hg_out_proj")
    a0, u0, h2_0 = _ln_matmul(x1, n2w0, sc2, sh2, wts["mlp_w1_0"], relu2=True, name="mlp0_up")
    x2, ymlp0 = _matmul_resid(u0, wts["mlp_w2_0"], x1, g2, name="mlp0_down")

    sh1b, sc1b, g1b, sh2b, sc2b, g2b = msplit[1]
    n1w1, n2w1 = row(small["norm1_w"][1]), row(small["norm2_w"][1])
    proj1, h1_1 = _ln_matmul(x2, n1w1, sc1b, sh1b, wts["fox_w_in"], relu2=False, name="fox_in_proj")
    nheads = 2 * HP
    bf_pad = jnp.pad(small["fox_b_f"].reshape(1, nheads), ((0, 0), (0, LANES - nheads)))
    qw2 = jnp.tile(small["fox_qn_w"].reshape(1, FOX_DH), (1, 2))
    kw2 = jnp.tile(small["fox_kn_w"].reshape(1, FOX_DH), (1, 2))
    fcum = _fox_cumsum(proj1, bf_pad, name="fox_cumsum")
    qa, ka, va, vat = _fox_prep(proj1, fcum, qw2, kw2, name="fox_prep")
    jmin, imax = _fox_skip_bounds(fcum, small["fox_qn_w"], small["fox_kn_w"], nheads)
    ypre1, o1, q2 = _fox_fwd(jmin, qa, ka, vat, proj1, name="fox_fwd")
    x3, ymix1 = _matmul_resid(ypre1, wts["fox_w_out"], x2, g1b, name="fox_out_proj")
    a1, u1, h2_1 = _ln_matmul(x3, n2w1, sc2b, sh2b, wts["mlp_w1_1"], relu2=True, name="mlp1_up")
    x4, ymlp1 = _matmul_resid(u1, wts["mlp_w2_1"], x3, g2b, name="mlp1_down")

    loss, dx4, dfw = _loss_kernel(x4, row(small["final_w"]), target, name="loss")
    gs["final_w"] = dfw.reshape(-1)

    def mlp_bwd(i, dx_out, x_in, h2, a, u, ymlp, n2w, sc2_, g2_):
        dz, dm, dg2 = _gate_matmul_nt(dx_out, g2_, ymlp, wts[f"mlp_w2_{i}"], a, name=f"mlp{i}_down_bwd")
        gw[f"mlp_w2_{i}"] = _matmul_tn(u, dm[None], name=f"mlp{i}_dw2")
        gw[f"mlp_w1_{i}"] = _matmul_tn(h2, dz[None], name=f"mlp{i}_dw1")
        dx_in, dsc, dsh, dnw = _matmul_nt_lnbwd(dz[None], wts[f"mlp_w1_{i}"], x_in, n2w, sc2_, dx_out,
                                                name=f"mlp{i}_up_bwd")
        dmod[i][3], dmod[i][4], dmod[i][5] = dsh, dsc, dg2
        return dx_in, dnw

    dx3, dn2w1 = mlp_bwd(1, dx4, x3, h2_1, a1, u1, ymlp1, n2w1, sc2b, g2b)
    dyp1, dm1, dg1b = _gate_matmul_nt(dx3, g1b, ymix1, wts["fox_w_out"], None, name="fox_out_bwd")
    gw["fox_w_out"] = _matmul_tn(ypre1, dm1[None], name="fox_dw_out")
    doa = _fox_bwd_prep(dyp1, o1, proj1, name="fox_bwd_prep")
    dqa, dka, dva, colsum = _fox_bwd(imax, q2, ka, va, doa, name="fox_bwd")
    colsum = jnp.pad(colsum[:, 0, :].T, ((0, 0), (0, LANES - nheads)))
    dproj1, dqw, dkw = _fox_bwd_post(dqa, dka, dva, proj1, dyp1, o1, qw2, kw2, name="fox_bwd_post")
    dproj1, dbf = _fox_dfz(colsum, nheads, proj1, bf_pad, dproj1, name="fox_dfz")
    gw["fox_w_in"] = _matmul_tn(h1_1, dproj1, name="fox_dw_in")
    dx2, dsc, dsh, dn1w1 = _matmul_nt_lnbwd(dproj1, wts["fox_w_in"], x2, n1w1, sc1b, dx3, name="fox_in_bwd")
    dmod[1][0], dmod[1][1], dmod[1][2] = dsh, dsc, dg1b
    gs["fox_qn_w"] = dqw[0, :FOX_DH] + dqw[0, FOX_DH:]
    gs["fox_kn_w"] = dkw[0, :FOX_DH] + dkw[0, FOX_DH:]
    gs["fox_b_f"] = dbf[0, :nheads]

    dx1, dn2w0 = mlp_bwd(0, dx2, x1, h2_0, a0, u0, ymlp0, n2w0, sc2, g2)
    dyp0, dm0, dg1 = _gate_matmul_nt(dx1, g1, ymix0, wts["hg_w_out"], None, name="hg_out_bwd")
    gw["hg_w_out"] = _matmul_tn(ypre0, dm0[None], name="hg_dw_out")
    dproj0, dlb, dgn = _hg_bwd(proj0, small["hg_lb"], gn, o0, states, dyp0, name="hg_bwd")
    gw["hg_w_in"] = _matmul_tn(h1_0, dproj0, name="hg_dw_in")
    dx0, dsc, dsh, dn1w0 = _matmul_nt_lnbwd(dproj0, wts["hg_w_in"], x, n1w0, sc1, dx1, name="hg_in_bwd")
    dmod[0][0], dmod[0][1], dmod[0][2] = dsh, dsc, dg1
    gs["hg_lb"] = dlb
    gs["hg_gn_w"] = jnp.sum(dgn, axis=0)

    gs["norm1_w"] = jnp.concatenate([dn1w0, dn1w1], axis=0)
    gs["norm2_w"] = jnp.concatenate([dn2w0, dn2w1], axis=0)
    gs["dmod"] = jnp.stack([jnp.concatenate(dmod[i], axis=1)[0] for i in range(2)])
    return loss, dx0, gw, gs


def _pack_halves(layout):
    rh = -(-max(sum(a.shape[0] for _, a in half) for half in layout) // 16) * 16
    place, parts = {}, []
    for h, half in enumerate(layout):
        off = 0
        for n, a in half:
            place[n] = (h, off, a.shape[0])
            off += a.shape[0]
        parts.append(jnp.pad(jnp.concatenate([a.astype(BF) for _, a in half], axis=0), ((0, rh - off), (0, 0))))
    return jnp.concatenate(parts, axis=0), place, rh


SMALL_NAMES = ["norm1_w", "norm2_w", "hg_lb", "hg_gn_w", "fox_b_f", "fox_qn_w", "fox_kn_w", "final_w"]


def _pack_small(d, names):
    rows, offs, r0 = [], {}, 0
    for n in names:
        flat = d[n].reshape(-1)
        nr = -(-flat.shape[0] // LANES)
        rows.append(jnp.pad(flat, (0, nr * LANES - flat.shape[0])).reshape(nr, LANES))
        offs[n] = (r0, nr)
        r0 += nr
    return jnp.concatenate(rows, axis=0), offs


def _unpack_small(packed, offs, name, like):
    r0, nr = offs[name]
    return packed[r0:r0 + nr].reshape(-1)[:like.size].reshape(like.shape)


def kernel(x, c, w_mod, b_mod, norm1_w, norm2_w, hg_w_in, hg_w_out, hg_lb, hg_gn_w, fox_w_in, fox_b_f, fox_qn_w, fox_kn_w, fox_w_out, mlp_w1, mlp_w2, final_w, loss_target, m_w_mod, m_b_mod, m_norm1_w, m_norm2_w, m_hg_w_in, m_hg_w_out, m_hg_lb, m_hg_gn_w, m_fox_w_in, m_fox_b_f, m_fox_qn_w, m_fox_kn_w, m_fox_w_out, m_mlp_w1, m_mlp_w2, m_final_w, v_w_mod, v_b_mod, v_norm1_w, v_norm2_w, v_hg_w_in, v_hg_w_out, v_hg_lb, v_hg_gn_w, v_fox_w_in, v_fox_b_f, v_fox_qn_w, v_fox_kn_w, v_fox_w_out, v_mlp_w1, v_mlp_w2, v_final_w):
    S, D = x.shape[1], x.shape[2]
    nheads = D // FOX_DH
    ax, ay, ac = _mesh_pos()
    chip = 2 * ax + ay
    dev = 2 * chip + ac
    xs, tgt = x.reshape(S, D), loss_target.reshape(S, D)

    c_all = _allgather_small(_pad_rows(c.reshape(-1, LANES), 8), name="gather_c")
    c_all = c_all.reshape(N_DEV, -1)[:, :D]
    c16 = _pad_rows(c_all, 16)
    nmod = w_mod.shape[2]
    b_shard = lax.dynamic_slice_in_dim(b_mod, chip * nmod, nmod, axis=1)
    mod_shard = _mod_fwd(c16, w_mod, b_shard[:, None, :], name="mod_fwd")[:, :N_DEV]
    mod_all = _allgather_small(mod_shard.reshape(-1, LANES), name="gather_mod")
    mod_all = mod_all.reshape(N_CHIPS, 2, 2, N_DEV, nmod)[:, 0]
    mod = lax.dynamic_index_in_dim(mod_all, dev, axis=2, keepdims=False)
    mod = mod.transpose(1, 0, 2).reshape(2, N_CHIPS * nmod)

    fox_rows = fox_w_in.shape[2]
    col = lambda g: g.transpose(1, 0, 2).reshape(g.shape[1], -1)
    rowsh = lambda g: g.reshape(-1, g.shape[2])
    own = lambda g, s: lax.dynamic_update_index_in_dim(g, s, chip, 0)

    slab_in = hg_w_in[0].astype(BF)
    wts = {"hg_w_in": col(own(_allgather_chip_slabs(slab_in, name="gather_hg_w_in"), slab_in))}
    slab_rest, place_rest, rh_rest = _pack_halves(
        [[("mlp_w1", mlp_w1.reshape(2 * D, D)), ("hg_w_out", hg_w_out[0]), ("fox_w_out", fox_w_out[0])],
         [("mlp_w2", mlp_w2.reshape(2 * D, D)), ("fox_w_in", fox_w_in[0].reshape(fox_rows, D))]])

    def unpack_rest(gathered):
        gathered = own(gathered, slab_rest)

        def seg(n):
            h, off, rows = place_rest[n]
            return gathered[:, h * rh_rest + off:h * rh_rest + off + rows, :]

        w1 = seg("mlp_w1").reshape(N_CHIPS, 2, D, D)
        w2 = seg("mlp_w2").reshape(N_CHIPS, 2, D, D)
        fox_in = col(seg("fox_w_in").reshape(N_CHIPS, D, fox_rows))
        return {
            "hg_w_out": rowsh(seg("hg_w_out")), "fox_w_out": rowsh(seg("fox_w_out")),
            "mlp_w1_0": col(w1[:, 0]), "mlp_w1_1": col(w1[:, 1]), "mlp_w2_0": rowsh(w2[:, 0]), "mlp_w2_1": rowsh(w2[:, 1]),
            "fox_w_in": jnp.pad(fox_in, ((0, 0), (0, 5 * D - fox_in.shape[1]))),
        }

    small = {"norm1_w": norm1_w, "norm2_w": norm2_w, "hg_lb": hg_lb, "hg_gn_w": hg_gn_w, "fox_b_f": fox_b_f,
             "fox_qn_w": fox_qn_w, "fox_kn_w": fox_kn_w, "final_w": final_w}

    loss_part, grad_x, gw, gs = _local_step(xs, tgt, mod, wts, small, slab_rest, unpack_rest)

    layout = [[("hg_w_in", hg_w_in[0]), ("mlp_w1", mlp_w1.reshape(2 * D, D)), ("hg_w_out", hg_w_out[0])],
              [("mlp_w2", mlp_w2.reshape(2 * D, D)), ("fox_w_out", fox_w_out[0]),
               ("fox_w_in", fox_w_in[0].reshape(fox_rows, D))]]
    Rh = -(-max(sum(a.shape[0] for _, a in half) for half in layout) // 16) * 16
    place = {}
    for h, half in enumerate(layout):
        off = 0
        for n, a in half:
            place[n] = (h, off, a.shape[0])
            off += a.shape[0]
    loss = lax.psum(loss_part[0, 0], ("x", "y", "c"))

    def uncol(g, n):
        return g.reshape(g.shape[0], N_CHIPS, n).transpose(1, 0, 2)

    gseg = {
        "hg_w_in": uncol(gw["hg_w_in"], D), "hg_w_out": gw["hg_w_out"].reshape(N_CHIPS, D // 4, D),
        "fox_w_out": gw["fox_w_out"].reshape(N_CHIPS, D // 4, D),
        "mlp_w1": jnp.concatenate([uncol(gw["mlp_w1_0"], D), uncol(gw["mlp_w1_1"], D)], axis=1),
        "mlp_w2": jnp.concatenate([gw["mlp_w2_0"].reshape(N_CHIPS, D, D), gw["mlp_w2_1"].reshape(N_CHIPS, D, D)], axis=1),
        "fox_w_in": uncol(gw["fox_w_in"][:, :4 * fox_rows], fox_rows).reshape(N_CHIPS, fox_rows, D),
    }
    pieces = []
    for half in layout:
        pieces += [gseg[n] for n, _ in half]
        pieces.append(jnp.zeros((N_CHIPS, Rh - sum(a.shape[0] for _, a in half), D), F32))
    g4 = jnp.concatenate(pieces, axis=1).reshape(N_CHIPS, 2, Rh, D)
    to_sibling = lax.dynamic_index_in_dim(g4, 1 - ac, axis=1, keepdims=False).astype(BF)
    from_sibling = _swap_halves(to_sibling, name="rs_swap_halves")
    chip_part = _add_halves(g4, from_sibling, ac.reshape(1), name="rs_add_halves")
    from_chips = _scatter_partials(chip_part, name="rs_scatter")
    my_half = _add_four(g4, from_sibling, from_chips, jnp.stack([chip, ac]), name="rs_add_chips")
    gshard = _join_halves(my_half, name="rs_join")

    names = ["dmod"] + SMALL_NAMES
    packed, offs = _pack_small(gs, names)
    packed = _pad_rows(packed, 8)
    rp = packed.shape[0]
    parts = _allgather_small(packed, name="gather_small").reshape(N_DEV, rp, LANES)
    total = _sum_parts(parts, name="sum_small")
    r0, nr = offs["dmod"]
    dmod_all = parts[:, r0:r0 + nr].reshape(N_DEV, 2, N_CHIPS * nmod)
    dmod_shard = lax.dynamic_slice_in_dim(dmod_all, chip * nmod, nmod, axis=2).transpose(1, 0, 2)
    g_w_mod = _mod_bwd(c16, jnp.pad(dmod_shard, ((0, 0), (0, 16 - N_DEV), (0, 0))), name="mod_bwd")

    grads = {"w_mod": g_w_mod, "b_mod": _unpack_small(total, offs, "dmod", b_mod)}
    for n in SMALL_NAMES:
        grads[n] = _unpack_small(total, offs, n, small[n])

    given = dict(w_mod=(w_mod, m_w_mod, v_w_mod), b_mod=(b_mod, m_b_mod, v_b_mod), norm1_w=(norm1_w, m_norm1_w, v_norm1_w),
                 norm2_w=(norm2_w, m_norm2_w, v_norm2_w), hg_w_in=(hg_w_in, m_hg_w_in, v_hg_w_in),
                 hg_w_out=(hg_w_out, m_hg_w_out, v_hg_w_out), hg_lb=(hg_lb, m_hg_lb, v_hg_lb),
                 hg_gn_w=(hg_gn_w, m_hg_gn_w, v_hg_gn_w), fox_w_in=(fox_w_in, m_fox_w_in, v_fox_w_in),
                 fox_b_f=(fox_b_f, m_fox_b_f, v_fox_b_f), fox_qn_w=(fox_qn_w, m_fox_qn_w, v_fox_qn_w),
                 fox_kn_w=(fox_kn_w, m_fox_kn_w, v_fox_kn_w), fox_w_out=(fox_w_out, m_fox_w_out, v_fox_w_out),
                 mlp_w1=(mlp_w1, m_mlp_w1, v_mlp_w1), mlp_w2=(mlp_w2, m_mlp_w2, v_mlp_w2), final_w=(final_w, m_final_w, v_final_w))
    upd = {}

    for n, (h, off, rows) in place.items():
        w, m, v = given[n]
        flat = lambda a: a.reshape(rows, D)
        d, mn, vn = _adamw(flat(w), gshard, flat(m), flat(v), g_at=(h, off), name=f"adamw_{n}")
        grads[n] = gshard[h, off:off + rows].reshape(w.shape)
        upd[n] = tuple(a.reshape(w.shape) for a in (d, mn, vn))

    w, m, v = given["w_mod"]
    flat = lambda a: a.reshape(-1, nmod)
    upd["w_mod"] = tuple(a.reshape(w.shape) for a in _adamw(flat(w), flat(g_w_mod), flat(m), flat(v), name="adamw_w_mod"))

    snames = ["b_mod"] + SMALL_NAMES
    pw, soffs = _pack_small({n: given[n][0] for n in snames}, snames)
    pm, _ = _pack_small({n: given[n][1] for n in snames}, snames)
    pv, _ = _pack_small({n: given[n][2] for n in snames}, snames)
    pg, _ = _pack_small({n: grads[n] for n in snames}, snames)
    pw, pm, pv, pg = (_pad_rows(a, 8) for a in (pw, pm, pv, pg))
    sd, smn, svn = _adamw(pw, pg, pm, pv, name="adamw_small")
    for n in snames:
        like = given[n][0]
        upd[n] = tuple(_unpack_small(a, soffs, n, like) for a in (sd, smn, svn))

    order = ["w_mod", "b_mod", "norm1_w", "norm2_w", "hg_w_in", "hg_w_out", "hg_lb", "hg_gn_w", "fox_w_in", "fox_b_f",
             "fox_qn_w", "fox_kn_w", "fox_w_out", "mlp_w1", "mlp_w2", "final_w"]
    return (loss, grad_x.reshape(x.shape), *[grads[n] for n in order], *[upd[n][0] for n in order],
            *[upd[n][1] for n in order], *[upd[n][2] for n in order])
```

```python
import math

import jax
import jax.numpy as jnp
from jax import lax
from jax.experimental import pallas as pl
from jax.experimental.pallas import tpu as pltpu

EPS = 1e-6
ADAM_LR, ADAM_B1, ADAM_B2, ADAM_EPS, ADAM_WD, ADAM_STEP = 0.001, 0.9, 0.999, 1e-08, 0.01, 10

F32 = jnp.float32
BF = jnp.bfloat16
LANES = 128
HG_CHUNK = 64
HG_HEADS_PER_STEP = 8
HG_TOKENS_PER_STEP = 256
FOX_BWD_TILES = (8, 4, 2, 1)
LOG2E = 1.4426950408889634
FOX_DH = 64
N_CHIPS = 4
N_DEV = 8
VMEM_LIMIT = 48 * 1024 * 1024
MESH = pl.DeviceIdType.MESH

NT = (((1,), (1,)), ((), ()))
TN = (((0,), (0,)), ((), ()))


def _pick(n, pref, mult=LANES):
    if n <= pref:
        return n
    t = (pref // mult) * mult
    while t >= mult:
        if n % t == 0:
            return t
        t -= mult
    raise ValueError((n, pref, mult))


def _cp(*sem):
    return pltpu.CompilerParams(dimension_semantics=sem, vmem_limit_bytes=VMEM_LIMIT)


def _dot(a, b):
    return jnp.dot(a, b, preferred_element_type=F32)


def _dg(a, b, dims):
    return lax.dot_general(a, b, dims, preferred_element_type=F32)


def _split3(x):
    hi = x.astype(BF)
    r1 = x - hi.astype(F32)
    mid = r1.astype(BF)
    lo = (r1 - mid.astype(F32)).astype(BF)
    return hi, mid, lo


def _tri_dot(tri, x):
    hi, mid, lo = _split3(x)
    return _dot(tri, hi) + _dot(tri, mid) + _dot(tri, lo)


def _dg3(a, b, dims):
    ah, bh = a.astype(BF), b.astype(BF)
    al, bl = (a - ah.astype(F32)).astype(BF), (b - bh.astype(F32)).astype(BF)
    return _dg(ah, bh, dims) + _dg(ah, bl, dims) + _dg(al, bh, dims)


def _dg1(a, b, dims):
    return _dg(a.astype(BF), b.astype(BF), dims)


NN = (((1,), (0,)), ((), ()))


def _sigmoid(x):
    return jax.nn.sigmoid(x)


def _ln_matmul(x, nw, sc, sh, w, *, relu2, name):
    S, D = x.shape
    N = w.shape[1]
    tm, tn = _pick(S, 1024, 16), _pick(N, 1024)

    def body(x_ref, nw_ref, sc_ref, sh_ref, w_ref, *rest):
        outs, hs = rest[:-1], rest[-1]
        h_ref = outs[-1]

        @pl.when(pl.program_id(1) == 0)
        def _():
            xv = x_ref[...]
            r = lax.rsqrt(jnp.mean(xv * xv, axis=-1, keepdims=True) + EPS)
            hb = ((xv * r * nw_ref[...]) * (1.0 + sc_ref[...]) + sh_ref[...]).astype(BF)
            hs[...] = hb
            h_ref[...] = hb

        z = _dot(hs[...], w_ref[...])
        if relu2:
            a = jnp.maximum(z, 0.0)
            outs[0][...] = a.astype(BF)
            outs[1][...] = (a * a).astype(BF)
        else:
            outs[0][...] = z

    vec = pl.BlockSpec((1, D), lambda i, j: (0, 0))
    tile = pl.BlockSpec((tm, tn), lambda i, j: (i, j))
    if relu2:
        out_shape = [jax.ShapeDtypeStruct((S, N), BF), jax.ShapeDtypeStruct((S, N), BF)]
        out_specs = [tile, tile]
    else:
        out_shape = [jax.ShapeDtypeStruct((S, N), F32)]
        out_specs = [tile]
    out_shape.append(jax.ShapeDtypeStruct((S, D), BF))
    out_specs.append(pl.BlockSpec((tm, D), lambda i, j: (i, 0)))
    return pl.pallas_call(
        body, name=name, grid=(S // tm, N // tn),
        in_specs=[pl.BlockSpec((tm, D), lambda i, j: (i, 0)), vec, vec, vec,
                  pl.BlockSpec((D, tn), lambda i, j: (0, j))],
        out_specs=out_specs, out_shape=out_shape,
        scratch_shapes=[pltpu.VMEM((tm, D), BF)],
        compiler_params=_cp("parallel", "arbitrary"),
    )(x, nw, sc, sh, w)


def _matmul_resid(a, w, x, gate, *, name):
    S, K = a.shape
    D = w.shape[1]
    big = 1024 if K <= 1024 else 512
    tm, tn = _pick(S, big, 16), _pick(D, big)

    def body(a_ref, w_ref, x_ref, g_ref, o_ref, y_ref):
        y = _dot(a_ref[...], w_ref[...])
        y_ref[...] = y.astype(BF)
        o_ref[...] = x_ref[...] + g_ref[...] * y

    tile = pl.BlockSpec((tm, tn), lambda i, j: (i, j))
    return pl.pallas_call(
        body, name=name, grid=(S // tm, D // tn),
        in_specs=[pl.BlockSpec((tm, K), lambda i, j: (i, 0)), pl.BlockSpec((K, tn), lambda i, j: (0, j)),
                  tile, pl.BlockSpec((1, tn), lambda i, j: (0, j))],
        out_specs=[tile, tile],
        out_shape=[jax.ShapeDtypeStruct((S, D), F32), jax.ShapeDtypeStruct((S, D), BF)],
        compiler_params=_cp("parallel", "arbitrary"),
    )(a, w, x, gate)


def _gate_matmul_nt(dx, gate, y, w, act, *, name):
    S, D = dx.shape
    K = w.shape[0]
    tm, tn = _pick(S, 1024, 16), _pick(K, 1024)
    fused = act is not None

    def body(dx_ref, g_ref, y_ref, w_ref, *rest):
        if fused:
            act_ref, da_ref, dm_ref, dg_ref, ms = rest
        else:
            da_ref, dm_ref, dg_ref, ms = rest
        i, j = pl.program_id(0), pl.program_id(1)

        @pl.when((i == 0) & (j == 0))
        def _():
            dg_ref[...] = jnp.zeros_like(dg_ref)

        @pl.when(j == 0)
        def _():
            dxv = dx_ref[...]
            dmb = (dxv * g_ref[...]).astype(BF)
            ms[...] = dmb
            dm_ref[...] = dmb
            dg_ref[...] += jnp.sum(dxv * y_ref[...].astype(F32), axis=0, keepdims=True)

        da = _dg(ms[...], w_ref[...], NT)
        if fused:
            da_ref[...] = (da * (2.0 * act_ref[...].astype(F32))).astype(BF)
        else:
            da_ref[...] = da

    row = pl.BlockSpec((tm, D), lambda i, j: (i, 0))
    vec = pl.BlockSpec((1, D), lambda i, j: (0, 0))
    tile = pl.BlockSpec((tm, tn), lambda i, j: (i, j))
    in_specs = [row, vec, row, pl.BlockSpec((tn, D), lambda i, j: (j, 0))]
    args = [dx, gate, y, w]
    if fused:
        in_specs.append(tile)
        args.append(act)
    return pl.pallas_call(
        body, name=name, grid=(S // tm, K // tn),
        in_specs=in_specs, out_specs=[tile, row, vec],
        out_shape=[jax.ShapeDtypeStruct((S, K), BF if fused else F32), jax.ShapeDtypeStruct((S, D), BF),
                   jax.ShapeDtypeStruct((1, D), F32)],
        scratch_shapes=[pltpu.VMEM((tm, D), BF)],
        compiler_params=_cp("arbitrary", "arbitrary"),
    )(*args)


def _matmul_tn(a, b, *, name):
    S, Ka = a.shape
    P, _, Db = b.shape
    tk, tn, ts = _pick(Ka, 1024), _pick(Db, 1024), _pick(S, 1024, 16)
    npb = Db // tn

    def body(a_ref, b_ref, o_ref, acc):
        s = pl.program_id(2)

        @pl.when(s == 0)
        def _():
            acc[...] = jnp.zeros_like(acc)

        acc[...] += _dg(a_ref[...], b_ref[...], TN)

        @pl.when(s == pl.num_programs(2) - 1)
        def _():
            o_ref[...] = acc[...]

    return pl.pallas_call(
        body, name=name, grid=(Ka // tk, P * npb, S // ts),
        in_specs=[pl.BlockSpec((ts, tk), lambda i, j, s: (s, i)),
                  pl.BlockSpec((None, ts, tn), lambda i, j, s: (j // npb, s, j % npb))],
        out_specs=pl.BlockSpec((tk, tn), lambda i, j, s: (i, j)),
        out_shape=jax.ShapeDtypeStruct((Ka, P * Db), F32),
        scratch_shapes=[pltpu.VMEM((tk, tn), F32)],
        compiler_params=_cp("parallel", "parallel", "arbitrary"),
    )(a, b)


def _matmul_nt_lnbwd(g, w, x, nw, sc, dx_out, *, name):
    P, S, Dg = g.shape
    D = x.shape[1]
    tm, tk = _pick(S, 1024, 16), _pick(Dg, 1024)
    npb = Dg // tk
    nk = P * npb

    def body(g_ref, w_ref, x_ref, nw_ref, sc_ref, dxo_ref, dx_ref, dsc_ref, dsh_ref, dnw_ref, acc):
        i, k = pl.program_id(0), pl.program_id(1)

        @pl.when((i == 0) & (k == 0))
        def _():
            dsc_ref[...] = jnp.zeros_like(dsc_ref)
            dsh_ref[...] = jnp.zeros_like(dsh_ref)
            dnw_ref[...] = jnp.zeros_like(dnw_ref)

        @pl.when(k == 0)
        def _():
            acc[...] = jnp.zeros_like(acc)

        acc[...] += _dg(g_ref[...], w_ref[...], NT)

        @pl.when(k == nk - 1)
        def _():
            dh = acc[...]
            xv = x_ref[...]
            nwv = nw_ref[...]
            r = lax.rsqrt(jnp.mean(xv * xv, axis=-1, keepdims=True) + EPS)
            xr = xv * r
            dn = dh * (1.0 + sc_ref[...])
            dsc_ref[...] += jnp.sum(dh * (xr * nwv), axis=0, keepdims=True)
            dsh_ref[...] += jnp.sum(dh, axis=0, keepdims=True)
            dnw_ref[...] += jnp.sum(dn * xr, axis=0, keepdims=True)
            u = dn * nwv
            dx_ref[...] = dxo_ref[...] + r * (u - xr * jnp.mean(u * xr, axis=-1, keepdims=True))

    row = pl.BlockSpec((tm, D), lambda i, k: (i, 0))
    vec = pl.BlockSpec((1, D), lambda i, k: (0, 0))
    return pl.pallas_call(
        body, name=name, grid=(S // tm, nk),
        in_specs=[pl.BlockSpec((None, tm, tk), lambda i, k: (k // npb, i, k % npb)),
                  pl.BlockSpec((D, tk), lambda i, k: (0, k)), row, vec, vec, row],
        out_specs=[row, vec, vec, vec],
        out_shape=[jax.ShapeDtypeStruct((S, D), F32)] + [jax.ShapeDtypeStruct((1, D), F32)] * 3,
        scratch_shapes=[pltpu.VMEM((tm, D), F32)],
        compiler_params=_cp("arbitrary", "arbitrary"),
    )(g, w, x, nw, sc, dx_out)


def _loss_kernel(x, fw, tgt, *, name):
    S, D = x.shape
    tm = _pick(S, 512, 8)

    def body(x_ref, fw_ref, t_ref, l_ref, dx_ref, dfw_ref):
        @pl.when(pl.program_id(0) == 0)
        def _():
            l_ref[...] = jnp.zeros_like(l_ref)
            dfw_ref[...] = jnp.zeros_like(dfw_ref)

        xv = x_ref[...]
        fwv = fw_ref[...]
        r = lax.rsqrt(jnp.mean(xv * xv, axis=-1, keepdims=True) + EPS)
        xr = xv * r
        err = xr * fwv - t_ref[...]
        per_tok = jnp.mean(err * err, axis=-1, keepdims=True)
        l_ref[...] += 0.5 * jnp.sum(per_tok, axis=0, keepdims=True)
        dy = err * (1.0 / D)
        dfw_ref[...] += jnp.sum(dy * xr, axis=0, keepdims=True)
        u = dy * fwv
        dx_ref[...] = r * (u - xr * jnp.mean(u * xr, axis=-1, keepdims=True))

    row = pl.BlockSpec((tm, D), lambda i: (i, 0))
    vec = pl.BlockSpec((1, D), lambda i: (0, 0))
    return pl.pallas_call(
        body, name=name, grid=(S // tm,),
        in_specs=[row, vec, row],
        out_specs=[pl.BlockSpec((1, LANES), lambda i: (0, 0)), row, vec],
        out_shape=[jax.ShapeDtypeStruct((1, LANES), F32), jax.ShapeDtypeStruct((S, D), F32),
                   jax.ShapeDtypeStruct((1, D), F32)],
        compiler_params=_cp("arbitrary"),
    )(x, fw, tgt)


def _hg_lower_bound(lb3):
    mx = jnp.max(lb3, axis=0, keepdims=True)
    e = jnp.exp(lb3 - mx)
    p = e / jnp.sum(e, axis=0, keepdims=True)
    return p[0:1, :], p


def _hg_chunk_common(qr, fz, lbv):
    sq = _sigmoid(qr)
    q = qr * sq
    sig = _sigmoid(fz)
    f = lbv + (1.0 - lbv) * sig
    k = (1.0 - lbv) * (1.0 - sig)
    return q, sq, sig, f, k, jnp.log(f)


def _row_of(x, rows, r):
    return jnp.sum(jnp.where(rows == r, x, 0.0), axis=0, keepdims=True)


def _hg_fwd(proj, hg_lb, gn, slab=None, *, name):
    S = proj.shape[0]
    D = proj.shape[1] // 4
    H = D // LANES
    HB = min(HG_HEADS_PER_STEP, H)
    W = HB * LANES
    C = HG_CHUNK
    T = _pick(S, HG_TOKENS_PER_STEP, C)
    nch, nb = T // C, S // T
    ng = H // HB
    fused = slab is not None

    def body(q_ref, fz_ref, v_ref, g_ref, lb_ref, gn_ref, *rest):
        if fused:
            s_ref, y_ref, o_ref, sts_ref, out_ref, st, send_sems, recv_sems = rest
            first, passed, landed, from_sibling = _chip_slab_copies(s_ref, out_ref, send_sems, recv_sems)
            hgrp, n = pl.program_id(0), pl.program_id(1)

            @pl.when((hgrp == 0) & (n == 0))
            def _():
                for cp in first:
                    cp.start()

            @pl.when((hgrp == ng - 1) & (n == (3 * nb) // 4))
            def _():
                for arrived, onward in zip(landed, passed):
                    arrived.wait_recv()
                    onward.start()
        else:
            y_ref, o_ref, sts_ref, st = rest

        @pl.when(pl.program_id(1) == 0)
        def _():
            st[...] = jnp.zeros_like(st)

        lb_all, _ = _hg_lower_bound(lb_ref[...])
        gnv = gn_ref[...]
        ri = lax.broadcasted_iota(jnp.int32, (C, C), 0)
        ci_ = lax.broadcasted_iota(jnp.int32, (C, C), 1)
        low = ri >= ci_
        tri = jnp.where(low, 1.0, 0.0).astype(BF)
        rows = lax.broadcasted_iota(jnp.int32, (C, LANES), 0)

        def chunk(ci, carry):
            sl = pl.ds(pl.multiple_of(ci * C, C), C)
            for hh in range(HB):
                ls = slice(hh * LANES, (hh + 1) * LANES)
                q, _, _, _, k, logf = _hg_chunk_common(q_ref[sl, ls], fz_ref[sl, ls], lb_all[:, ls])
                vv = v_ref[sl, ls]
                gg = g_ref[sl, ls]
                G = _tri_dot(tri, logf)
                Gm = _row_of(G, rows, C // 2 - 1)
                Gl = _row_of(G, rows, C - 1)
                qt = q * jnp.exp(G - Gm)
                kt = k * jnp.exp(Gm - G)
                A = jnp.where(low, _dg1(qt, kt, NT), 0.0)
                Sv = st[hh]
                sts_ref[hh, ci] = Sv
                o = _dg1(A, vv, NN) + _dg1(q * jnp.exp(G), Sv, NT)
                st[hh] = Sv * jnp.exp(Gl) + _dg1(vv, k * jnp.exp(Gl - G), TN)
                r = lax.rsqrt(jnp.mean(o * o, axis=-1, keepdims=True) + EPS)
                y_ref[sl, ls] = ((o * r * gnv) * (gg * _sigmoid(gg))).astype(BF)
                o_ref[sl, ls] = o
            return carry

        lax.fori_loop(0, nch, chunk, 0)

        if fused:
            @pl.when((hgrp == ng - 1) & (n == nb - 1))
            def _():
                for cp in from_sibling:
                    cp.wait_recv()
                for cp in first + passed:
                    cp.wait_send()

    def part(p):
        return pl.BlockSpec((T, W), lambda h, n: (n, p * ng + h))

    blk = pl.BlockSpec((T, W), lambda h, n: (n, h))
    in_specs = [part(0), part(1), part(2), part(3),
                pl.BlockSpec((3, W), lambda h, n: (0, h)), pl.BlockSpec((1, LANES), lambda h, n: (0, 0))]
    out_specs = [blk, blk, pl.BlockSpec((HB, nch, LANES, LANES), lambda h, n: (h, n, 0, 0))]
    out_shape = [jax.ShapeDtypeStruct((S, D), BF), jax.ShapeDtypeStruct((S, D), F32),
                 jax.ShapeDtypeStruct((H, S // C, LANES, LANES), F32)]
    scratch = [pltpu.VMEM((HB, LANES, LANES), F32)]
    args = [proj, proj, proj, proj, hg_lb, gn]
    if fused:
        in_specs.append(HBM)
        out_specs.append(HBM)
        out_shape.append(jax.ShapeDtypeStruct((N_CHIPS,) + slab.shape, slab.dtype))
        scratch += [pltpu.SemaphoreType.DMA((6,)), pltpu.SemaphoreType.DMA((6,))]
        args.append(slab)
    return pl.pallas_call(
        body, name=name, grid=(ng, nb), in_specs=in_specs, out_specs=out_specs, out_shape=out_shape,
        scratch_shapes=scratch, compiler_params=_cp("arbitrary", "arbitrary"),
    )(*args)


def _hg_bwd(proj, hg_lb, gn, o_all, states, dy, *, name):
    S = proj.shape[0]
    D = proj.shape[1] // 4
    H = D // LANES
    HB = min(HG_HEADS_PER_STEP, H)
    W = HB * LANES
    C = HG_CHUNK
    T = _pick(S, HG_TOKENS_PER_STEP, C)
    nch, nb = T // C, S // T

    def body(q_ref, fz_ref, v_ref, g_ref, lb_ref, gn_ref, o_ref, sts_ref, dy_ref,
             dp_ref, dlb_ref, dgn_ref, dst, dlb_acc):
        n = pl.program_id(1)

        @pl.when(n == 0)
        def _():
            dst[...] = jnp.zeros_like(dst)
            dlb_acc[...] = jnp.zeros_like(dlb_acc)
            dgn_ref[...] = jnp.zeros_like(dgn_ref)

        lb_all, p3 = _hg_lower_bound(lb_ref[...])
        gnv = gn_ref[...]
        ri = lax.broadcasted_iota(jnp.int32, (C, C), 0)
        ci_ = lax.broadcasted_iota(jnp.int32, (C, C), 1)
        low = ri >= ci_
        tri = jnp.where(low, 1.0, 0.0).astype(BF)
        triu = jnp.where(ri <= ci_, 1.0, 0.0).astype(BF)
        rows = lax.broadcasted_iota(jnp.int32, (C, LANES), 0)

        def chunk(cj, carry):
            ci = nch - 1 - cj
            sl = pl.ds(pl.multiple_of(ci * C, C), C)
            for hh in range(HB):
                ls = slice(hh * LANES, (hh + 1) * LANES)
                lbv = lb_all[:, ls]
                qr = q_ref[sl, ls]
                q, sq, sig, f, k, logf = _hg_chunk_common(qr, fz_ref[sl, ls], lbv)
                vv = v_ref[sl, ls]
                gg = g_ref[sl, ls]
                o = o_ref[sl, ls]
                dyv = dy_ref[sl, ls]
                G = _tri_dot(tri, logf)
                Gm = _row_of(G, rows, C // 2 - 1)
                Gl = _row_of(G, rows, C - 1)
                eG, e_qm, e_km, e_lk, eGl = jnp.exp(G), jnp.exp(G - Gm), jnp.exp(Gm - G), jnp.exp(Gl - G), jnp.exp(Gl)
                qt = q * e_qm
                kt = k * e_km
                A = jnp.where(low, _dg1(qt, kt, NT), 0.0)
                sg = _sigmoid(gg)
                r = lax.rsqrt(jnp.mean(o * o, axis=-1, keepdims=True) + EPS)
                on = o * r
                d_onw = dyv * (gg * sg)
                dgn_ref[hh] += jnp.sum(d_onw * on, axis=0, keepdims=True)
                dgg = dyv * (on * gnv) * (sg * (1.0 + gg * (1.0 - sg)))
                u = d_onw * gnv
                do = r * (u - on * jnp.mean(u * on, axis=-1, keepdims=True))
                Sv = sts_ref[hh, ci]
                dSv = dst[hh]
                dA = jnp.where(low, _dg3(do, vv, NT), 0.0)
                kdec = k * e_lk
                dv = _dg1(A, do, TN) + _dg1(kdec, dSv, NT)
                dq = _dg3(dA, kt, NN) * e_qm + eG * _dg3(do, Sv, NN)
                dk = _dg3(dA, qt, TN) * e_km + e_lk * _dg3(vv, dSv, NN)
                s_end = Sv * eGl + _dg3(vv, kdec, TN)
                dgl = jnp.sum(dSv * s_end, axis=0, keepdims=True)
                dG = q * dq - k * dk + jnp.where(rows == C - 1, dgl, 0.0)
                dlogf = _tri_dot(triu, dG) - f * dk
                dst[hh] = dSv * eGl + _dg1(do, q * eG, TN)
                dlf_f = dlogf / f
                dlb_acc[:, ls] += jnp.sum(dlf_f * (1.0 - sig), axis=0, keepdims=True)
                dp_ref[0, sl, ls] = (dq * (sq * (1.0 + qr * (1.0 - sq)))).astype(BF)
                dp_ref[1, sl, ls] = (dlf_f * (1.0 - lbv) * sig * (1.0 - sig)).astype(BF)
                dp_ref[2, sl, ls] = dv.astype(BF)
                dp_ref[3, sl, ls] = dgg.astype(BF)
            return carry

        lax.fori_loop(0, nch, chunk, 0)
        sel = jnp.where(lax.broadcasted_iota(jnp.int32, (3, W), 0) == 0, 1.0, 0.0)
        dlb_ref[...] = lb_all * (sel - p3) * dlb_acc[...]

    ng = H // HB

    def part(p):
        return pl.BlockSpec((T, W), lambda h, n: (nb - 1 - n, p * ng + h))

    blk = pl.BlockSpec((T, W), lambda h, n: (nb - 1 - n, h))
    return pl.pallas_call(
        body, name=name, grid=(ng, nb),
        in_specs=[part(0), part(1), part(2), part(3),
                  pl.BlockSpec((3, W), lambda h, n: (0, h)), pl.BlockSpec((1, LANES), lambda h, n: (0, 0)),
                  blk, pl.BlockSpec((HB, nch, LANES, LANES), lambda h, n: (h, nb - 1 - n, 0, 0)), blk],
        out_specs=[pl.BlockSpec((4, T, W), lambda h, n: (0, nb - 1 - n, h)),
                   pl.BlockSpec((3, W), lambda h, n: (0, h)),
                   pl.BlockSpec((HB, 1, LANES), lambda h, n: (h, 0, 0))],
        out_shape=[jax.ShapeDtypeStruct((4, S, D), BF), jax.ShapeDtypeStruct((3, D), F32),
                   jax.ShapeDtypeStruct((H, 1, LANES), F32)],
        scratch_shapes=[pltpu.VMEM((HB, LANES, LANES), F32), pltpu.VMEM((1, W), F32)],
        compiler_params=_cp("parallel", "arbitrary"),
    )(proj, proj, proj, proj, hg_lb, gn, o_all, states, dy)


def _log_sigmoid(u):
    return jnp.minimum(u, 0.0) - jnp.log(1.0 + jnp.exp(-jnp.abs(u)))


def _lane_put(base, lane, first, pieces):
    for n, p in enumerate(pieces):
        base = jnp.where(lane == first + n, p, base)
    return base


def _fox_cumsum(proj, bf_pad, *, name):
    S = proj.shape[0]
    D = proj.shape[1] // 5
    T = _pick(S, 256, 8)

    def body(fz_ref, b_ref, f_ref, carry):
        @pl.when(pl.program_id(0) == 0)
        def _():
            carry[...] = jnp.zeros_like(carry)

        logf = _log_sigmoid(fz_ref[...] + b_ref[...])
        tri = jnp.where(lax.broadcasted_iota(jnp.int32, (T, T), 0) >= lax.broadcasted_iota(jnp.int32, (T, T), 1),
                        1.0, 0.0).astype(BF)
        fv = _tri_dot(tri, logf) + carry[...]
        f_ref[...] = fv
        carry[...] = _row_of(fv, lax.broadcasted_iota(jnp.int32, (T, LANES), 0), T - 1)

    return pl.pallas_call(
        body, name=name, grid=(S // T,),
        in_specs=[pl.BlockSpec((T, LANES), lambda i: (i, 4 * D // LANES)), pl.BlockSpec((1, LANES), lambda i: (0, 0))],
        out_specs=pl.BlockSpec((T, LANES), lambda i: (i, 0)),
        out_shape=jax.ShapeDtypeStruct((S, LANES), F32),
        scratch_shapes=[pltpu.VMEM((1, LANES), F32)],
        compiler_params=_cp("arbitrary"),
    )(proj, bf_pad)


def _pair_stats(sq, lo):
    del lo
    a = lax.broadcasted_iota(jnp.int32, (LANES, LANES), 0) < FOX_DH
    b = lax.broadcasted_iota(jnp.int32, (LANES, LANES), 1) < FOX_DH
    avg = jnp.where(a == b, 1.0 / FOX_DH, 0.0).astype(BF)
    hi, mid, low = _split3(sq)
    return _dot(hi, avg) + _dot(mid, avg) + _dot(low, avg)


def _fox_prep(proj, fcum, qw2, kw2, *, name):
    S = proj.shape[0]
    D = proj.shape[1] // 5
    HP = D // LANES
    T = _pick(S, 512, 16)

    def body(q_ref, k_ref, v_ref, f_ref, qw_ref, kw_ref, qa_ref, ka_ref, va_ref, vt_ref):
        hp = pl.program_id(1)
        lane = lax.broadcasted_iota(jnp.int32, (T, LANES), 1)
        lo = lane < FOX_DH
        qv, kv, vv, fv = q_ref[...], k_ref[...], v_ref[...], f_ref[...]
        qn = qv * lax.rsqrt(_pair_stats(qv * qv, lo) + EPS) * qw_ref[...] * (0.125 * LOG2E)
        kn = kv * lax.rsqrt(_pair_stats(kv * kv, lo) + EPS) * kw_ref[...]
        ones_q = jnp.where((lane >= 67) & (lane <= 69), 1.0, 0.0)
        ones_k = jnp.where(((lane >= 64) & (lane <= 66)) | ((lane >= 70) & (lane <= 72)), 1.0, 0.0)
        ones_v = jnp.where((lane >= 64) & (lane <= 66), 1.0, 0.0)
        for hh in range(2):
            fh = jnp.sum(jnp.where(lane == 2 * hp + hh, fv, 0.0), axis=-1, keepdims=True) * LOG2E
            pieces = [p.astype(F32) for p in _split3(fh)]

            def half(x):
                return jnp.where(lo, x if hh == 0 else pltpu.roll(x, FOX_DH, 1), 0.0)

            qa_ref[hh] = _lane_put(half(qn) + ones_q, lane, 64, pieces).astype(BF)
            ka_ref[hh] = _lane_put(half(kn) + ones_k, lane, 67, [-p for p in pieces]).astype(BF)
            va = half(vv) + ones_v
            va_ref[hh] = va.astype(BF)
            vt_ref[hh] = va.T.astype(BF)

    def part(p):
        return pl.BlockSpec((T, LANES), lambda i, hp: (i, p * HP + hp))

    vec = pl.BlockSpec((1, LANES), lambda i, hp: (0, 0))
    aug = pl.BlockSpec((2, T, LANES), lambda i, hp: (hp, i, 0))
    return pl.pallas_call(
        body, name=name, grid=(S // T, HP),
        in_specs=[part(0), part(1), part(2), pl.BlockSpec((T, LANES), lambda i, hp: (i, 0)), vec, vec],
        out_specs=[aug, aug, aug, pl.BlockSpec((2, LANES, T), lambda i, hp: (hp, 0, i))],
        out_shape=[jax.ShapeDtypeStruct((2 * HP, S, LANES), BF)] * 3 + [jax.ShapeDtypeStruct((2 * HP, LANES, S), BF)],
        compiler_params=_cp("parallel", "arbitrary"),
    )(proj, proj, proj, fcum, qw2, kw2)


def _fox_block(S):
    return _pick(S, 256, 16)


def _fox_skip_bounds(fcum, qn_w, kn_w, nheads):
    S = fcum.shape[0]
    B = _fox_block(S)
    qk = 8.0 * LOG2E * 1.02 * jnp.max(jnp.abs(qn_w)) * jnp.max(jnp.abs(kn_w))
    thresh = -(2.0 * qk + 160.0)
    f2 = fcum[:, :nheads] * LOG2E
    first, last = f2[0::B], f2[B - 1::B]
    nb = S // B
    blk = jnp.arange(nb)
    dead = (first[0::2, None, :] - last[None, :, :]) < thresh
    jmin = jnp.sum(dead & (blk[None, :, None] < 2 * jnp.arange(nb // 2)[:, None, None]), axis=1)
    live = (first[:, None, :] - last[None, :, :]) >= thresh
    imax = blk[:, None] + jnp.sum(live & (blk[:, None, None] > blk[None, :, None]), axis=0)
    return jmin.T.astype(jnp.int32), imax.T.astype(jnp.int32)


def _fox_fwd(jmin, qa, ka, vat, proj, *, name):
    H, S, _ = qa.shape
    HP = H // 2
    D = HP * LANES
    B = _fox_block(S)
    BQ = 2 * B
    nq = S // BQ

    def body(jmin_ref, q_ref, k_ref, vt_ref, g_ref, y_ref, o_ref, q2_ref):
        hp, i = pl.program_id(0), pl.program_id(1)
        lane = lax.broadcasted_iota(jnp.int32, (BQ, LANES), 1)
        lo = lane < FOX_DH
        in_lse = (lane >= 70) & (lane <= 72)
        causal = lax.broadcasted_iota(jnp.int32, (BQ, BQ), 0) <= lax.broadcasted_iota(jnp.int32, (BQ, BQ), 1)
        row = lax.broadcasted_iota(jnp.int32, (LANES, BQ), 0)
        m0, acc0 = jnp.full((1, BQ), -jnp.inf, F32), jnp.zeros((LANES, BQ), F32)
        outs = []
        for hh in range(2):
            qb = q_ref[hh]

            def block(j, carry, masked=False):
                m, acc = carry
                sl = pl.ds(pl.multiple_of(j * BQ, BQ), BQ)
                st = _dg(k_ref[hh, sl, :], qb, NT)
                if masked:
                    st = jnp.where(causal, st, -jnp.inf)
                m_new = jnp.maximum(m, jnp.max(st, axis=0, keepdims=True))
                p = jnp.exp2(st - m_new)
                ph = p.astype(BF)
                pl_ = (p - ph.astype(F32)).astype(BF)
                vt = vt_ref[hh, :, sl]
                pv = _dot(jnp.concatenate([vt, vt], axis=1), jnp.concatenate([ph, pl_], axis=0))
                return m_new, acc * jnp.exp2(m - m_new) + pv

            carry = lax.fori_loop(jmin_ref[2 * hp + hh, i] // 2, i, block, (m0, acc0))
            m, acc = block(i, carry, masked=True)
            l = jnp.sum(jnp.where(row == FOX_DH, acc, 0.0), axis=0, keepdims=True)
            tile = acc / l
            for n, piece in enumerate(_split3(m + jnp.log2(l))):
                tile = jnp.where(row == 70 + n, -(piece.astype(F32)), tile)
            tile = tile.T
            outs.append(tile)
            q2_ref[hh] = jnp.where(in_lse, tile, qb.astype(F32)).astype(BF)
        o = jnp.where(lo, outs[0], pltpu.roll(outs[1], FOX_DH, 1))
        o_ref[...] = o
        y_ref[...] = (o * _sigmoid(g_ref[...])).astype(BF)

    blk = pl.BlockSpec((BQ, LANES), lambda hp, i, jm: (i, hp))
    qblk = pl.BlockSpec((2, BQ, LANES), lambda hp, i, jm: (hp, i, 0))
    full = pl.BlockSpec((2, S, LANES), lambda hp, i, jm: (hp, 0, 0))
    full_t = pl.BlockSpec((2, LANES, S), lambda hp, i, jm: (hp, 0, 0))
    return pl.pallas_call(
        body, name=name,
        grid_spec=pltpu.PrefetchScalarGridSpec(
            num_scalar_prefetch=1, grid=(HP, nq),
            in_specs=[qblk, full, full_t, pl.BlockSpec((BQ, LANES), lambda hp, i, jm: (i, 3 * HP + hp))],
            out_specs=[blk, blk, qblk]),
        out_shape=[jax.ShapeDtypeStruct((S, D), BF), jax.ShapeDtypeStruct((S, D), F32),
                   jax.ShapeDtypeStruct((H, S, LANES), BF)],
        compiler_params=_cp("parallel", "arbitrary"),
    )(jmin, qa, ka, vat, proj)


def _fox_bwd_prep(dy, o, proj, *, name):
    S, D = dy.shape
    HP = D // LANES
    T = _pick(S, 512, 16)

    def body(dy_ref, o_ref, g_ref, da_ref):
        lane = lax.broadcasted_iota(jnp.int32, (T, LANES), 1)
        lo = lane < FOX_DH
        do = (dy_ref[...] * _sigmoid(g_ref[...])).astype(BF).astype(F32)
        prod = do * o_ref[...]
        d_lo = jnp.sum(jnp.where(lo, prod, 0.0), axis=-1, keepdims=True)
        d_hi = jnp.sum(jnp.where(lo, 0.0, prod), axis=-1, keepdims=True)
        for hh, delta in enumerate((d_lo, d_hi)):
            base = jnp.where(lo, do if hh == 0 else pltpu.roll(do, FOX_DH, 1), 0.0)
            da_ref[hh] = _lane_put(base, lane, 64, [-(p.astype(F32)) for p in _split3(delta)]).astype(BF)

    blk = pl.BlockSpec((T, LANES), lambda i, hp: (i, hp))
    return pl.pallas_call(
        body, name=name, grid=(S // T, HP),
        in_specs=[blk, blk, pl.BlockSpec((T, LANES), lambda i, hp: (i, 3 * HP + hp))],
        out_specs=pl.BlockSpec((2, T, LANES), lambda i, hp: (hp, i, 0)),
        out_shape=jax.ShapeDtypeStruct((2 * HP, S, LANES), BF),
        compiler_params=_cp("parallel", "arbitrary"),
    )(dy, o, proj)


def _fox_bwd(imax, q2, ka, va, doa, *, name):
    H, S, _ = q2.shape
    B = _fox_block(S)
    nb = S // B

    def body(imax_ref, q_ref, do_ref, k_ref, v_ref, dq_ref, dk_ref, dv_ref, cs_ref):
        j = pl.program_id(1)
        end = imax_ref[pl.program_id(0), j] + 1

        @pl.when(j == 0)
        def _():
            dq_ref[...] = jnp.zeros_like(dq_ref)

        kb, vb = k_ref[...], v_ref[...]

        def step(i, carry, nblk=1):
            dk_acc, dv_acc, cs_acc = carry
            rows = nblk * B
            sl = pl.ds(pl.multiple_of(i * B, B), rows)
            qb, dob = q_ref[sl, :], do_ref[sl, :]
            s = _dg(qb, kb, NT)
            ahead = lax.broadcasted_iota(jnp.int32, (rows, B), 0) - lax.broadcasted_iota(jnp.int32, (rows, B), 1)
            p = jnp.exp2(jnp.where(ahead >= (j - i) * B, s, -jnp.inf))
            ds = p * _dg(dob, vb, NT)
            dsb = ds.astype(BF)
            cs_acc = cs_acc + jnp.sum(ds.reshape(rows // 8, 8, B), axis=0)
            dv_acc = dv_acc + _dg(p.astype(BF), dob, TN)
            dk_acc = dk_acc + _dg(dsb, qb, TN)
            dq_ref[sl, :] += _dot(dsb, kb)
            return dk_acc, dv_acc, cs_acc

        zero = jnp.zeros((B, LANES), F32)
        carry = (zero, zero, jnp.zeros((8, B), F32))
        pos = j
        for U in FOX_BWD_TILES:
            n = (end - pos) // U
            carry = lax.fori_loop(0, n, lambda ii, c, pos=pos, U=U: step(pos + U * ii, c, nblk=U), carry)
            pos = pos + U * n
        dk_acc, dv_acc, cs_acc = carry
        dk_ref[...] = dk_acc
        dv_ref[...] = dv_acc
        cs_ref[...] = jnp.sum(cs_acc, axis=0, keepdims=True)

    full = pl.BlockSpec((None, S, LANES), lambda h, j, im: (h, 0, 0))
    blk = pl.BlockSpec((None, B, LANES), lambda h, j, im: (h, j, 0))
    return pl.pallas_call(
        body, name=name,
        grid_spec=pltpu.PrefetchScalarGridSpec(
            num_scalar_prefetch=1, grid=(H, nb),
            in_specs=[full, full, blk, blk],
            out_specs=[full, blk, blk, pl.BlockSpec((None, 1, B), lambda h, j, im: (h, 0, j))]),
        out_shape=[jax.ShapeDtypeStruct((H, S, LANES), F32)] * 3 + [jax.ShapeDtypeStruct((H, 1, S), F32)],
        compiler_params=_cp("parallel", "arbitrary"),
    )(imax, q2, doa, ka, va)


def _fox_bwd_post(dqa, dka, dva, proj, dy, o, qw2, kw2, *, name):
    S, D = dy.shape
    HP = D // LANES
    T = _pick(S, 512, 16)

    def body(dq_ref, dk_ref, dv_ref, q_ref, k_ref, g_ref, dy_ref, o_ref, qw_ref, kw_ref, dp_ref, dqw_ref, dkw_ref):
        @pl.when((pl.program_id(0) == 0) & (pl.program_id(1) == 0))
        def _():
            dqw_ref[...] = jnp.zeros_like(dqw_ref)
            dkw_ref[...] = jnp.zeros_like(dkw_ref)

        lane = lax.broadcasted_iota(jnp.int32, (T, LANES), 1)
        lo = lane < FOX_DH

        def pair(ref):
            return jnp.where(lo, ref[0], pltpu.roll(ref[1], FOX_DH, 1))

        def norm_bwd(xv, w, dyn, dw_ref):
            r = lax.rsqrt(_pair_stats(xv * xv, lo) + EPS)
            xr = xv * r
            dw_ref[...] += jnp.sum(dyn * xr, axis=0, keepdims=True)
            u = dyn * w
            return r * (u - xr * _pair_stats(u * xr, lo))

        dp_ref[0] = norm_bwd(q_ref[...], qw_ref[...], pair(dq_ref) * 0.125, dqw_ref).astype(BF)
        dp_ref[1] = norm_bwd(k_ref[...], kw_ref[...], pair(dk_ref) * (1.0 / LOG2E), dkw_ref).astype(BF)
        dp_ref[2] = pair(dv_ref).astype(BF)
        sg = _sigmoid(g_ref[...])
        dp_ref[3] = (dy_ref[...] * o_ref[...] * sg * (1.0 - sg)).astype(BF)

    def part(p):
        return pl.BlockSpec((T, LANES), lambda i, hp: (i, p * HP + hp))

    aug = pl.BlockSpec((2, T, LANES), lambda i, hp: (hp, i, 0))
    blk = pl.BlockSpec((T, LANES), lambda i, hp: (i, hp))
    vec = pl.BlockSpec((1, LANES), lambda i, hp: (0, 0))
    return pl.pallas_call(
        body, name=name, grid=(S // T, HP),
        in_specs=[aug, aug, aug, part(0), part(1), part(3), blk, blk, vec, vec],
        out_specs=[pl.BlockSpec((4, T, LANES), lambda i, hp: (0, i, hp)), vec, vec],
        out_shape=[jax.ShapeDtypeStruct((5, S, D), BF), jax.ShapeDtypeStruct((1, LANES), F32),
                   jax.ShapeDtypeStruct((1, LANES), F32)],
        compiler_params=_cp("arbitrary", "arbitrary"),
    )(dqa, dka, dva, proj, proj, proj, dy, o, qw2, kw2)


def _fox_dfz(colsum, nheads, proj, bf_pad, dproj, *, name):
    S = colsum.shape[0]
    H = nheads
    D = dproj.shape[2]
    T = _pick(S, 256, 16)
    nb = S // T

    def body(cs_ref, fz_ref, b_ref, _, dp_ref, db_ref, carry):
        @pl.when(pl.program_id(0) == 0)
        def _():
            carry[...] = jnp.zeros_like(carry)
            db_ref[...] = jnp.zeros_like(db_ref)

        lane = lax.broadcasted_iota(jnp.int32, (T, LANES), 1)
        df = -cs_ref[...]
        triu = jnp.where(lax.broadcasted_iota(jnp.int32, (T, T), 0) <= lax.broadcasted_iota(jnp.int32, (T, T), 1),
                         1.0, 0.0).astype(BF)
        dlogf = _tri_dot(triu, df) + carry[...]
        carry[...] = _row_of(dlogf, lax.broadcasted_iota(jnp.int32, (T, LANES), 0), 0)
        dfz = jnp.where(lane < H, dlogf * _sigmoid(-(fz_ref[...] + b_ref[...])), 0.0)
        db_ref[...] += jnp.sum(dfz, axis=0, keepdims=True)
        dp_ref[...] = jnp.zeros_like(dp_ref)
        dp_ref[:, 0:LANES] = dfz.astype(BF)

    return pl.pallas_call(
        body, name=name, grid=(nb,),
        in_specs=[pl.BlockSpec((T, LANES), lambda i: (nb - 1 - i, 0)),
                  pl.BlockSpec((T, LANES), lambda i: (nb - 1 - i, 4 * D // LANES)),
                  pl.BlockSpec((1, LANES), lambda i: (0, 0)),
                  pl.BlockSpec(memory_space=pl.ANY)],
        out_specs=[pl.BlockSpec((None, T, D), lambda i: (4, nb - 1 - i, 0)), pl.BlockSpec((1, LANES), lambda i: (0, 0))],
        out_shape=[jax.ShapeDtypeStruct(dproj.shape, BF), jax.ShapeDtypeStruct((1, LANES), F32)],
        scratch_shapes=[pltpu.VMEM((1, LANES), F32)],
        input_output_aliases={3: 0},
        compiler_params=_cp("arbitrary"),
    )(colsum, proj, bf_pad, dproj)


def _mod_fwd(c16, w, b, *, name):
    L, D, N = w.shape
    tn = _pick(N, 512)

    def body(c_ref, w_ref, b_ref, o_ref):
        cv = c_ref[...]
        ca = (cv * _sigmoid(cv)).astype(BF)
        o_ref[...] = _dot(ca, w_ref[...].astype(BF)) + b_ref[...]

    return pl.pallas_call(
        body, name=name, grid=(L, N // tn),
        in_specs=[pl.BlockSpec((16, D), lambda l, j: (0, 0)), pl.BlockSpec((None, D, tn), lambda l, j: (l, 0, j)),
                  pl.BlockSpec((None, 1, tn), lambda l, j: (l, 0, j))],
        out_specs=pl.BlockSpec((None, 16, tn), lambda l, j: (l, 0, j)),
        out_shape=jax.ShapeDtypeStruct((L, 16, N), F32),
        compiler_params=_cp("parallel", "arbitrary"),
    )(c16, w, b)


def _mod_bwd(c16, dmod, *, name):
    L, _, N = dmod.shape
    D = c16.shape[1]
    tn = _pick(N, 512)

    def body(c_ref, d_ref, o_ref):
        cv = c_ref[...]
        ca = (cv * _sigmoid(cv)).astype(BF)
        o_ref[...] = _dg(ca, d_ref[...].astype(BF), TN)

    return pl.pallas_call(
        body, name=name, grid=(L, N // tn),
        in_specs=[pl.BlockSpec((16, D), lambda l, j: (0, 0)), pl.BlockSpec((None, 16, tn), lambda l, j: (l, 0, j))],
        out_specs=pl.BlockSpec((None, D, tn), lambda l, j: (l, 0, j)),
        out_shape=jax.ShapeDtypeStruct((L, D, N), F32),
        compiler_params=_cp("parallel", "arbitrary"),
    )(c16, dmod)


def _adamw_math(w, g, m, v):
    m = ADAM_B1 * m + (1.0 - ADAM_B1) * g
    v = ADAM_B2 * v + (1.0 - ADAM_B2) * (g * g)
    m_hat = m / (1.0 - ADAM_B1 ** ADAM_STEP)
    v_hat = v / (1.0 - ADAM_B2 ** ADAM_STEP)
    return -ADAM_LR * (m_hat / (jnp.sqrt(v_hat) + ADAM_EPS) + ADAM_WD * w), m, v


def _adamw(w, g, m, v, *, g_at=None, name):
    R, C = w.shape
    row0 = 0 if g_at is None else g_at[1]
    tr = min(math.gcd(row0, 256) if row0 else 256, -(-R // 8) * 8)
    g0 = row0 // tr
    if g_at is None:
        g_spec = pl.BlockSpec((tr, C), lambda i: (i, 0))
    else:
        g_spec = pl.BlockSpec((None, tr, C), lambda i: (g_at[0], g0 + i, 0))

    def body(w_ref, g_ref, m_ref, v_ref, d_ref, mo_ref, vo_ref):
        d, mn, vn = _adamw_math(w_ref[...], g_ref[...], m_ref[...], v_ref[...])
        d_ref[...] = d
        mo_ref[...] = mn
        vo_ref[...] = vn

    blk = pl.BlockSpec((tr, C), lambda i: (i, 0))
    return pl.pallas_call(
        body, name=name, grid=(pl.cdiv(R, tr),),
        in_specs=[blk, g_spec, blk, blk],
        out_specs=[blk, blk, blk],
        out_shape=[jax.ShapeDtypeStruct((R, C), F32)] * 3,
        compiler_params=_cp("parallel"),
    )(w, g, m, v)


def _sum_parts(parts, *, name):
    P, R, C = parts.shape

    def body(p_ref, o_ref):
        acc = p_ref[0]
        for p in range(1, P):
            acc = acc + p_ref[p]
        o_ref[...] = acc

    return pl.pallas_call(
        body, name=name, grid=(1,),
        in_specs=[pl.BlockSpec((P, R, C), lambda i: (0, 0, 0))],
        out_specs=pl.BlockSpec((R, C), lambda i: (0, 0)),
        out_shape=jax.ShapeDtypeStruct((R, C), F32),
        compiler_params=_cp("arbitrary"),
    )(parts)


def _add_halves(g4, recv, c_idx, *, name):
    _, _, Rh, C = g4.shape
    tr = _pick(Rh, 256, 16)

    def body(c_ref, a_ref, b_ref, o_ref):
        o_ref[...] = (a_ref[...] + b_ref[...].astype(F32)).astype(BF)

    return pl.pallas_call(
        body, name=name,
        grid_spec=pltpu.PrefetchScalarGridSpec(
            num_scalar_prefetch=1, grid=(4, pl.cdiv(Rh, tr)),
            in_specs=[pl.BlockSpec((None, None, tr, C), lambda j, r, c: (j, c[0], r, 0)),
                      pl.BlockSpec((None, tr, C), lambda j, r, c: (j, r, 0))],
            out_specs=pl.BlockSpec((None, tr, C), lambda j, r, c: (j, r, 0))),
        out_shape=jax.ShapeDtypeStruct((4, Rh, C), BF),
        compiler_params=_cp("parallel", "arbitrary"),
    )(c_idx, g4, recv)


def _add_four(g4, from_sibling, from_chips, pos, *, name):
    _, _, Rh, C = g4.shape
    tr = _pick(Rh, 256, 16)

    def body(p_ref, a_ref, s_ref, b_ref, o_ref):
        own = a_ref[...] + s_ref[...].astype(F32)
        o_ref[...] = ((own + b_ref[0].astype(F32)) + b_ref[1].astype(F32)) + b_ref[2].astype(F32)

    return pl.pallas_call(
        body, name=name,
        grid_spec=pltpu.PrefetchScalarGridSpec(
            num_scalar_prefetch=1, grid=(pl.cdiv(Rh, tr),),
            in_specs=[pl.BlockSpec((None, None, tr, C), lambda r, p: (p[0], p[1], r, 0)),
                      pl.BlockSpec((None, tr, C), lambda r, p: (p[0], r, 0)),
                      pl.BlockSpec((3, tr, C), lambda r, p: (0, r, 0))],
            out_specs=pl.BlockSpec((None, tr, C), lambda r, p: (p[1], r, 0))),
        out_shape=jax.ShapeDtypeStruct((2, Rh, C), F32),
        compiler_params=_cp("arbitrary"),
    )(pos, g4, from_sibling, from_chips)


HBM = pl.BlockSpec(memory_space=pltpu.HBM)


def _mesh_pos():
    return lax.axis_index("x"), lax.axis_index("y"), lax.axis_index("c")


def _other_chips(x, y):
    return [(1 - x, y), (x, 1 - y), (1 - x, 1 - y)]


def _allgather_small(xs, *, name):
    m_per, n = xs.shape

    def body(x_ref, out_ref, send_sems, recv_sems, local_sem):
        x, y, c = _mesh_pos()
        me, sibling = (x, y, c), (x, y, 1 - c)
        chips = _other_chips(x, y)

        def rows(px, py, pc):
            return out_ref.at[pl.ds((4 * px + 2 * py + pc) * m_per, m_per), :]

        def copy(k, block, to, src=None):
            return pltpu.make_async_remote_copy(
                src_ref=rows(*block) if src is None else src, dst_ref=rows(*block),
                send_sem=send_sems.at[k], recv_sem=recv_sems.at[k], device_id=to, device_id_type=MESH)

        mine = pltpu.make_async_copy(x_ref, rows(*me), local_sem)
        mine.start()
        first = [copy(0, me, sibling, src=x_ref)]
        first += [copy(1 + j, me, (*chip, c), src=x_ref) for j, chip in enumerate(chips)]
        for cp in first:
            cp.start()
        passed = [copy(4 + j, (*chip, c), sibling) for j, chip in enumerate(chips)]
        for j, chip in enumerate(chips):
            copy(1 + j, (*chip, c), me).wait_recv()
            passed[j].start()
        copy(0, sibling, me).wait_recv()
        for j, chip in enumerate(chips):
            copy(4 + j, (*chip, 1 - c), me).wait_recv()
        for cp in first + passed:
            cp.wait_send()
        mine.wait()

    return pl.pallas_call(
        body, name=name,
        out_shape=jax.ShapeDtypeStruct((N_DEV * m_per, n), xs.dtype),
        in_specs=[pl.BlockSpec(memory_space=pltpu.VMEM)],
        out_specs=pl.BlockSpec(memory_space=pltpu.VMEM),
        scratch_shapes=[pltpu.SemaphoreType.DMA((7,)), pltpu.SemaphoreType.DMA((7,)), pltpu.SemaphoreType.DMA],
    )(xs)


def _chip_slab_copies(s_ref, out_ref, send_sems, recv_sems):
    R = s_ref.shape[0]
    Rh = R // 2
    x, y, c = _mesh_pos()
    me, sibling = (x, y, c), (x, y, 1 - c)
    chips = _other_chips(x, y)

    def half(px, py, pc):
        return out_ref.at[2 * px + py, pl.ds(pc * Rh, Rh), :]

    def copy(k, block, to, src=None):
        return pltpu.make_async_remote_copy(
            src_ref=half(*block) if src is None else src, dst_ref=half(*block),
            send_sem=send_sems.at[k], recv_sem=recv_sems.at[k], device_id=to, device_id_type=MESH)

    first = [copy(j, me, (*chip, c), src=s_ref.at[pl.ds(c * Rh, Rh), :]) for j, chip in enumerate(chips)]
    passed = [copy(3 + j, (*chip, c), sibling) for j, chip in enumerate(chips)]
    landed = [copy(j, (*chip, c), me) for j, chip in enumerate(chips)]
    from_sibling = [copy(3 + j, (*chip, 1 - c), me) for j, chip in enumerate(chips)]
    return first, passed, landed, from_sibling


def _allgather_chip_slabs(slab, *, name):
    R, C = slab.shape

    def body(s_ref, out_ref, send_sems, recv_sems):
        first, passed, landed, from_sibling = _chip_slab_copies(s_ref, out_ref, send_sems, recv_sems)
        for cp in first:
            cp.start()
        for arrived, onward in zip(landed, passed):
            arrived.wait_recv()
            onward.start()
        for cp in from_sibling:
            cp.wait_recv()
        for cp in first + passed:
            cp.wait_send()

    return pl.pallas_call(
        body, name=name,
        out_shape=jax.ShapeDtypeStruct((N_CHIPS, R, C), slab.dtype),
        in_specs=[HBM], out_specs=HBM,
        scratch_shapes=[pltpu.SemaphoreType.DMA((6,)), pltpu.SemaphoreType.DMA((6,))],
    )(slab)


def _swap_halves(mine, *, name):
    def body(g_ref, out_ref, send_sems, recv_sems):
        x, y, c = _mesh_pos()
        copies = [pltpu.make_async_remote_copy(
            src_ref=g_ref.at[j], dst_ref=out_ref.at[j], send_sem=send_sems.at[j], recv_sem=recv_sems.at[j],
            device_id=(x, y, 1 - c), device_id_type=MESH) for j in range(N_CHIPS)]
        for cp in copies:
            cp.start()
        for cp in copies:
            cp.wait()

    return pl.pallas_call(
        body, name=name,
        out_shape=jax.ShapeDtypeStruct(mine.shape, mine.dtype),
        in_specs=[HBM], out_specs=HBM,
        scratch_shapes=[pltpu.SemaphoreType.DMA((N_CHIPS,)), pltpu.SemaphoreType.DMA((N_CHIPS,))],
    )(mine)


def _scatter_partials(part, *, name):
    _, Rh, C = part.shape

    def body(p_ref, out_ref, send_sems, recv_sems):
        x, y, c = _mesh_pos()
        copies = [pltpu.make_async_remote_copy(
            src_ref=p_ref.at[2 * px + py], dst_ref=out_ref.at[j], send_sem=send_sems.at[j], recv_sem=recv_sems.at[j],
            device_id=(px, py, c), device_id_type=MESH) for j, (px, py) in enumerate(_other_chips(x, y))]
        for cp in copies:
            cp.start()
        for cp in copies:
            cp.wait()

    return pl.pallas_call(
        body, name=name,
        out_shape=jax.ShapeDtypeStruct((3, Rh, C), part.dtype),
        in_specs=[HBM], out_specs=HBM,
        scratch_shapes=[pltpu.SemaphoreType.DMA((3,)), pltpu.SemaphoreType.DMA((3,))],
    )(part)


def _join_halves(buf, *, name):
    def body(b_ref, out_ref, send_sem, recv_sem):
        x, y, c = _mesh_pos()
        cp = pltpu.make_async_remote_copy(
            src_ref=b_ref.at[c], dst_ref=out_ref.at[c], send_sem=send_sem, recv_sem=recv_sem,
            device_id=(x, y, 1 - c), device_id_type=MESH)
        cp.start()
        cp.wait()

    return pl.pallas_call(
        body, name=name,
        out_shape=jax.ShapeDtypeStruct(buf.shape, buf.dtype),
        in_specs=[HBM], out_specs=HBM, input_output_aliases={0: 0},
        scratch_shapes=[pltpu.SemaphoreType.DMA, pltpu.SemaphoreType.DMA],
    )(buf)


def _pad_rows(a, mult):
    pad = (-a.shape[0]) % mult
    return a if pad == 0 else jnp.pad(a, ((0, pad),) + ((0, 0),) * (a.ndim - 1))


def _local_step(x, target, mod, wts, small, slab_rest=None, unpack_rest=None):
    S, D = x.shape
    HP = D // LANES
    row = lambda v: v.reshape(1, -1)
    msplit = [[row(mod[i, k * D:(k + 1) * D]) for k in range(6)] for i in range(2)]
    gw, gs = {}, {}
    dmod = [[None] * 6 for _ in range(2)]

    sh1, sc1, g1, sh2, sc2, g2 = msplit[0]
    n1w0, n2w0 = row(small["norm1_w"][0]), row(small["norm2_w"][0])
    proj0, h1_0 = _ln_matmul(x, n1w0, sc1, sh1, wts["hg_w_in"], relu2=False, name="hg_in_proj")
    gn = small["hg_gn_w"].reshape(1, LANES)
    ypre0, o0, states, *gathered = _hg_fwd(proj0, small["hg_lb"], gn, slab_rest, name="hg_fwd")
    if slab_rest is not None:
        wts = {**wts, **unpack_rest(gathered[0])}
    x1, ymix0 = _matmul_resid(ypre0, wts["hg_w_out"], x, g1, name="hg_out_proj")
    a0, u0, h2_0 = _ln_matmul(x1, n2w0, sc2, sh2, wts["mlp_w1_0"], relu2=True, name="mlp0_up")
    x2, ymlp0 = _matmul_resid(u0, wts["mlp_w2_0"], x1, g2, name="mlp0_down")

    sh1b, sc1b, g1b, sh2b, sc2b, g2b = msplit[1]
    n1w1, n2w1 = row(small["norm1_w"][1]), row(small["norm2_w"][1])
    proj1, h1_1 = _ln_matmul(x2, n1w1, sc1b, sh1b, wts["fox_w_in"], relu2=False, name="fox_in_proj")
    nheads = 2 * HP
    bf_pad = jnp.pad(small["fox_b_f"].reshape(1, nheads), ((0, 0), (0, LANES - nheads)))
    qw2 = jnp.tile(small["fox_qn_w"].reshape(1, FOX_DH), (1, 2))
    kw2 = jnp.tile(small["fox_kn_w"].reshape(1, FOX_DH), (1, 2))
    fcum = _fox_cumsum(proj1, bf_pad, name="fox_cumsum")
    qa, ka, va, vat = _fox_prep(proj1, fcum, qw2, kw2, name="fox_prep")
    jmin, imax = _fox_skip_bounds(fcum, small["fox_qn_w"], small["fox_kn_w"], nheads)
    ypre1, o1, q2 = _fox_fwd(jmin, qa, ka, vat, proj1, name="fox_fwd")
    x3, ymix1 = _matmul_resid(ypre1, wts["fox_w_out"], x2, g1b, name="fox_out_proj")
    a1, u1, h2_1 = _ln_matmul(x3, n2w1, sc2b, sh2b, wts["mlp_w1_1"], relu2=True, name="mlp1_up")
    x4, ymlp1 = _matmul_resid(u1, wts["mlp_w2_1"], x3, g2b, name="mlp1_down")

    loss, dx4, dfw = _loss_kernel(x4, row(small["final_w"]), target, name="loss")
    gs["final_w"] = dfw.reshape(-1)

    def mlp_bwd(i, dx_out, x_in, h2, a, u, ymlp, n2w, sc2_, g2_):
        dz, dm, dg2 = _gate_matmul_nt(dx_out, g2_, ymlp, wts[f"mlp_w2_{i}"], a, name=f"mlp{i}_down_bwd")
        gw[f"mlp_w2_{i}"] = _matmul_tn(u, dm[None], name=f"mlp{i}_dw2")
        gw[f"mlp_w1_{i}"] = _matmul_tn(h2, dz[None], name=f"mlp{i}_dw1")
        dx_in, dsc, dsh, dnw = _matmul_nt_lnbwd(dz[None], wts[f"mlp_w1_{i}"], x_in, n2w, sc2_, dx_out,
                                                name=f"mlp{i}_up_bwd")
        dmod[i][3], dmod[i][4], dmod[i][5] = dsh, dsc, dg2
        return dx_in, dnw

    dx3, dn2w1 = mlp_bwd(1, dx4, x3, h2_1, a1, u1, ymlp1, n2w1, sc2b, g2b)
    dyp1, dm1, dg1b = _gate_matmul_nt(dx3, g1b, ymix1, wts["fox_w_out"], None, name="fox_out_bwd")
    gw["fox_w_out"] = _matmul_tn(ypre1, dm1[None], name="fox_dw_out")
    doa = _fox_bwd_prep(dyp1, o1, proj1, name="fox_bwd_prep")
    dqa, dka, dva, colsum = _fox_bwd(imax, q2, ka, va, doa, name="fox_bwd")
    colsum = jnp.pad(colsum[:, 0, :].T, ((0, 0), (0, LANES - nheads)))
    dproj1, dqw, dkw = _fox_bwd_post(dqa, dka, dva, proj1, dyp1, o1, qw2, kw2, name="fox_bwd_post")
    dproj1, dbf = _fox_dfz(colsum, nheads, proj1, bf_pad, dproj1, name="fox_dfz")
    gw["fox_w_in"] = _matmul_tn(h1_1, dproj1, name="fox_dw_in")
    dx2, dsc, dsh, dn1w1 = _matmul_nt_lnbwd(dproj1, wts["fox_w_in"], x2, n1w1, sc1b, dx3, name="fox_in_bwd")
    dmod[1][0], dmod[1][1], dmod[1][2] = dsh, dsc, dg1b
    gs["fox_qn_w"] = dqw[0, :FOX_DH] + dqw[0, FOX_DH:]
    gs["fox_kn_w"] = dkw[0, :FOX_DH] + dkw[0, FOX_DH:]
    gs["fox_b_f"] = dbf[0, :nheads]

    dx1, dn2w0 = mlp_bwd(0, dx2, x1, h2_0, a0, u0, ymlp0, n2w0, sc2, g2)
    dyp0, dm0, dg1 = _gate_matmul_nt(dx1, g1, ymix0, wts["hg_w_out"], None, name="hg_out_bwd")
    gw["hg_w_out"] = _matmul_tn(ypre0, dm0[None], name="hg_dw_out")
    dproj0, dlb, dgn = _hg_bwd(proj0, small["hg_lb"], gn, o0, states, dyp0, name="hg_bwd")
    gw["hg_w_in"] = _matmul_tn(h1_0, dproj0, name="hg_dw_in")
    dx0, dsc, dsh, dn1w0 = _matmul_nt_lnbwd(dproj0, wts["hg_w_in"], x, n1w0, sc1, dx1, name="hg_in_bwd")
    dmod[0][0], dmod[0][1], dmod[0][2] = dsh, dsc, dg1
    gs["hg_lb"] = dlb
    gs["hg_gn_w"] = jnp.sum(dgn, axis=0)

    gs["norm1_w"] = jnp.concatenate([dn1w0, dn1w1], axis=0)
    gs["norm2_w"] = jnp.concatenate([dn2w0, dn2w1], axis=0)
    gs["dmod"] = jnp.stack([jnp.concatenate(dmod[i], axis=1)[0] for i in range(2)])
    return loss, dx0, gw, gs


def _pack_halves(layout):
    rh = -(-max(sum(a.shape[0] for _, a in half) for half in layout) // 16) * 16
    place, parts = {}, []
    for h, half in enumerate(layout):
        off = 0
        for n, a in half:
            place[n] = (h, off, a.shape[0])
            off += a.shape[0]
        parts.append(jnp.pad(jnp.concatenate([a.astype(BF) for _, a in half], axis=0), ((0, rh - off), (0, 0))))
    return jnp.concatenate(parts, axis=0), place, rh


SMALL_NAMES = ["norm1_w", "norm2_w", "hg_lb", "hg_gn_w", "fox_b_f", "fox_qn_w", "fox_kn_w", "final_w"]


def _pack_small(d, names):
    rows, offs, r0 = [], {}, 0
    for n in names:
        flat = d[n].reshape(-1)
        nr = -(-flat.shape[0] // LANES)
        rows.append(jnp.pad(flat, (0, nr * LANES - flat.shape[0])).reshape(nr, LANES))
        offs[n] = (r0, nr)
        r0 += nr
    return jnp.concatenate(rows, axis=0), offs


def _unpack_small(packed, offs, name, like):
    r0, nr = offs[name]
    return packed[r0:r0 + nr].reshape(-1)[:like.size].reshape(like.shape)


def kernel(x, c, w_mod, b_mod, norm1_w, norm2_w, hg_w_in, hg_w_out, hg_lb, hg_gn_w, fox_w_in, fox_b_f, fox_qn_w, fox_kn_w, fox_w_out, mlp_w1, mlp_w2, final_w, loss_target, m_w_mod, m_b_mod, m_norm1_w, m_norm2_w, m_hg_w_in, m_hg_w_out, m_hg_lb, m_hg_gn_w, m_fox_w_in, m_fox_b_f, m_fox_qn_w, m_fox_kn_w, m_fox_w_out, m_mlp_w1, m_mlp_w2, m_final_w, v_w_mod, v_b_mod, v_norm1_w, v_norm2_w, v_hg_w_in, v_hg_w_out, v_hg_lb, v_hg_gn_w, v_fox_w_in, v_fox_b_f, v_fox_qn_w, v_fox_kn_w, v_fox_w_out, v_mlp_w1, v_mlp_w2, v_final_w):
    S, D = x.shape[1], x.shape[2]
    nheads = D // FOX_DH
    ax, ay, ac = _mesh_pos()
    chip = 2 * ax + ay
    dev = 2 * chip + ac
    xs, tgt = x.reshape(S, D), loss_target.reshape(S, D)

    c_all = _allgather_small(_pad_rows(c.reshape(-1, LANES), 8), name="gather_c")
    c_all = c_all.reshape(N_DEV, -1)[:, :D]
    c16 = _pad_rows(c_all, 16)
    nmod = w_mod.shape[2]
    b_shard = lax.dynamic_slice_in_dim(b_mod, chip * nmod, nmod, axis=1)
    mod_shard = _mod_fwd(c16, w_mod, b_shard[:, None, :], name="mod_fwd")[:, :N_DEV]
    mod_all = _allgather_small(mod_shard.reshape(-1, LANES), name="gather_mod")
    mod_all = mod_all.reshape(N_CHIPS, 2, 2, N_DEV, nmod)[:, 0]
    mod = lax.dynamic_index_in_dim(mod_all, dev, axis=2, keepdims=False)
    mod = mod.transpose(1, 0, 2).reshape(2, N_CHIPS * nmod)

    fox_rows = fox_w_in.shape[2]
    col = lambda g: g.transpose(1, 0, 2).reshape(g.shape[1], -1)
    rowsh = lambda g: g.reshape(-1, g.shape[2])
    own = lambda g, s: lax.dynamic_update_index_in_dim(g, s, chip, 0)

    slab_in = hg_w_in[0].astype(BF)
    wts = {"hg_w_in": col(own(_allgather_chip_slabs(slab_in, name="gather_hg_w_in"), slab_in))}
    slab_rest, place_rest, rh_rest = _pack_halves(
        [[("mlp_w1", mlp_w1.reshape(2 * D, D)), ("hg_w_out", hg_w_out[0]), ("fox_w_out", fox_w_out[0])],
         [("mlp_w2", mlp_w2.reshape(2 * D, D)), ("fox_w_in", fox_w_in[0].reshape(fox_rows, D))]])

    def unpack_rest(gathered):
        gathered = own(gathered, slab_rest)

        def seg(n):
            h, off, rows = place_rest[n]
            return gathered[:, h * rh_rest + off:h * rh_rest + off + rows, :]

        w1 = seg("mlp_w1").reshape(N_CHIPS, 2, D, D)
        w2 = seg("mlp_w2").reshape(N_CHIPS, 2, D, D)
        fox_in = col(seg("fox_w_in").reshape(N_CHIPS, D, fox_rows))
        return {
            "hg_w_out": rowsh(seg("hg_w_out")), "fox_w_out": rowsh(seg("fox_w_out")),
            "mlp_w1_0": col(w1[:, 0]), "mlp_w1_1": col(w1[:, 1]), "mlp_w2_0": rowsh(w2[:, 0]), "mlp_w2_1": rowsh(w2[:, 1]),
            "fox_w_in": jnp.pad(fox_in, ((0, 0), (0, 5 * D - fox_in.shape[1]))),
        }

    small = {"norm1_w": norm1_w, "norm2_w": norm2_w, "hg_lb": hg_lb, "hg_gn_w": hg_gn_w, "fox_b_f": fox_b_f,
             "fox_qn_w": fox_qn_w, "fox_kn_w": fox_kn_w, "final_w": final_w}

    loss_part, grad_x, gw, gs = _local_step(xs, tgt, mod, wts, small, slab_rest, unpack_rest)

    layout = [[("hg_w_in", hg_w_in[0]), ("mlp_w1", mlp_w1.reshape(2 * D, D)), ("hg_w_out", hg_w_out[0])],
              [("mlp_w2", mlp_w2.reshape(2 * D, D)), ("fox_w_out", fox_w_out[0]),
               ("fox_w_in", fox_w_in[0].reshape(fox_rows, D))]]
    Rh = -(-max(sum(a.shape[0] for _, a in half) for half in layout) // 16) * 16
    place = {}
    for h, half in enumerate(layout):
        off = 0
        for n, a in half:
            place[n] = (h, off, a.shape[0])
            off += a.shape[0]
    loss = lax.psum(loss_part[0, 0], ("x", "y", "c"))

    def uncol(g, n):
        return g.reshape(g.shape[0], N_CHIPS, n).transpose(1, 0, 2)

    gseg = {
        "hg_w_in": uncol(gw["hg_w_in"], D), "hg_w_out": gw["hg_w_out"].reshape(N_CHIPS, D // 4, D),
        "fox_w_out": gw["fox_w_out"].reshape(N_CHIPS, D // 4, D),
        "mlp_w1": jnp.concatenate([uncol(gw["mlp_w1_0"], D), uncol(gw["mlp_w1_1"], D)], axis=1),
        "mlp_w2": jnp.concatenate([gw["mlp_w2_0"].reshape(N_CHIPS, D, D), gw["mlp_w2_1"].reshape(N_CHIPS, D, D)], axis=1),
        "fox_w_in": uncol(gw["fox_w_in"][:, :4 * fox_rows], fox_rows).reshape(N_CHIPS, fox_rows, D),
    }
    ghalves = []
    for half in layout:
        gh = jnp.concatenate([gseg[n] for n, _ in half], axis=1)
        ghalves.append(jnp.pad(gh, ((0, 0), (0, Rh - gh.shape[1]), (0, 0))))
    g4 = jnp.stack(ghalves, axis=1)
    to_sibling = lax.dynamic_index_in_dim(g4, 1 - ac, axis=1, keepdims=False).astype(BF)
    from_sibling = _swap_halves(to_sibling, name="rs_swap_halves")
    chip_part = _add_halves(g4, from_sibling, ac.reshape(1), name="rs_add_halves")
    from_chips = _scatter_partials(chip_part, name="rs_scatter")
    my_half = _add_four(g4, from_sibling, from_chips, jnp.stack([chip, ac]), name="rs_add_chips")
    gshard = _join_halves(my_half, name="rs_join")

    names = ["dmod"] + SMALL_NAMES
    packed, offs = _pack_small(gs, names)
    packed = _pad_rows(packed, 8)
    rp = packed.shape[0]
    parts = _allgather_small(packed, name="gather_small").reshape(N_DEV, rp, LANES)
    total = _sum_parts(parts, name="sum_small")
    r0, nr = offs["dmod"]
    dmod_all = parts[:, r0:r0 + nr].reshape(N_DEV, 2, N_CHIPS * nmod)
    dmod_shard = lax.dynamic_slice_in_dim(dmod_all, chip * nmod, nmod, axis=2).transpose(1, 0, 2)
    g_w_mod = _mod_bwd(c16, jnp.pad(dmod_shard, ((0, 0), (0, 16 - N_DEV), (0, 0))), name="mod_bwd")

    grads = {"w_mod": g_w_mod, "b_mod": _unpack_small(total, offs, "dmod", b_mod)}
    for n in SMALL_NAMES:
        grads[n] = _unpack_small(total, offs, n, small[n])

    given = dict(w_mod=(w_mod, m_w_mod, v_w_mod), b_mod=(b_mod, m_b_mod, v_b_mod), norm1_w=(norm1_w, m_norm1_w, v_norm1_w),
                 norm2_w=(norm2_w, m_norm2_w, v_norm2_w), hg_w_in=(hg_w_in, m_hg_w_in, v_hg_w_in),
                 hg_w_out=(hg_w_out, m_hg_w_out, v_hg_w_out), hg_lb=(hg_lb, m_hg_lb, v_hg_lb),
                 hg_gn_w=(hg_gn_w, m_hg_gn_w, v_hg_gn_w), fox_w_in=(fox_w_in, m_fox_w_in, v_fox_w_in),
                 fox_b_f=(fox_b_f, m_fox_b_f, v_fox_b_f), fox_qn_w=(fox_qn_w, m_fox_qn_w, v_fox_qn_w),
                 fox_kn_w=(fox_kn_w, m_fox_kn_w, v_fox_kn_w), fox_w_out=(fox_w_out, m_fox_w_out, v_fox_w_out),
                 mlp_w1=(mlp_w1, m_mlp_w1, v_mlp_w1), mlp_w2=(mlp_w2, m_mlp_w2, v_mlp_w2), final_w=(final_w, m_final_w, v_final_w))
    upd = {}

    for n, (h, off, rows) in place.items():
        w, m, v = given[n]
        flat = lambda a: a.reshape(rows, D)
        d, mn, vn = _adamw(flat(w), gshard, flat(m), flat(v), g_at=(h, off), name=f"adamw_{n}")
        grads[n] = gshard[h, off:off + rows].reshape(w.shape)
        upd[n] = tuple(a.reshape(w.shape) for a in (d, mn, vn))

    w, m, v = given["w_mod"]
    flat = lambda a: a.reshape(-1, nmod)
    upd["w_mod"] = tuple(a.reshape(w.shape) for a in _adamw(flat(w), flat(g_w_mod), flat(m), flat(v), name="adamw_w_mod"))

    snames = ["b_mod"] + SMALL_NAMES
    pw, soffs = _pack_small({n: given[n][0] for n in snames}, snames)
    pm, _ = _pack_small({n: given[n][1] for n in snames}, snames)
    pv, _ = _pack_small({n: given[n][2] for n in snames}, snames)
    pg, _ = _pack_small({n: grads[n] for n in snames}, snames)
    pw, pm, pv, pg = (_pad_rows(a, 8) for a in (pw, pm, pv, pg))
    sd, smn, svn = _adamw(pw, pg, pm, pv, name="adamw_small")
    for n in snames:
        like = given[n][0]
        upd[n] = tuple(_unpack_small(a, soffs, n, like) for a in (sd, smn, svn))

    order = ["w_mod", "b_mod", "norm1_w", "norm2_w", "hg_w_in", "hg_w_out", "hg_lb", "hg_gn_w", "fox_w_in", "fox_b_f",
             "fox_qn_w", "fox_kn_w", "fox_w_out", "mlp_w1", "mlp_w2", "final_w"]
    return (loss, grad_x.reshape(x.shape), *[grads[n] for n in order], *[upd[n][0] for n in order],
            *[upd[n][1] for n in order], *[upd[n][2] for n in order])
```

```python
import math

import jax
import jax.numpy as jnp
from jax import lax
from jax.experimental import pallas as pl
from jax.experimental.pallas import tpu as pltpu

EPS = 1e-6
ADAM_LR, ADAM_B1, ADAM_B2, ADAM_EPS, ADAM_WD, ADAM_STEP = 0.001, 0.9, 0.999, 1e-08, 0.01, 10

F32 = jnp.float32
BF = jnp.bfloat16
LANES = 128
HG_CHUNK = 64
HG_HEADS_PER_STEP = 8
HG_TOKENS_PER_STEP = 256
FOX_BWD_TILES = (8, 4, 2, 1)
LOG2E = 1.4426950408889634
FOX_DH = 64
N_CHIPS = 4
N_DEV = 8
VMEM_LIMIT = 56 * 1024 * 1024
MESH = pl.DeviceIdType.MESH

NT = (((1,), (1,)), ((), ()))
TN = (((0,), (0,)), ((), ()))


def _pick(n, pref, mult=LANES):
    if n <= pref:
        return n
    t = (pref // mult) * mult
    while t >= mult:
        if n % t == 0:
            return t
        t -= mult
    raise ValueError((n, pref, mult))


def _cp(*sem):
    return pltpu.CompilerParams(dimension_semantics=sem, vmem_limit_bytes=VMEM_LIMIT)


def _dot(a, b):
    return jnp.dot(a, b, preferred_element_type=F32)


def _dg(a, b, dims):
    return lax.dot_general(a, b, dims, preferred_element_type=F32)


def _split3(x):
    hi = x.astype(BF)
    r1 = x - hi.astype(F32)
    mid = r1.astype(BF)
    lo = (r1 - mid.astype(F32)).astype(BF)
    return hi, mid, lo


def _tri_dot(tri, x):
    hi, mid, lo = _split3(x)
    return _dot(tri, hi) + _dot(tri, mid) + _dot(tri, lo)


def _dg3(a, b, dims):
    ah, bh = a.astype(BF), b.astype(BF)
    al, bl = (a - ah.astype(F32)).astype(BF), (b - bh.astype(F32)).astype(BF)
    return _dg(ah, bh, dims) + _dg(ah, bl, dims) + _dg(al, bh, dims)


def _dg1(a, b, dims):
    return _dg(a.astype(BF), b.astype(BF), dims)


NN = (((1,), (0,)), ((), ()))


def _sigmoid(x):
    return jax.nn.sigmoid(x)


def _ln_matmul(x, nw, sc, sh, w, *, relu2, name):
    S, D = x.shape
    N = w.shape[1]
    tm, tn = _pick(S, 512, 16), N

    def body(x_ref, nw_ref, sc_ref, sh_ref, w_ref, *rest):
        outs, hs = rest[:-1], rest[-1]
        h_ref = outs[-1]

        @pl.when(pl.program_id(1) == 0)
        def _():
            xv = x_ref[...]
            r = lax.rsqrt(jnp.mean(xv * xv, axis=-1, keepdims=True) + EPS)
            hb = ((xv * r * nw_ref[...]) * (1.0 + sc_ref[...]) + sh_ref[...]).astype(BF)
            hs[...] = hb
            h_ref[...] = hb

        z = _dot(hs[...], w_ref[...])
        if relu2:
            a = jnp.maximum(z, 0.0)
            outs[0][...] = a.astype(BF)
            outs[1][...] = (a * a).astype(BF)
        else:
            outs[0][...] = z

    vec = pl.BlockSpec((1, D), lambda i, j: (0, 0))
    tile = pl.BlockSpec((tm, tn), lambda i, j: (i, j))
    if relu2:
        out_shape = [jax.ShapeDtypeStruct((S, N), BF), jax.ShapeDtypeStruct((S, N), BF)]
        out_specs = [tile, tile]
    else:
        out_shape = [jax.ShapeDtypeStruct((S, N), F32)]
        out_specs = [tile]
    out_shape.append(jax.ShapeDtypeStruct((S, D), BF))
    out_specs.append(pl.BlockSpec((tm, D), lambda i, j: (i, 0)))
    return pl.pallas_call(
        body, name=name, grid=(S // tm, N // tn),
        in_specs=[pl.BlockSpec((tm, D), lambda i, j: (i, 0)), vec, vec, vec,
                  pl.BlockSpec((D, tn), lambda i, j: (0, j))],
        out_specs=out_specs, out_shape=out_shape,
        scratch_shapes=[pltpu.VMEM((tm, D), BF)],
        compiler_params=_cp("parallel", "arbitrary"),
    )(x, nw, sc, sh, w)


def _matmul_resid(a, w, x, gate, *, name):
    S, K = a.shape
    D = w.shape[1]
    tm, tn = _pick(S, 1024 if K <= 1024 else 512, 16), D

    def body(a_ref, w_ref, x_ref, g_ref, o_ref, y_ref):
        y = _dot(a_ref[...], w_ref[...])
        y_ref[...] = y.astype(BF)
        o_ref[...] = x_ref[...] + g_ref[...] * y

    tile = pl.BlockSpec((tm, tn), lambda i, j: (i, j))
    return pl.pallas_call(
        body, name=name, grid=(S // tm, D // tn),
        in_specs=[pl.BlockSpec((tm, K), lambda i, j: (i, 0)), pl.BlockSpec((K, tn), lambda i, j: (0, j)),
                  tile, pl.BlockSpec((1, tn), lambda i, j: (0, j))],
        out_specs=[tile, tile],
        out_shape=[jax.ShapeDtypeStruct((S, D), F32), jax.ShapeDtypeStruct((S, D), BF)],
        compiler_params=_cp("parallel", "arbitrary"),
    )(a, w, x, gate)


def _gate_matmul_nt(dx, gate, y, w, act, *, name):
    S, D = dx.shape
    K = w.shape[0]
    tm, tn = _pick(S, 1024 if K <= 1024 else 512, 16), K
    fused = act is not None

    def body(dx_ref, g_ref, y_ref, w_ref, *rest):
        if fused:
            act_ref, da_ref, dm_ref, dg_ref, ms = rest
        else:
            da_ref, dm_ref, dg_ref, ms = rest
        i, j = pl.program_id(0), pl.program_id(1)

        @pl.when((i == 0) & (j == 0))
        def _():
            dg_ref[...] = jnp.zeros_like(dg_ref)

        @pl.when(j == 0)
        def _():
            dxv = dx_ref[...]
            dmb = (dxv * g_ref[...]).astype(BF)
            ms[...] = dmb
            dm_ref[...] = dmb
            dg_ref[...] += jnp.sum(dxv * y_ref[...].astype(F32), axis=0, keepdims=True)

        da = _dg(ms[...], w_ref[...], NT)
        if fused:
            da_ref[...] = (da * (2.0 * act_ref[...].astype(F32))).astype(BF)
        else:
            da_ref[...] = da

    row = pl.BlockSpec((tm, D), lambda i, j: (i, 0))
    vec = pl.BlockSpec((1, D), lambda i, j: (0, 0))
    tile = pl.BlockSpec((tm, tn), lambda i, j: (i, j))
    in_specs = [row, vec, row, pl.BlockSpec((tn, D), lambda i, j: (j, 0))]
    args = [dx, gate, y, w]
    if fused:
        in_specs.append(tile)
        args.append(act)
    return pl.pallas_call(
        body, name=name, grid=(S // tm, K // tn),
        in_specs=in_specs, out_specs=[tile, row, vec],
        out_shape=[jax.ShapeDtypeStruct((S, K), BF if fused else F32), jax.ShapeDtypeStruct((S, D), BF),
                   jax.ShapeDtypeStruct((1, D), F32)],
        scratch_shapes=[pltpu.VMEM((tm, D), BF)],
        compiler_params=_cp("arbitrary", "arbitrary"),
    )(*args)


def _matmul_tn(a, b, *, name):
    S, Ka = a.shape
    P, _, Db = b.shape
    tk, tn, ts = _pick(Ka, 1024), _pick(Db, 1024), _pick(S, 1024, 16)
    npb = Db // tn

    def body(a_ref, b_ref, o_ref, acc):
        s = pl.program_id(2)

        @pl.when(s == 0)
        def _():
            acc[...] = jnp.zeros_like(acc)

        acc[...] += _dg(a_ref[...], b_ref[...], TN)

        @pl.when(s == pl.num_programs(2) - 1)
        def _():
            o_ref[...] = acc[...]

    return pl.pallas_call(
        body, name=name, grid=(Ka // tk, P * npb, S // ts),
        in_specs=[pl.BlockSpec((ts, tk), lambda i, j, s: (s, i)),
                  pl.BlockSpec((None, ts, tn), lambda i, j, s: (j // npb, s, j % npb))],
        out_specs=pl.BlockSpec((tk, tn), lambda i, j, s: (i, j)),
        out_shape=jax.ShapeDtypeStruct((Ka, P * Db), F32),
        scratch_shapes=[pltpu.VMEM((tk, tn), F32)],
        compiler_params=_cp("parallel", "parallel", "arbitrary"),
    )(a, b)


def _matmul_nt_lnbwd(g, w, x, nw, sc, dx_out, *, name):
    P, S, Dg = g.shape
    D = x.shape[1]
    tm = _pick(S, 512, 16)

    def body(g_ref, w_ref, x_ref, nw_ref, sc_ref, dxo_ref, dx_ref, dsc_ref, dsh_ref, dnw_ref):
        @pl.when(pl.program_id(0) == 0)
        def _():
            dsc_ref[...] = jnp.zeros_like(dsc_ref)
            dsh_ref[...] = jnp.zeros_like(dsh_ref)
            dnw_ref[...] = jnp.zeros_like(dnw_ref)

        dh = _dg(g_ref[0], w_ref[:, 0:Dg], NT)
        for p in range(1, P):
            dh = dh + _dg(g_ref[p], w_ref[:, p * Dg:(p + 1) * Dg], NT)
        xv = x_ref[...]
        nwv = nw_ref[...]
        r = lax.rsqrt(jnp.mean(xv * xv, axis=-1, keepdims=True) + EPS)
        xr = xv * r
        dn = dh * (1.0 + sc_ref[...])
        dsc_ref[...] += jnp.sum(dh * (xr * nwv), axis=0, keepdims=True)
        dsh_ref[...] += jnp.sum(dh, axis=0, keepdims=True)
        dnw_ref[...] += jnp.sum(dn * xr, axis=0, keepdims=True)
        u = dn * nwv
        dx_ref[...] = dxo_ref[...] + r * (u - xr * jnp.mean(u * xr, axis=-1, keepdims=True))

    row = pl.BlockSpec((tm, D), lambda i: (i, 0))
    vec = pl.BlockSpec((1, D), lambda i: (0, 0))
    return pl.pallas_call(
        body, name=name, grid=(S // tm,),
        in_specs=[pl.BlockSpec((P, tm, Dg), lambda i: (0, i, 0)),
                  pl.BlockSpec((D, P * Dg), lambda i: (0, 0)), row, vec, vec, row],
        out_specs=[row, vec, vec, vec],
        out_shape=[jax.ShapeDtypeStruct((S, D), F32)] + [jax.ShapeDtypeStruct((1, D), F32)] * 3,
        compiler_params=_cp("arbitrary"),
    )(g, w, x, nw, sc, dx_out)


def _loss_kernel(x, fw, tgt, *, name):
    S, D = x.shape
    tm = _pick(S, 512, 8)

    def body(x_ref, fw_ref, t_ref, l_ref, dx_ref, dfw_ref):
        @pl.when(pl.program_id(0) == 0)
        def _():
            l_ref[...] = jnp.zeros_like(l_ref)
            dfw_ref[...] = jnp.zeros_like(dfw_ref)

        xv = x_ref[...]
        fwv = fw_ref[...]
        r = lax.rsqrt(jnp.mean(xv * xv, axis=-1, keepdims=True) + EPS)
        xr = xv * r
        err = xr * fwv - t_ref[...]
        per_tok = jnp.mean(err * err, axis=-1, keepdims=True)
        l_ref[...] += 0.5 * jnp.sum(per_tok, axis=0, keepdims=True)
        dy = err * (1.0 / D)
        dfw_ref[...] += jnp.sum(dy * xr, axis=0, keepdims=True)
        u = dy * fwv
        dx_ref[...] = r * (u - xr * jnp.mean(u * xr, axis=-1, keepdims=True))

    row = pl.BlockSpec((tm, D), lambda i: (i, 0))
    vec = pl.BlockSpec((1, D), lambda i: (0, 0))
    return pl.pallas_call(
        body, name=name, grid=(S // tm,),
        in_specs=[row, vec, row],
        out_specs=[pl.BlockSpec((1, LANES), lambda i: (0, 0)), row, vec],
        out_shape=[jax.ShapeDtypeStruct((1, LANES), F32), jax.ShapeDtypeStruct((S, D), F32),
                   jax.ShapeDtypeStruct((1, D), F32)],
        compiler_params=_cp("arbitrary"),
    )(x, fw, tgt)


def _hg_lower_bound(lb3):
    mx = jnp.max(lb3, axis=0, keepdims=True)
    e = jnp.exp(lb3 - mx)
    p = e / jnp.sum(e, axis=0, keepdims=True)
    return p[0:1, :], p


def _hg_chunk_common(qr, fz, lbv):
    sq = _sigmoid(qr)
    q = qr * sq
    sig = _sigmoid(fz)
    f = lbv + (1.0 - lbv) * sig
    k = (1.0 - lbv) * (1.0 - sig)
    return q, sq, sig, f, k, jnp.log(f)


def _row_of(x, rows, r):
    return jnp.sum(jnp.where(rows == r, x, 0.0), axis=0, keepdims=True)


def _hg_fwd(proj, hg_lb, gn, slab=None, *, name):
    S = proj.shape[0]
    D = proj.shape[1] // 4
    H = D // LANES
    HB = min(HG_HEADS_PER_STEP, H)
    W = HB * LANES
    C = HG_CHUNK
    T = _pick(S, HG_TOKENS_PER_STEP, C)
    nch, nb = T // C, S // T
    ng = H // HB
    fused = slab is not None

    def body(q_ref, fz_ref, v_ref, g_ref, lb_ref, gn_ref, *rest):
        if fused:
            s_ref, y_ref, o_ref, sts_ref, out_ref, st, send_sems, recv_sems = rest
            first, passed, landed, from_sibling = _chip_slab_copies(s_ref, out_ref, send_sems, recv_sems)
            hgrp, n = pl.program_id(0), pl.program_id(1)

            @pl.when((hgrp == 0) & (n == 0))
            def _():
                for cp in first:
                    cp.start()

            @pl.when((hgrp == ng - 1) & (n == (3 * nb) // 4))
            def _():
                for arrived, onward in zip(landed, passed):
                    arrived.wait_recv()
                    onward.start()
        else:
            y_ref, o_ref, sts_ref, st = rest

        @pl.when(pl.program_id(1) == 0)
        def _():
            st[...] = jnp.zeros_like(st)

        lb_all, _ = _hg_lower_bound(lb_ref[...])
        gnv = gn_ref[...]
        ri = lax.broadcasted_iota(jnp.int32, (C, C), 0)
        ci_ = lax.broadcasted_iota(jnp.int32, (C, C), 1)
        low = ri >= ci_
        tri = jnp.where(low, 1.0, 0.0).astype(BF)
        rows = lax.broadcasted_iota(jnp.int32, (C, LANES), 0)

        def chunk(ci, carry):
            sl = pl.ds(pl.multiple_of(ci * C, C), C)
            for hh in range(HB):
                ls = slice(hh * LANES, (hh + 1) * LANES)
                q, _, _, _, k, logf = _hg_chunk_common(q_ref[sl, ls], fz_ref[sl, ls], lb_all[:, ls])
                vv = v_ref[sl, ls]
                gg = g_ref[sl, ls]
                G = _tri_dot(tri, logf)
                Gm = _row_of(G, rows, C // 2 - 1)
                Gl = _row_of(G, rows, C - 1)
                qt = q * jnp.exp(G - Gm)
                kt = k * jnp.exp(Gm - G)
                A = jnp.where(low, _dg1(qt, kt, NT), 0.0)
                Sv = st[hh]
                sts_ref[hh, ci] = Sv
                o = _dg1(A, vv, NN) + _dg1(q * jnp.exp(G), Sv, NT)
                st[hh] = Sv * jnp.exp(Gl) + _dg1(vv, k * jnp.exp(Gl - G), TN)
                r = lax.rsqrt(jnp.mean(o * o, axis=-1, keepdims=True) + EPS)
                y_ref[sl, ls] = ((o * r * gnv) * (gg * _sigmoid(gg))).astype(BF)
                o_ref[sl, ls] = o
            return carry

        lax.fori_loop(0, nch, chunk, 0)

        if fused:
            @pl.when((hgrp == ng - 1) & (n == nb - 1))
            def _():
                for cp in from_sibling:
                    cp.wait_recv()
                for cp in first + passed:
                    cp.wait_send()

    def part(p):
        return pl.BlockSpec((T, W), lambda h, n: (n, p * ng + h))

    blk = pl.BlockSpec((T, W), lambda h, n: (n, h))
    in_specs = [part(0), part(1), part(2), part(3),
                pl.BlockSpec((3, W), lambda h, n: (0, h)), pl.BlockSpec((1, LANES), lambda h, n: (0, 0))]
    out_specs = [blk, blk, pl.BlockSpec((HB, nch, LANES, LANES), lambda h, n: (h, n, 0, 0))]
    out_shape = [jax.ShapeDtypeStruct((S, D), BF), jax.ShapeDtypeStruct((S, D), F32),
                 jax.ShapeDtypeStruct((H, S // C, LANES, LANES), F32)]
    scratch = [pltpu.VMEM((HB, LANES, LANES), F32)]
    args = [proj, proj, proj, proj, hg_lb, gn]
    if fused:
        in_specs.append(HBM)
        out_specs.append(HBM)
        out_shape.append(jax.ShapeDtypeStruct((N_CHIPS,) + slab.shape, slab.dtype))
        scratch += [pltpu.SemaphoreType.DMA((6,)), pltpu.SemaphoreType.DMA((6,))]
        args.append(slab)
    return pl.pallas_call(
        body, name=name, grid=(ng, nb), in_specs=in_specs, out_specs=out_specs, out_shape=out_shape,
        scratch_shapes=scratch, compiler_params=_cp("arbitrary", "arbitrary"),
    )(*args)


def _hg_bwd(proj, hg_lb, gn, o_all, states, dy, *, name):
    S = proj.shape[0]
    D = proj.shape[1] // 4
    H = D // LANES
    HB = min(HG_HEADS_PER_STEP, H)
    W = HB * LANES
    C = HG_CHUNK
    T = _pick(S, HG_TOKENS_PER_STEP, C)
    nch, nb = T // C, S // T

    def body(q_ref, fz_ref, v_ref, g_ref, lb_ref, gn_ref, o_ref, sts_ref, dy_ref,
             dp_ref, dlb_ref, dgn_ref, dst, dlb_acc):
        n = pl.program_id(1)

        @pl.when(n == 0)
        def _():
            dst[...] = jnp.zeros_like(dst)
            dlb_acc[...] = jnp.zeros_like(dlb_acc)
            dgn_ref[...] = jnp.zeros_like(dgn_ref)

        lb_all, p3 = _hg_lower_bound(lb_ref[...])
        gnv = gn_ref[...]
        ri = lax.broadcasted_iota(jnp.int32, (C, C), 0)
        ci_ = lax.broadcasted_iota(jnp.int32, (C, C), 1)
        low = ri >= ci_
        tri = jnp.where(low, 1.0, 0.0).astype(BF)
        triu = jnp.where(ri <= ci_, 1.0, 0.0).astype(BF)
        rows = lax.broadcasted_iota(jnp.int32, (C, LANES), 0)

        def chunk(cj, carry):
            ci = nch - 1 - cj
            sl = pl.ds(pl.multiple_of(ci * C, C), C)
            for hh in range(HB):
                ls = slice(hh * LANES, (hh + 1) * LANES)
                lbv = lb_all[:, ls]
                qr = q_ref[sl, ls]
                q, sq, sig, f, k, logf = _hg_chunk_common(qr, fz_ref[sl, ls], lbv)
                vv = v_ref[sl, ls]
                gg = g_ref[sl, ls]
                o = o_ref[sl, ls]
                dyv = dy_ref[sl, ls]
                G = _tri_dot(tri, logf)
                Gm = _row_of(G, rows, C // 2 - 1)
                Gl = _row_of(G, rows, C - 1)
                eG, e_qm, e_km, e_lk, eGl = jnp.exp(G), jnp.exp(G - Gm), jnp.exp(Gm - G), jnp.exp(Gl - G), jnp.exp(Gl)
                qt = q * e_qm
                kt = k * e_km
                A = jnp.where(low, _dg1(qt, kt, NT), 0.0)
                sg = _sigmoid(gg)
                r = lax.rsqrt(jnp.mean(o * o, axis=-1, keepdims=True) + EPS)
                on = o * r
                d_onw = dyv * (gg * sg)
                dgn_ref[hh] += jnp.sum(d_onw * on, axis=0, keepdims=True)
                dgg = dyv * (on * gnv) * (sg * (1.0 + gg * (1.0 - sg)))
                u = d_onw * gnv
                do = r * (u - on * jnp.mean(u * on, axis=-1, keepdims=True))
                Sv = sts_ref[hh, ci]
                dSv = dst[hh]
                dA = jnp.where(low, _dg3(do, vv, NT), 0.0)
                kdec = k * e_lk
                dv = _dg1(A, do, TN) + _dg1(kdec, dSv, NT)
                dq = _dg3(dA, kt, NN) * e_qm + eG * _dg3(do, Sv, NN)
                dk = _dg3(dA, qt, TN) * e_km + e_lk * _dg3(vv, dSv, NN)
                s_end = Sv * eGl + _dg3(vv, kdec, TN)
                dgl = jnp.sum(dSv * s_end, axis=0, keepdims=True)
                dG = q * dq - k * dk + jnp.where(rows == C - 1, dgl, 0.0)
                dlogf = _tri_dot(triu, dG) - f * dk
                dst[hh] = dSv * eGl + _dg1(do, q * eG, TN)
                dlf_f = dlogf / f
                dlb_acc[:, ls] += jnp.sum(dlf_f * (1.0 - sig), axis=0, keepdims=True)
                dp_ref[0, sl, ls] = (dq * (sq * (1.0 + qr * (1.0 - sq)))).astype(BF)
                dp_ref[1, sl, ls] = (dlf_f * (1.0 - lbv) * sig * (1.0 - sig)).astype(BF)
                dp_ref[2, sl, ls] = dv.astype(BF)
                dp_ref[3, sl, ls] = dgg.astype(BF)
            return carry

        lax.fori_loop(0, nch, chunk, 0)
        sel = jnp.where(lax.broadcasted_iota(jnp.int32, (3, W), 0) == 0, 1.0, 0.0)
        dlb_ref[...] = lb_all * (sel - p3) * dlb_acc[...]

    ng = H // HB

    def part(p):
        return pl.BlockSpec((T, W), lambda h, n: (nb - 1 - n, p * ng + h))

    blk = pl.BlockSpec((T, W), lambda h, n: (nb - 1 - n, h))
    return pl.pallas_call(
        body, name=name, grid=(ng, nb),
        in_specs=[part(0), part(1), part(2), part(3),
                  pl.BlockSpec((3, W), lambda h, n: (0, h)), pl.BlockSpec((1, LANES), lambda h, n: (0, 0)),
                  blk, pl.BlockSpec((HB, nch, LANES, LANES), lambda h, n: (h, nb - 1 - n, 0, 0)), blk],
        out_specs=[pl.BlockSpec((4, T, W), lambda h, n: (0, nb - 1 - n, h)),
                   pl.BlockSpec((3, W), lambda h, n: (0, h)),
                   pl.BlockSpec((HB, 1, LANES), lambda h, n: (h, 0, 0))],
        out_shape=[jax.ShapeDtypeStruct((4, S, D), BF), jax.ShapeDtypeStruct((3, D), F32),
                   jax.ShapeDtypeStruct((H, 1, LANES), F32)],
        scratch_shapes=[pltpu.VMEM((HB, LANES, LANES), F32), pltpu.VMEM((1, W), F32)],
        compiler_params=_cp("parallel", "arbitrary"),
    )(proj, proj, proj, proj, hg_lb, gn, o_all, states, dy)


def _log_sigmoid(u):
    return jnp.minimum(u, 0.0) - jnp.log(1.0 + jnp.exp(-jnp.abs(u)))


def _lane_put(base, lane, first, pieces):
    for n, p in enumerate(pieces):
        base = jnp.where(lane == first + n, p, base)
    return base


def _fox_cumsum(proj, bf_pad, *, name):
    S = proj.shape[0]
    D = proj.shape[1] // 5
    T = _pick(S, 256, 8)

    def body(fz_ref, b_ref, f_ref, carry):
        @pl.when(pl.program_id(0) == 0)
        def _():
            carry[...] = jnp.zeros_like(carry)

        logf = _log_sigmoid(fz_ref[...] + b_ref[...])
        tri = jnp.where(lax.broadcasted_iota(jnp.int32, (T, T), 0) >= lax.broadcasted_iota(jnp.int32, (T, T), 1),
                        1.0, 0.0).astype(BF)
        fv = _tri_dot(tri, logf) + carry[...]
        f_ref[...] = fv
        carry[...] = _row_of(fv, lax.broadcasted_iota(jnp.int32, (T, LANES), 0), T - 1)

    return pl.pallas_call(
        body, name=name, grid=(S // T,),
        in_specs=[pl.BlockSpec((T, LANES), lambda i: (i, 4 * D // LANES)), pl.BlockSpec((1, LANES), lambda i: (0, 0))],
        out_specs=pl.BlockSpec((T, LANES), lambda i: (i, 0)),
        out_shape=jax.ShapeDtypeStruct((S, LANES), F32),
        scratch_shapes=[pltpu.VMEM((1, LANES), F32)],
        compiler_params=_cp("arbitrary"),
    )(proj, bf_pad)


def _pair_stats(sq, lo):
    del lo
    a = lax.broadcasted_iota(jnp.int32, (LANES, LANES), 0) < FOX_DH
    b = lax.broadcasted_iota(jnp.int32, (LANES, LANES), 1) < FOX_DH
    avg = jnp.where(a == b, 1.0 / FOX_DH, 0.0).astype(BF)
    hi, mid, low = _split3(sq)
    return _dot(hi, avg) + _dot(mid, avg) + _dot(low, avg)


def _fox_prep(proj, fcum, qw2, kw2, *, name):
    S = proj.shape[0]
    D = proj.shape[1] // 5
    HP = D // LANES
    T = _pick(S, 512, 16)

    def body(q_ref, k_ref, v_ref, f_ref, qw_ref, kw_ref, qa_ref, ka_ref, va_ref, vt_ref):
        hp = pl.program_id(1)
        lane = lax.broadcasted_iota(jnp.int32, (T, LANES), 1)
        lo = lane < FOX_DH
        qv, kv, vv, fv = q_ref[...], k_ref[...], v_ref[...], f_ref[...]
        qn = qv * lax.rsqrt(_pair_stats(qv * qv, lo) + EPS) * qw_ref[...] * (0.125 * LOG2E)
        kn = kv * lax.rsqrt(_pair_stats(kv * kv, lo) + EPS) * kw_ref[...]
        ones_q = jnp.where((lane >= 67) & (lane <= 69), 1.0, 0.0)
        ones_k = jnp.where(((lane >= 64) & (lane <= 66)) | ((lane >= 70) & (lane <= 72)), 1.0, 0.0)
        ones_v = jnp.where((lane >= 64) & (lane <= 66), 1.0, 0.0)
        for hh in range(2):
            fh = jnp.sum(jnp.where(lane == 2 * hp + hh, fv, 0.0), axis=-1, keepdims=True) * LOG2E
            pieces = [p.astype(F32) for p in _split3(fh)]

            def half(x):
                return jnp.where(lo, x if hh == 0 else pltpu.roll(x, FOX_DH, 1), 0.0)

            qa_ref[hh] = _lane_put(half(qn) + ones_q, lane, 64, pieces).astype(BF)
            ka_ref[hh] = _lane_put(half(kn) + ones_k, lane, 67, [-p for p in pieces]).astype(BF)
            va = half(vv) + ones_v
            va_ref[hh] = va.astype(BF)
            vt_ref[hh] = va.T.astype(BF)

    def part(p):
        return pl.BlockSpec((T, LANES), lambda i, hp: (i, p * HP + hp))

    vec = pl.BlockSpec((1, LANES), lambda i, hp: (0, 0))
    aug = pl.BlockSpec((2, T, LANES), lambda i, hp: (hp, i, 0))
    return pl.pallas_call(
        body, name=name, grid=(S // T, HP),
        in_specs=[part(0), part(1), part(2), pl.BlockSpec((T, LANES), lambda i, hp: (i, 0)), vec, vec],
        out_specs=[aug, aug, aug, pl.BlockSpec((2, LANES, T), lambda i, hp: (hp, 0, i))],
        out_shape=[jax.ShapeDtypeStruct((2 * HP, S, LANES), BF)] * 3 + [jax.ShapeDtypeStruct((2 * HP, LANES, S), BF)],
        compiler_params=_cp("parallel", "arbitrary"),
    )(proj, proj, proj, fcum, qw2, kw2)


def _fox_block(S):
    return _pick(S, 256, 16)


def _fox_skip_bounds(fcum, qn_w, kn_w, nheads):
    S = fcum.shape[0]
    B = _fox_block(S)
    qk = 8.0 * LOG2E * 1.02 * jnp.max(jnp.abs(qn_w)) * jnp.max(jnp.abs(kn_w))
    thresh = -(2.0 * qk + 160.0)
    f2 = fcum[:, :nheads] * LOG2E
    first, last = f2[0::B], f2[B - 1::B]
    nb = S // B
    blk = jnp.arange(nb)
    dead = (first[0::2, None, :] - last[None, :, :]) < thresh
    jmin = jnp.sum(dead & (blk[None, :, None] < 2 * jnp.arange(nb // 2)[:, None, None]), axis=1)
    live = (first[:, None, :] - last[None, :, :]) >= thresh
    imax = blk[:, None] + jnp.sum(live & (blk[:, None, None] > blk[None, :, None]), axis=0)
    return jmin.T.astype(jnp.int32), imax.T.astype(jnp.int32)


def _fox_fwd(jmin, qa, ka, vat, proj, *, name):
    H, S, _ = qa.shape
    HP = H // 2
    D = HP * LANES
    B = _fox_block(S)
    BQ = 2 * B
    nq = S // BQ

    def body(jmin_ref, q_ref, k_ref, vt_ref, g_ref, y_ref, o_ref, q2_ref):
        hp, i = pl.program_id(0), pl.program_id(1)
        lane = lax.broadcasted_iota(jnp.int32, (BQ, LANES), 1)
        lo = lane < FOX_DH
        in_lse = (lane >= 70) & (lane <= 72)
        causal = lax.broadcasted_iota(jnp.int32, (BQ, BQ), 0) <= lax.broadcasted_iota(jnp.int32, (BQ, BQ), 1)
        row = lax.broadcasted_iota(jnp.int32, (LANES, BQ), 0)
        m0, acc0 = jnp.full((1, BQ), -jnp.inf, F32), jnp.zeros((LANES, BQ), F32)
        outs = []
        for hh in range(2):
            qb = q_ref[hh]

            def block(j, carry, masked=False):
                m, acc = carry
                sl = pl.ds(pl.multiple_of(j * BQ, BQ), BQ)
                st = _dg(k_ref[hh, sl, :], qb, NT)
                if masked:
                    st = jnp.where(causal, st, -jnp.inf)
                m_new = jnp.maximum(m, jnp.max(st, axis=0, keepdims=True))
                p = jnp.exp2(st - m_new)
                ph = p.astype(BF)
                pl_ = (p - ph.astype(F32)).astype(BF)
                vt = vt_ref[hh, :, sl]
                pv = _dot(jnp.concatenate([vt, vt], axis=1), jnp.concatenate([ph, pl_], axis=0))
                return m_new, acc * jnp.exp2(m - m_new) + pv

            carry = lax.fori_loop(jmin_ref[2 * hp + hh, i] // 2, i, block, (m0, acc0))
            m, acc = block(i, carry, masked=True)
            l = jnp.sum(jnp.where(row == FOX_DH, acc, 0.0), axis=0, keepdims=True)
            tile = acc / l
            for n, piece in enumerate(_split3(m + jnp.log2(l))):
                tile = jnp.where(row == 70 + n, -(piece.astype(F32)), tile)
            tile = tile.T
            outs.append(tile)
            q2_ref[hh] = jnp.where(in_lse, tile, qb.astype(F32)).astype(BF)
        o = jnp.where(lo, outs[0], pltpu.roll(outs[1], FOX_DH, 1))
        o_ref[...] = o
        y_ref[...] = (o * _sigmoid(g_ref[...])).astype(BF)

    blk = pl.BlockSpec((BQ, LANES), lambda hp, i, jm: (i, hp))
    qblk = pl.BlockSpec((2, BQ, LANES), lambda hp, i, jm: (hp, i, 0))
    full = pl.BlockSpec((2, S, LANES), lambda hp, i, jm: (hp, 0, 0))
    full_t = pl.BlockSpec((2, LANES, S), lambda hp, i, jm: (hp, 0, 0))
    return pl.pallas_call(
        body, name=name,
        grid_spec=pltpu.PrefetchScalarGridSpec(
            num_scalar_prefetch=1, grid=(HP, nq),
            in_specs=[qblk, full, full_t, pl.BlockSpec((BQ, LANES), lambda hp, i, jm: (i, 3 * HP + hp))],
            out_specs=[blk, blk, qblk]),
        out_shape=[jax.ShapeDtypeStruct((S, D), BF), jax.ShapeDtypeStruct((S, D), F32),
                   jax.ShapeDtypeStruct((H, S, LANES), BF)],
        compiler_params=_cp("parallel", "arbitrary"),
    )(jmin, qa, ka, vat, proj)


def _fox_bwd_prep(dy, o, proj, *, name):
    S, D = dy.shape
    HP = D // LANES
    T = _pick(S, 512, 16)

    def body(dy_ref, o_ref, g_ref, da_ref):
        lane = lax.broadcasted_iota(jnp.int32, (T, LANES), 1)
        lo = lane < FOX_DH
        do = (dy_ref[...] * _sigmoid(g_ref[...])).astype(BF).astype(F32)
        prod = do * o_ref[...]
        d_lo = jnp.sum(jnp.where(lo, prod, 0.0), axis=-1, keepdims=True)
        d_hi = jnp.sum(jnp.where(lo, 0.0, prod), axis=-1, keepdims=True)
        for hh, delta in enumerate((d_lo, d_hi)):
            base = jnp.where(lo, do if hh == 0 else pltpu.roll(do, FOX_DH, 1), 0.0)
            da_ref[hh] = _lane_put(base, lane, 64, [-(p.astype(F32)) for p in _split3(delta)]).astype(BF)

    blk = pl.BlockSpec((T, LANES), lambda i, hp: (i, hp))
    return pl.pallas_call(
        body, name=name, grid=(S // T, HP),
        in_specs=[blk, blk, pl.BlockSpec((T, LANES), lambda i, hp: (i, 3 * HP + hp))],
        out_specs=pl.BlockSpec((2, T, LANES), lambda i, hp: (hp, i, 0)),
        out_shape=jax.ShapeDtypeStruct((2 * HP, S, LANES), BF),
        compiler_params=_cp("parallel", "arbitrary"),
    )(dy, o, proj)


def _fox_bwd(imax, q2, ka, va, doa, *, name):
    H, S, _ = q2.shape
    B = _fox_block(S)
    nb = S // B

    def body(imax_ref, q_ref, do_ref, k_ref, v_ref, dq_ref, dk_ref, dv_ref, cs_ref):
        j = pl.program_id(1)
        end = imax_ref[pl.program_id(0), j] + 1

        @pl.when(j == 0)
        def _():
            dq_ref[...] = jnp.zeros_like(dq_ref)

        kb, vb = k_ref[...], v_ref[...]

        def step(i, carry, nblk=1):
            dk_acc, dv_acc, cs_acc = carry
            rows = nblk * B
            sl = pl.ds(pl.multiple_of(i * B, B), rows)
            qb, dob = q_ref[sl, :], do_ref[sl, :]
            s = _dg(qb, kb, NT)
            ahead = lax.broadcasted_iota(jnp.int32, (rows, B), 0) - lax.broadcasted_iota(jnp.int32, (rows, B), 1)
            p = jnp.exp2(jnp.where(ahead >= (j - i) * B, s, -jnp.inf))
            ds = p * _dg(dob, vb, NT)
            dsb = ds.astype(BF)
            cs_acc = cs_acc + jnp.sum(ds.reshape(rows // 8, 8, B), axis=0)
            dv_acc = dv_acc + _dg(p.astype(BF), dob, TN)
            dk_acc = dk_acc + _dg(dsb, qb, TN)
            dq_ref[sl, :] += _dot(dsb, kb)
            return dk_acc, dv_acc, cs_acc

        zero = jnp.zeros((B, LANES), F32)
        carry = (zero, zero, jnp.zeros((8, B), F32))
        pos = j
        for U in FOX_BWD_TILES:
            n = (end - pos) // U
            carry = lax.fori_loop(0, n, lambda ii, c, pos=pos, U=U: step(pos + U * ii, c, nblk=U), carry)
            pos = pos + U * n
        dk_acc, dv_acc, cs_acc = carry
        dk_ref[...] = dk_acc
        dv_ref[...] = dv_acc
        cs_ref[...] = jnp.sum(cs_acc, axis=0, keepdims=True)

    full = pl.BlockSpec((None, S, LANES), lambda h, j, im: (h, 0, 0))
    blk = pl.BlockSpec((None, B, LANES), lambda h, j, im: (h, j, 0))
    return pl.pallas_call(
        body, name=name,
        grid_spec=pltpu.PrefetchScalarGridSpec(
            num_scalar_prefetch=1, grid=(H, nb),
            in_specs=[full, full, blk, blk],
            out_specs=[full, blk, blk, pl.BlockSpec((None, 1, B), lambda h, j, im: (h, 0, j))]),
        out_shape=[jax.ShapeDtypeStruct((H, S, LANES), F32)] * 3 + [jax.ShapeDtypeStruct((H, 1, S), F32)],
        compiler_params=_cp("parallel", "arbitrary"),
    )(imax, q2, doa, ka, va)


def _fox_bwd_post(dqa, dka, dva, proj, dy, o, qw2, kw2, *, name):
    S, D = dy.shape
    HP = D // LANES
    T = _pick(S, 512, 16)

    def body(dq_ref, dk_ref, dv_ref, q_ref, k_ref, g_ref, dy_ref, o_ref, qw_ref, kw_ref, dp_ref, dqw_ref, dkw_ref):
        @pl.when((pl.program_id(0) == 0) & (pl.program_id(1) == 0))
        def _():
            dqw_ref[...] = jnp.zeros_like(dqw_ref)
            dkw_ref[...] = jnp.zeros_like(dkw_ref)

        lane = lax.broadcasted_iota(jnp.int32, (T, LANES), 1)
        lo = lane < FOX_DH

        def pair(ref):
            return jnp.where(lo, ref[0], pltpu.roll(ref[1], FOX_DH, 1))

        def norm_bwd(xv, w, dyn, dw_ref):
            r = lax.rsqrt(_pair_stats(xv * xv, lo) + EPS)
            xr = xv * r
            dw_ref[...] += jnp.sum(dyn * xr, axis=0, keepdims=True)
            u = dyn * w
            return r * (u - xr * _pair_stats(u * xr, lo))

        dp_ref[0] = norm_bwd(q_ref[...], qw_ref[...], pair(dq_ref) * 0.125, dqw_ref).astype(BF)
        dp_ref[1] = norm_bwd(k_ref[...], kw_ref[...], pair(dk_ref) * (1.0 / LOG2E), dkw_ref).astype(BF)
        dp_ref[2] = pair(dv_ref).astype(BF)
        sg = _sigmoid(g_ref[...])
        dp_ref[3] = (dy_ref[...] * o_ref[...] * sg * (1.0 - sg)).astype(BF)

    def part(p):
        return pl.BlockSpec((T, LANES), lambda i, hp: (i, p * HP + hp))

    aug = pl.BlockSpec((2, T, LANES), lambda i, hp: (hp, i, 0))
    blk = pl.BlockSpec((T, LANES), lambda i, hp: (i, hp))
    vec = pl.BlockSpec((1, LANES), lambda i, hp: (0, 0))
    return pl.pallas_call(
        body, name=name, grid=(S // T, HP),
        in_specs=[aug, aug, aug, part(0), part(1), part(3), blk, blk, vec, vec],
        out_specs=[pl.BlockSpec((4, T, LANES), lambda i, hp: (0, i, hp)), vec, vec],
        out_shape=[jax.ShapeDtypeStruct((5, S, D), BF), jax.ShapeDtypeStruct((1, LANES), F32),
                   jax.ShapeDtypeStruct((1, LANES), F32)],
        compiler_params=_cp("arbitrary", "arbitrary"),
    )(dqa, dka, dva, proj, proj, proj, dy, o, qw2, kw2)


def _fox_dfz(colsum, nheads, proj, bf_pad, dproj, *, name):
    S = colsum.shape[0]
    H = nheads
    D = dproj.shape[2]
    T = _pick(S, 256, 16)
    nb = S // T

    def body(cs_ref, fz_ref, b_ref, _, dp_ref, db_ref, carry):
        @pl.when(pl.program_id(0) == 0)
        def _():
            carry[...] = jnp.zeros_like(carry)
            db_ref[...] = jnp.zeros_like(db_ref)

        lane = lax.broadcasted_iota(jnp.int32, (T, LANES), 1)
        df = -cs_ref[...]
        triu = jnp.where(lax.broadcasted_iota(jnp.int32, (T, T), 0) <= lax.broadcasted_iota(jnp.int32, (T, T), 1),
                         1.0, 0.0).astype(BF)
        dlogf = _tri_dot(triu, df) + carry[...]
        carry[...] = _row_of(dlogf, lax.broadcasted_iota(jnp.int32, (T, LANES), 0), 0)
        dfz = jnp.where(lane < H, dlogf * _sigmoid(-(fz_ref[...] + b_ref[...])), 0.0)
        db_ref[...] += jnp.sum(dfz, axis=0, keepdims=True)
        dp_ref[...] = jnp.zeros_like(dp_ref)
        dp_ref[:, 0:LANES] = dfz.astype(BF)

    return pl.pallas_call(
        body, name=name, grid=(nb,),
        in_specs=[pl.BlockSpec((T, LANES), lambda i: (nb - 1 - i, 0)),
                  pl.BlockSpec((T, LANES), lambda i: (nb - 1 - i, 4 * D // LANES)),
                  pl.BlockSpec((1, LANES), lambda i: (0, 0)),
                  pl.BlockSpec(memory_space=pl.ANY)],
        out_specs=[pl.BlockSpec((None, T, D), lambda i: (4, nb - 1 - i, 0)), pl.BlockSpec((1, LANES), lambda i: (0, 0))],
        out_shape=[jax.ShapeDtypeStruct(dproj.shape, BF), jax.ShapeDtypeStruct((1, LANES), F32)],
        scratch_shapes=[pltpu.VMEM((1, LANES), F32)],
        input_output_aliases={3: 0},
        compiler_params=_cp("arbitrary"),
    )(colsum, proj, bf_pad, dproj)


def _mod_fwd(c16, w, b, *, name):
    L, D, N = w.shape
    tn = _pick(N, 512)

    def body(c_ref, w_ref, b_ref, o_ref):
        cv = c_ref[...]
        ca = (cv * _sigmoid(cv)).astype(BF)
        o_ref[...] = _dot(ca, w_ref[...].astype(BF)) + b_ref[...]

    return pl.pallas_call(
        body, name=name, grid=(L, N // tn),
        in_specs=[pl.BlockSpec((16, D), lambda l, j: (0, 0)), pl.BlockSpec((None, D, tn), lambda l, j: (l, 0, j)),
                  pl.BlockSpec((None, 1, tn), lambda l, j: (l, 0, j))],
        out_specs=pl.BlockSpec((None, 16, tn), lambda l, j: (l, 0, j)),
        out_shape=jax.ShapeDtypeStruct((L, 16, N), F32),
        compiler_params=_cp("parallel", "arbitrary"),
    )(c16, w, b)


def _mod_bwd(c16, dmod, *, name):
    L, _, N = dmod.shape
    D = c16.shape[1]
    tn = _pick(N, 512)

    def body(c_ref, d_ref, o_ref):
        cv = c_ref[...]
        ca = (cv * _sigmoid(cv)).astype(BF)
        o_ref[...] = _dg(ca, d_ref[...].astype(BF), TN)

    return pl.pallas_call(
        body, name=name, grid=(L, N // tn),
        in_specs=[pl.BlockSpec((16, D), lambda l, j: (0, 0)), pl.BlockSpec((None, 16, tn), lambda l, j: (l, 0, j))],
        out_specs=pl.BlockSpec((None, D, tn), lambda l, j: (l, 0, j)),
        out_shape=jax.ShapeDtypeStruct((L, D, N), F32),
        compiler_params=_cp("parallel", "arbitrary"),
    )(c16, dmod)


def _adamw_math(w, g, m, v):
    m = ADAM_B1 * m + (1.0 - ADAM_B1) * g
    v = ADAM_B2 * v + (1.0 - ADAM_B2) * (g * g)
    m_hat = m / (1.0 - ADAM_B1 ** ADAM_STEP)
    v_hat = v / (1.0 - ADAM_B2 ** ADAM_STEP)
    return -ADAM_LR * (m_hat / (jnp.sqrt(v_hat) + ADAM_EPS) + ADAM_WD * w), m, v


def _adamw(w, g, m, v, *, g_at=None, name):
    R, C = w.shape
    row0 = 0 if g_at is None else g_at[1]
    tr = min(math.gcd(row0, 256) if row0 else 256, -(-R // 8) * 8)
    g0 = row0 // tr
    if g_at is None:
        g_spec = pl.BlockSpec((tr, C), lambda i: (i, 0))
    else:
        g_spec = pl.BlockSpec((None, tr, C), lambda i: (g_at[0], g0 + i, 0))

    def body(w_ref, g_ref, m_ref, v_ref, d_ref, mo_ref, vo_ref):
        d, mn, vn = _adamw_math(w_ref[...], g_ref[...], m_ref[...], v_ref[...])
        d_ref[...] = d
        mo_ref[...] = mn
        vo_ref[...] = vn

    blk = pl.BlockSpec((tr, C), lambda i: (i, 0))
    return pl.pallas_call(
        body, name=name, grid=(pl.cdiv(R, tr),),
        in_specs=[blk, g_spec, blk, blk],
        out_specs=[blk, blk, blk],
        out_shape=[jax.ShapeDtypeStruct((R, C), F32)] * 3,
        compiler_params=_cp("parallel"),
    )(w, g, m, v)


def _sum_parts(parts, *, name):
    P, R, C = parts.shape

    def body(p_ref, o_ref):
        acc = p_ref[0]
        for p in range(1, P):
            acc = acc + p_ref[p]
        o_ref[...] = acc

    return pl.pallas_call(
        body, name=name, grid=(1,),
        in_specs=[pl.BlockSpec((P, R, C), lambda i: (0, 0, 0))],
        out_specs=pl.BlockSpec((R, C), lambda i: (0, 0)),
        out_shape=jax.ShapeDtypeStruct((R, C), F32),
        compiler_params=_cp("arbitrary"),
    )(parts)


def _add_halves(g4, recv, c_idx, *, name):
    _, _, Rh, C = g4.shape
    tr = _pick(Rh, 256, 16)

    def body(c_ref, a_ref, b_ref, o_ref):
        o_ref[...] = (a_ref[...] + b_ref[...].astype(F32)).astype(BF)

    return pl.pallas_call(
        body, name=name,
        grid_spec=pltpu.PrefetchScalarGridSpec(
            num_scalar_prefetch=1, grid=(4, pl.cdiv(Rh, tr)),
            in_specs=[pl.BlockSpec((None, None, tr, C), lambda j, r, c: (j, c[0], r, 0)),
                      pl.BlockSpec((None, tr, C), lambda j, r, c: (j, r, 0))],
            out_specs=pl.BlockSpec((None, tr, C), lambda j, r, c: (j, r, 0))),
        out_shape=jax.ShapeDtypeStruct((4, Rh, C), BF),
        compiler_params=_cp("parallel", "arbitrary"),
    )(c_idx, g4, recv)


def _add_four(g4, from_sibling, from_chips, pos, *, name):
    _, _, Rh, C = g4.shape
    tr = _pick(Rh, 256, 16)

    def body(p_ref, a_ref, s_ref, b_ref, o_ref):
        own = a_ref[...] + s_ref[...].astype(F32)
        o_ref[...] = ((own + b_ref[0].astype(F32)) + b_ref[1].astype(F32)) + b_ref[2].astype(F32)

    return pl.pallas_call(
        body, name=name,
        grid_spec=pltpu.PrefetchScalarGridSpec(
            num_scalar_prefetch=1, grid=(pl.cdiv(Rh, tr),),
            in_specs=[pl.BlockSpec((None, None, tr, C), lambda r, p: (p[0], p[1], r, 0)),
                      pl.BlockSpec((None, tr, C), lambda r, p: (p[0], r, 0)),
                      pl.BlockSpec((3, tr, C), lambda r, p: (0, r, 0))],
            out_specs=pl.BlockSpec((None, tr, C), lambda r, p: (p[1], r, 0))),
        out_shape=jax.ShapeDtypeStruct((2, Rh, C), F32),
        compiler_params=_cp("arbitrary"),
    )(pos, g4, from_sibling, from_chips)


HBM = pl.BlockSpec(memory_space=pltpu.HBM)


def _mesh_pos():
    return lax.axis_index("x"), lax.axis_index("y"), lax.axis_index("c")


def _other_chips(x, y):
    return [(1 - x, y), (x, 1 - y), (1 - x, 1 - y)]


def _allgather_small(xs, *, name):
    m_per, n = xs.shape

    def body(x_ref, out_ref, send_sems, recv_sems, local_sem):
        x, y, c = _mesh_pos()
        me, sibling = (x, y, c), (x, y, 1 - c)
        chips = _other_chips(x, y)

        def rows(px, py, pc):
            return out_ref.at[pl.ds((4 * px + 2 * py + pc) * m_per, m_per), :]

        def copy(k, block, to, src=None):
            return pltpu.make_async_remote_copy(
                src_ref=rows(*block) if src is None else src, dst_ref=rows(*block),
                send_sem=send_sems.at[k], recv_sem=recv_sems.at[k], device_id=to, device_id_type=MESH)

        mine = pltpu.make_async_copy(x_ref, rows(*me), local_sem)
        mine.start()
        first = [copy(0, me, sibling, src=x_ref)]
        first += [copy(1 + j, me, (*chip, c), src=x_ref) for j, chip in enumerate(chips)]
        for cp in first:
            cp.start()
        passed = [copy(4 + j, (*chip, c), sibling) for j, chip in enumerate(chips)]
        for j, chip in enumerate(chips):
            copy(1 + j, (*chip, c), me).wait_recv()
            passed[j].start()
        copy(0, sibling, me).wait_recv()
        for j, chip in enumerate(chips):
            copy(4 + j, (*chip, 1 - c), me).wait_recv()
        for cp in first + passed:
            cp.wait_send()
        mine.wait()

    return pl.pallas_call(
        body, name=name,
        out_shape=jax.ShapeDtypeStruct((N_DEV * m_per, n), xs.dtype),
        in_specs=[pl.BlockSpec(memory_space=pltpu.VMEM)],
        out_specs=pl.BlockSpec(memory_space=pltpu.VMEM),
        scratch_shapes=[pltpu.SemaphoreType.DMA((7,)), pltpu.SemaphoreType.DMA((7,)), pltpu.SemaphoreType.DMA],
    )(xs)


def _chip_slab_copies(s_ref, out_ref, send_sems, recv_sems):
    R = s_ref.shape[0]
    Rh = R // 2
    x, y, c = _mesh_pos()
    me, sibling = (x, y, c), (x, y, 1 - c)
    chips = _other_chips(x, y)

    def half(px, py, pc):
        return out_ref.at[2 * px + py, pl.ds(pc * Rh, Rh), :]

    def copy(k, block, to, src=None):
        return pltpu.make_async_remote_copy(
            src_ref=half(*block) if src is None else src, dst_ref=half(*block),
            send_sem=send_sems.at[k], recv_sem=recv_sems.at[k], device_id=to, device_id_type=MESH)

    first = [copy(j, me, (*chip, c), src=s_ref.at[pl.ds(c * Rh, Rh), :]) for j, chip in enumerate(chips)]
    passed = [copy(3 + j, (*chip, c), sibling) for j, chip in enumerate(chips)]
    landed = [copy(j, (*chip, c), me) for j, chip in enumerate(chips)]
    from_sibling = [copy(3 + j, (*chip, 1 - c), me) for j, chip in enumerate(chips)]
    return first, passed, landed, from_sibling


def _allgather_chip_slabs(slab, *, name):
    R, C = slab.shape

    def body(s_ref, out_ref, send_sems, recv_sems):
        first, passed, landed, from_sibling = _chip_slab_copies(s_ref, out_ref, send_sems, recv_sems)
        for cp in first:
            cp.start()
        for arrived, onward in zip(landed, passed):
            arrived.wait_recv()
            onward.start()
        for cp in from_sibling:
            cp.wait_recv()
        for cp in first + passed:
            cp.wait_send()

    return pl.pallas_call(
        body, name=name,
        out_shape=jax.ShapeDtypeStruct((N_CHIPS, R, C), slab.dtype),
        in_specs=[HBM], out_specs=HBM,
        scratch_shapes=[pltpu.SemaphoreType.DMA((6,)), pltpu.SemaphoreType.DMA((6,))],
    )(slab)


def _swap_halves(mine, *, name):
    def body(g_ref, out_ref, send_sems, recv_sems):
        x, y, c = _mesh_pos()
        copies = [pltpu.make_async_remote_copy(
            src_ref=g_ref.at[j], dst_ref=out_ref.at[j], send_sem=send_sems.at[j], recv_sem=recv_sems.at[j],
            device_id=(x, y, 1 - c), device_id_type=MESH) for j in range(N_CHIPS)]
        for cp in copies:
            cp.start()
        for cp in copies:
            cp.wait()

    return pl.pallas_call(
        body, name=name,
        out_shape=jax.ShapeDtypeStruct(mine.shape, mine.dtype),
        in_specs=[HBM], out_specs=HBM,
        scratch_shapes=[pltpu.SemaphoreType.DMA((N_CHIPS,)), pltpu.SemaphoreType.DMA((N_CHIPS,))],
    )(mine)


def _scatter_partials(part, *, name):
    _, Rh, C = part.shape

    def body(p_ref, out_ref, send_sems, recv_sems):
        x, y, c = _mesh_pos()
        copies = [pltpu.make_async_remote_copy(
            src_ref=p_ref.at[2 * px + py], dst_ref=out_ref.at[j], send_sem=send_sems.at[j], recv_sem=recv_sems.at[j],
            device_id=(px, py, c), device_id_type=MESH) for j, (px, py) in enumerate(_other_chips(x, y))]
        for cp in copies:
            cp.start()
        for cp in copies:
            cp.wait()

    return pl.pallas_call(
        body, name=name,
        out_shape=jax.ShapeDtypeStruct((3, Rh, C), part.dtype),
        in_specs=[HBM], out_specs=HBM,
        scratch_shapes=[pltpu.SemaphoreType.DMA((3,)), pltpu.SemaphoreType.DMA((3,))],
    )(part)


def _join_halves(buf, *, name):
    def body(b_ref, out_ref, send_sem, recv_sem):
        x, y, c = _mesh_pos()
        cp = pltpu.make_async_remote_copy(
            src_ref=b_ref.at[c], dst_ref=out_ref.at[c], send_sem=send_sem, recv_sem=recv_sem,
            device_id=(x, y, 1 - c), device_id_type=MESH)
        cp.start()
        cp.wait()

    return pl.pallas_call(
        body, name=name,
        out_shape=jax.ShapeDtypeStruct(buf.shape, buf.dtype),
        in_specs=[HBM], out_specs=HBM, input_output_aliases={0: 0},
        scratch_shapes=[pltpu.SemaphoreType.DMA, pltpu.SemaphoreType.DMA],
    )(buf)


def _pad_rows(a, mult):
    pad = (-a.shape[0]) % mult
    return a if pad == 0 else jnp.pad(a, ((0, pad),) + ((0, 0),) * (a.ndim - 1))


def _local_step(x, target, mod, wts, small, slab_rest=None, unpack_rest=None):
    S, D = x.shape
    HP = D // LANES
    row = lambda v: v.reshape(1, -1)
    msplit = [[row(mod[i, k * D:(k + 1) * D]) for k in range(6)] for i in range(2)]
    gw, gs = {}, {}
    dmod = [[None] * 6 for _ in range(2)]

    sh1, sc1, g1, sh2, sc2, g2 = msplit[0]
    n1w0, n2w0 = row(small["norm1_w"][0]), row(small["norm2_w"][0])
    proj0, h1_0 = _ln_matmul(x, n1w0, sc1, sh1, wts["hg_w_in"], relu2=False, name="hg_in_proj")
    gn = small["hg_gn_w"].reshape(1, LANES)
    ypre0, o0, states, *gathered = _hg_fwd(proj0, small["hg_lb"], gn, slab_rest, name="hg_fwd")
    if slab_rest is not None:
        wts = {**wts, **unpack_rest(gathered[0])}
    x1, ymix0 = _matmul_resid(ypre0, wts["hg_w_out"], x, g1, name="hg_out_proj")
    a0, u0, h2_0 = _ln_matmul(x1, n2w0, sc2, sh2, wts["mlp_w1_0"], relu2=True, name="mlp0_up")
    x2, ymlp0 = _matmul_resid(u0, wts["mlp_w2_0"], x1, g2, name="mlp0_down")

    sh1b, sc1b, g1b, sh2b, sc2b, g2b = msplit[1]
    n1w1, n2w1 = row(small["norm1_w"][1]), row(small["norm2_w"][1])
    proj1, h1_1 = _ln_matmul(x2, n1w1, sc1b, sh1b, wts["fox_w_in"], relu2=False, name="fox_in_proj")
    nheads = 2 * HP
    bf_pad = jnp.pad(small["fox_b_f"].reshape(1, nheads), ((0, 0), (0, LANES - nheads)))
    qw2 = jnp.tile(small["fox_qn_w"].reshape(1, FOX_DH), (1, 2))
    kw2 = jnp.tile(small["fox_kn_w"].reshape(1, FOX_DH), (1, 2))
    fcum = _fox_cumsum(proj1, bf_pad, name="fox_cumsum")
    qa, ka, va, vat = _fox_prep(proj1, fcum, qw2, kw2, name="fox_prep")
    jmin, imax = _fox_skip_bounds(fcum, small["fox_qn_w"], small["fox_kn_w"], nheads)
    ypre1, o1, q2 = _fox_fwd(jmin, qa, ka, vat, proj1, name="fox_fwd")
    x3, ymix1 = _matmul_resid(ypre1, wts["fox_w_out"], x2, g1b, name="fox_out_proj")
    a1, u1, h2_1 = _ln_matmul(x3, n2w1, sc2b, sh2b, wts["mlp_w1_1"], relu2=True, name="mlp1_up")
    x4, ymlp1 = _matmul_resid(u1, wts["mlp_w2_1"], x3, g2b, name="mlp1_down")

    loss, dx4, dfw = _loss_kernel(x4, row(small["final_w"]), target, name="loss")
    gs["final_w"] = dfw.reshape(-1)

    def mlp_bwd(i, dx_out, x_in, h2, a, u, ymlp, n2w, sc2_, g2_):
        dz, dm, dg2 = _gate_matmul_nt(dx_out, g2_, ymlp, wts[f"mlp_w2_{i}"], a, name=f"mlp{i}_down_bwd")
        gw[f"mlp_w2_{i}"] = _matmul_tn(u, dm[None], name=f"mlp{i}_dw2")
        gw[f"mlp_w1_{i}"] = _matmul_tn(h2, dz[None], name=f"mlp{i}_dw1")
        dx_in, dsc, dsh, dnw = _matmul_nt_lnbwd(dz[None], wts[f"mlp_w1_{i}"], x_in, n2w, sc2_, dx_out,
                                                name=f"mlp{i}_up_bwd")
        dmod[i][3], dmod[i][4], dmod[i][5] = dsh, dsc, dg2
        return dx_in, dnw

    dx3, dn2w1 = mlp_bwd(1, dx4, x3, h2_1, a1, u1, ymlp1, n2w1, sc2b, g2b)
    dyp1, dm1, dg1b = _gate_matmul_nt(dx3, g1b, ymix1, wts["fox_w_out"], None, name="fox_out_bwd")
    gw["fox_w_out"] = _matmul_tn(ypre1, dm1[None], name="fox_dw_out")
    doa = _fox_bwd_prep(dyp1, o1, proj1, name="fox_bwd_prep")
    dqa, dka, dva, colsum = _fox_bwd(imax, q2, ka, va, doa, name="fox_bwd")
    colsum = jnp.pad(colsum[:, 0, :].T, ((0, 0), (0, LANES - nheads)))
    dproj1, dqw, dkw = _fox_bwd_post(dqa, dka, dva, proj1, dyp1, o1, qw2, kw2, name="fox_bwd_post")
    dproj1, dbf = _fox_dfz(colsum, nheads, proj1, bf_pad, dproj1, name="fox_dfz")
    gw["fox_w_in"] = _matmul_tn(h1_1, dproj1, name="fox_dw_in")
    dx2, dsc, dsh, dn1w1 = _matmul_nt_lnbwd(dproj1, wts["fox_w_in"], x2, n1w1, sc1b, dx3, name="fox_in_bwd")
    dmod[1][0], dmod[1][1], dmod[1][2] = dsh, dsc, dg1b
    gs["fox_qn_w"] = dqw[0, :FOX_DH] + dqw[0, FOX_DH:]
    gs["fox_kn_w"] = dkw[0, :FOX_DH] + dkw[0, FOX_DH:]
    gs["fox_b_f"] = dbf[0, :nheads]

    dx1, dn2w0 = mlp_bwd(0, dx2, x1, h2_0, a0, u0, ymlp0, n2w0, sc2, g2)
    dyp0, dm0, dg1 = _gate_matmul_nt(dx1, g1, ymix0, wts["hg_w_out"], None, name="hg_out_bwd")
    gw["hg_w_out"] = _matmul_tn(ypre0, dm0[None], name="hg_dw_out")
    dproj0, dlb, dgn = _hg_bwd(proj0, small["hg_lb"], gn, o0, states, dyp0, name="hg_bwd")
    gw["hg_w_in"] = _matmul_tn(h1_0, dproj0, name="hg_dw_in")
    dx0, dsc, dsh, dn1w0 = _matmul_nt_lnbwd(dproj0, wts["hg_w_in"], x, n1w0, sc1, dx1, name="hg_in_bwd")
    dmod[0][0], dmod[0][1], dmod[0][2] = dsh, dsc, dg1
    gs["hg_lb"] = dlb
    gs["hg_gn_w"] = jnp.sum(dgn, axis=0)

    gs["norm1_w"] = jnp.concatenate([dn1w0, dn1w1], axis=0)
    gs["norm2_w"] = jnp.concatenate([dn2w0, dn2w1], axis=0)
    gs["dmod"] = jnp.stack([jnp.concatenate(dmod[i], axis=1)[0] for i in range(2)])
    return loss, dx0, gw, gs


def _pack_halves(layout):
    rh = -(-max(sum(a.shape[0] for _, a in half) for half in layout) // 16) * 16
    place, parts = {}, []
    for h, half in enumerate(layout):
        off = 0
        for n, a in half:
            place[n] = (h, off, a.shape[0])
            off += a.shape[0]
        parts.append(jnp.pad(jnp.concatenate([a.astype(BF) for _, a in half], axis=0), ((0, rh - off), (0, 0))))
    return jnp.concatenate(parts, axis=0), place, rh


SMALL_NAMES = ["norm1_w", "norm2_w", "hg_lb", "hg_gn_w", "fox_b_f", "fox_qn_w", "fox_kn_w", "final_w"]


def _pack_small(d, names):
    rows, offs, r0 = [], {}, 0
    for n in names:
        flat = d[n].reshape(-1)
        nr = -(-flat.shape[0] // LANES)
        rows.append(jnp.pad(flat, (0, nr * LANES - flat.shape[0])).reshape(nr, LANES))
        offs[n] = (r0, nr)
        r0 += nr
    return jnp.concatenate(rows, axis=0), offs


def _unpack_small(packed, offs, name, like):
    r0, nr = offs[name]
    return packed[r0:r0 + nr].reshape(-1)[:like.size].reshape(like.shape)


def kernel(x, c, w_mod, b_mod, norm1_w, norm2_w, hg_w_in, hg_w_out, hg_lb, hg_gn_w, fox_w_in, fox_b_f, fox_qn_w, fox_kn_w, fox_w_out, mlp_w1, mlp_w2, final_w, loss_target, m_w_mod, m_b_mod, m_norm1_w, m_norm2_w, m_hg_w_in, m_hg_w_out, m_hg_lb, m_hg_gn_w, m_fox_w_in, m_fox_b_f, m_fox_qn_w, m_fox_kn_w, m_fox_w_out, m_mlp_w1, m_mlp_w2, m_final_w, v_w_mod, v_b_mod, v_norm1_w, v_norm2_w, v_hg_w_in, v_hg_w_out, v_hg_lb, v_hg_gn_w, v_fox_w_in, v_fox_b_f, v_fox_qn_w, v_fox_kn_w, v_fox_w_out, v_mlp_w1, v_mlp_w2, v_final_w):
    S, D = x.shape[1], x.shape[2]
    nheads = D // FOX_DH
    ax, ay, ac = _mesh_pos()
    chip = 2 * ax + ay
    dev = 2 * chip + ac
    xs, tgt = x.reshape(S, D), loss_target.reshape(S, D)

    c_all = _allgather_small(_pad_rows(c.reshape(-1, LANES), 8), name="gather_c")
    c_all = c_all.reshape(N_DEV, -1)[:, :D]
    c16 = _pad_rows(c_all, 16)
    nmod = w_mod.shape[2]
    b_shard = lax.dynamic_slice_in_dim(b_mod, chip * nmod, nmod, axis=1)
    mod_shard = _mod_fwd(c16, w_mod, b_shard[:, None, :], name="mod_fwd")[:, :N_DEV]
    mod_all = _allgather_small(mod_shard.reshape(-1, LANES), name="gather_mod")
    mod_all = mod_all.reshape(N_CHIPS, 2, 2, N_DEV, nmod)[:, 0]
    mod = lax.dynamic_index_in_dim(mod_all, dev, axis=2, keepdims=False)
    mod = mod.transpose(1, 0, 2).reshape(2, N_CHIPS * nmod)

    fox_rows = fox_w_in.shape[2]
    col = lambda g: g.transpose(1, 0, 2).reshape(g.shape[1], -1)
    rowsh = lambda g: g.reshape(-1, g.shape[2])
    own = lambda g, s: lax.dynamic_update_index_in_dim(g, s, chip, 0)

    slab_in = hg_w_in[0].astype(BF)
    wts = {"hg_w_in": col(own(_allgather_chip_slabs(slab_in, name="gather_hg_w_in"), slab_in))}
    slab_rest, place_rest, rh_rest = _pack_halves(
        [[("mlp_w1", mlp_w1.reshape(2 * D, D)), ("hg_w_out", hg_w_out[0]), ("fox_w_out", fox_w_out[0])],
         [("mlp_w2", mlp_w2.reshape(2 * D, D)), ("fox_w_in", fox_w_in[0].reshape(fox_rows, D))]])

    def unpack_rest(gathered):
        gathered = own(gathered, slab_rest)

        def seg(n):
            h, off, rows = place_rest[n]
            return gathered[:, h * rh_rest + off:h * rh_rest + off + rows, :]

        w1 = seg("mlp_w1").reshape(N_CHIPS, 2, D, D)
        w2 = seg("mlp_w2").reshape(N_CHIPS, 2, D, D)
        fox_in = col(seg("fox_w_in").reshape(N_CHIPS, D, fox_rows))
        return {
            "hg_w_out": rowsh(seg("hg_w_out")), "fox_w_out": rowsh(seg("fox_w_out")),
            "mlp_w1_0": col(w1[:, 0]), "mlp_w1_1": col(w1[:, 1]), "mlp_w2_0": rowsh(w2[:, 0]), "mlp_w2_1": rowsh(w2[:, 1]),
            "fox_w_in": jnp.pad(fox_in, ((0, 0), (0, 5 * D - fox_in.shape[1]))),
        }

    small = {"norm1_w": norm1_w, "norm2_w": norm2_w, "hg_lb": hg_lb, "hg_gn_w": hg_gn_w, "fox_b_f": fox_b_f,
             "fox_qn_w": fox_qn_w, "fox_kn_w": fox_kn_w, "final_w": final_w}

    loss_part, grad_x, gw, gs = _local_step(xs, tgt, mod, wts, small, slab_rest, unpack_rest)

    layout = [[("hg_w_in", hg_w_in[0]), ("mlp_w1", mlp_w1.reshape(2 * D, D)), ("hg_w_out", hg_w_out[0])],
              [("mlp_w2", mlp_w2.reshape(2 * D, D)), ("fox_w_out", fox_w_out[0]),
               ("fox_w_in", fox_w_in[0].reshape(fox_rows, D))]]
    Rh = -(-max(sum(a.shape[0] for _, a in half) for half in layout) // 16) * 16
    place = {}
    for h, half in enumerate(layout):
        off = 0
        for n, a in half:
            place[n] = (h, off, a.shape[0])
            off += a.shape[0]
    loss = lax.psum(loss_part[0, 0], ("x", "y", "c"))

    def uncol(g, n):
        return g.reshape(g.shape[0], N_CHIPS, n).transpose(1, 0, 2)

    gseg = {
        "hg_w_in": uncol(gw["hg_w_in"], D), "hg_w_out": gw["hg_w_out"].reshape(N_CHIPS, D // 4, D),
        "fox_w_out": gw["fox_w_out"].reshape(N_CHIPS, D // 4, D),
        "mlp_w1": jnp.concatenate([uncol(gw["mlp_w1_0"], D), uncol(gw["mlp_w1_1"], D)], axis=1),
        "mlp_w2": jnp.concatenate([gw["mlp_w2_0"].reshape(N_CHIPS, D, D), gw["mlp_w2_1"].reshape(N_CHIPS, D, D)], axis=1),
        "fox_w_in": uncol(gw["fox_w_in"][:, :4 * fox_rows], fox_rows).reshape(N_CHIPS, fox_rows, D),
    }
    ghalves = []
    for half in layout:
        gh = jnp.concatenate([gseg[n] for n, _ in half], axis=1)
        ghalves.append(jnp.pad(gh, ((0, 0), (0, Rh - gh.shape[1]), (0, 0))))
    g4 = jnp.stack(ghalves, axis=1)
    to_sibling = lax.dynamic_index_in_dim(g4, 1 - ac, axis=1, keepdims=False).astype(BF)
    from_sibling = _swap_halves(to_sibling, name="rs_swap_halves")
    chip_part = _add_halves(g4, from_sibling, ac.reshape(1), name="rs_add_halves")
    from_chips = _scatter_partials(chip_part, name="rs_scatter")
    my_half = _add_four(g4, from_sibling, from_chips, jnp.stack([chip, ac]), name="rs_add_chips")
    gshard = _join_halves(my_half, name="rs_join")

    names = ["dmod"] + SMALL_NAMES
    packed, offs = _pack_small(gs, names)
    packed = _pad_rows(packed, 8)
    rp = packed.shape[0]
    parts = _allgather_small(packed, name="gather_small").reshape(N_DEV, rp, LANES)
    total = _sum_parts(parts, name="sum_small")
    r0, nr = offs["dmod"]
    dmod_all = parts[:, r0:r0 + nr].reshape(N_DEV, 2, N_CHIPS * nmod)
    dmod_shard = lax.dynamic_slice_in_dim(dmod_all, chip * nmod, nmod, axis=2).transpose(1, 0, 2)
    g_w_mod = _mod_bwd(c16, jnp.pad(dmod_shard, ((0, 0), (0, 16 - N_DEV), (0, 0))), name="mod_bwd")

    grads = {"w_mod": g_w_mod, "b_mod": _unpack_small(total, offs, "dmod", b_mod)}
    for n in SMALL_NAMES:
        grads[n] = _unpack_small(total, offs, n, small[n])

    given = dict(w_mod=(w_mod, m_w_mod, v_w_mod), b_mod=(b_mod, m_b_mod, v_b_mod), norm1_w=(norm1_w, m_norm1_w, v_norm1_w),
                 norm2_w=(norm2_w, m_norm2_w, v_norm2_w), hg_w_in=(hg_w_in, m_hg_w_in, v_hg_w_in),
                 hg_w_out=(hg_w_out, m_hg_w_out, v_hg_w_out), hg_lb=(hg_lb, m_hg_lb, v_hg_lb),
                 hg_gn_w=(hg_gn_w, m_hg_gn_w, v_hg_gn_w), fox_w_in=(fox_w_in, m_fox_w_in, v_fox_w_in),
                 fox_b_f=(fox_b_f, m_fox_b_f, v_fox_b_f), fox_qn_w=(fox_qn_w, m_fox_qn_w, v_fox_qn_w),
                 fox_kn_w=(fox_kn_w, m_fox_kn_w, v_fox_kn_w), fox_w_out=(fox_w_out, m_fox_w_out, v_fox_w_out),
                 mlp_w1=(mlp_w1, m_mlp_w1, v_mlp_w1), mlp_w2=(mlp_w2, m_mlp_w2, v_mlp_w2), final_w=(final_w, m_final_w, v_final_w))
    upd = {}

    for n, (h, off, rows) in place.items():
        w, m, v = given[n]
        flat = lambda a: a.reshape(rows, D)
        d, mn, vn = _adamw(flat(w), gshard, flat(m), flat(v), g_at=(h, off), name=f"adamw_{n}")
        grads[n] = gshard[h, off:off + rows].reshape(w.shape)
        upd[n] = tuple(a.reshape(w.shape) for a in (d, mn, vn))

    w, m, v = given["w_mod"]
    flat = lambda a: a.reshape(-1, nmod)
    upd["w_mod"] = tuple(a.reshape(w.shape) for a in _adamw(flat(w), flat(g_w_mod), flat(m), flat(v), name="adamw_w_mod"))

    snames = ["b_mod"] + SMALL_NAMES
    pw, soffs = _pack_small({n: given[n][0] for n in snames}, snames)
    pm, _ = _pack_small({n: given[n][1] for n in snames}, snames)
    pv, _ = _pack_small({n: given[n][2] for n in snames}, snames)
    pg, _ = _pack_small({n: grads[n] for n in snames}, snames)
    pw, pm, pv, pg = (_pad_rows(a, 8) for a in (pw, pm, pv, pg))
    sd, smn, svn = _adamw(pw, pg, pm, pv, name="adamw_small")
    for n in snames:
        like = given[n][0]
        upd[n] = tuple(_unpack_small(a, soffs, n, like) for a in (sd, smn, svn))

    order = ["w_mod", "b_mod", "norm1_w", "norm2_w", "hg_w_in", "hg_w_out", "hg_lb", "hg_gn_w", "fox_w_in", "fox_b_f",
             "fox_qn_w", "fox_kn_w", "fox_w_out", "mlp_w1", "mlp_w2", "final_w"]
    return (loss, grad_x.reshape(x.shape), *[grads[n] for n in order], *[upd[n][0] for n in order],
            *[upd[n][1] for n in order], *[upd[n][2] for n in order])
```

```python
import math

import jax
import jax.numpy as jnp
from jax import lax
from jax.experimental import pallas as pl
from jax.experimental.pallas import tpu as pltpu

EPS = 1e-6
ADAM_LR, ADAM_B1, ADAM_B2, ADAM_EPS, ADAM_WD, ADAM_STEP = 0.001, 0.9, 0.999, 1e-08, 0.01, 10

F32 = jnp.float32
BF = jnp.bfloat16
LANES = 128
HG_CHUNK = 64
HG_HEADS_PER_STEP = 8
HG_TOKENS_PER_STEP = 256
FOX_BWD_TILES = (8, 4, 2, 1)
LOG2E = 1.4426950408889634
FOX_DH = 64
N_CHIPS = 4
N_DEV = 8
VMEM_LIMIT = 56 * 1024 * 1024
MESH = pl.DeviceIdType.MESH

NT = (((1,), (1,)), ((), ()))
TN = (((0,), (0,)), ((), ()))


def _pick(n, pref, mult=LANES):
    if n <= pref:
        return n
    t = (pref // mult) * mult
    while t >= mult:
        if n % t == 0:
            return t
        t -= mult
    raise ValueError((n, pref, mult))


def _cp(*sem):
    return pltpu.CompilerParams(dimension_semantics=sem, vmem_limit_bytes=VMEM_LIMIT)


def _dot(a, b):
    return jnp.dot(a, b, preferred_element_type=F32)


def _dg(a, b, dims):
    return lax.dot_general(a, b, dims, preferred_element_type=F32)


def _split3(x):
    hi = x.astype(BF)
    r1 = x - hi.astype(F32)
    mid = r1.astype(BF)
    lo = (r1 - mid.astype(F32)).astype(BF)
    return hi, mid, lo


def _tri_dot(tri, x):
    hi, mid, lo = _split3(x)
    return _dot(tri, hi) + _dot(tri, mid) + _dot(tri, lo)


def _dg3(a, b, dims):
    ah, bh = a.astype(BF), b.astype(BF)
    al, bl = (a - ah.astype(F32)).astype(BF), (b - bh.astype(F32)).astype(BF)
    return _dg(ah, bh, dims) + _dg(ah, bl, dims) + _dg(al, bh, dims)


def _dg1(a, b, dims):
    return _dg(a.astype(BF), b.astype(BF), dims)


NN = (((1,), (0,)), ((), ()))


def _sigmoid(x):
    return jax.nn.sigmoid(x)


def _ln_matmul(x, nw, sc, sh, w, *, relu2, name):
    S, D = x.shape
    N = w.shape[1]
    tm, tn = _pick(S, 512, 16), N

    def body(x_ref, nw_ref, sc_ref, sh_ref, w_ref, *rest):
        outs, hs = rest[:-1], rest[-1]
        h_ref = outs[-1]

        @pl.when(pl.program_id(1) == 0)
        def _():
            xv = x_ref[...]
            r = lax.rsqrt(jnp.mean(xv * xv, axis=-1, keepdims=True) + EPS)
            hb = ((xv * r * nw_ref[...]) * (1.0 + sc_ref[...]) + sh_ref[...]).astype(BF)
            hs[...] = hb
            h_ref[...] = hb

        z = _dot(hs[...], w_ref[...])
        if relu2:
            a = jnp.maximum(z, 0.0)
            outs[0][...] = a.astype(BF)
            outs[1][...] = (a * a).astype(BF)
        else:
            outs[0][...] = z

    vec = pl.BlockSpec((1, D), lambda i, j: (0, 0))
    tile = pl.BlockSpec((tm, tn), lambda i, j: (i, j))
    if relu2:
        out_shape = [jax.ShapeDtypeStruct((S, N), BF), jax.ShapeDtypeStruct((S, N), BF)]
        out_specs = [tile, tile]
    else:
        out_shape = [jax.ShapeDtypeStruct((S, N), F32)]
        out_specs = [tile]
    out_shape.append(jax.ShapeDtypeStruct((S, D), BF))
    out_specs.append(pl.BlockSpec((tm, D), lambda i, j: (i, 0)))
    return pl.pallas_call(
        body, name=name, grid=(S // tm, N // tn),
        in_specs=[pl.BlockSpec((tm, D), lambda i, j: (i, 0)), vec, vec, vec,
                  pl.BlockSpec((D, tn), lambda i, j: (0, j))],
        out_specs=out_specs, out_shape=out_shape,
        scratch_shapes=[pltpu.VMEM((tm, D), BF)],
        compiler_params=_cp("parallel", "arbitrary"),
    )(x, nw, sc, sh, w)


def _matmul_resid(a, w, x, gate, *, name):
    S, K = a.shape
    D = w.shape[1]
    tm, tn = _pick(S, 1024 if K <= 1024 else 512, 16), D

    def body(a_ref, w_ref, x_ref, g_ref, o_ref, y_ref):
        y = _dot(a_ref[...], w_ref[...])
        y_ref[...] = y.astype(BF)
        o_ref[...] = x_ref[...] + g_ref[...] * y

    tile = pl.BlockSpec((tm, tn), lambda i, j: (i, j))
    return pl.pallas_call(
        body, name=name, grid=(S // tm, D // tn),
        in_specs=[pl.BlockSpec((tm, K), lambda i, j: (i, 0)), pl.BlockSpec((K, tn), lambda i, j: (0, j)),
                  tile, pl.BlockSpec((1, tn), lambda i, j: (0, j))],
        out_specs=[tile, tile],
        out_shape=[jax.ShapeDtypeStruct((S, D), F32), jax.ShapeDtypeStruct((S, D), BF)],
        compiler_params=_cp("parallel", "arbitrary"),
    )(a, w, x, gate)


def _gate_matmul_nt(dx, gate, y, w, act, *, name):
    S, D = dx.shape
    K = w.shape[0]
    tm, tn = _pick(S, 1024 if K <= 1024 else 512, 16), K
    fused = act is not None

    def body(dx_ref, g_ref, y_ref, w_ref, *rest):
        if fused:
            act_ref, da_ref, dm_ref, dg_ref, ms = rest
        else:
            da_ref, dm_ref, dg_ref, ms = rest
        i, j = pl.program_id(0), pl.program_id(1)

        @pl.when((i == 0) & (j == 0))
        def _():
            dg_ref[...] = jnp.zeros_like(dg_ref)

        @pl.when(j == 0)
        def _():
            dxv = dx_ref[...]
            dmb = (dxv * g_ref[...]).astype(BF)
            ms[...] = dmb
            dm_ref[...] = dmb
            dg_ref[...] += jnp.sum(dxv * y_ref[...].astype(F32), axis=0, keepdims=True)

        da = _dg(ms[...], w_ref[...], NT)
        if fused:
            da_ref[...] = (da * (2.0 * act_ref[...].astype(F32))).astype(BF)
        else:
            da_ref[...] = da

    row = pl.BlockSpec((tm, D), lambda i, j: (i, 0))
    vec = pl.BlockSpec((1, D), lambda i, j: (0, 0))
    tile = pl.BlockSpec((tm, tn), lambda i, j: (i, j))
    in_specs = [row, vec, row, pl.BlockSpec((tn, D), lambda i, j: (j, 0))]
    args = [dx, gate, y, w]
    if fused:
        in_specs.append(tile)
        args.append(act)
    return pl.pallas_call(
        body, name=name, grid=(S // tm, K // tn),
        in_specs=in_specs, out_specs=[tile, row, vec],
        out_shape=[jax.ShapeDtypeStruct((S, K), BF if fused else F32), jax.ShapeDtypeStruct((S, D), BF),
                   jax.ShapeDtypeStruct((1, D), F32)],
        scratch_shapes=[pltpu.VMEM((tm, D), BF)],
        compiler_params=_cp("arbitrary", "arbitrary"),
    )(*args)


def _matmul_tn(a, b, *, name):
    S, Ka = a.shape
    P, _, Db = b.shape
    tk, tn, ts = _pick(Ka, 1024), _pick(Db, 1024), _pick(S, 1024, 16)
    npb = Db // tn

    def body(a_ref, b_ref, o_ref, acc):
        s = pl.program_id(2)

        @pl.when(s == 0)
        def _():
            acc[...] = jnp.zeros_like(acc)

        acc[...] += _dg(a_ref[...], b_ref[...], TN)

        @pl.when(s == pl.num_programs(2) - 1)
        def _():
            o_ref[...] = acc[...]

    return pl.pallas_call(
        body, name=name, grid=(Ka // tk, P * npb, S // ts),
        in_specs=[pl.BlockSpec((ts, tk), lambda i, j, s: (s, i)),
                  pl.BlockSpec((None, ts, tn), lambda i, j, s: (j // npb, s, j % npb))],
        out_specs=pl.BlockSpec((tk, tn), lambda i, j, s: (i, j)),
        out_shape=jax.ShapeDtypeStruct((Ka, P * Db), F32),
        scratch_shapes=[pltpu.VMEM((tk, tn), F32)],
        compiler_params=_cp("parallel", "parallel", "arbitrary"),
    )(a, b)


def _matmul_nt_lnbwd(g, w, x, nw, sc, dx_out, *, name):
    P, S, Dg = g.shape
    D = x.shape[1]
    tm = _pick(S, 512, 16)

    def body(g_ref, w_ref, x_ref, nw_ref, sc_ref, dxo_ref, dx_ref, dsc_ref, dsh_ref, dnw_ref):
        @pl.when(pl.program_id(0) == 0)
        def _():
            dsc_ref[...] = jnp.zeros_like(dsc_ref)
            dsh_ref[...] = jnp.zeros_like(dsh_ref)
            dnw_ref[...] = jnp.zeros_like(dnw_ref)

        dh = _dg(g_ref[0], w_ref[:, 0:Dg], NT)
        for p in range(1, P):
            dh = dh + _dg(g_ref[p], w_ref[:, p * Dg:(p + 1) * Dg], NT)
        xv = x_ref[...]
        nwv = nw_ref[...]
        r = lax.rsqrt(jnp.mean(xv * xv, axis=-1, keepdims=True) + EPS)
        xr = xv * r
        dn = dh * (1.0 + sc_ref[...])
        dsc_ref[...] += jnp.sum(dh * (xr * nwv), axis=0, keepdims=True)
        dsh_ref[...] += jnp.sum(dh, axis=0, keepdims=True)
        dnw_ref[...] += jnp.sum(dn * xr, axis=0, keepdims=True)
        u = dn * nwv
        dx_ref[...] = dxo_ref[...] + r * (u - xr * jnp.mean(u * xr, axis=-1, keepdims=True))

    row = pl.BlockSpec((tm, D), lambda i: (i, 0))
    vec = pl.BlockSpec((1, D), lambda i: (0, 0))
    return pl.pallas_call(
        body, name=name, grid=(S // tm,),
        in_specs=[pl.BlockSpec((P, tm, Dg), lambda i: (0, i, 0)),
                  pl.BlockSpec((D, P * Dg), lambda i: (0, 0)), row, vec, vec, row],
        out_specs=[row, vec, vec, vec],
        out_shape=[jax.ShapeDtypeStruct((S, D), F32)] + [jax.ShapeDtypeStruct((1, D), F32)] * 3,
        compiler_params=_cp("arbitrary"),
    )(g, w, x, nw, sc, dx_out)


def _loss_kernel(x, fw, tgt, *, name):
    S, D = x.shape
    tm = _pick(S, 512, 8)

    def body(x_ref, fw_ref, t_ref, l_ref, dx_ref, dfw_ref):
        @pl.when(pl.program_id(0) == 0)
        def _():
            l_ref[...] = jnp.zeros_like(l_ref)
            dfw_ref[...] = jnp.zeros_like(dfw_ref)

        xv = x_ref[...]
        fwv = fw_ref[...]
        r = lax.rsqrt(jnp.mean(xv * xv, axis=-1, keepdims=True) + EPS)
        xr = xv * r
        err = xr * fwv - t_ref[...]
        per_tok = jnp.mean(err * err, axis=-1, keepdims=True)
        l_ref[...] += 0.5 * jnp.sum(per_tok, axis=0, keepdims=True)
        dy = err * (1.0 / D)
        dfw_ref[...] += jnp.sum(dy * xr, axis=0, keepdims=True)
        u = dy * fwv
        dx_ref[...] = r * (u - xr * jnp.mean(u * xr, axis=-1, keepdims=True))

    row = pl.BlockSpec((tm, D), lambda i: (i, 0))
    vec = pl.BlockSpec((1, D), lambda i: (0, 0))
    return pl.pallas_call(
        body, name=name, grid=(S // tm,),
        in_specs=[row, vec, row],
        out_specs=[pl.BlockSpec((1, LANES), lambda i: (0, 0)), row, vec],
        out_shape=[jax.ShapeDtypeStruct((1, LANES), F32), jax.ShapeDtypeStruct((S, D), F32),
                   jax.ShapeDtypeStruct((1, D), F32)],
        compiler_params=_cp("arbitrary"),
    )(x, fw, tgt)


def _hg_lower_bound(lb3):
    mx = jnp.max(lb3, axis=0, keepdims=True)
    e = jnp.exp(lb3 - mx)
    p = e / jnp.sum(e, axis=0, keepdims=True)
    return p[0:1, :], p


def _hg_chunk_common(qr, fz, lbv):
    sq = _sigmoid(qr)
    q = qr * sq
    sig = _sigmoid(fz)
    f = lbv + (1.0 - lbv) * sig
    k = (1.0 - lbv) * (1.0 - sig)
    return q, sq, sig, f, k, jnp.log(f)


def _row_of(x, rows, r):
    return jnp.sum(jnp.where(rows == r, x, 0.0), axis=0, keepdims=True)


def _hg_fwd(proj, hg_lb, gn, slab=None, *, name):
    S = proj.shape[0]
    D = proj.shape[1] // 4
    H = D // LANES
    HB = min(HG_HEADS_PER_STEP, H)
    W = HB * LANES
    C = HG_CHUNK
    T = _pick(S, HG_TOKENS_PER_STEP, C)
    nch, nb = T // C, S // T
    ng = H // HB
    fused = slab is not None

    def body(q_ref, fz_ref, v_ref, g_ref, lb_ref, gn_ref, *rest):
        if fused:
            s_ref, y_ref, o_ref, sts_ref, out_ref, st, send_sems, recv_sems = rest
            first, passed, landed, from_sibling = _chip_slab_copies(s_ref, out_ref, send_sems, recv_sems)
            hgrp, n = pl.program_id(0), pl.program_id(1)

            @pl.when((hgrp == 0) & (n == 0))
            def _():
                for cp in first:
                    cp.start()

            @pl.when((hgrp == ng - 1) & (n == (3 * nb) // 4))
            def _():
                for arrived, onward in zip(landed, passed):
                    arrived.wait_recv()
                    onward.start()
        else:
            y_ref, o_ref, sts_ref, st = rest

        @pl.when(pl.program_id(1) == 0)
        def _():
            st[...] = jnp.zeros_like(st)

        lb_all, _ = _hg_lower_bound(lb_ref[...])
        gnv = gn_ref[...]
        ri = lax.broadcasted_iota(jnp.int32, (C, C), 0)
        ci_ = lax.broadcasted_iota(jnp.int32, (C, C), 1)
        low = ri >= ci_
        tri = jnp.where(low, 1.0, 0.0).astype(BF)
        rows = lax.broadcasted_iota(jnp.int32, (C, LANES), 0)

        def chunk(ci, carry):
            sl = pl.ds(pl.multiple_of(ci * C, C), C)
            for hh in range(HB):
                ls = slice(hh * LANES, (hh + 1) * LANES)
                q, _, _, _, k, logf = _hg_chunk_common(q_ref[sl, ls], fz_ref[sl, ls], lb_all[:, ls])
                vv = v_ref[sl, ls]
                gg = g_ref[sl, ls]
                G = _tri_dot(tri, logf)
                Gm = _row_of(G, rows, C // 2 - 1)
                Gl = _row_of(G, rows, C - 1)
                qt = q * jnp.exp(G - Gm)
                kt = k * jnp.exp(Gm - G)
                A = jnp.where(low, _dg1(qt, kt, NT), 0.0)
                Sv = st[hh]
                sts_ref[hh, ci] = Sv
                o = _dg1(A, vv, NN) + _dg1(q * jnp.exp(G), Sv, NT)
                st[hh] = Sv * jnp.exp(Gl) + _dg1(vv, k * jnp.exp(Gl - G), TN)
                r = lax.rsqrt(jnp.mean(o * o, axis=-1, keepdims=True) + EPS)
                y_ref[sl, ls] = ((o * r * gnv) * (gg * _sigmoid(gg))).astype(BF)
                o_ref[sl, ls] = o
            return carry

        lax.fori_loop(0, nch, chunk, 0)

        if fused:
            @pl.when((hgrp == ng - 1) & (n == nb - 1))
            def _():
                for cp in from_sibling:
                    cp.wait_recv()
                for cp in first + passed:
                    cp.wait_send()

    def part(p):
        return pl.BlockSpec((T, W), lambda h, n: (n, p * ng + h))

    blk = pl.BlockSpec((T, W), lambda h, n: (n, h))
    in_specs = [part(0), part(1), part(2), part(3),
                pl.BlockSpec((3, W), lambda h, n: (0, h)), pl.BlockSpec((1, LANES), lambda h, n: (0, 0))]
    out_specs = [blk, blk, pl.BlockSpec((HB, nch, LANES, LANES), lambda h, n: (h, n, 0, 0))]
    out_shape = [jax.ShapeDtypeStruct((S, D), BF), jax.ShapeDtypeStruct((S, D), F32),
                 jax.ShapeDtypeStruct((H, S // C, LANES, LANES), F32)]
    scratch = [pltpu.VMEM((HB, LANES, LANES), F32)]
    args = [proj, proj, proj, proj, hg_lb, gn]
    if fused:
        in_specs.append(HBM)
        out_specs.append(HBM)
        out_shape.append(jax.ShapeDtypeStruct((N_CHIPS,) + slab.shape, slab.dtype))
        scratch += [pltpu.SemaphoreType.DMA((6,)), pltpu.SemaphoreType.DMA((6,))]
        args.append(slab)
    return pl.pallas_call(
        body, name=name, grid=(ng, nb), in_specs=in_specs, out_specs=out_specs, out_shape=out_shape,
        scratch_shapes=scratch, compiler_params=_cp("arbitrary", "arbitrary"),
    )(*args)


def _hg_bwd(proj, hg_lb, gn, o_all, states, dy, part=None, *, name):
    S = proj.shape[0]
    D = proj.shape[1] // 4
    H = D // LANES
    HB = min(HG_HEADS_PER_STEP, H)
    W = HB * LANES
    C = HG_CHUNK
    T = _pick(S, HG_TOKENS_PER_STEP, C)
    nch, nb = T // C, S // T
    ng = H // HB
    fused = part is not None

    def body(q_ref, fz_ref, v_ref, g_ref, lb_ref, gn_ref, o_ref, sts_ref, dy_ref, *rest):
        if fused:
            p_ref, dp_ref, dlb_ref, dgn_ref, recv_ref, dst, dlb_acc, send_sems, recv_sems = rest
            copies = _scatter_copies(p_ref, recv_ref, send_sems, recv_sems)

            @pl.when((pl.program_id(0) == 0) & (pl.program_id(1) == 0))
            def _():
                for cp in copies:
                    cp.start()
        else:
            dp_ref, dlb_ref, dgn_ref, dst, dlb_acc = rest
        n = pl.program_id(1)

        @pl.when(n == 0)
        def _():
            dst[...] = jnp.zeros_like(dst)
            dlb_acc[...] = jnp.zeros_like(dlb_acc)
            dgn_ref[...] = jnp.zeros_like(dgn_ref)

        lb_all, p3 = _hg_lower_bound(lb_ref[...])
        gnv = gn_ref[...]
        ri = lax.broadcasted_iota(jnp.int32, (C, C), 0)
        ci_ = lax.broadcasted_iota(jnp.int32, (C, C), 1)
        low = ri >= ci_
        tri = jnp.where(low, 1.0, 0.0).astype(BF)
        triu = jnp.where(ri <= ci_, 1.0, 0.0).astype(BF)
        rows = lax.broadcasted_iota(jnp.int32, (C, LANES), 0)

        def chunk(cj, carry):
            ci = nch - 1 - cj
            sl = pl.ds(pl.multiple_of(ci * C, C), C)
            for hh in range(HB):
                ls = slice(hh * LANES, (hh + 1) * LANES)
                lbv = lb_all[:, ls]
                qr = q_ref[sl, ls]
                q, sq, sig, f, k, logf = _hg_chunk_common(qr, fz_ref[sl, ls], lbv)
                vv = v_ref[sl, ls]
                gg = g_ref[sl, ls]
                o = o_ref[sl, ls]
                dyv = dy_ref[sl, ls]
                G = _tri_dot(tri, logf)
                Gm = _row_of(G, rows, C // 2 - 1)
                Gl = _row_of(G, rows, C - 1)
                eG, e_qm, e_km, e_lk, eGl = jnp.exp(G), jnp.exp(G - Gm), jnp.exp(Gm - G), jnp.exp(Gl - G), jnp.exp(Gl)
                qt = q * e_qm
                kt = k * e_km
                A = jnp.where(low, _dg1(qt, kt, NT), 0.0)
                sg = _sigmoid(gg)
                r = lax.rsqrt(jnp.mean(o * o, axis=-1, keepdims=True) + EPS)
                on = o * r
                d_onw = dyv * (gg * sg)
                dgn_ref[hh] += jnp.sum(d_onw * on, axis=0, keepdims=True)
                dgg = dyv * (on * gnv) * (sg * (1.0 + gg * (1.0 - sg)))
                u = d_onw * gnv
                do = r * (u - on * jnp.mean(u * on, axis=-1, keepdims=True))
                Sv = sts_ref[hh, ci]
                dSv = dst[hh]
                dA = jnp.where(low, _dg3(do, vv, NT), 0.0)
                kdec = k * e_lk
                dv = _dg1(A, do, TN) + _dg1(kdec, dSv, NT)
                dq = _dg3(dA, kt, NN) * e_qm + eG * _dg3(do, Sv, NN)
                dk = _dg3(dA, qt, TN) * e_km + e_lk * _dg3(vv, dSv, NN)
                s_end = Sv * eGl + _dg3(vv, kdec, TN)
                dgl = jnp.sum(dSv * s_end, axis=0, keepdims=True)
                dG = q * dq - k * dk + jnp.where(rows == C - 1, dgl, 0.0)
                dlogf = _tri_dot(triu, dG) - f * dk
                dst[hh] = dSv * eGl + _dg1(do, q * eG, TN)
                dlf_f = dlogf / f
                dlb_acc[:, ls] += jnp.sum(dlf_f * (1.0 - sig), axis=0, keepdims=True)
                dp_ref[0, sl, ls] = (dq * (sq * (1.0 + qr * (1.0 - sq)))).astype(BF)
                dp_ref[1, sl, ls] = (dlf_f * (1.0 - lbv) * sig * (1.0 - sig)).astype(BF)
                dp_ref[2, sl, ls] = dv.astype(BF)
                dp_ref[3, sl, ls] = dgg.astype(BF)
            return carry

        lax.fori_loop(0, nch, chunk, 0)
        sel = jnp.where(lax.broadcasted_iota(jnp.int32, (3, W), 0) == 0, 1.0, 0.0)
        dlb_ref[...] = lb_all * (sel - p3) * dlb_acc[...]

        if fused:
            @pl.when((pl.program_id(0) == ng - 1) & (n == nb - 1))
            def _():
                for cp in copies:
                    cp.wait()

    def col(p):
        return pl.BlockSpec((T, W), lambda h, n: (nb - 1 - n, p * ng + h))

    blk = pl.BlockSpec((T, W), lambda h, n: (nb - 1 - n, h))
    in_specs = [col(0), col(1), col(2), col(3),
                pl.BlockSpec((3, W), lambda h, n: (0, h)), pl.BlockSpec((1, LANES), lambda h, n: (0, 0)),
                blk, pl.BlockSpec((HB, nch, LANES, LANES), lambda h, n: (h, nb - 1 - n, 0, 0)), blk]
    out_specs = [pl.BlockSpec((4, T, W), lambda h, n: (0, nb - 1 - n, h)),
                 pl.BlockSpec((3, W), lambda h, n: (0, h)),
                 pl.BlockSpec((HB, 1, LANES), lambda h, n: (h, 0, 0))]
    out_shape = [jax.ShapeDtypeStruct((4, S, D), BF), jax.ShapeDtypeStruct((3, D), F32),
                 jax.ShapeDtypeStruct((H, 1, LANES), F32)]
    scratch = [pltpu.VMEM((HB, LANES, LANES), F32), pltpu.VMEM((1, W), F32)]
    args = [proj, proj, proj, proj, hg_lb, gn, o_all, states, dy]
    if fused:
        in_specs.append(HBM)
        out_specs.append(HBM)
        out_shape.append(jax.ShapeDtypeStruct((3,) + part.shape[1:], part.dtype))
        scratch += [pltpu.SemaphoreType.DMA((3,)), pltpu.SemaphoreType.DMA((3,))]
        args.append(part)
    return pl.pallas_call(
        body, name=name, grid=(ng, nb), in_specs=in_specs, out_specs=out_specs, out_shape=out_shape,
        scratch_shapes=scratch, compiler_params=_cp("arbitrary", "arbitrary"),
    )(*args)


def _log_sigmoid(u):
    return jnp.minimum(u, 0.0) - jnp.log(1.0 + jnp.exp(-jnp.abs(u)))


def _lane_put(base, lane, first, pieces):
    for n, p in enumerate(pieces):
        base = jnp.where(lane == first + n, p, base)
    return base


def _fox_cumsum(proj, bf_pad, *, name):
    S = proj.shape[0]
    D = proj.shape[1] // 5
    T = _pick(S, 256, 8)

    def body(fz_ref, b_ref, f_ref, carry):
        @pl.when(pl.program_id(0) == 0)
        def _():
            carry[...] = jnp.zeros_like(carry)

        logf = _log_sigmoid(fz_ref[...] + b_ref[...])
        tri = jnp.where(lax.broadcasted_iota(jnp.int32, (T, T), 0) >= lax.broadcasted_iota(jnp.int32, (T, T), 1),
                        1.0, 0.0).astype(BF)
        fv = _tri_dot(tri, logf) + carry[...]
        f_ref[...] = fv
        carry[...] = _row_of(fv, lax.broadcasted_iota(jnp.int32, (T, LANES), 0), T - 1)

    return pl.pallas_call(
        body, name=name, grid=(S // T,),
        in_specs=[pl.BlockSpec((T, LANES), lambda i: (i, 4 * D // LANES)), pl.BlockSpec((1, LANES), lambda i: (0, 0))],
        out_specs=pl.BlockSpec((T, LANES), lambda i: (i, 0)),
        out_shape=jax.ShapeDtypeStruct((S, LANES), F32),
        scratch_shapes=[pltpu.VMEM((1, LANES), F32)],
        compiler_params=_cp("arbitrary"),
    )(proj, bf_pad)


def _pair_stats(sq, lo):
    del lo
    a = lax.broadcasted_iota(jnp.int32, (LANES, LANES), 0) < FOX_DH
    b = lax.broadcasted_iota(jnp.int32, (LANES, LANES), 1) < FOX_DH
    avg = jnp.where(a == b, 1.0 / FOX_DH, 0.0).astype(BF)
    hi, mid, low = _split3(sq)
    return _dot(hi, avg) + _dot(mid, avg) + _dot(low, avg)


def _fox_prep(proj, fcum, qw2, kw2, *, name):
    S = proj.shape[0]
    D = proj.shape[1] // 5
    HP = D // LANES
    T = _pick(S, 512, 16)

    def body(q_ref, k_ref, v_ref, f_ref, qw_ref, kw_ref, qa_ref, ka_ref, va_ref, vt_ref):
        hp = pl.program_id(1)
        lane = lax.broadcasted_iota(jnp.int32, (T, LANES), 1)
        lo = lane < FOX_DH
        qv, kv, vv, fv = q_ref[...], k_ref[...], v_ref[...], f_ref[...]
        qn = qv * lax.rsqrt(_pair_stats(qv * qv, lo) + EPS) * qw_ref[...] * (0.125 * LOG2E)
        kn = kv * lax.rsqrt(_pair_stats(kv * kv, lo) + EPS) * kw_ref[...]
        ones_q = jnp.where((lane >= 67) & (lane <= 69), 1.0, 0.0)
        ones_k = jnp.where(((lane >= 64) & (lane <= 66)) | ((lane >= 70) & (lane <= 72)), 1.0, 0.0)
        ones_v = jnp.where((lane >= 64) & (lane <= 66), 1.0, 0.0)
        for hh in range(2):
            fh = jnp.sum(jnp.where(lane == 2 * hp + hh, fv, 0.0), axis=-1, keepdims=True) * LOG2E
            pieces = [p.astype(F32) for p in _split3(fh)]

            def half(x):
                return jnp.where(lo, x if hh == 0 else pltpu.roll(x, FOX_DH, 1), 0.0)

            qa_ref[hh] = _lane_put(half(qn) + ones_q, lane, 64, pieces).astype(BF)
            ka_ref[hh] = _lane_put(half(kn) + ones_k, lane, 67, [-p for p in pieces]).astype(BF)
            va = half(vv) + ones_v
            va_ref[hh] = va.astype(BF)
            vt_ref[hh] = va.T.astype(BF)

    def part(p):
        return pl.BlockSpec((T, LANES), lambda i, hp: (i, p * HP + hp))

    vec = pl.BlockSpec((1, LANES), lambda i, hp: (0, 0))
    aug = pl.BlockSpec((2, T, LANES), lambda i, hp: (hp, i, 0))
    return pl.pallas_call(
        body, name=name, grid=(S // T, HP),
        in_specs=[part(0), part(1), part(2), pl.BlockSpec((T, LANES), lambda i, hp: (i, 0)), vec, vec],
        out_specs=[aug, aug, aug, pl.BlockSpec((2, LANES, T), lambda i, hp: (hp, 0, i))],
        out_shape=[jax.ShapeDtypeStruct((2 * HP, S, LANES), BF)] * 3 + [jax.ShapeDtypeStruct((2 * HP, LANES, S), BF)],
        compiler_params=_cp("parallel", "arbitrary"),
    )(proj, proj, proj, fcum, qw2, kw2)


def _fox_block(S):
    return _pick(S, 256, 16)


def _fox_skip_bounds(fcum, qn_w, kn_w, nheads):
    S = fcum.shape[0]
    B = _fox_block(S)
    qk = 8.0 * LOG2E * 1.02 * jnp.max(jnp.abs(qn_w)) * jnp.max(jnp.abs(kn_w))
    thresh = -(2.0 * qk + 160.0)
    f2 = fcum[:, :nheads] * LOG2E
    first, last = f2[0::B], f2[B - 1::B]
    nb = S // B
    blk = jnp.arange(nb)
    dead = (first[0::2, None, :] - last[None, :, :]) < thresh
    jmin = jnp.sum(dead & (blk[None, :, None] < 2 * jnp.arange(nb // 2)[:, None, None]), axis=1)
    live = (first[:, None, :] - last[None, :, :]) >= thresh
    imax = blk[:, None] + jnp.sum(live & (blk[:, None, None] > blk[None, :, None]), axis=0)
    return jmin.T.astype(jnp.int32), imax.T.astype(jnp.int32)


def _fox_fwd(jmin, qa, ka, vat, proj, *, name):
    H, S, _ = qa.shape
    HP = H // 2
    D = HP * LANES
    B = _fox_block(S)
    BQ = 2 * B
    nq = S // BQ

    def body(jmin_ref, q_ref, k_ref, vt_ref, g_ref, y_ref, o_ref, q2_ref):
        hp, i = pl.program_id(0), pl.program_id(1)
        lane = lax.broadcasted_iota(jnp.int32, (BQ, LANES), 1)
        lo = lane < FOX_DH
        in_lse = (lane >= 70) & (lane <= 72)
        causal = lax.broadcasted_iota(jnp.int32, (BQ, BQ), 0) <= lax.broadcasted_iota(jnp.int32, (BQ, BQ), 1)
        row = lax.broadcasted_iota(jnp.int32, (LANES, BQ), 0)
        m0, acc0 = jnp.full((1, BQ), -jnp.inf, F32), jnp.zeros((LANES, BQ), F32)
        outs = []
        for hh in range(2):
            qb = q_ref[hh]

            def block(j, carry, masked=False):
                m, acc = carry
                sl = pl.ds(pl.multiple_of(j * BQ, BQ), BQ)
                st = _dg(k_ref[hh, sl, :], qb, NT)
                if masked:
                    st = jnp.where(causal, st, -jnp.inf)
                m_new = jnp.maximum(m, jnp.max(st, axis=0, keepdims=True))
                p = jnp.exp2(st - m_new)
                ph = p.astype(BF)
                pl_ = (p - ph.astype(F32)).astype(BF)
                vt = vt_ref[hh, :, sl]
                pv = _dot(jnp.concatenate([vt, vt], axis=1), jnp.concatenate([ph, pl_], axis=0))
                return m_new, acc * jnp.exp2(m - m_new) + pv

            carry = lax.fori_loop(jmin_ref[2 * hp + hh, i] // 2, i, block, (m0, acc0))
            m, acc = block(i, carry, masked=True)
            l = jnp.sum(jnp.where(row == FOX_DH, acc, 0.0), axis=0, keepdims=True)
            tile = acc / l
            for n, piece in enumerate(_split3(m + jnp.log2(l))):
                tile = jnp.where(row == 70 + n, -(piece.astype(F32)), tile)
            tile = tile.T
            outs.append(tile)
            q2_ref[hh] = jnp.where(in_lse, tile, qb.astype(F32)).astype(BF)
        o = jnp.where(lo, outs[0], pltpu.roll(outs[1], FOX_DH, 1))
        o_ref[...] = o
        y_ref[...] = (o * _sigmoid(g_ref[...])).astype(BF)

    blk = pl.BlockSpec((BQ, LANES), lambda hp, i, jm: (i, hp))
    qblk = pl.BlockSpec((2, BQ, LANES), lambda hp, i, jm: (hp, i, 0))
    full = pl.BlockSpec((2, S, LANES), lambda hp, i, jm: (hp, 0, 0))
    full_t = pl.BlockSpec((2, LANES, S), lambda hp, i, jm: (hp, 0, 0))
    return pl.pallas_call(
        body, name=name,
        grid_spec=pltpu.PrefetchScalarGridSpec(
            num_scalar_prefetch=1, grid=(HP, nq),
            in_specs=[qblk, full, full_t, pl.BlockSpec((BQ, LANES), lambda hp, i, jm: (i, 3 * HP + hp))],
            out_specs=[blk, blk, qblk]),
        out_shape=[jax.ShapeDtypeStruct((S, D), BF), jax.ShapeDtypeStruct((S, D), F32),
                   jax.ShapeDtypeStruct((H, S, LANES), BF)],
        compiler_params=_cp("parallel", "arbitrary"),
    )(jmin, qa, ka, vat, proj)


def _fox_bwd_prep(dy, o, proj, *, name):
    S, D = dy.shape
    HP = D // LANES
    T = _pick(S, 512, 16)

    def body(dy_ref, o_ref, g_ref, da_ref):
        lane = lax.broadcasted_iota(jnp.int32, (T, LANES), 1)
        lo = lane < FOX_DH
        do = (dy_ref[...] * _sigmoid(g_ref[...])).astype(BF).astype(F32)
        prod = do * o_ref[...]
        d_lo = jnp.sum(jnp.where(lo, prod, 0.0), axis=-1, keepdims=True)
        d_hi = jnp.sum(jnp.where(lo, 0.0, prod), axis=-1, keepdims=True)
        for hh, delta in enumerate((d_lo, d_hi)):
            base = jnp.where(lo, do if hh == 0 else pltpu.roll(do, FOX_DH, 1), 0.0)
            da_ref[hh] = _lane_put(base, lane, 64, [-(p.astype(F32)) for p in _split3(delta)]).astype(BF)

    blk = pl.BlockSpec((T, LANES), lambda i, hp: (i, hp))
    return pl.pallas_call(
        body, name=name, grid=(S // T, HP),
        in_specs=[blk, blk, pl.BlockSpec((T, LANES), lambda i, hp: (i, 3 * HP + hp))],
        out_specs=pl.BlockSpec((2, T, LANES), lambda i, hp: (hp, i, 0)),
        out_shape=jax.ShapeDtypeStruct((2 * HP, S, LANES), BF),
        compiler_params=_cp("parallel", "arbitrary"),
    )(dy, o, proj)


def _fox_bwd(imax, q2, ka, va, doa, *, name):
    H, S, _ = q2.shape
    B = _fox_block(S)
    nb = S // B

    def body(imax_ref, q_ref, do_ref, k_ref, v_ref, dq_ref, dk_ref, dv_ref, cs_ref):
        j = pl.program_id(1)
        end = imax_ref[pl.program_id(0), j] + 1

        @pl.when(j == 0)
        def _():
            dq_ref[...] = jnp.zeros_like(dq_ref)

        kb, vb = k_ref[...], v_ref[...]

        def step(i, carry, nblk=1):
            dk_acc, dv_acc, cs_acc = carry
            rows = nblk * B
            sl = pl.ds(pl.multiple_of(i * B, B), rows)
            qb, dob = q_ref[sl, :], do_ref[sl, :]
            s = _dg(qb, kb, NT)
            ahead = lax.broadcasted_iota(jnp.int32, (rows, B), 0) - lax.broadcasted_iota(jnp.int32, (rows, B), 1)
            p = jnp.exp2(jnp.where(ahead >= (j - i) * B, s, -jnp.inf))
            ds = p * _dg(dob, vb, NT)
            dsb = ds.astype(BF)
            cs_acc = cs_acc + jnp.sum(ds.reshape(rows // 8, 8, B), axis=0)
            dv_acc = dv_acc + _dg(p.astype(BF), dob, TN)
            dk_acc = dk_acc + _dg(dsb, qb, TN)
            dq_ref[sl, :] += _dot(dsb, kb)
            return dk_acc, dv_acc, cs_acc

        zero = jnp.zeros((B, LANES), F32)
        carry = (zero, zero, jnp.zeros((8, B), F32))
        pos = j
        for U in FOX_BWD_TILES:
            n = (end - pos) // U
            carry = lax.fori_loop(0, n, lambda ii, c, pos=pos, U=U: step(pos + U * ii, c, nblk=U), carry)
            pos = pos + U * n
        dk_acc, dv_acc, cs_acc = carry
        dk_ref[...] = dk_acc
        dv_ref[...] = dv_acc
        cs_ref[...] = jnp.sum(cs_acc, axis=0, keepdims=True)

    full = pl.BlockSpec((None, S, LANES), lambda h, j, im: (h, 0, 0))
    blk = pl.BlockSpec((None, B, LANES), lambda h, j, im: (h, j, 0))
    return pl.pallas_call(
        body, name=name,
        grid_spec=pltpu.PrefetchScalarGridSpec(
            num_scalar_prefetch=1, grid=(H, nb),
            in_specs=[full, full, blk, blk],
            out_specs=[full, blk, blk, pl.BlockSpec((None, 1, B), lambda h, j, im: (h, 0, j))]),
        out_shape=[jax.ShapeDtypeStruct((H, S, LANES), F32)] * 3 + [jax.ShapeDtypeStruct((H, 1, S), F32)],
        compiler_params=_cp("parallel", "arbitrary"),
    )(imax, q2, doa, ka, va)


def _fox_bwd_post(dqa, dka, dva, proj, dy, o, qw2, kw2, *, name):
    S, D = dy.shape
    HP = D // LANES
    T = _pick(S, 512, 16)

    def body(dq_ref, dk_ref, dv_ref, q_ref, k_ref, g_ref, dy_ref, o_ref, qw_ref, kw_ref, dp_ref, dqw_ref, dkw_ref):
        @pl.when((pl.program_id(0) == 0) & (pl.program_id(1) == 0))
        def _():
            dqw_ref[...] = jnp.zeros_like(dqw_ref)
            dkw_ref[...] = jnp.zeros_like(dkw_ref)

        lane = lax.broadcasted_iota(jnp.int32, (T, LANES), 1)
        lo = lane < FOX_DH

        def pair(ref):
            return jnp.where(lo, ref[0], pltpu.roll(ref[1], FOX_DH, 1))

        def norm_bwd(xv, w, dyn, dw_ref):
            r = lax.rsqrt(_pair_stats(xv * xv, lo) + EPS)
            xr = xv * r
            dw_ref[...] += jnp.sum(dyn * xr, axis=0, keepdims=True)
            u = dyn * w
            return r * (u - xr * _pair_stats(u * xr, lo))

        dp_ref[0] = norm_bwd(q_ref[...], qw_ref[...], pair(dq_ref) * 0.125, dqw_ref).astype(BF)
        dp_ref[1] = norm_bwd(k_ref[...], kw_ref[...], pair(dk_ref) * (1.0 / LOG2E), dkw_ref).astype(BF)
        dp_ref[2] = pair(dv_ref).astype(BF)
        sg = _sigmoid(g_ref[...])
        dp_ref[3] = (dy_ref[...] * o_ref[...] * sg * (1.0 - sg)).astype(BF)

    def part(p):
        return pl.BlockSpec((T, LANES), lambda i, hp: (i, p * HP + hp))

    aug = pl.BlockSpec((2, T, LANES), lambda i, hp: (hp, i, 0))
    blk = pl.BlockSpec((T, LANES), lambda i, hp: (i, hp))
    vec = pl.BlockSpec((1, LANES), lambda i, hp: (0, 0))
    return pl.pallas_call(
        body, name=name, grid=(S // T, HP),
        in_specs=[aug, aug, aug, part(0), part(1), part(3), blk, blk, vec, vec],
        out_specs=[pl.BlockSpec((4, T, LANES), lambda i, hp: (0, i, hp)), vec, vec],
        out_shape=[jax.ShapeDtypeStruct((5, S, D), BF), jax.ShapeDtypeStruct((1, LANES), F32),
                   jax.ShapeDtypeStruct((1, LANES), F32)],
        compiler_params=_cp("arbitrary", "arbitrary"),
    )(dqa, dka, dva, proj, proj, proj, dy, o, qw2, kw2)


def _fox_dfz(colsum, nheads, proj, bf_pad, dproj, *, name):
    S = colsum.shape[0]
    H = nheads
    D = dproj.shape[2]
    T = _pick(S, 256, 16)
    nb = S // T

    def body(cs_ref, fz_ref, b_ref, _, dp_ref, db_ref, carry):
        @pl.when(pl.program_id(0) == 0)
        def _():
            carry[...] = jnp.zeros_like(carry)
            db_ref[...] = jnp.zeros_like(db_ref)

        lane = lax.broadcasted_iota(jnp.int32, (T, LANES), 1)
        df = -cs_ref[...]
        triu = jnp.where(lax.broadcasted_iota(jnp.int32, (T, T), 0) <= lax.broadcasted_iota(jnp.int32, (T, T), 1),
                         1.0, 0.0).astype(BF)
        dlogf = _tri_dot(triu, df) + carry[...]
        carry[...] = _row_of(dlogf, lax.broadcasted_iota(jnp.int32, (T, LANES), 0), 0)
        dfz = jnp.where(lane < H, dlogf * _sigmoid(-(fz_ref[...] + b_ref[...])), 0.0)
        db_ref[...] += jnp.sum(dfz, axis=0, keepdims=True)
        dp_ref[...] = jnp.zeros_like(dp_ref)
        dp_ref[:, 0:LANES] = dfz.astype(BF)

    return pl.pallas_call(
        body, name=name, grid=(nb,),
        in_specs=[pl.BlockSpec((T, LANES), lambda i: (nb - 1 - i, 0)),
                  pl.BlockSpec((T, LANES), lambda i: (nb - 1 - i, 4 * D // LANES)),
                  pl.BlockSpec((1, LANES), lambda i: (0, 0)),
                  pl.BlockSpec(memory_space=pl.ANY)],
        out_specs=[pl.BlockSpec((None, T, D), lambda i: (4, nb - 1 - i, 0)), pl.BlockSpec((1, LANES), lambda i: (0, 0))],
        out_shape=[jax.ShapeDtypeStruct(dproj.shape, BF), jax.ShapeDtypeStruct((1, LANES), F32)],
        scratch_shapes=[pltpu.VMEM((1, LANES), F32)],
        input_output_aliases={3: 0},
        compiler_params=_cp("arbitrary"),
    )(colsum, proj, bf_pad, dproj)


def _mod_fwd(c16, w, b, *, name):
    L, D, N = w.shape
    tn = _pick(N, 512)

    def body(c_ref, w_ref, b_ref, o_ref):
        cv = c_ref[...]
        ca = (cv * _sigmoid(cv)).astype(BF)
        o_ref[...] = _dot(ca, w_ref[...].astype(BF)) + b_ref[...]

    return pl.pallas_call(
        body, name=name, grid=(L, N // tn),
        in_specs=[pl.BlockSpec((16, D), lambda l, j: (0, 0)), pl.BlockSpec((None, D, tn), lambda l, j: (l, 0, j)),
                  pl.BlockSpec((None, 1, tn), lambda l, j: (l, 0, j))],
        out_specs=pl.BlockSpec((None, 16, tn), lambda l, j: (l, 0, j)),
        out_shape=jax.ShapeDtypeStruct((L, 16, N), F32),
        compiler_params=_cp("parallel", "arbitrary"),
    )(c16, w, b)


def _mod_bwd(c16, dmod, *, name):
    L, _, N = dmod.shape
    D = c16.shape[1]
    tn = _pick(N, 512)

    def body(c_ref, d_ref, o_ref):
        cv = c_ref[...]
        ca = (cv * _sigmoid(cv)).astype(BF)
        o_ref[...] = _dg(ca, d_ref[...].astype(BF), TN)

    return pl.pallas_call(
        body, name=name, grid=(L, N // tn),
        in_specs=[pl.BlockSpec((16, D), lambda l, j: (0, 0)), pl.BlockSpec((None, 16, tn), lambda l, j: (l, 0, j))],
        out_specs=pl.BlockSpec((None, D, tn), lambda l, j: (l, 0, j)),
        out_shape=jax.ShapeDtypeStruct((L, D, N), F32),
        compiler_params=_cp("parallel", "arbitrary"),
    )(c16, dmod)


def _adamw_math(w, g, m, v):
    m = ADAM_B1 * m + (1.0 - ADAM_B1) * g
    v = ADAM_B2 * v + (1.0 - ADAM_B2) * (g * g)
    m_hat = m / (1.0 - ADAM_B1 ** ADAM_STEP)
    v_hat = v / (1.0 - ADAM_B2 ** ADAM_STEP)
    return -ADAM_LR * (m_hat / (jnp.sqrt(v_hat) + ADAM_EPS) + ADAM_WD * w), m, v


def _adamw(w, g, m, v, *, g_at=None, name):
    R, C = w.shape
    row0 = 0 if g_at is None else g_at[1]
    tr = min(math.gcd(row0, 256) if row0 else 256, -(-R // 8) * 8)
    g0 = row0 // tr
    if g_at is None:
        g_spec = pl.BlockSpec((tr, C), lambda i: (i, 0))
    else:
        g_spec = pl.BlockSpec((None, tr, C), lambda i: (g_at[0], g0 + i, 0))

    def body(w_ref, g_ref, m_ref, v_ref, d_ref, mo_ref, vo_ref):
        d, mn, vn = _adamw_math(w_ref[...], g_ref[...], m_ref[...], v_ref[...])
        d_ref[...] = d
        mo_ref[...] = mn
        vo_ref[...] = vn

    blk = pl.BlockSpec((tr, C), lambda i: (i, 0))
    return pl.pallas_call(
        body, name=name, grid=(pl.cdiv(R, tr),),
        in_specs=[blk, g_spec, blk, blk],
        out_specs=[blk, blk, blk],
        out_shape=[jax.ShapeDtypeStruct((R, C), F32)] * 3,
        compiler_params=_cp("parallel"),
    )(w, g, m, v)


def _sum_parts(parts, *, name):
    P, R, C = parts.shape

    def body(p_ref, o_ref):
        acc = p_ref[0]
        for p in range(1, P):
            acc = acc + p_ref[p]
        o_ref[...] = acc

    return pl.pallas_call(
        body, name=name, grid=(1,),
        in_specs=[pl.BlockSpec((P, R, C), lambda i: (0, 0, 0))],
        out_specs=pl.BlockSpec((R, C), lambda i: (0, 0)),
        out_shape=jax.ShapeDtypeStruct((R, C), F32),
        compiler_params=_cp("arbitrary"),
    )(parts)


def _add_halves(g4, recv, c_idx, *, name):
    _, _, Rh, C = g4.shape
    tr = min(256, Rh)

    def body(c_ref, a_ref, b_ref, o_ref):
        o_ref[...] = (a_ref[...] + b_ref[...].astype(F32)).astype(BF)

    return pl.pallas_call(
        body, name=name,
        grid_spec=pltpu.PrefetchScalarGridSpec(
            num_scalar_prefetch=1, grid=(4, pl.cdiv(Rh, tr)),
            in_specs=[pl.BlockSpec((None, None, tr, C), lambda j, r, c: (j, c[0], r, 0)),
                      pl.BlockSpec((None, tr, C), lambda j, r, c: (j, r, 0))],
            out_specs=pl.BlockSpec((None, tr, C), lambda j, r, c: (j, r, 0))),
        out_shape=jax.ShapeDtypeStruct((4, Rh, C), BF),
        compiler_params=_cp("parallel", "arbitrary"),
    )(c_idx, g4, recv)


def _add_four(g4, from_sibling, from_chips, pos, *, name):
    _, _, Rh, C = g4.shape
    tr = min(256, Rh)

    def body(p_ref, a_ref, s_ref, b_ref, o_ref):
        own = a_ref[...] + s_ref[...].astype(F32)
        o_ref[...] = ((own + b_ref[0].astype(F32)) + b_ref[1].astype(F32)) + b_ref[2].astype(F32)

    return pl.pallas_call(
        body, name=name,
        grid_spec=pltpu.PrefetchScalarGridSpec(
            num_scalar_prefetch=1, grid=(pl.cdiv(Rh, tr),),
            in_specs=[pl.BlockSpec((None, None, tr, C), lambda r, p: (p[0], p[1], r, 0)),
                      pl.BlockSpec((None, tr, C), lambda r, p: (p[0], r, 0)),
                      pl.BlockSpec((3, tr, C), lambda r, p: (0, r, 0))],
            out_specs=pl.BlockSpec((None, tr, C), lambda r, p: (p[1], r, 0))),
        out_shape=jax.ShapeDtypeStruct((2, Rh, C), F32),
        compiler_params=_cp("arbitrary"),
    )(pos, g4, from_sibling, from_chips)


HBM = pl.BlockSpec(memory_space=pltpu.HBM)


def _mesh_pos():
    return lax.axis_index("x"), lax.axis_index("y"), lax.axis_index("c")


def _other_chips(x, y):
    return [(1 - x, y), (x, 1 - y), (1 - x, 1 - y)]


def _allgather_small(xs, *, name):
    m_per, n = xs.shape

    def body(x_ref, out_ref, send_sems, recv_sems, local_sem):
        x, y, c = _mesh_pos()
        me, sibling = (x, y, c), (x, y, 1 - c)
        chips = _other_chips(x, y)

        def rows(px, py, pc):
            return out_ref.at[pl.ds((4 * px + 2 * py + pc) * m_per, m_per), :]

        def copy(k, block, to, src=None):
            return pltpu.make_async_remote_copy(
                src_ref=rows(*block) if src is None else src, dst_ref=rows(*block),
                send_sem=send_sems.at[k], recv_sem=recv_sems.at[k], device_id=to, device_id_type=MESH)

        mine = pltpu.make_async_copy(x_ref, rows(*me), local_sem)
        mine.start()
        first = [copy(0, me, sibling, src=x_ref)]
        first += [copy(1 + j, me, (*chip, c), src=x_ref) for j, chip in enumerate(chips)]
        for cp in first:
            cp.start()
        passed = [copy(4 + j, (*chip, c), sibling) for j, chip in enumerate(chips)]
        for j, chip in enumerate(chips):
            copy(1 + j, (*chip, c), me).wait_recv()
            passed[j].start()
        copy(0, sibling, me).wait_recv()
        for j, chip in enumerate(chips):
            copy(4 + j, (*chip, 1 - c), me).wait_recv()
        for cp in first + passed:
            cp.wait_send()
        mine.wait()

    return pl.pallas_call(
        body, name=name,
        out_shape=jax.ShapeDtypeStruct((N_DEV * m_per, n), xs.dtype),
        in_specs=[pl.BlockSpec(memory_space=pltpu.VMEM)],
        out_specs=pl.BlockSpec(memory_space=pltpu.VMEM),
        scratch_shapes=[pltpu.SemaphoreType.DMA((7,)), pltpu.SemaphoreType.DMA((7,)), pltpu.SemaphoreType.DMA],
    )(xs)


def _chip_slab_copies(s_ref, out_ref, send_sems, recv_sems):
    R = s_ref.shape[0]
    Rh = R // 2
    x, y, c = _mesh_pos()
    me, sibling = (x, y, c), (x, y, 1 - c)
    chips = _other_chips(x, y)

    def half(px, py, pc):
        return out_ref.at[2 * px + py, pl.ds(pc * Rh, Rh), :]

    def copy(k, block, to, src=None):
        return pltpu.make_async_remote_copy(
            src_ref=half(*block) if src is None else src, dst_ref=half(*block),
            send_sem=send_sems.at[k], recv_sem=recv_sems.at[k], device_id=to, device_id_type=MESH)

    first = [copy(j, me, (*chip, c), src=s_ref.at[pl.ds(c * Rh, Rh), :]) for j, chip in enumerate(chips)]
    passed = [copy(3 + j, (*chip, c), sibling) for j, chip in enumerate(chips)]
    landed = [copy(j, (*chip, c), me) for j, chip in enumerate(chips)]
    from_sibling = [copy(3 + j, (*chip, 1 - c), me) for j, chip in enumerate(chips)]
    return first, passed, landed, from_sibling


def _allgather_chip_slabs(slab, *, name):
    R, C = slab.shape

    def body(s_ref, out_ref, send_sems, recv_sems):
        first, passed, landed, from_sibling = _chip_slab_copies(s_ref, out_ref, send_sems, recv_sems)
        for cp in first:
            cp.start()
        for arrived, onward in zip(landed, passed):
            arrived.wait_recv()
            onward.start()
        for cp in from_sibling:
            cp.wait_recv()
        for cp in first + passed:
            cp.wait_send()

    return pl.pallas_call(
        body, name=name,
        out_shape=jax.ShapeDtypeStruct((N_CHIPS, R, C), slab.dtype),
        in_specs=[HBM], out_specs=HBM,
        scratch_shapes=[pltpu.SemaphoreType.DMA((6,)), pltpu.SemaphoreType.DMA((6,))],
    )(slab)


def _swap_halves(mine, *, name):
    def body(g_ref, out_ref, send_sems, recv_sems):
        x, y, c = _mesh_pos()
        copies = [pltpu.make_async_remote_copy(
            src_ref=g_ref.at[j], dst_ref=out_ref.at[j], send_sem=send_sems.at[j], recv_sem=recv_sems.at[j],
            device_id=(x, y, 1 - c), device_id_type=MESH) for j in range(N_CHIPS)]
        for cp in copies:
            cp.start()
        for cp in copies:
            cp.wait()

    return pl.pallas_call(
        body, name=name,
        out_shape=jax.ShapeDtypeStruct(mine.shape, mine.dtype),
        in_specs=[HBM], out_specs=HBM,
        scratch_shapes=[pltpu.SemaphoreType.DMA((N_CHIPS,)), pltpu.SemaphoreType.DMA((N_CHIPS,))],
    )(mine)


def _scatter_copies(p_ref, out_ref, send_sems, recv_sems):
    x, y, c = _mesh_pos()
    return [pltpu.make_async_remote_copy(
        src_ref=p_ref.at[2 * px + py], dst_ref=out_ref.at[j], send_sem=send_sems.at[j], recv_sem=recv_sems.at[j],
        device_id=(px, py, c), device_id_type=MESH) for j, (px, py) in enumerate(_other_chips(x, y))]


def _scatter_partials(part, *, name):
    _, Rh, C = part.shape

    def body(p_ref, out_ref, send_sems, recv_sems):
        copies = _scatter_copies(p_ref, out_ref, send_sems, recv_sems)
        for cp in copies:
            cp.start()
        for cp in copies:
            cp.wait()

    return pl.pallas_call(
        body, name=name,
        out_shape=jax.ShapeDtypeStruct((3, Rh, C), part.dtype),
        in_specs=[HBM], out_specs=HBM,
        scratch_shapes=[pltpu.SemaphoreType.DMA((3,)), pltpu.SemaphoreType.DMA((3,))],
    )(part)


def _join_halves(buf, *, name):
    def body(b_ref, out_ref, send_sem, recv_sem):
        x, y, c = _mesh_pos()
        cp = pltpu.make_async_remote_copy(
            src_ref=b_ref.at[c], dst_ref=out_ref.at[c], send_sem=send_sem, recv_sem=recv_sem,
            device_id=(x, y, 1 - c), device_id_type=MESH)
        cp.start()
        cp.wait()

    return pl.pallas_call(
        body, name=name,
        out_shape=jax.ShapeDtypeStruct(buf.shape, buf.dtype),
        in_specs=[HBM], out_specs=HBM, input_output_aliases={0: 0},
        scratch_shapes=[pltpu.SemaphoreType.DMA, pltpu.SemaphoreType.DMA],
    )(buf)


def _pad_rows(a, mult):
    pad = (-a.shape[0]) % mult
    return a if pad == 0 else jnp.pad(a, ((0, pad),) + ((0, 0),) * (a.ndim - 1))


def _local_step(x, target, mod, wts, small, slab_rest=None, unpack_rest=None, reduce_early=None):
    S, D = x.shape
    HP = D // LANES
    row = lambda v: v.reshape(1, -1)
    msplit = [[row(mod[i, k * D:(k + 1) * D]) for k in range(6)] for i in range(2)]
    gw, gs = {}, {}
    dmod = [[None] * 6 for _ in range(2)]

    sh1, sc1, g1, sh2, sc2, g2 = msplit[0]
    n1w0, n2w0 = row(small["norm1_w"][0]), row(small["norm2_w"][0])
    proj0, h1_0 = _ln_matmul(x, n1w0, sc1, sh1, wts["hg_w_in"], relu2=False, name="hg_in_proj")
    gn = small["hg_gn_w"].reshape(1, LANES)
    ypre0, o0, states, *gathered = _hg_fwd(proj0, small["hg_lb"], gn, slab_rest, name="hg_fwd")
    if slab_rest is not None:
        wts = {**wts, **unpack_rest(gathered[0])}
    x1, ymix0 = _matmul_resid(ypre0, wts["hg_w_out"], x, g1, name="hg_out_proj")
    a0, u0, h2_0 = _ln_matmul(x1, n2w0, sc2, sh2, wts["mlp_w1_0"], relu2=True, name="mlp0_up")
    x2, ymlp0 = _matmul_resid(u0, wts["mlp_w2_0"], x1, g2, name="mlp0_down")

    sh1b, sc1b, g1b, sh2b, sc2b, g2b = msplit[1]
    n1w1, n2w1 = row(small["norm1_w"][1]), row(small["norm2_w"][1])
    proj1, h1_1 = _ln_matmul(x2, n1w1, sc1b, sh1b, wts["fox_w_in"], relu2=False, name="fox_in_proj")
    nheads = 2 * HP
    bf_pad = jnp.pad(small["fox_b_f"].reshape(1, nheads), ((0, 0), (0, LANES - nheads)))
    qw2 = jnp.tile(small["fox_qn_w"].reshape(1, FOX_DH), (1, 2))
    kw2 = jnp.tile(small["fox_kn_w"].reshape(1, FOX_DH), (1, 2))
    fcum = _fox_cumsum(proj1, bf_pad, name="fox_cumsum")
    qa, ka, va, vat = _fox_prep(proj1, fcum, qw2, kw2, name="fox_prep")
    jmin, imax = _fox_skip_bounds(fcum, small["fox_qn_w"], small["fox_kn_w"], nheads)
    ypre1, o1, q2 = _fox_fwd(jmin, qa, ka, vat, proj1, name="fox_fwd")
    x3, ymix1 = _matmul_resid(ypre1, wts["fox_w_out"], x2, g1b, name="fox_out_proj")
    a1, u1, h2_1 = _ln_matmul(x3, n2w1, sc2b, sh2b, wts["mlp_w1_1"], relu2=True, name="mlp1_up")
    x4, ymlp1 = _matmul_resid(u1, wts["mlp_w2_1"], x3, g2b, name="mlp1_down")

    loss, dx4, dfw = _loss_kernel(x4, row(small["final_w"]), target, name="loss")
    gs["final_w"] = dfw.reshape(-1)

    def mlp_bwd(i, dx_out, x_in, h2, a, u, ymlp, n2w, sc2_, g2_):
        dz, dm, dg2 = _gate_matmul_nt(dx_out, g2_, ymlp, wts[f"mlp_w2_{i}"], a, name=f"mlp{i}_down_bwd")
        gw[f"mlp_w2_{i}"] = _matmul_tn(u, dm[None], name=f"mlp{i}_dw2")
        gw[f"mlp_w1_{i}"] = _matmul_tn(h2, dz[None], name=f"mlp{i}_dw1")
        dx_in, dsc, dsh, dnw = _matmul_nt_lnbwd(dz[None], wts[f"mlp_w1_{i}"], x_in, n2w, sc2_, dx_out,
                                                name=f"mlp{i}_up_bwd")
        dmod[i][3], dmod[i][4], dmod[i][5] = dsh, dsc, dg2
        return dx_in, dnw

    dx3, dn2w1 = mlp_bwd(1, dx4, x3, h2_1, a1, u1, ymlp1, n2w1, sc2b, g2b)
    dyp1, dm1, dg1b = _gate_matmul_nt(dx3, g1b, ymix1, wts["fox_w_out"], None, name="fox_out_bwd")
    gw["fox_w_out"] = _matmul_tn(ypre1, dm1[None], name="fox_dw_out")
    doa = _fox_bwd_prep(dyp1, o1, proj1, name="fox_bwd_prep")
    dqa, dka, dva, colsum = _fox_bwd(imax, q2, ka, va, doa, name="fox_bwd")
    colsum = jnp.pad(colsum[:, 0, :].T, ((0, 0), (0, LANES - nheads)))
    dproj1, dqw, dkw = _fox_bwd_post(dqa, dka, dva, proj1, dyp1, o1, qw2, kw2, name="fox_bwd_post")
    dproj1, dbf = _fox_dfz(colsum, nheads, proj1, bf_pad, dproj1, name="fox_dfz")
    gw["fox_w_in"] = _matmul_tn(h1_1, dproj1, name="fox_dw_in")
    dx2, dsc, dsh, dn1w1 = _matmul_nt_lnbwd(dproj1, wts["fox_w_in"], x2, n1w1, sc1b, dx3, name="fox_in_bwd")
    dmod[1][0], dmod[1][1], dmod[1][2] = dsh, dsc, dg1b
    gs["fox_qn_w"] = dqw[0, :FOX_DH] + dqw[0, FOX_DH:]
    gs["fox_kn_w"] = dkw[0, :FOX_DH] + dkw[0, FOX_DH:]
    gs["fox_b_f"] = dbf[0, :nheads]

    dx1, dn2w0 = mlp_bwd(0, dx2, x1, h2_0, a0, u0, ymlp0, n2w0, sc2, g2)
    dyp0, dm0, dg1 = _gate_matmul_nt(dx1, g1, ymix0, wts["hg_w_out"], None, name="hg_out_bwd")
    gw["hg_w_out"] = _matmul_tn(ypre0, dm0[None], name="hg_dw_out")
    part, ctx = reduce_early(gw) if reduce_early is not None else (None, None)
    dproj0, dlb, dgn, *from_chips = _hg_bwd(proj0, small["hg_lb"], gn, o0, states, dyp0, part, name="hg_bwd")
    early = (ctx, from_chips[0]) if reduce_early is not None else None
    gw["hg_w_in"] = _matmul_tn(h1_0, dproj0, name="hg_dw_in")
    dx0, dsc, dsh, dn1w0 = _matmul_nt_lnbwd(dproj0, wts["hg_w_in"], x, n1w0, sc1, dx1, name="hg_in_bwd")
    dmod[0][0], dmod[0][1], dmod[0][2] = dsh, dsc, dg1
    gs["hg_lb"] = dlb
    gs["hg_gn_w"] = jnp.sum(dgn, axis=0)

    gs["norm1_w"] = jnp.concatenate([dn1w0, dn1w1], axis=0)
    gs["norm2_w"] = jnp.concatenate([dn2w0, dn2w1], axis=0)
    gs["dmod"] = jnp.stack([jnp.concatenate(dmod[i], axis=1)[0] for i in range(2)])
    return loss, dx0, gw, gs, early


def _pack_halves(layout):
    rh = -(-max(sum(a.shape[0] for _, a in half) for half in layout) // 16) * 16
    place, parts = {}, []
    for h, half in enumerate(layout):
        off = 0
        for n, a in half:
            place[n] = (h, off, a.shape[0])
            off += a.shape[0]
        parts.append(jnp.pad(jnp.concatenate([a.astype(BF) for _, a in half], axis=0), ((0, rh - off), (0, 0))))
    return jnp.concatenate(parts, axis=0), place, rh


SMALL_NAMES = ["norm1_w", "norm2_w", "hg_lb", "hg_gn_w", "fox_b_f", "fox_qn_w", "fox_kn_w", "final_w"]


def _pack_small(d, names):
    rows, offs, r0 = [], {}, 0
    for n in names:
        flat = d[n].reshape(-1)
        nr = -(-flat.shape[0] // LANES)
        rows.append(jnp.pad(flat, (0, nr * LANES - flat.shape[0])).reshape(nr, LANES))
        offs[n] = (r0, nr)
        r0 += nr
    return jnp.concatenate(rows, axis=0), offs


def _unpack_small(packed, offs, name, like):
    r0, nr = offs[name]
    return packed[r0:r0 + nr].reshape(-1)[:like.size].reshape(like.shape)


def kernel(x, c, w_mod, b_mod, norm1_w, norm2_w, hg_w_in, hg_w_out, hg_lb, hg_gn_w, fox_w_in, fox_b_f, fox_qn_w, fox_kn_w, fox_w_out, mlp_w1, mlp_w2, final_w, loss_target, m_w_mod, m_b_mod, m_norm1_w, m_norm2_w, m_hg_w_in, m_hg_w_out, m_hg_lb, m_hg_gn_w, m_fox_w_in, m_fox_b_f, m_fox_qn_w, m_fox_kn_w, m_fox_w_out, m_mlp_w1, m_mlp_w2, m_final_w, v_w_mod, v_b_mod, v_norm1_w, v_norm2_w, v_hg_w_in, v_hg_w_out, v_hg_lb, v_hg_gn_w, v_fox_w_in, v_fox_b_f, v_fox_qn_w, v_fox_kn_w, v_fox_w_out, v_mlp_w1, v_mlp_w2, v_final_w):
    S, D = x.shape[1], x.shape[2]
    nheads = D // FOX_DH
    ax, ay, ac = _mesh_pos()
    chip = 2 * ax + ay
    dev = 2 * chip + ac
    xs, tgt = x.reshape(S, D), loss_target.reshape(S, D)

    c_all = _allgather_small(_pad_rows(c.reshape(-1, LANES), 8), name="gather_c")
    c_all = c_all.reshape(N_DEV, -1)[:, :D]
    c16 = _pad_rows(c_all, 16)
    nmod = w_mod.shape[2]
    b_shard = lax.dynamic_slice_in_dim(b_mod, chip * nmod, nmod, axis=1)
    mod_shard = _mod_fwd(c16, w_mod, b_shard[:, None, :], name="mod_fwd")[:, :N_DEV]
    mod_all = _allgather_small(mod_shard.reshape(-1, LANES), name="gather_mod")
    mod_all = mod_all.reshape(N_CHIPS, 2, 2, N_DEV, nmod)[:, 0]
    mod = lax.dynamic_index_in_dim(mod_all, dev, axis=2, keepdims=False)
    mod = mod.transpose(1, 0, 2).reshape(2, N_CHIPS * nmod)

    fox_rows = fox_w_in.shape[2]
    col = lambda g: g.transpose(1, 0, 2).reshape(g.shape[1], -1)
    rowsh = lambda g: g.reshape(-1, g.shape[2])
    own = lambda g, s: lax.dynamic_update_index_in_dim(g, s, chip, 0)

    slab_in = hg_w_in[0].astype(BF)
    wts = {"hg_w_in": col(own(_allgather_chip_slabs(slab_in, name="gather_hg_w_in"), slab_in))}
    slab_rest, place_rest, rh_rest = _pack_halves(
        [[("mlp_w1", mlp_w1.reshape(2 * D, D)), ("hg_w_out", hg_w_out[0]), ("fox_w_out", fox_w_out[0])],
         [("mlp_w2", mlp_w2.reshape(2 * D, D)), ("fox_w_in", fox_w_in[0].reshape(fox_rows, D))]])

    def unpack_rest(gathered):
        gathered = own(gathered, slab_rest)

        def seg(n):
            h, off, rows = place_rest[n]
            return gathered[:, h * rh_rest + off:h * rh_rest + off + rows, :]

        w1 = seg("mlp_w1").reshape(N_CHIPS, 2, D, D)
        w2 = seg("mlp_w2").reshape(N_CHIPS, 2, D, D)
        fox_in = col(seg("fox_w_in").reshape(N_CHIPS, D, fox_rows))
        return {
            "hg_w_out": rowsh(seg("hg_w_out")), "fox_w_out": rowsh(seg("fox_w_out")),
            "mlp_w1_0": col(w1[:, 0]), "mlp_w1_1": col(w1[:, 1]), "mlp_w2_0": rowsh(w2[:, 0]), "mlp_w2_1": rowsh(w2[:, 1]),
            "fox_w_in": jnp.pad(fox_in, ((0, 0), (0, 5 * D - fox_in.shape[1]))),
        }

    small = {"norm1_w": norm1_w, "norm2_w": norm2_w, "hg_lb": hg_lb, "hg_gn_w": hg_gn_w, "fox_b_f": fox_b_f,
             "fox_qn_w": fox_qn_w, "fox_kn_w": fox_kn_w, "final_w": final_w}

    def uncol(g, n):
        return g.reshape(g.shape[0], N_CHIPS, n).transpose(1, 0, 2)

    pos = jnp.stack([chip, ac])

    def swap_and_add(halves, tag):
        rh = -(-max(h.shape[1] for h in halves) // 16) * 16
        g4 = jnp.stack([jnp.pad(h, ((0, 0), (0, rh - h.shape[1]), (0, 0))) for h in halves], axis=1)
        to_sibling = lax.dynamic_index_in_dim(g4, 1 - ac, axis=1, keepdims=False).astype(BF)
        from_sibling = _swap_halves(to_sibling, name=f"rs_swap_{tag}")
        return g4, from_sibling, _add_halves(g4, from_sibling, ac.reshape(1), name=f"rs_add_halves_{tag}")

    def finish(g4, from_sibling, from_chips, tag):
        my_half = _add_four(g4, from_sibling, from_chips, pos, name=f"rs_add_chips_{tag}")
        return _join_halves(my_half, name=f"rs_join_{tag}")

    layout = [[("mlp_w1", 2 * D), ("hg_w_out", D // 4), ("fox_w_out", D // 4)], [("mlp_w2", 2 * D), ("fox_w_in", fox_rows)]]
    place = {}
    for h, half in enumerate(layout):
        off = 0
        for n, rows in half:
            place[n] = (h, off, rows)
            off += rows

    def reduce_early(gw):
        gseg = {
            "hg_w_out": gw["hg_w_out"].reshape(N_CHIPS, D // 4, D), "fox_w_out": gw["fox_w_out"].reshape(N_CHIPS, D // 4, D),
            "mlp_w1": jnp.concatenate([uncol(gw["mlp_w1_0"], D), uncol(gw["mlp_w1_1"], D)], axis=1),
            "mlp_w2": jnp.concatenate([gw["mlp_w2_0"].reshape(N_CHIPS, D, D), gw["mlp_w2_1"].reshape(N_CHIPS, D, D)], axis=1),
            "fox_w_in": uncol(gw["fox_w_in"][:, :4 * fox_rows], fox_rows).reshape(N_CHIPS, fox_rows, D),
        }
        g4, from_sibling, part = swap_and_add([jnp.concatenate([gseg[n] for n, _ in half], axis=1) for half in layout], "early")
        return part, (g4, from_sibling)

    loss_part, grad_x, gw, gs, ((g4, from_sibling), from_chips) = _local_step(
        xs, tgt, mod, wts, small, slab_rest, unpack_rest, reduce_early)
    loss = lax.psum(loss_part[0, 0], ("x", "y", "c"))
    gshard = finish(g4, from_sibling, from_chips, "early")

    g_in = uncol(gw["hg_w_in"], D)
    g4, from_sibling, part = swap_and_add([g_in[:, :D // 2], g_in[:, D // 2:]], "late")
    g_hg_w_in = finish(g4, from_sibling, _scatter_partials(part, name="rs_scatter_late"), "late").reshape(D, D)

    names = ["dmod"] + SMALL_NAMES
    packed, offs = _pack_small(gs, names)
    packed = _pad_rows(packed, 8)
    rp = packed.shape[0]
    parts = _allgather_small(packed, name="gather_small").reshape(N_DEV, rp, LANES)
    total = _sum_parts(parts, name="sum_small")
    r0, nr = offs["dmod"]
    dmod_all = parts[:, r0:r0 + nr].reshape(N_DEV, 2, N_CHIPS * nmod)
    dmod_shard = lax.dynamic_slice_in_dim(dmod_all, chip * nmod, nmod, axis=2).transpose(1, 0, 2)
    g_w_mod = _mod_bwd(c16, jnp.pad(dmod_shard, ((0, 0), (0, 16 - N_DEV), (0, 0))), name="mod_bwd")

    grads = {"w_mod": g_w_mod, "b_mod": _unpack_small(total, offs, "dmod", b_mod)}
    for n in SMALL_NAMES:
        grads[n] = _unpack_small(total, offs, n, small[n])

    given = dict(w_mod=(w_mod, m_w_mod, v_w_mod), b_mod=(b_mod, m_b_mod, v_b_mod), norm1_w=(norm1_w, m_norm1_w, v_norm1_w),
                 norm2_w=(norm2_w, m_norm2_w, v_norm2_w), hg_w_in=(hg_w_in, m_hg_w_in, v_hg_w_in),
                 hg_w_out=(hg_w_out, m_hg_w_out, v_hg_w_out), hg_lb=(hg_lb, m_hg_lb, v_hg_lb),
                 hg_gn_w=(hg_gn_w, m_hg_gn_w, v_hg_gn_w), fox_w_in=(fox_w_in, m_fox_w_in, v_fox_w_in),
                 fox_b_f=(fox_b_f, m_fox_b_f, v_fox_b_f), fox_qn_w=(fox_qn_w, m_fox_qn_w, v_fox_qn_w),
                 fox_kn_w=(fox_kn_w, m_fox_kn_w, v_fox_kn_w), fox_w_out=(fox_w_out, m_fox_w_out, v_fox_w_out),
                 mlp_w1=(mlp_w1, m_mlp_w1, v_mlp_w1), mlp_w2=(mlp_w2, m_mlp_w2, v_mlp_w2), final_w=(final_w, m_final_w, v_final_w))
    upd = {}

    for n, (h, off, rows) in place.items():
        w, m, v = given[n]
        flat = lambda a: a.reshape(rows, D)
        d, mn, vn = _adamw(flat(w), gshard, flat(m), flat(v), g_at=(h, off), name=f"adamw_{n}")
        grads[n] = gshard[h, off:off + rows].reshape(w.shape)
        upd[n] = tuple(a.reshape(w.shape) for a in (d, mn, vn))

    w, m, v = given["hg_w_in"]
    grads["hg_w_in"] = g_hg_w_in.reshape(w.shape)
    upd["hg_w_in"] = tuple(a.reshape(w.shape) for a in _adamw(w[0], g_hg_w_in, m[0], v[0], name="adamw_hg_w_in"))

    w, m, v = given["w_mod"]
    flat = lambda a: a.reshape(-1, nmod)
    upd["w_mod"] = tuple(a.reshape(w.shape) for a in _adamw(flat(w), flat(g_w_mod), flat(m), flat(v), name="adamw_w_mod"))

    snames = ["b_mod"] + SMALL_NAMES
    pw, soffs = _pack_small({n: given[n][0] for n in snames}, snames)
    pm, _ = _pack_small({n: given[n][1] for n in snames}, snames)
    pv, _ = _pack_small({n: given[n][2] for n in snames}, snames)
    pg, _ = _pack_small({n: grads[n] for n in snames}, snames)
    pw, pm, pv, pg = (_pad_rows(a, 8) for a in (pw, pm, pv, pg))
    sd, smn, svn = _adamw(pw, pg, pm, pv, name="adamw_small")
    for n in snames:
        like = given[n][0]
        upd[n] = tuple(_unpack_small(a, soffs, n, like) for a in (sd, smn, svn))

    order = ["w_mod", "b_mod", "norm1_w", "norm2_w", "hg_w_in", "hg_w_out", "hg_lb", "hg_gn_w", "fox_w_in", "fox_b_f",
             "fox_qn_w", "fox_kn_w", "fox_w_out", "mlp_w1", "mlp_w2", "final_w"]
    return (loss, grad_x.reshape(x.shape), *[grads[n] for n in order], *[upd[n][0] for n in order],
            *[upd[n][1] for n in order], *[upd[n][2] for n in order])
```

```python
import math

import jax
import jax.numpy as jnp
from jax import lax
from jax.experimental import pallas as pl
from jax.experimental.pallas import tpu as pltpu

EPS = 1e-6
ADAM_LR, ADAM_B1, ADAM_B2, ADAM_EPS, ADAM_WD, ADAM_STEP = 0.001, 0.9, 0.999, 1e-08, 0.01, 10

F32 = jnp.float32
BF = jnp.bfloat16
LANES = 128
HG_CHUNK = 64
HG_HEADS_PER_STEP = 8
HG_TOKENS_PER_STEP = 256
FOX_BWD_TILES = (8, 4, 2, 1)
LOG2E = 1.4426950408889634
FOX_DH = 64
N_CHIPS = 4
N_DEV = 8
VMEM_LIMIT = 56 * 1024 * 1024
MESH = pl.DeviceIdType.MESH

NT = (((1,), (1,)), ((), ()))
TN = (((0,), (0,)), ((), ()))


def _pick(n, pref, mult=LANES):
    if n <= pref:
        return n
    t = (pref // mult) * mult
    while t >= mult:
        if n % t == 0:
            return t
        t -= mult
    raise ValueError((n, pref, mult))


def _cp(*sem):
    return pltpu.CompilerParams(dimension_semantics=sem, vmem_limit_bytes=VMEM_LIMIT)


def _dot(a, b):
    return jnp.dot(a, b, preferred_element_type=F32)


def _dg(a, b, dims):
    return lax.dot_general(a, b, dims, preferred_element_type=F32)


def _split3(x):
    hi = x.astype(BF)
    r1 = x - hi.astype(F32)
    mid = r1.astype(BF)
    lo = (r1 - mid.astype(F32)).astype(BF)
    return hi, mid, lo


def _tri_dot(tri, x):
    hi, mid, lo = _split3(x)
    return _dot(tri, hi) + _dot(tri, mid) + _dot(tri, lo)


def _dg3(a, b, dims):
    ah, bh = a.astype(BF), b.astype(BF)
    al, bl = (a - ah.astype(F32)).astype(BF), (b - bh.astype(F32)).astype(BF)
    return _dg(ah, bh, dims) + _dg(ah, bl, dims) + _dg(al, bh, dims)


def _dg1(a, b, dims):
    return _dg(a.astype(BF), b.astype(BF), dims)


NN = (((1,), (0,)), ((), ()))


def _sigmoid(x):
    return jax.nn.sigmoid(x)


def _ln_matmul(x, nw, sc, sh, w, *, relu2, name):
    S, D = x.shape
    N = w.shape[1]
    tm, tn = _pick(S, 512, 16), N

    def body(x_ref, nw_ref, sc_ref, sh_ref, w_ref, *rest):
        outs, hs = rest[:-1], rest[-1]
        h_ref = outs[-1]

        @pl.when(pl.program_id(1) == 0)
        def _():
            xv = x_ref[...]
            r = lax.rsqrt(jnp.mean(xv * xv, axis=-1, keepdims=True) + EPS)
            hb = ((xv * r * nw_ref[...]) * (1.0 + sc_ref[...]) + sh_ref[...]).astype(BF)
            hs[...] = hb
            h_ref[...] = hb

        z = _dot(hs[...], w_ref[...])
        if relu2:
            a = jnp.maximum(z, 0.0)
            outs[0][...] = a.astype(BF)
            outs[1][...] = (a * a).astype(BF)
        else:
            outs[0][...] = z

    vec = pl.BlockSpec((1, D), lambda i, j: (0, 0))
    tile = pl.BlockSpec((tm, tn), lambda i, j: (i, j))
    if relu2:
        out_shape = [jax.ShapeDtypeStruct((S, N), BF), jax.ShapeDtypeStruct((S, N), BF)]
        out_specs = [tile, tile]
    else:
        out_shape = [jax.ShapeDtypeStruct((S, N), F32)]
        out_specs = [tile]
    out_shape.append(jax.ShapeDtypeStruct((S, D), BF))
    out_specs.append(pl.BlockSpec((tm, D), lambda i, j: (i, 0)))
    return pl.pallas_call(
        body, name=name, grid=(S // tm, N // tn),
        in_specs=[pl.BlockSpec((tm, D), lambda i, j: (i, 0)), vec, vec, vec,
                  pl.BlockSpec((D, tn), lambda i, j: (0, j))],
        out_specs=out_specs, out_shape=out_shape,
        scratch_shapes=[pltpu.VMEM((tm, D), BF)],
        compiler_params=_cp("parallel", "arbitrary"),
    )(x, nw, sc, sh, w)


def _matmul_resid(a, w, x, gate, *, name):
    S, K = a.shape
    D = w.shape[1]
    tm, tn = _pick(S, 1024 if K <= 1024 else 512, 16), D

    def body(a_ref, w_ref, x_ref, g_ref, o_ref, y_ref):
        y = _dot(a_ref[...], w_ref[...])
        y_ref[...] = y.astype(BF)
        o_ref[...] = x_ref[...] + g_ref[...] * y

    tile = pl.BlockSpec((tm, tn), lambda i, j: (i, j))
    return pl.pallas_call(
        body, name=name, grid=(S // tm, D // tn),
        in_specs=[pl.BlockSpec((tm, K), lambda i, j: (i, 0)), pl.BlockSpec((K, tn), lambda i, j: (0, j)),
                  tile, pl.BlockSpec((1, tn), lambda i, j: (0, j))],
        out_specs=[tile, tile],
        out_shape=[jax.ShapeDtypeStruct((S, D), F32), jax.ShapeDtypeStruct((S, D), BF)],
        compiler_params=_cp("parallel", "arbitrary"),
    )(a, w, x, gate)


def _gate_matmul_nt(dx, gate, y, w, act, *, name):
    S, D = dx.shape
    K = w.shape[0]
    tm, tn = _pick(S, 1024 if K <= 1024 else 512, 16), K
    fused = act is not None

    def body(dx_ref, g_ref, y_ref, w_ref, *rest):
        if fused:
            act_ref, da_ref, dm_ref, dg_ref, ms = rest
        else:
            da_ref, dm_ref, dg_ref, ms = rest
        i, j = pl.program_id(0), pl.program_id(1)

        @pl.when((i == 0) & (j == 0))
        def _():
            dg_ref[...] = jnp.zeros_like(dg_ref)

        @pl.when(j == 0)
        def _():
            dxv = dx_ref[...]
            dmb = (dxv * g_ref[...]).astype(BF)
            ms[...] = dmb
            dm_ref[...] = dmb
            dg_ref[...] += jnp.sum(dxv * y_ref[...].astype(F32), axis=0, keepdims=True)

        da = _dg(ms[...], w_ref[...], NT)
        if fused:
            da_ref[...] = (da * (2.0 * act_ref[...].astype(F32))).astype(BF)
        else:
            da_ref[...] = da

    row = pl.BlockSpec((tm, D), lambda i, j: (i, 0))
    vec = pl.BlockSpec((1, D), lambda i, j: (0, 0))
    tile = pl.BlockSpec((tm, tn), lambda i, j: (i, j))
    in_specs = [row, vec, row, pl.BlockSpec((tn, D), lambda i, j: (j, 0))]
    args = [dx, gate, y, w]
    if fused:
        in_specs.append(tile)
        args.append(act)
    return pl.pallas_call(
        body, name=name, grid=(S // tm, K // tn),
        in_specs=in_specs, out_specs=[tile, row, vec],
        out_shape=[jax.ShapeDtypeStruct((S, K), BF if fused else F32), jax.ShapeDtypeStruct((S, D), BF),
                   jax.ShapeDtypeStruct((1, D), F32)],
        scratch_shapes=[pltpu.VMEM((tm, D), BF)],
        compiler_params=_cp("arbitrary", "arbitrary"),
    )(*args)


def _matmul_tn(a, b, *, name):
    S, Ka = a.shape
    P, _, Db = b.shape
    tk, tn, ts = _pick(Ka, 1024), _pick(Db, 1024), _pick(S, 1024, 16)
    npb = Db // tn

    def body(a_ref, b_ref, o_ref, acc):
        s = pl.program_id(2)

        @pl.when(s == 0)
        def _():
            acc[...] = jnp.zeros_like(acc)

        acc[...] += _dg(a_ref[...], b_ref[...], TN)

        @pl.when(s == pl.num_programs(2) - 1)
        def _():
            o_ref[...] = acc[...]

    return pl.pallas_call(
        body, name=name, grid=(Ka // tk, P * npb, S // ts),
        in_specs=[pl.BlockSpec((ts, tk), lambda i, j, s: (s, i)),
                  pl.BlockSpec((None, ts, tn), lambda i, j, s: (j // npb, s, j % npb))],
        out_specs=pl.BlockSpec((tk, tn), lambda i, j, s: (i, j)),
        out_shape=jax.ShapeDtypeStruct((Ka, P * Db), F32),
        scratch_shapes=[pltpu.VMEM((tk, tn), F32)],
        compiler_params=_cp("parallel", "parallel", "arbitrary"),
    )(a, b)


def _matmul_nt_lnbwd(g, w, x, nw, sc, dx_out, *, name):
    P, S, Dg = g.shape
    D = x.shape[1]
    tm = _pick(S, 512, 16)

    def body(g_ref, w_ref, x_ref, nw_ref, sc_ref, dxo_ref, dx_ref, dsc_ref, dsh_ref, dnw_ref):
        @pl.when(pl.program_id(0) == 0)
        def _():
            dsc_ref[...] = jnp.zeros_like(dsc_ref)
            dsh_ref[...] = jnp.zeros_like(dsh_ref)
            dnw_ref[...] = jnp.zeros_like(dnw_ref)

        dh = _dg(g_ref[0], w_ref[:, 0:Dg], NT)
        for p in range(1, P):
            dh = dh + _dg(g_ref[p], w_ref[:, p * Dg:(p + 1) * Dg], NT)
        xv = x_ref[...]
        nwv = nw_ref[...]
        r = lax.rsqrt(jnp.mean(xv * xv, axis=-1, keepdims=True) + EPS)
        xr = xv * r
        dn = dh * (1.0 + sc_ref[...])
        dsc_ref[...] += jnp.sum(dh * (xr * nwv), axis=0, keepdims=True)
        dsh_ref[...] += jnp.sum(dh, axis=0, keepdims=True)
        dnw_ref[...] += jnp.sum(dn * xr, axis=0, keepdims=True)
        u = dn * nwv
        dx_ref[...] = dxo_ref[...] + r * (u - xr * jnp.mean(u * xr, axis=-1, keepdims=True))

    row = pl.BlockSpec((tm, D), lambda i: (i, 0))
    vec = pl.BlockSpec((1, D), lambda i: (0, 0))
    return pl.pallas_call(
        body, name=name, grid=(S // tm,),
        in_specs=[pl.BlockSpec((P, tm, Dg), lambda i: (0, i, 0)),
                  pl.BlockSpec((D, P * Dg), lambda i: (0, 0)), row, vec, vec, row],
        out_specs=[row, vec, vec, vec],
        out_shape=[jax.ShapeDtypeStruct((S, D), F32)] + [jax.ShapeDtypeStruct((1, D), F32)] * 3,
        compiler_params=_cp("arbitrary"),
    )(g, w, x, nw, sc, dx_out)


def _loss_kernel(x, fw, tgt, *, name):
    S, D = x.shape
    tm = _pick(S, 512, 8)

    def body(x_ref, fw_ref, t_ref, l_ref, dx_ref, dfw_ref):
        @pl.when(pl.program_id(0) == 0)
        def _():
            l_ref[...] = jnp.zeros_like(l_ref)
            dfw_ref[...] = jnp.zeros_like(dfw_ref)

        xv = x_ref[...]
        fwv = fw_ref[...]
        r = lax.rsqrt(jnp.mean(xv * xv, axis=-1, keepdims=True) + EPS)
        xr = xv * r
        err = xr * fwv - t_ref[...]
        per_tok = jnp.mean(err * err, axis=-1, keepdims=True)
        l_ref[...] += 0.5 * jnp.sum(per_tok, axis=0, keepdims=True)
        dy = err * (1.0 / D)
        dfw_ref[...] += jnp.sum(dy * xr, axis=0, keepdims=True)
        u = dy * fwv
        dx_ref[...] = r * (u - xr * jnp.mean(u * xr, axis=-1, keepdims=True))

    row = pl.BlockSpec((tm, D), lambda i: (i, 0))
    vec = pl.BlockSpec((1, D), lambda i: (0, 0))
    return pl.pallas_call(
        body, name=name, grid=(S // tm,),
        in_specs=[row, vec, row],
        out_specs=[pl.BlockSpec((1, LANES), lambda i: (0, 0)), row, vec],
        out_shape=[jax.ShapeDtypeStruct((1, LANES), F32), jax.ShapeDtypeStruct((S, D), F32),
                   jax.ShapeDtypeStruct((1, D), F32)],
        compiler_params=_cp("arbitrary"),
    )(x, fw, tgt)


def _hg_lower_bound(lb3):
    mx = jnp.max(lb3, axis=0, keepdims=True)
    e = jnp.exp(lb3 - mx)
    p = e / jnp.sum(e, axis=0, keepdims=True)
    return p[0:1, :], p


def _hg_chunk_common(qr, fz, lbv):
    sq = _sigmoid(qr)
    q = qr * sq
    sig = _sigmoid(fz)
    f = lbv + (1.0 - lbv) * sig
    k = (1.0 - lbv) * (1.0 - sig)
    return q, sq, sig, f, k, jnp.log(f)


def _row_of(x, rows, r):
    return jnp.sum(jnp.where(rows == r, x, 0.0), axis=0, keepdims=True)


def _hg_fwd(proj, hg_lb, gn, slab=None, *, name):
    S = proj.shape[0]
    D = proj.shape[1] // 4
    H = D // LANES
    HB = min(HG_HEADS_PER_STEP, H)
    W = HB * LANES
    C = HG_CHUNK
    T = _pick(S, HG_TOKENS_PER_STEP, C)
    nch, nb = T // C, S // T
    ng = H // HB
    fused = slab is not None

    def body(q_ref, fz_ref, v_ref, g_ref, lb_ref, gn_ref, *rest):
        if fused:
            s_ref, y_ref, o_ref, sts_ref, out_ref, st, send_sems, recv_sems = rest
            first, passed, landed, from_sibling = _chip_slab_copies(s_ref, out_ref, send_sems, recv_sems)
            hgrp, n = pl.program_id(0), pl.program_id(1)

            @pl.when((hgrp == 0) & (n == 0))
            def _():
                for cp in first:
                    cp.start()

            @pl.when((hgrp == ng - 1) & (n == (3 * nb) // 4))
            def _():
                for arrived, onward in zip(landed, passed):
                    arrived.wait_recv()
                    onward.start()
        else:
            y_ref, o_ref, sts_ref, st = rest

        @pl.when(pl.program_id(1) == 0)
        def _():
            st[...] = jnp.zeros_like(st)

        lb_all, _ = _hg_lower_bound(lb_ref[...])
        gnv = gn_ref[...]
        ri = lax.broadcasted_iota(jnp.int32, (C, C), 0)
        ci_ = lax.broadcasted_iota(jnp.int32, (C, C), 1)
        low = ri >= ci_
        tri = jnp.where(low, 1.0, 0.0).astype(BF)
        rows = lax.broadcasted_iota(jnp.int32, (C, LANES), 0)

        def chunk(ci, carry):
            sl = pl.ds(pl.multiple_of(ci * C, C), C)
            for hh in range(HB):
                ls = slice(hh * LANES, (hh + 1) * LANES)
                q, _, _, _, k, logf = _hg_chunk_common(q_ref[sl, ls], fz_ref[sl, ls], lb_all[:, ls])
                vv = v_ref[sl, ls]
                gg = g_ref[sl, ls]
                G = _tri_dot(tri, logf)
                Gm = _row_of(G, rows, C // 2 - 1)
                Gl = _row_of(G, rows, C - 1)
                qt = q * jnp.exp(G - Gm)
                kt = k * jnp.exp(Gm - G)
                A = jnp.where(low, _dg1(qt, kt, NT), 0.0)
                Sv = st[hh]
                sts_ref[hh, ci] = Sv
                o = _dg1(A, vv, NN) + _dg1(q * jnp.exp(G), Sv, NT)
                st[hh] = Sv * jnp.exp(Gl) + _dg1(vv, k * jnp.exp(Gl - G), TN)
                r = lax.rsqrt(jnp.mean(o * o, axis=-1, keepdims=True) + EPS)
                y_ref[sl, ls] = ((o * r * gnv) * (gg * _sigmoid(gg))).astype(BF)
                o_ref[sl, ls] = o
            return carry

        lax.fori_loop(0, nch, chunk, 0)

        if fused:
            @pl.when((hgrp == ng - 1) & (n == nb - 1))
            def _():
                for cp in from_sibling:
                    cp.wait_recv()
                for cp in first + passed:
                    cp.wait_send()

    def part(p):
        return pl.BlockSpec((T, W), lambda h, n: (n, p * ng + h))

    blk = pl.BlockSpec((T, W), lambda h, n: (n, h))
    in_specs = [part(0), part(1), part(2), part(3),
                pl.BlockSpec((3, W), lambda h, n: (0, h)), pl.BlockSpec((1, LANES), lambda h, n: (0, 0))]
    out_specs = [blk, blk, pl.BlockSpec((HB, nch, LANES, LANES), lambda h, n: (h, n, 0, 0))]
    out_shape = [jax.ShapeDtypeStruct((S, D), BF), jax.ShapeDtypeStruct((S, D), F32),
                 jax.ShapeDtypeStruct((H, S // C, LANES, LANES), F32)]
    scratch = [pltpu.VMEM((HB, LANES, LANES), F32)]
    args = [proj, proj, proj, proj, hg_lb, gn]
    if fused:
        in_specs.append(HBM)
        out_specs.append(HBM)
        out_shape.append(jax.ShapeDtypeStruct((N_CHIPS,) + slab.shape, slab.dtype))
        scratch += [pltpu.SemaphoreType.DMA((6,)), pltpu.SemaphoreType.DMA((6,))]
        args.append(slab)
    return pl.pallas_call(
        body, name=name, grid=(ng, nb), in_specs=in_specs, out_specs=out_specs, out_shape=out_shape,
        scratch_shapes=scratch, compiler_params=_cp("arbitrary", "arbitrary"),
    )(*args)


def _hg_bwd(proj, hg_lb, gn, o_all, states, dy, part=None, *, name):
    S = proj.shape[0]
    D = proj.shape[1] // 4
    H = D // LANES
    HB = min(HG_HEADS_PER_STEP, H)
    W = HB * LANES
    C = HG_CHUNK
    T = _pick(S, HG_TOKENS_PER_STEP, C)
    nch, nb = T // C, S // T
    ng = H // HB
    fused = part is not None

    def body(q_ref, fz_ref, v_ref, g_ref, lb_ref, gn_ref, o_ref, sts_ref, dy_ref, *rest):
        if fused:
            p_ref, dp_ref, dlb_ref, dgn_ref, recv_ref, dst, dlb_acc, send_sems, recv_sems = rest
            copies = _scatter_copies(p_ref, recv_ref, send_sems, recv_sems)

            @pl.when((pl.program_id(0) == 0) & (pl.program_id(1) == 0))
            def _():
                for cp in copies:
                    cp.start()
        else:
            dp_ref, dlb_ref, dgn_ref, dst, dlb_acc = rest
        n = pl.program_id(1)

        @pl.when(n == 0)
        def _():
            dst[...] = jnp.zeros_like(dst)
            dlb_acc[...] = jnp.zeros_like(dlb_acc)
            dgn_ref[...] = jnp.zeros_like(dgn_ref)

        lb_all, p3 = _hg_lower_bound(lb_ref[...])
        gnv = gn_ref[...]
        ri = lax.broadcasted_iota(jnp.int32, (C, C), 0)
        ci_ = lax.broadcasted_iota(jnp.int32, (C, C), 1)
        low = ri >= ci_
        tri = jnp.where(low, 1.0, 0.0).astype(BF)
        triu = jnp.where(ri <= ci_, 1.0, 0.0).astype(BF)
        rows = lax.broadcasted_iota(jnp.int32, (C, LANES), 0)

        def chunk(cj, carry):
            ci = nch - 1 - cj
            sl = pl.ds(pl.multiple_of(ci * C, C), C)
            for hh in range(HB):
                ls = slice(hh * LANES, (hh + 1) * LANES)
                lbv = lb_all[:, ls]
                qr = q_ref[sl, ls]
                q, sq, sig, f, k, logf = _hg_chunk_common(qr, fz_ref[sl, ls], lbv)
                vv = v_ref[sl, ls]
                gg = g_ref[sl, ls]
                o = o_ref[sl, ls]
                dyv = dy_ref[sl, ls]
                G = _tri_dot(tri, logf)
                Gm = _row_of(G, rows, C // 2 - 1)
                Gl = _row_of(G, rows, C - 1)
                eG, e_qm, e_km, e_lk, eGl = jnp.exp(G), jnp.exp(G - Gm), jnp.exp(Gm - G), jnp.exp(Gl - G), jnp.exp(Gl)
                qt = q * e_qm
                kt = k * e_km
                A = jnp.where(low, _dg1(qt, kt, NT), 0.0)
                sg = _sigmoid(gg)
                r = lax.rsqrt(jnp.mean(o * o, axis=-1, keepdims=True) + EPS)
                on = o * r
                d_onw = dyv * (gg * sg)
                dgn_ref[hh] += jnp.sum(d_onw * on, axis=0, keepdims=True)
                dgg = dyv * (on * gnv) * (sg * (1.0 + gg * (1.0 - sg)))
                u = d_onw * gnv
                do = r * (u - on * jnp.mean(u * on, axis=-1, keepdims=True))
                Sv = sts_ref[hh, ci]
                dSv = dst[hh]
                dA = jnp.where(low, _dg3(do, vv, NT), 0.0)
                kdec = k * e_lk
                dv = _dg1(A, do, TN) + _dg1(kdec, dSv, NT)
                dq = _dg3(dA, kt, NN) * e_qm + eG * _dg3(do, Sv, NN)
                dk = _dg3(dA, qt, TN) * e_km + e_lk * _dg3(vv, dSv, NN)
                s_end = Sv * eGl + _dg3(vv, kdec, TN)
                dgl = jnp.sum(dSv * s_end, axis=0, keepdims=True)
                dG = q * dq - k * dk + jnp.where(rows == C - 1, dgl, 0.0)
                dlogf = _tri_dot(triu, dG) - f * dk
                dst[hh] = dSv * eGl + _dg1(do, q * eG, TN)
                dlf_f = dlogf / f
                dlb_acc[:, ls] += jnp.sum(dlf_f * (1.0 - sig), axis=0, keepdims=True)
                dp_ref[0, sl, ls] = (dq * (sq * (1.0 + qr * (1.0 - sq)))).astype(BF)
                dp_ref[1, sl, ls] = (dlf_f * (1.0 - lbv) * sig * (1.0 - sig)).astype(BF)
                dp_ref[2, sl, ls] = dv.astype(BF)
                dp_ref[3, sl, ls] = dgg.astype(BF)
            return carry

        lax.fori_loop(0, nch, chunk, 0)
        sel = jnp.where(lax.broadcasted_iota(jnp.int32, (3, W), 0) == 0, 1.0, 0.0)
        dlb_ref[...] = lb_all * (sel - p3) * dlb_acc[...]

        if fused:
            @pl.when((pl.program_id(0) == ng - 1) & (n == nb - 1))
            def _():
                for cp in copies:
                    cp.wait()

    def col(p):
        return pl.BlockSpec((T, W), lambda h, n: (nb - 1 - n, p * ng + h))

    blk = pl.BlockSpec((T, W), lambda h, n: (nb - 1 - n, h))
    in_specs = [col(0), col(1), col(2), col(3),
                pl.BlockSpec((3, W), lambda h, n: (0, h)), pl.BlockSpec((1, LANES), lambda h, n: (0, 0)),
                blk, pl.BlockSpec((HB, nch, LANES, LANES), lambda h, n: (h, nb - 1 - n, 0, 0)), blk]
    out_specs = [pl.BlockSpec((4, T, W), lambda h, n: (0, nb - 1 - n, h)),
                 pl.BlockSpec((3, W), lambda h, n: (0, h)),
                 pl.BlockSpec((HB, 1, LANES), lambda h, n: (h, 0, 0))]
    out_shape = [jax.ShapeDtypeStruct((4, S, D), BF), jax.ShapeDtypeStruct((3, D), F32),
                 jax.ShapeDtypeStruct((H, 1, LANES), F32)]
    scratch = [pltpu.VMEM((HB, LANES, LANES), F32), pltpu.VMEM((1, W), F32)]
    args = [proj, proj, proj, proj, hg_lb, gn, o_all, states, dy]
    if fused:
        in_specs.append(HBM)
        out_specs.append(HBM)
        out_shape.append(jax.ShapeDtypeStruct((3,) + part.shape[1:], part.dtype))
        scratch += [pltpu.SemaphoreType.DMA((3,)), pltpu.SemaphoreType.DMA((3,))]
        args.append(part)
    return pl.pallas_call(
        body, name=name, grid=(ng, nb), in_specs=in_specs, out_specs=out_specs, out_shape=out_shape,
        scratch_shapes=scratch, compiler_params=_cp("arbitrary", "arbitrary"),
    )(*args)


def _log_sigmoid(u):
    return jnp.minimum(u, 0.0) - jnp.log(1.0 + jnp.exp(-jnp.abs(u)))


def _lane_put(base, lane, first, pieces):
    for n, p in enumerate(pieces):
        base = jnp.where(lane == first + n, p, base)
    return base


def _fox_cumsum(proj, bf_pad, *, name):
    S = proj.shape[0]
    D = proj.shape[1] // 5
    T = _pick(S, 256, 8)

    def body(fz_ref, b_ref, f_ref, carry):
        @pl.when(pl.program_id(0) == 0)
        def _():
            carry[...] = jnp.zeros_like(carry)

        logf = _log_sigmoid(fz_ref[...] + b_ref[...])
        tri = jnp.where(lax.broadcasted_iota(jnp.int32, (T, T), 0) >= lax.broadcasted_iota(jnp.int32, (T, T), 1),
                        1.0, 0.0).astype(BF)
        fv = _tri_dot(tri, logf) + carry[...]
        f_ref[...] = fv
        carry[...] = _row_of(fv, lax.broadcasted_iota(jnp.int32, (T, LANES), 0), T - 1)

    return pl.pallas_call(
        body, name=name, grid=(S // T,),
        in_specs=[pl.BlockSpec((T, LANES), lambda i: (i, 4 * D // LANES)), pl.BlockSpec((1, LANES), lambda i: (0, 0))],
        out_specs=pl.BlockSpec((T, LANES), lambda i: (i, 0)),
        out_shape=jax.ShapeDtypeStruct((S, LANES), F32),
        scratch_shapes=[pltpu.VMEM((1, LANES), F32)],
        compiler_params=_cp("arbitrary"),
    )(proj, bf_pad)


def _pair_stats(sq, lo):
    del lo
    a = lax.broadcasted_iota(jnp.int32, (LANES, LANES), 0) < FOX_DH
    b = lax.broadcasted_iota(jnp.int32, (LANES, LANES), 1) < FOX_DH
    avg = jnp.where(a == b, 1.0 / FOX_DH, 0.0).astype(BF)
    hi, mid, low = _split3(sq)
    return _dot(hi, avg) + _dot(mid, avg) + _dot(low, avg)


def _fox_prep(proj, fcum, qw2, kw2, *, name):
    S = proj.shape[0]
    D = proj.shape[1] // 5
    HP = D // LANES
    T = _pick(S, 512, 16)

    def body(q_ref, k_ref, v_ref, f_ref, qw_ref, kw_ref, qa_ref, ka_ref, va_ref, vt_ref):
        hp = pl.program_id(1)
        lane = lax.broadcasted_iota(jnp.int32, (T, LANES), 1)
        lo = lane < FOX_DH
        qv, kv, vv, fv = q_ref[...], k_ref[...], v_ref[...], f_ref[...]
        qn = qv * lax.rsqrt(_pair_stats(qv * qv, lo) + EPS) * qw_ref[...] * (0.125 * LOG2E)
        kn = kv * lax.rsqrt(_pair_stats(kv * kv, lo) + EPS) * kw_ref[...]
        ones_q = jnp.where((lane >= 67) & (lane <= 69), 1.0, 0.0)
        ones_k = jnp.where(((lane >= 64) & (lane <= 66)) | ((lane >= 70) & (lane <= 72)), 1.0, 0.0)
        ones_v = jnp.where((lane >= 64) & (lane <= 66), 1.0, 0.0)
        for hh in range(2):
            fh = jnp.sum(jnp.where(lane == 2 * hp + hh, fv, 0.0), axis=-1, keepdims=True) * LOG2E
            pieces = [p.astype(F32) for p in _split3(fh)]

            def half(x):
                return jnp.where(lo, x if hh == 0 else pltpu.roll(x, FOX_DH, 1), 0.0)

            qa_ref[hh] = _lane_put(half(qn) + ones_q, lane, 64, pieces).astype(BF)
            ka_ref[hh] = _lane_put(half(kn) + ones_k, lane, 67, [-p for p in pieces]).astype(BF)
            va = half(vv) + ones_v
            va_ref[hh] = va.astype(BF)
            vt_ref[hh] = va.T.astype(BF)

    def part(p):
        return pl.BlockSpec((T, LANES), lambda i, hp: (i, p * HP + hp))

    vec = pl.BlockSpec((1, LANES), lambda i, hp: (0, 0))
    aug = pl.BlockSpec((2, T, LANES), lambda i, hp: (hp, i, 0))
    return pl.pallas_call(
        body, name=name, grid=(S // T, HP),
        in_specs=[part(0), part(1), part(2), pl.BlockSpec((T, LANES), lambda i, hp: (i, 0)), vec, vec],
        out_specs=[aug, aug, aug, pl.BlockSpec((2, LANES, T), lambda i, hp: (hp, 0, i))],
        out_shape=[jax.ShapeDtypeStruct((2 * HP, S, LANES), BF)] * 3 + [jax.ShapeDtypeStruct((2 * HP, LANES, S), BF)],
        compiler_params=_cp("parallel", "arbitrary"),
    )(proj, proj, proj, fcum, qw2, kw2)


def _fox_block(S):
    return _pick(S, 256, 16)


def _fox_skip_bounds(fcum, qn_w, kn_w, nheads):
    S = fcum.shape[0]
    B = _fox_block(S)
    qk = 8.0 * LOG2E * 1.02 * jnp.max(jnp.abs(qn_w)) * jnp.max(jnp.abs(kn_w))
    thresh = -(2.0 * qk + 160.0)
    f2 = fcum[:, :nheads] * LOG2E
    first, last = f2[0::B], f2[B - 1::B]
    nb = S // B
    blk = jnp.arange(nb)
    dead = (first[0::2, None, :] - last[None, :, :]) < thresh
    jmin = jnp.sum(dead & (blk[None, :, None] < 2 * jnp.arange(nb // 2)[:, None, None]), axis=1)
    live = (first[:, None, :] - last[None, :, :]) >= thresh
    imax = blk[:, None] + jnp.sum(live & (blk[:, None, None] > blk[None, :, None]), axis=0)
    return jmin.T.astype(jnp.int32), imax.T.astype(jnp.int32)


def _fox_fwd(jmin, qa, ka, vat, proj, *, name):
    H, S, _ = qa.shape
    HP = H // 2
    D = HP * LANES
    B = _fox_block(S)
    BQ = 2 * B
    nq = S // BQ

    def body(jmin_ref, q_ref, k_ref, vt_ref, g_ref, y_ref, o_ref, q2_ref):
        hp, i = pl.program_id(0), pl.program_id(1)
        lane = lax.broadcasted_iota(jnp.int32, (BQ, LANES), 1)
        lo = lane < FOX_DH
        in_stat = (lane >= 70) & (lane <= 75)
        causal = lax.broadcasted_iota(jnp.int32, (BQ, BQ), 0) <= lax.broadcasted_iota(jnp.int32, (BQ, BQ), 1)
        row = lax.broadcasted_iota(jnp.int32, (LANES, BQ), 0)
        m0, acc0 = jnp.full((1, BQ), -jnp.inf, F32), jnp.zeros((LANES, BQ), F32)
        outs = []
        for hh in range(2):
            qb = q_ref[hh]

            def block(j, carry, masked=False):
                m, acc = carry
                sl = pl.ds(pl.multiple_of(j * BQ, BQ), BQ)
                st = _dg(k_ref[hh, sl, :], qb, NT)
                if masked:
                    st = jnp.where(causal, st, -jnp.inf)
                m_new = jnp.maximum(m, jnp.ceil(jnp.max(st, axis=0, keepdims=True)))
                p = jnp.exp2(st - m_new).astype(BF)
                return m_new, acc * jnp.exp2(m - m_new) + _dot(vt_ref[hh, :, sl], p)

            carry = lax.fori_loop(jmin_ref[2 * hp + hh, i] // 2, i, block, (m0, acc0))
            m, acc = block(i, carry, masked=True)
            linv = 1.0 / jnp.sum(jnp.where(row == FOX_DH, acc, 0.0), axis=0, keepdims=True)
            tile = acc * linv
            for n, piece in enumerate(_split3(m) + _split3(linv)):
                tile = jnp.where(row == 70 + n, piece.astype(F32), tile)
            tile = tile.T
            outs.append(tile)
            q2_ref[hh] = jnp.where(in_stat, jnp.where(lane <= 72, -tile, tile), qb.astype(F32)).astype(BF)
        o = jnp.where(lo, outs[0], pltpu.roll(outs[1], FOX_DH, 1))
        o_ref[...] = o
        y_ref[...] = (o * _sigmoid(g_ref[...])).astype(BF)

    blk = pl.BlockSpec((BQ, LANES), lambda hp, i, jm: (i, hp))
    qblk = pl.BlockSpec((2, BQ, LANES), lambda hp, i, jm: (hp, i, 0))
    full = pl.BlockSpec((2, S, LANES), lambda hp, i, jm: (hp, 0, 0))
    full_t = pl.BlockSpec((2, LANES, S), lambda hp, i, jm: (hp, 0, 0))
    return pl.pallas_call(
        body, name=name,
        grid_spec=pltpu.PrefetchScalarGridSpec(
            num_scalar_prefetch=1, grid=(HP, nq),
            in_specs=[qblk, full, full_t, pl.BlockSpec((BQ, LANES), lambda hp, i, jm: (i, 3 * HP + hp))],
            out_specs=[blk, blk, qblk]),
        out_shape=[jax.ShapeDtypeStruct((S, D), BF), jax.ShapeDtypeStruct((S, D), F32),
                   jax.ShapeDtypeStruct((H, S, LANES), BF)],
        compiler_params=_cp("parallel", "arbitrary"),
    )(jmin, qa, ka, vat, proj)


def _fox_bwd_prep(dy, o, proj, q2, *, name):
    S, D = dy.shape
    HP = D // LANES
    T = _pick(S, 512, 16)

    def body(dy_ref, o_ref, g_ref, q2_ref, da_ref):
        lane = lax.broadcasted_iota(jnp.int32, (T, LANES), 1)
        lo = lane < FOX_DH
        in_linv = (lane >= 73) & (lane <= 75)
        linv = [jnp.sum(jnp.where(in_linv, q2_ref[hh].astype(F32), 0.0), axis=-1, keepdims=True) for hh in range(2)]
        u = (dy_ref[...] * _sigmoid(g_ref[...]) * jnp.where(lo, linv[0], linv[1])).astype(BF).astype(F32)
        prod = u * o_ref[...]
        d_lo = jnp.sum(jnp.where(lo, prod, 0.0), axis=-1, keepdims=True)
        d_hi = jnp.sum(jnp.where(lo, 0.0, prod), axis=-1, keepdims=True)
        for hh, delta in enumerate((d_lo, d_hi)):
            base = jnp.where(lo, u if hh == 0 else pltpu.roll(u, FOX_DH, 1), 0.0)
            da_ref[hh] = _lane_put(base, lane, 64, [-(p.astype(F32)) for p in _split3(delta)]).astype(BF)

    blk = pl.BlockSpec((T, LANES), lambda i, hp: (i, hp))
    aug = pl.BlockSpec((2, T, LANES), lambda i, hp: (hp, i, 0))
    return pl.pallas_call(
        body, name=name, grid=(S // T, HP),
        in_specs=[blk, blk, pl.BlockSpec((T, LANES), lambda i, hp: (i, 3 * HP + hp)), aug],
        out_specs=aug,
        out_shape=jax.ShapeDtypeStruct((2 * HP, S, LANES), BF),
        compiler_params=_cp("parallel", "arbitrary"),
    )(dy, o, proj, q2)


def _fox_bwd(imax, q2, ka, va, doa, *, name):
    H, S, _ = q2.shape
    B = _fox_block(S)
    nb = S // B

    def body(imax_ref, q_ref, do_ref, k_ref, v_ref, dq_ref, dk_ref, dv_ref, cs_ref):
        j = pl.program_id(1)
        end = imax_ref[pl.program_id(0), j] + 1

        @pl.when(j == 0)
        def _():
            dq_ref[...] = jnp.zeros_like(dq_ref)

        kb, vb = k_ref[...], v_ref[...]

        def step(i, carry, nblk=1):
            dk_acc, dv_acc, cs_acc = carry
            rows = nblk * B
            sl = pl.ds(pl.multiple_of(i * B, B), rows)
            qb, dob = q_ref[sl, :], do_ref[sl, :]
            s = _dg(qb, kb, NT)
            ahead = lax.broadcasted_iota(jnp.int32, (rows, B), 0) - lax.broadcasted_iota(jnp.int32, (rows, B), 1)
            pb = jnp.exp2(jnp.where(ahead >= (j - i) * B, s, -jnp.inf)).astype(BF)
            ds = pb.astype(F32) * _dg(dob, vb, NT)
            dsb = ds.astype(BF)
            cs_acc = cs_acc + jnp.sum(ds.reshape(rows // 8, 8, B), axis=0)
            dv_acc = dv_acc + _dg(pb, dob, TN)
            dk_acc = dk_acc + _dg(dsb, qb, TN)
            dq_ref[sl, :] += _dot(dsb, kb)
            return dk_acc, dv_acc, cs_acc

        zero = jnp.zeros((B, LANES), F32)
        carry = (zero, zero, jnp.zeros((8, B), F32))
        pos = j
        for U in FOX_BWD_TILES:
            n = (end - pos) // U
            carry = lax.fori_loop(0, n, lambda ii, c, pos=pos, U=U: step(pos + U * ii, c, nblk=U), carry)
            pos = pos + U * n
        dk_acc, dv_acc, cs_acc = carry
        dk_ref[...] = dk_acc
        dv_ref[...] = dv_acc
        cs_ref[...] = jnp.sum(cs_acc, axis=0, keepdims=True)

    full = pl.BlockSpec((None, S, LANES), lambda h, j, im: (h, 0, 0))
    blk = pl.BlockSpec((None, B, LANES), lambda h, j, im: (h, j, 0))
    return pl.pallas_call(
        body, name=name,
        grid_spec=pltpu.PrefetchScalarGridSpec(
            num_scalar_prefetch=1, grid=(H, nb),
            in_specs=[full, full, blk, blk],
            out_specs=[full, blk, blk, pl.BlockSpec((None, 1, B), lambda h, j, im: (h, 0, j))]),
        out_shape=[jax.ShapeDtypeStruct((H, S, LANES), F32)] * 3 + [jax.ShapeDtypeStruct((H, 1, S), F32)],
        compiler_params=_cp("parallel", "arbitrary"),
    )(imax, q2, doa, ka, va)


def _fox_bwd_post(dqa, dka, dva, proj, dy, o, qw2, kw2, *, name):
    S, D = dy.shape
    HP = D // LANES
    T = _pick(S, 512, 16)

    def body(dq_ref, dk_ref, dv_ref, q_ref, k_ref, g_ref, dy_ref, o_ref, qw_ref, kw_ref, dp_ref, dqw_ref, dkw_ref):
        @pl.when((pl.program_id(0) == 0) & (pl.program_id(1) == 0))
        def _():
            dqw_ref[...] = jnp.zeros_like(dqw_ref)
            dkw_ref[...] = jnp.zeros_like(dkw_ref)

        lane = lax.broadcasted_iota(jnp.int32, (T, LANES), 1)
        lo = lane < FOX_DH

        def pair(ref):
            return jnp.where(lo, ref[0], pltpu.roll(ref[1], FOX_DH, 1))

        def norm_bwd(xv, w, dyn, dw_ref):
            r = lax.rsqrt(_pair_stats(xv * xv, lo) + EPS)
            xr = xv * r
            dw_ref[...] += jnp.sum(dyn * xr, axis=0, keepdims=True)
            u = dyn * w
            return r * (u - xr * _pair_stats(u * xr, lo))

        dp_ref[0] = norm_bwd(q_ref[...], qw_ref[...], pair(dq_ref) * 0.125, dqw_ref).astype(BF)
        dp_ref[1] = norm_bwd(k_ref[...], kw_ref[...], pair(dk_ref) * (1.0 / LOG2E), dkw_ref).astype(BF)
        dp_ref[2] = pair(dv_ref).astype(BF)
        sg = _sigmoid(g_ref[...])
        dp_ref[3] = (dy_ref[...] * o_ref[...] * sg * (1.0 - sg)).astype(BF)

    def part(p):
        return pl.BlockSpec((T, LANES), lambda i, hp: (i, p * HP + hp))

    aug = pl.BlockSpec((2, T, LANES), lambda i, hp: (hp, i, 0))
    blk = pl.BlockSpec((T, LANES), lambda i, hp: (i, hp))
    vec = pl.BlockSpec((1, LANES), lambda i, hp: (0, 0))
    return pl.pallas_call(
        body, name=name, grid=(S // T, HP),
        in_specs=[aug, aug, aug, part(0), part(1), part(3), blk, blk, vec, vec],
        out_specs=[pl.BlockSpec((4, T, LANES), lambda i, hp: (0, i, hp)), vec, vec],
        out_shape=[jax.ShapeDtypeStruct((5, S, D), BF), jax.ShapeDtypeStruct((1, LANES), F32),
                   jax.ShapeDtypeStruct((1, LANES), F32)],
        compiler_params=_cp("arbitrary", "arbitrary"),
    )(dqa, dka, dva, proj, proj, proj, dy, o, qw2, kw2)


def _fox_dfz(colsum, nheads, proj, bf_pad, dproj, *, name):
    S = colsum.shape[0]
    H = nheads
    D = dproj.shape[2]
    T = _pick(S, 256, 16)
    nb = S // T

    def body(cs_ref, fz_ref, b_ref, _, dp_ref, db_ref, carry):
        @pl.when(pl.program_id(0) == 0)
        def _():
            carry[...] = jnp.zeros_like(carry)
            db_ref[...] = jnp.zeros_like(db_ref)

        lane = lax.broadcasted_iota(jnp.int32, (T, LANES), 1)
        df = -cs_ref[...]
        triu = jnp.where(lax.broadcasted_iota(jnp.int32, (T, T), 0) <= lax.broadcasted_iota(jnp.int32, (T, T), 1),
                         1.0, 0.0).astype(BF)
        dlogf = _tri_dot(triu, df) + carry[...]
        carry[...] = _row_of(dlogf, lax.broadcasted_iota(jnp.int32, (T, LANES), 0), 0)
        dfz = jnp.where(lane < H, dlogf * _sigmoid(-(fz_ref[...] + b_ref[...])), 0.0)
        db_ref[...] += jnp.sum(dfz, axis=0, keepdims=True)
        dp_ref[...] = jnp.zeros_like(dp_ref)
        dp_ref[:, 0:LANES] = dfz.astype(BF)

    return pl.pallas_call(
        body, name=name, grid=(nb,),
        in_specs=[pl.BlockSpec((T, LANES), lambda i: (nb - 1 - i, 0)),
                  pl.BlockSpec((T, LANES), lambda i: (nb - 1 - i, 4 * D // LANES)),
                  pl.BlockSpec((1, LANES), lambda i: (0, 0)),
                  pl.BlockSpec(memory_space=pl.ANY)],
        out_specs=[pl.BlockSpec((None, T, D), lambda i: (4, nb - 1 - i, 0)), pl.BlockSpec((1, LANES), lambda i: (0, 0))],
        out_shape=[jax.ShapeDtypeStruct(dproj.shape, BF), jax.ShapeDtypeStruct((1, LANES), F32)],
        scratch_shapes=[pltpu.VMEM((1, LANES), F32)],
        input_output_aliases={3: 0},
        compiler_params=_cp("arbitrary"),
    )(colsum, proj, bf_pad, dproj)


def _mod_fwd(c16, w, b, *, name):
    L, D, N = w.shape
    tn = _pick(N, 512)

    def body(c_ref, w_ref, b_ref, o_ref):
        cv = c_ref[...]
        ca = (cv * _sigmoid(cv)).astype(BF)
        o_ref[...] = _dot(ca, w_ref[...].astype(BF)) + b_ref[...]

    return pl.pallas_call(
        body, name=name, grid=(L, N // tn),
        in_specs=[pl.BlockSpec((16, D), lambda l, j: (0, 0)), pl.BlockSpec((None, D, tn), lambda l, j: (l, 0, j)),
                  pl.BlockSpec((None, 1, tn), lambda l, j: (l, 0, j))],
        out_specs=pl.BlockSpec((None, 16, tn), lambda l, j: (l, 0, j)),
        out_shape=jax.ShapeDtypeStruct((L, 16, N), F32),
        compiler_params=_cp("parallel", "arbitrary"),
    )(c16, w, b)


def _mod_bwd(c16, dmod, *, name):
    L, _, N = dmod.shape
    D = c16.shape[1]
    tn = _pick(N, 512)

    def body(c_ref, d_ref, o_ref):
        cv = c_ref[...]
        ca = (cv * _sigmoid(cv)).astype(BF)
        o_ref[...] = _dg(ca, d_ref[...].astype(BF), TN)

    return pl.pallas_call(
        body, name=name, grid=(L, N // tn),
        in_specs=[pl.BlockSpec((16, D), lambda l, j: (0, 0)), pl.BlockSpec((None, 16, tn), lambda l, j: (l, 0, j))],
        out_specs=pl.BlockSpec((None, D, tn), lambda l, j: (l, 0, j)),
        out_shape=jax.ShapeDtypeStruct((L, D, N), F32),
        compiler_params=_cp("parallel", "arbitrary"),
    )(c16, dmod)


def _adamw_math(w, g, m, v):
    m = ADAM_B1 * m + (1.0 - ADAM_B1) * g
    v = ADAM_B2 * v + (1.0 - ADAM_B2) * (g * g)
    m_hat = m / (1.0 - ADAM_B1 ** ADAM_STEP)
    v_hat = v / (1.0 - ADAM_B2 ** ADAM_STEP)
    return -ADAM_LR * (m_hat / (jnp.sqrt(v_hat) + ADAM_EPS) + ADAM_WD * w), m, v


def _adamw(w, g, m, v, *, g_at=None, name):
    R, C = w.shape
    row0 = 0 if g_at is None else g_at[1]
    tr = min(math.gcd(row0, 256) if row0 else 256, -(-R // 8) * 8)
    g0 = row0 // tr
    if g_at is None:
        g_spec = pl.BlockSpec((tr, C), lambda i: (i, 0))
    else:
        g_spec = pl.BlockSpec((None, tr, C), lambda i: (g_at[0], g0 + i, 0))

    def body(w_ref, g_ref, m_ref, v_ref, d_ref, mo_ref, vo_ref):
        d, mn, vn = _adamw_math(w_ref[...], g_ref[...], m_ref[...], v_ref[...])
        d_ref[...] = d
        mo_ref[...] = mn
        vo_ref[...] = vn

    blk = pl.BlockSpec((tr, C), lambda i: (i, 0))
    return pl.pallas_call(
        body, name=name, grid=(pl.cdiv(R, tr),),
        in_specs=[blk, g_spec, blk, blk],
        out_specs=[blk, blk, blk],
        out_shape=[jax.ShapeDtypeStruct((R, C), F32)] * 3,
        compiler_params=_cp("parallel"),
    )(w, g, m, v)


def _sum_parts(parts, *, name):
    P, R, C = parts.shape

    def body(p_ref, o_ref):
        acc = p_ref[0]
        for p in range(1, P):
            acc = acc + p_ref[p]
        o_ref[...] = acc

    return pl.pallas_call(
        body, name=name, grid=(1,),
        in_specs=[pl.BlockSpec((P, R, C), lambda i: (0, 0, 0))],
        out_specs=pl.BlockSpec((R, C), lambda i: (0, 0)),
        out_shape=jax.ShapeDtypeStruct((R, C), F32),
        compiler_params=_cp("arbitrary"),
    )(parts)


def _add_halves(g4, recv, c_idx, *, name):
    _, _, Rh, C = g4.shape
    tr = min(256, Rh)

    def body(c_ref, a_ref, b_ref, o_ref):
        o_ref[...] = (a_ref[...] + b_ref[...].astype(F32)).astype(BF)

    return pl.pallas_call(
        body, name=name,
        grid_spec=pltpu.PrefetchScalarGridSpec(
            num_scalar_prefetch=1, grid=(4, pl.cdiv(Rh, tr)),
            in_specs=[pl.BlockSpec((None, None, tr, C), lambda j, r, c: (j, c[0], r, 0)),
                      pl.BlockSpec((None, tr, C), lambda j, r, c: (j, r, 0))],
            out_specs=pl.BlockSpec((None, tr, C), lambda j, r, c: (j, r, 0))),
        out_shape=jax.ShapeDtypeStruct((4, Rh, C), BF),
        compiler_params=_cp("parallel", "arbitrary"),
    )(c_idx, g4, recv)


def _add_four(g4, from_sibling, from_chips, pos, *, name):
    _, _, Rh, C = g4.shape
    tr = min(256, Rh)

    def body(p_ref, a_ref, s_ref, b_ref, o_ref):
        own = a_ref[...] + s_ref[...].astype(F32)
        o_ref[...] = ((own + b_ref[0].astype(F32)) + b_ref[1].astype(F32)) + b_ref[2].astype(F32)

    return pl.pallas_call(
        body, name=name,
        grid_spec=pltpu.PrefetchScalarGridSpec(
            num_scalar_prefetch=1, grid=(pl.cdiv(Rh, tr),),
            in_specs=[pl.BlockSpec((None, None, tr, C), lambda r, p: (p[0], p[1], r, 0)),
                      pl.BlockSpec((None, tr, C), lambda r, p: (p[0], r, 0)),
                      pl.BlockSpec((3, tr, C), lambda r, p: (0, r, 0))],
            out_specs=pl.BlockSpec((None, tr, C), lambda r, p: (p[1], r, 0))),
        out_shape=jax.ShapeDtypeStruct((2, Rh, C), F32),
        compiler_params=_cp("arbitrary"),
    )(pos, g4, from_sibling, from_chips)


HBM = pl.BlockSpec(memory_space=pltpu.HBM)


def _mesh_pos():
    return lax.axis_index("x"), lax.axis_index("y"), lax.axis_index("c")


def _other_chips(x, y):
    return [(1 - x, y), (x, 1 - y), (1 - x, 1 - y)]


def _allgather_small(xs, *, name):
    m_per, n = xs.shape

    def body(x_ref, out_ref, send_sems, recv_sems, local_sem):
        x, y, c = _mesh_pos()
        me, sibling = (x, y, c), (x, y, 1 - c)
        chips = _other_chips(x, y)

        def rows(px, py, pc):
            return out_ref.at[pl.ds((4 * px + 2 * py + pc) * m_per, m_per), :]

        def copy(k, block, to, src=None):
            return pltpu.make_async_remote_copy(
                src_ref=rows(*block) if src is None else src, dst_ref=rows(*block),
                send_sem=send_sems.at[k], recv_sem=recv_sems.at[k], device_id=to, device_id_type=MESH)

        mine = pltpu.make_async_copy(x_ref, rows(*me), local_sem)
        mine.start()
        first = [copy(0, me, sibling, src=x_ref)]
        first += [copy(1 + j, me, (*chip, c), src=x_ref) for j, chip in enumerate(chips)]
        for cp in first:
            cp.start()
        passed = [copy(4 + j, (*chip, c), sibling) for j, chip in enumerate(chips)]
        for j, chip in enumerate(chips):
            copy(1 + j, (*chip, c), me).wait_recv()
            passed[j].start()
        copy(0, sibling, me).wait_recv()
        for j, chip in enumerate(chips):
            copy(4 + j, (*chip, 1 - c), me).wait_recv()
        for cp in first + passed:
            cp.wait_send()
        mine.wait()

    return pl.pallas_call(
        body, name=name,
        out_shape=jax.ShapeDtypeStruct((N_DEV * m_per, n), xs.dtype),
        in_specs=[pl.BlockSpec(memory_space=pltpu.VMEM)],
        out_specs=pl.BlockSpec(memory_space=pltpu.VMEM),
        scratch_shapes=[pltpu.SemaphoreType.DMA((7,)), pltpu.SemaphoreType.DMA((7,)), pltpu.SemaphoreType.DMA],
    )(xs)


def _chip_slab_copies(s_ref, out_ref, send_sems, recv_sems):
    R = s_ref.shape[0]
    Rh = R // 2
    x, y, c = _mesh_pos()
    me, sibling = (x, y, c), (x, y, 1 - c)
    chips = _other_chips(x, y)

    def half(px, py, pc):
        return out_ref.at[2 * px + py, pl.ds(pc * Rh, Rh), :]

    def copy(k, block, to, src=None):
        return pltpu.make_async_remote_copy(
            src_ref=half(*block) if src is None else src, dst_ref=half(*block),
            send_sem=send_sems.at[k], recv_sem=recv_sems.at[k], device_id=to, device_id_type=MESH)

    first = [copy(j, me, (*chip, c), src=s_ref.at[pl.ds(c * Rh, Rh), :]) for j, chip in enumerate(chips)]
    passed = [copy(3 + j, (*chip, c), sibling) for j, chip in enumerate(chips)]
    landed = [copy(j, (*chip, c), me) for j, chip in enumerate(chips)]
    from_sibling = [copy(3 + j, (*chip, 1 - c), me) for j, chip in enumerate(chips)]
    return first, passed, landed, from_sibling


def _allgather_chip_slabs(slab, *, name):
    R, C = slab.shape

    def body(s_ref, out_ref, send_sems, recv_sems):
        first, passed, landed, from_sibling = _chip_slab_copies(s_ref, out_ref, send_sems, recv_sems)
        for cp in first:
            cp.start()
        for arrived, onward in zip(landed, passed):
            arrived.wait_recv()
            onward.start()
        for cp in from_sibling:
            cp.wait_recv()
        for cp in first + passed:
            cp.wait_send()

    return pl.pallas_call(
        body, name=name,
        out_shape=jax.ShapeDtypeStruct((N_CHIPS, R, C), slab.dtype),
        in_specs=[HBM], out_specs=HBM,
        scratch_shapes=[pltpu.SemaphoreType.DMA((6,)), pltpu.SemaphoreType.DMA((6,))],
    )(slab)


def _swap_halves(mine, *, name):
    def body(g_ref, out_ref, send_sems, recv_sems):
        x, y, c = _mesh_pos()
        copies = [pltpu.make_async_remote_copy(
            src_ref=g_ref.at[j], dst_ref=out_ref.at[j], send_sem=send_sems.at[j], recv_sem=recv_sems.at[j],
            device_id=(x, y, 1 - c), device_id_type=MESH) for j in range(N_CHIPS)]
        for cp in copies:
            cp.start()
        for cp in copies:
            cp.wait()

    return pl.pallas_call(
        body, name=name,
        out_shape=jax.ShapeDtypeStruct(mine.shape, mine.dtype),
        in_specs=[HBM], out_specs=HBM,
        scratch_shapes=[pltpu.SemaphoreType.DMA((N_CHIPS,)), pltpu.SemaphoreType.DMA((N_CHIPS,))],
    )(mine)


def _scatter_copies(p_ref, out_ref, send_sems, recv_sems):
    x, y, c = _mesh_pos()
    return [pltpu.make_async_remote_copy(
        src_ref=p_ref.at[2 * px + py], dst_ref=out_ref.at[j], send_sem=send_sems.at[j], recv_sem=recv_sems.at[j],
        device_id=(px, py, c), device_id_type=MESH) for j, (px, py) in enumerate(_other_chips(x, y))]


def _scatter_partials(part, *, name):
    _, Rh, C = part.shape

    def body(p_ref, out_ref, send_sems, recv_sems):
        copies = _scatter_copies(p_ref, out_ref, send_sems, recv_sems)
        for cp in copies:
            cp.start()
        for cp in copies:
            cp.wait()

    return pl.pallas_call(
        body, name=name,
        out_shape=jax.ShapeDtypeStruct((3, Rh, C), part.dtype),
        in_specs=[HBM], out_specs=HBM,
        scratch_shapes=[pltpu.SemaphoreType.DMA((3,)), pltpu.SemaphoreType.DMA((3,))],
    )(part)


def _join_halves(buf, *, name):
    def body(b_ref, out_ref, send_sem, recv_sem):
        x, y, c = _mesh_pos()
        cp = pltpu.make_async_remote_copy(
            src_ref=b_ref.at[c], dst_ref=out_ref.at[c], send_sem=send_sem, recv_sem=recv_sem,
            device_id=(x, y, 1 - c), device_id_type=MESH)
        cp.start()
        cp.wait()

    return pl.pallas_call(
        body, name=name,
        out_shape=jax.ShapeDtypeStruct(buf.shape, buf.dtype),
        in_specs=[HBM], out_specs=HBM, input_output_aliases={0: 0},
        scratch_shapes=[pltpu.SemaphoreType.DMA, pltpu.SemaphoreType.DMA],
    )(buf)


def _pad_rows(a, mult):
    pad = (-a.shape[0]) % mult
    return a if pad == 0 else jnp.pad(a, ((0, pad),) + ((0, 0),) * (a.ndim - 1))


def _local_step(x, target, mod, wts, small, slab_rest=None, unpack_rest=None, reduce_early=None):
    S, D = x.shape
    HP = D // LANES
    row = lambda v: v.reshape(1, -1)
    msplit = [[row(mod[i, k * D:(k + 1) * D]) for k in range(6)] for i in range(2)]
    gw, gs = {}, {}
    dmod = [[None] * 6 for _ in range(2)]

    sh1, sc1, g1, sh2, sc2, g2 = msplit[0]
    n1w0, n2w0 = row(small["norm1_w"][0]), row(small["norm2_w"][0])
    proj0, h1_0 = _ln_matmul(x, n1w0, sc1, sh1, wts["hg_w_in"], relu2=False, name="hg_in_proj")
    gn = small["hg_gn_w"].reshape(1, LANES)
    ypre0, o0, states, *gathered = _hg_fwd(proj0, small["hg_lb"], gn, slab_rest, name="hg_fwd")
    if slab_rest is not None:
        wts = {**wts, **unpack_rest(gathered[0])}
    x1, ymix0 = _matmul_resid(ypre0, wts["hg_w_out"], x, g1, name="hg_out_proj")
    a0, u0, h2_0 = _ln_matmul(x1, n2w0, sc2, sh2, wts["mlp_w1_0"], relu2=True, name="mlp0_up")
    x2, ymlp0 = _matmul_resid(u0, wts["mlp_w2_0"], x1, g2, name="mlp0_down")

    sh1b, sc1b, g1b, sh2b, sc2b, g2b = msplit[1]
    n1w1, n2w1 = row(small["norm1_w"][1]), row(small["norm2_w"][1])
    proj1, h1_1 = _ln_matmul(x2, n1w1, sc1b, sh1b, wts["fox_w_in"], relu2=False, name="fox_in_proj")
    nheads = 2 * HP
    bf_pad = jnp.pad(small["fox_b_f"].reshape(1, nheads), ((0, 0), (0, LANES - nheads)))
    qw2 = jnp.tile(small["fox_qn_w"].reshape(1, FOX_DH), (1, 2))
    kw2 = jnp.tile(small["fox_kn_w"].reshape(1, FOX_DH), (1, 2))
    fcum = _fox_cumsum(proj1, bf_pad, name="fox_cumsum")
    qa, ka, va, vat = _fox_prep(proj1, fcum, qw2, kw2, name="fox_prep")
    jmin, imax = _fox_skip_bounds(fcum, small["fox_qn_w"], small["fox_kn_w"], nheads)
    ypre1, o1, q2 = _fox_fwd(jmin, qa, ka, vat, proj1, name="fox_fwd")
    x3, ymix1 = _matmul_resid(ypre1, wts["fox_w_out"], x2, g1b, name="fox_out_proj")
    a1, u1, h2_1 = _ln_matmul(x3, n2w1, sc2b, sh2b, wts["mlp_w1_1"], relu2=True, name="mlp1_up")
    x4, ymlp1 = _matmul_resid(u1, wts["mlp_w2_1"], x3, g2b, name="mlp1_down")

    loss, dx4, dfw = _loss_kernel(x4, row(small["final_w"]), target, name="loss")
    gs["final_w"] = dfw.reshape(-1)

    def mlp_bwd(i, dx_out, x_in, h2, a, u, ymlp, n2w, sc2_, g2_):
        dz, dm, dg2 = _gate_matmul_nt(dx_out, g2_, ymlp, wts[f"mlp_w2_{i}"], a, name=f"mlp{i}_down_bwd")
        gw[f"mlp_w2_{i}"] = _matmul_tn(u, dm[None], name=f"mlp{i}_dw2")
        gw[f"mlp_w1_{i}"] = _matmul_tn(h2, dz[None], name=f"mlp{i}_dw1")
        dx_in, dsc, dsh, dnw = _matmul_nt_lnbwd(dz[None], wts[f"mlp_w1_{i}"], x_in, n2w, sc2_, dx_out,
                                                name=f"mlp{i}_up_bwd")
        dmod[i][3], dmod[i][4], dmod[i][5] = dsh, dsc, dg2
        return dx_in, dnw

    dx3, dn2w1 = mlp_bwd(1, dx4, x3, h2_1, a1, u1, ymlp1, n2w1, sc2b, g2b)
    dyp1, dm1, dg1b = _gate_matmul_nt(dx3, g1b, ymix1, wts["fox_w_out"], None, name="fox_out_bwd")
    gw["fox_w_out"] = _matmul_tn(ypre1, dm1[None], name="fox_dw_out")
    doa = _fox_bwd_prep(dyp1, o1, proj1, q2, name="fox_bwd_prep")
    dqa, dka, dva, colsum = _fox_bwd(imax, q2, ka, va, doa, name="fox_bwd")
    colsum = jnp.pad(colsum[:, 0, :].T, ((0, 0), (0, LANES - nheads)))
    dproj1, dqw, dkw = _fox_bwd_post(dqa, dka, dva, proj1, dyp1, o1, qw2, kw2, name="fox_bwd_post")
    dproj1, dbf = _fox_dfz(colsum, nheads, proj1, bf_pad, dproj1, name="fox_dfz")
    gw["fox_w_in"] = _matmul_tn(h1_1, dproj1, name="fox_dw_in")
    dx2, dsc, dsh, dn1w1 = _matmul_nt_lnbwd(dproj1, wts["fox_w_in"], x2, n1w1, sc1b, dx3, name="fox_in_bwd")
    dmod[1][0], dmod[1][1], dmod[1][2] = dsh, dsc, dg1b
    gs["fox_qn_w"] = dqw[0, :FOX_DH] + dqw[0, FOX_DH:]
    gs["fox_kn_w"] = dkw[0, :FOX_DH] + dkw[0, FOX_DH:]
    gs["fox_b_f"] = dbf[0, :nheads]

    dx1, dn2w0 = mlp_bwd(0, dx2, x1, h2_0, a0, u0, ymlp0, n2w0, sc2, g2)
    dyp0, dm0, dg1 = _gate_matmul_nt(dx1, g1, ymix0, wts["hg_w_out"], None, name="hg_out_bwd")
    gw["hg_w_out"] = _matmul_tn(ypre0, dm0[None], name="hg_dw_out")
    part, ctx = reduce_early(gw) if reduce_early is not None else (None, None)
    dproj0, dlb, dgn, *from_chips = _hg_bwd(proj0, small["hg_lb"], gn, o0, states, dyp0, part, name="hg_bwd")
    early = (ctx, from_chips[0]) if reduce_early is not None else None
    gw["hg_w_in"] = _matmul_tn(h1_0, dproj0, name="hg_dw_in")
    dx0, dsc, dsh, dn1w0 = _matmul_nt_lnbwd(dproj0, wts["hg_w_in"], x, n1w0, sc1, dx1, name="hg_in_bwd")
    dmod[0][0], dmod[0][1], dmod[0][2] = dsh, dsc, dg1
    gs["hg_lb"] = dlb
    gs["hg_gn_w"] = jnp.sum(dgn, axis=0)

    gs["norm1_w"] = jnp.concatenate([dn1w0, dn1w1], axis=0)
    gs["norm2_w"] = jnp.concatenate([dn2w0, dn2w1], axis=0)
    gs["dmod"] = jnp.stack([jnp.concatenate(dmod[i], axis=1)[0] for i in range(2)])
    return loss, dx0, gw, gs, early


def _pack_halves(layout):
    rh = -(-max(sum(a.shape[0] for _, a in half) for half in layout) // 16) * 16
    place, parts = {}, []
    for h, half in enumerate(layout):
        off = 0
        for n, a in half:
            place[n] = (h, off, a.shape[0])
            off += a.shape[0]
        parts.append(jnp.pad(jnp.concatenate([a.astype(BF) for _, a in half], axis=0), ((0, rh - off), (0, 0))))
    return jnp.concatenate(parts, axis=0), place, rh


SMALL_NAMES = ["norm1_w", "norm2_w", "hg_lb", "hg_gn_w", "fox_b_f", "fox_qn_w", "fox_kn_w", "final_w"]


def _pack_small(d, names):
    rows, offs, r0 = [], {}, 0
    for n in names:
        flat = d[n].reshape(-1)
        nr = -(-flat.shape[0] // LANES)
        rows.append(jnp.pad(flat, (0, nr * LANES - flat.shape[0])).reshape(nr, LANES))
        offs[n] = (r0, nr)
        r0 += nr
    return jnp.concatenate(rows, axis=0), offs


def _unpack_small(packed, offs, name, like):
    r0, nr = offs[name]
    return packed[r0:r0 + nr].reshape(-1)[:like.size].reshape(like.shape)


def kernel(x, c, w_mod, b_mod, norm1_w, norm2_w, hg_w_in, hg_w_out, hg_lb, hg_gn_w, fox_w_in, fox_b_f, fox_qn_w, fox_kn_w, fox_w_out, mlp_w1, mlp_w2, final_w, loss_target, m_w_mod, m_b_mod, m_norm1_w, m_norm2_w, m_hg_w_in, m_hg_w_out, m_hg_lb, m_hg_gn_w, m_fox_w_in, m_fox_b_f, m_fox_qn_w, m_fox_kn_w, m_fox_w_out, m_mlp_w1, m_mlp_w2, m_final_w, v_w_mod, v_b_mod, v_norm1_w, v_norm2_w, v_hg_w_in, v_hg_w_out, v_hg_lb, v_hg_gn_w, v_fox_w_in, v_fox_b_f, v_fox_qn_w, v_fox_kn_w, v_fox_w_out, v_mlp_w1, v_mlp_w2, v_final_w):
    S, D = x.shape[1], x.shape[2]
    nheads = D // FOX_DH
    ax, ay, ac = _mesh_pos()
    chip = 2 * ax + ay
    dev = 2 * chip + ac
    xs, tgt = x.reshape(S, D), loss_target.reshape(S, D)

    c_all = _allgather_small(_pad_rows(c.reshape(-1, LANES), 8), name="gather_c")
    c_all = c_all.reshape(N_DEV, -1)[:, :D]
    c16 = _pad_rows(c_all, 16)
    nmod = w_mod.shape[2]
    b_shard = lax.dynamic_slice_in_dim(b_mod, chip * nmod, nmod, axis=1)
    mod_shard = _mod_fwd(c16, w_mod, b_shard[:, None, :], name="mod_fwd")[:, :N_DEV]
    mod_all = _allgather_small(mod_shard.reshape(-1, LANES), name="gather_mod")
    mod_all = mod_all.reshape(N_CHIPS, 2, 2, N_DEV, nmod)[:, 0]
    mod = lax.dynamic_index_in_dim(mod_all, dev, axis=2, keepdims=False)
    mod = mod.transpose(1, 0, 2).reshape(2, N_CHIPS * nmod)

    fox_rows = fox_w_in.shape[2]
    col = lambda g: g.transpose(1, 0, 2).reshape(g.shape[1], -1)
    rowsh = lambda g: g.reshape(-1, g.shape[2])
    own = lambda g, s: lax.dynamic_update_index_in_dim(g, s, chip, 0)

    slab_in = hg_w_in[0].astype(BF)
    wts = {"hg_w_in": col(own(_allgather_chip_slabs(slab_in, name="gather_hg_w_in"), slab_in))}
    slab_rest, place_rest, rh_rest = _pack_halves(
        [[("mlp_w1", mlp_w1.reshape(2 * D, D)), ("hg_w_out", hg_w_out[0]), ("fox_w_out", fox_w_out[0])],
         [("mlp_w2", mlp_w2.reshape(2 * D, D)), ("fox_w_in", fox_w_in[0].reshape(fox_rows, D))]])

    def unpack_rest(gathered):
        gathered = own(gathered, slab_rest)

        def seg(n):
            h, off, rows = place_rest[n]
            return gathered[:, h * rh_rest + off:h * rh_rest + off + rows, :]

        w1 = seg("mlp_w1").reshape(N_CHIPS, 2, D, D)
        w2 = seg("mlp_w2").reshape(N_CHIPS, 2, D, D)
        fox_in = col(seg("fox_w_in").reshape(N_CHIPS, D, fox_rows))
        return {
            "hg_w_out": rowsh(seg("hg_w_out")), "fox_w_out": rowsh(seg("fox_w_out")),
            "mlp_w1_0": col(w1[:, 0]), "mlp_w1_1": col(w1[:, 1]), "mlp_w2_0": rowsh(w2[:, 0]), "mlp_w2_1": rowsh(w2[:, 1]),
            "fox_w_in": jnp.pad(fox_in, ((0, 0), (0, 5 * D - fox_in.shape[1]))),
        }

    small = {"norm1_w": norm1_w, "norm2_w": norm2_w, "hg_lb": hg_lb, "hg_gn_w": hg_gn_w, "fox_b_f": fox_b_f,
             "fox_qn_w": fox_qn_w, "fox_kn_w": fox_kn_w, "final_w": final_w}

    def uncol(g, n):
        return g.reshape(g.shape[0], N_CHIPS, n).transpose(1, 0, 2)

    pos = jnp.stack([chip, ac])

    def swap_and_add(halves, tag):
        rh = -(-max(h.shape[1] for h in halves) // 16) * 16
        g4 = jnp.stack([jnp.pad(h, ((0, 0), (0, rh - h.shape[1]), (0, 0))) for h in halves], axis=1)
        to_sibling = lax.dynamic_index_in_dim(g4, 1 - ac, axis=1, keepdims=False).astype(BF)
        from_sibling = _swap_halves(to_sibling, name=f"rs_swap_{tag}")
        return g4, from_sibling, _add_halves(g4, from_sibling, ac.reshape(1), name=f"rs_add_halves_{tag}")

    def finish(g4, from_sibling, from_chips, tag):
        my_half = _add_four(g4, from_sibling, from_chips, pos, name=f"rs_add_chips_{tag}")
        return _join_halves(my_half, name=f"rs_join_{tag}")

    layout = [[("mlp_w1", 2 * D), ("hg_w_out", D // 4), ("fox_w_out", D // 4)], [("mlp_w2", 2 * D), ("fox_w_in", fox_rows)]]
    place = {}
    for h, half in enumerate(layout):
        off = 0
        for n, rows in half:
            place[n] = (h, off, rows)
            off += rows

    def reduce_early(gw):
        gseg = {
            "hg_w_out": gw["hg_w_out"].reshape(N_CHIPS, D // 4, D), "fox_w_out": gw["fox_w_out"].reshape(N_CHIPS, D // 4, D),
            "mlp_w1": jnp.concatenate([uncol(gw["mlp_w1_0"], D), uncol(gw["mlp_w1_1"], D)], axis=1),
            "mlp_w2": jnp.concatenate([gw["mlp_w2_0"].reshape(N_CHIPS, D, D), gw["mlp_w2_1"].reshape(N_CHIPS, D, D)], axis=1),
            "fox_w_in": uncol(gw["fox_w_in"][:, :4 * fox_rows], fox_rows).reshape(N_CHIPS, fox_rows, D),
        }
        g4, from_sibling, part = swap_and_add([jnp.concatenate([gseg[n] for n, _ in half], axis=1) for half in layout], "early")
        return part, (g4, from_sibling)

    loss_part, grad_x, gw, gs, ((g4, from_sibling), from_chips) = _local_step(
        xs, tgt, mod, wts, small, slab_rest, unpack_rest, reduce_early)
    loss = lax.psum(loss_part[0, 0], ("x", "y", "c"))
    gshard = finish(g4, from_sibling, from_chips, "early")

    g_in = uncol(gw["hg_w_in"], D)
    g4, from_sibling, part = swap_and_add([g_in[:, :D // 2], g_in[:, D // 2:]], "late")
    g_hg_w_in = finish(g4, from_sibling, _scatter_partials(part, name="rs_scatter_late"), "late").reshape(D, D)

    names = ["dmod"] + SMALL_NAMES
    packed, offs = _pack_small(gs, names)
    packed = _pad_rows(packed, 8)
    rp = packed.shape[0]
    parts = _allgather_small(packed, name="gather_small").reshape(N_DEV, rp, LANES)
    total = _sum_parts(parts, name="sum_small")
    r0, nr = offs["dmod"]
    dmod_all = parts[:, r0:r0 + nr].reshape(N_DEV, 2, N_CHIPS * nmod)
    dmod_shard = lax.dynamic_slice_in_dim(dmod_all, chip * nmod, nmod, axis=2).transpose(1, 0, 2)
    g_w_mod = _mod_bwd(c16, jnp.pad(dmod_shard, ((0, 0), (0, 16 - N_DEV), (0, 0))), name="mod_bwd")

    grads = {"w_mod": g_w_mod, "b_mod": _unpack_small(total, offs, "dmod", b_mod)}
    for n in SMALL_NAMES:
        grads[n] = _unpack_small(total, offs, n, small[n])

    given = dict(w_mod=(w_mod, m_w_mod, v_w_mod), b_mod=(b_mod, m_b_mod, v_b_mod), norm1_w=(norm1_w, m_norm1_w, v_norm1_w),
                 norm2_w=(norm2_w, m_norm2_w, v_norm2_w), hg_w_in=(hg_w_in, m_hg_w_in, v_hg_w_in),
                 hg_w_out=(hg_w_out, m_hg_w_out, v_hg_w_out), hg_lb=(hg_lb, m_hg_lb, v_hg_lb),
                 hg_gn_w=(hg_gn_w, m_hg_gn_w, v_hg_gn_w), fox_w_in=(fox_w_in, m_fox_w_in, v_fox_w_in),
                 fox_b_f=(fox_b_f, m_fox_b_f, v_fox_b_f), fox_qn_w=(fox_qn_w, m_fox_qn_w, v_fox_qn_w),
                 fox_kn_w=(fox_kn_w, m_fox_kn_w, v_fox_kn_w), fox_w_out=(fox_w_out, m_fox_w_out, v_fox_w_out),
                 mlp_w1=(mlp_w1, m_mlp_w1, v_mlp_w1), mlp_w2=(mlp_w2, m_mlp_w2, v_mlp_w2), final_w=(final_w, m_final_w, v_final_w))
    upd = {}

    for n, (h, off, rows) in place.items():
        w, m, v = given[n]
        flat = lambda a: a.reshape(rows, D)
        d, mn, vn = _adamw(flat(w), gshard, flat(m), flat(v), g_at=(h, off), name=f"adamw_{n}")
        grads[n] = gshard[h, off:off + rows].reshape(w.shape)
        upd[n] = tuple(a.reshape(w.shape) for a in (d, mn, vn))

    w, m, v = given["hg_w_in"]
    grads["hg_w_in"] = g_hg_w_in.reshape(w.shape)
    upd["hg_w_in"] = tuple(a.reshape(w.shape) for a in _adamw(w[0], g_hg_w_in, m[0], v[0], name="adamw_hg_w_in"))

    w, m, v = given["w_mod"]
    flat = lambda a: a.reshape(-1, nmod)
    upd["w_mod"] = tuple(a.reshape(w.shape) for a in _adamw(flat(w), flat(g_w_mod), flat(m), flat(v), name="adamw_w_mod"))

    snames = ["b_mod"] + SMALL_NAMES
    pw, soffs = _pack_small({n: given[n][0] for n in snames}, snames)
    pm, _ = _pack_small({n: given[n][1] for n in snames}, snames)
    pv, _ = _pack_small({n: given[n][2] for n in snames}, snames)
    pg, _ = _pack_small({n: grads[n] for n in snames}, snames)
    pw, pm, pv, pg = (_pad_rows(a, 8) for a in (pw, pm, pv, pg))
    sd, smn, svn = _adamw(pw, pg, pm, pv, name="adamw_small")
    for n in snames:
        like = given[n][0]
        upd[n] = tuple(_unpack_small(a, soffs, n, like) for a in (sd, smn, svn))

    order = ["w_mod", "b_mod", "norm1_w", "norm2_w", "hg_w_in", "hg_w_out", "hg_lb", "hg_gn_w", "fox_w_in", "fox_b_f",
             "fox_qn_w", "fox_kn_w", "fox_w_out", "mlp_w1", "mlp_w2", "final_w"]
    return (loss, grad_x.reshape(x.shape), *[grads[n] for n in order], *[upd[n][0] for n in order],
            *[upd[n][1] for n in order], *[upd[n][2] for n in order])
```

```python
import math

import jax
import jax.numpy as jnp
from jax import lax
from jax.experimental import pallas as pl
from jax.experimental.pallas import tpu as pltpu

EPS = 1e-6
ADAM_LR, ADAM_B1, ADAM_B2, ADAM_EPS, ADAM_WD, ADAM_STEP = 0.001, 0.9, 0.999, 1e-08, 0.01, 10

F32 = jnp.float32
BF = jnp.bfloat16
LANES = 128
HG_CHUNK = 64
HG_HEADS_PER_STEP = 8
HG_TOKENS_PER_STEP = 256
FOX_BWD_TILES = (8, 4, 2, 1)
LOG2E = 1.4426950408889634
FOX_DH = 64
N_CHIPS = 4
N_DEV = 8
VMEM_LIMIT = 56 * 1024 * 1024
MESH = pl.DeviceIdType.MESH

NT = (((1,), (1,)), ((), ()))
TN = (((0,), (0,)), ((), ()))


def _pick(n, pref, mult=LANES):
    if n <= pref:
        return n
    t = (pref // mult) * mult
    while t >= mult:
        if n % t == 0:
            return t
        t -= mult
    raise ValueError((n, pref, mult))


def _cp(*sem):
    return pltpu.CompilerParams(dimension_semantics=sem, vmem_limit_bytes=VMEM_LIMIT)


def _dot(a, b):
    return jnp.dot(a, b, preferred_element_type=F32)


def _dg(a, b, dims):
    return lax.dot_general(a, b, dims, preferred_element_type=F32)


def _split3(x):
    hi = x.astype(BF)
    r1 = x - hi.astype(F32)
    mid = r1.astype(BF)
    lo = (r1 - mid.astype(F32)).astype(BF)
    return hi, mid, lo


def _tri_dot(tri, x):
    hi, mid, lo = _split3(x)
    return _dot(tri, hi) + _dot(tri, mid) + _dot(tri, lo)


def _dg3(a, b, dims):
    ah, bh = a.astype(BF), b.astype(BF)
    al, bl = (a - ah.astype(F32)).astype(BF), (b - bh.astype(F32)).astype(BF)
    return _dg(ah, bh, dims) + _dg(ah, bl, dims) + _dg(al, bh, dims)


def _dg1(a, b, dims):
    return _dg(a.astype(BF), b.astype(BF), dims)


NN = (((1,), (0,)), ((), ()))


def _sigmoid(x):
    return jax.nn.sigmoid(x)


def _ln_matmul(x, nw, sc, sh, w, *, relu2, name):
    S, D = x.shape
    N = w.shape[1]
    tm, tn = _pick(S, 512, 16), N

    def body(x_ref, nw_ref, sc_ref, sh_ref, w_ref, *rest):
        outs, hs = rest[:-1], rest[-1]
        h_ref = outs[-1]

        @pl.when(pl.program_id(1) == 0)
        def _():
            xv = x_ref[...]
            r = lax.rsqrt(jnp.mean(xv * xv, axis=-1, keepdims=True) + EPS)
            hb = ((xv * r * nw_ref[...]) * (1.0 + sc_ref[...]) + sh_ref[...]).astype(BF)
            hs[...] = hb
            h_ref[...] = hb

        z = _dot(hs[...], w_ref[...])
        if relu2:
            a = jnp.maximum(z, 0.0)
            outs[0][...] = a.astype(BF)
            outs[1][...] = (a * a).astype(BF)
        else:
            outs[0][...] = z

    vec = pl.BlockSpec((1, D), lambda i, j: (0, 0))
    tile = pl.BlockSpec((tm, tn), lambda i, j: (i, j))
    if relu2:
        out_shape = [jax.ShapeDtypeStruct((S, N), BF), jax.ShapeDtypeStruct((S, N), BF)]
        out_specs = [tile, tile]
    else:
        out_shape = [jax.ShapeDtypeStruct((S, N), F32)]
        out_specs = [tile]
    out_shape.append(jax.ShapeDtypeStruct((S, D), BF))
    out_specs.append(pl.BlockSpec((tm, D), lambda i, j: (i, 0)))
    return pl.pallas_call(
        body, name=name, grid=(S // tm, N // tn),
        in_specs=[pl.BlockSpec((tm, D), lambda i, j: (i, 0)), vec, vec, vec,
                  pl.BlockSpec((D, tn), lambda i, j: (0, j))],
        out_specs=out_specs, out_shape=out_shape,
        scratch_shapes=[pltpu.VMEM((tm, D), BF)],
        compiler_params=_cp("parallel", "arbitrary"),
    )(x, nw, sc, sh, w)


def _matmul_resid(a, w, x, gate, *, name):
    S, K = a.shape
    D = w.shape[1]
    tm, tn = _pick(S, 1024 if K <= 1024 else 512, 16), D

    def body(a_ref, w_ref, x_ref, g_ref, o_ref, y_ref):
        y = _dot(a_ref[...], w_ref[...])
        y_ref[...] = y.astype(BF)
        o_ref[...] = x_ref[...] + g_ref[...] * y

    tile = pl.BlockSpec((tm, tn), lambda i, j: (i, j))
    return pl.pallas_call(
        body, name=name, grid=(S // tm, D // tn),
        in_specs=[pl.BlockSpec((tm, K), lambda i, j: (i, 0)), pl.BlockSpec((K, tn), lambda i, j: (0, j)),
                  tile, pl.BlockSpec((1, tn), lambda i, j: (0, j))],
        out_specs=[tile, tile],
        out_shape=[jax.ShapeDtypeStruct((S, D), F32), jax.ShapeDtypeStruct((S, D), BF)],
        compiler_params=_cp("parallel", "arbitrary"),
    )(a, w, x, gate)


def _gate_matmul_nt(dx, gate, y, w, act, *, name):
    S, D = dx.shape
    K = w.shape[0]
    tm, tn = _pick(S, 1024 if K <= 1024 else 512, 16), K
    fused = act is not None

    def body(dx_ref, g_ref, y_ref, w_ref, *rest):
        if fused:
            act_ref, da_ref, dm_ref, dg_ref, ms = rest
        else:
            da_ref, dm_ref, dg_ref, ms = rest
        i, j = pl.program_id(0), pl.program_id(1)

        @pl.when((i == 0) & (j == 0))
        def _():
            dg_ref[...] = jnp.zeros_like(dg_ref)

        @pl.when(j == 0)
        def _():
            dxv = dx_ref[...]
            dmb = (dxv * g_ref[...]).astype(BF)
            ms[...] = dmb
            dm_ref[...] = dmb
            dg_ref[...] += jnp.sum(dxv * y_ref[...].astype(F32), axis=0, keepdims=True)

        da = _dg(ms[...], w_ref[...], NT)
        if fused:
            da_ref[...] = (da * (2.0 * act_ref[...].astype(F32))).astype(BF)
        else:
            da_ref[...] = da

    row = pl.BlockSpec((tm, D), lambda i, j: (i, 0))
    vec = pl.BlockSpec((1, D), lambda i, j: (0, 0))
    tile = pl.BlockSpec((tm, tn), lambda i, j: (i, j))
    in_specs = [row, vec, row, pl.BlockSpec((tn, D), lambda i, j: (j, 0))]
    args = [dx, gate, y, w]
    if fused:
        in_specs.append(tile)
        args.append(act)
    return pl.pallas_call(
        body, name=name, grid=(S // tm, K // tn),
        in_specs=in_specs, out_specs=[tile, row, vec],
        out_shape=[jax.ShapeDtypeStruct((S, K), BF if fused else F32), jax.ShapeDtypeStruct((S, D), BF),
                   jax.ShapeDtypeStruct((1, D), F32)],
        scratch_shapes=[pltpu.VMEM((tm, D), BF)],
        compiler_params=_cp("arbitrary", "arbitrary"),
    )(*args)


def _matmul_tn(a, b, *, name, into=None):
    S, Ka = a.shape
    P, _, Db = b.shape
    tk, tn, ts = _pick(Ka, 1024), _pick(Db, 1024), _pick(S, 1024, 16)
    if into is not None:
        slab, kind, half, off = into
        C = tn = slab.shape[3]
        if kind == "row":
            tk = min(tk, Ka // N_CHIPS)
        assert tn == C and P * Db == (N_CHIPS * C if kind == "col" else C) and off % tk == 0
        assert tk == Ka if kind == "col" else (Ka // N_CHIPS) % tk == 0
    npb = Db // tn

    def body(a_ref, b_ref, *rest):
        o_ref, acc = rest[-2:]
        s = pl.program_id(2)

        @pl.when(s == 0)
        def _():
            acc[...] = jnp.zeros_like(acc)

        acc[...] += _dg(a_ref[...], b_ref[...], TN)

        @pl.when(s == pl.num_programs(2) - 1)
        def _():
            o_ref[...] = acc[...]

    in_specs = [pl.BlockSpec((ts, tk), lambda i, j, s: (s, i)),
                pl.BlockSpec((None, ts, tn), lambda i, j, s: (j // npb, s, j % npb))]
    args = [a, b]
    if into is None:
        out_spec = pl.BlockSpec((tk, tn), lambda i, j, s: (i, j))
        out_shape = jax.ShapeDtypeStruct((Ka, P * Db), F32)
        aliases = {}
    else:
        per = (Ka // N_CHIPS) // tk if kind == "row" else 1
        if kind == "col":
            out_spec = pl.BlockSpec((None, None, tk, tn), lambda i, j, s: (j, half, off // tk + i, 0))
        else:
            out_spec = pl.BlockSpec((None, None, tk, tn), lambda i, j, s: (i // per, half, off // tk + i % per, 0))
        out_shape = jax.ShapeDtypeStruct(slab.shape, F32)
        in_specs.append(pl.BlockSpec(memory_space=pl.ANY))
        args.append(slab)
        aliases = {2: 0}
    return pl.pallas_call(
        body, name=name, grid=(Ka // tk, P * npb, S // ts),
        in_specs=in_specs, out_specs=out_spec, out_shape=out_shape,
        scratch_shapes=[pltpu.VMEM((tk, tn), F32)], input_output_aliases=aliases,
        compiler_params=_cp("parallel", "parallel", "arbitrary"),
    )(*args)


def _matmul_nt_lnbwd(g, w, x, nw, sc, dx_out, *, name):
    P, S, Dg = g.shape
    D = x.shape[1]
    tm = _pick(S, 512, 16)

    def body(g_ref, w_ref, x_ref, nw_ref, sc_ref, dxo_ref, dx_ref, dsc_ref, dsh_ref, dnw_ref):
        @pl.when(pl.program_id(0) == 0)
        def _():
            dsc_ref[...] = jnp.zeros_like(dsc_ref)
            dsh_ref[...] = jnp.zeros_like(dsh_ref)
            dnw_ref[...] = jnp.zeros_like(dnw_ref)

        dh = _dg(g_ref[0], w_ref[:, 0:Dg], NT)
        for p in range(1, P):
            dh = dh + _dg(g_ref[p], w_ref[:, p * Dg:(p + 1) * Dg], NT)
        xv = x_ref[...]
        nwv = nw_ref[...]
        r = lax.rsqrt(jnp.mean(xv * xv, axis=-1, keepdims=True) + EPS)
        xr = xv * r
        dn = dh * (1.0 + sc_ref[...])
        dsc_ref[...] += jnp.sum(dh * (xr * nwv), axis=0, keepdims=True)
        dsh_ref[...] += jnp.sum(dh, axis=0, keepdims=True)
        dnw_ref[...] += jnp.sum(dn * xr, axis=0, keepdims=True)
        u = dn * nwv
        dx_ref[...] = dxo_ref[...] + r * (u - xr * jnp.mean(u * xr, axis=-1, keepdims=True))

    row = pl.BlockSpec((tm, D), lambda i: (i, 0))
    vec = pl.BlockSpec((1, D), lambda i: (0, 0))
    return pl.pallas_call(
        body, name=name, grid=(S // tm,),
        in_specs=[pl.BlockSpec((P, tm, Dg), lambda i: (0, i, 0)),
                  pl.BlockSpec((D, P * Dg), lambda i: (0, 0)), row, vec, vec, row],
        out_specs=[row, vec, vec, vec],
        out_shape=[jax.ShapeDtypeStruct((S, D), F32)] + [jax.ShapeDtypeStruct((1, D), F32)] * 3,
        compiler_params=_cp("arbitrary"),
    )(g, w, x, nw, sc, dx_out)


def _loss_kernel(x, fw, tgt, *, name):
    S, D = x.shape
    tm = _pick(S, 512, 8)

    def body(x_ref, fw_ref, t_ref, l_ref, dx_ref, dfw_ref):
        @pl.when(pl.program_id(0) == 0)
        def _():
            l_ref[...] = jnp.zeros_like(l_ref)
            dfw_ref[...] = jnp.zeros_like(dfw_ref)

        xv = x_ref[...]
        fwv = fw_ref[...]
        r = lax.rsqrt(jnp.mean(xv * xv, axis=-1, keepdims=True) + EPS)
        xr = xv * r
        err = xr * fwv - t_ref[...]
        per_tok = jnp.mean(err * err, axis=-1, keepdims=True)
        l_ref[...] += 0.5 * jnp.sum(per_tok, axis=0, keepdims=True)
        dy = err * (1.0 / D)
        dfw_ref[...] += jnp.sum(dy * xr, axis=0, keepdims=True)
        u = dy * fwv
        dx_ref[...] = r * (u - xr * jnp.mean(u * xr, axis=-1, keepdims=True))

    row = pl.BlockSpec((tm, D), lambda i: (i, 0))
    vec = pl.BlockSpec((1, D), lambda i: (0, 0))
    return pl.pallas_call(
        body, name=name, grid=(S // tm,),
        in_specs=[row, vec, row],
        out_specs=[pl.BlockSpec((1, LANES), lambda i: (0, 0)), row, vec],
        out_shape=[jax.ShapeDtypeStruct((1, LANES), F32), jax.ShapeDtypeStruct((S, D), F32),
                   jax.ShapeDtypeStruct((1, D), F32)],
        compiler_params=_cp("arbitrary"),
    )(x, fw, tgt)


def _hg_lower_bound(lb3):
    mx = jnp.max(lb3, axis=0, keepdims=True)
    e = jnp.exp(lb3 - mx)
    p = e / jnp.sum(e, axis=0, keepdims=True)
    return p[0:1, :], p


def _hg_chunk_common(qr, fz, lbv):
    sq = _sigmoid(qr)
    q = qr * sq
    sig = _sigmoid(fz)
    f = lbv + (1.0 - lbv) * sig
    k = (1.0 - lbv) * (1.0 - sig)
    return q, sq, sig, f, k, jnp.log(f)


def _row_of(x, rows, r):
    return jnp.sum(jnp.where(rows == r, x, 0.0), axis=0, keepdims=True)


def _hg_fwd(proj, hg_lb, gn, slab=None, *, name):
    S = proj.shape[0]
    D = proj.shape[1] // 4
    H = D // LANES
    HB = min(HG_HEADS_PER_STEP, H)
    W = HB * LANES
    C = HG_CHUNK
    T = _pick(S, HG_TOKENS_PER_STEP, C)
    nch, nb = T // C, S // T
    ng = H // HB
    fused = slab is not None

    def body(q_ref, fz_ref, v_ref, g_ref, lb_ref, gn_ref, *rest):
        if fused:
            s_ref, y_ref, o_ref, sts_ref, out_ref, st, send_sems, recv_sems = rest
            first, passed, landed, from_sibling = _chip_slab_copies(s_ref, out_ref, send_sems, recv_sems)
            hgrp, n = pl.program_id(0), pl.program_id(1)

            @pl.when((hgrp == 0) & (n == 0))
            def _():
                for cp in first:
                    cp.start()

            @pl.when((hgrp == ng - 1) & (n == (3 * nb) // 4))
            def _():
                for arrived, onward in zip(landed, passed):
                    arrived.wait_recv()
                    onward.start()
        else:
            y_ref, o_ref, sts_ref, st = rest

        @pl.when(pl.program_id(1) == 0)
        def _():
            st[...] = jnp.zeros_like(st)

        lb_all, _ = _hg_lower_bound(lb_ref[...])
        gnv = gn_ref[...]
        ri = lax.broadcasted_iota(jnp.int32, (C, C), 0)
        ci_ = lax.broadcasted_iota(jnp.int32, (C, C), 1)
        low = ri >= ci_
        tri = jnp.where(low, 1.0, 0.0).astype(BF)
        rows = lax.broadcasted_iota(jnp.int32, (C, LANES), 0)

        def chunk(ci, carry):
            sl = pl.ds(pl.multiple_of(ci * C, C), C)
            for hh in range(HB):
                ls = slice(hh * LANES, (hh + 1) * LANES)
                q, _, _, _, k, logf = _hg_chunk_common(q_ref[sl, ls], fz_ref[sl, ls], lb_all[:, ls])
                vv = v_ref[sl, ls]
                gg = g_ref[sl, ls]
                G = _tri_dot(tri, logf)
                Gm = _row_of(G, rows, C // 2 - 1)
                Gl = _row_of(G, rows, C - 1)
                qt = q * jnp.exp(G - Gm)
                kt = k * jnp.exp(Gm - G)
                A = jnp.where(low, _dg1(qt, kt, NT), 0.0)
                Sv = st[hh]
                sts_ref[hh, ci] = Sv
                o = _dg1(A, vv, NN) + _dg1(q * jnp.exp(G), Sv, NT)
                st[hh] = Sv * jnp.exp(Gl) + _dg1(vv, k * jnp.exp(Gl - G), TN)
                r = lax.rsqrt(jnp.mean(o * o, axis=-1, keepdims=True) + EPS)
                y_ref[sl, ls] = ((o * r * gnv) * (gg * _sigmoid(gg))).astype(BF)
                o_ref[sl, ls] = o
            return carry

        lax.fori_loop(0, nch, chunk, 0)

        if fused:
            @pl.when((hgrp == ng - 1) & (n == nb - 1))
            def _():
                for cp in from_sibling:
                    cp.wait_recv()
                for cp in first + passed:
                    cp.wait_send()

    def part(p):
        return pl.BlockSpec((T, W), lambda h, n: (n, p * ng + h))

    blk = pl.BlockSpec((T, W), lambda h, n: (n, h))
    in_specs = [part(0), part(1), part(2), part(3),
                pl.BlockSpec((3, W), lambda h, n: (0, h)), pl.BlockSpec((1, LANES), lambda h, n: (0, 0))]
    out_specs = [blk, blk, pl.BlockSpec((HB, nch, LANES, LANES), lambda h, n: (h, n, 0, 0))]
    out_shape = [jax.ShapeDtypeStruct((S, D), BF), jax.ShapeDtypeStruct((S, D), F32),
                 jax.ShapeDtypeStruct((H, S // C, LANES, LANES), F32)]
    scratch = [pltpu.VMEM((HB, LANES, LANES), F32)]
    args = [proj, proj, proj, proj, hg_lb, gn]
    if fused:
        in_specs.append(HBM)
        out_specs.append(HBM)
        out_shape.append(jax.ShapeDtypeStruct((N_CHIPS,) + slab.shape, slab.dtype))
        scratch += [pltpu.SemaphoreType.DMA((6,)), pltpu.SemaphoreType.DMA((6,))]
        args.append(slab)
    return pl.pallas_call(
        body, name=name, grid=(ng, nb), in_specs=in_specs, out_specs=out_specs, out_shape=out_shape,
        scratch_shapes=scratch, compiler_params=_cp("arbitrary", "arbitrary"),
    )(*args)


def _hg_bwd(proj, hg_lb, gn, o_all, states, dy, part=None, *, name):
    S = proj.shape[0]
    D = proj.shape[1] // 4
    H = D // LANES
    HB = min(HG_HEADS_PER_STEP, H)
    W = HB * LANES
    C = HG_CHUNK
    T = _pick(S, HG_TOKENS_PER_STEP, C)
    nch, nb = T // C, S // T
    ng = H // HB
    fused = part is not None

    def body(q_ref, fz_ref, v_ref, g_ref, lb_ref, gn_ref, o_ref, sts_ref, dy_ref, *rest):
        if fused:
            p_ref, dp_ref, dlb_ref, dgn_ref, recv_ref, dst, dlb_acc, send_sems, recv_sems = rest
            copies = _scatter_copies(p_ref, recv_ref, send_sems, recv_sems)

            @pl.when((pl.program_id(0) == 0) & (pl.program_id(1) == 0))
            def _():
                for cp in copies:
                    cp.start()
        else:
            dp_ref, dlb_ref, dgn_ref, dst, dlb_acc = rest
        n = pl.program_id(1)

        @pl.when(n == 0)
        def _():
            dst[...] = jnp.zeros_like(dst)
            dlb_acc[...] = jnp.zeros_like(dlb_acc)
            dgn_ref[...] = jnp.zeros_like(dgn_ref)

        lb_all, p3 = _hg_lower_bound(lb_ref[...])
        gnv = gn_ref[...]
        ri = lax.broadcasted_iota(jnp.int32, (C, C), 0)
        ci_ = lax.broadcasted_iota(jnp.int32, (C, C), 1)
        low = ri >= ci_
        tri = jnp.where(low, 1.0, 0.0).astype(BF)
        triu = jnp.where(ri <= ci_, 1.0, 0.0).astype(BF)
        rows = lax.broadcasted_iota(jnp.int32, (C, LANES), 0)

        def chunk(cj, carry):
            ci = nch - 1 - cj
            sl = pl.ds(pl.multiple_of(ci * C, C), C)
            for hh in range(HB):
                ls = slice(hh * LANES, (hh + 1) * LANES)
                lbv = lb_all[:, ls]
                qr = q_ref[sl, ls]
                q, sq, sig, f, k, logf = _hg_chunk_common(qr, fz_ref[sl, ls], lbv)
                vv = v_ref[sl, ls]
                gg = g_ref[sl, ls]
                o = o_ref[sl, ls]
                dyv = dy_ref[sl, ls]
                G = _tri_dot(tri, logf)
                Gm = _row_of(G, rows, C // 2 - 1)
                Gl = _row_of(G, rows, C - 1)
                eG, e_qm, e_km, e_lk, eGl = jnp.exp(G), jnp.exp(G - Gm), jnp.exp(Gm - G), jnp.exp(Gl - G), jnp.exp(Gl)
                qt = q * e_qm
                kt = k * e_km
                A = jnp.where(low, _dg1(qt, kt, NT), 0.0)
                sg = _sigmoid(gg)
                r = lax.rsqrt(jnp.mean(o * o, axis=-1, keepdims=True) + EPS)
                on = o * r
                d_onw = dyv * (gg * sg)
                dgn_ref[hh] += jnp.sum(d_onw * on, axis=0, keepdims=True)
                dgg = dyv * (on * gnv) * (sg * (1.0 + gg * (1.0 - sg)))
                u = d_onw * gnv
                do = r * (u - on * jnp.mean(u * on, axis=-1, keepdims=True))
                Sv = sts_ref[hh, ci]
                dSv = dst[hh]
                dA = jnp.where(low, _dg3(do, vv, NT), 0.0)
                kdec = k * e_lk
                dv = _dg1(A, do, TN) + _dg1(kdec, dSv, NT)
                dq = _dg3(dA, kt, NN) * e_qm + eG * _dg3(do, Sv, NN)
                dk = _dg3(dA, qt, TN) * e_km + e_lk * _dg3(vv, dSv, NN)
                s_end = Sv * eGl + _dg3(vv, kdec, TN)
                dgl = jnp.sum(dSv * s_end, axis=0, keepdims=True)
                dG = q * dq - k * dk + jnp.where(rows == C - 1, dgl, 0.0)
                dlogf = _tri_dot(triu, dG) - f * dk
                dst[hh] = dSv * eGl + _dg1(do, q * eG, TN)
                dlf_f = dlogf / f
                dlb_acc[:, ls] += jnp.sum(dlf_f * (1.0 - sig), axis=0, keepdims=True)
                dp_ref[0, sl, ls] = (dq * (sq * (1.0 + qr * (1.0 - sq)))).astype(BF)
                dp_ref[1, sl, ls] = (dlf_f * (1.0 - lbv) * sig * (1.0 - sig)).astype(BF)
                dp_ref[2, sl, ls] = dv.astype(BF)
                dp_ref[3, sl, ls] = dgg.astype(BF)
            return carry

        lax.fori_loop(0, nch, chunk, 0)
        sel = jnp.where(lax.broadcasted_iota(jnp.int32, (3, W), 0) == 0, 1.0, 0.0)
        dlb_ref[...] = lb_all * (sel - p3) * dlb_acc[...]

        if fused:
            @pl.when((pl.program_id(0) == ng - 1) & (n == nb - 1))
            def _():
                for cp in copies:
                    cp.wait()

    def col(p):
        return pl.BlockSpec((T, W), lambda h, n: (nb - 1 - n, p * ng + h))

    blk = pl.BlockSpec((T, W), lambda h, n: (nb - 1 - n, h))
    in_specs = [col(0), col(1), col(2), col(3),
                pl.BlockSpec((3, W), lambda h, n: (0, h)), pl.BlockSpec((1, LANES), lambda h, n: (0, 0)),
                blk, pl.BlockSpec((HB, nch, LANES, LANES), lambda h, n: (h, nb - 1 - n, 0, 0)), blk]
    out_specs = [pl.BlockSpec((4, T, W), lambda h, n: (0, nb - 1 - n, h)),
                 pl.BlockSpec((3, W), lambda h, n: (0, h)),
                 pl.BlockSpec((HB, 1, LANES), lambda h, n: (h, 0, 0))]
    out_shape = [jax.ShapeDtypeStruct((4, S, D), BF), jax.ShapeDtypeStruct((3, D), F32),
                 jax.ShapeDtypeStruct((H, 1, LANES), F32)]
    scratch = [pltpu.VMEM((HB, LANES, LANES), F32), pltpu.VMEM((1, W), F32)]
    args = [proj, proj, proj, proj, hg_lb, gn, o_all, states, dy]
    if fused:
        in_specs.append(HBM)
        out_specs.append(HBM)
        out_shape.append(jax.ShapeDtypeStruct((3,) + part.shape[1:], part.dtype))
        scratch += [pltpu.SemaphoreType.DMA((3,)), pltpu.SemaphoreType.DMA((3,))]
        args.append(part)
    return pl.pallas_call(
        body, name=name, grid=(ng, nb), in_specs=in_specs, out_specs=out_specs, out_shape=out_shape,
        scratch_shapes=scratch, compiler_params=_cp("arbitrary", "arbitrary"),
    )(*args)


def _log_sigmoid(u):
    return jnp.minimum(u, 0.0) - jnp.log(1.0 + jnp.exp(-jnp.abs(u)))


def _lane_put(base, lane, first, pieces):
    for n, p in enumerate(pieces):
        base = jnp.where(lane == first + n, p, base)
    return base


def _fox_cumsum(proj, bf_pad, *, name):
    S = proj.shape[0]
    D = proj.shape[1] // 5
    T = _pick(S, 256, 8)

    def body(fz_ref, b_ref, f_ref, carry):
        @pl.when(pl.program_id(0) == 0)
        def _():
            carry[...] = jnp.zeros_like(carry)

        logf = _log_sigmoid(fz_ref[...] + b_ref[...])
        tri = jnp.where(lax.broadcasted_iota(jnp.int32, (T, T), 0) >= lax.broadcasted_iota(jnp.int32, (T, T), 1),
                        1.0, 0.0).astype(BF)
        fv = _tri_dot(tri, logf) + carry[...]
        f_ref[...] = fv
        carry[...] = _row_of(fv, lax.broadcasted_iota(jnp.int32, (T, LANES), 0), T - 1)

    return pl.pallas_call(
        body, name=name, grid=(S // T,),
        in_specs=[pl.BlockSpec((T, LANES), lambda i: (i, 4 * D // LANES)), pl.BlockSpec((1, LANES), lambda i: (0, 0))],
        out_specs=pl.BlockSpec((T, LANES), lambda i: (i, 0)),
        out_shape=jax.ShapeDtypeStruct((S, LANES), F32),
        scratch_shapes=[pltpu.VMEM((1, LANES), F32)],
        compiler_params=_cp("arbitrary"),
    )(proj, bf_pad)


def _pair_stats(sq, lo):
    del lo
    a = lax.broadcasted_iota(jnp.int32, (LANES, LANES), 0) < FOX_DH
    b = lax.broadcasted_iota(jnp.int32, (LANES, LANES), 1) < FOX_DH
    avg = jnp.where(a == b, 1.0 / FOX_DH, 0.0).astype(BF)
    hi, mid, low = _split3(sq)
    return _dot(hi, avg) + _dot(mid, avg) + _dot(low, avg)


def _fox_prep(proj, fcum, qw2, kw2, *, name):
    S = proj.shape[0]
    D = proj.shape[1] // 5
    HP = D // LANES
    T = _pick(S, 512, 16)

    def body(q_ref, k_ref, v_ref, f_ref, qw_ref, kw_ref, qa_ref, ka_ref, va_ref, vt_ref):
        hp = pl.program_id(1)
        lane = lax.broadcasted_iota(jnp.int32, (T, LANES), 1)
        lo = lane < FOX_DH
        qv, kv, vv, fv = q_ref[...], k_ref[...], v_ref[...], f_ref[...]
        qn = qv * lax.rsqrt(_pair_stats(qv * qv, lo) + EPS) * qw_ref[...] * (0.125 * LOG2E)
        kn = kv * lax.rsqrt(_pair_stats(kv * kv, lo) + EPS) * kw_ref[...]
        ones_q = jnp.where((lane >= 67) & (lane <= 69), 1.0, 0.0)
        ones_k = jnp.where(((lane >= 64) & (lane <= 66)) | ((lane >= 70) & (lane <= 72)), 1.0, 0.0)
        ones_v = jnp.where((lane >= 64) & (lane <= 66), 1.0, 0.0)
        for hh in range(2):
            fh = jnp.sum(jnp.where(lane == 2 * hp + hh, fv, 0.0), axis=-1, keepdims=True) * LOG2E
            pieces = [p.astype(F32) for p in _split3(fh)]

            def half(x):
                return jnp.where(lo, x if hh == 0 else pltpu.roll(x, FOX_DH, 1), 0.0)

            qa_ref[hh] = _lane_put(half(qn) + ones_q, lane, 64, pieces).astype(BF)
            ka_ref[hh] = _lane_put(half(kn) + ones_k, lane, 67, [-p for p in pieces]).astype(BF)
            va = half(vv) + ones_v
            va_ref[hh] = va.astype(BF)
            vt_ref[hh] = va.T.astype(BF)

    def part(p):
        return pl.BlockSpec((T, LANES), lambda i, hp: (i, p * HP + hp))

    vec = pl.BlockSpec((1, LANES), lambda i, hp: (0, 0))
    aug = pl.BlockSpec((2, T, LANES), lambda i, hp: (hp, i, 0))
    return pl.pallas_call(
        body, name=name, grid=(S // T, HP),
        in_specs=[part(0), part(1), part(2), pl.BlockSpec((T, LANES), lambda i, hp: (i, 0)), vec, vec],
        out_specs=[aug, aug, aug, pl.BlockSpec((2, LANES, T), lambda i, hp: (hp, 0, i))],
        out_shape=[jax.ShapeDtypeStruct((2 * HP, S, LANES), BF)] * 3 + [jax.ShapeDtypeStruct((2 * HP, LANES, S), BF)],
        compiler_params=_cp("parallel", "arbitrary"),
    )(proj, proj, proj, fcum, qw2, kw2)


def _fox_block(S):
    return _pick(S, 256, 16)


def _fox_skip_bounds(fcum, qn_w, kn_w, nheads):
    S = fcum.shape[0]
    B = _fox_block(S)
    qk = 8.0 * LOG2E * 1.02 * jnp.max(jnp.abs(qn_w)) * jnp.max(jnp.abs(kn_w))
    thresh = -(2.0 * qk + 160.0)
    f2 = fcum[:, :nheads] * LOG2E
    first, last = f2[0::B], f2[B - 1::B]
    nb = S // B
    blk = jnp.arange(nb)
    dead = (first[0::2, None, :] - last[None, :, :]) < thresh
    jmin = jnp.sum(dead & (blk[None, :, None] < 2 * jnp.arange(nb // 2)[:, None, None]), axis=1)
    live = (first[:, None, :] - last[None, :, :]) >= thresh
    imax = blk[:, None] + jnp.sum(live & (blk[:, None, None] > blk[None, :, None]), axis=0)
    return jmin.T.astype(jnp.int32), imax.T.astype(jnp.int32)


def _fox_fwd(jmin, qa, ka, vat, proj, *, name):
    H, S, _ = qa.shape
    HP = H // 2
    D = HP * LANES
    B = _fox_block(S)
    BQ = 2 * B
    nq = S // BQ

    def body(jmin_ref, q_ref, k_ref, vt_ref, g_ref, y_ref, o_ref, q2_ref):
        hp, i = pl.program_id(0), pl.program_id(1)
        lane = lax.broadcasted_iota(jnp.int32, (BQ, LANES), 1)
        lo = lane < FOX_DH
        in_stat = (lane >= 70) & (lane <= 75)
        causal = lax.broadcasted_iota(jnp.int32, (BQ, BQ), 0) <= lax.broadcasted_iota(jnp.int32, (BQ, BQ), 1)
        row = lax.broadcasted_iota(jnp.int32, (LANES, BQ), 0)
        m0, acc0 = jnp.full((1, BQ), -jnp.inf, F32), jnp.zeros((LANES, BQ), F32)
        outs = []
        for hh in range(2):
            qb = q_ref[hh]

            def block(j, carry, masked=False):
                m, acc = carry
                sl = pl.ds(pl.multiple_of(j * BQ, BQ), BQ)
                st = _dg(k_ref[hh, sl, :], qb, NT)
                if masked:
                    st = jnp.where(causal, st, -jnp.inf)
                m_new = jnp.maximum(m, jnp.ceil(jnp.max(st, axis=0, keepdims=True)))
                p = jnp.exp2(st - m_new).astype(BF)
                return m_new, acc * jnp.exp2(m - m_new) + _dot(vt_ref[hh, :, sl], p)

            carry = lax.fori_loop(jmin_ref[2 * hp + hh, i] // 2, i, block, (m0, acc0))
            m, acc = block(i, carry, masked=True)
            linv = 1.0 / jnp.sum(jnp.where(row == FOX_DH, acc, 0.0), axis=0, keepdims=True)
            tile = acc * linv
            for n, piece in enumerate(_split3(m) + _split3(linv)):
                tile = jnp.where(row == 70 + n, piece.astype(F32), tile)
            tile = tile.T
            outs.append(tile)
            q2_ref[hh] = jnp.where(in_stat, jnp.where(lane <= 72, -tile, tile), qb.astype(F32)).astype(BF)
        o = jnp.where(lo, outs[0], pltpu.roll(outs[1], FOX_DH, 1))
        o_ref[...] = o
        y_ref[...] = (o * _sigmoid(g_ref[...])).astype(BF)

    blk = pl.BlockSpec((BQ, LANES), lambda hp, i, jm: (i, hp))
    qblk = pl.BlockSpec((2, BQ, LANES), lambda hp, i, jm: (hp, i, 0))
    full = pl.BlockSpec((2, S, LANES), lambda hp, i, jm: (hp, 0, 0))
    full_t = pl.BlockSpec((2, LANES, S), lambda hp, i, jm: (hp, 0, 0))
    return pl.pallas_call(
        body, name=name,
        grid_spec=pltpu.PrefetchScalarGridSpec(
            num_scalar_prefetch=1, grid=(HP, nq),
            in_specs=[qblk, full, full_t, pl.BlockSpec((BQ, LANES), lambda hp, i, jm: (i, 3 * HP + hp))],
            out_specs=[blk, blk, qblk]),
        out_shape=[jax.ShapeDtypeStruct((S, D), BF), jax.ShapeDtypeStruct((S, D), F32),
                   jax.ShapeDtypeStruct((H, S, LANES), BF)],
        compiler_params=_cp("parallel", "arbitrary"),
    )(jmin, qa, ka, vat, proj)


def _fox_bwd_prep(dy, o, proj, q2, *, name):
    S, D = dy.shape
    HP = D // LANES
    T = _pick(S, 512, 16)

    def body(dy_ref, o_ref, g_ref, q2_ref, da_ref):
        lane = lax.broadcasted_iota(jnp.int32, (T, LANES), 1)
        lo = lane < FOX_DH
        in_linv = (lane >= 73) & (lane <= 75)
        linv = [jnp.sum(jnp.where(in_linv, q2_ref[hh].astype(F32), 0.0), axis=-1, keepdims=True) for hh in range(2)]
        u = (dy_ref[...] * _sigmoid(g_ref[...]) * jnp.where(lo, linv[0], linv[1])).astype(BF).astype(F32)
        prod = u * o_ref[...]
        d_lo = jnp.sum(jnp.where(lo, prod, 0.0), axis=-1, keepdims=True)
        d_hi = jnp.sum(jnp.where(lo, 0.0, prod), axis=-1, keepdims=True)
        for hh, delta in enumerate((d_lo, d_hi)):
            base = jnp.where(lo, u if hh == 0 else pltpu.roll(u, FOX_DH, 1), 0.0)
            da_ref[hh] = _lane_put(base, lane, 64, [-(p.astype(F32)) for p in _split3(delta)]).astype(BF)

    blk = pl.BlockSpec((T, LANES), lambda i, hp: (i, hp))
    aug = pl.BlockSpec((2, T, LANES), lambda i, hp: (hp, i, 0))
    return pl.pallas_call(
        body, name=name, grid=(S // T, HP),
        in_specs=[blk, blk, pl.BlockSpec((T, LANES), lambda i, hp: (i, 3 * HP + hp)), aug],
        out_specs=aug,
        out_shape=jax.ShapeDtypeStruct((2 * HP, S, LANES), BF),
        compiler_params=_cp("parallel", "arbitrary"),
    )(dy, o, proj, q2)


def _fox_bwd(imax, q2, ka, va, doa, *, name):
    H, S, _ = q2.shape
    B = _fox_block(S)
    nb = S // B

    def body(imax_ref, q_ref, do_ref, k_ref, v_ref, dq_ref, dk_ref, dv_ref, cs_ref):
        j = pl.program_id(1)
        end = imax_ref[pl.program_id(0), j] + 1

        @pl.when(j == 0)
        def _():
            dq_ref[...] = jnp.zeros_like(dq_ref)

        kb, vb = k_ref[...], v_ref[...]

        def step(i, carry, nblk=1):
            dk_acc, dv_acc, cs_acc = carry
            rows = nblk * B
            sl = pl.ds(pl.multiple_of(i * B, B), rows)
            qb, dob = q_ref[sl, :], do_ref[sl, :]
            s = _dg(qb, kb, NT)
            ahead = lax.broadcasted_iota(jnp.int32, (rows, B), 0) - lax.broadcasted_iota(jnp.int32, (rows, B), 1)
            pb = jnp.exp2(jnp.where(ahead >= (j - i) * B, s, -jnp.inf)).astype(BF)
            ds = pb.astype(F32) * _dg(dob, vb, NT)
            dsb = ds.astype(BF)
            cs_acc = cs_acc + jnp.sum(ds.reshape(rows // 8, 8, B), axis=0)
            dv_acc = dv_acc + _dg(pb, dob, TN)
            dk_acc = dk_acc + _dg(dsb, qb, TN)
            dq_ref[sl, :] += _dot(dsb, kb)
            return dk_acc, dv_acc, cs_acc

        zero = jnp.zeros((B, LANES), F32)
        carry = (zero, zero, jnp.zeros((8, B), F32))
        pos = j
        for U in FOX_BWD_TILES:
            n = (end - pos) // U
            carry = lax.fori_loop(0, n, lambda ii, c, pos=pos, U=U: step(pos + U * ii, c, nblk=U), carry)
            pos = pos + U * n
        dk_acc, dv_acc, cs_acc = carry
        dk_ref[...] = dk_acc
        dv_ref[...] = dv_acc
        cs_ref[...] = jnp.sum(cs_acc, axis=0, keepdims=True)

    full = pl.BlockSpec((None, S, LANES), lambda h, j, im: (h, 0, 0))
    blk = pl.BlockSpec((None, B, LANES), lambda h, j, im: (h, j, 0))
    return pl.pallas_call(
        body, name=name,
        grid_spec=pltpu.PrefetchScalarGridSpec(
            num_scalar_prefetch=1, grid=(H, nb),
            in_specs=[full, full, blk, blk],
            out_specs=[full, blk, blk, pl.BlockSpec((None, 1, B), lambda h, j, im: (h, 0, j))]),
        out_shape=[jax.ShapeDtypeStruct((H, S, LANES), F32)] * 3 + [jax.ShapeDtypeStruct((H, 1, S), F32)],
        compiler_params=_cp("parallel", "arbitrary"),
    )(imax, q2, doa, ka, va)


def _fox_bwd_post(dqa, dka, dva, proj, dy, o, qw2, kw2, *, name):
    S, D = dy.shape
    HP = D // LANES
    T = _pick(S, 512, 16)

    def body(dq_ref, dk_ref, dv_ref, q_ref, k_ref, g_ref, dy_ref, o_ref, qw_ref, kw_ref, dp_ref, dqw_ref, dkw_ref):
        @pl.when((pl.program_id(0) == 0) & (pl.program_id(1) == 0))
        def _():
            dqw_ref[...] = jnp.zeros_like(dqw_ref)
            dkw_ref[...] = jnp.zeros_like(dkw_ref)

        lane = lax.broadcasted_iota(jnp.int32, (T, LANES), 1)
        lo = lane < FOX_DH

        def pair(ref):
            return jnp.where(lo, ref[0], pltpu.roll(ref[1], FOX_DH, 1))

        def norm_bwd(xv, w, dyn, dw_ref):
            r = lax.rsqrt(_pair_stats(xv * xv, lo) + EPS)
            xr = xv * r
            dw_ref[...] += jnp.sum(dyn * xr, axis=0, keepdims=True)
            u = dyn * w
            return r * (u - xr * _pair_stats(u * xr, lo))

        dp_ref[0] = norm_bwd(q_ref[...], qw_ref[...], pair(dq_ref) * 0.125, dqw_ref).astype(BF)
        dp_ref[1] = norm_bwd(k_ref[...], kw_ref[...], pair(dk_ref) * (1.0 / LOG2E), dkw_ref).astype(BF)
        dp_ref[2] = pair(dv_ref).astype(BF)
        sg = _sigmoid(g_ref[...])
        dp_ref[3] = (dy_ref[...] * o_ref[...] * sg * (1.0 - sg)).astype(BF)

    def part(p):
        return pl.BlockSpec((T, LANES), lambda i, hp: (i, p * HP + hp))

    aug = pl.BlockSpec((2, T, LANES), lambda i, hp: (hp, i, 0))
    blk = pl.BlockSpec((T, LANES), lambda i, hp: (i, hp))
    vec = pl.BlockSpec((1, LANES), lambda i, hp: (0, 0))
    return pl.pallas_call(
        body, name=name, grid=(S // T, HP),
        in_specs=[aug, aug, aug, part(0), part(1), part(3), blk, blk, vec, vec],
        out_specs=[pl.BlockSpec((4, T, LANES), lambda i, hp: (0, i, hp)), vec, vec],
        out_shape=[jax.ShapeDtypeStruct((5, S, D), BF), jax.ShapeDtypeStruct((1, LANES), F32),
                   jax.ShapeDtypeStruct((1, LANES), F32)],
        compiler_params=_cp("arbitrary", "arbitrary"),
    )(dqa, dka, dva, proj, proj, proj, dy, o, qw2, kw2)


def _fox_dfz(colsum, nheads, proj, bf_pad, dproj, *, name):
    S = colsum.shape[0]
    H = nheads
    D = dproj.shape[2]
    T = _pick(S, 256, 16)
    nb = S // T

    def body(cs_ref, fz_ref, b_ref, _, dp_ref, db_ref, carry):
        @pl.when(pl.program_id(0) == 0)
        def _():
            carry[...] = jnp.zeros_like(carry)
            db_ref[...] = jnp.zeros_like(db_ref)

        lane = lax.broadcasted_iota(jnp.int32, (T, LANES), 1)
        df = -cs_ref[...]
        triu = jnp.where(lax.broadcasted_iota(jnp.int32, (T, T), 0) <= lax.broadcasted_iota(jnp.int32, (T, T), 1),
                         1.0, 0.0).astype(BF)
        dlogf = _tri_dot(triu, df) + carry[...]
        carry[...] = _row_of(dlogf, lax.broadcasted_iota(jnp.int32, (T, LANES), 0), 0)
        dfz = jnp.where(lane < H, dlogf * _sigmoid(-(fz_ref[...] + b_ref[...])), 0.0)
        db_ref[...] += jnp.sum(dfz, axis=0, keepdims=True)
        dp_ref[...] = jnp.zeros_like(dp_ref)
        dp_ref[:, 0:LANES] = dfz.astype(BF)

    return pl.pallas_call(
        body, name=name, grid=(nb,),
        in_specs=[pl.BlockSpec((T, LANES), lambda i: (nb - 1 - i, 0)),
                  pl.BlockSpec((T, LANES), lambda i: (nb - 1 - i, 4 * D // LANES)),
                  pl.BlockSpec((1, LANES), lambda i: (0, 0)),
                  pl.BlockSpec(memory_space=pl.ANY)],
        out_specs=[pl.BlockSpec((None, T, D), lambda i: (4, nb - 1 - i, 0)), pl.BlockSpec((1, LANES), lambda i: (0, 0))],
        out_shape=[jax.ShapeDtypeStruct(dproj.shape, BF), jax.ShapeDtypeStruct((1, LANES), F32)],
        scratch_shapes=[pltpu.VMEM((1, LANES), F32)],
        input_output_aliases={3: 0},
        compiler_params=_cp("arbitrary"),
    )(colsum, proj, bf_pad, dproj)


def _mod_fwd(c16, w, b, *, name):
    L, D, N = w.shape
    tn = _pick(N, 512)

    def body(c_ref, w_ref, b_ref, o_ref):
        cv = c_ref[...]
        ca = (cv * _sigmoid(cv)).astype(BF)
        o_ref[...] = _dot(ca, w_ref[...].astype(BF)) + b_ref[...]

    return pl.pallas_call(
        body, name=name, grid=(L, N // tn),
        in_specs=[pl.BlockSpec((16, D), lambda l, j: (0, 0)), pl.BlockSpec((None, D, tn), lambda l, j: (l, 0, j)),
                  pl.BlockSpec((None, 1, tn), lambda l, j: (l, 0, j))],
        out_specs=pl.BlockSpec((None, 16, tn), lambda l, j: (l, 0, j)),
        out_shape=jax.ShapeDtypeStruct((L, 16, N), F32),
        compiler_params=_cp("parallel", "arbitrary"),
    )(c16, w, b)


def _mod_bwd(c16, dmod, *, name):
    L, _, N = dmod.shape
    D = c16.shape[1]
    tn = _pick(N, 512)

    def body(c_ref, d_ref, o_ref):
        cv = c_ref[...]
        ca = (cv * _sigmoid(cv)).astype(BF)
        o_ref[...] = _dg(ca, d_ref[...].astype(BF), TN)

    return pl.pallas_call(
        body, name=name, grid=(L, N // tn),
        in_specs=[pl.BlockSpec((16, D), lambda l, j: (0, 0)), pl.BlockSpec((None, 16, tn), lambda l, j: (l, 0, j))],
        out_specs=pl.BlockSpec((None, D, tn), lambda l, j: (l, 0, j)),
        out_shape=jax.ShapeDtypeStruct((L, D, N), F32),
        compiler_params=_cp("parallel", "arbitrary"),
    )(c16, dmod)


def _adamw_math(w, g, m, v):
    m = ADAM_B1 * m + (1.0 - ADAM_B1) * g
    v = ADAM_B2 * v + (1.0 - ADAM_B2) * (g * g)
    m_hat = m / (1.0 - ADAM_B1 ** ADAM_STEP)
    v_hat = v / (1.0 - ADAM_B2 ** ADAM_STEP)
    return -ADAM_LR * (m_hat / (jnp.sqrt(v_hat) + ADAM_EPS) + ADAM_WD * w), m, v


def _adamw(w, g, m, v, *, g_at=None, name):
    R, C = w.shape
    row0 = 0 if g_at is None else g_at[1]
    tr = min(math.gcd(row0, 256) if row0 else 256, -(-R // 8) * 8)
    g0 = row0 // tr
    if g_at is None:
        g_spec = pl.BlockSpec((tr, C), lambda i: (i, 0))
    else:
        g_spec = pl.BlockSpec((None, tr, C), lambda i: (g_at[0], g0 + i, 0))

    def body(w_ref, g_ref, m_ref, v_ref, d_ref, mo_ref, vo_ref):
        d, mn, vn = _adamw_math(w_ref[...], g_ref[...], m_ref[...], v_ref[...])
        d_ref[...] = d
        mo_ref[...] = mn
        vo_ref[...] = vn

    blk = pl.BlockSpec((tr, C), lambda i: (i, 0))
    return pl.pallas_call(
        body, name=name, grid=(pl.cdiv(R, tr),),
        in_specs=[blk, g_spec, blk, blk],
        out_specs=[blk, blk, blk],
        out_shape=[jax.ShapeDtypeStruct((R, C), F32)] * 3,
        compiler_params=_cp("parallel"),
    )(w, g, m, v)


def _sum_parts(parts, *, name):
    P, R, C = parts.shape

    def body(p_ref, o_ref):
        acc = p_ref[0]
        for p in range(1, P):
            acc = acc + p_ref[p]
        o_ref[...] = acc

    return pl.pallas_call(
        body, name=name, grid=(1,),
        in_specs=[pl.BlockSpec((P, R, C), lambda i: (0, 0, 0))],
        out_specs=pl.BlockSpec((R, C), lambda i: (0, 0)),
        out_shape=jax.ShapeDtypeStruct((R, C), F32),
        compiler_params=_cp("arbitrary"),
    )(parts)


def _add_halves(g4, recv, c_idx, *, name):
    _, _, Rh, C = g4.shape
    tr = min(256, Rh)

    def body(c_ref, a_ref, b_ref, o_ref):
        o_ref[...] = (a_ref[...] + b_ref[...].astype(F32)).astype(BF)

    return pl.pallas_call(
        body, name=name,
        grid_spec=pltpu.PrefetchScalarGridSpec(
            num_scalar_prefetch=1, grid=(4, pl.cdiv(Rh, tr)),
            in_specs=[pl.BlockSpec((None, None, tr, C), lambda j, r, c: (j, c[0], r, 0)),
                      pl.BlockSpec((None, tr, C), lambda j, r, c: (j, r, 0))],
            out_specs=pl.BlockSpec((None, tr, C), lambda j, r, c: (j, r, 0))),
        out_shape=jax.ShapeDtypeStruct((4, Rh, C), BF),
        compiler_params=_cp("parallel", "arbitrary"),
    )(c_idx, g4, recv)


def _add_four(g4, from_sibling, from_chips, pos, *, name):
    _, _, Rh, C = g4.shape
    tr = min(256, Rh)

    def body(p_ref, a_ref, s_ref, b_ref, o_ref):
        own = a_ref[...] + s_ref[...].astype(F32)
        o_ref[...] = ((own + b_ref[0].astype(F32)) + b_ref[1].astype(F32)) + b_ref[2].astype(F32)

    return pl.pallas_call(
        body, name=name,
        grid_spec=pltpu.PrefetchScalarGridSpec(
            num_scalar_prefetch=1, grid=(pl.cdiv(Rh, tr),),
            in_specs=[pl.BlockSpec((None, None, tr, C), lambda r, p: (p[0], p[1], r, 0)),
                      pl.BlockSpec((None, tr, C), lambda r, p: (p[0], r, 0)),
                      pl.BlockSpec((3, tr, C), lambda r, p: (0, r, 0))],
            out_specs=pl.BlockSpec((None, tr, C), lambda r, p: (p[1], r, 0))),
        out_shape=jax.ShapeDtypeStruct((2, Rh, C), F32),
        compiler_params=_cp("arbitrary"),
    )(pos, g4, from_sibling, from_chips)


HBM = pl.BlockSpec(memory_space=pltpu.HBM)


def _mesh_pos():
    return lax.axis_index("x"), lax.axis_index("y"), lax.axis_index("c")


def _other_chips(x, y):
    return [(1 - x, y), (x, 1 - y), (1 - x, 1 - y)]


def _allgather_small(xs, *, name):
    m_per, n = xs.shape

    def body(x_ref, out_ref, send_sems, recv_sems, local_sem):
        x, y, c = _mesh_pos()
        me, sibling = (x, y, c), (x, y, 1 - c)
        chips = _other_chips(x, y)

        def rows(px, py, pc):
            return out_ref.at[pl.ds((4 * px + 2 * py + pc) * m_per, m_per), :]

        def copy(k, block, to, src=None):
            return pltpu.make_async_remote_copy(
                src_ref=rows(*block) if src is None else src, dst_ref=rows(*block),
                send_sem=send_sems.at[k], recv_sem=recv_sems.at[k], device_id=to, device_id_type=MESH)

        mine = pltpu.make_async_copy(x_ref, rows(*me), local_sem)
        mine.start()
        first = [copy(0, me, sibling, src=x_ref)]
        first += [copy(1 + j, me, (*chip, c), src=x_ref) for j, chip in enumerate(chips)]
        for cp in first:
            cp.start()
        passed = [copy(4 + j, (*chip, c), sibling) for j, chip in enumerate(chips)]
        for j, chip in enumerate(chips):
            copy(1 + j, (*chip, c), me).wait_recv()
            passed[j].start()
        copy(0, sibling, me).wait_recv()
        for j, chip in enumerate(chips):
            copy(4 + j, (*chip, 1 - c), me).wait_recv()
        for cp in first + passed:
            cp.wait_send()
        mine.wait()

    return pl.pallas_call(
        body, name=name,
        out_shape=jax.ShapeDtypeStruct((N_DEV * m_per, n), xs.dtype),
        in_specs=[pl.BlockSpec(memory_space=pltpu.VMEM)],
        out_specs=pl.BlockSpec(memory_space=pltpu.VMEM),
        scratch_shapes=[pltpu.SemaphoreType.DMA((7,)), pltpu.SemaphoreType.DMA((7,)), pltpu.SemaphoreType.DMA],
    )(xs)


def _chip_slab_copies(s_ref, out_ref, send_sems, recv_sems):
    R = s_ref.shape[0]
    Rh = R // 2
    x, y, c = _mesh_pos()
    me, sibling = (x, y, c), (x, y, 1 - c)
    chips = _other_chips(x, y)

    def half(px, py, pc):
        return out_ref.at[2 * px + py, pl.ds(pc * Rh, Rh), :]

    def copy(k, block, to, src=None):
        return pltpu.make_async_remote_copy(
            src_ref=half(*block) if src is None else src, dst_ref=half(*block),
            send_sem=send_sems.at[k], recv_sem=recv_sems.at[k], device_id=to, device_id_type=MESH)

    first = [copy(j, me, (*chip, c), src=s_ref.at[pl.ds(c * Rh, Rh), :]) for j, chip in enumerate(chips)]
    passed = [copy(3 + j, (*chip, c), sibling) for j, chip in enumerate(chips)]
    landed = [copy(j, (*chip, c), me) for j, chip in enumerate(chips)]
    from_sibling = [copy(3 + j, (*chip, 1 - c), me) for j, chip in enumerate(chips)]
    return first, passed, landed, from_sibling


def _allgather_chip_slabs(slab, *, name):
    R, C = slab.shape

    def body(s_ref, out_ref, send_sems, recv_sems):
        first, passed, landed, from_sibling = _chip_slab_copies(s_ref, out_ref, send_sems, recv_sems)
        for cp in first:
            cp.start()
        for arrived, onward in zip(landed, passed):
            arrived.wait_recv()
            onward.start()
        for cp in from_sibling:
            cp.wait_recv()
        for cp in first + passed:
            cp.wait_send()

    return pl.pallas_call(
        body, name=name,
        out_shape=jax.ShapeDtypeStruct((N_CHIPS, R, C), slab.dtype),
        in_specs=[HBM], out_specs=HBM,
        scratch_shapes=[pltpu.SemaphoreType.DMA((6,)), pltpu.SemaphoreType.DMA((6,))],
    )(slab)


def _swap_halves(mine, *, name):
    def body(g_ref, out_ref, send_sems, recv_sems):
        x, y, c = _mesh_pos()
        copies = [pltpu.make_async_remote_copy(
            src_ref=g_ref.at[j], dst_ref=out_ref.at[j], send_sem=send_sems.at[j], recv_sem=recv_sems.at[j],
            device_id=(x, y, 1 - c), device_id_type=MESH) for j in range(N_CHIPS)]
        for cp in copies:
            cp.start()
        for cp in copies:
            cp.wait()

    return pl.pallas_call(
        body, name=name,
        out_shape=jax.ShapeDtypeStruct(mine.shape, mine.dtype),
        in_specs=[HBM], out_specs=HBM,
        scratch_shapes=[pltpu.SemaphoreType.DMA((N_CHIPS,)), pltpu.SemaphoreType.DMA((N_CHIPS,))],
    )(mine)


def _scatter_copies(p_ref, out_ref, send_sems, recv_sems):
    x, y, c = _mesh_pos()
    return [pltpu.make_async_remote_copy(
        src_ref=p_ref.at[2 * px + py], dst_ref=out_ref.at[j], send_sem=send_sems.at[j], recv_sem=recv_sems.at[j],
        device_id=(px, py, c), device_id_type=MESH) for j, (px, py) in enumerate(_other_chips(x, y))]


def _scatter_partials(part, *, name):
    _, Rh, C = part.shape

    def body(p_ref, out_ref, send_sems, recv_sems):
        copies = _scatter_copies(p_ref, out_ref, send_sems, recv_sems)
        for cp in copies:
            cp.start()
        for cp in copies:
            cp.wait()

    return pl.pallas_call(
        body, name=name,
        out_shape=jax.ShapeDtypeStruct((3, Rh, C), part.dtype),
        in_specs=[HBM], out_specs=HBM,
        scratch_shapes=[pltpu.SemaphoreType.DMA((3,)), pltpu.SemaphoreType.DMA((3,))],
    )(part)


def _join_halves(buf, *, name):
    def body(b_ref, out_ref, send_sem, recv_sem):
        x, y, c = _mesh_pos()
        cp = pltpu.make_async_remote_copy(
            src_ref=b_ref.at[c], dst_ref=out_ref.at[c], send_sem=send_sem, recv_sem=recv_sem,
            device_id=(x, y, 1 - c), device_id_type=MESH)
        cp.start()
        cp.wait()

    return pl.pallas_call(
        body, name=name,
        out_shape=jax.ShapeDtypeStruct(buf.shape, buf.dtype),
        in_specs=[HBM], out_specs=HBM, input_output_aliases={0: 0},
        scratch_shapes=[pltpu.SemaphoreType.DMA, pltpu.SemaphoreType.DMA],
    )(buf)


def _pad_rows(a, mult):
    pad = (-a.shape[0]) % mult
    return a if pad == 0 else jnp.pad(a, ((0, pad),) + ((0, 0),) * (a.ndim - 1))


def _local_step(x, target, mod, wts, small, slab_rest=None, unpack_rest=None, reduce_early=None, grad_slab=None):
    S, D = x.shape
    HP = D // LANES
    row = lambda v: v.reshape(1, -1)
    msplit = [[row(mod[i, k * D:(k + 1) * D]) for k in range(6)] for i in range(2)]
    gw, gs = {}, {}
    dmod = [[None] * 6 for _ in range(2)]
    slab, where = grad_slab if grad_slab is not None else (None, {})

    def dw(key, a, b, name):
        nonlocal slab
        if key in where:
            slab = _matmul_tn(a, b, name=name, into=(slab,) + where[key])
        else:
            gw[key] = _matmul_tn(a, b, name=name)

    sh1, sc1, g1, sh2, sc2, g2 = msplit[0]
    n1w0, n2w0 = row(small["norm1_w"][0]), row(small["norm2_w"][0])
    proj0, h1_0 = _ln_matmul(x, n1w0, sc1, sh1, wts["hg_w_in"], relu2=False, name="hg_in_proj")
    gn = small["hg_gn_w"].reshape(1, LANES)
    ypre0, o0, states, *gathered = _hg_fwd(proj0, small["hg_lb"], gn, slab_rest, name="hg_fwd")
    if slab_rest is not None:
        wts = {**wts, **unpack_rest(gathered[0])}
    x1, ymix0 = _matmul_resid(ypre0, wts["hg_w_out"], x, g1, name="hg_out_proj")
    a0, u0, h2_0 = _ln_matmul(x1, n2w0, sc2, sh2, wts["mlp_w1_0"], relu2=True, name="mlp0_up")
    x2, ymlp0 = _matmul_resid(u0, wts["mlp_w2_0"], x1, g2, name="mlp0_down")

    sh1b, sc1b, g1b, sh2b, sc2b, g2b = msplit[1]
    n1w1, n2w1 = row(small["norm1_w"][1]), row(small["norm2_w"][1])
    proj1, h1_1 = _ln_matmul(x2, n1w1, sc1b, sh1b, wts["fox_w_in"], relu2=False, name="fox_in_proj")
    nheads = 2 * HP
    bf_pad = jnp.pad(small["fox_b_f"].reshape(1, nheads), ((0, 0), (0, LANES - nheads)))
    qw2 = jnp.tile(small["fox_qn_w"].reshape(1, FOX_DH), (1, 2))
    kw2 = jnp.tile(small["fox_kn_w"].reshape(1, FOX_DH), (1, 2))
    fcum = _fox_cumsum(proj1, bf_pad, name="fox_cumsum")
    qa, ka, va, vat = _fox_prep(proj1, fcum, qw2, kw2, name="fox_prep")
    jmin, imax = _fox_skip_bounds(fcum, small["fox_qn_w"], small["fox_kn_w"], nheads)
    ypre1, o1, q2 = _fox_fwd(jmin, qa, ka, vat, proj1, name="fox_fwd")
    x3, ymix1 = _matmul_resid(ypre1, wts["fox_w_out"], x2, g1b, name="fox_out_proj")
    a1, u1, h2_1 = _ln_matmul(x3, n2w1, sc2b, sh2b, wts["mlp_w1_1"], relu2=True, name="mlp1_up")
    x4, ymlp1 = _matmul_resid(u1, wts["mlp_w2_1"], x3, g2b, name="mlp1_down")

    loss, dx4, dfw = _loss_kernel(x4, row(small["final_w"]), target, name="loss")
    gs["final_w"] = dfw.reshape(-1)

    def mlp_bwd(i, dx_out, x_in, h2, a, u, ymlp, n2w, sc2_, g2_):
        dz, dm, dg2 = _gate_matmul_nt(dx_out, g2_, ymlp, wts[f"mlp_w2_{i}"], a, name=f"mlp{i}_down_bwd")
        dw(f"mlp_w2_{i}", u, dm[None], f"mlp{i}_dw2")
        dw(f"mlp_w1_{i}", h2, dz[None], f"mlp{i}_dw1")
        dx_in, dsc, dsh, dnw = _matmul_nt_lnbwd(dz[None], wts[f"mlp_w1_{i}"], x_in, n2w, sc2_, dx_out,
                                                name=f"mlp{i}_up_bwd")
        dmod[i][3], dmod[i][4], dmod[i][5] = dsh, dsc, dg2
        return dx_in, dnw

    dx3, dn2w1 = mlp_bwd(1, dx4, x3, h2_1, a1, u1, ymlp1, n2w1, sc2b, g2b)
    dyp1, dm1, dg1b = _gate_matmul_nt(dx3, g1b, ymix1, wts["fox_w_out"], None, name="fox_out_bwd")
    dw("fox_w_out", ypre1, dm1[None], "fox_dw_out")
    doa = _fox_bwd_prep(dyp1, o1, proj1, q2, name="fox_bwd_prep")
    dqa, dka, dva, colsum = _fox_bwd(imax, q2, ka, va, doa, name="fox_bwd")
    colsum = jnp.pad(colsum[:, 0, :].T, ((0, 0), (0, LANES - nheads)))
    dproj1, dqw, dkw = _fox_bwd_post(dqa, dka, dva, proj1, dyp1, o1, qw2, kw2, name="fox_bwd_post")
    dproj1, dbf = _fox_dfz(colsum, nheads, proj1, bf_pad, dproj1, name="fox_dfz")
    dw("fox_w_in", h1_1, dproj1, "fox_dw_in")
    dx2, dsc, dsh, dn1w1 = _matmul_nt_lnbwd(dproj1, wts["fox_w_in"], x2, n1w1, sc1b, dx3, name="fox_in_bwd")
    dmod[1][0], dmod[1][1], dmod[1][2] = dsh, dsc, dg1b
    gs["fox_qn_w"] = dqw[0, :FOX_DH] + dqw[0, FOX_DH:]
    gs["fox_kn_w"] = dkw[0, :FOX_DH] + dkw[0, FOX_DH:]
    gs["fox_b_f"] = dbf[0, :nheads]

    dx1, dn2w0 = mlp_bwd(0, dx2, x1, h2_0, a0, u0, ymlp0, n2w0, sc2, g2)
    dyp0, dm0, dg1 = _gate_matmul_nt(dx1, g1, ymix0, wts["hg_w_out"], None, name="hg_out_bwd")
    dw("hg_w_out", ypre0, dm0[None], "hg_dw_out")
    part, ctx = reduce_early(gw, slab) if reduce_early is not None else (None, None)
    dproj0, dlb, dgn, *from_chips = _hg_bwd(proj0, small["hg_lb"], gn, o0, states, dyp0, part, name="hg_bwd")
    early = (ctx, from_chips[0]) if reduce_early is not None else None
    dw("hg_w_in", h1_0, dproj0, "hg_dw_in")
    dx0, dsc, dsh, dn1w0 = _matmul_nt_lnbwd(dproj0, wts["hg_w_in"], x, n1w0, sc1, dx1, name="hg_in_bwd")
    dmod[0][0], dmod[0][1], dmod[0][2] = dsh, dsc, dg1
    gs["hg_lb"] = dlb
    gs["hg_gn_w"] = jnp.sum(dgn, axis=0)

    gs["norm1_w"] = jnp.concatenate([dn1w0, dn1w1], axis=0)
    gs["norm2_w"] = jnp.concatenate([dn2w0, dn2w1], axis=0)
    gs["dmod"] = jnp.stack([jnp.concatenate(dmod[i], axis=1)[0] for i in range(2)])
    return loss, dx0, gw, gs, early


def _pack_halves(layout):
    rh = -(-max(sum(a.shape[0] for _, a in half) for half in layout) // 16) * 16
    place, parts = {}, []
    for h, half in enumerate(layout):
        off = 0
        for n, a in half:
            place[n] = (h, off, a.shape[0])
            off += a.shape[0]
        parts.append(jnp.pad(jnp.concatenate([a.astype(BF) for _, a in half], axis=0), ((0, rh - off), (0, 0))))
    return jnp.concatenate(parts, axis=0), place, rh


SMALL_NAMES = ["norm1_w", "norm2_w", "hg_lb", "hg_gn_w", "fox_b_f", "fox_qn_w", "fox_kn_w", "final_w"]


def _pack_small(d, names):
    rows, offs, r0 = [], {}, 0
    for n in names:
        flat = d[n].reshape(-1)
        nr = -(-flat.shape[0] // LANES)
        rows.append(jnp.pad(flat, (0, nr * LANES - flat.shape[0])).reshape(nr, LANES))
        offs[n] = (r0, nr)
        r0 += nr
    return jnp.concatenate(rows, axis=0), offs


def _unpack_small(packed, offs, name, like):
    r0, nr = offs[name]
    return packed[r0:r0 + nr].reshape(-1)[:like.size].reshape(like.shape)


def kernel(x, c, w_mod, b_mod, norm1_w, norm2_w, hg_w_in, hg_w_out, hg_lb, hg_gn_w, fox_w_in, fox_b_f, fox_qn_w, fox_kn_w, fox_w_out, mlp_w1, mlp_w2, final_w, loss_target, m_w_mod, m_b_mod, m_norm1_w, m_norm2_w, m_hg_w_in, m_hg_w_out, m_hg_lb, m_hg_gn_w, m_fox_w_in, m_fox_b_f, m_fox_qn_w, m_fox_kn_w, m_fox_w_out, m_mlp_w1, m_mlp_w2, m_final_w, v_w_mod, v_b_mod, v_norm1_w, v_norm2_w, v_hg_w_in, v_hg_w_out, v_hg_lb, v_hg_gn_w, v_fox_w_in, v_fox_b_f, v_fox_qn_w, v_fox_kn_w, v_fox_w_out, v_mlp_w1, v_mlp_w2, v_final_w):
    S, D = x.shape[1], x.shape[2]
    nheads = D // FOX_DH
    ax, ay, ac = _mesh_pos()
    chip = 2 * ax + ay
    dev = 2 * chip + ac
    xs, tgt = x.reshape(S, D), loss_target.reshape(S, D)

    c_all = _allgather_small(_pad_rows(c.reshape(-1, LANES), 8), name="gather_c")
    c_all = c_all.reshape(N_DEV, -1)[:, :D]
    c16 = _pad_rows(c_all, 16)
    nmod = w_mod.shape[2]
    b_shard = lax.dynamic_slice_in_dim(b_mod, chip * nmod, nmod, axis=1)
    mod_shard = _mod_fwd(c16, w_mod, b_shard[:, None, :], name="mod_fwd")[:, :N_DEV]
    mod_all = _allgather_small(mod_shard.reshape(-1, LANES), name="gather_mod")
    mod_all = mod_all.reshape(N_CHIPS, 2, 2, N_DEV, nmod)[:, 0]
    mod = lax.dynamic_index_in_dim(mod_all, dev, axis=2, keepdims=False)
    mod = mod.transpose(1, 0, 2).reshape(2, N_CHIPS * nmod)

    fox_rows = fox_w_in.shape[2]
    col = lambda g: g.transpose(1, 0, 2).reshape(g.shape[1], -1)
    rowsh = lambda g: g.reshape(-1, g.shape[2])
    own = lambda g, s: lax.dynamic_update_index_in_dim(g, s, chip, 0)

    slab_in = hg_w_in[0].astype(BF)
    wts = {"hg_w_in": col(own(_allgather_chip_slabs(slab_in, name="gather_hg_w_in"), slab_in))}
    slab_rest, place_rest, rh_rest = _pack_halves(
        [[("mlp_w1", mlp_w1.reshape(2 * D, D)), ("hg_w_out", hg_w_out[0]), ("fox_w_out", fox_w_out[0])],
         [("mlp_w2", mlp_w2.reshape(2 * D, D)), ("fox_w_in", fox_w_in[0].reshape(fox_rows, D))]])

    def unpack_rest(gathered):
        gathered = own(gathered, slab_rest)

        def seg(n):
            h, off, rows = place_rest[n]
            return gathered[:, h * rh_rest + off:h * rh_rest + off + rows, :]

        w1 = seg("mlp_w1").reshape(N_CHIPS, 2, D, D)
        w2 = seg("mlp_w2").reshape(N_CHIPS, 2, D, D)
        fox_in = col(seg("fox_w_in").reshape(N_CHIPS, D, fox_rows))
        return {
            "hg_w_out": rowsh(seg("hg_w_out")), "fox_w_out": rowsh(seg("fox_w_out")),
            "mlp_w1_0": col(w1[:, 0]), "mlp_w1_1": col(w1[:, 1]), "mlp_w2_0": rowsh(w2[:, 0]), "mlp_w2_1": rowsh(w2[:, 1]),
            "fox_w_in": jnp.pad(fox_in, ((0, 0), (0, 5 * D - fox_in.shape[1]))),
        }

    small = {"norm1_w": norm1_w, "norm2_w": norm2_w, "hg_lb": hg_lb, "hg_gn_w": hg_gn_w, "fox_b_f": fox_b_f,
             "fox_qn_w": fox_qn_w, "fox_kn_w": fox_kn_w, "final_w": final_w}

    def uncol(g, n):
        return g.reshape(g.shape[0], N_CHIPS, n).transpose(1, 0, 2)

    pos = jnp.stack([chip, ac])

    def swap_and_add(g4, tag):
        to_sibling = lax.dynamic_index_in_dim(g4, 1 - ac, axis=1, keepdims=False).astype(BF)
        from_sibling = _swap_halves(to_sibling, name=f"rs_swap_{tag}")
        return from_sibling, _add_halves(g4, from_sibling, ac.reshape(1), name=f"rs_add_halves_{tag}")

    def finish(g4, from_sibling, from_chips, tag):
        my_half = _add_four(g4, from_sibling, from_chips, pos, name=f"rs_add_chips_{tag}")
        return _join_halves(my_half, name=f"rs_join_{tag}")

    layout = [[("mlp_w1", 2 * D), ("hg_w_out", D // 4), ("fox_w_out", D // 4)], [("mlp_w2", 2 * D), ("fox_w_in", fox_rows)]]
    place = {}
    for h, half in enumerate(layout):
        off = 0
        for n, rows in half:
            place[n] = (h, off, rows)
            off += rows

    rh = -(-max(sum(rows for _, rows in half) for half in layout) // 16) * 16
    where = {"hg_w_out": ("row",) + place["hg_w_out"][:2], "fox_w_out": ("row",) + place["fox_w_out"][:2]}
    for i in range(2):
        where[f"mlp_w1_{i}"] = ("col", place["mlp_w1"][0], place["mlp_w1"][1] + i * D)
        where[f"mlp_w2_{i}"] = ("row", place["mlp_w2"][0], place["mlp_w2"][1] + i * D)

    def reduce_early(gw, slab):
        gfox = uncol(gw["fox_w_in"][:, :4 * fox_rows], fox_rows).reshape(N_CHIPS, 1, fox_rows, D)
        h, off, _ = place["fox_w_in"]
        slab = lax.dynamic_update_slice(slab, gfox, (0, h, off, 0))
        for h, half in enumerate(layout):
            used = sum(rows for _, rows in half)
            if used < rh:
                slab = lax.dynamic_update_slice(slab, jnp.zeros((N_CHIPS, 1, rh - used, D), F32), (0, h, used, 0))
        from_sibling, part = swap_and_add(slab, "early")
        return part, (slab, from_sibling)

    loss_part, grad_x, gw, gs, ((g4, from_sibling), from_chips) = _local_step(
        xs, tgt, mod, wts, small, slab_rest, unpack_rest, reduce_early, (lax.empty((N_CHIPS, 2, rh, D), F32), where))
    loss = lax.psum(loss_part[0, 0], ("x", "y", "c"))
    gshard = finish(g4, from_sibling, from_chips, "early")

    g4 = uncol(gw["hg_w_in"], D).reshape(N_CHIPS, 2, D // 2, D)
    from_sibling, part = swap_and_add(g4, "late")
    g_hg_w_in = finish(g4, from_sibling, _scatter_partials(part, name="rs_scatter_late"), "late").reshape(D, D)

    names = ["dmod"] + SMALL_NAMES
    packed, offs = _pack_small(gs, names)
    packed = _pad_rows(packed, 8)
    rp = packed.shape[0]
    parts = _allgather_small(packed, name="gather_small").reshape(N_DEV, rp, LANES)
    total = _sum_parts(parts, name="sum_small")
    r0, nr = offs["dmod"]
    dmod_all = parts[:, r0:r0 + nr].reshape(N_DEV, 2, N_CHIPS * nmod)
    dmod_shard = lax.dynamic_slice_in_dim(dmod_all, chip * nmod, nmod, axis=2).transpose(1, 0, 2)
    g_w_mod = _mod_bwd(c16, jnp.pad(dmod_shard, ((0, 0), (0, 16 - N_DEV), (0, 0))), name="mod_bwd")

    grads = {"w_mod": g_w_mod, "b_mod": _unpack_small(total, offs, "dmod", b_mod)}
    for n in SMALL_NAMES:
        grads[n] = _unpack_small(total, offs, n, small[n])

    given = dict(w_mod=(w_mod, m_w_mod, v_w_mod), b_mod=(b_mod, m_b_mod, v_b_mod), norm1_w=(norm1_w, m_norm1_w, v_norm1_w),
                 norm2_w=(norm2_w, m_norm2_w, v_norm2_w), hg_w_in=(hg_w_in, m_hg_w_in, v_hg_w_in),
                 hg_w_out=(hg_w_out, m_hg_w_out, v_hg_w_out), hg_lb=(hg_lb, m_hg_lb, v_hg_lb),
                 hg_gn_w=(hg_gn_w, m_hg_gn_w, v_hg_gn_w), fox_w_in=(fox_w_in, m_fox_w_in, v_fox_w_in),
                 fox_b_f=(fox_b_f, m_fox_b_f, v_fox_b_f), fox_qn_w=(fox_qn_w, m_fox_qn_w, v_fox_qn_w),
                 fox_kn_w=(fox_kn_w, m_fox_kn_w, v_fox_kn_w), fox_w_out=(fox_w_out, m_fox_w_out, v_fox_w_out),
                 mlp_w1=(mlp_w1, m_mlp_w1, v_mlp_w1), mlp_w2=(mlp_w2, m_mlp_w2, v_mlp_w2), final_w=(final_w, m_final_w, v_final_w))
    upd = {}

    for n, (h, off, rows) in place.items():
        w, m, v = given[n]
        flat = lambda a: a.reshape(rows, D)
        d, mn, vn = _adamw(flat(w), gshard, flat(m), flat(v), g_at=(h, off), name=f"adamw_{n}")
        grads[n] = gshard[h, off:off + rows].reshape(w.shape)
        upd[n] = tuple(a.reshape(w.shape) for a in (d, mn, vn))

    w, m, v = given["hg_w_in"]
    grads["hg_w_in"] = g_hg_w_in.reshape(w.shape)
    upd["hg_w_in"] = tuple(a.reshape(w.shape) for a in _adamw(w[0], g_hg_w_in, m[0], v[0], name="adamw_hg_w_in"))

    w, m, v = given["w_mod"]
    flat = lambda a: a.reshape(-1, nmod)
    upd["w_mod"] = tuple(a.reshape(w.shape) for a in _adamw(flat(w), flat(g_w_mod), flat(m), flat(v), name="adamw_w_mod"))

    snames = ["b_mod"] + SMALL_NAMES
    pw, soffs = _pack_small({n: given[n][0] for n in snames}, snames)
    pm, _ = _pack_small({n: given[n][1] for n in snames}, snames)
    pv, _ = _pack_small({n: given[n][2] for n in snames}, snames)
    pg, _ = _pack_small({n: grads[n] for n in snames}, snames)
    pw, pm, pv, pg = (_pad_rows(a, 8) for a in (pw, pm, pv, pg))
    sd, smn, svn = _adamw(pw, pg, pm, pv, name="adamw_small")
    for n in snames:
        like = given[n][0]
        upd[n] = tuple(_unpack_small(a, soffs, n, like) for a in (sd, smn, svn))

    order = ["w_mod", "b_mod", "norm1_w", "norm2_w", "hg_w_in", "hg_w_out", "hg_lb", "hg_gn_w", "fox_w_in", "fox_b_f",
             "fox_qn_w", "fox_kn_w", "fox_w_out", "mlp_w1", "mlp_w2", "final_w"]
    return (loss, grad_x.reshape(x.shape), *[grads[n] for n in order], *[upd[n][0] for n in order],
            *[upd[n][1] for n in order], *[upd[n][2] for n in order])
```

```python
import math

import jax
import jax.numpy as jnp
from jax import lax
from jax.experimental import pallas as pl
from jax.experimental.pallas import tpu as pltpu

EPS = 1e-6
ADAM_LR, ADAM_B1, ADAM_B2, ADAM_EPS, ADAM_WD, ADAM_STEP = 0.001, 0.9, 0.999, 1e-08, 0.01, 10

F32 = jnp.float32
BF = jnp.bfloat16
LANES = 128
HG_CHUNK = 64
HG_HEADS_PER_STEP = 8
HG_TOKENS_PER_STEP = 256
FOX_BWD_TILES = (8, 4, 2, 1)
LOG2E = 1.4426950408889634
FOX_DH = 64
N_CHIPS = 4
N_DEV = 8
VMEM_LIMIT = 56 * 1024 * 1024
MESH = pl.DeviceIdType.MESH

NT = (((1,), (1,)), ((), ()))
TN = (((0,), (0,)), ((), ()))


def _pick(n, pref, mult=LANES):
    if n <= pref:
        return n
    t = (pref // mult) * mult
    while t >= mult:
        if n % t == 0:
            return t
        t -= mult
    raise ValueError((n, pref, mult))


def _cp(*sem):
    return pltpu.CompilerParams(dimension_semantics=sem, vmem_limit_bytes=VMEM_LIMIT)


def _dot(a, b):
    return jnp.dot(a, b, preferred_element_type=F32)


def _dg(a, b, dims):
    return lax.dot_general(a, b, dims, preferred_element_type=F32)


def _split3(x):
    hi = x.astype(BF)
    r1 = x - hi.astype(F32)
    mid = r1.astype(BF)
    lo = (r1 - mid.astype(F32)).astype(BF)
    return hi, mid, lo


def _tri_dot(tri, x):
    hi, mid, lo = _split3(x)
    return _dot(tri, hi) + _dot(tri, mid) + _dot(tri, lo)


def _dg3(a, b, dims):
    ah, bh = a.astype(BF), b.astype(BF)
    al, bl = (a - ah.astype(F32)).astype(BF), (b - bh.astype(F32)).astype(BF)
    return _dg(ah, bh, dims) + _dg(ah, bl, dims) + _dg(al, bh, dims)


def _dg1(a, b, dims):
    return _dg(a.astype(BF), b.astype(BF), dims)


NN = (((1,), (0,)), ((), ()))


def _sigmoid(x):
    return jax.nn.sigmoid(x)


def _ln_matmul(x, nw, sc, sh, w, *, relu2, name):
    S, D = x.shape
    N = w.shape[1]
    tm, tn = _pick(S, 512, 16), N

    def body(x_ref, nw_ref, sc_ref, sh_ref, w_ref, *rest):
        outs, hs = rest[:-1], rest[-1]
        h_ref = outs[-1]

        @pl.when(pl.program_id(1) == 0)
        def _():
            xv = x_ref[...]
            r = lax.rsqrt(jnp.mean(xv * xv, axis=-1, keepdims=True) + EPS)
            hb = ((xv * r * nw_ref[...]) * (1.0 + sc_ref[...]) + sh_ref[...]).astype(BF)
            hs[...] = hb
            h_ref[...] = hb

        z = _dot(hs[...], w_ref[...])
        if relu2:
            a = jnp.maximum(z, 0.0)
            outs[0][...] = a.astype(BF)
            outs[1][...] = (a * a).astype(BF)
        else:
            outs[0][...] = z

    vec = pl.BlockSpec((1, D), lambda i, j: (0, 0))
    tile = pl.BlockSpec((tm, tn), lambda i, j: (i, j))
    if relu2:
        out_shape = [jax.ShapeDtypeStruct((S, N), BF), jax.ShapeDtypeStruct((S, N), BF)]
        out_specs = [tile, tile]
    else:
        out_shape = [jax.ShapeDtypeStruct((S, N), F32)]
        out_specs = [tile]
    out_shape.append(jax.ShapeDtypeStruct((S, D), BF))
    out_specs.append(pl.BlockSpec((tm, D), lambda i, j: (i, 0)))
    return pl.pallas_call(
        body, name=name, grid=(S // tm, N // tn),
        in_specs=[pl.BlockSpec((tm, D), lambda i, j: (i, 0)), vec, vec, vec,
                  pl.BlockSpec((D, tn), lambda i, j: (0, j))],
        out_specs=out_specs, out_shape=out_shape,
        scratch_shapes=[pltpu.VMEM((tm, D), BF)],
        compiler_params=_cp("parallel", "arbitrary"),
    )(x, nw, sc, sh, w)


def _matmul_resid(a, w, x, gate, *, name):
    S, K = a.shape
    D = w.shape[1]
    tm, tn = _pick(S, 1024 if K <= 1024 else 512, 16), D

    def body(a_ref, w_ref, x_ref, g_ref, o_ref, y_ref):
        y = _dot(a_ref[...], w_ref[...])
        y_ref[...] = y.astype(BF)
        o_ref[...] = x_ref[...] + g_ref[...] * y

    tile = pl.BlockSpec((tm, tn), lambda i, j: (i, j))
    return pl.pallas_call(
        body, name=name, grid=(S // tm, D // tn),
        in_specs=[pl.BlockSpec((tm, K), lambda i, j: (i, 0)), pl.BlockSpec((K, tn), lambda i, j: (0, j)),
                  tile, pl.BlockSpec((1, tn), lambda i, j: (0, j))],
        out_specs=[tile, tile],
        out_shape=[jax.ShapeDtypeStruct((S, D), F32), jax.ShapeDtypeStruct((S, D), BF)],
        compiler_params=_cp("parallel", "arbitrary"),
    )(a, w, x, gate)


def _gate_matmul_nt(dx, gate, y, w, act, *, name):
    S, D = dx.shape
    K = w.shape[0]
    tm, tn = _pick(S, 1024 if K <= 1024 else 512, 16), K
    fused = act is not None

    def body(dx_ref, g_ref, y_ref, w_ref, *rest):
        if fused:
            act_ref, da_ref, dm_ref, dg_ref, ms = rest
        else:
            da_ref, dm_ref, dg_ref, ms = rest
        i, j = pl.program_id(0), pl.program_id(1)

        @pl.when((i == 0) & (j == 0))
        def _():
            dg_ref[...] = jnp.zeros_like(dg_ref)

        @pl.when(j == 0)
        def _():
            dxv = dx_ref[...]
            dmb = (dxv * g_ref[...]).astype(BF)
            ms[...] = dmb
            dm_ref[...] = dmb
            dg_ref[...] += jnp.sum(dxv * y_ref[...].astype(F32), axis=0, keepdims=True)

        da = _dg(ms[...], w_ref[...], NT)
        if fused:
            da_ref[...] = (da * (2.0 * act_ref[...].astype(F32))).astype(BF)
        else:
            da_ref[...] = da

    row = pl.BlockSpec((tm, D), lambda i, j: (i, 0))
    vec = pl.BlockSpec((1, D), lambda i, j: (0, 0))
    tile = pl.BlockSpec((tm, tn), lambda i, j: (i, j))
    in_specs = [row, vec, row, pl.BlockSpec((tn, D), lambda i, j: (j, 0))]
    args = [dx, gate, y, w]
    if fused:
        in_specs.append(tile)
        args.append(act)
    return pl.pallas_call(
        body, name=name, grid=(S // tm, K // tn),
        in_specs=in_specs, out_specs=[tile, row, vec],
        out_shape=[jax.ShapeDtypeStruct((S, K), BF if fused else F32), jax.ShapeDtypeStruct((S, D), BF),
                   jax.ShapeDtypeStruct((1, D), F32)],
        scratch_shapes=[pltpu.VMEM((tm, D), BF)],
        compiler_params=_cp("arbitrary", "arbitrary"),
    )(*args)


def _matmul_tn(a, b, *, name, into=None):
    S, Ka = a.shape
    P, _, Db = b.shape
    tk, tn, ts = _pick(Ka, 1024), _pick(Db, 1024), _pick(S, 1024, 16)
    if into is not None:
        slab, kind, half, off = into
        C = tn = slab.shape[3]
        if kind == "row":
            tk = min(tk, Ka // N_CHIPS)
        assert tn == C and P * Db == (N_CHIPS * C if kind == "col" else C) and off % tk == 0
        assert tk == Ka if kind == "col" else (Ka // N_CHIPS) % tk == 0
    npb = Db // tn

    def body(a_ref, b_ref, *rest):
        o_ref, acc = rest[-2:]
        s = pl.program_id(2)

        @pl.when(s == 0)
        def _():
            acc[...] = jnp.zeros_like(acc)

        acc[...] += _dg(a_ref[...], b_ref[...], TN)

        @pl.when(s == pl.num_programs(2) - 1)
        def _():
            o_ref[...] = acc[...]

    in_specs = [pl.BlockSpec((ts, tk), lambda i, j, s: (s, i)),
                pl.BlockSpec((None, ts, tn), lambda i, j, s: (j // npb, s, j % npb))]
    args = [a, b]
    if into is None:
        out_spec = pl.BlockSpec((tk, tn), lambda i, j, s: (i, j))
        out_shape = jax.ShapeDtypeStruct((Ka, P * Db), F32)
        aliases = {}
    else:
        per = (Ka // N_CHIPS) // tk if kind == "row" else 1
        if kind == "col":
            out_spec = pl.BlockSpec((None, None, tk, tn), lambda i, j, s: (j, half, off // tk + i, 0))
        else:
            out_spec = pl.BlockSpec((None, None, tk, tn), lambda i, j, s: (i // per, half, off // tk + i % per, 0))
        out_shape = jax.ShapeDtypeStruct(slab.shape, F32)
        in_specs.append(pl.BlockSpec(memory_space=pl.ANY))
        args.append(slab)
        aliases = {2: 0}
    return pl.pallas_call(
        body, name=name, grid=(Ka // tk, P * npb, S // ts),
        in_specs=in_specs, out_specs=out_spec, out_shape=out_shape,
        scratch_shapes=[pltpu.VMEM((tk, tn), F32)], input_output_aliases=aliases,
        compiler_params=_cp("parallel", "parallel", "arbitrary"),
    )(*args)


def _matmul_nt_lnbwd(g, w, x, nw, sc, dx_out, *, name):
    P, S, Dg = g.shape
    D = x.shape[1]
    tm = _pick(S, 512, 16)

    def body(g_ref, w_ref, x_ref, nw_ref, sc_ref, dxo_ref, dx_ref, dsc_ref, dsh_ref, dnw_ref):
        @pl.when(pl.program_id(0) == 0)
        def _():
            dsc_ref[...] = jnp.zeros_like(dsc_ref)
            dsh_ref[...] = jnp.zeros_like(dsh_ref)
            dnw_ref[...] = jnp.zeros_like(dnw_ref)

        dh = _dg(g_ref[0], w_ref[:, 0:Dg], NT)
        for p in range(1, P):
            dh = dh + _dg(g_ref[p], w_ref[:, p * Dg:(p + 1) * Dg], NT)
        xv = x_ref[...]
        nwv = nw_ref[...]
        r = lax.rsqrt(jnp.mean(xv * xv, axis=-1, keepdims=True) + EPS)
        xr = xv * r
        dn = dh * (1.0 + sc_ref[...])
        dsc_ref[...] += jnp.sum(dh * (xr * nwv), axis=0, keepdims=True)
        dsh_ref[...] += jnp.sum(dh, axis=0, keepdims=True)
        dnw_ref[...] += jnp.sum(dn * xr, axis=0, keepdims=True)
        u = dn * nwv
        dx_ref[...] = dxo_ref[...] + r * (u - xr * jnp.mean(u * xr, axis=-1, keepdims=True))

    row = pl.BlockSpec((tm, D), lambda i: (i, 0))
    vec = pl.BlockSpec((1, D), lambda i: (0, 0))
    return pl.pallas_call(
        body, name=name, grid=(S // tm,),
        in_specs=[pl.BlockSpec((P, tm, Dg), lambda i: (0, i, 0)),
                  pl.BlockSpec((D, P * Dg), lambda i: (0, 0)), row, vec, vec, row],
        out_specs=[row, vec, vec, vec],
        out_shape=[jax.ShapeDtypeStruct((S, D), F32)] + [jax.ShapeDtypeStruct((1, D), F32)] * 3,
        compiler_params=_cp("arbitrary"),
    )(g, w, x, nw, sc, dx_out)


def _loss_kernel(x, fw, tgt, *, name):
    S, D = x.shape
    tm = _pick(S, 512, 8)

    def body(x_ref, fw_ref, t_ref, l_ref, dx_ref, dfw_ref):
        @pl.when(pl.program_id(0) == 0)
        def _():
            l_ref[...] = jnp.zeros_like(l_ref)
            dfw_ref[...] = jnp.zeros_like(dfw_ref)

        xv = x_ref[...]
        fwv = fw_ref[...]
        r = lax.rsqrt(jnp.mean(xv * xv, axis=-1, keepdims=True) + EPS)
        xr = xv * r
        err = xr * fwv - t_ref[...]
        per_tok = jnp.mean(err * err, axis=-1, keepdims=True)
        l_ref[...] += 0.5 * jnp.sum(per_tok, axis=0, keepdims=True)
        dy = err * (1.0 / D)
        dfw_ref[...] += jnp.sum(dy * xr, axis=0, keepdims=True)
        u = dy * fwv
        dx_ref[...] = r * (u - xr * jnp.mean(u * xr, axis=-1, keepdims=True))

    row = pl.BlockSpec((tm, D), lambda i: (i, 0))
    vec = pl.BlockSpec((1, D), lambda i: (0, 0))
    return pl.pallas_call(
        body, name=name, grid=(S // tm,),
        in_specs=[row, vec, row],
        out_specs=[pl.BlockSpec((1, LANES), lambda i: (0, 0)), row, vec],
        out_shape=[jax.ShapeDtypeStruct((1, LANES), F32), jax.ShapeDtypeStruct((S, D), F32),
                   jax.ShapeDtypeStruct((1, D), F32)],
        compiler_params=_cp("arbitrary"),
    )(x, fw, tgt)


def _hg_lower_bound(lb3):
    mx = jnp.max(lb3, axis=0, keepdims=True)
    e = jnp.exp(lb3 - mx)
    p = e / jnp.sum(e, axis=0, keepdims=True)
    return p[0:1, :], p


def _hg_chunk_common(qr, fz, lbv):
    sq = _sigmoid(qr)
    q = qr * sq
    sig = _sigmoid(fz)
    f = lbv + (1.0 - lbv) * sig
    k = (1.0 - lbv) * (1.0 - sig)
    return q, sq, sig, f, k, jnp.log(f)


def _row_of(x, rows, r):
    return jnp.sum(jnp.where(rows == r, x, 0.0), axis=0, keepdims=True)


def _hg_fwd(proj, hg_lb, gn, slab=None, *, name):
    S = proj.shape[0]
    D = proj.shape[1] // 4
    H = D // LANES
    HB = min(HG_HEADS_PER_STEP, H)
    W = HB * LANES
    C = HG_CHUNK
    T = _pick(S, HG_TOKENS_PER_STEP, C)
    nch, nb = T // C, S // T
    ng = H // HB
    fused = slab is not None

    def body(q_ref, fz_ref, v_ref, g_ref, lb_ref, gn_ref, *rest):
        if fused:
            s_ref, y_ref, o_ref, sts_ref, out_ref, st, send_sems, recv_sems = rest
            first, passed, landed, from_sibling = _chip_slab_copies(s_ref, out_ref, send_sems, recv_sems)
            hgrp, n = pl.program_id(0), pl.program_id(1)

            @pl.when((hgrp == 0) & (n == 0))
            def _():
                for cp in first:
                    cp.start()

            @pl.when((hgrp == ng - 1) & (n == (3 * nb) // 4))
            def _():
                for arrived, onward in zip(landed, passed):
                    arrived.wait_recv()
                    onward.start()
        else:
            y_ref, o_ref, sts_ref, st = rest

        @pl.when(pl.program_id(1) == 0)
        def _():
            st[...] = jnp.zeros_like(st)

        lb_all, _ = _hg_lower_bound(lb_ref[...])
        gnv = gn_ref[...]
        ri = lax.broadcasted_iota(jnp.int32, (C, C), 0)
        ci_ = lax.broadcasted_iota(jnp.int32, (C, C), 1)
        low = ri >= ci_
        tri = jnp.where(low, 1.0, 0.0).astype(BF)
        rows_w = lax.broadcasted_iota(jnp.int32, (C, W), 0)

        def chunk(ci, carry):
            sl = pl.ds(pl.multiple_of(ci * C, C), C)
            heads = [slice(hh * LANES, (hh + 1) * LANES) for hh in range(HB)]
            q, _, _, _, k, logf = _hg_chunk_common(q_ref[sl, :], fz_ref[sl, :], lb_all)
            vv, gg = v_ref[sl, :], g_ref[sl, :]
            G = _tri_dot(tri, logf)
            Gm = _row_of(G, rows_w, C // 2 - 1)
            Gl = _row_of(G, rows_w, C - 1)
            qt, kt = q * jnp.exp(G - Gm), k * jnp.exp(Gm - G)
            qe, kd, eGl = q * jnp.exp(G), k * jnp.exp(Gl - G), jnp.exp(Gl)
            A = [jnp.where(low, _dg1(qt[:, ls], kt[:, ls], NT), 0.0) for ls in heads]
            Sv = [st[hh] for hh in range(HB)]
            for hh in range(HB):
                sts_ref[hh, ci] = Sv[hh]
            o = [_dg1(A[hh], vv[:, ls], NN) + _dg1(qe[:, ls], Sv[hh], NT) for hh, ls in enumerate(heads)]
            for hh, ls in enumerate(heads):
                st[hh] = Sv[hh] * eGl[:, ls] + _dg1(vv[:, ls], kd[:, ls], TN)
            gate = gg * _sigmoid(gg)
            for hh, ls in enumerate(heads):
                r = lax.rsqrt(jnp.mean(o[hh] * o[hh], axis=-1, keepdims=True) + EPS)
                y_ref[sl, ls] = ((o[hh] * r * gnv) * gate[:, ls]).astype(BF)
                o_ref[sl, ls] = o[hh]
            return carry

        lax.fori_loop(0, nch, chunk, 0)

        if fused:
            @pl.when((hgrp == ng - 1) & (n == nb - 1))
            def _():
                for cp in from_sibling:
                    cp.wait_recv()
                for cp in first + passed:
                    cp.wait_send()

    def part(p):
        return pl.BlockSpec((T, W), lambda h, n: (n, p * ng + h))

    blk = pl.BlockSpec((T, W), lambda h, n: (n, h))
    in_specs = [part(0), part(1), part(2), part(3),
                pl.BlockSpec((3, W), lambda h, n: (0, h)), pl.BlockSpec((1, LANES), lambda h, n: (0, 0))]
    out_specs = [blk, blk, pl.BlockSpec((HB, nch, LANES, LANES), lambda h, n: (h, n, 0, 0))]
    out_shape = [jax.ShapeDtypeStruct((S, D), BF), jax.ShapeDtypeStruct((S, D), F32),
                 jax.ShapeDtypeStruct((H, S // C, LANES, LANES), F32)]
    scratch = [pltpu.VMEM((HB, LANES, LANES), F32)]
    args = [proj, proj, proj, proj, hg_lb, gn]
    if fused:
        in_specs.append(HBM)
        out_specs.append(HBM)
        out_shape.append(jax.ShapeDtypeStruct((N_CHIPS,) + slab.shape, slab.dtype))
        scratch += [pltpu.SemaphoreType.DMA((6,)), pltpu.SemaphoreType.DMA((6,))]
        args.append(slab)
    return pl.pallas_call(
        body, name=name, grid=(ng, nb), in_specs=in_specs, out_specs=out_specs, out_shape=out_shape,
        scratch_shapes=scratch, compiler_params=_cp("arbitrary", "arbitrary"),
    )(*args)


def _hg_bwd(proj, hg_lb, gn, o_all, states, dy, part=None, *, name):
    S = proj.shape[0]
    D = proj.shape[1] // 4
    H = D // LANES
    HB = min(HG_HEADS_PER_STEP, H)
    W = HB * LANES
    C = HG_CHUNK
    T = _pick(S, HG_TOKENS_PER_STEP, C)
    nch, nb = T // C, S // T
    ng = H // HB
    fused = part is not None

    def body(q_ref, fz_ref, v_ref, g_ref, lb_ref, gn_ref, o_ref, sts_ref, dy_ref, *rest):
        if fused:
            p_ref, dp_ref, dlb_ref, dgn_ref, recv_ref, dst, dlb_acc, send_sems, recv_sems = rest
            copies = _scatter_copies(p_ref, recv_ref, send_sems, recv_sems)

            @pl.when((pl.program_id(0) == 0) & (pl.program_id(1) == 0))
            def _():
                for cp in copies:
                    cp.start()
        else:
            dp_ref, dlb_ref, dgn_ref, dst, dlb_acc = rest
        n = pl.program_id(1)

        @pl.when(n == 0)
        def _():
            dst[...] = jnp.zeros_like(dst)
            dlb_acc[...] = jnp.zeros_like(dlb_acc)
            dgn_ref[...] = jnp.zeros_like(dgn_ref)

        lb_all, p3 = _hg_lower_bound(lb_ref[...])
        gnv = gn_ref[...]
        ri = lax.broadcasted_iota(jnp.int32, (C, C), 0)
        ci_ = lax.broadcasted_iota(jnp.int32, (C, C), 1)
        low = ri >= ci_
        tri = jnp.where(low, 1.0, 0.0).astype(BF)
        triu = jnp.where(ri <= ci_, 1.0, 0.0).astype(BF)
        rows_w = lax.broadcasted_iota(jnp.int32, (C, W), 0)
        gnw = jnp.tile(gnv, (1, HB))

        def chunk(cj, carry):
            ci = nch - 1 - cj
            sl = pl.ds(pl.multiple_of(ci * C, C), C)
            heads = list(enumerate(slice(hh * LANES, (hh + 1) * LANES) for hh in range(HB)))
            wide = lambda parts: jnp.concatenate(parts, axis=1)
            qr, vv, gg = q_ref[sl, :], v_ref[sl, :], g_ref[sl, :]
            q, sq, sig, f, k, logf = _hg_chunk_common(qr, fz_ref[sl, :], lb_all)
            G = _tri_dot(tri, logf)
            Gm = _row_of(G, rows_w, C // 2 - 1)
            Gl = _row_of(G, rows_w, C - 1)
            eG, e_qm, e_km, e_lk, eGl = jnp.exp(G), jnp.exp(G - Gm), jnp.exp(Gm - G), jnp.exp(Gl - G), jnp.exp(Gl)
            qt, kt, kdec, qe = q * e_qm, k * e_km, k * e_lk, q * eG
            sg = _sigmoid(gg)
            d_onw = dy_ref[sl, :] * (gg * sg)
            u = d_onw * gnw
            o = o_ref[sl, :]
            on, do = [], []
            for hh, ls in heads:
                r = lax.rsqrt(jnp.mean(o[:, ls] * o[:, ls], axis=-1, keepdims=True) + EPS)
                on.append(o[:, ls] * r)
                dgn_ref[hh] += jnp.sum(d_onw[:, ls] * on[hh], axis=0, keepdims=True)
                do.append(r * (u[:, ls] - on[hh] * jnp.mean(u[:, ls] * on[hh], axis=-1, keepdims=True)))
            dgg = dy_ref[sl, :] * (wide(on) * gnw) * (sg * (1.0 + gg * (1.0 - sg)))
            Sv = [sts_ref[hh, ci] for hh, _ in heads]
            dSv = [dst[hh] for hh, _ in heads]
            A = [jnp.where(low, _dg1(qt[:, ls], kt[:, ls], NT), 0.0) for _, ls in heads]
            dA = [jnp.where(low, _dg3(do[hh], vv[:, ls], NT), 0.0) for hh, ls in heads]
            dv = wide([_dg1(A[hh], do[hh], TN) + _dg1(kdec[:, ls], dSv[hh], NT) for hh, ls in heads])
            dq = wide([_dg3(dA[hh], kt[:, ls], NN) for hh, ls in heads]) * e_qm \
                + eG * wide([_dg3(do[hh], Sv[hh], NN) for hh, _ in heads])
            dk = wide([_dg3(dA[hh], qt[:, ls], TN) for hh, ls in heads]) * e_km \
                + e_lk * wide([_dg3(vv[:, ls], dSv[hh], NN) for hh, ls in heads])
            s_end = [Sv[hh] * eGl[:, ls] + _dg3(vv[:, ls], kdec[:, ls], TN) for hh, ls in heads]
            dgl = wide([jnp.sum(dSv[hh] * s_end[hh], axis=0, keepdims=True) for hh, _ in heads])
            for hh, ls in heads:
                dst[hh] = dSv[hh] * eGl[:, ls] + _dg1(do[hh], qe[:, ls], TN)
            dG = q * dq - k * dk + jnp.where(rows_w == C - 1, dgl, 0.0)
            dlogf = _tri_dot(triu, dG) - f * dk
            dlf_f = dlogf / f
            dlb_acc[...] += jnp.sum(dlf_f * (1.0 - sig), axis=0, keepdims=True)
            dp_ref[0, sl, :] = (dq * (sq * (1.0 + qr * (1.0 - sq)))).astype(BF)
            dp_ref[1, sl, :] = (dlf_f * (1.0 - lb_all) * sig * (1.0 - sig)).astype(BF)
            dp_ref[2, sl, :] = dv.astype(BF)
            dp_ref[3, sl, :] = dgg.astype(BF)
            return carry

        lax.fori_loop(0, nch, chunk, 0)
        sel = jnp.where(lax.broadcasted_iota(jnp.int32, (3, W), 0) == 0, 1.0, 0.0)
        dlb_ref[...] = lb_all * (sel - p3) * dlb_acc[...]

        if fused:
            @pl.when((pl.program_id(0) == ng - 1) & (n == nb - 1))
            def _():
                for cp in copies:
                    cp.wait()

    def col(p):
        return pl.BlockSpec((T, W), lambda h, n: (nb - 1 - n, p * ng + h))

    blk = pl.BlockSpec((T, W), lambda h, n: (nb - 1 - n, h))
    in_specs = [col(0), col(1), col(2), col(3),
                pl.BlockSpec((3, W), lambda h, n: (0, h)), pl.BlockSpec((1, LANES), lambda h, n: (0, 0)),
                blk, pl.BlockSpec((HB, nch, LANES, LANES), lambda h, n: (h, nb - 1 - n, 0, 0)), blk]
    out_specs = [pl.BlockSpec((4, T, W), lambda h, n: (0, nb - 1 - n, h)),
                 pl.BlockSpec((3, W), lambda h, n: (0, h)),
                 pl.BlockSpec((HB, 1, LANES), lambda h, n: (h, 0, 0))]
    out_shape = [jax.ShapeDtypeStruct((4, S, D), BF), jax.ShapeDtypeStruct((3, D), F32),
                 jax.ShapeDtypeStruct((H, 1, LANES), F32)]
    scratch = [pltpu.VMEM((HB, LANES, LANES), F32), pltpu.VMEM((1, W), F32)]
    args = [proj, proj, proj, proj, hg_lb, gn, o_all, states, dy]
    if fused:
        in_specs.append(HBM)
        out_specs.append(HBM)
        out_shape.append(jax.ShapeDtypeStruct((3,) + part.shape[1:], part.dtype))
        scratch += [pltpu.SemaphoreType.DMA((3,)), pltpu.SemaphoreType.DMA((3,))]
        args.append(part)
    return pl.pallas_call(
        body, name=name, grid=(ng, nb), in_specs=in_specs, out_specs=out_specs, out_shape=out_shape,
        scratch_shapes=scratch, compiler_params=_cp("arbitrary", "arbitrary"),
    )(*args)


def _log_sigmoid(u):
    return jnp.minimum(u, 0.0) - jnp.log(1.0 + jnp.exp(-jnp.abs(u)))


def _lane_put(base, lane, first, pieces):
    for n, p in enumerate(pieces):
        base = jnp.where(lane == first + n, p, base)
    return base


def _fox_cumsum(proj, bf_pad, *, name):
    S = proj.shape[0]
    D = proj.shape[1] // 5
    T = _pick(S, 256, 8)

    def body(fz_ref, b_ref, f_ref, carry):
        @pl.when(pl.program_id(0) == 0)
        def _():
            carry[...] = jnp.zeros_like(carry)

        logf = _log_sigmoid(fz_ref[...] + b_ref[...])
        tri = jnp.where(lax.broadcasted_iota(jnp.int32, (T, T), 0) >= lax.broadcasted_iota(jnp.int32, (T, T), 1),
                        1.0, 0.0).astype(BF)
        fv = _tri_dot(tri, logf) + carry[...]
        f_ref[...] = fv
        carry[...] = _row_of(fv, lax.broadcasted_iota(jnp.int32, (T, LANES), 0), T - 1)

    return pl.pallas_call(
        body, name=name, grid=(S // T,),
        in_specs=[pl.BlockSpec((T, LANES), lambda i: (i, 4 * D // LANES)), pl.BlockSpec((1, LANES), lambda i: (0, 0))],
        out_specs=pl.BlockSpec((T, LANES), lambda i: (i, 0)),
        out_shape=jax.ShapeDtypeStruct((S, LANES), F32),
        scratch_shapes=[pltpu.VMEM((1, LANES), F32)],
        compiler_params=_cp("arbitrary"),
    )(proj, bf_pad)


def _pair_stats(sq, lo):
    del lo
    a = lax.broadcasted_iota(jnp.int32, (LANES, LANES), 0) < FOX_DH
    b = lax.broadcasted_iota(jnp.int32, (LANES, LANES), 1) < FOX_DH
    avg = jnp.where(a == b, 1.0 / FOX_DH, 0.0).astype(BF)
    hi, mid, low = _split3(sq)
    return _dot(hi, avg) + _dot(mid, avg) + _dot(low, avg)


def _fox_prep(proj, fcum, qw2, kw2, *, name):
    S = proj.shape[0]
    D = proj.shape[1] // 5
    HP = D // LANES
    T = _pick(S, 512, 16)

    def body(q_ref, k_ref, v_ref, f_ref, qw_ref, kw_ref, qa_ref, ka_ref, va_ref, vt_ref):
        hp = pl.program_id(1)
        lane = lax.broadcasted_iota(jnp.int32, (T, LANES), 1)
        lo = lane < FOX_DH
        qv, kv, vv, fv = q_ref[...], k_ref[...], v_ref[...], f_ref[...]
        qn = qv * lax.rsqrt(_pair_stats(qv * qv, lo) + EPS) * qw_ref[...] * (0.125 * LOG2E)
        kn = kv * lax.rsqrt(_pair_stats(kv * kv, lo) + EPS) * kw_ref[...]
        ones_q = jnp.where((lane >= 67) & (lane <= 69), 1.0, 0.0)
        ones_k = jnp.where(((lane >= 64) & (lane <= 66)) | ((lane >= 70) & (lane <= 72)), 1.0, 0.0)
        ones_v = jnp.where((lane >= 64) & (lane <= 66), 1.0, 0.0)
        for hh in range(2):
            fh = jnp.sum(jnp.where(lane == 2 * hp + hh, fv, 0.0), axis=-1, keepdims=True) * LOG2E
            pieces = [p.astype(F32) for p in _split3(fh)]

            def half(x):
                return jnp.where(lo, x if hh == 0 else pltpu.roll(x, FOX_DH, 1), 0.0)

            qa_ref[hh] = _lane_put(half(qn) + ones_q, lane, 64, pieces).astype(BF)
            ka_ref[hh] = _lane_put(half(kn) + ones_k, lane, 67, [-p for p in pieces]).astype(BF)
            va = half(vv) + ones_v
            va_ref[hh] = va.astype(BF)
            vt_ref[hh] = va.T.astype(BF)

    def part(p):
        return pl.BlockSpec((T, LANES), lambda i, hp: (i, p * HP + hp))

    vec = pl.BlockSpec((1, LANES), lambda i, hp: (0, 0))
    aug = pl.BlockSpec((2, T, LANES), lambda i, hp: (hp, i, 0))
    return pl.pallas_call(
        body, name=name, grid=(S // T, HP),
        in_specs=[part(0), part(1), part(2), pl.BlockSpec((T, LANES), lambda i, hp: (i, 0)), vec, vec],
        out_specs=[aug, aug, aug, pl.BlockSpec((2, LANES, T), lambda i, hp: (hp, 0, i))],
        out_shape=[jax.ShapeDtypeStruct((2 * HP, S, LANES), BF)] * 3 + [jax.ShapeDtypeStruct((2 * HP, LANES, S), BF)],
        compiler_params=_cp("parallel", "arbitrary"),
    )(proj, proj, proj, fcum, qw2, kw2)


def _fox_block(S):
    return _pick(S, 256, 16)


def _fox_skip_bounds(fcum, qn_w, kn_w, nheads):
    S = fcum.shape[0]
    B = _fox_block(S)
    qk = 8.0 * LOG2E * 1.02 * jnp.max(jnp.abs(qn_w)) * jnp.max(jnp.abs(kn_w))
    thresh = -(2.0 * qk + 160.0)
    f2 = fcum[:, :nheads] * LOG2E
    first, last = f2[0::B], f2[B - 1::B]
    nb = S // B
    blk = jnp.arange(nb)
    dead = (first[0::2, None, :] - last[None, :, :]) < thresh
    jmin = jnp.sum(dead & (blk[None, :, None] < 2 * jnp.arange(nb // 2)[:, None, None]), axis=1)
    live = (first[:, None, :] - last[None, :, :]) >= thresh
    imax = blk[:, None] + jnp.sum(live & (blk[:, None, None] > blk[None, :, None]), axis=0)
    return jmin.T.astype(jnp.int32), imax.T.astype(jnp.int32)


def _fox_fwd(jmin, qa, ka, vat, proj, *, name):
    H, S, _ = qa.shape
    HP = H // 2
    D = HP * LANES
    B = _fox_block(S)
    BQ = 2 * B
    nq = S // BQ

    def body(jmin_ref, q_ref, k_ref, vt_ref, g_ref, y_ref, o_ref, q2_ref):
        hp, i = pl.program_id(0), pl.program_id(1)
        lane = lax.broadcasted_iota(jnp.int32, (BQ, LANES), 1)
        lo = lane < FOX_DH
        in_stat = (lane >= 70) & (lane <= 75)
        causal = lax.broadcasted_iota(jnp.int32, (BQ, BQ), 0) <= lax.broadcasted_iota(jnp.int32, (BQ, BQ), 1)
        row = lax.broadcasted_iota(jnp.int32, (LANES, BQ), 0)
        m0, acc0 = jnp.full((1, BQ), -jnp.inf, F32), jnp.zeros((LANES, BQ), F32)
        outs = []
        for hh in range(2):
            qb = q_ref[hh]

            def block(j, carry, masked=False):
                m, acc = carry
                sl = pl.ds(pl.multiple_of(j * BQ, BQ), BQ)
                st = _dg(k_ref[hh, sl, :], qb, NT)
                if masked:
                    st = jnp.where(causal, st, -jnp.inf)
                m_new = jnp.maximum(m, jnp.ceil(jnp.max(st, axis=0, keepdims=True)))
                p = jnp.exp2(st - m_new).astype(BF)
                return m_new, acc * jnp.exp2(m - m_new) + _dot(vt_ref[hh, :, sl], p)

            carry = lax.fori_loop(jmin_ref[2 * hp + hh, i] // 2, i, block, (m0, acc0))
            m, acc = block(i, carry, masked=True)
            linv = 1.0 / jnp.sum(jnp.where(row == FOX_DH, acc, 0.0), axis=0, keepdims=True)
            tile = acc * linv
            for n, piece in enumerate(_split3(m) + _split3(linv)):
                tile = jnp.where(row == 70 + n, piece.astype(F32), tile)
            tile = tile.T
            outs.append(tile)
            q2_ref[hh] = jnp.where(in_stat, jnp.where(lane <= 72, -tile, tile), qb.astype(F32)).astype(BF)
        o = jnp.where(lo, outs[0], pltpu.roll(outs[1], FOX_DH, 1))
        o_ref[...] = o
        y_ref[...] = (o * _sigmoid(g_ref[...])).astype(BF)

    blk = pl.BlockSpec((BQ, LANES), lambda hp, i, jm: (i, hp))
    qblk = pl.BlockSpec((2, BQ, LANES), lambda hp, i, jm: (hp, i, 0))
    full = pl.BlockSpec((2, S, LANES), lambda hp, i, jm: (hp, 0, 0))
    full_t = pl.BlockSpec((2, LANES, S), lambda hp, i, jm: (hp, 0, 0))
    return pl.pallas_call(
        body, name=name,
        grid_spec=pltpu.PrefetchScalarGridSpec(
            num_scalar_prefetch=1, grid=(HP, nq),
            in_specs=[qblk, full, full_t, pl.BlockSpec((BQ, LANES), lambda hp, i, jm: (i, 3 * HP + hp))],
            out_specs=[blk, blk, qblk]),
        out_shape=[jax.ShapeDtypeStruct((S, D), BF), jax.ShapeDtypeStruct((S, D), F32),
                   jax.ShapeDtypeStruct((H, S, LANES), BF)],
        compiler_params=_cp("parallel", "arbitrary"),
    )(jmin, qa, ka, vat, proj)


def _fox_bwd_prep(dy, o, proj, q2, *, name):
    S, D = dy.shape
    HP = D // LANES
    T = _pick(S, 512, 16)

    def body(dy_ref, o_ref, g_ref, q2_ref, da_ref):
        lane = lax.broadcasted_iota(jnp.int32, (T, LANES), 1)
        lo = lane < FOX_DH
        in_linv = (lane >= 73) & (lane <= 75)
        linv = [jnp.sum(jnp.where(in_linv, q2_ref[hh].astype(F32), 0.0), axis=-1, keepdims=True) for hh in range(2)]
        u = (dy_ref[...] * _sigmoid(g_ref[...]) * jnp.where(lo, linv[0], linv[1])).astype(BF).astype(F32)
        prod = u * o_ref[...]
        d_lo = jnp.sum(jnp.where(lo, prod, 0.0), axis=-1, keepdims=True)
        d_hi = jnp.sum(jnp.where(lo, 0.0, prod), axis=-1, keepdims=True)
        for hh, delta in enumerate((d_lo, d_hi)):
            base = jnp.where(lo, u if hh == 0 else pltpu.roll(u, FOX_DH, 1), 0.0)
            da_ref[hh] = _lane_put(base, lane, 64, [-(p.astype(F32)) for p in _split3(delta)]).astype(BF)

    blk = pl.BlockSpec((T, LANES), lambda i, hp: (i, hp))
    aug = pl.BlockSpec((2, T, LANES), lambda i, hp: (hp, i, 0))
    return pl.pallas_call(
        body, name=name, grid=(S // T, HP),
        in_specs=[blk, blk, pl.BlockSpec((T, LANES), lambda i, hp: (i, 3 * HP + hp)), aug],
        out_specs=aug,
        out_shape=jax.ShapeDtypeStruct((2 * HP, S, LANES), BF),
        compiler_params=_cp("parallel", "arbitrary"),
    )(dy, o, proj, q2)


def _fox_bwd(imax, q2, ka, va, doa, *, name):
    H, S, _ = q2.shape
    B = _fox_block(S)
    nb = S // B

    def body(imax_ref, q_ref, do_ref, k_ref, v_ref, dq_ref, dk_ref, dv_ref, cs_ref):
        j = pl.program_id(1)
        end = imax_ref[pl.program_id(0), j] + 1

        @pl.when(j == 0)
        def _():
            dq_ref[...] = jnp.zeros_like(dq_ref)

        kb, vb = k_ref[...], v_ref[...]

        def step(i, carry, nblk=1):
            dk_acc, dv_acc, cs_acc = carry
            rows = nblk * B
            sl = pl.ds(pl.multiple_of(i * B, B), rows)
            qb, dob = q_ref[sl, :], do_ref[sl, :]
            s = _dg(qb, kb, NT)
            ahead = lax.broadcasted_iota(jnp.int32, (rows, B), 0) - lax.broadcasted_iota(jnp.int32, (rows, B), 1)
            pb = jnp.exp2(jnp.where(ahead >= (j - i) * B, s, -jnp.inf)).astype(BF)
            ds = pb.astype(F32) * _dg(dob, vb, NT)
            dsb = ds.astype(BF)
            cs_acc = cs_acc + jnp.sum(ds.reshape(rows // 8, 8, B), axis=0)
            dv_acc = dv_acc + _dg(pb, dob, TN)
            dk_acc = dk_acc + _dg(dsb, qb, TN)
            dq_ref[sl, :] += _dot(dsb, kb)
            return dk_acc, dv_acc, cs_acc

        zero = jnp.zeros((B, LANES), F32)
        carry = (zero, zero, jnp.zeros((8, B), F32))
        pos = j
        for U in FOX_BWD_TILES:
            n = (end - pos) // U
            carry = lax.fori_loop(0, n, lambda ii, c, pos=pos, U=U: step(pos + U * ii, c, nblk=U), carry)
            pos = pos + U * n
        dk_acc, dv_acc, cs_acc = carry
        dk_ref[...] = dk_acc
        dv_ref[...] = dv_acc
        cs_ref[...] = jnp.sum(cs_acc, axis=0, keepdims=True)

    full = pl.BlockSpec((None, S, LANES), lambda h, j, im: (h, 0, 0))
    blk = pl.BlockSpec((None, B, LANES), lambda h, j, im: (h, j, 0))
    return pl.pallas_call(
        body, name=name,
        grid_spec=pltpu.PrefetchScalarGridSpec(
            num_scalar_prefetch=1, grid=(H, nb),
            in_specs=[full, full, blk, blk],
            out_specs=[full, blk, blk, pl.BlockSpec((None, 1, B), lambda h, j, im: (h, 0, j))]),
        out_shape=[jax.ShapeDtypeStruct((H, S, LANES), F32)] * 3 + [jax.ShapeDtypeStruct((H, 1, S), F32)],
        compiler_params=_cp("parallel", "arbitrary"),
    )(imax, q2, doa, ka, va)


def _fox_bwd_post(dqa, dka, dva, proj, dy, o, qw2, kw2, *, name):
    S, D = dy.shape
    HP = D // LANES
    T = _pick(S, 512, 16)

    def body(dq_ref, dk_ref, dv_ref, q_ref, k_ref, g_ref, dy_ref, o_ref, qw_ref, kw_ref, dp_ref, dqw_ref, dkw_ref):
        @pl.when((pl.program_id(0) == 0) & (pl.program_id(1) == 0))
        def _():
            dqw_ref[...] = jnp.zeros_like(dqw_ref)
            dkw_ref[...] = jnp.zeros_like(dkw_ref)

        lane = lax.broadcasted_iota(jnp.int32, (T, LANES), 1)
        lo = lane < FOX_DH

        def pair(ref):
            return jnp.where(lo, ref[0], pltpu.roll(ref[1], FOX_DH, 1))

        def norm_bwd(xv, w, dyn, dw_ref):
            r = lax.rsqrt(_pair_stats(xv * xv, lo) + EPS)
            xr = xv * r
            dw_ref[...] += jnp.sum(dyn * xr, axis=0, keepdims=True)
            u = dyn * w
            return r * (u - xr * _pair_stats(u * xr, lo))

        dp_ref[0] = norm_bwd(q_ref[...], qw_ref[...], pair(dq_ref) * 0.125, dqw_ref).astype(BF)
        dp_ref[1] = norm_bwd(k_ref[...], kw_ref[...], pair(dk_ref) * (1.0 / LOG2E), dkw_ref).astype(BF)
        dp_ref[2] = pair(dv_ref).astype(BF)
        sg = _sigmoid(g_ref[...])
        dp_ref[3] = (dy_ref[...] * o_ref[...] * sg * (1.0 - sg)).astype(BF)

    def part(p):
        return pl.BlockSpec((T, LANES), lambda i, hp: (i, p * HP + hp))

    aug = pl.BlockSpec((2, T, LANES), lambda i, hp: (hp, i, 0))
    blk = pl.BlockSpec((T, LANES), lambda i, hp: (i, hp))
    vec = pl.BlockSpec((1, LANES), lambda i, hp: (0, 0))
    return pl.pallas_call(
        body, name=name, grid=(S // T, HP),
        in_specs=[aug, aug, aug, part(0), part(1), part(3), blk, blk, vec, vec],
        out_specs=[pl.BlockSpec((4, T, LANES), lambda i, hp: (0, i, hp)), vec, vec],
        out_shape=[jax.ShapeDtypeStruct((5, S, D), BF), jax.ShapeDtypeStruct((1, LANES), F32),
                   jax.ShapeDtypeStruct((1, LANES), F32)],
        compiler_params=_cp("arbitrary", "arbitrary"),
    )(dqa, dka, dva, proj, proj, proj, dy, o, qw2, kw2)


def _fox_dfz(colsum, nheads, proj, bf_pad, dproj, *, name):
    S = colsum.shape[0]
    H = nheads
    D = dproj.shape[2]
    T = _pick(S, 256, 16)
    nb = S // T

    def body(cs_ref, fz_ref, b_ref, _, dp_ref, db_ref, carry):
        @pl.when(pl.program_id(0) == 0)
        def _():
            carry[...] = jnp.zeros_like(carry)
            db_ref[...] = jnp.zeros_like(db_ref)

        lane = lax.broadcasted_iota(jnp.int32, (T, LANES), 1)
        df = -cs_ref[...]
        triu = jnp.where(lax.broadcasted_iota(jnp.int32, (T, T), 0) <= lax.broadcasted_iota(jnp.int32, (T, T), 1),
                         1.0, 0.0).astype(BF)
        dlogf = _tri_dot(triu, df) + carry[...]
        carry[...] = _row_of(dlogf, lax.broadcasted_iota(jnp.int32, (T, LANES), 0), 0)
        dfz = jnp.where(lane < H, dlogf * _sigmoid(-(fz_ref[...] + b_ref[...])), 0.0)
        db_ref[...] += jnp.sum(dfz, axis=0, keepdims=True)
        dp_ref[...] = jnp.zeros_like(dp_ref)
        dp_ref[:, 0:LANES] = dfz.astype(BF)

    return pl.pallas_call(
        body, name=name, grid=(nb,),
        in_specs=[pl.BlockSpec((T, LANES), lambda i: (nb - 1 - i, 0)),
                  pl.BlockSpec((T, LANES), lambda i: (nb - 1 - i, 4 * D // LANES)),
                  pl.BlockSpec((1, LANES), lambda i: (0, 0)),
                  pl.BlockSpec(memory_space=pl.ANY)],
        out_specs=[pl.BlockSpec((None, T, D), lambda i: (4, nb - 1 - i, 0)), pl.BlockSpec((1, LANES), lambda i: (0, 0))],
        out_shape=[jax.ShapeDtypeStruct(dproj.shape, BF), jax.ShapeDtypeStruct((1, LANES), F32)],
        scratch_shapes=[pltpu.VMEM((1, LANES), F32)],
        input_output_aliases={3: 0},
        compiler_params=_cp("arbitrary"),
    )(colsum, proj, bf_pad, dproj)


def _mod_fwd(c16, w, b, *, name):
    L, D, N = w.shape
    tn = _pick(N, 512)

    def body(c_ref, w_ref, b_ref, o_ref):
        cv = c_ref[...]
        ca = (cv * _sigmoid(cv)).astype(BF)
        o_ref[...] = _dot(ca, w_ref[...].astype(BF)) + b_ref[...]

    return pl.pallas_call(
        body, name=name, grid=(L, N // tn),
        in_specs=[pl.BlockSpec((16, D), lambda l, j: (0, 0)), pl.BlockSpec((None, D, tn), lambda l, j: (l, 0, j)),
                  pl.BlockSpec((None, 1, tn), lambda l, j: (l, 0, j))],
        out_specs=pl.BlockSpec((None, 16, tn), lambda l, j: (l, 0, j)),
        out_shape=jax.ShapeDtypeStruct((L, 16, N), F32),
        compiler_params=_cp("parallel", "arbitrary"),
    )(c16, w, b)


def _mod_bwd(c16, dmod, *, name):
    L, _, N = dmod.shape
    D = c16.shape[1]
    tn = _pick(N, 512)

    def body(c_ref, d_ref, o_ref):
        cv = c_ref[...]
        ca = (cv * _sigmoid(cv)).astype(BF)
        o_ref[...] = _dg(ca, d_ref[...].astype(BF), TN)

    return pl.pallas_call(
        body, name=name, grid=(L, N // tn),
        in_specs=[pl.BlockSpec((16, D), lambda l, j: (0, 0)), pl.BlockSpec((None, 16, tn), lambda l, j: (l, 0, j))],
        out_specs=pl.BlockSpec((None, D, tn), lambda l, j: (l, 0, j)),
        out_shape=jax.ShapeDtypeStruct((L, D, N), F32),
        compiler_params=_cp("parallel", "arbitrary"),
    )(c16, dmod)


def _adamw_math(w, g, m, v):
    m = ADAM_B1 * m + (1.0 - ADAM_B1) * g
    v = ADAM_B2 * v + (1.0 - ADAM_B2) * (g * g)
    m_hat = m / (1.0 - ADAM_B1 ** ADAM_STEP)
    v_hat = v / (1.0 - ADAM_B2 ** ADAM_STEP)
    return -ADAM_LR * (m_hat / (jnp.sqrt(v_hat) + ADAM_EPS) + ADAM_WD * w), m, v


def _adamw(w, g, m, v, *, g_at=None, name):
    R, C = w.shape
    row0 = 0 if g_at is None else g_at[1]
    tr = min(math.gcd(row0, 256) if row0 else 256, -(-R // 8) * 8)
    g0 = row0 // tr
    if g_at is None:
        g_spec = pl.BlockSpec((tr, C), lambda i: (i, 0))
    else:
        g_spec = pl.BlockSpec((None, tr, C), lambda i: (g_at[0], g0 + i, 0))

    def body(w_ref, g_ref, m_ref, v_ref, d_ref, mo_ref, vo_ref):
        d, mn, vn = _adamw_math(w_ref[...], g_ref[...], m_ref[...], v_ref[...])
        d_ref[...] = d
        mo_ref[...] = mn
        vo_ref[...] = vn

    blk = pl.BlockSpec((tr, C), lambda i: (i, 0))
    return pl.pallas_call(
        body, name=name, grid=(pl.cdiv(R, tr),),
        in_specs=[blk, g_spec, blk, blk],
        out_specs=[blk, blk, blk],
        out_shape=[jax.ShapeDtypeStruct((R, C), F32)] * 3,
        compiler_params=_cp("parallel"),
    )(w, g, m, v)


def _sum_parts(parts, *, name):
    P, R, C = parts.shape

    def body(p_ref, o_ref):
        acc = p_ref[0]
        for p in range(1, P):
            acc = acc + p_ref[p]
        o_ref[...] = acc

    return pl.pallas_call(
        body, name=name, grid=(1,),
        in_specs=[pl.BlockSpec((P, R, C), lambda i: (0, 0, 0))],
        out_specs=pl.BlockSpec((R, C), lambda i: (0, 0)),
        out_shape=jax.ShapeDtypeStruct((R, C), F32),
        compiler_params=_cp("arbitrary"),
    )(parts)


def _add_halves(g4, recv, c_idx, *, name):
    _, _, Rh, C = g4.shape
    tr = min(256, Rh)

    def body(c_ref, a_ref, b_ref, o_ref):
        o_ref[...] = (a_ref[...] + b_ref[...].astype(F32)).astype(BF)

    return pl.pallas_call(
        body, name=name,
        grid_spec=pltpu.PrefetchScalarGridSpec(
            num_scalar_prefetch=1, grid=(4, pl.cdiv(Rh, tr)),
            in_specs=[pl.BlockSpec((None, None, tr, C), lambda j, r, c: (j, c[0], r, 0)),
                      pl.BlockSpec((None, tr, C), lambda j, r, c: (j, r, 0))],
            out_specs=pl.BlockSpec((None, tr, C), lambda j, r, c: (j, r, 0))),
        out_shape=jax.ShapeDtypeStruct((4, Rh, C), BF),
        compiler_params=_cp("parallel", "arbitrary"),
    )(c_idx, g4, recv)


def _add_four(g4, from_sibling, from_chips, pos, *, name):
    _, _, Rh, C = g4.shape
    tr = min(256, Rh)

    def body(p_ref, a_ref, s_ref, b_ref, o_ref):
        own = a_ref[...] + s_ref[...].astype(F32)
        o_ref[...] = ((own + b_ref[0].astype(F32)) + b_ref[1].astype(F32)) + b_ref[2].astype(F32)

    return pl.pallas_call(
        body, name=name,
        grid_spec=pltpu.PrefetchScalarGridSpec(
            num_scalar_prefetch=1, grid=(pl.cdiv(Rh, tr),),
            in_specs=[pl.BlockSpec((None, None, tr, C), lambda r, p: (p[0], p[1], r, 0)),
                      pl.BlockSpec((None, tr, C), lambda r, p: (p[0], r, 0)),
                      pl.BlockSpec((3, tr, C), lambda r, p: (0, r, 0))],
            out_specs=pl.BlockSpec((None, tr, C), lambda r, p: (p[1], r, 0))),
        out_shape=jax.ShapeDtypeStruct((2, Rh, C), F32),
        compiler_params=_cp("arbitrary"),
    )(pos, g4, from_sibling, from_chips)


HBM = pl.BlockSpec(memory_space=pltpu.HBM)


def _mesh_pos():
    return lax.axis_index("x"), lax.axis_index("y"), lax.axis_index("c")


def _other_chips(x, y):
    return [(1 - x, y), (x, 1 - y), (1 - x, 1 - y)]


def _allgather_small(xs, *, name):
    m_per, n = xs.shape

    def body(x_ref, out_ref, send_sems, recv_sems, local_sem):
        x, y, c = _mesh_pos()
        me, sibling = (x, y, c), (x, y, 1 - c)
        chips = _other_chips(x, y)

        def rows(px, py, pc):
            return out_ref.at[pl.ds((4 * px + 2 * py + pc) * m_per, m_per), :]

        def copy(k, block, to, src=None):
            return pltpu.make_async_remote_copy(
                src_ref=rows(*block) if src is None else src, dst_ref=rows(*block),
                send_sem=send_sems.at[k], recv_sem=recv_sems.at[k], device_id=to, device_id_type=MESH)

        mine = pltpu.make_async_copy(x_ref, rows(*me), local_sem)
        mine.start()
        first = [copy(0, me, sibling, src=x_ref)]
        first += [copy(1 + j, me, (*chip, c), src=x_ref) for j, chip in enumerate(chips)]
        for cp in first:
            cp.start()
        passed = [copy(4 + j, (*chip, c), sibling) for j, chip in enumerate(chips)]
        for j, chip in enumerate(chips):
            copy(1 + j, (*chip, c), me).wait_recv()
            passed[j].start()
        copy(0, sibling, me).wait_recv()
        for j, chip in enumerate(chips):
            copy(4 + j, (*chip, 1 - c), me).wait_recv()
        for cp in first + passed:
            cp.wait_send()
        mine.wait()

    return pl.pallas_call(
        body, name=name,
        out_shape=jax.ShapeDtypeStruct((N_DEV * m_per, n), xs.dtype),
        in_specs=[pl.BlockSpec(memory_space=pltpu.VMEM)],
        out_specs=pl.BlockSpec(memory_space=pltpu.VMEM),
        scratch_shapes=[pltpu.SemaphoreType.DMA((7,)), pltpu.SemaphoreType.DMA((7,)), pltpu.SemaphoreType.DMA],
    )(xs)


def _chip_slab_copies(s_ref, out_ref, send_sems, recv_sems):
    R = s_ref.shape[0]
    Rh = R // 2
    x, y, c = _mesh_pos()
    me, sibling = (x, y, c), (x, y, 1 - c)
    chips = _other_chips(x, y)

    def half(px, py, pc):
        return out_ref.at[2 * px + py, pl.ds(pc * Rh, Rh), :]

    def copy(k, block, to, src=None):
        return pltpu.make_async_remote_copy(
            src_ref=half(*block) if src is None else src, dst_ref=half(*block),
            send_sem=send_sems.at[k], recv_sem=recv_sems.at[k], device_id=to, device_id_type=MESH)

    first = [copy(j, me, (*chip, c), src=s_ref.at[pl.ds(c * Rh, Rh), :]) for j, chip in enumerate(chips)]
    passed = [copy(3 + j, (*chip, c), sibling) for j, chip in enumerate(chips)]
    landed = [copy(j, (*chip, c), me) for j, chip in enumerate(chips)]
    from_sibling = [copy(3 + j, (*chip, 1 - c), me) for j, chip in enumerate(chips)]
    return first, passed, landed, from_sibling


def _allgather_chip_slabs(slab, *, name):
    R, C = slab.shape

    def body(s_ref, out_ref, send_sems, recv_sems):
        first, passed, landed, from_sibling = _chip_slab_copies(s_ref, out_ref, send_sems, recv_sems)
        for cp in first:
            cp.start()
        for arrived, onward in zip(landed, passed):
            arrived.wait_recv()
            onward.start()
        for cp in from_sibling:
            cp.wait_recv()
        for cp in first + passed:
            cp.wait_send()

    return pl.pallas_call(
        body, name=name,
        out_shape=jax.ShapeDtypeStruct((N_CHIPS, R, C), slab.dtype),
        in_specs=[HBM], out_specs=HBM,
        scratch_shapes=[pltpu.SemaphoreType.DMA((6,)), pltpu.SemaphoreType.DMA((6,))],
    )(slab)


def _swap_halves(mine, *, name):
    def body(g_ref, out_ref, send_sems, recv_sems):
        x, y, c = _mesh_pos()
        copies = [pltpu.make_async_remote_copy(
            src_ref=g_ref.at[j], dst_ref=out_ref.at[j], send_sem=send_sems.at[j], recv_sem=recv_sems.at[j],
            device_id=(x, y, 1 - c), device_id_type=MESH) for j in range(N_CHIPS)]
        for cp in copies:
            cp.start()
        for cp in copies:
            cp.wait()

    return pl.pallas_call(
        body, name=name,
        out_shape=jax.ShapeDtypeStruct(mine.shape, mine.dtype),
        in_specs=[HBM], out_specs=HBM,
        scratch_shapes=[pltpu.SemaphoreType.DMA((N_CHIPS,)), pltpu.SemaphoreType.DMA((N_CHIPS,))],
    )(mine)


def _scatter_copies(p_ref, out_ref, send_sems, recv_sems):
    x, y, c = _mesh_pos()
    return [pltpu.make_async_remote_copy(
        src_ref=p_ref.at[2 * px + py], dst_ref=out_ref.at[j], send_sem=send_sems.at[j], recv_sem=recv_sems.at[j],
        device_id=(px, py, c), device_id_type=MESH) for j, (px, py) in enumerate(_other_chips(x, y))]


def _scatter_partials(part, *, name):
    _, Rh, C = part.shape

    def body(p_ref, out_ref, send_sems, recv_sems):
        copies = _scatter_copies(p_ref, out_ref, send_sems, recv_sems)
        for cp in copies:
            cp.start()
        for cp in copies:
            cp.wait()

    return pl.pallas_call(
        body, name=name,
        out_shape=jax.ShapeDtypeStruct((3, Rh, C), part.dtype),
        in_specs=[HBM], out_specs=HBM,
        scratch_shapes=[pltpu.SemaphoreType.DMA((3,)), pltpu.SemaphoreType.DMA((3,))],
    )(part)


def _join_halves(buf, *, name):
    def body(b_ref, out_ref, send_sem, recv_sem):
        x, y, c = _mesh_pos()
        cp = pltpu.make_async_remote_copy(
            src_ref=b_ref.at[c], dst_ref=out_ref.at[c], send_sem=send_sem, recv_sem=recv_sem,
            device_id=(x, y, 1 - c), device_id_type=MESH)
        cp.start()
        cp.wait()

    return pl.pallas_call(
        body, name=name,
        out_shape=jax.ShapeDtypeStruct(buf.shape, buf.dtype),
        in_specs=[HBM], out_specs=HBM, input_output_aliases={0: 0},
        scratch_shapes=[pltpu.SemaphoreType.DMA, pltpu.SemaphoreType.DMA],
    )(buf)


def _pad_rows(a, mult):
    pad = (-a.shape[0]) % mult
    return a if pad == 0 else jnp.pad(a, ((0, pad),) + ((0, 0),) * (a.ndim - 1))


def _local_step(x, target, mod, wts, small, slab_rest=None, unpack_rest=None, reduce_early=None, grad_slab=None):
    S, D = x.shape
    HP = D // LANES
    row = lambda v: v.reshape(1, -1)
    msplit = [[row(mod[i, k * D:(k + 1) * D]) for k in range(6)] for i in range(2)]
    gw, gs = {}, {}
    dmod = [[None] * 6 for _ in range(2)]
    slab, where = grad_slab if grad_slab is not None else (None, {})

    def dw(key, a, b, name):
        nonlocal slab
        if key in where:
            slab = _matmul_tn(a, b, name=name, into=(slab,) + where[key])
        else:
            gw[key] = _matmul_tn(a, b, name=name)

    sh1, sc1, g1, sh2, sc2, g2 = msplit[0]
    n1w0, n2w0 = row(small["norm1_w"][0]), row(small["norm2_w"][0])
    proj0, h1_0 = _ln_matmul(x, n1w0, sc1, sh1, wts["hg_w_in"], relu2=False, name="hg_in_proj")
    gn = small["hg_gn_w"].reshape(1, LANES)
    ypre0, o0, states, *gathered = _hg_fwd(proj0, small["hg_lb"], gn, slab_rest, name="hg_fwd")
    if slab_rest is not None:
        wts = {**wts, **unpack_rest(gathered[0])}
    x1, ymix0 = _matmul_resid(ypre0, wts["hg_w_out"], x, g1, name="hg_out_proj")
    a0, u0, h2_0 = _ln_matmul(x1, n2w0, sc2, sh2, wts["mlp_w1_0"], relu2=True, name="mlp0_up")
    x2, ymlp0 = _matmul_resid(u0, wts["mlp_w2_0"], x1, g2, name="mlp0_down")

    sh1b, sc1b, g1b, sh2b, sc2b, g2b = msplit[1]
    n1w1, n2w1 = row(small["norm1_w"][1]), row(small["norm2_w"][1])
    proj1, h1_1 = _ln_matmul(x2, n1w1, sc1b, sh1b, wts["fox_w_in"], relu2=False, name="fox_in_proj")
    nheads = 2 * HP
    bf_pad = jnp.pad(small["fox_b_f"].reshape(1, nheads), ((0, 0), (0, LANES - nheads)))
    qw2 = jnp.tile(small["fox_qn_w"].reshape(1, FOX_DH), (1, 2))
    kw2 = jnp.tile(small["fox_kn_w"].reshape(1, FOX_DH), (1, 2))
    fcum = _fox_cumsum(proj1, bf_pad, name="fox_cumsum")
    qa, ka, va, vat = _fox_prep(proj1, fcum, qw2, kw2, name="fox_prep")
    jmin, imax = _fox_skip_bounds(fcum, small["fox_qn_w"], small["fox_kn_w"], nheads)
    ypre1, o1, q2 = _fox_fwd(jmin, qa, ka, vat, proj1, name="fox_fwd")
    x3, ymix1 = _matmul_resid(ypre1, wts["fox_w_out"], x2, g1b, name="fox_out_proj")
    a1, u1, h2_1 = _ln_matmul(x3, n2w1, sc2b, sh2b, wts["mlp_w1_1"], relu2=True, name="mlp1_up")
    x4, ymlp1 = _matmul_resid(u1, wts["mlp_w2_1"], x3, g2b, name="mlp1_down")

    loss, dx4, dfw = _loss_kernel(x4, row(small["final_w"]), target, name="loss")
    gs["final_w"] = dfw.reshape(-1)

    def mlp_bwd(i, dx_out, x_in, h2, a, u, ymlp, n2w, sc2_, g2_):
        dz, dm, dg2 = _gate_matmul_nt(dx_out, g2_, ymlp, wts[f"mlp_w2_{i}"], a, name=f"mlp{i}_down_bwd")
        dw(f"mlp_w2_{i}", u, dm[None], f"mlp{i}_dw2")
        dw(f"mlp_w1_{i}", h2, dz[None], f"mlp{i}_dw1")
        dx_in, dsc, dsh, dnw = _matmul_nt_lnbwd(dz[None], wts[f"mlp_w1_{i}"], x_in, n2w, sc2_, dx_out,
                                                name=f"mlp{i}_up_bwd")
        dmod[i][3], dmod[i][4], dmod[i][5] = dsh, dsc, dg2
        return dx_in, dnw

    dx3, dn2w1 = mlp_bwd(1, dx4, x3, h2_1, a1, u1, ymlp1, n2w1, sc2b, g2b)
    dyp1, dm1, dg1b = _gate_matmul_nt(dx3, g1b, ymix1, wts["fox_w_out"], None, name="fox_out_bwd")
    dw("fox_w_out", ypre1, dm1[None], "fox_dw_out")
    doa = _fox_bwd_prep(dyp1, o1, proj1, q2, name="fox_bwd_prep")
    dqa, dka, dva, colsum = _fox_bwd(imax, q2, ka, va, doa, name="fox_bwd")
    colsum = jnp.pad(colsum[:, 0, :].T, ((0, 0), (0, LANES - nheads)))
    dproj1, dqw, dkw = _fox_bwd_post(dqa, dka, dva, proj1, dyp1, o1, qw2, kw2, name="fox_bwd_post")
    dproj1, dbf = _fox_dfz(colsum, nheads, proj1, bf_pad, dproj1, name="fox_dfz")
    dw("fox_w_in", h1_1, dproj1, "fox_dw_in")
    dx2, dsc, dsh, dn1w1 = _matmul_nt_lnbwd(dproj1, wts["fox_w_in"], x2, n1w1, sc1b, dx3, name="fox_in_bwd")
    dmod[1][0], dmod[1][1], dmod[1][2] = dsh, dsc, dg1b
    gs["fox_qn_w"] = dqw[0, :FOX_DH] + dqw[0, FOX_DH:]
    gs["fox_kn_w"] = dkw[0, :FOX_DH] + dkw[0, FOX_DH:]
    gs["fox_b_f"] = dbf[0, :nheads]

    dx1, dn2w0 = mlp_bwd(0, dx2, x1, h2_0, a0, u0, ymlp0, n2w0, sc2, g2)
    dyp0, dm0, dg1 = _gate_matmul_nt(dx1, g1, ymix0, wts["hg_w_out"], None, name="hg_out_bwd")
    dw("hg_w_out", ypre0, dm0[None], "hg_dw_out")
    part, ctx = reduce_early(gw, slab) if reduce_early is not None else (None, None)
    dproj0, dlb, dgn, *from_chips = _hg_bwd(proj0, small["hg_lb"], gn, o0, states, dyp0, part, name="hg_bwd")
    early = (ctx, from_chips[0]) if reduce_early is not None else None
    dw("hg_w_in", h1_0, dproj0, "hg_dw_in")
    dx0, dsc, dsh, dn1w0 = _matmul_nt_lnbwd(dproj0, wts["hg_w_in"], x, n1w0, sc1, dx1, name="hg_in_bwd")
    dmod[0][0], dmod[0][1], dmod[0][2] = dsh, dsc, dg1
    gs["hg_lb"] = dlb
    gs["hg_gn_w"] = jnp.sum(dgn, axis=0)

    gs["norm1_w"] = jnp.concatenate([dn1w0, dn1w1], axis=0)
    gs["norm2_w"] = jnp.concatenate([dn2w0, dn2w1], axis=0)
    gs["dmod"] = jnp.stack([jnp.concatenate(dmod[i], axis=1)[0] for i in range(2)])
    return loss, dx0, gw, gs, early


def _pack_halves(layout):
    rh = -(-max(sum(a.shape[0] for _, a in half) for half in layout) // 16) * 16
    place, parts = {}, []
    for h, half in enumerate(layout):
        off = 0
        for n, a in half:
            place[n] = (h, off, a.shape[0])
            off += a.shape[0]
        parts.append(jnp.pad(jnp.concatenate([a.astype(BF) for _, a in half], axis=0), ((0, rh - off), (0, 0))))
    return jnp.concatenate(parts, axis=0), place, rh


SMALL_NAMES = ["norm1_w", "norm2_w", "hg_lb", "hg_gn_w", "fox_b_f", "fox_qn_w", "fox_kn_w", "final_w"]


def _pack_small(d, names):
    rows, offs, r0 = [], {}, 0
    for n in names:
        flat = d[n].reshape(-1)
        nr = -(-flat.shape[0] // LANES)
        rows.append(jnp.pad(flat, (0, nr * LANES - flat.shape[0])).reshape(nr, LANES))
        offs[n] = (r0, nr)
        r0 += nr
    return jnp.concatenate(rows, axis=0), offs


def _unpack_small(packed, offs, name, like):
    r0, nr = offs[name]
    return packed[r0:r0 + nr].reshape(-1)[:like.size].reshape(like.shape)


def kernel(x, c, w_mod, b_mod, norm1_w, norm2_w, hg_w_in, hg_w_out, hg_lb, hg_gn_w, fox_w_in, fox_b_f, fox_qn_w, fox_kn_w, fox_w_out, mlp_w1, mlp_w2, final_w, loss_target, m_w_mod, m_b_mod, m_norm1_w, m_norm2_w, m_hg_w_in, m_hg_w_out, m_hg_lb, m_hg_gn_w, m_fox_w_in, m_fox_b_f, m_fox_qn_w, m_fox_kn_w, m_fox_w_out, m_mlp_w1, m_mlp_w2, m_final_w, v_w_mod, v_b_mod, v_norm1_w, v_norm2_w, v_hg_w_in, v_hg_w_out, v_hg_lb, v_hg_gn_w, v_fox_w_in, v_fox_b_f, v_fox_qn_w, v_fox_kn_w, v_fox_w_out, v_mlp_w1, v_mlp_w2, v_final_w):
    S, D = x.shape[1], x.shape[2]
    nheads = D // FOX_DH
    ax, ay, ac = _mesh_pos()
    chip = 2 * ax + ay
    dev = 2 * chip + ac
    xs, tgt = x.reshape(S, D), loss_target.reshape(S, D)

    c_all = _allgather_small(_pad_rows(c.reshape(-1, LANES), 8), name="gather_c")
    c_all = c_all.reshape(N_DEV, -1)[:, :D]
    c16 = _pad_rows(c_all, 16)
    nmod = w_mod.shape[2]
    b_shard = lax.dynamic_slice_in_dim(b_mod, chip * nmod, nmod, axis=1)
    mod_shard = _mod_fwd(c16, w_mod, b_shard[:, None, :], name="mod_fwd")[:, :N_DEV]
    mod_all = _allgather_small(mod_shard.reshape(-1, LANES), name="gather_mod")
    mod_all = mod_all.reshape(N_CHIPS, 2, 2, N_DEV, nmod)[:, 0]
    mod = lax.dynamic_index_in_dim(mod_all, dev, axis=2, keepdims=False)
    mod = mod.transpose(1, 0, 2).reshape(2, N_CHIPS * nmod)

    fox_rows = fox_w_in.shape[2]
    col = lambda g: g.transpose(1, 0, 2).reshape(g.shape[1], -1)
    rowsh = lambda g: g.reshape(-1, g.shape[2])
    own = lambda g, s: lax.dynamic_update_index_in_dim(g, s, chip, 0)

    slab_in = hg_w_in[0].astype(BF)
    wts = {"hg_w_in": col(own(_allgather_chip_slabs(slab_in, name="gather_hg_w_in"), slab_in))}
    slab_rest, place_rest, rh_rest = _pack_halves(
        [[("mlp_w1", mlp_w1.reshape(2 * D, D)), ("hg_w_out", hg_w_out[0]), ("fox_w_out", fox_w_out[0])],
         [("mlp_w2", mlp_w2.reshape(2 * D, D)), ("fox_w_in", fox_w_in[0].reshape(fox_rows, D))]])

    def unpack_rest(gathered):
        gathered = own(gathered, slab_rest)

        def seg(n):
            h, off, rows = place_rest[n]
            return gathered[:, h * rh_rest + off:h * rh_rest + off + rows, :]

        w1 = seg("mlp_w1").reshape(N_CHIPS, 2, D, D)
        w2 = seg("mlp_w2").reshape(N_CHIPS, 2, D, D)
        fox_in = col(seg("fox_w_in").reshape(N_CHIPS, D, fox_rows))
        return {
            "hg_w_out": rowsh(seg("hg_w_out")), "fox_w_out": rowsh(seg("fox_w_out")),
            "mlp_w1_0": col(w1[:, 0]), "mlp_w1_1": col(w1[:, 1]), "mlp_w2_0": rowsh(w2[:, 0]), "mlp_w2_1": rowsh(w2[:, 1]),
            "fox_w_in": jnp.pad(fox_in, ((0, 0), (0, 5 * D - fox_in.shape[1]))),
        }

    small = {"norm1_w": norm1_w, "norm2_w": norm2_w, "hg_lb": hg_lb, "hg_gn_w": hg_gn_w, "fox_b_f": fox_b_f,
             "fox_qn_w": fox_qn_w, "fox_kn_w": fox_kn_w, "final_w": final_w}

    def uncol(g, n):
        return g.reshape(g.shape[0], N_CHIPS, n).transpose(1, 0, 2)

    pos = jnp.stack([chip, ac])

    def swap_and_add(g4, tag):
        to_sibling = lax.dynamic_index_in_dim(g4, 1 - ac, axis=1, keepdims=False).astype(BF)
        from_sibling = _swap_halves(to_sibling, name=f"rs_swap_{tag}")
        return from_sibling, _add_halves(g4, from_sibling, ac.reshape(1), name=f"rs_add_halves_{tag}")

    def finish(g4, from_sibling, from_chips, tag):
        my_half = _add_four(g4, from_sibling, from_chips, pos, name=f"rs_add_chips_{tag}")
        return _join_halves(my_half, name=f"rs_join_{tag}")

    layout = [[("mlp_w1", 2 * D), ("hg_w_out", D // 4), ("fox_w_out", D // 4)], [("mlp_w2", 2 * D), ("fox_w_in", fox_rows)]]
    place = {}
    for h, half in enumerate(layout):
        off = 0
        for n, rows in half:
            place[n] = (h, off, rows)
            off += rows

    rh = -(-max(sum(rows for _, rows in half) for half in layout) // 16) * 16
    where = {"hg_w_out": ("row",) + place["hg_w_out"][:2], "fox_w_out": ("row",) + place["fox_w_out"][:2]}
    for i in range(2):
        where[f"mlp_w1_{i}"] = ("col", place["mlp_w1"][0], place["mlp_w1"][1] + i * D)
        where[f"mlp_w2_{i}"] = ("row", place["mlp_w2"][0], place["mlp_w2"][1] + i * D)

    def reduce_early(gw, slab):
        gfox = uncol(gw["fox_w_in"][:, :4 * fox_rows], fox_rows).reshape(N_CHIPS, 1, fox_rows, D)
        h, off, _ = place["fox_w_in"]
        slab = lax.dynamic_update_slice(slab, gfox, (0, h, off, 0))
        for h, half in enumerate(layout):
            used = sum(rows for _, rows in half)
            if used < rh:
                slab = lax.dynamic_update_slice(slab, jnp.zeros((N_CHIPS, 1, rh - used, D), F32), (0, h, used, 0))
        from_sibling, part = swap_and_add(slab, "early")
        return part, (slab, from_sibling)

    loss_part, grad_x, gw, gs, ((g4, from_sibling), from_chips) = _local_step(
        xs, tgt, mod, wts, small, slab_rest, unpack_rest, reduce_early, (lax.empty((N_CHIPS, 2, rh, D), F32), where))
    loss = lax.psum(loss_part[0, 0], ("x", "y", "c"))
    gshard = finish(g4, from_sibling, from_chips, "early")

    g4 = uncol(gw["hg_w_in"], D).reshape(N_CHIPS, 2, D // 2, D)
    from_sibling, part = swap_and_add(g4, "late")
    g_hg_w_in = finish(g4, from_sibling, _scatter_partials(part, name="rs_scatter_late"), "late").reshape(D, D)

    names = ["dmod"] + SMALL_NAMES
    packed, offs = _pack_small(gs, names)
    packed = _pad_rows(packed, 8)
    rp = packed.shape[0]
    parts = _allgather_small(packed, name="gather_small").reshape(N_DEV, rp, LANES)
    total = _sum_parts(parts, name="sum_small")
    r0, nr = offs["dmod"]
    dmod_all = parts[:, r0:r0 + nr].reshape(N_DEV, 2, N_CHIPS * nmod)
    dmod_shard = lax.dynamic_slice_in_dim(dmod_all, chip * nmod, nmod, axis=2).transpose(1, 0, 2)
    g_w_mod = _mod_bwd(c16, jnp.pad(dmod_shard, ((0, 0), (0, 16 - N_DEV), (0, 0))), name="mod_bwd")

    grads = {"w_mod": g_w_mod, "b_mod": _unpack_small(total, offs, "dmod", b_mod)}
    for n in SMALL_NAMES:
        grads[n] = _unpack_small(total, offs, n, small[n])

    given = dict(w_mod=(w_mod, m_w_mod, v_w_mod), b_mod=(b_mod, m_b_mod, v_b_mod), norm1_w=(norm1_w, m_norm1_w, v_norm1_w),
                 norm2_w=(norm2_w, m_norm2_w, v_norm2_w), hg_w_in=(hg_w_in, m_hg_w_in, v_hg_w_in),
                 hg_w_out=(hg_w_out, m_hg_w_out, v_hg_w_out), hg_lb=(hg_lb, m_hg_lb, v_hg_lb),
                 hg_gn_w=(hg_gn_w, m_hg_gn_w, v_hg_gn_w), fox_w_in=(fox_w_in, m_fox_w_in, v_fox_w_in),
                 fox_b_f=(fox_b_f, m_fox_b_f, v_fox_b_f), fox_qn_w=(fox_qn_w, m_fox_qn_w, v_fox_qn_w),
                 fox_kn_w=(fox_kn_w, m_fox_kn_w, v_fox_kn_w), fox_w_out=(fox_w_out, m_fox_w_out, v_fox_w_out),
                 mlp_w1=(mlp_w1, m_mlp_w1, v_mlp_w1), mlp_w2=(mlp_w2, m_mlp_w2, v_mlp_w2), final_w=(final_w, m_final_w, v_final_w))
    upd = {}

    for n, (h, off, rows) in place.items():
        w, m, v = given[n]
        flat = lambda a: a.reshape(rows, D)
        d, mn, vn = _adamw(flat(w), gshard, flat(m), flat(v), g_at=(h, off), name=f"adamw_{n}")
        grads[n] = gshard[h, off:off + rows].reshape(w.shape)
        upd[n] = tuple(a.reshape(w.shape) for a in (d, mn, vn))

    w, m, v = given["hg_w_in"]
    grads["hg_w_in"] = g_hg_w_in.reshape(w.shape)
    upd["hg_w_in"] = tuple(a.reshape(w.shape) for a in _adamw(w[0], g_hg_w_in, m[0], v[0], name="adamw_hg_w_in"))

    w, m, v = given["w_mod"]
    flat = lambda a: a.reshape(-1, nmod)
    upd["w_mod"] = tuple(a.reshape(w.shape) for a in _adamw(flat(w), flat(g_w_mod), flat(m), flat(v), name="adamw_w_mod"))

    snames = ["b_mod"] + SMALL_NAMES
    pw, soffs = _pack_small({n: given[n][0] for n in snames}, snames)
    pm, _ = _pack_small({n: given[n][1] for n in snames}, snames)
    pv, _ = _pack_small({n: given[n][2] for n in snames}, snames)
    pg, _ = _pack_small({n: grads[n] for n in snames}, snames)
    pw, pm, pv, pg = (_pad_rows(a, 8) for a in (pw, pm, pv, pg))
    sd, smn, svn = _adamw(pw, pg, pm, pv, name="adamw_small")
    for n in snames:
        like = given[n][0]
        upd[n] = tuple(_unpack_small(a, soffs, n, like) for a in (sd, smn, svn))

    order = ["w_mod", "b_mod", "norm1_w", "norm2_w", "hg_w_in", "hg_w_out", "hg_lb", "hg_gn_w", "fox_w_in", "fox_b_f",
             "fox_qn_w", "fox_kn_w", "fox_w_out", "mlp_w1", "mlp_w2", "final_w"]
    return (loss, grad_x.reshape(x.shape), *[grads[n] for n in order], *[upd[n][0] for n in order],
            *[upd[n][1] for n in order], *[upd[n][2] for n in order])
```

```python
import math

import jax
import jax.numpy as jnp
from jax import lax
from jax.experimental import pallas as pl
from jax.experimental.pallas import tpu as pltpu

EPS = 1e-6
ADAM_LR, ADAM_B1, ADAM_B2, ADAM_EPS, ADAM_WD, ADAM_STEP = 0.001, 0.9, 0.999, 1e-08, 0.01, 10

F32 = jnp.float32
BF = jnp.bfloat16
LANES = 128
HG_CHUNK = 64
HG_HEADS_PER_STEP = 8
HG_TOKENS_PER_STEP = 256
FOX_BWD_TILES = (8, 4, 2, 1)
LOG2E = 1.4426950408889634
FOX_DH = 64
N_CHIPS = 4
N_DEV = 8
VMEM_LIMIT = 56 * 1024 * 1024
MESH = pl.DeviceIdType.MESH

NT = (((1,), (1,)), ((), ()))
TN = (((0,), (0,)), ((), ()))


def _pick(n, pref, mult=LANES):
    if n <= pref:
        return n
    t = (pref // mult) * mult
    while t >= mult:
        if n % t == 0:
            return t
        t -= mult
    raise ValueError((n, pref, mult))


def _cp(*sem):
    return pltpu.CompilerParams(dimension_semantics=sem, vmem_limit_bytes=VMEM_LIMIT)


def _dot(a, b):
    return jnp.dot(a, b, preferred_element_type=F32)


def _dg(a, b, dims):
    return lax.dot_general(a, b, dims, preferred_element_type=F32)


def _split3(x):
    hi = x.astype(BF)
    r1 = x - hi.astype(F32)
    mid = r1.astype(BF)
    lo = (r1 - mid.astype(F32)).astype(BF)
    return hi, mid, lo


def _tri_dot(tri, x):
    hi, mid, lo = _split3(x)
    return _dot(tri, hi) + _dot(tri, mid) + _dot(tri, lo)


def _dg3(a, b, dims):
    ah, bh = a.astype(BF), b.astype(BF)
    al, bl = (a - ah.astype(F32)).astype(BF), (b - bh.astype(F32)).astype(BF)
    return _dg(ah, bh, dims) + _dg(ah, bl, dims) + _dg(al, bh, dims)


def _dg1(a, b, dims):
    return _dg(a.astype(BF), b.astype(BF), dims)


NN = (((1,), (0,)), ((), ()))


def _sigmoid(x):
    return jax.nn.sigmoid(x)


def _ln_matmul(x, nw, sc, sh, w, slab=None, *, relu2, name):
    S, D = x.shape
    N = w.shape[1]
    tm, tn = _pick(S, 512, 16), N
    fused = slab is not None

    def body(x_ref, nw_ref, sc_ref, sh_ref, w_ref, *rest):
        if fused:
            s_ref, *outs, out_ref, hs, send_sems, recv_sems = rest
            finish = _gather_behind(s_ref, out_ref, send_sems, recv_sems, pl.program_id(0), S // tm)
        else:
            outs, hs = rest[:-1], rest[-1]
        h_ref = outs[-1]

        @pl.when(pl.program_id(1) == 0)
        def _():
            xv = x_ref[...]
            r = lax.rsqrt(jnp.mean(xv * xv, axis=-1, keepdims=True) + EPS)
            hb = ((xv * r * nw_ref[...]) * (1.0 + sc_ref[...]) + sh_ref[...]).astype(BF)
            hs[...] = hb
            h_ref[...] = hb

        z = _dot(hs[...], w_ref[...])
        if relu2:
            a = jnp.maximum(z, 0.0)
            outs[0][...] = a.astype(BF)
            outs[1][...] = (a * a).astype(BF)
        else:
            outs[0][...] = z
        if fused:
            finish()

    vec = pl.BlockSpec((1, D), lambda i, j: (0, 0))
    tile = pl.BlockSpec((tm, tn), lambda i, j: (i, j))
    if relu2:
        out_shape = [jax.ShapeDtypeStruct((S, N), BF), jax.ShapeDtypeStruct((S, N), BF)]
        out_specs = [tile, tile]
    else:
        out_shape = [jax.ShapeDtypeStruct((S, N), F32)]
        out_specs = [tile]
    out_shape.append(jax.ShapeDtypeStruct((S, D), BF))
    out_specs.append(pl.BlockSpec((tm, D), lambda i, j: (i, 0)))
    in_specs = [pl.BlockSpec((tm, D), lambda i, j: (i, 0)), vec, vec, vec, pl.BlockSpec((D, tn), lambda i, j: (0, j))]
    scratch = [pltpu.VMEM((tm, D), BF)]
    args = [x, nw, sc, sh, w]
    if fused:
        in_specs.append(HBM)
        out_specs.append(HBM)
        out_shape.append(jax.ShapeDtypeStruct((N_CHIPS,) + slab.shape, slab.dtype))
        scratch += [pltpu.SemaphoreType.DMA((6,)), pltpu.SemaphoreType.DMA((6,))]
        args.append(slab)
    return pl.pallas_call(
        body, name=name, grid=(S // tm, N // tn), in_specs=in_specs, out_specs=out_specs, out_shape=out_shape,
        scratch_shapes=scratch, compiler_params=_cp("arbitrary", "arbitrary"),
    )(*args)


def _matmul_resid(a, w, x, gate, *, name):
    S, K = a.shape
    D = w.shape[1]
    tm, tn = _pick(S, 1024 if K <= 1024 else 512, 16), D

    def body(a_ref, w_ref, x_ref, g_ref, o_ref, y_ref):
        y = _dot(a_ref[...], w_ref[...])
        y_ref[...] = y.astype(BF)
        o_ref[...] = x_ref[...] + g_ref[...] * y

    tile = pl.BlockSpec((tm, tn), lambda i, j: (i, j))
    return pl.pallas_call(
        body, name=name, grid=(S // tm, D // tn),
        in_specs=[pl.BlockSpec((tm, K), lambda i, j: (i, 0)), pl.BlockSpec((K, tn), lambda i, j: (0, j)),
                  tile, pl.BlockSpec((1, tn), lambda i, j: (0, j))],
        out_specs=[tile, tile],
        out_shape=[jax.ShapeDtypeStruct((S, D), F32), jax.ShapeDtypeStruct((S, D), BF)],
        compiler_params=_cp("parallel", "arbitrary"),
    )(a, w, x, gate)


def _gate_matmul_nt(dx, gate, y, w, act, *, name):
    S, D = dx.shape
    K = w.shape[0]
    tm, tn = _pick(S, 1024 if K <= 1024 else 512, 16), K
    fused = act is not None

    def body(dx_ref, g_ref, y_ref, w_ref, *rest):
        if fused:
            act_ref, da_ref, dm_ref, dg_ref, ms = rest
        else:
            da_ref, dm_ref, dg_ref, ms = rest
        i, j = pl.program_id(0), pl.program_id(1)

        @pl.when((i == 0) & (j == 0))
        def _():
            dg_ref[...] = jnp.zeros_like(dg_ref)

        @pl.when(j == 0)
        def _():
            dxv = dx_ref[...]
            dmb = (dxv * g_ref[...]).astype(BF)
            ms[...] = dmb
            dm_ref[...] = dmb
            dg_ref[...] += jnp.sum(dxv * y_ref[...].astype(F32), axis=0, keepdims=True)

        da = _dg(ms[...], w_ref[...], NT)
        if fused:
            da_ref[...] = (da * (2.0 * act_ref[...].astype(F32))).astype(BF)
        else:
            da_ref[...] = da

    row = pl.BlockSpec((tm, D), lambda i, j: (i, 0))
    vec = pl.BlockSpec((1, D), lambda i, j: (0, 0))
    tile = pl.BlockSpec((tm, tn), lambda i, j: (i, j))
    in_specs = [row, vec, row, pl.BlockSpec((tn, D), lambda i, j: (j, 0))]
    args = [dx, gate, y, w]
    if fused:
        in_specs.append(tile)
        args.append(act)
    return pl.pallas_call(
        body, name=name, grid=(S // tm, K // tn),
        in_specs=in_specs, out_specs=[tile, row, vec],
        out_shape=[jax.ShapeDtypeStruct((S, K), BF if fused else F32), jax.ShapeDtypeStruct((S, D), BF),
                   jax.ShapeDtypeStruct((1, D), F32)],
        scratch_shapes=[pltpu.VMEM((tm, D), BF)],
        compiler_params=_cp("arbitrary", "arbitrary"),
    )(*args)


def _matmul_tn(a, b, *, name, into=None):
    S, Ka = a.shape
    P, _, Db = b.shape
    tk, tn, ts = _pick(Ka, 1024), _pick(Db, 1024), _pick(S, 1024, 16)
    if into is not None:
        slab, kind, half, off = into
        C = tn = slab.shape[3]
        if kind == "row":
            tk = min(tk, Ka // N_CHIPS)
        assert tn == C and P * Db == (N_CHIPS * C if kind == "col" else C) and off % tk == 0
        assert tk == Ka if kind == "col" else (Ka // N_CHIPS) % tk == 0
    npb = Db // tn

    def body(a_ref, b_ref, *rest):
        o_ref, acc = rest[-2:]
        s = pl.program_id(2)

        @pl.when(s == 0)
        def _():
            acc[...] = jnp.zeros_like(acc)

        acc[...] += _dg(a_ref[...], b_ref[...], TN)

        @pl.when(s == pl.num_programs(2) - 1)
        def _():
            o_ref[...] = acc[...]

    in_specs = [pl.BlockSpec((ts, tk), lambda i, j, s: (s, i)),
                pl.BlockSpec((None, ts, tn), lambda i, j, s: (j // npb, s, j % npb))]
    args = [a, b]
    if into is None:
        out_spec = pl.BlockSpec((tk, tn), lambda i, j, s: (i, j))
        out_shape = jax.ShapeDtypeStruct((Ka, P * Db), F32)
        aliases = {}
    else:
        per = (Ka // N_CHIPS) // tk if kind == "row" else 1
        if kind == "col":
            out_spec = pl.BlockSpec((None, None, tk, tn), lambda i, j, s: (j, half, off // tk + i, 0))
        else:
            out_spec = pl.BlockSpec((None, None, tk, tn), lambda i, j, s: (i // per, half, off // tk + i % per, 0))
        out_shape = jax.ShapeDtypeStruct(slab.shape, F32)
        in_specs.append(pl.BlockSpec(memory_space=pl.ANY))
        args.append(slab)
        aliases = {2: 0}
    return pl.pallas_call(
        body, name=name, grid=(Ka // tk, P * npb, S // ts),
        in_specs=in_specs, out_specs=out_spec, out_shape=out_shape,
        scratch_shapes=[pltpu.VMEM((tk, tn), F32)], input_output_aliases=aliases,
        compiler_params=_cp("parallel", "parallel", "arbitrary"),
    )(*args)


def _matmul_nt_lnbwd(g, w, x, nw, sc, dx_out, *, name):
    P, S, Dg = g.shape
    D = x.shape[1]
    tm = _pick(S, 512, 16)

    def body(g_ref, w_ref, x_ref, nw_ref, sc_ref, dxo_ref, dx_ref, dsc_ref, dsh_ref, dnw_ref):
        @pl.when(pl.program_id(0) == 0)
        def _():
            dsc_ref[...] = jnp.zeros_like(dsc_ref)
            dsh_ref[...] = jnp.zeros_like(dsh_ref)
            dnw_ref[...] = jnp.zeros_like(dnw_ref)

        dh = _dg(g_ref[0], w_ref[:, 0:Dg], NT)
        for p in range(1, P):
            dh = dh + _dg(g_ref[p], w_ref[:, p * Dg:(p + 1) * Dg], NT)
        xv = x_ref[...]
        nwv = nw_ref[...]
        r = lax.rsqrt(jnp.mean(xv * xv, axis=-1, keepdims=True) + EPS)
        xr = xv * r
        dn = dh * (1.0 + sc_ref[...])
        dsc_ref[...] += jnp.sum(dh * (xr * nwv), axis=0, keepdims=True)
        dsh_ref[...] += jnp.sum(dh, axis=0, keepdims=True)
        dnw_ref[...] += jnp.sum(dn * xr, axis=0, keepdims=True)
        u = dn * nwv
        dx_ref[...] = dxo_ref[...] + r * (u - xr * jnp.mean(u * xr, axis=-1, keepdims=True))

    row = pl.BlockSpec((tm, D), lambda i: (i, 0))
    vec = pl.BlockSpec((1, D), lambda i: (0, 0))
    return pl.pallas_call(
        body, name=name, grid=(S // tm,),
        in_specs=[pl.BlockSpec((P, tm, Dg), lambda i: (0, i, 0)),
                  pl.BlockSpec((D, P * Dg), lambda i: (0, 0)), row, vec, vec, row],
        out_specs=[row, vec, vec, vec],
        out_shape=[jax.ShapeDtypeStruct((S, D), F32)] + [jax.ShapeDtypeStruct((1, D), F32)] * 3,
        compiler_params=_cp("arbitrary"),
    )(g, w, x, nw, sc, dx_out)


def _loss_kernel(x, fw, tgt, *, name):
    S, D = x.shape
    tm = _pick(S, 512, 8)

    def body(x_ref, fw_ref, t_ref, l_ref, dx_ref, dfw_ref):
        @pl.when(pl.program_id(0) == 0)
        def _():
            l_ref[...] = jnp.zeros_like(l_ref)
            dfw_ref[...] = jnp.zeros_like(dfw_ref)

        xv = x_ref[...]
        fwv = fw_ref[...]
        r = lax.rsqrt(jnp.mean(xv * xv, axis=-1, keepdims=True) + EPS)
        xr = xv * r
        err = xr * fwv - t_ref[...]
        per_tok = jnp.mean(err * err, axis=-1, keepdims=True)
        l_ref[...] += 0.5 * jnp.sum(per_tok, axis=0, keepdims=True)
        dy = err * (1.0 / D)
        dfw_ref[...] += jnp.sum(dy * xr, axis=0, keepdims=True)
        u = dy * fwv
        dx_ref[...] = r * (u - xr * jnp.mean(u * xr, axis=-1, keepdims=True))

    row = pl.BlockSpec((tm, D), lambda i: (i, 0))
    vec = pl.BlockSpec((1, D), lambda i: (0, 0))
    return pl.pallas_call(
        body, name=name, grid=(S // tm,),
        in_specs=[row, vec, row],
        out_specs=[pl.BlockSpec((1, LANES), lambda i: (0, 0)), row, vec],
        out_shape=[jax.ShapeDtypeStruct((1, LANES), F32), jax.ShapeDtypeStruct((S, D), F32),
                   jax.ShapeDtypeStruct((1, D), F32)],
        compiler_params=_cp("arbitrary"),
    )(x, fw, tgt)


def _hg_lower_bound(lb3):
    mx = jnp.max(lb3, axis=0, keepdims=True)
    e = jnp.exp(lb3 - mx)
    p = e / jnp.sum(e, axis=0, keepdims=True)
    return p[0:1, :], p


def _hg_chunk_common(qr, fz, lbv):
    sq = _sigmoid(qr)
    q = qr * sq
    sig = _sigmoid(fz)
    f = lbv + (1.0 - lbv) * sig
    k = (1.0 - lbv) * (1.0 - sig)
    return q, sq, sig, f, k, jnp.log(f)


def _row_of(x, rows, r):
    return jnp.sum(jnp.where(rows == r, x, 0.0), axis=0, keepdims=True)


def _hg_fwd(proj, hg_lb, gn, slab=None, *, name):
    S = proj.shape[0]
    D = proj.shape[1] // 4
    H = D // LANES
    HB = min(HG_HEADS_PER_STEP, H)
    W = HB * LANES
    C = HG_CHUNK
    T = _pick(S, HG_TOKENS_PER_STEP, C)
    nch, nb = T // C, S // T
    ng = H // HB
    fused = slab is not None

    def body(q_ref, fz_ref, v_ref, g_ref, lb_ref, gn_ref, *rest):
        if fused:
            s_ref, y_ref, o_ref, sts_ref, out_ref, st, send_sems, recv_sems = rest
            finish = _gather_behind(s_ref, out_ref, send_sems, recv_sems,
                                    pl.program_id(0) * nb + pl.program_id(1), ng * nb)
        else:
            y_ref, o_ref, sts_ref, st = rest

        @pl.when(pl.program_id(1) == 0)
        def _():
            st[...] = jnp.zeros_like(st)

        lb_all, _ = _hg_lower_bound(lb_ref[...])
        gnv = gn_ref[...]
        ri = lax.broadcasted_iota(jnp.int32, (C, C), 0)
        ci_ = lax.broadcasted_iota(jnp.int32, (C, C), 1)
        low = ri >= ci_
        tri = jnp.where(low, 1.0, 0.0).astype(BF)
        rows_w = lax.broadcasted_iota(jnp.int32, (C, W), 0)

        def chunk(ci, carry):
            sl = pl.ds(pl.multiple_of(ci * C, C), C)
            heads = [slice(hh * LANES, (hh + 1) * LANES) for hh in range(HB)]
            q, _, _, _, k, logf = _hg_chunk_common(q_ref[sl, :], fz_ref[sl, :], lb_all)
            vv, gg = v_ref[sl, :], g_ref[sl, :]
            G = _tri_dot(tri, logf)
            Gm = _row_of(G, rows_w, C // 2 - 1)
            Gl = _row_of(G, rows_w, C - 1)
            qt, kt = q * jnp.exp(G - Gm), k * jnp.exp(Gm - G)
            qe, kd, eGl = q * jnp.exp(G), k * jnp.exp(Gl - G), jnp.exp(Gl)
            A = [jnp.where(low, _dg1(qt[:, ls], kt[:, ls], NT), 0.0) for ls in heads]
            Sv = [st[hh] for hh in range(HB)]
            for hh in range(HB):
                sts_ref[hh, ci] = Sv[hh]
            o = [_dg1(A[hh], vv[:, ls], NN) + _dg1(qe[:, ls], Sv[hh], NT) for hh, ls in enumerate(heads)]
            for hh, ls in enumerate(heads):
                st[hh] = Sv[hh] * eGl[:, ls] + _dg1(vv[:, ls], kd[:, ls], TN)
            gate = gg * _sigmoid(gg)
            for hh, ls in enumerate(heads):
                r = lax.rsqrt(jnp.mean(o[hh] * o[hh], axis=-1, keepdims=True) + EPS)
                y_ref[sl, ls] = ((o[hh] * r * gnv) * gate[:, ls]).astype(BF)
                o_ref[sl, ls] = o[hh]
            return carry

        lax.fori_loop(0, nch, chunk, 0)

        if fused:
            finish()

    def part(p):
        return pl.BlockSpec((T, W), lambda h, n: (n, p * ng + h))

    blk = pl.BlockSpec((T, W), lambda h, n: (n, h))
    in_specs = [part(0), part(1), part(2), part(3),
                pl.BlockSpec((3, W), lambda h, n: (0, h)), pl.BlockSpec((1, LANES), lambda h, n: (0, 0))]
    out_specs = [blk, blk, pl.BlockSpec((HB, nch, LANES, LANES), lambda h, n: (h, n, 0, 0))]
    out_shape = [jax.ShapeDtypeStruct((S, D), BF), jax.ShapeDtypeStruct((S, D), F32),
                 jax.ShapeDtypeStruct((H, S // C, LANES, LANES), F32)]
    scratch = [pltpu.VMEM((HB, LANES, LANES), F32)]
    args = [proj, proj, proj, proj, hg_lb, gn]
    if fused:
        in_specs.append(HBM)
        out_specs.append(HBM)
        out_shape.append(jax.ShapeDtypeStruct((N_CHIPS,) + slab.shape, slab.dtype))
        scratch += [pltpu.SemaphoreType.DMA((6,)), pltpu.SemaphoreType.DMA((6,))]
        args.append(slab)
    return pl.pallas_call(
        body, name=name, grid=(ng, nb), in_specs=in_specs, out_specs=out_specs, out_shape=out_shape,
        scratch_shapes=scratch, compiler_params=_cp("arbitrary", "arbitrary"),
    )(*args)


def _hg_bwd(proj, hg_lb, gn, o_all, states, dy, part=None, *, name):
    S = proj.shape[0]
    D = proj.shape[1] // 4
    H = D // LANES
    HB = min(HG_HEADS_PER_STEP, H)
    W = HB * LANES
    C = HG_CHUNK
    T = _pick(S, HG_TOKENS_PER_STEP, C)
    nch, nb = T // C, S // T
    ng = H // HB
    fused = part is not None

    def body(q_ref, fz_ref, v_ref, g_ref, lb_ref, gn_ref, o_ref, sts_ref, dy_ref, *rest):
        if fused:
            p_ref, dp_ref, dlb_ref, dgn_ref, recv_ref, dst, dlb_acc, send_sems, recv_sems = rest
            copies = _scatter_copies(p_ref, recv_ref, send_sems, recv_sems)

            @pl.when((pl.program_id(0) == 0) & (pl.program_id(1) == 0))
            def _():
                for cp in copies:
                    cp.start()
        else:
            dp_ref, dlb_ref, dgn_ref, dst, dlb_acc = rest
        n = pl.program_id(1)

        @pl.when(n == 0)
        def _():
            dst[...] = jnp.zeros_like(dst)
            dlb_acc[...] = jnp.zeros_like(dlb_acc)
            dgn_ref[...] = jnp.zeros_like(dgn_ref)

        lb_all, p3 = _hg_lower_bound(lb_ref[...])
        gnv = gn_ref[...]
        ri = lax.broadcasted_iota(jnp.int32, (C, C), 0)
        ci_ = lax.broadcasted_iota(jnp.int32, (C, C), 1)
        low = ri >= ci_
        tri = jnp.where(low, 1.0, 0.0).astype(BF)
        triu = jnp.where(ri <= ci_, 1.0, 0.0).astype(BF)
        rows_w = lax.broadcasted_iota(jnp.int32, (C, W), 0)
        gnw = jnp.tile(gnv, (1, HB))

        def chunk(cj, carry):
            ci = nch - 1 - cj
            sl = pl.ds(pl.multiple_of(ci * C, C), C)
            heads = list(enumerate(slice(hh * LANES, (hh + 1) * LANES) for hh in range(HB)))
            wide = lambda parts: jnp.concatenate(parts, axis=1)
            qr, vv, gg = q_ref[sl, :], v_ref[sl, :], g_ref[sl, :]
            q, sq, sig, f, k, logf = _hg_chunk_common(qr, fz_ref[sl, :], lb_all)
            G = _tri_dot(tri, logf)
            Gm = _row_of(G, rows_w, C // 2 - 1)
            Gl = _row_of(G, rows_w, C - 1)
            eG, e_qm, e_km, e_lk, eGl = jnp.exp(G), jnp.exp(G - Gm), jnp.exp(Gm - G), jnp.exp(Gl - G), jnp.exp(Gl)
            qt, kt, kdec, qe = q * e_qm, k * e_km, k * e_lk, q * eG
            sg = _sigmoid(gg)
            d_onw = dy_ref[sl, :] * (gg * sg)
            u = d_onw * gnw
            o = o_ref[sl, :]
            on, do = [], []
            for hh, ls in heads:
                r = lax.rsqrt(jnp.mean(o[:, ls] * o[:, ls], axis=-1, keepdims=True) + EPS)
                on.append(o[:, ls] * r)
                dgn_ref[hh] += jnp.sum(d_onw[:, ls] * on[hh], axis=0, keepdims=True)
                do.append(r * (u[:, ls] - on[hh] * jnp.mean(u[:, ls] * on[hh], axis=-1, keepdims=True)))
            dgg = dy_ref[sl, :] * (wide(on) * gnw) * (sg * (1.0 + gg * (1.0 - sg)))
            Sv = [sts_ref[hh, ci] for hh, _ in heads]
            dSv = [dst[hh] for hh, _ in heads]
            A = [jnp.where(low, _dg1(qt[:, ls], kt[:, ls], NT), 0.0) for _, ls in heads]
            dA = [jnp.where(low, _dg3(do[hh], vv[:, ls], NT), 0.0) for hh, ls in heads]
            dv = wide([_dg1(A[hh], do[hh], TN) + _dg1(kdec[:, ls], dSv[hh], NT) for hh, ls in heads])
            dq = wide([_dg3(dA[hh], kt[:, ls], NN) for hh, ls in heads]) * e_qm \
                + eG * wide([_dg3(do[hh], Sv[hh], NN) for hh, _ in heads])
            dk = wide([_dg3(dA[hh], qt[:, ls], TN) for hh, ls in heads]) * e_km \
                + e_lk * wide([_dg3(vv[:, ls], dSv[hh], NN) for hh, ls in heads])
            s_end = [Sv[hh] * eGl[:, ls] + _dg3(vv[:, ls], kdec[:, ls], TN) for hh, ls in heads]
            dgl = wide([jnp.sum(dSv[hh] * s_end[hh], axis=0, keepdims=True) for hh, _ in heads])
            for hh, ls in heads:
                dst[hh] = dSv[hh] * eGl[:, ls] + _dg1(do[hh], qe[:, ls], TN)
            dG = q * dq - k * dk + jnp.where(rows_w == C - 1, dgl, 0.0)
            dlogf = _tri_dot(triu, dG) - f * dk
            dlf_f = dlogf / f
            dlb_acc[...] += jnp.sum(dlf_f * (1.0 - sig), axis=0, keepdims=True)
            dp_ref[0, sl, :] = (dq * (sq * (1.0 + qr * (1.0 - sq)))).astype(BF)
            dp_ref[1, sl, :] = (dlf_f * (1.0 - lb_all) * sig * (1.0 - sig)).astype(BF)
            dp_ref[2, sl, :] = dv.astype(BF)
            dp_ref[3, sl, :] = dgg.astype(BF)
            return carry

        lax.fori_loop(0, nch, chunk, 0)
        sel = jnp.where(lax.broadcasted_iota(jnp.int32, (3, W), 0) == 0, 1.0, 0.0)
        dlb_ref[...] = lb_all * (sel - p3) * dlb_acc[...]

        if fused:
            @pl.when((pl.program_id(0) == ng - 1) & (n == nb - 1))
            def _():
                for cp in copies:
                    cp.wait()

    def col(p):
        return pl.BlockSpec((T, W), lambda h, n: (nb - 1 - n, p * ng + h))

    blk = pl.BlockSpec((T, W), lambda h, n: (nb - 1 - n, h))
    in_specs = [col(0), col(1), col(2), col(3),
                pl.BlockSpec((3, W), lambda h, n: (0, h)), pl.BlockSpec((1, LANES), lambda h, n: (0, 0)),
                blk, pl.BlockSpec((HB, nch, LANES, LANES), lambda h, n: (h, nb - 1 - n, 0, 0)), blk]
    out_specs = [pl.BlockSpec((4, T, W), lambda h, n: (0, nb - 1 - n, h)),
                 pl.BlockSpec((3, W), lambda h, n: (0, h)),
                 pl.BlockSpec((HB, 1, LANES), lambda h, n: (h, 0, 0))]
    out_shape = [jax.ShapeDtypeStruct((4, S, D), BF), jax.ShapeDtypeStruct((3, D), F32),
                 jax.ShapeDtypeStruct((H, 1, LANES), F32)]
    scratch = [pltpu.VMEM((HB, LANES, LANES), F32), pltpu.VMEM((1, W), F32)]
    args = [proj, proj, proj, proj, hg_lb, gn, o_all, states, dy]
    if fused:
        in_specs.append(HBM)
        out_specs.append(HBM)
        out_shape.append(jax.ShapeDtypeStruct((3,) + part.shape[1:], part.dtype))
        scratch += [pltpu.SemaphoreType.DMA((3,)), pltpu.SemaphoreType.DMA((3,))]
        args.append(part)
    return pl.pallas_call(
        body, name=name, grid=(ng, nb), in_specs=in_specs, out_specs=out_specs, out_shape=out_shape,
        scratch_shapes=scratch, compiler_params=_cp("arbitrary", "arbitrary"),
    )(*args)


def _log_sigmoid(u):
    return jnp.minimum(u, 0.0) - jnp.log(1.0 + jnp.exp(-jnp.abs(u)))


def _lane_put(base, lane, first, pieces):
    for n, p in enumerate(pieces):
        base = jnp.where(lane == first + n, p, base)
    return base


def _fox_cumsum(proj, bf_pad, *, name):
    S = proj.shape[0]
    D = proj.shape[1] // 5
    T = _pick(S, 256, 8)

    def body(fz_ref, b_ref, f_ref, carry):
        @pl.when(pl.program_id(0) == 0)
        def _():
            carry[...] = jnp.zeros_like(carry)

        logf = _log_sigmoid(fz_ref[...] + b_ref[...])
        tri = jnp.where(lax.broadcasted_iota(jnp.int32, (T, T), 0) >= lax.broadcasted_iota(jnp.int32, (T, T), 1),
                        1.0, 0.0).astype(BF)
        fv = _tri_dot(tri, logf) + carry[...]
        f_ref[...] = fv
        carry[...] = _row_of(fv, lax.broadcasted_iota(jnp.int32, (T, LANES), 0), T - 1)

    return pl.pallas_call(
        body, name=name, grid=(S // T,),
        in_specs=[pl.BlockSpec((T, LANES), lambda i: (i, 4 * D // LANES)), pl.BlockSpec((1, LANES), lambda i: (0, 0))],
        out_specs=pl.BlockSpec((T, LANES), lambda i: (i, 0)),
        out_shape=jax.ShapeDtypeStruct((S, LANES), F32),
        scratch_shapes=[pltpu.VMEM((1, LANES), F32)],
        compiler_params=_cp("arbitrary"),
    )(proj, bf_pad)


def _pair_stats(sq, lo):
    del lo
    a = lax.broadcasted_iota(jnp.int32, (LANES, LANES), 0) < FOX_DH
    b = lax.broadcasted_iota(jnp.int32, (LANES, LANES), 1) < FOX_DH
    avg = jnp.where(a == b, 1.0 / FOX_DH, 0.0).astype(BF)
    hi, mid, low = _split3(sq)
    return _dot(hi, avg) + _dot(mid, avg) + _dot(low, avg)


def _fox_prep(proj, fcum, qw2, kw2, *, name):
    S = proj.shape[0]
    D = proj.shape[1] // 5
    HP = D // LANES
    T = _pick(S, 512, 16)

    def body(q_ref, k_ref, v_ref, f_ref, qw_ref, kw_ref, qa_ref, ka_ref, va_ref, vt_ref):
        hp = pl.program_id(1)
        lane = lax.broadcasted_iota(jnp.int32, (T, LANES), 1)
        lo = lane < FOX_DH
        qv, kv, vv, fv = q_ref[...], k_ref[...], v_ref[...], f_ref[...]
        qn = qv * lax.rsqrt(_pair_stats(qv * qv, lo) + EPS) * qw_ref[...] * (0.125 * LOG2E)
        kn = kv * lax.rsqrt(_pair_stats(kv * kv, lo) + EPS) * kw_ref[...]
        ones_q = jnp.where((lane >= 67) & (lane <= 69), 1.0, 0.0)
        ones_k = jnp.where(((lane >= 64) & (lane <= 66)) | ((lane >= 70) & (lane <= 72)), 1.0, 0.0)
        ones_v = jnp.where((lane >= 64) & (lane <= 66), 1.0, 0.0)
        for hh in range(2):
            fh = jnp.sum(jnp.where(lane == 2 * hp + hh, fv, 0.0), axis=-1, keepdims=True) * LOG2E
            pieces = [p.astype(F32) for p in _split3(fh)]

            def half(x):
                return jnp.where(lo, x if hh == 0 else pltpu.roll(x, FOX_DH, 1), 0.0)

            qa_ref[hh] = _lane_put(half(qn) + ones_q, lane, 64, pieces).astype(BF)
            ka_ref[hh] = _lane_put(half(kn) + ones_k, lane, 67, [-p for p in pieces]).astype(BF)
            va = half(vv) + ones_v
            va_ref[hh] = va.astype(BF)
            vt_ref[hh] = va.T.astype(BF)

    def part(p):
        return pl.BlockSpec((T, LANES), lambda i, hp: (i, p * HP + hp))

    vec = pl.BlockSpec((1, LANES), lambda i, hp: (0, 0))
    aug = pl.BlockSpec((2, T, LANES), lambda i, hp: (hp, i, 0))
    return pl.pallas_call(
        body, name=name, grid=(S // T, HP),
        in_specs=[part(0), part(1), part(2), pl.BlockSpec((T, LANES), lambda i, hp: (i, 0)), vec, vec],
        out_specs=[aug, aug, aug, pl.BlockSpec((2, LANES, T), lambda i, hp: (hp, 0, i))],
        out_shape=[jax.ShapeDtypeStruct((2 * HP, S, LANES), BF)] * 3 + [jax.ShapeDtypeStruct((2 * HP, LANES, S), BF)],
        compiler_params=_cp("parallel", "arbitrary"),
    )(proj, proj, proj, fcum, qw2, kw2)


def _fox_block(S):
    return _pick(S, 256, 16)


def _fox_skip_bounds(fcum, qn_w, kn_w, nheads):
    S = fcum.shape[0]
    B = _fox_block(S)
    qk = 8.0 * LOG2E * 1.02 * jnp.max(jnp.abs(qn_w)) * jnp.max(jnp.abs(kn_w))
    thresh = -(2.0 * qk + 160.0)
    f2 = fcum[:, :nheads] * LOG2E
    first, last = f2[0::B], f2[B - 1::B]
    nb = S // B
    blk = jnp.arange(nb)
    dead = (first[0::2, None, :] - last[None, :, :]) < thresh
    jmin = jnp.sum(dead & (blk[None, :, None] < 2 * jnp.arange(nb // 2)[:, None, None]), axis=1)
    live = (first[:, None, :] - last[None, :, :]) >= thresh
    imax = blk[:, None] + jnp.sum(live & (blk[:, None, None] > blk[None, :, None]), axis=0)
    return jmin.T.astype(jnp.int32), imax.T.astype(jnp.int32)


def _fox_fwd(jmin, qa, ka, vat, proj, *, name):
    H, S, _ = qa.shape
    HP = H // 2
    D = HP * LANES
    B = _fox_block(S)
    BQ = 2 * B
    nq = S // BQ

    def body(jmin_ref, q_ref, k_ref, vt_ref, g_ref, y_ref, o_ref, q2_ref):
        hp, i = pl.program_id(0), pl.program_id(1)
        lane = lax.broadcasted_iota(jnp.int32, (BQ, LANES), 1)
        lo = lane < FOX_DH
        in_stat = (lane >= 70) & (lane <= 75)
        causal = lax.broadcasted_iota(jnp.int32, (BQ, BQ), 0) <= lax.broadcasted_iota(jnp.int32, (BQ, BQ), 1)
        row = lax.broadcasted_iota(jnp.int32, (LANES, BQ), 0)
        m0, acc0 = jnp.full((1, BQ), -jnp.inf, F32), jnp.zeros((LANES, BQ), F32)
        outs = []
        for hh in range(2):
            qb = q_ref[hh]

            def block(j, carry, masked=False):
                m, acc = carry
                sl = pl.ds(pl.multiple_of(j * BQ, BQ), BQ)
                st = _dg(k_ref[hh, sl, :], qb, NT)
                if masked:
                    st = jnp.where(causal, st, -jnp.inf)
                m_new = jnp.maximum(m, jnp.ceil(jnp.max(st, axis=0, keepdims=True)))
                p = jnp.exp2(st - m_new).astype(BF)
                return m_new, acc * jnp.exp2(m - m_new) + _dot(vt_ref[hh, :, sl], p)

            carry = lax.fori_loop(jmin_ref[2 * hp + hh, i] // 2, i, block, (m0, acc0))
            m, acc = block(i, carry, masked=True)
            linv = 1.0 / jnp.sum(jnp.where(row == FOX_DH, acc, 0.0), axis=0, keepdims=True)
            tile = acc * linv
            for n, piece in enumerate(_split3(m) + _split3(linv)):
                tile = jnp.where(row == 70 + n, piece.astype(F32), tile)
            tile = tile.T
            outs.append(tile)
            q2_ref[hh] = jnp.where(in_stat, jnp.where(lane <= 72, -tile, tile), qb.astype(F32)).astype(BF)
        o = jnp.where(lo, outs[0], pltpu.roll(outs[1], FOX_DH, 1))
        o_ref[...] = o
        y_ref[...] = (o * _sigmoid(g_ref[...])).astype(BF)

    blk = pl.BlockSpec((BQ, LANES), lambda hp, i, jm: (i, hp))
    qblk = pl.BlockSpec((2, BQ, LANES), lambda hp, i, jm: (hp, i, 0))
    full = pl.BlockSpec((2, S, LANES), lambda hp, i, jm: (hp, 0, 0))
    full_t = pl.BlockSpec((2, LANES, S), lambda hp, i, jm: (hp, 0, 0))
    return pl.pallas_call(
        body, name=name,
        grid_spec=pltpu.PrefetchScalarGridSpec(
            num_scalar_prefetch=1, grid=(HP, nq),
            in_specs=[qblk, full, full_t, pl.BlockSpec((BQ, LANES), lambda hp, i, jm: (i, 3 * HP + hp))],
            out_specs=[blk, blk, qblk]),
        out_shape=[jax.ShapeDtypeStruct((S, D), BF), jax.ShapeDtypeStruct((S, D), F32),
                   jax.ShapeDtypeStruct((H, S, LANES), BF)],
        compiler_params=_cp("parallel", "arbitrary"),
    )(jmin, qa, ka, vat, proj)


def _fox_bwd_prep(dy, o, proj, q2, *, name):
    S, D = dy.shape
    HP = D // LANES
    T = _pick(S, 512, 16)

    def body(dy_ref, o_ref, g_ref, q2_ref, da_ref):
        lane = lax.broadcasted_iota(jnp.int32, (T, LANES), 1)
        lo = lane < FOX_DH
        in_linv = (lane >= 73) & (lane <= 75)
        linv = [jnp.sum(jnp.where(in_linv, q2_ref[hh].astype(F32), 0.0), axis=-1, keepdims=True) for hh in range(2)]
        u = (dy_ref[...] * _sigmoid(g_ref[...]) * jnp.where(lo, linv[0], linv[1])).astype(BF).astype(F32)
        prod = u * o_ref[...]
        d_lo = jnp.sum(jnp.where(lo, prod, 0.0), axis=-1, keepdims=True)
        d_hi = jnp.sum(jnp.where(lo, 0.0, prod), axis=-1, keepdims=True)
        for hh, delta in enumerate((d_lo, d_hi)):
            base = jnp.where(lo, u if hh == 0 else pltpu.roll(u, FOX_DH, 1), 0.0)
            da_ref[hh] = _lane_put(base, lane, 64, [-(p.astype(F32)) for p in _split3(delta)]).astype(BF)

    blk = pl.BlockSpec((T, LANES), lambda i, hp: (i, hp))
    aug = pl.BlockSpec((2, T, LANES), lambda i, hp: (hp, i, 0))
    return pl.pallas_call(
        body, name=name, grid=(S // T, HP),
        in_specs=[blk, blk, pl.BlockSpec((T, LANES), lambda i, hp: (i, 3 * HP + hp)), aug],
        out_specs=aug,
        out_shape=jax.ShapeDtypeStruct((2 * HP, S, LANES), BF),
        compiler_params=_cp("parallel", "arbitrary"),
    )(dy, o, proj, q2)


def _fox_bwd(imax, q2, ka, va, doa, *, name):
    H, S, _ = q2.shape
    B = _fox_block(S)
    nb = S // B

    def body(imax_ref, q_ref, do_ref, k_ref, v_ref, dq_ref, dk_ref, dv_ref, cs_ref):
        j = pl.program_id(1)
        end = imax_ref[pl.program_id(0), j] + 1

        @pl.when(j == 0)
        def _():
            dq_ref[...] = jnp.zeros_like(dq_ref)

        kb, vb = k_ref[...], v_ref[...]

        def step(i, carry, nblk=1):
            dk_acc, dv_acc, cs_acc = carry
            rows = nblk * B
            sl = pl.ds(pl.multiple_of(i * B, B), rows)
            qb, dob = q_ref[sl, :], do_ref[sl, :]
            s = _dg(qb, kb, NT)
            ahead = lax.broadcasted_iota(jnp.int32, (rows, B), 0) - lax.broadcasted_iota(jnp.int32, (rows, B), 1)
            pb = jnp.exp2(jnp.where(ahead >= (j - i) * B, s, -jnp.inf)).astype(BF)
            ds = pb.astype(F32) * _dg(dob, vb, NT)
            dsb = ds.astype(BF)
            cs_acc = cs_acc + jnp.sum(ds.reshape(rows // 8, 8, B), axis=0)
            dv_acc = dv_acc + _dg(pb, dob, TN)
            dk_acc = dk_acc + _dg(dsb, qb, TN)
            dq_ref[sl, :] += _dot(dsb, kb)
            return dk_acc, dv_acc, cs_acc

        zero = jnp.zeros((B, LANES), F32)
        carry = (zero, zero, jnp.zeros((8, B), F32))
        pos = j
        for U in FOX_BWD_TILES:
            n = (end - pos) // U
            carry = lax.fori_loop(0, n, lambda ii, c, pos=pos, U=U: step(pos + U * ii, c, nblk=U), carry)
            pos = pos + U * n
        dk_acc, dv_acc, cs_acc = carry
        dk_ref[...] = dk_acc
        dv_ref[...] = dv_acc
        cs_ref[...] = jnp.sum(cs_acc, axis=0, keepdims=True)

    full = pl.BlockSpec((None, S, LANES), lambda h, j, im: (h, 0, 0))
    blk = pl.BlockSpec((None, B, LANES), lambda h, j, im: (h, j, 0))
    return pl.pallas_call(
        body, name=name,
        grid_spec=pltpu.PrefetchScalarGridSpec(
            num_scalar_prefetch=1, grid=(H, nb),
            in_specs=[full, full, blk, blk],
            out_specs=[full, blk, blk, pl.BlockSpec((None, 1, B), lambda h, j, im: (h, 0, j))]),
        out_shape=[jax.ShapeDtypeStruct((H, S, LANES), F32)] * 3 + [jax.ShapeDtypeStruct((H, 1, S), F32)],
        compiler_params=_cp("parallel", "arbitrary"),
    )(imax, q2, doa, ka, va)


def _fox_bwd_post(dqa, dka, dva, proj, dy, o, qw2, kw2, *, name):
    S, D = dy.shape
    HP = D // LANES
    T = _pick(S, 512, 16)

    def body(dq_ref, dk_ref, dv_ref, q_ref, k_ref, g_ref, dy_ref, o_ref, qw_ref, kw_ref, dp_ref, dqw_ref, dkw_ref):
        @pl.when((pl.program_id(0) == 0) & (pl.program_id(1) == 0))
        def _():
            dqw_ref[...] = jnp.zeros_like(dqw_ref)
            dkw_ref[...] = jnp.zeros_like(dkw_ref)

        lane = lax.broadcasted_iota(jnp.int32, (T, LANES), 1)
        lo = lane < FOX_DH

        def pair(ref):
            return jnp.where(lo, ref[0], pltpu.roll(ref[1], FOX_DH, 1))

        def norm_bwd(xv, w, dyn, dw_ref):
            r = lax.rsqrt(_pair_stats(xv * xv, lo) + EPS)
            xr = xv * r
            dw_ref[...] += jnp.sum(dyn * xr, axis=0, keepdims=True)
            u = dyn * w
            return r * (u - xr * _pair_stats(u * xr, lo))

        dp_ref[0] = norm_bwd(q_ref[...], qw_ref[...], pair(dq_ref) * 0.125, dqw_ref).astype(BF)
        dp_ref[1] = norm_bwd(k_ref[...], kw_ref[...], pair(dk_ref) * (1.0 / LOG2E), dkw_ref).astype(BF)
        dp_ref[2] = pair(dv_ref).astype(BF)
        sg = _sigmoid(g_ref[...])
        dp_ref[3] = (dy_ref[...] * o_ref[...] * sg * (1.0 - sg)).astype(BF)

    def part(p):
        return pl.BlockSpec((T, LANES), lambda i, hp: (i, p * HP + hp))

    aug = pl.BlockSpec((2, T, LANES), lambda i, hp: (hp, i, 0))
    blk = pl.BlockSpec((T, LANES), lambda i, hp: (i, hp))
    vec = pl.BlockSpec((1, LANES), lambda i, hp: (0, 0))
    return pl.pallas_call(
        body, name=name, grid=(S // T, HP),
        in_specs=[aug, aug, aug, part(0), part(1), part(3), blk, blk, vec, vec],
        out_specs=[pl.BlockSpec((4, T, LANES), lambda i, hp: (0, i, hp)), vec, vec],
        out_shape=[jax.ShapeDtypeStruct((5, S, D), BF), jax.ShapeDtypeStruct((1, LANES), F32),
                   jax.ShapeDtypeStruct((1, LANES), F32)],
        compiler_params=_cp("arbitrary", "arbitrary"),
    )(dqa, dka, dva, proj, proj, proj, dy, o, qw2, kw2)


def _fox_dfz(colsum, nheads, proj, bf_pad, dproj, *, name):
    S = colsum.shape[0]
    H = nheads
    D = dproj.shape[2]
    T = _pick(S, 256, 16)
    nb = S // T

    def body(cs_ref, fz_ref, b_ref, _, dp_ref, db_ref, carry):
        @pl.when(pl.program_id(0) == 0)
        def _():
            carry[...] = jnp.zeros_like(carry)
            db_ref[...] = jnp.zeros_like(db_ref)

        lane = lax.broadcasted_iota(jnp.int32, (T, LANES), 1)
        df = -cs_ref[...]
        triu = jnp.where(lax.broadcasted_iota(jnp.int32, (T, T), 0) <= lax.broadcasted_iota(jnp.int32, (T, T), 1),
                         1.0, 0.0).astype(BF)
        dlogf = _tri_dot(triu, df) + carry[...]
        carry[...] = _row_of(dlogf, lax.broadcasted_iota(jnp.int32, (T, LANES), 0), 0)
        dfz = jnp.where(lane < H, dlogf * _sigmoid(-(fz_ref[...] + b_ref[...])), 0.0)
        db_ref[...] += jnp.sum(dfz, axis=0, keepdims=True)
        dp_ref[...] = jnp.zeros_like(dp_ref)
        dp_ref[:, 0:LANES] = dfz.astype(BF)

    return pl.pallas_call(
        body, name=name, grid=(nb,),
        in_specs=[pl.BlockSpec((T, LANES), lambda i: (nb - 1 - i, 0)),
                  pl.BlockSpec((T, LANES), lambda i: (nb - 1 - i, 4 * D // LANES)),
                  pl.BlockSpec((1, LANES), lambda i: (0, 0)),
                  pl.BlockSpec(memory_space=pl.ANY)],
        out_specs=[pl.BlockSpec((None, T, D), lambda i: (4, nb - 1 - i, 0)), pl.BlockSpec((1, LANES), lambda i: (0, 0))],
        out_shape=[jax.ShapeDtypeStruct(dproj.shape, BF), jax.ShapeDtypeStruct((1, LANES), F32)],
        scratch_shapes=[pltpu.VMEM((1, LANES), F32)],
        input_output_aliases={3: 0},
        compiler_params=_cp("arbitrary"),
    )(colsum, proj, bf_pad, dproj)


def _mod_fwd(c16, w, b, *, name):
    L, D, N = w.shape
    tn = _pick(N, 512)

    def body(c_ref, w_ref, b_ref, o_ref):
        cv = c_ref[...]
        ca = (cv * _sigmoid(cv)).astype(BF)
        o_ref[...] = _dot(ca, w_ref[...].astype(BF)) + b_ref[...]

    return pl.pallas_call(
        body, name=name, grid=(L, N // tn),
        in_specs=[pl.BlockSpec((16, D), lambda l, j: (0, 0)), pl.BlockSpec((None, D, tn), lambda l, j: (l, 0, j)),
                  pl.BlockSpec((None, 1, tn), lambda l, j: (l, 0, j))],
        out_specs=pl.BlockSpec((None, 16, tn), lambda l, j: (l, 0, j)),
        out_shape=jax.ShapeDtypeStruct((L, 16, N), F32),
        compiler_params=_cp("parallel", "arbitrary"),
    )(c16, w, b)


def _mod_bwd(c16, dmod, *, name):
    L, _, N = dmod.shape
    D = c16.shape[1]
    tn = _pick(N, 512)

    def body(c_ref, d_ref, o_ref):
        cv = c_ref[...]
        ca = (cv * _sigmoid(cv)).astype(BF)
        o_ref[...] = _dg(ca, d_ref[...].astype(BF), TN)

    return pl.pallas_call(
        body, name=name, grid=(L, N // tn),
        in_specs=[pl.BlockSpec((16, D), lambda l, j: (0, 0)), pl.BlockSpec((None, 16, tn), lambda l, j: (l, 0, j))],
        out_specs=pl.BlockSpec((None, D, tn), lambda l, j: (l, 0, j)),
        out_shape=jax.ShapeDtypeStruct((L, D, N), F32),
        compiler_params=_cp("parallel", "arbitrary"),
    )(c16, dmod)


def _adamw_math(w, g, m, v):
    m = ADAM_B1 * m + (1.0 - ADAM_B1) * g
    v = ADAM_B2 * v + (1.0 - ADAM_B2) * (g * g)
    m_hat = m / (1.0 - ADAM_B1 ** ADAM_STEP)
    v_hat = v / (1.0 - ADAM_B2 ** ADAM_STEP)
    return -ADAM_LR * (m_hat / (jnp.sqrt(v_hat) + ADAM_EPS) + ADAM_WD * w), m, v


def _adamw(w, g, m, v, *, g_at=None, name):
    R, C = w.shape
    row0 = 0 if g_at is None else g_at[1]
    tr = min(math.gcd(row0, 256) if row0 else 256, -(-R // 8) * 8)
    g0 = row0 // tr
    if g_at is None:
        g_spec = pl.BlockSpec((tr, C), lambda i: (i, 0))
    else:
        g_spec = pl.BlockSpec((None, tr, C), lambda i: (g_at[0], g0 + i, 0))

    def body(w_ref, g_ref, m_ref, v_ref, d_ref, mo_ref, vo_ref):
        d, mn, vn = _adamw_math(w_ref[...], g_ref[...], m_ref[...], v_ref[...])
        d_ref[...] = d
        mo_ref[...] = mn
        vo_ref[...] = vn

    blk = pl.BlockSpec((tr, C), lambda i: (i, 0))
    return pl.pallas_call(
        body, name=name, grid=(pl.cdiv(R, tr),),
        in_specs=[blk, g_spec, blk, blk],
        out_specs=[blk, blk, blk],
        out_shape=[jax.ShapeDtypeStruct((R, C), F32)] * 3,
        compiler_params=_cp("parallel"),
    )(w, g, m, v)


def _sum_parts(parts, *, name):
    P, R, C = parts.shape

    def body(p_ref, o_ref):
        acc = p_ref[0]
        for p in range(1, P):
            acc = acc + p_ref[p]
        o_ref[...] = acc

    return pl.pallas_call(
        body, name=name, grid=(1,),
        in_specs=[pl.BlockSpec((P, R, C), lambda i: (0, 0, 0))],
        out_specs=pl.BlockSpec((R, C), lambda i: (0, 0)),
        out_shape=jax.ShapeDtypeStruct((R, C), F32),
        compiler_params=_cp("arbitrary"),
    )(parts)


def _add_halves(g4, recv, c_idx, *, name):
    _, _, Rh, C = g4.shape
    tr = min(256, Rh)

    def body(c_ref, a_ref, b_ref, o_ref):
        o_ref[...] = (a_ref[...] + b_ref[...].astype(F32)).astype(BF)

    return pl.pallas_call(
        body, name=name,
        grid_spec=pltpu.PrefetchScalarGridSpec(
            num_scalar_prefetch=1, grid=(4, pl.cdiv(Rh, tr)),
            in_specs=[pl.BlockSpec((None, None, tr, C), lambda j, r, c: (j, c[0], r, 0)),
                      pl.BlockSpec((None, tr, C), lambda j, r, c: (j, r, 0))],
            out_specs=pl.BlockSpec((None, tr, C), lambda j, r, c: (j, r, 0))),
        out_shape=jax.ShapeDtypeStruct((4, Rh, C), BF),
        compiler_params=_cp("parallel", "arbitrary"),
    )(c_idx, g4, recv)


def _add_four(g4, from_sibling, from_chips, pos, *, name):
    _, _, Rh, C = g4.shape
    tr = min(256, Rh)

    def body(p_ref, a_ref, s_ref, b_ref, o_ref):
        own = a_ref[...] + s_ref[...].astype(F32)
        o_ref[...] = ((own + b_ref[0].astype(F32)) + b_ref[1].astype(F32)) + b_ref[2].astype(F32)

    return pl.pallas_call(
        body, name=name,
        grid_spec=pltpu.PrefetchScalarGridSpec(
            num_scalar_prefetch=1, grid=(pl.cdiv(Rh, tr),),
            in_specs=[pl.BlockSpec((None, None, tr, C), lambda r, p: (p[0], p[1], r, 0)),
                      pl.BlockSpec((None, tr, C), lambda r, p: (p[0], r, 0)),
                      pl.BlockSpec((3, tr, C), lambda r, p: (0, r, 0))],
            out_specs=pl.BlockSpec((None, tr, C), lambda r, p: (p[1], r, 0))),
        out_shape=jax.ShapeDtypeStruct((2, Rh, C), F32),
        compiler_params=_cp("arbitrary"),
    )(pos, g4, from_sibling, from_chips)


HBM = pl.BlockSpec(memory_space=pltpu.HBM)


def _mesh_pos():
    return lax.axis_index("x"), lax.axis_index("y"), lax.axis_index("c")


def _other_chips(x, y):
    return [(1 - x, y), (x, 1 - y), (1 - x, 1 - y)]


def _allgather_small(xs, *, name):
    m_per, n = xs.shape

    def body(x_ref, out_ref, send_sems, recv_sems, local_sem):
        x, y, c = _mesh_pos()
        me, sibling = (x, y, c), (x, y, 1 - c)
        chips = _other_chips(x, y)

        def rows(px, py, pc):
            return out_ref.at[pl.ds((4 * px + 2 * py + pc) * m_per, m_per), :]

        def copy(k, block, to, src=None):
            return pltpu.make_async_remote_copy(
                src_ref=rows(*block) if src is None else src, dst_ref=rows(*block),
                send_sem=send_sems.at[k], recv_sem=recv_sems.at[k], device_id=to, device_id_type=MESH)

        mine = pltpu.make_async_copy(x_ref, rows(*me), local_sem)
        mine.start()
        first = [copy(0, me, sibling, src=x_ref)]
        first += [copy(1 + j, me, (*chip, c), src=x_ref) for j, chip in enumerate(chips)]
        for cp in first:
            cp.start()
        passed = [copy(4 + j, (*chip, c), sibling) for j, chip in enumerate(chips)]
        for j, chip in enumerate(chips):
            copy(1 + j, (*chip, c), me).wait_recv()
            passed[j].start()
        copy(0, sibling, me).wait_recv()
        for j, chip in enumerate(chips):
            copy(4 + j, (*chip, 1 - c), me).wait_recv()
        for cp in first + passed:
            cp.wait_send()
        mine.wait()

    return pl.pallas_call(
        body, name=name,
        out_shape=jax.ShapeDtypeStruct((N_DEV * m_per, n), xs.dtype),
        in_specs=[pl.BlockSpec(memory_space=pltpu.VMEM)],
        out_specs=pl.BlockSpec(memory_space=pltpu.VMEM),
        scratch_shapes=[pltpu.SemaphoreType.DMA((7,)), pltpu.SemaphoreType.DMA((7,)), pltpu.SemaphoreType.DMA],
    )(xs)


def _chip_slab_copies(s_ref, out_ref, send_sems, recv_sems):
    R = s_ref.shape[0]
    Rh = R // 2
    x, y, c = _mesh_pos()
    me, sibling = (x, y, c), (x, y, 1 - c)
    chips = _other_chips(x, y)

    def half(px, py, pc):
        return out_ref.at[2 * px + py, pl.ds(pc * Rh, Rh), :]

    def copy(k, block, to, src=None):
        return pltpu.make_async_remote_copy(
            src_ref=half(*block) if src is None else src, dst_ref=half(*block),
            send_sem=send_sems.at[k], recv_sem=recv_sems.at[k], device_id=to, device_id_type=MESH)

    first = [copy(j, me, (*chip, c), src=s_ref.at[pl.ds(c * Rh, Rh), :]) for j, chip in enumerate(chips)]
    passed = [copy(3 + j, (*chip, c), sibling) for j, chip in enumerate(chips)]
    landed = [copy(j, (*chip, c), me) for j, chip in enumerate(chips)]
    from_sibling = [copy(3 + j, (*chip, 1 - c), me) for j, chip in enumerate(chips)]
    return first, passed, landed, from_sibling


def _gather_behind(s_ref, out_ref, send_sems, recv_sems, step, nsteps):
    first, passed, landed, from_sibling = _chip_slab_copies(s_ref, out_ref, send_sems, recv_sems)

    @pl.when(step == 0)
    def _():
        for cp in first:
            cp.start()

    @pl.when(step == (3 * nsteps) // 4)
    def _():
        for arrived, onward in zip(landed, passed):
            arrived.wait_recv()
            onward.start()

    def finish():
        @pl.when(step == nsteps - 1)
        def _():
            for cp in from_sibling:
                cp.wait_recv()
            for cp in first + passed:
                cp.wait_send()

    return finish


def _allgather_chip_slabs(slab, *, name):
    R, C = slab.shape

    def body(s_ref, out_ref, send_sems, recv_sems):
        first, passed, landed, from_sibling = _chip_slab_copies(s_ref, out_ref, send_sems, recv_sems)
        for cp in first:
            cp.start()
        for arrived, onward in zip(landed, passed):
            arrived.wait_recv()
            onward.start()
        for cp in from_sibling:
            cp.wait_recv()
        for cp in first + passed:
            cp.wait_send()

    return pl.pallas_call(
        body, name=name,
        out_shape=jax.ShapeDtypeStruct((N_CHIPS, R, C), slab.dtype),
        in_specs=[HBM], out_specs=HBM,
        scratch_shapes=[pltpu.SemaphoreType.DMA((6,)), pltpu.SemaphoreType.DMA((6,))],
    )(slab)


def _swap_halves(mine, *, name):
    def body(g_ref, out_ref, send_sems, recv_sems):
        x, y, c = _mesh_pos()
        copies = [pltpu.make_async_remote_copy(
            src_ref=g_ref.at[j], dst_ref=out_ref.at[j], send_sem=send_sems.at[j], recv_sem=recv_sems.at[j],
            device_id=(x, y, 1 - c), device_id_type=MESH) for j in range(N_CHIPS)]
        for cp in copies:
            cp.start()
        for cp in copies:
            cp.wait()

    return pl.pallas_call(
        body, name=name,
        out_shape=jax.ShapeDtypeStruct(mine.shape, mine.dtype),
        in_specs=[HBM], out_specs=HBM,
        scratch_shapes=[pltpu.SemaphoreType.DMA((N_CHIPS,)), pltpu.SemaphoreType.DMA((N_CHIPS,))],
    )(mine)


def _scatter_copies(p_ref, out_ref, send_sems, recv_sems):
    x, y, c = _mesh_pos()
    return [pltpu.make_async_remote_copy(
        src_ref=p_ref.at[2 * px + py], dst_ref=out_ref.at[j], send_sem=send_sems.at[j], recv_sem=recv_sems.at[j],
        device_id=(px, py, c), device_id_type=MESH) for j, (px, py) in enumerate(_other_chips(x, y))]


def _scatter_partials(part, *, name):
    _, Rh, C = part.shape

    def body(p_ref, out_ref, send_sems, recv_sems):
        copies = _scatter_copies(p_ref, out_ref, send_sems, recv_sems)
        for cp in copies:
            cp.start()
        for cp in copies:
            cp.wait()

    return pl.pallas_call(
        body, name=name,
        out_shape=jax.ShapeDtypeStruct((3, Rh, C), part.dtype),
        in_specs=[HBM], out_specs=HBM,
        scratch_shapes=[pltpu.SemaphoreType.DMA((3,)), pltpu.SemaphoreType.DMA((3,))],
    )(part)


def _join_halves(buf, *, name):
    def body(b_ref, out_ref, send_sem, recv_sem):
        x, y, c = _mesh_pos()
        cp = pltpu.make_async_remote_copy(
            src_ref=b_ref.at[c], dst_ref=out_ref.at[c], send_sem=send_sem, recv_sem=recv_sem,
            device_id=(x, y, 1 - c), device_id_type=MESH)
        cp.start()
        cp.wait()

    return pl.pallas_call(
        body, name=name,
        out_shape=jax.ShapeDtypeStruct(buf.shape, buf.dtype),
        in_specs=[HBM], out_specs=HBM, input_output_aliases={0: 0},
        scratch_shapes=[pltpu.SemaphoreType.DMA, pltpu.SemaphoreType.DMA],
    )(buf)


def _pad_rows(a, mult):
    pad = (-a.shape[0]) % mult
    return a if pad == 0 else jnp.pad(a, ((0, pad),) + ((0, 0),) * (a.ndim - 1))


def _local_step(x, target, mod, wts, small, slabs=None, unpacks=None, reduce_early=None, grad_slab=None):
    S, D = x.shape
    HP = D // LANES
    row = lambda v: v.reshape(1, -1)
    msplit = [[row(mod[i, k * D:(k + 1) * D]) for k in range(6)] for i in range(2)]
    gw, gs = {}, {}
    dmod = [[None] * 6 for _ in range(2)]
    slab, where = grad_slab if grad_slab is not None else (None, {})

    def dw(key, a, b, name):
        nonlocal slab
        if key in where:
            slab = _matmul_tn(a, b, name=name, into=(slab,) + where[key])
        else:
            gw[key] = _matmul_tn(a, b, name=name)

    sh1, sc1, g1, sh2, sc2, g2 = msplit[0]
    n1w0, n2w0 = row(small["norm1_w"][0]), row(small["norm2_w"][0])
    slabs = slabs if slabs is not None else (None, None)
    proj0, h1_0, *gathered = _ln_matmul(x, n1w0, sc1, sh1, wts["hg_w_in"], slabs[0], relu2=False, name="hg_in_proj")
    if slabs[0] is not None:
        wts = {**wts, **unpacks[0](gathered[0])}
    gn = small["hg_gn_w"].reshape(1, LANES)
    ypre0, o0, states, *gathered = _hg_fwd(proj0, small["hg_lb"], gn, slabs[1], name="hg_fwd")
    if slabs[1] is not None:
        wts = {**wts, **unpacks[1](gathered[0])}
    x1, ymix0 = _matmul_resid(ypre0, wts["hg_w_out"], x, g1, name="hg_out_proj")
    a0, u0, h2_0 = _ln_matmul(x1, n2w0, sc2, sh2, wts["mlp_w1_0"], relu2=True, name="mlp0_up")
    x2, ymlp0 = _matmul_resid(u0, wts["mlp_w2_0"], x1, g2, name="mlp0_down")

    sh1b, sc1b, g1b, sh2b, sc2b, g2b = msplit[1]
    n1w1, n2w1 = row(small["norm1_w"][1]), row(small["norm2_w"][1])
    proj1, h1_1 = _ln_matmul(x2, n1w1, sc1b, sh1b, wts["fox_w_in"], relu2=False, name="fox_in_proj")
    nheads = 2 * HP
    bf_pad = jnp.pad(small["fox_b_f"].reshape(1, nheads), ((0, 0), (0, LANES - nheads)))
    qw2 = jnp.tile(small["fox_qn_w"].reshape(1, FOX_DH), (1, 2))
    kw2 = jnp.tile(small["fox_kn_w"].reshape(1, FOX_DH), (1, 2))
    fcum = _fox_cumsum(proj1, bf_pad, name="fox_cumsum")
    qa, ka, va, vat = _fox_prep(proj1, fcum, qw2, kw2, name="fox_prep")
    jmin, imax = _fox_skip_bounds(fcum, small["fox_qn_w"], small["fox_kn_w"], nheads)
    ypre1, o1, q2 = _fox_fwd(jmin, qa, ka, vat, proj1, name="fox_fwd")
    x3, ymix1 = _matmul_resid(ypre1, wts["fox_w_out"], x2, g1b, name="fox_out_proj")
    a1, u1, h2_1 = _ln_matmul(x3, n2w1, sc2b, sh2b, wts["mlp_w1_1"], relu2=True, name="mlp1_up")
    x4, ymlp1 = _matmul_resid(u1, wts["mlp_w2_1"], x3, g2b, name="mlp1_down")

    loss, dx4, dfw = _loss_kernel(x4, row(small["final_w"]), target, name="loss")
    gs["final_w"] = dfw.reshape(-1)

    def mlp_bwd(i, dx_out, x_in, h2, a, u, ymlp, n2w, sc2_, g2_):
        dz, dm, dg2 = _gate_matmul_nt(dx_out, g2_, ymlp, wts[f"mlp_w2_{i}"], a, name=f"mlp{i}_down_bwd")
        dw(f"mlp_w2_{i}", u, dm[None], f"mlp{i}_dw2")
        dw(f"mlp_w1_{i}", h2, dz[None], f"mlp{i}_dw1")
        dx_in, dsc, dsh, dnw = _matmul_nt_lnbwd(dz[None], wts[f"mlp_w1_{i}"], x_in, n2w, sc2_, dx_out,
                                                name=f"mlp{i}_up_bwd")
        dmod[i][3], dmod[i][4], dmod[i][5] = dsh, dsc, dg2
        return dx_in, dnw

    dx3, dn2w1 = mlp_bwd(1, dx4, x3, h2_1, a1, u1, ymlp1, n2w1, sc2b, g2b)
    dyp1, dm1, dg1b = _gate_matmul_nt(dx3, g1b, ymix1, wts["fox_w_out"], None, name="fox_out_bwd")
    dw("fox_w_out", ypre1, dm1[None], "fox_dw_out")
    doa = _fox_bwd_prep(dyp1, o1, proj1, q2, name="fox_bwd_prep")
    dqa, dka, dva, colsum = _fox_bwd(imax, q2, ka, va, doa, name="fox_bwd")
    colsum = jnp.pad(colsum[:, 0, :].T, ((0, 0), (0, LANES - nheads)))
    dproj1, dqw, dkw = _fox_bwd_post(dqa, dka, dva, proj1, dyp1, o1, qw2, kw2, name="fox_bwd_post")
    dproj1, dbf = _fox_dfz(colsum, nheads, proj1, bf_pad, dproj1, name="fox_dfz")
    dw("fox_w_in", h1_1, dproj1, "fox_dw_in")
    dx2, dsc, dsh, dn1w1 = _matmul_nt_lnbwd(dproj1, wts["fox_w_in"], x2, n1w1, sc1b, dx3, name="fox_in_bwd")
    dmod[1][0], dmod[1][1], dmod[1][2] = dsh, dsc, dg1b
    gs["fox_qn_w"] = dqw[0, :FOX_DH] + dqw[0, FOX_DH:]
    gs["fox_kn_w"] = dkw[0, :FOX_DH] + dkw[0, FOX_DH:]
    gs["fox_b_f"] = dbf[0, :nheads]

    dx1, dn2w0 = mlp_bwd(0, dx2, x1, h2_0, a0, u0, ymlp0, n2w0, sc2, g2)
    dyp0, dm0, dg1 = _gate_matmul_nt(dx1, g1, ymix0, wts["hg_w_out"], None, name="hg_out_bwd")
    dw("hg_w_out", ypre0, dm0[None], "hg_dw_out")
    part, ctx = reduce_early(gw, slab) if reduce_early is not None else (None, None)
    dproj0, dlb, dgn, *from_chips = _hg_bwd(proj0, small["hg_lb"], gn, o0, states, dyp0, part, name="hg_bwd")
    early = (ctx, from_chips[0]) if reduce_early is not None else None
    dw("hg_w_in", h1_0, dproj0, "hg_dw_in")
    dx0, dsc, dsh, dn1w0 = _matmul_nt_lnbwd(dproj0, wts["hg_w_in"], x, n1w0, sc1, dx1, name="hg_in_bwd")
    dmod[0][0], dmod[0][1], dmod[0][2] = dsh, dsc, dg1
    gs["hg_lb"] = dlb
    gs["hg_gn_w"] = jnp.sum(dgn, axis=0)

    gs["norm1_w"] = jnp.concatenate([dn1w0, dn1w1], axis=0)
    gs["norm2_w"] = jnp.concatenate([dn2w0, dn2w1], axis=0)
    gs["dmod"] = jnp.stack([jnp.concatenate(dmod[i], axis=1)[0] for i in range(2)])
    return loss, dx0, gw, gs, early


def _pack_halves(layout):
    rh = -(-max(sum(a.shape[0] for _, a in half) for half in layout) // 16) * 16
    place, parts = {}, []
    for h, half in enumerate(layout):
        off = 0
        for n, a in half:
            place[n] = (h, off, a.shape[0])
            off += a.shape[0]
        parts.append(jnp.pad(jnp.concatenate([a.astype(BF) for _, a in half], axis=0), ((0, rh - off), (0, 0))))
    return jnp.concatenate(parts, axis=0), place, rh


SMALL_NAMES = ["norm1_w", "norm2_w", "hg_lb", "hg_gn_w", "fox_b_f", "fox_qn_w", "fox_kn_w", "final_w"]


def _pack_small(d, names):
    rows, offs, r0 = [], {}, 0
    for n in names:
        flat = d[n].reshape(-1)
        nr = -(-flat.shape[0] // LANES)
        rows.append(jnp.pad(flat, (0, nr * LANES - flat.shape[0])).reshape(nr, LANES))
        offs[n] = (r0, nr)
        r0 += nr
    return jnp.concatenate(rows, axis=0), offs


def _unpack_small(packed, offs, name, like):
    r0, nr = offs[name]
    return packed[r0:r0 + nr].reshape(-1)[:like.size].reshape(like.shape)


def kernel(x, c, w_mod, b_mod, norm1_w, norm2_w, hg_w_in, hg_w_out, hg_lb, hg_gn_w, fox_w_in, fox_b_f, fox_qn_w, fox_kn_w, fox_w_out, mlp_w1, mlp_w2, final_w, loss_target, m_w_mod, m_b_mod, m_norm1_w, m_norm2_w, m_hg_w_in, m_hg_w_out, m_hg_lb, m_hg_gn_w, m_fox_w_in, m_fox_b_f, m_fox_qn_w, m_fox_kn_w, m_fox_w_out, m_mlp_w1, m_mlp_w2, m_final_w, v_w_mod, v_b_mod, v_norm1_w, v_norm2_w, v_hg_w_in, v_hg_w_out, v_hg_lb, v_hg_gn_w, v_fox_w_in, v_fox_b_f, v_fox_qn_w, v_fox_kn_w, v_fox_w_out, v_mlp_w1, v_mlp_w2, v_final_w):
    S, D = x.shape[1], x.shape[2]
    nheads = D // FOX_DH
    ax, ay, ac = _mesh_pos()
    chip = 2 * ax + ay
    dev = 2 * chip + ac
    xs, tgt = x.reshape(S, D), loss_target.reshape(S, D)

    c_all = _allgather_small(_pad_rows(c.reshape(-1, LANES), 8), name="gather_c")
    c_all = c_all.reshape(N_DEV, -1)[:, :D]
    c16 = _pad_rows(c_all, 16)
    nmod = w_mod.shape[2]
    b_shard = lax.dynamic_slice_in_dim(b_mod, chip * nmod, nmod, axis=1)
    mod_shard = _mod_fwd(c16, w_mod, b_shard[:, None, :], name="mod_fwd")[:, :N_DEV]
    mod_all = _allgather_small(mod_shard.reshape(-1, LANES), name="gather_mod")
    mod_all = mod_all.reshape(N_CHIPS, 2, 2, N_DEV, nmod)[:, 0]
    mod = lax.dynamic_index_in_dim(mod_all, dev, axis=2, keepdims=False)
    mod = mod.transpose(1, 0, 2).reshape(2, N_CHIPS * nmod)

    fox_rows = fox_w_in.shape[2]
    col = lambda g: g.transpose(1, 0, 2).reshape(g.shape[1], -1)
    rowsh = lambda g: g.reshape(-1, g.shape[2])
    own = lambda g, s: lax.dynamic_update_index_in_dim(g, s, chip, 0)

    slab_in = hg_w_in[0].astype(BF)
    wts = {"hg_w_in": col(own(_allgather_chip_slabs(slab_in, name="gather_hg_w_in"), slab_in))}
    fox_flat, fox_cut = fox_w_in[0].reshape(fox_rows, D), fox_rows // 2
    slabs, unpacks = [], []
    for layout_w in ([[("mlp_w1_0", mlp_w1[0]), ("hg_w_out", hg_w_out[0])], [("mlp_w2_0", mlp_w2[0]), ("fox_w_out", fox_w_out[0])]],
                     [[("mlp_w1_1", mlp_w1[1]), ("fox_a", fox_flat[:fox_cut])], [("mlp_w2_1", mlp_w2[1]), ("fox_b", fox_flat[fox_cut:])]]):
        slab_w, place_w, rh_w = _pack_halves(layout_w)

        def unpack(gathered, slab_w=slab_w, place_w=place_w, rh_w=rh_w):
            gathered = own(gathered, slab_w)
            out = {}
            for n, (h, off, rows) in place_w.items():
                g = gathered[:, h * rh_w + off:h * rh_w + off + rows, :]
                out[n] = col(g) if n.startswith("mlp_w1") else rowsh(g) if n.startswith(("mlp_w2", "hg_", "fox_w")) else g
            if "fox_a" in out:
                fox_in = col(jnp.concatenate([out.pop("fox_a"), out.pop("fox_b")], axis=1).reshape(N_CHIPS, D, fox_rows))
                out["fox_w_in"] = jnp.pad(fox_in, ((0, 0), (0, 5 * D - fox_in.shape[1])))
            return out

        slabs.append(slab_w)
        unpacks.append(unpack)

    small = {"norm1_w": norm1_w, "norm2_w": norm2_w, "hg_lb": hg_lb, "hg_gn_w": hg_gn_w, "fox_b_f": fox_b_f,
             "fox_qn_w": fox_qn_w, "fox_kn_w": fox_kn_w, "final_w": final_w}

    def uncol(g, n):
        return g.reshape(g.shape[0], N_CHIPS, n).transpose(1, 0, 2)

    pos = jnp.stack([chip, ac])

    def swap_and_add(g4, tag):
        to_sibling = lax.dynamic_index_in_dim(g4, 1 - ac, axis=1, keepdims=False).astype(BF)
        from_sibling = _swap_halves(to_sibling, name=f"rs_swap_{tag}")
        return from_sibling, _add_halves(g4, from_sibling, ac.reshape(1), name=f"rs_add_halves_{tag}")

    def finish(g4, from_sibling, from_chips, tag):
        my_half = _add_four(g4, from_sibling, from_chips, pos, name=f"rs_add_chips_{tag}")
        return _join_halves(my_half, name=f"rs_join_{tag}")

    layout = [[("mlp_w1", 2 * D), ("hg_w_out", D // 4), ("fox_w_out", D // 4)], [("mlp_w2", 2 * D), ("fox_w_in", fox_rows)]]
    place = {}
    for h, half in enumerate(layout):
        off = 0
        for n, rows in half:
            place[n] = (h, off, rows)
            off += rows

    rh = -(-max(sum(rows for _, rows in half) for half in layout) // 16) * 16
    where = {"hg_w_out": ("row",) + place["hg_w_out"][:2], "fox_w_out": ("row",) + place["fox_w_out"][:2]}
    for i in range(2):
        where[f"mlp_w1_{i}"] = ("col", place["mlp_w1"][0], place["mlp_w1"][1] + i * D)
        where[f"mlp_w2_{i}"] = ("row", place["mlp_w2"][0], place["mlp_w2"][1] + i * D)

    def reduce_early(gw, slab):
        gfox = uncol(gw["fox_w_in"][:, :4 * fox_rows], fox_rows).reshape(N_CHIPS, 1, fox_rows, D)
        h, off, _ = place["fox_w_in"]
        slab = lax.dynamic_update_slice(slab, gfox, (0, h, off, 0))
        for h, half in enumerate(layout):
            used = sum(rows for _, rows in half)
            if used < rh:
                slab = lax.dynamic_update_slice(slab, jnp.zeros((N_CHIPS, 1, rh - used, D), F32), (0, h, used, 0))
        from_sibling, part = swap_and_add(slab, "early")
        return part, (slab, from_sibling)

    loss_part, grad_x, gw, gs, ((g4, from_sibling), from_chips) = _local_step(
        xs, tgt, mod, wts, small, slabs, unpacks, reduce_early, (lax.empty((N_CHIPS, 2, rh, D), F32), where))
    loss = lax.psum(loss_part[0, 0], ("x", "y", "c"))
    gshard = finish(g4, from_sibling, from_chips, "early")

    g4 = uncol(gw["hg_w_in"], D).reshape(N_CHIPS, 2, D // 2, D)
    from_sibling, part = swap_and_add(g4, "late")
    g_hg_w_in = finish(g4, from_sibling, _scatter_partials(part, name="rs_scatter_late"), "late").reshape(D, D)

    names = ["dmod"] + SMALL_NAMES
    packed, offs = _pack_small(gs, names)
    packed = _pad_rows(packed, 8)
    rp = packed.shape[0]
    parts = _allgather_small(packed, name="gather_small").reshape(N_DEV, rp, LANES)
    total = _sum_parts(parts, name="sum_small")
    r0, nr = offs["dmod"]
    dmod_all = parts[:, r0:r0 + nr].reshape(N_DEV, 2, N_CHIPS * nmod)
    dmod_shard = lax.dynamic_slice_in_dim(dmod_all, chip * nmod, nmod, axis=2).transpose(1, 0, 2)
    g_w_mod = _mod_bwd(c16, jnp.pad(dmod_shard, ((0, 0), (0, 16 - N_DEV), (0, 0))), name="mod_bwd")

    grads = {"w_mod": g_w_mod, "b_mod": _unpack_small(total, offs, "dmod", b_mod)}
    for n in SMALL_NAMES:
        grads[n] = _unpack_small(total, offs, n, small[n])

    given = dict(w_mod=(w_mod, m_w_mod, v_w_mod), b_mod=(b_mod, m_b_mod, v_b_mod), norm1_w=(norm1_w, m_norm1_w, v_norm1_w),
                 norm2_w=(norm2_w, m_norm2_w, v_norm2_w), hg_w_in=(hg_w_in, m_hg_w_in, v_hg_w_in),
                 hg_w_out=(hg_w_out, m_hg_w_out, v_hg_w_out), hg_lb=(hg_lb, m_hg_lb, v_hg_lb),
                 hg_gn_w=(hg_gn_w, m_hg_gn_w, v_hg_gn_w), fox_w_in=(fox_w_in, m_fox_w_in, v_fox_w_in),
                 fox_b_f=(fox_b_f, m_fox_b_f, v_fox_b_f), fox_qn_w=(fox_qn_w, m_fox_qn_w, v_fox_qn_w),
                 fox_kn_w=(fox_kn_w, m_fox_kn_w, v_fox_kn_w), fox_w_out=(fox_w_out, m_fox_w_out, v_fox_w_out),
                 mlp_w1=(mlp_w1, m_mlp_w1, v_mlp_w1), mlp_w2=(mlp_w2, m_mlp_w2, v_mlp_w2), final_w=(final_w, m_final_w, v_final_w))
    upd = {}

    for n, (h, off, rows) in place.items():
        w, m, v = given[n]
        flat = lambda a: a.reshape(rows, D)
        d, mn, vn = _adamw(flat(w), gshard, flat(m), flat(v), g_at=(h, off), name=f"adamw_{n}")
        grads[n] = gshard[h, off:off + rows].reshape(w.shape)
        upd[n] = tuple(a.reshape(w.shape) for a in (d, mn, vn))

    w, m, v = given["hg_w_in"]
    grads["hg_w_in"] = g_hg_w_in.reshape(w.shape)
    upd["hg_w_in"] = tuple(a.reshape(w.shape) for a in _adamw(w[0], g_hg_w_in, m[0], v[0], name="adamw_hg_w_in"))

    w, m, v = given["w_mod"]
    flat = lambda a: a.reshape(-1, nmod)
    upd["w_mod"] = tuple(a.reshape(w.shape) for a in _adamw(flat(w), flat(g_w_mod), flat(m), flat(v), name="adamw_w_mod"))

    snames = ["b_mod"] + SMALL_NAMES
    pw, soffs = _pack_small({n: given[n][0] for n in snames}, snames)
    pm, _ = _pack_small({n: given[n][1] for n in snames}, snames)
    pv, _ = _pack_small({n: given[n][2] for n in snames}, snames)
    pg, _ = _pack_small({n: grads[n] for n in snames}, snames)
    pw, pm, pv, pg = (_pad_rows(a, 8) for a in (pw, pm, pv, pg))
    sd, smn, svn = _adamw(pw, pg, pm, pv, name="adamw_small")
    for n in snames:
        like = given[n][0]
        upd[n] = tuple(_unpack_small(a, soffs, n, like) for a in (sd, smn, svn))

    order = ["w_mod", "b_mod", "norm1_w", "norm2_w", "hg_w_in", "hg_w_out", "hg_lb", "hg_gn_w", "fox_w_in", "fox_b_f",
             "fox_qn_w", "fox_kn_w", "fox_w_out", "mlp_w1", "mlp_w2", "final_w"]
    return (loss, grad_x.reshape(x.shape), *[grads[n] for n in order], *[upd[n][0] for n in order],
            *[upd[n][1] for n in order], *[upd[n][2] for n in order])
```

```python
import math

import jax
import jax.numpy as jnp
from jax import lax
from jax.experimental import pallas as pl
from jax.experimental.pallas import tpu as pltpu

EPS = 1e-6
ADAM_LR, ADAM_B1, ADAM_B2, ADAM_EPS, ADAM_WD, ADAM_STEP = 0.001, 0.9, 0.999, 1e-08, 0.01, 10

F32 = jnp.float32
BF = jnp.bfloat16
LANES = 128
HG_CHUNK = 64
HG_HEADS_PER_STEP = 8
HG_TOKENS_PER_STEP = 256
FOX_BWD_TILES = (8, 4, 2, 1)
LOG2E = 1.4426950408889634
FOX_DH = 64
N_CHIPS = 4
N_DEV = 8
VMEM_LIMIT = 56 * 1024 * 1024
MESH = pl.DeviceIdType.MESH

NT = (((1,), (1,)), ((), ()))
TN = (((0,), (0,)), ((), ()))


def _pick(n, pref, mult=LANES):
    if n <= pref:
        return n
    t = (pref // mult) * mult
    while t >= mult:
        if n % t == 0:
            return t
        t -= mult
    raise ValueError((n, pref, mult))


def _cp(*sem):
    return pltpu.CompilerParams(dimension_semantics=sem, vmem_limit_bytes=VMEM_LIMIT)


def _dot(a, b):
    return jnp.dot(a, b, preferred_element_type=F32)


def _dg(a, b, dims):
    return lax.dot_general(a, b, dims, preferred_element_type=F32)


def _split3(x):
    hi = x.astype(BF)
    r1 = x - hi.astype(F32)
    mid = r1.astype(BF)
    lo = (r1 - mid.astype(F32)).astype(BF)
    return hi, mid, lo


def _tri_dot(tri, x):
    hi, mid, lo = _split3(x)
    return _dot(tri, hi) + _dot(tri, mid) + _dot(tri, lo)


def _dg3(a, b, dims):
    ah, bh = a.astype(BF), b.astype(BF)
    al, bl = (a - ah.astype(F32)).astype(BF), (b - bh.astype(F32)).astype(BF)
    return _dg(ah, bh, dims) + _dg(ah, bl, dims) + _dg(al, bh, dims)


def _dg1(a, b, dims):
    return _dg(a.astype(BF), b.astype(BF), dims)


NN = (((1,), (0,)), ((), ()))


def _sigmoid(x):
    return jax.nn.sigmoid(x)


def _ln_matmul(x, nw, sc, sh, w, slab=None, *, relu2, name):
    S, D = x.shape
    N = w.shape[1]
    tm, tn = _pick(S, 512, 16), N
    fused = slab is not None

    def body(x_ref, nw_ref, sc_ref, sh_ref, w_ref, *rest):
        if fused:
            s_ref, *outs, out_ref, hs, send_sems, recv_sems = rest
            finish = _gather_behind(s_ref, out_ref, send_sems, recv_sems, pl.program_id(0), S // tm)
        else:
            outs, hs = rest[:-1], rest[-1]
        h_ref = outs[-1]

        @pl.when(pl.program_id(1) == 0)
        def _():
            xv = x_ref[...]
            r = lax.rsqrt(jnp.mean(xv * xv, axis=-1, keepdims=True) + EPS)
            hb = ((xv * r * nw_ref[...]) * (1.0 + sc_ref[...]) + sh_ref[...]).astype(BF)
            hs[...] = hb
            h_ref[...] = hb

        z = _dot(hs[...], w_ref[...])
        if relu2:
            a = jnp.maximum(z, 0.0)
            outs[0][...] = a.astype(BF)
            outs[1][...] = (a * a).astype(BF)
        else:
            outs[0][...] = z
        if fused:
            finish()

    vec = pl.BlockSpec((1, D), lambda i, j: (0, 0))
    tile = pl.BlockSpec((tm, tn), lambda i, j: (i, j))
    if relu2:
        out_shape = [jax.ShapeDtypeStruct((S, N), BF), jax.ShapeDtypeStruct((S, N), BF)]
        out_specs = [tile, tile]
    else:
        out_shape = [jax.ShapeDtypeStruct((S, N), F32)]
        out_specs = [tile]
    out_shape.append(jax.ShapeDtypeStruct((S, D), BF))
    out_specs.append(pl.BlockSpec((tm, D), lambda i, j: (i, 0)))
    in_specs = [pl.BlockSpec((tm, D), lambda i, j: (i, 0)), vec, vec, vec, pl.BlockSpec((D, tn), lambda i, j: (0, j))]
    scratch = [pltpu.VMEM((tm, D), BF)]
    args = [x, nw, sc, sh, w]
    if fused:
        in_specs.append(HBM)
        out_specs.append(HBM)
        out_shape.append(jax.ShapeDtypeStruct((N_CHIPS,) + slab.shape, slab.dtype))
        scratch += [pltpu.SemaphoreType.DMA((6,)), pltpu.SemaphoreType.DMA((6,))]
        args.append(slab)
    return pl.pallas_call(
        body, name=name, grid=(S // tm, N // tn), in_specs=in_specs, out_specs=out_specs, out_shape=out_shape,
        scratch_shapes=scratch, compiler_params=_cp("arbitrary", "arbitrary"),
    )(*args)


def _matmul_resid(a, w, x, gate, *, name):
    S, K = a.shape
    D = w.shape[1]
    tm, tn = _pick(S, 1024 if K <= 1024 else 512, 16), D

    def body(a_ref, w_ref, x_ref, g_ref, o_ref, y_ref):
        y = _dot(a_ref[...], w_ref[...])
        y_ref[...] = y.astype(BF)
        o_ref[...] = x_ref[...] + g_ref[...] * y

    tile = pl.BlockSpec((tm, tn), lambda i, j: (i, j))
    return pl.pallas_call(
        body, name=name, grid=(S // tm, D // tn),
        in_specs=[pl.BlockSpec((tm, K), lambda i, j: (i, 0)), pl.BlockSpec((K, tn), lambda i, j: (0, j)),
                  tile, pl.BlockSpec((1, tn), lambda i, j: (0, j))],
        out_specs=[tile, tile],
        out_shape=[jax.ShapeDtypeStruct((S, D), F32), jax.ShapeDtypeStruct((S, D), BF)],
        compiler_params=_cp("parallel", "arbitrary"),
    )(a, w, x, gate)


def _gate_matmul_nt(dx, gate, y, w, act, *, name):
    S, D = dx.shape
    K = w.shape[0]
    tm, tn = _pick(S, 1024 if K <= 1024 else 512, 16), K
    fused = act is not None

    def body(dx_ref, g_ref, y_ref, w_ref, *rest):
        if fused:
            act_ref, da_ref, dm_ref, dg_ref, ms = rest
        else:
            da_ref, dm_ref, dg_ref, ms = rest
        i, j = pl.program_id(0), pl.program_id(1)

        @pl.when((i == 0) & (j == 0))
        def _():
            dg_ref[...] = jnp.zeros_like(dg_ref)

        @pl.when(j == 0)
        def _():
            dxv = dx_ref[...]
            dmb = (dxv * g_ref[...]).astype(BF)
            ms[...] = dmb
            dm_ref[...] = dmb
            dg_ref[...] += jnp.sum(dxv * y_ref[...].astype(F32), axis=0, keepdims=True)

        da = _dg(ms[...], w_ref[...], NT)
        if fused:
            da_ref[...] = (da * (2.0 * act_ref[...].astype(F32))).astype(BF)
        else:
            da_ref[...] = da

    row = pl.BlockSpec((tm, D), lambda i, j: (i, 0))
    vec = pl.BlockSpec((1, D), lambda i, j: (0, 0))
    tile = pl.BlockSpec((tm, tn), lambda i, j: (i, j))
    in_specs = [row, vec, row, pl.BlockSpec((tn, D), lambda i, j: (j, 0))]
    args = [dx, gate, y, w]
    if fused:
        in_specs.append(tile)
        args.append(act)
    return pl.pallas_call(
        body, name=name, grid=(S // tm, K // tn),
        in_specs=in_specs, out_specs=[tile, row, vec],
        out_shape=[jax.ShapeDtypeStruct((S, K), BF if fused else F32), jax.ShapeDtypeStruct((S, D), BF),
                   jax.ShapeDtypeStruct((1, D), F32)],
        scratch_shapes=[pltpu.VMEM((tm, D), BF)],
        compiler_params=_cp("arbitrary", "arbitrary"),
    )(*args)


def _matmul_tn(a, b, *, name, into=None):
    S, Ka = a.shape
    P, _, Db = b.shape
    tk, tn, ts = _pick(Ka, 1024), _pick(Db, 1024), _pick(S, 1024, 16)
    if into is not None:
        slab, kind, half, off = into
        C = tn = slab.shape[3]
        if kind == "row":
            tk = min(tk, Ka // N_CHIPS)
        assert tn == C and P * Db == (N_CHIPS * C if kind == "col" else C) and off % tk == 0
        assert tk == Ka if kind == "col" else (Ka // N_CHIPS) % tk == 0
    npb = Db // tn

    def body(a_ref, b_ref, *rest):
        o_ref, acc = rest[-2:]
        s = pl.program_id(2)

        @pl.when(s == 0)
        def _():
            acc[...] = jnp.zeros_like(acc)

        acc[...] += _dg(a_ref[...], b_ref[...], TN)

        @pl.when(s == pl.num_programs(2) - 1)
        def _():
            o_ref[...] = acc[...]

    in_specs = [pl.BlockSpec((ts, tk), lambda i, j, s: (s, i)),
                pl.BlockSpec((None, ts, tn), lambda i, j, s: (j // npb, s, j % npb))]
    args = [a, b]
    if into is None:
        out_spec = pl.BlockSpec((tk, tn), lambda i, j, s: (i, j))
        out_shape = jax.ShapeDtypeStruct((Ka, P * Db), F32)
        aliases = {}
    else:
        per = (Ka // N_CHIPS) // tk if kind == "row" else 1
        if kind == "col":
            out_spec = pl.BlockSpec((None, None, tk, tn), lambda i, j, s: (j, half, off // tk + i, 0))
        else:
            out_spec = pl.BlockSpec((None, None, tk, tn), lambda i, j, s: (i // per, half, off // tk + i % per, 0))
        out_shape = jax.ShapeDtypeStruct(slab.shape, F32)
        in_specs.append(pl.BlockSpec(memory_space=pl.ANY))
        args.append(slab)
        aliases = {2: 0}
    return pl.pallas_call(
        body, name=name, grid=(Ka // tk, P * npb, S // ts),
        in_specs=in_specs, out_specs=out_spec, out_shape=out_shape,
        scratch_shapes=[pltpu.VMEM((tk, tn), F32)], input_output_aliases=aliases,
        compiler_params=_cp("parallel", "parallel", "arbitrary"),
    )(*args)


def _matmul_nt_lnbwd(g, w, x, nw, sc, dx_out, part=None, *, name):
    P, S, Dg = g.shape
    D = x.shape[1]
    tm = _pick(S, 512, 16)
    fused = part is not None

    def body(g_ref, w_ref, x_ref, nw_ref, sc_ref, dxo_ref, *rest):
        if fused:
            p_ref, dx_ref, dsc_ref, dsh_ref, dnw_ref, recv_ref, send_sems, recv_sems = rest
            copies = _scatter_copies(p_ref, recv_ref, send_sems, recv_sems)
        else:
            dx_ref, dsc_ref, dsh_ref, dnw_ref = rest

        @pl.when(pl.program_id(0) == 0)
        def _():
            dsc_ref[...] = jnp.zeros_like(dsc_ref)
            dsh_ref[...] = jnp.zeros_like(dsh_ref)
            dnw_ref[...] = jnp.zeros_like(dnw_ref)
            if fused:
                for cp in copies:
                    cp.start()

        dh = _dg(g_ref[0], w_ref[:, 0:Dg], NT)
        for p in range(1, P):
            dh = dh + _dg(g_ref[p], w_ref[:, p * Dg:(p + 1) * Dg], NT)
        xv = x_ref[...]
        nwv = nw_ref[...]
        r = lax.rsqrt(jnp.mean(xv * xv, axis=-1, keepdims=True) + EPS)
        xr = xv * r
        dn = dh * (1.0 + sc_ref[...])
        dsc_ref[...] += jnp.sum(dh * (xr * nwv), axis=0, keepdims=True)
        dsh_ref[...] += jnp.sum(dh, axis=0, keepdims=True)
        dnw_ref[...] += jnp.sum(dn * xr, axis=0, keepdims=True)
        u = dn * nwv
        dx_ref[...] = dxo_ref[...] + r * (u - xr * jnp.mean(u * xr, axis=-1, keepdims=True))

        if fused:
            @pl.when(pl.program_id(0) == S // tm - 1)
            def _():
                for cp in copies:
                    cp.wait()

    row = pl.BlockSpec((tm, D), lambda i: (i, 0))
    vec = pl.BlockSpec((1, D), lambda i: (0, 0))
    in_specs = [pl.BlockSpec((P, tm, Dg), lambda i: (0, i, 0)), pl.BlockSpec((D, P * Dg), lambda i: (0, 0)), row, vec, vec, row]
    out_specs = [row, vec, vec, vec]
    out_shape = [jax.ShapeDtypeStruct((S, D), F32)] + [jax.ShapeDtypeStruct((1, D), F32)] * 3
    scratch, args = [], [g, w, x, nw, sc, dx_out]
    if fused:
        in_specs.append(HBM)
        out_specs.append(HBM)
        out_shape.append(jax.ShapeDtypeStruct((3,) + part.shape[1:], part.dtype))
        scratch = [pltpu.SemaphoreType.DMA((3,)), pltpu.SemaphoreType.DMA((3,))]
        args.append(part)
    return pl.pallas_call(
        body, name=name, grid=(S // tm,), in_specs=in_specs, out_specs=out_specs, out_shape=out_shape,
        scratch_shapes=scratch, compiler_params=_cp("arbitrary"),
    )(*args)


def _loss_kernel(x, fw, tgt, *, name):
    S, D = x.shape
    tm = _pick(S, 512, 8)

    def body(x_ref, fw_ref, t_ref, l_ref, dx_ref, dfw_ref):
        @pl.when(pl.program_id(0) == 0)
        def _():
            l_ref[...] = jnp.zeros_like(l_ref)
            dfw_ref[...] = jnp.zeros_like(dfw_ref)

        xv = x_ref[...]
        fwv = fw_ref[...]
        r = lax.rsqrt(jnp.mean(xv * xv, axis=-1, keepdims=True) + EPS)
        xr = xv * r
        err = xr * fwv - t_ref[...]
        per_tok = jnp.mean(err * err, axis=-1, keepdims=True)
        l_ref[...] += 0.5 * jnp.sum(per_tok, axis=0, keepdims=True)
        dy = err * (1.0 / D)
        dfw_ref[...] += jnp.sum(dy * xr, axis=0, keepdims=True)
        u = dy * fwv
        dx_ref[...] = r * (u - xr * jnp.mean(u * xr, axis=-1, keepdims=True))

    row = pl.BlockSpec((tm, D), lambda i: (i, 0))
    vec = pl.BlockSpec((1, D), lambda i: (0, 0))
    return pl.pallas_call(
        body, name=name, grid=(S // tm,),
        in_specs=[row, vec, row],
        out_specs=[pl.BlockSpec((1, LANES), lambda i: (0, 0)), row, vec],
        out_shape=[jax.ShapeDtypeStruct((1, LANES), F32), jax.ShapeDtypeStruct((S, D), F32),
                   jax.ShapeDtypeStruct((1, D), F32)],
        compiler_params=_cp("arbitrary"),
    )(x, fw, tgt)


def _hg_lower_bound(lb3):
    mx = jnp.max(lb3, axis=0, keepdims=True)
    e = jnp.exp(lb3 - mx)
    p = e / jnp.sum(e, axis=0, keepdims=True)
    return p[0:1, :], p


def _hg_chunk_common(qr, fz, lbv):
    sq = _sigmoid(qr)
    q = qr * sq
    sig = _sigmoid(fz)
    f = lbv + (1.0 - lbv) * sig
    k = (1.0 - lbv) * (1.0 - sig)
    return q, sq, sig, f, k, jnp.log(f)


def _row_of(x, rows, r):
    return jnp.sum(jnp.where(rows == r, x, 0.0), axis=0, keepdims=True)


def _hg_fwd(proj, hg_lb, gn, slab=None, *, name):
    S = proj.shape[0]
    D = proj.shape[1] // 4
    H = D // LANES
    HB = min(HG_HEADS_PER_STEP, H)
    W = HB * LANES
    C = HG_CHUNK
    T = _pick(S, HG_TOKENS_PER_STEP, C)
    nch, nb = T // C, S // T
    ng = H // HB
    fused = slab is not None

    def body(q_ref, fz_ref, v_ref, g_ref, lb_ref, gn_ref, *rest):
        if fused:
            s_ref, y_ref, o_ref, sts_ref, out_ref, st, send_sems, recv_sems = rest
            finish = _gather_behind(s_ref, out_ref, send_sems, recv_sems,
                                    pl.program_id(0) * nb + pl.program_id(1), ng * nb)
        else:
            y_ref, o_ref, sts_ref, st = rest

        @pl.when(pl.program_id(1) == 0)
        def _():
            st[...] = jnp.zeros_like(st)

        lb_all, _ = _hg_lower_bound(lb_ref[...])
        gnv = gn_ref[...]
        ri = lax.broadcasted_iota(jnp.int32, (C, C), 0)
        ci_ = lax.broadcasted_iota(jnp.int32, (C, C), 1)
        low = ri >= ci_
        tri = jnp.where(low, 1.0, 0.0).astype(BF)
        rows_w = lax.broadcasted_iota(jnp.int32, (C, W), 0)

        def chunk(ci, carry):
            sl = pl.ds(pl.multiple_of(ci * C, C), C)
            heads = [slice(hh * LANES, (hh + 1) * LANES) for hh in range(HB)]
            q, _, _, _, k, logf = _hg_chunk_common(q_ref[sl, :], fz_ref[sl, :], lb_all)
            vv, gg = v_ref[sl, :], g_ref[sl, :]
            G = _tri_dot(tri, logf)
            Gm = _row_of(G, rows_w, C // 2 - 1)
            Gl = _row_of(G, rows_w, C - 1)
            qt, kt = q * jnp.exp(G - Gm), k * jnp.exp(Gm - G)
            qe, kd, eGl = q * jnp.exp(G), k * jnp.exp(Gl - G), jnp.exp(Gl)
            A = [jnp.where(low, _dg1(qt[:, ls], kt[:, ls], NT), 0.0) for ls in heads]
            Sv = [st[hh] for hh in range(HB)]
            for hh in range(HB):
                sts_ref[hh, ci] = Sv[hh]
            o = [_dg1(A[hh], vv[:, ls], NN) + _dg1(qe[:, ls], Sv[hh], NT) for hh, ls in enumerate(heads)]
            for hh, ls in enumerate(heads):
                st[hh] = Sv[hh] * eGl[:, ls] + _dg1(vv[:, ls], kd[:, ls], TN)
            gate = gg * _sigmoid(gg)
            for hh, ls in enumerate(heads):
                r = lax.rsqrt(jnp.mean(o[hh] * o[hh], axis=-1, keepdims=True) + EPS)
                y_ref[sl, ls] = ((o[hh] * r * gnv) * gate[:, ls]).astype(BF)
                o_ref[sl, ls] = o[hh]
            return carry

        lax.fori_loop(0, nch, chunk, 0)

        if fused:
            finish()

    def part(p):
        return pl.BlockSpec((T, W), lambda h, n: (n, p * ng + h))

    blk = pl.BlockSpec((T, W), lambda h, n: (n, h))
    in_specs = [part(0), part(1), part(2), part(3),
                pl.BlockSpec((3, W), lambda h, n: (0, h)), pl.BlockSpec((1, LANES), lambda h, n: (0, 0))]
    out_specs = [blk, blk, pl.BlockSpec((HB, nch, LANES, LANES), lambda h, n: (h, n, 0, 0))]
    out_shape = [jax.ShapeDtypeStruct((S, D), BF), jax.ShapeDtypeStruct((S, D), F32),
                 jax.ShapeDtypeStruct((H, S // C, LANES, LANES), F32)]
    scratch = [pltpu.VMEM((HB, LANES, LANES), F32)]
    args = [proj, proj, proj, proj, hg_lb, gn]
    if fused:
        in_specs.append(HBM)
        out_specs.append(HBM)
        out_shape.append(jax.ShapeDtypeStruct((N_CHIPS,) + slab.shape, slab.dtype))
        scratch += [pltpu.SemaphoreType.DMA((6,)), pltpu.SemaphoreType.DMA((6,))]
        args.append(slab)
    return pl.pallas_call(
        body, name=name, grid=(ng, nb), in_specs=in_specs, out_specs=out_specs, out_shape=out_shape,
        scratch_shapes=scratch, compiler_params=_cp("arbitrary", "arbitrary"),
    )(*args)


def _hg_bwd(proj, hg_lb, gn, o_all, states, dy, part=None, *, name):
    S = proj.shape[0]
    D = proj.shape[1] // 4
    H = D // LANES
    HB = min(HG_HEADS_PER_STEP, H)
    W = HB * LANES
    C = HG_CHUNK
    T = _pick(S, HG_TOKENS_PER_STEP, C)
    nch, nb = T // C, S // T
    ng = H // HB
    fused = part is not None

    def body(q_ref, fz_ref, v_ref, g_ref, lb_ref, gn_ref, o_ref, sts_ref, dy_ref, *rest):
        if fused:
            p_ref, dp_ref, dlb_ref, dgn_ref, recv_ref, dst, dlb_acc, send_sems, recv_sems = rest
            copies = _scatter_copies(p_ref, recv_ref, send_sems, recv_sems)

            @pl.when((pl.program_id(0) == 0) & (pl.program_id(1) == 0))
            def _():
                for cp in copies:
                    cp.start()
        else:
            dp_ref, dlb_ref, dgn_ref, dst, dlb_acc = rest
        n = pl.program_id(1)

        @pl.when(n == 0)
        def _():
            dst[...] = jnp.zeros_like(dst)
            dlb_acc[...] = jnp.zeros_like(dlb_acc)
            dgn_ref[...] = jnp.zeros_like(dgn_ref)

        lb_all, p3 = _hg_lower_bound(lb_ref[...])
        gnv = gn_ref[...]
        ri = lax.broadcasted_iota(jnp.int32, (C, C), 0)
        ci_ = lax.broadcasted_iota(jnp.int32, (C, C), 1)
        low = ri >= ci_
        tri = jnp.where(low, 1.0, 0.0).astype(BF)
        triu = jnp.where(ri <= ci_, 1.0, 0.0).astype(BF)
        rows_w = lax.broadcasted_iota(jnp.int32, (C, W), 0)
        gnw = jnp.tile(gnv, (1, HB))

        def chunk(cj, carry):
            ci = nch - 1 - cj
            sl = pl.ds(pl.multiple_of(ci * C, C), C)
            heads = list(enumerate(slice(hh * LANES, (hh + 1) * LANES) for hh in range(HB)))
            wide = lambda parts: jnp.concatenate(parts, axis=1)
            qr, vv, gg = q_ref[sl, :], v_ref[sl, :], g_ref[sl, :]
            q, sq, sig, f, k, logf = _hg_chunk_common(qr, fz_ref[sl, :], lb_all)
            G = _tri_dot(tri, logf)
            Gm = _row_of(G, rows_w, C // 2 - 1)
            Gl = _row_of(G, rows_w, C - 1)
            eG, e_qm, e_km, e_lk, eGl = jnp.exp(G), jnp.exp(G - Gm), jnp.exp(Gm - G), jnp.exp(Gl - G), jnp.exp(Gl)
            qt, kt, kdec, qe = q * e_qm, k * e_km, k * e_lk, q * eG
            sg = _sigmoid(gg)
            d_onw = dy_ref[sl, :] * (gg * sg)
            u = d_onw * gnw
            o = o_ref[sl, :]
            on, do = [], []
            for hh, ls in heads:
                r = lax.rsqrt(jnp.mean(o[:, ls] * o[:, ls], axis=-1, keepdims=True) + EPS)
                on.append(o[:, ls] * r)
                dgn_ref[hh] += jnp.sum(d_onw[:, ls] * on[hh], axis=0, keepdims=True)
                do.append(r * (u[:, ls] - on[hh] * jnp.mean(u[:, ls] * on[hh], axis=-1, keepdims=True)))
            dgg = dy_ref[sl, :] * (wide(on) * gnw) * (sg * (1.0 + gg * (1.0 - sg)))
            Sv = [sts_ref[hh, ci] for hh, _ in heads]
            dSv = [dst[hh] for hh, _ in heads]
            A = [jnp.where(low, _dg1(qt[:, ls], kt[:, ls], NT), 0.0) for _, ls in heads]
            dA = [jnp.where(low, _dg3(do[hh], vv[:, ls], NT), 0.0) for hh, ls in heads]
            dv = wide([_dg1(A[hh], do[hh], TN) + _dg1(kdec[:, ls], dSv[hh], NT) for hh, ls in heads])
            dq = wide([_dg3(dA[hh], kt[:, ls], NN) for hh, ls in heads]) * e_qm \
                + eG * wide([_dg3(do[hh], Sv[hh], NN) for hh, _ in heads])
            dk = wide([_dg3(dA[hh], qt[:, ls], TN) for hh, ls in heads]) * e_km \
                + e_lk * wide([_dg3(vv[:, ls], dSv[hh], NN) for hh, ls in heads])
            s_end = [Sv[hh] * eGl[:, ls] + _dg3(vv[:, ls], kdec[:, ls], TN) for hh, ls in heads]
            dgl = wide([jnp.sum(dSv[hh] * s_end[hh], axis=0, keepdims=True) for hh, _ in heads])
            for hh, ls in heads:
                dst[hh] = dSv[hh] * eGl[:, ls] + _dg1(do[hh], qe[:, ls], TN)
            dG = q * dq - k * dk + jnp.where(rows_w == C - 1, dgl, 0.0)
            dlogf = _tri_dot(triu, dG) - f * dk
            dlf_f = dlogf / f
            dlb_acc[...] += jnp.sum(dlf_f * (1.0 - sig), axis=0, keepdims=True)
            dp_ref[0, sl, :] = (dq * (sq * (1.0 + qr * (1.0 - sq)))).astype(BF)
            dp_ref[1, sl, :] = (dlf_f * (1.0 - lb_all) * sig * (1.0 - sig)).astype(BF)
            dp_ref[2, sl, :] = dv.astype(BF)
            dp_ref[3, sl, :] = dgg.astype(BF)
            return carry

        lax.fori_loop(0, nch, chunk, 0)
        sel = jnp.where(lax.broadcasted_iota(jnp.int32, (3, W), 0) == 0, 1.0, 0.0)
        dlb_ref[...] = lb_all * (sel - p3) * dlb_acc[...]

        if fused:
            @pl.when((pl.program_id(0) == ng - 1) & (n == nb - 1))
            def _():
                for cp in copies:
                    cp.wait()

    def col(p):
        return pl.BlockSpec((T, W), lambda h, n: (nb - 1 - n, p * ng + h))

    blk = pl.BlockSpec((T, W), lambda h, n: (nb - 1 - n, h))
    in_specs = [col(0), col(1), col(2), col(3),
                pl.BlockSpec((3, W), lambda h, n: (0, h)), pl.BlockSpec((1, LANES), lambda h, n: (0, 0)),
                blk, pl.BlockSpec((HB, nch, LANES, LANES), lambda h, n: (h, nb - 1 - n, 0, 0)), blk]
    out_specs = [pl.BlockSpec((4, T, W), lambda h, n: (0, nb - 1 - n, h)),
                 pl.BlockSpec((3, W), lambda h, n: (0, h)),
                 pl.BlockSpec((HB, 1, LANES), lambda h, n: (h, 0, 0))]
    out_shape = [jax.ShapeDtypeStruct((4, S, D), BF), jax.ShapeDtypeStruct((3, D), F32),
                 jax.ShapeDtypeStruct((H, 1, LANES), F32)]
    scratch = [pltpu.VMEM((HB, LANES, LANES), F32), pltpu.VMEM((1, W), F32)]
    args = [proj, proj, proj, proj, hg_lb, gn, o_all, states, dy]
    if fused:
        in_specs.append(HBM)
        out_specs.append(HBM)
        out_shape.append(jax.ShapeDtypeStruct((3,) + part.shape[1:], part.dtype))
        scratch += [pltpu.SemaphoreType.DMA((3,)), pltpu.SemaphoreType.DMA((3,))]
        args.append(part)
    return pl.pallas_call(
        body, name=name, grid=(ng, nb), in_specs=in_specs, out_specs=out_specs, out_shape=out_shape,
        scratch_shapes=scratch, compiler_params=_cp("arbitrary", "arbitrary"),
    )(*args)


def _log_sigmoid(u):
    return jnp.minimum(u, 0.0) - jnp.log(1.0 + jnp.exp(-jnp.abs(u)))


def _lane_put(base, lane, first, pieces):
    for n, p in enumerate(pieces):
        base = jnp.where(lane == first + n, p, base)
    return base


def _fox_cumsum(proj, bf_pad, *, name):
    S = proj.shape[0]
    D = proj.shape[1] // 5
    T = _pick(S, 256, 8)

    def body(fz_ref, b_ref, f_ref, carry):
        @pl.when(pl.program_id(0) == 0)
        def _():
            carry[...] = jnp.zeros_like(carry)

        logf = _log_sigmoid(fz_ref[...] + b_ref[...])
        tri = jnp.where(lax.broadcasted_iota(jnp.int32, (T, T), 0) >= lax.broadcasted_iota(jnp.int32, (T, T), 1),
                        1.0, 0.0).astype(BF)
        fv = _tri_dot(tri, logf) + carry[...]
        f_ref[...] = fv
        carry[...] = _row_of(fv, lax.broadcasted_iota(jnp.int32, (T, LANES), 0), T - 1)

    return pl.pallas_call(
        body, name=name, grid=(S // T,),
        in_specs=[pl.BlockSpec((T, LANES), lambda i: (i, 4 * D // LANES)), pl.BlockSpec((1, LANES), lambda i: (0, 0))],
        out_specs=pl.BlockSpec((T, LANES), lambda i: (i, 0)),
        out_shape=jax.ShapeDtypeStruct((S, LANES), F32),
        scratch_shapes=[pltpu.VMEM((1, LANES), F32)],
        compiler_params=_cp("arbitrary"),
    )(proj, bf_pad)


def _pair_stats(sq, lo):
    del lo
    a = lax.broadcasted_iota(jnp.int32, (LANES, LANES), 0) < FOX_DH
    b = lax.broadcasted_iota(jnp.int32, (LANES, LANES), 1) < FOX_DH
    avg = jnp.where(a == b, 1.0 / FOX_DH, 0.0).astype(BF)
    hi, mid, low = _split3(sq)
    return _dot(hi, avg) + _dot(mid, avg) + _dot(low, avg)


def _fox_prep(proj, fcum, qw2, kw2, *, name):
    S = proj.shape[0]
    D = proj.shape[1] // 5
    HP = D // LANES
    T = _pick(S, 512, 16)

    def body(q_ref, k_ref, v_ref, f_ref, qw_ref, kw_ref, qa_ref, ka_ref, va_ref, vt_ref):
        hp = pl.program_id(1)
        lane = lax.broadcasted_iota(jnp.int32, (T, LANES), 1)
        lo = lane < FOX_DH
        qv, kv, vv, fv = q_ref[...], k_ref[...], v_ref[...], f_ref[...]
        qn = qv * lax.rsqrt(_pair_stats(qv * qv, lo) + EPS) * qw_ref[...] * (0.125 * LOG2E)
        kn = kv * lax.rsqrt(_pair_stats(kv * kv, lo) + EPS) * kw_ref[...]
        ones_q = jnp.where((lane >= 67) & (lane <= 69), 1.0, 0.0)
        ones_k = jnp.where(((lane >= 64) & (lane <= 66)) | ((lane >= 70) & (lane <= 72)), 1.0, 0.0)
        ones_v = jnp.where((lane >= 64) & (lane <= 66), 1.0, 0.0)
        for hh in range(2):
            fh = jnp.sum(jnp.where(lane == 2 * hp + hh, fv, 0.0), axis=-1, keepdims=True) * LOG2E
            pieces = [p.astype(F32) for p in _split3(fh)]

            def half(x):
                return jnp.where(lo, x if hh == 0 else pltpu.roll(x, FOX_DH, 1), 0.0)

            qa_ref[hh] = _lane_put(half(qn) + ones_q, lane, 64, pieces).astype(BF)
            ka_ref[hh] = _lane_put(half(kn) + ones_k, lane, 67, [-p for p in pieces]).astype(BF)
            va = half(vv) + ones_v
            va_ref[hh] = va.astype(BF)
            vt_ref[hh] = va.T.astype(BF)

    def part(p):
        return pl.BlockSpec((T, LANES), lambda i, hp: (i, p * HP + hp))

    vec = pl.BlockSpec((1, LANES), lambda i, hp: (0, 0))
    aug = pl.BlockSpec((2, T, LANES), lambda i, hp: (hp, i, 0))
    return pl.pallas_call(
        body, name=name, grid=(S // T, HP),
        in_specs=[part(0), part(1), part(2), pl.BlockSpec((T, LANES), lambda i, hp: (i, 0)), vec, vec],
        out_specs=[aug, aug, aug, pl.BlockSpec((2, LANES, T), lambda i, hp: (hp, 0, i))],
        out_shape=[jax.ShapeDtypeStruct((2 * HP, S, LANES), BF)] * 3 + [jax.ShapeDtypeStruct((2 * HP, LANES, S), BF)],
        compiler_params=_cp("parallel", "arbitrary"),
    )(proj, proj, proj, fcum, qw2, kw2)


def _fox_block(S):
    return _pick(S, 256, 16)


def _fox_skip_bounds(fcum, qn_w, kn_w, nheads):
    S = fcum.shape[0]
    B = _fox_block(S)
    qk = 8.0 * LOG2E * 1.02 * jnp.max(jnp.abs(qn_w)) * jnp.max(jnp.abs(kn_w))
    thresh = -(2.0 * qk + 160.0)
    f2 = fcum[:, :nheads] * LOG2E
    first, last = f2[0::B], f2[B - 1::B]
    nb = S // B
    blk = jnp.arange(nb)
    dead = (first[0::2, None, :] - last[None, :, :]) < thresh
    jmin = jnp.sum(dead & (blk[None, :, None] < 2 * jnp.arange(nb // 2)[:, None, None]), axis=1)
    live = (first[:, None, :] - last[None, :, :]) >= thresh
    imax = blk[:, None] + jnp.sum(live & (blk[:, None, None] > blk[None, :, None]), axis=0)
    return jmin.T.astype(jnp.int32), imax.T.astype(jnp.int32)


def _fox_fwd(jmin, qa, ka, vat, proj, *, name):
    H, S, _ = qa.shape
    HP = H // 2
    D = HP * LANES
    B = _fox_block(S)
    BQ = 2 * B
    nq = S // BQ

    def body(jmin_ref, q_ref, k_ref, vt_ref, g_ref, y_ref, o_ref, q2_ref):
        hp, i = pl.program_id(0), pl.program_id(1)
        lane = lax.broadcasted_iota(jnp.int32, (BQ, LANES), 1)
        lo = lane < FOX_DH
        in_stat = (lane >= 70) & (lane <= 75)
        causal = lax.broadcasted_iota(jnp.int32, (BQ, BQ), 0) <= lax.broadcasted_iota(jnp.int32, (BQ, BQ), 1)
        row = lax.broadcasted_iota(jnp.int32, (LANES, BQ), 0)
        m0, acc0 = jnp.full((1, BQ), -jnp.inf, F32), jnp.zeros((LANES, BQ), F32)
        outs = []
        for hh in range(2):
            qb = q_ref[hh]

            def block(j, carry, masked=False):
                m, acc = carry
                sl = pl.ds(pl.multiple_of(j * BQ, BQ), BQ)
                st = _dg(k_ref[hh, sl, :], qb, NT)
                if masked:
                    st = jnp.where(causal, st, -jnp.inf)
                m_new = jnp.maximum(m, jnp.ceil(jnp.max(st, axis=0, keepdims=True)))
                p = jnp.exp2(st - m_new).astype(BF)
                return m_new, acc * jnp.exp2(m - m_new) + _dot(vt_ref[hh, :, sl], p)

            carry = lax.fori_loop(jmin_ref[2 * hp + hh, i] // 2, i, block, (m0, acc0))
            m, acc = block(i, carry, masked=True)
            linv = 1.0 / jnp.sum(jnp.where(row == FOX_DH, acc, 0.0), axis=0, keepdims=True)
            tile = acc * linv
            for n, piece in enumerate(_split3(m) + _split3(linv)):
                tile = jnp.where(row == 70 + n, piece.astype(F32), tile)
            tile = tile.T
            outs.append(tile)
            q2_ref[hh] = jnp.where(in_stat, jnp.where(lane <= 72, -tile, tile), qb.astype(F32)).astype(BF)
        o = jnp.where(lo, outs[0], pltpu.roll(outs[1], FOX_DH, 1))
        o_ref[...] = o
        y_ref[...] = (o * _sigmoid(g_ref[...])).astype(BF)

    blk = pl.BlockSpec((BQ, LANES), lambda hp, i, jm: (i, hp))
    qblk = pl.BlockSpec((2, BQ, LANES), lambda hp, i, jm: (hp, i, 0))
    full = pl.BlockSpec((2, S, LANES), lambda hp, i, jm: (hp, 0, 0))
    full_t = pl.BlockSpec((2, LANES, S), lambda hp, i, jm: (hp, 0, 0))
    return pl.pallas_call(
        body, name=name,
        grid_spec=pltpu.PrefetchScalarGridSpec(
            num_scalar_prefetch=1, grid=(HP, nq),
            in_specs=[qblk, full, full_t, pl.BlockSpec((BQ, LANES), lambda hp, i, jm: (i, 3 * HP + hp))],
            out_specs=[blk, blk, qblk]),
        out_shape=[jax.ShapeDtypeStruct((S, D), BF), jax.ShapeDtypeStruct((S, D), F32),
                   jax.ShapeDtypeStruct((H, S, LANES), BF)],
        compiler_params=_cp("parallel", "arbitrary"),
    )(jmin, qa, ka, vat, proj)


def _fox_bwd_prep(dy, o, proj, q2, *, name):
    S, D = dy.shape
    HP = D // LANES
    T = _pick(S, 512, 16)

    def body(dy_ref, o_ref, g_ref, q2_ref, da_ref):
        lane = lax.broadcasted_iota(jnp.int32, (T, LANES), 1)
        lo = lane < FOX_DH
        in_linv = (lane >= 73) & (lane <= 75)
        linv = [jnp.sum(jnp.where(in_linv, q2_ref[hh].astype(F32), 0.0), axis=-1, keepdims=True) for hh in range(2)]
        u = (dy_ref[...] * _sigmoid(g_ref[...]) * jnp.where(lo, linv[0], linv[1])).astype(BF).astype(F32)
        prod = u * o_ref[...]
        d_lo = jnp.sum(jnp.where(lo, prod, 0.0), axis=-1, keepdims=True)
        d_hi = jnp.sum(jnp.where(lo, 0.0, prod), axis=-1, keepdims=True)
        for hh, delta in enumerate((d_lo, d_hi)):
            base = jnp.where(lo, u if hh == 0 else pltpu.roll(u, FOX_DH, 1), 0.0)
            da_ref[hh] = _lane_put(base, lane, 64, [-(p.astype(F32)) for p in _split3(delta)]).astype(BF)

    blk = pl.BlockSpec((T, LANES), lambda i, hp: (i, hp))
    aug = pl.BlockSpec((2, T, LANES), lambda i, hp: (hp, i, 0))
    return pl.pallas_call(
        body, name=name, grid=(S // T, HP),
        in_specs=[blk, blk, pl.BlockSpec((T, LANES), lambda i, hp: (i, 3 * HP + hp)), aug],
        out_specs=aug,
        out_shape=jax.ShapeDtypeStruct((2 * HP, S, LANES), BF),
        compiler_params=_cp("parallel", "arbitrary"),
    )(dy, o, proj, q2)


def _fox_bwd(imax, q2, ka, va, doa, *, name):
    H, S, _ = q2.shape
    B = _fox_block(S)
    nb = S // B

    def body(imax_ref, q_ref, do_ref, k_ref, v_ref, dq_ref, dk_ref, dv_ref, cs_ref):
        j = pl.program_id(1)
        end = imax_ref[pl.program_id(0), j] + 1

        @pl.when(j == 0)
        def _():
            dq_ref[...] = jnp.zeros_like(dq_ref)

        kb, vb = k_ref[...], v_ref[...]

        def step(i, carry, nblk=1):
            dk_acc, dv_acc, cs_acc = carry
            rows = nblk * B
            sl = pl.ds(pl.multiple_of(i * B, B), rows)
            qb, dob = q_ref[sl, :], do_ref[sl, :]
            s = _dg(qb, kb, NT)
            ahead = lax.broadcasted_iota(jnp.int32, (rows, B), 0) - lax.broadcasted_iota(jnp.int32, (rows, B), 1)
            pb = jnp.exp2(jnp.where(ahead >= (j - i) * B, s, -jnp.inf)).astype(BF)
            ds = pb.astype(F32) * _dg(dob, vb, NT)
            dsb = ds.astype(BF)
            cs_acc = cs_acc + jnp.sum(ds.reshape(rows // 8, 8, B), axis=0)
            dv_acc = dv_acc + _dg(pb, dob, TN)
            dk_acc = dk_acc + _dg(dsb, qb, TN)
            dq_ref[sl, :] += _dot(dsb, kb)
            return dk_acc, dv_acc, cs_acc

        zero = jnp.zeros((B, LANES), F32)
        carry = (zero, zero, jnp.zeros((8, B), F32))
        pos = j
        for U in FOX_BWD_TILES:
            n = (end - pos) // U
            carry = lax.fori_loop(0, n, lambda ii, c, pos=pos, U=U: step(pos + U * ii, c, nblk=U), carry)
            pos = pos + U * n
        dk_acc, dv_acc, cs_acc = carry
        dk_ref[...] = dk_acc
        dv_ref[...] = dv_acc
        cs_ref[...] = jnp.sum(cs_acc, axis=0, keepdims=True)

    full = pl.BlockSpec((None, S, LANES), lambda h, j, im: (h, 0, 0))
    blk = pl.BlockSpec((None, B, LANES), lambda h, j, im: (h, j, 0))
    return pl.pallas_call(
        body, name=name,
        grid_spec=pltpu.PrefetchScalarGridSpec(
            num_scalar_prefetch=1, grid=(H, nb),
            in_specs=[full, full, blk, blk],
            out_specs=[full, blk, blk, pl.BlockSpec((None, 1, B), lambda h, j, im: (h, 0, j))]),
        out_shape=[jax.ShapeDtypeStruct((H, S, LANES), F32)] * 3 + [jax.ShapeDtypeStruct((H, 1, S), F32)],
        compiler_params=_cp("parallel", "arbitrary"),
    )(imax, q2, doa, ka, va)


def _fox_bwd_post(dqa, dka, dva, proj, dy, o, qw2, kw2, *, name):
    S, D = dy.shape
    HP = D // LANES
    T = _pick(S, 512, 16)

    def body(dq_ref, dk_ref, dv_ref, q_ref, k_ref, g_ref, dy_ref, o_ref, qw_ref, kw_ref, dp_ref, dqw_ref, dkw_ref):
        @pl.when((pl.program_id(0) == 0) & (pl.program_id(1) == 0))
        def _():
            dqw_ref[...] = jnp.zeros_like(dqw_ref)
            dkw_ref[...] = jnp.zeros_like(dkw_ref)

        lane = lax.broadcasted_iota(jnp.int32, (T, LANES), 1)
        lo = lane < FOX_DH

        def pair(ref):
            return jnp.where(lo, ref[0], pltpu.roll(ref[1], FOX_DH, 1))

        def norm_bwd(xv, w, dyn, dw_ref):
            r = lax.rsqrt(_pair_stats(xv * xv, lo) + EPS)
            xr = xv * r
            dw_ref[...] += jnp.sum(dyn * xr, axis=0, keepdims=True)
            u = dyn * w
            return r * (u - xr * _pair_stats(u * xr, lo))

        dp_ref[0] = norm_bwd(q_ref[...], qw_ref[...], pair(dq_ref) * 0.125, dqw_ref).astype(BF)
        dp_ref[1] = norm_bwd(k_ref[...], kw_ref[...], pair(dk_ref) * (1.0 / LOG2E), dkw_ref).astype(BF)
        dp_ref[2] = pair(dv_ref).astype(BF)
        sg = _sigmoid(g_ref[...])
        dp_ref[3] = (dy_ref[...] * o_ref[...] * sg * (1.0 - sg)).astype(BF)

    def part(p):
        return pl.BlockSpec((T, LANES), lambda i, hp: (i, p * HP + hp))

    aug = pl.BlockSpec((2, T, LANES), lambda i, hp: (hp, i, 0))
    blk = pl.BlockSpec((T, LANES), lambda i, hp: (i, hp))
    vec = pl.BlockSpec((1, LANES), lambda i, hp: (0, 0))
    return pl.pallas_call(
        body, name=name, grid=(S // T, HP),
        in_specs=[aug, aug, aug, part(0), part(1), part(3), blk, blk, vec, vec],
        out_specs=[pl.BlockSpec((4, T, LANES), lambda i, hp: (0, i, hp)), vec, vec],
        out_shape=[jax.ShapeDtypeStruct((5, S, D), BF), jax.ShapeDtypeStruct((1, LANES), F32),
                   jax.ShapeDtypeStruct((1, LANES), F32)],
        compiler_params=_cp("arbitrary", "arbitrary"),
    )(dqa, dka, dva, proj, proj, proj, dy, o, qw2, kw2)


def _fox_dfz(colsum, nheads, proj, bf_pad, dproj, *, name):
    S = colsum.shape[0]
    H = nheads
    D = dproj.shape[2]
    T = _pick(S, 256, 16)
    nb = S // T

    def body(cs_ref, fz_ref, b_ref, _, dp_ref, db_ref, carry):
        @pl.when(pl.program_id(0) == 0)
        def _():
            carry[...] = jnp.zeros_like(carry)
            db_ref[...] = jnp.zeros_like(db_ref)

        lane = lax.broadcasted_iota(jnp.int32, (T, LANES), 1)
        df = -cs_ref[...]
        triu = jnp.where(lax.broadcasted_iota(jnp.int32, (T, T), 0) <= lax.broadcasted_iota(jnp.int32, (T, T), 1),
                         1.0, 0.0).astype(BF)
        dlogf = _tri_dot(triu, df) + carry[...]
        carry[...] = _row_of(dlogf, lax.broadcasted_iota(jnp.int32, (T, LANES), 0), 0)
        dfz = jnp.where(lane < H, dlogf * _sigmoid(-(fz_ref[...] + b_ref[...])), 0.0)
        db_ref[...] += jnp.sum(dfz, axis=0, keepdims=True)
        dp_ref[...] = jnp.zeros_like(dp_ref)
        dp_ref[:, 0:LANES] = dfz.astype(BF)

    return pl.pallas_call(
        body, name=name, grid=(nb,),
        in_specs=[pl.BlockSpec((T, LANES), lambda i: (nb - 1 - i, 0)),
                  pl.BlockSpec((T, LANES), lambda i: (nb - 1 - i, 4 * D // LANES)),
                  pl.BlockSpec((1, LANES), lambda i: (0, 0)),
                  pl.BlockSpec(memory_space=pl.ANY)],
        out_specs=[pl.BlockSpec((None, T, D), lambda i: (4, nb - 1 - i, 0)), pl.BlockSpec((1, LANES), lambda i: (0, 0))],
        out_shape=[jax.ShapeDtypeStruct(dproj.shape, BF), jax.ShapeDtypeStruct((1, LANES), F32)],
        scratch_shapes=[pltpu.VMEM((1, LANES), F32)],
        input_output_aliases={3: 0},
        compiler_params=_cp("arbitrary"),
    )(colsum, proj, bf_pad, dproj)


def _mod_fwd(c16, w, b, *, name):
    L, D, N = w.shape
    tn = _pick(N, 512)

    def body(c_ref, w_ref, b_ref, o_ref):
        cv = c_ref[...]
        ca = (cv * _sigmoid(cv)).astype(BF)
        o_ref[...] = _dot(ca, w_ref[...].astype(BF)) + b_ref[...]

    return pl.pallas_call(
        body, name=name, grid=(L, N // tn),
        in_specs=[pl.BlockSpec((16, D), lambda l, j: (0, 0)), pl.BlockSpec((None, D, tn), lambda l, j: (l, 0, j)),
                  pl.BlockSpec((None, 1, tn), lambda l, j: (l, 0, j))],
        out_specs=pl.BlockSpec((None, 16, tn), lambda l, j: (l, 0, j)),
        out_shape=jax.ShapeDtypeStruct((L, 16, N), F32),
        compiler_params=_cp("parallel", "arbitrary"),
    )(c16, w, b)


def _mod_bwd(c16, dmod, *, name):
    L, _, N = dmod.shape
    D = c16.shape[1]
    tn = _pick(N, 512)

    def body(c_ref, d_ref, o_ref):
        cv = c_ref[...]
        ca = (cv * _sigmoid(cv)).astype(BF)
        o_ref[...] = _dg(ca, d_ref[...].astype(BF), TN)

    return pl.pallas_call(
        body, name=name, grid=(L, N // tn),
        in_specs=[pl.BlockSpec((16, D), lambda l, j: (0, 0)), pl.BlockSpec((None, 16, tn), lambda l, j: (l, 0, j))],
        out_specs=pl.BlockSpec((None, D, tn), lambda l, j: (l, 0, j)),
        out_shape=jax.ShapeDtypeStruct((L, D, N), F32),
        compiler_params=_cp("parallel", "arbitrary"),
    )(c16, dmod)


def _adamw_math(w, g, m, v):
    m = ADAM_B1 * m + (1.0 - ADAM_B1) * g
    v = ADAM_B2 * v + (1.0 - ADAM_B2) * (g * g)
    m_hat = m / (1.0 - ADAM_B1 ** ADAM_STEP)
    v_hat = v / (1.0 - ADAM_B2 ** ADAM_STEP)
    return -ADAM_LR * (m_hat / (jnp.sqrt(v_hat) + ADAM_EPS) + ADAM_WD * w), m, v


def _adamw(w, g, m, v, *, g_at=None, name):
    R, C = w.shape
    row0 = 0 if g_at is None else g_at[1]
    tr = min(math.gcd(row0, 256) if row0 else 256, -(-R // 8) * 8)
    g0 = row0 // tr
    if g_at is None:
        g_spec = pl.BlockSpec((tr, C), lambda i: (i, 0))
    else:
        g_spec = pl.BlockSpec((None, tr, C), lambda i: (g_at[0], g0 + i, 0))

    def body(w_ref, g_ref, m_ref, v_ref, d_ref, mo_ref, vo_ref):
        d, mn, vn = _adamw_math(w_ref[...], g_ref[...], m_ref[...], v_ref[...])
        d_ref[...] = d
        mo_ref[...] = mn
        vo_ref[...] = vn

    blk = pl.BlockSpec((tr, C), lambda i: (i, 0))
    return pl.pallas_call(
        body, name=name, grid=(pl.cdiv(R, tr),),
        in_specs=[blk, g_spec, blk, blk],
        out_specs=[blk, blk, blk],
        out_shape=[jax.ShapeDtypeStruct((R, C), F32)] * 3,
        compiler_params=_cp("parallel"),
    )(w, g, m, v)


def _sum_parts(parts, *, name):
    P, R, C = parts.shape

    def body(p_ref, o_ref):
        acc = p_ref[0]
        for p in range(1, P):
            acc = acc + p_ref[p]
        o_ref[...] = acc

    return pl.pallas_call(
        body, name=name, grid=(1,),
        in_specs=[pl.BlockSpec((P, R, C), lambda i: (0, 0, 0))],
        out_specs=pl.BlockSpec((R, C), lambda i: (0, 0)),
        out_shape=jax.ShapeDtypeStruct((R, C), F32),
        compiler_params=_cp("arbitrary"),
    )(parts)


def _add_halves(g4, recv, c_idx, *, name):
    _, _, Rh, C = g4.shape
    tr = min(256, Rh)

    def body(c_ref, a_ref, b_ref, o_ref):
        o_ref[...] = (a_ref[...] + b_ref[...].astype(F32)).astype(BF)

    return pl.pallas_call(
        body, name=name,
        grid_spec=pltpu.PrefetchScalarGridSpec(
            num_scalar_prefetch=1, grid=(4, pl.cdiv(Rh, tr)),
            in_specs=[pl.BlockSpec((None, None, tr, C), lambda j, r, c: (j, c[0], r, 0)),
                      pl.BlockSpec((None, tr, C), lambda j, r, c: (j, r, 0))],
            out_specs=pl.BlockSpec((None, tr, C), lambda j, r, c: (j, r, 0))),
        out_shape=jax.ShapeDtypeStruct((4, Rh, C), BF),
        compiler_params=_cp("parallel", "arbitrary"),
    )(c_idx, g4, recv)


def _add_four(g4, from_sibling, from_chips, pos, *, name):
    _, _, Rh, C = g4.shape
    tr = min(256, Rh)

    def body(p_ref, a_ref, s_ref, b_ref, o_ref):
        own = a_ref[...] + s_ref[...].astype(F32)
        o_ref[...] = ((own + b_ref[0].astype(F32)) + b_ref[1].astype(F32)) + b_ref[2].astype(F32)

    return pl.pallas_call(
        body, name=name,
        grid_spec=pltpu.PrefetchScalarGridSpec(
            num_scalar_prefetch=1, grid=(pl.cdiv(Rh, tr),),
            in_specs=[pl.BlockSpec((None, None, tr, C), lambda r, p: (p[0], p[1], r, 0)),
                      pl.BlockSpec((None, tr, C), lambda r, p: (p[0], r, 0)),
                      pl.BlockSpec((3, tr, C), lambda r, p: (0, r, 0))],
            out_specs=pl.BlockSpec((None, tr, C), lambda r, p: (p[1], r, 0))),
        out_shape=jax.ShapeDtypeStruct((2, Rh, C), F32),
        compiler_params=_cp("arbitrary"),
    )(pos, g4, from_sibling, from_chips)


HBM = pl.BlockSpec(memory_space=pltpu.HBM)


def _mesh_pos():
    return lax.axis_index("x"), lax.axis_index("y"), lax.axis_index("c")


def _other_chips(x, y):
    return [(1 - x, y), (x, 1 - y), (1 - x, 1 - y)]


def _allgather_small(xs, *, name):
    m_per, n = xs.shape

    def body(x_ref, out_ref, send_sems, recv_sems, local_sem):
        x, y, c = _mesh_pos()
        me, sibling = (x, y, c), (x, y, 1 - c)
        chips = _other_chips(x, y)

        def rows(px, py, pc):
            return out_ref.at[pl.ds((4 * px + 2 * py + pc) * m_per, m_per), :]

        def copy(k, block, to, src=None):
            return pltpu.make_async_remote_copy(
                src_ref=rows(*block) if src is None else src, dst_ref=rows(*block),
                send_sem=send_sems.at[k], recv_sem=recv_sems.at[k], device_id=to, device_id_type=MESH)

        mine = pltpu.make_async_copy(x_ref, rows(*me), local_sem)
        mine.start()
        first = [copy(0, me, sibling, src=x_ref)]
        first += [copy(1 + j, me, (*chip, c), src=x_ref) for j, chip in enumerate(chips)]
        for cp in first:
            cp.start()
        passed = [copy(4 + j, (*chip, c), sibling) for j, chip in enumerate(chips)]
        for j, chip in enumerate(chips):
            copy(1 + j, (*chip, c), me).wait_recv()
            passed[j].start()
        copy(0, sibling, me).wait_recv()
        for j, chip in enumerate(chips):
            copy(4 + j, (*chip, 1 - c), me).wait_recv()
        for cp in first + passed:
            cp.wait_send()
        mine.wait()

    return pl.pallas_call(
        body, name=name,
        out_shape=jax.ShapeDtypeStruct((N_DEV * m_per, n), xs.dtype),
        in_specs=[pl.BlockSpec(memory_space=pltpu.VMEM)],
        out_specs=pl.BlockSpec(memory_space=pltpu.VMEM),
        scratch_shapes=[pltpu.SemaphoreType.DMA((7,)), pltpu.SemaphoreType.DMA((7,)), pltpu.SemaphoreType.DMA],
    )(xs)


def _chip_slab_copies(s_ref, out_ref, send_sems, recv_sems):
    R = s_ref.shape[0]
    Rh = R // 2
    x, y, c = _mesh_pos()
    me, sibling = (x, y, c), (x, y, 1 - c)
    chips = _other_chips(x, y)

    def half(px, py, pc):
        return out_ref.at[2 * px + py, pl.ds(pc * Rh, Rh), :]

    def copy(k, block, to, src=None):
        return pltpu.make_async_remote_copy(
            src_ref=half(*block) if src is None else src, dst_ref=half(*block),
            send_sem=send_sems.at[k], recv_sem=recv_sems.at[k], device_id=to, device_id_type=MESH)

    first = [copy(j, me, (*chip, c), src=s_ref.at[pl.ds(c * Rh, Rh), :]) for j, chip in enumerate(chips)]
    passed = [copy(3 + j, (*chip, c), sibling) for j, chip in enumerate(chips)]
    landed = [copy(j, (*chip, c), me) for j, chip in enumerate(chips)]
    from_sibling = [copy(3 + j, (*chip, 1 - c), me) for j, chip in enumerate(chips)]
    return first, passed, landed, from_sibling


def _gather_behind(s_ref, out_ref, send_sems, recv_sems, step, nsteps):
    first, passed, landed, from_sibling = _chip_slab_copies(s_ref, out_ref, send_sems, recv_sems)

    @pl.when(step == 0)
    def _():
        for cp in first:
            cp.start()

    @pl.when(step == (3 * nsteps) // 4)
    def _():
        for arrived, onward in zip(landed, passed):
            arrived.wait_recv()
            onward.start()

    def finish():
        @pl.when(step == nsteps - 1)
        def _():
            for cp in from_sibling:
                cp.wait_recv()
            for cp in first + passed:
                cp.wait_send()

    return finish


def _allgather_chip_slabs(slab, *, name):
    R, C = slab.shape

    def body(s_ref, out_ref, send_sems, recv_sems):
        first, passed, landed, from_sibling = _chip_slab_copies(s_ref, out_ref, send_sems, recv_sems)
        for cp in first:
            cp.start()
        for arrived, onward in zip(landed, passed):
            arrived.wait_recv()
            onward.start()
        for cp in from_sibling:
            cp.wait_recv()
        for cp in first + passed:
            cp.wait_send()

    return pl.pallas_call(
        body, name=name,
        out_shape=jax.ShapeDtypeStruct((N_CHIPS, R, C), slab.dtype),
        in_specs=[HBM], out_specs=HBM,
        scratch_shapes=[pltpu.SemaphoreType.DMA((6,)), pltpu.SemaphoreType.DMA((6,))],
    )(slab)


def _swap_halves(mine, *, name):
    def body(g_ref, out_ref, send_sems, recv_sems):
        x, y, c = _mesh_pos()
        copies = [pltpu.make_async_remote_copy(
            src_ref=g_ref.at[j], dst_ref=out_ref.at[j], send_sem=send_sems.at[j], recv_sem=recv_sems.at[j],
            device_id=(x, y, 1 - c), device_id_type=MESH) for j in range(N_CHIPS)]
        for cp in copies:
            cp.start()
        for cp in copies:
            cp.wait()

    return pl.pallas_call(
        body, name=name,
        out_shape=jax.ShapeDtypeStruct(mine.shape, mine.dtype),
        in_specs=[HBM], out_specs=HBM,
        scratch_shapes=[pltpu.SemaphoreType.DMA((N_CHIPS,)), pltpu.SemaphoreType.DMA((N_CHIPS,))],
    )(mine)


def _scatter_copies(p_ref, out_ref, send_sems, recv_sems):
    x, y, c = _mesh_pos()
    return [pltpu.make_async_remote_copy(
        src_ref=p_ref.at[2 * px + py], dst_ref=out_ref.at[j], send_sem=send_sems.at[j], recv_sem=recv_sems.at[j],
        device_id=(px, py, c), device_id_type=MESH) for j, (px, py) in enumerate(_other_chips(x, y))]


def _join_halves(buf, *, name):
    def body(b_ref, out_ref, send_sem, recv_sem):
        x, y, c = _mesh_pos()
        cp = pltpu.make_async_remote_copy(
            src_ref=b_ref.at[c], dst_ref=out_ref.at[c], send_sem=send_sem, recv_sem=recv_sem,
            device_id=(x, y, 1 - c), device_id_type=MESH)
        cp.start()
        cp.wait()

    return pl.pallas_call(
        body, name=name,
        out_shape=jax.ShapeDtypeStruct(buf.shape, buf.dtype),
        in_specs=[HBM], out_specs=HBM, input_output_aliases={0: 0},
        scratch_shapes=[pltpu.SemaphoreType.DMA, pltpu.SemaphoreType.DMA],
    )(buf)


def _pad_rows(a, mult):
    pad = (-a.shape[0]) % mult
    return a if pad == 0 else jnp.pad(a, ((0, pad),) + ((0, 0),) * (a.ndim - 1))


def _local_step(x, target, mod, wts, small, slabs=None, unpacks=None, reduce_early=None, grad_slab=None,
                reduce_late=None):
    S, D = x.shape
    HP = D // LANES
    row = lambda v: v.reshape(1, -1)
    msplit = [[row(mod[i, k * D:(k + 1) * D]) for k in range(6)] for i in range(2)]
    gw, gs = {}, {}
    dmod = [[None] * 6 for _ in range(2)]
    slab, where = grad_slab if grad_slab is not None else (None, {})

    def dw(key, a, b, name):
        nonlocal slab
        if key in where:
            slab = _matmul_tn(a, b, name=name, into=(slab,) + where[key])
        else:
            gw[key] = _matmul_tn(a, b, name=name)

    sh1, sc1, g1, sh2, sc2, g2 = msplit[0]
    n1w0, n2w0 = row(small["norm1_w"][0]), row(small["norm2_w"][0])
    slabs = slabs if slabs is not None else (None, None)
    proj0, h1_0, *gathered = _ln_matmul(x, n1w0, sc1, sh1, wts["hg_w_in"], slabs[0], relu2=False, name="hg_in_proj")
    if slabs[0] is not None:
        wts = {**wts, **unpacks[0](gathered[0])}
    gn = small["hg_gn_w"].reshape(1, LANES)
    ypre0, o0, states, *gathered = _hg_fwd(proj0, small["hg_lb"], gn, slabs[1], name="hg_fwd")
    if slabs[1] is not None:
        wts = {**wts, **unpacks[1](gathered[0])}
    x1, ymix0 = _matmul_resid(ypre0, wts["hg_w_out"], x, g1, name="hg_out_proj")
    a0, u0, h2_0 = _ln_matmul(x1, n2w0, sc2, sh2, wts["mlp_w1_0"], relu2=True, name="mlp0_up")
    x2, ymlp0 = _matmul_resid(u0, wts["mlp_w2_0"], x1, g2, name="mlp0_down")

    sh1b, sc1b, g1b, sh2b, sc2b, g2b = msplit[1]
    n1w1, n2w1 = row(small["norm1_w"][1]), row(small["norm2_w"][1])
    proj1, h1_1 = _ln_matmul(x2, n1w1, sc1b, sh1b, wts["fox_w_in"], relu2=False, name="fox_in_proj")
    nheads = 2 * HP
    bf_pad = jnp.pad(small["fox_b_f"].reshape(1, nheads), ((0, 0), (0, LANES - nheads)))
    qw2 = jnp.tile(small["fox_qn_w"].reshape(1, FOX_DH), (1, 2))
    kw2 = jnp.tile(small["fox_kn_w"].reshape(1, FOX_DH), (1, 2))
    fcum = _fox_cumsum(proj1, bf_pad, name="fox_cumsum")
    qa, ka, va, vat = _fox_prep(proj1, fcum, qw2, kw2, name="fox_prep")
    jmin, imax = _fox_skip_bounds(fcum, small["fox_qn_w"], small["fox_kn_w"], nheads)
    ypre1, o1, q2 = _fox_fwd(jmin, qa, ka, vat, proj1, name="fox_fwd")
    x3, ymix1 = _matmul_resid(ypre1, wts["fox_w_out"], x2, g1b, name="fox_out_proj")
    a1, u1, h2_1 = _ln_matmul(x3, n2w1, sc2b, sh2b, wts["mlp_w1_1"], relu2=True, name="mlp1_up")
    x4, ymlp1 = _matmul_resid(u1, wts["mlp_w2_1"], x3, g2b, name="mlp1_down")

    loss, dx4, dfw = _loss_kernel(x4, row(small["final_w"]), target, name="loss")
    gs["final_w"] = dfw.reshape(-1)

    def mlp_bwd(i, dx_out, x_in, h2, a, u, ymlp, n2w, sc2_, g2_):
        dz, dm, dg2 = _gate_matmul_nt(dx_out, g2_, ymlp, wts[f"mlp_w2_{i}"], a, name=f"mlp{i}_down_bwd")
        dw(f"mlp_w2_{i}", u, dm[None], f"mlp{i}_dw2")
        dw(f"mlp_w1_{i}", h2, dz[None], f"mlp{i}_dw1")
        dx_in, dsc, dsh, dnw = _matmul_nt_lnbwd(dz[None], wts[f"mlp_w1_{i}"], x_in, n2w, sc2_, dx_out,
                                                name=f"mlp{i}_up_bwd")
        dmod[i][3], dmod[i][4], dmod[i][5] = dsh, dsc, dg2
        return dx_in, dnw

    dx3, dn2w1 = mlp_bwd(1, dx4, x3, h2_1, a1, u1, ymlp1, n2w1, sc2b, g2b)
    dyp1, dm1, dg1b = _gate_matmul_nt(dx3, g1b, ymix1, wts["fox_w_out"], None, name="fox_out_bwd")
    dw("fox_w_out", ypre1, dm1[None], "fox_dw_out")
    doa = _fox_bwd_prep(dyp1, o1, proj1, q2, name="fox_bwd_prep")
    dqa, dka, dva, colsum = _fox_bwd(imax, q2, ka, va, doa, name="fox_bwd")
    colsum = jnp.pad(colsum[:, 0, :].T, ((0, 0), (0, LANES - nheads)))
    dproj1, dqw, dkw = _fox_bwd_post(dqa, dka, dva, proj1, dyp1, o1, qw2, kw2, name="fox_bwd_post")
    dproj1, dbf = _fox_dfz(colsum, nheads, proj1, bf_pad, dproj1, name="fox_dfz")
    dw("fox_w_in", h1_1, dproj1, "fox_dw_in")
    dx2, dsc, dsh, dn1w1 = _matmul_nt_lnbwd(dproj1, wts["fox_w_in"], x2, n1w1, sc1b, dx3, name="fox_in_bwd")
    dmod[1][0], dmod[1][1], dmod[1][2] = dsh, dsc, dg1b
    gs["fox_qn_w"] = dqw[0, :FOX_DH] + dqw[0, FOX_DH:]
    gs["fox_kn_w"] = dkw[0, :FOX_DH] + dkw[0, FOX_DH:]
    gs["fox_b_f"] = dbf[0, :nheads]

    dx1, dn2w0 = mlp_bwd(0, dx2, x1, h2_0, a0, u0, ymlp0, n2w0, sc2, g2)
    dyp0, dm0, dg1 = _gate_matmul_nt(dx1, g1, ymix0, wts["hg_w_out"], None, name="hg_out_bwd")
    dw("hg_w_out", ypre0, dm0[None], "hg_dw_out")
    part, ctx = reduce_early(gw, slab) if reduce_early is not None else (None, None)
    dproj0, dlb, dgn, *from_chips = _hg_bwd(proj0, small["hg_lb"], gn, o0, states, dyp0, part, name="hg_bwd")
    early = (ctx, from_chips[0]) if reduce_early is not None else None
    dw("hg_w_in", h1_0, dproj0, "hg_dw_in")
    part, ctx = reduce_late(gw) if reduce_late is not None else (None, None)
    dx0, dsc, dsh, dn1w0, *from_chips = _matmul_nt_lnbwd(dproj0, wts["hg_w_in"], x, n1w0, sc1, dx1, part, name="hg_in_bwd")
    late = (ctx, from_chips[0]) if reduce_late is not None else None
    dmod[0][0], dmod[0][1], dmod[0][2] = dsh, dsc, dg1
    gs["hg_lb"] = dlb
    gs["hg_gn_w"] = jnp.sum(dgn, axis=0)

    gs["norm1_w"] = jnp.concatenate([dn1w0, dn1w1], axis=0)
    gs["norm2_w"] = jnp.concatenate([dn2w0, dn2w1], axis=0)
    gs["dmod"] = jnp.stack([jnp.concatenate(dmod[i], axis=1)[0] for i in range(2)])
    return loss, dx0, gw, gs, early, late


def _pack_halves(layout):
    rh = -(-max(sum(a.shape[0] for _, a in half) for half in layout) // 16) * 16
    place, parts = {}, []
    for h, half in enumerate(layout):
        off = 0
        for n, a in half:
            place[n] = (h, off, a.shape[0])
            off += a.shape[0]
        parts.append(jnp.pad(jnp.concatenate([a.astype(BF) for _, a in half], axis=0), ((0, rh - off), (0, 0))))
    return jnp.concatenate(parts, axis=0), place, rh


SMALL_NAMES = ["norm1_w", "norm2_w", "hg_lb", "hg_gn_w", "fox_b_f", "fox_qn_w", "fox_kn_w", "final_w"]


def _pack_small(d, names):
    rows, offs, r0 = [], {}, 0
    for n in names:
        flat = d[n].reshape(-1)
        nr = -(-flat.shape[0] // LANES)
        rows.append(jnp.pad(flat, (0, nr * LANES - flat.shape[0])).reshape(nr, LANES))
        offs[n] = (r0, nr)
        r0 += nr
    return jnp.concatenate(rows, axis=0), offs


def _unpack_small(packed, offs, name, like):
    r0, nr = offs[name]
    return packed[r0:r0 + nr].reshape(-1)[:like.size].reshape(like.shape)


def kernel(x, c, w_mod, b_mod, norm1_w, norm2_w, hg_w_in, hg_w_out, hg_lb, hg_gn_w, fox_w_in, fox_b_f, fox_qn_w, fox_kn_w, fox_w_out, mlp_w1, mlp_w2, final_w, loss_target, m_w_mod, m_b_mod, m_norm1_w, m_norm2_w, m_hg_w_in, m_hg_w_out, m_hg_lb, m_hg_gn_w, m_fox_w_in, m_fox_b_f, m_fox_qn_w, m_fox_kn_w, m_fox_w_out, m_mlp_w1, m_mlp_w2, m_final_w, v_w_mod, v_b_mod, v_norm1_w, v_norm2_w, v_hg_w_in, v_hg_w_out, v_hg_lb, v_hg_gn_w, v_fox_w_in, v_fox_b_f, v_fox_qn_w, v_fox_kn_w, v_fox_w_out, v_mlp_w1, v_mlp_w2, v_final_w):
    S, D = x.shape[1], x.shape[2]
    nheads = D // FOX_DH
    ax, ay, ac = _mesh_pos()
    chip = 2 * ax + ay
    dev = 2 * chip + ac
    xs, tgt = x.reshape(S, D), loss_target.reshape(S, D)

    c_all = _allgather_small(_pad_rows(c.reshape(-1, LANES), 8), name="gather_c")
    c_all = c_all.reshape(N_DEV, -1)[:, :D]
    c16 = _pad_rows(c_all, 16)
    nmod = w_mod.shape[2]
    b_shard = lax.dynamic_slice_in_dim(b_mod, chip * nmod, nmod, axis=1)
    mod_shard = _mod_fwd(c16, w_mod, b_shard[:, None, :], name="mod_fwd")[:, :N_DEV]
    mod_all = _allgather_small(mod_shard.reshape(-1, LANES), name="gather_mod")
    mod_all = mod_all.reshape(N_CHIPS, 2, 2, N_DEV, nmod)[:, 0]
    mod = lax.dynamic_index_in_dim(mod_all, dev, axis=2, keepdims=False)
    mod = mod.transpose(1, 0, 2).reshape(2, N_CHIPS * nmod)

    fox_rows = fox_w_in.shape[2]
    col = lambda g: g.transpose(1, 0, 2).reshape(g.shape[1], -1)
    rowsh = lambda g: g.reshape(-1, g.shape[2])
    own = lambda g, s: lax.dynamic_update_index_in_dim(g, s, chip, 0)

    slab_in = hg_w_in[0].astype(BF)
    wts = {"hg_w_in": col(own(_allgather_chip_slabs(slab_in, name="gather_hg_w_in"), slab_in))}
    fox_flat, fox_cut = fox_w_in[0].reshape(fox_rows, D), fox_rows // 2
    slabs, unpacks = [], []
    for layout_w in ([[("mlp_w1_0", mlp_w1[0]), ("hg_w_out", hg_w_out[0])], [("mlp_w2_0", mlp_w2[0]), ("fox_w_out", fox_w_out[0])]],
                     [[("mlp_w1_1", mlp_w1[1]), ("fox_a", fox_flat[:fox_cut])], [("mlp_w2_1", mlp_w2[1]), ("fox_b", fox_flat[fox_cut:])]]):
        slab_w, place_w, rh_w = _pack_halves(layout_w)

        def unpack(gathered, slab_w=slab_w, place_w=place_w, rh_w=rh_w):
            gathered = own(gathered, slab_w)
            out = {}
            for n, (h, off, rows) in place_w.items():
                g = gathered[:, h * rh_w + off:h * rh_w + off + rows, :]
                out[n] = col(g) if n.startswith("mlp_w1") else rowsh(g) if n.startswith(("mlp_w2", "hg_", "fox_w")) else g
            if "fox_a" in out:
                fox_in = col(jnp.concatenate([out.pop("fox_a"), out.pop("fox_b")], axis=1).reshape(N_CHIPS, D, fox_rows))
                out["fox_w_in"] = jnp.pad(fox_in, ((0, 0), (0, 5 * D - fox_in.shape[1])))
            return out

        slabs.append(slab_w)
        unpacks.append(unpack)

    small = {"norm1_w": norm1_w, "norm2_w": norm2_w, "hg_lb": hg_lb, "hg_gn_w": hg_gn_w, "fox_b_f": fox_b_f,
             "fox_qn_w": fox_qn_w, "fox_kn_w": fox_kn_w, "final_w": final_w}

    def uncol(g, n):
        return g.reshape(g.shape[0], N_CHIPS, n).transpose(1, 0, 2)

    pos = jnp.stack([chip, ac])

    def swap_and_add(g4, tag):
        to_sibling = lax.dynamic_index_in_dim(g4, 1 - ac, axis=1, keepdims=False).astype(BF)
        from_sibling = _swap_halves(to_sibling, name=f"rs_swap_{tag}")
        return from_sibling, _add_halves(g4, from_sibling, ac.reshape(1), name=f"rs_add_halves_{tag}")

    def finish(g4, from_sibling, from_chips, tag):
        my_half = _add_four(g4, from_sibling, from_chips, pos, name=f"rs_add_chips_{tag}")
        return _join_halves(my_half, name=f"rs_join_{tag}")

    layout = [[("mlp_w1", 2 * D), ("hg_w_out", D // 4), ("fox_w_out", D // 4)], [("mlp_w2", 2 * D), ("fox_w_in", fox_rows)]]
    place = {}
    for h, half in enumerate(layout):
        off = 0
        for n, rows in half:
            place[n] = (h, off, rows)
            off += rows

    rh = -(-max(sum(rows for _, rows in half) for half in layout) // 16) * 16
    where = {"hg_w_out": ("row",) + place["hg_w_out"][:2], "fox_w_out": ("row",) + place["fox_w_out"][:2]}
    for i in range(2):
        where[f"mlp_w1_{i}"] = ("col", place["mlp_w1"][0], place["mlp_w1"][1] + i * D)
        where[f"mlp_w2_{i}"] = ("row", place["mlp_w2"][0], place["mlp_w2"][1] + i * D)

    def reduce_early(gw, slab):
        gfox = uncol(gw["fox_w_in"][:, :4 * fox_rows], fox_rows).reshape(N_CHIPS, 1, fox_rows, D)
        h, off, _ = place["fox_w_in"]
        slab = lax.dynamic_update_slice(slab, gfox, (0, h, off, 0))
        for h, half in enumerate(layout):
            used = sum(rows for _, rows in half)
            if used < rh:
                slab = lax.dynamic_update_slice(slab, jnp.zeros((N_CHIPS, 1, rh - used, D), F32), (0, h, used, 0))
        from_sibling, part = swap_and_add(slab, "early")
        return part, (slab, from_sibling)

    def reduce_late(gw):
        g4 = uncol(gw["hg_w_in"], D).reshape(N_CHIPS, 2, D // 2, D)
        from_sibling, part = swap_and_add(g4, "late")
        return part, (g4, from_sibling)

    loss_part, grad_x, gw, gs, (early, from_chips_early), (late, from_chips_late) = _local_step(
        xs, tgt, mod, wts, small, slabs, unpacks, reduce_early, (lax.empty((N_CHIPS, 2, rh, D), F32), where), reduce_late)
    gshard = finish(*early, from_chips_early, "early")
    g_hg_w_in = finish(*late, from_chips_late, "late").reshape(D, D)

    names = ["dmod", "loss"] + SMALL_NAMES
    packed, offs = _pack_small({**gs, "loss": loss_part[0, :1]}, names)
    packed = _pad_rows(packed, 8)
    rp = packed.shape[0]
    parts = _allgather_small(packed, name="gather_small").reshape(N_DEV, rp, LANES)
    total = _sum_parts(parts, name="sum_small")
    r0, nr = offs["dmod"]
    dmod_all = parts[:, r0:r0 + nr].reshape(N_DEV, 2, N_CHIPS * nmod)
    dmod_shard = lax.dynamic_slice_in_dim(dmod_all, chip * nmod, nmod, axis=2).transpose(1, 0, 2)
    g_w_mod = _mod_bwd(c16, jnp.pad(dmod_shard, ((0, 0), (0, 16 - N_DEV), (0, 0))), name="mod_bwd")

    loss = _unpack_small(total, offs, "loss", loss_part[0, :1]).reshape(())
    grads = {"w_mod": g_w_mod, "b_mod": _unpack_small(total, offs, "dmod", b_mod)}
    for n in SMALL_NAMES:
        grads[n] = _unpack_small(total, offs, n, small[n])

    given = dict(w_mod=(w_mod, m_w_mod, v_w_mod), b_mod=(b_mod, m_b_mod, v_b_mod), norm1_w=(norm1_w, m_norm1_w, v_norm1_w),
                 norm2_w=(norm2_w, m_norm2_w, v_norm2_w), hg_w_in=(hg_w_in, m_hg_w_in, v_hg_w_in),
                 hg_w_out=(hg_w_out, m_hg_w_out, v_hg_w_out), hg_lb=(hg_lb, m_hg_lb, v_hg_lb),
                 hg_gn_w=(hg_gn_w, m_hg_gn_w, v_hg_gn_w), fox_w_in=(fox_w_in, m_fox_w_in, v_fox_w_in),
                 fox_b_f=(fox_b_f, m_fox_b_f, v_fox_b_f), fox_qn_w=(fox_qn_w, m_fox_qn_w, v_fox_qn_w),
                 fox_kn_w=(fox_kn_w, m_fox_kn_w, v_fox_kn_w), fox_w_out=(fox_w_out, m_fox_w_out, v_fox_w_out),
                 mlp_w1=(mlp_w1, m_mlp_w1, v_mlp_w1), mlp_w2=(mlp_w2, m_mlp_w2, v_mlp_w2), final_w=(final_w, m_final_w, v_final_w))
    upd = {}

    for n, (h, off, rows) in place.items():
        w, m, v = given[n]
        flat = lambda a: a.reshape(rows, D)
        d, mn, vn = _adamw(flat(w), gshard, flat(m), flat(v), g_at=(h, off), name=f"adamw_{n}")
        grads[n] = gshard[h, off:off + rows].reshape(w.shape)
        upd[n] = tuple(a.reshape(w.shape) for a in (d, mn, vn))

    w, m, v = given["hg_w_in"]
    grads["hg_w_in"] = g_hg_w_in.reshape(w.shape)
    upd["hg_w_in"] = tuple(a.reshape(w.shape) for a in _adamw(w[0], g_hg_w_in, m[0], v[0], name="adamw_hg_w_in"))

    w, m, v = given["w_mod"]
    flat = lambda a: a.reshape(-1, nmod)
    upd["w_mod"] = tuple(a.reshape(w.shape) for a in _adamw(flat(w), flat(g_w_mod), flat(m), flat(v), name="adamw_w_mod"))

    snames = ["b_mod"] + SMALL_NAMES
    pw, soffs = _pack_small({n: given[n][0] for n in snames}, snames)
    pm, _ = _pack_small({n: given[n][1] for n in snames}, snames)
    pv, _ = _pack_small({n: given[n][2] for n in snames}, snames)
    pg, _ = _pack_small({n: grads[n] for n in snames}, snames)
    pw, pm, pv, pg = (_pad_rows(a, 8) for a in (pw, pm, pv, pg))
    sd, smn, svn = _adamw(pw, pg, pm, pv, name="adamw_small")
    for n in snames:
        like = given[n][0]
        upd[n] = tuple(_unpack_small(a, soffs, n, like) for a in (sd, smn, svn))

    order = ["w_mod", "b_mod", "norm1_w", "norm2_w", "hg_w_in", "hg_w_out", "hg_lb", "hg_gn_w", "fox_w_in", "fox_b_f",
             "fox_qn_w", "fox_kn_w", "fox_w_out", "mlp_w1", "mlp_w2", "final_w"]
    return (loss, grad_x.reshape(x.shape), *[grads[n] for n in order], *[upd[n][0] for n in order],
            *[upd[n][1] for n in order], *[upd[n][2] for n in order])
```

```python
import math

import jax
import jax.numpy as jnp
from jax import lax
from jax.experimental import pallas as pl
from jax.experimental.pallas import tpu as pltpu

EPS = 1e-6
ADAM_LR, ADAM_B1, ADAM_B2, ADAM_EPS, ADAM_WD, ADAM_STEP = 0.001, 0.9, 0.999, 1e-08, 0.01, 10

F32 = jnp.float32
BF = jnp.bfloat16
LANES = 128
HG_CHUNK = 64
HG_HEADS_PER_STEP = 8
HG_TOKENS_PER_STEP = 256
FOX_ROWS_PER_STEP = 1024
FOX_BWD_TILES = (8, 4, 2, 1)
LOG2E = 1.4426950408889634
FOX_DH = 64
N_CHIPS = 4
N_DEV = 8
VMEM_LIMIT = 56 * 1024 * 1024
MESH = pl.DeviceIdType.MESH

NT = (((1,), (1,)), ((), ()))
TN = (((0,), (0,)), ((), ()))


def _pick(n, pref, mult=LANES):
    if n <= pref:
        return n
    t = (pref // mult) * mult
    while t >= mult:
        if n % t == 0:
            return t
        t -= mult
    raise ValueError((n, pref, mult))


def _cp(*sem):
    return pltpu.CompilerParams(dimension_semantics=sem, vmem_limit_bytes=VMEM_LIMIT)


def _dot(a, b):
    return jnp.dot(a, b, preferred_element_type=F32)


def _dg(a, b, dims):
    return lax.dot_general(a, b, dims, preferred_element_type=F32)


def _split3(x):
    hi = x.astype(BF)
    r1 = x - hi.astype(F32)
    mid = r1.astype(BF)
    lo = (r1 - mid.astype(F32)).astype(BF)
    return hi, mid, lo


def _tri_dot(tri, x):
    hi, mid, lo = _split3(x)
    return _dot(tri, hi) + _dot(tri, mid) + _dot(tri, lo)


def _dg3(a, b, dims):
    ah, bh = a.astype(BF), b.astype(BF)
    al, bl = (a - ah.astype(F32)).astype(BF), (b - bh.astype(F32)).astype(BF)
    return _dg(ah, bh, dims) + _dg(ah, bl, dims) + _dg(al, bh, dims)


def _dg1(a, b, dims):
    return _dg(a.astype(BF), b.astype(BF), dims)


NN = (((1,), (0,)), ((), ()))


def _sigmoid(x):
    return jax.nn.sigmoid(x)


def _ln_matmul(x, nw, sc, sh, w, slab=None, *, relu2, name):
    S, D = x.shape
    N = w.shape[1]
    tm, tn = _pick(S, 512, 16), N
    fused = slab is not None

    def body(x_ref, nw_ref, sc_ref, sh_ref, w_ref, *rest):
        if fused:
            s_ref, *outs, out_ref, hs, send_sems, recv_sems = rest
            finish = _gather_behind(s_ref, out_ref, send_sems, recv_sems, pl.program_id(0), S // tm)
        else:
            outs, hs = rest[:-1], rest[-1]
        h_ref = outs[-1]

        @pl.when(pl.program_id(1) == 0)
        def _():
            xv = x_ref[...]
            r = lax.rsqrt(jnp.mean(xv * xv, axis=-1, keepdims=True) + EPS)
            hb = ((xv * r * nw_ref[...]) * (1.0 + sc_ref[...]) + sh_ref[...]).astype(BF)
            hs[...] = hb
            h_ref[...] = hb

        z = _dot(hs[...], w_ref[...])
        if relu2:
            a = jnp.maximum(z, 0.0)
            outs[0][...] = a.astype(BF)
            outs[1][...] = (a * a).astype(BF)
        else:
            outs[0][...] = z
        if fused:
            finish()

    vec = pl.BlockSpec((1, D), lambda i, j: (0, 0))
    tile = pl.BlockSpec((tm, tn), lambda i, j: (i, j))
    if relu2:
        out_shape = [jax.ShapeDtypeStruct((S, N), BF), jax.ShapeDtypeStruct((S, N), BF)]
        out_specs = [tile, tile]
    else:
        out_shape = [jax.ShapeDtypeStruct((S, N), F32)]
        out_specs = [tile]
    out_shape.append(jax.ShapeDtypeStruct((S, D), BF))
    out_specs.append(pl.BlockSpec((tm, D), lambda i, j: (i, 0)))
    in_specs = [pl.BlockSpec((tm, D), lambda i, j: (i, 0)), vec, vec, vec, pl.BlockSpec((D, tn), lambda i, j: (0, j))]
    scratch = [pltpu.VMEM((tm, D), BF)]
    args = [x, nw, sc, sh, w]
    if fused:
        in_specs.append(HBM)
        out_specs.append(HBM)
        out_shape.append(jax.ShapeDtypeStruct((N_CHIPS,) + slab.shape, slab.dtype))
        scratch += [pltpu.SemaphoreType.DMA((6,)), pltpu.SemaphoreType.DMA((6,))]
        args.append(slab)
    return pl.pallas_call(
        body, name=name, grid=(S // tm, N // tn), in_specs=in_specs, out_specs=out_specs, out_shape=out_shape,
        scratch_shapes=scratch, compiler_params=_cp("arbitrary", "arbitrary"),
    )(*args)


def _matmul_resid(a, w, x, gate, *, name):
    S, K = a.shape
    D = w.shape[1]
    tm, tn = _pick(S, 1024 if K <= 1024 else 512, 16), D

    def body(a_ref, w_ref, x_ref, g_ref, o_ref, y_ref):
        y = _dot(a_ref[...], w_ref[...])
        y_ref[...] = y.astype(BF)
        o_ref[...] = x_ref[...] + g_ref[...] * y

    tile = pl.BlockSpec((tm, tn), lambda i, j: (i, j))
    return pl.pallas_call(
        body, name=name, grid=(S // tm, D // tn),
        in_specs=[pl.BlockSpec((tm, K), lambda i, j: (i, 0)), pl.BlockSpec((K, tn), lambda i, j: (0, j)),
                  tile, pl.BlockSpec((1, tn), lambda i, j: (0, j))],
        out_specs=[tile, tile],
        out_shape=[jax.ShapeDtypeStruct((S, D), F32), jax.ShapeDtypeStruct((S, D), BF)],
        compiler_params=_cp("parallel", "arbitrary"),
    )(a, w, x, gate)


def _gate_matmul_nt(dx, gate, y, w, act, *, name):
    S, D = dx.shape
    K = w.shape[0]
    tm, tn = _pick(S, 1024 if K <= 1024 else 512, 16), K
    fused = act is not None

    def body(dx_ref, g_ref, y_ref, w_ref, *rest):
        if fused:
            act_ref, da_ref, dm_ref, dg_ref, ms = rest
        else:
            da_ref, dm_ref, dg_ref, ms = rest
        i, j = pl.program_id(0), pl.program_id(1)

        @pl.when((i == 0) & (j == 0))
        def _():
            dg_ref[...] = jnp.zeros_like(dg_ref)

        @pl.when(j == 0)
        def _():
            dxv = dx_ref[...]
            dmb = (dxv * g_ref[...]).astype(BF)
            ms[...] = dmb
            dm_ref[...] = dmb
            dg_ref[...] += jnp.sum(dxv * y_ref[...].astype(F32), axis=0, keepdims=True)

        da = _dg(ms[...], w_ref[...], NT)
        if fused:
            da_ref[...] = (da * (2.0 * act_ref[...].astype(F32))).astype(BF)
        else:
            da_ref[...] = da

    row = pl.BlockSpec((tm, D), lambda i, j: (i, 0))
    vec = pl.BlockSpec((1, D), lambda i, j: (0, 0))
    tile = pl.BlockSpec((tm, tn), lambda i, j: (i, j))
    in_specs = [row, vec, row, pl.BlockSpec((tn, D), lambda i, j: (j, 0))]
    args = [dx, gate, y, w]
    if fused:
        in_specs.append(tile)
        args.append(act)
    return pl.pallas_call(
        body, name=name, grid=(S // tm, K // tn),
        in_specs=in_specs, out_specs=[tile, row, vec],
        out_shape=[jax.ShapeDtypeStruct((S, K), BF if fused else F32), jax.ShapeDtypeStruct((S, D), BF),
                   jax.ShapeDtypeStruct((1, D), F32)],
        scratch_shapes=[pltpu.VMEM((tm, D), BF)],
        compiler_params=_cp("arbitrary", "arbitrary"),
    )(*args)


def _matmul_tn(a, b, *, name, into=None):
    S, Ka = a.shape
    P, _, Db = b.shape
    tk, tn, ts = _pick(Ka, 1024), _pick(Db, 1024), _pick(S, 1024, 16)
    if into is not None:
        slab, kind, half, off = into
        C = tn = slab.shape[3]
        if kind == "row":
            tk = min(tk, Ka // N_CHIPS)
        assert tn == C and P * Db == (N_CHIPS * C if kind == "col" else C) and off % tk == 0
        assert tk == Ka if kind == "col" else (Ka // N_CHIPS) % tk == 0
    npb = Db // tn

    def body(a_ref, b_ref, *rest):
        o_ref, acc = rest[-2:]
        s = pl.program_id(2)

        @pl.when(s == 0)
        def _():
            acc[...] = jnp.zeros_like(acc)

        acc[...] += _dg(a_ref[...], b_ref[...], TN)

        @pl.when(s == pl.num_programs(2) - 1)
        def _():
            o_ref[...] = acc[...]

    in_specs = [pl.BlockSpec((ts, tk), lambda i, j, s: (s, i)),
                pl.BlockSpec((None, ts, tn), lambda i, j, s: (j // npb, s, j % npb))]
    args = [a, b]
    if into is None:
        out_spec = pl.BlockSpec((tk, tn), lambda i, j, s: (i, j))
        out_shape = jax.ShapeDtypeStruct((Ka, P * Db), F32)
        aliases = {}
    else:
        per = (Ka // N_CHIPS) // tk if kind == "row" else 1
        if kind == "col":
            out_spec = pl.BlockSpec((None, None, tk, tn), lambda i, j, s: (j, half, off // tk + i, 0))
        else:
            out_spec = pl.BlockSpec((None, None, tk, tn), lambda i, j, s: (i // per, half, off // tk + i % per, 0))
        out_shape = jax.ShapeDtypeStruct(slab.shape, F32)
        in_specs.append(pl.BlockSpec(memory_space=pl.ANY))
        args.append(slab)
        aliases = {2: 0}
    return pl.pallas_call(
        body, name=name, grid=(Ka // tk, P * npb, S // ts),
        in_specs=in_specs, out_specs=out_spec, out_shape=out_shape,
        scratch_shapes=[pltpu.VMEM((tk, tn), F32)], input_output_aliases=aliases,
        compiler_params=_cp("parallel", "parallel", "arbitrary"),
    )(*args)


def _matmul_nt_lnbwd(g, w, x, nw, sc, dx_out, part=None, *, name):
    P, S, Dg = g.shape
    D = x.shape[1]
    tm = _pick(S, 512, 16)
    fused = part is not None

    def body(g_ref, w_ref, x_ref, nw_ref, sc_ref, dxo_ref, *rest):
        if fused:
            p_ref, dx_ref, dsc_ref, dsh_ref, dnw_ref, recv_ref, send_sems, recv_sems = rest
            copies = _scatter_copies(p_ref, recv_ref, send_sems, recv_sems)
        else:
            dx_ref, dsc_ref, dsh_ref, dnw_ref = rest

        @pl.when(pl.program_id(0) == 0)
        def _():
            dsc_ref[...] = jnp.zeros_like(dsc_ref)
            dsh_ref[...] = jnp.zeros_like(dsh_ref)
            dnw_ref[...] = jnp.zeros_like(dnw_ref)
            if fused:
                for cp in copies:
                    cp.start()

        dh = _dg(g_ref[0], w_ref[:, 0:Dg], NT)
        for p in range(1, P):
            dh = dh + _dg(g_ref[p], w_ref[:, p * Dg:(p + 1) * Dg], NT)
        xv = x_ref[...]
        nwv = nw_ref[...]
        r = lax.rsqrt(jnp.mean(xv * xv, axis=-1, keepdims=True) + EPS)
        xr = xv * r
        dn = dh * (1.0 + sc_ref[...])
        dsc_ref[...] += jnp.sum(dh * (xr * nwv), axis=0, keepdims=True)
        dsh_ref[...] += jnp.sum(dh, axis=0, keepdims=True)
        dnw_ref[...] += jnp.sum(dn * xr, axis=0, keepdims=True)
        u = dn * nwv
        dx_ref[...] = dxo_ref[...] + r * (u - xr * jnp.mean(u * xr, axis=-1, keepdims=True))

        if fused:
            @pl.when(pl.program_id(0) == S // tm - 1)
            def _():
                for cp in copies:
                    cp.wait()

    row = pl.BlockSpec((tm, D), lambda i: (i, 0))
    vec = pl.BlockSpec((1, D), lambda i: (0, 0))
    in_specs = [pl.BlockSpec((P, tm, Dg), lambda i: (0, i, 0)), pl.BlockSpec((D, P * Dg), lambda i: (0, 0)), row, vec, vec, row]
    out_specs = [row, vec, vec, vec]
    out_shape = [jax.ShapeDtypeStruct((S, D), F32)] + [jax.ShapeDtypeStruct((1, D), F32)] * 3
    scratch, args = [], [g, w, x, nw, sc, dx_out]
    if fused:
        in_specs.append(HBM)
        out_specs.append(HBM)
        out_shape.append(jax.ShapeDtypeStruct((3,) + part.shape[1:], part.dtype))
        scratch = [pltpu.SemaphoreType.DMA((3,)), pltpu.SemaphoreType.DMA((3,))]
        args.append(part)
    return pl.pallas_call(
        body, name=name, grid=(S // tm,), in_specs=in_specs, out_specs=out_specs, out_shape=out_shape,
        scratch_shapes=scratch, compiler_params=_cp("arbitrary"),
    )(*args)


def _loss_kernel(x, fw, tgt, *, name):
    S, D = x.shape
    tm = _pick(S, 512, 8)

    def body(x_ref, fw_ref, t_ref, l_ref, dx_ref, dfw_ref):
        @pl.when(pl.program_id(0) == 0)
        def _():
            l_ref[...] = jnp.zeros_like(l_ref)
            dfw_ref[...] = jnp.zeros_like(dfw_ref)

        xv = x_ref[...]
        fwv = fw_ref[...]
        r = lax.rsqrt(jnp.mean(xv * xv, axis=-1, keepdims=True) + EPS)
        xr = xv * r
        err = xr * fwv - t_ref[...]
        per_tok = jnp.mean(err * err, axis=-1, keepdims=True)
        l_ref[...] += 0.5 * jnp.sum(per_tok, axis=0, keepdims=True)
        dy = err * (1.0 / D)
        dfw_ref[...] += jnp.sum(dy * xr, axis=0, keepdims=True)
        u = dy * fwv
        dx_ref[...] = r * (u - xr * jnp.mean(u * xr, axis=-1, keepdims=True))

    row = pl.BlockSpec((tm, D), lambda i: (i, 0))
    vec = pl.BlockSpec((1, D), lambda i: (0, 0))
    return pl.pallas_call(
        body, name=name, grid=(S // tm,),
        in_specs=[row, vec, row],
        out_specs=[pl.BlockSpec((1, LANES), lambda i: (0, 0)), row, vec],
        out_shape=[jax.ShapeDtypeStruct((1, LANES), F32), jax.ShapeDtypeStruct((S, D), F32),
                   jax.ShapeDtypeStruct((1, D), F32)],
        compiler_params=_cp("arbitrary"),
    )(x, fw, tgt)


def _hg_lower_bound(lb3):
    mx = jnp.max(lb3, axis=0, keepdims=True)
    e = jnp.exp(lb3 - mx)
    p = e / jnp.sum(e, axis=0, keepdims=True)
    return p[0:1, :], p


def _hg_chunk_common(qr, fz, lbv):
    sq = _sigmoid(qr)
    q = qr * sq
    sig = _sigmoid(fz)
    f = lbv + (1.0 - lbv) * sig
    k = (1.0 - lbv) * (1.0 - sig)
    return q, sq, sig, f, k, jnp.log(f)


def _row_of(x, rows, r):
    return jnp.sum(jnp.where(rows == r, x, 0.0), axis=0, keepdims=True)


def _hg_fwd(proj, hg_lb, gn, slab=None, *, name):
    S = proj.shape[0]
    D = proj.shape[1] // 4
    H = D // LANES
    HB = min(HG_HEADS_PER_STEP, H)
    W = HB * LANES
    C = HG_CHUNK
    T = _pick(S, HG_TOKENS_PER_STEP, C)
    nch, nb = T // C, S // T
    ng = H // HB
    fused = slab is not None

    def body(q_ref, fz_ref, v_ref, g_ref, lb_ref, gn_ref, *rest):
        if fused:
            s_ref, y_ref, o_ref, sts_ref, out_ref, st, send_sems, recv_sems = rest
            finish = _gather_behind(s_ref, out_ref, send_sems, recv_sems,
                                    pl.program_id(0) * nb + pl.program_id(1), ng * nb)
        else:
            y_ref, o_ref, sts_ref, st = rest

        @pl.when(pl.program_id(1) == 0)
        def _():
            st[...] = jnp.zeros_like(st)

        lb_all, _ = _hg_lower_bound(lb_ref[...])
        gnv = gn_ref[...]
        ri = lax.broadcasted_iota(jnp.int32, (C, C), 0)
        ci_ = lax.broadcasted_iota(jnp.int32, (C, C), 1)
        low = ri >= ci_
        tri = jnp.where(low, 1.0, 0.0).astype(BF)
        rows_w = lax.broadcasted_iota(jnp.int32, (C, W), 0)

        def chunk(ci, carry):
            sl = pl.ds(pl.multiple_of(ci * C, C), C)
            heads = [slice(hh * LANES, (hh + 1) * LANES) for hh in range(HB)]
            q, _, _, _, k, logf = _hg_chunk_common(q_ref[sl, :], fz_ref[sl, :], lb_all)
            vv, gg = v_ref[sl, :], g_ref[sl, :]
            G = _tri_dot(tri, logf)
            Gm = _row_of(G, rows_w, C // 2 - 1)
            Gl = _row_of(G, rows_w, C - 1)
            qt, kt = q * jnp.exp(G - Gm), k * jnp.exp(Gm - G)
            qe, kd, eGl = q * jnp.exp(G), k * jnp.exp(Gl - G), jnp.exp(Gl)
            A = [jnp.where(low, _dg1(qt[:, ls], kt[:, ls], NT), 0.0) for ls in heads]
            Sv = [st[hh] for hh in range(HB)]
            for hh in range(HB):
                sts_ref[hh, ci] = Sv[hh]
            o = [_dg1(A[hh], vv[:, ls], NN) + _dg1(qe[:, ls], Sv[hh], NT) for hh, ls in enumerate(heads)]
            for hh, ls in enumerate(heads):
                st[hh] = Sv[hh] * eGl[:, ls] + _dg1(vv[:, ls], kd[:, ls], TN)
            gate = gg * _sigmoid(gg)
            for hh, ls in enumerate(heads):
                r = lax.rsqrt(jnp.mean(o[hh] * o[hh], axis=-1, keepdims=True) + EPS)
                y_ref[sl, ls] = ((o[hh] * r * gnv) * gate[:, ls]).astype(BF)
                o_ref[sl, ls] = o[hh]
            return carry

        lax.fori_loop(0, nch, chunk, 0)

        if fused:
            finish()

    def part(p):
        return pl.BlockSpec((T, W), lambda h, n: (n, p * ng + h))

    blk = pl.BlockSpec((T, W), lambda h, n: (n, h))
    in_specs = [part(0), part(1), part(2), part(3),
                pl.BlockSpec((3, W), lambda h, n: (0, h)), pl.BlockSpec((1, LANES), lambda h, n: (0, 0))]
    out_specs = [blk, blk, pl.BlockSpec((HB, nch, LANES, LANES), lambda h, n: (h, n, 0, 0))]
    out_shape = [jax.ShapeDtypeStruct((S, D), BF), jax.ShapeDtypeStruct((S, D), F32),
                 jax.ShapeDtypeStruct((H, S // C, LANES, LANES), F32)]
    scratch = [pltpu.VMEM((HB, LANES, LANES), F32)]
    args = [proj, proj, proj, proj, hg_lb, gn]
    if fused:
        in_specs.append(HBM)
        out_specs.append(HBM)
        out_shape.append(jax.ShapeDtypeStruct((N_CHIPS,) + slab.shape, slab.dtype))
        scratch += [pltpu.SemaphoreType.DMA((6,)), pltpu.SemaphoreType.DMA((6,))]
        args.append(slab)
    return pl.pallas_call(
        body, name=name, grid=(ng, nb), in_specs=in_specs, out_specs=out_specs, out_shape=out_shape,
        scratch_shapes=scratch, compiler_params=_cp("arbitrary", "arbitrary"),
    )(*args)


def _hg_bwd(proj, hg_lb, gn, o_all, states, dy, part=None, *, name):
    S = proj.shape[0]
    D = proj.shape[1] // 4
    H = D // LANES
    HB = min(HG_HEADS_PER_STEP, H)
    W = HB * LANES
    C = HG_CHUNK
    T = _pick(S, HG_TOKENS_PER_STEP, C)
    nch, nb = T // C, S // T
    ng = H // HB
    fused = part is not None

    def body(q_ref, fz_ref, v_ref, g_ref, lb_ref, gn_ref, o_ref, sts_ref, dy_ref, *rest):
        if fused:
            p_ref, dp_ref, dlb_ref, dgn_ref, recv_ref, dst, dlb_acc, send_sems, recv_sems = rest
            copies = _scatter_copies(p_ref, recv_ref, send_sems, recv_sems)

            @pl.when((pl.program_id(0) == 0) & (pl.program_id(1) == 0))
            def _():
                for cp in copies:
                    cp.start()
        else:
            dp_ref, dlb_ref, dgn_ref, dst, dlb_acc = rest
        n = pl.program_id(1)

        @pl.when(n == 0)
        def _():
            dst[...] = jnp.zeros_like(dst)
            dlb_acc[...] = jnp.zeros_like(dlb_acc)
            dgn_ref[...] = jnp.zeros_like(dgn_ref)

        lb_all, p3 = _hg_lower_bound(lb_ref[...])
        gnv = gn_ref[...]
        ri = lax.broadcasted_iota(jnp.int32, (C, C), 0)
        ci_ = lax.broadcasted_iota(jnp.int32, (C, C), 1)
        low = ri >= ci_
        tri = jnp.where(low, 1.0, 0.0).astype(BF)
        triu = jnp.where(ri <= ci_, 1.0, 0.0).astype(BF)
        rows_w = lax.broadcasted_iota(jnp.int32, (C, W), 0)
        gnw = jnp.tile(gnv, (1, HB))

        def chunk(cj, carry):
            ci = nch - 1 - cj
            sl = pl.ds(pl.multiple_of(ci * C, C), C)
            heads = list(enumerate(slice(hh * LANES, (hh + 1) * LANES) for hh in range(HB)))
            wide = lambda parts: jnp.concatenate(parts, axis=1)
            qr, vv, gg = q_ref[sl, :], v_ref[sl, :], g_ref[sl, :]
            q, sq, sig, f, k, logf = _hg_chunk_common(qr, fz_ref[sl, :], lb_all)
            G = _tri_dot(tri, logf)
            Gm = _row_of(G, rows_w, C // 2 - 1)
            Gl = _row_of(G, rows_w, C - 1)
            eG, e_qm, e_km, e_lk, eGl = jnp.exp(G), jnp.exp(G - Gm), jnp.exp(Gm - G), jnp.exp(Gl - G), jnp.exp(Gl)
            qt, kt, kdec, qe = q * e_qm, k * e_km, k * e_lk, q * eG
            sg = _sigmoid(gg)
            d_onw = dy_ref[sl, :] * (gg * sg)
            u = d_onw * gnw
            o = o_ref[sl, :]
            on, do = [], []
            for hh, ls in heads:
                r = lax.rsqrt(jnp.mean(o[:, ls] * o[:, ls], axis=-1, keepdims=True) + EPS)
                on.append(o[:, ls] * r)
                dgn_ref[hh] += jnp.sum(d_onw[:, ls] * on[hh], axis=0, keepdims=True)
                do.append(r * (u[:, ls] - on[hh] * jnp.mean(u[:, ls] * on[hh], axis=-1, keepdims=True)))
            dgg = dy_ref[sl, :] * (wide(on) * gnw) * (sg * (1.0 + gg * (1.0 - sg)))
            Sv = [sts_ref[hh, ci] for hh, _ in heads]
            dSv = [dst[hh] for hh, _ in heads]
            A = [jnp.where(low, _dg1(qt[:, ls], kt[:, ls], NT), 0.0) for _, ls in heads]
            dA = [jnp.where(low, _dg3(do[hh], vv[:, ls], NT), 0.0) for hh, ls in heads]
            dv = wide([_dg1(A[hh], do[hh], TN) + _dg1(kdec[:, ls], dSv[hh], NT) for hh, ls in heads])
            dq = wide([_dg3(dA[hh], kt[:, ls], NN) for hh, ls in heads]) * e_qm \
                + eG * wide([_dg3(do[hh], Sv[hh], NN) for hh, _ in heads])
            dk = wide([_dg3(dA[hh], qt[:, ls], TN) for hh, ls in heads]) * e_km \
                + e_lk * wide([_dg3(vv[:, ls], dSv[hh], NN) for hh, ls in heads])
            s_end = [Sv[hh] * eGl[:, ls] + _dg3(vv[:, ls], kdec[:, ls], TN) for hh, ls in heads]
            dgl = wide([jnp.sum(dSv[hh] * s_end[hh], axis=0, keepdims=True) for hh, _ in heads])
            for hh, ls in heads:
                dst[hh] = dSv[hh] * eGl[:, ls] + _dg1(do[hh], qe[:, ls], TN)
            dG = q * dq - k * dk + jnp.where(rows_w == C - 1, dgl, 0.0)
            dlogf = _tri_dot(triu, dG) - f * dk
            dlf_f = dlogf / f
            dlb_acc[...] += jnp.sum(dlf_f * (1.0 - sig), axis=0, keepdims=True)
            dp_ref[0, sl, :] = (dq * (sq * (1.0 + qr * (1.0 - sq)))).astype(BF)
            dp_ref[1, sl, :] = (dlf_f * (1.0 - lb_all) * sig * (1.0 - sig)).astype(BF)
            dp_ref[2, sl, :] = dv.astype(BF)
            dp_ref[3, sl, :] = dgg.astype(BF)
            return carry

        lax.fori_loop(0, nch, chunk, 0)
        sel = jnp.where(lax.broadcasted_iota(jnp.int32, (3, W), 0) == 0, 1.0, 0.0)
        dlb_ref[...] = lb_all * (sel - p3) * dlb_acc[...]

        if fused:
            @pl.when((pl.program_id(0) == ng - 1) & (n == nb - 1))
            def _():
                for cp in copies:
                    cp.wait()

    def col(p):
        return pl.BlockSpec((T, W), lambda h, n: (nb - 1 - n, p * ng + h))

    blk = pl.BlockSpec((T, W), lambda h, n: (nb - 1 - n, h))
    in_specs = [col(0), col(1), col(2), col(3),
                pl.BlockSpec((3, W), lambda h, n: (0, h)), pl.BlockSpec((1, LANES), lambda h, n: (0, 0)),
                blk, pl.BlockSpec((HB, nch, LANES, LANES), lambda h, n: (h, nb - 1 - n, 0, 0)), blk]
    out_specs = [pl.BlockSpec((4, T, W), lambda h, n: (0, nb - 1 - n, h)),
                 pl.BlockSpec((3, W), lambda h, n: (0, h)),
                 pl.BlockSpec((HB, 1, LANES), lambda h, n: (h, 0, 0))]
    out_shape = [jax.ShapeDtypeStruct((4, S, D), BF), jax.ShapeDtypeStruct((3, D), F32),
                 jax.ShapeDtypeStruct((H, 1, LANES), F32)]
    scratch = [pltpu.VMEM((HB, LANES, LANES), F32), pltpu.VMEM((1, W), F32)]
    args = [proj, proj, proj, proj, hg_lb, gn, o_all, states, dy]
    if fused:
        in_specs.append(HBM)
        out_specs.append(HBM)
        out_shape.append(jax.ShapeDtypeStruct((3,) + part.shape[1:], part.dtype))
        scratch += [pltpu.SemaphoreType.DMA((3,)), pltpu.SemaphoreType.DMA((3,))]
        args.append(part)
    return pl.pallas_call(
        body, name=name, grid=(ng, nb), in_specs=in_specs, out_specs=out_specs, out_shape=out_shape,
        scratch_shapes=scratch, compiler_params=_cp("arbitrary", "arbitrary"),
    )(*args)


def _log_sigmoid(u):
    return jnp.minimum(u, 0.0) - jnp.log(1.0 + jnp.exp(-jnp.abs(u)))


def _lane_put(base, lane, first, pieces):
    for n, p in enumerate(pieces):
        base = jnp.where(lane == first + n, p, base)
    return base


def _fox_cumsum(proj, bf_pad, *, name):
    S = proj.shape[0]
    D = proj.shape[1] // 5
    T = _pick(S, 256, 8)

    def body(fz_ref, b_ref, f_ref, carry):
        @pl.when(pl.program_id(0) == 0)
        def _():
            carry[...] = jnp.zeros_like(carry)

        logf = _log_sigmoid(fz_ref[...] + b_ref[...])
        tri = jnp.where(lax.broadcasted_iota(jnp.int32, (T, T), 0) >= lax.broadcasted_iota(jnp.int32, (T, T), 1),
                        1.0, 0.0).astype(BF)
        fv = _tri_dot(tri, logf) + carry[...]
        f_ref[...] = fv
        carry[...] = _row_of(fv, lax.broadcasted_iota(jnp.int32, (T, LANES), 0), T - 1)

    return pl.pallas_call(
        body, name=name, grid=(S // T,),
        in_specs=[pl.BlockSpec((T, LANES), lambda i: (i, 4 * D // LANES)), pl.BlockSpec((1, LANES), lambda i: (0, 0))],
        out_specs=pl.BlockSpec((T, LANES), lambda i: (i, 0)),
        out_shape=jax.ShapeDtypeStruct((S, LANES), F32),
        scratch_shapes=[pltpu.VMEM((1, LANES), F32)],
        compiler_params=_cp("arbitrary"),
    )(proj, bf_pad)


def _pair_stats(sq, lo):
    del lo
    a = lax.broadcasted_iota(jnp.int32, (LANES, LANES), 0) < FOX_DH
    b = lax.broadcasted_iota(jnp.int32, (LANES, LANES), 1) < FOX_DH
    avg = jnp.where(a == b, 1.0 / FOX_DH, 0.0).astype(BF)
    hi, mid, low = _split3(sq)
    return _dot(hi, avg) + _dot(mid, avg) + _dot(low, avg)


def _fox_prep(proj, fcum, qw2, kw2, *, name):
    S = proj.shape[0]
    D = proj.shape[1] // 5
    HP = D // LANES
    T = _pick(S, FOX_ROWS_PER_STEP, 16)

    def body(q_ref, k_ref, v_ref, f_ref, qw_ref, kw_ref, qa_ref, ka_ref, va_ref, vt_ref):
        hp = pl.program_id(1)
        lane = lax.broadcasted_iota(jnp.int32, (T, LANES), 1)
        lo = lane < FOX_DH
        qv, kv, vv, fv = q_ref[...], k_ref[...], v_ref[...], f_ref[...]
        qn = qv * lax.rsqrt(_pair_stats(qv * qv, lo) + EPS) * qw_ref[...] * (0.125 * LOG2E)
        kn = kv * lax.rsqrt(_pair_stats(kv * kv, lo) + EPS) * kw_ref[...]
        ones_q = jnp.where((lane >= 67) & (lane <= 69), 1.0, 0.0)
        ones_k = jnp.where(((lane >= 64) & (lane <= 66)) | ((lane >= 70) & (lane <= 72)), 1.0, 0.0)
        ones_v = jnp.where((lane >= 64) & (lane <= 66), 1.0, 0.0)
        for hh in range(2):
            fh = jnp.sum(jnp.where(lane == 2 * hp + hh, fv, 0.0), axis=-1, keepdims=True) * LOG2E
            pieces = [p.astype(F32) for p in _split3(fh)]

            def half(x):
                return jnp.where(lo, x if hh == 0 else pltpu.roll(x, FOX_DH, 1), 0.0)

            qa_ref[hh] = _lane_put(half(qn) + ones_q, lane, 64, pieces).astype(BF)
            ka_ref[hh] = _lane_put(half(kn) + ones_k, lane, 67, [-p for p in pieces]).astype(BF)
            va = half(vv) + ones_v
            va_ref[hh] = va.astype(BF)
            vt_ref[hh] = va.T.astype(BF)

    def part(p):
        return pl.BlockSpec((T, LANES), lambda i, hp: (i, p * HP + hp))

    vec = pl.BlockSpec((1, LANES), lambda i, hp: (0, 0))
    aug = pl.BlockSpec((2, T, LANES), lambda i, hp: (hp, i, 0))
    return pl.pallas_call(
        body, name=name, grid=(S // T, HP),
        in_specs=[part(0), part(1), part(2), pl.BlockSpec((T, LANES), lambda i, hp: (i, 0)), vec, vec],
        out_specs=[aug, aug, aug, pl.BlockSpec((2, LANES, T), lambda i, hp: (hp, 0, i))],
        out_shape=[jax.ShapeDtypeStruct((2 * HP, S, LANES), BF)] * 3 + [jax.ShapeDtypeStruct((2 * HP, LANES, S), BF)],
        compiler_params=_cp("parallel", "arbitrary"),
    )(proj, proj, proj, fcum, qw2, kw2)


def _fox_block(S):
    return _pick(S, 256, 16)


def _fox_skip_bounds(fcum, qn_w, kn_w, nheads):
    S = fcum.shape[0]
    B = _fox_block(S)
    qk = 8.0 * LOG2E * 1.02 * jnp.max(jnp.abs(qn_w)) * jnp.max(jnp.abs(kn_w))
    thresh = -(2.0 * qk + 160.0)
    f2 = fcum[:, :nheads] * LOG2E
    first, last = f2[0::B], f2[B - 1::B]
    nb = S // B
    blk = jnp.arange(nb)
    dead = (first[0::2, None, :] - last[None, :, :]) < thresh
    jmin = jnp.sum(dead & (blk[None, :, None] < 2 * jnp.arange(nb // 2)[:, None, None]), axis=1)
    live = (first[:, None, :] - last[None, :, :]) >= thresh
    imax = blk[:, None] + jnp.sum(live & (blk[:, None, None] > blk[None, :, None]), axis=0)
    return jmin.T.astype(jnp.int32), imax.T.astype(jnp.int32)


def _fox_fwd(jmin, qa, ka, vat, proj, *, name):
    H, S, _ = qa.shape
    HP = H // 2
    D = HP * LANES
    B = _fox_block(S)
    BQ = 2 * B
    nq = S // BQ

    def body(jmin_ref, q_ref, k_ref, vt_ref, g_ref, y_ref, o_ref, q2_ref):
        hp, i = pl.program_id(0), pl.program_id(1)
        lane = lax.broadcasted_iota(jnp.int32, (BQ, LANES), 1)
        lo = lane < FOX_DH
        in_stat = (lane >= 70) & (lane <= 75)
        causal = lax.broadcasted_iota(jnp.int32, (BQ, BQ), 0) <= lax.broadcasted_iota(jnp.int32, (BQ, BQ), 1)
        row = lax.broadcasted_iota(jnp.int32, (LANES, BQ), 0)
        m0, acc0 = jnp.full((1, BQ), -jnp.inf, F32), jnp.zeros((LANES, BQ), F32)
        outs = []
        for hh in range(2):
            qb = q_ref[hh]

            def block(j, carry, masked=False):
                m, acc = carry
                sl = pl.ds(pl.multiple_of(j * BQ, BQ), BQ)
                st = _dg(k_ref[hh, sl, :], qb, NT)
                if masked:
                    st = jnp.where(causal, st, -jnp.inf)
                m_new = jnp.maximum(m, jnp.ceil(jnp.max(st, axis=0, keepdims=True)))
                p = jnp.exp2(st - m_new).astype(BF)
                return m_new, acc * jnp.exp2(m - m_new) + _dot(vt_ref[hh, :, sl], p)

            carry = lax.fori_loop(jmin_ref[2 * hp + hh, i] // 2, i, block, (m0, acc0))
            m, acc = block(i, carry, masked=True)
            linv = 1.0 / jnp.sum(jnp.where(row == FOX_DH, acc, 0.0), axis=0, keepdims=True)
            tile = acc * linv
            for n, piece in enumerate(_split3(m) + _split3(linv)):
                tile = jnp.where(row == 70 + n, piece.astype(F32), tile)
            tile = tile.T
            outs.append(tile)
            q2_ref[hh] = jnp.where(in_stat, jnp.where(lane <= 72, -tile, tile), qb.astype(F32)).astype(BF)
        o = jnp.where(lo, outs[0], pltpu.roll(outs[1], FOX_DH, 1))
        o_ref[...] = o
        y_ref[...] = (o * _sigmoid(g_ref[...])).astype(BF)

    blk = pl.BlockSpec((BQ, LANES), lambda hp, i, jm: (i, hp))
    qblk = pl.BlockSpec((2, BQ, LANES), lambda hp, i, jm: (hp, i, 0))
    full = pl.BlockSpec((2, S, LANES), lambda hp, i, jm: (hp, 0, 0))
    full_t = pl.BlockSpec((2, LANES, S), lambda hp, i, jm: (hp, 0, 0))
    return pl.pallas_call(
        body, name=name,
        grid_spec=pltpu.PrefetchScalarGridSpec(
            num_scalar_prefetch=1, grid=(HP, nq),
            in_specs=[qblk, full, full_t, pl.BlockSpec((BQ, LANES), lambda hp, i, jm: (i, 3 * HP + hp))],
            out_specs=[blk, blk, qblk]),
        out_shape=[jax.ShapeDtypeStruct((S, D), BF), jax.ShapeDtypeStruct((S, D), F32),
                   jax.ShapeDtypeStruct((H, S, LANES), BF)],
        compiler_params=_cp("parallel", "arbitrary"),
    )(jmin, qa, ka, vat, proj)


def _fox_bwd_prep(dy, o, proj, q2, *, name):
    S, D = dy.shape
    HP = D // LANES
    T = _pick(S, FOX_ROWS_PER_STEP, 16)

    def body(dy_ref, o_ref, g_ref, q2_ref, da_ref):
        lane = lax.broadcasted_iota(jnp.int32, (T, LANES), 1)
        lo = lane < FOX_DH
        in_linv = (lane >= 73) & (lane <= 75)
        linv = [jnp.sum(jnp.where(in_linv, q2_ref[hh].astype(F32), 0.0), axis=-1, keepdims=True) for hh in range(2)]
        u = (dy_ref[...] * _sigmoid(g_ref[...]) * jnp.where(lo, linv[0], linv[1])).astype(BF).astype(F32)
        prod = u * o_ref[...]
        d_lo = jnp.sum(jnp.where(lo, prod, 0.0), axis=-1, keepdims=True)
        d_hi = jnp.sum(jnp.where(lo, 0.0, prod), axis=-1, keepdims=True)
        for hh, delta in enumerate((d_lo, d_hi)):
            base = jnp.where(lo, u if hh == 0 else pltpu.roll(u, FOX_DH, 1), 0.0)
            da_ref[hh] = _lane_put(base, lane, 64, [-(p.astype(F32)) for p in _split3(delta)]).astype(BF)

    blk = pl.BlockSpec((T, LANES), lambda i, hp: (i, hp))
    aug = pl.BlockSpec((2, T, LANES), lambda i, hp: (hp, i, 0))
    return pl.pallas_call(
        body, name=name, grid=(S // T, HP),
        in_specs=[blk, blk, pl.BlockSpec((T, LANES), lambda i, hp: (i, 3 * HP + hp)), aug],
        out_specs=aug,
        out_shape=jax.ShapeDtypeStruct((2 * HP, S, LANES), BF),
        compiler_params=_cp("parallel", "arbitrary"),
    )(dy, o, proj, q2)


def _fox_bwd(imax, q2, ka, va, doa, *, name):
    H, S, _ = q2.shape
    B = _fox_block(S)
    nb = S // B

    def body(imax_ref, q_ref, do_ref, k_ref, v_ref, dq_ref, dk_ref, dv_ref, cs_ref):
        j = pl.program_id(1)
        end = imax_ref[pl.program_id(0), j] + 1

        @pl.when(j == 0)
        def _():
            dq_ref[...] = jnp.zeros_like(dq_ref)

        kb, vb = k_ref[...], v_ref[...]

        def step(i, carry, nblk=1):
            dk_acc, dv_acc, cs_acc = carry
            rows = nblk * B
            sl = pl.ds(pl.multiple_of(i * B, B), rows)
            qb, dob = q_ref[sl, :], do_ref[sl, :]
            s = _dg(qb, kb, NT)
            ahead = lax.broadcasted_iota(jnp.int32, (rows, B), 0) - lax.broadcasted_iota(jnp.int32, (rows, B), 1)
            pb = jnp.exp2(jnp.where(ahead >= (j - i) * B, s, -jnp.inf)).astype(BF)
            ds = pb.astype(F32) * _dg(dob, vb, NT)
            dsb = ds.astype(BF)
            cs_acc = cs_acc + jnp.sum(ds.reshape(rows // 8, 8, B), axis=0)
            dv_acc = dv_acc + _dg(pb, dob, TN)
            dk_acc = dk_acc + _dg(dsb, qb, TN)
            dq_ref[sl, :] += _dot(dsb, kb)
            return dk_acc, dv_acc, cs_acc

        zero = jnp.zeros((B, LANES), F32)
        carry = (zero, zero, jnp.zeros((8, B), F32))
        pos = j
        for U in FOX_BWD_TILES:
            n = (end - pos) // U
            carry = lax.fori_loop(0, n, lambda ii, c, pos=pos, U=U: step(pos + U * ii, c, nblk=U), carry)
            pos = pos + U * n
        dk_acc, dv_acc, cs_acc = carry
        dk_ref[...] = dk_acc
        dv_ref[...] = dv_acc
        cs_ref[...] = jnp.sum(cs_acc, axis=0, keepdims=True)

    full = pl.BlockSpec((None, S, LANES), lambda h, j, im: (h, 0, 0))
    blk = pl.BlockSpec((None, B, LANES), lambda h, j, im: (h, j, 0))
    return pl.pallas_call(
        body, name=name,
        grid_spec=pltpu.PrefetchScalarGridSpec(
            num_scalar_prefetch=1, grid=(H, nb),
            in_specs=[full, full, blk, blk],
            out_specs=[full, blk, blk, pl.BlockSpec((None, 1, B), lambda h, j, im: (h, 0, j))]),
        out_shape=[jax.ShapeDtypeStruct((H, S, LANES), F32)] * 3 + [jax.ShapeDtypeStruct((H, 1, S), F32)],
        compiler_params=_cp("parallel", "arbitrary"),
    )(imax, q2, doa, ka, va)


def _fox_bwd_post(dqa, dka, dva, proj, dy, o, qw2, kw2, *, name):
    S, D = dy.shape
    HP = D // LANES
    T = _pick(S, FOX_ROWS_PER_STEP, 16)

    def body(dq_ref, dk_ref, dv_ref, q_ref, k_ref, g_ref, dy_ref, o_ref, qw_ref, kw_ref, dp_ref, dqw_ref, dkw_ref):
        @pl.when((pl.program_id(0) == 0) & (pl.program_id(1) == 0))
        def _():
            dqw_ref[...] = jnp.zeros_like(dqw_ref)
            dkw_ref[...] = jnp.zeros_like(dkw_ref)

        lane = lax.broadcasted_iota(jnp.int32, (T, LANES), 1)
        lo = lane < FOX_DH

        def pair(ref):
            return jnp.where(lo, ref[0], pltpu.roll(ref[1], FOX_DH, 1))

        def norm_bwd(xv, w, dyn, dw_ref):
            r = lax.rsqrt(_pair_stats(xv * xv, lo) + EPS)
            xr = xv * r
            dw_ref[...] += jnp.sum(dyn * xr, axis=0, keepdims=True)
            u = dyn * w
            return r * (u - xr * _pair_stats(u * xr, lo))

        dp_ref[0] = norm_bwd(q_ref[...], qw_ref[...], pair(dq_ref) * 0.125, dqw_ref).astype(BF)
        dp_ref[1] = norm_bwd(k_ref[...], kw_ref[...], pair(dk_ref) * (1.0 / LOG2E), dkw_ref).astype(BF)
        dp_ref[2] = pair(dv_ref).astype(BF)
        sg = _sigmoid(g_ref[...])
        dp_ref[3] = (dy_ref[...] * o_ref[...] * sg * (1.0 - sg)).astype(BF)

    def part(p):
        return pl.BlockSpec((T, LANES), lambda i, hp: (i, p * HP + hp))

    aug = pl.BlockSpec((2, T, LANES), lambda i, hp: (hp, i, 0))
    blk = pl.BlockSpec((T, LANES), lambda i, hp: (i, hp))
    vec = pl.BlockSpec((1, LANES), lambda i, hp: (0, 0))
    return pl.pallas_call(
        body, name=name, grid=(S // T, HP),
        in_specs=[aug, aug, aug, part(0), part(1), part(3), blk, blk, vec, vec],
        out_specs=[pl.BlockSpec((4, T, LANES), lambda i, hp: (0, i, hp)), vec, vec],
        out_shape=[jax.ShapeDtypeStruct((5, S, D), BF), jax.ShapeDtypeStruct((1, LANES), F32),
                   jax.ShapeDtypeStruct((1, LANES), F32)],
        compiler_params=_cp("arbitrary", "arbitrary"),
    )(dqa, dka, dva, proj, proj, proj, dy, o, qw2, kw2)


def _fox_dfz(colsum, nheads, proj, bf_pad, dproj, *, name):
    S = colsum.shape[0]
    H = nheads
    D = dproj.shape[2]
    T = _pick(S, 256, 16)
    nb = S // T

    def body(cs_ref, fz_ref, b_ref, _, dp_ref, db_ref, carry):
        @pl.when(pl.program_id(0) == 0)
        def _():
            carry[...] = jnp.zeros_like(carry)
            db_ref[...] = jnp.zeros_like(db_ref)

        lane = lax.broadcasted_iota(jnp.int32, (T, LANES), 1)
        df = -cs_ref[...]
        triu = jnp.where(lax.broadcasted_iota(jnp.int32, (T, T), 0) <= lax.broadcasted_iota(jnp.int32, (T, T), 1),
                         1.0, 0.0).astype(BF)
        dlogf = _tri_dot(triu, df) + carry[...]
        carry[...] = _row_of(dlogf, lax.broadcasted_iota(jnp.int32, (T, LANES), 0), 0)
        dfz = jnp.where(lane < H, dlogf * _sigmoid(-(fz_ref[...] + b_ref[...])), 0.0)
        db_ref[...] += jnp.sum(dfz, axis=0, keepdims=True)
        dp_ref[...] = jnp.zeros_like(dp_ref)
        dp_ref[:, 0:LANES] = dfz.astype(BF)

    return pl.pallas_call(
        body, name=name, grid=(nb,),
        in_specs=[pl.BlockSpec((T, LANES), lambda i: (nb - 1 - i, 0)),
                  pl.BlockSpec((T, LANES), lambda i: (nb - 1 - i, 4 * D // LANES)),
                  pl.BlockSpec((1, LANES), lambda i: (0, 0)),
                  pl.BlockSpec(memory_space=pl.ANY)],
        out_specs=[pl.BlockSpec((None, T, D), lambda i: (4, nb - 1 - i, 0)), pl.BlockSpec((1, LANES), lambda i: (0, 0))],
        out_shape=[jax.ShapeDtypeStruct(dproj.shape, BF), jax.ShapeDtypeStruct((1, LANES), F32)],
        scratch_shapes=[pltpu.VMEM((1, LANES), F32)],
        input_output_aliases={3: 0},
        compiler_params=_cp("arbitrary"),
    )(colsum, proj, bf_pad, dproj)


def _mod_fwd(c16, w, b, *, name):
    L, D, N = w.shape
    tn = _pick(N, 512)

    def body(c_ref, w_ref, b_ref, o_ref):
        cv = c_ref[...]
        ca = (cv * _sigmoid(cv)).astype(BF)
        o_ref[...] = _dot(ca, w_ref[...].astype(BF)) + b_ref[...]

    return pl.pallas_call(
        body, name=name, grid=(L, N // tn),
        in_specs=[pl.BlockSpec((16, D), lambda l, j: (0, 0)), pl.BlockSpec((None, D, tn), lambda l, j: (l, 0, j)),
                  pl.BlockSpec((None, 1, tn), lambda l, j: (l, 0, j))],
        out_specs=pl.BlockSpec((None, 16, tn), lambda l, j: (l, 0, j)),
        out_shape=jax.ShapeDtypeStruct((L, 16, N), F32),
        compiler_params=_cp("parallel", "arbitrary"),
    )(c16, w, b)


def _mod_bwd(c16, dmod, *, name):
    L, _, N = dmod.shape
    D = c16.shape[1]
    tn = _pick(N, 512)

    def body(c_ref, d_ref, o_ref):
        cv = c_ref[...]
        ca = (cv * _sigmoid(cv)).astype(BF)
        o_ref[...] = _dg(ca, d_ref[...].astype(BF), TN)

    return pl.pallas_call(
        body, name=name, grid=(L, N // tn),
        in_specs=[pl.BlockSpec((16, D), lambda l, j: (0, 0)), pl.BlockSpec((None, 16, tn), lambda l, j: (l, 0, j))],
        out_specs=pl.BlockSpec((None, D, tn), lambda l, j: (l, 0, j)),
        out_shape=jax.ShapeDtypeStruct((L, D, N), F32),
        compiler_params=_cp("parallel", "arbitrary"),
    )(c16, dmod)


def _adamw_math(w, g, m, v):
    m = ADAM_B1 * m + (1.0 - ADAM_B1) * g
    v = ADAM_B2 * v + (1.0 - ADAM_B2) * (g * g)
    m_hat = m / (1.0 - ADAM_B1 ** ADAM_STEP)
    v_hat = v / (1.0 - ADAM_B2 ** ADAM_STEP)
    return -ADAM_LR * (m_hat / (jnp.sqrt(v_hat) + ADAM_EPS) + ADAM_WD * w), m, v


def _adamw(w, g, m, v, *, g_at=None, name):
    R, C = w.shape
    row0 = 0 if g_at is None else g_at[1]
    tr = min(math.gcd(row0, 256) if row0 else 256, -(-R // 8) * 8)
    g0 = row0 // tr
    if g_at is None:
        g_spec = pl.BlockSpec((tr, C), lambda i: (i, 0))
    else:
        g_spec = pl.BlockSpec((None, tr, C), lambda i: (g_at[0], g0 + i, 0))

    def body(w_ref, g_ref, m_ref, v_ref, d_ref, mo_ref, vo_ref):
        d, mn, vn = _adamw_math(w_ref[...], g_ref[...], m_ref[...], v_ref[...])
        d_ref[...] = d
        mo_ref[...] = mn
        vo_ref[...] = vn

    blk = pl.BlockSpec((tr, C), lambda i: (i, 0))
    return pl.pallas_call(
        body, name=name, grid=(pl.cdiv(R, tr),),
        in_specs=[blk, g_spec, blk, blk],
        out_specs=[blk, blk, blk],
        out_shape=[jax.ShapeDtypeStruct((R, C), F32)] * 3,
        compiler_params=_cp("parallel"),
    )(w, g, m, v)


def _sum_parts(parts, *, name):
    P, R, C = parts.shape

    def body(p_ref, o_ref):
        acc = p_ref[0]
        for p in range(1, P):
            acc = acc + p_ref[p]
        o_ref[...] = acc

    return pl.pallas_call(
        body, name=name, grid=(1,),
        in_specs=[pl.BlockSpec((P, R, C), lambda i: (0, 0, 0))],
        out_specs=pl.BlockSpec((R, C), lambda i: (0, 0)),
        out_shape=jax.ShapeDtypeStruct((R, C), F32),
        compiler_params=_cp("arbitrary"),
    )(parts)


def _add_halves(g4, recv, c_idx, *, name):
    _, _, Rh, C = g4.shape
    tr = min(256, Rh)

    def body(c_ref, a_ref, b_ref, o_ref):
        o_ref[...] = (a_ref[...] + b_ref[...].astype(F32)).astype(BF)

    return pl.pallas_call(
        body, name=name,
        grid_spec=pltpu.PrefetchScalarGridSpec(
            num_scalar_prefetch=1, grid=(4, pl.cdiv(Rh, tr)),
            in_specs=[pl.BlockSpec((None, None, tr, C), lambda j, r, c: (j, c[0], r, 0)),
                      pl.BlockSpec((None, tr, C), lambda j, r, c: (j, r, 0))],
            out_specs=pl.BlockSpec((None, tr, C), lambda j, r, c: (j, r, 0))),
        out_shape=jax.ShapeDtypeStruct((4, Rh, C), BF),
        compiler_params=_cp("parallel", "arbitrary"),
    )(c_idx, g4, recv)


def _add_four(g4, from_sibling, from_chips, pos, *, name):
    _, _, Rh, C = g4.shape
    tr = min(256, Rh)

    def body(p_ref, a_ref, s_ref, b_ref, o_ref):
        own = a_ref[...] + s_ref[...].astype(F32)
        o_ref[...] = ((own + b_ref[0].astype(F32)) + b_ref[1].astype(F32)) + b_ref[2].astype(F32)

    return pl.pallas_call(
        body, name=name,
        grid_spec=pltpu.PrefetchScalarGridSpec(
            num_scalar_prefetch=1, grid=(pl.cdiv(Rh, tr),),
            in_specs=[pl.BlockSpec((None, None, tr, C), lambda r, p: (p[0], p[1], r, 0)),
                      pl.BlockSpec((None, tr, C), lambda r, p: (p[0], r, 0)),
                      pl.BlockSpec((3, tr, C), lambda r, p: (0, r, 0))],
            out_specs=pl.BlockSpec((None, tr, C), lambda r, p: (p[1], r, 0))),
        out_shape=jax.ShapeDtypeStruct((2, Rh, C), F32),
        compiler_params=_cp("arbitrary"),
    )(pos, g4, from_sibling, from_chips)


HBM = pl.BlockSpec(memory_space=pltpu.HBM)


def _mesh_pos():
    return lax.axis_index("x"), lax.axis_index("y"), lax.axis_index("c")


def _other_chips(x, y):
    return [(1 - x, y), (x, 1 - y), (1 - x, 1 - y)]


def _allgather_small(xs, *, name):
    m_per, n = xs.shape

    def body(x_ref, out_ref, send_sems, recv_sems, local_sem):
        x, y, c = _mesh_pos()
        me, sibling = (x, y, c), (x, y, 1 - c)
        chips = _other_chips(x, y)

        def rows(px, py, pc):
            return out_ref.at[pl.ds((4 * px + 2 * py + pc) * m_per, m_per), :]

        def copy(k, block, to, src=None):
            return pltpu.make_async_remote_copy(
                src_ref=rows(*block) if src is None else src, dst_ref=rows(*block),
                send_sem=send_sems.at[k], recv_sem=recv_sems.at[k], device_id=to, device_id_type=MESH)

        mine = pltpu.make_async_copy(x_ref, rows(*me), local_sem)
        mine.start()
        first = [copy(0, me, sibling, src=x_ref)]
        first += [copy(1 + j, me, (*chip, c), src=x_ref) for j, chip in enumerate(chips)]
        for cp in first:
            cp.start()
        passed = [copy(4 + j, (*chip, c), sibling) for j, chip in enumerate(chips)]
        for j, chip in enumerate(chips):
            copy(1 + j, (*chip, c), me).wait_recv()
            passed[j].start()
        copy(0, sibling, me).wait_recv()
        for j, chip in enumerate(chips):
            copy(4 + j, (*chip, 1 - c), me).wait_recv()
        for cp in first + passed:
            cp.wait_send()
        mine.wait()

    return pl.pallas_call(
        body, name=name,
        out_shape=jax.ShapeDtypeStruct((N_DEV * m_per, n), xs.dtype),
        in_specs=[pl.BlockSpec(memory_space=pltpu.VMEM)],
        out_specs=pl.BlockSpec(memory_space=pltpu.VMEM),
        scratch_shapes=[pltpu.SemaphoreType.DMA((7,)), pltpu.SemaphoreType.DMA((7,)), pltpu.SemaphoreType.DMA],
    )(xs)


def _chip_slab_copies(s_ref, out_ref, send_sems, recv_sems):
    R = s_ref.shape[0]
    Rh = R // 2
    x, y, c = _mesh_pos()
    me, sibling = (x, y, c), (x, y, 1 - c)
    chips = _other_chips(x, y)

    def half(px, py, pc):
        return out_ref.at[2 * px + py, pl.ds(pc * Rh, Rh), :]

    def copy(k, block, to, src=None):
        return pltpu.make_async_remote_copy(
            src_ref=half(*block) if src is None else src, dst_ref=half(*block),
            send_sem=send_sems.at[k], recv_sem=recv_sems.at[k], device_id=to, device_id_type=MESH)

    first = [copy(j, me, (*chip, c), src=s_ref.at[pl.ds(c * Rh, Rh), :]) for j, chip in enumerate(chips)]
    passed = [copy(3 + j, (*chip, c), sibling) for j, chip in enumerate(chips)]
    landed = [copy(j, (*chip, c), me) for j, chip in enumerate(chips)]
    from_sibling = [copy(3 + j, (*chip, 1 - c), me) for j, chip in enumerate(chips)]
    return first, passed, landed, from_sibling


def _gather_behind(s_ref, out_ref, send_sems, recv_sems, step, nsteps):
    first, passed, landed, from_sibling = _chip_slab_copies(s_ref, out_ref, send_sems, recv_sems)

    @pl.when(step == 0)
    def _():
        for cp in first:
            cp.start()

    @pl.when(step == (3 * nsteps) // 4)
    def _():
        for arrived, onward in zip(landed, passed):
            arrived.wait_recv()
            onward.start()

    def finish():
        @pl.when(step == nsteps - 1)
        def _():
            for cp in from_sibling:
                cp.wait_recv()
            for cp in first + passed:
                cp.wait_send()

    return finish


def _allgather_chip_slabs(slab, *, name):
    R, C = slab.shape

    def body(s_ref, out_ref, send_sems, recv_sems):
        first, passed, landed, from_sibling = _chip_slab_copies(s_ref, out_ref, send_sems, recv_sems)
        for cp in first:
            cp.start()
        for arrived, onward in zip(landed, passed):
            arrived.wait_recv()
            onward.start()
        for cp in from_sibling:
            cp.wait_recv()
        for cp in first + passed:
            cp.wait_send()

    return pl.pallas_call(
        body, name=name,
        out_shape=jax.ShapeDtypeStruct((N_CHIPS, R, C), slab.dtype),
        in_specs=[HBM], out_specs=HBM,
        scratch_shapes=[pltpu.SemaphoreType.DMA((6,)), pltpu.SemaphoreType.DMA((6,))],
    )(slab)


def _swap_halves(mine, *, name):
    def body(g_ref, out_ref, send_sems, recv_sems):
        x, y, c = _mesh_pos()
        copies = [pltpu.make_async_remote_copy(
            src_ref=g_ref.at[j], dst_ref=out_ref.at[j], send_sem=send_sems.at[j], recv_sem=recv_sems.at[j],
            device_id=(x, y, 1 - c), device_id_type=MESH) for j in range(N_CHIPS)]
        for cp in copies:
            cp.start()
        for cp in copies:
            cp.wait()

    return pl.pallas_call(
        body, name=name,
        out_shape=jax.ShapeDtypeStruct(mine.shape, mine.dtype),
        in_specs=[HBM], out_specs=HBM,
        scratch_shapes=[pltpu.SemaphoreType.DMA((N_CHIPS,)), pltpu.SemaphoreType.DMA((N_CHIPS,))],
    )(mine)


def _scatter_copies(p_ref, out_ref, send_sems, recv_sems):
    x, y, c = _mesh_pos()
    return [pltpu.make_async_remote_copy(
        src_ref=p_ref.at[2 * px + py], dst_ref=out_ref.at[j], send_sem=send_sems.at[j], recv_sem=recv_sems.at[j],
        device_id=(px, py, c), device_id_type=MESH) for j, (px, py) in enumerate(_other_chips(x, y))]


def _join_halves(buf, *, name):
    def body(b_ref, out_ref, send_sem, recv_sem):
        x, y, c = _mesh_pos()
        cp = pltpu.make_async_remote_copy(
            src_ref=b_ref.at[c], dst_ref=out_ref.at[c], send_sem=send_sem, recv_sem=recv_sem,
            device_id=(x, y, 1 - c), device_id_type=MESH)
        cp.start()
        cp.wait()

    return pl.pallas_call(
        body, name=name,
        out_shape=jax.ShapeDtypeStruct(buf.shape, buf.dtype),
        in_specs=[HBM], out_specs=HBM, input_output_aliases={0: 0},
        scratch_shapes=[pltpu.SemaphoreType.DMA, pltpu.SemaphoreType.DMA],
    )(buf)


def _pad_rows(a, mult):
    pad = (-a.shape[0]) % mult
    return a if pad == 0 else jnp.pad(a, ((0, pad),) + ((0, 0),) * (a.ndim - 1))


def _local_step(x, target, mod, wts, small, slabs=None, unpacks=None, reduce_early=None, grad_slab=None,
                reduce_late=None):
    S, D = x.shape
    HP = D // LANES
    row = lambda v: v.reshape(1, -1)
    msplit = [[row(mod[i, k * D:(k + 1) * D]) for k in range(6)] for i in range(2)]
    gw, gs = {}, {}
    dmod = [[None] * 6 for _ in range(2)]
    slab, where = grad_slab if grad_slab is not None else (None, {})

    def dw(key, a, b, name):
        nonlocal slab
        if key in where:
            slab = _matmul_tn(a, b, name=name, into=(slab,) + where[key])
        else:
            gw[key] = _matmul_tn(a, b, name=name)

    sh1, sc1, g1, sh2, sc2, g2 = msplit[0]
    n1w0, n2w0 = row(small["norm1_w"][0]), row(small["norm2_w"][0])
    slabs = slabs if slabs is not None else (None, None)
    proj0, h1_0, *gathered = _ln_matmul(x, n1w0, sc1, sh1, wts["hg_w_in"], slabs[0], relu2=False, name="hg_in_proj")
    if slabs[0] is not None:
        wts = {**wts, **unpacks[0](gathered[0])}
    gn = small["hg_gn_w"].reshape(1, LANES)
    ypre0, o0, states, *gathered = _hg_fwd(proj0, small["hg_lb"], gn, slabs[1], name="hg_fwd")
    if slabs[1] is not None:
        wts = {**wts, **unpacks[1](gathered[0])}
    x1, ymix0 = _matmul_resid(ypre0, wts["hg_w_out"], x, g1, name="hg_out_proj")
    a0, u0, h2_0 = _ln_matmul(x1, n2w0, sc2, sh2, wts["mlp_w1_0"], relu2=True, name="mlp0_up")
    x2, ymlp0 = _matmul_resid(u0, wts["mlp_w2_0"], x1, g2, name="mlp0_down")

    sh1b, sc1b, g1b, sh2b, sc2b, g2b = msplit[1]
    n1w1, n2w1 = row(small["norm1_w"][1]), row(small["norm2_w"][1])
    proj1, h1_1 = _ln_matmul(x2, n1w1, sc1b, sh1b, wts["fox_w_in"], relu2=False, name="fox_in_proj")
    nheads = 2 * HP
    bf_pad = jnp.pad(small["fox_b_f"].reshape(1, nheads), ((0, 0), (0, LANES - nheads)))
    qw2 = jnp.tile(small["fox_qn_w"].reshape(1, FOX_DH), (1, 2))
    kw2 = jnp.tile(small["fox_kn_w"].reshape(1, FOX_DH), (1, 2))
    fcum = _fox_cumsum(proj1, bf_pad, name="fox_cumsum")
    qa, ka, va, vat = _fox_prep(proj1, fcum, qw2, kw2, name="fox_prep")
    jmin, imax = _fox_skip_bounds(fcum, small["fox_qn_w"], small["fox_kn_w"], nheads)
    ypre1, o1, q2 = _fox_fwd(jmin, qa, ka, vat, proj1, name="fox_fwd")
    x3, ymix1 = _matmul_resid(ypre1, wts["fox_w_out"], x2, g1b, name="fox_out_proj")
    a1, u1, h2_1 = _ln_matmul(x3, n2w1, sc2b, sh2b, wts["mlp_w1_1"], relu2=True, name="mlp1_up")
    x4, ymlp1 = _matmul_resid(u1, wts["mlp_w2_1"], x3, g2b, name="mlp1_down")

    loss, dx4, dfw = _loss_kernel(x4, row(small["final_w"]), target, name="loss")
    gs["final_w"] = dfw.reshape(-1)

    def mlp_bwd(i, dx_out, x_in, h2, a, u, ymlp, n2w, sc2_, g2_):
        dz, dm, dg2 = _gate_matmul_nt(dx_out, g2_, ymlp, wts[f"mlp_w2_{i}"], a, name=f"mlp{i}_down_bwd")
        dw(f"mlp_w2_{i}", u, dm[None], f"mlp{i}_dw2")
        dw(f"mlp_w1_{i}", h2, dz[None], f"mlp{i}_dw1")
        dx_in, dsc, dsh, dnw = _matmul_nt_lnbwd(dz[None], wts[f"mlp_w1_{i}"], x_in, n2w, sc2_, dx_out,
                                                name=f"mlp{i}_up_bwd")
        dmod[i][3], dmod[i][4], dmod[i][5] = dsh, dsc, dg2
        return dx_in, dnw

    dx3, dn2w1 = mlp_bwd(1, dx4, x3, h2_1, a1, u1, ymlp1, n2w1, sc2b, g2b)
    dyp1, dm1, dg1b = _gate_matmul_nt(dx3, g1b, ymix1, wts["fox_w_out"], None, name="fox_out_bwd")
    dw("fox_w_out", ypre1, dm1[None], "fox_dw_out")
    doa = _fox_bwd_prep(dyp1, o1, proj1, q2, name="fox_bwd_prep")
    dqa, dka, dva, colsum = _fox_bwd(imax, q2, ka, va, doa, name="fox_bwd")
    colsum = jnp.pad(colsum[:, 0, :].T, ((0, 0), (0, LANES - nheads)))
    dproj1, dqw, dkw = _fox_bwd_post(dqa, dka, dva, proj1, dyp1, o1, qw2, kw2, name="fox_bwd_post")
    dproj1, dbf = _fox_dfz(colsum, nheads, proj1, bf_pad, dproj1, name="fox_dfz")
    dw("fox_w_in", h1_1, dproj1, "fox_dw_in")
    dx2, dsc, dsh, dn1w1 = _matmul_nt_lnbwd(dproj1, wts["fox_w_in"], x2, n1w1, sc1b, dx3, name="fox_in_bwd")
    dmod[1][0], dmod[1][1], dmod[1][2] = dsh, dsc, dg1b
    gs["fox_qn_w"] = dqw[0, :FOX_DH] + dqw[0, FOX_DH:]
    gs["fox_kn_w"] = dkw[0, :FOX_DH] + dkw[0, FOX_DH:]
    gs["fox_b_f"] = dbf[0, :nheads]

    dx1, dn2w0 = mlp_bwd(0, dx2, x1, h2_0, a0, u0, ymlp0, n2w0, sc2, g2)
    dyp0, dm0, dg1 = _gate_matmul_nt(dx1, g1, ymix0, wts["hg_w_out"], None, name="hg_out_bwd")
    dw("hg_w_out", ypre0, dm0[None], "hg_dw_out")
    part, ctx = reduce_early(gw, slab) if reduce_early is not None else (None, None)
    dproj0, dlb, dgn, *from_chips = _hg_bwd(proj0, small["hg_lb"], gn, o0, states, dyp0, part, name="hg_bwd")
    early = (ctx, from_chips[0]) if reduce_early is not None else None
    dw("hg_w_in", h1_0, dproj0, "hg_dw_in")
    part, ctx = reduce_late(gw) if reduce_late is not None else (None, None)
    dx0, dsc, dsh, dn1w0, *from_chips = _matmul_nt_lnbwd(dproj0, wts["hg_w_in"], x, n1w0, sc1, dx1, part, name="hg_in_bwd")
    late = (ctx, from_chips[0]) if reduce_late is not None else None
    dmod[0][0], dmod[0][1], dmod[0][2] = dsh, dsc, dg1
    gs["hg_lb"] = dlb
    gs["hg_gn_w"] = jnp.sum(dgn, axis=0)

    gs["norm1_w"] = jnp.concatenate([dn1w0, dn1w1], axis=0)
    gs["norm2_w"] = jnp.concatenate([dn2w0, dn2w1], axis=0)
    gs["dmod"] = jnp.stack([jnp.concatenate(dmod[i], axis=1)[0] for i in range(2)])
    return loss, dx0, gw, gs, early, late


def _pack_halves(layout):
    rh = -(-max(sum(a.shape[0] for _, a in half) for half in layout) // 16) * 16
    place, parts = {}, []
    for h, half in enumerate(layout):
        off = 0
        for n, a in half:
            place[n] = (h, off, a.shape[0])
            off += a.shape[0]
        parts.append(jnp.pad(jnp.concatenate([a.astype(BF) for _, a in half], axis=0), ((0, rh - off), (0, 0))))
    return jnp.concatenate(parts, axis=0), place, rh


SMALL_NAMES = ["norm1_w", "norm2_w", "hg_lb", "hg_gn_w", "fox_b_f", "fox_qn_w", "fox_kn_w", "final_w"]


def _pack_small(d, names):
    rows, offs, r0 = [], {}, 0
    for n in names:
        flat = d[n].reshape(-1)
        nr = -(-flat.shape[0] // LANES)
        rows.append(jnp.pad(flat, (0, nr * LANES - flat.shape[0])).reshape(nr, LANES))
        offs[n] = (r0, nr)
        r0 += nr
    return jnp.concatenate(rows, axis=0), offs


def _unpack_small(packed, offs, name, like):
    r0, nr = offs[name]
    return packed[r0:r0 + nr].reshape(-1)[:like.size].reshape(like.shape)


def kernel(x, c, w_mod, b_mod, norm1_w, norm2_w, hg_w_in, hg_w_out, hg_lb, hg_gn_w, fox_w_in, fox_b_f, fox_qn_w, fox_kn_w, fox_w_out, mlp_w1, mlp_w2, final_w, loss_target, m_w_mod, m_b_mod, m_norm1_w, m_norm2_w, m_hg_w_in, m_hg_w_out, m_hg_lb, m_hg_gn_w, m_fox_w_in, m_fox_b_f, m_fox_qn_w, m_fox_kn_w, m_fox_w_out, m_mlp_w1, m_mlp_w2, m_final_w, v_w_mod, v_b_mod, v_norm1_w, v_norm2_w, v_hg_w_in, v_hg_w_out, v_hg_lb, v_hg_gn_w, v_fox_w_in, v_fox_b_f, v_fox_qn_w, v_fox_kn_w, v_fox_w_out, v_mlp_w1, v_mlp_w2, v_final_w):
    S, D = x.shape[1], x.shape[2]
    nheads = D // FOX_DH
    ax, ay, ac = _mesh_pos()
    chip = 2 * ax + ay
    dev = 2 * chip + ac
    xs, tgt = x.reshape(S, D), loss_target.reshape(S, D)

    c_all = _allgather_small(_pad_rows(c.reshape(-1, LANES), 8), name="gather_c")
    c_all = c_all.reshape(N_DEV, -1)[:, :D]
    c16 = _pad_rows(c_all, 16)
    nmod = w_mod.shape[2]
    b_shard = lax.dynamic_slice_in_dim(b_mod, chip * nmod, nmod, axis=1)
    mod_shard = _mod_fwd(c16, w_mod, b_shard[:, None, :], name="mod_fwd")[:, :N_DEV]
    mod_all = _allgather_small(mod_shard.reshape(-1, LANES), name="gather_mod")
    mod_all = mod_all.reshape(N_CHIPS, 2, 2, N_DEV, nmod)[:, 0]
    mod = lax.dynamic_index_in_dim(mod_all, dev, axis=2, keepdims=False)
    mod = mod.transpose(1, 0, 2).reshape(2, N_CHIPS * nmod)

    fox_rows = fox_w_in.shape[2]
    col = lambda g: g.transpose(1, 0, 2).reshape(g.shape[1], -1)
    rowsh = lambda g: g.reshape(-1, g.shape[2])
    own = lambda g, s: lax.dynamic_update_index_in_dim(g, s, chip, 0)

    slab_in = hg_w_in[0].astype(BF)
    wts = {"hg_w_in": col(own(_allgather_chip_slabs(slab_in, name="gather_hg_w_in"), slab_in))}
    fox_flat, fox_cut = fox_w_in[0].reshape(fox_rows, D), fox_rows // 2
    slabs, unpacks = [], []
    for layout_w in ([[("mlp_w1_0", mlp_w1[0]), ("hg_w_out", hg_w_out[0])], [("mlp_w2_0", mlp_w2[0]), ("fox_w_out", fox_w_out[0])]],
                     [[("mlp_w1_1", mlp_w1[1]), ("fox_a", fox_flat[:fox_cut])], [("mlp_w2_1", mlp_w2[1]), ("fox_b", fox_flat[fox_cut:])]]):
        slab_w, place_w, rh_w = _pack_halves(layout_w)

        def unpack(gathered, slab_w=slab_w, place_w=place_w, rh_w=rh_w):
            gathered = own(gathered, slab_w)
            out = {}
            for n, (h, off, rows) in place_w.items():
                g = gathered[:, h * rh_w + off:h * rh_w + off + rows, :]
                out[n] = col(g) if n.startswith("mlp_w1") else rowsh(g) if n.startswith(("mlp_w2", "hg_", "fox_w")) else g
            if "fox_a" in out:
                fox_in = col(jnp.concatenate([out.pop("fox_a"), out.pop("fox_b")], axis=1).reshape(N_CHIPS, D, fox_rows))
                out["fox_w_in"] = jnp.pad(fox_in, ((0, 0), (0, 5 * D - fox_in.shape[1])))
            return out

        slabs.append(slab_w)
        unpacks.append(unpack)

    small = {"norm1_w": norm1_w, "norm2_w": norm2_w, "hg_lb": hg_lb, "hg_gn_w": hg_gn_w, "fox_b_f": fox_b_f,
             "fox_qn_w": fox_qn_w, "fox_kn_w": fox_kn_w, "final_w": final_w}

    def uncol(g, n):
        return g.reshape(g.shape[0], N_CHIPS, n).transpose(1, 0, 2)

    pos = jnp.stack([chip, ac])

    def swap_and_add(g4, tag):
        to_sibling = lax.dynamic_index_in_dim(g4, 1 - ac, axis=1, keepdims=False).astype(BF)
        from_sibling = _swap_halves(to_sibling, name=f"rs_swap_{tag}")
        return from_sibling, _add_halves(g4, from_sibling, ac.reshape(1), name=f"rs_add_halves_{tag}")

    def finish(g4, from_sibling, from_chips, tag):
        my_half = _add_four(g4, from_sibling, from_chips, pos, name=f"rs_add_chips_{tag}")
        return _join_halves(my_half, name=f"rs_join_{tag}")

    layout = [[("mlp_w1", 2 * D), ("hg_w_out", D // 4), ("fox_w_out", D // 4)], [("mlp_w2", 2 * D), ("fox_w_in", fox_rows)]]
    place = {}
    for h, half in enumerate(layout):
        off = 0
        for n, rows in half:
            place[n] = (h, off, rows)
            off += rows

    rh = -(-max(sum(rows for _, rows in half) for half in layout) // 16) * 16
    where = {"hg_w_out": ("row",) + place["hg_w_out"][:2], "fox_w_out": ("row",) + place["fox_w_out"][:2]}
    for i in range(2):
        where[f"mlp_w1_{i}"] = ("col", place["mlp_w1"][0], place["mlp_w1"][1] + i * D)
        where[f"mlp_w2_{i}"] = ("row", place["mlp_w2"][0], place["mlp_w2"][1] + i * D)

    def reduce_early(gw, slab):
        gfox = uncol(gw["fox_w_in"][:, :4 * fox_rows], fox_rows).reshape(N_CHIPS, 1, fox_rows, D)
        h, off, _ = place["fox_w_in"]
        slab = lax.dynamic_update_slice(slab, gfox, (0, h, off, 0))
        for h, half in enumerate(layout):
            used = sum(rows for _, rows in half)
            if used < rh:
                slab = lax.dynamic_update_slice(slab, jnp.zeros((N_CHIPS, 1, rh - used, D), F32), (0, h, used, 0))
        from_sibling, part = swap_and_add(slab, "early")
        return part, (slab, from_sibling)

    def reduce_late(gw):
        g4 = uncol(gw["hg_w_in"], D).reshape(N_CHIPS, 2, D // 2, D)
        from_sibling, part = swap_and_add(g4, "late")
        return part, (g4, from_sibling)

    loss_part, grad_x, gw, gs, (early, from_chips_early), (late, from_chips_late) = _local_step(
        xs, tgt, mod, wts, small, slabs, unpacks, reduce_early, (lax.empty((N_CHIPS, 2, rh, D), F32), where), reduce_late)
    gshard = finish(*early, from_chips_early, "early")
    g_hg_w_in = finish(*late, from_chips_late, "late").reshape(D, D)

    names = ["dmod", "loss"] + SMALL_NAMES
    packed, offs = _pack_small({**gs, "loss": loss_part[0, :1]}, names)
    packed = _pad_rows(packed, 8)
    rp = packed.shape[0]
    parts = _allgather_small(packed, name="gather_small").reshape(N_DEV, rp, LANES)
    total = _sum_parts(parts, name="sum_small")
    r0, nr = offs["dmod"]
    dmod_all = parts[:, r0:r0 + nr].reshape(N_DEV, 2, N_CHIPS * nmod)
    dmod_shard = lax.dynamic_slice_in_dim(dmod_all, chip * nmod, nmod, axis=2).transpose(1, 0, 2)
    g_w_mod = _mod_bwd(c16, jnp.pad(dmod_shard, ((0, 0), (0, 16 - N_DEV), (0, 0))), name="mod_bwd")

    loss = _unpack_small(total, offs, "loss", loss_part[0, :1]).reshape(())
    grads = {"w_mod": g_w_mod, "b_mod": _unpack_small(total, offs, "dmod", b_mod)}
    for n in SMALL_NAMES:
        grads[n] = _unpack_small(total, offs, n, small[n])

    given = dict(w_mod=(w_mod, m_w_mod, v_w_mod), b_mod=(b_mod, m_b_mod, v_b_mod), norm1_w=(norm1_w, m_norm1_w, v_norm1_w),
                 norm2_w=(norm2_w, m_norm2_w, v_norm2_w), hg_w_in=(hg_w_in, m_hg_w_in, v_hg_w_in),
                 hg_w_out=(hg_w_out, m_hg_w_out, v_hg_w_out), hg_lb=(hg_lb, m_hg_lb, v_hg_lb),
                 hg_gn_w=(hg_gn_w, m_hg_gn_w, v_hg_gn_w), fox_w_in=(fox_w_in, m_fox_w_in, v_fox_w_in),
                 fox_b_f=(fox_b_f, m_fox_b_f, v_fox_b_f), fox_qn_w=(fox_qn_w, m_fox_qn_w, v_fox_qn_w),
                 fox_kn_w=(fox_kn_w, m_fox_kn_w, v_fox_kn_w), fox_w_out=(fox_w_out, m_fox_w_out, v_fox_w_out),
                 mlp_w1=(mlp_w1, m_mlp_w1, v_mlp_w1), mlp_w2=(mlp_w2, m_mlp_w2, v_mlp_w2), final_w=(final_w, m_final_w, v_final_w))
    upd = {}

    for n, (h, off, rows) in place.items():
        w, m, v = given[n]
        flat = lambda a: a.reshape(rows, D)
        d, mn, vn = _adamw(flat(w), gshard, flat(m), flat(v), g_at=(h, off), name=f"adamw_{n}")
        grads[n] = gshard[h, off:off + rows].reshape(w.shape)
        upd[n] = tuple(a.reshape(w.shape) for a in (d, mn, vn))

    w, m, v = given["hg_w_in"]
    grads["hg_w_in"] = g_hg_w_in.reshape(w.shape)
    upd["hg_w_in"] = tuple(a.reshape(w.shape) for a in _adamw(w[0], g_hg_w_in, m[0], v[0], name="adamw_hg_w_in"))

    w, m, v = given["w_mod"]
    flat = lambda a: a.reshape(-1, nmod)
    upd["w_mod"] = tuple(a.reshape(w.shape) for a in _adamw(flat(w), flat(g_w_mod), flat(m), flat(v), name="adamw_w_mod"))

    snames = ["b_mod"] + SMALL_NAMES
    pw, soffs = _pack_small({n: given[n][0] for n in snames}, snames)
    pm, _ = _pack_small({n: given[n][1] for n in snames}, snames)
    pv, _ = _pack_small({n: given[n][2] for n in snames}, snames)
    pg, _ = _pack_small({n: grads[n] for n in snames}, snames)
    pw, pm, pv, pg = (_pad_rows(a, 8) for a in (pw, pm, pv, pg))
    sd, smn, svn = _adamw(pw, pg, pm, pv, name="adamw_small")
    for n in snames:
        like = given[n][0]
        upd[n] = tuple(_unpack_small(a, soffs, n, like) for a in (sd, smn, svn))

    order = ["w_mod", "b_mod", "norm1_w", "norm2_w", "hg_w_in", "hg_w_out", "hg_lb", "hg_gn_w", "fox_w_in", "fox_b_f",
             "fox_qn_w", "fox_kn_w", "fox_w_out", "mlp_w1", "mlp_w2", "final_w"]
    return (loss, grad_x.reshape(x.shape), *[grads[n] for n in order], *[upd[n][0] for n in order],
            *[upd[n][1] for n in order], *[upd[n][2] for n in order])
```

```python
import math

import jax
import jax.numpy as jnp
from jax import lax
from jax.experimental import pallas as pl
from jax.experimental.pallas import tpu as pltpu

EPS = 1e-6
ADAM_LR, ADAM_B1, ADAM_B2, ADAM_EPS, ADAM_WD, ADAM_STEP = 0.001, 0.9, 0.999, 1e-08, 0.01, 10

F32 = jnp.float32
BF = jnp.bfloat16
LANES = 128
HG_CHUNK = 64
HG_HEADS_PER_STEP = 8
HG_TOKENS_PER_STEP = 256
FOX_ROWS_PER_STEP = 1024
FOX_BWD_TILES = (8, 4, 2, 1)
LOG2E = 1.4426950408889634
FOX_DH = 64
N_CHIPS = 4
N_DEV = 8
VMEM_LIMIT = 56 * 1024 * 1024
MESH = pl.DeviceIdType.MESH

NT = (((1,), (1,)), ((), ()))
TN = (((0,), (0,)), ((), ()))


def _pick(n, pref, mult=LANES):
    if n <= pref:
        return n
    t = (pref // mult) * mult
    while t >= mult:
        if n % t == 0:
            return t
        t -= mult
    raise ValueError((n, pref, mult))


def _cp(*sem):
    return pltpu.CompilerParams(dimension_semantics=sem, vmem_limit_bytes=VMEM_LIMIT)


def _dot(a, b):
    return jnp.dot(a, b, preferred_element_type=F32)


def _dg(a, b, dims):
    return lax.dot_general(a, b, dims, preferred_element_type=F32)


def _split3(x):
    hi = x.astype(BF)
    r1 = x - hi.astype(F32)
    mid = r1.astype(BF)
    lo = (r1 - mid.astype(F32)).astype(BF)
    return hi, mid, lo


def _tri_dot(tri, x):
    hi, mid, lo = _split3(x)
    return _dot(tri, hi) + _dot(tri, mid) + _dot(tri, lo)


def _dg3(a, b, dims):
    ah, bh = a.astype(BF), b.astype(BF)
    al, bl = (a - ah.astype(F32)).astype(BF), (b - bh.astype(F32)).astype(BF)
    return _dg(ah, bh, dims) + _dg(ah, bl, dims) + _dg(al, bh, dims)


def _dg1(a, b, dims):
    return _dg(a.astype(BF), b.astype(BF), dims)


NN = (((1,), (0,)), ((), ()))


def _sigmoid(x):
    return jax.nn.sigmoid(x)


def _ln_matmul(x, nw, sc, sh, w, slab=None, *, relu2, name):
    S, D = x.shape
    N = w.shape[1]
    tm, tn = _pick(S, 512, 16), N
    fused = slab is not None

    def body(x_ref, nw_ref, sc_ref, sh_ref, w_ref, *rest):
        if fused:
            s_ref, *outs, out_ref, hs, send_sems, recv_sems = rest
            finish = _gather_behind(s_ref, out_ref, send_sems, recv_sems, pl.program_id(0), S // tm)
        else:
            outs, hs = rest[:-1], rest[-1]
        h_ref = outs[-1]

        @pl.when(pl.program_id(1) == 0)
        def _():
            xv = x_ref[...]
            r = lax.rsqrt(jnp.mean(xv * xv, axis=-1, keepdims=True) + EPS)
            hb = ((xv * r * nw_ref[...]) * (1.0 + sc_ref[...]) + sh_ref[...]).astype(BF)
            hs[...] = hb
            h_ref[...] = hb

        z = _dot(hs[...], w_ref[...])
        if relu2:
            a = jnp.maximum(z, 0.0)
            outs[0][...] = a.astype(BF)
            outs[1][...] = (a * a).astype(BF)
        else:
            outs[0][...] = z
        if fused:
            finish()

    vec = pl.BlockSpec((1, D), lambda i, j: (0, 0))
    tile = pl.BlockSpec((tm, tn), lambda i, j: (i, j))
    if relu2:
        out_shape = [jax.ShapeDtypeStruct((S, N), BF), jax.ShapeDtypeStruct((S, N), BF)]
        out_specs = [tile, tile]
    else:
        out_shape = [jax.ShapeDtypeStruct((S, N), F32)]
        out_specs = [tile]
    out_shape.append(jax.ShapeDtypeStruct((S, D), BF))
    out_specs.append(pl.BlockSpec((tm, D), lambda i, j: (i, 0)))
    in_specs = [pl.BlockSpec((tm, D), lambda i, j: (i, 0)), vec, vec, vec, pl.BlockSpec((D, tn), lambda i, j: (0, j))]
    scratch = [pltpu.VMEM((tm, D), BF)]
    args = [x, nw, sc, sh, w]
    if fused:
        in_specs.append(HBM)
        out_specs.append(HBM)
        out_shape.append(jax.ShapeDtypeStruct((N_CHIPS,) + slab.shape, slab.dtype))
        scratch += [pltpu.SemaphoreType.DMA((6,)), pltpu.SemaphoreType.DMA((6,))]
        args.append(slab)
    return pl.pallas_call(
        body, name=name, grid=(S // tm, N // tn), in_specs=in_specs, out_specs=out_specs, out_shape=out_shape,
        scratch_shapes=scratch, compiler_params=_cp("arbitrary", "arbitrary"),
    )(*args)


def _matmul_resid(a, w, x, gate, *, name):
    S, K = a.shape
    D = w.shape[1]
    tm, tn = _pick(S, 1024 if K <= 1024 else 512, 16), D

    def body(a_ref, w_ref, x_ref, g_ref, o_ref, y_ref):
        y = _dot(a_ref[...], w_ref[...])
        y_ref[...] = y.astype(BF)
        o_ref[...] = x_ref[...] + g_ref[...] * y

    tile = pl.BlockSpec((tm, tn), lambda i, j: (i, j))
    return pl.pallas_call(
        body, name=name, grid=(S // tm, D // tn),
        in_specs=[pl.BlockSpec((tm, K), lambda i, j: (i, 0)), pl.BlockSpec((K, tn), lambda i, j: (0, j)),
                  tile, pl.BlockSpec((1, tn), lambda i, j: (0, j))],
        out_specs=[tile, tile],
        out_shape=[jax.ShapeDtypeStruct((S, D), F32), jax.ShapeDtypeStruct((S, D), BF)],
        compiler_params=_cp("parallel", "arbitrary"),
    )(a, w, x, gate)


def _gate_matmul_nt(dx, gate, y, w, act, *, name):
    S, D = dx.shape
    K = w.shape[0]
    tm, tn = _pick(S, 1024 if K <= 1024 else 512, 16), K
    fused = act is not None

    def body(dx_ref, g_ref, y_ref, w_ref, *rest):
        if fused:
            act_ref, da_ref, dm_ref, dg_ref, ms = rest
        else:
            da_ref, dm_ref, dg_ref, ms = rest
        i, j = pl.program_id(0), pl.program_id(1)

        @pl.when((i == 0) & (j == 0))
        def _():
            dg_ref[...] = jnp.zeros_like(dg_ref)

        @pl.when(j == 0)
        def _():
            dxv = dx_ref[...]
            dmb = (dxv * g_ref[...]).astype(BF)
            ms[...] = dmb
            dm_ref[...] = dmb
            dg_ref[...] += jnp.sum(dxv * y_ref[...].astype(F32), axis=0, keepdims=True)

        da = _dg(ms[...], w_ref[...], NT)
        if fused:
            da_ref[...] = (da * (2.0 * act_ref[...].astype(F32))).astype(BF)
        else:
            da_ref[...] = da

    row = pl.BlockSpec((tm, D), lambda i, j: (i, 0))
    vec = pl.BlockSpec((1, D), lambda i, j: (0, 0))
    tile = pl.BlockSpec((tm, tn), lambda i, j: (i, j))
    in_specs = [row, vec, row, pl.BlockSpec((tn, D), lambda i, j: (j, 0))]
    args = [dx, gate, y, w]
    if fused:
        in_specs.append(tile)
        args.append(act)
    return pl.pallas_call(
        body, name=name, grid=(S // tm, K // tn),
        in_specs=in_specs, out_specs=[tile, row, vec],
        out_shape=[jax.ShapeDtypeStruct((S, K), BF if fused else F32), jax.ShapeDtypeStruct((S, D), BF),
                   jax.ShapeDtypeStruct((1, D), F32)],
        scratch_shapes=[pltpu.VMEM((tm, D), BF)],
        compiler_params=_cp("arbitrary", "arbitrary"),
    )(*args)


def _matmul_tn(a, b, *, name, into=None):
    S, Ka = a.shape
    P, _, Db = b.shape
    tk, tn, ts = _pick(Ka, 1024), _pick(Db, 1024), _pick(S, 1024, 16)
    if into is not None:
        slab, kind, half, off = into
        C = tn = slab.shape[3]
        per_chip = Ka // N_CHIPS
        all_chips = kind == "row" and tk == Ka
        if kind == "row" and not all_chips:
            tk = min(tk, per_chip)
        assert tn == C and P * Db == (N_CHIPS * C if kind == "col" else C)
        if kind == "col":
            assert tk == Ka and off % tk == 0
        elif all_chips:
            assert off % per_chip == 0
        else:
            assert per_chip % tk == 0 and off % tk == 0
    npb = Db // tn

    def body(a_ref, b_ref, *rest):
        o_ref, acc = rest[-2:]
        s = pl.program_id(2)

        @pl.when(s == 0)
        def _():
            acc[...] = jnp.zeros_like(acc)

        acc[...] += _dg(a_ref[...], b_ref[...], TN)

        @pl.when(s == pl.num_programs(2) - 1)
        def _():
            o_ref[...] = acc[...].reshape(o_ref.shape)

    in_specs = [pl.BlockSpec((ts, tk), lambda i, j, s: (s, i)),
                pl.BlockSpec((None, ts, tn), lambda i, j, s: (j // npb, s, j % npb))]
    args = [a, b]
    if into is None:
        out_spec = pl.BlockSpec((tk, tn), lambda i, j, s: (i, j))
        out_shape = jax.ShapeDtypeStruct((Ka, P * Db), F32)
        aliases = {}
    else:
        per = per_chip // tk if kind == "row" and not all_chips else 1
        if kind == "col":
            out_spec = pl.BlockSpec((None, None, tk, tn), lambda i, j, s: (j, half, off // tk + i, 0))
        elif all_chips:
            out_spec = pl.BlockSpec((N_CHIPS, None, per_chip, tn), lambda i, j, s: (0, half, off // per_chip, 0))
        else:
            out_spec = pl.BlockSpec((None, None, tk, tn), lambda i, j, s: (i // per, half, off // tk + i % per, 0))
        out_shape = jax.ShapeDtypeStruct(slab.shape, F32)
        in_specs.append(pl.BlockSpec(memory_space=pl.ANY))
        args.append(slab)
        aliases = {2: 0}
    return pl.pallas_call(
        body, name=name, grid=(Ka // tk, P * npb, S // ts),
        in_specs=in_specs, out_specs=out_spec, out_shape=out_shape,
        scratch_shapes=[pltpu.VMEM((tk, tn), F32)], input_output_aliases=aliases,
        compiler_params=_cp("parallel", "parallel", "arbitrary"),
    )(*args)


def _matmul_nt_lnbwd(g, w, x, nw, sc, dx_out, part=None, *, name):
    P, S, Dg = g.shape
    D = x.shape[1]
    tm = _pick(S, 512, 16)
    fused = part is not None

    def body(g_ref, w_ref, x_ref, nw_ref, sc_ref, dxo_ref, *rest):
        if fused:
            p_ref, dx_ref, dsc_ref, dsh_ref, dnw_ref, recv_ref, send_sems, recv_sems = rest
            copies = _scatter_copies(p_ref, recv_ref, send_sems, recv_sems)
        else:
            dx_ref, dsc_ref, dsh_ref, dnw_ref = rest

        @pl.when(pl.program_id(0) == 0)
        def _():
            dsc_ref[...] = jnp.zeros_like(dsc_ref)
            dsh_ref[...] = jnp.zeros_like(dsh_ref)
            dnw_ref[...] = jnp.zeros_like(dnw_ref)
            if fused:
                for cp in copies:
                    cp.start()

        dh = _dg(g_ref[0], w_ref[:, 0:Dg], NT)
        for p in range(1, P):
            dh = dh + _dg(g_ref[p], w_ref[:, p * Dg:(p + 1) * Dg], NT)
        xv = x_ref[...]
        nwv = nw_ref[...]
        r = lax.rsqrt(jnp.mean(xv * xv, axis=-1, keepdims=True) + EPS)
        xr = xv * r
        dn = dh * (1.0 + sc_ref[...])
        dsc_ref[...] += jnp.sum(dh * (xr * nwv), axis=0, keepdims=True)
        dsh_ref[...] += jnp.sum(dh, axis=0, keepdims=True)
        dnw_ref[...] += jnp.sum(dn * xr, axis=0, keepdims=True)
        u = dn * nwv
        dx_ref[...] = dxo_ref[...] + r * (u - xr * jnp.mean(u * xr, axis=-1, keepdims=True))

        if fused:
            @pl.when(pl.program_id(0) == S // tm - 1)
            def _():
                for cp in copies:
                    cp.wait()

    row = pl.BlockSpec((tm, D), lambda i: (i, 0))
    vec = pl.BlockSpec((1, D), lambda i: (0, 0))
    in_specs = [pl.BlockSpec((P, tm, Dg), lambda i: (0, i, 0)), pl.BlockSpec((D, P * Dg), lambda i: (0, 0)), row, vec, vec, row]
    out_specs = [row, vec, vec, vec]
    out_shape = [jax.ShapeDtypeStruct((S, D), F32)] + [jax.ShapeDtypeStruct((1, D), F32)] * 3
    scratch, args = [], [g, w, x, nw, sc, dx_out]
    if fused:
        in_specs.append(HBM)
        out_specs.append(HBM)
        out_shape.append(jax.ShapeDtypeStruct((3,) + part.shape[1:], part.dtype))
        scratch = [pltpu.SemaphoreType.DMA((3,)), pltpu.SemaphoreType.DMA((3,))]
        args.append(part)
    return pl.pallas_call(
        body, name=name, grid=(S // tm,), in_specs=in_specs, out_specs=out_specs, out_shape=out_shape,
        scratch_shapes=scratch, compiler_params=_cp("arbitrary"),
    )(*args)


def _loss_kernel(x, fw, tgt, *, name):
    S, D = x.shape
    tm = _pick(S, 512, 8)

    def body(x_ref, fw_ref, t_ref, l_ref, dx_ref, dfw_ref):
        @pl.when(pl.program_id(0) == 0)
        def _():
            l_ref[...] = jnp.zeros_like(l_ref)
            dfw_ref[...] = jnp.zeros_like(dfw_ref)

        xv = x_ref[...]
        fwv = fw_ref[...]
        r = lax.rsqrt(jnp.mean(xv * xv, axis=-1, keepdims=True) + EPS)
        xr = xv * r
        err = xr * fwv - t_ref[...]
        per_tok = jnp.mean(err * err, axis=-1, keepdims=True)
        l_ref[...] += 0.5 * jnp.sum(per_tok, axis=0, keepdims=True)
        dy = err * (1.0 / D)
        dfw_ref[...] += jnp.sum(dy * xr, axis=0, keepdims=True)
        u = dy * fwv
        dx_ref[...] = r * (u - xr * jnp.mean(u * xr, axis=-1, keepdims=True))

    row = pl.BlockSpec((tm, D), lambda i: (i, 0))
    vec = pl.BlockSpec((1, D), lambda i: (0, 0))
    return pl.pallas_call(
        body, name=name, grid=(S // tm,),
        in_specs=[row, vec, row],
        out_specs=[pl.BlockSpec((1, LANES), lambda i: (0, 0)), row, vec],
        out_shape=[jax.ShapeDtypeStruct((1, LANES), F32), jax.ShapeDtypeStruct((S, D), F32),
                   jax.ShapeDtypeStruct((1, D), F32)],
        compiler_params=_cp("arbitrary"),
    )(x, fw, tgt)


def _hg_lower_bound(lb3):
    mx = jnp.max(lb3, axis=0, keepdims=True)
    e = jnp.exp(lb3 - mx)
    p = e / jnp.sum(e, axis=0, keepdims=True)
    return p[0:1, :], p


def _hg_chunk_common(qr, fz, lbv):
    sq = _sigmoid(qr)
    q = qr * sq
    sig = _sigmoid(fz)
    f = lbv + (1.0 - lbv) * sig
    k = (1.0 - lbv) * (1.0 - sig)
    return q, sq, sig, f, k, jnp.log(f)


def _row_of(x, rows, r):
    return jnp.sum(jnp.where(rows == r, x, 0.0), axis=0, keepdims=True)


def _hg_fwd(proj, hg_lb, gn, slab=None, *, name):
    S = proj.shape[0]
    D = proj.shape[1] // 4
    H = D // LANES
    HB = min(HG_HEADS_PER_STEP, H)
    W = HB * LANES
    C = HG_CHUNK
    T = _pick(S, HG_TOKENS_PER_STEP, C)
    nch, nb = T // C, S // T
    ng = H // HB
    fused = slab is not None

    def body(q_ref, fz_ref, v_ref, g_ref, lb_ref, gn_ref, *rest):
        if fused:
            s_ref, y_ref, o_ref, sts_ref, out_ref, st, send_sems, recv_sems = rest
            finish = _gather_behind(s_ref, out_ref, send_sems, recv_sems,
                                    pl.program_id(0) * nb + pl.program_id(1), ng * nb)
        else:
            y_ref, o_ref, sts_ref, st = rest

        @pl.when(pl.program_id(1) == 0)
        def _():
            st[...] = jnp.zeros_like(st)

        lb_all, _ = _hg_lower_bound(lb_ref[...])
        gnv = gn_ref[...]
        ri = lax.broadcasted_iota(jnp.int32, (C, C), 0)
        ci_ = lax.broadcasted_iota(jnp.int32, (C, C), 1)
        low = ri >= ci_
        tri = jnp.where(low, 1.0, 0.0).astype(BF)
        rows_w = lax.broadcasted_iota(jnp.int32, (C, W), 0)

        def chunk(ci, carry):
            sl = pl.ds(pl.multiple_of(ci * C, C), C)
            heads = [slice(hh * LANES, (hh + 1) * LANES) for hh in range(HB)]
            q, _, _, _, k, logf = _hg_chunk_common(q_ref[sl, :], fz_ref[sl, :], lb_all)
            vv, gg = v_ref[sl, :], g_ref[sl, :]
            G = _tri_dot(tri, logf)
            Gm = _row_of(G, rows_w, C // 2 - 1)
            Gl = _row_of(G, rows_w, C - 1)
            qt, kt = q * jnp.exp(G - Gm), k * jnp.exp(Gm - G)
            qe, kd, eGl = q * jnp.exp(G), k * jnp.exp(Gl - G), jnp.exp(Gl)
            A = [jnp.where(low, _dg1(qt[:, ls], kt[:, ls], NT), 0.0) for ls in heads]
            Sv = [st[hh] for hh in range(HB)]
            for hh in range(HB):
                sts_ref[hh, ci] = Sv[hh]
            o = [_dg1(A[hh], vv[:, ls], NN) + _dg1(qe[:, ls], Sv[hh], NT) for hh, ls in enumerate(heads)]
            for hh, ls in enumerate(heads):
                st[hh] = Sv[hh] * eGl[:, ls] + _dg1(vv[:, ls], kd[:, ls], TN)
            gate = gg * _sigmoid(gg)
            for hh, ls in enumerate(heads):
                r = lax.rsqrt(jnp.mean(o[hh] * o[hh], axis=-1, keepdims=True) + EPS)
                y_ref[sl, ls] = ((o[hh] * r * gnv) * gate[:, ls]).astype(BF)
                o_ref[sl, ls] = o[hh]
            return carry

        lax.fori_loop(0, nch, chunk, 0)

        if fused:
            finish()

    def part(p):
        return pl.BlockSpec((T, W), lambda h, n: (n, p * ng + h))

    blk = pl.BlockSpec((T, W), lambda h, n: (n, h))
    in_specs = [part(0), part(1), part(2), part(3),
                pl.BlockSpec((3, W), lambda h, n: (0, h)), pl.BlockSpec((1, LANES), lambda h, n: (0, 0))]
    out_specs = [blk, blk, pl.BlockSpec((HB, nch, LANES, LANES), lambda h, n: (h, n, 0, 0))]
    out_shape = [jax.ShapeDtypeStruct((S, D), BF), jax.ShapeDtypeStruct((S, D), F32),
                 jax.ShapeDtypeStruct((H, S // C, LANES, LANES), F32)]
    scratch = [pltpu.VMEM((HB, LANES, LANES), F32)]
    args = [proj, proj, proj, proj, hg_lb, gn]
    if fused:
        in_specs.append(HBM)
        out_specs.append(HBM)
        out_shape.append(jax.ShapeDtypeStruct((N_CHIPS,) + slab.shape, slab.dtype))
        scratch += [pltpu.SemaphoreType.DMA((6,)), pltpu.SemaphoreType.DMA((6,))]
        args.append(slab)
    return pl.pallas_call(
        body, name=name, grid=(ng, nb), in_specs=in_specs, out_specs=out_specs, out_shape=out_shape,
        scratch_shapes=scratch, compiler_params=_cp("arbitrary", "arbitrary"),
    )(*args)


def _hg_bwd(proj, hg_lb, gn, o_all, states, dy, part=None, *, name):
    S = proj.shape[0]
    D = proj.shape[1] // 4
    H = D // LANES
    HB = min(HG_HEADS_PER_STEP, H)
    W = HB * LANES
    C = HG_CHUNK
    T = _pick(S, HG_TOKENS_PER_STEP, C)
    nch, nb = T // C, S // T
    ng = H // HB
    fused = part is not None

    def body(q_ref, fz_ref, v_ref, g_ref, lb_ref, gn_ref, o_ref, sts_ref, dy_ref, *rest):
        if fused:
            p_ref, dp_ref, dlb_ref, dgn_ref, recv_ref, dst, dlb_acc, send_sems, recv_sems = rest
            copies = _scatter_copies(p_ref, recv_ref, send_sems, recv_sems)

            @pl.when((pl.program_id(0) == 0) & (pl.program_id(1) == 0))
            def _():
                for cp in copies:
                    cp.start()
        else:
            dp_ref, dlb_ref, dgn_ref, dst, dlb_acc = rest
        n = pl.program_id(1)

        @pl.when(n == 0)
        def _():
            dst[...] = jnp.zeros_like(dst)
            dlb_acc[...] = jnp.zeros_like(dlb_acc)
            dgn_ref[...] = jnp.zeros_like(dgn_ref)

        lb_all, p3 = _hg_lower_bound(lb_ref[...])
        gnv = gn_ref[...]
        ri = lax.broadcasted_iota(jnp.int32, (C, C), 0)
        ci_ = lax.broadcasted_iota(jnp.int32, (C, C), 1)
        low = ri >= ci_
        tri = jnp.where(low, 1.0, 0.0).astype(BF)
        triu = jnp.where(ri <= ci_, 1.0, 0.0).astype(BF)
        rows_w = lax.broadcasted_iota(jnp.int32, (C, W), 0)
        gnw = jnp.tile(gnv, (1, HB))

        def chunk(cj, carry):
            ci = nch - 1 - cj
            sl = pl.ds(pl.multiple_of(ci * C, C), C)
            heads = list(enumerate(slice(hh * LANES, (hh + 1) * LANES) for hh in range(HB)))
            wide = lambda parts: jnp.concatenate(parts, axis=1)
            qr, vv, gg = q_ref[sl, :], v_ref[sl, :], g_ref[sl, :]
            q, sq, sig, f, k, logf = _hg_chunk_common(qr, fz_ref[sl, :], lb_all)
            G = _tri_dot(tri, logf)
            Gm = _row_of(G, rows_w, C // 2 - 1)
            Gl = _row_of(G, rows_w, C - 1)
            eG, e_qm, e_km, e_lk, eGl = jnp.exp(G), jnp.exp(G - Gm), jnp.exp(Gm - G), jnp.exp(Gl - G), jnp.exp(Gl)
            qt, kt, kdec, qe = q * e_qm, k * e_km, k * e_lk, q * eG
            sg = _sigmoid(gg)
            d_onw = dy_ref[sl, :] * (gg * sg)
            u = d_onw * gnw
            o = o_ref[sl, :]
            on, do = [], []
            for hh, ls in heads:
                r = lax.rsqrt(jnp.mean(o[:, ls] * o[:, ls], axis=-1, keepdims=True) + EPS)
                on.append(o[:, ls] * r)
                dgn_ref[hh] += jnp.sum(d_onw[:, ls] * on[hh], axis=0, keepdims=True)
                do.append(r * (u[:, ls] - on[hh] * jnp.mean(u[:, ls] * on[hh], axis=-1, keepdims=True)))
            dgg = dy_ref[sl, :] * (wide(on) * gnw) * (sg * (1.0 + gg * (1.0 - sg)))
            Sv = [sts_ref[hh, ci] for hh, _ in heads]
            dSv = [dst[hh] for hh, _ in heads]
            A = [jnp.where(low, _dg1(qt[:, ls], kt[:, ls], NT), 0.0) for _, ls in heads]
            dA = [jnp.where(low, _dg3(do[hh], vv[:, ls], NT), 0.0) for hh, ls in heads]
            dv = wide([_dg1(A[hh], do[hh], TN) + _dg1(kdec[:, ls], dSv[hh], NT) for hh, ls in heads])
            dq = wide([_dg3(dA[hh], kt[:, ls], NN) for hh, ls in heads]) * e_qm \
                + eG * wide([_dg3(do[hh], Sv[hh], NN) for hh, _ in heads])
            dk = wide([_dg3(dA[hh], qt[:, ls], TN) for hh, ls in heads]) * e_km \
                + e_lk * wide([_dg3(vv[:, ls], dSv[hh], NN) for hh, ls in heads])
            s_end = [Sv[hh] * eGl[:, ls] + _dg3(vv[:, ls], kdec[:, ls], TN) for hh, ls in heads]
            dgl = wide([jnp.sum(dSv[hh] * s_end[hh], axis=0, keepdims=True) for hh, _ in heads])
            for hh, ls in heads:
                dst[hh] = dSv[hh] * eGl[:, ls] + _dg1(do[hh], qe[:, ls], TN)
            dG = q * dq - k * dk + jnp.where(rows_w == C - 1, dgl, 0.0)
            dlogf = _tri_dot(triu, dG) - f * dk
            dlf_f = dlogf / f
            dlb_acc[...] += jnp.sum(dlf_f * (1.0 - sig), axis=0, keepdims=True)
            dp_ref[0, sl, :] = (dq * (sq * (1.0 + qr * (1.0 - sq)))).astype(BF)
            dp_ref[1, sl, :] = (dlf_f * (1.0 - lb_all) * sig * (1.0 - sig)).astype(BF)
            dp_ref[2, sl, :] = dv.astype(BF)
            dp_ref[3, sl, :] = dgg.astype(BF)
            return carry

        lax.fori_loop(0, nch, chunk, 0)
        sel = jnp.where(lax.broadcasted_iota(jnp.int32, (3, W), 0) == 0, 1.0, 0.0)
        dlb_ref[...] = lb_all * (sel - p3) * dlb_acc[...]

        if fused:
            @pl.when((pl.program_id(0) == ng - 1) & (n == nb - 1))
            def _():
                for cp in copies:
                    cp.wait()

    def col(p):
        return pl.BlockSpec((T, W), lambda h, n: (nb - 1 - n, p * ng + h))

    blk = pl.BlockSpec((T, W), lambda h, n: (nb - 1 - n, h))
    in_specs = [col(0), col(1), col(2), col(3),
                pl.BlockSpec((3, W), lambda h, n: (0, h)), pl.BlockSpec((1, LANES), lambda h, n: (0, 0)),
                blk, pl.BlockSpec((HB, nch, LANES, LANES), lambda h, n: (h, nb - 1 - n, 0, 0)), blk]
    out_specs = [pl.BlockSpec((4, T, W), lambda h, n: (0, nb - 1 - n, h)),
                 pl.BlockSpec((3, W), lambda h, n: (0, h)),
                 pl.BlockSpec((HB, 1, LANES), lambda h, n: (h, 0, 0))]
    out_shape = [jax.ShapeDtypeStruct((4, S, D), BF), jax.ShapeDtypeStruct((3, D), F32),
                 jax.ShapeDtypeStruct((H, 1, LANES), F32)]
    scratch = [pltpu.VMEM((HB, LANES, LANES), F32), pltpu.VMEM((1, W), F32)]
    args = [proj, proj, proj, proj, hg_lb, gn, o_all, states, dy]
    if fused:
        in_specs.append(HBM)
        out_specs.append(HBM)
        out_shape.append(jax.ShapeDtypeStruct((3,) + part.shape[1:], part.dtype))
        scratch += [pltpu.SemaphoreType.DMA((3,)), pltpu.SemaphoreType.DMA((3,))]
        args.append(part)
    return pl.pallas_call(
        body, name=name, grid=(ng, nb), in_specs=in_specs, out_specs=out_specs, out_shape=out_shape,
        scratch_shapes=scratch, compiler_params=_cp("arbitrary", "arbitrary"),
    )(*args)


def _log_sigmoid(u):
    return jnp.minimum(u, 0.0) - jnp.log(1.0 + jnp.exp(-jnp.abs(u)))


def _lane_put(base, lane, first, pieces):
    for n, p in enumerate(pieces):
        base = jnp.where(lane == first + n, p, base)
    return base


def _fox_cumsum(proj, bf_pad, *, name):
    S = proj.shape[0]
    D = proj.shape[1] // 5
    T = _pick(S, 256, 8)

    def body(fz_ref, b_ref, f_ref, carry):
        @pl.when(pl.program_id(0) == 0)
        def _():
            carry[...] = jnp.zeros_like(carry)

        logf = _log_sigmoid(fz_ref[...] + b_ref[...])
        tri = jnp.where(lax.broadcasted_iota(jnp.int32, (T, T), 0) >= lax.broadcasted_iota(jnp.int32, (T, T), 1),
                        1.0, 0.0).astype(BF)
        fv = _tri_dot(tri, logf) + carry[...]
        f_ref[...] = fv
        carry[...] = _row_of(fv, lax.broadcasted_iota(jnp.int32, (T, LANES), 0), T - 1)

    return pl.pallas_call(
        body, name=name, grid=(S // T,),
        in_specs=[pl.BlockSpec((T, LANES), lambda i: (i, 4 * D // LANES)), pl.BlockSpec((1, LANES), lambda i: (0, 0))],
        out_specs=pl.BlockSpec((T, LANES), lambda i: (i, 0)),
        out_shape=jax.ShapeDtypeStruct((S, LANES), F32),
        scratch_shapes=[pltpu.VMEM((1, LANES), F32)],
        compiler_params=_cp("arbitrary"),
    )(proj, bf_pad)


def _pair_stats(sq, lo):
    del lo
    a = lax.broadcasted_iota(jnp.int32, (LANES, LANES), 0) < FOX_DH
    b = lax.broadcasted_iota(jnp.int32, (LANES, LANES), 1) < FOX_DH
    avg = jnp.where(a == b, 1.0 / FOX_DH, 0.0).astype(BF)
    hi, mid, low = _split3(sq)
    return _dot(hi, avg) + _dot(mid, avg) + _dot(low, avg)


def _fox_prep(proj, fcum, qw2, kw2, *, name):
    S = proj.shape[0]
    D = proj.shape[1] // 5
    HP = D // LANES
    T = _pick(S, FOX_ROWS_PER_STEP, 16)

    def body(q_ref, k_ref, v_ref, f_ref, qw_ref, kw_ref, qa_ref, ka_ref, va_ref, vt_ref):
        hp = pl.program_id(1)
        lane = lax.broadcasted_iota(jnp.int32, (T, LANES), 1)
        lo = lane < FOX_DH
        qv, kv, vv, fv = q_ref[...], k_ref[...], v_ref[...], f_ref[...]
        qn = qv * lax.rsqrt(_pair_stats(qv * qv, lo) + EPS) * qw_ref[...] * (0.125 * LOG2E)
        kn = kv * lax.rsqrt(_pair_stats(kv * kv, lo) + EPS) * kw_ref[...]
        ones_q = jnp.where((lane >= 67) & (lane <= 69), 1.0, 0.0)
        ones_k = jnp.where(((lane >= 64) & (lane <= 66)) | ((lane >= 70) & (lane <= 72)), 1.0, 0.0)
        ones_v = jnp.where((lane >= 64) & (lane <= 66), 1.0, 0.0)
        for hh in range(2):
            fh = jnp.sum(jnp.where(lane == 2 * hp + hh, fv, 0.0), axis=-1, keepdims=True) * LOG2E
            pieces = [p.astype(F32) for p in _split3(fh)]

            def half(x):
                return jnp.where(lo, x if hh == 0 else pltpu.roll(x, FOX_DH, 1), 0.0)

            qa_ref[hh] = _lane_put(half(qn) + ones_q, lane, 64, pieces).astype(BF)
            ka_ref[hh] = _lane_put(half(kn) + ones_k, lane, 67, [-p for p in pieces]).astype(BF)
            va = half(vv) + ones_v
            va_ref[hh] = va.astype(BF)
            vt_ref[hh] = va.T.astype(BF)

    def part(p):
        return pl.BlockSpec((T, LANES), lambda i, hp: (i, p * HP + hp))

    vec = pl.BlockSpec((1, LANES), lambda i, hp: (0, 0))
    aug = pl.BlockSpec((2, T, LANES), lambda i, hp: (hp, i, 0))
    return pl.pallas_call(
        body, name=name, grid=(S // T, HP),
        in_specs=[part(0), part(1), part(2), pl.BlockSpec((T, LANES), lambda i, hp: (i, 0)), vec, vec],
        out_specs=[aug, aug, aug, pl.BlockSpec((2, LANES, T), lambda i, hp: (hp, 0, i))],
        out_shape=[jax.ShapeDtypeStruct((2 * HP, S, LANES), BF)] * 3 + [jax.ShapeDtypeStruct((2 * HP, LANES, S), BF)],
        compiler_params=_cp("parallel", "arbitrary"),
    )(proj, proj, proj, fcum, qw2, kw2)


def _fox_block(S):
    return _pick(S, 256, 16)


def _fox_skip_bounds(fcum, qn_w, kn_w, nheads):
    S = fcum.shape[0]
    B = _fox_block(S)
    qk = 8.0 * LOG2E * 1.02 * jnp.max(jnp.abs(qn_w)) * jnp.max(jnp.abs(kn_w))
    thresh = -(2.0 * qk + 160.0)
    f2 = fcum[:, :nheads] * LOG2E
    first, last = f2[0::B], f2[B - 1::B]
    nb = S // B
    blk = jnp.arange(nb)
    dead = (first[0::2, None, :] - last[None, :, :]) < thresh
    jmin = jnp.sum(dead & (blk[None, :, None] < 2 * jnp.arange(nb // 2)[:, None, None]), axis=1)
    live = (first[:, None, :] - last[None, :, :]) >= thresh
    imax = blk[:, None] + jnp.sum(live & (blk[:, None, None] > blk[None, :, None]), axis=0)
    return jmin.T.astype(jnp.int32), imax.T.astype(jnp.int32)


def _fox_fwd(jmin, qa, ka, vat, proj, *, name):
    H, S, _ = qa.shape
    HP = H // 2
    D = HP * LANES
    B = _fox_block(S)
    BQ = 2 * B
    nq = S // BQ

    def body(jmin_ref, q_ref, k_ref, vt_ref, g_ref, y_ref, o_ref, q2_ref):
        hp, i = pl.program_id(0), pl.program_id(1)
        lane = lax.broadcasted_iota(jnp.int32, (BQ, LANES), 1)
        lo = lane < FOX_DH
        in_stat = (lane >= 70) & (lane <= 75)
        causal = lax.broadcasted_iota(jnp.int32, (BQ, BQ), 0) <= lax.broadcasted_iota(jnp.int32, (BQ, BQ), 1)
        row = lax.broadcasted_iota(jnp.int32, (LANES, BQ), 0)
        m0, acc0 = jnp.full((1, BQ), -jnp.inf, F32), jnp.zeros((LANES, BQ), F32)
        outs = []
        for hh in range(2):
            qb = q_ref[hh]

            def block(j, carry, masked=False):
                m, acc = carry
                sl = pl.ds(pl.multiple_of(j * BQ, BQ), BQ)
                st = _dg(k_ref[hh, sl, :], qb, NT)
                if masked:
                    st = jnp.where(causal, st, -jnp.inf)
                m_new = jnp.maximum(m, jnp.ceil(jnp.max(st, axis=0, keepdims=True)))
                p = jnp.exp2(st - m_new).astype(BF)
                return m_new, acc * jnp.exp2(m - m_new) + _dot(vt_ref[hh, :, sl], p)

            carry = lax.fori_loop(jmin_ref[2 * hp + hh, i] // 2, i, block, (m0, acc0))
            m, acc = block(i, carry, masked=True)
            linv = 1.0 / jnp.sum(jnp.where(row == FOX_DH, acc, 0.0), axis=0, keepdims=True)
            tile = acc * linv
            for n, piece in enumerate(_split3(m) + _split3(linv)):
                tile = jnp.where(row == 70 + n, piece.astype(F32), tile)
            tile = tile.T
            outs.append(tile)
            q2_ref[hh] = jnp.where(in_stat, jnp.where(lane <= 72, -tile, tile), qb.astype(F32)).astype(BF)
        o = jnp.where(lo, outs[0], pltpu.roll(outs[1], FOX_DH, 1))
        o_ref[...] = o
        y_ref[...] = (o * _sigmoid(g_ref[...])).astype(BF)

    blk = pl.BlockSpec((BQ, LANES), lambda hp, i, jm: (i, hp))
    qblk = pl.BlockSpec((2, BQ, LANES), lambda hp, i, jm: (hp, i, 0))
    full = pl.BlockSpec((2, S, LANES), lambda hp, i, jm: (hp, 0, 0))
    full_t = pl.BlockSpec((2, LANES, S), lambda hp, i, jm: (hp, 0, 0))
    return pl.pallas_call(
        body, name=name,
        grid_spec=pltpu.PrefetchScalarGridSpec(
            num_scalar_prefetch=1, grid=(HP, nq),
            in_specs=[qblk, full, full_t, pl.BlockSpec((BQ, LANES), lambda hp, i, jm: (i, 3 * HP + hp))],
            out_specs=[blk, blk, qblk]),
        out_shape=[jax.ShapeDtypeStruct((S, D), BF), jax.ShapeDtypeStruct((S, D), F32),
                   jax.ShapeDtypeStruct((H, S, LANES), BF)],
        compiler_params=_cp("parallel", "arbitrary"),
    )(jmin, qa, ka, vat, proj)


def _fox_bwd_prep(dy, o, proj, q2, *, name):
    S, D = dy.shape
    HP = D // LANES
    T = _pick(S, FOX_ROWS_PER_STEP, 16)

    def body(dy_ref, o_ref, g_ref, q2_ref, da_ref):
        lane = lax.broadcasted_iota(jnp.int32, (T, LANES), 1)
        lo = lane < FOX_DH
        in_linv = (lane >= 73) & (lane <= 75)
        linv = [jnp.sum(jnp.where(in_linv, q2_ref[hh].astype(F32), 0.0), axis=-1, keepdims=True) for hh in range(2)]
        u = (dy_ref[...] * _sigmoid(g_ref[...]) * jnp.where(lo, linv[0], linv[1])).astype(BF).astype(F32)
        prod = u * o_ref[...]
        d_lo = jnp.sum(jnp.where(lo, prod, 0.0), axis=-1, keepdims=True)
        d_hi = jnp.sum(jnp.where(lo, 0.0, prod), axis=-1, keepdims=True)
        for hh, delta in enumerate((d_lo, d_hi)):
            base = jnp.where(lo, u if hh == 0 else pltpu.roll(u, FOX_DH, 1), 0.0)
            da_ref[hh] = _lane_put(base, lane, 64, [-(p.astype(F32)) for p in _split3(delta)]).astype(BF)

    blk = pl.BlockSpec((T, LANES), lambda i, hp: (i, hp))
    aug = pl.BlockSpec((2, T, LANES), lambda i, hp: (hp, i, 0))
    return pl.pallas_call(
        body, name=name, grid=(S // T, HP),
        in_specs=[blk, blk, pl.BlockSpec((T, LANES), lambda i, hp: (i, 3 * HP + hp)), aug],
        out_specs=aug,
        out_shape=jax.ShapeDtypeStruct((2 * HP, S, LANES), BF),
        compiler_params=_cp("parallel", "arbitrary"),
    )(dy, o, proj, q2)


def _fox_bwd(imax, q2, ka, va, doa, *, name):
    H, S, _ = q2.shape
    B = _fox_block(S)
    nb = S // B

    def body(imax_ref, q_ref, do_ref, k_ref, v_ref, dq_ref, dk_ref, dv_ref, cs_ref):
        j = pl.program_id(1)
        end = imax_ref[pl.program_id(0), j] + 1

        @pl.when(j == 0)
        def _():
            dq_ref[...] = jnp.zeros_like(dq_ref)

        kb, vb = k_ref[...], v_ref[...]

        def step(i, carry, nblk=1):
            dk_acc, dv_acc, cs_acc = carry
            rows = nblk * B
            sl = pl.ds(pl.multiple_of(i * B, B), rows)
            qb, dob = q_ref[sl, :], do_ref[sl, :]
            s = _dg(qb, kb, NT)
            ahead = lax.broadcasted_iota(jnp.int32, (rows, B), 0) - lax.broadcasted_iota(jnp.int32, (rows, B), 1)
            pb = jnp.exp2(jnp.where(ahead >= (j - i) * B, s, -jnp.inf)).astype(BF)
            ds = pb.astype(F32) * _dg(dob, vb, NT)
            dsb = ds.astype(BF)
            cs_acc = cs_acc + jnp.sum(ds.reshape(rows // 8, 8, B), axis=0)
            dv_acc = dv_acc + _dg(pb, dob, TN)
            dk_acc = dk_acc + _dg(dsb, qb, TN)
            dq_ref[sl, :] += _dot(dsb, kb)
            return dk_acc, dv_acc, cs_acc

        zero = jnp.zeros((B, LANES), F32)
        carry = (zero, zero, jnp.zeros((8, B), F32))
        pos = j
        for U in FOX_BWD_TILES:
            n = (end - pos) // U
            carry = lax.fori_loop(0, n, lambda ii, c, pos=pos, U=U: step(pos + U * ii, c, nblk=U), carry)
            pos = pos + U * n
        dk_acc, dv_acc, cs_acc = carry
        dk_ref[...] = dk_acc
        dv_ref[...] = dv_acc
        cs_ref[...] = jnp.sum(cs_acc, axis=0, keepdims=True)

    full = pl.BlockSpec((None, S, LANES), lambda h, j, im: (h, 0, 0))
    blk = pl.BlockSpec((None, B, LANES), lambda h, j, im: (h, j, 0))
    return pl.pallas_call(
        body, name=name,
        grid_spec=pltpu.PrefetchScalarGridSpec(
            num_scalar_prefetch=1, grid=(H, nb),
            in_specs=[full, full, blk, blk],
            out_specs=[full, blk, blk, pl.BlockSpec((None, 1, B), lambda h, j, im: (h, 0, j))]),
        out_shape=[jax.ShapeDtypeStruct((H, S, LANES), F32)] * 3 + [jax.ShapeDtypeStruct((H, 1, S), F32)],
        compiler_params=_cp("parallel", "arbitrary"),
    )(imax, q2, doa, ka, va)


def _fox_bwd_post(dqa, dka, dva, proj, dy, o, qw2, kw2, *, name):
    S, D = dy.shape
    HP = D // LANES
    T = _pick(S, FOX_ROWS_PER_STEP, 16)

    def body(dq_ref, dk_ref, dv_ref, q_ref, k_ref, g_ref, dy_ref, o_ref, qw_ref, kw_ref, dp_ref, dqw_ref, dkw_ref):
        @pl.when((pl.program_id(0) == 0) & (pl.program_id(1) == 0))
        def _():
            dqw_ref[...] = jnp.zeros_like(dqw_ref)
            dkw_ref[...] = jnp.zeros_like(dkw_ref)

        lane = lax.broadcasted_iota(jnp.int32, (T, LANES), 1)
        lo = lane < FOX_DH

        def pair(ref):
            return jnp.where(lo, ref[0], pltpu.roll(ref[1], FOX_DH, 1))

        def norm_bwd(xv, w, dyn, dw_ref):
            r = lax.rsqrt(_pair_stats(xv * xv, lo) + EPS)
            xr = xv * r
            dw_ref[...] += jnp.sum(dyn * xr, axis=0, keepdims=True)
            u = dyn * w
            return r * (u - xr * _pair_stats(u * xr, lo))

        dp_ref[0] = norm_bwd(q_ref[...], qw_ref[...], pair(dq_ref) * 0.125, dqw_ref).astype(BF)
        dp_ref[1] = norm_bwd(k_ref[...], kw_ref[...], pair(dk_ref) * (1.0 / LOG2E), dkw_ref).astype(BF)
        dp_ref[2] = pair(dv_ref).astype(BF)
        sg = _sigmoid(g_ref[...])
        dp_ref[3] = (dy_ref[...] * o_ref[...] * sg * (1.0 - sg)).astype(BF)

    def part(p):
        return pl.BlockSpec((T, LANES), lambda i, hp: (i, p * HP + hp))

    aug = pl.BlockSpec((2, T, LANES), lambda i, hp: (hp, i, 0))
    blk = pl.BlockSpec((T, LANES), lambda i, hp: (i, hp))
    vec = pl.BlockSpec((1, LANES), lambda i, hp: (0, 0))
    return pl.pallas_call(
        body, name=name, grid=(S // T, HP),
        in_specs=[aug, aug, aug, part(0), part(1), part(3), blk, blk, vec, vec],
        out_specs=[pl.BlockSpec((4, T, LANES), lambda i, hp: (0, i, hp)), vec, vec],
        out_shape=[jax.ShapeDtypeStruct((5, S, D), BF), jax.ShapeDtypeStruct((1, LANES), F32),
                   jax.ShapeDtypeStruct((1, LANES), F32)],
        compiler_params=_cp("arbitrary", "arbitrary"),
    )(dqa, dka, dva, proj, proj, proj, dy, o, qw2, kw2)


def _fox_dfz(colsum, nheads, proj, bf_pad, dproj, *, name):
    S = colsum.shape[0]
    H = nheads
    D = dproj.shape[2]
    T = _pick(S, 256, 16)
    nb = S // T

    def body(cs_ref, fz_ref, b_ref, _, dp_ref, db_ref, carry):
        @pl.when(pl.program_id(0) == 0)
        def _():
            carry[...] = jnp.zeros_like(carry)
            db_ref[...] = jnp.zeros_like(db_ref)

        lane = lax.broadcasted_iota(jnp.int32, (T, LANES), 1)
        df = -cs_ref[...]
        triu = jnp.where(lax.broadcasted_iota(jnp.int32, (T, T), 0) <= lax.broadcasted_iota(jnp.int32, (T, T), 1),
                         1.0, 0.0).astype(BF)
        dlogf = _tri_dot(triu, df) + carry[...]
        carry[...] = _row_of(dlogf, lax.broadcasted_iota(jnp.int32, (T, LANES), 0), 0)
        dfz = jnp.where(lane < H, dlogf * _sigmoid(-(fz_ref[...] + b_ref[...])), 0.0)
        db_ref[...] += jnp.sum(dfz, axis=0, keepdims=True)
        dp_ref[...] = jnp.zeros_like(dp_ref)
        dp_ref[:, 0:LANES] = dfz.astype(BF)

    return pl.pallas_call(
        body, name=name, grid=(nb,),
        in_specs=[pl.BlockSpec((T, LANES), lambda i: (nb - 1 - i, 0)),
                  pl.BlockSpec((T, LANES), lambda i: (nb - 1 - i, 4 * D // LANES)),
                  pl.BlockSpec((1, LANES), lambda i: (0, 0)),
                  pl.BlockSpec(memory_space=pl.ANY)],
        out_specs=[pl.BlockSpec((None, T, D), lambda i: (4, nb - 1 - i, 0)), pl.BlockSpec((1, LANES), lambda i: (0, 0))],
        out_shape=[jax.ShapeDtypeStruct(dproj.shape, BF), jax.ShapeDtypeStruct((1, LANES), F32)],
        scratch_shapes=[pltpu.VMEM((1, LANES), F32)],
        input_output_aliases={3: 0},
        compiler_params=_cp("arbitrary"),
    )(colsum, proj, bf_pad, dproj)


def _mod_fwd(c16, w, b, *, name):
    L, D, N = w.shape
    tn = _pick(N, 512)

    def body(c_ref, w_ref, b_ref, o_ref):
        cv = c_ref[...]
        ca = (cv * _sigmoid(cv)).astype(BF)
        o_ref[...] = _dot(ca, w_ref[...].astype(BF)) + b_ref[...]

    return pl.pallas_call(
        body, name=name, grid=(L, N // tn),
        in_specs=[pl.BlockSpec((16, D), lambda l, j: (0, 0)), pl.BlockSpec((None, D, tn), lambda l, j: (l, 0, j)),
                  pl.BlockSpec((None, 1, tn), lambda l, j: (l, 0, j))],
        out_specs=pl.BlockSpec((None, 16, tn), lambda l, j: (l, 0, j)),
        out_shape=jax.ShapeDtypeStruct((L, 16, N), F32),
        compiler_params=_cp("parallel", "arbitrary"),
    )(c16, w, b)


def _mod_bwd(c16, dmod, *, name):
    L, _, N = dmod.shape
    D = c16.shape[1]
    tn = _pick(N, 512)

    def body(c_ref, d_ref, o_ref):
        cv = c_ref[...]
        ca = (cv * _sigmoid(cv)).astype(BF)
        o_ref[...] = _dg(ca, d_ref[...].astype(BF), TN)

    return pl.pallas_call(
        body, name=name, grid=(L, N // tn),
        in_specs=[pl.BlockSpec((16, D), lambda l, j: (0, 0)), pl.BlockSpec((None, 16, tn), lambda l, j: (l, 0, j))],
        out_specs=pl.BlockSpec((None, D, tn), lambda l, j: (l, 0, j)),
        out_shape=jax.ShapeDtypeStruct((L, D, N), F32),
        compiler_params=_cp("parallel", "arbitrary"),
    )(c16, dmod)


def _adamw_math(w, g, m, v):
    m = ADAM_B1 * m + (1.0 - ADAM_B1) * g
    v = ADAM_B2 * v + (1.0 - ADAM_B2) * (g * g)
    m_hat = m / (1.0 - ADAM_B1 ** ADAM_STEP)
    v_hat = v / (1.0 - ADAM_B2 ** ADAM_STEP)
    return -ADAM_LR * (m_hat / (jnp.sqrt(v_hat) + ADAM_EPS) + ADAM_WD * w), m, v


def _adamw(w, g, m, v, *, g_at=None, name):
    R, C = w.shape
    row0 = 0 if g_at is None else g_at[1]
    tr = min(math.gcd(row0, 256) if row0 else 256, -(-R // 8) * 8)
    g0 = row0 // tr
    if g_at is None:
        g_spec = pl.BlockSpec((tr, C), lambda i: (i, 0))
    else:
        g_spec = pl.BlockSpec((None, tr, C), lambda i: (g_at[0], g0 + i, 0))

    def body(w_ref, g_ref, m_ref, v_ref, d_ref, mo_ref, vo_ref):
        d, mn, vn = _adamw_math(w_ref[...], g_ref[...], m_ref[...], v_ref[...])
        d_ref[...] = d
        mo_ref[...] = mn
        vo_ref[...] = vn

    blk = pl.BlockSpec((tr, C), lambda i: (i, 0))
    return pl.pallas_call(
        body, name=name, grid=(pl.cdiv(R, tr),),
        in_specs=[blk, g_spec, blk, blk],
        out_specs=[blk, blk, blk],
        out_shape=[jax.ShapeDtypeStruct((R, C), F32)] * 3,
        compiler_params=_cp("parallel"),
    )(w, g, m, v)


def _sum_parts(parts, *, name):
    P, R, C = parts.shape

    def body(p_ref, o_ref):
        acc = p_ref[0]
        for p in range(1, P):
            acc = acc + p_ref[p]
        o_ref[...] = acc

    return pl.pallas_call(
        body, name=name, grid=(1,),
        in_specs=[pl.BlockSpec((P, R, C), lambda i: (0, 0, 0))],
        out_specs=pl.BlockSpec((R, C), lambda i: (0, 0)),
        out_shape=jax.ShapeDtypeStruct((R, C), F32),
        compiler_params=_cp("arbitrary"),
    )(parts)


def _add_halves(g4, recv, c_idx, *, name):
    _, _, Rh, C = g4.shape
    tr = min(256, Rh)

    def body(c_ref, a_ref, b_ref, o_ref):
        o_ref[...] = (a_ref[...] + b_ref[...].astype(F32)).astype(BF)

    return pl.pallas_call(
        body, name=name,
        grid_spec=pltpu.PrefetchScalarGridSpec(
            num_scalar_prefetch=1, grid=(4, pl.cdiv(Rh, tr)),
            in_specs=[pl.BlockSpec((None, None, tr, C), lambda j, r, c: (j, c[0], r, 0)),
                      pl.BlockSpec((None, tr, C), lambda j, r, c: (j, r, 0))],
            out_specs=pl.BlockSpec((None, tr, C), lambda j, r, c: (j, r, 0))),
        out_shape=jax.ShapeDtypeStruct((4, Rh, C), BF),
        compiler_params=_cp("parallel", "arbitrary"),
    )(c_idx, g4, recv)


def _add_four(g4, from_sibling, from_chips, pos, *, name):
    _, _, Rh, C = g4.shape
    tr = min(256, Rh)

    def body(p_ref, a_ref, s_ref, b_ref, o_ref):
        own = a_ref[...] + s_ref[...].astype(F32)
        o_ref[...] = ((own + b_ref[0].astype(F32)) + b_ref[1].astype(F32)) + b_ref[2].astype(F32)

    return pl.pallas_call(
        body, name=name,
        grid_spec=pltpu.PrefetchScalarGridSpec(
            num_scalar_prefetch=1, grid=(pl.cdiv(Rh, tr),),
            in_specs=[pl.BlockSpec((None, None, tr, C), lambda r, p: (p[0], p[1], r, 0)),
                      pl.BlockSpec((None, tr, C), lambda r, p: (p[0], r, 0)),
                      pl.BlockSpec((3, tr, C), lambda r, p: (0, r, 0))],
            out_specs=pl.BlockSpec((None, tr, C), lambda r, p: (p[1], r, 0))),
        out_shape=jax.ShapeDtypeStruct((2, Rh, C), F32),
        compiler_params=_cp("arbitrary"),
    )(pos, g4, from_sibling, from_chips)


HBM = pl.BlockSpec(memory_space=pltpu.HBM)


def _mesh_pos():
    return lax.axis_index("x"), lax.axis_index("y"), lax.axis_index("c")


def _other_chips(x, y):
    return [(1 - x, y), (x, 1 - y), (1 - x, 1 - y)]


def _allgather_small(xs, *, name):
    m_per, n = xs.shape

    def body(x_ref, out_ref, send_sems, recv_sems, local_sem):
        x, y, c = _mesh_pos()
        me, sibling = (x, y, c), (x, y, 1 - c)
        chips = _other_chips(x, y)

        def rows(px, py, pc):
            return out_ref.at[pl.ds((4 * px + 2 * py + pc) * m_per, m_per), :]

        def copy(k, block, to, src=None):
            return pltpu.make_async_remote_copy(
                src_ref=rows(*block) if src is None else src, dst_ref=rows(*block),
                send_sem=send_sems.at[k], recv_sem=recv_sems.at[k], device_id=to, device_id_type=MESH)

        mine = pltpu.make_async_copy(x_ref, rows(*me), local_sem)
        mine.start()
        first = [copy(0, me, sibling, src=x_ref)]
        first += [copy(1 + j, me, (*chip, c), src=x_ref) for j, chip in enumerate(chips)]
        for cp in first:
            cp.start()
        passed = [copy(4 + j, (*chip, c), sibling) for j, chip in enumerate(chips)]
        for j, chip in enumerate(chips):
            copy(1 + j, (*chip, c), me).wait_recv()
            passed[j].start()
        copy(0, sibling, me).wait_recv()
        for j, chip in enumerate(chips):
            copy(4 + j, (*chip, 1 - c), me).wait_recv()
        for cp in first + passed:
            cp.wait_send()
        mine.wait()

    return pl.pallas_call(
        body, name=name,
        out_shape=jax.ShapeDtypeStruct((N_DEV * m_per, n), xs.dtype),
        in_specs=[pl.BlockSpec(memory_space=pltpu.VMEM)],
        out_specs=pl.BlockSpec(memory_space=pltpu.VMEM),
        scratch_shapes=[pltpu.SemaphoreType.DMA((7,)), pltpu.SemaphoreType.DMA((7,)), pltpu.SemaphoreType.DMA],
    )(xs)


def _chip_slab_copies(s_ref, out_ref, send_sems, recv_sems):
    R = s_ref.shape[0]
    Rh = R // 2
    x, y, c = _mesh_pos()
    me, sibling = (x, y, c), (x, y, 1 - c)
    chips = _other_chips(x, y)

    def half(px, py, pc):
        return out_ref.at[2 * px + py, pl.ds(pc * Rh, Rh), :]

    def copy(k, block, to, src=None):
        return pltpu.make_async_remote_copy(
            src_ref=half(*block) if src is None else src, dst_ref=half(*block),
            send_sem=send_sems.at[k], recv_sem=recv_sems.at[k], device_id=to, device_id_type=MESH)

    first = [copy(j, me, (*chip, c), src=s_ref.at[pl.ds(c * Rh, Rh), :]) for j, chip in enumerate(chips)]
    passed = [copy(3 + j, (*chip, c), sibling) for j, chip in enumerate(chips)]
    landed = [copy(j, (*chip, c), me) for j, chip in enumerate(chips)]
    from_sibling = [copy(3 + j, (*chip, 1 - c), me) for j, chip in enumerate(chips)]
    return first, passed, landed, from_sibling


def _gather_behind(s_ref, out_ref, send_sems, recv_sems, step, nsteps):
    first, passed, landed, from_sibling = _chip_slab_copies(s_ref, out_ref, send_sems, recv_sems)

    @pl.when(step == 0)
    def _():
        for cp in first:
            cp.start()

    @pl.when(step == (3 * nsteps) // 4)
    def _():
        for arrived, onward in zip(landed, passed):
            arrived.wait_recv()
            onward.start()

    def finish():
        @pl.when(step == nsteps - 1)
        def _():
            for cp in from_sibling:
                cp.wait_recv()
            for cp in first + passed:
                cp.wait_send()

    return finish


def _allgather_chip_slabs(slab, *, name):
    R, C = slab.shape

    def body(s_ref, out_ref, send_sems, recv_sems):
        first, passed, landed, from_sibling = _chip_slab_copies(s_ref, out_ref, send_sems, recv_sems)
        for cp in first:
            cp.start()
        for arrived, onward in zip(landed, passed):
            arrived.wait_recv()
            onward.start()
        for cp in from_sibling:
            cp.wait_recv()
        for cp in first + passed:
            cp.wait_send()

    return pl.pallas_call(
        body, name=name,
        out_shape=jax.ShapeDtypeStruct((N_CHIPS, R, C), slab.dtype),
        in_specs=[HBM], out_specs=HBM,
        scratch_shapes=[pltpu.SemaphoreType.DMA((6,)), pltpu.SemaphoreType.DMA((6,))],
    )(slab)


def _swap_halves(mine, *, name):
    def body(g_ref, out_ref, send_sems, recv_sems):
        x, y, c = _mesh_pos()
        copies = [pltpu.make_async_remote_copy(
            src_ref=g_ref.at[j], dst_ref=out_ref.at[j], send_sem=send_sems.at[j], recv_sem=recv_sems.at[j],
            device_id=(x, y, 1 - c), device_id_type=MESH) for j in range(N_CHIPS)]
        for cp in copies:
            cp.start()
        for cp in copies:
            cp.wait()

    return pl.pallas_call(
        body, name=name,
        out_shape=jax.ShapeDtypeStruct(mine.shape, mine.dtype),
        in_specs=[HBM], out_specs=HBM,
        scratch_shapes=[pltpu.SemaphoreType.DMA((N_CHIPS,)), pltpu.SemaphoreType.DMA((N_CHIPS,))],
    )(mine)


def _scatter_copies(p_ref, out_ref, send_sems, recv_sems):
    x, y, c = _mesh_pos()
    return [pltpu.make_async_remote_copy(
        src_ref=p_ref.at[2 * px + py], dst_ref=out_ref.at[j], send_sem=send_sems.at[j], recv_sem=recv_sems.at[j],
        device_id=(px, py, c), device_id_type=MESH) for j, (px, py) in enumerate(_other_chips(x, y))]


def _join_halves(buf, *, name):
    def body(b_ref, out_ref, send_sem, recv_sem):
        x, y, c = _mesh_pos()
        cp = pltpu.make_async_remote_copy(
            src_ref=b_ref.at[c], dst_ref=out_ref.at[c], send_sem=send_sem, recv_sem=recv_sem,
            device_id=(x, y, 1 - c), device_id_type=MESH)
        cp.start()
        cp.wait()

    return pl.pallas_call(
        body, name=name,
        out_shape=jax.ShapeDtypeStruct(buf.shape, buf.dtype),
        in_specs=[HBM], out_specs=HBM, input_output_aliases={0: 0},
        scratch_shapes=[pltpu.SemaphoreType.DMA, pltpu.SemaphoreType.DMA],
    )(buf)


def _pad_rows(a, mult):
    pad = (-a.shape[0]) % mult
    return a if pad == 0 else jnp.pad(a, ((0, pad),) + ((0, 0),) * (a.ndim - 1))


def _local_step(x, target, mod, wts, small, slabs=None, unpacks=None, reduce_early=None, grad_slab=None,
                reduce_late=None):
    S, D = x.shape
    HP = D // LANES
    row = lambda v: v.reshape(1, -1)
    msplit = [[row(mod[i, k * D:(k + 1) * D]) for k in range(6)] for i in range(2)]
    gw, gs = {}, {}
    dmod = [[None] * 6 for _ in range(2)]
    slab, where = grad_slab if grad_slab is not None else (None, {})

    def dw(key, a, b, name):
        nonlocal slab
        if key in where:
            slab = _matmul_tn(a, b, name=name, into=(slab,) + where[key])
        else:
            gw[key] = _matmul_tn(a, b, name=name)

    sh1, sc1, g1, sh2, sc2, g2 = msplit[0]
    n1w0, n2w0 = row(small["norm1_w"][0]), row(small["norm2_w"][0])
    slabs = slabs if slabs is not None else (None, None)
    proj0, h1_0, *gathered = _ln_matmul(x, n1w0, sc1, sh1, wts["hg_w_in"], slabs[0], relu2=False, name="hg_in_proj")
    if slabs[0] is not None:
        wts = {**wts, **unpacks[0](gathered[0])}
    gn = small["hg_gn_w"].reshape(1, LANES)
    ypre0, o0, states, *gathered = _hg_fwd(proj0, small["hg_lb"], gn, slabs[1], name="hg_fwd")
    if slabs[1] is not None:
        wts = {**wts, **unpacks[1](gathered[0])}
    x1, ymix0 = _matmul_resid(ypre0, wts["hg_w_out"], x, g1, name="hg_out_proj")
    a0, u0, h2_0 = _ln_matmul(x1, n2w0, sc2, sh2, wts["mlp_w1_0"], relu2=True, name="mlp0_up")
    x2, ymlp0 = _matmul_resid(u0, wts["mlp_w2_0"], x1, g2, name="mlp0_down")

    sh1b, sc1b, g1b, sh2b, sc2b, g2b = msplit[1]
    n1w1, n2w1 = row(small["norm1_w"][1]), row(small["norm2_w"][1])
    proj1, h1_1 = _ln_matmul(x2, n1w1, sc1b, sh1b, wts["fox_w_in"], relu2=False, name="fox_in_proj")
    nheads = 2 * HP
    bf_pad = jnp.pad(small["fox_b_f"].reshape(1, nheads), ((0, 0), (0, LANES - nheads)))
    qw2 = jnp.tile(small["fox_qn_w"].reshape(1, FOX_DH), (1, 2))
    kw2 = jnp.tile(small["fox_kn_w"].reshape(1, FOX_DH), (1, 2))
    fcum = _fox_cumsum(proj1, bf_pad, name="fox_cumsum")
    qa, ka, va, vat = _fox_prep(proj1, fcum, qw2, kw2, name="fox_prep")
    jmin, imax = _fox_skip_bounds(fcum, small["fox_qn_w"], small["fox_kn_w"], nheads)
    ypre1, o1, q2 = _fox_fwd(jmin, qa, ka, vat, proj1, name="fox_fwd")
    x3, ymix1 = _matmul_resid(ypre1, wts["fox_w_out"], x2, g1b, name="fox_out_proj")
    a1, u1, h2_1 = _ln_matmul(x3, n2w1, sc2b, sh2b, wts["mlp_w1_1"], relu2=True, name="mlp1_up")
    x4, ymlp1 = _matmul_resid(u1, wts["mlp_w2_1"], x3, g2b, name="mlp1_down")

    loss, dx4, dfw = _loss_kernel(x4, row(small["final_w"]), target, name="loss")
    gs["final_w"] = dfw.reshape(-1)

    def mlp_bwd(i, dx_out, x_in, h2, a, u, ymlp, n2w, sc2_, g2_):
        dz, dm, dg2 = _gate_matmul_nt(dx_out, g2_, ymlp, wts[f"mlp_w2_{i}"], a, name=f"mlp{i}_down_bwd")
        dw(f"mlp_w2_{i}", u, dm[None], f"mlp{i}_dw2")
        dw(f"mlp_w1_{i}", h2, dz[None], f"mlp{i}_dw1")
        dx_in, dsc, dsh, dnw = _matmul_nt_lnbwd(dz[None], wts[f"mlp_w1_{i}"], x_in, n2w, sc2_, dx_out,
                                                name=f"mlp{i}_up_bwd")
        dmod[i][3], dmod[i][4], dmod[i][5] = dsh, dsc, dg2
        return dx_in, dnw

    dx3, dn2w1 = mlp_bwd(1, dx4, x3, h2_1, a1, u1, ymlp1, n2w1, sc2b, g2b)
    dyp1, dm1, dg1b = _gate_matmul_nt(dx3, g1b, ymix1, wts["fox_w_out"], None, name="fox_out_bwd")
    dw("fox_w_out", ypre1, dm1[None], "fox_dw_out")
    doa = _fox_bwd_prep(dyp1, o1, proj1, q2, name="fox_bwd_prep")
    dqa, dka, dva, colsum = _fox_bwd(imax, q2, ka, va, doa, name="fox_bwd")
    colsum = jnp.pad(colsum[:, 0, :].T, ((0, 0), (0, LANES - nheads)))
    dproj1, dqw, dkw = _fox_bwd_post(dqa, dka, dva, proj1, dyp1, o1, qw2, kw2, name="fox_bwd_post")
    dproj1, dbf = _fox_dfz(colsum, nheads, proj1, bf_pad, dproj1, name="fox_dfz")
    dw("fox_w_in", h1_1, dproj1, "fox_dw_in")
    dx2, dsc, dsh, dn1w1 = _matmul_nt_lnbwd(dproj1, wts["fox_w_in"], x2, n1w1, sc1b, dx3, name="fox_in_bwd")
    dmod[1][0], dmod[1][1], dmod[1][2] = dsh, dsc, dg1b
    gs["fox_qn_w"] = dqw[0, :FOX_DH] + dqw[0, FOX_DH:]
    gs["fox_kn_w"] = dkw[0, :FOX_DH] + dkw[0, FOX_DH:]
    gs["fox_b_f"] = dbf[0, :nheads]

    dx1, dn2w0 = mlp_bwd(0, dx2, x1, h2_0, a0, u0, ymlp0, n2w0, sc2, g2)
    dyp0, dm0, dg1 = _gate_matmul_nt(dx1, g1, ymix0, wts["hg_w_out"], None, name="hg_out_bwd")
    dw("hg_w_out", ypre0, dm0[None], "hg_dw_out")
    part, ctx = reduce_early(gw, slab) if reduce_early is not None else (None, None)
    dproj0, dlb, dgn, *from_chips = _hg_bwd(proj0, small["hg_lb"], gn, o0, states, dyp0, part, name="hg_bwd")
    early = (ctx, from_chips[0]) if reduce_early is not None else None
    dw("hg_w_in", h1_0, dproj0, "hg_dw_in")
    part, ctx = reduce_late(gw) if reduce_late is not None else (None, None)
    dx0, dsc, dsh, dn1w0, *from_chips = _matmul_nt_lnbwd(dproj0, wts["hg_w_in"], x, n1w0, sc1, dx1, part, name="hg_in_bwd")
    late = (ctx, from_chips[0]) if reduce_late is not None else None
    dmod[0][0], dmod[0][1], dmod[0][2] = dsh, dsc, dg1
    gs["hg_lb"] = dlb
    gs["hg_gn_w"] = jnp.sum(dgn, axis=0)

    gs["norm1_w"] = jnp.concatenate([dn1w0, dn1w1], axis=0)
    gs["norm2_w"] = jnp.concatenate([dn2w0, dn2w1], axis=0)
    gs["dmod"] = jnp.stack([jnp.concatenate(dmod[i], axis=1)[0] for i in range(2)])
    return loss, dx0, gw, gs, early, late


def _pack_halves(layout):
    rh = -(-max(sum(a.shape[0] for _, a in half) for half in layout) // 16) * 16
    place, parts = {}, []
    for h, half in enumerate(layout):
        off = 0
        for n, a in half:
            place[n] = (h, off, a.shape[0])
            off += a.shape[0]
        parts.append(jnp.pad(jnp.concatenate([a.astype(BF) for _, a in half], axis=0), ((0, rh - off), (0, 0))))
    return jnp.concatenate(parts, axis=0), place, rh


SMALL_NAMES = ["norm1_w", "norm2_w", "hg_lb", "hg_gn_w", "fox_b_f", "fox_qn_w", "fox_kn_w", "final_w"]


def _pack_small(d, names):
    rows, offs, r0 = [], {}, 0
    for n in names:
        flat = d[n].reshape(-1)
        nr = -(-flat.shape[0] // LANES)
        rows.append(jnp.pad(flat, (0, nr * LANES - flat.shape[0])).reshape(nr, LANES))
        offs[n] = (r0, nr)
        r0 += nr
    return jnp.concatenate(rows, axis=0), offs


def _unpack_small(packed, offs, name, like):
    r0, nr = offs[name]
    return packed[r0:r0 + nr].reshape(-1)[:like.size].reshape(like.shape)


def kernel(x, c, w_mod, b_mod, norm1_w, norm2_w, hg_w_in, hg_w_out, hg_lb, hg_gn_w, fox_w_in, fox_b_f, fox_qn_w, fox_kn_w, fox_w_out, mlp_w1, mlp_w2, final_w, loss_target, m_w_mod, m_b_mod, m_norm1_w, m_norm2_w, m_hg_w_in, m_hg_w_out, m_hg_lb, m_hg_gn_w, m_fox_w_in, m_fox_b_f, m_fox_qn_w, m_fox_kn_w, m_fox_w_out, m_mlp_w1, m_mlp_w2, m_final_w, v_w_mod, v_b_mod, v_norm1_w, v_norm2_w, v_hg_w_in, v_hg_w_out, v_hg_lb, v_hg_gn_w, v_fox_w_in, v_fox_b_f, v_fox_qn_w, v_fox_kn_w, v_fox_w_out, v_mlp_w1, v_mlp_w2, v_final_w):
    S, D = x.shape[1], x.shape[2]
    nheads = D // FOX_DH
    ax, ay, ac = _mesh_pos()
    chip = 2 * ax + ay
    dev = 2 * chip + ac
    xs, tgt = x.reshape(S, D), loss_target.reshape(S, D)

    c_all = _allgather_small(_pad_rows(c.reshape(-1, LANES), 8), name="gather_c")
    c_all = c_all.reshape(N_DEV, -1)[:, :D]
    c16 = _pad_rows(c_all, 16)
    nmod = w_mod.shape[2]
    b_shard = lax.dynamic_slice_in_dim(b_mod, chip * nmod, nmod, axis=1)
    mod_shard = _mod_fwd(c16, w_mod, b_shard[:, None, :], name="mod_fwd")[:, :N_DEV]
    mod_all = _allgather_small(mod_shard.reshape(-1, LANES), name="gather_mod")
    mod_all = mod_all.reshape(N_CHIPS, 2, 2, N_DEV, nmod)[:, 0]
    mod = lax.dynamic_index_in_dim(mod_all, dev, axis=2, keepdims=False)
    mod = mod.transpose(1, 0, 2).reshape(2, N_CHIPS * nmod)

    fox_rows = fox_w_in.shape[2]
    col = lambda g: g.transpose(1, 0, 2).reshape(g.shape[1], -1)
    rowsh = lambda g: g.reshape(-1, g.shape[2])
    own = lambda g, s: lax.dynamic_update_index_in_dim(g, s, chip, 0)

    slab_in = hg_w_in[0].astype(BF)
    wts = {"hg_w_in": col(own(_allgather_chip_slabs(slab_in, name="gather_hg_w_in"), slab_in))}
    fox_flat, fox_cut = fox_w_in[0].reshape(fox_rows, D), fox_rows // 2
    slabs, unpacks = [], []
    for layout_w in ([[("mlp_w1_0", mlp_w1[0]), ("hg_w_out", hg_w_out[0])], [("mlp_w2_0", mlp_w2[0]), ("fox_w_out", fox_w_out[0])]],
                     [[("mlp_w1_1", mlp_w1[1]), ("fox_a", fox_flat[:fox_cut])], [("mlp_w2_1", mlp_w2[1]), ("fox_b", fox_flat[fox_cut:])]]):
        slab_w, place_w, rh_w = _pack_halves(layout_w)

        def unpack(gathered, slab_w=slab_w, place_w=place_w, rh_w=rh_w):
            gathered = own(gathered, slab_w)
            out = {}
            for n, (h, off, rows) in place_w.items():
                g = gathered[:, h * rh_w + off:h * rh_w + off + rows, :]
                out[n] = col(g) if n.startswith("mlp_w1") else rowsh(g) if n.startswith(("mlp_w2", "hg_", "fox_w")) else g
            if "fox_a" in out:
                fox_in = col(jnp.concatenate([out.pop("fox_a"), out.pop("fox_b")], axis=1).reshape(N_CHIPS, D, fox_rows))
                out["fox_w_in"] = jnp.pad(fox_in, ((0, 0), (0, 5 * D - fox_in.shape[1])))
            return out

        slabs.append(slab_w)
        unpacks.append(unpack)

    small = {"norm1_w": norm1_w, "norm2_w": norm2_w, "hg_lb": hg_lb, "hg_gn_w": hg_gn_w, "fox_b_f": fox_b_f,
             "fox_qn_w": fox_qn_w, "fox_kn_w": fox_kn_w, "final_w": final_w}

    def uncol(g, n):
        return g.reshape(g.shape[0], N_CHIPS, n).transpose(1, 0, 2)

    pos = jnp.stack([chip, ac])

    def swap_and_add(g4, tag):
        to_sibling = lax.dynamic_index_in_dim(g4, 1 - ac, axis=1, keepdims=False).astype(BF)
        from_sibling = _swap_halves(to_sibling, name=f"rs_swap_{tag}")
        return from_sibling, _add_halves(g4, from_sibling, ac.reshape(1), name=f"rs_add_halves_{tag}")

    def finish(g4, from_sibling, from_chips, tag):
        my_half = _add_four(g4, from_sibling, from_chips, pos, name=f"rs_add_chips_{tag}")
        return _join_halves(my_half, name=f"rs_join_{tag}")

    layout = [[("mlp_w1", 2 * D), ("hg_w_out", D // 4), ("fox_w_out", D // 4)], [("mlp_w2", 2 * D), ("fox_w_in", fox_rows)]]
    place = {}
    for h, half in enumerate(layout):
        off = 0
        for n, rows in half:
            place[n] = (h, off, rows)
            off += rows

    rh = -(-max(sum(rows for _, rows in half) for half in layout) // 16) * 16
    where = {"hg_w_out": ("row",) + place["hg_w_out"][:2], "fox_w_out": ("row",) + place["fox_w_out"][:2]}
    for i in range(2):
        where[f"mlp_w1_{i}"] = ("col", place["mlp_w1"][0], place["mlp_w1"][1] + i * D)
        where[f"mlp_w2_{i}"] = ("row", place["mlp_w2"][0], place["mlp_w2"][1] + i * D)

    def reduce_early(gw, slab):
        gfox = uncol(gw["fox_w_in"][:, :4 * fox_rows], fox_rows).reshape(N_CHIPS, 1, fox_rows, D)
        h, off, _ = place["fox_w_in"]
        slab = lax.dynamic_update_slice(slab, gfox, (0, h, off, 0))
        for h, half in enumerate(layout):
            used = sum(rows for _, rows in half)
            if used < rh:
                slab = lax.dynamic_update_slice(slab, jnp.zeros((N_CHIPS, 1, rh - used, D), F32), (0, h, used, 0))
        from_sibling, part = swap_and_add(slab, "early")
        return part, (slab, from_sibling)

    def reduce_late(gw):
        g4 = uncol(gw["hg_w_in"], D).reshape(N_CHIPS, 2, D // 2, D)
        from_sibling, part = swap_and_add(g4, "late")
        return part, (g4, from_sibling)

    loss_part, grad_x, gw, gs, (early, from_chips_early), (late, from_chips_late) = _local_step(
        xs, tgt, mod, wts, small, slabs, unpacks, reduce_early, (lax.empty((N_CHIPS, 2, rh, D), F32), where), reduce_late)
    gshard = finish(*early, from_chips_early, "early")
    g_hg_w_in = finish(*late, from_chips_late, "late").reshape(D, D)

    names = ["dmod", "loss"] + SMALL_NAMES
    packed, offs = _pack_small({**gs, "loss": loss_part[0, :1]}, names)
    packed = _pad_rows(packed, 8)
    rp = packed.shape[0]
    parts = _allgather_small(packed, name="gather_small").reshape(N_DEV, rp, LANES)
    total = _sum_parts(parts, name="sum_small")
    r0, nr = offs["dmod"]
    dmod_all = parts[:, r0:r0 + nr].reshape(N_DEV, 2, N_CHIPS * nmod)
    dmod_shard = lax.dynamic_slice_in_dim(dmod_all, chip * nmod, nmod, axis=2).transpose(1, 0, 2)
    g_w_mod = _mod_bwd(c16, jnp.pad(dmod_shard, ((0, 0), (0, 16 - N_DEV), (0, 0))), name="mod_bwd")

    loss = _unpack_small(total, offs, "loss", loss_part[0, :1]).reshape(())
    grads = {"w_mod": g_w_mod, "b_mod": _unpack_small(total, offs, "dmod", b_mod)}
    for n in SMALL_NAMES:
        grads[n] = _unpack_small(total, offs, n, small[n])

    given = dict(w_mod=(w_mod, m_w_mod, v_w_mod), b_mod=(b_mod, m_b_mod, v_b_mod), norm1_w=(norm1_w, m_norm1_w, v_norm1_w),
                 norm2_w=(norm2_w, m_norm2_w, v_norm2_w), hg_w_in=(hg_w_in, m_hg_w_in, v_hg_w_in),
                 hg_w_out=(hg_w_out, m_hg_w_out, v_hg_w_out), hg_lb=(hg_lb, m_hg_lb, v_hg_lb),
                 hg_gn_w=(hg_gn_w, m_hg_gn_w, v_hg_gn_w), fox_w_in=(fox_w_in, m_fox_w_in, v_fox_w_in),
                 fox_b_f=(fox_b_f, m_fox_b_f, v_fox_b_f), fox_qn_w=(fox_qn_w, m_fox_qn_w, v_fox_qn_w),
                 fox_kn_w=(fox_kn_w, m_fox_kn_w, v_fox_kn_w), fox_w_out=(fox_w_out, m_fox_w_out, v_fox_w_out),
                 mlp_w1=(mlp_w1, m_mlp_w1, v_mlp_w1), mlp_w2=(mlp_w2, m_mlp_w2, v_mlp_w2), final_w=(final_w, m_final_w, v_final_w))
    upd = {}

    for n, (h, off, rows) in place.items():
        w, m, v = given[n]
        flat = lambda a: a.reshape(rows, D)
        d, mn, vn = _adamw(flat(w), gshard, flat(m), flat(v), g_at=(h, off), name=f"adamw_{n}")
        grads[n] = gshard[h, off:off + rows].reshape(w.shape)
        upd[n] = tuple(a.reshape(w.shape) for a in (d, mn, vn))

    w, m, v = given["hg_w_in"]
    grads["hg_w_in"] = g_hg_w_in.reshape(w.shape)
    upd["hg_w_in"] = tuple(a.reshape(w.shape) for a in _adamw(w[0], g_hg_w_in, m[0], v[0], name="adamw_hg_w_in"))

    w, m, v = given["w_mod"]
    flat = lambda a: a.reshape(-1, nmod)
    upd["w_mod"] = tuple(a.reshape(w.shape) for a in _adamw(flat(w), flat(g_w_mod), flat(m), flat(v), name="adamw_w_mod"))

    snames = ["b_mod"] + SMALL_NAMES
    pw, soffs = _pack_small({n: given[n][0] for n in snames}, snames)
    pm, _ = _pack_small({n: given[n][1] for n in snames}, snames)
    pv, _ = _pack_small({n: given[n][2] for n in snames}, snames)
    pg, _ = _pack_small({n: grads[n] for n in snames}, snames)
    pw, pm, pv, pg = (_pad_rows(a, 8) for a in (pw, pm, pv, pg))
    sd, smn, svn = _adamw(pw, pg, pm, pv, name="adamw_small")
    for n in snames:
        like = given[n][0]
        upd[n] = tuple(_unpack_small(a, soffs, n, like) for a in (sd, smn, svn))

    order = ["w_mod", "b_mod", "norm1_w", "norm2_w", "hg_w_in", "hg_w_out", "hg_lb", "hg_gn_w", "fox_w_in", "fox_b_f",
             "fox_qn_w", "fox_kn_w", "fox_w_out", "mlp_w1", "mlp_w2", "final_w"]
    return (loss, grad_x.reshape(x.shape), *[grads[n] for n in order], *[upd[n][0] for n in order],
            *[upd[n][1] for n in order], *[upd[n][2] for n in order])
```

```python
import math

import jax
import jax.numpy as jnp
from jax import lax
from jax.experimental import pallas as pl
from jax.experimental.pallas import tpu as pltpu

EPS = 1e-6
ADAM_LR, ADAM_B1, ADAM_B2, ADAM_EPS, ADAM_WD, ADAM_STEP = 0.001, 0.9, 0.999, 1e-08, 0.01, 10

F32 = jnp.float32
BF = jnp.bfloat16
LANES = 128
HG_CHUNK = 64
HG_HEADS_PER_STEP = 8
HG_TOKENS_PER_STEP = 256
FOX_ROWS_PER_STEP = 1024
FOX_BWD_TILES = (8, 4, 2, 1)
LOG2E = 1.4426950408889634
FOX_DH = 64
N_CHIPS = 4
N_DEV = 8
VMEM_LIMIT = 56 * 1024 * 1024
MESH = pl.DeviceIdType.MESH

NT = (((1,), (1,)), ((), ()))
TN = (((0,), (0,)), ((), ()))


def _pick(n, pref, mult=LANES):
    if n <= pref:
        return n
    t = (pref // mult) * mult
    while t >= mult:
        if n % t == 0:
            return t
        t -= mult
    raise ValueError((n, pref, mult))


def _cp(*sem):
    return pltpu.CompilerParams(dimension_semantics=sem, vmem_limit_bytes=VMEM_LIMIT)


def _dot(a, b):
    return jnp.dot(a, b, preferred_element_type=F32)


def _dg(a, b, dims):
    return lax.dot_general(a, b, dims, preferred_element_type=F32)


def _split3(x):
    hi = x.astype(BF)
    r1 = x - hi.astype(F32)
    mid = r1.astype(BF)
    lo = (r1 - mid.astype(F32)).astype(BF)
    return hi, mid, lo


def _tri_dot(tri, x):
    hi, mid, lo = _split3(x)
    return _dot(tri, hi) + _dot(tri, mid) + _dot(tri, lo)


def _dg3(a, b, dims):
    ah, bh = a.astype(BF), b.astype(BF)
    al, bl = (a - ah.astype(F32)).astype(BF), (b - bh.astype(F32)).astype(BF)
    return _dg(ah, bh, dims) + _dg(ah, bl, dims) + _dg(al, bh, dims)


def _dg1(a, b, dims):
    return _dg(a.astype(BF), b.astype(BF), dims)


NN = (((1,), (0,)), ((), ()))


def _sigmoid(x):
    return jax.nn.sigmoid(x)


def _ln_matmul(x, nw, sc, sh, w, slab=None, *, relu2, name):
    S, D = x.shape
    N = w.shape[1]
    tm, tn = _pick(S, 512, 16), N
    fused = slab is not None

    def body(x_ref, nw_ref, sc_ref, sh_ref, w_ref, *rest):
        if fused:
            s_ref, *outs, out_ref, hs, send_sems, recv_sems = rest
            finish = _gather_behind(s_ref, out_ref, send_sems, recv_sems, pl.program_id(0), S // tm)
        else:
            outs, hs = rest[:-1], rest[-1]
        h_ref = outs[-1]

        @pl.when(pl.program_id(1) == 0)
        def _():
            xv = x_ref[...]
            r = lax.rsqrt(jnp.mean(xv * xv, axis=-1, keepdims=True) + EPS)
            hb = ((xv * r * nw_ref[...]) * (1.0 + sc_ref[...]) + sh_ref[...]).astype(BF)
            hs[...] = hb
            h_ref[...] = hb

        z = _dot(hs[...], w_ref[...])
        if relu2:
            a = jnp.maximum(z, 0.0)
            outs[0][...] = a.astype(BF)
            outs[1][...] = (a * a).astype(BF)
        else:
            outs[0][...] = z
        if fused:
            finish()

    vec = pl.BlockSpec((1, D), lambda i, j: (0, 0))
    tile = pl.BlockSpec((tm, tn), lambda i, j: (i, j))
    if relu2:
        out_shape = [jax.ShapeDtypeStruct((S, N), BF), jax.ShapeDtypeStruct((S, N), BF)]
        out_specs = [tile, tile]
    else:
        out_shape = [jax.ShapeDtypeStruct((S, N), F32)]
        out_specs = [tile]
    out_shape.append(jax.ShapeDtypeStruct((S, D), BF))
    out_specs.append(pl.BlockSpec((tm, D), lambda i, j: (i, 0)))
    in_specs = [pl.BlockSpec((tm, D), lambda i, j: (i, 0)), vec, vec, vec, pl.BlockSpec((D, tn), lambda i, j: (0, j))]
    scratch = [pltpu.VMEM((tm, D), BF)]
    args = [x, nw, sc, sh, w]
    if fused:
        in_specs.append(HBM)
        out_specs.append(HBM)
        out_shape.append(jax.ShapeDtypeStruct((N_CHIPS,) + slab.shape, slab.dtype))
        scratch += [pltpu.SemaphoreType.DMA((6,)), pltpu.SemaphoreType.DMA((6,))]
        args.append(slab)
    return pl.pallas_call(
        body, name=name, grid=(S // tm, N // tn), in_specs=in_specs, out_specs=out_specs, out_shape=out_shape,
        scratch_shapes=scratch, compiler_params=_cp("arbitrary", "arbitrary"),
    )(*args)


def _matmul_resid(a, w, x, gate, *, name):
    S, K = a.shape
    D = w.shape[1]
    tm, tn = _pick(S, 1024 if K <= 1024 else 512, 16), D

    def body(a_ref, w_ref, x_ref, g_ref, o_ref, y_ref):
        y = _dot(a_ref[...], w_ref[...])
        y_ref[...] = y.astype(BF)
        o_ref[...] = x_ref[...] + g_ref[...] * y

    tile = pl.BlockSpec((tm, tn), lambda i, j: (i, j))
    return pl.pallas_call(
        body, name=name, grid=(S // tm, D // tn),
        in_specs=[pl.BlockSpec((tm, K), lambda i, j: (i, 0)), pl.BlockSpec((K, tn), lambda i, j: (0, j)),
                  tile, pl.BlockSpec((1, tn), lambda i, j: (0, j))],
        out_specs=[tile, tile],
        out_shape=[jax.ShapeDtypeStruct((S, D), F32), jax.ShapeDtypeStruct((S, D), BF)],
        compiler_params=_cp("parallel", "arbitrary"),
    )(a, w, x, gate)


def _gate_matmul_nt(dx, gate, y, w, act, *, name):
    S, D = dx.shape
    K = w.shape[0]
    tm, tn = _pick(S, 1024 if K <= 1024 else 512, 16), K
    fused = act is not None

    def body(dx_ref, g_ref, y_ref, w_ref, *rest):
        if fused:
            act_ref, da_ref, dm_ref, dg_ref, ms = rest
        else:
            da_ref, dm_ref, dg_ref, ms = rest
        i, j = pl.program_id(0), pl.program_id(1)

        @pl.when((i == 0) & (j == 0))
        def _():
            dg_ref[...] = jnp.zeros_like(dg_ref)

        @pl.when(j == 0)
        def _():
            dxv = dx_ref[...]
            dmb = (dxv * g_ref[...]).astype(BF)
            ms[...] = dmb
            dm_ref[...] = dmb
            dg_ref[...] += jnp.sum(dxv * y_ref[...].astype(F32), axis=0, keepdims=True)

        da = _dg(ms[...], w_ref[...], NT)
        if fused:
            da_ref[...] = (da * (2.0 * act_ref[...].astype(F32))).astype(BF)
        else:
            da_ref[...] = da

    row = pl.BlockSpec((tm, D), lambda i, j: (i, 0))
    vec = pl.BlockSpec((1, D), lambda i, j: (0, 0))
    tile = pl.BlockSpec((tm, tn), lambda i, j: (i, j))
    in_specs = [row, vec, row, pl.BlockSpec((tn, D), lambda i, j: (j, 0))]
    args = [dx, gate, y, w]
    if fused:
        in_specs.append(tile)
        args.append(act)
    return pl.pallas_call(
        body, name=name, grid=(S // tm, K // tn),
        in_specs=in_specs, out_specs=[tile, row, vec],
        out_shape=[jax.ShapeDtypeStruct((S, K), BF if fused else F32), jax.ShapeDtypeStruct((S, D), BF),
                   jax.ShapeDtypeStruct((1, D), F32)],
        scratch_shapes=[pltpu.VMEM((tm, D), BF)],
        compiler_params=_cp("arbitrary", "arbitrary"),
    )(*args)


def _matmul_tn(a, b, *, name, into=None):
    S, Ka = a.shape
    P, _, Db = b.shape
    tk, tn, ts = _pick(Ka, 1024), _pick(Db, 1024), _pick(S, 1024, 16)
    if into is not None:
        slab, kind, half, off = into
        C = tn = slab.shape[3]
        per_chip = Ka // N_CHIPS
        all_chips = kind == "row" and tk == Ka
        if kind == "row" and not all_chips:
            tk = min(tk, per_chip)
        assert tn == C and P * Db == (N_CHIPS * C if kind == "col" else C)
        if kind == "col":
            assert tk == Ka and off % tk == 0
        elif all_chips:
            assert off % per_chip == 0
        else:
            assert per_chip % tk == 0 and off % tk == 0
    npb = Db // tn

    def body(a_ref, b_ref, *rest):
        o_ref, acc = rest[-2:]
        s = pl.program_id(2)

        @pl.when(s == 0)
        def _():
            acc[...] = jnp.zeros_like(acc)

        acc[...] += _dg(a_ref[...], b_ref[...], TN)

        @pl.when(s == pl.num_programs(2) - 1)
        def _():
            o_ref[...] = acc[...].reshape(o_ref.shape)

    in_specs = [pl.BlockSpec((ts, tk), lambda i, j, s: (s, i)),
                pl.BlockSpec((None, ts, tn), lambda i, j, s: (j // npb, s, j % npb))]
    args = [a, b]
    if into is None:
        out_spec = pl.BlockSpec((tk, tn), lambda i, j, s: (i, j))
        out_shape = jax.ShapeDtypeStruct((Ka, P * Db), F32)
        aliases = {}
    else:
        per = per_chip // tk if kind == "row" and not all_chips else 1
        if kind == "col":
            out_spec = pl.BlockSpec((None, None, tk, tn), lambda i, j, s: (j, half, off // tk + i, 0))
        elif all_chips:
            out_spec = pl.BlockSpec((N_CHIPS, None, per_chip, tn), lambda i, j, s: (0, half, off // per_chip, 0))
        else:
            out_spec = pl.BlockSpec((None, None, tk, tn), lambda i, j, s: (i // per, half, off // tk + i % per, 0))
        out_shape = jax.ShapeDtypeStruct(slab.shape, F32)
        in_specs.append(pl.BlockSpec(memory_space=pl.ANY))
        args.append(slab)
        aliases = {2: 0}
    return pl.pallas_call(
        body, name=name, grid=(Ka // tk, P * npb, S // ts),
        in_specs=in_specs, out_specs=out_spec, out_shape=out_shape,
        scratch_shapes=[pltpu.VMEM((tk, tn), F32)], input_output_aliases=aliases,
        compiler_params=_cp("parallel", "parallel", "arbitrary"),
    )(*args)


def _matmul_nt_lnbwd(g, w, x, nw, sc, dx_out, part=None, *, name):
    P, S, Dg = g.shape
    D = x.shape[1]
    tm = _pick(S, 512, 16)
    fused = part is not None

    def body(g_ref, w_ref, x_ref, nw_ref, sc_ref, dxo_ref, *rest):
        if fused:
            p_ref, dx_ref, dsc_ref, dsh_ref, dnw_ref, recv_ref, send_sems, recv_sems = rest
            copies = _scatter_copies(p_ref, recv_ref, send_sems, recv_sems)
        else:
            dx_ref, dsc_ref, dsh_ref, dnw_ref = rest

        @pl.when(pl.program_id(0) == 0)
        def _():
            dsc_ref[...] = jnp.zeros_like(dsc_ref)
            dsh_ref[...] = jnp.zeros_like(dsh_ref)
            dnw_ref[...] = jnp.zeros_like(dnw_ref)
            if fused:
                for cp in copies:
                    cp.start()

        dh = _dg(g_ref[0], w_ref[:, 0:Dg], NT)
        for p in range(1, P):
            dh = dh + _dg(g_ref[p], w_ref[:, p * Dg:(p + 1) * Dg], NT)
        xv = x_ref[...]
        nwv = nw_ref[...]
        r = lax.rsqrt(jnp.mean(xv * xv, axis=-1, keepdims=True) + EPS)
        xr = xv * r
        dn = dh * (1.0 + sc_ref[...])
        dsc_ref[...] += jnp.sum(dh * (xr * nwv), axis=0, keepdims=True)
        dsh_ref[...] += jnp.sum(dh, axis=0, keepdims=True)
        dnw_ref[...] += jnp.sum(dn * xr, axis=0, keepdims=True)
        u = dn * nwv
        dx_ref[...] = dxo_ref[...] + r * (u - xr * jnp.mean(u * xr, axis=-1, keepdims=True))

        if fused:
            @pl.when(pl.program_id(0) == S // tm - 1)
            def _():
                for cp in copies:
                    cp.wait()

    row = pl.BlockSpec((tm, D), lambda i: (i, 0))
    vec = pl.BlockSpec((1, D), lambda i: (0, 0))
    in_specs = [pl.BlockSpec((P, tm, Dg), lambda i: (0, i, 0)), pl.BlockSpec((D, P * Dg), lambda i: (0, 0)), row, vec, vec, row]
    out_specs = [row, vec, vec, vec]
    out_shape = [jax.ShapeDtypeStruct((S, D), F32)] + [jax.ShapeDtypeStruct((1, D), F32)] * 3
    scratch, args = [], [g, w, x, nw, sc, dx_out]
    if fused:
        in_specs.append(HBM)
        out_specs.append(HBM)
        out_shape.append(jax.ShapeDtypeStruct((3,) + part.shape[1:], part.dtype))
        scratch = [pltpu.SemaphoreType.DMA((3,)), pltpu.SemaphoreType.DMA((3,))]
        args.append(part)
    return pl.pallas_call(
        body, name=name, grid=(S // tm,), in_specs=in_specs, out_specs=out_specs, out_shape=out_shape,
        scratch_shapes=scratch, compiler_params=_cp("arbitrary"),
    )(*args)


def _loss_kernel(x, fw, tgt, *, name):
    S, D = x.shape
    tm = _pick(S, 512, 8)

    def body(x_ref, fw_ref, t_ref, l_ref, dx_ref, dfw_ref):
        @pl.when(pl.program_id(0) == 0)
        def _():
            l_ref[...] = jnp.zeros_like(l_ref)
            dfw_ref[...] = jnp.zeros_like(dfw_ref)

        xv = x_ref[...]
        fwv = fw_ref[...]
        r = lax.rsqrt(jnp.mean(xv * xv, axis=-1, keepdims=True) + EPS)
        xr = xv * r
        err = xr * fwv - t_ref[...]
        per_tok = jnp.mean(err * err, axis=-1, keepdims=True)
        l_ref[...] += 0.5 * jnp.sum(per_tok, axis=0, keepdims=True)
        dy = err * (1.0 / D)
        dfw_ref[...] += jnp.sum(dy * xr, axis=0, keepdims=True)
        u = dy * fwv
        dx_ref[...] = r * (u - xr * jnp.mean(u * xr, axis=-1, keepdims=True))

    row = pl.BlockSpec((tm, D), lambda i: (i, 0))
    vec = pl.BlockSpec((1, D), lambda i: (0, 0))
    return pl.pallas_call(
        body, name=name, grid=(S // tm,),
        in_specs=[row, vec, row],
        out_specs=[pl.BlockSpec((1, LANES), lambda i: (0, 0)), row, vec],
        out_shape=[jax.ShapeDtypeStruct((1, LANES), F32), jax.ShapeDtypeStruct((S, D), F32),
                   jax.ShapeDtypeStruct((1, D), F32)],
        compiler_params=_cp("arbitrary"),
    )(x, fw, tgt)


def _hg_lower_bound(lb3):
    mx = jnp.max(lb3, axis=0, keepdims=True)
    e = jnp.exp(lb3 - mx)
    p = e / jnp.sum(e, axis=0, keepdims=True)
    return p[0:1, :], p


def _hg_chunk_common(qr, fz, lbv):
    sq = _sigmoid(qr)
    q = qr * sq
    sig = _sigmoid(fz)
    f = lbv + (1.0 - lbv) * sig
    k = (1.0 - lbv) * (1.0 - sig)
    return q, sq, sig, f, k, jnp.log(f)


def _row_of(x, rows, r):
    return jnp.sum(jnp.where(rows == r, x, 0.0), axis=0, keepdims=True)


def _hg_fwd(proj, hg_lb, gn, slab=None, *, name):
    S = proj.shape[0]
    D = proj.shape[1] // 4
    H = D // LANES
    HB = min(HG_HEADS_PER_STEP, H)
    W = HB * LANES
    C = HG_CHUNK
    T = _pick(S, HG_TOKENS_PER_STEP, C)
    nch, nb = T // C, S // T
    ng = H // HB
    fused = slab is not None

    def body(q_ref, fz_ref, v_ref, g_ref, lb_ref, gn_ref, *rest):
        if fused:
            s_ref, y_ref, o_ref, sts_ref, out_ref, st, send_sems, recv_sems = rest
            finish = _gather_behind(s_ref, out_ref, send_sems, recv_sems,
                                    pl.program_id(0) * nb + pl.program_id(1), ng * nb)
        else:
            y_ref, o_ref, sts_ref, st = rest

        @pl.when(pl.program_id(1) == 0)
        def _():
            st[...] = jnp.zeros_like(st)

        lb_all, _ = _hg_lower_bound(lb_ref[...])
        gnv = gn_ref[...]
        ri = lax.broadcasted_iota(jnp.int32, (C, C), 0)
        ci_ = lax.broadcasted_iota(jnp.int32, (C, C), 1)
        low = ri >= ci_
        tri = jnp.where(low, 1.0, 0.0).astype(BF)
        rows_w = lax.broadcasted_iota(jnp.int32, (C, W), 0)

        def chunk(ci, carry):
            sl = pl.ds(pl.multiple_of(ci * C, C), C)
            heads = [slice(hh * LANES, (hh + 1) * LANES) for hh in range(HB)]
            q, _, _, _, k, logf = _hg_chunk_common(q_ref[sl, :], fz_ref[sl, :], lb_all)
            vv, gg = v_ref[sl, :], g_ref[sl, :]
            G = _tri_dot(tri, logf)
            Gm = _row_of(G, rows_w, C // 2 - 1)
            Gl = _row_of(G, rows_w, C - 1)
            qt, kt = q * jnp.exp(G - Gm), k * jnp.exp(Gm - G)
            qe, kd, eGl = q * jnp.exp(G), k * jnp.exp(Gl - G), jnp.exp(Gl)
            A = [jnp.where(low, _dg1(qt[:, ls], kt[:, ls], NT), 0.0) for ls in heads]
            Sv = [st[hh] for hh in range(HB)]
            for hh in range(HB):
                sts_ref[hh, ci] = Sv[hh]
            o = [_dg1(A[hh], vv[:, ls], NN) + _dg1(qe[:, ls], Sv[hh], NT) for hh, ls in enumerate(heads)]
            for hh, ls in enumerate(heads):
                st[hh] = Sv[hh] * eGl[:, ls] + _dg1(vv[:, ls], kd[:, ls], TN)
            gate = gg * _sigmoid(gg)
            for hh, ls in enumerate(heads):
                r = lax.rsqrt(jnp.mean(o[hh] * o[hh], axis=-1, keepdims=True) + EPS)
                y_ref[sl, ls] = ((o[hh] * r * gnv) * gate[:, ls]).astype(BF)
                o_ref[sl, ls] = o[hh]
            return carry

        lax.fori_loop(0, nch, chunk, 0)

        if fused:
            finish()

    def part(p):
        return pl.BlockSpec((T, W), lambda h, n: (n, p * ng + h))

    blk = pl.BlockSpec((T, W), lambda h, n: (n, h))
    in_specs = [part(0), part(1), part(2), part(3),
                pl.BlockSpec((3, W), lambda h, n: (0, h)), pl.BlockSpec((1, LANES), lambda h, n: (0, 0))]
    out_specs = [blk, blk, pl.BlockSpec((HB, nch, LANES, LANES), lambda h, n: (h, n, 0, 0))]
    out_shape = [jax.ShapeDtypeStruct((S, D), BF), jax.ShapeDtypeStruct((S, D), F32),
                 jax.ShapeDtypeStruct((H, S // C, LANES, LANES), F32)]
    scratch = [pltpu.VMEM((HB, LANES, LANES), F32)]
    args = [proj, proj, proj, proj, hg_lb, gn]
    if fused:
        in_specs.append(HBM)
        out_specs.append(HBM)
        out_shape.append(jax.ShapeDtypeStruct((N_CHIPS,) + slab.shape, slab.dtype))
        scratch += [pltpu.SemaphoreType.DMA((6,)), pltpu.SemaphoreType.DMA((6,))]
        args.append(slab)
    return pl.pallas_call(
        body, name=name, grid=(ng, nb), in_specs=in_specs, out_specs=out_specs, out_shape=out_shape,
        scratch_shapes=scratch, compiler_params=_cp("arbitrary", "arbitrary"),
    )(*args)


def _hg_bwd(proj, hg_lb, gn, o_all, states, dy, part=None, *, name):
    S = proj.shape[0]
    D = proj.shape[1] // 4
    H = D // LANES
    HB = min(HG_HEADS_PER_STEP, H)
    W = HB * LANES
    C = HG_CHUNK
    T = _pick(S, HG_TOKENS_PER_STEP, C)
    nch, nb = T // C, S // T
    ng = H // HB
    fused = part is not None

    def body(q_ref, fz_ref, v_ref, g_ref, lb_ref, gn_ref, o_ref, sts_ref, dy_ref, *rest):
        if fused:
            p_ref, dp_ref, dlb_ref, dgn_ref, recv_ref, dst, dlb_acc, send_sems, recv_sems = rest
            copies = _scatter_copies(p_ref, recv_ref, send_sems, recv_sems)

            @pl.when((pl.program_id(0) == 0) & (pl.program_id(1) == 0))
            def _():
                for cp in copies:
                    cp.start()
        else:
            dp_ref, dlb_ref, dgn_ref, dst, dlb_acc = rest
        n = pl.program_id(1)

        @pl.when(n == 0)
        def _():
            dst[...] = jnp.zeros_like(dst)
            dlb_acc[...] = jnp.zeros_like(dlb_acc)
            dgn_ref[...] = jnp.zeros_like(dgn_ref)

        lb_all, p3 = _hg_lower_bound(lb_ref[...])
        gnv = gn_ref[...]
        ri = lax.broadcasted_iota(jnp.int32, (C, C), 0)
        ci_ = lax.broadcasted_iota(jnp.int32, (C, C), 1)
        low = ri >= ci_
        tri = jnp.where(low, 1.0, 0.0).astype(BF)
        triu = jnp.where(ri <= ci_, 1.0, 0.0).astype(BF)
        rows_w = lax.broadcasted_iota(jnp.int32, (C, W), 0)
        gnw = jnp.tile(gnv, (1, HB))

        def chunk(cj, carry):
            ci = nch - 1 - cj
            sl = pl.ds(pl.multiple_of(ci * C, C), C)
            heads = list(enumerate(slice(hh * LANES, (hh + 1) * LANES) for hh in range(HB)))
            wide = lambda parts: jnp.concatenate(parts, axis=1)
            qr, vv, gg = q_ref[sl, :], v_ref[sl, :], g_ref[sl, :]
            q, sq, sig, f, k, logf = _hg_chunk_common(qr, fz_ref[sl, :], lb_all)
            G = _tri_dot(tri, logf)
            Gm = _row_of(G, rows_w, C // 2 - 1)
            Gl = _row_of(G, rows_w, C - 1)
            eG, e_qm, e_km, e_lk, eGl = jnp.exp(G), jnp.exp(G - Gm), jnp.exp(Gm - G), jnp.exp(Gl - G), jnp.exp(Gl)
            qt, kt, kdec, qe = q * e_qm, k * e_km, k * e_lk, q * eG
            sg = _sigmoid(gg)
            d_onw = dy_ref[sl, :] * (gg * sg)
            u = d_onw * gnw
            o = o_ref[sl, :]
            on, do = [], []
            for hh, ls in heads:
                r = lax.rsqrt(jnp.mean(o[:, ls] * o[:, ls], axis=-1, keepdims=True) + EPS)
                on.append(o[:, ls] * r)
                dgn_ref[hh] += jnp.sum(d_onw[:, ls] * on[hh], axis=0, keepdims=True)
                do.append(r * (u[:, ls] - on[hh] * jnp.mean(u[:, ls] * on[hh], axis=-1, keepdims=True)))
            dgg = dy_ref[sl, :] * (wide(on) * gnw) * (sg * (1.0 + gg * (1.0 - sg)))
            Sv = [sts_ref[hh, ci] for hh, _ in heads]
            dSv = [dst[hh] for hh, _ in heads]
            A = [jnp.where(low, _dg1(qt[:, ls], kt[:, ls], NT), 0.0) for _, ls in heads]
            dA = [jnp.where(low, _dg3(do[hh], vv[:, ls], NT), 0.0) for hh, ls in heads]
            dv = wide([_dg1(A[hh], do[hh], TN) + _dg1(kdec[:, ls], dSv[hh], NT) for hh, ls in heads])
            dq = wide([_dg3(dA[hh], kt[:, ls], NN) for hh, ls in heads]) * e_qm \
                + eG * wide([_dg3(do[hh], Sv[hh], NN) for hh, _ in heads])
            dk = wide([_dg3(dA[hh], qt[:, ls], TN) for hh, ls in heads]) * e_km \
                + e_lk * wide([_dg3(vv[:, ls], dSv[hh], NN) for hh, ls in heads])
            s_end = [Sv[hh] * eGl[:, ls] + _dg3(vv[:, ls], kdec[:, ls], TN) for hh, ls in heads]
            dgl = wide([jnp.sum(dSv[hh] * s_end[hh], axis=0, keepdims=True) for hh, _ in heads])
            for hh, ls in heads:
                dst[hh] = dSv[hh] * eGl[:, ls] + _dg1(do[hh], qe[:, ls], TN)
            dG = q * dq - k * dk + jnp.where(rows_w == C - 1, dgl, 0.0)
            dlogf = _tri_dot(triu, dG) - f * dk
            dlf_f = dlogf / f
            dlb_acc[...] += jnp.sum(dlf_f * (1.0 - sig), axis=0, keepdims=True)
            dp_ref[0, sl, :] = (dq * (sq * (1.0 + qr * (1.0 - sq)))).astype(BF)
            dp_ref[1, sl, :] = (dlf_f * (1.0 - lb_all) * sig * (1.0 - sig)).astype(BF)
            dp_ref[2, sl, :] = dv.astype(BF)
            dp_ref[3, sl, :] = dgg.astype(BF)
            return carry

        lax.fori_loop(0, nch, chunk, 0)
        sel = jnp.where(lax.broadcasted_iota(jnp.int32, (3, W), 0) == 0, 1.0, 0.0)
        dlb_ref[...] = lb_all * (sel - p3) * dlb_acc[...]

        if fused:
            @pl.when((pl.program_id(0) == ng - 1) & (n == nb - 1))
            def _():
                for cp in copies:
                    cp.wait()

    def col(p):
        return pl.BlockSpec((T, W), lambda h, n: (nb - 1 - n, p * ng + h))

    blk = pl.BlockSpec((T, W), lambda h, n: (nb - 1 - n, h))
    in_specs = [col(0), col(1), col(2), col(3),
                pl.BlockSpec((3, W), lambda h, n: (0, h)), pl.BlockSpec((1, LANES), lambda h, n: (0, 0)),
                blk, pl.BlockSpec((HB, nch, LANES, LANES), lambda h, n: (h, nb - 1 - n, 0, 0)), blk]
    out_specs = [pl.BlockSpec((4, T, W), lambda h, n: (0, nb - 1 - n, h)),
                 pl.BlockSpec((3, W), lambda h, n: (0, h)),
                 pl.BlockSpec((HB, 1, LANES), lambda h, n: (h, 0, 0))]
    out_shape = [jax.ShapeDtypeStruct((4, S, D), BF), jax.ShapeDtypeStruct((3, D), F32),
                 jax.ShapeDtypeStruct((H, 1, LANES), F32)]
    scratch = [pltpu.VMEM((HB, LANES, LANES), F32), pltpu.VMEM((1, W), F32)]
    args = [proj, proj, proj, proj, hg_lb, gn, o_all, states, dy]
    if fused:
        in_specs.append(HBM)
        out_specs.append(HBM)
        out_shape.append(jax.ShapeDtypeStruct((3,) + part.shape[1:], part.dtype))
        scratch += [pltpu.SemaphoreType.DMA((3,)), pltpu.SemaphoreType.DMA((3,))]
        args.append(part)
    return pl.pallas_call(
        body, name=name, grid=(ng, nb), in_specs=in_specs, out_specs=out_specs, out_shape=out_shape,
        scratch_shapes=scratch, compiler_params=_cp("arbitrary", "arbitrary"),
    )(*args)


def _log_sigmoid(u):
    return jnp.minimum(u, 0.0) - jnp.log(1.0 + jnp.exp(-jnp.abs(u)))


def _lane_put(base, lane, first, pieces):
    for n, p in enumerate(pieces):
        base = jnp.where(lane == first + n, p, base)
    return base


def _fox_cumsum(proj, bf_pad, *, name):
    S = proj.shape[0]
    D = proj.shape[1] // 5
    T = _pick(S, 256, 8)

    def body(fz_ref, b_ref, f_ref, carry):
        @pl.when(pl.program_id(0) == 0)
        def _():
            carry[...] = jnp.zeros_like(carry)

        logf = _log_sigmoid(fz_ref[...] + b_ref[...])
        tri = jnp.where(lax.broadcasted_iota(jnp.int32, (T, T), 0) >= lax.broadcasted_iota(jnp.int32, (T, T), 1),
                        1.0, 0.0).astype(BF)
        fv = _tri_dot(tri, logf) + carry[...]
        f_ref[...] = fv
        carry[...] = _row_of(fv, lax.broadcasted_iota(jnp.int32, (T, LANES), 0), T - 1)

    return pl.pallas_call(
        body, name=name, grid=(S // T,),
        in_specs=[pl.BlockSpec((T, LANES), lambda i: (i, 4 * D // LANES)), pl.BlockSpec((1, LANES), lambda i: (0, 0))],
        out_specs=pl.BlockSpec((T, LANES), lambda i: (i, 0)),
        out_shape=jax.ShapeDtypeStruct((S, LANES), F32),
        scratch_shapes=[pltpu.VMEM((1, LANES), F32)],
        compiler_params=_cp("arbitrary"),
    )(proj, bf_pad)


def _pair_stats(sq, lo):
    del lo
    a = lax.broadcasted_iota(jnp.int32, (LANES, LANES), 0) < FOX_DH
    b = lax.broadcasted_iota(jnp.int32, (LANES, LANES), 1) < FOX_DH
    avg = jnp.where(a == b, 1.0 / FOX_DH, 0.0).astype(BF)
    hi, mid, low = _split3(sq)
    return _dot(hi, avg) + _dot(mid, avg) + _dot(low, avg)


def _fox_prep(proj, fcum, qw2, kw2, *, name):
    S = proj.shape[0]
    D = proj.shape[1] // 5
    HP = D // LANES
    T = _pick(S, FOX_ROWS_PER_STEP, 16)

    def body(q_ref, k_ref, v_ref, f_ref, qw_ref, kw_ref, qa_ref, ka_ref, va_ref, vt_ref):
        hp = pl.program_id(1)
        lane = lax.broadcasted_iota(jnp.int32, (T, LANES), 1)
        lo = lane < FOX_DH
        qv, kv, vv, fv = q_ref[...], k_ref[...], v_ref[...], f_ref[...]
        qn = qv * lax.rsqrt(_pair_stats(qv * qv, lo) + EPS) * qw_ref[...] * (0.125 * LOG2E)
        kn = kv * lax.rsqrt(_pair_stats(kv * kv, lo) + EPS) * kw_ref[...]
        ones_q = jnp.where((lane >= 67) & (lane <= 69), 1.0, 0.0)
        ones_k = jnp.where(((lane >= 64) & (lane <= 66)) | ((lane >= 70) & (lane <= 72)), 1.0, 0.0)
        ones_v = jnp.where((lane >= 64) & (lane <= 66), 1.0, 0.0)
        for hh in range(2):
            fh = jnp.sum(jnp.where(lane == 2 * hp + hh, fv, 0.0), axis=-1, keepdims=True) * LOG2E
            pieces = [p.astype(F32) for p in _split3(fh)]

            def half(x):
                return jnp.where(lo, x if hh == 0 else pltpu.roll(x, FOX_DH, 1), 0.0)

            qa_ref[hh] = _lane_put(half(qn) + ones_q, lane, 64, pieces).astype(BF)
            ka_ref[hh] = _lane_put(half(kn) + ones_k, lane, 67, [-p for p in pieces]).astype(BF)
            va = half(vv) + ones_v
            va_ref[hh] = va.astype(BF)
            vt_ref[hh] = va.T.astype(BF)

    def part(p):
        return pl.BlockSpec((T, LANES), lambda i, hp: (i, p * HP + hp))

    vec = pl.BlockSpec((1, LANES), lambda i, hp: (0, 0))
    aug = pl.BlockSpec((2, T, LANES), lambda i, hp: (hp, i, 0))
    return pl.pallas_call(
        body, name=name, grid=(S // T, HP),
        in_specs=[part(0), part(1), part(2), pl.BlockSpec((T, LANES), lambda i, hp: (i, 0)), vec, vec],
        out_specs=[aug, aug, aug, pl.BlockSpec((2, LANES, T), lambda i, hp: (hp, 0, i))],
        out_shape=[jax.ShapeDtypeStruct((2 * HP, S, LANES), BF)] * 3 + [jax.ShapeDtypeStruct((2 * HP, LANES, S), BF)],
        compiler_params=_cp("parallel", "arbitrary"),
    )(proj, proj, proj, fcum, qw2, kw2)


def _fox_block(S):
    return _pick(S, 256, 16)


def _fox_skip_bounds(fcum, qn_w, kn_w, nheads):
    S = fcum.shape[0]
    B = _fox_block(S)
    qk = 8.0 * LOG2E * 1.02 * jnp.max(jnp.abs(qn_w)) * jnp.max(jnp.abs(kn_w))
    thresh = -(2.0 * qk + 160.0)
    f2 = fcum[:, :nheads] * LOG2E
    first, last = f2[0::B], f2[B - 1::B]
    nb = S // B
    blk = jnp.arange(nb)
    dead = (first[0::2, None, :] - last[None, :, :]) < thresh
    jmin = jnp.sum(dead & (blk[None, :, None] < 2 * jnp.arange(nb // 2)[:, None, None]), axis=1)
    live = (first[:, None, :] - last[None, :, :]) >= thresh
    imax = blk[:, None] + jnp.sum(live & (blk[:, None, None] > blk[None, :, None]), axis=0)
    return jmin.T.astype(jnp.int32), imax.T.astype(jnp.int32)


def _fox_fwd(jmin, qa, ka, vat, proj, *, name):
    H, S, _ = qa.shape
    HP = H // 2
    D = HP * LANES
    B = _fox_block(S)
    BQ = 2 * B
    nq = S // BQ

    def body(jmin_ref, q_ref, k_ref, vt_ref, g_ref, y_ref, o_ref, q2_ref):
        hp, i = pl.program_id(0), pl.program_id(1)
        lane = lax.broadcasted_iota(jnp.int32, (BQ, LANES), 1)
        lo = lane < FOX_DH
        in_stat = (lane >= 70) & (lane <= 75)
        causal = lax.broadcasted_iota(jnp.int32, (BQ, BQ), 0) <= lax.broadcasted_iota(jnp.int32, (BQ, BQ), 1)
        row = lax.broadcasted_iota(jnp.int32, (LANES, BQ), 0)
        m0, acc0 = jnp.full((1, BQ), -jnp.inf, F32), jnp.zeros((LANES, BQ), F32)
        outs = []
        for hh in range(2):
            qb = q_ref[hh]

            def block(j, carry, masked=False):
                m, acc = carry
                sl = pl.ds(pl.multiple_of(j * BQ, BQ), BQ)
                st = _dg(k_ref[hh, sl, :], qb, NT)
                if masked:
                    st = jnp.where(causal, st, -jnp.inf)
                m_new = jnp.maximum(m, jnp.ceil(jnp.max(st, axis=0, keepdims=True)))
                p = jnp.exp2(st - m_new).astype(BF)
                return m_new, acc * jnp.exp2(m - m_new) + _dot(vt_ref[hh, :, sl], p)

            carry = lax.fori_loop(jmin_ref[2 * hp + hh, i] // 2, i, block, (m0, acc0))
            m, acc = block(i, carry, masked=True)
            linv = 1.0 / jnp.sum(jnp.where(row == FOX_DH, acc, 0.0), axis=0, keepdims=True)
            tile = acc * linv
            for n, piece in enumerate(_split3(m) + _split3(linv)):
                tile = jnp.where(row == 70 + n, piece.astype(F32), tile)
            tile = tile.T
            outs.append(tile)
            q2_ref[hh] = jnp.where(in_stat, jnp.where(lane <= 72, -tile, tile), qb.astype(F32)).astype(BF)
        o = jnp.where(lo, outs[0], pltpu.roll(outs[1], FOX_DH, 1))
        o_ref[...] = o
        y_ref[...] = (o * _sigmoid(g_ref[...])).astype(BF)

    blk = pl.BlockSpec((BQ, LANES), lambda hp, i, jm: (i, hp))
    qblk = pl.BlockSpec((2, BQ, LANES), lambda hp, i, jm: (hp, i, 0))
    full = pl.BlockSpec((2, S, LANES), lambda hp, i, jm: (hp, 0, 0))
    full_t = pl.BlockSpec((2, LANES, S), lambda hp, i, jm: (hp, 0, 0))
    return pl.pallas_call(
        body, name=name,
        grid_spec=pltpu.PrefetchScalarGridSpec(
            num_scalar_prefetch=1, grid=(HP, nq),
            in_specs=[qblk, full, full_t, pl.BlockSpec((BQ, LANES), lambda hp, i, jm: (i, 3 * HP + hp))],
            out_specs=[blk, blk, qblk]),
        out_shape=[jax.ShapeDtypeStruct((S, D), BF), jax.ShapeDtypeStruct((S, D), F32),
                   jax.ShapeDtypeStruct((H, S, LANES), BF)],
        compiler_params=_cp("parallel", "arbitrary"),
    )(jmin, qa, ka, vat, proj)


def _fox_bwd_prep(dy, o, proj, q2, *, name):
    S, D = dy.shape
    HP = D // LANES
    T = _pick(S, FOX_ROWS_PER_STEP, 16)

    def body(dy_ref, o_ref, g_ref, q2_ref, da_ref):
        lane = lax.broadcasted_iota(jnp.int32, (T, LANES), 1)
        lo = lane < FOX_DH
        in_linv = (lane >= 73) & (lane <= 75)
        linv = [jnp.sum(jnp.where(in_linv, q2_ref[hh].astype(F32), 0.0), axis=-1, keepdims=True) for hh in range(2)]
        u = (dy_ref[...] * _sigmoid(g_ref[...]) * jnp.where(lo, linv[0], linv[1])).astype(BF).astype(F32)
        prod = u * o_ref[...]
        d_lo = jnp.sum(jnp.where(lo, prod, 0.0), axis=-1, keepdims=True)
        d_hi = jnp.sum(jnp.where(lo, 0.0, prod), axis=-1, keepdims=True)
        for hh, delta in enumerate((d_lo, d_hi)):
            base = jnp.where(lo, u if hh == 0 else pltpu.roll(u, FOX_DH, 1), 0.0)
            da_ref[hh] = _lane_put(base, lane, 64, [-(p.astype(F32)) for p in _split3(delta)]).astype(BF)

    blk = pl.BlockSpec((T, LANES), lambda i, hp: (i, hp))
    aug = pl.BlockSpec((2, T, LANES), lambda i, hp: (hp, i, 0))
    return pl.pallas_call(
        body, name=name, grid=(S // T, HP),
        in_specs=[blk, blk, pl.BlockSpec((T, LANES), lambda i, hp: (i, 3 * HP + hp)), aug],
        out_specs=aug,
        out_shape=jax.ShapeDtypeStruct((2 * HP, S, LANES), BF),
        compiler_params=_cp("parallel", "arbitrary"),
    )(dy, o, proj, q2)


def _fox_bwd(imax, q2, ka, va, doa, *, name):
    H, S, _ = q2.shape
    B = _fox_block(S)
    nb = S // B

    def body(imax_ref, q_ref, do_ref, k_ref, v_ref, dq_ref, dk_ref, dv_ref, cs_ref):
        j = pl.program_id(1)
        end = imax_ref[pl.program_id(0), j] + 1

        @pl.when(j == 0)
        def _():
            dq_ref[...] = jnp.zeros_like(dq_ref)

        kb, vb = k_ref[...], v_ref[...]

        def step(i, carry, nblk=1):
            dk_acc, dv_acc, cs_acc = carry
            rows = nblk * B
            sl = pl.ds(pl.multiple_of(i * B, B), rows)
            qb, dob = q_ref[sl, :], do_ref[sl, :]
            s = _dg(qb, kb, NT)
            ahead = lax.broadcasted_iota(jnp.int32, (rows, B), 0) - lax.broadcasted_iota(jnp.int32, (rows, B), 1)
            pb = jnp.exp2(jnp.where(ahead >= (j - i) * B, s, -jnp.inf)).astype(BF)
            ds = pb.astype(F32) * _dg(dob, vb, NT)
            dsb = ds.astype(BF)
            cs_acc = cs_acc + jnp.sum(ds.reshape(rows // 8, 8, B), axis=0)
            dv_acc = dv_acc + _dg(pb, dob, TN)
            dk_acc = dk_acc + _dg(dsb, qb, TN)
            dq_ref[sl, :] += _dot(dsb, kb)
            return dk_acc, dv_acc, cs_acc

        zero = jnp.zeros((B, LANES), F32)
        carry = (zero, zero, jnp.zeros((8, B), F32))
        pos = j
        for U in FOX_BWD_TILES:
            n = (end - pos) // U
            carry = lax.fori_loop(0, n, lambda ii, c, pos=pos, U=U: step(pos + U * ii, c, nblk=U), carry)
            pos = pos + U * n
        dk_acc, dv_acc, cs_acc = carry
        dk_ref[...] = dk_acc
        dv_ref[...] = dv_acc
        cs_ref[...] = jnp.sum(cs_acc, axis=0, keepdims=True)

    full = pl.BlockSpec((None, S, LANES), lambda h, j, im: (h, 0, 0))
    blk = pl.BlockSpec((None, B, LANES), lambda h, j, im: (h, j, 0))
    return pl.pallas_call(
        body, name=name,
        grid_spec=pltpu.PrefetchScalarGridSpec(
            num_scalar_prefetch=1, grid=(H, nb),
            in_specs=[full, full, blk, blk],
            out_specs=[full, blk, blk, pl.BlockSpec((None, 1, B), lambda h, j, im: (h, 0, j))]),
        out_shape=[jax.ShapeDtypeStruct((H, S, LANES), F32)] * 3 + [jax.ShapeDtypeStruct((H, 1, S), F32)],
        compiler_params=_cp("parallel", "arbitrary"),
    )(imax, q2, doa, ka, va)


def _fox_bwd_post(dqa, dka, dva, proj, dy, o, qw2, kw2, *, name):
    S, D = dy.shape
    HP = D // LANES
    T = _pick(S, FOX_ROWS_PER_STEP, 16)

    def body(dq_ref, dk_ref, dv_ref, q_ref, k_ref, g_ref, dy_ref, o_ref, qw_ref, kw_ref, dp_ref, dqw_ref, dkw_ref):
        @pl.when((pl.program_id(0) == 0) & (pl.program_id(1) == 0))
        def _():
            dqw_ref[...] = jnp.zeros_like(dqw_ref)
            dkw_ref[...] = jnp.zeros_like(dkw_ref)

        lane = lax.broadcasted_iota(jnp.int32, (T, LANES), 1)
        lo = lane < FOX_DH

        def pair(ref):
            return jnp.where(lo, ref[0], pltpu.roll(ref[1], FOX_DH, 1))

        def norm_bwd(xv, w, dyn, dw_ref):
            r = lax.rsqrt(_pair_stats(xv * xv, lo) + EPS)
            xr = xv * r
            dw_ref[...] += jnp.sum(dyn * xr, axis=0, keepdims=True)
            u = dyn * w
            return r * (u - xr * _pair_stats(u * xr, lo))

        dp_ref[0] = norm_bwd(q_ref[...], qw_ref[...], pair(dq_ref) * 0.125, dqw_ref).astype(BF)
        dp_ref[1] = norm_bwd(k_ref[...], kw_ref[...], pair(dk_ref) * (1.0 / LOG2E), dkw_ref).astype(BF)
        dp_ref[2] = pair(dv_ref).astype(BF)
        sg = _sigmoid(g_ref[...])
        dp_ref[3] = (dy_ref[...] * o_ref[...] * sg * (1.0 - sg)).astype(BF)

    def part(p):
        return pl.BlockSpec((T, LANES), lambda i, hp: (i, p * HP + hp))

    aug = pl.BlockSpec((2, T, LANES), lambda i, hp: (hp, i, 0))
    blk = pl.BlockSpec((T, LANES), lambda i, hp: (i, hp))
    vec = pl.BlockSpec((1, LANES), lambda i, hp: (0, 0))
    return pl.pallas_call(
        body, name=name, grid=(S // T, HP),
        in_specs=[aug, aug, aug, part(0), part(1), part(3), blk, blk, vec, vec],
        out_specs=[pl.BlockSpec((4, T, LANES), lambda i, hp: (0, i, hp)), vec, vec],
        out_shape=[jax.ShapeDtypeStruct((5, S, D), BF), jax.ShapeDtypeStruct((1, LANES), F32),
                   jax.ShapeDtypeStruct((1, LANES), F32)],
        compiler_params=_cp("arbitrary", "arbitrary"),
    )(dqa, dka, dva, proj, proj, proj, dy, o, qw2, kw2)


def _fox_dfz(colsum, nheads, proj, bf_pad, dproj, *, name):
    S = colsum.shape[0]
    H = nheads
    D = dproj.shape[2]
    T = _pick(S, 256, 16)
    nb = S // T

    def body(cs_ref, fz_ref, b_ref, _, dp_ref, db_ref, carry):
        @pl.when(pl.program_id(0) == 0)
        def _():
            carry[...] = jnp.zeros_like(carry)
            db_ref[...] = jnp.zeros_like(db_ref)

        lane = lax.broadcasted_iota(jnp.int32, (T, LANES), 1)
        df = -cs_ref[...]
        triu = jnp.where(lax.broadcasted_iota(jnp.int32, (T, T), 0) <= lax.broadcasted_iota(jnp.int32, (T, T), 1),
                         1.0, 0.0).astype(BF)
        dlogf = _tri_dot(triu, df) + carry[...]
        carry[...] = _row_of(dlogf, lax.broadcasted_iota(jnp.int32, (T, LANES), 0), 0)
        dfz = jnp.where(lane < H, dlogf * _sigmoid(-(fz_ref[...] + b_ref[...])), 0.0)
        db_ref[...] += jnp.sum(dfz, axis=0, keepdims=True)
        dp_ref[...] = jnp.zeros_like(dp_ref)
        dp_ref[:, 0:LANES] = dfz.astype(BF)

    return pl.pallas_call(
        body, name=name, grid=(nb,),
        in_specs=[pl.BlockSpec((T, LANES), lambda i: (nb - 1 - i, 0)),
                  pl.BlockSpec((T, LANES), lambda i: (nb - 1 - i, 4 * D // LANES)),
                  pl.BlockSpec((1, LANES), lambda i: (0, 0)),
                  pl.BlockSpec(memory_space=pl.ANY)],
        out_specs=[pl.BlockSpec((None, T, D), lambda i: (4, nb - 1 - i, 0)), pl.BlockSpec((1, LANES), lambda i: (0, 0))],
        out_shape=[jax.ShapeDtypeStruct(dproj.shape, BF), jax.ShapeDtypeStruct((1, LANES), F32)],
        scratch_shapes=[pltpu.VMEM((1, LANES), F32)],
        input_output_aliases={3: 0},
        compiler_params=_cp("arbitrary"),
    )(colsum, proj, bf_pad, dproj)


def _mod_fwd(c16, w, b, *, name):
    L, D, N = w.shape
    tn = _pick(N, 512)

    def body(c_ref, w_ref, b_ref, o_ref):
        cv = c_ref[...]
        ca = (cv * _sigmoid(cv)).astype(BF)
        o_ref[...] = _dot(ca, w_ref[...].astype(BF)) + b_ref[...]

    return pl.pallas_call(
        body, name=name, grid=(L, N // tn),
        in_specs=[pl.BlockSpec((16, D), lambda l, j: (0, 0)), pl.BlockSpec((None, D, tn), lambda l, j: (l, 0, j)),
                  pl.BlockSpec((None, 1, tn), lambda l, j: (l, 0, j))],
        out_specs=pl.BlockSpec((None, 16, tn), lambda l, j: (l, 0, j)),
        out_shape=jax.ShapeDtypeStruct((L, 16, N), F32),
        compiler_params=_cp("parallel", "arbitrary"),
    )(c16, w, b)


def _mod_bwd(c16, dmod, *, name):
    L, _, N = dmod.shape
    D = c16.shape[1]
    tn = _pick(N, 512)

    def body(c_ref, d_ref, o_ref):
        cv = c_ref[...]
        ca = (cv * _sigmoid(cv)).astype(BF)
        o_ref[...] = _dg(ca, d_ref[...].astype(BF), TN)

    return pl.pallas_call(
        body, name=name, grid=(L, N // tn),
        in_specs=[pl.BlockSpec((16, D), lambda l, j: (0, 0)), pl.BlockSpec((None, 16, tn), lambda l, j: (l, 0, j))],
        out_specs=pl.BlockSpec((None, D, tn), lambda l, j: (l, 0, j)),
        out_shape=jax.ShapeDtypeStruct((L, D, N), F32),
        compiler_params=_cp("parallel", "arbitrary"),
    )(c16, dmod)


def _adamw_math(w, g, m, v):
    m = ADAM_B1 * m + (1.0 - ADAM_B1) * g
    v = ADAM_B2 * v + (1.0 - ADAM_B2) * (g * g)
    m_hat = m / (1.0 - ADAM_B1 ** ADAM_STEP)
    v_hat = v / (1.0 - ADAM_B2 ** ADAM_STEP)
    return -ADAM_LR * (m_hat / (jnp.sqrt(v_hat) + ADAM_EPS) + ADAM_WD * w), m, v


def _adamw(w, g, m, v, *, g_at=None, name):
    R, C = w.shape
    row0 = 0 if g_at is None else g_at[1]
    tr = min(math.gcd(row0, 256) if row0 else 256, -(-R // 8) * 8)
    g0 = row0 // tr
    if g_at is None:
        g_spec = pl.BlockSpec((tr, C), lambda i: (i, 0))
    else:
        g_spec = pl.BlockSpec((None, tr, C), lambda i: (g_at[0], g0 + i, 0))

    def body(w_ref, g_ref, m_ref, v_ref, d_ref, mo_ref, vo_ref):
        d, mn, vn = _adamw_math(w_ref[...], g_ref[...], m_ref[...], v_ref[...])
        d_ref[...] = d
        mo_ref[...] = mn
        vo_ref[...] = vn

    blk = pl.BlockSpec((tr, C), lambda i: (i, 0))
    return pl.pallas_call(
        body, name=name, grid=(pl.cdiv(R, tr),),
        in_specs=[blk, g_spec, blk, blk],
        out_specs=[blk, blk, blk],
        out_shape=[jax.ShapeDtypeStruct((R, C), F32)] * 3,
        compiler_params=_cp("parallel"),
    )(w, g, m, v)


def _sum_parts(parts, *, name):
    P, R, C = parts.shape

    def body(p_ref, o_ref):
        acc = p_ref[0]
        for p in range(1, P):
            acc = acc + p_ref[p]
        o_ref[...] = acc

    return pl.pallas_call(
        body, name=name, grid=(1,),
        in_specs=[pl.BlockSpec((P, R, C), lambda i: (0, 0, 0))],
        out_specs=pl.BlockSpec((R, C), lambda i: (0, 0)),
        out_shape=jax.ShapeDtypeStruct((R, C), F32),
        compiler_params=_cp("arbitrary"),
    )(parts)


def _add_halves(g4, recv, c_idx, *, name):
    _, _, Rh, C = g4.shape
    tr = min(256, Rh)

    def body(c_ref, a_ref, b_ref, o_ref):
        o_ref[...] = (a_ref[...] + b_ref[...].astype(F32)).astype(BF)

    return pl.pallas_call(
        body, name=name,
        grid_spec=pltpu.PrefetchScalarGridSpec(
            num_scalar_prefetch=1, grid=(4, pl.cdiv(Rh, tr)),
            in_specs=[pl.BlockSpec((None, None, tr, C), lambda j, r, c: (j, c[0], r, 0)),
                      pl.BlockSpec((None, tr, C), lambda j, r, c: (j, r, 0))],
            out_specs=pl.BlockSpec((None, tr, C), lambda j, r, c: (j, r, 0))),
        out_shape=jax.ShapeDtypeStruct((4, Rh, C), BF),
        compiler_params=_cp("parallel", "arbitrary"),
    )(c_idx, g4, recv)


def _add_four(g4, from_sibling, from_chips, pos, *, name):
    _, _, Rh, C = g4.shape
    tr = min(256, Rh)

    def body(p_ref, a_ref, s_ref, b_ref, o_ref):
        own = a_ref[...] + s_ref[...].astype(F32)
        o_ref[...] = ((own + b_ref[0].astype(F32)) + b_ref[1].astype(F32)) + b_ref[2].astype(F32)

    return pl.pallas_call(
        body, name=name,
        grid_spec=pltpu.PrefetchScalarGridSpec(
            num_scalar_prefetch=1, grid=(pl.cdiv(Rh, tr),),
            in_specs=[pl.BlockSpec((None, None, tr, C), lambda r, p: (p[0], p[1], r, 0)),
                      pl.BlockSpec((None, tr, C), lambda r, p: (p[0], r, 0)),
                      pl.BlockSpec((3, tr, C), lambda r, p: (0, r, 0))],
            out_specs=pl.BlockSpec((None, tr, C), lambda r, p: (p[1], r, 0))),
        out_shape=jax.ShapeDtypeStruct((2, Rh, C), F32),
        compiler_params=_cp("arbitrary"),
    )(pos, g4, from_sibling, from_chips)


HBM = pl.BlockSpec(memory_space=pltpu.HBM)


def _mesh_pos():
    return lax.axis_index("x"), lax.axis_index("y"), lax.axis_index("c")


def _other_chips(x, y):
    return [(1 - x, y), (x, 1 - y), (1 - x, 1 - y)]


def _allgather_small(xs, *, name):
    m_per, n = xs.shape

    def body(x_ref, out_ref, send_sems, recv_sems, local_sem):
        x, y, c = _mesh_pos()
        me, sibling = (x, y, c), (x, y, 1 - c)
        chips = _other_chips(x, y)

        def rows(px, py, pc):
            return out_ref.at[pl.ds((4 * px + 2 * py + pc) * m_per, m_per), :]

        def copy(k, block, to, src=None):
            return pltpu.make_async_remote_copy(
                src_ref=rows(*block) if src is None else src, dst_ref=rows(*block),
                send_sem=send_sems.at[k], recv_sem=recv_sems.at[k], device_id=to, device_id_type=MESH)

        mine = pltpu.make_async_copy(x_ref, rows(*me), local_sem)
        mine.start()
        first = [copy(0, me, sibling, src=x_ref)]
        first += [copy(1 + j, me, (*chip, c), src=x_ref) for j, chip in enumerate(chips)]
        for cp in first:
            cp.start()
        passed = [copy(4 + j, (*chip, c), sibling) for j, chip in enumerate(chips)]
        for j, chip in enumerate(chips):
            copy(1 + j, (*chip, c), me).wait_recv()
            passed[j].start()
        copy(0, sibling, me).wait_recv()
        for j, chip in enumerate(chips):
            copy(4 + j, (*chip, 1 - c), me).wait_recv()
        for cp in first + passed:
            cp.wait_send()
        mine.wait()

    return pl.pallas_call(
        body, name=name,
        out_shape=jax.ShapeDtypeStruct((N_DEV * m_per, n), xs.dtype),
        in_specs=[pl.BlockSpec(memory_space=pltpu.VMEM)],
        out_specs=pl.BlockSpec(memory_space=pltpu.VMEM),
        scratch_shapes=[pltpu.SemaphoreType.DMA((7,)), pltpu.SemaphoreType.DMA((7,)), pltpu.SemaphoreType.DMA],
    )(xs)


def _chip_slab_copies(s_ref, out_ref, send_sems, recv_sems):
    R = s_ref.shape[0]
    Rh = R // 2
    x, y, c = _mesh_pos()
    me, sibling = (x, y, c), (x, y, 1 - c)
    chips = _other_chips(x, y)

    def half(px, py, pc):
        return out_ref.at[2 * px + py, pl.ds(pc * Rh, Rh), :]

    def copy(k, block, to, src=None):
        return pltpu.make_async_remote_copy(
            src_ref=half(*block) if src is None else src, dst_ref=half(*block),
            send_sem=send_sems.at[k], recv_sem=recv_sems.at[k], device_id=to, device_id_type=MESH)

    first = [copy(j, me, (*chip, c), src=s_ref.at[pl.ds(c * Rh, Rh), :]) for j, chip in enumerate(chips)]
    passed = [copy(3 + j, (*chip, c), sibling) for j, chip in enumerate(chips)]
    landed = [copy(j, (*chip, c), me) for j, chip in enumerate(chips)]
    from_sibling = [copy(3 + j, (*chip, 1 - c), me) for j, chip in enumerate(chips)]
    return first, passed, landed, from_sibling


def _gather_behind(s_ref, out_ref, send_sems, recv_sems, step, nsteps):
    first, passed, landed, from_sibling = _chip_slab_copies(s_ref, out_ref, send_sems, recv_sems)

    @pl.when(step == 0)
    def _():
        for cp in first:
            cp.start()

    @pl.when(step == (3 * nsteps) // 4)
    def _():
        for arrived, onward in zip(landed, passed):
            arrived.wait_recv()
            onward.start()

    def finish():
        @pl.when(step == nsteps - 1)
        def _():
            for cp in from_sibling:
                cp.wait_recv()
            for cp in first + passed:
                cp.wait_send()

    return finish


def _allgather_chip_slabs(slab, *, name):
    R, C = slab.shape

    def body(s_ref, out_ref, send_sems, recv_sems):
        first, passed, landed, from_sibling = _chip_slab_copies(s_ref, out_ref, send_sems, recv_sems)
        for cp in first:
            cp.start()
        for arrived, onward in zip(landed, passed):
            arrived.wait_recv()
            onward.start()
        for cp in from_sibling:
            cp.wait_recv()
        for cp in first + passed:
            cp.wait_send()

    return pl.pallas_call(
        body, name=name,
        out_shape=jax.ShapeDtypeStruct((N_CHIPS, R, C), slab.dtype),
        in_specs=[HBM], out_specs=HBM,
        scratch_shapes=[pltpu.SemaphoreType.DMA((6,)), pltpu.SemaphoreType.DMA((6,))],
    )(slab)


def _swap_halves(mine, *, name):
    def body(g_ref, out_ref, send_sems, recv_sems):
        x, y, c = _mesh_pos()
        copies = [pltpu.make_async_remote_copy(
            src_ref=g_ref.at[j], dst_ref=out_ref.at[j], send_sem=send_sems.at[j], recv_sem=recv_sems.at[j],
            device_id=(x, y, 1 - c), device_id_type=MESH) for j in range(N_CHIPS)]
        for cp in copies:
            cp.start()
        for cp in copies:
            cp.wait()

    return pl.pallas_call(
        body, name=name,
        out_shape=jax.ShapeDtypeStruct(mine.shape, mine.dtype),
        in_specs=[HBM], out_specs=HBM,
        scratch_shapes=[pltpu.SemaphoreType.DMA((N_CHIPS,)), pltpu.SemaphoreType.DMA((N_CHIPS,))],
    )(mine)


def _scatter_copies(p_ref, out_ref, send_sems, recv_sems):
    x, y, c = _mesh_pos()
    return [pltpu.make_async_remote_copy(
        src_ref=p_ref.at[2 * px + py], dst_ref=out_ref.at[j], send_sem=send_sems.at[j], recv_sem=recv_sems.at[j],
        device_id=(px, py, c), device_id_type=MESH) for j, (px, py) in enumerate(_other_chips(x, y))]


def _join_halves(buf, *, name):
    def body(b_ref, out_ref, send_sem, recv_sem):
        x, y, c = _mesh_pos()
        cp = pltpu.make_async_remote_copy(
            src_ref=b_ref.at[c], dst_ref=out_ref.at[c], send_sem=send_sem, recv_sem=recv_sem,
            device_id=(x, y, 1 - c), device_id_type=MESH)
        cp.start()
        cp.wait()

    return pl.pallas_call(
        body, name=name,
        out_shape=jax.ShapeDtypeStruct(buf.shape, buf.dtype),
        in_specs=[HBM], out_specs=HBM, input_output_aliases={0: 0},
        scratch_shapes=[pltpu.SemaphoreType.DMA, pltpu.SemaphoreType.DMA],
    )(buf)


def _pad_rows(a, mult):
    pad = (-a.shape[0]) % mult
    return a if pad == 0 else jnp.pad(a, ((0, pad),) + ((0, 0),) * (a.ndim - 1))


def _local_step(x, target, mod, wts, small, slabs=None, unpacks=None, reduce_early=None, grad_slab=None,
                reduce_late=None):
    S, D = x.shape
    HP = D // LANES
    row = lambda v: v.reshape(1, -1)
    msplit = [[row(mod[i, k * D:(k + 1) * D]) for k in range(6)] for i in range(2)]
    gw, gs = {}, {}
    dmod = [[None] * 6 for _ in range(2)]
    slab, where = grad_slab if grad_slab is not None else (None, {})

    def dw(key, a, b, name):
        nonlocal slab
        if key in where:
            slab = _matmul_tn(a, b, name=name, into=(slab,) + where[key])
        else:
            gw[key] = _matmul_tn(a, b, name=name)

    sh1, sc1, g1, sh2, sc2, g2 = msplit[0]
    n1w0, n2w0 = row(small["norm1_w"][0]), row(small["norm2_w"][0])
    slabs = slabs if slabs is not None else (None, None, None)
    proj0, h1_0, *gathered = _ln_matmul(x, n1w0, sc1, sh1, wts["hg_w_in"], slabs[0], relu2=False, name="hg_in_proj")
    if slabs[0] is not None:
        wts = {**wts, **unpacks[0](gathered[0])}
    gn = small["hg_gn_w"].reshape(1, LANES)
    ypre0, o0, states, *gathered = _hg_fwd(proj0, small["hg_lb"], gn, slabs[1], name="hg_fwd")
    if slabs[1] is not None:
        wts = {**wts, **unpacks[1](gathered[0])}
    x1, ymix0 = _matmul_resid(ypre0, wts["hg_w_out"], x, g1, name="hg_out_proj")
    a0, u0, h2_0, *gathered = _ln_matmul(x1, n2w0, sc2, sh2, wts["mlp_w1_0"], slabs[2], relu2=True, name="mlp0_up")
    if slabs[2] is not None:
        wts = {**wts, **unpacks[2](gathered[0])}
    x2, ymlp0 = _matmul_resid(u0, wts["mlp_w2_0"], x1, g2, name="mlp0_down")

    sh1b, sc1b, g1b, sh2b, sc2b, g2b = msplit[1]
    n1w1, n2w1 = row(small["norm1_w"][1]), row(small["norm2_w"][1])
    proj1, h1_1 = _ln_matmul(x2, n1w1, sc1b, sh1b, wts["fox_w_in"], relu2=False, name="fox_in_proj")
    nheads = 2 * HP
    bf_pad = jnp.pad(small["fox_b_f"].reshape(1, nheads), ((0, 0), (0, LANES - nheads)))
    qw2 = jnp.tile(small["fox_qn_w"].reshape(1, FOX_DH), (1, 2))
    kw2 = jnp.tile(small["fox_kn_w"].reshape(1, FOX_DH), (1, 2))
    fcum = _fox_cumsum(proj1, bf_pad, name="fox_cumsum")
    qa, ka, va, vat = _fox_prep(proj1, fcum, qw2, kw2, name="fox_prep")
    jmin, imax = _fox_skip_bounds(fcum, small["fox_qn_w"], small["fox_kn_w"], nheads)
    ypre1, o1, q2 = _fox_fwd(jmin, qa, ka, vat, proj1, name="fox_fwd")
    x3, ymix1 = _matmul_resid(ypre1, wts["fox_w_out"], x2, g1b, name="fox_out_proj")
    a1, u1, h2_1 = _ln_matmul(x3, n2w1, sc2b, sh2b, wts["mlp_w1_1"], relu2=True, name="mlp1_up")
    x4, ymlp1 = _matmul_resid(u1, wts["mlp_w2_1"], x3, g2b, name="mlp1_down")

    loss, dx4, dfw = _loss_kernel(x4, row(small["final_w"]), target, name="loss")
    gs["final_w"] = dfw.reshape(-1)

    def mlp_bwd(i, dx_out, x_in, h2, a, u, ymlp, n2w, sc2_, g2_):
        dz, dm, dg2 = _gate_matmul_nt(dx_out, g2_, ymlp, wts[f"mlp_w2_{i}"], a, name=f"mlp{i}_down_bwd")
        dw(f"mlp_w2_{i}", u, dm[None], f"mlp{i}_dw2")
        dw(f"mlp_w1_{i}", h2, dz[None], f"mlp{i}_dw1")
        dx_in, dsc, dsh, dnw = _matmul_nt_lnbwd(dz[None], wts[f"mlp_w1_{i}"], x_in, n2w, sc2_, dx_out,
                                                name=f"mlp{i}_up_bwd")
        dmod[i][3], dmod[i][4], dmod[i][5] = dsh, dsc, dg2
        return dx_in, dnw

    dx3, dn2w1 = mlp_bwd(1, dx4, x3, h2_1, a1, u1, ymlp1, n2w1, sc2b, g2b)
    dyp1, dm1, dg1b = _gate_matmul_nt(dx3, g1b, ymix1, wts["fox_w_out"], None, name="fox_out_bwd")
    dw("fox_w_out", ypre1, dm1[None], "fox_dw_out")
    doa = _fox_bwd_prep(dyp1, o1, proj1, q2, name="fox_bwd_prep")
    dqa, dka, dva, colsum = _fox_bwd(imax, q2, ka, va, doa, name="fox_bwd")
    colsum = jnp.pad(colsum[:, 0, :].T, ((0, 0), (0, LANES - nheads)))
    dproj1, dqw, dkw = _fox_bwd_post(dqa, dka, dva, proj1, dyp1, o1, qw2, kw2, name="fox_bwd_post")
    dproj1, dbf = _fox_dfz(colsum, nheads, proj1, bf_pad, dproj1, name="fox_dfz")
    dw("fox_w_in", h1_1, dproj1, "fox_dw_in")
    dx2, dsc, dsh, dn1w1 = _matmul_nt_lnbwd(dproj1, wts["fox_w_in"], x2, n1w1, sc1b, dx3, name="fox_in_bwd")
    dmod[1][0], dmod[1][1], dmod[1][2] = dsh, dsc, dg1b
    gs["fox_qn_w"] = dqw[0, :FOX_DH] + dqw[0, FOX_DH:]
    gs["fox_kn_w"] = dkw[0, :FOX_DH] + dkw[0, FOX_DH:]
    gs["fox_b_f"] = dbf[0, :nheads]

    dx1, dn2w0 = mlp_bwd(0, dx2, x1, h2_0, a0, u0, ymlp0, n2w0, sc2, g2)
    dyp0, dm0, dg1 = _gate_matmul_nt(dx1, g1, ymix0, wts["hg_w_out"], None, name="hg_out_bwd")
    dw("hg_w_out", ypre0, dm0[None], "hg_dw_out")
    part, ctx = reduce_early(gw, slab) if reduce_early is not None else (None, None)
    dproj0, dlb, dgn, *from_chips = _hg_bwd(proj0, small["hg_lb"], gn, o0, states, dyp0, part, name="hg_bwd")
    early = (ctx, from_chips[0]) if reduce_early is not None else None
    dw("hg_w_in", h1_0, dproj0, "hg_dw_in")
    part, ctx = reduce_late(gw) if reduce_late is not None else (None, None)
    dx0, dsc, dsh, dn1w0, *from_chips = _matmul_nt_lnbwd(dproj0, wts["hg_w_in"], x, n1w0, sc1, dx1, part, name="hg_in_bwd")
    late = (ctx, from_chips[0]) if reduce_late is not None else None
    dmod[0][0], dmod[0][1], dmod[0][2] = dsh, dsc, dg1
    gs["hg_lb"] = dlb
    gs["hg_gn_w"] = jnp.sum(dgn, axis=0)

    gs["norm1_w"] = jnp.concatenate([dn1w0, dn1w1], axis=0)
    gs["norm2_w"] = jnp.concatenate([dn2w0, dn2w1], axis=0)
    gs["dmod"] = jnp.stack([jnp.concatenate(dmod[i], axis=1)[0] for i in range(2)])
    return loss, dx0, gw, gs, early, late


def _pack_halves(layout):
    rh = -(-max(sum(a.shape[0] for _, a in half) for half in layout) // 16) * 16
    place, parts = {}, []
    for h, half in enumerate(layout):
        off = 0
        for n, a in half:
            place[n] = (h, off, a.shape[0])
            off += a.shape[0]
        parts.append(jnp.pad(jnp.concatenate([a.astype(BF) for _, a in half], axis=0), ((0, rh - off), (0, 0))))
    return jnp.concatenate(parts, axis=0), place, rh


SMALL_NAMES = ["norm1_w", "norm2_w", "hg_lb", "hg_gn_w", "fox_b_f", "fox_qn_w", "fox_kn_w", "final_w"]


def _pack_small(d, names):
    rows, offs, r0 = [], {}, 0
    for n in names:
        flat = d[n].reshape(-1)
        nr = -(-flat.shape[0] // LANES)
        rows.append(jnp.pad(flat, (0, nr * LANES - flat.shape[0])).reshape(nr, LANES))
        offs[n] = (r0, nr)
        r0 += nr
    return jnp.concatenate(rows, axis=0), offs


def _unpack_small(packed, offs, name, like):
    r0, nr = offs[name]
    return packed[r0:r0 + nr].reshape(-1)[:like.size].reshape(like.shape)


def kernel(x, c, w_mod, b_mod, norm1_w, norm2_w, hg_w_in, hg_w_out, hg_lb, hg_gn_w, fox_w_in, fox_b_f, fox_qn_w, fox_kn_w, fox_w_out, mlp_w1, mlp_w2, final_w, loss_target, m_w_mod, m_b_mod, m_norm1_w, m_norm2_w, m_hg_w_in, m_hg_w_out, m_hg_lb, m_hg_gn_w, m_fox_w_in, m_fox_b_f, m_fox_qn_w, m_fox_kn_w, m_fox_w_out, m_mlp_w1, m_mlp_w2, m_final_w, v_w_mod, v_b_mod, v_norm1_w, v_norm2_w, v_hg_w_in, v_hg_w_out, v_hg_lb, v_hg_gn_w, v_fox_w_in, v_fox_b_f, v_fox_qn_w, v_fox_kn_w, v_fox_w_out, v_mlp_w1, v_mlp_w2, v_final_w):
    S, D = x.shape[1], x.shape[2]
    nheads = D // FOX_DH
    ax, ay, ac = _mesh_pos()
    chip = 2 * ax + ay
    dev = 2 * chip + ac
    xs, tgt = x.reshape(S, D), loss_target.reshape(S, D)

    c_all = _allgather_small(_pad_rows(c.reshape(-1, LANES), 8), name="gather_c")
    c_all = c_all.reshape(N_DEV, -1)[:, :D]
    c16 = _pad_rows(c_all, 16)
    nmod = w_mod.shape[2]
    b_shard = lax.dynamic_slice_in_dim(b_mod, chip * nmod, nmod, axis=1)
    mod_shard = _mod_fwd(c16, w_mod, b_shard[:, None, :], name="mod_fwd")[:, :N_DEV]
    mod_all = _allgather_small(mod_shard.reshape(-1, LANES), name="gather_mod")
    mod_all = mod_all.reshape(N_CHIPS, 2, 2, N_DEV, nmod)[:, 0]
    mod = lax.dynamic_index_in_dim(mod_all, dev, axis=2, keepdims=False)
    mod = mod.transpose(1, 0, 2).reshape(2, N_CHIPS * nmod)

    fox_rows = fox_w_in.shape[2]
    col = lambda g: g.transpose(1, 0, 2).reshape(g.shape[1], -1)
    rowsh = lambda g: g.reshape(-1, g.shape[2])
    own = lambda g, s: lax.dynamic_update_index_in_dim(g, s, chip, 0)

    slab_in = hg_w_in[0].astype(BF)
    wts = {"hg_w_in": col(own(_allgather_chip_slabs(slab_in, name="gather_hg_w_in"), slab_in))}
    fox_flat, fox_cut = fox_w_in[0].reshape(fox_rows, D), fox_rows // 2
    slabs, unpacks = [], []
    for layout_w in ([[("mlp_w1_0", mlp_w1[0])], [("mlp_w2_0", mlp_w2[0])]],
                     [[("mlp_w1_1", mlp_w1[1]), ("hg_w_out", hg_w_out[0])], [("mlp_w2_1", mlp_w2[1]), ("fox_w_out", fox_w_out[0])]],
                     [[("fox_a", fox_flat[:fox_cut])], [("fox_b", fox_flat[fox_cut:])]]):
        slab_w, place_w, rh_w = _pack_halves(layout_w)

        def unpack(gathered, slab_w=slab_w, place_w=place_w, rh_w=rh_w):
            gathered = own(gathered, slab_w)
            out = {}
            for n, (h, off, rows) in place_w.items():
                g = gathered[:, h * rh_w + off:h * rh_w + off + rows, :]
                out[n] = col(g) if n.startswith("mlp_w1") else rowsh(g) if n.startswith(("mlp_w2", "hg_", "fox_w")) else g
            if "fox_a" in out:
                fox_in = col(jnp.concatenate([out.pop("fox_a"), out.pop("fox_b")], axis=1).reshape(N_CHIPS, D, fox_rows))
                out["fox_w_in"] = jnp.pad(fox_in, ((0, 0), (0, 5 * D - fox_in.shape[1])))
            return out

        slabs.append(slab_w)
        unpacks.append(unpack)

    small = {"norm1_w": norm1_w, "norm2_w": norm2_w, "hg_lb": hg_lb, "hg_gn_w": hg_gn_w, "fox_b_f": fox_b_f,
             "fox_qn_w": fox_qn_w, "fox_kn_w": fox_kn_w, "final_w": final_w}

    def uncol(g, n):
        return g.reshape(g.shape[0], N_CHIPS, n).transpose(1, 0, 2)

    pos = jnp.stack([chip, ac])

    def swap_and_add(g4, tag):
        to_sibling = lax.dynamic_index_in_dim(g4, 1 - ac, axis=1, keepdims=False).astype(BF)
        from_sibling = _swap_halves(to_sibling, name=f"rs_swap_{tag}")
        return from_sibling, _add_halves(g4, from_sibling, ac.reshape(1), name=f"rs_add_halves_{tag}")

    def finish(g4, from_sibling, from_chips, tag):
        my_half = _add_four(g4, from_sibling, from_chips, pos, name=f"rs_add_chips_{tag}")
        return _join_halves(my_half, name=f"rs_join_{tag}")

    layout = [[("mlp_w1", 2 * D), ("hg_w_out", D // 4), ("fox_w_out", D // 4)], [("mlp_w2", 2 * D), ("fox_w_in", fox_rows)]]
    place = {}
    for h, half in enumerate(layout):
        off = 0
        for n, rows in half:
            place[n] = (h, off, rows)
            off += rows

    rh = -(-max(sum(rows for _, rows in half) for half in layout) // 16) * 16
    where = {"hg_w_out": ("row",) + place["hg_w_out"][:2], "fox_w_out": ("row",) + place["fox_w_out"][:2]}
    for i in range(2):
        where[f"mlp_w1_{i}"] = ("col", place["mlp_w1"][0], place["mlp_w1"][1] + i * D)
        where[f"mlp_w2_{i}"] = ("row", place["mlp_w2"][0], place["mlp_w2"][1] + i * D)

    def reduce_early(gw, slab):
        gfox = uncol(gw["fox_w_in"][:, :4 * fox_rows], fox_rows).reshape(N_CHIPS, 1, fox_rows, D)
        h, off, _ = place["fox_w_in"]
        slab = lax.dynamic_update_slice(slab, gfox, (0, h, off, 0))
        for h, half in enumerate(layout):
            used = sum(rows for _, rows in half)
            if used < rh:
                slab = lax.dynamic_update_slice(slab, jnp.zeros((N_CHIPS, 1, rh - used, D), F32), (0, h, used, 0))
        from_sibling, part = swap_and_add(slab, "early")
        return part, (slab, from_sibling)

    def reduce_late(gw):
        g4 = uncol(gw["hg_w_in"], D).reshape(N_CHIPS, 2, D // 2, D)
        from_sibling, part = swap_and_add(g4, "late")
        return part, (g4, from_sibling)

    loss_part, grad_x, gw, gs, (early, from_chips_early), (late, from_chips_late) = _local_step(
        xs, tgt, mod, wts, small, slabs, unpacks, reduce_early, (lax.empty((N_CHIPS, 2, rh, D), F32), where), reduce_late)
    gshard = finish(*early, from_chips_early, "early")
    g_hg_w_in = finish(*late, from_chips_late, "late").reshape(D, D)

    names = ["dmod", "loss"] + SMALL_NAMES
    packed, offs = _pack_small({**gs, "loss": loss_part[0, :1]}, names)
    packed = _pad_rows(packed, 8)
    rp = packed.shape[0]
    parts = _allgather_small(packed, name="gather_small").reshape(N_DEV, rp, LANES)
    total = _sum_parts(parts, name="sum_small")
    r0, nr = offs["dmod"]
    dmod_all = parts[:, r0:r0 + nr].reshape(N_DEV, 2, N_CHIPS * nmod)
    dmod_shard = lax.dynamic_slice_in_dim(dmod_all, chip * nmod, nmod, axis=2).transpose(1, 0, 2)
    g_w_mod = _mod_bwd(c16, jnp.pad(dmod_shard, ((0, 0), (0, 16 - N_DEV), (0, 0))), name="mod_bwd")

    loss = _unpack_small(total, offs, "loss", loss_part[0, :1]).reshape(())
    grads = {"w_mod": g_w_mod, "b_mod": _unpack_small(total, offs, "dmod", b_mod)}
    for n in SMALL_NAMES:
        grads[n] = _unpack_small(total, offs, n, small[n])

    given = dict(w_mod=(w_mod, m_w_mod, v_w_mod), b_mod=(b_mod, m_b_mod, v_b_mod), norm1_w=(norm1_w, m_norm1_w, v_norm1_w),
                 norm2_w=(norm2_w, m_norm2_w, v_norm2_w), hg_w_in=(hg_w_in, m_hg_w_in, v_hg_w_in),
                 hg_w_out=(hg_w_out, m_hg_w_out, v_hg_w_out), hg_lb=(hg_lb, m_hg_lb, v_hg_lb),
                 hg_gn_w=(hg_gn_w, m_hg_gn_w, v_hg_gn_w), fox_w_in=(fox_w_in, m_fox_w_in, v_fox_w_in),
                 fox_b_f=(fox_b_f, m_fox_b_f, v_fox_b_f), fox_qn_w=(fox_qn_w, m_fox_qn_w, v_fox_qn_w),
                 fox_kn_w=(fox_kn_w, m_fox_kn_w, v_fox_kn_w), fox_w_out=(fox_w_out, m_fox_w_out, v_fox_w_out),
                 mlp_w1=(mlp_w1, m_mlp_w1, v_mlp_w1), mlp_w2=(mlp_w2, m_mlp_w2, v_mlp_w2), final_w=(final_w, m_final_w, v_final_w))
    upd = {}

    for n, (h, off, rows) in place.items():
        w, m, v = given[n]
        flat = lambda a: a.reshape(rows, D)
        d, mn, vn = _adamw(flat(w), gshard, flat(m), flat(v), g_at=(h, off), name=f"adamw_{n}")
        grads[n] = gshard[h, off:off + rows].reshape(w.shape)
        upd[n] = tuple(a.reshape(w.shape) for a in (d, mn, vn))

    w, m, v = given["hg_w_in"]
    grads["hg_w_in"] = g_hg_w_in.reshape(w.shape)
    upd["hg_w_in"] = tuple(a.reshape(w.shape) for a in _adamw(w[0], g_hg_w_in, m[0], v[0], name="adamw_hg_w_in"))

    w, m, v = given["w_mod"]
    flat = lambda a: a.reshape(-1, nmod)
    upd["w_mod"] = tuple(a.reshape(w.shape) for a in _adamw(flat(w), flat(g_w_mod), flat(m), flat(v), name="adamw_w_mod"))

    snames = ["b_mod"] + SMALL_NAMES
    pw, soffs = _pack_small({n: given[n][0] for n in snames}, snames)
    pm, _ = _pack_small({n: given[n][1] for n in snames}, snames)
    pv, _ = _pack_small({n: given[n][2] for n in snames}, snames)
    pg, _ = _pack_small({n: grads[n] for n in snames}, snames)
    pw, pm, pv, pg = (_pad_rows(a, 8) for a in (pw, pm, pv, pg))
    sd, smn, svn = _adamw(pw, pg, pm, pv, name="adamw_small")
    for n in snames:
        like = given[n][0]
        upd[n] = tuple(_unpack_small(a, soffs, n, like) for a in (sd, smn, svn))

    order = ["w_mod", "b_mod", "norm1_w", "norm2_w", "hg_w_in", "hg_w_out", "hg_lb", "hg_gn_w", "fox_w_in", "fox_b_f",
             "fox_qn_w", "fox_kn_w", "fox_w_out", "mlp_w1", "mlp_w2", "final_w"]
    return (loss, grad_x.reshape(x.shape), *[grads[n] for n in order], *[upd[n][0] for n in order],
            *[upd[n][1] for n in order], *[upd[n][2] for n in order])
```

```python
import math

import jax
import jax.numpy as jnp
from jax import lax
from jax.experimental import pallas as pl
from jax.experimental.pallas import tpu as pltpu

EPS = 1e-6
ADAM_LR, ADAM_B1, ADAM_B2, ADAM_EPS, ADAM_WD, ADAM_STEP = 0.001, 0.9, 0.999, 1e-08, 0.01, 10

F32 = jnp.float32
BF = jnp.bfloat16
LANES = 128
HG_CHUNK = 64
HG_HEADS_PER_STEP = 8
HG_TOKENS_PER_STEP = 256
FOX_ROWS_PER_STEP = 1024
FOX_BWD_TILES = (8, 4, 2, 1)
LOG2E = 1.4426950408889634
FOX_DH = 64
N_CHIPS = 4
N_DEV = 8
VMEM_LIMIT = 56 * 1024 * 1024
MESH = pl.DeviceIdType.MESH

NT = (((1,), (1,)), ((), ()))
TN = (((0,), (0,)), ((), ()))


def _pick(n, pref, mult=LANES):
    if n <= pref:
        return n
    t = (pref // mult) * mult
    while t >= mult:
        if n % t == 0:
            return t
        t -= mult
    raise ValueError((n, pref, mult))


def _cp(*sem):
    return pltpu.CompilerParams(dimension_semantics=sem, vmem_limit_bytes=VMEM_LIMIT)


def _dot(a, b):
    return jnp.dot(a, b, preferred_element_type=F32)


def _dg(a, b, dims):
    return lax.dot_general(a, b, dims, preferred_element_type=F32)


def _split3(x):
    hi = x.astype(BF)
    r1 = x - hi.astype(F32)
    mid = r1.astype(BF)
    lo = (r1 - mid.astype(F32)).astype(BF)
    return hi, mid, lo


def _tri_dot(tri, x):
    hi, mid, lo = _split3(x)
    return _dot(tri, hi) + _dot(tri, mid) + _dot(tri, lo)


def _dg3(a, b, dims):
    ah, bh = a.astype(BF), b.astype(BF)
    al, bl = (a - ah.astype(F32)).astype(BF), (b - bh.astype(F32)).astype(BF)
    return _dg(ah, bh, dims) + _dg(ah, bl, dims) + _dg(al, bh, dims)


def _dg1(a, b, dims):
    return _dg(a.astype(BF), b.astype(BF), dims)


NN = (((1,), (0,)), ((), ()))


def _sigmoid(x):
    return jax.nn.sigmoid(x)


def _ln_matmul(x, nw, sc, sh, w, slab=None, *, relu2, name):
    S, D = x.shape
    N = w.shape[1]
    tm, tn = _pick(S, 512, 16), N
    fused = slab is not None

    def body(x_ref, nw_ref, sc_ref, sh_ref, w_ref, *rest):
        if fused:
            s_ref, *outs, out_ref, hs, send_sems, recv_sems = rest
            finish = _gather_behind(s_ref, out_ref, send_sems, recv_sems, pl.program_id(0), S // tm)
        else:
            outs, hs = rest[:-1], rest[-1]
        h_ref = outs[-1]

        @pl.when(pl.program_id(1) == 0)
        def _():
            xv = x_ref[...]
            r = lax.rsqrt(jnp.mean(xv * xv, axis=-1, keepdims=True) + EPS)
            hb = ((xv * r * nw_ref[...]) * (1.0 + sc_ref[...]) + sh_ref[...]).astype(BF)
            hs[...] = hb
            h_ref[...] = hb

        z = _dot(hs[...], w_ref[...])
        if relu2:
            a = jnp.maximum(z, 0.0)
            outs[0][...] = a.astype(BF)
            outs[1][...] = (a * a).astype(BF)
        else:
            outs[0][...] = z
        if fused:
            finish()

    vec = pl.BlockSpec((1, D), lambda i, j: (0, 0))
    tile = pl.BlockSpec((tm, tn), lambda i, j: (i, j))
    if relu2:
        out_shape = [jax.ShapeDtypeStruct((S, N), BF), jax.ShapeDtypeStruct((S, N), BF)]
        out_specs = [tile, tile]
    else:
        out_shape = [jax.ShapeDtypeStruct((S, N), F32)]
        out_specs = [tile]
    out_shape.append(jax.ShapeDtypeStruct((S, D), BF))
    out_specs.append(pl.BlockSpec((tm, D), lambda i, j: (i, 0)))
    in_specs = [pl.BlockSpec((tm, D), lambda i, j: (i, 0)), vec, vec, vec, pl.BlockSpec((D, tn), lambda i, j: (0, j))]
    scratch = [pltpu.VMEM((tm, D), BF)]
    args = [x, nw, sc, sh, w]
    if fused:
        in_specs.append(HBM)
        out_specs.append(HBM)
        out_shape.append(jax.ShapeDtypeStruct((N_CHIPS,) + slab.shape, slab.dtype))
        scratch += [pltpu.SemaphoreType.DMA((6,)), pltpu.SemaphoreType.DMA((6,))]
        args.append(slab)
    return pl.pallas_call(
        body, name=name, grid=(S // tm, N // tn), in_specs=in_specs, out_specs=out_specs, out_shape=out_shape,
        scratch_shapes=scratch, compiler_params=_cp("arbitrary", "arbitrary"),
    )(*args)


def _matmul_resid(a, w, x, gate, *, name):
    S, K = a.shape
    D = w.shape[1]
    tm, tn = _pick(S, 1024 if K <= 1024 else 512, 16), D

    def body(a_ref, w_ref, x_ref, g_ref, o_ref, y_ref):
        y = _dot(a_ref[...], w_ref[...])
        y_ref[...] = y.astype(BF)
        o_ref[...] = x_ref[...] + g_ref[...] * y

    tile = pl.BlockSpec((tm, tn), lambda i, j: (i, j))
    return pl.pallas_call(
        body, name=name, grid=(S // tm, D // tn),
        in_specs=[pl.BlockSpec((tm, K), lambda i, j: (i, 0)), pl.BlockSpec((K, tn), lambda i, j: (0, j)),
                  tile, pl.BlockSpec((1, tn), lambda i, j: (0, j))],
        out_specs=[tile, tile],
        out_shape=[jax.ShapeDtypeStruct((S, D), F32), jax.ShapeDtypeStruct((S, D), BF)],
        compiler_params=_cp("parallel", "arbitrary"),
    )(a, w, x, gate)


def _gate_matmul_nt(dx, gate, y, w, act, *, name):
    S, D = dx.shape
    K = w.shape[0]
    tm, tn = _pick(S, 1024 if K <= 1024 else 512, 16), K
    fused = act is not None

    def body(dx_ref, g_ref, y_ref, w_ref, *rest):
        if fused:
            act_ref, da_ref, dm_ref, dg_ref, ms = rest
        else:
            da_ref, dm_ref, dg_ref, ms = rest
        i, j = pl.program_id(0), pl.program_id(1)

        @pl.when((i == 0) & (j == 0))
        def _():
            dg_ref[...] = jnp.zeros_like(dg_ref)

        @pl.when(j == 0)
        def _():
            dxv = dx_ref[...]
            dmb = (dxv * g_ref[...]).astype(BF)
            ms[...] = dmb
            dm_ref[...] = dmb
            dg_ref[...] += jnp.sum(dxv * y_ref[...].astype(F32), axis=0, keepdims=True)

        da = _dg(ms[...], w_ref[...], NT)
        if fused:
            da_ref[...] = (da * (2.0 * act_ref[...].astype(F32))).astype(BF)
        else:
            da_ref[...] = da

    row = pl.BlockSpec((tm, D), lambda i, j: (i, 0))
    vec = pl.BlockSpec((1, D), lambda i, j: (0, 0))
    tile = pl.BlockSpec((tm, tn), lambda i, j: (i, j))
    in_specs = [row, vec, row, pl.BlockSpec((tn, D), lambda i, j: (j, 0))]
    args = [dx, gate, y, w]
    if fused:
        in_specs.append(tile)
        args.append(act)
    return pl.pallas_call(
        body, name=name, grid=(S // tm, K // tn),
        in_specs=in_specs, out_specs=[tile, row, vec],
        out_shape=[jax.ShapeDtypeStruct((S, K), BF if fused else F32), jax.ShapeDtypeStruct((S, D), BF),
                   jax.ShapeDtypeStruct((1, D), F32)],
        scratch_shapes=[pltpu.VMEM((tm, D), BF)],
        compiler_params=_cp("arbitrary", "arbitrary"),
    )(*args)


def _matmul_tn(a, b, *, name, into=None):
    S, Ka = a.shape
    P, _, Db = b.shape
    tk, tn, ts = _pick(Ka, 1024), _pick(Db, 1024), _pick(S, 1024, 16)
    if into is not None:
        slab, kind, half, off = into
        C = tn = slab.shape[3]
        per_chip = Ka // N_CHIPS
        all_chips = kind == "row" and tk == Ka
        if kind == "row" and not all_chips:
            tk = min(tk, per_chip)
        assert tn == C and P * Db == (N_CHIPS * C if kind == "col" else C)
        if kind == "col":
            assert tk == Ka and off % tk == 0
        elif all_chips:
            assert off % per_chip == 0
        else:
            assert per_chip % tk == 0 and off % tk == 0
    npb = Db // tn

    def body(a_ref, b_ref, *rest):
        o_ref, acc = rest[-2:]
        s = pl.program_id(2)

        @pl.when(s == 0)
        def _():
            acc[...] = jnp.zeros_like(acc)

        acc[...] += _dg(a_ref[...], b_ref[...], TN)

        @pl.when(s == pl.num_programs(2) - 1)
        def _():
            o_ref[...] = acc[...].reshape(o_ref.shape)

    in_specs = [pl.BlockSpec((ts, tk), lambda i, j, s: (s, i)),
                pl.BlockSpec((None, ts, tn), lambda i, j, s: (j // npb, s, j % npb))]
    args = [a, b]
    if into is None:
        out_spec = pl.BlockSpec((tk, tn), lambda i, j, s: (i, j))
        out_shape = jax.ShapeDtypeStruct((Ka, P * Db), F32)
        aliases = {}
    else:
        per = per_chip // tk if kind == "row" and not all_chips else 1
        if kind == "col":
            out_spec = pl.BlockSpec((None, None, tk, tn), lambda i, j, s: (j, half, off // tk + i, 0))
        elif all_chips:
            out_spec = pl.BlockSpec((N_CHIPS, None, per_chip, tn), lambda i, j, s: (0, half, off // per_chip, 0))
        else:
            out_spec = pl.BlockSpec((None, None, tk, tn), lambda i, j, s: (i // per, half, off // tk + i % per, 0))
        out_shape = jax.ShapeDtypeStruct(slab.shape, F32)
        in_specs.append(pl.BlockSpec(memory_space=pl.ANY))
        args.append(slab)
        aliases = {2: 0}
    return pl.pallas_call(
        body, name=name, grid=(Ka // tk, P * npb, S // ts),
        in_specs=in_specs, out_specs=out_spec, out_shape=out_shape,
        scratch_shapes=[pltpu.VMEM((tk, tn), F32)], input_output_aliases=aliases,
        compiler_params=_cp("parallel", "parallel", "arbitrary"),
    )(*args)


def _matmul_nt_lnbwd(g, w, x, nw, sc, dx_out, part=None, *, name):
    P, S, Dg = g.shape
    D = x.shape[1]
    tm = _pick(S, 512, 16)
    fused = part is not None

    def body(g_ref, w_ref, x_ref, nw_ref, sc_ref, dxo_ref, *rest):
        if fused:
            p_ref, dx_ref, dsc_ref, dsh_ref, dnw_ref, recv_ref, send_sems, recv_sems = rest
            copies = _scatter_copies(p_ref, recv_ref, send_sems, recv_sems)
        else:
            dx_ref, dsc_ref, dsh_ref, dnw_ref = rest

        @pl.when(pl.program_id(0) == 0)
        def _():
            dsc_ref[...] = jnp.zeros_like(dsc_ref)
            dsh_ref[...] = jnp.zeros_like(dsh_ref)
            dnw_ref[...] = jnp.zeros_like(dnw_ref)
            if fused:
                for cp in copies:
                    cp.start()

        dh = _dg(g_ref[0], w_ref[:, 0:Dg], NT)
        for p in range(1, P):
            dh = dh + _dg(g_ref[p], w_ref[:, p * Dg:(p + 1) * Dg], NT)
        xv = x_ref[...]
        nwv = nw_ref[...]
        r = lax.rsqrt(jnp.mean(xv * xv, axis=-1, keepdims=True) + EPS)
        xr = xv * r
        dn = dh * (1.0 + sc_ref[...])
        dsc_ref[...] += jnp.sum(dh * (xr * nwv), axis=0, keepdims=True)
        dsh_ref[...] += jnp.sum(dh, axis=0, keepdims=True)
        dnw_ref[...] += jnp.sum(dn * xr, axis=0, keepdims=True)
        u = dn * nwv
        dx_ref[...] = dxo_ref[...] + r * (u - xr * jnp.mean(u * xr, axis=-1, keepdims=True))

        if fused:
            @pl.when(pl.program_id(0) == S // tm - 1)
            def _():
                for cp in copies:
                    cp.wait()

    row = pl.BlockSpec((tm, D), lambda i: (i, 0))
    vec = pl.BlockSpec((1, D), lambda i: (0, 0))
    in_specs = [pl.BlockSpec((P, tm, Dg), lambda i: (0, i, 0)), pl.BlockSpec((D, P * Dg), lambda i: (0, 0)), row, vec, vec, row]
    out_specs = [row, vec, vec, vec]
    out_shape = [jax.ShapeDtypeStruct((S, D), F32)] + [jax.ShapeDtypeStruct((1, D), F32)] * 3
    scratch, args = [], [g, w, x, nw, sc, dx_out]
    if fused:
        in_specs.append(HBM)
        out_specs.append(HBM)
        out_shape.append(jax.ShapeDtypeStruct((3,) + part.shape[1:], part.dtype))
        scratch = [pltpu.SemaphoreType.DMA((3,)), pltpu.SemaphoreType.DMA((3,))]
        args.append(part)
    return pl.pallas_call(
        body, name=name, grid=(S // tm,), in_specs=in_specs, out_specs=out_specs, out_shape=out_shape,
        scratch_shapes=scratch, compiler_params=_cp("arbitrary"),
    )(*args)


def _loss_kernel(x, fw, tgt, *, name):
    S, D = x.shape
    tm = _pick(S, 512, 8)

    def body(x_ref, fw_ref, t_ref, l_ref, dx_ref, dfw_ref):
        @pl.when(pl.program_id(0) == 0)
        def _():
            l_ref[...] = jnp.zeros_like(l_ref)
            dfw_ref[...] = jnp.zeros_like(dfw_ref)

        xv = x_ref[...]
        fwv = fw_ref[...]
        r = lax.rsqrt(jnp.mean(xv * xv, axis=-1, keepdims=True) + EPS)
        xr = xv * r
        err = xr * fwv - t_ref[...]
        per_tok = jnp.mean(err * err, axis=-1, keepdims=True)
        l_ref[...] += 0.5 * jnp.sum(per_tok, axis=0, keepdims=True)
        dy = err * (1.0 / D)
        dfw_ref[...] += jnp.sum(dy * xr, axis=0, keepdims=True)
        u = dy * fwv
        dx_ref[...] = r * (u - xr * jnp.mean(u * xr, axis=-1, keepdims=True))

    row = pl.BlockSpec((tm, D), lambda i: (i, 0))
    vec = pl.BlockSpec((1, D), lambda i: (0, 0))
    return pl.pallas_call(
        body, name=name, grid=(S // tm,),
        in_specs=[row, vec, row],
        out_specs=[pl.BlockSpec((1, LANES), lambda i: (0, 0)), row, vec],
        out_shape=[jax.ShapeDtypeStruct((1, LANES), F32), jax.ShapeDtypeStruct((S, D), F32),
                   jax.ShapeDtypeStruct((1, D), F32)],
        compiler_params=_cp("arbitrary"),
    )(x, fw, tgt)


def _hg_lower_bound(lb3):
    mx = jnp.max(lb3, axis=0, keepdims=True)
    e = jnp.exp(lb3 - mx)
    p = e / jnp.sum(e, axis=0, keepdims=True)
    return p[0:1, :], p


def _hg_chunk_common(qr, fz, lbv):
    sq = _sigmoid(qr)
    q = qr * sq
    sig = _sigmoid(fz)
    f = lbv + (1.0 - lbv) * sig
    k = (1.0 - lbv) * (1.0 - sig)
    return q, sq, sig, f, k, jnp.log(f)


def _row_of(x, rows, r):
    return jnp.sum(jnp.where(rows == r, x, 0.0), axis=0, keepdims=True)


def _hg_fwd(proj, hg_lb, gn, slab=None, *, name):
    S = proj.shape[0]
    D = proj.shape[1] // 4
    H = D // LANES
    HB = min(HG_HEADS_PER_STEP, H)
    W = HB * LANES
    C = HG_CHUNK
    T = _pick(S, HG_TOKENS_PER_STEP, C)
    nch, nb = T // C, S // T
    ng = H // HB
    fused = slab is not None

    def body(q_ref, fz_ref, v_ref, g_ref, lb_ref, gn_ref, *rest):
        if fused:
            s_ref, y_ref, o_ref, sts_ref, out_ref, st, send_sems, recv_sems = rest
            finish = _gather_behind(s_ref, out_ref, send_sems, recv_sems,
                                    pl.program_id(0) * nb + pl.program_id(1), ng * nb)
        else:
            y_ref, o_ref, sts_ref, st = rest

        @pl.when(pl.program_id(1) == 0)
        def _():
            st[...] = jnp.zeros_like(st)

        lb_all, _ = _hg_lower_bound(lb_ref[...])
        gnv = gn_ref[...]
        ri = lax.broadcasted_iota(jnp.int32, (C, C), 0)
        ci_ = lax.broadcasted_iota(jnp.int32, (C, C), 1)
        low = ri >= ci_
        tri = jnp.where(low, 1.0, 0.0).astype(BF)
        rows_w = lax.broadcasted_iota(jnp.int32, (C, W), 0)

        def chunk(ci, carry):
            sl = pl.ds(pl.multiple_of(ci * C, C), C)
            heads = [slice(hh * LANES, (hh + 1) * LANES) for hh in range(HB)]
            q, _, _, _, k, logf = _hg_chunk_common(q_ref[sl, :], fz_ref[sl, :], lb_all)
            vv, gg = v_ref[sl, :], g_ref[sl, :]
            G = _tri_dot(tri, logf)
            Gm = _row_of(G, rows_w, C // 2 - 1)
            Gl = _row_of(G, rows_w, C - 1)
            qt, kt = q * jnp.exp(G - Gm), k * jnp.exp(Gm - G)
            qe, kd, eGl = q * jnp.exp(G), k * jnp.exp(Gl - G), jnp.exp(Gl)
            A = [jnp.where(low, _dg1(qt[:, ls], kt[:, ls], NT), 0.0) for ls in heads]
            Sv = [st[hh] for hh in range(HB)]
            for hh in range(HB):
                sts_ref[hh, ci] = Sv[hh]
            o = [_dg1(A[hh], vv[:, ls], NN) + _dg1(qe[:, ls], Sv[hh], NT) for hh, ls in enumerate(heads)]
            for hh, ls in enumerate(heads):
                st[hh] = Sv[hh] * eGl[:, ls] + _dg1(vv[:, ls], kd[:, ls], TN)
            gate = gg * _sigmoid(gg)
            for hh, ls in enumerate(heads):
                r = lax.rsqrt(jnp.mean(o[hh] * o[hh], axis=-1, keepdims=True) + EPS)
                y_ref[sl, ls] = ((o[hh] * r * gnv) * gate[:, ls]).astype(BF)
                o_ref[sl, ls] = o[hh]
            return carry

        lax.fori_loop(0, nch, chunk, 0)

        if fused:
            finish()

    def part(p):
        return pl.BlockSpec((T, W), lambda h, n: (n, p * ng + h))

    blk = pl.BlockSpec((T, W), lambda h, n: (n, h))
    in_specs = [part(0), part(1), part(2), part(3),
                pl.BlockSpec((3, W), lambda h, n: (0, h)), pl.BlockSpec((1, LANES), lambda h, n: (0, 0))]
    out_specs = [blk, blk, pl.BlockSpec((HB, nch, LANES, LANES), lambda h, n: (h, n, 0, 0))]
    out_shape = [jax.ShapeDtypeStruct((S, D), BF), jax.ShapeDtypeStruct((S, D), F32),
                 jax.ShapeDtypeStruct((H, S // C, LANES, LANES), F32)]
    scratch = [pltpu.VMEM((HB, LANES, LANES), F32)]
    args = [proj, proj, proj, proj, hg_lb, gn]
    if fused:
        in_specs.append(HBM)
        out_specs.append(HBM)
        out_shape.append(jax.ShapeDtypeStruct((N_CHIPS,) + slab.shape, slab.dtype))
        scratch += [pltpu.SemaphoreType.DMA((6,)), pltpu.SemaphoreType.DMA((6,))]
        args.append(slab)
    return pl.pallas_call(
        body, name=name, grid=(ng, nb), in_specs=in_specs, out_specs=out_specs, out_shape=out_shape,
        scratch_shapes=scratch, compiler_params=_cp("arbitrary", "arbitrary"),
    )(*args)


def _hg_bwd(proj, hg_lb, gn, o_all, states, dy, part=None, *, name):
    S = proj.shape[0]
    D = proj.shape[1] // 4
    H = D // LANES
    HB = min(HG_HEADS_PER_STEP, H)
    W = HB * LANES
    C = HG_CHUNK
    T = _pick(S, HG_TOKENS_PER_STEP, C)
    nch, nb = T // C, S // T
    ng = H // HB
    fused = part is not None

    def body(q_ref, fz_ref, v_ref, g_ref, lb_ref, gn_ref, o_ref, sts_ref, dy_ref, *rest):
        if fused:
            p_ref, dp_ref, dlb_ref, dgn_ref, recv_ref, dst, dlb_acc, send_sems, recv_sems = rest
            copies = _scatter_copies(p_ref, recv_ref, send_sems, recv_sems)

            @pl.when((pl.program_id(0) == 0) & (pl.program_id(1) == 0))
            def _():
                for cp in copies:
                    cp.start()
        else:
            dp_ref, dlb_ref, dgn_ref, dst, dlb_acc = rest
        n = pl.program_id(1)

        @pl.when(n == 0)
        def _():
            dst[...] = jnp.zeros_like(dst)
            dlb_acc[...] = jnp.zeros_like(dlb_acc)
            dgn_ref[...] = jnp.zeros_like(dgn_ref)

        lb_all, p3 = _hg_lower_bound(lb_ref[...])
        gnv = gn_ref[...]
        ri = lax.broadcasted_iota(jnp.int32, (C, C), 0)
        ci_ = lax.broadcasted_iota(jnp.int32, (C, C), 1)
        low = ri >= ci_
        tri = jnp.where(low, 1.0, 0.0).astype(BF)
        triu = jnp.where(ri <= ci_, 1.0, 0.0).astype(BF)
        rows_w = lax.broadcasted_iota(jnp.int32, (C, W), 0)
        gnw = jnp.tile(gnv, (1, HB))

        def chunk(cj, carry):
            ci = nch - 1 - cj
            sl = pl.ds(pl.multiple_of(ci * C, C), C)
            heads = list(enumerate(slice(hh * LANES, (hh + 1) * LANES) for hh in range(HB)))
            wide = lambda parts: jnp.concatenate(parts, axis=1)
            qr, vv, gg = q_ref[sl, :], v_ref[sl, :], g_ref[sl, :]
            q, sq, sig, f, k, logf = _hg_chunk_common(qr, fz_ref[sl, :], lb_all)
            G = _tri_dot(tri, logf)
            Gm = _row_of(G, rows_w, C // 2 - 1)
            Gl = _row_of(G, rows_w, C - 1)
            eG, e_qm, e_km, e_lk, eGl = jnp.exp(G), jnp.exp(G - Gm), jnp.exp(Gm - G), jnp.exp(Gl - G), jnp.exp(Gl)
            qt, kt, kdec, qe = q * e_qm, k * e_km, k * e_lk, q * eG
            sg = _sigmoid(gg)
            d_onw = dy_ref[sl, :] * (gg * sg)
            u = d_onw * gnw
            o = o_ref[sl, :]
            on, do = [], []
            for hh, ls in heads:
                r = lax.rsqrt(jnp.mean(o[:, ls] * o[:, ls], axis=-1, keepdims=True) + EPS)
                on.append(o[:, ls] * r)
                dgn_ref[hh] += jnp.sum(d_onw[:, ls] * on[hh], axis=0, keepdims=True)
                do.append(r * (u[:, ls] - on[hh] * jnp.mean(u[:, ls] * on[hh], axis=-1, keepdims=True)))
            dgg = dy_ref[sl, :] * (wide(on) * gnw) * (sg * (1.0 + gg * (1.0 - sg)))
            Sv = [sts_ref[hh, ci] for hh, _ in heads]
            dSv = [dst[hh] for hh, _ in heads]
            A = [jnp.where(low, _dg1(qt[:, ls], kt[:, ls], NT), 0.0) for _, ls in heads]
            dA = [jnp.where(low, _dg3(do[hh], vv[:, ls], NT), 0.0) for hh, ls in heads]
            dv = wide([_dg1(A[hh], do[hh], TN) + _dg1(kdec[:, ls], dSv[hh], NT) for hh, ls in heads])
            dq = wide([_dg3(dA[hh], kt[:, ls], NN) for hh, ls in heads]) * e_qm \
                + eG * wide([_dg3(do[hh], Sv[hh], NN) for hh, _ in heads])
            dk = wide([_dg3(dA[hh], qt[:, ls], TN) for hh, ls in heads]) * e_km \
                + e_lk * wide([_dg3(vv[:, ls], dSv[hh], NN) for hh, ls in heads])
            s_end = [Sv[hh] * eGl[:, ls] + _dg3(vv[:, ls], kdec[:, ls], TN) for hh, ls in heads]
            dgl = wide([jnp.sum(dSv[hh] * s_end[hh], axis=0, keepdims=True) for hh, _ in heads])
            for hh, ls in heads:
                dst[hh] = dSv[hh] * eGl[:, ls] + _dg1(do[hh], qe[:, ls], TN)
            dG = q * dq - k * dk + jnp.where(rows_w == C - 1, dgl, 0.0)
            dlogf = _tri_dot(triu, dG) - f * dk
            dlf_f = dlogf / f
            dlb_acc[...] += jnp.sum(dlf_f * (1.0 - sig), axis=0, keepdims=True)
            dp_ref[0, sl, :] = (dq * (sq * (1.0 + qr * (1.0 - sq)))).astype(BF)
            dp_ref[1, sl, :] = (dlf_f * (1.0 - lb_all) * sig * (1.0 - sig)).astype(BF)
            dp_ref[2, sl, :] = dv.astype(BF)
            dp_ref[3, sl, :] = dgg.astype(BF)
            return carry

        lax.fori_loop(0, nch, chunk, 0)
        sel = jnp.where(lax.broadcasted_iota(jnp.int32, (3, W), 0) == 0, 1.0, 0.0)
        dlb_ref[...] = lb_all * (sel - p3) * dlb_acc[...]

        if fused:
            @pl.when((pl.program_id(0) == ng - 1) & (n == nb - 1))
            def _():
                for cp in copies:
                    cp.wait()

    def col(p):
        return pl.BlockSpec((T, W), lambda h, n: (nb - 1 - n, p * ng + h))

    blk = pl.BlockSpec((T, W), lambda h, n: (nb - 1 - n, h))
    in_specs = [col(0), col(1), col(2), col(3),
                pl.BlockSpec((3, W), lambda h, n: (0, h)), pl.BlockSpec((1, LANES), lambda h, n: (0, 0)),
                blk, pl.BlockSpec((HB, nch, LANES, LANES), lambda h, n: (h, nb - 1 - n, 0, 0)), blk]
    out_specs = [pl.BlockSpec((4, T, W), lambda h, n: (0, nb - 1 - n, h)),
                 pl.BlockSpec((3, W), lambda h, n: (0, h)),
                 pl.BlockSpec((HB, 1, LANES), lambda h, n: (h, 0, 0))]
    out_shape = [jax.ShapeDtypeStruct((4, S, D), BF), jax.ShapeDtypeStruct((3, D), F32),
                 jax.ShapeDtypeStruct((H, 1, LANES), F32)]
    scratch = [pltpu.VMEM((HB, LANES, LANES), F32), pltpu.VMEM((1, W), F32)]
    args = [proj, proj, proj, proj, hg_lb, gn, o_all, states, dy]
    if fused:
        in_specs.append(HBM)
        out_specs.append(HBM)
        out_shape.append(jax.ShapeDtypeStruct((3,) + part.shape[1:], part.dtype))
        scratch += [pltpu.SemaphoreType.DMA((3,)), pltpu.SemaphoreType.DMA((3,))]
        args.append(part)
    return pl.pallas_call(
        body, name=name, grid=(ng, nb), in_specs=in_specs, out_specs=out_specs, out_shape=out_shape,
        scratch_shapes=scratch, compiler_params=_cp("arbitrary", "arbitrary"),
    )(*args)


def _log_sigmoid(u):
    return jnp.minimum(u, 0.0) - jnp.log(1.0 + jnp.exp(-jnp.abs(u)))


def _lane_put(base, lane, first, pieces):
    for n, p in enumerate(pieces):
        base = jnp.where(lane == first + n, p, base)
    return base


def _fox_cumsum(proj, bf_pad, *, name):
    S = proj.shape[0]
    D = proj.shape[1] // 5
    T = _pick(S, 256, 8)

    def body(fz_ref, b_ref, f_ref, carry):
        @pl.when(pl.program_id(0) == 0)
        def _():
            carry[...] = jnp.zeros_like(carry)

        logf = _log_sigmoid(fz_ref[...] + b_ref[...])
        tri = jnp.where(lax.broadcasted_iota(jnp.int32, (T, T), 0) >= lax.broadcasted_iota(jnp.int32, (T, T), 1),
                        1.0, 0.0).astype(BF)
        fv = _tri_dot(tri, logf) + carry[...]
        f_ref[...] = fv
        carry[...] = _row_of(fv, lax.broadcasted_iota(jnp.int32, (T, LANES), 0), T - 1)

    return pl.pallas_call(
        body, name=name, grid=(S // T,),
        in_specs=[pl.BlockSpec((T, LANES), lambda i: (i, 4 * D // LANES)), pl.BlockSpec((1, LANES), lambda i: (0, 0))],
        out_specs=pl.BlockSpec((T, LANES), lambda i: (i, 0)),
        out_shape=jax.ShapeDtypeStruct((S, LANES), F32),
        scratch_shapes=[pltpu.VMEM((1, LANES), F32)],
        compiler_params=_cp("arbitrary"),
    )(proj, bf_pad)


def _pair_stats(sq, lo):
    del lo
    a = lax.broadcasted_iota(jnp.int32, (LANES, LANES), 0) < FOX_DH
    b = lax.broadcasted_iota(jnp.int32, (LANES, LANES), 1) < FOX_DH
    avg = jnp.where(a == b, 1.0 / FOX_DH, 0.0).astype(BF)
    hi, mid, low = _split3(sq)
    return _dot(hi, avg) + _dot(mid, avg) + _dot(low, avg)


def _fox_prep(proj, fcum, qw2, kw2, *, name):
    S = proj.shape[0]
    D = proj.shape[1] // 5
    HP = D // LANES
    T = _pick(S, FOX_ROWS_PER_STEP, 16)

    def body(q_ref, k_ref, v_ref, f_ref, qw_ref, kw_ref, qa_ref, ka_ref, va_ref, vt_ref):
        hp = pl.program_id(1)
        lane = lax.broadcasted_iota(jnp.int32, (T, LANES), 1)
        lo = lane < FOX_DH
        qv, kv, vv, fv = q_ref[...], k_ref[...], v_ref[...], f_ref[...]
        qn = qv * lax.rsqrt(_pair_stats(qv * qv, lo) + EPS) * qw_ref[...] * (0.125 * LOG2E)
        kn = kv * lax.rsqrt(_pair_stats(kv * kv, lo) + EPS) * kw_ref[...]
        ones_q = jnp.where((lane >= 67) & (lane <= 69), 1.0, 0.0)
        ones_k = jnp.where(((lane >= 64) & (lane <= 66)) | ((lane >= 70) & (lane <= 72)), 1.0, 0.0)
        ones_v = jnp.where((lane >= 64) & (lane <= 66), 1.0, 0.0)
        for hh in range(2):
            fh = jnp.sum(jnp.where(lane == 2 * hp + hh, fv, 0.0), axis=-1, keepdims=True) * LOG2E
            pieces = [p.astype(F32) for p in _split3(fh)]

            def half(x):
                return jnp.where(lo, x if hh == 0 else pltpu.roll(x, FOX_DH, 1), 0.0)

            qa_ref[hh] = _lane_put(half(qn) + ones_q, lane, 64, pieces).astype(BF)
            ka_ref[hh] = _lane_put(half(kn) + ones_k, lane, 67, [-p for p in pieces]).astype(BF)
            va = half(vv) + ones_v
            va_ref[hh] = va.astype(BF)
            vt_ref[hh] = va.T.astype(BF)

    def part(p):
        return pl.BlockSpec((T, LANES), lambda i, hp: (i, p * HP + hp))

    vec = pl.BlockSpec((1, LANES), lambda i, hp: (0, 0))
    aug = pl.BlockSpec((2, T, LANES), lambda i, hp: (hp, i, 0))
    return pl.pallas_call(
        body, name=name, grid=(S // T, HP),
        in_specs=[part(0), part(1), part(2), pl.BlockSpec((T, LANES), lambda i, hp: (i, 0)), vec, vec],
        out_specs=[aug, aug, aug, pl.BlockSpec((2, LANES, T), lambda i, hp: (hp, 0, i))],
        out_shape=[jax.ShapeDtypeStruct((2 * HP, S, LANES), BF)] * 3 + [jax.ShapeDtypeStruct((2 * HP, LANES, S), BF)],
        compiler_params=_cp("parallel", "arbitrary"),
    )(proj, proj, proj, fcum, qw2, kw2)


def _fox_block(S):
    return _pick(S, 256, 16)


def _fox_skip_bounds(fcum, qn_w, kn_w, nheads):
    S = fcum.shape[0]
    B = _fox_block(S)
    qk = 8.0 * LOG2E * 1.02 * jnp.max(jnp.abs(qn_w)) * jnp.max(jnp.abs(kn_w))
    thresh = -(2.0 * qk + 160.0)
    f2 = fcum[:, :nheads] * LOG2E
    first, last = f2[0::B], f2[B - 1::B]
    nb = S // B
    blk = jnp.arange(nb)
    dead = (first[0::2, None, :] - last[None, :, :]) < thresh
    jmin = jnp.sum(dead & (blk[None, :, None] < 2 * jnp.arange(nb // 2)[:, None, None]), axis=1)
    live = (first[:, None, :] - last[None, :, :]) >= thresh
    imax = blk[:, None] + jnp.sum(live & (blk[:, None, None] > blk[None, :, None]), axis=0)
    jmin, imax = jnp.zeros_like(jmin), jnp.full_like(imax, nb - 1)
    return jmin.T.astype(jnp.int32), imax.T.astype(jnp.int32)


def _fox_fwd(jmin, qa, ka, vat, proj, *, name):
    H, S, _ = qa.shape
    HP = H // 2
    D = HP * LANES
    B = _fox_block(S)
    BQ = 2 * B
    nq = S // BQ

    def body(jmin_ref, q_ref, k_ref, vt_ref, g_ref, y_ref, o_ref, q2_ref):
        hp, i = pl.program_id(0), pl.program_id(1)
        lane = lax.broadcasted_iota(jnp.int32, (BQ, LANES), 1)
        lo = lane < FOX_DH
        in_stat = (lane >= 70) & (lane <= 75)
        causal = lax.broadcasted_iota(jnp.int32, (BQ, BQ), 0) <= lax.broadcasted_iota(jnp.int32, (BQ, BQ), 1)
        row = lax.broadcasted_iota(jnp.int32, (LANES, BQ), 0)
        m0, acc0 = jnp.full((1, BQ), -jnp.inf, F32), jnp.zeros((LANES, BQ), F32)
        outs = []
        for hh in range(2):
            qb = q_ref[hh]

            def block(j, carry, masked=False):
                m, acc = carry
                sl = pl.ds(pl.multiple_of(j * BQ, BQ), BQ)
                st = _dg(k_ref[hh, sl, :], qb, NT)
                if masked:
                    st = jnp.where(causal, st, -jnp.inf)
                m_new = jnp.maximum(m, jnp.ceil(jnp.max(st, axis=0, keepdims=True)))
                p = jnp.exp2(st - m_new).astype(BF)
                return m_new, acc * jnp.exp2(m - m_new) + _dot(vt_ref[hh, :, sl], p)

            carry = lax.fori_loop(jmin_ref[2 * hp + hh, i] // 2, i, block, (m0, acc0))
            m, acc = block(i, carry, masked=True)
            linv = 1.0 / jnp.sum(jnp.where(row == FOX_DH, acc, 0.0), axis=0, keepdims=True)
            tile = acc * linv
            for n, piece in enumerate(_split3(m) + _split3(linv)):
                tile = jnp.where(row == 70 + n, piece.astype(F32), tile)
            tile = tile.T
            outs.append(tile)
            q2_ref[hh] = jnp.where(in_stat, jnp.where(lane <= 72, -tile, tile), qb.astype(F32)).astype(BF)
        o = jnp.where(lo, outs[0], pltpu.roll(outs[1], FOX_DH, 1))
        o_ref[...] = o
        y_ref[...] = (o * _sigmoid(g_ref[...])).astype(BF)

    blk = pl.BlockSpec((BQ, LANES), lambda hp, i, jm: (i, hp))
    qblk = pl.BlockSpec((2, BQ, LANES), lambda hp, i, jm: (hp, i, 0))
    full = pl.BlockSpec((2, S, LANES), lambda hp, i, jm: (hp, 0, 0))
    full_t = pl.BlockSpec((2, LANES, S), lambda hp, i, jm: (hp, 0, 0))
    return pl.pallas_call(
        body, name=name,
        grid_spec=pltpu.PrefetchScalarGridSpec(
            num_scalar_prefetch=1, grid=(HP, nq),
            in_specs=[qblk, full, full_t, pl.BlockSpec((BQ, LANES), lambda hp, i, jm: (i, 3 * HP + hp))],
            out_specs=[blk, blk, qblk]),
        out_shape=[jax.ShapeDtypeStruct((S, D), BF), jax.ShapeDtypeStruct((S, D), F32),
                   jax.ShapeDtypeStruct((H, S, LANES), BF)],
        compiler_params=_cp("parallel", "arbitrary"),
    )(jmin, qa, ka, vat, proj)


def _fox_bwd_prep(dy, o, proj, q2, *, name):
    S, D = dy.shape
    HP = D // LANES
    T = _pick(S, FOX_ROWS_PER_STEP, 16)

    def body(dy_ref, o_ref, g_ref, q2_ref, da_ref):
        lane = lax.broadcasted_iota(jnp.int32, (T, LANES), 1)
        lo = lane < FOX_DH
        in_linv = (lane >= 73) & (lane <= 75)
        linv = [jnp.sum(jnp.where(in_linv, q2_ref[hh].astype(F32), 0.0), axis=-1, keepdims=True) for hh in range(2)]
        u = (dy_ref[...] * _sigmoid(g_ref[...]) * jnp.where(lo, linv[0], linv[1])).astype(BF).astype(F32)
        prod = u * o_ref[...]
        d_lo = jnp.sum(jnp.where(lo, prod, 0.0), axis=-1, keepdims=True)
        d_hi = jnp.sum(jnp.where(lo, 0.0, prod), axis=-1, keepdims=True)
        for hh, delta in enumerate((d_lo, d_hi)):
            base = jnp.where(lo, u if hh == 0 else pltpu.roll(u, FOX_DH, 1), 0.0)
            da_ref[hh] = _lane_put(base, lane, 64, [-(p.astype(F32)) for p in _split3(delta)]).astype(BF)

    blk = pl.BlockSpec((T, LANES), lambda i, hp: (i, hp))
    aug = pl.BlockSpec((2, T, LANES), lambda i, hp: (hp, i, 0))
    return pl.pallas_call(
        body, name=name, grid=(S // T, HP),
        in_specs=[blk, blk, pl.BlockSpec((T, LANES), lambda i, hp: (i, 3 * HP + hp)), aug],
        out_specs=aug,
        out_shape=jax.ShapeDtypeStruct((2 * HP, S, LANES), BF),
        compiler_params=_cp("parallel", "arbitrary"),
    )(dy, o, proj, q2)


def _fox_bwd(imax, q2, ka, va, doa, *, name):
    H, S, _ = q2.shape
    B = _fox_block(S)
    nb = S // B

    def body(imax_ref, q_ref, do_ref, k_ref, v_ref, dq_ref, dk_ref, dv_ref, cs_ref):
        j = pl.program_id(1)
        end = imax_ref[pl.program_id(0), j] + 1

        @pl.when(j == 0)
        def _():
            dq_ref[...] = jnp.zeros_like(dq_ref)

        kb, vb = k_ref[...], v_ref[...]

        def step(i, carry, nblk=1):
            dk_acc, dv_acc, cs_acc = carry
            rows = nblk * B
            sl = pl.ds(pl.multiple_of(i * B, B), rows)
            qb, dob = q_ref[sl, :], do_ref[sl, :]
            s = _dg(qb, kb, NT)
            ahead = lax.broadcasted_iota(jnp.int32, (rows, B), 0) - lax.broadcasted_iota(jnp.int32, (rows, B), 1)
            pb = jnp.exp2(jnp.where(ahead >= (j - i) * B, s, -jnp.inf)).astype(BF)
            ds = pb.astype(F32) * _dg(dob, vb, NT)
            dsb = ds.astype(BF)
            cs_acc = cs_acc + jnp.sum(ds.reshape(rows // 8, 8, B), axis=0)
            dv_acc = dv_acc + _dg(pb, dob, TN)
            dk_acc = dk_acc + _dg(dsb, qb, TN)
            dq_ref[sl, :] += _dot(dsb, kb)
            return dk_acc, dv_acc, cs_acc

        zero = jnp.zeros((B, LANES), F32)
        carry = (zero, zero, jnp.zeros((8, B), F32))
        pos = j
        for U in FOX_BWD_TILES:
            n = (end - pos) // U
            carry = lax.fori_loop(0, n, lambda ii, c, pos=pos, U=U: step(pos + U * ii, c, nblk=U), carry)
            pos = pos + U * n
        dk_acc, dv_acc, cs_acc = carry
        dk_ref[...] = dk_acc
        dv_ref[...] = dv_acc
        cs_ref[...] = jnp.sum(cs_acc, axis=0, keepdims=True)

    full = pl.BlockSpec((None, S, LANES), lambda h, j, im: (h, 0, 0))
    blk = pl.BlockSpec((None, B, LANES), lambda h, j, im: (h, j, 0))
    return pl.pallas_call(
        body, name=name,
        grid_spec=pltpu.PrefetchScalarGridSpec(
            num_scalar_prefetch=1, grid=(H, nb),
            in_specs=[full, full, blk, blk],
            out_specs=[full, blk, blk, pl.BlockSpec((None, 1, B), lambda h, j, im: (h, 0, j))]),
        out_shape=[jax.ShapeDtypeStruct((H, S, LANES), F32)] * 3 + [jax.ShapeDtypeStruct((H, 1, S), F32)],
        compiler_params=_cp("parallel", "arbitrary"),
    )(imax, q2, doa, ka, va)


def _fox_bwd_post(dqa, dka, dva, proj, dy, o, qw2, kw2, *, name):
    S, D = dy.shape
    HP = D // LANES
    T = _pick(S, FOX_ROWS_PER_STEP, 16)

    def body(dq_ref, dk_ref, dv_ref, q_ref, k_ref, g_ref, dy_ref, o_ref, qw_ref, kw_ref, dp_ref, dqw_ref, dkw_ref):
        @pl.when((pl.program_id(0) == 0) & (pl.program_id(1) == 0))
        def _():
            dqw_ref[...] = jnp.zeros_like(dqw_ref)
            dkw_ref[...] = jnp.zeros_like(dkw_ref)

        lane = lax.broadcasted_iota(jnp.int32, (T, LANES), 1)
        lo = lane < FOX_DH

        def pair(ref):
            return jnp.where(lo, ref[0], pltpu.roll(ref[1], FOX_DH, 1))

        def norm_bwd(xv, w, dyn, dw_ref):
            r = lax.rsqrt(_pair_stats(xv * xv, lo) + EPS)
            xr = xv * r
            dw_ref[...] += jnp.sum(dyn * xr, axis=0, keepdims=True)
            u = dyn * w
            return r * (u - xr * _pair_stats(u * xr, lo))

        dp_ref[0] = norm_bwd(q_ref[...], qw_ref[...], pair(dq_ref) * 0.125, dqw_ref).astype(BF)
        dp_ref[1] = norm_bwd(k_ref[...], kw_ref[...], pair(dk_ref) * (1.0 / LOG2E), dkw_ref).astype(BF)
        dp_ref[2] = pair(dv_ref).astype(BF)
        sg = _sigmoid(g_ref[...])
        dp_ref[3] = (dy_ref[...] * o_ref[...] * sg * (1.0 - sg)).astype(BF)

    def part(p):
        return pl.BlockSpec((T, LANES), lambda i, hp: (i, p * HP + hp))

    aug = pl.BlockSpec((2, T, LANES), lambda i, hp: (hp, i, 0))
    blk = pl.BlockSpec((T, LANES), lambda i, hp: (i, hp))
    vec = pl.BlockSpec((1, LANES), lambda i, hp: (0, 0))
    return pl.pallas_call(
        body, name=name, grid=(S // T, HP),
        in_specs=[aug, aug, aug, part(0), part(1), part(3), blk, blk, vec, vec],
        out_specs=[pl.BlockSpec((4, T, LANES), lambda i, hp: (0, i, hp)), vec, vec],
        out_shape=[jax.ShapeDtypeStruct((5, S, D), BF), jax.ShapeDtypeStruct((1, LANES), F32),
                   jax.ShapeDtypeStruct((1, LANES), F32)],
        compiler_params=_cp("arbitrary", "arbitrary"),
    )(dqa, dka, dva, proj, proj, proj, dy, o, qw2, kw2)


def _fox_dfz(colsum, nheads, proj, bf_pad, dproj, *, name):
    S = colsum.shape[0]
    H = nheads
    D = dproj.shape[2]
    T = _pick(S, 256, 16)
    nb = S // T

    def body(cs_ref, fz_ref, b_ref, _, dp_ref, db_ref, carry):
        @pl.when(pl.program_id(0) == 0)
        def _():
            carry[...] = jnp.zeros_like(carry)
            db_ref[...] = jnp.zeros_like(db_ref)

        lane = lax.broadcasted_iota(jnp.int32, (T, LANES), 1)
        df = -cs_ref[...]
        triu = jnp.where(lax.broadcasted_iota(jnp.int32, (T, T), 0) <= lax.broadcasted_iota(jnp.int32, (T, T), 1),
                         1.0, 0.0).astype(BF)
        dlogf = _tri_dot(triu, df) + carry[...]
        carry[...] = _row_of(dlogf, lax.broadcasted_iota(jnp.int32, (T, LANES), 0), 0)
        dfz = jnp.where(lane < H, dlogf * _sigmoid(-(fz_ref[...] + b_ref[...])), 0.0)
        db_ref[...] += jnp.sum(dfz, axis=0, keepdims=True)
        dp_ref[...] = jnp.zeros_like(dp_ref)
        dp_ref[:, 0:LANES] = dfz.astype(BF)

    return pl.pallas_call(
        body, name=name, grid=(nb,),
        in_specs=[pl.BlockSpec((T, LANES), lambda i: (nb - 1 - i, 0)),
                  pl.BlockSpec((T, LANES), lambda i: (nb - 1 - i, 4 * D // LANES)),
                  pl.BlockSpec((1, LANES), lambda i: (0, 0)),
                  pl.BlockSpec(memory_space=pl.ANY)],
        out_specs=[pl.BlockSpec((None, T, D), lambda i: (4, nb - 1 - i, 0)), pl.BlockSpec((1, LANES), lambda i: (0, 0))],
        out_shape=[jax.ShapeDtypeStruct(dproj.shape, BF), jax.ShapeDtypeStruct((1, LANES), F32)],
        scratch_shapes=[pltpu.VMEM((1, LANES), F32)],
        input_output_aliases={3: 0},
        compiler_params=_cp("arbitrary"),
    )(colsum, proj, bf_pad, dproj)


def _mod_fwd(c16, w, b, *, name):
    L, D, N = w.shape
    tn = _pick(N, 512)

    def body(c_ref, w_ref, b_ref, o_ref):
        cv = c_ref[...]
        ca = (cv * _sigmoid(cv)).astype(BF)
        o_ref[...] = _dot(ca, w_ref[...].astype(BF)) + b_ref[...]

    return pl.pallas_call(
        body, name=name, grid=(L, N // tn),
        in_specs=[pl.BlockSpec((16, D), lambda l, j: (0, 0)), pl.BlockSpec((None, D, tn), lambda l, j: (l, 0, j)),
                  pl.BlockSpec((None, 1, tn), lambda l, j: (l, 0, j))],
        out_specs=pl.BlockSpec((None, 16, tn), lambda l, j: (l, 0, j)),
        out_shape=jax.ShapeDtypeStruct((L, 16, N), F32),
        compiler_params=_cp("parallel", "arbitrary"),
    )(c16, w, b)


def _mod_bwd(c16, dmod, *, name):
    L, _, N = dmod.shape
    D = c16.shape[1]
    tn = _pick(N, 512)

    def body(c_ref, d_ref, o_ref):
        cv = c_ref[...]
        ca = (cv * _sigmoid(cv)).astype(BF)
        o_ref[...] = _dg(ca, d_ref[...].astype(BF), TN)

    return pl.pallas_call(
        body, name=name, grid=(L, N // tn),
        in_specs=[pl.BlockSpec((16, D), lambda l, j: (0, 0)), pl.BlockSpec((None, 16, tn), lambda l, j: (l, 0, j))],
        out_specs=pl.BlockSpec((None, D, tn), lambda l, j: (l, 0, j)),
        out_shape=jax.ShapeDtypeStruct((L, D, N), F32),
        compiler_params=_cp("parallel", "arbitrary"),
    )(c16, dmod)


def _adamw_math(w, g, m, v):
    m = ADAM_B1 * m + (1.0 - ADAM_B1) * g
    v = ADAM_B2 * v + (1.0 - ADAM_B2) * (g * g)
    m_hat = m / (1.0 - ADAM_B1 ** ADAM_STEP)
    v_hat = v / (1.0 - ADAM_B2 ** ADAM_STEP)
    return -ADAM_LR * (m_hat / (jnp.sqrt(v_hat) + ADAM_EPS) + ADAM_WD * w), m, v


def _adamw(w, g, m, v, *, g_at=None, name):
    R, C = w.shape
    row0 = 0 if g_at is None else g_at[1]
    tr = min(math.gcd(row0, 256) if row0 else 256, -(-R // 8) * 8)
    g0 = row0 // tr
    if g_at is None:
        g_spec = pl.BlockSpec((tr, C), lambda i: (i, 0))
    else:
        g_spec = pl.BlockSpec((None, tr, C), lambda i: (g_at[0], g0 + i, 0))

    def body(w_ref, g_ref, m_ref, v_ref, d_ref, mo_ref, vo_ref):
        d, mn, vn = _adamw_math(w_ref[...], g_ref[...], m_ref[...], v_ref[...])
        d_ref[...] = d
        mo_ref[...] = mn
        vo_ref[...] = vn

    blk = pl.BlockSpec((tr, C), lambda i: (i, 0))
    return pl.pallas_call(
        body, name=name, grid=(pl.cdiv(R, tr),),
        in_specs=[blk, g_spec, blk, blk],
        out_specs=[blk, blk, blk],
        out_shape=[jax.ShapeDtypeStruct((R, C), F32)] * 3,
        compiler_params=_cp("parallel"),
    )(w, g, m, v)


def _sum_parts(parts, *, name):
    P, R, C = parts.shape

    def body(p_ref, o_ref):
        acc = p_ref[0]
        for p in range(1, P):
            acc = acc + p_ref[p]
        o_ref[...] = acc

    return pl.pallas_call(
        body, name=name, grid=(1,),
        in_specs=[pl.BlockSpec((P, R, C), lambda i: (0, 0, 0))],
        out_specs=pl.BlockSpec((R, C), lambda i: (0, 0)),
        out_shape=jax.ShapeDtypeStruct((R, C), F32),
        compiler_params=_cp("arbitrary"),
    )(parts)


def _add_halves(g4, recv, c_idx, *, name):
    _, _, Rh, C = g4.shape
    tr = min(256, Rh)

    def body(c_ref, a_ref, b_ref, o_ref):
        o_ref[...] = (a_ref[...] + b_ref[...].astype(F32)).astype(BF)

    return pl.pallas_call(
        body, name=name,
        grid_spec=pltpu.PrefetchScalarGridSpec(
            num_scalar_prefetch=1, grid=(4, pl.cdiv(Rh, tr)),
            in_specs=[pl.BlockSpec((None, None, tr, C), lambda j, r, c: (j, c[0], r, 0)),
                      pl.BlockSpec((None, tr, C), lambda j, r, c: (j, r, 0))],
            out_specs=pl.BlockSpec((None, tr, C), lambda j, r, c: (j, r, 0))),
        out_shape=jax.ShapeDtypeStruct((4, Rh, C), BF),
        compiler_params=_cp("parallel", "arbitrary"),
    )(c_idx, g4, recv)


def _add_four(g4, from_sibling, from_chips, pos, *, name):
    _, _, Rh, C = g4.shape
    tr = min(256, Rh)

    def body(p_ref, a_ref, s_ref, b_ref, o_ref):
        own = a_ref[...] + s_ref[...].astype(F32)
        o_ref[...] = ((own + b_ref[0].astype(F32)) + b_ref[1].astype(F32)) + b_ref[2].astype(F32)

    return pl.pallas_call(
        body, name=name,
        grid_spec=pltpu.PrefetchScalarGridSpec(
            num_scalar_prefetch=1, grid=(pl.cdiv(Rh, tr),),
            in_specs=[pl.BlockSpec((None, None, tr, C), lambda r, p: (p[0], p[1], r, 0)),
                      pl.BlockSpec((None, tr, C), lambda r, p: (p[0], r, 0)),
                      pl.BlockSpec((3, tr, C), lambda r, p: (0, r, 0))],
            out_specs=pl.BlockSpec((None, tr, C), lambda r, p: (p[1], r, 0))),
        out_shape=jax.ShapeDtypeStruct((2, Rh, C), F32),
        compiler_params=_cp("arbitrary"),
    )(pos, g4, from_sibling, from_chips)


HBM = pl.BlockSpec(memory_space=pltpu.HBM)


def _mesh_pos():
    return lax.axis_index("x"), lax.axis_index("y"), lax.axis_index("c")


def _other_chips(x, y):
    return [(1 - x, y), (x, 1 - y), (1 - x, 1 - y)]


def _allgather_small(xs, *, name):
    m_per, n = xs.shape

    def body(x_ref, out_ref, send_sems, recv_sems, local_sem):
        x, y, c = _mesh_pos()
        me, sibling = (x, y, c), (x, y, 1 - c)
        chips = _other_chips(x, y)

        def rows(px, py, pc):
            return out_ref.at[pl.ds((4 * px + 2 * py + pc) * m_per, m_per), :]

        def copy(k, block, to, src=None):
            return pltpu.make_async_remote_copy(
                src_ref=rows(*block) if src is None else src, dst_ref=rows(*block),
                send_sem=send_sems.at[k], recv_sem=recv_sems.at[k], device_id=to, device_id_type=MESH)

        mine = pltpu.make_async_copy(x_ref, rows(*me), local_sem)
        mine.start()
        first = [copy(0, me, sibling, src=x_ref)]
        first += [copy(1 + j, me, (*chip, c), src=x_ref) for j, chip in enumerate(chips)]
        for cp in first:
            cp.start()
        passed = [copy(4 + j, (*chip, c), sibling) for j, chip in enumerate(chips)]
        for j, chip in enumerate(chips):
            copy(1 + j, (*chip, c), me).wait_recv()
            passed[j].start()
        copy(0, sibling, me).wait_recv()
        for j, chip in enumerate(chips):
            copy(4 + j, (*chip, 1 - c), me).wait_recv()
        for cp in first + passed:
            cp.wait_send()
        mine.wait()

    return pl.pallas_call(
        body, name=name,
        out_shape=jax.ShapeDtypeStruct((N_DEV * m_per, n), xs.dtype),
        in_specs=[pl.BlockSpec(memory_space=pltpu.VMEM)],
        out_specs=pl.BlockSpec(memory_space=pltpu.VMEM),
        scratch_shapes=[pltpu.SemaphoreType.DMA((7,)), pltpu.SemaphoreType.DMA((7,)), pltpu.SemaphoreType.DMA],
    )(xs)


def _chip_slab_copies(s_ref, out_ref, send_sems, recv_sems):
    R = s_ref.shape[0]
    Rh = R // 2
    x, y, c = _mesh_pos()
    me, sibling = (x, y, c), (x, y, 1 - c)
    chips = _other_chips(x, y)

    def half(px, py, pc):
        return out_ref.at[2 * px + py, pl.ds(pc * Rh, Rh), :]

    def copy(k, block, to, src=None):
        return pltpu.make_async_remote_copy(
            src_ref=half(*block) if src is None else src, dst_ref=half(*block),
            send_sem=send_sems.at[k], recv_sem=recv_sems.at[k], device_id=to, device_id_type=MESH)

    first = [copy(j, me, (*chip, c), src=s_ref.at[pl.ds(c * Rh, Rh), :]) for j, chip in enumerate(chips)]
    passed = [copy(3 + j, (*chip, c), sibling) for j, chip in enumerate(chips)]
    landed = [copy(j, (*chip, c), me) for j, chip in enumerate(chips)]
    from_sibling = [copy(3 + j, (*chip, 1 - c), me) for j, chip in enumerate(chips)]
    return first, passed, landed, from_sibling


def _gather_behind(s_ref, out_ref, send_sems, recv_sems, step, nsteps):
    first, passed, landed, from_sibling = _chip_slab_copies(s_ref, out_ref, send_sems, recv_sems)

    @pl.when(step == 0)
    def _():
        for cp in first:
            cp.start()

    @pl.when(step == (3 * nsteps) // 4)
    def _():
        for arrived, onward in zip(landed, passed):
            arrived.wait_recv()
            onward.start()

    def finish():
        @pl.when(step == nsteps - 1)
        def _():
            for cp in from_sibling:
                cp.wait_recv()
            for cp in first + passed:
                cp.wait_send()

    return finish


def _allgather_chip_slabs(slab, *, name):
    R, C = slab.shape

    def body(s_ref, out_ref, send_sems, recv_sems):
        first, passed, landed, from_sibling = _chip_slab_copies(s_ref, out_ref, send_sems, recv_sems)
        for cp in first:
            cp.start()
        for arrived, onward in zip(landed, passed):
            arrived.wait_recv()
            onward.start()
        for cp in from_sibling:
            cp.wait_recv()
        for cp in first + passed:
            cp.wait_send()

    return pl.pallas_call(
        body, name=name,
        out_shape=jax.ShapeDtypeStruct((N_CHIPS, R, C), slab.dtype),
        in_specs=[HBM], out_specs=HBM,
        scratch_shapes=[pltpu.SemaphoreType.DMA((6,)), pltpu.SemaphoreType.DMA((6,))],
    )(slab)


def _swap_halves(mine, *, name):
    def body(g_ref, out_ref, send_sems, recv_sems):
        x, y, c = _mesh_pos()
        copies = [pltpu.make_async_remote_copy(
            src_ref=g_ref.at[j], dst_ref=out_ref.at[j], send_sem=send_sems.at[j], recv_sem=recv_sems.at[j],
            device_id=(x, y, 1 - c), device_id_type=MESH) for j in range(N_CHIPS)]
        for cp in copies:
            cp.start()
        for cp in copies:
            cp.wait()

    return pl.pallas_call(
        body, name=name,
        out_shape=jax.ShapeDtypeStruct(mine.shape, mine.dtype),
        in_specs=[HBM], out_specs=HBM,
        scratch_shapes=[pltpu.SemaphoreType.DMA((N_CHIPS,)), pltpu.SemaphoreType.DMA((N_CHIPS,))],
    )(mine)


def _scatter_copies(p_ref, out_ref, send_sems, recv_sems):
    x, y, c = _mesh_pos()
    return [pltpu.make_async_remote_copy(
        src_ref=p_ref.at[2 * px + py], dst_ref=out_ref.at[j], send_sem=send_sems.at[j], recv_sem=recv_sems.at[j],
        device_id=(px, py, c), device_id_type=MESH) for j, (px, py) in enumerate(_other_chips(x, y))]


def _join_halves(buf, *, name):
    def body(b_ref, out_ref, send_sem, recv_sem):
        x, y, c = _mesh_pos()
        cp = pltpu.make_async_remote_copy(
            src_ref=b_ref.at[c], dst_ref=out_ref.at[c], send_sem=send_sem, recv_sem=recv_sem,
            device_id=(x, y, 1 - c), device_id_type=MESH)
        cp.start()
        cp.wait()

    return pl.pallas_call(
        body, name=name,
        out_shape=jax.ShapeDtypeStruct(buf.shape, buf.dtype),
        in_specs=[HBM], out_specs=HBM, input_output_aliases={0: 0},
        scratch_shapes=[pltpu.SemaphoreType.DMA, pltpu.SemaphoreType.DMA],
    )(buf)


def _pad_rows(a, mult):
    pad = (-a.shape[0]) % mult
    return a if pad == 0 else jnp.pad(a, ((0, pad),) + ((0, 0),) * (a.ndim - 1))


def _local_step(x, target, mod, wts, small, slabs=None, unpacks=None, reduce_early=None, grad_slab=None,
                reduce_late=None):
    S, D = x.shape
    HP = D // LANES
    row = lambda v: v.reshape(1, -1)
    msplit = [[row(mod[i, k * D:(k + 1) * D]) for k in range(6)] for i in range(2)]
    gw, gs = {}, {}
    dmod = [[None] * 6 for _ in range(2)]
    slab, where = grad_slab if grad_slab is not None else (None, {})

    def dw(key, a, b, name):
        nonlocal slab
        if key in where:
            slab = _matmul_tn(a, b, name=name, into=(slab,) + where[key])
        else:
            gw[key] = _matmul_tn(a, b, name=name)

    sh1, sc1, g1, sh2, sc2, g2 = msplit[0]
    n1w0, n2w0 = row(small["norm1_w"][0]), row(small["norm2_w"][0])
    slabs = slabs if slabs is not None else (None, None, None)
    proj0, h1_0, *gathered = _ln_matmul(x, n1w0, sc1, sh1, wts["hg_w_in"], slabs[0], relu2=False, name="hg_in_proj")
    if slabs[0] is not None:
        wts = {**wts, **unpacks[0](gathered[0])}
    gn = small["hg_gn_w"].reshape(1, LANES)
    ypre0, o0, states, *gathered = _hg_fwd(proj0, small["hg_lb"], gn, slabs[1], name="hg_fwd")
    if slabs[1] is not None:
        wts = {**wts, **unpacks[1](gathered[0])}
    x1, ymix0 = _matmul_resid(ypre0, wts["hg_w_out"], x, g1, name="hg_out_proj")
    a0, u0, h2_0, *gathered = _ln_matmul(x1, n2w0, sc2, sh2, wts["mlp_w1_0"], slabs[2], relu2=True, name="mlp0_up")
    if slabs[2] is not None:
        wts = {**wts, **unpacks[2](gathered[0])}
    x2, ymlp0 = _matmul_resid(u0, wts["mlp_w2_0"], x1, g2, name="mlp0_down")

    sh1b, sc1b, g1b, sh2b, sc2b, g2b = msplit[1]
    n1w1, n2w1 = row(small["norm1_w"][1]), row(small["norm2_w"][1])
    proj1, h1_1 = _ln_matmul(x2, n1w1, sc1b, sh1b, wts["fox_w_in"], relu2=False, name="fox_in_proj")
    nheads = 2 * HP
    bf_pad = jnp.pad(small["fox_b_f"].reshape(1, nheads), ((0, 0), (0, LANES - nheads)))
    qw2 = jnp.tile(small["fox_qn_w"].reshape(1, FOX_DH), (1, 2))
    kw2 = jnp.tile(small["fox_kn_w"].reshape(1, FOX_DH), (1, 2))
    fcum = _fox_cumsum(proj1, bf_pad, name="fox_cumsum")
    qa, ka, va, vat = _fox_prep(proj1, fcum, qw2, kw2, name="fox_prep")
    jmin, imax = _fox_skip_bounds(fcum, small["fox_qn_w"], small["fox_kn_w"], nheads)
    ypre1, o1, q2 = _fox_fwd(jmin, qa, ka, vat, proj1, name="fox_fwd")
    x3, ymix1 = _matmul_resid(ypre1, wts["fox_w_out"], x2, g1b, name="fox_out_proj")
    a1, u1, h2_1 = _ln_matmul(x3, n2w1, sc2b, sh2b, wts["mlp_w1_1"], relu2=True, name="mlp1_up")
    x4, ymlp1 = _matmul_resid(u1, wts["mlp_w2_1"], x3, g2b, name="mlp1_down")

    loss, dx4, dfw = _loss_kernel(x4, row(small["final_w"]), target, name="loss")
    gs["final_w"] = dfw.reshape(-1)

    def mlp_bwd(i, dx_out, x_in, h2, a, u, ymlp, n2w, sc2_, g2_):
        dz, dm, dg2 = _gate_matmul_nt(dx_out, g2_, ymlp, wts[f"mlp_w2_{i}"], a, name=f"mlp{i}_down_bwd")
        dw(f"mlp_w2_{i}", u, dm[None], f"mlp{i}_dw2")
        dw(f"mlp_w1_{i}", h2, dz[None], f"mlp{i}_dw1")
        dx_in, dsc, dsh, dnw = _matmul_nt_lnbwd(dz[None], wts[f"mlp_w1_{i}"], x_in, n2w, sc2_, dx_out,
                                                name=f"mlp{i}_up_bwd")
        dmod[i][3], dmod[i][4], dmod[i][5] = dsh, dsc, dg2
        return dx_in, dnw

    dx3, dn2w1 = mlp_bwd(1, dx4, x3, h2_1, a1, u1, ymlp1, n2w1, sc2b, g2b)
    dyp1, dm1, dg1b = _gate_matmul_nt(dx3, g1b, ymix1, wts["fox_w_out"], None, name="fox_out_bwd")
    dw("fox_w_out", ypre1, dm1[None], "fox_dw_out")
    doa = _fox_bwd_prep(dyp1, o1, proj1, q2, name="fox_bwd_prep")
    dqa, dka, dva, colsum = _fox_bwd(imax, q2, ka, va, doa, name="fox_bwd")
    colsum = jnp.pad(colsum[:, 0, :].T, ((0, 0), (0, LANES - nheads)))
    dproj1, dqw, dkw = _fox_bwd_post(dqa, dka, dva, proj1, dyp1, o1, qw2, kw2, name="fox_bwd_post")
    dproj1, dbf = _fox_dfz(colsum, nheads, proj1, bf_pad, dproj1, name="fox_dfz")
    dw("fox_w_in", h1_1, dproj1, "fox_dw_in")
    dx2, dsc, dsh, dn1w1 = _matmul_nt_lnbwd(dproj1, wts["fox_w_in"], x2, n1w1, sc1b, dx3, name="fox_in_bwd")
    dmod[1][0], dmod[1][1], dmod[1][2] = dsh, dsc, dg1b
    gs["fox_qn_w"] = dqw[0, :FOX_DH] + dqw[0, FOX_DH:]
    gs["fox_kn_w"] = dkw[0, :FOX_DH] + dkw[0, FOX_DH:]
    gs["fox_b_f"] = dbf[0, :nheads]

    dx1, dn2w0 = mlp_bwd(0, dx2, x1, h2_0, a0, u0, ymlp0, n2w0, sc2, g2)
    dyp0, dm0, dg1 = _gate_matmul_nt(dx1, g1, ymix0, wts["hg_w_out"], None, name="hg_out_bwd")
    dw("hg_w_out", ypre0, dm0[None], "hg_dw_out")
    part, ctx = reduce_early(gw, slab) if reduce_early is not None else (None, None)
    dproj0, dlb, dgn, *from_chips = _hg_bwd(proj0, small["hg_lb"], gn, o0, states, dyp0, part, name="hg_bwd")
    early = (ctx, from_chips[0]) if reduce_early is not None else None
    dw("hg_w_in", h1_0, dproj0, "hg_dw_in")
    part, ctx = reduce_late(gw) if reduce_late is not None else (None, None)
    dx0, dsc, dsh, dn1w0, *from_chips = _matmul_nt_lnbwd(dproj0, wts["hg_w_in"], x, n1w0, sc1, dx1, part, name="hg_in_bwd")
    late = (ctx, from_chips[0]) if reduce_late is not None else None
    dmod[0][0], dmod[0][1], dmod[0][2] = dsh, dsc, dg1
    gs["hg_lb"] = dlb
    gs["hg_gn_w"] = jnp.sum(dgn, axis=0)

    gs["norm1_w"] = jnp.concatenate([dn1w0, dn1w1], axis=0)
    gs["norm2_w"] = jnp.concatenate([dn2w0, dn2w1], axis=0)
    gs["dmod"] = jnp.stack([jnp.concatenate(dmod[i], axis=1)[0] for i in range(2)])
    return loss, dx0, gw, gs, early, late


def _pack_halves(layout):
    rh = -(-max(sum(a.shape[0] for _, a in half) for half in layout) // 16) * 16
    place, parts = {}, []
    for h, half in enumerate(layout):
        off = 0
        for n, a in half:
            place[n] = (h, off, a.shape[0])
            off += a.shape[0]
        parts.append(jnp.pad(jnp.concatenate([a.astype(BF) for _, a in half], axis=0), ((0, rh - off), (0, 0))))
    return jnp.concatenate(parts, axis=0), place, rh


SMALL_NAMES = ["norm1_w", "norm2_w", "hg_lb", "hg_gn_w", "fox_b_f", "fox_qn_w", "fox_kn_w", "final_w"]


def _pack_small(d, names):
    rows, offs, r0 = [], {}, 0
    for n in names:
        flat = d[n].reshape(-1)
        nr = -(-flat.shape[0] // LANES)
        rows.append(jnp.pad(flat, (0, nr * LANES - flat.shape[0])).reshape(nr, LANES))
        offs[n] = (r0, nr)
        r0 += nr
    return jnp.concatenate(rows, axis=0), offs


def _unpack_small(packed, offs, name, like):
    r0, nr = offs[name]
    return packed[r0:r0 + nr].reshape(-1)[:like.size].reshape(like.shape)


def kernel(x, c, w_mod, b_mod, norm1_w, norm2_w, hg_w_in, hg_w_out, hg_lb, hg_gn_w, fox_w_in, fox_b_f, fox_qn_w, fox_kn_w, fox_w_out, mlp_w1, mlp_w2, final_w, loss_target, m_w_mod, m_b_mod, m_norm1_w, m_norm2_w, m_hg_w_in, m_hg_w_out, m_hg_lb, m_hg_gn_w, m_fox_w_in, m_fox_b_f, m_fox_qn_w, m_fox_kn_w, m_fox_w_out, m_mlp_w1, m_mlp_w2, m_final_w, v_w_mod, v_b_mod, v_norm1_w, v_norm2_w, v_hg_w_in, v_hg_w_out, v_hg_lb, v_hg_gn_w, v_fox_w_in, v_fox_b_f, v_fox_qn_w, v_fox_kn_w, v_fox_w_out, v_mlp_w1, v_mlp_w2, v_final_w):
    S, D = x.shape[1], x.shape[2]
    nheads = D // FOX_DH
    ax, ay, ac = _mesh_pos()
    chip = 2 * ax + ay
    dev = 2 * chip + ac
    xs, tgt = x.reshape(S, D), loss_target.reshape(S, D)

    c_all = _allgather_small(_pad_rows(c.reshape(-1, LANES), 8), name="gather_c")
    c_all = c_all.reshape(N_DEV, -1)[:, :D]
    c16 = _pad_rows(c_all, 16)
    nmod = w_mod.shape[2]
    b_shard = lax.dynamic_slice_in_dim(b_mod, chip * nmod, nmod, axis=1)
    mod_shard = _mod_fwd(c16, w_mod, b_shard[:, None, :], name="mod_fwd")[:, :N_DEV]
    mod_all = _allgather_small(mod_shard.reshape(-1, LANES), name="gather_mod")
    mod_all = mod_all.reshape(N_CHIPS, 2, 2, N_DEV, nmod)[:, 0]
    mod = lax.dynamic_index_in_dim(mod_all, dev, axis=2, keepdims=False)
    mod = mod.transpose(1, 0, 2).reshape(2, N_CHIPS * nmod)

    fox_rows = fox_w_in.shape[2]
    col = lambda g: g.transpose(1, 0, 2).reshape(g.shape[1], -1)
    rowsh = lambda g: g.reshape(-1, g.shape[2])
    own = lambda g, s: lax.dynamic_update_index_in_dim(g, s, chip, 0)

    slab_in = hg_w_in[0].astype(BF)
    wts = {"hg_w_in": col(own(_allgather_chip_slabs(slab_in, name="gather_hg_w_in"), slab_in))}
    fox_flat, fox_cut = fox_w_in[0].reshape(fox_rows, D), fox_rows // 2
    slabs, unpacks = [], []
    for layout_w in ([[("mlp_w1_0", mlp_w1[0])], [("mlp_w2_0", mlp_w2[0])]],
                     [[("mlp_w1_1", mlp_w1[1]), ("hg_w_out", hg_w_out[0])], [("mlp_w2_1", mlp_w2[1]), ("fox_w_out", fox_w_out[0])]],
                     [[("fox_a", fox_flat[:fox_cut])], [("fox_b", fox_flat[fox_cut:])]]):
        slab_w, place_w, rh_w = _pack_halves(layout_w)

        def unpack(gathered, slab_w=slab_w, place_w=place_w, rh_w=rh_w):
            gathered = own(gathered, slab_w)
            out = {}
            for n, (h, off, rows) in place_w.items():
                g = gathered[:, h * rh_w + off:h * rh_w + off + rows, :]
                out[n] = col(g) if n.startswith("mlp_w1") else rowsh(g) if n.startswith(("mlp_w2", "hg_", "fox_w")) else g
            if "fox_a" in out:
                fox_in = col(jnp.concatenate([out.pop("fox_a"), out.pop("fox_b")], axis=1).reshape(N_CHIPS, D, fox_rows))
                out["fox_w_in"] = jnp.pad(fox_in, ((0, 0), (0, 5 * D - fox_in.shape[1])))
            return out

        slabs.append(slab_w)
        unpacks.append(unpack)

    small = {"norm1_w": norm1_w, "norm2_w": norm2_w, "hg_lb": hg_lb, "hg_gn_w": hg_gn_w, "fox_b_f": fox_b_f,
             "fox_qn_w": fox_qn_w, "fox_kn_w": fox_kn_w, "final_w": final_w}

    def uncol(g, n):
        return g.reshape(g.shape[0], N_CHIPS, n).transpose(1, 0, 2)

    pos = jnp.stack([chip, ac])

    def swap_and_add(g4, tag):
        to_sibling = lax.dynamic_index_in_dim(g4, 1 - ac, axis=1, keepdims=False).astype(BF)
        from_sibling = _swap_halves(to_sibling, name=f"rs_swap_{tag}")
        return from_sibling, _add_halves(g4, from_sibling, ac.reshape(1), name=f"rs_add_halves_{tag}")

    def finish(g4, from_sibling, from_chips, tag):
        my_half = _add_four(g4, from_sibling, from_chips, pos, name=f"rs_add_chips_{tag}")
        return _join_halves(my_half, name=f"rs_join_{tag}")

    layout = [[("mlp_w1", 2 * D), ("hg_w_out", D // 4), ("fox_w_out", D // 4)], [("mlp_w2", 2 * D), ("fox_w_in", fox_rows)]]
    place = {}
    for h, half in enumerate(layout):
        off = 0
        for n, rows in half:
            place[n] = (h, off, rows)
            off += rows

    rh = -(-max(sum(rows for _, rows in half) for half in layout) // 16) * 16
    where = {"hg_w_out": ("row",) + place["hg_w_out"][:2], "fox_w_out": ("row",) + place["fox_w_out"][:2]}
    for i in range(2):
        where[f"mlp_w1_{i}"] = ("col", place["mlp_w1"][0], place["mlp_w1"][1] + i * D)
        where[f"mlp_w2_{i}"] = ("row", place["mlp_w2"][0], place["mlp_w2"][1] + i * D)

    def reduce_early(gw, slab):
        gfox = uncol(gw["fox_w_in"][:, :4 * fox_rows], fox_rows).reshape(N_CHIPS, 1, fox_rows, D)
        h, off, _ = place["fox_w_in"]
        slab = lax.dynamic_update_slice(slab, gfox, (0, h, off, 0))
        for h, half in enumerate(layout):
            used = sum(rows for _, rows in half)
            if used < rh:
                slab = lax.dynamic_update_slice(slab, jnp.zeros((N_CHIPS, 1, rh - used, D), F32), (0, h, used, 0))
        from_sibling, part = swap_and_add(slab, "early")
        return part, (slab, from_sibling)

    def reduce_late(gw):
        g4 = uncol(gw["hg_w_in"], D).reshape(N_CHIPS, 2, D // 2, D)
        from_sibling, part = swap_and_add(g4, "late")
        return part, (g4, from_sibling)

    loss_part, grad_x, gw, gs, (early, from_chips_early), (late, from_chips_late) = _local_step(
        xs, tgt, mod, wts, small, slabs, unpacks, reduce_early, (lax.empty((N_CHIPS, 2, rh, D), F32), where), reduce_late)
    gshard = finish(*early, from_chips_early, "early")
    g_hg_w_in = finish(*late, from_chips_late, "late").reshape(D, D)

    names = ["dmod", "loss"] + SMALL_NAMES
    packed, offs = _pack_small({**gs, "loss": loss_part[0, :1]}, names)
    packed = _pad_rows(packed, 8)
    rp = packed.shape[0]
    parts = _allgather_small(packed, name="gather_small").reshape(N_DEV, rp, LANES)
    total = _sum_parts(parts, name="sum_small")
    r0, nr = offs["dmod"]
    dmod_all = parts[:, r0:r0 + nr].reshape(N_DEV, 2, N_CHIPS * nmod)
    dmod_shard = lax.dynamic_slice_in_dim(dmod_all, chip * nmod, nmod, axis=2).transpose(1, 0, 2)
    g_w_mod = _mod_bwd(c16, jnp.pad(dmod_shard, ((0, 0), (0, 16 - N_DEV), (0, 0))), name="mod_bwd")

    loss = _unpack_small(total, offs, "loss", loss_part[0, :1]).reshape(())
    grads = {"w_mod": g_w_mod, "b_mod": _unpack_small(total, offs, "dmod", b_mod)}
    for n in SMALL_NAMES:
        grads[n] = _unpack_small(total, offs, n, small[n])

    given = dict(w_mod=(w_mod, m_w_mod, v_w_mod), b_mod=(b_mod, m_b_mod, v_b_mod), norm1_w=(norm1_w, m_norm1_w, v_norm1_w),
                 norm2_w=(norm2_w, m_norm2_w, v_norm2_w), hg_w_in=(hg_w_in, m_hg_w_in, v_hg_w_in),
                 hg_w_out=(hg_w_out, m_hg_w_out, v_hg_w_out), hg_lb=(hg_lb, m_hg_lb, v_hg_lb),
                 hg_gn_w=(hg_gn_w, m_hg_gn_w, v_hg_gn_w), fox_w_in=(fox_w_in, m_fox_w_in, v_fox_w_in),
                 fox_b_f=(fox_b_f, m_fox_b_f, v_fox_b_f), fox_qn_w=(fox_qn_w, m_fox_qn_w, v_fox_qn_w),
                 fox_kn_w=(fox_kn_w, m_fox_kn_w, v_fox_kn_w), fox_w_out=(fox_w_out, m_fox_w_out, v_fox_w_out),
                 mlp_w1=(mlp_w1, m_mlp_w1, v_mlp_w1), mlp_w2=(mlp_w2, m_mlp_w2, v_mlp_w2), final_w=(final_w, m_final_w, v_final_w))
    upd = {}

    for n, (h, off, rows) in place.items():
        w, m, v = given[n]
        flat = lambda a: a.reshape(rows, D)
        d, mn, vn = _adamw(flat(w), gshard, flat(m), flat(v), g_at=(h, off), name=f"adamw_{n}")
        grads[n] = gshard[h, off:off + rows].reshape(w.shape)
        upd[n] = tuple(a.reshape(w.shape) for a in (d, mn, vn))

    w, m, v = given["hg_w_in"]
    grads["hg_w_in"] = g_hg_w_in.reshape(w.shape)
    upd["hg_w_in"] = tuple(a.reshape(w.shape) for a in _adamw(w[0], g_hg_w_in, m[0], v[0], name="adamw_hg_w_in"))

    w, m, v = given["w_mod"]
    flat = lambda a: a.reshape(-1, nmod)
    upd["w_mod"] = tuple(a.reshape(w.shape) for a in _adamw(flat(w), flat(g_w_mod), flat(m), flat(v), name="adamw_w_mod"))

    snames = ["b_mod"] + SMALL_NAMES
    pw, soffs = _pack_small({n: given[n][0] for n in snames}, snames)
    pm, _ = _pack_small({n: given[n][1] for n in snames}, snames)
    pv, _ = _pack_small({n: given[n][2] for n in snames}, snames)
    pg, _ = _pack_small({n: grads[n] for n in snames}, snames)
    pw, pm, pv, pg = (_pad_rows(a, 8) for a in (pw, pm, pv, pg))
    sd, smn, svn = _adamw(pw, pg, pm, pv, name="adamw_small")
    for n in snames:
        like = given[n][0]
        upd[n] = tuple(_unpack_small(a, soffs, n, like) for a in (sd, smn, svn))

    order = ["w_mod", "b_mod", "norm1_w", "norm2_w", "hg_w_in", "hg_w_out", "hg_lb", "hg_gn_w", "fox_w_in", "fox_b_f",
             "fox_qn_w", "fox_kn_w", "fox_w_out", "mlp_w1", "mlp_w2", "final_w"]
    return (loss, grad_x.reshape(x.shape), *[grads[n] for n in order], *[upd[n][0] for n in order],
            *[upd[n][1] for n in order], *[upd[n][2] for n in order])
```

```python
import math

import jax
import jax.numpy as jnp
from jax import lax
from jax.experimental import pallas as pl
from jax.experimental.pallas import tpu as pltpu

EPS = 1e-6
ADAM_LR, ADAM_B1, ADAM_B2, ADAM_EPS, ADAM_WD, ADAM_STEP = 0.001, 0.9, 0.999, 1e-08, 0.01, 10

F32 = jnp.float32
BF = jnp.bfloat16
LANES = 128
HG_CHUNK = 64
HG_HEADS_PER_STEP = 8
HG_TOKENS_PER_STEP = 256
FOX_ROWS_PER_STEP = 2048
FOX_BWD_TILES = (8, 4, 2, 1)
LOG2E = 1.4426950408889634
FOX_DH = 64
N_CHIPS = 4
N_DEV = 8
VMEM_LIMIT = 56 * 1024 * 1024
MESH = pl.DeviceIdType.MESH

NT = (((1,), (1,)), ((), ()))
TN = (((0,), (0,)), ((), ()))


def _pick(n, pref, mult=LANES):
    if n <= pref:
        return n
    t = (pref // mult) * mult
    while t >= mult:
        if n % t == 0:
            return t
        t -= mult
    raise ValueError((n, pref, mult))


def _cp(*sem):
    return pltpu.CompilerParams(dimension_semantics=sem, vmem_limit_bytes=VMEM_LIMIT)


def _dot(a, b):
    return jnp.dot(a, b, preferred_element_type=F32)


def _dg(a, b, dims):
    return lax.dot_general(a, b, dims, preferred_element_type=F32)


def _split3(x):
    hi = x.astype(BF)
    r1 = x - hi.astype(F32)
    mid = r1.astype(BF)
    lo = (r1 - mid.astype(F32)).astype(BF)
    return hi, mid, lo


def _tri_dot(tri, x):
    hi, mid, lo = _split3(x)
    return _dot(tri, hi) + _dot(tri, mid) + _dot(tri, lo)


def _dg3(a, b, dims):
    ah, bh = a.astype(BF), b.astype(BF)
    al, bl = (a - ah.astype(F32)).astype(BF), (b - bh.astype(F32)).astype(BF)
    return _dg(ah, bh, dims) + _dg(ah, bl, dims) + _dg(al, bh, dims)


def _dg1(a, b, dims):
    return _dg(a.astype(BF), b.astype(BF), dims)


NN = (((1,), (0,)), ((), ()))


def _sigmoid(x):
    return jax.nn.sigmoid(x)


def _ln_matmul(x, nw, sc, sh, w, slab=None, *, relu2, name):
    S, D = x.shape
    N = w.shape[1]
    tm, tn = _pick(S, 512, 16), N
    fused = slab is not None

    def body(x_ref, nw_ref, sc_ref, sh_ref, w_ref, *rest):
        if fused:
            s_ref, *outs, out_ref, hs, send_sems, recv_sems = rest
            finish = _gather_behind(s_ref, out_ref, send_sems, recv_sems, pl.program_id(0), S // tm)
        else:
            outs, hs = rest[:-1], rest[-1]
        h_ref = outs[-1]

        @pl.when(pl.program_id(1) == 0)
        def _():
            xv = x_ref[...]
            r = lax.rsqrt(jnp.mean(xv * xv, axis=-1, keepdims=True) + EPS)
            hb = ((xv * r * nw_ref[...]) * (1.0 + sc_ref[...]) + sh_ref[...]).astype(BF)
            hs[...] = hb
            h_ref[...] = hb

        z = _dot(hs[...], w_ref[...])
        if relu2:
            a = jnp.maximum(z, 0.0)
            outs[0][...] = a.astype(BF)
            outs[1][...] = (a * a).astype(BF)
        else:
            outs[0][...] = z
        if fused:
            finish()

    vec = pl.BlockSpec((1, D), lambda i, j: (0, 0))
    tile = pl.BlockSpec((tm, tn), lambda i, j: (i, j))
    if relu2:
        out_shape = [jax.ShapeDtypeStruct((S, N), BF), jax.ShapeDtypeStruct((S, N), BF)]
        out_specs = [tile, tile]
    else:
        out_shape = [jax.ShapeDtypeStruct((S, N), F32)]
        out_specs = [tile]
    out_shape.append(jax.ShapeDtypeStruct((S, D), BF))
    out_specs.append(pl.BlockSpec((tm, D), lambda i, j: (i, 0)))
    in_specs = [pl.BlockSpec((tm, D), lambda i, j: (i, 0)), vec, vec, vec, pl.BlockSpec((D, tn), lambda i, j: (0, j))]
    scratch = [pltpu.VMEM((tm, D), BF)]
    args = [x, nw, sc, sh, w]
    if fused:
        in_specs.append(HBM)
        out_specs.append(HBM)
        out_shape.append(jax.ShapeDtypeStruct((N_CHIPS,) + slab.shape, slab.dtype))
        scratch += [pltpu.SemaphoreType.DMA((6,)), pltpu.SemaphoreType.DMA((6,))]
        args.append(slab)
    return pl.pallas_call(
        body, name=name, grid=(S // tm, N // tn), in_specs=in_specs, out_specs=out_specs, out_shape=out_shape,
        scratch_shapes=scratch, compiler_params=_cp("arbitrary", "arbitrary"),
    )(*args)


def _matmul_resid(a, w, x, gate, *, name):
    S, K = a.shape
    D = w.shape[1]
    tm, tn = _pick(S, 1024 if K <= 1024 else 512, 16), D

    def body(a_ref, w_ref, x_ref, g_ref, o_ref, y_ref):
        y = _dot(a_ref[...], w_ref[...])
        y_ref[...] = y.astype(BF)
        o_ref[...] = x_ref[...] + g_ref[...] * y

    tile = pl.BlockSpec((tm, tn), lambda i, j: (i, j))
    return pl.pallas_call(
        body, name=name, grid=(S // tm, D // tn),
        in_specs=[pl.BlockSpec((tm, K), lambda i, j: (i, 0)), pl.BlockSpec((K, tn), lambda i, j: (0, j)),
                  tile, pl.BlockSpec((1, tn), lambda i, j: (0, j))],
        out_specs=[tile, tile],
        out_shape=[jax.ShapeDtypeStruct((S, D), F32), jax.ShapeDtypeStruct((S, D), BF)],
        compiler_params=_cp("parallel", "arbitrary"),
    )(a, w, x, gate)


def _gate_matmul_nt(dx, gate, y, w, act, *, name):
    S, D = dx.shape
    K = w.shape[0]
    tm, tn = _pick(S, 1024 if K <= 1024 else 512, 16), K
    fused = act is not None

    def body(dx_ref, g_ref, y_ref, w_ref, *rest):
        if fused:
            act_ref, da_ref, dm_ref, dg_ref, ms = rest
        else:
            da_ref, dm_ref, dg_ref, ms = rest
        i, j = pl.program_id(0), pl.program_id(1)

        @pl.when((i == 0) & (j == 0))
        def _():
            dg_ref[...] = jnp.zeros_like(dg_ref)

        @pl.when(j == 0)
        def _():
            dxv = dx_ref[...]
            dmb = (dxv * g_ref[...]).astype(BF)
            ms[...] = dmb
            dm_ref[...] = dmb
            dg_ref[...] += jnp.sum(dxv * y_ref[...].astype(F32), axis=0, keepdims=True)

        da = _dg(ms[...], w_ref[...], NT)
        if fused:
            da_ref[...] = (da * (2.0 * act_ref[...].astype(F32))).astype(BF)
        else:
            da_ref[...] = da

    row = pl.BlockSpec((tm, D), lambda i, j: (i, 0))
    vec = pl.BlockSpec((1, D), lambda i, j: (0, 0))
    tile = pl.BlockSpec((tm, tn), lambda i, j: (i, j))
    in_specs = [row, vec, row, pl.BlockSpec((tn, D), lambda i, j: (j, 0))]
    args = [dx, gate, y, w]
    if fused:
        in_specs.append(tile)
        args.append(act)
    return pl.pallas_call(
        body, name=name, grid=(S // tm, K // tn),
        in_specs=in_specs, out_specs=[tile, row, vec],
        out_shape=[jax.ShapeDtypeStruct((S, K), BF if fused else F32), jax.ShapeDtypeStruct((S, D), BF),
                   jax.ShapeDtypeStruct((1, D), F32)],
        scratch_shapes=[pltpu.VMEM((tm, D), BF)],
        compiler_params=_cp("arbitrary", "arbitrary"),
    )(*args)


def _matmul_tn(a, b, *, name, into=None):
    S, Ka = a.shape
    P, _, Db = b.shape
    tk, tn, ts = _pick(Ka, 1024), _pick(Db, 1024), _pick(S, 1024, 16)
    if into is not None:
        slab, kind, half, off = into
        C = tn = slab.shape[3]
        per_chip = Ka // N_CHIPS
        all_chips = kind == "row" and tk == Ka
        if kind == "row" and not all_chips:
            tk = min(tk, per_chip)
        assert tn == C and P * Db == (N_CHIPS * C if kind == "col" else C)
        if kind == "col":
            assert tk == Ka and off % tk == 0
        elif all_chips:
            assert off % per_chip == 0
        else:
            assert per_chip % tk == 0 and off % tk == 0
    npb = Db // tn

    def body(a_ref, b_ref, *rest):
        o_ref, acc = rest[-2:]
        s = pl.program_id(2)

        @pl.when(s == 0)
        def _():
            acc[...] = jnp.zeros_like(acc)

        acc[...] += _dg(a_ref[...], b_ref[...], TN)

        @pl.when(s == pl.num_programs(2) - 1)
        def _():
            o_ref[...] = acc[...].reshape(o_ref.shape)

    in_specs = [pl.BlockSpec((ts, tk), lambda i, j, s: (s, i)),
                pl.BlockSpec((None, ts, tn), lambda i, j, s: (j // npb, s, j % npb))]
    args = [a, b]
    if into is None:
        out_spec = pl.BlockSpec((tk, tn), lambda i, j, s: (i, j))
        out_shape = jax.ShapeDtypeStruct((Ka, P * Db), F32)
        aliases = {}
    else:
        per = per_chip // tk if kind == "row" and not all_chips else 1
        if kind == "col":
            out_spec = pl.BlockSpec((None, None, tk, tn), lambda i, j, s: (j, half, off // tk + i, 0))
        elif all_chips:
            out_spec = pl.BlockSpec((N_CHIPS, None, per_chip, tn), lambda i, j, s: (0, half, off // per_chip, 0))
        else:
            out_spec = pl.BlockSpec((None, None, tk, tn), lambda i, j, s: (i // per, half, off // tk + i % per, 0))
        out_shape = jax.ShapeDtypeStruct(slab.shape, F32)
        in_specs.append(pl.BlockSpec(memory_space=pl.ANY))
        args.append(slab)
        aliases = {2: 0}
    return pl.pallas_call(
        body, name=name, grid=(Ka // tk, P * npb, S // ts),
        in_specs=in_specs, out_specs=out_spec, out_shape=out_shape,
        scratch_shapes=[pltpu.VMEM((tk, tn), F32)], input_output_aliases=aliases,
        compiler_params=_cp("parallel", "parallel", "arbitrary"),
    )(*args)


def _matmul_nt_lnbwd(g, w, x, nw, sc, dx_out, part=None, *, name):
    P, S, Dg = g.shape
    D = x.shape[1]
    tm = _pick(S, 512, 16)
    fused = part is not None

    def body(g_ref, w_ref, x_ref, nw_ref, sc_ref, dxo_ref, *rest):
        if fused:
            p_ref, dx_ref, dsc_ref, dsh_ref, dnw_ref, recv_ref, send_sems, recv_sems = rest
            copies = _scatter_copies(p_ref, recv_ref, send_sems, recv_sems)
        else:
            dx_ref, dsc_ref, dsh_ref, dnw_ref = rest

        @pl.when(pl.program_id(0) == 0)
        def _():
            dsc_ref[...] = jnp.zeros_like(dsc_ref)
            dsh_ref[...] = jnp.zeros_like(dsh_ref)
            dnw_ref[...] = jnp.zeros_like(dnw_ref)
            if fused:
                for cp in copies:
                    cp.start()

        dh = _dg(g_ref[0], w_ref[:, 0:Dg], NT)
        for p in range(1, P):
            dh = dh + _dg(g_ref[p], w_ref[:, p * Dg:(p + 1) * Dg], NT)
        xv = x_ref[...]
        nwv = nw_ref[...]
        r = lax.rsqrt(jnp.mean(xv * xv, axis=-1, keepdims=True) + EPS)
        xr = xv * r
        dn = dh * (1.0 + sc_ref[...])
        dsc_ref[...] += jnp.sum(dh * (xr * nwv), axis=0, keepdims=True)
        dsh_ref[...] += jnp.sum(dh, axis=0, keepdims=True)
        dnw_ref[...] += jnp.sum(dn * xr, axis=0, keepdims=True)
        u = dn * nwv
        dx_ref[...] = dxo_ref[...] + r * (u - xr * jnp.mean(u * xr, axis=-1, keepdims=True))

        if fused:
            @pl.when(pl.program_id(0) == S // tm - 1)
            def _():
                for cp in copies:
                    cp.wait()

    row = pl.BlockSpec((tm, D), lambda i: (i, 0))
    vec = pl.BlockSpec((1, D), lambda i: (0, 0))
    in_specs = [pl.BlockSpec((P, tm, Dg), lambda i: (0, i, 0)), pl.BlockSpec((D, P * Dg), lambda i: (0, 0)), row, vec, vec, row]
    out_specs = [row, vec, vec, vec]
    out_shape = [jax.ShapeDtypeStruct((S, D), F32)] + [jax.ShapeDtypeStruct((1, D), F32)] * 3
    scratch, args = [], [g, w, x, nw, sc, dx_out]
    if fused:
        in_specs.append(HBM)
        out_specs.append(HBM)
        out_shape.append(jax.ShapeDtypeStruct((3,) + part.shape[1:], part.dtype))
        scratch = [pltpu.SemaphoreType.DMA((3,)), pltpu.SemaphoreType.DMA((3,))]
        args.append(part)
    return pl.pallas_call(
        body, name=name, grid=(S // tm,), in_specs=in_specs, out_specs=out_specs, out_shape=out_shape,
        scratch_shapes=scratch, compiler_params=_cp("arbitrary"),
    )(*args)


def _loss_kernel(x, fw, tgt, *, name):
    S, D = x.shape
    tm = _pick(S, 512, 8)

    def body(x_ref, fw_ref, t_ref, l_ref, dx_ref, dfw_ref):
        @pl.when(pl.program_id(0) == 0)
        def _():
            l_ref[...] = jnp.zeros_like(l_ref)
            dfw_ref[...] = jnp.zeros_like(dfw_ref)

        xv = x_ref[...]
        fwv = fw_ref[...]
        r = lax.rsqrt(jnp.mean(xv * xv, axis=-1, keepdims=True) + EPS)
        xr = xv * r
        err = xr * fwv - t_ref[...]
        per_tok = jnp.mean(err * err, axis=-1, keepdims=True)
        l_ref[...] += 0.5 * jnp.sum(per_tok, axis=0, keepdims=True)
        dy = err * (1.0 / D)
        dfw_ref[...] += jnp.sum(dy * xr, axis=0, keepdims=True)
        u = dy * fwv
        dx_ref[...] = r * (u - xr * jnp.mean(u * xr, axis=-1, keepdims=True))

    row = pl.BlockSpec((tm, D), lambda i: (i, 0))
    vec = pl.BlockSpec((1, D), lambda i: (0, 0))
    return pl.pallas_call(
        body, name=name, grid=(S // tm,),
        in_specs=[row, vec, row],
        out_specs=[pl.BlockSpec((1, LANES), lambda i: (0, 0)), row, vec],
        out_shape=[jax.ShapeDtypeStruct((1, LANES), F32), jax.ShapeDtypeStruct((S, D), F32),
                   jax.ShapeDtypeStruct((1, D), F32)],
        compiler_params=_cp("arbitrary"),
    )(x, fw, tgt)


def _hg_lower_bound(lb3):
    mx = jnp.max(lb3, axis=0, keepdims=True)
    e = jnp.exp(lb3 - mx)
    p = e / jnp.sum(e, axis=0, keepdims=True)
    return p[0:1, :], p


def _hg_chunk_common(qr, fz, lbv):
    sq = _sigmoid(qr)
    q = qr * sq
    sig = _sigmoid(fz)
    f = lbv + (1.0 - lbv) * sig
    k = (1.0 - lbv) * (1.0 - sig)
    return q, sq, sig, f, k, jnp.log(f)


def _row_of(x, rows, r):
    return jnp.sum(jnp.where(rows == r, x, 0.0), axis=0, keepdims=True)


def _hg_fwd(proj, hg_lb, gn, slab=None, *, name):
    S = proj.shape[0]
    D = proj.shape[1] // 4
    H = D // LANES
    HB = min(HG_HEADS_PER_STEP, H)
    W = HB * LANES
    C = HG_CHUNK
    T = _pick(S, HG_TOKENS_PER_STEP, C)
    nch, nb = T // C, S // T
    ng = H // HB
    fused = slab is not None

    def body(q_ref, fz_ref, v_ref, g_ref, lb_ref, gn_ref, *rest):
        if fused:
            s_ref, y_ref, o_ref, sts_ref, out_ref, st, send_sems, recv_sems = rest
            finish = _gather_behind(s_ref, out_ref, send_sems, recv_sems,
                                    pl.program_id(0) * nb + pl.program_id(1), ng * nb)
        else:
            y_ref, o_ref, sts_ref, st = rest

        @pl.when(pl.program_id(1) == 0)
        def _():
            st[...] = jnp.zeros_like(st)

        lb_all, _ = _hg_lower_bound(lb_ref[...])
        gnv = gn_ref[...]
        ri = lax.broadcasted_iota(jnp.int32, (C, C), 0)
        ci_ = lax.broadcasted_iota(jnp.int32, (C, C), 1)
        low = ri >= ci_
        tri = jnp.where(low, 1.0, 0.0).astype(BF)
        rows_w = lax.broadcasted_iota(jnp.int32, (C, W), 0)

        def chunk(ci, carry):
            sl = pl.ds(pl.multiple_of(ci * C, C), C)
            heads = [slice(hh * LANES, (hh + 1) * LANES) for hh in range(HB)]
            q, _, _, _, k, logf = _hg_chunk_common(q_ref[sl, :], fz_ref[sl, :], lb_all)
            vv, gg = v_ref[sl, :], g_ref[sl, :]
            G = _tri_dot(tri, logf)
            Gm = _row_of(G, rows_w, C // 2 - 1)
            Gl = _row_of(G, rows_w, C - 1)
            qt, kt = q * jnp.exp(G - Gm), k * jnp.exp(Gm - G)
            qe, kd, eGl = q * jnp.exp(G), k * jnp.exp(Gl - G), jnp.exp(Gl)
            A = [jnp.where(low, _dg1(qt[:, ls], kt[:, ls], NT), 0.0) for ls in heads]
            Sv = [st[hh] for hh in range(HB)]
            for hh in range(HB):
                sts_ref[hh, ci] = Sv[hh]
            o = [_dg1(A[hh], vv[:, ls], NN) + _dg1(qe[:, ls], Sv[hh], NT) for hh, ls in enumerate(heads)]
            for hh, ls in enumerate(heads):
                st[hh] = Sv[hh] * eGl[:, ls] + _dg1(vv[:, ls], kd[:, ls], TN)
            gate = gg * _sigmoid(gg)
            for hh, ls in enumerate(heads):
                r = lax.rsqrt(jnp.mean(o[hh] * o[hh], axis=-1, keepdims=True) + EPS)
                y_ref[sl, ls] = ((o[hh] * r * gnv) * gate[:, ls]).astype(BF)
                o_ref[sl, ls] = o[hh]
            return carry

        lax.fori_loop(0, nch, chunk, 0)

        if fused:
            finish()

    def part(p):
        return pl.BlockSpec((T, W), lambda h, n: (n, p * ng + h))

    blk = pl.BlockSpec((T, W), lambda h, n: (n, h))
    in_specs = [part(0), part(1), part(2), part(3),
                pl.BlockSpec((3, W), lambda h, n: (0, h)), pl.BlockSpec((1, LANES), lambda h, n: (0, 0))]
    out_specs = [blk, blk, pl.BlockSpec((HB, nch, LANES, LANES), lambda h, n: (h, n, 0, 0))]
    out_shape = [jax.ShapeDtypeStruct((S, D), BF), jax.ShapeDtypeStruct((S, D), F32),
                 jax.ShapeDtypeStruct((H, S // C, LANES, LANES), F32)]
    scratch = [pltpu.VMEM((HB, LANES, LANES), F32)]
    args = [proj, proj, proj, proj, hg_lb, gn]
    if fused:
        in_specs.append(HBM)
        out_specs.append(HBM)
        out_shape.append(jax.ShapeDtypeStruct((N_CHIPS,) + slab.shape, slab.dtype))
        scratch += [pltpu.SemaphoreType.DMA((6,)), pltpu.SemaphoreType.DMA((6,))]
        args.append(slab)
    return pl.pallas_call(
        body, name=name, grid=(ng, nb), in_specs=in_specs, out_specs=out_specs, out_shape=out_shape,
        scratch_shapes=scratch, compiler_params=_cp("arbitrary", "arbitrary"),
    )(*args)


def _hg_bwd(proj, hg_lb, gn, o_all, states, dy, part=None, *, name):
    S = proj.shape[0]
    D = proj.shape[1] // 4
    H = D // LANES
    HB = min(HG_HEADS_PER_STEP, H)
    W = HB * LANES
    C = HG_CHUNK
    T = _pick(S, HG_TOKENS_PER_STEP, C)
    nch, nb = T // C, S // T
    ng = H // HB
    fused = part is not None

    def body(q_ref, fz_ref, v_ref, g_ref, lb_ref, gn_ref, o_ref, sts_ref, dy_ref, *rest):
        if fused:
            p_ref, dp_ref, dlb_ref, dgn_ref, recv_ref, dst, dlb_acc, send_sems, recv_sems = rest
            copies = _scatter_copies(p_ref, recv_ref, send_sems, recv_sems)

            @pl.when((pl.program_id(0) == 0) & (pl.program_id(1) == 0))
            def _():
                for cp in copies:
                    cp.start()
        else:
            dp_ref, dlb_ref, dgn_ref, dst, dlb_acc = rest
        n = pl.program_id(1)

        @pl.when(n == 0)
        def _():
            dst[...] = jnp.zeros_like(dst)
            dlb_acc[...] = jnp.zeros_like(dlb_acc)
            dgn_ref[...] = jnp.zeros_like(dgn_ref)

        lb_all, p3 = _hg_lower_bound(lb_ref[...])
        gnv = gn_ref[...]
        ri = lax.broadcasted_iota(jnp.int32, (C, C), 0)
        ci_ = lax.broadcasted_iota(jnp.int32, (C, C), 1)
        low = ri >= ci_
        tri = jnp.where(low, 1.0, 0.0).astype(BF)
        triu = jnp.where(ri <= ci_, 1.0, 0.0).astype(BF)
        rows_w = lax.broadcasted_iota(jnp.int32, (C, W), 0)
        gnw = jnp.tile(gnv, (1, HB))

        def chunk(cj, carry):
            ci = nch - 1 - cj
            sl = pl.ds(pl.multiple_of(ci * C, C), C)
            heads = list(enumerate(slice(hh * LANES, (hh + 1) * LANES) for hh in range(HB)))
            wide = lambda parts: jnp.concatenate(parts, axis=1)
            qr, vv, gg = q_ref[sl, :], v_ref[sl, :], g_ref[sl, :]
            q, sq, sig, f, k, logf = _hg_chunk_common(qr, fz_ref[sl, :], lb_all)
            G = _tri_dot(tri, logf)
            Gm = _row_of(G, rows_w, C // 2 - 1)
            Gl = _row_of(G, rows_w, C - 1)
            eG, e_qm, e_km, e_lk, eGl = jnp.exp(G), jnp.exp(G - Gm), jnp.exp(Gm - G), jnp.exp(Gl - G), jnp.exp(Gl)
            qt, kt, kdec, qe = q * e_qm, k * e_km, k * e_lk, q * eG
            sg = _sigmoid(gg)
            d_onw = dy_ref[sl, :] * (gg * sg)
            u = d_onw * gnw
            o = o_ref[sl, :]
            on, do = [], []
            for hh, ls in heads:
                r = lax.rsqrt(jnp.mean(o[:, ls] * o[:, ls], axis=-1, keepdims=True) + EPS)
                on.append(o[:, ls] * r)
                dgn_ref[hh] += jnp.sum(d_onw[:, ls] * on[hh], axis=0, keepdims=True)
                do.append(r * (u[:, ls] - on[hh] * jnp.mean(u[:, ls] * on[hh], axis=-1, keepdims=True)))
            dgg = dy_ref[sl, :] * (wide(on) * gnw) * (sg * (1.0 + gg * (1.0 - sg)))
            Sv = [sts_ref[hh, ci] for hh, _ in heads]
            dSv = [dst[hh] for hh, _ in heads]
            A = [jnp.where(low, _dg1(qt[:, ls], kt[:, ls], NT), 0.0) for _, ls in heads]
            dA = [jnp.where(low, _dg3(do[hh], vv[:, ls], NT), 0.0) for hh, ls in heads]
            dv = wide([_dg1(A[hh], do[hh], TN) + _dg1(kdec[:, ls], dSv[hh], NT) for hh, ls in heads])
            dq = wide([_dg3(dA[hh], kt[:, ls], NN) for hh, ls in heads]) * e_qm \
                + eG * wide([_dg3(do[hh], Sv[hh], NN) for hh, _ in heads])
            dk = wide([_dg3(dA[hh], qt[:, ls], TN) for hh, ls in heads]) * e_km \
                + e_lk * wide([_dg3(vv[:, ls], dSv[hh], NN) for hh, ls in heads])
            s_end = [Sv[hh] * eGl[:, ls] + _dg3(vv[:, ls], kdec[:, ls], TN) for hh, ls in heads]
            dgl = wide([jnp.sum(dSv[hh] * s_end[hh], axis=0, keepdims=True) for hh, _ in heads])
            for hh, ls in heads:
                dst[hh] = dSv[hh] * eGl[:, ls] + _dg1(do[hh], qe[:, ls], TN)
            dG = q * dq - k * dk + jnp.where(rows_w == C - 1, dgl, 0.0)
            dlogf = _tri_dot(triu, dG) - f * dk
            dlf_f = dlogf / f
            dlb_acc[...] += jnp.sum(dlf_f * (1.0 - sig), axis=0, keepdims=True)
            dp_ref[0, sl, :] = (dq * (sq * (1.0 + qr * (1.0 - sq)))).astype(BF)
            dp_ref[1, sl, :] = (dlf_f * (1.0 - lb_all) * sig * (1.0 - sig)).astype(BF)
            dp_ref[2, sl, :] = dv.astype(BF)
            dp_ref[3, sl, :] = dgg.astype(BF)
            return carry

        lax.fori_loop(0, nch, chunk, 0)
        sel = jnp.where(lax.broadcasted_iota(jnp.int32, (3, W), 0) == 0, 1.0, 0.0)
        dlb_ref[...] = lb_all * (sel - p3) * dlb_acc[...]

        if fused:
            @pl.when((pl.program_id(0) == ng - 1) & (n == nb - 1))
            def _():
                for cp in copies:
                    cp.wait()

    def col(p):
        return pl.BlockSpec((T, W), lambda h, n: (nb - 1 - n, p * ng + h))

    blk = pl.BlockSpec((T, W), lambda h, n: (nb - 1 - n, h))
    in_specs = [col(0), col(1), col(2), col(3),
                pl.BlockSpec((3, W), lambda h, n: (0, h)), pl.BlockSpec((1, LANES), lambda h, n: (0, 0)),
                blk, pl.BlockSpec((HB, nch, LANES, LANES), lambda h, n: (h, nb - 1 - n, 0, 0)), blk]
    out_specs = [pl.BlockSpec((4, T, W), lambda h, n: (0, nb - 1 - n, h)),
                 pl.BlockSpec((3, W), lambda h, n: (0, h)),
                 pl.BlockSpec((HB, 1, LANES), lambda h, n: (h, 0, 0))]
    out_shape = [jax.ShapeDtypeStruct((4, S, D), BF), jax.ShapeDtypeStruct((3, D), F32),
                 jax.ShapeDtypeStruct((H, 1, LANES), F32)]
    scratch = [pltpu.VMEM((HB, LANES, LANES), F32), pltpu.VMEM((1, W), F32)]
    args = [proj, proj, proj, proj, hg_lb, gn, o_all, states, dy]
    if fused:
        in_specs.append(HBM)
        out_specs.append(HBM)
        out_shape.append(jax.ShapeDtypeStruct((3,) + part.shape[1:], part.dtype))
        scratch += [pltpu.SemaphoreType.DMA((3,)), pltpu.SemaphoreType.DMA((3,))]
        args.append(part)
    return pl.pallas_call(
        body, name=name, grid=(ng, nb), in_specs=in_specs, out_specs=out_specs, out_shape=out_shape,
        scratch_shapes=scratch, compiler_params=_cp("arbitrary", "arbitrary"),
    )(*args)


def _log_sigmoid(u):
    return jnp.minimum(u, 0.0) - jnp.log(1.0 + jnp.exp(-jnp.abs(u)))


def _lane_put(base, lane, first, pieces):
    for n, p in enumerate(pieces):
        base = jnp.where(lane == first + n, p, base)
    return base


def _fox_cumsum(proj, bf_pad, *, name):
    S = proj.shape[0]
    D = proj.shape[1] // 5
    T = _pick(S, 256, 8)

    def body(fz_ref, b_ref, f_ref, carry):
        @pl.when(pl.program_id(0) == 0)
        def _():
            carry[...] = jnp.zeros_like(carry)

        logf = _log_sigmoid(fz_ref[...] + b_ref[...])
        tri = jnp.where(lax.broadcasted_iota(jnp.int32, (T, T), 0) >= lax.broadcasted_iota(jnp.int32, (T, T), 1),
                        1.0, 0.0).astype(BF)
        fv = _tri_dot(tri, logf) + carry[...]
        f_ref[...] = fv
        carry[...] = _row_of(fv, lax.broadcasted_iota(jnp.int32, (T, LANES), 0), T - 1)

    return pl.pallas_call(
        body, name=name, grid=(S // T,),
        in_specs=[pl.BlockSpec((T, LANES), lambda i: (i, 4 * D // LANES)), pl.BlockSpec((1, LANES), lambda i: (0, 0))],
        out_specs=pl.BlockSpec((T, LANES), lambda i: (i, 0)),
        out_shape=jax.ShapeDtypeStruct((S, LANES), F32),
        scratch_shapes=[pltpu.VMEM((1, LANES), F32)],
        compiler_params=_cp("arbitrary"),
    )(proj, bf_pad)


def _pair_stats(sq, lo):
    del lo
    a = lax.broadcasted_iota(jnp.int32, (LANES, LANES), 0) < FOX_DH
    b = lax.broadcasted_iota(jnp.int32, (LANES, LANES), 1) < FOX_DH
    avg = jnp.where(a == b, 1.0 / FOX_DH, 0.0).astype(BF)
    hi, mid, low = _split3(sq)
    return _dot(hi, avg) + _dot(mid, avg) + _dot(low, avg)


def _fox_prep(proj, fcum, qw2, kw2, *, name):
    S = proj.shape[0]
    D = proj.shape[1] // 5
    HP = D // LANES
    T = _pick(S, FOX_ROWS_PER_STEP, 16)

    def body(q_ref, k_ref, v_ref, f_ref, qw_ref, kw_ref, qa_ref, ka_ref, va_ref, vt_ref):
        hp = pl.program_id(1)
        lane = lax.broadcasted_iota(jnp.int32, (T, LANES), 1)
        lo = lane < FOX_DH
        qv, kv, vv, fv = q_ref[...], k_ref[...], v_ref[...], f_ref[...]
        qn = qv * lax.rsqrt(_pair_stats(qv * qv, lo) + EPS) * qw_ref[...] * (0.125 * LOG2E)
        kn = kv * lax.rsqrt(_pair_stats(kv * kv, lo) + EPS) * kw_ref[...]
        ones_q = jnp.where((lane >= 67) & (lane <= 69), 1.0, 0.0)
        ones_k = jnp.where(((lane >= 64) & (lane <= 66)) | ((lane >= 70) & (lane <= 72)), 1.0, 0.0)
        ones_v = jnp.where((lane >= 64) & (lane <= 66), 1.0, 0.0)
        for hh in range(2):
            fh = jnp.sum(jnp.where(lane == 2 * hp + hh, fv, 0.0), axis=-1, keepdims=True) * LOG2E
            pieces = [p.astype(F32) for p in _split3(fh)]

            def half(x):
                return jnp.where(lo, x if hh == 0 else pltpu.roll(x, FOX_DH, 1), 0.0)

            qa_ref[hh] = _lane_put(half(qn) + ones_q, lane, 64, pieces).astype(BF)
            ka_ref[hh] = _lane_put(half(kn) + ones_k, lane, 67, [-p for p in pieces]).astype(BF)
            va = half(vv) + ones_v
            va_ref[hh] = va.astype(BF)
            vt_ref[hh] = va.T.astype(BF)

    def part(p):
        return pl.BlockSpec((T, LANES), lambda i, hp: (i, p * HP + hp))

    vec = pl.BlockSpec((1, LANES), lambda i, hp: (0, 0))
    aug = pl.BlockSpec((2, T, LANES), lambda i, hp: (hp, i, 0))
    return pl.pallas_call(
        body, name=name, grid=(S // T, HP),
        in_specs=[part(0), part(1), part(2), pl.BlockSpec((T, LANES), lambda i, hp: (i, 0)), vec, vec],
        out_specs=[aug, aug, aug, pl.BlockSpec((2, LANES, T), lambda i, hp: (hp, 0, i))],
        out_shape=[jax.ShapeDtypeStruct((2 * HP, S, LANES), BF)] * 3 + [jax.ShapeDtypeStruct((2 * HP, LANES, S), BF)],
        compiler_params=_cp("parallel", "arbitrary"),
    )(proj, proj, proj, fcum, qw2, kw2)


def _fox_block(S):
    return _pick(S, 256, 16)


def _fox_skip_bounds(fcum, qn_w, kn_w, nheads):
    S = fcum.shape[0]
    B = _fox_block(S)
    qk = 8.0 * LOG2E * 1.02 * jnp.max(jnp.abs(qn_w)) * jnp.max(jnp.abs(kn_w))
    thresh = -(2.0 * qk + 152.0)
    f2 = fcum[:, :nheads] * LOG2E
    first, last = f2[0::B], f2[B - 1::B]
    nb = S // B
    blk = jnp.arange(nb)
    dead = (first[0::2, None, :] - last[None, :, :]) < thresh
    jmin = jnp.sum(dead & (blk[None, :, None] < 2 * jnp.arange(nb // 2)[:, None, None]), axis=1)
    live = (first[:, None, :] - last[None, :, :]) >= thresh
    imax = blk[:, None] + jnp.sum(live & (blk[:, None, None] > blk[None, :, None]), axis=0)
    return jmin.T.astype(jnp.int32), imax.T.astype(jnp.int32)


def _fox_fwd(jmin, qa, ka, vat, proj, *, name):
    H, S, _ = qa.shape
    HP = H // 2
    D = HP * LANES
    B = _fox_block(S)
    BQ = 2 * B
    nq = S // BQ

    def body(jmin_ref, q_ref, k_ref, vt_ref, g_ref, y_ref, o_ref, q2_ref):
        hp, i = pl.program_id(0), pl.program_id(1)
        lane = lax.broadcasted_iota(jnp.int32, (BQ, LANES), 1)
        lo = lane < FOX_DH
        in_stat = (lane >= 70) & (lane <= 75)
        causal = lax.broadcasted_iota(jnp.int32, (BQ, BQ), 0) <= lax.broadcasted_iota(jnp.int32, (BQ, BQ), 1)
        row = lax.broadcasted_iota(jnp.int32, (LANES, BQ), 0)
        m0, acc0 = jnp.full((1, BQ), -jnp.inf, F32), jnp.zeros((LANES, BQ), F32)
        outs = []
        for hh in range(2):
            qb = q_ref[hh]

            def block(j, carry, masked=False):
                m, acc = carry
                sl = pl.ds(pl.multiple_of(j * BQ, BQ), BQ)
                st = _dg(k_ref[hh, sl, :], qb, NT)
                if masked:
                    st = jnp.where(causal, st, -jnp.inf)
                m_new = jnp.maximum(m, jnp.ceil(jnp.max(st, axis=0, keepdims=True)))
                p = jnp.exp2(st - m_new).astype(BF)
                return m_new, acc * jnp.exp2(m - m_new) + _dot(vt_ref[hh, :, sl], p)

            carry = lax.fori_loop(jmin_ref[2 * hp + hh, i] // 2, i, block, (m0, acc0))
            m, acc = block(i, carry, masked=True)
            linv = 1.0 / jnp.sum(jnp.where(row == FOX_DH, acc, 0.0), axis=0, keepdims=True)
            tile = acc * linv
            for n, piece in enumerate(_split3(m) + _split3(linv)):
                tile = jnp.where(row == 70 + n, piece.astype(F32), tile)
            tile = tile.T
            outs.append(tile)
            q2_ref[hh] = jnp.where(in_stat, jnp.where(lane <= 72, -tile, tile), qb.astype(F32)).astype(BF)
        o = jnp.where(lo, outs[0], pltpu.roll(outs[1], FOX_DH, 1))
        o_ref[...] = o
        y_ref[...] = (o * _sigmoid(g_ref[...])).astype(BF)

    blk = pl.BlockSpec((BQ, LANES), lambda hp, i, jm: (i, hp))
    qblk = pl.BlockSpec((2, BQ, LANES), lambda hp, i, jm: (hp, i, 0))
    full = pl.BlockSpec((2, S, LANES), lambda hp, i, jm: (hp, 0, 0))
    full_t = pl.BlockSpec((2, LANES, S), lambda hp, i, jm: (hp, 0, 0))
    return pl.pallas_call(
        body, name=name,
        grid_spec=pltpu.PrefetchScalarGridSpec(
            num_scalar_prefetch=1, grid=(HP, nq),
            in_specs=[qblk, full, full_t, pl.BlockSpec((BQ, LANES), lambda hp, i, jm: (i, 3 * HP + hp))],
            out_specs=[blk, blk, qblk]),
        out_shape=[jax.ShapeDtypeStruct((S, D), BF), jax.ShapeDtypeStruct((S, D), F32),
                   jax.ShapeDtypeStruct((H, S, LANES), BF)],
        compiler_params=_cp("parallel", "arbitrary"),
    )(jmin, qa, ka, vat, proj)


def _fox_bwd_prep(dy, o, proj, q2, *, name):
    S, D = dy.shape
    HP = D // LANES
    T = _pick(S, FOX_ROWS_PER_STEP, 16)

    def body(dy_ref, o_ref, g_ref, q2_ref, da_ref):
        lane = lax.broadcasted_iota(jnp.int32, (T, LANES), 1)
        lo = lane < FOX_DH
        in_linv = (lane >= 73) & (lane <= 75)
        linv = [jnp.sum(jnp.where(in_linv, q2_ref[hh].astype(F32), 0.0), axis=-1, keepdims=True) for hh in range(2)]
        u = (dy_ref[...] * _sigmoid(g_ref[...]) * jnp.where(lo, linv[0], linv[1])).astype(BF).astype(F32)
        prod = u * o_ref[...]
        d_lo = jnp.sum(jnp.where(lo, prod, 0.0), axis=-1, keepdims=True)
        d_hi = jnp.sum(jnp.where(lo, 0.0, prod), axis=-1, keepdims=True)
        for hh, delta in enumerate((d_lo, d_hi)):
            base = jnp.where(lo, u if hh == 0 else pltpu.roll(u, FOX_DH, 1), 0.0)
            da_ref[hh] = _lane_put(base, lane, 64, [-(p.astype(F32)) for p in _split3(delta)]).astype(BF)

    blk = pl.BlockSpec((T, LANES), lambda i, hp: (i, hp))
    aug = pl.BlockSpec((2, T, LANES), lambda i, hp: (hp, i, 0))
    return pl.pallas_call(
        body, name=name, grid=(S // T, HP),
        in_specs=[blk, blk, pl.BlockSpec((T, LANES), lambda i, hp: (i, 3 * HP + hp)), aug],
        out_specs=aug,
        out_shape=jax.ShapeDtypeStruct((2 * HP, S, LANES), BF),
        compiler_params=_cp("parallel", "arbitrary"),
    )(dy, o, proj, q2)


def _fox_bwd(imax, q2, ka, va, doa, *, name):
    H, S, _ = q2.shape
    B = _fox_block(S)
    nb = S // B

    def body(imax_ref, q_ref, do_ref, k_ref, v_ref, dq_ref, dk_ref, dv_ref, cs_ref):
        j = pl.program_id(1)
        end = imax_ref[pl.program_id(0), j] + 1

        @pl.when(j == 0)
        def _():
            dq_ref[...] = jnp.zeros_like(dq_ref)

        kb, vb = k_ref[...], v_ref[...]

        def step(i, carry, nblk=1):
            dk_acc, dv_acc, cs_acc = carry
            rows = nblk * B
            sl = pl.ds(pl.multiple_of(i * B, B), rows)
            qb, dob = q_ref[sl, :], do_ref[sl, :]
            s = _dg(qb, kb, NT)
            ahead = lax.broadcasted_iota(jnp.int32, (rows, B), 0) - lax.broadcasted_iota(jnp.int32, (rows, B), 1)
            pb = jnp.exp2(jnp.where(ahead >= (j - i) * B, s, -jnp.inf)).astype(BF)
            ds = pb.astype(F32) * _dg(dob, vb, NT)
            dsb = ds.astype(BF)
            cs_acc = cs_acc + jnp.sum(ds.reshape(rows // 8, 8, B), axis=0)
            dv_acc = dv_acc + _dg(pb, dob, TN)
            dk_acc = dk_acc + _dg(dsb, qb, TN)
            dq_ref[sl, :] += _dot(dsb, kb)
            return dk_acc, dv_acc, cs_acc

        zero = jnp.zeros((B, LANES), F32)
        carry = (zero, zero, jnp.zeros((8, B), F32))
        pos = j
        for U in FOX_BWD_TILES:
            n = (end - pos) // U
            carry = lax.fori_loop(0, n, lambda ii, c, pos=pos, U=U: step(pos + U * ii, c, nblk=U), carry)
            pos = pos + U * n
        dk_acc, dv_acc, cs_acc = carry
        dk_ref[...] = dk_acc
        dv_ref[...] = dv_acc
        cs_ref[...] = jnp.sum(cs_acc, axis=0, keepdims=True)

    full = pl.BlockSpec((None, S, LANES), lambda h, j, im: (h, 0, 0))
    blk = pl.BlockSpec((None, B, LANES), lambda h, j, im: (h, j, 0))
    return pl.pallas_call(
        body, name=name,
        grid_spec=pltpu.PrefetchScalarGridSpec(
            num_scalar_prefetch=1, grid=(H, nb),
            in_specs=[full, full, blk, blk],
            out_specs=[full, blk, blk, pl.BlockSpec((None, 1, B), lambda h, j, im: (h, 0, j))]),
        out_shape=[jax.ShapeDtypeStruct((H, S, LANES), F32)] * 3 + [jax.ShapeDtypeStruct((H, 1, S), F32)],
        compiler_params=_cp("parallel", "arbitrary"),
    )(imax, q2, doa, ka, va)


def _fox_bwd_post(dqa, dka, dva, proj, dy, o, qw2, kw2, *, name):
    S, D = dy.shape
    HP = D // LANES
    T = _pick(S, FOX_ROWS_PER_STEP, 16)

    def body(dq_ref, dk_ref, dv_ref, q_ref, k_ref, g_ref, dy_ref, o_ref, qw_ref, kw_ref, dp_ref, dqw_ref, dkw_ref):
        @pl.when((pl.program_id(0) == 0) & (pl.program_id(1) == 0))
        def _():
            dqw_ref[...] = jnp.zeros_like(dqw_ref)
            dkw_ref[...] = jnp.zeros_like(dkw_ref)

        lane = lax.broadcasted_iota(jnp.int32, (T, LANES), 1)
        lo = lane < FOX_DH

        def pair(ref):
            return jnp.where(lo, ref[0], pltpu.roll(ref[1], FOX_DH, 1))

        def norm_bwd(xv, w, dyn, dw_ref):
            r = lax.rsqrt(_pair_stats(xv * xv, lo) + EPS)
            xr = xv * r
            dw_ref[...] += jnp.sum(dyn * xr, axis=0, keepdims=True)
            u = dyn * w
            return r * (u - xr * _pair_stats(u * xr, lo))

        dp_ref[0] = norm_bwd(q_ref[...], qw_ref[...], pair(dq_ref) * 0.125, dqw_ref).astype(BF)
        dp_ref[1] = norm_bwd(k_ref[...], kw_ref[...], pair(dk_ref) * (1.0 / LOG2E), dkw_ref).astype(BF)
        dp_ref[2] = pair(dv_ref).astype(BF)
        sg = _sigmoid(g_ref[...])
        dp_ref[3] = (dy_ref[...] * o_ref[...] * sg * (1.0 - sg)).astype(BF)

    def part(p):
        return pl.BlockSpec((T, LANES), lambda i, hp: (i, p * HP + hp))

    aug = pl.BlockSpec((2, T, LANES), lambda i, hp: (hp, i, 0))
    blk = pl.BlockSpec((T, LANES), lambda i, hp: (i, hp))
    vec = pl.BlockSpec((1, LANES), lambda i, hp: (0, 0))
    return pl.pallas_call(
        body, name=name, grid=(S // T, HP),
        in_specs=[aug, aug, aug, part(0), part(1), part(3), blk, blk, vec, vec],
        out_specs=[pl.BlockSpec((4, T, LANES), lambda i, hp: (0, i, hp)), vec, vec],
        out_shape=[jax.ShapeDtypeStruct((5, S, D), BF), jax.ShapeDtypeStruct((1, LANES), F32),
                   jax.ShapeDtypeStruct((1, LANES), F32)],
        compiler_params=_cp("arbitrary", "arbitrary"),
    )(dqa, dka, dva, proj, proj, proj, dy, o, qw2, kw2)


def _fox_dfz(colsum, nheads, proj, bf_pad, dproj, *, name):
    S = colsum.shape[0]
    H = nheads
    D = dproj.shape[2]
    T = _pick(S, 256, 16)
    nb = S // T

    def body(cs_ref, fz_ref, b_ref, _, dp_ref, db_ref, carry):
        @pl.when(pl.program_id(0) == 0)
        def _():
            carry[...] = jnp.zeros_like(carry)
            db_ref[...] = jnp.zeros_like(db_ref)

        lane = lax.broadcasted_iota(jnp.int32, (T, LANES), 1)
        df = -cs_ref[...]
        triu = jnp.where(lax.broadcasted_iota(jnp.int32, (T, T), 0) <= lax.broadcasted_iota(jnp.int32, (T, T), 1),
                         1.0, 0.0).astype(BF)
        dlogf = _tri_dot(triu, df) + carry[...]
        carry[...] = _row_of(dlogf, lax.broadcasted_iota(jnp.int32, (T, LANES), 0), 0)
        dfz = jnp.where(lane < H, dlogf * _sigmoid(-(fz_ref[...] + b_ref[...])), 0.0)
        db_ref[...] += jnp.sum(dfz, axis=0, keepdims=True)
        dp_ref[...] = jnp.zeros_like(dp_ref)
        dp_ref[:, 0:LANES] = dfz.astype(BF)

    return pl.pallas_call(
        body, name=name, grid=(nb,),
        in_specs=[pl.BlockSpec((T, LANES), lambda i: (nb - 1 - i, 0)),
                  pl.BlockSpec((T, LANES), lambda i: (nb - 1 - i, 4 * D // LANES)),
                  pl.BlockSpec((1, LANES), lambda i: (0, 0)),
                  pl.BlockSpec(memory_space=pl.ANY)],
        out_specs=[pl.BlockSpec((None, T, D), lambda i: (4, nb - 1 - i, 0)), pl.BlockSpec((1, LANES), lambda i: (0, 0))],
        out_shape=[jax.ShapeDtypeStruct(dproj.shape, BF), jax.ShapeDtypeStruct((1, LANES), F32)],
        scratch_shapes=[pltpu.VMEM((1, LANES), F32)],
        input_output_aliases={3: 0},
        compiler_params=_cp("arbitrary"),
    )(colsum, proj, bf_pad, dproj)


def _mod_fwd(c16, w, b, *, name):
    L, D, N = w.shape
    tn = _pick(N, 512)

    def body(c_ref, w_ref, b_ref, o_ref):
        cv = c_ref[...]
        ca = (cv * _sigmoid(cv)).astype(BF)
        o_ref[...] = _dot(ca, w_ref[...].astype(BF)) + b_ref[...]

    return pl.pallas_call(
        body, name=name, grid=(L, N // tn),
        in_specs=[pl.BlockSpec((16, D), lambda l, j: (0, 0)), pl.BlockSpec((None, D, tn), lambda l, j: (l, 0, j)),
                  pl.BlockSpec((None, 1, tn), lambda l, j: (l, 0, j))],
        out_specs=pl.BlockSpec((None, 16, tn), lambda l, j: (l, 0, j)),
        out_shape=jax.ShapeDtypeStruct((L, 16, N), F32),
        compiler_params=_cp("parallel", "arbitrary"),
    )(c16, w, b)


def _mod_bwd(c16, dmod, *, name):
    L, _, N = dmod.shape
    D = c16.shape[1]
    tn = _pick(N, 512)

    def body(c_ref, d_ref, o_ref):
        cv = c_ref[...]
        ca = (cv * _sigmoid(cv)).astype(BF)
        o_ref[...] = _dg(ca, d_ref[...].astype(BF), TN)

    return pl.pallas_call(
        body, name=name, grid=(L, N // tn),
        in_specs=[pl.BlockSpec((16, D), lambda l, j: (0, 0)), pl.BlockSpec((None, 16, tn), lambda l, j: (l, 0, j))],
        out_specs=pl.BlockSpec((None, D, tn), lambda l, j: (l, 0, j)),
        out_shape=jax.ShapeDtypeStruct((L, D, N), F32),
        compiler_params=_cp("parallel", "arbitrary"),
    )(c16, dmod)


def _adamw_math(w, g, m, v):
    m = ADAM_B1 * m + (1.0 - ADAM_B1) * g
    v = ADAM_B2 * v + (1.0 - ADAM_B2) * (g * g)
    m_hat = m / (1.0 - ADAM_B1 ** ADAM_STEP)
    v_hat = v / (1.0 - ADAM_B2 ** ADAM_STEP)
    return -ADAM_LR * (m_hat / (jnp.sqrt(v_hat) + ADAM_EPS) + ADAM_WD * w), m, v


def _adamw(w, g, m, v, *, g_at=None, name):
    R, C = w.shape
    row0 = 0 if g_at is None else g_at[1]
    tr = min(math.gcd(row0, 256) if row0 else 256, -(-R // 8) * 8)
    g0 = row0 // tr
    if g_at is None:
        g_spec = pl.BlockSpec((tr, C), lambda i: (i, 0))
    else:
        g_spec = pl.BlockSpec((None, tr, C), lambda i: (g_at[0], g0 + i, 0))

    def body(w_ref, g_ref, m_ref, v_ref, d_ref, mo_ref, vo_ref):
        d, mn, vn = _adamw_math(w_ref[...], g_ref[...], m_ref[...], v_ref[...])
        d_ref[...] = d
        mo_ref[...] = mn
        vo_ref[...] = vn

    blk = pl.BlockSpec((tr, C), lambda i: (i, 0))
    return pl.pallas_call(
        body, name=name, grid=(pl.cdiv(R, tr),),
        in_specs=[blk, g_spec, blk, blk],
        out_specs=[blk, blk, blk],
        out_shape=[jax.ShapeDtypeStruct((R, C), F32)] * 3,
        compiler_params=_cp("parallel"),
    )(w, g, m, v)


def _sum_parts(parts, *, name):
    P, R, C = parts.shape

    def body(p_ref, o_ref):
        acc = p_ref[0]
        for p in range(1, P):
            acc = acc + p_ref[p]
        o_ref[...] = acc

    return pl.pallas_call(
        body, name=name, grid=(1,),
        in_specs=[pl.BlockSpec((P, R, C), lambda i: (0, 0, 0))],
        out_specs=pl.BlockSpec((R, C), lambda i: (0, 0)),
        out_shape=jax.ShapeDtypeStruct((R, C), F32),
        compiler_params=_cp("arbitrary"),
    )(parts)


def _add_halves(g4, recv, c_idx, *, name):
    _, _, Rh, C = g4.shape
    tr = min(256, Rh)

    def body(c_ref, a_ref, b_ref, o_ref):
        o_ref[...] = (a_ref[...] + b_ref[...].astype(F32)).astype(BF)

    return pl.pallas_call(
        body, name=name,
        grid_spec=pltpu.PrefetchScalarGridSpec(
            num_scalar_prefetch=1, grid=(4, pl.cdiv(Rh, tr)),
            in_specs=[pl.BlockSpec((None, None, tr, C), lambda j, r, c: (j, c[0], r, 0)),
                      pl.BlockSpec((None, tr, C), lambda j, r, c: (j, r, 0))],
            out_specs=pl.BlockSpec((None, tr, C), lambda j, r, c: (j, r, 0))),
        out_shape=jax.ShapeDtypeStruct((4, Rh, C), BF),
        compiler_params=_cp("parallel", "arbitrary"),
    )(c_idx, g4, recv)


def _add_four(g4, from_sibling, from_chips, pos, *, name):
    _, _, Rh, C = g4.shape
    tr = min(256, Rh)

    def body(p_ref, a_ref, s_ref, b_ref, o_ref):
        own = a_ref[...] + s_ref[...].astype(F32)
        o_ref[...] = ((own + b_ref[0].astype(F32)) + b_ref[1].astype(F32)) + b_ref[2].astype(F32)

    return pl.pallas_call(
        body, name=name,
        grid_spec=pltpu.PrefetchScalarGridSpec(
            num_scalar_prefetch=1, grid=(pl.cdiv(Rh, tr),),
            in_specs=[pl.BlockSpec((None, None, tr, C), lambda r, p: (p[0], p[1], r, 0)),
                      pl.BlockSpec((None, tr, C), lambda r, p: (p[0], r, 0)),
                      pl.BlockSpec((3, tr, C), lambda r, p: (0, r, 0))],
            out_specs=pl.BlockSpec((None, tr, C), lambda r, p: (p[1], r, 0))),
        out_shape=jax.ShapeDtypeStruct((2, Rh, C), F32),
        compiler_params=_cp("arbitrary"),
    )(pos, g4, from_sibling, from_chips)


HBM = pl.BlockSpec(memory_space=pltpu.HBM)


def _mesh_pos():
    return lax.axis_index("x"), lax.axis_index("y"), lax.axis_index("c")


def _other_chips(x, y):
    return [(1 - x, y), (x, 1 - y), (1 - x, 1 - y)]


def _allgather_small(xs, *, name):
    m_per, n = xs.shape

    def body(x_ref, out_ref, send_sems, recv_sems, local_sem):
        x, y, c = _mesh_pos()
        me, sibling = (x, y, c), (x, y, 1 - c)
        chips = _other_chips(x, y)

        def rows(px, py, pc):
            return out_ref.at[pl.ds((4 * px + 2 * py + pc) * m_per, m_per), :]

        def copy(k, block, to, src=None):
            return pltpu.make_async_remote_copy(
                src_ref=rows(*block) if src is None else src, dst_ref=rows(*block),
                send_sem=send_sems.at[k], recv_sem=recv_sems.at[k], device_id=to, device_id_type=MESH)

        mine = pltpu.make_async_copy(x_ref, rows(*me), local_sem)
        mine.start()
        first = [copy(0, me, sibling, src=x_ref)]
        first += [copy(1 + j, me, (*chip, c), src=x_ref) for j, chip in enumerate(chips)]
        for cp in first:
            cp.start()
        passed = [copy(4 + j, (*chip, c), sibling) for j, chip in enumerate(chips)]
        for j, chip in enumerate(chips):
            copy(1 + j, (*chip, c), me).wait_recv()
            passed[j].start()
        copy(0, sibling, me).wait_recv()
        for j, chip in enumerate(chips):
            copy(4 + j, (*chip, 1 - c), me).wait_recv()
        for cp in first + passed:
            cp.wait_send()
        mine.wait()

    return pl.pallas_call(
        body, name=name,
        out_shape=jax.ShapeDtypeStruct((N_DEV * m_per, n), xs.dtype),
        in_specs=[pl.BlockSpec(memory_space=pltpu.VMEM)],
        out_specs=pl.BlockSpec(memory_space=pltpu.VMEM),
        scratch_shapes=[pltpu.SemaphoreType.DMA((7,)), pltpu.SemaphoreType.DMA((7,)), pltpu.SemaphoreType.DMA],
    )(xs)


def _chip_slab_copies(s_ref, out_ref, send_sems, recv_sems):
    R = s_ref.shape[0]
    Rh = R // 2
    x, y, c = _mesh_pos()
    me, sibling = (x, y, c), (x, y, 1 - c)
    chips = _other_chips(x, y)

    def half(px, py, pc):
        return out_ref.at[2 * px + py, pl.ds(pc * Rh, Rh), :]

    def copy(k, block, to, src=None):
        return pltpu.make_async_remote_copy(
            src_ref=half(*block) if src is None else src, dst_ref=half(*block),
            send_sem=send_sems.at[k], recv_sem=recv_sems.at[k], device_id=to, device_id_type=MESH)

    first = [copy(j, me, (*chip, c), src=s_ref.at[pl.ds(c * Rh, Rh), :]) for j, chip in enumerate(chips)]
    passed = [copy(3 + j, (*chip, c), sibling) for j, chip in enumerate(chips)]
    landed = [copy(j, (*chip, c), me) for j, chip in enumerate(chips)]
    from_sibling = [copy(3 + j, (*chip, 1 - c), me) for j, chip in enumerate(chips)]
    return first, passed, landed, from_sibling


def _gather_behind(s_ref, out_ref, send_sems, recv_sems, step, nsteps):
    first, passed, landed, from_sibling = _chip_slab_copies(s_ref, out_ref, send_sems, recv_sems)

    @pl.when(step == 0)
    def _():
        for cp in first:
            cp.start()

    @pl.when(step == (3 * nsteps) // 4)
    def _():
        for arrived, onward in zip(landed, passed):
            arrived.wait_recv()
            onward.start()

    def finish():
        @pl.when(step == nsteps - 1)
        def _():
            for cp in from_sibling:
                cp.wait_recv()
            for cp in first + passed:
                cp.wait_send()

    return finish


def _allgather_chip_slabs(slab, *, name):
    R, C = slab.shape

    def body(s_ref, out_ref, send_sems, recv_sems):
        first, passed, landed, from_sibling = _chip_slab_copies(s_ref, out_ref, send_sems, recv_sems)
        for cp in first:
            cp.start()
        for arrived, onward in zip(landed, passed):
            arrived.wait_recv()
            onward.start()
        for cp in from_sibling:
            cp.wait_recv()
        for cp in first + passed:
            cp.wait_send()

    return pl.pallas_call(
        body, name=name,
        out_shape=jax.ShapeDtypeStruct((N_CHIPS, R, C), slab.dtype),
        in_specs=[HBM], out_specs=HBM,
        scratch_shapes=[pltpu.SemaphoreType.DMA((6,)), pltpu.SemaphoreType.DMA((6,))],
    )(slab)


def _swap_halves(mine, *, name):
    def body(g_ref, out_ref, send_sems, recv_sems):
        x, y, c = _mesh_pos()
        copies = [pltpu.make_async_remote_copy(
            src_ref=g_ref.at[j], dst_ref=out_ref.at[j], send_sem=send_sems.at[j], recv_sem=recv_sems.at[j],
            device_id=(x, y, 1 - c), device_id_type=MESH) for j in range(N_CHIPS)]
        for cp in copies:
            cp.start()
        for cp in copies:
            cp.wait()

    return pl.pallas_call(
        body, name=name,
        out_shape=jax.ShapeDtypeStruct(mine.shape, mine.dtype),
        in_specs=[HBM], out_specs=HBM,
        scratch_shapes=[pltpu.SemaphoreType.DMA((N_CHIPS,)), pltpu.SemaphoreType.DMA((N_CHIPS,))],
    )(mine)


def _scatter_copies(p_ref, out_ref, send_sems, recv_sems):
    x, y, c = _mesh_pos()
    return [pltpu.make_async_remote_copy(
        src_ref=p_ref.at[2 * px + py], dst_ref=out_ref.at[j], send_sem=send_sems.at[j], recv_sem=recv_sems.at[j],
        device_id=(px, py, c), device_id_type=MESH) for j, (px, py) in enumerate(_other_chips(x, y))]


def _join_halves(buf, *, name):
    def body(b_ref, out_ref, send_sem, recv_sem):
        x, y, c = _mesh_pos()
        cp = pltpu.make_async_remote_copy(
            src_ref=b_ref.at[c], dst_ref=out_ref.at[c], send_sem=send_sem, recv_sem=recv_sem,
            device_id=(x, y, 1 - c), device_id_type=MESH)
        cp.start()
        cp.wait()

    return pl.pallas_call(
        body, name=name,
        out_shape=jax.ShapeDtypeStruct(buf.shape, buf.dtype),
        in_specs=[HBM], out_specs=HBM, input_output_aliases={0: 0},
        scratch_shapes=[pltpu.SemaphoreType.DMA, pltpu.SemaphoreType.DMA],
    )(buf)


def _pad_rows(a, mult):
    pad = (-a.shape[0]) % mult
    return a if pad == 0 else jnp.pad(a, ((0, pad),) + ((0, 0),) * (a.ndim - 1))


def _local_step(x, target, mod, wts, small, slabs=None, unpacks=None, reduce_early=None, grad_slab=None,
                reduce_late=None):
    S, D = x.shape
    HP = D // LANES
    row = lambda v: v.reshape(1, -1)
    msplit = [[row(mod[i, k * D:(k + 1) * D]) for k in range(6)] for i in range(2)]
    gw, gs = {}, {}
    dmod = [[None] * 6 for _ in range(2)]
    slab, where = grad_slab if grad_slab is not None else (None, {})

    def dw(key, a, b, name):
        nonlocal slab
        if key in where:
            slab = _matmul_tn(a, b, name=name, into=(slab,) + where[key])
        else:
            gw[key] = _matmul_tn(a, b, name=name)

    sh1, sc1, g1, sh2, sc2, g2 = msplit[0]
    n1w0, n2w0 = row(small["norm1_w"][0]), row(small["norm2_w"][0])
    slabs = slabs if slabs is not None else (None, None, None)
    proj0, h1_0, *gathered = _ln_matmul(x, n1w0, sc1, sh1, wts["hg_w_in"], slabs[0], relu2=False, name="hg_in_proj")
    if slabs[0] is not None:
        wts = {**wts, **unpacks[0](gathered[0])}
    gn = small["hg_gn_w"].reshape(1, LANES)
    ypre0, o0, states, *gathered = _hg_fwd(proj0, small["hg_lb"], gn, slabs[1], name="hg_fwd")
    if slabs[1] is not None:
        wts = {**wts, **unpacks[1](gathered[0])}
    x1, ymix0 = _matmul_resid(ypre0, wts["hg_w_out"], x, g1, name="hg_out_proj")
    a0, u0, h2_0, *gathered = _ln_matmul(x1, n2w0, sc2, sh2, wts["mlp_w1_0"], slabs[2], relu2=True, name="mlp0_up")
    if slabs[2] is not None:
        wts = {**wts, **unpacks[2](gathered[0])}
    x2, ymlp0 = _matmul_resid(u0, wts["mlp_w2_0"], x1, g2, name="mlp0_down")

    sh1b, sc1b, g1b, sh2b, sc2b, g2b = msplit[1]
    n1w1, n2w1 = row(small["norm1_w"][1]), row(small["norm2_w"][1])
    proj1, h1_1 = _ln_matmul(x2, n1w1, sc1b, sh1b, wts["fox_w_in"], relu2=False, name="fox_in_proj")
    nheads = 2 * HP
    bf_pad = jnp.pad(small["fox_b_f"].reshape(1, nheads), ((0, 0), (0, LANES - nheads)))
    qw2 = jnp.tile(small["fox_qn_w"].reshape(1, FOX_DH), (1, 2))
    kw2 = jnp.tile(small["fox_kn_w"].reshape(1, FOX_DH), (1, 2))
    fcum = _fox_cumsum(proj1, bf_pad, name="fox_cumsum")
    qa, ka, va, vat = _fox_prep(proj1, fcum, qw2, kw2, name="fox_prep")
    jmin, imax = _fox_skip_bounds(fcum, small["fox_qn_w"], small["fox_kn_w"], nheads)
    ypre1, o1, q2 = _fox_fwd(jmin, qa, ka, vat, proj1, name="fox_fwd")
    x3, ymix1 = _matmul_resid(ypre1, wts["fox_w_out"], x2, g1b, name="fox_out_proj")
    a1, u1, h2_1 = _ln_matmul(x3, n2w1, sc2b, sh2b, wts["mlp_w1_1"], relu2=True, name="mlp1_up")
    x4, ymlp1 = _matmul_resid(u1, wts["mlp_w2_1"], x3, g2b, name="mlp1_down")

    loss, dx4, dfw = _loss_kernel(x4, row(small["final_w"]), target, name="loss")
    gs["final_w"] = dfw.reshape(-1)

    def mlp_bwd(i, dx_out, x_in, h2, a, u, ymlp, n2w, sc2_, g2_):
        dz, dm, dg2 = _gate_matmul_nt(dx_out, g2_, ymlp, wts[f"mlp_w2_{i}"], a, name=f"mlp{i}_down_bwd")
        dw(f"mlp_w2_{i}", u, dm[None], f"mlp{i}_dw2")
        dw(f"mlp_w1_{i}", h2, dz[None], f"mlp{i}_dw1")
        dx_in, dsc, dsh, dnw = _matmul_nt_lnbwd(dz[None], wts[f"mlp_w1_{i}"], x_in, n2w, sc2_, dx_out,
                                                name=f"mlp{i}_up_bwd")
        dmod[i][3], dmod[i][4], dmod[i][5] = dsh, dsc, dg2
        return dx_in, dnw

    dx3, dn2w1 = mlp_bwd(1, dx4, x3, h2_1, a1, u1, ymlp1, n2w1, sc2b, g2b)
    dyp1, dm1, dg1b = _gate_matmul_nt(dx3, g1b, ymix1, wts["fox_w_out"], None, name="fox_out_bwd")
    dw("fox_w_out", ypre1, dm1[None], "fox_dw_out")
    doa = _fox_bwd_prep(dyp1, o1, proj1, q2, name="fox_bwd_prep")
    dqa, dka, dva, colsum = _fox_bwd(imax, q2, ka, va, doa, name="fox_bwd")
    colsum = jnp.pad(colsum[:, 0, :].T, ((0, 0), (0, LANES - nheads)))
    dproj1, dqw, dkw = _fox_bwd_post(dqa, dka, dva, proj1, dyp1, o1, qw2, kw2, name="fox_bwd_post")
    dproj1, dbf = _fox_dfz(colsum, nheads, proj1, bf_pad, dproj1, name="fox_dfz")
    dw("fox_w_in", h1_1, dproj1, "fox_dw_in")
    dx2, dsc, dsh, dn1w1 = _matmul_nt_lnbwd(dproj1, wts["fox_w_in"], x2, n1w1, sc1b, dx3, name="fox_in_bwd")
    dmod[1][0], dmod[1][1], dmod[1][2] = dsh, dsc, dg1b
    gs["fox_qn_w"] = dqw[0, :FOX_DH] + dqw[0, FOX_DH:]
    gs["fox_kn_w"] = dkw[0, :FOX_DH] + dkw[0, FOX_DH:]
    gs["fox_b_f"] = dbf[0, :nheads]

    dx1, dn2w0 = mlp_bwd(0, dx2, x1, h2_0, a0, u0, ymlp0, n2w0, sc2, g2)
    dyp0, dm0, dg1 = _gate_matmul_nt(dx1, g1, ymix0, wts["hg_w_out"], None, name="hg_out_bwd")
    dw("hg_w_out", ypre0, dm0[None], "hg_dw_out")
    part, ctx = reduce_early(gw, slab) if reduce_early is not None else (None, None)
    dproj0, dlb, dgn, *from_chips = _hg_bwd(proj0, small["hg_lb"], gn, o0, states, dyp0, part, name="hg_bwd")
    early = (ctx, from_chips[0]) if reduce_early is not None else None
    dw("hg_w_in", h1_0, dproj0, "hg_dw_in")
    part, ctx = reduce_late(gw) if reduce_late is not None else (None, None)
    dx0, dsc, dsh, dn1w0, *from_chips = _matmul_nt_lnbwd(dproj0, wts["hg_w_in"], x, n1w0, sc1, dx1, part, name="hg_in_bwd")
    late = (ctx, from_chips[0]) if reduce_late is not None else None
    dmod[0][0], dmod[0][1], dmod[0][2] = dsh, dsc, dg1
    gs["hg_lb"] = dlb
    gs["hg_gn_w"] = jnp.sum(dgn, axis=0)

    gs["norm1_w"] = jnp.concatenate([dn1w0, dn1w1], axis=0)
    gs["norm2_w"] = jnp.concatenate([dn2w0, dn2w1], axis=0)
    gs["dmod"] = jnp.stack([jnp.concatenate(dmod[i], axis=1)[0] for i in range(2)])
    return loss, dx0, gw, gs, early, late


def _pack_halves(layout):
    rh = -(-max(sum(a.shape[0] for _, a in half) for half in layout) // 16) * 16
    place, parts = {}, []
    for h, half in enumerate(layout):
        off = 0
        for n, a in half:
            place[n] = (h, off, a.shape[0])
            off += a.shape[0]
        parts.append(jnp.pad(jnp.concatenate([a.astype(BF) for _, a in half], axis=0), ((0, rh - off), (0, 0))))
    return jnp.concatenate(parts, axis=0), place, rh


SMALL_NAMES = ["norm1_w", "norm2_w", "hg_lb", "hg_gn_w", "fox_b_f", "fox_qn_w", "fox_kn_w", "final_w"]


def _pack_small(d, names):
    rows, offs, r0 = [], {}, 0
    for n in names:
        flat = d[n].reshape(-1)
        nr = -(-flat.shape[0] // LANES)
        rows.append(jnp.pad(flat, (0, nr * LANES - flat.shape[0])).reshape(nr, LANES))
        offs[n] = (r0, nr)
        r0 += nr
    return jnp.concatenate(rows, axis=0), offs


def _unpack_small(packed, offs, name, like):
    r0, nr = offs[name]
    return packed[r0:r0 + nr].reshape(-1)[:like.size].reshape(like.shape)


def kernel(x, c, w_mod, b_mod, norm1_w, norm2_w, hg_w_in, hg_w_out, hg_lb, hg_gn_w, fox_w_in, fox_b_f, fox_qn_w, fox_kn_w, fox_w_out, mlp_w1, mlp_w2, final_w, loss_target, m_w_mod, m_b_mod, m_norm1_w, m_norm2_w, m_hg_w_in, m_hg_w_out, m_hg_lb, m_hg_gn_w, m_fox_w_in, m_fox_b_f, m_fox_qn_w, m_fox_kn_w, m_fox_w_out, m_mlp_w1, m_mlp_w2, m_final_w, v_w_mod, v_b_mod, v_norm1_w, v_norm2_w, v_hg_w_in, v_hg_w_out, v_hg_lb, v_hg_gn_w, v_fox_w_in, v_fox_b_f, v_fox_qn_w, v_fox_kn_w, v_fox_w_out, v_mlp_w1, v_mlp_w2, v_final_w):
    S, D = x.shape[1], x.shape[2]
    nheads = D // FOX_DH
    ax, ay, ac = _mesh_pos()
    chip = 2 * ax + ay
    dev = 2 * chip + ac
    xs, tgt = x.reshape(S, D), loss_target.reshape(S, D)

    c_all = _allgather_small(_pad_rows(c.reshape(-1, LANES), 8), name="gather_c")
    c_all = c_all.reshape(N_DEV, -1)[:, :D]
    c16 = _pad_rows(c_all, 16)
    nmod = w_mod.shape[2]
    b_shard = lax.dynamic_slice_in_dim(b_mod, chip * nmod, nmod, axis=1)
    mod_shard = _mod_fwd(c16, w_mod, b_shard[:, None, :], name="mod_fwd")[:, :N_DEV]
    mod_all = _allgather_small(mod_shard.reshape(-1, LANES), name="gather_mod")
    mod_all = mod_all.reshape(N_CHIPS, 2, 2, N_DEV, nmod)[:, 0]
    mod = lax.dynamic_index_in_dim(mod_all, dev, axis=2, keepdims=False)
    mod = mod.transpose(1, 0, 2).reshape(2, N_CHIPS * nmod)

    fox_rows = fox_w_in.shape[2]
    col = lambda g: g.transpose(1, 0, 2).reshape(g.shape[1], -1)
    rowsh = lambda g: g.reshape(-1, g.shape[2])
    own = lambda g, s: lax.dynamic_update_index_in_dim(g, s, chip, 0)

    slab_in = hg_w_in[0].astype(BF)
    wts = {"hg_w_in": col(own(_allgather_chip_slabs(slab_in, name="gather_hg_w_in"), slab_in))}
    fox_flat, fox_cut = fox_w_in[0].reshape(fox_rows, D), fox_rows // 2
    slabs, unpacks = [], []
    for layout_w in ([[("mlp_w1_0", mlp_w1[0])], [("mlp_w2_0", mlp_w2[0])]],
                     [[("mlp_w1_1", mlp_w1[1]), ("hg_w_out", hg_w_out[0])], [("mlp_w2_1", mlp_w2[1]), ("fox_w_out", fox_w_out[0])]],
                     [[("fox_a", fox_flat[:fox_cut])], [("fox_b", fox_flat[fox_cut:])]]):
        slab_w, place_w, rh_w = _pack_halves(layout_w)

        def unpack(gathered, slab_w=slab_w, place_w=place_w, rh_w=rh_w):
            gathered = own(gathered, slab_w)
            out = {}
            for n, (h, off, rows) in place_w.items():
                g = gathered[:, h * rh_w + off:h * rh_w + off + rows, :]
                out[n] = col(g) if n.startswith("mlp_w1") else rowsh(g) if n.startswith(("mlp_w2", "hg_", "fox_w")) else g
            if "fox_a" in out:
                fox_in = col(jnp.concatenate([out.pop("fox_a"), out.pop("fox_b")], axis=1).reshape(N_CHIPS, D, fox_rows))
                out["fox_w_in"] = jnp.pad(fox_in, ((0, 0), (0, 5 * D - fox_in.shape[1])))
            return out

        slabs.append(slab_w)
        unpacks.append(unpack)

    small = {"norm1_w": norm1_w, "norm2_w": norm2_w, "hg_lb": hg_lb, "hg_gn_w": hg_gn_w, "fox_b_f": fox_b_f,
             "fox_qn_w": fox_qn_w, "fox_kn_w": fox_kn_w, "final_w": final_w}

    def uncol(g, n):
        return g.reshape(g.shape[0], N_CHIPS, n).transpose(1, 0, 2)

    pos = jnp.stack([chip, ac])

    def swap_and_add(g4, tag):
        to_sibling = lax.dynamic_index_in_dim(g4, 1 - ac, axis=1, keepdims=False).astype(BF)
        from_sibling = _swap_halves(to_sibling, name=f"rs_swap_{tag}")
        return from_sibling, _add_halves(g4, from_sibling, ac.reshape(1), name=f"rs_add_halves_{tag}")

    def finish(g4, from_sibling, from_chips, tag):
        my_half = _add_four(g4, from_sibling, from_chips, pos, name=f"rs_add_chips_{tag}")
        return _join_halves(my_half, name=f"rs_join_{tag}")

    layout = [[("mlp_w1", 2 * D), ("hg_w_out", D // 4), ("fox_w_out", D // 4)], [("mlp_w2", 2 * D), ("fox_w_in", fox_rows)]]
    place = {}
    for h, half in enumerate(layout):
        off = 0
        for n, rows in half:
            place[n] = (h, off, rows)
            off += rows

    rh = -(-max(sum(rows for _, rows in half) for half in layout) // 16) * 16
    where = {"hg_w_out": ("row",) + place["hg_w_out"][:2], "fox_w_out": ("row",) + place["fox_w_out"][:2]}
    for i in range(2):
        where[f"mlp_w1_{i}"] = ("col", place["mlp_w1"][0], place["mlp_w1"][1] + i * D)
        where[f"mlp_w2_{i}"] = ("row", place["mlp_w2"][0], place["mlp_w2"][1] + i * D)

    def reduce_early(gw, slab):
        gfox = uncol(gw["fox_w_in"][:, :4 * fox_rows], fox_rows).reshape(N_CHIPS, 1, fox_rows, D)
        h, off, _ = place["fox_w_in"]
        slab = lax.dynamic_update_slice(slab, gfox, (0, h, off, 0))
        for h, half in enumerate(layout):
            used = sum(rows for _, rows in half)
            if used < rh:
                slab = lax.dynamic_update_slice(slab, jnp.zeros((N_CHIPS, 1, rh - used, D), F32), (0, h, used, 0))
        from_sibling, part = swap_and_add(slab, "early")
        return part, (slab, from_sibling)

    def reduce_late(gw):
        g4 = uncol(gw["hg_w_in"], D).reshape(N_CHIPS, 2, D // 2, D)
        from_sibling, part = swap_and_add(g4, "late")
        return part, (g4, from_sibling)

    loss_part, grad_x, gw, gs, (early, from_chips_early), (late, from_chips_late) = _local_step(
        xs, tgt, mod, wts, small, slabs, unpacks, reduce_early, (lax.empty((N_CHIPS, 2, rh, D), F32), where), reduce_late)
    gshard = finish(*early, from_chips_early, "early")
    g_hg_w_in = finish(*late, from_chips_late, "late").reshape(D, D)

    names = ["dmod", "loss"] + SMALL_NAMES
    packed, offs = _pack_small({**gs, "loss": loss_part[0, :1]}, names)
    packed = _pad_rows(packed, 8)
    rp = packed.shape[0]
    parts = _allgather_small(packed, name="gather_small").reshape(N_DEV, rp, LANES)
    total = _sum_parts(parts, name="sum_small")
    r0, nr = offs["dmod"]
    dmod_all = parts[:, r0:r0 + nr].reshape(N_DEV, 2, N_CHIPS * nmod)
    dmod_shard = lax.dynamic_slice_in_dim(dmod_all, chip * nmod, nmod, axis=2).transpose(1, 0, 2)
    g_w_mod = _mod_bwd(c16, jnp.pad(dmod_shard, ((0, 0), (0, 16 - N_DEV), (0, 0))), name="mod_bwd")

    loss = _unpack_small(total, offs, "loss", loss_part[0, :1]).reshape(())
    grads = {"w_mod": g_w_mod, "b_mod": _unpack_small(total, offs, "dmod", b_mod)}
    for n in SMALL_NAMES:
        grads[n] = _unpack_small(total, offs, n, small[n])

    given = dict(w_mod=(w_mod, m_w_mod, v_w_mod), b_mod=(b_mod, m_b_mod, v_b_mod), norm1_w=(norm1_w, m_norm1_w, v_norm1_w),
                 norm2_w=(norm2_w, m_norm2_w, v_norm2_w), hg_w_in=(hg_w_in, m_hg_w_in, v_hg_w_in),
                 hg_w_out=(hg_w_out, m_hg_w_out, v_hg_w_out), hg_lb=(hg_lb, m_hg_lb, v_hg_lb),
                 hg_gn_w=(hg_gn_w, m_hg_gn_w, v_hg_gn_w), fox_w_in=(fox_w_in, m_fox_w_in, v_fox_w_in),
                 fox_b_f=(fox_b_f, m_fox_b_f, v_fox_b_f), fox_qn_w=(fox_qn_w, m_fox_qn_w, v_fox_qn_w),
                 fox_kn_w=(fox_kn_w, m_fox_kn_w, v_fox_kn_w), fox_w_out=(fox_w_out, m_fox_w_out, v_fox_w_out),
                 mlp_w1=(mlp_w1, m_mlp_w1, v_mlp_w1), mlp_w2=(mlp_w2, m_mlp_w2, v_mlp_w2), final_w=(final_w, m_final_w, v_final_w))
    upd = {}

    for n, (h, off, rows) in place.items():
        w, m, v = given[n]
        flat = lambda a: a.reshape(rows, D)
        d, mn, vn = _adamw(flat(w), gshard, flat(m), flat(v), g_at=(h, off), name=f"adamw_{n}")
        grads[n] = gshard[h, off:off + rows].reshape(w.shape)
        upd[n] = tuple(a.reshape(w.shape) for a in (d, mn, vn))

    w, m, v = given["hg_w_in"]
    grads["hg_w_in"] = g_hg_w_in.reshape(w.shape)
    upd["hg_w_in"] = tuple(a.reshape(w.shape) for a in _adamw(w[0], g_hg_w_in, m[0], v[0], name="adamw_hg_w_in"))

    w, m, v = given["w_mod"]
    flat = lambda a: a.reshape(-1, nmod)
    upd["w_mod"] = tuple(a.reshape(w.shape) for a in _adamw(flat(w), flat(g_w_mod), flat(m), flat(v), name="adamw_w_mod"))

    snames = ["b_mod"] + SMALL_NAMES
    pw, soffs = _pack_small({n: given[n][0] for n in snames}, snames)
    pm, _ = _pack_small({n: given[n][1] for n in snames}, snames)
    pv, _ = _pack_small({n: given[n][2] for n in snames}, snames)
    pg, _ = _pack_small({n: grads[n] for n in snames}, snames)
    pw, pm, pv, pg = (_pad_rows(a, 8) for a in (pw, pm, pv, pg))
    sd, smn, svn = _adamw(pw, pg, pm, pv, name="adamw_small")
    for n in snames:
        like = given[n][0]
        upd[n] = tuple(_unpack_small(a, soffs, n, like) for a in (sd, smn, svn))

    order = ["w_mod", "b_mod", "norm1_w", "norm2_w", "hg_w_in", "hg_w_out", "hg_lb", "hg_gn_w", "fox_w_in", "fox_b_f",
             "fox_qn_w", "fox_kn_w", "fox_w_out", "mlp_w1", "mlp_w2", "final_w"]
    return (loss, grad_x.reshape(x.shape), *[grads[n] for n in order], *[upd[n][0] for n in order],
            *[upd[n][1] for n in order], *[upd[n][2] for n in order])
```

```python
import math

import jax
import jax.numpy as jnp
from jax import lax
from jax.experimental import pallas as pl
from jax.experimental.pallas import tpu as pltpu

EPS = 1e-6
ADAM_LR, ADAM_B1, ADAM_B2, ADAM_EPS, ADAM_WD, ADAM_STEP = 0.001, 0.9, 0.999, 1e-08, 0.01, 10

F32 = jnp.float32
BF = jnp.bfloat16
LANES = 128
HG_CHUNK = 64
HG_HEADS_PER_STEP = 8
HG_TOKENS_PER_STEP = 256
FOX_ROWS_PER_STEP = 2048
FOX_BWD_TILES = (8, 4, 2, 1)
LOG2E = 1.4426950408889634
FOX_DH = 64
N_CHIPS = 4
N_DEV = 8
VMEM_LIMIT = 56 * 1024 * 1024
MESH = pl.DeviceIdType.MESH

NT = (((1,), (1,)), ((), ()))
TN = (((0,), (0,)), ((), ()))


def _pick(n, pref, mult=LANES):
    if n <= pref:
        return n
    t = (pref // mult) * mult
    while t >= mult:
        if n % t == 0:
            return t
        t -= mult
    raise ValueError((n, pref, mult))


def _cp(*sem):
    return pltpu.CompilerParams(dimension_semantics=sem, vmem_limit_bytes=VMEM_LIMIT)


def _dot(a, b):
    return jnp.dot(a, b, preferred_element_type=F32)


def _dg(a, b, dims):
    return lax.dot_general(a, b, dims, preferred_element_type=F32)


def _split3(x):
    hi = x.astype(BF)
    r1 = x - hi.astype(F32)
    mid = r1.astype(BF)
    lo = (r1 - mid.astype(F32)).astype(BF)
    return hi, mid, lo


def _tri_dot(tri, x):
    hi, mid, lo = _split3(x)
    return _dot(tri, hi) + _dot(tri, mid) + _dot(tri, lo)


def _dg3(a, b, dims):
    ah, bh = a.astype(BF), b.astype(BF)
    al, bl = (a - ah.astype(F32)).astype(BF), (b - bh.astype(F32)).astype(BF)
    return _dg(ah, bh, dims) + _dg(ah, bl, dims) + _dg(al, bh, dims)


def _dg1(a, b, dims):
    return _dg(a.astype(BF), b.astype(BF), dims)


NN = (((1,), (0,)), ((), ()))


def _sigmoid(x):
    return jax.nn.sigmoid(x)


def _ln_matmul(x, nw, sc, sh, w, slab=None, *, relu2, name):
    S, D = x.shape
    N = w.shape[1]
    tm, tn = _pick(S, 512, 16), N
    fused = slab is not None

    def body(x_ref, nw_ref, sc_ref, sh_ref, w_ref, *rest):
        if fused:
            s_ref, *outs, out_ref, hs, send_sems, recv_sems = rest
            finish = _gather_behind(s_ref, out_ref, send_sems, recv_sems, pl.program_id(0), S // tm)
        else:
            outs, hs = rest[:-1], rest[-1]
        h_ref = outs[-1]

        @pl.when(pl.program_id(1) == 0)
        def _():
            xv = x_ref[...]
            r = lax.rsqrt(jnp.mean(xv * xv, axis=-1, keepdims=True) + EPS)
            hb = ((xv * r * nw_ref[...]) * (1.0 + sc_ref[...]) + sh_ref[...]).astype(BF)
            hs[...] = hb
            h_ref[...] = hb

        z = _dot(hs[...], w_ref[...])
        if relu2:
            a = jnp.maximum(z, 0.0)
            outs[0][...] = a.astype(BF)
            outs[1][...] = (a * a).astype(BF)
        else:
            outs[0][...] = z
        if fused:
            finish()

    vec = pl.BlockSpec((1, D), lambda i, j: (0, 0))
    tile = pl.BlockSpec((tm, tn), lambda i, j: (i, j))
    if relu2:
        out_shape = [jax.ShapeDtypeStruct((S, N), BF), jax.ShapeDtypeStruct((S, N), BF)]
        out_specs = [tile, tile]
    else:
        out_shape = [jax.ShapeDtypeStruct((S, N), F32)]
        out_specs = [tile]
    out_shape.append(jax.ShapeDtypeStruct((S, D), BF))
    out_specs.append(pl.BlockSpec((tm, D), lambda i, j: (i, 0)))
    in_specs = [pl.BlockSpec((tm, D), lambda i, j: (i, 0)), vec, vec, vec, pl.BlockSpec((D, tn), lambda i, j: (0, j))]
    scratch = [pltpu.VMEM((tm, D), BF)]
    args = [x, nw, sc, sh, w]
    if fused:
        in_specs.append(HBM)
        out_specs.append(HBM)
        out_shape.append(jax.ShapeDtypeStruct((N_CHIPS,) + slab.shape, slab.dtype))
        scratch += [pltpu.SemaphoreType.DMA((6,)), pltpu.SemaphoreType.DMA((6,))]
        args.append(slab)
    return pl.pallas_call(
        body, name=name, grid=(S // tm, N // tn), in_specs=in_specs, out_specs=out_specs, out_shape=out_shape,
        scratch_shapes=scratch, compiler_params=_cp("arbitrary", "arbitrary"),
    )(*args)


def _matmul_resid(a, w, x, gate, *, name):
    S, K = a.shape
    D = w.shape[1]
    tm, tn = _pick(S, 1024 if K <= 1024 else 512, 16), D

    def body(a_ref, w_ref, x_ref, g_ref, o_ref, y_ref):
        y = _dot(a_ref[...], w_ref[...])
        y_ref[...] = y.astype(BF)
        o_ref[...] = x_ref[...] + g_ref[...] * y

    tile = pl.BlockSpec((tm, tn), lambda i, j: (i, j))
    return pl.pallas_call(
        body, name=name, grid=(S // tm, D // tn),
        in_specs=[pl.BlockSpec((tm, K), lambda i, j: (i, 0)), pl.BlockSpec((K, tn), lambda i, j: (0, j)),
                  tile, pl.BlockSpec((1, tn), lambda i, j: (0, j))],
        out_specs=[tile, tile],
        out_shape=[jax.ShapeDtypeStruct((S, D), F32), jax.ShapeDtypeStruct((S, D), BF)],
        compiler_params=_cp("parallel", "arbitrary"),
    )(a, w, x, gate)


def _gate_matmul_nt(dx, gate, y, w, act, *, name):
    S, D = dx.shape
    K = w.shape[0]
    tm, tn = _pick(S, 1024 if K <= 1024 else 512, 16), K
    fused = act is not None

    def body(dx_ref, g_ref, y_ref, w_ref, *rest):
        if fused:
            act_ref, da_ref, dm_ref, dg_ref, ms = rest
        else:
            da_ref, dm_ref, dg_ref, ms = rest
        i, j = pl.program_id(0), pl.program_id(1)

        @pl.when((i == 0) & (j == 0))
        def _():
            dg_ref[...] = jnp.zeros_like(dg_ref)

        @pl.when(j == 0)
        def _():
            dxv = dx_ref[...]
            dmb = (dxv * g_ref[...]).astype(BF)
            ms[...] = dmb
            dm_ref[...] = dmb
            dg_ref[...] += jnp.sum(dxv * y_ref[...].astype(F32), axis=0, keepdims=True)

        da = _dg(ms[...], w_ref[...], NT)
        if fused:
            da_ref[...] = (da * (2.0 * act_ref[...].astype(F32))).astype(BF)
        else:
            da_ref[...] = da

    row = pl.BlockSpec((tm, D), lambda i, j: (i, 0))
    vec = pl.BlockSpec((1, D), lambda i, j: (0, 0))
    tile = pl.BlockSpec((tm, tn), lambda i, j: (i, j))
    in_specs = [row, vec, row, pl.BlockSpec((tn, D), lambda i, j: (j, 0))]
    args = [dx, gate, y, w]
    if fused:
        in_specs.append(tile)
        args.append(act)
    return pl.pallas_call(
        body, name=name, grid=(S // tm, K // tn),
        in_specs=in_specs, out_specs=[tile, row, vec],
        out_shape=[jax.ShapeDtypeStruct((S, K), BF if fused else F32), jax.ShapeDtypeStruct((S, D), BF),
                   jax.ShapeDtypeStruct((1, D), F32)],
        scratch_shapes=[pltpu.VMEM((tm, D), BF)],
        compiler_params=_cp("arbitrary", "arbitrary"),
    )(*args)


def _matmul_tn(a, b, *, name, into=None):
    S, Ka = a.shape
    P, _, Db = b.shape
    tk, tn, ts = _pick(Ka, 1024), _pick(Db, 1024), _pick(S, 1024, 16)
    if into is not None:
        slab, kind, half, off = into
        C = tn = slab.shape[3]
        per_chip = Ka // N_CHIPS
        all_chips = kind == "row" and tk == Ka
        if kind == "row" and not all_chips:
            tk = min(tk, per_chip)
        assert tn == C and P * Db == (N_CHIPS * C if kind == "col" else C)
        if kind == "col":
            assert tk == Ka and off % tk == 0
        elif all_chips:
            assert off % per_chip == 0
        else:
            assert per_chip % tk == 0 and off % tk == 0
    npb = Db // tn

    def body(a_ref, b_ref, *rest):
        o_ref, acc = rest[-2:]
        s = pl.program_id(2)

        @pl.when(s == 0)
        def _():
            acc[...] = jnp.zeros_like(acc)

        acc[...] += _dg(a_ref[...], b_ref[...], TN)

        @pl.when(s == pl.num_programs(2) - 1)
        def _():
            o_ref[...] = acc[...].reshape(o_ref.shape)

    in_specs = [pl.BlockSpec((ts, tk), lambda i, j, s: (s, i)),
                pl.BlockSpec((None, ts, tn), lambda i, j, s: (j // npb, s, j % npb))]
    args = [a, b]
    if into is None:
        out_spec = pl.BlockSpec((tk, tn), lambda i, j, s: (i, j))
        out_shape = jax.ShapeDtypeStruct((Ka, P * Db), F32)
        aliases = {}
    else:
        per = per_chip // tk if kind == "row" and not all_chips else 1
        if kind == "col":
            out_spec = pl.BlockSpec((None, None, tk, tn), lambda i, j, s: (j, half, off // tk + i, 0))
        elif all_chips:
            out_spec = pl.BlockSpec((N_CHIPS, None, per_chip, tn), lambda i, j, s: (0, half, off // per_chip, 0))
        else:
            out_spec = pl.BlockSpec((None, None, tk, tn), lambda i, j, s: (i // per, half, off // tk + i % per, 0))
        out_shape = jax.ShapeDtypeStruct(slab.shape, F32)
        in_specs.append(pl.BlockSpec(memory_space=pl.ANY))
        args.append(slab)
        aliases = {2: 0}
    return pl.pallas_call(
        body, name=name, grid=(Ka // tk, P * npb, S // ts),
        in_specs=in_specs, out_specs=out_spec, out_shape=out_shape,
        scratch_shapes=[pltpu.VMEM((tk, tn), F32)], input_output_aliases=aliases,
        compiler_params=_cp("parallel", "parallel", "arbitrary"),
    )(*args)


def _rows_into_slab(slab, src, half, off, *, name):
    rows, C = src.shape[1:]
    tr = min(math.gcd(off, 256) if off else 256, -(-rows // 8) * 8)
    assert off % tr == 0

    def body(s_ref, _, o_ref):
        o_ref[...] = s_ref[...]

    return pl.pallas_call(
        body, name=name, grid=(N_CHIPS, pl.cdiv(rows, tr)),
        in_specs=[pl.BlockSpec((None, tr, C), lambda j, r: (j, r, 0)), pl.BlockSpec(memory_space=pl.ANY)],
        out_specs=pl.BlockSpec((None, None, tr, C), lambda j, r: (j, half, off // tr + r, 0)),
        out_shape=jax.ShapeDtypeStruct(slab.shape, F32), input_output_aliases={1: 0},
        compiler_params=_cp("parallel", "arbitrary"),
    )(src, slab)


def _matmul_nt_lnbwd(g, w, x, nw, sc, dx_out, part=None, *, name):
    P, S, Dg = g.shape
    D = x.shape[1]
    tm = _pick(S, 512, 16)
    fused = part is not None

    def body(g_ref, w_ref, x_ref, nw_ref, sc_ref, dxo_ref, *rest):
        if fused:
            p_ref, dx_ref, dsc_ref, dsh_ref, dnw_ref, recv_ref, send_sems, recv_sems = rest
            copies = _scatter_copies(p_ref, recv_ref, send_sems, recv_sems)
        else:
            dx_ref, dsc_ref, dsh_ref, dnw_ref = rest

        @pl.when(pl.program_id(0) == 0)
        def _():
            dsc_ref[...] = jnp.zeros_like(dsc_ref)
            dsh_ref[...] = jnp.zeros_like(dsh_ref)
            dnw_ref[...] = jnp.zeros_like(dnw_ref)
            if fused:
                for cp in copies:
                    cp.start()

        dh = _dg(g_ref[0], w_ref[:, 0:Dg], NT)
        for p in range(1, P):
            dh = dh + _dg(g_ref[p], w_ref[:, p * Dg:(p + 1) * Dg], NT)
        xv = x_ref[...]
        nwv = nw_ref[...]
        r = lax.rsqrt(jnp.mean(xv * xv, axis=-1, keepdims=True) + EPS)
        xr = xv * r
        dn = dh * (1.0 + sc_ref[...])
        dsc_ref[...] += jnp.sum(dh * (xr * nwv), axis=0, keepdims=True)
        dsh_ref[...] += jnp.sum(dh, axis=0, keepdims=True)
        dnw_ref[...] += jnp.sum(dn * xr, axis=0, keepdims=True)
        u = dn * nwv
        dx_ref[...] = dxo_ref[...] + r * (u - xr * jnp.mean(u * xr, axis=-1, keepdims=True))

        if fused:
            @pl.when(pl.program_id(0) == S // tm - 1)
            def _():
                for cp in copies:
                    cp.wait()

    row = pl.BlockSpec((tm, D), lambda i: (i, 0))
    vec = pl.BlockSpec((1, D), lambda i: (0, 0))
    in_specs = [pl.BlockSpec((P, tm, Dg), lambda i: (0, i, 0)), pl.BlockSpec((D, P * Dg), lambda i: (0, 0)), row, vec, vec, row]
    out_specs = [row, vec, vec, vec]
    out_shape = [jax.ShapeDtypeStruct((S, D), F32)] + [jax.ShapeDtypeStruct((1, D), F32)] * 3
    scratch, args = [], [g, w, x, nw, sc, dx_out]
    if fused:
        in_specs.append(HBM)
        out_specs.append(HBM)
        out_shape.append(jax.ShapeDtypeStruct((3,) + part.shape[1:], part.dtype))
        scratch = [pltpu.SemaphoreType.DMA((3,)), pltpu.SemaphoreType.DMA((3,))]
        args.append(part)
    return pl.pallas_call(
        body, name=name, grid=(S // tm,), in_specs=in_specs, out_specs=out_specs, out_shape=out_shape,
        scratch_shapes=scratch, compiler_params=_cp("arbitrary"),
    )(*args)


def _loss_kernel(x, fw, tgt, *, name):
    S, D = x.shape
    tm = _pick(S, 512, 8)

    def body(x_ref, fw_ref, t_ref, l_ref, dx_ref, dfw_ref):
        @pl.when(pl.program_id(0) == 0)
        def _():
            l_ref[...] = jnp.zeros_like(l_ref)
            dfw_ref[...] = jnp.zeros_like(dfw_ref)

        xv = x_ref[...]
        fwv = fw_ref[...]
        r = lax.rsqrt(jnp.mean(xv * xv, axis=-1, keepdims=True) + EPS)
        xr = xv * r
        err = xr * fwv - t_ref[...]
        per_tok = jnp.mean(err * err, axis=-1, keepdims=True)
        l_ref[...] += 0.5 * jnp.sum(per_tok, axis=0, keepdims=True)
        dy = err * (1.0 / D)
        dfw_ref[...] += jnp.sum(dy * xr, axis=0, keepdims=True)
        u = dy * fwv
        dx_ref[...] = r * (u - xr * jnp.mean(u * xr, axis=-1, keepdims=True))

    row = pl.BlockSpec((tm, D), lambda i: (i, 0))
    vec = pl.BlockSpec((1, D), lambda i: (0, 0))
    return pl.pallas_call(
        body, name=name, grid=(S // tm,),
        in_specs=[row, vec, row],
        out_specs=[pl.BlockSpec((1, LANES), lambda i: (0, 0)), row, vec],
        out_shape=[jax.ShapeDtypeStruct((1, LANES), F32), jax.ShapeDtypeStruct((S, D), F32),
                   jax.ShapeDtypeStruct((1, D), F32)],
        compiler_params=_cp("arbitrary"),
    )(x, fw, tgt)


def _hg_lower_bound(lb3):
    mx = jnp.max(lb3, axis=0, keepdims=True)
    e = jnp.exp(lb3 - mx)
    p = e / jnp.sum(e, axis=0, keepdims=True)
    return p[0:1, :], p


def _hg_chunk_common(qr, fz, lbv):
    sq = _sigmoid(qr)
    q = qr * sq
    sig = _sigmoid(fz)
    f = lbv + (1.0 - lbv) * sig
    k = (1.0 - lbv) * (1.0 - sig)
    return q, sq, sig, f, k, jnp.log(f)


def _row_of(x, rows, r):
    return jnp.sum(jnp.where(rows == r, x, 0.0), axis=0, keepdims=True)


def _hg_fwd(proj, hg_lb, gn, slab=None, *, name):
    S = proj.shape[0]
    D = proj.shape[1] // 4
    H = D // LANES
    HB = min(HG_HEADS_PER_STEP, H)
    W = HB * LANES
    C = HG_CHUNK
    T = _pick(S, HG_TOKENS_PER_STEP, C)
    nch, nb = T // C, S // T
    ng = H // HB
    fused = slab is not None

    def body(q_ref, fz_ref, v_ref, g_ref, lb_ref, gn_ref, *rest):
        if fused:
            s_ref, y_ref, o_ref, sts_ref, out_ref, st, send_sems, recv_sems = rest
            finish = _gather_behind(s_ref, out_ref, send_sems, recv_sems,
                                    pl.program_id(0) * nb + pl.program_id(1), ng * nb)
        else:
            y_ref, o_ref, sts_ref, st = rest

        @pl.when(pl.program_id(1) == 0)
        def _():
            st[...] = jnp.zeros_like(st)

        lb_all, _ = _hg_lower_bound(lb_ref[...])
        gnv = gn_ref[...]
        ri = lax.broadcasted_iota(jnp.int32, (C, C), 0)
        ci_ = lax.broadcasted_iota(jnp.int32, (C, C), 1)
        low = ri >= ci_
        tri = jnp.where(low, 1.0, 0.0).astype(BF)
        rows_w = lax.broadcasted_iota(jnp.int32, (C, W), 0)

        def chunk(ci, carry):
            sl = pl.ds(pl.multiple_of(ci * C, C), C)
            heads = [slice(hh * LANES, (hh + 1) * LANES) for hh in range(HB)]
            q, _, _, _, k, logf = _hg_chunk_common(q_ref[sl, :], fz_ref[sl, :], lb_all)
            vv, gg = v_ref[sl, :], g_ref[sl, :]
            G = _tri_dot(tri, logf)
            Gm = _row_of(G, rows_w, C // 2 - 1)
            Gl = _row_of(G, rows_w, C - 1)
            qt, kt = q * jnp.exp(G - Gm), k * jnp.exp(Gm - G)
            qe, kd, eGl = q * jnp.exp(G), k * jnp.exp(Gl - G), jnp.exp(Gl)
            A = [jnp.where(low, _dg1(qt[:, ls], kt[:, ls], NT), 0.0) for ls in heads]
            Sv = [st[hh] for hh in range(HB)]
            for hh in range(HB):
                sts_ref[hh, ci] = Sv[hh]
            o = [_dg1(A[hh], vv[:, ls], NN) + _dg1(qe[:, ls], Sv[hh], NT) for hh, ls in enumerate(heads)]
            for hh, ls in enumerate(heads):
                st[hh] = Sv[hh] * eGl[:, ls] + _dg1(vv[:, ls], kd[:, ls], TN)
            gate = gg * _sigmoid(gg)
            for hh, ls in enumerate(heads):
                r = lax.rsqrt(jnp.mean(o[hh] * o[hh], axis=-1, keepdims=True) + EPS)
                y_ref[sl, ls] = ((o[hh] * r * gnv) * gate[:, ls]).astype(BF)
                o_ref[sl, ls] = o[hh]
            return carry

        lax.fori_loop(0, nch, chunk, 0)

        if fused:
            finish()

    def part(p):
        return pl.BlockSpec((T, W), lambda h, n: (n, p * ng + h))

    blk = pl.BlockSpec((T, W), lambda h, n: (n, h))
    in_specs = [part(0), part(1), part(2), part(3),
                pl.BlockSpec((3, W), lambda h, n: (0, h)), pl.BlockSpec((1, LANES), lambda h, n: (0, 0))]
    out_specs = [blk, blk, pl.BlockSpec((HB, nch, LANES, LANES), lambda h, n: (h, n, 0, 0))]
    out_shape = [jax.ShapeDtypeStruct((S, D), BF), jax.ShapeDtypeStruct((S, D), F32),
                 jax.ShapeDtypeStruct((H, S // C, LANES, LANES), F32)]
    scratch = [pltpu.VMEM((HB, LANES, LANES), F32)]
    args = [proj, proj, proj, proj, hg_lb, gn]
    if fused:
        in_specs.append(HBM)
        out_specs.append(HBM)
        out_shape.append(jax.ShapeDtypeStruct((N_CHIPS,) + slab.shape, slab.dtype))
        scratch += [pltpu.SemaphoreType.DMA((6,)), pltpu.SemaphoreType.DMA((6,))]
        args.append(slab)
    return pl.pallas_call(
        body, name=name, grid=(ng, nb), in_specs=in_specs, out_specs=out_specs, out_shape=out_shape,
        scratch_shapes=scratch, compiler_params=_cp("arbitrary", "arbitrary"),
    )(*args)


def _hg_bwd(proj, hg_lb, gn, o_all, states, dy, part=None, *, name):
    S = proj.shape[0]
    D = proj.shape[1] // 4
    H = D // LANES
    HB = min(HG_HEADS_PER_STEP, H)
    W = HB * LANES
    C = HG_CHUNK
    T = _pick(S, HG_TOKENS_PER_STEP, C)
    nch, nb = T // C, S // T
    ng = H // HB
    fused = part is not None

    def body(q_ref, fz_ref, v_ref, g_ref, lb_ref, gn_ref, o_ref, sts_ref, dy_ref, *rest):
        if fused:
            p_ref, dp_ref, dlb_ref, dgn_ref, recv_ref, dst, dlb_acc, send_sems, recv_sems = rest
            copies = _scatter_copies(p_ref, recv_ref, send_sems, recv_sems)

            @pl.when((pl.program_id(0) == 0) & (pl.program_id(1) == 0))
            def _():
                for cp in copies:
                    cp.start()
        else:
            dp_ref, dlb_ref, dgn_ref, dst, dlb_acc = rest
        n = pl.program_id(1)

        @pl.when(n == 0)
        def _():
            dst[...] = jnp.zeros_like(dst)
            dlb_acc[...] = jnp.zeros_like(dlb_acc)
            dgn_ref[...] = jnp.zeros_like(dgn_ref)

        lb_all, p3 = _hg_lower_bound(lb_ref[...])
        gnv = gn_ref[...]
        ri = lax.broadcasted_iota(jnp.int32, (C, C), 0)
        ci_ = lax.broadcasted_iota(jnp.int32, (C, C), 1)
        low = ri >= ci_
        tri = jnp.where(low, 1.0, 0.0).astype(BF)
        triu = jnp.where(ri <= ci_, 1.0, 0.0).astype(BF)
        rows_w = lax.broadcasted_iota(jnp.int32, (C, W), 0)
        gnw = jnp.tile(gnv, (1, HB))

        def chunk(cj, carry):
            ci = nch - 1 - cj
            sl = pl.ds(pl.multiple_of(ci * C, C), C)
            heads = list(enumerate(slice(hh * LANES, (hh + 1) * LANES) for hh in range(HB)))
            wide = lambda parts: jnp.concatenate(parts, axis=1)
            qr, vv, gg = q_ref[sl, :], v_ref[sl, :], g_ref[sl, :]
            q, sq, sig, f, k, logf = _hg_chunk_common(qr, fz_ref[sl, :], lb_all)
            G = _tri_dot(tri, logf)
            Gm = _row_of(G, rows_w, C // 2 - 1)
            Gl = _row_of(G, rows_w, C - 1)
            eG, e_qm, e_km, e_lk, eGl = jnp.exp(G), jnp.exp(G - Gm), jnp.exp(Gm - G), jnp.exp(Gl - G), jnp.exp(Gl)
            qt, kt, kdec, qe = q * e_qm, k * e_km, k * e_lk, q * eG
            sg = _sigmoid(gg)
            d_onw = dy_ref[sl, :] * (gg * sg)
            u = d_onw * gnw
            o = o_ref[sl, :]
            on, do = [], []
            for hh, ls in heads:
                r = lax.rsqrt(jnp.mean(o[:, ls] * o[:, ls], axis=-1, keepdims=True) + EPS)
                on.append(o[:, ls] * r)
                dgn_ref[hh] += jnp.sum(d_onw[:, ls] * on[hh], axis=0, keepdims=True)
                do.append(r * (u[:, ls] - on[hh] * jnp.mean(u[:, ls] * on[hh], axis=-1, keepdims=True)))
            dgg = dy_ref[sl, :] * (wide(on) * gnw) * (sg * (1.0 + gg * (1.0 - sg)))
            Sv = [sts_ref[hh, ci] for hh, _ in heads]
            dSv = [dst[hh] for hh, _ in heads]
            A = [jnp.where(low, _dg1(qt[:, ls], kt[:, ls], NT), 0.0) for _, ls in heads]
            dA = [jnp.where(low, _dg3(do[hh], vv[:, ls], NT), 0.0) for hh, ls in heads]
            dv = wide([_dg1(A[hh], do[hh], TN) + _dg1(kdec[:, ls], dSv[hh], NT) for hh, ls in heads])
            dq = wide([_dg3(dA[hh], kt[:, ls], NN) for hh, ls in heads]) * e_qm \
                + eG * wide([_dg3(do[hh], Sv[hh], NN) for hh, _ in heads])
            dk = wide([_dg3(dA[hh], qt[:, ls], TN) for hh, ls in heads]) * e_km \
                + e_lk * wide([_dg3(vv[:, ls], dSv[hh], NN) for hh, ls in heads])
            s_end = [Sv[hh] * eGl[:, ls] + _dg3(vv[:, ls], kdec[:, ls], TN) for hh, ls in heads]
            dgl = wide([jnp.sum(dSv[hh] * s_end[hh], axis=0, keepdims=True) for hh, _ in heads])
            for hh, ls in heads:
                dst[hh] = dSv[hh] * eGl[:, ls] + _dg1(do[hh], qe[:, ls], TN)
            dG = q * dq - k * dk + jnp.where(rows_w == C - 1, dgl, 0.0)
            dlogf = _tri_dot(triu, dG) - f * dk
            dlf_f = dlogf / f
            dlb_acc[...] += jnp.sum(dlf_f * (1.0 - sig), axis=0, keepdims=True)
            dp_ref[0, sl, :] = (dq * (sq * (1.0 + qr * (1.0 - sq)))).astype(BF)
            dp_ref[1, sl, :] = (dlf_f * (1.0 - lb_all) * sig * (1.0 - sig)).astype(BF)
            dp_ref[2, sl, :] = dv.astype(BF)
            dp_ref[3, sl, :] = dgg.astype(BF)
            return carry

        lax.fori_loop(0, nch, chunk, 0)
        sel = jnp.where(lax.broadcasted_iota(jnp.int32, (3, W), 0) == 0, 1.0, 0.0)
        dlb_ref[...] = lb_all * (sel - p3) * dlb_acc[...]

        if fused:
            @pl.when((pl.program_id(0) == ng - 1) & (n == nb - 1))
            def _():
                for cp in copies:
                    cp.wait()

    def col(p):
        return pl.BlockSpec((T, W), lambda h, n: (nb - 1 - n, p * ng + h))

    blk = pl.BlockSpec((T, W), lambda h, n: (nb - 1 - n, h))
    in_specs = [col(0), col(1), col(2), col(3),
                pl.BlockSpec((3, W), lambda h, n: (0, h)), pl.BlockSpec((1, LANES), lambda h, n: (0, 0)),
                blk, pl.BlockSpec((HB, nch, LANES, LANES), lambda h, n: (h, nb - 1 - n, 0, 0)), blk]
    out_specs = [pl.BlockSpec((4, T, W), lambda h, n: (0, nb - 1 - n, h)),
                 pl.BlockSpec((3, W), lambda h, n: (0, h)),
                 pl.BlockSpec((HB, 1, LANES), lambda h, n: (h, 0, 0))]
    out_shape = [jax.ShapeDtypeStruct((4, S, D), BF), jax.ShapeDtypeStruct((3, D), F32),
                 jax.ShapeDtypeStruct((H, 1, LANES), F32)]
    scratch = [pltpu.VMEM((HB, LANES, LANES), F32), pltpu.VMEM((1, W), F32)]
    args = [proj, proj, proj, proj, hg_lb, gn, o_all, states, dy]
    if fused:
        in_specs.append(HBM)
        out_specs.append(HBM)
        out_shape.append(jax.ShapeDtypeStruct((3,) + part.shape[1:], part.dtype))
        scratch += [pltpu.SemaphoreType.DMA((3,)), pltpu.SemaphoreType.DMA((3,))]
        args.append(part)
    return pl.pallas_call(
        body, name=name, grid=(ng, nb), in_specs=in_specs, out_specs=out_specs, out_shape=out_shape,
        scratch_shapes=scratch, compiler_params=_cp("arbitrary", "arbitrary"),
    )(*args)


def _log_sigmoid(u):
    return jnp.minimum(u, 0.0) - jnp.log(1.0 + jnp.exp(-jnp.abs(u)))


def _lane_put(base, lane, first, pieces):
    for n, p in enumerate(pieces):
        base = jnp.where(lane == first + n, p, base)
    return base


def _fox_cumsum(proj, bf_pad, *, name):
    S = proj.shape[0]
    D = proj.shape[1] // 5
    T = _pick(S, 256, 8)

    def body(fz_ref, b_ref, f_ref, carry):
        @pl.when(pl.program_id(0) == 0)
        def _():
            carry[...] = jnp.zeros_like(carry)

        logf = _log_sigmoid(fz_ref[...] + b_ref[...])
        tri = jnp.where(lax.broadcasted_iota(jnp.int32, (T, T), 0) >= lax.broadcasted_iota(jnp.int32, (T, T), 1),
                        1.0, 0.0).astype(BF)
        fv = _tri_dot(tri, logf) + carry[...]
        f_ref[...] = fv
        carry[...] = _row_of(fv, lax.broadcasted_iota(jnp.int32, (T, LANES), 0), T - 1)

    return pl.pallas_call(
        body, name=name, grid=(S // T,),
        in_specs=[pl.BlockSpec((T, LANES), lambda i: (i, 4 * D // LANES)), pl.BlockSpec((1, LANES), lambda i: (0, 0))],
        out_specs=pl.BlockSpec((T, LANES), lambda i: (i, 0)),
        out_shape=jax.ShapeDtypeStruct((S, LANES), F32),
        scratch_shapes=[pltpu.VMEM((1, LANES), F32)],
        compiler_params=_cp("arbitrary"),
    )(proj, bf_pad)


def _pair_stats(sq, lo):
    del lo
    a = lax.broadcasted_iota(jnp.int32, (LANES, LANES), 0) < FOX_DH
    b = lax.broadcasted_iota(jnp.int32, (LANES, LANES), 1) < FOX_DH
    avg = jnp.where(a == b, 1.0 / FOX_DH, 0.0).astype(BF)
    hi, mid, low = _split3(sq)
    return _dot(hi, avg) + _dot(mid, avg) + _dot(low, avg)


def _fox_prep(proj, fcum, qw2, kw2, *, name):
    S = proj.shape[0]
    D = proj.shape[1] // 5
    HP = D // LANES
    T = _pick(S, FOX_ROWS_PER_STEP, 16)

    def body(q_ref, k_ref, v_ref, f_ref, qw_ref, kw_ref, qa_ref, ka_ref, va_ref, vt_ref):
        hp = pl.program_id(1)
        lane = lax.broadcasted_iota(jnp.int32, (T, LANES), 1)
        lo = lane < FOX_DH
        qv, kv, vv, fv = q_ref[...], k_ref[...], v_ref[...], f_ref[...]
        qn = qv * lax.rsqrt(_pair_stats(qv * qv, lo) + EPS) * qw_ref[...] * (0.125 * LOG2E)
        kn = kv * lax.rsqrt(_pair_stats(kv * kv, lo) + EPS) * kw_ref[...]
        ones_q = jnp.where((lane >= 67) & (lane <= 69), 1.0, 0.0)
        ones_k = jnp.where(((lane >= 64) & (lane <= 66)) | ((lane >= 70) & (lane <= 72)), 1.0, 0.0)
        ones_v = jnp.where((lane >= 64) & (lane <= 66), 1.0, 0.0)
        for hh in range(2):
            fh = jnp.sum(jnp.where(lane == 2 * hp + hh, fv, 0.0), axis=-1, keepdims=True) * LOG2E
            pieces = [p.astype(F32) for p in _split3(fh)]

            def half(x):
                return jnp.where(lo, x if hh == 0 else pltpu.roll(x, FOX_DH, 1), 0.0)

            qa_ref[hh] = _lane_put(half(qn) + ones_q, lane, 64, pieces).astype(BF)
            ka_ref[hh] = _lane_put(half(kn) + ones_k, lane, 67, [-p for p in pieces]).astype(BF)
            va = half(vv) + ones_v
            va_ref[hh] = va.astype(BF)
            vt_ref[hh] = va.T.astype(BF)

    def part(p):
        return pl.BlockSpec((T, LANES), lambda i, hp: (i, p * HP + hp))

    vec = pl.BlockSpec((1, LANES), lambda i, hp: (0, 0))
    aug = pl.BlockSpec((2, T, LANES), lambda i, hp: (hp, i, 0))
    return pl.pallas_call(
        body, name=name, grid=(S // T, HP),
        in_specs=[part(0), part(1), part(2), pl.BlockSpec((T, LANES), lambda i, hp: (i, 0)), vec, vec],
        out_specs=[aug, aug, aug, pl.BlockSpec((2, LANES, T), lambda i, hp: (hp, 0, i))],
        out_shape=[jax.ShapeDtypeStruct((2 * HP, S, LANES), BF)] * 3 + [jax.ShapeDtypeStruct((2 * HP, LANES, S), BF)],
        compiler_params=_cp("parallel", "arbitrary"),
    )(proj, proj, proj, fcum, qw2, kw2)


def _fox_block(S):
    return _pick(S, 256, 16)


def _fox_skip_bounds(fcum, qn_w, kn_w, nheads):
    S = fcum.shape[0]
    B = _fox_block(S)
    qk = 8.0 * LOG2E * 1.02 * jnp.max(jnp.abs(qn_w)) * jnp.max(jnp.abs(kn_w))
    thresh = -(2.0 * qk + 152.0)
    f2 = fcum[:, :nheads] * LOG2E
    first, last = f2[0::B], f2[B - 1::B]
    nb = S // B
    blk = jnp.arange(nb)
    dead = (first[0::2, None, :] - last[None, :, :]) < thresh
    jmin = jnp.sum(dead & (blk[None, :, None] < 2 * jnp.arange(nb // 2)[:, None, None]), axis=1)
    live = (first[:, None, :] - last[None, :, :]) >= thresh
    imax = blk[:, None] + jnp.sum(live & (blk[:, None, None] > blk[None, :, None]), axis=0)
    return jmin.T.astype(jnp.int32), imax.T.astype(jnp.int32)


def _fox_fwd(jmin, qa, ka, vat, proj, *, name):
    H, S, _ = qa.shape
    HP = H // 2
    D = HP * LANES
    B = _fox_block(S)
    BQ = 2 * B
    nq = S // BQ

    def body(jmin_ref, q_ref, k_ref, vt_ref, g_ref, y_ref, o_ref, q2_ref):
        hp, i = pl.program_id(0), pl.program_id(1)
        lane = lax.broadcasted_iota(jnp.int32, (BQ, LANES), 1)
        lo = lane < FOX_DH
        in_stat = (lane >= 70) & (lane <= 75)
        causal = lax.broadcasted_iota(jnp.int32, (BQ, BQ), 0) <= lax.broadcasted_iota(jnp.int32, (BQ, BQ), 1)
        row = lax.broadcasted_iota(jnp.int32, (LANES, BQ), 0)
        m0, acc0 = jnp.full((1, BQ), -jnp.inf, F32), jnp.zeros((LANES, BQ), F32)
        outs = []
        for hh in range(2):
            qb = q_ref[hh]

            def block(j, carry, masked=False):
                m, acc = carry
                sl = pl.ds(pl.multiple_of(j * BQ, BQ), BQ)
                st = _dg(k_ref[hh, sl, :], qb, NT)
                if masked:
                    st = jnp.where(causal, st, -jnp.inf)
                m_new = jnp.maximum(m, jnp.ceil(jnp.max(st, axis=0, keepdims=True)))
                p = jnp.exp2(st - m_new).astype(BF)
                return m_new, acc * jnp.exp2(m - m_new) + _dot(vt_ref[hh, :, sl], p)

            carry = lax.fori_loop(jmin_ref[2 * hp + hh, i] // 2, i, block, (m0, acc0))
            m, acc = block(i, carry, masked=True)
            linv = 1.0 / jnp.sum(jnp.where(row == FOX_DH, acc, 0.0), axis=0, keepdims=True)
            tile = acc * linv
            for n, piece in enumerate(_split3(m) + _split3(linv)):
                tile = jnp.where(row == 70 + n, piece.astype(F32), tile)
            tile = tile.T
            outs.append(tile)
            q2_ref[hh] = jnp.where(in_stat, jnp.where(lane <= 72, -tile, tile), qb.astype(F32)).astype(BF)
        o = jnp.where(lo, outs[0], pltpu.roll(outs[1], FOX_DH, 1))
        o_ref[...] = o
        y_ref[...] = (o * _sigmoid(g_ref[...])).astype(BF)

    blk = pl.BlockSpec((BQ, LANES), lambda hp, i, jm: (i, hp))
    qblk = pl.BlockSpec((2, BQ, LANES), lambda hp, i, jm: (hp, i, 0))
    full = pl.BlockSpec((2, S, LANES), lambda hp, i, jm: (hp, 0, 0))
    full_t = pl.BlockSpec((2, LANES, S), lambda hp, i, jm: (hp, 0, 0))
    return pl.pallas_call(
        body, name=name,
        grid_spec=pltpu.PrefetchScalarGridSpec(
            num_scalar_prefetch=1, grid=(HP, nq),
            in_specs=[qblk, full, full_t, pl.BlockSpec((BQ, LANES), lambda hp, i, jm: (i, 3 * HP + hp))],
            out_specs=[blk, blk, qblk]),
        out_shape=[jax.ShapeDtypeStruct((S, D), BF), jax.ShapeDtypeStruct((S, D), F32),
                   jax.ShapeDtypeStruct((H, S, LANES), BF)],
        compiler_params=_cp("parallel", "arbitrary"),
    )(jmin, qa, ka, vat, proj)


def _fox_bwd_prep(dy, o, proj, q2, *, name):
    S, D = dy.shape
    HP = D // LANES
    T = _pick(S, FOX_ROWS_PER_STEP, 16)

    def body(dy_ref, o_ref, g_ref, q2_ref, da_ref):
        lane = lax.broadcasted_iota(jnp.int32, (T, LANES), 1)
        lo = lane < FOX_DH
        in_linv = (lane >= 73) & (lane <= 75)
        linv = [jnp.sum(jnp.where(in_linv, q2_ref[hh].astype(F32), 0.0), axis=-1, keepdims=True) for hh in range(2)]
        u = (dy_ref[...] * _sigmoid(g_ref[...]) * jnp.where(lo, linv[0], linv[1])).astype(BF).astype(F32)
        prod = u * o_ref[...]
        d_lo = jnp.sum(jnp.where(lo, prod, 0.0), axis=-1, keepdims=True)
        d_hi = jnp.sum(jnp.where(lo, 0.0, prod), axis=-1, keepdims=True)
        for hh, delta in enumerate((d_lo, d_hi)):
            base = jnp.where(lo, u if hh == 0 else pltpu.roll(u, FOX_DH, 1), 0.0)
            da_ref[hh] = _lane_put(base, lane, 64, [-(p.astype(F32)) for p in _split3(delta)]).astype(BF)

    blk = pl.BlockSpec((T, LANES), lambda i, hp: (i, hp))
    aug = pl.BlockSpec((2, T, LANES), lambda i, hp: (hp, i, 0))
    return pl.pallas_call(
        body, name=name, grid=(S // T, HP),
        in_specs=[blk, blk, pl.BlockSpec((T, LANES), lambda i, hp: (i, 3 * HP + hp)), aug],
        out_specs=aug,
        out_shape=jax.ShapeDtypeStruct((2 * HP, S, LANES), BF),
        compiler_params=_cp("parallel", "arbitrary"),
    )(dy, o, proj, q2)


def _fox_bwd(imax, q2, ka, va, doa, *, name):
    H, S, _ = q2.shape
    B = _fox_block(S)
    nb = S // B

    def body(imax_ref, q_ref, do_ref, k_ref, v_ref, dq_ref, dk_ref, dv_ref, cs_ref):
        j = pl.program_id(1)
        end = imax_ref[pl.program_id(0), j] + 1

        @pl.when(j == 0)
        def _():
            dq_ref[...] = jnp.zeros_like(dq_ref)

        kb, vb = k_ref[...], v_ref[...]

        def step(i, carry, nblk=1):
            dk_acc, dv_acc, cs_acc = carry
            rows = nblk * B
            sl = pl.ds(pl.multiple_of(i * B, B), rows)
            qb, dob = q_ref[sl, :], do_ref[sl, :]
            s = _dg(qb, kb, NT)
            ahead = lax.broadcasted_iota(jnp.int32, (rows, B), 0) - lax.broadcasted_iota(jnp.int32, (rows, B), 1)
            pb = jnp.exp2(jnp.where(ahead >= (j - i) * B, s, -jnp.inf)).astype(BF)
            ds = pb.astype(F32) * _dg(dob, vb, NT)
            dsb = ds.astype(BF)
            cs_acc = cs_acc + jnp.sum(ds.reshape(rows // 8, 8, B), axis=0)
            dv_acc = dv_acc + _dg(pb, dob, TN)
            dk_acc = dk_acc + _dg(dsb, qb, TN)
            dq_ref[sl, :] += _dot(dsb, kb)
            return dk_acc, dv_acc, cs_acc

        zero = jnp.zeros((B, LANES), F32)
        carry = (zero, zero, jnp.zeros((8, B), F32))
        pos = j
        for U in FOX_BWD_TILES:
            n = (end - pos) // U
            carry = lax.fori_loop(0, n, lambda ii, c, pos=pos, U=U: step(pos + U * ii, c, nblk=U), carry)
            pos = pos + U * n
        dk_acc, dv_acc, cs_acc = carry
        dk_ref[...] = dk_acc
        dv_ref[...] = dv_acc
        cs_ref[...] = jnp.sum(cs_acc, axis=0, keepdims=True)

    full = pl.BlockSpec((None, S, LANES), lambda h, j, im: (h, 0, 0))
    blk = pl.BlockSpec((None, B, LANES), lambda h, j, im: (h, j, 0))
    return pl.pallas_call(
        body, name=name,
        grid_spec=pltpu.PrefetchScalarGridSpec(
            num_scalar_prefetch=1, grid=(H, nb),
            in_specs=[full, full, blk, blk],
            out_specs=[full, blk, blk, pl.BlockSpec((None, 1, B), lambda h, j, im: (h, 0, j))]),
        out_shape=[jax.ShapeDtypeStruct((H, S, LANES), F32)] * 3 + [jax.ShapeDtypeStruct((H, 1, S), F32)],
        compiler_params=_cp("parallel", "arbitrary"),
    )(imax, q2, doa, ka, va)


def _fox_bwd_post(dqa, dka, dva, proj, dy, o, qw2, kw2, *, name):
    S, D = dy.shape
    HP = D // LANES
    T = _pick(S, FOX_ROWS_PER_STEP, 16)

    def body(dq_ref, dk_ref, dv_ref, q_ref, k_ref, g_ref, dy_ref, o_ref, qw_ref, kw_ref, dp_ref, dqw_ref, dkw_ref):
        @pl.when((pl.program_id(0) == 0) & (pl.program_id(1) == 0))
        def _():
            dqw_ref[...] = jnp.zeros_like(dqw_ref)
            dkw_ref[...] = jnp.zeros_like(dkw_ref)

        lane = lax.broadcasted_iota(jnp.int32, (T, LANES), 1)
        lo = lane < FOX_DH

        def pair(ref):
            return jnp.where(lo, ref[0], pltpu.roll(ref[1], FOX_DH, 1))

        def norm_bwd(xv, w, dyn, dw_ref):
            r = lax.rsqrt(_pair_stats(xv * xv, lo) + EPS)
            xr = xv * r
            dw_ref[...] += jnp.sum(dyn * xr, axis=0, keepdims=True)
            u = dyn * w
            return r * (u - xr * _pair_stats(u * xr, lo))

        dp_ref[0] = norm_bwd(q_ref[...], qw_ref[...], pair(dq_ref) * 0.125, dqw_ref).astype(BF)
        dp_ref[1] = norm_bwd(k_ref[...], kw_ref[...], pair(dk_ref) * (1.0 / LOG2E), dkw_ref).astype(BF)
        dp_ref[2] = pair(dv_ref).astype(BF)
        sg = _sigmoid(g_ref[...])
        dp_ref[3] = (dy_ref[...] * o_ref[...] * sg * (1.0 - sg)).astype(BF)

    def part(p):
        return pl.BlockSpec((T, LANES), lambda i, hp: (i, p * HP + hp))

    aug = pl.BlockSpec((2, T, LANES), lambda i, hp: (hp, i, 0))
    blk = pl.BlockSpec((T, LANES), lambda i, hp: (i, hp))
    vec = pl.BlockSpec((1, LANES), lambda i, hp: (0, 0))
    return pl.pallas_call(
        body, name=name, grid=(S // T, HP),
        in_specs=[aug, aug, aug, part(0), part(1), part(3), blk, blk, vec, vec],
        out_specs=[pl.BlockSpec((4, T, LANES), lambda i, hp: (0, i, hp)), vec, vec],
        out_shape=[jax.ShapeDtypeStruct((5, S, D), BF), jax.ShapeDtypeStruct((1, LANES), F32),
                   jax.ShapeDtypeStruct((1, LANES), F32)],
        compiler_params=_cp("arbitrary", "arbitrary"),
    )(dqa, dka, dva, proj, proj, proj, dy, o, qw2, kw2)


def _fox_dfz(colsum, nheads, proj, bf_pad, dproj, *, name):
    S = colsum.shape[0]
    H = nheads
    D = dproj.shape[2]
    T = _pick(S, 256, 16)
    nb = S // T

    def body(cs_ref, fz_ref, b_ref, _, dp_ref, db_ref, carry):
        @pl.when(pl.program_id(0) == 0)
        def _():
            carry[...] = jnp.zeros_like(carry)
            db_ref[...] = jnp.zeros_like(db_ref)

        lane = lax.broadcasted_iota(jnp.int32, (T, LANES), 1)
        df = -cs_ref[...]
        triu = jnp.where(lax.broadcasted_iota(jnp.int32, (T, T), 0) <= lax.broadcasted_iota(jnp.int32, (T, T), 1),
                         1.0, 0.0).astype(BF)
        dlogf = _tri_dot(triu, df) + carry[...]
        carry[...] = _row_of(dlogf, lax.broadcasted_iota(jnp.int32, (T, LANES), 0), 0)
        dfz = jnp.where(lane < H, dlogf * _sigmoid(-(fz_ref[...] + b_ref[...])), 0.0)
        db_ref[...] += jnp.sum(dfz, axis=0, keepdims=True)
        dp_ref[...] = jnp.zeros_like(dp_ref)
        dp_ref[:, 0:LANES] = dfz.astype(BF)

    return pl.pallas_call(
        body, name=name, grid=(nb,),
        in_specs=[pl.BlockSpec((T, LANES), lambda i: (nb - 1 - i, 0)),
                  pl.BlockSpec((T, LANES), lambda i: (nb - 1 - i, 4 * D // LANES)),
                  pl.BlockSpec((1, LANES), lambda i: (0, 0)),
                  pl.BlockSpec(memory_space=pl.ANY)],
        out_specs=[pl.BlockSpec((None, T, D), lambda i: (4, nb - 1 - i, 0)), pl.BlockSpec((1, LANES), lambda i: (0, 0))],
        out_shape=[jax.ShapeDtypeStruct(dproj.shape, BF), jax.ShapeDtypeStruct((1, LANES), F32)],
        scratch_shapes=[pltpu.VMEM((1, LANES), F32)],
        input_output_aliases={3: 0},
        compiler_params=_cp("arbitrary"),
    )(colsum, proj, bf_pad, dproj)


def _mod_fwd(c16, w, b, *, name):
    L, D, N = w.shape
    tn = _pick(N, 512)

    def body(c_ref, w_ref, b_ref, o_ref):
        cv = c_ref[...]
        ca = (cv * _sigmoid(cv)).astype(BF)
        o_ref[...] = _dot(ca, w_ref[...].astype(BF)) + b_ref[...]

    return pl.pallas_call(
        body, name=name, grid=(L, N // tn),
        in_specs=[pl.BlockSpec((16, D), lambda l, j: (0, 0)), pl.BlockSpec((None, D, tn), lambda l, j: (l, 0, j)),
                  pl.BlockSpec((None, 1, tn), lambda l, j: (l, 0, j))],
        out_specs=pl.BlockSpec((None, 16, tn), lambda l, j: (l, 0, j)),
        out_shape=jax.ShapeDtypeStruct((L, 16, N), F32),
        compiler_params=_cp("parallel", "arbitrary"),
    )(c16, w, b)


def _mod_bwd(c16, dmod, *, name):
    L, _, N = dmod.shape
    D = c16.shape[1]
    tn = _pick(N, 512)

    def body(c_ref, d_ref, o_ref):
        cv = c_ref[...]
        ca = (cv * _sigmoid(cv)).astype(BF)
        o_ref[...] = _dg(ca, d_ref[...].astype(BF), TN)

    return pl.pallas_call(
        body, name=name, grid=(L, N // tn),
        in_specs=[pl.BlockSpec((16, D), lambda l, j: (0, 0)), pl.BlockSpec((None, 16, tn), lambda l, j: (l, 0, j))],
        out_specs=pl.BlockSpec((None, D, tn), lambda l, j: (l, 0, j)),
        out_shape=jax.ShapeDtypeStruct((L, D, N), F32),
        compiler_params=_cp("parallel", "arbitrary"),
    )(c16, dmod)


def _adamw_math(w, g, m, v):
    m = ADAM_B1 * m + (1.0 - ADAM_B1) * g
    v = ADAM_B2 * v + (1.0 - ADAM_B2) * (g * g)
    m_hat = m / (1.0 - ADAM_B1 ** ADAM_STEP)
    v_hat = v / (1.0 - ADAM_B2 ** ADAM_STEP)
    return -ADAM_LR * (m_hat / (jnp.sqrt(v_hat) + ADAM_EPS) + ADAM_WD * w), m, v


def _adamw(w, g, m, v, *, g_at=None, name):
    R, C = w.shape
    row0 = 0 if g_at is None else g_at[1]
    tr = min(math.gcd(row0, 256) if row0 else 256, -(-R // 8) * 8)
    g0 = row0 // tr
    if g_at is None:
        g_spec = pl.BlockSpec((tr, C), lambda i: (i, 0))
    else:
        g_spec = pl.BlockSpec((None, tr, C), lambda i: (g_at[0], g0 + i, 0))

    def body(w_ref, g_ref, m_ref, v_ref, d_ref, mo_ref, vo_ref):
        d, mn, vn = _adamw_math(w_ref[...], g_ref[...], m_ref[...], v_ref[...])
        d_ref[...] = d
        mo_ref[...] = mn
        vo_ref[...] = vn

    blk = pl.BlockSpec((tr, C), lambda i: (i, 0))
    return pl.pallas_call(
        body, name=name, grid=(pl.cdiv(R, tr),),
        in_specs=[blk, g_spec, blk, blk],
        out_specs=[blk, blk, blk],
        out_shape=[jax.ShapeDtypeStruct((R, C), F32)] * 3,
        compiler_params=_cp("parallel"),
    )(w, g, m, v)


def _sum_parts(parts, *, name):
    P, R, C = parts.shape

    def body(p_ref, o_ref):
        acc = p_ref[0]
        for p in range(1, P):
            acc = acc + p_ref[p]
        o_ref[...] = acc

    return pl.pallas_call(
        body, name=name, grid=(1,),
        in_specs=[pl.BlockSpec((P, R, C), lambda i: (0, 0, 0))],
        out_specs=pl.BlockSpec((R, C), lambda i: (0, 0)),
        out_shape=jax.ShapeDtypeStruct((R, C), F32),
        compiler_params=_cp("arbitrary"),
    )(parts)


def _add_halves(g4, recv, c_idx, *, name):
    _, _, Rh, C = g4.shape
    tr = min(256, Rh)

    def body(c_ref, a_ref, b_ref, o_ref):
        o_ref[...] = (a_ref[...] + b_ref[...].astype(F32)).astype(BF)

    return pl.pallas_call(
        body, name=name,
        grid_spec=pltpu.PrefetchScalarGridSpec(
            num_scalar_prefetch=1, grid=(4, pl.cdiv(Rh, tr)),
            in_specs=[pl.BlockSpec((None, None, tr, C), lambda j, r, c: (j, c[0], r, 0)),
                      pl.BlockSpec((None, tr, C), lambda j, r, c: (j, r, 0))],
            out_specs=pl.BlockSpec((None, tr, C), lambda j, r, c: (j, r, 0))),
        out_shape=jax.ShapeDtypeStruct((4, Rh, C), BF),
        compiler_params=_cp("parallel", "arbitrary"),
    )(c_idx, g4, recv)


def _add_four(g4, from_sibling, from_chips, pos, *, name):
    _, _, Rh, C = g4.shape
    tr = min(256, Rh)

    def body(p_ref, a_ref, s_ref, b_ref, o_ref):
        own = a_ref[...] + s_ref[...].astype(F32)
        o_ref[...] = ((own + b_ref[0].astype(F32)) + b_ref[1].astype(F32)) + b_ref[2].astype(F32)

    return pl.pallas_call(
        body, name=name,
        grid_spec=pltpu.PrefetchScalarGridSpec(
            num_scalar_prefetch=1, grid=(pl.cdiv(Rh, tr),),
            in_specs=[pl.BlockSpec((None, None, tr, C), lambda r, p: (p[0], p[1], r, 0)),
                      pl.BlockSpec((None, tr, C), lambda r, p: (p[0], r, 0)),
                      pl.BlockSpec((3, tr, C), lambda r, p: (0, r, 0))],
            out_specs=pl.BlockSpec((None, tr, C), lambda r, p: (p[1], r, 0))),
        out_shape=jax.ShapeDtypeStruct((2, Rh, C), F32),
        compiler_params=_cp("arbitrary"),
    )(pos, g4, from_sibling, from_chips)


HBM = pl.BlockSpec(memory_space=pltpu.HBM)


def _mesh_pos():
    return lax.axis_index("x"), lax.axis_index("y"), lax.axis_index("c")


def _other_chips(x, y):
    return [(1 - x, y), (x, 1 - y), (1 - x, 1 - y)]


def _allgather_small(xs, *, name):
    m_per, n = xs.shape

    def body(x_ref, out_ref, send_sems, recv_sems, local_sem):
        x, y, c = _mesh_pos()
        me, sibling = (x, y, c), (x, y, 1 - c)
        chips = _other_chips(x, y)

        def rows(px, py, pc):
            return out_ref.at[pl.ds((4 * px + 2 * py + pc) * m_per, m_per), :]

        def copy(k, block, to, src=None):
            return pltpu.make_async_remote_copy(
                src_ref=rows(*block) if src is None else src, dst_ref=rows(*block),
                send_sem=send_sems.at[k], recv_sem=recv_sems.at[k], device_id=to, device_id_type=MESH)

        mine = pltpu.make_async_copy(x_ref, rows(*me), local_sem)
        mine.start()
        first = [copy(0, me, sibling, src=x_ref)]
        first += [copy(1 + j, me, (*chip, c), src=x_ref) for j, chip in enumerate(chips)]
        for cp in first:
            cp.start()
        passed = [copy(4 + j, (*chip, c), sibling) for j, chip in enumerate(chips)]
        for j, chip in enumerate(chips):
            copy(1 + j, (*chip, c), me).wait_recv()
            passed[j].start()
        copy(0, sibling, me).wait_recv()
        for j, chip in enumerate(chips):
            copy(4 + j, (*chip, 1 - c), me).wait_recv()
        for cp in first + passed:
            cp.wait_send()
        mine.wait()

    return pl.pallas_call(
        body, name=name,
        out_shape=jax.ShapeDtypeStruct((N_DEV * m_per, n), xs.dtype),
        in_specs=[pl.BlockSpec(memory_space=pltpu.VMEM)],
        out_specs=pl.BlockSpec(memory_space=pltpu.VMEM),
        scratch_shapes=[pltpu.SemaphoreType.DMA((7,)), pltpu.SemaphoreType.DMA((7,)), pltpu.SemaphoreType.DMA],
    )(xs)


def _chip_slab_copies(s_ref, out_ref, send_sems, recv_sems):
    R = s_ref.shape[0]
    Rh = R // 2
    x, y, c = _mesh_pos()
    me, sibling = (x, y, c), (x, y, 1 - c)
    chips = _other_chips(x, y)

    def half(px, py, pc):
        return out_ref.at[2 * px + py, pl.ds(pc * Rh, Rh), :]

    def copy(k, block, to, src=None):
        return pltpu.make_async_remote_copy(
            src_ref=half(*block) if src is None else src, dst_ref=half(*block),
            send_sem=send_sems.at[k], recv_sem=recv_sems.at[k], device_id=to, device_id_type=MESH)

    first = [copy(j, me, (*chip, c), src=s_ref.at[pl.ds(c * Rh, Rh), :]) for j, chip in enumerate(chips)]
    passed = [copy(3 + j, (*chip, c), sibling) for j, chip in enumerate(chips)]
    landed = [copy(j, (*chip, c), me) for j, chip in enumerate(chips)]
    from_sibling = [copy(3 + j, (*chip, 1 - c), me) for j, chip in enumerate(chips)]
    return first, passed, landed, from_sibling


def _gather_behind(s_ref, out_ref, send_sems, recv_sems, step, nsteps):
    first, passed, landed, from_sibling = _chip_slab_copies(s_ref, out_ref, send_sems, recv_sems)

    @pl.when(step == 0)
    def _():
        for cp in first:
            cp.start()

    @pl.when(step == (3 * nsteps) // 4)
    def _():
        for arrived, onward in zip(landed, passed):
            arrived.wait_recv()
            onward.start()

    def finish():
        @pl.when(step == nsteps - 1)
        def _():
            for cp in from_sibling:
                cp.wait_recv()
            for cp in first + passed:
                cp.wait_send()

    return finish


def _allgather_chip_slabs(slab, *, name):
    R, C = slab.shape

    def body(s_ref, out_ref, send_sems, recv_sems):
        first, passed, landed, from_sibling = _chip_slab_copies(s_ref, out_ref, send_sems, recv_sems)
        for cp in first:
            cp.start()
        for arrived, onward in zip(landed, passed):
            arrived.wait_recv()
            onward.start()
        for cp in from_sibling:
            cp.wait_recv()
        for cp in first + passed:
            cp.wait_send()

    return pl.pallas_call(
        body, name=name,
        out_shape=jax.ShapeDtypeStruct((N_CHIPS, R, C), slab.dtype),
        in_specs=[HBM], out_specs=HBM,
        scratch_shapes=[pltpu.SemaphoreType.DMA((6,)), pltpu.SemaphoreType.DMA((6,))],
    )(slab)


def _swap_halves(mine, *, name):
    def body(g_ref, out_ref, send_sems, recv_sems):
        x, y, c = _mesh_pos()
        copies = [pltpu.make_async_remote_copy(
            src_ref=g_ref.at[j], dst_ref=out_ref.at[j], send_sem=send_sems.at[j], recv_sem=recv_sems.at[j],
            device_id=(x, y, 1 - c), device_id_type=MESH) for j in range(N_CHIPS)]
        for cp in copies:
            cp.start()
        for cp in copies:
            cp.wait()

    return pl.pallas_call(
        body, name=name,
        out_shape=jax.ShapeDtypeStruct(mine.shape, mine.dtype),
        in_specs=[HBM], out_specs=HBM,
        scratch_shapes=[pltpu.SemaphoreType.DMA((N_CHIPS,)), pltpu.SemaphoreType.DMA((N_CHIPS,))],
    )(mine)


def _scatter_copies(p_ref, out_ref, send_sems, recv_sems):
    x, y, c = _mesh_pos()
    return [pltpu.make_async_remote_copy(
        src_ref=p_ref.at[2 * px + py], dst_ref=out_ref.at[j], send_sem=send_sems.at[j], recv_sem=recv_sems.at[j],
        device_id=(px, py, c), device_id_type=MESH) for j, (px, py) in enumerate(_other_chips(x, y))]


def _join_halves(buf, *, name):
    def body(b_ref, out_ref, send_sem, recv_sem):
        x, y, c = _mesh_pos()
        cp = pltpu.make_async_remote_copy(
            src_ref=b_ref.at[c], dst_ref=out_ref.at[c], send_sem=send_sem, recv_sem=recv_sem,
            device_id=(x, y, 1 - c), device_id_type=MESH)
        cp.start()
        cp.wait()

    return pl.pallas_call(
        body, name=name,
        out_shape=jax.ShapeDtypeStruct(buf.shape, buf.dtype),
        in_specs=[HBM], out_specs=HBM, input_output_aliases={0: 0},
        scratch_shapes=[pltpu.SemaphoreType.DMA, pltpu.SemaphoreType.DMA],
    )(buf)


def _pad_rows(a, mult):
    pad = (-a.shape[0]) % mult
    return a if pad == 0 else jnp.pad(a, ((0, pad),) + ((0, 0),) * (a.ndim - 1))


def _local_step(x, target, mod, wts, small, slabs=None, unpacks=None, reduce_early=None, grad_slab=None,
                reduce_late=None):
    S, D = x.shape
    HP = D // LANES
    row = lambda v: v.reshape(1, -1)
    msplit = [[row(mod[i, k * D:(k + 1) * D]) for k in range(6)] for i in range(2)]
    gw, gs = {}, {}
    dmod = [[None] * 6 for _ in range(2)]
    slab, where = grad_slab if grad_slab is not None else (None, {})

    def dw(key, a, b, name):
        nonlocal slab
        if key in where:
            slab = _matmul_tn(a, b, name=name, into=(slab,) + where[key])
        else:
            gw[key] = _matmul_tn(a, b, name=name)

    sh1, sc1, g1, sh2, sc2, g2 = msplit[0]
    n1w0, n2w0 = row(small["norm1_w"][0]), row(small["norm2_w"][0])
    slabs = slabs if slabs is not None else (None, None, None)
    proj0, h1_0, *gathered = _ln_matmul(x, n1w0, sc1, sh1, wts["hg_w_in"], slabs[0], relu2=False, name="hg_in_proj")
    if slabs[0] is not None:
        wts = {**wts, **unpacks[0](gathered[0])}
    gn = small["hg_gn_w"].reshape(1, LANES)
    ypre0, o0, states, *gathered = _hg_fwd(proj0, small["hg_lb"], gn, slabs[1], name="hg_fwd")
    if slabs[1] is not None:
        wts = {**wts, **unpacks[1](gathered[0])}
    x1, ymix0 = _matmul_resid(ypre0, wts["hg_w_out"], x, g1, name="hg_out_proj")
    a0, u0, h2_0, *gathered = _ln_matmul(x1, n2w0, sc2, sh2, wts["mlp_w1_0"], slabs[2], relu2=True, name="mlp0_up")
    if slabs[2] is not None:
        wts = {**wts, **unpacks[2](gathered[0])}
    x2, ymlp0 = _matmul_resid(u0, wts["mlp_w2_0"], x1, g2, name="mlp0_down")

    sh1b, sc1b, g1b, sh2b, sc2b, g2b = msplit[1]
    n1w1, n2w1 = row(small["norm1_w"][1]), row(small["norm2_w"][1])
    proj1, h1_1 = _ln_matmul(x2, n1w1, sc1b, sh1b, wts["fox_w_in"], relu2=False, name="fox_in_proj")
    nheads = 2 * HP
    bf_pad = jnp.pad(small["fox_b_f"].reshape(1, nheads), ((0, 0), (0, LANES - nheads)))
    qw2 = jnp.tile(small["fox_qn_w"].reshape(1, FOX_DH), (1, 2))
    kw2 = jnp.tile(small["fox_kn_w"].reshape(1, FOX_DH), (1, 2))
    fcum = _fox_cumsum(proj1, bf_pad, name="fox_cumsum")
    qa, ka, va, vat = _fox_prep(proj1, fcum, qw2, kw2, name="fox_prep")
    jmin, imax = _fox_skip_bounds(fcum, small["fox_qn_w"], small["fox_kn_w"], nheads)
    ypre1, o1, q2 = _fox_fwd(jmin, qa, ka, vat, proj1, name="fox_fwd")
    x3, ymix1 = _matmul_resid(ypre1, wts["fox_w_out"], x2, g1b, name="fox_out_proj")
    a1, u1, h2_1 = _ln_matmul(x3, n2w1, sc2b, sh2b, wts["mlp_w1_1"], relu2=True, name="mlp1_up")
    x4, ymlp1 = _matmul_resid(u1, wts["mlp_w2_1"], x3, g2b, name="mlp1_down")

    loss, dx4, dfw = _loss_kernel(x4, row(small["final_w"]), target, name="loss")
    gs["final_w"] = dfw.reshape(-1)

    def mlp_bwd(i, dx_out, x_in, h2, a, u, ymlp, n2w, sc2_, g2_):
        dz, dm, dg2 = _gate_matmul_nt(dx_out, g2_, ymlp, wts[f"mlp_w2_{i}"], a, name=f"mlp{i}_down_bwd")
        dw(f"mlp_w2_{i}", u, dm[None], f"mlp{i}_dw2")
        dw(f"mlp_w1_{i}", h2, dz[None], f"mlp{i}_dw1")
        dx_in, dsc, dsh, dnw = _matmul_nt_lnbwd(dz[None], wts[f"mlp_w1_{i}"], x_in, n2w, sc2_, dx_out,
                                                name=f"mlp{i}_up_bwd")
        dmod[i][3], dmod[i][4], dmod[i][5] = dsh, dsc, dg2
        return dx_in, dnw

    dx3, dn2w1 = mlp_bwd(1, dx4, x3, h2_1, a1, u1, ymlp1, n2w1, sc2b, g2b)
    dyp1, dm1, dg1b = _gate_matmul_nt(dx3, g1b, ymix1, wts["fox_w_out"], None, name="fox_out_bwd")
    dw("fox_w_out", ypre1, dm1[None], "fox_dw_out")
    doa = _fox_bwd_prep(dyp1, o1, proj1, q2, name="fox_bwd_prep")
    dqa, dka, dva, colsum = _fox_bwd(imax, q2, ka, va, doa, name="fox_bwd")
    colsum = jnp.pad(colsum[:, 0, :].T, ((0, 0), (0, LANES - nheads)))
    dproj1, dqw, dkw = _fox_bwd_post(dqa, dka, dva, proj1, dyp1, o1, qw2, kw2, name="fox_bwd_post")
    dproj1, dbf = _fox_dfz(colsum, nheads, proj1, bf_pad, dproj1, name="fox_dfz")
    dw("fox_w_in", h1_1, dproj1, "fox_dw_in")
    dx2, dsc, dsh, dn1w1 = _matmul_nt_lnbwd(dproj1, wts["fox_w_in"], x2, n1w1, sc1b, dx3, name="fox_in_bwd")
    dmod[1][0], dmod[1][1], dmod[1][2] = dsh, dsc, dg1b
    gs["fox_qn_w"] = dqw[0, :FOX_DH] + dqw[0, FOX_DH:]
    gs["fox_kn_w"] = dkw[0, :FOX_DH] + dkw[0, FOX_DH:]
    gs["fox_b_f"] = dbf[0, :nheads]

    dx1, dn2w0 = mlp_bwd(0, dx2, x1, h2_0, a0, u0, ymlp0, n2w0, sc2, g2)
    dyp0, dm0, dg1 = _gate_matmul_nt(dx1, g1, ymix0, wts["hg_w_out"], None, name="hg_out_bwd")
    dw("hg_w_out", ypre0, dm0[None], "hg_dw_out")
    part, ctx = reduce_early(gw, slab) if reduce_early is not None else (None, None)
    dproj0, dlb, dgn, *from_chips = _hg_bwd(proj0, small["hg_lb"], gn, o0, states, dyp0, part, name="hg_bwd")
    early = (ctx, from_chips[0]) if reduce_early is not None else None
    dw("hg_w_in", h1_0, dproj0, "hg_dw_in")
    part, ctx = reduce_late(gw) if reduce_late is not None else (None, None)
    dx0, dsc, dsh, dn1w0, *from_chips = _matmul_nt_lnbwd(dproj0, wts["hg_w_in"], x, n1w0, sc1, dx1, part, name="hg_in_bwd")
    late = (ctx, from_chips[0]) if reduce_late is not None else None
    dmod[0][0], dmod[0][1], dmod[0][2] = dsh, dsc, dg1
    gs["hg_lb"] = dlb
    gs["hg_gn_w"] = jnp.sum(dgn, axis=0)

    gs["norm1_w"] = jnp.concatenate([dn1w0, dn1w1], axis=0)
    gs["norm2_w"] = jnp.concatenate([dn2w0, dn2w1], axis=0)
    gs["dmod"] = jnp.stack([jnp.concatenate(dmod[i], axis=1)[0] for i in range(2)])
    return loss, dx0, gw, gs, early, late


def _pack_halves(layout):
    rh = -(-max(sum(a.shape[0] for _, a in half) for half in layout) // 16) * 16
    place, parts = {}, []
    for h, half in enumerate(layout):
        off = 0
        for n, a in half:
            place[n] = (h, off, a.shape[0])
            off += a.shape[0]
        parts.append(jnp.pad(jnp.concatenate([a.astype(BF) for _, a in half], axis=0), ((0, rh - off), (0, 0))))
    return jnp.concatenate(parts, axis=0), place, rh


SMALL_NAMES = ["norm1_w", "norm2_w", "hg_lb", "hg_gn_w", "fox_b_f", "fox_qn_w", "fox_kn_w", "final_w"]


def _pack_small(d, names):
    rows, offs, r0 = [], {}, 0
    for n in names:
        flat = d[n].reshape(-1)
        nr = -(-flat.shape[0] // LANES)
        rows.append(jnp.pad(flat, (0, nr * LANES - flat.shape[0])).reshape(nr, LANES))
        offs[n] = (r0, nr)
        r0 += nr
    return jnp.concatenate(rows, axis=0), offs


def _unpack_small(packed, offs, name, like):
    r0, nr = offs[name]
    return packed[r0:r0 + nr].reshape(-1)[:like.size].reshape(like.shape)


def kernel(x, c, w_mod, b_mod, norm1_w, norm2_w, hg_w_in, hg_w_out, hg_lb, hg_gn_w, fox_w_in, fox_b_f, fox_qn_w, fox_kn_w, fox_w_out, mlp_w1, mlp_w2, final_w, loss_target, m_w_mod, m_b_mod, m_norm1_w, m_norm2_w, m_hg_w_in, m_hg_w_out, m_hg_lb, m_hg_gn_w, m_fox_w_in, m_fox_b_f, m_fox_qn_w, m_fox_kn_w, m_fox_w_out, m_mlp_w1, m_mlp_w2, m_final_w, v_w_mod, v_b_mod, v_norm1_w, v_norm2_w, v_hg_w_in, v_hg_w_out, v_hg_lb, v_hg_gn_w, v_fox_w_in, v_fox_b_f, v_fox_qn_w, v_fox_kn_w, v_fox_w_out, v_mlp_w1, v_mlp_w2, v_final_w):
    S, D = x.shape[1], x.shape[2]
    nheads = D // FOX_DH
    ax, ay, ac = _mesh_pos()
    chip = 2 * ax + ay
    dev = 2 * chip + ac
    xs, tgt = x.reshape(S, D), loss_target.reshape(S, D)

    c_all = _allgather_small(_pad_rows(c.reshape(-1, LANES), 8), name="gather_c")
    c_all = c_all.reshape(N_DEV, -1)[:, :D]
    c16 = _pad_rows(c_all, 16)
    nmod = w_mod.shape[2]
    b_shard = lax.dynamic_slice_in_dim(b_mod, chip * nmod, nmod, axis=1)
    mod_shard = _mod_fwd(c16, w_mod, b_shard[:, None, :], name="mod_fwd")[:, :N_DEV]
    mod_all = _allgather_small(mod_shard.reshape(-1, LANES), name="gather_mod")
    mod_all = mod_all.reshape(N_CHIPS, 2, 2, N_DEV, nmod)[:, 0]
    mod = lax.dynamic_index_in_dim(mod_all, dev, axis=2, keepdims=False)
    mod = mod.transpose(1, 0, 2).reshape(2, N_CHIPS * nmod)

    fox_rows = fox_w_in.shape[2]
    col = lambda g: g.transpose(1, 0, 2).reshape(g.shape[1], -1)
    rowsh = lambda g: g.reshape(-1, g.shape[2])
    own = lambda g, s: lax.dynamic_update_index_in_dim(g, s, chip, 0)

    slab_in = hg_w_in[0].astype(BF)
    wts = {"hg_w_in": col(own(_allgather_chip_slabs(slab_in, name="gather_hg_w_in"), slab_in))}
    fox_flat, fox_cut = fox_w_in[0].reshape(fox_rows, D), fox_rows // 2
    slabs, unpacks = [], []
    for layout_w in ([[("mlp_w1_0", mlp_w1[0])], [("mlp_w2_0", mlp_w2[0])]],
                     [[("mlp_w1_1", mlp_w1[1]), ("hg_w_out", hg_w_out[0])], [("mlp_w2_1", mlp_w2[1]), ("fox_w_out", fox_w_out[0])]],
                     [[("fox_a", fox_flat[:fox_cut])], [("fox_b", fox_flat[fox_cut:])]]):
        slab_w, place_w, rh_w = _pack_halves(layout_w)

        def unpack(gathered, slab_w=slab_w, place_w=place_w, rh_w=rh_w):
            gathered = own(gathered, slab_w)
            out = {}
            for n, (h, off, rows) in place_w.items():
                g = gathered[:, h * rh_w + off:h * rh_w + off + rows, :]
                out[n] = col(g) if n.startswith("mlp_w1") else rowsh(g) if n.startswith(("mlp_w2", "hg_", "fox_w")) else g
            if "fox_a" in out:
                fox_in = col(jnp.concatenate([out.pop("fox_a"), out.pop("fox_b")], axis=1).reshape(N_CHIPS, D, fox_rows))
                out["fox_w_in"] = jnp.pad(fox_in, ((0, 0), (0, 5 * D - fox_in.shape[1])))
            return out

        slabs.append(slab_w)
        unpacks.append(unpack)

    small = {"norm1_w": norm1_w, "norm2_w": norm2_w, "hg_lb": hg_lb, "hg_gn_w": hg_gn_w, "fox_b_f": fox_b_f,
             "fox_qn_w": fox_qn_w, "fox_kn_w": fox_kn_w, "final_w": final_w}

    def uncol(g, n):
        return g.reshape(g.shape[0], N_CHIPS, n).transpose(1, 0, 2)

    pos = jnp.stack([chip, ac])

    def swap_and_add(g4, tag):
        to_sibling = lax.dynamic_index_in_dim(g4, 1 - ac, axis=1, keepdims=False).astype(BF)
        from_sibling = _swap_halves(to_sibling, name=f"rs_swap_{tag}")
        return from_sibling, _add_halves(g4, from_sibling, ac.reshape(1), name=f"rs_add_halves_{tag}")

    def finish(g4, from_sibling, from_chips, tag):
        my_half = _add_four(g4, from_sibling, from_chips, pos, name=f"rs_add_chips_{tag}")
        return _join_halves(my_half, name=f"rs_join_{tag}")

    layout = [[("mlp_w1", 2 * D), ("hg_w_out", D // 4), ("fox_a", fox_cut)],
              [("mlp_w2", 2 * D), ("fox_w_out", D // 4), ("fox_b", fox_rows - fox_cut)]]
    place = {}
    for h, half in enumerate(layout):
        off = 0
        for n, rows in half:
            place[n] = (h, off, rows)
            off += rows

    rh = -(-max(sum(rows for _, rows in half) for half in layout) // 256) * 256
    where = {"hg_w_out": ("row",) + place["hg_w_out"][:2], "fox_w_out": ("row",) + place["fox_w_out"][:2]}
    for i in range(2):
        where[f"mlp_w1_{i}"] = ("col", place["mlp_w1"][0], place["mlp_w1"][1] + i * D)
        where[f"mlp_w2_{i}"] = ("row", place["mlp_w2"][0], place["mlp_w2"][1] + i * D)

    def reduce_early(gw, slab):
        gfox = uncol(gw["fox_w_in"][:, :4 * fox_rows], fox_rows).reshape(N_CHIPS, fox_rows, D)
        for n, piece in (("fox_a", gfox[:, :fox_cut]), ("fox_b", gfox[:, fox_cut:])):
            h, off, rows = place[n]
            src = jnp.pad(piece, ((0, 0), (0, rh - off - rows), (0, 0)))
            slab = _rows_into_slab(slab, src, h, off, name=f"rs_fill_{h}")
        from_sibling, part = swap_and_add(slab, "early")
        return part, (slab, from_sibling)

    def reduce_late(gw):
        g4 = uncol(gw["hg_w_in"], D).reshape(N_CHIPS, 2, D // 2, D)
        from_sibling, part = swap_and_add(g4, "late")
        return part, (g4, from_sibling)

    loss_part, grad_x, gw, gs, (early, from_chips_early), (late, from_chips_late) = _local_step(
        xs, tgt, mod, wts, small, slabs, unpacks, reduce_early, (lax.empty((N_CHIPS, 2, rh, D), F32), where), reduce_late)
    gshard = finish(*early, from_chips_early, "early")
    g_hg_w_in = finish(*late, from_chips_late, "late").reshape(D, D)

    names = ["dmod", "loss"] + SMALL_NAMES
    packed, offs = _pack_small({**gs, "loss": loss_part[0, :1]}, names)
    packed = _pad_rows(packed, 8)
    rp = packed.shape[0]
    parts = _allgather_small(packed, name="gather_small").reshape(N_DEV, rp, LANES)
    total = _sum_parts(parts, name="sum_small")
    r0, nr = offs["dmod"]
    dmod_all = parts[:, r0:r0 + nr].reshape(N_DEV, 2, N_CHIPS * nmod)
    dmod_shard = lax.dynamic_slice_in_dim(dmod_all, chip * nmod, nmod, axis=2).transpose(1, 0, 2)
    g_w_mod = _mod_bwd(c16, jnp.pad(dmod_shard, ((0, 0), (0, 16 - N_DEV), (0, 0))), name="mod_bwd")

    loss = _unpack_small(total, offs, "loss", loss_part[0, :1]).reshape(())
    grads = {"w_mod": g_w_mod, "b_mod": _unpack_small(total, offs, "dmod", b_mod)}
    for n in SMALL_NAMES:
        grads[n] = _unpack_small(total, offs, n, small[n])

    given = dict(w_mod=(w_mod, m_w_mod, v_w_mod), b_mod=(b_mod, m_b_mod, v_b_mod), norm1_w=(norm1_w, m_norm1_w, v_norm1_w),
                 norm2_w=(norm2_w, m_norm2_w, v_norm2_w), hg_w_in=(hg_w_in, m_hg_w_in, v_hg_w_in),
                 hg_w_out=(hg_w_out, m_hg_w_out, v_hg_w_out), hg_lb=(hg_lb, m_hg_lb, v_hg_lb),
                 hg_gn_w=(hg_gn_w, m_hg_gn_w, v_hg_gn_w), fox_w_in=(fox_w_in, m_fox_w_in, v_fox_w_in),
                 fox_b_f=(fox_b_f, m_fox_b_f, v_fox_b_f), fox_qn_w=(fox_qn_w, m_fox_qn_w, v_fox_qn_w),
                 fox_kn_w=(fox_kn_w, m_fox_kn_w, v_fox_kn_w), fox_w_out=(fox_w_out, m_fox_w_out, v_fox_w_out),
                 mlp_w1=(mlp_w1, m_mlp_w1, v_mlp_w1), mlp_w2=(mlp_w2, m_mlp_w2, v_mlp_w2), final_w=(final_w, m_final_w, v_final_w))
    upd = {}

    for n, (h, off, rows) in place.items():
        if n.startswith("fox_") and n != "fox_w_out":
            continue
        w, m, v = given[n]
        flat = lambda a: a.reshape(rows, D)
        d, mn, vn = _adamw(flat(w), gshard, flat(m), flat(v), g_at=(h, off), name=f"adamw_{n}")
        grads[n] = gshard[h, off:off + rows].reshape(w.shape)
        upd[n] = tuple(a.reshape(w.shape) for a in (d, mn, vn))

    w, m, v = given["fox_w_in"]
    pieces, gpieces, row0 = [], [], 0
    for n in ("fox_a", "fox_b"):
        h, off, rows = place[n]
        flat = lambda a: a.reshape(fox_rows, D)[row0:row0 + rows]
        pieces.append(_adamw(flat(w), gshard, flat(m), flat(v), g_at=(h, off), name=f"adamw_{n}"))
        gpieces.append(gshard[h, off:off + rows])
        row0 += rows
    grads["fox_w_in"] = jnp.concatenate(gpieces, axis=0).reshape(w.shape)
    upd["fox_w_in"] = tuple(jnp.concatenate([p[i] for p in pieces], axis=0).reshape(w.shape) for i in range(3))

    w, m, v = given["hg_w_in"]
    grads["hg_w_in"] = g_hg_w_in.reshape(w.shape)
    upd["hg_w_in"] = tuple(a.reshape(w.shape) for a in _adamw(w[0], g_hg_w_in, m[0], v[0], name="adamw_hg_w_in"))

    w, m, v = given["w_mod"]
    flat = lambda a: a.reshape(-1, nmod)
    upd["w_mod"] = tuple(a.reshape(w.shape) for a in _adamw(flat(w), flat(g_w_mod), flat(m), flat(v), name="adamw_w_mod"))

    snames = ["b_mod"] + SMALL_NAMES
    pw, soffs = _pack_small({n: given[n][0] for n in snames}, snames)
    pm, _ = _pack_small({n: given[n][1] for n in snames}, snames)
    pv, _ = _pack_small({n: given[n][2] for n in snames}, snames)
    pg, _ = _pack_small({n: grads[n] for n in snames}, snames)
    pw, pm, pv, pg = (_pad_rows(a, 8) for a in (pw, pm, pv, pg))
    sd, smn, svn = _adamw(pw, pg, pm, pv, name="adamw_small")
    for n in snames:
        like = given[n][0]
        upd[n] = tuple(_unpack_small(a, soffs, n, like) for a in (sd, smn, svn))

    order = ["w_mod", "b_mod", "norm1_w", "norm2_w", "hg_w_in", "hg_w_out", "hg_lb", "hg_gn_w", "fox_w_in", "fox_b_f",
             "fox_qn_w", "fox_kn_w", "fox_w_out", "mlp_w1", "mlp_w2", "final_w"]
    return (loss, grad_x.reshape(x.shape), *[grads[n] for n in order], *[upd[n][0] for n in order],
            *[upd[n][1] for n in order], *[upd[n][2] for n in order])
```

```python
import math

import jax
import jax.numpy as jnp
from jax import lax
from jax.experimental import pallas as pl
from jax.experimental.pallas import tpu as pltpu

EPS = 1e-6
ADAM_LR, ADAM_B1, ADAM_B2, ADAM_EPS, ADAM_WD, ADAM_STEP = 0.001, 0.9, 0.999, 1e-08, 0.01, 10

F32 = jnp.float32
BF = jnp.bfloat16
LANES = 128
HG_CHUNK = 64
HG_HEADS_PER_STEP = 8
HG_TOKENS_PER_STEP = 256
FOX_ROWS_PER_STEP = 2048
FOX_BWD_TILES = (8, 4, 2, 1)
LOG2E = 1.4426950408889634
FOX_DH = 64
N_CHIPS = 4
N_DEV = 8
VMEM_LIMIT = 56 * 1024 * 1024
MESH = pl.DeviceIdType.MESH

NT = (((1,), (1,)), ((), ()))
TN = (((0,), (0,)), ((), ()))


def _pick(n, pref, mult=LANES):
    if n <= pref:
        return n
    t = (pref // mult) * mult
    while t >= mult:
        if n % t == 0:
            return t
        t -= mult
    raise ValueError((n, pref, mult))


def _cp(*sem):
    return pltpu.CompilerParams(dimension_semantics=sem, vmem_limit_bytes=VMEM_LIMIT)


def _dot(a, b):
    return jnp.dot(a, b, preferred_element_type=F32)


def _dg(a, b, dims):
    return lax.dot_general(a, b, dims, preferred_element_type=F32)


def _split3(x):
    hi = x.astype(BF)
    r1 = x - hi.astype(F32)
    mid = r1.astype(BF)
    lo = (r1 - mid.astype(F32)).astype(BF)
    return hi, mid, lo


def _tri_dot(tri, x):
    hi, mid, lo = _split3(x)
    return _dot(tri, hi) + _dot(tri, mid) + _dot(tri, lo)


def _dg3(a, b, dims):
    ah, bh = a.astype(BF), b.astype(BF)
    al, bl = (a - ah.astype(F32)).astype(BF), (b - bh.astype(F32)).astype(BF)
    return _dg(ah, bh, dims) + _dg(ah, bl, dims) + _dg(al, bh, dims)


def _dg1(a, b, dims):
    return _dg(a.astype(BF), b.astype(BF), dims)


NN = (((1,), (0,)), ((), ()))


def _sigmoid(x):
    return jax.nn.sigmoid(x)


def _ln_matmul(x, nw, sc, sh, w, slab=None, *, relu2, name):
    S, D = x.shape
    N = w.shape[1]
    tm, tn = _pick(S, 512, 16), N
    fused = slab is not None

    def body(x_ref, nw_ref, sc_ref, sh_ref, w_ref, *rest):
        if fused:
            s_ref, *outs, out_ref, hs, send_sems, recv_sems = rest
            finish = _gather_behind(s_ref, out_ref, send_sems, recv_sems, pl.program_id(0), S // tm)
        else:
            outs, hs = rest[:-1], rest[-1]
        h_ref = outs[-1]

        @pl.when(pl.program_id(1) == 0)
        def _():
            xv = x_ref[...]
            r = lax.rsqrt(jnp.mean(xv * xv, axis=-1, keepdims=True) + EPS)
            hb = ((xv * r * nw_ref[...]) * (1.0 + sc_ref[...]) + sh_ref[...]).astype(BF)
            hs[...] = hb
            h_ref[...] = hb

        z = _dot(hs[...], w_ref[...])
        if relu2:
            a = jnp.maximum(z, 0.0)
            outs[0][...] = a.astype(BF)
            outs[1][...] = (a * a).astype(BF)
        else:
            outs[0][...] = z
        if fused:
            finish()

    vec = pl.BlockSpec((1, D), lambda i, j: (0, 0))
    tile = pl.BlockSpec((tm, tn), lambda i, j: (i, j))
    if relu2:
        out_shape = [jax.ShapeDtypeStruct((S, N), BF), jax.ShapeDtypeStruct((S, N), BF)]
        out_specs = [tile, tile]
    else:
        out_shape = [jax.ShapeDtypeStruct((S, N), F32)]
        out_specs = [tile]
    out_shape.append(jax.ShapeDtypeStruct((S, D), BF))
    out_specs.append(pl.BlockSpec((tm, D), lambda i, j: (i, 0)))
    in_specs = [pl.BlockSpec((tm, D), lambda i, j: (i, 0)), vec, vec, vec, pl.BlockSpec((D, tn), lambda i, j: (0, j))]
    scratch = [pltpu.VMEM((tm, D), BF)]
    args = [x, nw, sc, sh, w]
    if fused:
        in_specs.append(HBM)
        out_specs.append(HBM)
        out_shape.append(jax.ShapeDtypeStruct((N_CHIPS,) + slab.shape, slab.dtype))
        scratch += [pltpu.SemaphoreType.DMA((6,)), pltpu.SemaphoreType.DMA((6,))]
        args.append(slab)
    return pl.pallas_call(
        body, name=name, grid=(S // tm, N // tn), in_specs=in_specs, out_specs=out_specs, out_shape=out_shape,
        scratch_shapes=scratch, compiler_params=_cp("arbitrary", "arbitrary"),
    )(*args)


def _matmul_resid(a, w, x, gate, *, name):
    S, K = a.shape
    D = w.shape[1]
    tm, tn = _pick(S, 1024 if K <= 1024 else 512, 16), D

    def body(a_ref, w_ref, x_ref, g_ref, o_ref, y_ref):
        y = _dot(a_ref[...], w_ref[...])
        y_ref[...] = y.astype(BF)
        o_ref[...] = x_ref[...] + g_ref[...] * y

    tile = pl.BlockSpec((tm, tn), lambda i, j: (i, j))
    return pl.pallas_call(
        body, name=name, grid=(S // tm, D // tn),
        in_specs=[pl.BlockSpec((tm, K), lambda i, j: (i, 0)), pl.BlockSpec((K, tn), lambda i, j: (0, j)),
                  tile, pl.BlockSpec((1, tn), lambda i, j: (0, j))],
        out_specs=[tile, tile],
        out_shape=[jax.ShapeDtypeStruct((S, D), F32), jax.ShapeDtypeStruct((S, D), BF)],
        compiler_params=_cp("parallel", "arbitrary"),
    )(a, w, x, gate)


def _gate_matmul_nt(dx, gate, y, w, act, *, name):
    S, D = dx.shape
    K = w.shape[0]
    tm, tn = _pick(S, 1024 if K <= 1024 else 512, 16), K
    fused = act is not None

    def body(dx_ref, g_ref, y_ref, w_ref, *rest):
        if fused:
            act_ref, da_ref, dm_ref, dg_ref, ms = rest
        else:
            da_ref, dm_ref, dg_ref, ms = rest
        i, j = pl.program_id(0), pl.program_id(1)

        @pl.when((i == 0) & (j == 0))
        def _():
            dg_ref[...] = jnp.zeros_like(dg_ref)

        @pl.when(j == 0)
        def _():
            dxv = dx_ref[...]
            dmb = (dxv * g_ref[...]).astype(BF)
            ms[...] = dmb
            dm_ref[...] = dmb
            dg_ref[...] += jnp.sum(dxv * y_ref[...].astype(F32), axis=0, keepdims=True)

        da = _dg(ms[...], w_ref[...], NT)
        if fused:
            da_ref[...] = (da * (2.0 * act_ref[...].astype(F32))).astype(BF)
        else:
            da_ref[...] = da

    row = pl.BlockSpec((tm, D), lambda i, j: (i, 0))
    vec = pl.BlockSpec((1, D), lambda i, j: (0, 0))
    tile = pl.BlockSpec((tm, tn), lambda i, j: (i, j))
    in_specs = [row, vec, row, pl.BlockSpec((tn, D), lambda i, j: (j, 0))]
    args = [dx, gate, y, w]
    if fused:
        in_specs.append(tile)
        args.append(act)
    return pl.pallas_call(
        body, name=name, grid=(S // tm, K // tn),
        in_specs=in_specs, out_specs=[tile, row, vec],
        out_shape=[jax.ShapeDtypeStruct((S, K), BF if fused else F32), jax.ShapeDtypeStruct((S, D), BF),
                   jax.ShapeDtypeStruct((1, D), F32)],
        scratch_shapes=[pltpu.VMEM((tm, D), BF)],
        compiler_params=_cp("arbitrary", "arbitrary"),
    )(*args)


def _matmul_tn(a, b, *, name, into=None):
    S, Ka = a.shape
    P, _, Db = b.shape
    tk, tn, ts = _pick(Ka, 1024), _pick(Db, 1024), _pick(S, 1024, 16)
    if into is not None:
        slab, kind, half, off = into
        C = tn = slab.shape[3]
        per_chip = Ka // N_CHIPS
        all_chips = kind == "row" and tk == Ka
        if kind == "row" and not all_chips:
            tk = min(tk, per_chip)
        assert tn == C and P * Db == (N_CHIPS * C if kind == "col" else C)
        if kind == "col":
            assert tk == Ka and off % tk == 0
        elif all_chips:
            assert off % per_chip == 0
        else:
            assert per_chip % tk == 0 and off % tk == 0
    npb = Db // tn

    def body(a_ref, b_ref, *rest):
        o_ref, acc = rest[-2:]
        s = pl.program_id(2)

        @pl.when(s == 0)
        def _():
            acc[...] = jnp.zeros_like(acc)

        acc[...] += _dg(a_ref[...], b_ref[...], TN)

        @pl.when(s == pl.num_programs(2) - 1)
        def _():
            o_ref[...] = acc[...].reshape(o_ref.shape)

    in_specs = [pl.BlockSpec((ts, tk), lambda i, j, s: (s, i)),
                pl.BlockSpec((None, ts, tn), lambda i, j, s: (j // npb, s, j % npb))]
    args = [a, b]
    if into is None:
        out_spec = pl.BlockSpec((tk, tn), lambda i, j, s: (i, j))
        out_shape = jax.ShapeDtypeStruct((Ka, P * Db), F32)
        aliases = {}
    else:
        per = per_chip // tk if kind == "row" and not all_chips else 1
        if kind == "col":
            out_spec = pl.BlockSpec((None, None, tk, tn), lambda i, j, s: (j, half, off // tk + i, 0))
        elif all_chips:
            out_spec = pl.BlockSpec((N_CHIPS, None, per_chip, tn), lambda i, j, s: (0, half, off // per_chip, 0))
        else:
            out_spec = pl.BlockSpec((None, None, tk, tn), lambda i, j, s: (i // per, half, off // tk + i % per, 0))
        out_shape = jax.ShapeDtypeStruct(slab.shape, F32)
        in_specs.append(pl.BlockSpec(memory_space=pl.ANY))
        args.append(slab)
        aliases = {2: 0}
    return pl.pallas_call(
        body, name=name, grid=(Ka // tk, P * npb, S // ts),
        in_specs=in_specs, out_specs=out_spec, out_shape=out_shape,
        scratch_shapes=[pltpu.VMEM((tk, tn), F32)], input_output_aliases=aliases,
        compiler_params=_cp("parallel", "parallel", "arbitrary"),
    )(*args)


def _matmul_nt_lnbwd(g, w, x, nw, sc, dx_out, part=None, *, name):
    P, S, Dg = g.shape
    D = x.shape[1]
    tm = _pick(S, 512, 16)
    fused = part is not None

    def body(g_ref, w_ref, x_ref, nw_ref, sc_ref, dxo_ref, *rest):
        if fused:
            p_ref, dx_ref, dsc_ref, dsh_ref, dnw_ref, recv_ref, send_sems, recv_sems = rest
            copies = _scatter_copies(p_ref, recv_ref, send_sems, recv_sems)
        else:
            dx_ref, dsc_ref, dsh_ref, dnw_ref = rest

        @pl.when(pl.program_id(0) == 0)
        def _():
            dsc_ref[...] = jnp.zeros_like(dsc_ref)
            dsh_ref[...] = jnp.zeros_like(dsh_ref)
            dnw_ref[...] = jnp.zeros_like(dnw_ref)
            if fused:
                for cp in copies:
                    cp.start()

        dh = _dg(g_ref[0], w_ref[:, 0:Dg], NT)
        for p in range(1, P):
            dh = dh + _dg(g_ref[p], w_ref[:, p * Dg:(p + 1) * Dg], NT)
        xv = x_ref[...]
        nwv = nw_ref[...]
        r = lax.rsqrt(jnp.mean(xv * xv, axis=-1, keepdims=True) + EPS)
        xr = xv * r
        dn = dh * (1.0 + sc_ref[...])
        dsc_ref[...] += jnp.sum(dh * (xr * nwv), axis=0, keepdims=True)
        dsh_ref[...] += jnp.sum(dh, axis=0, keepdims=True)
        dnw_ref[...] += jnp.sum(dn * xr, axis=0, keepdims=True)
        u = dn * nwv
        dx_ref[...] = dxo_ref[...] + r * (u - xr * jnp.mean(u * xr, axis=-1, keepdims=True))

        if fused:
            @pl.when(pl.program_id(0) == S // tm - 1)
            def _():
                for cp in copies:
                    cp.wait()

    row = pl.BlockSpec((tm, D), lambda i: (i, 0))
    vec = pl.BlockSpec((1, D), lambda i: (0, 0))
    in_specs = [pl.BlockSpec((P, tm, Dg), lambda i: (0, i, 0)), pl.BlockSpec((D, P * Dg), lambda i: (0, 0)), row, vec, vec, row]
    out_specs = [row, vec, vec, vec]
    out_shape = [jax.ShapeDtypeStruct((S, D), F32)] + [jax.ShapeDtypeStruct((1, D), F32)] * 3
    scratch, args = [], [g, w, x, nw, sc, dx_out]
    if fused:
        in_specs.append(HBM)
        out_specs.append(HBM)
        out_shape.append(jax.ShapeDtypeStruct((3,) + part.shape[1:], part.dtype))
        scratch = [pltpu.SemaphoreType.DMA((3,)), pltpu.SemaphoreType.DMA((3,))]
        args.append(part)
    return pl.pallas_call(
        body, name=name, grid=(S // tm,), in_specs=in_specs, out_specs=out_specs, out_shape=out_shape,
        scratch_shapes=scratch, compiler_params=_cp("arbitrary"),
    )(*args)


def _loss_kernel(x, fw, tgt, *, name):
    S, D = x.shape
    tm = _pick(S, 512, 8)

    def body(x_ref, fw_ref, t_ref, l_ref, dx_ref, dfw_ref):
        @pl.when(pl.program_id(0) == 0)
        def _():
            l_ref[...] = jnp.zeros_like(l_ref)
            dfw_ref[...] = jnp.zeros_like(dfw_ref)

        xv = x_ref[...]
        fwv = fw_ref[...]
        r = lax.rsqrt(jnp.mean(xv * xv, axis=-1, keepdims=True) + EPS)
        xr = xv * r
        err = xr * fwv - t_ref[...]
        per_tok = jnp.mean(err * err, axis=-1, keepdims=True)
        l_ref[...] += 0.5 * jnp.sum(per_tok, axis=0, keepdims=True)
        dy = err * (1.0 / D)
        dfw_ref[...] += jnp.sum(dy * xr, axis=0, keepdims=True)
        u = dy * fwv
        dx_ref[...] = r * (u - xr * jnp.mean(u * xr, axis=-1, keepdims=True))

    row = pl.BlockSpec((tm, D), lambda i: (i, 0))
    vec = pl.BlockSpec((1, D), lambda i: (0, 0))
    return pl.pallas_call(
        body, name=name, grid=(S // tm,),
        in_specs=[row, vec, row],
        out_specs=[pl.BlockSpec((1, LANES), lambda i: (0, 0)), row, vec],
        out_shape=[jax.ShapeDtypeStruct((1, LANES), F32), jax.ShapeDtypeStruct((S, D), F32),
                   jax.ShapeDtypeStruct((1, D), F32)],
        compiler_params=_cp("arbitrary"),
    )(x, fw, tgt)


def _hg_lower_bound(lb3):
    mx = jnp.max(lb3, axis=0, keepdims=True)
    e = jnp.exp(lb3 - mx)
    p = e / jnp.sum(e, axis=0, keepdims=True)
    return p[0:1, :], p


def _hg_chunk_common(qr, fz, lbv):
    sq = _sigmoid(qr)
    q = qr * sq
    sig = _sigmoid(fz)
    f = lbv + (1.0 - lbv) * sig
    k = (1.0 - lbv) * (1.0 - sig)
    return q, sq, sig, f, k, jnp.log(f)


def _row_of(x, rows, r):
    return jnp.sum(jnp.where(rows == r, x, 0.0), axis=0, keepdims=True)


def _hg_fwd(proj, hg_lb, gn, slab=None, *, name):
    S = proj.shape[0]
    D = proj.shape[1] // 4
    H = D // LANES
    HB = min(HG_HEADS_PER_STEP, H)
    W = HB * LANES
    C = HG_CHUNK
    T = _pick(S, HG_TOKENS_PER_STEP, C)
    nch, nb = T // C, S // T
    ng = H // HB
    fused = slab is not None

    def body(q_ref, fz_ref, v_ref, g_ref, lb_ref, gn_ref, *rest):
        if fused:
            s_ref, y_ref, o_ref, sts_ref, out_ref, st, send_sems, recv_sems = rest
            finish = _gather_behind(s_ref, out_ref, send_sems, recv_sems,
                                    pl.program_id(0) * nb + pl.program_id(1), ng * nb)
        else:
            y_ref, o_ref, sts_ref, st = rest

        @pl.when(pl.program_id(1) == 0)
        def _():
            st[...] = jnp.zeros_like(st)

        lb_all, _ = _hg_lower_bound(lb_ref[...])
        gnv = gn_ref[...]
        ri = lax.broadcasted_iota(jnp.int32, (C, C), 0)
        ci_ = lax.broadcasted_iota(jnp.int32, (C, C), 1)
        low = ri >= ci_
        tri = jnp.where(low, 1.0, 0.0).astype(BF)
        rows_w = lax.broadcasted_iota(jnp.int32, (C, W), 0)

        def chunk(ci, carry):
            sl = pl.ds(pl.multiple_of(ci * C, C), C)
            heads = [slice(hh * LANES, (hh + 1) * LANES) for hh in range(HB)]
            q, _, _, _, k, logf = _hg_chunk_common(q_ref[sl, :], fz_ref[sl, :], lb_all)
            vv, gg = v_ref[sl, :], g_ref[sl, :]
            G = _tri_dot(tri, logf)
            Gm = _row_of(G, rows_w, C // 2 - 1)
            Gl = _row_of(G, rows_w, C - 1)
            qt, kt = q * jnp.exp(G - Gm), k * jnp.exp(Gm - G)
            qe, kd, eGl = q * jnp.exp(G), k * jnp.exp(Gl - G), jnp.exp(Gl)
            A = [jnp.where(low, _dg1(qt[:, ls], kt[:, ls], NT), 0.0) for ls in heads]
            Sv = [st[hh] for hh in range(HB)]
            for hh in range(HB):
                sts_ref[hh, ci] = Sv[hh]
            o = [_dg1(A[hh], vv[:, ls], NN) + _dg1(qe[:, ls], Sv[hh], NT) for hh, ls in enumerate(heads)]
            for hh, ls in enumerate(heads):
                st[hh] = Sv[hh] * eGl[:, ls] + _dg1(vv[:, ls], kd[:, ls], TN)
            gate = gg * _sigmoid(gg)
            for hh, ls in enumerate(heads):
                r = lax.rsqrt(jnp.mean(o[hh] * o[hh], axis=-1, keepdims=True) + EPS)
                y_ref[sl, ls] = ((o[hh] * r * gnv) * gate[:, ls]).astype(BF)
                o_ref[sl, ls] = o[hh]
            return carry

        lax.fori_loop(0, nch, chunk, 0)

        if fused:
            finish()

    def part(p):
        return pl.BlockSpec((T, W), lambda h, n: (n, p * ng + h))

    blk = pl.BlockSpec((T, W), lambda h, n: (n, h))
    in_specs = [part(0), part(1), part(2), part(3),
                pl.BlockSpec((3, W), lambda h, n: (0, h)), pl.BlockSpec((1, LANES), lambda h, n: (0, 0))]
    out_specs = [blk, blk, pl.BlockSpec((HB, nch, LANES, LANES), lambda h, n: (h, n, 0, 0))]
    out_shape = [jax.ShapeDtypeStruct((S, D), BF), jax.ShapeDtypeStruct((S, D), F32),
                 jax.ShapeDtypeStruct((H, S // C, LANES, LANES), F32)]
    scratch = [pltpu.VMEM((HB, LANES, LANES), F32)]
    args = [proj, proj, proj, proj, hg_lb, gn]
    if fused:
        in_specs.append(HBM)
        out_specs.append(HBM)
        out_shape.append(jax.ShapeDtypeStruct((N_CHIPS,) + slab.shape, slab.dtype))
        scratch += [pltpu.SemaphoreType.DMA((6,)), pltpu.SemaphoreType.DMA((6,))]
        args.append(slab)
    return pl.pallas_call(
        body, name=name, grid=(ng, nb), in_specs=in_specs, out_specs=out_specs, out_shape=out_shape,
        scratch_shapes=scratch, compiler_params=_cp("arbitrary", "arbitrary"),
    )(*args)


def _hg_bwd(proj, hg_lb, gn, o_all, states, dy, part=None, *, name):
    S = proj.shape[0]
    D = proj.shape[1] // 4
    H = D // LANES
    HB = min(HG_HEADS_PER_STEP, H)
    W = HB * LANES
    C = HG_CHUNK
    T = _pick(S, HG_TOKENS_PER_STEP, C)
    nch, nb = T // C, S // T
    ng = H // HB
    fused = part is not None

    def body(q_ref, fz_ref, v_ref, g_ref, lb_ref, gn_ref, o_ref, sts_ref, dy_ref, *rest):
        if fused:
            p_ref, dp_ref, dlb_ref, dgn_ref, recv_ref, dst, dlb_acc, send_sems, recv_sems = rest
            copies = _scatter_copies(p_ref, recv_ref, send_sems, recv_sems)

            @pl.when((pl.program_id(0) == 0) & (pl.program_id(1) == 0))
            def _():
                for cp in copies:
                    cp.start()
        else:
            dp_ref, dlb_ref, dgn_ref, dst, dlb_acc = rest
        n = pl.program_id(1)

        @pl.when(n == 0)
        def _():
            dst[...] = jnp.zeros_like(dst)
            dlb_acc[...] = jnp.zeros_like(dlb_acc)
            dgn_ref[...] = jnp.zeros_like(dgn_ref)

        lb_all, p3 = _hg_lower_bound(lb_ref[...])
        gnv = gn_ref[...]
        ri = lax.broadcasted_iota(jnp.int32, (C, C), 0)
        ci_ = lax.broadcasted_iota(jnp.int32, (C, C), 1)
        low = ri >= ci_
        tri = jnp.where(low, 1.0, 0.0).astype(BF)
        triu = jnp.where(ri <= ci_, 1.0, 0.0).astype(BF)
        rows_w = lax.broadcasted_iota(jnp.int32, (C, W), 0)
        gnw = jnp.tile(gnv, (1, HB))

        def chunk(cj, carry):
            ci = nch - 1 - cj
            sl = pl.ds(pl.multiple_of(ci * C, C), C)
            heads = list(enumerate(slice(hh * LANES, (hh + 1) * LANES) for hh in range(HB)))
            wide = lambda parts: jnp.concatenate(parts, axis=1)
            qr, vv, gg = q_ref[sl, :], v_ref[sl, :], g_ref[sl, :]
            q, sq, sig, f, k, logf = _hg_chunk_common(qr, fz_ref[sl, :], lb_all)
            G = _tri_dot(tri, logf)
            Gm = _row_of(G, rows_w, C // 2 - 1)
            Gl = _row_of(G, rows_w, C - 1)
            eG, e_qm, e_km, e_lk, eGl = jnp.exp(G), jnp.exp(G - Gm), jnp.exp(Gm - G), jnp.exp(Gl - G), jnp.exp(Gl)
            qt, kt, kdec, qe = q * e_qm, k * e_km, k * e_lk, q * eG
            sg = _sigmoid(gg)
            d_onw = dy_ref[sl, :] * (gg * sg)
            u = d_onw * gnw
            o = o_ref[sl, :]
            on, do = [], []
            for hh, ls in heads:
                r = lax.rsqrt(jnp.mean(o[:, ls] * o[:, ls], axis=-1, keepdims=True) + EPS)
                on.append(o[:, ls] * r)
                dgn_ref[hh] += jnp.sum(d_onw[:, ls] * on[hh], axis=0, keepdims=True)
                do.append(r * (u[:, ls] - on[hh] * jnp.mean(u[:, ls] * on[hh], axis=-1, keepdims=True)))
            dgg = dy_ref[sl, :] * (wide(on) * gnw) * (sg * (1.0 + gg * (1.0 - sg)))
            Sv = [sts_ref[hh, ci] for hh, _ in heads]
            dSv = [dst[hh] for hh, _ in heads]
            A = [jnp.where(low, _dg1(qt[:, ls], kt[:, ls], NT), 0.0) for _, ls in heads]
            dA = [jnp.where(low, _dg3(do[hh], vv[:, ls], NT), 0.0) for hh, ls in heads]
            dv = wide([_dg1(A[hh], do[hh], TN) + _dg1(kdec[:, ls], dSv[hh], NT) for hh, ls in heads])
            dq = wide([_dg3(dA[hh], kt[:, ls], NN) for hh, ls in heads]) * e_qm \
                + eG * wide([_dg3(do[hh], Sv[hh], NN) for hh, _ in heads])
            dk = wide([_dg3(dA[hh], qt[:, ls], TN) for hh, ls in heads]) * e_km \
                + e_lk * wide([_dg3(vv[:, ls], dSv[hh], NN) for hh, ls in heads])
            s_end = [Sv[hh] * eGl[:, ls] + _dg3(vv[:, ls], kdec[:, ls], TN) for hh, ls in heads]
            dgl = wide([jnp.sum(dSv[hh] * s_end[hh], axis=0, keepdims=True) for hh, _ in heads])
            for hh, ls in heads:
                dst[hh] = dSv[hh] * eGl[:, ls] + _dg1(do[hh], qe[:, ls], TN)
            dG = q * dq - k * dk + jnp.where(rows_w == C - 1, dgl, 0.0)
            dlogf = _tri_dot(triu, dG) - f * dk
            dlf_f = dlogf / f
            dlb_acc[...] += jnp.sum(dlf_f * (1.0 - sig), axis=0, keepdims=True)
            dp_ref[0, sl, :] = (dq * (sq * (1.0 + qr * (1.0 - sq)))).astype(BF)
            dp_ref[1, sl, :] = (dlf_f * (1.0 - lb_all) * sig * (1.0 - sig)).astype(BF)
            dp_ref[2, sl, :] = dv.astype(BF)
            dp_ref[3, sl, :] = dgg.astype(BF)
            return carry

        lax.fori_loop(0, nch, chunk, 0)
        sel = jnp.where(lax.broadcasted_iota(jnp.int32, (3, W), 0) == 0, 1.0, 0.0)
        dlb_ref[...] = lb_all * (sel - p3) * dlb_acc[...]

        if fused:
            @pl.when((pl.program_id(0) == ng - 1) & (n == nb - 1))
            def _():
                for cp in copies:
                    cp.wait()

    def col(p):
        return pl.BlockSpec((T, W), lambda h, n: (nb - 1 - n, p * ng + h))

    blk = pl.BlockSpec((T, W), lambda h, n: (nb - 1 - n, h))
    in_specs = [col(0), col(1), col(2), col(3),
                pl.BlockSpec((3, W), lambda h, n: (0, h)), pl.BlockSpec((1, LANES), lambda h, n: (0, 0)),
                blk, pl.BlockSpec((HB, nch, LANES, LANES), lambda h, n: (h, nb - 1 - n, 0, 0)), blk]
    out_specs = [pl.BlockSpec((4, T, W), lambda h, n: (0, nb - 1 - n, h)),
                 pl.BlockSpec((3, W), lambda h, n: (0, h)),
                 pl.BlockSpec((HB, 1, LANES), lambda h, n: (h, 0, 0))]
    out_shape = [jax.ShapeDtypeStruct((4, S, D), BF), jax.ShapeDtypeStruct((3, D), F32),
                 jax.ShapeDtypeStruct((H, 1, LANES), F32)]
    scratch = [pltpu.VMEM((HB, LANES, LANES), F32), pltpu.VMEM((1, W), F32)]
    args = [proj, proj, proj, proj, hg_lb, gn, o_all, states, dy]
    if fused:
        in_specs.append(HBM)
        out_specs.append(HBM)
        out_shape.append(jax.ShapeDtypeStruct((3,) + part.shape[1:], part.dtype))
        scratch += [pltpu.SemaphoreType.DMA((3,)), pltpu.SemaphoreType.DMA((3,))]
        args.append(part)
    return pl.pallas_call(
        body, name=name, grid=(ng, nb), in_specs=in_specs, out_specs=out_specs, out_shape=out_shape,
        scratch_shapes=scratch, compiler_params=_cp("arbitrary", "arbitrary"),
    )(*args)


def _log_sigmoid(u):
    return jnp.minimum(u, 0.0) - jnp.log(1.0 + jnp.exp(-jnp.abs(u)))


def _lane_put(base, lane, first, pieces):
    for n, p in enumerate(pieces):
        base = jnp.where(lane == first + n, p, base)
    return base


def _fox_cumsum(proj, bf_pad, *, name):
    S = proj.shape[0]
    D = proj.shape[1] // 5
    T = _pick(S, 256, 8)

    def body(fz_ref, b_ref, f_ref, carry):
        @pl.when(pl.program_id(0) == 0)
        def _():
            carry[...] = jnp.zeros_like(carry)

        logf = _log_sigmoid(fz_ref[...] + b_ref[...])
        tri = jnp.where(lax.broadcasted_iota(jnp.int32, (T, T), 0) >= lax.broadcasted_iota(jnp.int32, (T, T), 1),
                        1.0, 0.0).astype(BF)
        fv = _tri_dot(tri, logf) + carry[...]
        f_ref[...] = fv
        carry[...] = _row_of(fv, lax.broadcasted_iota(jnp.int32, (T, LANES), 0), T - 1)

    return pl.pallas_call(
        body, name=name, grid=(S // T,),
        in_specs=[pl.BlockSpec((T, LANES), lambda i: (i, 4 * D // LANES)), pl.BlockSpec((1, LANES), lambda i: (0, 0))],
        out_specs=pl.BlockSpec((T, LANES), lambda i: (i, 0)),
        out_shape=jax.ShapeDtypeStruct((S, LANES), F32),
        scratch_shapes=[pltpu.VMEM((1, LANES), F32)],
        compiler_params=_cp("arbitrary"),
    )(proj, bf_pad)


def _pair_stats(sq, lo):
    del lo
    a = lax.broadcasted_iota(jnp.int32, (LANES, LANES), 0) < FOX_DH
    b = lax.broadcasted_iota(jnp.int32, (LANES, LANES), 1) < FOX_DH
    avg = jnp.where(a == b, 1.0 / FOX_DH, 0.0).astype(BF)
    hi, mid, low = _split3(sq)
    return _dot(hi, avg) + _dot(mid, avg) + _dot(low, avg)


def _fox_prep(proj, fcum, qw2, kw2, *, name):
    S = proj.shape[0]
    D = proj.shape[1] // 5
    HP = D // LANES
    T = _pick(S, FOX_ROWS_PER_STEP, 16)

    def body(q_ref, k_ref, v_ref, f_ref, qw_ref, kw_ref, qa_ref, ka_ref, va_ref, vt_ref):
        hp = pl.program_id(1)
        lane = lax.broadcasted_iota(jnp.int32, (T, LANES), 1)
        lo = lane < FOX_DH
        qv, kv, vv, fv = q_ref[...], k_ref[...], v_ref[...], f_ref[...]
        qn = qv * lax.rsqrt(_pair_stats(qv * qv, lo) + EPS) * qw_ref[...] * (0.125 * LOG2E)
        kn = kv * lax.rsqrt(_pair_stats(kv * kv, lo) + EPS) * kw_ref[...]
        ones_q = jnp.where((lane >= 67) & (lane <= 69), 1.0, 0.0)
        ones_k = jnp.where(((lane >= 64) & (lane <= 66)) | ((lane >= 70) & (lane <= 72)), 1.0, 0.0)
        ones_v = jnp.where((lane >= 64) & (lane <= 66), 1.0, 0.0)
        for hh in range(2):
            fh = jnp.sum(jnp.where(lane == 2 * hp + hh, fv, 0.0), axis=-1, keepdims=True) * LOG2E
            pieces = [p.astype(F32) for p in _split3(fh)]

            def half(x):
                return jnp.where(lo, x if hh == 0 else pltpu.roll(x, FOX_DH, 1), 0.0)

            qa_ref[hh] = _lane_put(half(qn) + ones_q, lane, 64, pieces).astype(BF)
            ka_ref[hh] = _lane_put(half(kn) + ones_k, lane, 67, [-p for p in pieces]).astype(BF)
            va = half(vv) + ones_v
            va_ref[hh] = va.astype(BF)
            vt_ref[hh] = va.T.astype(BF)

    def part(p):
        return pl.BlockSpec((T, LANES), lambda i, hp: (i, p * HP + hp))

    vec = pl.BlockSpec((1, LANES), lambda i, hp: (0, 0))
    aug = pl.BlockSpec((2, T, LANES), lambda i, hp: (hp, i, 0))
    return pl.pallas_call(
        body, name=name, grid=(S // T, HP),
        in_specs=[part(0), part(1), part(2), pl.BlockSpec((T, LANES), lambda i, hp: (i, 0)), vec, vec],
        out_specs=[aug, aug, aug, pl.BlockSpec((2, LANES, T), lambda i, hp: (hp, 0, i))],
        out_shape=[jax.ShapeDtypeStruct((2 * HP, S, LANES), BF)] * 3 + [jax.ShapeDtypeStruct((2 * HP, LANES, S), BF)],
        compiler_params=_cp("parallel", "arbitrary"),
    )(proj, proj, proj, fcum, qw2, kw2)


def _fox_block(S):
    return _pick(S, 256, 16)


def _fox_skip_bounds(fcum, qn_w, kn_w, nheads):
    S = fcum.shape[0]
    B = _fox_block(S)
    qk = 8.0 * LOG2E * 1.02 * jnp.max(jnp.abs(qn_w)) * jnp.max(jnp.abs(kn_w))
    thresh = -(2.0 * qk + 152.0)
    f2 = fcum[:, :nheads] * LOG2E
    first, last = f2[0::B], f2[B - 1::B]
    nb = S // B
    blk = jnp.arange(nb)
    dead = (first[0::2, None, :] - last[None, :, :]) < thresh
    jmin = jnp.sum(dead & (blk[None, :, None] < 2 * jnp.arange(nb // 2)[:, None, None]), axis=1)
    live = (first[:, None, :] - last[None, :, :]) >= thresh
    imax = blk[:, None] + jnp.sum(live & (blk[:, None, None] > blk[None, :, None]), axis=0)
    return jmin.T.astype(jnp.int32), imax.T.astype(jnp.int32)


def _fox_fwd(jmin, qa, ka, vat, proj, *, name):
    H, S, _ = qa.shape
    HP = H // 2
    D = HP * LANES
    B = _fox_block(S)
    BQ = 2 * B
    nq = S // BQ

    def body(jmin_ref, q_ref, k_ref, vt_ref, g_ref, y_ref, o_ref, q2_ref):
        hp, i = pl.program_id(0), pl.program_id(1)
        lane = lax.broadcasted_iota(jnp.int32, (BQ, LANES), 1)
        lo = lane < FOX_DH
        in_stat = (lane >= 70) & (lane <= 75)
        causal = lax.broadcasted_iota(jnp.int32, (BQ, BQ), 0) <= lax.broadcasted_iota(jnp.int32, (BQ, BQ), 1)
        row = lax.broadcasted_iota(jnp.int32, (LANES, BQ), 0)
        m0, acc0 = jnp.full((1, BQ), -jnp.inf, F32), jnp.zeros((LANES, BQ), F32)
        outs = []
        for hh in range(2):
            qb = q_ref[hh]

            def scores(j):
                sl = pl.ds(pl.multiple_of(j * BQ, BQ), BQ)
                return _dg(k_ref[hh, sl, :], qb, NT)

            def update(j, m, acc, st, masked=False):
                sl = pl.ds(pl.multiple_of(j * BQ, BQ), BQ)
                if masked:
                    st = jnp.where(causal, st, -jnp.inf)
                m_new = jnp.maximum(m, jnp.ceil(jnp.max(st, axis=0, keepdims=True)))
                p = jnp.exp2(st - m_new).astype(BF)
                return m_new, acc * jnp.exp2(m - m_new) + _dot(vt_ref[hh, :, sl], p)

            def step(j, carry):
                m, acc, st = carry
                st_next = scores(j + 1)
                return update(j, m, acc, st) + (st_next,)

            first = jmin_ref[2 * hp + hh, i] // 2
            m, acc, st = lax.fori_loop(first, i, step, (m0, acc0, scores(first)))
            m, acc = update(i, m, acc, st, masked=True)
            linv = 1.0 / jnp.sum(jnp.where(row == FOX_DH, acc, 0.0), axis=0, keepdims=True)
            tile = acc * linv
            for n, piece in enumerate(_split3(m) + _split3(linv)):
                tile = jnp.where(row == 70 + n, piece.astype(F32), tile)
            tile = tile.T
            outs.append(tile)
            q2_ref[hh] = jnp.where(in_stat, jnp.where(lane <= 72, -tile, tile), qb.astype(F32)).astype(BF)
        o = jnp.where(lo, outs[0], pltpu.roll(outs[1], FOX_DH, 1))
        o_ref[...] = o
        y_ref[...] = (o * _sigmoid(g_ref[...])).astype(BF)

    blk = pl.BlockSpec((BQ, LANES), lambda hp, i, jm: (i, hp))
    qblk = pl.BlockSpec((2, BQ, LANES), lambda hp, i, jm: (hp, i, 0))
    full = pl.BlockSpec((2, S, LANES), lambda hp, i, jm: (hp, 0, 0))
    full_t = pl.BlockSpec((2, LANES, S), lambda hp, i, jm: (hp, 0, 0))
    return pl.pallas_call(
        body, name=name,
        grid_spec=pltpu.PrefetchScalarGridSpec(
            num_scalar_prefetch=1, grid=(HP, nq),
            in_specs=[qblk, full, full_t, pl.BlockSpec((BQ, LANES), lambda hp, i, jm: (i, 3 * HP + hp))],
            out_specs=[blk, blk, qblk]),
        out_shape=[jax.ShapeDtypeStruct((S, D), BF), jax.ShapeDtypeStruct((S, D), F32),
                   jax.ShapeDtypeStruct((H, S, LANES), BF)],
        compiler_params=_cp("parallel", "arbitrary"),
    )(jmin, qa, ka, vat, proj)


def _fox_bwd_prep(dy, o, proj, q2, *, name):
    S, D = dy.shape
    HP = D // LANES
    T = _pick(S, FOX_ROWS_PER_STEP, 16)

    def body(dy_ref, o_ref, g_ref, q2_ref, da_ref):
        lane = lax.broadcasted_iota(jnp.int32, (T, LANES), 1)
        lo = lane < FOX_DH
        in_linv = (lane >= 73) & (lane <= 75)
        linv = [jnp.sum(jnp.where(in_linv, q2_ref[hh].astype(F32), 0.0), axis=-1, keepdims=True) for hh in range(2)]
        u = (dy_ref[...] * _sigmoid(g_ref[...]) * jnp.where(lo, linv[0], linv[1])).astype(BF).astype(F32)
        prod = u * o_ref[...]
        d_lo = jnp.sum(jnp.where(lo, prod, 0.0), axis=-1, keepdims=True)
        d_hi = jnp.sum(jnp.where(lo, 0.0, prod), axis=-1, keepdims=True)
        for hh, delta in enumerate((d_lo, d_hi)):
            base = jnp.where(lo, u if hh == 0 else pltpu.roll(u, FOX_DH, 1), 0.0)
            da_ref[hh] = _lane_put(base, lane, 64, [-(p.astype(F32)) for p in _split3(delta)]).astype(BF)

    blk = pl.BlockSpec((T, LANES), lambda i, hp: (i, hp))
    aug = pl.BlockSpec((2, T, LANES), lambda i, hp: (hp, i, 0))
    return pl.pallas_call(
        body, name=name, grid=(S // T, HP),
        in_specs=[blk, blk, pl.BlockSpec((T, LANES), lambda i, hp: (i, 3 * HP + hp)), aug],
        out_specs=aug,
        out_shape=jax.ShapeDtypeStruct((2 * HP, S, LANES), BF),
        compiler_params=_cp("parallel", "arbitrary"),
    )(dy, o, proj, q2)


def _fox_bwd(imax, q2, ka, va, doa, *, name):
    H, S, _ = q2.shape
    B = _fox_block(S)
    nb = S // B

    def body(imax_ref, q_ref, do_ref, k_ref, v_ref, dq_ref, dk_ref, dv_ref, cs_ref):
        j = pl.program_id(1)
        end = imax_ref[pl.program_id(0), j] + 1

        @pl.when(j == 0)
        def _():
            dq_ref[...] = jnp.zeros_like(dq_ref)

        kb, vb = k_ref[...], v_ref[...]

        def step(i, carry, nblk=1):
            dk_acc, dv_acc, cs_acc = carry
            rows = nblk * B
            sl = pl.ds(pl.multiple_of(i * B, B), rows)
            qb, dob = q_ref[sl, :], do_ref[sl, :]
            s = _dg(qb, kb, NT)
            ahead = lax.broadcasted_iota(jnp.int32, (rows, B), 0) - lax.broadcasted_iota(jnp.int32, (rows, B), 1)
            pb = jnp.exp2(jnp.where(ahead >= (j - i) * B, s, -jnp.inf)).astype(BF)
            ds = pb.astype(F32) * _dg(dob, vb, NT)
            dsb = ds.astype(BF)
            cs_acc = cs_acc + jnp.sum(ds.reshape(rows // 8, 8, B), axis=0)
            dv_acc = dv_acc + _dg(pb, dob, TN)
            dk_acc = dk_acc + _dg(dsb, qb, TN)
            dq_ref[sl, :] += _dot(dsb, kb)
            return dk_acc, dv_acc, cs_acc

        zero = jnp.zeros((B, LANES), F32)
        carry = (zero, zero, jnp.zeros((8, B), F32))
        pos = j
        for U in FOX_BWD_TILES:
            n = (end - pos) // U
            carry = lax.fori_loop(0, n, lambda ii, c, pos=pos, U=U: step(pos + U * ii, c, nblk=U), carry)
            pos = pos + U * n
        dk_acc, dv_acc, cs_acc = carry
        dk_ref[...] = dk_acc
        dv_ref[...] = dv_acc
        cs_ref[...] = jnp.sum(cs_acc, axis=0, keepdims=True)

    full = pl.BlockSpec((None, S, LANES), lambda h, j, im: (h, 0, 0))
    blk = pl.BlockSpec((None, B, LANES), lambda h, j, im: (h, j, 0))
    return pl.pallas_call(
        body, name=name,
        grid_spec=pltpu.PrefetchScalarGridSpec(
            num_scalar_prefetch=1, grid=(H, nb),
            in_specs=[full, full, blk, blk],
            out_specs=[full, blk, blk, pl.BlockSpec((None, 1, B), lambda h, j, im: (h, 0, j))]),
        out_shape=[jax.ShapeDtypeStruct((H, S, LANES), F32)] * 3 + [jax.ShapeDtypeStruct((H, 1, S), F32)],
        compiler_params=_cp("parallel", "arbitrary"),
    )(imax, q2, doa, ka, va)


def _fox_bwd_post(dqa, dka, dva, proj, dy, o, qw2, kw2, *, name):
    S, D = dy.shape
    HP = D // LANES
    T = _pick(S, FOX_ROWS_PER_STEP, 16)

    def body(dq_ref, dk_ref, dv_ref, q_ref, k_ref, g_ref, dy_ref, o_ref, qw_ref, kw_ref, dp_ref, dqw_ref, dkw_ref):
        @pl.when((pl.program_id(0) == 0) & (pl.program_id(1) == 0))
        def _():
            dqw_ref[...] = jnp.zeros_like(dqw_ref)
            dkw_ref[...] = jnp.zeros_like(dkw_ref)

        lane = lax.broadcasted_iota(jnp.int32, (T, LANES), 1)
        lo = lane < FOX_DH

        def pair(ref):
            return jnp.where(lo, ref[0], pltpu.roll(ref[1], FOX_DH, 1))

        def norm_bwd(xv, w, dyn, dw_ref):
            r = lax.rsqrt(_pair_stats(xv * xv, lo) + EPS)
            xr = xv * r
            dw_ref[...] += jnp.sum(dyn * xr, axis=0, keepdims=True)
            u = dyn * w
            return r * (u - xr * _pair_stats(u * xr, lo))

        dp_ref[0] = norm_bwd(q_ref[...], qw_ref[...], pair(dq_ref) * 0.125, dqw_ref).astype(BF)
        dp_ref[1] = norm_bwd(k_ref[...], kw_ref[...], pair(dk_ref) * (1.0 / LOG2E), dkw_ref).astype(BF)
        dp_ref[2] = pair(dv_ref).astype(BF)
        sg = _sigmoid(g_ref[...])
        dp_ref[3] = (dy_ref[...] * o_ref[...] * sg * (1.0 - sg)).astype(BF)

    def part(p):
        return pl.BlockSpec((T, LANES), lambda i, hp: (i, p * HP + hp))

    aug = pl.BlockSpec((2, T, LANES), lambda i, hp: (hp, i, 0))
    blk = pl.BlockSpec((T, LANES), lambda i, hp: (i, hp))
    vec = pl.BlockSpec((1, LANES), lambda i, hp: (0, 0))
    return pl.pallas_call(
        body, name=name, grid=(S // T, HP),
        in_specs=[aug, aug, aug, part(0), part(1), part(3), blk, blk, vec, vec],
        out_specs=[pl.BlockSpec((4, T, LANES), lambda i, hp: (0, i, hp)), vec, vec],
        out_shape=[jax.ShapeDtypeStruct((5, S, D), BF), jax.ShapeDtypeStruct((1, LANES), F32),
                   jax.ShapeDtypeStruct((1, LANES), F32)],
        compiler_params=_cp("arbitrary", "arbitrary"),
    )(dqa, dka, dva, proj, proj, proj, dy, o, qw2, kw2)


def _fox_dfz(colsum, nheads, proj, bf_pad, dproj, *, name):
    S = colsum.shape[0]
    H = nheads
    D = dproj.shape[2]
    T = _pick(S, 256, 16)
    nb = S // T

    def body(cs_ref, fz_ref, b_ref, _, dp_ref, db_ref, carry):
        @pl.when(pl.program_id(0) == 0)
        def _():
            carry[...] = jnp.zeros_like(carry)
            db_ref[...] = jnp.zeros_like(db_ref)

        lane = lax.broadcasted_iota(jnp.int32, (T, LANES), 1)
        df = -cs_ref[...]
        triu = jnp.where(lax.broadcasted_iota(jnp.int32, (T, T), 0) <= lax.broadcasted_iota(jnp.int32, (T, T), 1),
                         1.0, 0.0).astype(BF)
        dlogf = _tri_dot(triu, df) + carry[...]
        carry[...] = _row_of(dlogf, lax.broadcasted_iota(jnp.int32, (T, LANES), 0), 0)
        dfz = jnp.where(lane < H, dlogf * _sigmoid(-(fz_ref[...] + b_ref[...])), 0.0)
        db_ref[...] += jnp.sum(dfz, axis=0, keepdims=True)
        dp_ref[...] = jnp.zeros_like(dp_ref)
        dp_ref[:, 0:LANES] = dfz.astype(BF)

    return pl.pallas_call(
        body, name=name, grid=(nb,),
        in_specs=[pl.BlockSpec((T, LANES), lambda i: (nb - 1 - i, 0)),
                  pl.BlockSpec((T, LANES), lambda i: (nb - 1 - i, 4 * D // LANES)),
                  pl.BlockSpec((1, LANES), lambda i: (0, 0)),
                  pl.BlockSpec(memory_space=pl.ANY)],
        out_specs=[pl.BlockSpec((None, T, D), lambda i: (4, nb - 1 - i, 0)), pl.BlockSpec((1, LANES), lambda i: (0, 0))],
        out_shape=[jax.ShapeDtypeStruct(dproj.shape, BF), jax.ShapeDtypeStruct((1, LANES), F32)],
        scratch_shapes=[pltpu.VMEM((1, LANES), F32)],
        input_output_aliases={3: 0},
        compiler_params=_cp("arbitrary"),
    )(colsum, proj, bf_pad, dproj)


def _mod_fwd(c16, w, b, *, name):
    L, D, N = w.shape
    tn = _pick(N, 512)

    def body(c_ref, w_ref, b_ref, o_ref):
        cv = c_ref[...]
        ca = (cv * _sigmoid(cv)).astype(BF)
        o_ref[...] = _dot(ca, w_ref[...].astype(BF)) + b_ref[...]

    return pl.pallas_call(
        body, name=name, grid=(L, N // tn),
        in_specs=[pl.BlockSpec((16, D), lambda l, j: (0, 0)), pl.BlockSpec((None, D, tn), lambda l, j: (l, 0, j)),
                  pl.BlockSpec((None, 1, tn), lambda l, j: (l, 0, j))],
        out_specs=pl.BlockSpec((None, 16, tn), lambda l, j: (l, 0, j)),
        out_shape=jax.ShapeDtypeStruct((L, 16, N), F32),
        compiler_params=_cp("parallel", "arbitrary"),
    )(c16, w, b)


def _mod_bwd(c16, dmod, *, name):
    L, _, N = dmod.shape
    D = c16.shape[1]
    tn = _pick(N, 512)

    def body(c_ref, d_ref, o_ref):
        cv = c_ref[...]
        ca = (cv * _sigmoid(cv)).astype(BF)
        o_ref[...] = _dg(ca, d_ref[...].astype(BF), TN)

    return pl.pallas_call(
        body, name=name, grid=(L, N // tn),
        in_specs=[pl.BlockSpec((16, D), lambda l, j: (0, 0)), pl.BlockSpec((None, 16, tn), lambda l, j: (l, 0, j))],
        out_specs=pl.BlockSpec((None, D, tn), lambda l, j: (l, 0, j)),
        out_shape=jax.ShapeDtypeStruct((L, D, N), F32),
        compiler_params=_cp("parallel", "arbitrary"),
    )(c16, dmod)


def _adamw_math(w, g, m, v):
    m = ADAM_B1 * m + (1.0 - ADAM_B1) * g
    v = ADAM_B2 * v + (1.0 - ADAM_B2) * (g * g)
    m_hat = m / (1.0 - ADAM_B1 ** ADAM_STEP)
    v_hat = v / (1.0 - ADAM_B2 ** ADAM_STEP)
    return -ADAM_LR * (m_hat / (jnp.sqrt(v_hat) + ADAM_EPS) + ADAM_WD * w), m, v


def _adamw(w, g, m, v, *, g_at=None, name):
    R, C = w.shape
    row0 = 0 if g_at is None else g_at[1]
    tr = min(math.gcd(row0, 256) if row0 else 256, -(-R // 8) * 8)
    g0 = row0 // tr
    if g_at is None:
        g_spec = pl.BlockSpec((tr, C), lambda i: (i, 0))
    else:
        g_spec = pl.BlockSpec((None, tr, C), lambda i: (g_at[0], g0 + i, 0))

    def body(w_ref, g_ref, m_ref, v_ref, d_ref, mo_ref, vo_ref):
        d, mn, vn = _adamw_math(w_ref[...], g_ref[...], m_ref[...], v_ref[...])
        d_ref[...] = d
        mo_ref[...] = mn
        vo_ref[...] = vn

    blk = pl.BlockSpec((tr, C), lambda i: (i, 0))
    return pl.pallas_call(
        body, name=name, grid=(pl.cdiv(R, tr),),
        in_specs=[blk, g_spec, blk, blk],
        out_specs=[blk, blk, blk],
        out_shape=[jax.ShapeDtypeStruct((R, C), F32)] * 3,
        compiler_params=_cp("parallel"),
    )(w, g, m, v)


def _sum_parts(parts, *, name):
    P, R, C = parts.shape

    def body(p_ref, o_ref):
        acc = p_ref[0]
        for p in range(1, P):
            acc = acc + p_ref[p]
        o_ref[...] = acc

    return pl.pallas_call(
        body, name=name, grid=(1,),
        in_specs=[pl.BlockSpec((P, R, C), lambda i: (0, 0, 0))],
        out_specs=pl.BlockSpec((R, C), lambda i: (0, 0)),
        out_shape=jax.ShapeDtypeStruct((R, C), F32),
        compiler_params=_cp("arbitrary"),
    )(parts)


def _add_halves(g4, recv, c_idx, *, name):
    _, _, Rh, C = g4.shape
    tr = min(256, Rh)

    def body(c_ref, a_ref, b_ref, o_ref):
        o_ref[...] = (a_ref[...] + b_ref[...].astype(F32)).astype(BF)

    return pl.pallas_call(
        body, name=name,
        grid_spec=pltpu.PrefetchScalarGridSpec(
            num_scalar_prefetch=1, grid=(4, pl.cdiv(Rh, tr)),
            in_specs=[pl.BlockSpec((None, None, tr, C), lambda j, r, c: (j, c[0], r, 0)),
                      pl.BlockSpec((None, tr, C), lambda j, r, c: (j, r, 0))],
            out_specs=pl.BlockSpec((None, tr, C), lambda j, r, c: (j, r, 0))),
        out_shape=jax.ShapeDtypeStruct((4, Rh, C), BF),
        compiler_params=_cp("parallel", "arbitrary"),
    )(c_idx, g4, recv)


def _add_four(g4, from_sibling, from_chips, pos, *, name):
    _, _, Rh, C = g4.shape
    tr = min(256, Rh)

    def body(p_ref, a_ref, s_ref, b_ref, o_ref):
        own = a_ref[...] + s_ref[...].astype(F32)
        o_ref[...] = ((own + b_ref[0].astype(F32)) + b_ref[1].astype(F32)) + b_ref[2].astype(F32)

    return pl.pallas_call(
        body, name=name,
        grid_spec=pltpu.PrefetchScalarGridSpec(
            num_scalar_prefetch=1, grid=(pl.cdiv(Rh, tr),),
            in_specs=[pl.BlockSpec((None, None, tr, C), lambda r, p: (p[0], p[1], r, 0)),
                      pl.BlockSpec((None, tr, C), lambda r, p: (p[0], r, 0)),
                      pl.BlockSpec((3, tr, C), lambda r, p: (0, r, 0))],
            out_specs=pl.BlockSpec((None, tr, C), lambda r, p: (p[1], r, 0))),
        out_shape=jax.ShapeDtypeStruct((2, Rh, C), F32),
        compiler_params=_cp("arbitrary"),
    )(pos, g4, from_sibling, from_chips)


HBM = pl.BlockSpec(memory_space=pltpu.HBM)


def _mesh_pos():
    return lax.axis_index("x"), lax.axis_index("y"), lax.axis_index("c")


def _other_chips(x, y):
    return [(1 - x, y), (x, 1 - y), (1 - x, 1 - y)]


def _allgather_small(xs, *, name):
    m_per, n = xs.shape

    def body(x_ref, out_ref, send_sems, recv_sems, local_sem):
        x, y, c = _mesh_pos()
        me, sibling = (x, y, c), (x, y, 1 - c)
        chips = _other_chips(x, y)

        def rows(px, py, pc):
            return out_ref.at[pl.ds((4 * px + 2 * py + pc) * m_per, m_per), :]

        def copy(k, block, to, src=None):
            return pltpu.make_async_remote_copy(
                src_ref=rows(*block) if src is None else src, dst_ref=rows(*block),
                send_sem=send_sems.at[k], recv_sem=recv_sems.at[k], device_id=to, device_id_type=MESH)

        mine = pltpu.make_async_copy(x_ref, rows(*me), local_sem)
        mine.start()
        first = [copy(0, me, sibling, src=x_ref)]
        first += [copy(1 + j, me, (*chip, c), src=x_ref) for j, chip in enumerate(chips)]
        for cp in first:
            cp.start()
        passed = [copy(4 + j, (*chip, c), sibling) for j, chip in enumerate(chips)]
        for j, chip in enumerate(chips):
            copy(1 + j, (*chip, c), me).wait_recv()
            passed[j].start()
        copy(0, sibling, me).wait_recv()
        for j, chip in enumerate(chips):
            copy(4 + j, (*chip, 1 - c), me).wait_recv()
        for cp in first + passed:
            cp.wait_send()
        mine.wait()

    return pl.pallas_call(
        body, name=name,
        out_shape=jax.ShapeDtypeStruct((N_DEV * m_per, n), xs.dtype),
        in_specs=[pl.BlockSpec(memory_space=pltpu.VMEM)],
        out_specs=pl.BlockSpec(memory_space=pltpu.VMEM),
        scratch_shapes=[pltpu.SemaphoreType.DMA((7,)), pltpu.SemaphoreType.DMA((7,)), pltpu.SemaphoreType.DMA],
    )(xs)


def _chip_slab_copies(s_ref, out_ref, send_sems, recv_sems):
    R = s_ref.shape[0]
    Rh = R // 2
    x, y, c = _mesh_pos()
    me, sibling = (x, y, c), (x, y, 1 - c)
    chips = _other_chips(x, y)

    def half(px, py, pc):
        return out_ref.at[2 * px + py, pl.ds(pc * Rh, Rh), :]

    def copy(k, block, to, src=None):
        return pltpu.make_async_remote_copy(
            src_ref=half(*block) if src is None else src, dst_ref=half(*block),
            send_sem=send_sems.at[k], recv_sem=recv_sems.at[k], device_id=to, device_id_type=MESH)

    first = [copy(j, me, (*chip, c), src=s_ref.at[pl.ds(c * Rh, Rh), :]) for j, chip in enumerate(chips)]
    passed = [copy(3 + j, (*chip, c), sibling) for j, chip in enumerate(chips)]
    landed = [copy(j, (*chip, c), me) for j, chip in enumerate(chips)]
    from_sibling = [copy(3 + j, (*chip, 1 - c), me) for j, chip in enumerate(chips)]
    return first, passed, landed, from_sibling


def _gather_behind(s_ref, out_ref, send_sems, recv_sems, step, nsteps):
    first, passed, landed, from_sibling = _chip_slab_copies(s_ref, out_ref, send_sems, recv_sems)

    @pl.when(step == 0)
    def _():
        for cp in first:
            cp.start()

    @pl.when(step == (3 * nsteps) // 4)
    def _():
        for arrived, onward in zip(landed, passed):
            arrived.wait_recv()
            onward.start()

    def finish():
        @pl.when(step == nsteps - 1)
        def _():
            for cp in from_sibling:
                cp.wait_recv()
            for cp in first + passed:
                cp.wait_send()

    return finish


def _allgather_chip_slabs(slab, *, name):
    R, C = slab.shape

    def body(s_ref, out_ref, send_sems, recv_sems):
        first, passed, landed, from_sibling = _chip_slab_copies(s_ref, out_ref, send_sems, recv_sems)
        for cp in first:
            cp.start()
        for arrived, onward in zip(landed, passed):
            arrived.wait_recv()
            onward.start()
        for cp in from_sibling:
            cp.wait_recv()
        for cp in first + passed:
            cp.wait_send()

    return pl.pallas_call(
        body, name=name,
        out_shape=jax.ShapeDtypeStruct((N_CHIPS, R, C), slab.dtype),
        in_specs=[HBM], out_specs=HBM,
        scratch_shapes=[pltpu.SemaphoreType.DMA((6,)), pltpu.SemaphoreType.DMA((6,))],
    )(slab)


def _swap_halves(mine, *, name):
    def body(g_ref, out_ref, send_sems, recv_sems):
        x, y, c = _mesh_pos()
        copies = [pltpu.make_async_remote_copy(
            src_ref=g_ref.at[j], dst_ref=out_ref.at[j], send_sem=send_sems.at[j], recv_sem=recv_sems.at[j],
            device_id=(x, y, 1 - c), device_id_type=MESH) for j in range(N_CHIPS)]
        for cp in copies:
            cp.start()
        for cp in copies:
            cp.wait()

    return pl.pallas_call(
        body, name=name,
        out_shape=jax.ShapeDtypeStruct(mine.shape, mine.dtype),
        in_specs=[HBM], out_specs=HBM,
        scratch_shapes=[pltpu.SemaphoreType.DMA((N_CHIPS,)), pltpu.SemaphoreType.DMA((N_CHIPS,))],
    )(mine)


def _scatter_copies(p_ref, out_ref, send_sems, recv_sems):
    x, y, c = _mesh_pos()
    return [pltpu.make_async_remote_copy(
        src_ref=p_ref.at[2 * px + py], dst_ref=out_ref.at[j], send_sem=send_sems.at[j], recv_sem=recv_sems.at[j],
        device_id=(px, py, c), device_id_type=MESH) for j, (px, py) in enumerate(_other_chips(x, y))]


def _join_halves(buf, *, name):
    def body(b_ref, out_ref, send_sem, recv_sem):
        x, y, c = _mesh_pos()
        cp = pltpu.make_async_remote_copy(
            src_ref=b_ref.at[c], dst_ref=out_ref.at[c], send_sem=send_sem, recv_sem=recv_sem,
            device_id=(x, y, 1 - c), device_id_type=MESH)
        cp.start()
        cp.wait()

    return pl.pallas_call(
        body, name=name,
        out_shape=jax.ShapeDtypeStruct(buf.shape, buf.dtype),
        in_specs=[HBM], out_specs=HBM, input_output_aliases={0: 0},
        scratch_shapes=[pltpu.SemaphoreType.DMA, pltpu.SemaphoreType.DMA],
    )(buf)


def _pad_rows(a, mult):
    pad = (-a.shape[0]) % mult
    return a if pad == 0 else jnp.pad(a, ((0, pad),) + ((0, 0),) * (a.ndim - 1))


def _local_step(x, target, mod, wts, small, slabs=None, unpacks=None, reduce_early=None, grad_slab=None,
                reduce_late=None):
    S, D = x.shape
    HP = D // LANES
    row = lambda v: v.reshape(1, -1)
    msplit = [[row(mod[i, k * D:(k + 1) * D]) for k in range(6)] for i in range(2)]
    gw, gs = {}, {}
    dmod = [[None] * 6 for _ in range(2)]
    slab, where = grad_slab if grad_slab is not None else (None, {})

    def dw(key, a, b, name):
        nonlocal slab
        if key in where:
            slab = _matmul_tn(a, b, name=name, into=(slab,) + where[key])
        else:
            gw[key] = _matmul_tn(a, b, name=name)

    sh1, sc1, g1, sh2, sc2, g2 = msplit[0]
    n1w0, n2w0 = row(small["norm1_w"][0]), row(small["norm2_w"][0])
    slabs = slabs if slabs is not None else (None, None, None)
    proj0, h1_0, *gathered = _ln_matmul(x, n1w0, sc1, sh1, wts["hg_w_in"], slabs[0], relu2=False, name="hg_in_proj")
    if slabs[0] is not None:
        wts = {**wts, **unpacks[0](gathered[0])}
    gn = small["hg_gn_w"].reshape(1, LANES)
    ypre0, o0, states, *gathered = _hg_fwd(proj0, small["hg_lb"], gn, slabs[1], name="hg_fwd")
    if slabs[1] is not None:
        wts = {**wts, **unpacks[1](gathered[0])}
    x1, ymix0 = _matmul_resid(ypre0, wts["hg_w_out"], x, g1, name="hg_out_proj")
    a0, u0, h2_0, *gathered = _ln_matmul(x1, n2w0, sc2, sh2, wts["mlp_w1_0"], slabs[2], relu2=True, name="mlp0_up")
    if slabs[2] is not None:
        wts = {**wts, **unpacks[2](gathered[0])}
    x2, ymlp0 = _matmul_resid(u0, wts["mlp_w2_0"], x1, g2, name="mlp0_down")

    sh1b, sc1b, g1b, sh2b, sc2b, g2b = msplit[1]
    n1w1, n2w1 = row(small["norm1_w"][1]), row(small["norm2_w"][1])
    proj1, h1_1 = _ln_matmul(x2, n1w1, sc1b, sh1b, wts["fox_w_in"], relu2=False, name="fox_in_proj")
    nheads = 2 * HP
    bf_pad = jnp.pad(small["fox_b_f"].reshape(1, nheads), ((0, 0), (0, LANES - nheads)))
    qw2 = jnp.tile(small["fox_qn_w"].reshape(1, FOX_DH), (1, 2))
    kw2 = jnp.tile(small["fox_kn_w"].reshape(1, FOX_DH), (1, 2))
    fcum = _fox_cumsum(proj1, bf_pad, name="fox_cumsum")
    qa, ka, va, vat = _fox_prep(proj1, fcum, qw2, kw2, name="fox_prep")
    jmin, imax = _fox_skip_bounds(fcum, small["fox_qn_w"], small["fox_kn_w"], nheads)
    ypre1, o1, q2 = _fox_fwd(jmin, qa, ka, vat, proj1, name="fox_fwd")
    x3, ymix1 = _matmul_resid(ypre1, wts["fox_w_out"], x2, g1b, name="fox_out_proj")
    a1, u1, h2_1 = _ln_matmul(x3, n2w1, sc2b, sh2b, wts["mlp_w1_1"], relu2=True, name="mlp1_up")
    x4, ymlp1 = _matmul_resid(u1, wts["mlp_w2_1"], x3, g2b, name="mlp1_down")

    loss, dx4, dfw = _loss_kernel(x4, row(small["final_w"]), target, name="loss")
    gs["final_w"] = dfw.reshape(-1)

    def mlp_bwd(i, dx_out, x_in, h2, a, u, ymlp, n2w, sc2_, g2_):
        dz, dm, dg2 = _gate_matmul_nt(dx_out, g2_, ymlp, wts[f"mlp_w2_{i}"], a, name=f"mlp{i}_down_bwd")
        dw(f"mlp_w2_{i}", u, dm[None], f"mlp{i}_dw2")
        dw(f"mlp_w1_{i}", h2, dz[None], f"mlp{i}_dw1")
        dx_in, dsc, dsh, dnw = _matmul_nt_lnbwd(dz[None], wts[f"mlp_w1_{i}"], x_in, n2w, sc2_, dx_out,
                                                name=f"mlp{i}_up_bwd")
        dmod[i][3], dmod[i][4], dmod[i][5] = dsh, dsc, dg2
        return dx_in, dnw

    dx3, dn2w1 = mlp_bwd(1, dx4, x3, h2_1, a1, u1, ymlp1, n2w1, sc2b, g2b)
    dyp1, dm1, dg1b = _gate_matmul_nt(dx3, g1b, ymix1, wts["fox_w_out"], None, name="fox_out_bwd")
    dw("fox_w_out", ypre1, dm1[None], "fox_dw_out")
    doa = _fox_bwd_prep(dyp1, o1, proj1, q2, name="fox_bwd_prep")
    dqa, dka, dva, colsum = _fox_bwd(imax, q2, ka, va, doa, name="fox_bwd")
    colsum = jnp.pad(colsum[:, 0, :].T, ((0, 0), (0, LANES - nheads)))
    dproj1, dqw, dkw = _fox_bwd_post(dqa, dka, dva, proj1, dyp1, o1, qw2, kw2, name="fox_bwd_post")
    dproj1, dbf = _fox_dfz(colsum, nheads, proj1, bf_pad, dproj1, name="fox_dfz")
    dw("fox_w_in", h1_1, dproj1, "fox_dw_in")
    dx2, dsc, dsh, dn1w1 = _matmul_nt_lnbwd(dproj1, wts["fox_w_in"], x2, n1w1, sc1b, dx3, name="fox_in_bwd")
    dmod[1][0], dmod[1][1], dmod[1][2] = dsh, dsc, dg1b
    gs["fox_qn_w"] = dqw[0, :FOX_DH] + dqw[0, FOX_DH:]
    gs["fox_kn_w"] = dkw[0, :FOX_DH] + dkw[0, FOX_DH:]
    gs["fox_b_f"] = dbf[0, :nheads]

    dx1, dn2w0 = mlp_bwd(0, dx2, x1, h2_0, a0, u0, ymlp0, n2w0, sc2, g2)
    dyp0, dm0, dg1 = _gate_matmul_nt(dx1, g1, ymix0, wts["hg_w_out"], None, name="hg_out_bwd")
    dw("hg_w_out", ypre0, dm0[None], "hg_dw_out")
    part, ctx = reduce_early(gw, slab) if reduce_early is not None else (None, None)
    dproj0, dlb, dgn, *from_chips = _hg_bwd(proj0, small["hg_lb"], gn, o0, states, dyp0, part, name="hg_bwd")
    early = (ctx, from_chips[0]) if reduce_early is not None else None
    dw("hg_w_in", h1_0, dproj0, "hg_dw_in")
    part, ctx = reduce_late(gw) if reduce_late is not None else (None, None)
    dx0, dsc, dsh, dn1w0, *from_chips = _matmul_nt_lnbwd(dproj0, wts["hg_w_in"], x, n1w0, sc1, dx1, part, name="hg_in_bwd")
    late = (ctx, from_chips[0]) if reduce_late is not None else None
    dmod[0][0], dmod[0][1], dmod[0][2] = dsh, dsc, dg1
    gs["hg_lb"] = dlb
    gs["hg_gn_w"] = jnp.sum(dgn, axis=0)

    gs["norm1_w"] = jnp.concatenate([dn1w0, dn1w1], axis=0)
    gs["norm2_w"] = jnp.concatenate([dn2w0, dn2w1], axis=0)
    gs["dmod"] = jnp.stack([jnp.concatenate(dmod[i], axis=1)[0] for i in range(2)])
    return loss, dx0, gw, gs, early, late


def _pack_halves(layout):
    rh = -(-max(sum(a.shape[0] for _, a in half) for half in layout) // 16) * 16
    place, parts = {}, []
    for h, half in enumerate(layout):
        off = 0
        for n, a in half:
            place[n] = (h, off, a.shape[0])
            off += a.shape[0]
        parts.append(jnp.pad(jnp.concatenate([a.astype(BF) for _, a in half], axis=0), ((0, rh - off), (0, 0))))
    return jnp.concatenate(parts, axis=0), place, rh


SMALL_NAMES = ["norm1_w", "norm2_w", "hg_lb", "hg_gn_w", "fox_b_f", "fox_qn_w", "fox_kn_w", "final_w"]


def _pack_small(d, names):
    rows, offs, r0 = [], {}, 0
    for n in names:
        flat = d[n].reshape(-1)
        nr = -(-flat.shape[0] // LANES)
        rows.append(jnp.pad(flat, (0, nr * LANES - flat.shape[0])).reshape(nr, LANES))
        offs[n] = (r0, nr)
        r0 += nr
    return jnp.concatenate(rows, axis=0), offs


def _unpack_small(packed, offs, name, like):
    r0, nr = offs[name]
    return packed[r0:r0 + nr].reshape(-1)[:like.size].reshape(like.shape)


def kernel(x, c, w_mod, b_mod, norm1_w, norm2_w, hg_w_in, hg_w_out, hg_lb, hg_gn_w, fox_w_in, fox_b_f, fox_qn_w, fox_kn_w, fox_w_out, mlp_w1, mlp_w2, final_w, loss_target, m_w_mod, m_b_mod, m_norm1_w, m_norm2_w, m_hg_w_in, m_hg_w_out, m_hg_lb, m_hg_gn_w, m_fox_w_in, m_fox_b_f, m_fox_qn_w, m_fox_kn_w, m_fox_w_out, m_mlp_w1, m_mlp_w2, m_final_w, v_w_mod, v_b_mod, v_norm1_w, v_norm2_w, v_hg_w_in, v_hg_w_out, v_hg_lb, v_hg_gn_w, v_fox_w_in, v_fox_b_f, v_fox_qn_w, v_fox_kn_w, v_fox_w_out, v_mlp_w1, v_mlp_w2, v_final_w):
    S, D = x.shape[1], x.shape[2]
    nheads = D // FOX_DH
    ax, ay, ac = _mesh_pos()
    chip = 2 * ax + ay
    dev = 2 * chip + ac
    xs, tgt = x.reshape(S, D), loss_target.reshape(S, D)

    c_all = _allgather_small(_pad_rows(c.reshape(-1, LANES), 8), name="gather_c")
    c_all = c_all.reshape(N_DEV, -1)[:, :D]
    c16 = _pad_rows(c_all, 16)
    nmod = w_mod.shape[2]
    b_shard = lax.dynamic_slice_in_dim(b_mod, chip * nmod, nmod, axis=1)
    mod_shard = _mod_fwd(c16, w_mod, b_shard[:, None, :], name="mod_fwd")[:, :N_DEV]
    mod_all = _allgather_small(mod_shard.reshape(-1, LANES), name="gather_mod")
    mod_all = mod_all.reshape(N_CHIPS, 2, 2, N_DEV, nmod)[:, 0]
    mod = lax.dynamic_index_in_dim(mod_all, dev, axis=2, keepdims=False)
    mod = mod.transpose(1, 0, 2).reshape(2, N_CHIPS * nmod)

    fox_rows = fox_w_in.shape[2]
    col = lambda g: g.transpose(1, 0, 2).reshape(g.shape[1], -1)
    rowsh = lambda g: g.reshape(-1, g.shape[2])
    own = lambda g, s: lax.dynamic_update_index_in_dim(g, s, chip, 0)

    slab_in = hg_w_in[0].astype(BF)
    wts = {"hg_w_in": col(own(_allgather_chip_slabs(slab_in, name="gather_hg_w_in"), slab_in))}
    fox_flat, fox_cut = fox_w_in[0].reshape(fox_rows, D), fox_rows // 2
    slabs, unpacks = [], []
    for layout_w in ([[("mlp_w1_0", mlp_w1[0])], [("mlp_w2_0", mlp_w2[0])]],
                     [[("mlp_w1_1", mlp_w1[1]), ("hg_w_out", hg_w_out[0])], [("mlp_w2_1", mlp_w2[1]), ("fox_w_out", fox_w_out[0])]],
                     [[("fox_a", fox_flat[:fox_cut])], [("fox_b", fox_flat[fox_cut:])]]):
        slab_w, place_w, rh_w = _pack_halves(layout_w)

        def unpack(gathered, slab_w=slab_w, place_w=place_w, rh_w=rh_w):
            gathered = own(gathered, slab_w)
            out = {}
            for n, (h, off, rows) in place_w.items():
                g = gathered[:, h * rh_w + off:h * rh_w + off + rows, :]
                out[n] = col(g) if n.startswith("mlp_w1") else rowsh(g) if n.startswith(("mlp_w2", "hg_", "fox_w")) else g
            if "fox_a" in out:
                fox_in = col(jnp.concatenate([out.pop("fox_a"), out.pop("fox_b")], axis=1).reshape(N_CHIPS, D, fox_rows))
                out["fox_w_in"] = jnp.pad(fox_in, ((0, 0), (0, 5 * D - fox_in.shape[1])))
            return out

        slabs.append(slab_w)
        unpacks.append(unpack)

    small = {"norm1_w": norm1_w, "norm2_w": norm2_w, "hg_lb": hg_lb, "hg_gn_w": hg_gn_w, "fox_b_f": fox_b_f,
             "fox_qn_w": fox_qn_w, "fox_kn_w": fox_kn_w, "final_w": final_w}

    def uncol(g, n):
        return g.reshape(g.shape[0], N_CHIPS, n).transpose(1, 0, 2)

    pos = jnp.stack([chip, ac])

    def swap_and_add(g4, tag):
        to_sibling = lax.dynamic_index_in_dim(g4, 1 - ac, axis=1, keepdims=False).astype(BF)
        from_sibling = _swap_halves(to_sibling, name=f"rs_swap_{tag}")
        return from_sibling, _add_halves(g4, from_sibling, ac.reshape(1), name=f"rs_add_halves_{tag}")

    def finish(g4, from_sibling, from_chips, tag):
        my_half = _add_four(g4, from_sibling, from_chips, pos, name=f"rs_add_chips_{tag}")
        return _join_halves(my_half, name=f"rs_join_{tag}")

    layout = [[("mlp_w1", 2 * D), ("hg_w_out", D // 4), ("fox_w_out", D // 4)], [("mlp_w2", 2 * D), ("fox_w_in", fox_rows)]]
    place = {}
    for h, half in enumerate(layout):
        off = 0
        for n, rows in half:
            place[n] = (h, off, rows)
            off += rows

    rh = -(-max(sum(rows for _, rows in half) for half in layout) // 16) * 16
    where = {"hg_w_out": ("row",) + place["hg_w_out"][:2], "fox_w_out": ("row",) + place["fox_w_out"][:2]}
    for i in range(2):
        where[f"mlp_w1_{i}"] = ("col", place["mlp_w1"][0], place["mlp_w1"][1] + i * D)
        where[f"mlp_w2_{i}"] = ("row", place["mlp_w2"][0], place["mlp_w2"][1] + i * D)

    def reduce_early(gw, slab):
        gfox = uncol(gw["fox_w_in"][:, :4 * fox_rows], fox_rows).reshape(N_CHIPS, 1, fox_rows, D)
        h, off, _ = place["fox_w_in"]
        slab = lax.dynamic_update_slice(slab, gfox, (0, h, off, 0))
        for h, half in enumerate(layout):
            used = sum(rows for _, rows in half)
            if used < rh:
                slab = lax.dynamic_update_slice(slab, jnp.zeros((N_CHIPS, 1, rh - used, D), F32), (0, h, used, 0))
        from_sibling, part = swap_and_add(slab, "early")
        return part, (slab, from_sibling)

    def reduce_late(gw):
        g4 = uncol(gw["hg_w_in"], D).reshape(N_CHIPS, 2, D // 2, D)
        from_sibling, part = swap_and_add(g4, "late")
        return part, (g4, from_sibling)

    loss_part, grad_x, gw, gs, (early, from_chips_early), (late, from_chips_late) = _local_step(
        xs, tgt, mod, wts, small, slabs, unpacks, reduce_early, (lax.empty((N_CHIPS, 2, rh, D), F32), where), reduce_late)
    gshard = finish(*early, from_chips_early, "early")
    g_hg_w_in = finish(*late, from_chips_late, "late").reshape(D, D)

    names = ["dmod", "loss"] + SMALL_NAMES
    packed, offs = _pack_small({**gs, "loss": loss_part[0, :1]}, names)
    packed = _pad_rows(packed, 8)
    rp = packed.shape[0]
    parts = _allgather_small(packed, name="gather_small").reshape(N_DEV, rp, LANES)
    total = _sum_parts(parts, name="sum_small")
    r0, nr = offs["dmod"]
    dmod_all = parts[:, r0:r0 + nr].reshape(N_DEV, 2, N_CHIPS * nmod)
    dmod_shard = lax.dynamic_slice_in_dim(dmod_all, chip * nmod, nmod, axis=2).transpose(1, 0, 2)
    g_w_mod = _mod_bwd(c16, jnp.pad(dmod_shard, ((0, 0), (0, 16 - N_DEV), (0, 0))), name="mod_bwd")

    loss = _unpack_small(total, offs, "loss", loss_part[0, :1]).reshape(())
    grads = {"w_mod": g_w_mod, "b_mod": _unpack_small(total, offs, "dmod", b_mod)}
    for n in SMALL_NAMES:
        grads[n] = _unpack_small(total, offs, n, small[n])

    given = dict(w_mod=(w_mod, m_w_mod, v_w_mod), b_mod=(b_mod, m_b_mod, v_b_mod), norm1_w=(norm1_w, m_norm1_w, v_norm1_w),
                 norm2_w=(norm2_w, m_norm2_w, v_norm2_w), hg_w_in=(hg_w_in, m_hg_w_in, v_hg_w_in),
                 hg_w_out=(hg_w_out, m_hg_w_out, v_hg_w_out), hg_lb=(hg_lb, m_hg_lb, v_hg_lb),
                 hg_gn_w=(hg_gn_w, m_hg_gn_w, v_hg_gn_w), fox_w_in=(fox_w_in, m_fox_w_in, v_fox_w_in),
                 fox_b_f=(fox_b_f, m_fox_b_f, v_fox_b_f), fox_qn_w=(fox_qn_w, m_fox_qn_w, v_fox_qn_w),
                 fox_kn_w=(fox_kn_w, m_fox_kn_w, v_fox_kn_w), fox_w_out=(fox_w_out, m_fox_w_out, v_fox_w_out),
                 mlp_w1=(mlp_w1, m_mlp_w1, v_mlp_w1), mlp_w2=(mlp_w2, m_mlp_w2, v_mlp_w2), final_w=(final_w, m_final_w, v_final_w))
    upd = {}

    for n, (h, off, rows) in place.items():
        w, m, v = given[n]
        flat = lambda a: a.reshape(rows, D)
        d, mn, vn = _adamw(flat(w), gshard, flat(m), flat(v), g_at=(h, off), name=f"adamw_{n}")
        grads[n] = gshard[h, off:off + rows].reshape(w.shape)
        upd[n] = tuple(a.reshape(w.shape) for a in (d, mn, vn))

    w, m, v = given["hg_w_in"]
    grads["hg_w_in"] = g_hg_w_in.reshape(w.shape)
    upd["hg_w_in"] = tuple(a.reshape(w.shape) for a in _adamw(w[0], g_hg_w_in, m[0], v[0], name="adamw_hg_w_in"))

    w, m, v = given["w_mod"]
    flat = lambda a: a.reshape(-1, nmod)
    upd["w_mod"] = tuple(a.reshape(w.shape) for a in _adamw(flat(w), flat(g_w_mod), flat(m), flat(v), name="adamw_w_mod"))

    snames = ["b_mod"] + SMALL_NAMES
    pw, soffs = _pack_small({n: given[n][0] for n in snames}, snames)
    pm, _ = _pack_small({n: given[n][1] for n in snames}, snames)
    pv, _ = _pack_small({n: given[n][2] for n in snames}, snames)
    pg, _ = _pack_small({n: grads[n] for n in snames}, snames)
    pw, pm, pv, pg = (_pad_rows(a, 8) for a in (pw, pm, pv, pg))
    sd, smn, svn = _adamw(pw, pg, pm, pv, name="adamw_small")
    for n in snames:
        like = given[n][0]
        upd[n] = tuple(_unpack_small(a, soffs, n, like) for a in (sd, smn, svn))

    order = ["w_mod", "b_mod", "norm1_w", "norm2_w", "hg_w_in", "hg_w_out", "hg_lb", "hg_gn_w", "fox_w_in", "fox_b_f",
             "fox_qn_w", "fox_kn_w", "fox_w_out", "mlp_w1", "mlp_w2", "final_w"]
    return (loss, grad_x.reshape(x.shape), *[grads[n] for n in order], *[upd[n][0] for n in order],
            *[upd[n][1] for n in order], *[upd[n][2] for n in order])
```

```python
import math

import jax
import jax.numpy as jnp
from jax import lax
from jax.experimental import pallas as pl
from jax.experimental.pallas import tpu as pltpu

EPS = 1e-6
ADAM_LR, ADAM_B1, ADAM_B2, ADAM_EPS, ADAM_WD, ADAM_STEP = 0.001, 0.9, 0.999, 1e-08, 0.01, 10

F32 = jnp.float32
BF = jnp.bfloat16
LANES = 128
HG_CHUNK = 64
HG_HEADS_PER_STEP = 8
HG_TOKENS_PER_STEP = 256
FOX_ROWS_PER_STEP = 2048
FOX_BWD_TILES = (8, 4, 2, 1)
LOG2E = 1.4426950408889634
FOX_DH = 64
N_CHIPS = 4
N_DEV = 8
VMEM_LIMIT = 56 * 1024 * 1024
MESH = pl.DeviceIdType.MESH

NT = (((1,), (1,)), ((), ()))
TN = (((0,), (0,)), ((), ()))


def _pick(n, pref, mult=LANES):
    if n <= pref:
        return n
    t = (pref // mult) * mult
    while t >= mult:
        if n % t == 0:
            return t
        t -= mult
    raise ValueError((n, pref, mult))


def _cp(*sem):
    return pltpu.CompilerParams(dimension_semantics=sem, vmem_limit_bytes=VMEM_LIMIT)


def _dot(a, b):
    return jnp.dot(a, b, preferred_element_type=F32)


def _dg(a, b, dims):
    return lax.dot_general(a, b, dims, preferred_element_type=F32)


def _split3(x):
    hi = x.astype(BF)
    r1 = x - hi.astype(F32)
    mid = r1.astype(BF)
    lo = (r1 - mid.astype(F32)).astype(BF)
    return hi, mid, lo


def _tri_dot(tri, x):
    hi, mid, lo = _split3(x)
    return _dot(tri, hi) + _dot(tri, mid) + _dot(tri, lo)


def _dg3(a, b, dims):
    ah, bh = a.astype(BF), b.astype(BF)
    al, bl = (a - ah.astype(F32)).astype(BF), (b - bh.astype(F32)).astype(BF)
    return _dg(ah, bh, dims) + _dg(ah, bl, dims) + _dg(al, bh, dims)


def _dg1(a, b, dims):
    return _dg(a.astype(BF), b.astype(BF), dims)


NN = (((1,), (0,)), ((), ()))


def _sigmoid(x):
    return jax.nn.sigmoid(x)


def _ln_matmul(x, nw, sc, sh, w, slab=None, *, relu2, name):
    S, D = x.shape
    N = w.shape[1]
    tm, tn = _pick(S, 512, 16), N
    fused = slab is not None

    def body(x_ref, nw_ref, sc_ref, sh_ref, w_ref, *rest):
        if fused:
            s_ref, *outs, out_ref, hs, send_sems, recv_sems = rest
            finish = _gather_behind(s_ref, out_ref, send_sems, recv_sems, pl.program_id(0), S // tm)
        else:
            outs, hs = rest[:-1], rest[-1]
        h_ref = outs[-1]

        @pl.when(pl.program_id(1) == 0)
        def _():
            xv = x_ref[...]
            r = lax.rsqrt(jnp.mean(xv * xv, axis=-1, keepdims=True) + EPS)
            hb = ((xv * r * nw_ref[...]) * (1.0 + sc_ref[...]) + sh_ref[...]).astype(BF)
            hs[...] = hb
            h_ref[...] = hb

        z = _dot(hs[...], w_ref[...])
        if relu2:
            a = jnp.maximum(z, 0.0)
            outs[0][...] = a.astype(BF)
            outs[1][...] = (a * a).astype(BF)
        else:
            outs[0][...] = z
        if fused:
            finish()

    vec = pl.BlockSpec((1, D), lambda i, j: (0, 0))
    tile = pl.BlockSpec((tm, tn), lambda i, j: (i, j))
    if relu2:
        out_shape = [jax.ShapeDtypeStruct((S, N), BF), jax.ShapeDtypeStruct((S, N), BF)]
        out_specs = [tile, tile]
    else:
        out_shape = [jax.ShapeDtypeStruct((S, N), F32)]
        out_specs = [tile]
    out_shape.append(jax.ShapeDtypeStruct((S, D), BF))
    out_specs.append(pl.BlockSpec((tm, D), lambda i, j: (i, 0)))
    in_specs = [pl.BlockSpec((tm, D), lambda i, j: (i, 0)), vec, vec, vec, pl.BlockSpec((D, tn), lambda i, j: (0, j))]
    scratch = [pltpu.VMEM((tm, D), BF)]
    args = [x, nw, sc, sh, w]
    if fused:
        in_specs.append(HBM)
        out_specs.append(HBM)
        out_shape.append(jax.ShapeDtypeStruct((N_CHIPS,) + slab.shape, slab.dtype))
        scratch += [pltpu.SemaphoreType.DMA((6,)), pltpu.SemaphoreType.DMA((6,))]
        args.append(slab)
    return pl.pallas_call(
        body, name=name, grid=(S // tm, N // tn), in_specs=in_specs, out_specs=out_specs, out_shape=out_shape,
        scratch_shapes=scratch, compiler_params=_cp("arbitrary", "arbitrary"),
    )(*args)


def _matmul_resid(a, w, x, gate, *, name):
    S, K = a.shape
    D = w.shape[1]
    tm, tn = _pick(S, 1024 if K <= 1024 else 512, 16), D

    def body(a_ref, w_ref, x_ref, g_ref, o_ref, y_ref):
        y = _dot(a_ref[...], w_ref[...])
        y_ref[...] = y.astype(BF)
        o_ref[...] = x_ref[...] + g_ref[...] * y

    tile = pl.BlockSpec((tm, tn), lambda i, j: (i, j))
    return pl.pallas_call(
        body, name=name, grid=(S // tm, D // tn),
        in_specs=[pl.BlockSpec((tm, K), lambda i, j: (i, 0)), pl.BlockSpec((K, tn), lambda i, j: (0, j)),
                  tile, pl.BlockSpec((1, tn), lambda i, j: (0, j))],
        out_specs=[tile, tile],
        out_shape=[jax.ShapeDtypeStruct((S, D), F32), jax.ShapeDtypeStruct((S, D), BF)],
        compiler_params=_cp("parallel", "arbitrary"),
    )(a, w, x, gate)


def _gate_matmul_nt(dx, gate, y, w, act, *, name):
    S, D = dx.shape
    K = w.shape[0]
    tm, tn = _pick(S, 1024 if K <= 1024 else 512, 16), K
    fused = act is not None

    def body(dx_ref, g_ref, y_ref, w_ref, *rest):
        if fused:
            act_ref, da_ref, dm_ref, dg_ref, ms = rest
        else:
            da_ref, dm_ref, dg_ref, ms = rest
        i, j = pl.program_id(0), pl.program_id(1)

        @pl.when((i == 0) & (j == 0))
        def _():
            dg_ref[...] = jnp.zeros_like(dg_ref)

        @pl.when(j == 0)
        def _():
            dxv = dx_ref[...]
            dmb = (dxv * g_ref[...]).astype(BF)
            ms[...] = dmb
            dm_ref[...] = dmb
            dg_ref[...] += jnp.sum(dxv * y_ref[...].astype(F32), axis=0, keepdims=True)

        da = _dg(ms[...], w_ref[...], NT)
        if fused:
            da_ref[...] = (da * (2.0 * act_ref[...].astype(F32))).astype(BF)
        else:
            da_ref[...] = da

    row = pl.BlockSpec((tm, D), lambda i, j: (i, 0))
    vec = pl.BlockSpec((1, D), lambda i, j: (0, 0))
    tile = pl.BlockSpec((tm, tn), lambda i, j: (i, j))
    in_specs = [row, vec, row, pl.BlockSpec((tn, D), lambda i, j: (j, 0))]
    args = [dx, gate, y, w]
    if fused:
        in_specs.append(tile)
        args.append(act)
    return pl.pallas_call(
        body, name=name, grid=(S // tm, K // tn),
        in_specs=in_specs, out_specs=[tile, row, vec],
        out_shape=[jax.ShapeDtypeStruct((S, K), BF if fused else F32), jax.ShapeDtypeStruct((S, D), BF),
                   jax.ShapeDtypeStruct((1, D), F32)],
        scratch_shapes=[pltpu.VMEM((tm, D), BF)],
        compiler_params=_cp("arbitrary", "arbitrary"),
    )(*args)


def _matmul_tn(a, b, *, name, into=None):
    S, Ka = a.shape
    P, _, Db = b.shape
    tk, tn, ts = _pick(Ka, 1024), _pick(Db, 1024), _pick(S, 1024, 16)
    if into is not None:
        slab, kind, half, off = into
        C = tn = slab.shape[3]
        per_chip = Ka // N_CHIPS
        all_chips = kind == "row" and tk == Ka
        if kind == "row" and not all_chips:
            tk = min(tk, per_chip)
        assert tn == C and P * Db == (N_CHIPS * C if kind == "col" else C)
        if kind == "col":
            assert tk == Ka and off % tk == 0
        elif all_chips:
            assert off % per_chip == 0
        else:
            assert per_chip % tk == 0 and off % tk == 0
    npb = Db // tn

    def body(a_ref, b_ref, *rest):
        o_ref, acc = rest[-2:]
        s = pl.program_id(2)

        @pl.when(s == 0)
        def _():
            acc[...] = jnp.zeros_like(acc)

        acc[...] += _dg(a_ref[...], b_ref[...], TN)

        @pl.when(s == pl.num_programs(2) - 1)
        def _():
            o_ref[...] = acc[...].reshape(o_ref.shape)

    in_specs = [pl.BlockSpec((ts, tk), lambda i, j, s: (s, i)),
                pl.BlockSpec((None, ts, tn), lambda i, j, s: (j // npb, s, j % npb))]
    args = [a, b]
    if into is None:
        out_spec = pl.BlockSpec((tk, tn), lambda i, j, s: (i, j))
        out_shape = jax.ShapeDtypeStruct((Ka, P * Db), F32)
        aliases = {}
    else:
        per = per_chip // tk if kind == "row" and not all_chips else 1
        if kind == "col":
            out_spec = pl.BlockSpec((None, None, tk, tn), lambda i, j, s: (j, half, off // tk + i, 0))
        elif all_chips:
            out_spec = pl.BlockSpec((N_CHIPS, None, per_chip, tn), lambda i, j, s: (0, half, off // per_chip, 0))
        else:
            out_spec = pl.BlockSpec((None, None, tk, tn), lambda i, j, s: (i // per, half, off // tk + i % per, 0))
        out_shape = jax.ShapeDtypeStruct(slab.shape, F32)
        in_specs.append(pl.BlockSpec(memory_space=pl.ANY))
        args.append(slab)
        aliases = {2: 0}
    return pl.pallas_call(
        body, name=name, grid=(Ka // tk, P * npb, S // ts),
        in_specs=in_specs, out_specs=out_spec, out_shape=out_shape,
        scratch_shapes=[pltpu.VMEM((tk, tn), F32)], input_output_aliases=aliases,
        compiler_params=_cp("parallel", "parallel", "arbitrary"),
    )(*args)


def _matmul_nt_lnbwd(g, w, x, nw, sc, dx_out, part=None, *, name):
    P, S, Dg = g.shape
    D = x.shape[1]
    tm = _pick(S, 512, 16)
    fused = part is not None

    def body(g_ref, w_ref, x_ref, nw_ref, sc_ref, dxo_ref, *rest):
        if fused:
            p_ref, dx_ref, dsc_ref, dsh_ref, dnw_ref, recv_ref, send_sems, recv_sems = rest
            copies = _scatter_copies(p_ref, recv_ref, send_sems, recv_sems)
        else:
            dx_ref, dsc_ref, dsh_ref, dnw_ref = rest

        @pl.when(pl.program_id(0) == 0)
        def _():
            dsc_ref[...] = jnp.zeros_like(dsc_ref)
            dsh_ref[...] = jnp.zeros_like(dsh_ref)
            dnw_ref[...] = jnp.zeros_like(dnw_ref)
            if fused:
                for cp in copies:
                    cp.start()

        dh = _dg(g_ref[0], w_ref[:, 0:Dg], NT)
        for p in range(1, P):
            dh = dh + _dg(g_ref[p], w_ref[:, p * Dg:(p + 1) * Dg], NT)
        xv = x_ref[...]
        nwv = nw_ref[...]
        r = lax.rsqrt(jnp.mean(xv * xv, axis=-1, keepdims=True) + EPS)
        xr = xv * r
        dn = dh * (1.0 + sc_ref[...])
        dsc_ref[...] += jnp.sum(dh * (xr * nwv), axis=0, keepdims=True)
        dsh_ref[...] += jnp.sum(dh, axis=0, keepdims=True)
        dnw_ref[...] += jnp.sum(dn * xr, axis=0, keepdims=True)
        u = dn * nwv
        dx_ref[...] = dxo_ref[...] + r * (u - xr * jnp.mean(u * xr, axis=-1, keepdims=True))

        if fused:
            @pl.when(pl.program_id(0) == S // tm - 1)
            def _():
                for cp in copies:
                    cp.wait()

    row = pl.BlockSpec((tm, D), lambda i: (i, 0))
    vec = pl.BlockSpec((1, D), lambda i: (0, 0))
    in_specs = [pl.BlockSpec((P, tm, Dg), lambda i: (0, i, 0)), pl.BlockSpec((D, P * Dg), lambda i: (0, 0)), row, vec, vec, row]
    out_specs = [row, vec, vec, vec]
    out_shape = [jax.ShapeDtypeStruct((S, D), F32)] + [jax.ShapeDtypeStruct((1, D), F32)] * 3
    scratch, args = [], [g, w, x, nw, sc, dx_out]
    if fused:
        in_specs.append(HBM)
        out_specs.append(HBM)
        out_shape.append(jax.ShapeDtypeStruct((3,) + part.shape[1:], part.dtype))
        scratch = [pltpu.SemaphoreType.DMA((3,)), pltpu.SemaphoreType.DMA((3,))]
        args.append(part)
    return pl.pallas_call(
        body, name=name, grid=(S // tm,), in_specs=in_specs, out_specs=out_specs, out_shape=out_shape,
        scratch_shapes=scratch, compiler_params=_cp("arbitrary"),
    )(*args)


def _loss_kernel(x, fw, tgt, *, name):
    S, D = x.shape
    tm = _pick(S, 512, 8)

    def body(x_ref, fw_ref, t_ref, l_ref, dx_ref, dfw_ref):
        @pl.when(pl.program_id(0) == 0)
        def _():
            l_ref[...] = jnp.zeros_like(l_ref)
            dfw_ref[...] = jnp.zeros_like(dfw_ref)

        xv = x_ref[...]
        fwv = fw_ref[...]
        r = lax.rsqrt(jnp.mean(xv * xv, axis=-1, keepdims=True) + EPS)
        xr = xv * r
        err = xr * fwv - t_ref[...]
        per_tok = jnp.mean(err * err, axis=-1, keepdims=True)
        l_ref[...] += 0.5 * jnp.sum(per_tok, axis=0, keepdims=True)
        dy = err * (1.0 / D)
        dfw_ref[...] += jnp.sum(dy * xr, axis=0, keepdims=True)
        u = dy * fwv
        dx_ref[...] = r * (u - xr * jnp.mean(u * xr, axis=-1, keepdims=True))

    row = pl.BlockSpec((tm, D), lambda i: (i, 0))
    vec = pl.BlockSpec((1, D), lambda i: (0, 0))
    return pl.pallas_call(
        body, name=name, grid=(S // tm,),
        in_specs=[row, vec, row],
        out_specs=[pl.BlockSpec((1, LANES), lambda i: (0, 0)), row, vec],
        out_shape=[jax.ShapeDtypeStruct((1, LANES), F32), jax.ShapeDtypeStruct((S, D), F32),
                   jax.ShapeDtypeStruct((1, D), F32)],
        compiler_params=_cp("arbitrary"),
    )(x, fw, tgt)


def _hg_lower_bound(lb3):
    mx = jnp.max(lb3, axis=0, keepdims=True)
    e = jnp.exp(lb3 - mx)
    p = e / jnp.sum(e, axis=0, keepdims=True)
    return p[0:1, :], p


def _hg_chunk_common(qr, fz, lbv):
    sq = _sigmoid(qr)
    q = qr * sq
    sig = _sigmoid(fz)
    f = lbv + (1.0 - lbv) * sig
    k = (1.0 - lbv) * (1.0 - sig)
    return q, sq, sig, f, k, jnp.log(f)


def _row_of(x, rows, r):
    return jnp.sum(jnp.where(rows == r, x, 0.0), axis=0, keepdims=True)


def _hg_fwd(proj, hg_lb, gn, slab=None, *, name):
    S = proj.shape[0]
    D = proj.shape[1] // 4
    H = D // LANES
    HB = min(HG_HEADS_PER_STEP, H)
    W = HB * LANES
    C = HG_CHUNK
    T = _pick(S, HG_TOKENS_PER_STEP, C)
    nch, nb = T // C, S // T
    ng = H // HB
    fused = slab is not None

    def body(q_ref, fz_ref, v_ref, g_ref, lb_ref, gn_ref, *rest):
        if fused:
            s_ref, y_ref, o_ref, sts_ref, out_ref, st, send_sems, recv_sems = rest
            finish = _gather_behind(s_ref, out_ref, send_sems, recv_sems,
                                    pl.program_id(0) * nb + pl.program_id(1), ng * nb)
        else:
            y_ref, o_ref, sts_ref, st = rest

        @pl.when(pl.program_id(1) == 0)
        def _():
            st[...] = jnp.zeros_like(st)

        lb_all, _ = _hg_lower_bound(lb_ref[...])
        gnv = gn_ref[...]
        ri = lax.broadcasted_iota(jnp.int32, (C, C), 0)
        ci_ = lax.broadcasted_iota(jnp.int32, (C, C), 1)
        low = ri >= ci_
        tri = jnp.where(low, 1.0, 0.0).astype(BF)
        rows_w = lax.broadcasted_iota(jnp.int32, (C, W), 0)

        def chunk(ci, carry):
            sl = pl.ds(pl.multiple_of(ci * C, C), C)
            heads = [slice(hh * LANES, (hh + 1) * LANES) for hh in range(HB)]
            q, _, _, _, k, logf = _hg_chunk_common(q_ref[sl, :], fz_ref[sl, :], lb_all)
            vv, gg = v_ref[sl, :], g_ref[sl, :]
            G = _tri_dot(tri, logf)
            Gm = _row_of(G, rows_w, C // 2 - 1)
            Gl = _row_of(G, rows_w, C - 1)
            qt, kt = q * jnp.exp(G - Gm), k * jnp.exp(Gm - G)
            qe, kd, eGl = q * jnp.exp(G), k * jnp.exp(Gl - G), jnp.exp(Gl)
            A = [jnp.where(low, _dg1(qt[:, ls], kt[:, ls], NT), 0.0) for ls in heads]
            Sv = [st[hh] for hh in range(HB)]
            for hh in range(HB):
                sts_ref[hh, ci] = Sv[hh]
            o = [_dg1(A[hh], vv[:, ls], NN) + _dg1(qe[:, ls], Sv[hh], NT) for hh, ls in enumerate(heads)]
            for hh, ls in enumerate(heads):
                st[hh] = Sv[hh] * eGl[:, ls] + _dg1(vv[:, ls], kd[:, ls], TN)
            gate = gg * _sigmoid(gg)
            for hh, ls in enumerate(heads):
                r = lax.rsqrt(jnp.mean(o[hh] * o[hh], axis=-1, keepdims=True) + EPS)
                y_ref[sl, ls] = ((o[hh] * r * gnv) * gate[:, ls]).astype(BF)
                o_ref[sl, ls] = o[hh]
            return carry

        lax.fori_loop(0, nch, chunk, 0)

        if fused:
            finish()

    def part(p):
        return pl.BlockSpec((T, W), lambda h, n: (n, p * ng + h))

    blk = pl.BlockSpec((T, W), lambda h, n: (n, h))
    in_specs = [part(0), part(1), part(2), part(3),
                pl.BlockSpec((3, W), lambda h, n: (0, h)), pl.BlockSpec((1, LANES), lambda h, n: (0, 0))]
    out_specs = [blk, blk, pl.BlockSpec((HB, nch, LANES, LANES), lambda h, n: (h, n, 0, 0))]
    out_shape = [jax.ShapeDtypeStruct((S, D), BF), jax.ShapeDtypeStruct((S, D), F32),
                 jax.ShapeDtypeStruct((H, S // C, LANES, LANES), F32)]
    scratch = [pltpu.VMEM((HB, LANES, LANES), F32)]
    args = [proj, proj, proj, proj, hg_lb, gn]
    if fused:
        in_specs.append(HBM)
        out_specs.append(HBM)
        out_shape.append(jax.ShapeDtypeStruct((N_CHIPS,) + slab.shape, slab.dtype))
        scratch += [pltpu.SemaphoreType.DMA((6,)), pltpu.SemaphoreType.DMA((6,))]
        args.append(slab)
    return pl.pallas_call(
        body, name=name, grid=(ng, nb), in_specs=in_specs, out_specs=out_specs, out_shape=out_shape,
        scratch_shapes=scratch, compiler_params=_cp("arbitrary", "arbitrary"),
    )(*args)


def _hg_bwd(proj, hg_lb, gn, o_all, states, dy, part=None, *, name):
    S = proj.shape[0]
    D = proj.shape[1] // 4
    H = D // LANES
    HB = min(HG_HEADS_PER_STEP, H)
    W = HB * LANES
    C = HG_CHUNK
    T = _pick(S, HG_TOKENS_PER_STEP, C)
    nch, nb = T // C, S // T
    ng = H // HB
    fused = part is not None

    def body(q_ref, fz_ref, v_ref, g_ref, lb_ref, gn_ref, o_ref, sts_ref, dy_ref, *rest):
        if fused:
            p_ref, dp_ref, dlb_ref, dgn_ref, recv_ref, dst, dlb_acc, send_sems, recv_sems = rest
            copies = _scatter_copies(p_ref, recv_ref, send_sems, recv_sems)

            @pl.when((pl.program_id(0) == 0) & (pl.program_id(1) == 0))
            def _():
                for cp in copies:
                    cp.start()
        else:
            dp_ref, dlb_ref, dgn_ref, dst, dlb_acc = rest
        n = pl.program_id(1)

        @pl.when(n == 0)
        def _():
            dst[...] = jnp.zeros_like(dst)
            dlb_acc[...] = jnp.zeros_like(dlb_acc)
            dgn_ref[...] = jnp.zeros_like(dgn_ref)

        lb_all, p3 = _hg_lower_bound(lb_ref[...])
        gnv = gn_ref[...]
        ri = lax.broadcasted_iota(jnp.int32, (C, C), 0)
        ci_ = lax.broadcasted_iota(jnp.int32, (C, C), 1)
        low = ri >= ci_
        tri = jnp.where(low, 1.0, 0.0).astype(BF)
        triu = jnp.where(ri <= ci_, 1.0, 0.0).astype(BF)
        rows_w = lax.broadcasted_iota(jnp.int32, (C, W), 0)
        gnw = jnp.tile(gnv, (1, HB))

        def chunk(cj, carry):
            ci = nch - 1 - cj
            sl = pl.ds(pl.multiple_of(ci * C, C), C)
            heads = list(enumerate(slice(hh * LANES, (hh + 1) * LANES) for hh in range(HB)))
            wide = lambda parts: jnp.concatenate(parts, axis=1)
            qr, vv, gg = q_ref[sl, :], v_ref[sl, :], g_ref[sl, :]
            q, sq, sig, f, k, logf = _hg_chunk_common(qr, fz_ref[sl, :], lb_all)
            G = _tri_dot(tri, logf)
            Gm = _row_of(G, rows_w, C // 2 - 1)
            Gl = _row_of(G, rows_w, C - 1)
            eG, e_qm, e_km, e_lk, eGl = jnp.exp(G), jnp.exp(G - Gm), jnp.exp(Gm - G), jnp.exp(Gl - G), jnp.exp(Gl)
            qt, kt, kdec, qe = q * e_qm, k * e_km, k * e_lk, q * eG
            sg = _sigmoid(gg)
            d_onw = dy_ref[sl, :] * (gg * sg)
            u = d_onw * gnw
            o = o_ref[sl, :]
            on, do = [], []
            for hh, ls in heads:
                r = lax.rsqrt(jnp.mean(o[:, ls] * o[:, ls], axis=-1, keepdims=True) + EPS)
                on.append(o[:, ls] * r)
                dgn_ref[hh] += jnp.sum(d_onw[:, ls] * on[hh], axis=0, keepdims=True)
                do.append(r * (u[:, ls] - on[hh] * jnp.mean(u[:, ls] * on[hh], axis=-1, keepdims=True)))
            dgg = dy_ref[sl, :] * (wide(on) * gnw) * (sg * (1.0 + gg * (1.0 - sg)))
            Sv = [sts_ref[hh, ci] for hh, _ in heads]
            dSv = [dst[hh] for hh, _ in heads]
            A = [jnp.where(low, _dg1(qt[:, ls], kt[:, ls], NT), 0.0) for _, ls in heads]
            dA = [jnp.where(low, _dg3(do[hh], vv[:, ls], NT), 0.0) for hh, ls in heads]
            dv = wide([_dg1(A[hh], do[hh], TN) + _dg1(kdec[:, ls], dSv[hh], NT) for hh, ls in heads])
            dq = wide([_dg3(dA[hh], kt[:, ls], NN) for hh, ls in heads]) * e_qm \
                + eG * wide([_dg3(do[hh], Sv[hh], NN) for hh, _ in heads])
            dk = wide([_dg3(dA[hh], qt[:, ls], TN) for hh, ls in heads]) * e_km \
                + e_lk * wide([_dg3(vv[:, ls], dSv[hh], NN) for hh, ls in heads])
            s_end = [Sv[hh] * eGl[:, ls] + _dg3(vv[:, ls], kdec[:, ls], TN) for hh, ls in heads]
            dgl = wide([jnp.sum(dSv[hh] * s_end[hh], axis=0, keepdims=True) for hh, _ in heads])
            for hh, ls in heads:
                dst[hh] = dSv[hh] * eGl[:, ls] + _dg1(do[hh], qe[:, ls], TN)
            dG = q * dq - k * dk + jnp.where(rows_w == C - 1, dgl, 0.0)
            dlogf = _tri_dot(triu, dG) - f * dk
            dlf_f = dlogf / f
            dlb_acc[...] += jnp.sum(dlf_f * (1.0 - sig), axis=0, keepdims=True)
            dp_ref[0, sl, :] = (dq * (sq * (1.0 + qr * (1.0 - sq)))).astype(BF)
            dp_ref[1, sl, :] = (dlf_f * (1.0 - lb_all) * sig * (1.0 - sig)).astype(BF)
            dp_ref[2, sl, :] = dv.astype(BF)
            dp_ref[3, sl, :] = dgg.astype(BF)
            return carry

        lax.fori_loop(0, nch, chunk, 0)
        sel = jnp.where(lax.broadcasted_iota(jnp.int32, (3, W), 0) == 0, 1.0, 0.0)
        dlb_ref[...] = lb_all * (sel - p3) * dlb_acc[...]

        if fused:
            @pl.when((pl.program_id(0) == ng - 1) & (n == nb - 1))
            def _():
                for cp in copies:
                    cp.wait()

    def col(p):
        return pl.BlockSpec((T, W), lambda h, n: (nb - 1 - n, p * ng + h))

    blk = pl.BlockSpec((T, W), lambda h, n: (nb - 1 - n, h))
    in_specs = [col(0), col(1), col(2), col(3),
                pl.BlockSpec((3, W), lambda h, n: (0, h)), pl.BlockSpec((1, LANES), lambda h, n: (0, 0)),
                blk, pl.BlockSpec((HB, nch, LANES, LANES), lambda h, n: (h, nb - 1 - n, 0, 0)), blk]
    out_specs = [pl.BlockSpec((4, T, W), lambda h, n: (0, nb - 1 - n, h)),
                 pl.BlockSpec((3, W), lambda h, n: (0, h)),
                 pl.BlockSpec((HB, 1, LANES), lambda h, n: (h, 0, 0))]
    out_shape = [jax.ShapeDtypeStruct((4, S, D), BF), jax.ShapeDtypeStruct((3, D), F32),
                 jax.ShapeDtypeStruct((H, 1, LANES), F32)]
    scratch = [pltpu.VMEM((HB, LANES, LANES), F32), pltpu.VMEM((1, W), F32)]
    args = [proj, proj, proj, proj, hg_lb, gn, o_all, states, dy]
    if fused:
        in_specs.append(HBM)
        out_specs.append(HBM)
        out_shape.append(jax.ShapeDtypeStruct((3,) + part.shape[1:], part.dtype))
        scratch += [pltpu.SemaphoreType.DMA((3,)), pltpu.SemaphoreType.DMA((3,))]
        args.append(part)
    return pl.pallas_call(
        body, name=name, grid=(ng, nb), in_specs=in_specs, out_specs=out_specs, out_shape=out_shape,
        scratch_shapes=scratch, compiler_params=_cp("arbitrary", "arbitrary"),
    )(*args)


def _log_sigmoid(u):
    return jnp.minimum(u, 0.0) - jnp.log(1.0 + jnp.exp(-jnp.abs(u)))


def _lane_put(base, lane, first, pieces):
    for n, p in enumerate(pieces):
        base = jnp.where(lane == first + n, p, base)
    return base


def _fox_cumsum(proj, bf_pad, *, name):
    S = proj.shape[0]
    D = proj.shape[1] // 5
    T = _pick(S, 256, 8)

    def body(fz_ref, b_ref, f_ref, carry):
        @pl.when(pl.program_id(0) == 0)
        def _():
            carry[...] = jnp.zeros_like(carry)

        logf = _log_sigmoid(fz_ref[...] + b_ref[...])
        tri = jnp.where(lax.broadcasted_iota(jnp.int32, (T, T), 0) >= lax.broadcasted_iota(jnp.int32, (T, T), 1),
                        1.0, 0.0).astype(BF)
        fv = _tri_dot(tri, logf) + carry[...]
        f_ref[...] = fv
        carry[...] = _row_of(fv, lax.broadcasted_iota(jnp.int32, (T, LANES), 0), T - 1)

    return pl.pallas_call(
        body, name=name, grid=(S // T,),
        in_specs=[pl.BlockSpec((T, LANES), lambda i: (i, 4 * D // LANES)), pl.BlockSpec((1, LANES), lambda i: (0, 0))],
        out_specs=pl.BlockSpec((T, LANES), lambda i: (i, 0)),
        out_shape=jax.ShapeDtypeStruct((S, LANES), F32),
        scratch_shapes=[pltpu.VMEM((1, LANES), F32)],
        compiler_params=_cp("arbitrary"),
    )(proj, bf_pad)


def _pair_stats(sq, lo):
    del lo
    a = lax.broadcasted_iota(jnp.int32, (LANES, LANES), 0) < FOX_DH
    b = lax.broadcasted_iota(jnp.int32, (LANES, LANES), 1) < FOX_DH
    avg = jnp.where(a == b, 1.0 / FOX_DH, 0.0).astype(BF)
    hi = sq.astype(BF)
    mid = (sq - hi.astype(F32)).astype(BF)
    return _dot(hi, avg) + _dot(mid, avg)


def _fox_prep(proj, fcum, qw2, kw2, *, name):
    S = proj.shape[0]
    D = proj.shape[1] // 5
    HP = D // LANES
    T = _pick(S, FOX_ROWS_PER_STEP, 16)

    def body(q_ref, k_ref, v_ref, f_ref, qw_ref, kw_ref, qa_ref, ka_ref, va_ref, vt_ref):
        hp = pl.program_id(1)
        lane = lax.broadcasted_iota(jnp.int32, (T, LANES), 1)
        lo = lane < FOX_DH
        qv, kv, vv, fv = q_ref[...], k_ref[...], v_ref[...], f_ref[...]
        qn = qv * lax.rsqrt(_pair_stats(qv * qv, lo) + EPS) * qw_ref[...] * (0.125 * LOG2E)
        kn = kv * lax.rsqrt(_pair_stats(kv * kv, lo) + EPS) * kw_ref[...]
        ones_q = jnp.where((lane >= 67) & (lane <= 69), 1.0, 0.0)
        ones_k = jnp.where(((lane >= 64) & (lane <= 66)) | ((lane >= 70) & (lane <= 72)), 1.0, 0.0)
        ones_v = jnp.where((lane >= 64) & (lane <= 66), 1.0, 0.0)
        for hh in range(2):
            fh = jnp.sum(jnp.where(lane == 2 * hp + hh, fv, 0.0), axis=-1, keepdims=True) * LOG2E
            pieces = [p.astype(F32) for p in _split3(fh)]

            def half(x):
                return jnp.where(lo, x if hh == 0 else pltpu.roll(x, FOX_DH, 1), 0.0)

            qa_ref[hh] = _lane_put(half(qn) + ones_q, lane, 64, pieces).astype(BF)
            ka_ref[hh] = _lane_put(half(kn) + ones_k, lane, 67, [-p for p in pieces]).astype(BF)
            va = half(vv) + ones_v
            va_ref[hh] = va.astype(BF)
            vt_ref[hh] = va.T.astype(BF)

    def part(p):
        return pl.BlockSpec((T, LANES), lambda i, hp: (i, p * HP + hp))

    vec = pl.BlockSpec((1, LANES), lambda i, hp: (0, 0))
    aug = pl.BlockSpec((2, T, LANES), lambda i, hp: (hp, i, 0))
    return pl.pallas_call(
        body, name=name, grid=(S // T, HP),
        in_specs=[part(0), part(1), part(2), pl.BlockSpec((T, LANES), lambda i, hp: (i, 0)), vec, vec],
        out_specs=[aug, aug, aug, pl.BlockSpec((2, LANES, T), lambda i, hp: (hp, 0, i))],
        out_shape=[jax.ShapeDtypeStruct((2 * HP, S, LANES), BF)] * 3 + [jax.ShapeDtypeStruct((2 * HP, LANES, S), BF)],
        compiler_params=_cp("parallel", "arbitrary"),
    )(proj, proj, proj, fcum, qw2, kw2)


def _fox_block(S):
    return _pick(S, 256, 16)


def _fox_skip_bounds(fcum, qn_w, kn_w, nheads):
    S = fcum.shape[0]
    B = _fox_block(S)
    qk = 8.0 * LOG2E * 1.02 * jnp.max(jnp.abs(qn_w)) * jnp.max(jnp.abs(kn_w))
    thresh = -(2.0 * qk + 152.0)
    f2 = fcum[:, :nheads] * LOG2E
    first, last = f2[0::B], f2[B - 1::B]
    nb = S // B
    blk = jnp.arange(nb)
    dead = (first[0::2, None, :] - last[None, :, :]) < thresh
    jmin = jnp.sum(dead & (blk[None, :, None] < 2 * jnp.arange(nb // 2)[:, None, None]), axis=1)
    live = (first[:, None, :] - last[None, :, :]) >= thresh
    imax = blk[:, None] + jnp.sum(live & (blk[:, None, None] > blk[None, :, None]), axis=0)
    return jmin.T.astype(jnp.int32), imax.T.astype(jnp.int32)


def _fox_fwd(jmin, qa, ka, vat, proj, *, name):
    H, S, _ = qa.shape
    HP = H // 2
    D = HP * LANES
    B = _fox_block(S)
    BQ = 2 * B
    nq = S // BQ

    def body(jmin_ref, q_ref, k_ref, vt_ref, g_ref, y_ref, o_ref, q2_ref):
        hp, i = pl.program_id(0), pl.program_id(1)
        lane = lax.broadcasted_iota(jnp.int32, (BQ, LANES), 1)
        lo = lane < FOX_DH
        in_stat = (lane >= 70) & (lane <= 75)
        causal = lax.broadcasted_iota(jnp.int32, (BQ, BQ), 0) <= lax.broadcasted_iota(jnp.int32, (BQ, BQ), 1)
        row = lax.broadcasted_iota(jnp.int32, (LANES, BQ), 0)
        m0, acc0 = jnp.full((1, BQ), -jnp.inf, F32), jnp.zeros((LANES, BQ), F32)
        outs = []
        for hh in range(2):
            qb = q_ref[hh]

            def scores(j):
                sl = pl.ds(pl.multiple_of(j * BQ, BQ), BQ)
                return _dg(k_ref[hh, sl, :], qb, NT)

            def update(j, m, acc, st, masked=False):
                sl = pl.ds(pl.multiple_of(j * BQ, BQ), BQ)
                if masked:
                    st = jnp.where(causal, st, -jnp.inf)
                m_new = jnp.maximum(m, jnp.ceil(jnp.max(st, axis=0, keepdims=True)))
                p = jnp.exp2(st - m_new).astype(BF)
                return m_new, acc * jnp.exp2(m - m_new) + _dot(vt_ref[hh, :, sl], p)

            def step(j, carry):
                m, acc, st = carry
                st_next = scores(j + 1)
                return update(j, m, acc, st) + (st_next,)

            first = jmin_ref[2 * hp + hh, i] // 2
            m, acc, st = lax.fori_loop(first, i, step, (m0, acc0, scores(first)))
            m, acc = update(i, m, acc, st, masked=True)
            linv = 1.0 / jnp.sum(jnp.where(row == FOX_DH, acc, 0.0), axis=0, keepdims=True)
            tile = acc * linv
            for n, piece in enumerate(_split3(m) + _split3(linv)):
                tile = jnp.where(row == 70 + n, piece.astype(F32), tile)
            tile = tile.T
            outs.append(tile)
            q2_ref[hh] = jnp.where(in_stat, jnp.where(lane <= 72, -tile, tile), qb.astype(F32)).astype(BF)
        o = jnp.where(lo, outs[0], pltpu.roll(outs[1], FOX_DH, 1))
        o_ref[...] = o
        y_ref[...] = (o * _sigmoid(g_ref[...])).astype(BF)

    blk = pl.BlockSpec((BQ, LANES), lambda hp, i, jm: (i, hp))
    qblk = pl.BlockSpec((2, BQ, LANES), lambda hp, i, jm: (hp, i, 0))
    full = pl.BlockSpec((2, S, LANES), lambda hp, i, jm: (hp, 0, 0))
    full_t = pl.BlockSpec((2, LANES, S), lambda hp, i, jm: (hp, 0, 0))
    return pl.pallas_call(
        body, name=name,
        grid_spec=pltpu.PrefetchScalarGridSpec(
            num_scalar_prefetch=1, grid=(HP, nq),
            in_specs=[qblk, full, full_t, pl.BlockSpec((BQ, LANES), lambda hp, i, jm: (i, 3 * HP + hp))],
            out_specs=[blk, blk, qblk]),
        out_shape=[jax.ShapeDtypeStruct((S, D), BF), jax.ShapeDtypeStruct((S, D), F32),
                   jax.ShapeDtypeStruct((H, S, LANES), BF)],
        compiler_params=_cp("parallel", "arbitrary"),
    )(jmin, qa, ka, vat, proj)


def _fox_bwd_prep(dy, o, proj, q2, *, name):
    S, D = dy.shape
    HP = D // LANES
    T = _pick(S, FOX_ROWS_PER_STEP, 16)

    def body(dy_ref, o_ref, g_ref, q2_ref, da_ref):
        lane = lax.broadcasted_iota(jnp.int32, (T, LANES), 1)
        lo = lane < FOX_DH
        in_linv = (lane >= 73) & (lane <= 75)
        linv = [jnp.sum(jnp.where(in_linv, q2_ref[hh].astype(F32), 0.0), axis=-1, keepdims=True) for hh in range(2)]
        u = (dy_ref[...] * _sigmoid(g_ref[...]) * jnp.where(lo, linv[0], linv[1])).astype(BF).astype(F32)
        prod = u * o_ref[...]
        d_lo = jnp.sum(jnp.where(lo, prod, 0.0), axis=-1, keepdims=True)
        d_hi = jnp.sum(jnp.where(lo, 0.0, prod), axis=-1, keepdims=True)
        for hh, delta in enumerate((d_lo, d_hi)):
            base = jnp.where(lo, u if hh == 0 else pltpu.roll(u, FOX_DH, 1), 0.0)
            da_ref[hh] = _lane_put(base, lane, 64, [-(p.astype(F32)) for p in _split3(delta)]).astype(BF)

    blk = pl.BlockSpec((T, LANES), lambda i, hp: (i, hp))
    aug = pl.BlockSpec((2, T, LANES), lambda i, hp: (hp, i, 0))
    return pl.pallas_call(
        body, name=name, grid=(S // T, HP),
        in_specs=[blk, blk, pl.BlockSpec((T, LANES), lambda i, hp: (i, 3 * HP + hp)), aug],
        out_specs=aug,
        out_shape=jax.ShapeDtypeStruct((2 * HP, S, LANES), BF),
        compiler_params=_cp("parallel", "arbitrary"),
    )(dy, o, proj, q2)


def _fox_bwd(imax, q2, ka, va, doa, *, name):
    H, S, _ = q2.shape
    B = _fox_block(S)
    nb = S // B

    def body(imax_ref, q_ref, do_ref, k_ref, v_ref, dq_ref, dk_ref, dv_ref, cs_ref):
        j = pl.program_id(1)
        end = imax_ref[pl.program_id(0), j] + 1

        @pl.when(j == 0)
        def _():
            dq_ref[...] = jnp.zeros_like(dq_ref)

        kb, vb = k_ref[...], v_ref[...]

        def step(i, carry, nblk=1):
            dk_acc, dv_acc, cs_acc = carry
            rows = nblk * B
            sl = pl.ds(pl.multiple_of(i * B, B), rows)
            qb, dob = q_ref[sl, :], do_ref[sl, :]
            s = _dg(qb, kb, NT)
            ahead = lax.broadcasted_iota(jnp.int32, (rows, B), 0) - lax.broadcasted_iota(jnp.int32, (rows, B), 1)
            pb = jnp.exp2(jnp.where(ahead >= (j - i) * B, s, -jnp.inf)).astype(BF)
            ds = pb.astype(F32) * _dg(dob, vb, NT)
            dsb = ds.astype(BF)
            cs_acc = cs_acc + jnp.sum(ds.reshape(rows // 8, 8, B), axis=0)
            dv_acc = dv_acc + _dg(pb, dob, TN)
            dk_acc = dk_acc + _dg(dsb, qb, TN)
            dq_ref[sl, :] += _dot(dsb, kb)
            return dk_acc, dv_acc, cs_acc

        zero = jnp.zeros((B, LANES), F32)
        carry = (zero, zero, jnp.zeros((8, B), F32))
        pos = j
        for U in FOX_BWD_TILES:
            n = (end - pos) // U
            carry = lax.fori_loop(0, n, lambda ii, c, pos=pos, U=U: step(pos + U * ii, c, nblk=U), carry)
            pos = pos + U * n
        dk_acc, dv_acc, cs_acc = carry
        dk_ref[...] = dk_acc
        dv_ref[...] = dv_acc
        cs_ref[...] = jnp.sum(cs_acc, axis=0, keepdims=True)

    full = pl.BlockSpec((None, S, LANES), lambda h, j, im: (h, 0, 0))
    blk = pl.BlockSpec((None, B, LANES), lambda h, j, im: (h, j, 0))
    return pl.pallas_call(
        body, name=name,
        grid_spec=pltpu.PrefetchScalarGridSpec(
            num_scalar_prefetch=1, grid=(H, nb),
            in_specs=[full, full, blk, blk],
            out_specs=[full, blk, blk, pl.BlockSpec((None, 1, B), lambda h, j, im: (h, 0, j))]),
        out_shape=[jax.ShapeDtypeStruct((H, S, LANES), F32)] * 3 + [jax.ShapeDtypeStruct((H, 1, S), F32)],
        compiler_params=_cp("parallel", "arbitrary"),
    )(imax, q2, doa, ka, va)


def _fox_bwd_post(dqa, dka, dva, proj, dy, o, qw2, kw2, *, name):
    S, D = dy.shape
    HP = D // LANES
    T = _pick(S, FOX_ROWS_PER_STEP, 16)

    def body(dq_ref, dk_ref, dv_ref, q_ref, k_ref, g_ref, dy_ref, o_ref, qw_ref, kw_ref, dp_ref, dqw_ref, dkw_ref):
        @pl.when((pl.program_id(0) == 0) & (pl.program_id(1) == 0))
        def _():
            dqw_ref[...] = jnp.zeros_like(dqw_ref)
            dkw_ref[...] = jnp.zeros_like(dkw_ref)

        lane = lax.broadcasted_iota(jnp.int32, (T, LANES), 1)
        lo = lane < FOX_DH

        def pair(ref):
            return jnp.where(lo, ref[0], pltpu.roll(ref[1], FOX_DH, 1))

        def norm_bwd(xv, w, dyn, dw_ref):
            r = lax.rsqrt(_pair_stats(xv * xv, lo) + EPS)
            xr = xv * r
            dw_ref[...] += jnp.sum(dyn * xr, axis=0, keepdims=True)
            u = dyn * w
            return r * (u - xr * _pair_stats(u * xr, lo))

        dp_ref[0] = norm_bwd(q_ref[...], qw_ref[...], pair(dq_ref) * 0.125, dqw_ref).astype(BF)
        dp_ref[1] = norm_bwd(k_ref[...], kw_ref[...], pair(dk_ref) * (1.0 / LOG2E), dkw_ref).astype(BF)
        dp_ref[2] = pair(dv_ref).astype(BF)
        sg = _sigmoid(g_ref[...])
        dp_ref[3] = (dy_ref[...] * o_ref[...] * sg * (1.0 - sg)).astype(BF)

    def part(p):
        return pl.BlockSpec((T, LANES), lambda i, hp: (i, p * HP + hp))

    aug = pl.BlockSpec((2, T, LANES), lambda i, hp: (hp, i, 0))
    blk = pl.BlockSpec((T, LANES), lambda i, hp: (i, hp))
    vec = pl.BlockSpec((1, LANES), lambda i, hp: (0, 0))
    return pl.pallas_call(
        body, name=name, grid=(S // T, HP),
        in_specs=[aug, aug, aug, part(0), part(1), part(3), blk, blk, vec, vec],
        out_specs=[pl.BlockSpec((4, T, LANES), lambda i, hp: (0, i, hp)), vec, vec],
        out_shape=[jax.ShapeDtypeStruct((5, S, D), BF), jax.ShapeDtypeStruct((1, LANES), F32),
                   jax.ShapeDtypeStruct((1, LANES), F32)],
        compiler_params=_cp("arbitrary", "arbitrary"),
    )(dqa, dka, dva, proj, proj, proj, dy, o, qw2, kw2)


def _fox_dfz(colsum, nheads, proj, bf_pad, dproj, *, name):
    S = colsum.shape[0]
    H = nheads
    D = dproj.shape[2]
    T = _pick(S, 256, 16)
    nb = S // T

    def body(cs_ref, fz_ref, b_ref, _, dp_ref, db_ref, carry):
        @pl.when(pl.program_id(0) == 0)
        def _():
            carry[...] = jnp.zeros_like(carry)
            db_ref[...] = jnp.zeros_like(db_ref)

        lane = lax.broadcasted_iota(jnp.int32, (T, LANES), 1)
        df = -cs_ref[...]
        triu = jnp.where(lax.broadcasted_iota(jnp.int32, (T, T), 0) <= lax.broadcasted_iota(jnp.int32, (T, T), 1),
                         1.0, 0.0).astype(BF)
        dlogf = _tri_dot(triu, df) + carry[...]
        carry[...] = _row_of(dlogf, lax.broadcasted_iota(jnp.int32, (T, LANES), 0), 0)
        dfz = jnp.where(lane < H, dlogf * _sigmoid(-(fz_ref[...] + b_ref[...])), 0.0)
        db_ref[...] += jnp.sum(dfz, axis=0, keepdims=True)
        dp_ref[...] = jnp.zeros_like(dp_ref)
        dp_ref[:, 0:LANES] = dfz.astype(BF)

    return pl.pallas_call(
        body, name=name, grid=(nb,),
        in_specs=[pl.BlockSpec((T, LANES), lambda i: (nb - 1 - i, 0)),
                  pl.BlockSpec((T, LANES), lambda i: (nb - 1 - i, 4 * D // LANES)),
                  pl.BlockSpec((1, LANES), lambda i: (0, 0)),
                  pl.BlockSpec(memory_space=pl.ANY)],
        out_specs=[pl.BlockSpec((None, T, D), lambda i: (4, nb - 1 - i, 0)), pl.BlockSpec((1, LANES), lambda i: (0, 0))],
        out_shape=[jax.ShapeDtypeStruct(dproj.shape, BF), jax.ShapeDtypeStruct((1, LANES), F32)],
        scratch_shapes=[pltpu.VMEM((1, LANES), F32)],
        input_output_aliases={3: 0},
        compiler_params=_cp("arbitrary"),
    )(colsum, proj, bf_pad, dproj)


def _mod_fwd(c16, w, b, *, name):
    L, D, N = w.shape
    tn = _pick(N, 512)

    def body(c_ref, w_ref, b_ref, o_ref):
        cv = c_ref[...]
        ca = (cv * _sigmoid(cv)).astype(BF)
        o_ref[...] = _dot(ca, w_ref[...].astype(BF)) + b_ref[...]

    return pl.pallas_call(
        body, name=name, grid=(L, N // tn),
        in_specs=[pl.BlockSpec((16, D), lambda l, j: (0, 0)), pl.BlockSpec((None, D, tn), lambda l, j: (l, 0, j)),
                  pl.BlockSpec((None, 1, tn), lambda l, j: (l, 0, j))],
        out_specs=pl.BlockSpec((None, 16, tn), lambda l, j: (l, 0, j)),
        out_shape=jax.ShapeDtypeStruct((L, 16, N), F32),
        compiler_params=_cp("parallel", "arbitrary"),
    )(c16, w, b)


def _mod_bwd(c16, dmod, *, name):
    L, _, N = dmod.shape
    D = c16.shape[1]
    tn = _pick(N, 512)

    def body(c_ref, d_ref, o_ref):
        cv = c_ref[...]
        ca = (cv * _sigmoid(cv)).astype(BF)
        o_ref[...] = _dg(ca, d_ref[...].astype(BF), TN)

    return pl.pallas_call(
        body, name=name, grid=(L, N // tn),
        in_specs=[pl.BlockSpec((16, D), lambda l, j: (0, 0)), pl.BlockSpec((None, 16, tn), lambda l, j: (l, 0, j))],
        out_specs=pl.BlockSpec((None, D, tn), lambda l, j: (l, 0, j)),
        out_shape=jax.ShapeDtypeStruct((L, D, N), F32),
        compiler_params=_cp("parallel", "arbitrary"),
    )(c16, dmod)


def _adamw_math(w, g, m, v):
    m = ADAM_B1 * m + (1.0 - ADAM_B1) * g
    v = ADAM_B2 * v + (1.0 - ADAM_B2) * (g * g)
    m_hat = m / (1.0 - ADAM_B1 ** ADAM_STEP)
    v_hat = v / (1.0 - ADAM_B2 ** ADAM_STEP)
    return -ADAM_LR * (m_hat / (jnp.sqrt(v_hat) + ADAM_EPS) + ADAM_WD * w), m, v


def _adamw(w, g, m, v, *, g_at=None, name):
    R, C = w.shape
    row0 = 0 if g_at is None else g_at[1]
    tr = min(math.gcd(row0, 256) if row0 else 256, -(-R // 8) * 8)
    g0 = row0 // tr
    if g_at is None:
        g_spec = pl.BlockSpec((tr, C), lambda i: (i, 0))
    else:
        g_spec = pl.BlockSpec((None, tr, C), lambda i: (g_at[0], g0 + i, 0))

    def body(w_ref, g_ref, m_ref, v_ref, d_ref, mo_ref, vo_ref):
        d, mn, vn = _adamw_math(w_ref[...], g_ref[...], m_ref[...], v_ref[...])
        d_ref[...] = d
        mo_ref[...] = mn
        vo_ref[...] = vn

    blk = pl.BlockSpec((tr, C), lambda i: (i, 0))
    return pl.pallas_call(
        body, name=name, grid=(pl.cdiv(R, tr),),
        in_specs=[blk, g_spec, blk, blk],
        out_specs=[blk, blk, blk],
        out_shape=[jax.ShapeDtypeStruct((R, C), F32)] * 3,
        compiler_params=_cp("parallel"),
    )(w, g, m, v)


def _sum_parts(parts, *, name):
    P, R, C = parts.shape

    def body(p_ref, o_ref):
        acc = p_ref[0]
        for p in range(1, P):
            acc = acc + p_ref[p]
        o_ref[...] = acc

    return pl.pallas_call(
        body, name=name, grid=(1,),
        in_specs=[pl.BlockSpec((P, R, C), lambda i: (0, 0, 0))],
        out_specs=pl.BlockSpec((R, C), lambda i: (0, 0)),
        out_shape=jax.ShapeDtypeStruct((R, C), F32),
        compiler_params=_cp("arbitrary"),
    )(parts)


def _add_halves(g4, recv, c_idx, *, name):
    _, _, Rh, C = g4.shape
    tr = min(256, Rh)

    def body(c_ref, a_ref, b_ref, o_ref):
        o_ref[...] = (a_ref[...] + b_ref[...].astype(F32)).astype(BF)

    return pl.pallas_call(
        body, name=name,
        grid_spec=pltpu.PrefetchScalarGridSpec(
            num_scalar_prefetch=1, grid=(4, pl.cdiv(Rh, tr)),
            in_specs=[pl.BlockSpec((None, None, tr, C), lambda j, r, c: (j, c[0], r, 0)),
                      pl.BlockSpec((None, tr, C), lambda j, r, c: (j, r, 0))],
            out_specs=pl.BlockSpec((None, tr, C), lambda j, r, c: (j, r, 0))),
        out_shape=jax.ShapeDtypeStruct((4, Rh, C), BF),
        compiler_params=_cp("parallel", "arbitrary"),
    )(c_idx, g4, recv)


def _add_four(g4, from_sibling, from_chips, pos, *, name):
    _, _, Rh, C = g4.shape
    tr = min(256, Rh)

    def body(p_ref, a_ref, s_ref, b_ref, o_ref):
        own = a_ref[...] + s_ref[...].astype(F32)
        o_ref[...] = ((own + b_ref[0].astype(F32)) + b_ref[1].astype(F32)) + b_ref[2].astype(F32)

    return pl.pallas_call(
        body, name=name,
        grid_spec=pltpu.PrefetchScalarGridSpec(
            num_scalar_prefetch=1, grid=(pl.cdiv(Rh, tr),),
            in_specs=[pl.BlockSpec((None, None, tr, C), lambda r, p: (p[0], p[1], r, 0)),
                      pl.BlockSpec((None, tr, C), lambda r, p: (p[0], r, 0)),
                      pl.BlockSpec((3, tr, C), lambda r, p: (0, r, 0))],
            out_specs=pl.BlockSpec((None, tr, C), lambda r, p: (p[1], r, 0))),
        out_shape=jax.ShapeDtypeStruct((2, Rh, C), F32),
        compiler_params=_cp("arbitrary"),
    )(pos, g4, from_sibling, from_chips)


HBM = pl.BlockSpec(memory_space=pltpu.HBM)


def _mesh_pos():
    return lax.axis_index("x"), lax.axis_index("y"), lax.axis_index("c")


def _other_chips(x, y):
    return [(1 - x, y), (x, 1 - y), (1 - x, 1 - y)]


def _allgather_small(xs, *, name):
    m_per, n = xs.shape

    def body(x_ref, out_ref, send_sems, recv_sems, local_sem):
        x, y, c = _mesh_pos()
        me, sibling = (x, y, c), (x, y, 1 - c)
        chips = _other_chips(x, y)

        def rows(px, py, pc):
            return out_ref.at[pl.ds((4 * px + 2 * py + pc) * m_per, m_per), :]

        def copy(k, block, to, src=None):
            return pltpu.make_async_remote_copy(
                src_ref=rows(*block) if src is None else src, dst_ref=rows(*block),
                send_sem=send_sems.at[k], recv_sem=recv_sems.at[k], device_id=to, device_id_type=MESH)

        mine = pltpu.make_async_copy(x_ref, rows(*me), local_sem)
        mine.start()
        first = [copy(0, me, sibling, src=x_ref)]
        first += [copy(1 + j, me, (*chip, c), src=x_ref) for j, chip in enumerate(chips)]
        for cp in first:
            cp.start()
        passed = [copy(4 + j, (*chip, c), sibling) for j, chip in enumerate(chips)]
        for j, chip in enumerate(chips):
            copy(1 + j, (*chip, c), me).wait_recv()
            passed[j].start()
        copy(0, sibling, me).wait_recv()
        for j, chip in enumerate(chips):
            copy(4 + j, (*chip, 1 - c), me).wait_recv()
        for cp in first + passed:
            cp.wait_send()
        mine.wait()

    return pl.pallas_call(
        body, name=name,
        out_shape=jax.ShapeDtypeStruct((N_DEV * m_per, n), xs.dtype),
        in_specs=[pl.BlockSpec(memory_space=pltpu.VMEM)],
        out_specs=pl.BlockSpec(memory_space=pltpu.VMEM),
        scratch_shapes=[pltpu.SemaphoreType.DMA((7,)), pltpu.SemaphoreType.DMA((7,)), pltpu.SemaphoreType.DMA],
    )(xs)


def _chip_slab_copies(s_ref, out_ref, send_sems, recv_sems):
    R = s_ref.shape[0]
    Rh = R // 2
    x, y, c = _mesh_pos()
    me, sibling = (x, y, c), (x, y, 1 - c)
    chips = _other_chips(x, y)

    def half(px, py, pc):
        return out_ref.at[2 * px + py, pl.ds(pc * Rh, Rh), :]

    def copy(k, block, to, src=None):
        return pltpu.make_async_remote_copy(
            src_ref=half(*block) if src is None else src, dst_ref=half(*block),
            send_sem=send_sems.at[k], recv_sem=recv_sems.at[k], device_id=to, device_id_type=MESH)

    first = [copy(j, me, (*chip, c), src=s_ref.at[pl.ds(c * Rh, Rh), :]) for j, chip in enumerate(chips)]
    passed = [copy(3 + j, (*chip, c), sibling) for j, chip in enumerate(chips)]
    landed = [copy(j, (*chip, c), me) for j, chip in enumerate(chips)]
    from_sibling = [copy(3 + j, (*chip, 1 - c), me) for j, chip in enumerate(chips)]
    return first, passed, landed, from_sibling


def _gather_behind(s_ref, out_ref, send_sems, recv_sems, step, nsteps):
    first, passed, landed, from_sibling = _chip_slab_copies(s_ref, out_ref, send_sems, recv_sems)

    @pl.when(step == 0)
    def _():
        for cp in first:
            cp.start()

    @pl.when(step == (3 * nsteps) // 4)
    def _():
        for arrived, onward in zip(landed, passed):
            arrived.wait_recv()
            onward.start()

    def finish():
        @pl.when(step == nsteps - 1)
        def _():
            for cp in from_sibling:
                cp.wait_recv()
            for cp in first + passed:
                cp.wait_send()

    return finish


def _allgather_chip_slabs(slab, *, name):
    R, C = slab.shape

    def body(s_ref, out_ref, send_sems, recv_sems):
        first, passed, landed, from_sibling = _chip_slab_copies(s_ref, out_ref, send_sems, recv_sems)
        for cp in first:
            cp.start()
        for arrived, onward in zip(landed, passed):
            arrived.wait_recv()
            onward.start()
        for cp in from_sibling:
            cp.wait_recv()
        for cp in first + passed:
            cp.wait_send()

    return pl.pallas_call(
        body, name=name,
        out_shape=jax.ShapeDtypeStruct((N_CHIPS, R, C), slab.dtype),
        in_specs=[HBM], out_specs=HBM,
        scratch_shapes=[pltpu.SemaphoreType.DMA((6,)), pltpu.SemaphoreType.DMA((6,))],
    )(slab)


def _swap_halves(mine, *, name):
    def body(g_ref, out_ref, send_sems, recv_sems):
        x, y, c = _mesh_pos()
        copies = [pltpu.make_async_remote_copy(
            src_ref=g_ref.at[j], dst_ref=out_ref.at[j], send_sem=send_sems.at[j], recv_sem=recv_sems.at[j],
            device_id=(x, y, 1 - c), device_id_type=MESH) for j in range(N_CHIPS)]
        for cp in copies:
            cp.start()
        for cp in copies:
            cp.wait()

    return pl.pallas_call(
        body, name=name,
        out_shape=jax.ShapeDtypeStruct(mine.shape, mine.dtype),
        in_specs=[HBM], out_specs=HBM,
        scratch_shapes=[pltpu.SemaphoreType.DMA((N_CHIPS,)), pltpu.SemaphoreType.DMA((N_CHIPS,))],
    )(mine)


def _scatter_copies(p_ref, out_ref, send_sems, recv_sems):
    x, y, c = _mesh_pos()
    return [pltpu.make_async_remote_copy(
        src_ref=p_ref.at[2 * px + py], dst_ref=out_ref.at[j], send_sem=send_sems.at[j], recv_sem=recv_sems.at[j],
        device_id=(px, py, c), device_id_type=MESH) for j, (px, py) in enumerate(_other_chips(x, y))]


def _join_halves(buf, *, name):
    def body(b_ref, out_ref, send_sem, recv_sem):
        x, y, c = _mesh_pos()
        cp = pltpu.make_async_remote_copy(
            src_ref=b_ref.at[c], dst_ref=out_ref.at[c], send_sem=send_sem, recv_sem=recv_sem,
            device_id=(x, y, 1 - c), device_id_type=MESH)
        cp.start()
        cp.wait()

    return pl.pallas_call(
        body, name=name,
        out_shape=jax.ShapeDtypeStruct(buf.shape, buf.dtype),
        in_specs=[HBM], out_specs=HBM, input_output_aliases={0: 0},
        scratch_shapes=[pltpu.SemaphoreType.DMA, pltpu.SemaphoreType.DMA],
    )(buf)


def _pad_rows(a, mult):
    pad = (-a.shape[0]) % mult
    return a if pad == 0 else jnp.pad(a, ((0, pad),) + ((0, 0),) * (a.ndim - 1))


def _local_step(x, target, mod, wts, small, slabs=None, unpacks=None, reduce_early=None, grad_slab=None,
                reduce_late=None):
    S, D = x.shape
    HP = D // LANES
    row = lambda v: v.reshape(1, -1)
    msplit = [[row(mod[i, k * D:(k + 1) * D]) for k in range(6)] for i in range(2)]
    gw, gs = {}, {}
    dmod = [[None] * 6 for _ in range(2)]
    slab, where = grad_slab if grad_slab is not None else (None, {})

    def dw(key, a, b, name):
        nonlocal slab
        if key in where:
            slab = _matmul_tn(a, b, name=name, into=(slab,) + where[key])
        else:
            gw[key] = _matmul_tn(a, b, name=name)

    sh1, sc1, g1, sh2, sc2, g2 = msplit[0]
    n1w0, n2w0 = row(small["norm1_w"][0]), row(small["norm2_w"][0])
    slabs = slabs if slabs is not None else (None, None, None)
    proj0, h1_0, *gathered = _ln_matmul(x, n1w0, sc1, sh1, wts["hg_w_in"], slabs[0], relu2=False, name="hg_in_proj")
    if slabs[0] is not None:
        wts = {**wts, **unpacks[0](gathered[0])}
    gn = small["hg_gn_w"].reshape(1, LANES)
    ypre0, o0, states, *gathered = _hg_fwd(proj0, small["hg_lb"], gn, slabs[1], name="hg_fwd")
    if slabs[1] is not None:
        wts = {**wts, **unpacks[1](gathered[0])}
    x1, ymix0 = _matmul_resid(ypre0, wts["hg_w_out"], x, g1, name="hg_out_proj")
    a0, u0, h2_0, *gathered = _ln_matmul(x1, n2w0, sc2, sh2, wts["mlp_w1_0"], slabs[2], relu2=True, name="mlp0_up")
    if slabs[2] is not None:
        wts = {**wts, **unpacks[2](gathered[0])}
    x2, ymlp0 = _matmul_resid(u0, wts["mlp_w2_0"], x1, g2, name="mlp0_down")

    sh1b, sc1b, g1b, sh2b, sc2b, g2b = msplit[1]
    n1w1, n2w1 = row(small["norm1_w"][1]), row(small["norm2_w"][1])
    proj1, h1_1 = _ln_matmul(x2, n1w1, sc1b, sh1b, wts["fox_w_in"], relu2=False, name="fox_in_proj")
    nheads = 2 * HP
    bf_pad = jnp.pad(small["fox_b_f"].reshape(1, nheads), ((0, 0), (0, LANES - nheads)))
    qw2 = jnp.tile(small["fox_qn_w"].reshape(1, FOX_DH), (1, 2))
    kw2 = jnp.tile(small["fox_kn_w"].reshape(1, FOX_DH), (1, 2))
    fcum = _fox_cumsum(proj1, bf_pad, name="fox_cumsum")
    qa, ka, va, vat = _fox_prep(proj1, fcum, qw2, kw2, name="fox_prep")
    jmin, imax = _fox_skip_bounds(fcum, small["fox_qn_w"], small["fox_kn_w"], nheads)
    ypre1, o1, q2 = _fox_fwd(jmin, qa, ka, vat, proj1, name="fox_fwd")
    x3, ymix1 = _matmul_resid(ypre1, wts["fox_w_out"], x2, g1b, name="fox_out_proj")
    a1, u1, h2_1 = _ln_matmul(x3, n2w1, sc2b, sh2b, wts["mlp_w1_1"], relu2=True, name="mlp1_up")
    x4, ymlp1 = _matmul_resid(u1, wts["mlp_w2_1"], x3, g2b, name="mlp1_down")

    loss, dx4, dfw = _loss_kernel(x4, row(small["final_w"]), target, name="loss")
    gs["final_w"] = dfw.reshape(-1)

    def mlp_bwd(i, dx_out, x_in, h2, a, u, ymlp, n2w, sc2_, g2_):
        dz, dm, dg2 = _gate_matmul_nt(dx_out, g2_, ymlp, wts[f"mlp_w2_{i}"], a, name=f"mlp{i}_down_bwd")
        dw(f"mlp_w2_{i}", u, dm[None], f"mlp{i}_dw2")
        dw(f"mlp_w1_{i}", h2, dz[None], f"mlp{i}_dw1")
        dx_in, dsc, dsh, dnw = _matmul_nt_lnbwd(dz[None], wts[f"mlp_w1_{i}"], x_in, n2w, sc2_, dx_out,
                                                name=f"mlp{i}_up_bwd")
        dmod[i][3], dmod[i][4], dmod[i][5] = dsh, dsc, dg2
        return dx_in, dnw

    dx3, dn2w1 = mlp_bwd(1, dx4, x3, h2_1, a1, u1, ymlp1, n2w1, sc2b, g2b)
    dyp1, dm1, dg1b = _gate_matmul_nt(dx3, g1b, ymix1, wts["fox_w_out"], None, name="fox_out_bwd")
    dw("fox_w_out", ypre1, dm1[None], "fox_dw_out")
    doa = _fox_bwd_prep(dyp1, o1, proj1, q2, name="fox_bwd_prep")
    dqa, dka, dva, colsum = _fox_bwd(imax, q2, ka, va, doa, name="fox_bwd")
    colsum = jnp.pad(colsum[:, 0, :].T, ((0, 0), (0, LANES - nheads)))
    dproj1, dqw, dkw = _fox_bwd_post(dqa, dka, dva, proj1, dyp1, o1, qw2, kw2, name="fox_bwd_post")
    dproj1, dbf = _fox_dfz(colsum, nheads, proj1, bf_pad, dproj1, name="fox_dfz")
    dw("fox_w_in", h1_1, dproj1, "fox_dw_in")
    dx2, dsc, dsh, dn1w1 = _matmul_nt_lnbwd(dproj1, wts["fox_w_in"], x2, n1w1, sc1b, dx3, name="fox_in_bwd")
    dmod[1][0], dmod[1][1], dmod[1][2] = dsh, dsc, dg1b
    gs["fox_qn_w"] = dqw[0, :FOX_DH] + dqw[0, FOX_DH:]
    gs["fox_kn_w"] = dkw[0, :FOX_DH] + dkw[0, FOX_DH:]
    gs["fox_b_f"] = dbf[0, :nheads]

    dx1, dn2w0 = mlp_bwd(0, dx2, x1, h2_0, a0, u0, ymlp0, n2w0, sc2, g2)
    dyp0, dm0, dg1 = _gate_matmul_nt(dx1, g1, ymix0, wts["hg_w_out"], None, name="hg_out_bwd")
    dw("hg_w_out", ypre0, dm0[None], "hg_dw_out")
    part, ctx = reduce_early(gw, slab) if reduce_early is not None else (None, None)
    dproj0, dlb, dgn, *from_chips = _hg_bwd(proj0, small["hg_lb"], gn, o0, states, dyp0, part, name="hg_bwd")
    early = (ctx, from_chips[0]) if reduce_early is not None else None
    dw("hg_w_in", h1_0, dproj0, "hg_dw_in")
    part, ctx = reduce_late(gw) if reduce_late is not None else (None, None)
    dx0, dsc, dsh, dn1w0, *from_chips = _matmul_nt_lnbwd(dproj0, wts["hg_w_in"], x, n1w0, sc1, dx1, part, name="hg_in_bwd")
    late = (ctx, from_chips[0]) if reduce_late is not None else None
    dmod[0][0], dmod[0][1], dmod[0][2] = dsh, dsc, dg1
    gs["hg_lb"] = dlb
    gs["hg_gn_w"] = jnp.sum(dgn, axis=0)

    gs["norm1_w"] = jnp.concatenate([dn1w0, dn1w1], axis=0)
    gs["norm2_w"] = jnp.concatenate([dn2w0, dn2w1], axis=0)
    gs["dmod"] = jnp.stack([jnp.concatenate(dmod[i], axis=1)[0] for i in range(2)])
    return loss, dx0, gw, gs, early, late


def _pack_halves(layout):
    rh = -(-max(sum(a.shape[0] for _, a in half) for half in layout) // 16) * 16
    place, parts = {}, []
    for h, half in enumerate(layout):
        off = 0
        for n, a in half:
            place[n] = (h, off, a.shape[0])
            off += a.shape[0]
        parts.append(jnp.pad(jnp.concatenate([a.astype(BF) for _, a in half], axis=0), ((0, rh - off), (0, 0))))
    return jnp.concatenate(parts, axis=0), place, rh


SMALL_NAMES = ["norm1_w", "norm2_w", "hg_lb", "hg_gn_w", "fox_b_f", "fox_qn_w", "fox_kn_w", "final_w"]


def _pack_small(d, names):
    rows, offs, r0 = [], {}, 0
    for n in names:
        flat = d[n].reshape(-1)
        nr = -(-flat.shape[0] // LANES)
        rows.append(jnp.pad(flat, (0, nr * LANES - flat.shape[0])).reshape(nr, LANES))
        offs[n] = (r0, nr)
        r0 += nr
    return jnp.concatenate(rows, axis=0), offs


def _unpack_small(packed, offs, name, like):
    r0, nr = offs[name]
    return packed[r0:r0 + nr].reshape(-1)[:like.size].reshape(like.shape)


def kernel(x, c, w_mod, b_mod, norm1_w, norm2_w, hg_w_in, hg_w_out, hg_lb, hg_gn_w, fox_w_in, fox_b_f, fox_qn_w, fox_kn_w, fox_w_out, mlp_w1, mlp_w2, final_w, loss_target, m_w_mod, m_b_mod, m_norm1_w, m_norm2_w, m_hg_w_in, m_hg_w_out, m_hg_lb, m_hg_gn_w, m_fox_w_in, m_fox_b_f, m_fox_qn_w, m_fox_kn_w, m_fox_w_out, m_mlp_w1, m_mlp_w2, m_final_w, v_w_mod, v_b_mod, v_norm1_w, v_norm2_w, v_hg_w_in, v_hg_w_out, v_hg_lb, v_hg_gn_w, v_fox_w_in, v_fox_b_f, v_fox_qn_w, v_fox_kn_w, v_fox_w_out, v_mlp_w1, v_mlp_w2, v_final_w):
    S, D = x.shape[1], x.shape[2]
    nheads = D // FOX_DH
    ax, ay, ac = _mesh_pos()
    chip = 2 * ax + ay
    dev = 2 * chip + ac
    xs, tgt = x.reshape(S, D), loss_target.reshape(S, D)

    c_all = _allgather_small(_pad_rows(c.reshape(-1, LANES), 8), name="gather_c")
    c_all = c_all.reshape(N_DEV, -1)[:, :D]
    c16 = _pad_rows(c_all, 16)
    nmod = w_mod.shape[2]
    b_shard = lax.dynamic_slice_in_dim(b_mod, chip * nmod, nmod, axis=1)
    mod_shard = _mod_fwd(c16, w_mod, b_shard[:, None, :], name="mod_fwd")[:, :N_DEV]
    mod_all = _allgather_small(mod_shard.reshape(-1, LANES), name="gather_mod")
    mod_all = mod_all.reshape(N_CHIPS, 2, 2, N_DEV, nmod)[:, 0]
    mod = lax.dynamic_index_in_dim(mod_all, dev, axis=2, keepdims=False)
    mod = mod.transpose(1, 0, 2).reshape(2, N_CHIPS * nmod)

    fox_rows = fox_w_in.shape[2]
    col = lambda g: g.transpose(1, 0, 2).reshape(g.shape[1], -1)
    rowsh = lambda g: g.reshape(-1, g.shape[2])
    own = lambda g, s: lax.dynamic_update_index_in_dim(g, s, chip, 0)

    slab_in = hg_w_in[0].astype(BF)
    wts = {"hg_w_in": col(own(_allgather_chip_slabs(slab_in, name="gather_hg_w_in"), slab_in))}
    fox_flat, fox_cut = fox_w_in[0].reshape(fox_rows, D), fox_rows // 2
    slabs, unpacks = [], []
    for layout_w in ([[("mlp_w1_0", mlp_w1[0])], [("mlp_w2_0", mlp_w2[0])]],
                     [[("mlp_w1_1", mlp_w1[1]), ("hg_w_out", hg_w_out[0])], [("mlp_w2_1", mlp_w2[1]), ("fox_w_out", fox_w_out[0])]],
                     [[("fox_a", fox_flat[:fox_cut])], [("fox_b", fox_flat[fox_cut:])]]):
        slab_w, place_w, rh_w = _pack_halves(layout_w)

        def unpack(gathered, slab_w=slab_w, place_w=place_w, rh_w=rh_w):
            gathered = own(gathered, slab_w)
            out = {}
            for n, (h, off, rows) in place_w.items():
                g = gathered[:, h * rh_w + off:h * rh_w + off + rows, :]
                out[n] = col(g) if n.startswith("mlp_w1") else rowsh(g) if n.startswith(("mlp_w2", "hg_", "fox_w")) else g
            if "fox_a" in out:
                fox_in = col(jnp.concatenate([out.pop("fox_a"), out.pop("fox_b")], axis=1).reshape(N_CHIPS, D, fox_rows))
                out["fox_w_in"] = jnp.pad(fox_in, ((0, 0), (0, 5 * D - fox_in.shape[1])))
            return out

        slabs.append(slab_w)
        unpacks.append(unpack)

    small = {"norm1_w": norm1_w, "norm2_w": norm2_w, "hg_lb": hg_lb, "hg_gn_w": hg_gn_w, "fox_b_f": fox_b_f,
             "fox_qn_w": fox_qn_w, "fox_kn_w": fox_kn_w, "final_w": final_w}

    def uncol(g, n):
        return g.reshape(g.shape[0], N_CHIPS, n).transpose(1, 0, 2)

    pos = jnp.stack([chip, ac])

    def swap_and_add(g4, tag):
        to_sibling = lax.dynamic_index_in_dim(g4, 1 - ac, axis=1, keepdims=False).astype(BF)
        from_sibling = _swap_halves(to_sibling, name=f"rs_swap_{tag}")
        return from_sibling, _add_halves(g4, from_sibling, ac.reshape(1), name=f"rs_add_halves_{tag}")

    def finish(g4, from_sibling, from_chips, tag):
        my_half = _add_four(g4, from_sibling, from_chips, pos, name=f"rs_add_chips_{tag}")
        return _join_halves(my_half, name=f"rs_join_{tag}")

    layout = [[("mlp_w1", 2 * D), ("hg_w_out", D // 4), ("fox_w_out", D // 4)], [("mlp_w2", 2 * D), ("fox_w_in", fox_rows)]]
    place = {}
    for h, half in enumerate(layout):
        off = 0
        for n, rows in half:
            place[n] = (h, off, rows)
            off += rows

    rh = -(-max(sum(rows for _, rows in half) for half in layout) // 16) * 16
    where = {"hg_w_out": ("row",) + place["hg_w_out"][:2], "fox_w_out": ("row",) + place["fox_w_out"][:2]}
    for i in range(2):
        where[f"mlp_w1_{i}"] = ("col", place["mlp_w1"][0], place["mlp_w1"][1] + i * D)
        where[f"mlp_w2_{i}"] = ("row", place["mlp_w2"][0], place["mlp_w2"][1] + i * D)

    def reduce_early(gw, slab):
        gfox = uncol(gw["fox_w_in"][:, :4 * fox_rows], fox_rows).reshape(N_CHIPS, 1, fox_rows, D)
        h, off, _ = place["fox_w_in"]
        slab = lax.dynamic_update_slice(slab, gfox, (0, h, off, 0))
        for h, half in enumerate(layout):
            used = sum(rows for _, rows in half)
            if used < rh:
                slab = lax.dynamic_update_slice(slab, jnp.zeros((N_CHIPS, 1, rh - used, D), F32), (0, h, used, 0))
        from_sibling, part = swap_and_add(slab, "early")
        return part, (slab, from_sibling)

    def reduce_late(gw):
        g4 = uncol(gw["hg_w_in"], D).reshape(N_CHIPS, 2, D // 2, D)
        from_sibling, part = swap_and_add(g4, "late")
        return part, (g4, from_sibling)

    loss_part, grad_x, gw, gs, (early, from_chips_early), (late, from_chips_late) = _local_step(
        xs, tgt, mod, wts, small, slabs, unpacks, reduce_early, (lax.empty((N_CHIPS, 2, rh, D), F32), where), reduce_late)
    gshard = finish(*early, from_chips_early, "early")
    g_hg_w_in = finish(*late, from_chips_late, "late").reshape(D, D)

    names = ["dmod", "loss"] + SMALL_NAMES
    packed, offs = _pack_small({**gs, "loss": loss_part[0, :1]}, names)
    packed = _pad_rows(packed, 8)
    rp = packed.shape[0]
    parts = _allgather_small(packed, name="gather_small").reshape(N_DEV, rp, LANES)
    total = _sum_parts(parts, name="sum_small")
    r0, nr = offs["dmod"]
    dmod_all = parts[:, r0:r0 + nr].reshape(N_DEV, 2, N_CHIPS * nmod)
    dmod_shard = lax.dynamic_slice_in_dim(dmod_all, chip * nmod, nmod, axis=2).transpose(1, 0, 2)
    g_w_mod = _mod_bwd(c16, jnp.pad(dmod_shard, ((0, 0), (0, 16 - N_DEV), (0, 0))), name="mod_bwd")

    loss = _unpack_small(total, offs, "loss", loss_part[0, :1]).reshape(())
    grads = {"w_mod": g_w_mod, "b_mod": _unpack_small(total, offs, "dmod", b_mod)}
    for n in SMALL_NAMES:
        grads[n] = _unpack_small(total, offs, n, small[n])

    given = dict(w_mod=(w_mod, m_w_mod, v_w_mod), b_mod=(b_mod, m_b_mod, v_b_mod), norm1_w=(norm1_w, m_norm1_w, v_norm1_w),
                 norm2_w=(norm2_w, m_norm2_w, v_norm2_w), hg_w_in=(hg_w_in, m_hg_w_in, v_hg_w_in),
                 hg_w_out=(hg_w_out, m_hg_w_out, v_hg_w_out), hg_lb=(hg_lb, m_hg_lb, v_hg_lb),
                 hg_gn_w=(hg_gn_w, m_hg_gn_w, v_hg_gn_w), fox_w_in=(fox_w_in, m_fox_w_in, v_fox_w_in),
                 fox_b_f=(fox_b_f, m_fox_b_f, v_fox_b_f), fox_qn_w=(fox_qn_w, m_fox_qn_w, v_fox_qn_w),
                 fox_kn_w=(fox_kn_w, m_fox_kn_w, v_fox_kn_w), fox_w_out=(fox_w_out, m_fox_w_out, v_fox_w_out),
                 mlp_w1=(mlp_w1, m_mlp_w1, v_mlp_w1), mlp_w2=(mlp_w2, m_mlp_w2, v_mlp_w2), final_w=(final_w, m_final_w, v_final_w))
    upd = {}

    for n, (h, off, rows) in place.items():
        w, m, v = given[n]
        flat = lambda a: a.reshape(rows, D)
        d, mn, vn = _adamw(flat(w), gshard, flat(m), flat(v), g_at=(h, off), name=f"adamw_{n}")
        grads[n] = gshard[h, off:off + rows].reshape(w.shape)
        upd[n] = tuple(a.reshape(w.shape) for a in (d, mn, vn))

    w, m, v = given["hg_w_in"]
    grads["hg_w_in"] = g_hg_w_in.reshape(w.shape)
    upd["hg_w_in"] = tuple(a.reshape(w.shape) for a in _adamw(w[0], g_hg_w_in, m[0], v[0], name="adamw_hg_w_in"))

    w, m, v = given["w_mod"]
    flat = lambda a: a.reshape(-1, nmod)
    upd["w_mod"] = tuple(a.reshape(w.shape) for a in _adamw(flat(w), flat(g_w_mod), flat(m), flat(v), name="adamw_w_mod"))

    snames = ["b_mod"] + SMALL_NAMES
    pw, soffs = _pack_small({n: given[n][0] for n in snames}, snames)
    pm, _ = _pack_small({n: given[n][1] for n in snames}, snames)
    pv, _ = _pack_small({n: given[n][2] for n in snames}, snames)
    pg, _ = _pack_small({n: grads[n] for n in snames}, snames)
    pw, pm, pv, pg = (_pad_rows(a, 8) for a in (pw, pm, pv, pg))
    sd, smn, svn = _adamw(pw, pg, pm, pv, name="adamw_small")
    for n in snames:
        like = given[n][0]
        upd[n] = tuple(_unpack_small(a, soffs, n, like) for a in (sd, smn, svn))

    order = ["w_mod", "b_mod", "norm1_w", "norm2_w", "hg_w_in", "hg_w_out", "hg_lb", "hg_gn_w", "fox_w_in", "fox_b_f",
             "fox_qn_w", "fox_kn_w", "fox_w_out", "mlp_w1", "mlp_w2", "final_w"]
    return (loss, grad_x.reshape(x.shape), *[grads[n] for n in order], *[upd[n][0] for n in order],
            *[upd[n][1] for n in order], *[upd[n][2] for n in order])
```

```python
import math

import jax
import jax.numpy as jnp
from jax import lax
from jax.experimental import pallas as pl
from jax.experimental.pallas import tpu as pltpu

EPS = 1e-6
ADAM_LR, ADAM_B1, ADAM_B2, ADAM_EPS, ADAM_WD, ADAM_STEP = 0.001, 0.9, 0.999, 1e-08, 0.01, 10

F32 = jnp.float32
BF = jnp.bfloat16
LANES = 128
HG_CHUNK = 64
HG_HEADS_PER_STEP = 8
HG_TOKENS_PER_STEP = 256
FOX_ROWS_PER_STEP = 2048
FOX_BWD_TILES = (16, 8, 4, 2, 1)
LOG2E = 1.4426950408889634
FOX_DH = 64
N_CHIPS = 4
N_DEV = 8
VMEM_LIMIT = 56 * 1024 * 1024
MESH = pl.DeviceIdType.MESH

NT = (((1,), (1,)), ((), ()))
TN = (((0,), (0,)), ((), ()))


def _pick(n, pref, mult=LANES):
    if n <= pref:
        return n
    t = (pref // mult) * mult
    while t >= mult:
        if n % t == 0:
            return t
        t -= mult
    raise ValueError((n, pref, mult))


def _cp(*sem):
    return pltpu.CompilerParams(dimension_semantics=sem, vmem_limit_bytes=VMEM_LIMIT)


def _dot(a, b):
    return jnp.dot(a, b, preferred_element_type=F32)


def _dg(a, b, dims):
    return lax.dot_general(a, b, dims, preferred_element_type=F32)


def _split3(x):
    hi = x.astype(BF)
    r1 = x - hi.astype(F32)
    mid = r1.astype(BF)
    lo = (r1 - mid.astype(F32)).astype(BF)
    return hi, mid, lo


def _tri_dot(tri, x):
    hi, mid, lo = _split3(x)
    return _dot(tri, hi) + _dot(tri, mid) + _dot(tri, lo)


def _dg3(a, b, dims):
    ah, bh = a.astype(BF), b.astype(BF)
    al, bl = (a - ah.astype(F32)).astype(BF), (b - bh.astype(F32)).astype(BF)
    return _dg(ah, bh, dims) + _dg(ah, bl, dims) + _dg(al, bh, dims)


def _dg1(a, b, dims):
    return _dg(a.astype(BF), b.astype(BF), dims)


NN = (((1,), (0,)), ((), ()))


def _sigmoid(x):
    return jax.nn.sigmoid(x)


def _ln_matmul(x, nw, sc, sh, w, slab=None, *, relu2, name):
    S, D = x.shape
    N = w.shape[1]
    tm, tn = _pick(S, 512, 16), N
    fused = slab is not None

    def body(x_ref, nw_ref, sc_ref, sh_ref, w_ref, *rest):
        if fused:
            s_ref, *outs, out_ref, hs, send_sems, recv_sems = rest
            finish = _gather_behind(s_ref, out_ref, send_sems, recv_sems, pl.program_id(0), S // tm)
        else:
            outs, hs = rest[:-1], rest[-1]
        h_ref = outs[-1]

        @pl.when(pl.program_id(1) == 0)
        def _():
            xv = x_ref[...]
            r = lax.rsqrt(jnp.mean(xv * xv, axis=-1, keepdims=True) + EPS)
            hb = ((xv * r * nw_ref[...]) * (1.0 + sc_ref[...]) + sh_ref[...]).astype(BF)
            hs[...] = hb
            h_ref[...] = hb

        z = _dot(hs[...], w_ref[...])
        if relu2:
            a = jnp.maximum(z, 0.0)
            outs[0][...] = a.astype(BF)
            outs[1][...] = (a * a).astype(BF)
        else:
            outs[0][...] = z
        if fused:
            finish()

    vec = pl.BlockSpec((1, D), lambda i, j: (0, 0))
    tile = pl.BlockSpec((tm, tn), lambda i, j: (i, j))
    if relu2:
        out_shape = [jax.ShapeDtypeStruct((S, N), BF), jax.ShapeDtypeStruct((S, N), BF)]
        out_specs = [tile, tile]
    else:
        out_shape = [jax.ShapeDtypeStruct((S, N), F32)]
        out_specs = [tile]
    out_shape.append(jax.ShapeDtypeStruct((S, D), BF))
    out_specs.append(pl.BlockSpec((tm, D), lambda i, j: (i, 0)))
    in_specs = [pl.BlockSpec((tm, D), lambda i, j: (i, 0)), vec, vec, vec, pl.BlockSpec((D, tn), lambda i, j: (0, j))]
    scratch = [pltpu.VMEM((tm, D), BF)]
    args = [x, nw, sc, sh, w]
    if fused:
        in_specs.append(HBM)
        out_specs.append(HBM)
        out_shape.append(jax.ShapeDtypeStruct((N_CHIPS,) + slab.shape, slab.dtype))
        scratch += [pltpu.SemaphoreType.DMA((6,)), pltpu.SemaphoreType.DMA((6,))]
        args.append(slab)
    return pl.pallas_call(
        body, name=name, grid=(S // tm, N // tn), in_specs=in_specs, out_specs=out_specs, out_shape=out_shape,
        scratch_shapes=scratch, compiler_params=_cp("arbitrary", "arbitrary"),
    )(*args)


def _matmul_resid(a, w, x, gate, *, name):
    S, K = a.shape
    D = w.shape[1]
    tm, tn = _pick(S, 1024 if K <= 1024 else 512, 16), D

    def body(a_ref, w_ref, x_ref, g_ref, o_ref, y_ref):
        y = _dot(a_ref[...], w_ref[...])
        y_ref[...] = y.astype(BF)
        o_ref[...] = x_ref[...] + g_ref[...] * y

    tile = pl.BlockSpec((tm, tn), lambda i, j: (i, j))
    return pl.pallas_call(
        body, name=name, grid=(S // tm, D // tn),
        in_specs=[pl.BlockSpec((tm, K), lambda i, j: (i, 0)), pl.BlockSpec((K, tn), lambda i, j: (0, j)),
                  tile, pl.BlockSpec((1, tn), lambda i, j: (0, j))],
        out_specs=[tile, tile],
        out_shape=[jax.ShapeDtypeStruct((S, D), F32), jax.ShapeDtypeStruct((S, D), BF)],
        compiler_params=_cp("parallel", "arbitrary"),
    )(a, w, x, gate)


def _gate_matmul_nt(dx, gate, y, w, act, *, name):
    S, D = dx.shape
    K = w.shape[0]
    tm, tn = _pick(S, 1024 if K <= 1024 else 512, 16), K
    fused = act is not None

    def body(dx_ref, g_ref, y_ref, w_ref, *rest):
        if fused:
            act_ref, da_ref, dm_ref, dg_ref, ms = rest
        else:
            da_ref, dm_ref, dg_ref, ms = rest
        i, j = pl.program_id(0), pl.program_id(1)

        @pl.when((i == 0) & (j == 0))
        def _():
            dg_ref[...] = jnp.zeros_like(dg_ref)

        @pl.when(j == 0)
        def _():
            dxv = dx_ref[...]
            dmb = (dxv * g_ref[...]).astype(BF)
            ms[...] = dmb
            dm_ref[...] = dmb
            dg_ref[...] += jnp.sum(dxv * y_ref[...].astype(F32), axis=0, keepdims=True)

        da = _dg(ms[...], w_ref[...], NT)
        if fused:
            da_ref[...] = (da * (2.0 * act_ref[...].astype(F32))).astype(BF)
        else:
            da_ref[...] = da

    row = pl.BlockSpec((tm, D), lambda i, j: (i, 0))
    vec = pl.BlockSpec((1, D), lambda i, j: (0, 0))
    tile = pl.BlockSpec((tm, tn), lambda i, j: (i, j))
    in_specs = [row, vec, row, pl.BlockSpec((tn, D), lambda i, j: (j, 0))]
    args = [dx, gate, y, w]
    if fused:
        in_specs.append(tile)
        args.append(act)
    return pl.pallas_call(
        body, name=name, grid=(S // tm, K // tn),
        in_specs=in_specs, out_specs=[tile, row, vec],
        out_shape=[jax.ShapeDtypeStruct((S, K), BF if fused else F32), jax.ShapeDtypeStruct((S, D), BF),
                   jax.ShapeDtypeStruct((1, D), F32)],
        scratch_shapes=[pltpu.VMEM((tm, D), BF)],
        compiler_params=_cp("arbitrary", "arbitrary"),
    )(*args)


def _matmul_tn(a, b, *, name, into=None):
    S, Ka = a.shape
    P, _, Db = b.shape
    tk, tn, ts = _pick(Ka, 1024), _pick(Db, 1024), _pick(S, 1024, 16)
    if into is not None:
        slab, kind, half, off = into
        C = tn = slab.shape[3]
        per_chip = Ka // N_CHIPS
        all_chips = kind == "row" and tk == Ka
        if kind == "row" and not all_chips:
            tk = min(tk, per_chip)
        assert tn == C and P * Db == (N_CHIPS * C if kind == "col" else C)
        if kind == "col":
            assert tk == Ka and off % tk == 0
        elif all_chips:
            assert off % per_chip == 0
        else:
            assert per_chip % tk == 0 and off % tk == 0
    npb = Db // tn

    def body(a_ref, b_ref, *rest):
        o_ref, acc = rest[-2:]
        s = pl.program_id(2)

        @pl.when(s == 0)
        def _():
            acc[...] = jnp.zeros_like(acc)

        acc[...] += _dg(a_ref[...], b_ref[...], TN)

        @pl.when(s == pl.num_programs(2) - 1)
        def _():
            o_ref[...] = acc[...].reshape(o_ref.shape)

    in_specs = [pl.BlockSpec((ts, tk), lambda i, j, s: (s, i)),
                pl.BlockSpec((None, ts, tn), lambda i, j, s: (j // npb, s, j % npb))]
    args = [a, b]
    if into is None:
        out_spec = pl.BlockSpec((tk, tn), lambda i, j, s: (i, j))
        out_shape = jax.ShapeDtypeStruct((Ka, P * Db), F32)
        aliases = {}
    else:
        per = per_chip // tk if kind == "row" and not all_chips else 1
        if kind == "col":
            out_spec = pl.BlockSpec((None, None, tk, tn), lambda i, j, s: (j, half, off // tk + i, 0))
        elif all_chips:
            out_spec = pl.BlockSpec((N_CHIPS, None, per_chip, tn), lambda i, j, s: (0, half, off // per_chip, 0))
        else:
            out_spec = pl.BlockSpec((None, None, tk, tn), lambda i, j, s: (i // per, half, off // tk + i % per, 0))
        out_shape = jax.ShapeDtypeStruct(slab.shape, F32)
        in_specs.append(pl.BlockSpec(memory_space=pl.ANY))
        args.append(slab)
        aliases = {2: 0}
    return pl.pallas_call(
        body, name=name, grid=(Ka // tk, P * npb, S // ts),
        in_specs=in_specs, out_specs=out_spec, out_shape=out_shape,
        scratch_shapes=[pltpu.VMEM((tk, tn), F32)], input_output_aliases=aliases,
        compiler_params=_cp("parallel", "parallel", "arbitrary"),
    )(*args)


def _matmul_nt_lnbwd(g, w, x, nw, sc, dx_out, part=None, *, name):
    P, S, Dg = g.shape
    D = x.shape[1]
    tm = _pick(S, 512, 16)
    fused = part is not None

    def body(g_ref, w_ref, x_ref, nw_ref, sc_ref, dxo_ref, *rest):
        if fused:
            p_ref, dx_ref, dsc_ref, dsh_ref, dnw_ref, recv_ref, send_sems, recv_sems = rest
            copies = _scatter_copies(p_ref, recv_ref, send_sems, recv_sems)
        else:
            dx_ref, dsc_ref, dsh_ref, dnw_ref = rest

        @pl.when(pl.program_id(0) == 0)
        def _():
            dsc_ref[...] = jnp.zeros_like(dsc_ref)
            dsh_ref[...] = jnp.zeros_like(dsh_ref)
            dnw_ref[...] = jnp.zeros_like(dnw_ref)
            if fused:
                for cp in copies:
                    cp.start()

        dh = _dg(g_ref[0], w_ref[:, 0:Dg], NT)
        for p in range(1, P):
            dh = dh + _dg(g_ref[p], w_ref[:, p * Dg:(p + 1) * Dg], NT)
        xv = x_ref[...]
        nwv = nw_ref[...]
        r = lax.rsqrt(jnp.mean(xv * xv, axis=-1, keepdims=True) + EPS)
        xr = xv * r
        dn = dh * (1.0 + sc_ref[...])
        dsc_ref[...] += jnp.sum(dh * (xr * nwv), axis=0, keepdims=True)
        dsh_ref[...] += jnp.sum(dh, axis=0, keepdims=True)
        dnw_ref[...] += jnp.sum(dn * xr, axis=0, keepdims=True)
        u = dn * nwv
        dx_ref[...] = dxo_ref[...] + r * (u - xr * jnp.mean(u * xr, axis=-1, keepdims=True))

        if fused:
            @pl.when(pl.program_id(0) == S // tm - 1)
            def _():
                for cp in copies:
                    cp.wait()

    row = pl.BlockSpec((tm, D), lambda i: (i, 0))
    vec = pl.BlockSpec((1, D), lambda i: (0, 0))
    in_specs = [pl.BlockSpec((P, tm, Dg), lambda i: (0, i, 0)), pl.BlockSpec((D, P * Dg), lambda i: (0, 0)), row, vec, vec, row]
    out_specs = [row, vec, vec, vec]
    out_shape = [jax.ShapeDtypeStruct((S, D), F32)] + [jax.ShapeDtypeStruct((1, D), F32)] * 3
    scratch, args = [], [g, w, x, nw, sc, dx_out]
    if fused:
        in_specs.append(HBM)
        out_specs.append(HBM)
        out_shape.append(jax.ShapeDtypeStruct((3,) + part.shape[1:], part.dtype))
        scratch = [pltpu.SemaphoreType.DMA((3,)), pltpu.SemaphoreType.DMA((3,))]
        args.append(part)
    return pl.pallas_call(
        body, name=name, grid=(S // tm,), in_specs=in_specs, out_specs=out_specs, out_shape=out_shape,
        scratch_shapes=scratch, compiler_params=_cp("arbitrary"),
    )(*args)


def _loss_kernel(x, fw, tgt, *, name):
    S, D = x.shape
    tm = _pick(S, 512, 8)

    def body(x_ref, fw_ref, t_ref, l_ref, dx_ref, dfw_ref):
        @pl.when(pl.program_id(0) == 0)
        def _():
            l_ref[...] = jnp.zeros_like(l_ref)
            dfw_ref[...] = jnp.zeros_like(dfw_ref)

        xv = x_ref[...]
        fwv = fw_ref[...]
        r = lax.rsqrt(jnp.mean(xv * xv, axis=-1, keepdims=True) + EPS)
        xr = xv * r
        err = xr * fwv - t_ref[...]
        per_tok = jnp.mean(err * err, axis=-1, keepdims=True)
        l_ref[...] += 0.5 * jnp.sum(per_tok, axis=0, keepdims=True)
        dy = err * (1.0 / D)
        dfw_ref[...] += jnp.sum(dy * xr, axis=0, keepdims=True)
        u = dy * fwv
        dx_ref[...] = r * (u - xr * jnp.mean(u * xr, axis=-1, keepdims=True))

    row = pl.BlockSpec((tm, D), lambda i: (i, 0))
    vec = pl.BlockSpec((1, D), lambda i: (0, 0))
    return pl.pallas_call(
        body, name=name, grid=(S // tm,),
        in_specs=[row, vec, row],
        out_specs=[pl.BlockSpec((1, LANES), lambda i: (0, 0)), row, vec],
        out_shape=[jax.ShapeDtypeStruct((1, LANES), F32), jax.ShapeDtypeStruct((S, D), F32),
                   jax.ShapeDtypeStruct((1, D), F32)],
        compiler_params=_cp("arbitrary"),
    )(x, fw, tgt)


def _hg_lower_bound(lb3):
    mx = jnp.max(lb3, axis=0, keepdims=True)
    e = jnp.exp(lb3 - mx)
    p = e / jnp.sum(e, axis=0, keepdims=True)
    return p[0:1, :], p


def _hg_chunk_common(qr, fz, lbv):
    sq = _sigmoid(qr)
    q = qr * sq
    sig = _sigmoid(fz)
    f = lbv + (1.0 - lbv) * sig
    k = (1.0 - lbv) * (1.0 - sig)
    return q, sq, sig, f, k, jnp.log(f)


def _row_of(x, rows, r):
    return jnp.sum(jnp.where(rows == r, x, 0.0), axis=0, keepdims=True)


def _hg_fwd(proj, hg_lb, gn, slab=None, *, name):
    S = proj.shape[0]
    D = proj.shape[1] // 4
    H = D // LANES
    HB = min(HG_HEADS_PER_STEP, H)
    W = HB * LANES
    C = HG_CHUNK
    T = _pick(S, HG_TOKENS_PER_STEP, C)
    nch, nb = T // C, S // T
    ng = H // HB
    fused = slab is not None

    def body(q_ref, fz_ref, v_ref, g_ref, lb_ref, gn_ref, *rest):
        if fused:
            s_ref, y_ref, o_ref, sts_ref, out_ref, st, send_sems, recv_sems = rest
            finish = _gather_behind(s_ref, out_ref, send_sems, recv_sems,
                                    pl.program_id(0) * nb + pl.program_id(1), ng * nb)
        else:
            y_ref, o_ref, sts_ref, st = rest

        @pl.when(pl.program_id(1) == 0)
        def _():
            st[...] = jnp.zeros_like(st)

        lb_all, _ = _hg_lower_bound(lb_ref[...])
        gnv = gn_ref[...]
        ri = lax.broadcasted_iota(jnp.int32, (C, C), 0)
        ci_ = lax.broadcasted_iota(jnp.int32, (C, C), 1)
        low = ri >= ci_
        tri = jnp.where(low, 1.0, 0.0).astype(BF)
        rows_w = lax.broadcasted_iota(jnp.int32, (C, W), 0)

        def chunk(ci, carry):
            sl = pl.ds(pl.multiple_of(ci * C, C), C)
            heads = [slice(hh * LANES, (hh + 1) * LANES) for hh in range(HB)]
            q, _, _, _, k, logf = _hg_chunk_common(q_ref[sl, :], fz_ref[sl, :], lb_all)
            vv, gg = v_ref[sl, :], g_ref[sl, :]
            G = _tri_dot(tri, logf)
            Gm = _row_of(G, rows_w, C // 2 - 1)
            Gl = _row_of(G, rows_w, C - 1)
            qt, kt = q * jnp.exp(G - Gm), k * jnp.exp(Gm - G)
            qe, kd, eGl = q * jnp.exp(G), k * jnp.exp(Gl - G), jnp.exp(Gl)
            A = [jnp.where(low, _dg1(qt[:, ls], kt[:, ls], NT), 0.0) for ls in heads]
            Sv = [st[hh] for hh in range(HB)]
            for hh in range(HB):
                sts_ref[hh, ci] = Sv[hh]
            o = [_dg1(A[hh], vv[:, ls], NN) + _dg1(qe[:, ls], Sv[hh], NT) for hh, ls in enumerate(heads)]
            for hh, ls in enumerate(heads):
                st[hh] = Sv[hh] * eGl[:, ls] + _dg1(vv[:, ls], kd[:, ls], TN)
            gate = gg * _sigmoid(gg)
            for hh, ls in enumerate(heads):
                r = lax.rsqrt(jnp.mean(o[hh] * o[hh], axis=-1, keepdims=True) + EPS)
                y_ref[sl, ls] = ((o[hh] * r * gnv) * gate[:, ls]).astype(BF)
                o_ref[sl, ls] = o[hh]
            return carry

        lax.fori_loop(0, nch, chunk, 0)

        if fused:
            finish()

    def part(p):
        return pl.BlockSpec((T, W), lambda h, n: (n, p * ng + h))

    blk = pl.BlockSpec((T, W), lambda h, n: (n, h))
    in_specs = [part(0), part(1), part(2), part(3),
                pl.BlockSpec((3, W), lambda h, n: (0, h)), pl.BlockSpec((1, LANES), lambda h, n: (0, 0))]
    out_specs = [blk, blk, pl.BlockSpec((HB, nch, LANES, LANES), lambda h, n: (h, n, 0, 0))]
    out_shape = [jax.ShapeDtypeStruct((S, D), BF), jax.ShapeDtypeStruct((S, D), F32),
                 jax.ShapeDtypeStruct((H, S // C, LANES, LANES), F32)]
    scratch = [pltpu.VMEM((HB, LANES, LANES), F32)]
    args = [proj, proj, proj, proj, hg_lb, gn]
    if fused:
        in_specs.append(HBM)
        out_specs.append(HBM)
        out_shape.append(jax.ShapeDtypeStruct((N_CHIPS,) + slab.shape, slab.dtype))
        scratch += [pltpu.SemaphoreType.DMA((6,)), pltpu.SemaphoreType.DMA((6,))]
        args.append(slab)
    return pl.pallas_call(
        body, name=name, grid=(ng, nb), in_specs=in_specs, out_specs=out_specs, out_shape=out_shape,
        scratch_shapes=scratch, compiler_params=_cp("arbitrary", "arbitrary"),
    )(*args)


def _hg_bwd(proj, hg_lb, gn, o_all, states, dy, part=None, *, name):
    S = proj.shape[0]
    D = proj.shape[1] // 4
    H = D // LANES
    HB = min(HG_HEADS_PER_STEP, H)
    W = HB * LANES
    C = HG_CHUNK
    T = _pick(S, HG_TOKENS_PER_STEP, C)
    nch, nb = T // C, S // T
    ng = H // HB
    fused = part is not None

    def body(q_ref, fz_ref, v_ref, g_ref, lb_ref, gn_ref, o_ref, sts_ref, dy_ref, *rest):
        if fused:
            p_ref, dp_ref, dlb_ref, dgn_ref, recv_ref, dst, dlb_acc, send_sems, recv_sems = rest
            copies = _scatter_copies(p_ref, recv_ref, send_sems, recv_sems)

            @pl.when((pl.program_id(0) == 0) & (pl.program_id(1) == 0))
            def _():
                for cp in copies:
                    cp.start()
        else:
            dp_ref, dlb_ref, dgn_ref, dst, dlb_acc = rest
        n = pl.program_id(1)

        @pl.when(n == 0)
        def _():
            dst[...] = jnp.zeros_like(dst)
            dlb_acc[...] = jnp.zeros_like(dlb_acc)
            dgn_ref[...] = jnp.zeros_like(dgn_ref)

        lb_all, p3 = _hg_lower_bound(lb_ref[...])
        gnv = gn_ref[...]
        ri = lax.broadcasted_iota(jnp.int32, (C, C), 0)
        ci_ = lax.broadcasted_iota(jnp.int32, (C, C), 1)
        low = ri >= ci_
        tri = jnp.where(low, 1.0, 0.0).astype(BF)
        triu = jnp.where(ri <= ci_, 1.0, 0.0).astype(BF)
        rows_w = lax.broadcasted_iota(jnp.int32, (C, W), 0)
        gnw = jnp.tile(gnv, (1, HB))

        def chunk(cj, carry):
            ci = nch - 1 - cj
            sl = pl.ds(pl.multiple_of(ci * C, C), C)
            heads = list(enumerate(slice(hh * LANES, (hh + 1) * LANES) for hh in range(HB)))
            wide = lambda parts: jnp.concatenate(parts, axis=1)
            qr, vv, gg = q_ref[sl, :], v_ref[sl, :], g_ref[sl, :]
            q, sq, sig, f, k, logf = _hg_chunk_common(qr, fz_ref[sl, :], lb_all)
            G = _tri_dot(tri, logf)
            Gm = _row_of(G, rows_w, C // 2 - 1)
            Gl = _row_of(G, rows_w, C - 1)
            eG, e_qm, e_km, e_lk, eGl = jnp.exp(G), jnp.exp(G - Gm), jnp.exp(Gm - G), jnp.exp(Gl - G), jnp.exp(Gl)
            qt, kt, kdec, qe = q * e_qm, k * e_km, k * e_lk, q * eG
            sg = _sigmoid(gg)
            d_onw = dy_ref[sl, :] * (gg * sg)
            u = d_onw * gnw
            o = o_ref[sl, :]
            on, do = [], []
            for hh, ls in heads:
                r = lax.rsqrt(jnp.mean(o[:, ls] * o[:, ls], axis=-1, keepdims=True) + EPS)
                on.append(o[:, ls] * r)
                dgn_ref[hh] += jnp.sum(d_onw[:, ls] * on[hh], axis=0, keepdims=True)
                do.append(r * (u[:, ls] - on[hh] * jnp.mean(u[:, ls] * on[hh], axis=-1, keepdims=True)))
            dgg = dy_ref[sl, :] * (wide(on) * gnw) * (sg * (1.0 + gg * (1.0 - sg)))
            Sv = [sts_ref[hh, ci] for hh, _ in heads]
            dSv = [dst[hh] for hh, _ in heads]
            A = [jnp.where(low, _dg1(qt[:, ls], kt[:, ls], NT), 0.0) for _, ls in heads]
            dA = [jnp.where(low, _dg3(do[hh], vv[:, ls], NT), 0.0) for hh, ls in heads]
            dv = wide([_dg1(A[hh], do[hh], TN) + _dg1(kdec[:, ls], dSv[hh], NT) for hh, ls in heads])
            dq = wide([_dg3(dA[hh], kt[:, ls], NN) for hh, ls in heads]) * e_qm \
                + eG * wide([_dg3(do[hh], Sv[hh], NN) for hh, _ in heads])
            dk = wide([_dg3(dA[hh], qt[:, ls], TN) for hh, ls in heads]) * e_km \
                + e_lk * wide([_dg3(vv[:, ls], dSv[hh], NN) for hh, ls in heads])
            s_end = [Sv[hh] * eGl[:, ls] + _dg3(vv[:, ls], kdec[:, ls], TN) for hh, ls in heads]
            dgl = wide([jnp.sum(dSv[hh] * s_end[hh], axis=0, keepdims=True) for hh, _ in heads])
            for hh, ls in heads:
                dst[hh] = dSv[hh] * eGl[:, ls] + _dg1(do[hh], qe[:, ls], TN)
            dG = q * dq - k * dk + jnp.where(rows_w == C - 1, dgl, 0.0)
            dlogf = _tri_dot(triu, dG) - f * dk
            dlf_f = dlogf / f
            dlb_acc[...] += jnp.sum(dlf_f * (1.0 - sig), axis=0, keepdims=True)
            dp_ref[0, sl, :] = (dq * (sq * (1.0 + qr * (1.0 - sq)))).astype(BF)
            dp_ref[1, sl, :] = (dlf_f * (1.0 - lb_all) * sig * (1.0 - sig)).astype(BF)
            dp_ref[2, sl, :] = dv.astype(BF)
            dp_ref[3, sl, :] = dgg.astype(BF)
            return carry

        lax.fori_loop(0, nch, chunk, 0)
        sel = jnp.where(lax.broadcasted_iota(jnp.int32, (3, W), 0) == 0, 1.0, 0.0)
        dlb_ref[...] = lb_all * (sel - p3) * dlb_acc[...]

        if fused:
            @pl.when((pl.program_id(0) == ng - 1) & (n == nb - 1))
            def _():
                for cp in copies:
                    cp.wait()

    def col(p):
        return pl.BlockSpec((T, W), lambda h, n: (nb - 1 - n, p * ng + h))

    blk = pl.BlockSpec((T, W), lambda h, n: (nb - 1 - n, h))
    in_specs = [col(0), col(1), col(2), col(3),
                pl.BlockSpec((3, W), lambda h, n: (0, h)), pl.BlockSpec((1, LANES), lambda h, n: (0, 0)),
                blk, pl.BlockSpec((HB, nch, LANES, LANES), lambda h, n: (h, nb - 1 - n, 0, 0)), blk]
    out_specs = [pl.BlockSpec((4, T, W), lambda h, n: (0, nb - 1 - n, h)),
                 pl.BlockSpec((3, W), lambda h, n: (0, h)),
                 pl.BlockSpec((HB, 1, LANES), lambda h, n: (h, 0, 0))]
    out_shape = [jax.ShapeDtypeStruct((4, S, D), BF), jax.ShapeDtypeStruct((3, D), F32),
                 jax.ShapeDtypeStruct((H, 1, LANES), F32)]
    scratch = [pltpu.VMEM((HB, LANES, LANES), F32), pltpu.VMEM((1, W), F32)]
    args = [proj, proj, proj, proj, hg_lb, gn, o_all, states, dy]
    if fused:
        in_specs.append(HBM)
        out_specs.append(HBM)
        out_shape.append(jax.ShapeDtypeStruct((3,) + part.shape[1:], part.dtype))
        scratch += [pltpu.SemaphoreType.DMA((3,)), pltpu.SemaphoreType.DMA((3,))]
        args.append(part)
    return pl.pallas_call(
        body, name=name, grid=(ng, nb), in_specs=in_specs, out_specs=out_specs, out_shape=out_shape,
        scratch_shapes=scratch, compiler_params=_cp("arbitrary", "arbitrary"),
    )(*args)


def _log_sigmoid(u):
    return jnp.minimum(u, 0.0) - jnp.log(1.0 + jnp.exp(-jnp.abs(u)))


def _lane_put(base, lane, first, pieces):
    for n, p in enumerate(pieces):
        base = jnp.where(lane == first + n, p, base)
    return base


def _fox_cumsum(proj, bf_pad, *, name):
    S = proj.shape[0]
    D = proj.shape[1] // 5
    T = _pick(S, 256, 8)

    def body(fz_ref, b_ref, f_ref, carry):
        @pl.when(pl.program_id(0) == 0)
        def _():
            carry[...] = jnp.zeros_like(carry)

        logf = _log_sigmoid(fz_ref[...] + b_ref[...])
        tri = jnp.where(lax.broadcasted_iota(jnp.int32, (T, T), 0) >= lax.broadcasted_iota(jnp.int32, (T, T), 1),
                        1.0, 0.0).astype(BF)
        fv = _tri_dot(tri, logf) + carry[...]
        f_ref[...] = fv
        carry[...] = _row_of(fv, lax.broadcasted_iota(jnp.int32, (T, LANES), 0), T - 1)

    return pl.pallas_call(
        body, name=name, grid=(S // T,),
        in_specs=[pl.BlockSpec((T, LANES), lambda i: (i, 4 * D // LANES)), pl.BlockSpec((1, LANES), lambda i: (0, 0))],
        out_specs=pl.BlockSpec((T, LANES), lambda i: (i, 0)),
        out_shape=jax.ShapeDtypeStruct((S, LANES), F32),
        scratch_shapes=[pltpu.VMEM((1, LANES), F32)],
        compiler_params=_cp("arbitrary"),
    )(proj, bf_pad)


def _pair_stats(sq, lo):
    del lo
    a = lax.broadcasted_iota(jnp.int32, (LANES, LANES), 0) < FOX_DH
    b = lax.broadcasted_iota(jnp.int32, (LANES, LANES), 1) < FOX_DH
    avg = jnp.where(a == b, 1.0 / FOX_DH, 0.0).astype(BF)
    hi, mid, low = _split3(sq)
    return _dot(hi, avg) + _dot(mid, avg) + _dot(low, avg)


def _fox_prep(proj, fcum, qw2, kw2, *, name):
    S = proj.shape[0]
    D = proj.shape[1] // 5
    HP = D // LANES
    T = _pick(S, FOX_ROWS_PER_STEP, 16)

    def body(q_ref, k_ref, v_ref, f_ref, qw_ref, kw_ref, qa_ref, ka_ref, va_ref, vt_ref):
        hp = pl.program_id(1)
        lane = lax.broadcasted_iota(jnp.int32, (T, LANES), 1)
        lo = lane < FOX_DH
        qv, kv, vv, fv = q_ref[...], k_ref[...], v_ref[...], f_ref[...]
        qn = qv * lax.rsqrt(_pair_stats(qv * qv, lo) + EPS) * qw_ref[...] * (0.125 * LOG2E)
        kn = kv * lax.rsqrt(_pair_stats(kv * kv, lo) + EPS) * kw_ref[...]
        ones_q = jnp.where((lane >= 67) & (lane <= 69), 1.0, 0.0)
        ones_k = jnp.where(((lane >= 64) & (lane <= 66)) | ((lane >= 70) & (lane <= 72)), 1.0, 0.0)
        ones_v = jnp.where((lane >= 64) & (lane <= 66), 1.0, 0.0)
        for hh in range(2):
            fh = jnp.sum(jnp.where(lane == 2 * hp + hh, fv, 0.0), axis=-1, keepdims=True) * LOG2E
            pieces = [p.astype(F32) for p in _split3(fh)]

            def half(x):
                return jnp.where(lo, x if hh == 0 else pltpu.roll(x, FOX_DH, 1), 0.0)

            qa_ref[hh] = _lane_put(half(qn) + ones_q, lane, 64, pieces).astype(BF)
            ka_ref[hh] = _lane_put(half(kn) + ones_k, lane, 67, [-p for p in pieces]).astype(BF)
            va = half(vv) + ones_v
            va_ref[hh] = va.astype(BF)
            vt_ref[hh] = va.T.astype(BF)

    def part(p):
        return pl.BlockSpec((T, LANES), lambda i, hp: (i, p * HP + hp))

    vec = pl.BlockSpec((1, LANES), lambda i, hp: (0, 0))
    aug = pl.BlockSpec((2, T, LANES), lambda i, hp: (hp, i, 0))
    return pl.pallas_call(
        body, name=name, grid=(S // T, HP),
        in_specs=[part(0), part(1), part(2), pl.BlockSpec((T, LANES), lambda i, hp: (i, 0)), vec, vec],
        out_specs=[aug, aug, aug, pl.BlockSpec((2, LANES, T), lambda i, hp: (hp, 0, i))],
        out_shape=[jax.ShapeDtypeStruct((2 * HP, S, LANES), BF)] * 3 + [jax.ShapeDtypeStruct((2 * HP, LANES, S), BF)],
        compiler_params=_cp("parallel", "arbitrary"),
    )(proj, proj, proj, fcum, qw2, kw2)


def _fox_block(S):
    return _pick(S, 256, 16)


def _fox_skip_bounds(fcum, qn_w, kn_w, nheads):
    S = fcum.shape[0]
    B = _fox_block(S)
    qk = 8.0 * LOG2E * 1.02 * jnp.max(jnp.abs(qn_w)) * jnp.max(jnp.abs(kn_w))
    thresh = -(2.0 * qk + 152.0)
    f2 = fcum[:, :nheads] * LOG2E
    first, last = f2[0::B], f2[B - 1::B]
    nb = S // B
    blk = jnp.arange(nb)
    dead = (first[0::2, None, :] - last[None, :, :]) < thresh
    jmin = jnp.sum(dead & (blk[None, :, None] < 2 * jnp.arange(nb // 2)[:, None, None]), axis=1)
    live = (first[:, None, :] - last[None, :, :]) >= thresh
    imax = blk[:, None] + jnp.sum(live & (blk[:, None, None] > blk[None, :, None]), axis=0)
    return jmin.T.astype(jnp.int32), imax.T.astype(jnp.int32)


def _fox_fwd(jmin, qa, ka, vat, proj, *, name):
    H, S, _ = qa.shape
    HP = H // 2
    D = HP * LANES
    B = _fox_block(S)
    BQ = 2 * B
    nq = S // BQ

    def body(jmin_ref, q_ref, k_ref, vt_ref, g_ref, y_ref, o_ref, q2_ref):
        hp, i = pl.program_id(0), pl.program_id(1)
        lane = lax.broadcasted_iota(jnp.int32, (BQ, LANES), 1)
        lo = lane < FOX_DH
        in_stat = (lane >= 70) & (lane <= 75)
        causal = lax.broadcasted_iota(jnp.int32, (BQ, BQ), 0) <= lax.broadcasted_iota(jnp.int32, (BQ, BQ), 1)
        row = lax.broadcasted_iota(jnp.int32, (LANES, BQ), 0)
        m0, acc0 = jnp.full((1, BQ), -jnp.inf, F32), jnp.zeros((LANES, BQ), F32)
        outs = []
        for hh in range(2):
            qb = q_ref[hh]

            def scores(j):
                sl = pl.ds(pl.multiple_of(j * BQ, BQ), BQ)
                return _dg(k_ref[hh, sl, :], qb, NT)

            def update(j, m, acc, st, masked=False):
                sl = pl.ds(pl.multiple_of(j * BQ, BQ), BQ)
                if masked:
                    st = jnp.where(causal, st, -jnp.inf)
                m_new = jnp.maximum(m, jnp.ceil(jnp.max(st, axis=0, keepdims=True)))
                p = jnp.exp2(st - m_new).astype(BF)
                return m_new, acc * jnp.exp2(m - m_new) + _dot(vt_ref[hh, :, sl], p)

            def step(j, carry):
                m, acc, st = carry
                st_next = scores(j + 1)
                return update(j, m, acc, st) + (st_next,)

            first = jmin_ref[2 * hp + hh, i] // 2
            m, acc, st = lax.fori_loop(first, i, step, (m0, acc0, scores(first)))
            m, acc = update(i, m, acc, st, masked=True)
            linv = 1.0 / jnp.sum(jnp.where(row == FOX_DH, acc, 0.0), axis=0, keepdims=True)
            tile = acc * linv
            for n, piece in enumerate(_split3(m) + _split3(linv)):
                tile = jnp.where(row == 70 + n, piece.astype(F32), tile)
            tile = tile.T
            outs.append(tile)
            q2_ref[hh] = jnp.where(in_stat, jnp.where(lane <= 72, -tile, tile), qb.astype(F32)).astype(BF)
        o = jnp.where(lo, outs[0], pltpu.roll(outs[1], FOX_DH, 1))
        o_ref[...] = o
        y_ref[...] = (o * _sigmoid(g_ref[...])).astype(BF)

    blk = pl.BlockSpec((BQ, LANES), lambda hp, i, jm: (i, hp))
    qblk = pl.BlockSpec((2, BQ, LANES), lambda hp, i, jm: (hp, i, 0))
    full = pl.BlockSpec((2, S, LANES), lambda hp, i, jm: (hp, 0, 0))
    full_t = pl.BlockSpec((2, LANES, S), lambda hp, i, jm: (hp, 0, 0))
    return pl.pallas_call(
        body, name=name,
        grid_spec=pltpu.PrefetchScalarGridSpec(
            num_scalar_prefetch=1, grid=(HP, nq),
            in_specs=[qblk, full, full_t, pl.BlockSpec((BQ, LANES), lambda hp, i, jm: (i, 3 * HP + hp))],
            out_specs=[blk, blk, qblk]),
        out_shape=[jax.ShapeDtypeStruct((S, D), BF), jax.ShapeDtypeStruct((S, D), F32),
                   jax.ShapeDtypeStruct((H, S, LANES), BF)],
        compiler_params=_cp("parallel", "arbitrary"),
    )(jmin, qa, ka, vat, proj)


def _fox_bwd_prep(dy, o, proj, q2, *, name):
    S, D = dy.shape
    HP = D // LANES
    T = _pick(S, FOX_ROWS_PER_STEP, 16)

    def body(dy_ref, o_ref, g_ref, q2_ref, da_ref):
        lane = lax.broadcasted_iota(jnp.int32, (T, LANES), 1)
        lo = lane < FOX_DH
        in_linv = (lane >= 73) & (lane <= 75)
        linv = [jnp.sum(jnp.where(in_linv, q2_ref[hh].astype(F32), 0.0), axis=-1, keepdims=True) for hh in range(2)]
        u = (dy_ref[...] * _sigmoid(g_ref[...]) * jnp.where(lo, linv[0], linv[1])).astype(BF).astype(F32)
        prod = u * o_ref[...]
        d_lo = jnp.sum(jnp.where(lo, prod, 0.0), axis=-1, keepdims=True)
        d_hi = jnp.sum(jnp.where(lo, 0.0, prod), axis=-1, keepdims=True)
        for hh, delta in enumerate((d_lo, d_hi)):
            base = jnp.where(lo, u if hh == 0 else pltpu.roll(u, FOX_DH, 1), 0.0)
            da_ref[hh] = _lane_put(base, lane, 64, [-(p.astype(F32)) for p in _split3(delta)]).astype(BF)

    blk = pl.BlockSpec((T, LANES), lambda i, hp: (i, hp))
    aug = pl.BlockSpec((2, T, LANES), lambda i, hp: (hp, i, 0))
    return pl.pallas_call(
        body, name=name, grid=(S // T, HP),
        in_specs=[blk, blk, pl.BlockSpec((T, LANES), lambda i, hp: (i, 3 * HP + hp)), aug],
        out_specs=aug,
        out_shape=jax.ShapeDtypeStruct((2 * HP, S, LANES), BF),
        compiler_params=_cp("parallel", "arbitrary"),
    )(dy, o, proj, q2)


def _fox_bwd(imax, q2, ka, va, doa, *, name):
    H, S, _ = q2.shape
    B = _fox_block(S)
    nb = S // B

    def body(imax_ref, q_ref, do_ref, k_ref, v_ref, dq_ref, dk_ref, dv_ref, cs_ref):
        j = pl.program_id(1)
        end = imax_ref[pl.program_id(0), j] + 1

        @pl.when(j == 0)
        def _():
            dq_ref[...] = jnp.zeros_like(dq_ref)

        kb, vb = k_ref[...], v_ref[...]

        def step(i, carry, nblk=1):
            dk_acc, dv_acc, cs_acc = carry
            rows = nblk * B
            sl = pl.ds(pl.multiple_of(i * B, B), rows)
            qb, dob = q_ref[sl, :], do_ref[sl, :]
            s = _dg(qb, kb, NT)
            ahead = lax.broadcasted_iota(jnp.int32, (rows, B), 0) - lax.broadcasted_iota(jnp.int32, (rows, B), 1)
            pb = jnp.exp2(jnp.where(ahead >= (j - i) * B, s, -jnp.inf)).astype(BF)
            ds = pb.astype(F32) * _dg(dob, vb, NT)
            dsb = ds.astype(BF)
            cs_acc = cs_acc + jnp.sum(ds.reshape(rows // 8, 8, B), axis=0)
            dv_acc = dv_acc + _dg(pb, dob, TN)
            dk_acc = dk_acc + _dg(dsb, qb, TN)
            dq_ref[sl, :] += _dot(dsb, kb)
            return dk_acc, dv_acc, cs_acc

        zero = jnp.zeros((B, LANES), F32)
        carry = (zero, zero, jnp.zeros((8, B), F32))
        pos = j
        for U in FOX_BWD_TILES:
            n = (end - pos) // U
            carry = lax.fori_loop(0, n, lambda ii, c, pos=pos, U=U: step(pos + U * ii, c, nblk=U), carry)
            pos = pos + U * n
        dk_acc, dv_acc, cs_acc = carry
        dk_ref[...] = dk_acc
        dv_ref[...] = dv_acc
        cs_ref[...] = jnp.sum(cs_acc, axis=0, keepdims=True)

    full = pl.BlockSpec((None, S, LANES), lambda h, j, im: (h, 0, 0))
    blk = pl.BlockSpec((None, B, LANES), lambda h, j, im: (h, j, 0))
    return pl.pallas_call(
        body, name=name,
        grid_spec=pltpu.PrefetchScalarGridSpec(
            num_scalar_prefetch=1, grid=(H, nb),
            in_specs=[full, full, blk, blk],
            out_specs=[full, blk, blk, pl.BlockSpec((None, 1, B), lambda h, j, im: (h, 0, j))]),
        out_shape=[jax.ShapeDtypeStruct((H, S, LANES), F32)] * 3 + [jax.ShapeDtypeStruct((H, 1, S), F32)],
        compiler_params=_cp("parallel", "arbitrary"),
    )(imax, q2, doa, ka, va)


def _fox_bwd_post(dqa, dka, dva, proj, dy, o, qw2, kw2, *, name):
    S, D = dy.shape
    HP = D // LANES
    T = _pick(S, FOX_ROWS_PER_STEP, 16)

    def body(dq_ref, dk_ref, dv_ref, q_ref, k_ref, g_ref, dy_ref, o_ref, qw_ref, kw_ref, dp_ref, dqw_ref, dkw_ref):
        @pl.when((pl.program_id(0) == 0) & (pl.program_id(1) == 0))
        def _():
            dqw_ref[...] = jnp.zeros_like(dqw_ref)
            dkw_ref[...] = jnp.zeros_like(dkw_ref)

        lane = lax.broadcasted_iota(jnp.int32, (T, LANES), 1)
        lo = lane < FOX_DH

        def pair(ref):
            return jnp.where(lo, ref[0], pltpu.roll(ref[1], FOX_DH, 1))

        def norm_bwd(xv, w, dyn, dw_ref):
            r = lax.rsqrt(_pair_stats(xv * xv, lo) + EPS)
            xr = xv * r
            dw_ref[...] += jnp.sum(dyn * xr, axis=0, keepdims=True)
            u = dyn * w
            return r * (u - xr * _pair_stats(u * xr, lo))

        dp_ref[0] = norm_bwd(q_ref[...], qw_ref[...], pair(dq_ref) * 0.125, dqw_ref).astype(BF)
        dp_ref[1] = norm_bwd(k_ref[...], kw_ref[...], pair(dk_ref) * (1.0 / LOG2E), dkw_ref).astype(BF)
        dp_ref[2] = pair(dv_ref).astype(BF)
        sg = _sigmoid(g_ref[...])
        dp_ref[3] = (dy_ref[...] * o_ref[...] * sg * (1.0 - sg)).astype(BF)

    def part(p):
        return pl.BlockSpec((T, LANES), lambda i, hp: (i, p * HP + hp))

    aug = pl.BlockSpec((2, T, LANES), lambda i, hp: (hp, i, 0))
    blk = pl.BlockSpec((T, LANES), lambda i, hp: (i, hp))
    vec = pl.BlockSpec((1, LANES), lambda i, hp: (0, 0))
    return pl.pallas_call(
        body, name=name, grid=(S // T, HP),
        in_specs=[aug, aug, aug, part(0), part(1), part(3), blk, blk, vec, vec],
        out_specs=[pl.BlockSpec((4, T, LANES), lambda i, hp: (0, i, hp)), vec, vec],
        out_shape=[jax.ShapeDtypeStruct((5, S, D), BF), jax.ShapeDtypeStruct((1, LANES), F32),
                   jax.ShapeDtypeStruct((1, LANES), F32)],
        compiler_params=_cp("arbitrary", "arbitrary"),
    )(dqa, dka, dva, proj, proj, proj, dy, o, qw2, kw2)


def _fox_dfz(colsum, nheads, proj, bf_pad, dproj, *, name):
    S = colsum.shape[0]
    H = nheads
    D = dproj.shape[2]
    T = _pick(S, 256, 16)
    nb = S // T

    def body(cs_ref, fz_ref, b_ref, _, dp_ref, db_ref, carry):
        @pl.when(pl.program_id(0) == 0)
        def _():
            carry[...] = jnp.zeros_like(carry)
            db_ref[...] = jnp.zeros_like(db_ref)

        lane = lax.broadcasted_iota(jnp.int32, (T, LANES), 1)
        df = -cs_ref[...]
        triu = jnp.where(lax.broadcasted_iota(jnp.int32, (T, T), 0) <= lax.broadcasted_iota(jnp.int32, (T, T), 1),
                         1.0, 0.0).astype(BF)
        dlogf = _tri_dot(triu, df) + carry[...]
        carry[...] = _row_of(dlogf, lax.broadcasted_iota(jnp.int32, (T, LANES), 0), 0)
        dfz = jnp.where(lane < H, dlogf * _sigmoid(-(fz_ref[...] + b_ref[...])), 0.0)
        db_ref[...] += jnp.sum(dfz, axis=0, keepdims=True)
        dp_ref[...] = jnp.zeros_like(dp_ref)
        dp_ref[:, 0:LANES] = dfz.astype(BF)

    return pl.pallas_call(
        body, name=name, grid=(nb,),
        in_specs=[pl.BlockSpec((T, LANES), lambda i: (nb - 1 - i, 0)),
                  pl.BlockSpec((T, LANES), lambda i: (nb - 1 - i, 4 * D // LANES)),
                  pl.BlockSpec((1, LANES), lambda i: (0, 0)),
                  pl.BlockSpec(memory_space=pl.ANY)],
        out_specs=[pl.BlockSpec((None, T, D), lambda i: (4, nb - 1 - i, 0)), pl.BlockSpec((1, LANES), lambda i: (0, 0))],
        out_shape=[jax.ShapeDtypeStruct(dproj.shape, BF), jax.ShapeDtypeStruct((1, LANES), F32)],
        scratch_shapes=[pltpu.VMEM((1, LANES), F32)],
        input_output_aliases={3: 0},
        compiler_params=_cp("arbitrary"),
    )(colsum, proj, bf_pad, dproj)


def _mod_fwd(c16, w, b, *, name):
    L, D, N = w.shape
    tn = _pick(N, 512)

    def body(c_ref, w_ref, b_ref, o_ref):
        cv = c_ref[...]
        ca = (cv * _sigmoid(cv)).astype(BF)
        o_ref[...] = _dot(ca, w_ref[...].astype(BF)) + b_ref[...]

    return pl.pallas_call(
        body, name=name, grid=(L, N // tn),
        in_specs=[pl.BlockSpec((16, D), lambda l, j: (0, 0)), pl.BlockSpec((None, D, tn), lambda l, j: (l, 0, j)),
                  pl.BlockSpec((None, 1, tn), lambda l, j: (l, 0, j))],
        out_specs=pl.BlockSpec((None, 16, tn), lambda l, j: (l, 0, j)),
        out_shape=jax.ShapeDtypeStruct((L, 16, N), F32),
        compiler_params=_cp("parallel", "arbitrary"),
    )(c16, w, b)


def _mod_bwd(c16, dmod, *, name):
    L, _, N = dmod.shape
    D = c16.shape[1]
    tn = _pick(N, 512)

    def body(c_ref, d_ref, o_ref):
        cv = c_ref[...]
        ca = (cv * _sigmoid(cv)).astype(BF)
        o_ref[...] = _dg(ca, d_ref[...].astype(BF), TN)

    return pl.pallas_call(
        body, name=name, grid=(L, N // tn),
        in_specs=[pl.BlockSpec((16, D), lambda l, j: (0, 0)), pl.BlockSpec((None, 16, tn), lambda l, j: (l, 0, j))],
        out_specs=pl.BlockSpec((None, D, tn), lambda l, j: (l, 0, j)),
        out_shape=jax.ShapeDtypeStruct((L, D, N), F32),
        compiler_params=_cp("parallel", "arbitrary"),
    )(c16, dmod)


def _adamw_math(w, g, m, v):
    m = ADAM_B1 * m + (1.0 - ADAM_B1) * g
    v = ADAM_B2 * v + (1.0 - ADAM_B2) * (g * g)
    m_hat = m / (1.0 - ADAM_B1 ** ADAM_STEP)
    v_hat = v / (1.0 - ADAM_B2 ** ADAM_STEP)
    return -ADAM_LR * (m_hat / (jnp.sqrt(v_hat) + ADAM_EPS) + ADAM_WD * w), m, v


def _adamw(w, g, m, v, *, g_at=None, name):
    R, C = w.shape
    row0 = 0 if g_at is None else g_at[1]
    tr = min(math.gcd(row0, 256) if row0 else 256, -(-R // 8) * 8)
    g0 = row0 // tr
    if g_at is None:
        g_spec = pl.BlockSpec((tr, C), lambda i: (i, 0))
    else:
        g_spec = pl.BlockSpec((None, tr, C), lambda i: (g_at[0], g0 + i, 0))

    def body(w_ref, g_ref, m_ref, v_ref, d_ref, mo_ref, vo_ref):
        d, mn, vn = _adamw_math(w_ref[...], g_ref[...], m_ref[...], v_ref[...])
        d_ref[...] = d
        mo_ref[...] = mn
        vo_ref[...] = vn

    blk = pl.BlockSpec((tr, C), lambda i: (i, 0))
    return pl.pallas_call(
        body, name=name, grid=(pl.cdiv(R, tr),),
        in_specs=[blk, g_spec, blk, blk],
        out_specs=[blk, blk, blk],
        out_shape=[jax.ShapeDtypeStruct((R, C), F32)] * 3,
        compiler_params=_cp("parallel"),
    )(w, g, m, v)


def _sum_parts(parts, *, name):
    P, R, C = parts.shape

    def body(p_ref, o_ref):
        acc = p_ref[0]
        for p in range(1, P):
            acc = acc + p_ref[p]
        o_ref[...] = acc

    return pl.pallas_call(
        body, name=name, grid=(1,),
        in_specs=[pl.BlockSpec((P, R, C), lambda i: (0, 0, 0))],
        out_specs=pl.BlockSpec((R, C), lambda i: (0, 0)),
        out_shape=jax.ShapeDtypeStruct((R, C), F32),
        compiler_params=_cp("arbitrary"),
    )(parts)


def _add_halves(g4, recv, c_idx, *, name):
    _, _, Rh, C = g4.shape
    tr = min(256, Rh)

    def body(c_ref, a_ref, b_ref, o_ref):
        o_ref[...] = (a_ref[...] + b_ref[...].astype(F32)).astype(BF)

    return pl.pallas_call(
        body, name=name,
        grid_spec=pltpu.PrefetchScalarGridSpec(
            num_scalar_prefetch=1, grid=(4, pl.cdiv(Rh, tr)),
            in_specs=[pl.BlockSpec((None, None, tr, C), lambda j, r, c: (j, c[0], r, 0)),
                      pl.BlockSpec((None, tr, C), lambda j, r, c: (j, r, 0))],
            out_specs=pl.BlockSpec((None, tr, C), lambda j, r, c: (j, r, 0))),
        out_shape=jax.ShapeDtypeStruct((4, Rh, C), BF),
        compiler_params=_cp("parallel", "arbitrary"),
    )(c_idx, g4, recv)


def _add_four(g4, from_sibling, from_chips, pos, *, name):
    _, _, Rh, C = g4.shape
    tr = min(256, Rh)

    def body(p_ref, a_ref, s_ref, b_ref, o_ref):
        own = a_ref[...] + s_ref[...].astype(F32)
        o_ref[...] = ((own + b_ref[0].astype(F32)) + b_ref[1].astype(F32)) + b_ref[2].astype(F32)

    return pl.pallas_call(
        body, name=name,
        grid_spec=pltpu.PrefetchScalarGridSpec(
            num_scalar_prefetch=1, grid=(pl.cdiv(Rh, tr),),
            in_specs=[pl.BlockSpec((None, None, tr, C), lambda r, p: (p[0], p[1], r, 0)),
                      pl.BlockSpec((None, tr, C), lambda r, p: (p[0], r, 0)),
                      pl.BlockSpec((3, tr, C), lambda r, p: (0, r, 0))],
            out_specs=pl.BlockSpec((None, tr, C), lambda r, p: (p[1], r, 0))),
        out_shape=jax.ShapeDtypeStruct((2, Rh, C), F32),
        compiler_params=_cp("arbitrary"),
    )(pos, g4, from_sibling, from_chips)


HBM = pl.BlockSpec(memory_space=pltpu.HBM)


def _mesh_pos():
    return lax.axis_index("x"), lax.axis_index("y"), lax.axis_index("c")


def _other_chips(x, y):
    return [(1 - x, y), (x, 1 - y), (1 - x, 1 - y)]


def _allgather_small(xs, *, name):
    m_per, n = xs.shape

    def body(x_ref, out_ref, send_sems, recv_sems, local_sem):
        x, y, c = _mesh_pos()
        me, sibling = (x, y, c), (x, y, 1 - c)
        chips = _other_chips(x, y)

        def rows(px, py, pc):
            return out_ref.at[pl.ds((4 * px + 2 * py + pc) * m_per, m_per), :]

        def copy(k, block, to, src=None):
            return pltpu.make_async_remote_copy(
                src_ref=rows(*block) if src is None else src, dst_ref=rows(*block),
                send_sem=send_sems.at[k], recv_sem=recv_sems.at[k], device_id=to, device_id_type=MESH)

        mine = pltpu.make_async_copy(x_ref, rows(*me), local_sem)
        mine.start()
        first = [copy(0, me, sibling, src=x_ref)]
        first += [copy(1 + j, me, (*chip, c), src=x_ref) for j, chip in enumerate(chips)]
        for cp in first:
            cp.start()
        passed = [copy(4 + j, (*chip, c), sibling) for j, chip in enumerate(chips)]
        for j, chip in enumerate(chips):
            copy(1 + j, (*chip, c), me).wait_recv()
            passed[j].start()
        copy(0, sibling, me).wait_recv()
        for j, chip in enumerate(chips):
            copy(4 + j, (*chip, 1 - c), me).wait_recv()
        for cp in first + passed:
            cp.wait_send()
        mine.wait()

    return pl.pallas_call(
        body, name=name,
        out_shape=jax.ShapeDtypeStruct((N_DEV * m_per, n), xs.dtype),
        in_specs=[pl.BlockSpec(memory_space=pltpu.VMEM)],
        out_specs=pl.BlockSpec(memory_space=pltpu.VMEM),
        scratch_shapes=[pltpu.SemaphoreType.DMA((7,)), pltpu.SemaphoreType.DMA((7,)), pltpu.SemaphoreType.DMA],
    )(xs)


def _chip_slab_copies(s_ref, out_ref, send_sems, recv_sems):
    R = s_ref.shape[0]
    Rh = R // 2
    x, y, c = _mesh_pos()
    me, sibling = (x, y, c), (x, y, 1 - c)
    chips = _other_chips(x, y)

    def half(px, py, pc):
        return out_ref.at[2 * px + py, pl.ds(pc * Rh, Rh), :]

    def copy(k, block, to, src=None):
        return pltpu.make_async_remote_copy(
            src_ref=half(*block) if src is None else src, dst_ref=half(*block),
            send_sem=send_sems.at[k], recv_sem=recv_sems.at[k], device_id=to, device_id_type=MESH)

    first = [copy(j, me, (*chip, c), src=s_ref.at[pl.ds(c * Rh, Rh), :]) for j, chip in enumerate(chips)]
    passed = [copy(3 + j, (*chip, c), sibling) for j, chip in enumerate(chips)]
    landed = [copy(j, (*chip, c), me) for j, chip in enumerate(chips)]
    from_sibling = [copy(3 + j, (*chip, 1 - c), me) for j, chip in enumerate(chips)]
    return first, passed, landed, from_sibling


def _gather_behind(s_ref, out_ref, send_sems, recv_sems, step, nsteps):
    first, passed, landed, from_sibling = _chip_slab_copies(s_ref, out_ref, send_sems, recv_sems)

    @pl.when(step == 0)
    def _():
        for cp in first:
            cp.start()

    @pl.when(step == (3 * nsteps) // 4)
    def _():
        for arrived, onward in zip(landed, passed):
            arrived.wait_recv()
            onward.start()

    def finish():
        @pl.when(step == nsteps - 1)
        def _():
            for cp in from_sibling:
                cp.wait_recv()
            for cp in first + passed:
                cp.wait_send()

    return finish


def _allgather_chip_slabs(slab, *, name):
    R, C = slab.shape

    def body(s_ref, out_ref, send_sems, recv_sems):
        first, passed, landed, from_sibling = _chip_slab_copies(s_ref, out_ref, send_sems, recv_sems)
        for cp in first:
            cp.start()
        for arrived, onward in zip(landed, passed):
            arrived.wait_recv()
            onward.start()
        for cp in from_sibling:
            cp.wait_recv()
        for cp in first + passed:
            cp.wait_send()

    return pl.pallas_call(
        body, name=name,
        out_shape=jax.ShapeDtypeStruct((N_CHIPS, R, C), slab.dtype),
        in_specs=[HBM], out_specs=HBM,
        scratch_shapes=[pltpu.SemaphoreType.DMA((6,)), pltpu.SemaphoreType.DMA((6,))],
    )(slab)


def _swap_halves(mine, *, name):
    def body(g_ref, out_ref, send_sems, recv_sems):
        x, y, c = _mesh_pos()
        copies = [pltpu.make_async_remote_copy(
            src_ref=g_ref.at[j], dst_ref=out_ref.at[j], send_sem=send_sems.at[j], recv_sem=recv_sems.at[j],
            device_id=(x, y, 1 - c), device_id_type=MESH) for j in range(N_CHIPS)]
        for cp in copies:
            cp.start()
        for cp in copies:
            cp.wait()

    return pl.pallas_call(
        body, name=name,
        out_shape=jax.ShapeDtypeStruct(mine.shape, mine.dtype),
        in_specs=[HBM], out_specs=HBM,
        scratch_shapes=[pltpu.SemaphoreType.DMA((N_CHIPS,)), pltpu.SemaphoreType.DMA((N_CHIPS,))],
    )(mine)


def _scatter_copies(p_ref, out_ref, send_sems, recv_sems):
    x, y, c = _mesh_pos()
    return [pltpu.make_async_remote_copy(
        src_ref=p_ref.at[2 * px + py], dst_ref=out_ref.at[j], send_sem=send_sems.at[j], recv_sem=recv_sems.at[j],
        device_id=(px, py, c), device_id_type=MESH) for j, (px, py) in enumerate(_other_chips(x, y))]


def _join_halves(buf, *, name):
    def body(b_ref, out_ref, send_sem, recv_sem):
        x, y, c = _mesh_pos()
        cp = pltpu.make_async_remote_copy(
            src_ref=b_ref.at[c], dst_ref=out_ref.at[c], send_sem=send_sem, recv_sem=recv_sem,
            device_id=(x, y, 1 - c), device_id_type=MESH)
        cp.start()
        cp.wait()

    return pl.pallas_call(
        body, name=name,
        out_shape=jax.ShapeDtypeStruct(buf.shape, buf.dtype),
        in_specs=[HBM], out_specs=HBM, input_output_aliases={0: 0},
        scratch_shapes=[pltpu.SemaphoreType.DMA, pltpu.SemaphoreType.DMA],
    )(buf)


def _pad_rows(a, mult):
    pad = (-a.shape[0]) % mult
    return a if pad == 0 else jnp.pad(a, ((0, pad),) + ((0, 0),) * (a.ndim - 1))


def _local_step(x, target, mod, wts, small, slabs=None, unpacks=None, reduce_early=None, grad_slab=None,
                reduce_late=None):
    S, D = x.shape
    HP = D // LANES
    row = lambda v: v.reshape(1, -1)
    msplit = [[row(mod[i, k * D:(k + 1) * D]) for k in range(6)] for i in range(2)]
    gw, gs = {}, {}
    dmod = [[None] * 6 for _ in range(2)]
    slab, where = grad_slab if grad_slab is not None else (None, {})

    def dw(key, a, b, name):
        nonlocal slab
        if key in where:
            slab = _matmul_tn(a, b, name=name, into=(slab,) + where[key])
        else:
            gw[key] = _matmul_tn(a, b, name=name)

    sh1, sc1, g1, sh2, sc2, g2 = msplit[0]
    n1w0, n2w0 = row(small["norm1_w"][0]), row(small["norm2_w"][0])
    slabs = slabs if slabs is not None else (None, None, None)
    proj0, h1_0, *gathered = _ln_matmul(x, n1w0, sc1, sh1, wts["hg_w_in"], slabs[0], relu2=False, name="hg_in_proj")
    if slabs[0] is not None:
        wts = {**wts, **unpacks[0](gathered[0])}
    gn = small["hg_gn_w"].reshape(1, LANES)
    ypre0, o0, states, *gathered = _hg_fwd(proj0, small["hg_lb"], gn, slabs[1], name="hg_fwd")
    if slabs[1] is not None:
        wts = {**wts, **unpacks[1](gathered[0])}
    x1, ymix0 = _matmul_resid(ypre0, wts["hg_w_out"], x, g1, name="hg_out_proj")
    a0, u0, h2_0, *gathered = _ln_matmul(x1, n2w0, sc2, sh2, wts["mlp_w1_0"], slabs[2], relu2=True, name="mlp0_up")
    if slabs[2] is not None:
        wts = {**wts, **unpacks[2](gathered[0])}
    x2, ymlp0 = _matmul_resid(u0, wts["mlp_w2_0"], x1, g2, name="mlp0_down")

    sh1b, sc1b, g1b, sh2b, sc2b, g2b = msplit[1]
    n1w1, n2w1 = row(small["norm1_w"][1]), row(small["norm2_w"][1])
    proj1, h1_1 = _ln_matmul(x2, n1w1, sc1b, sh1b, wts["fox_w_in"], relu2=False, name="fox_in_proj")
    nheads = 2 * HP
    bf_pad = jnp.pad(small["fox_b_f"].reshape(1, nheads), ((0, 0), (0, LANES - nheads)))
    qw2 = jnp.tile(small["fox_qn_w"].reshape(1, FOX_DH), (1, 2))
    kw2 = jnp.tile(small["fox_kn_w"].reshape(1, FOX_DH), (1, 2))
    fcum = _fox_cumsum(proj1, bf_pad, name="fox_cumsum")
    qa, ka, va, vat = _fox_prep(proj1, fcum, qw2, kw2, name="fox_prep")
    jmin, imax = _fox_skip_bounds(fcum, small["fox_qn_w"], small["fox_kn_w"], nheads)
    ypre1, o1, q2 = _fox_fwd(jmin, qa, ka, vat, proj1, name="fox_fwd")
    x3, ymix1 = _matmul_resid(ypre1, wts["fox_w_out"], x2, g1b, name="fox_out_proj")
    a1, u1, h2_1 = _ln_matmul(x3, n2w1, sc2b, sh2b, wts["mlp_w1_1"], relu2=True, name="mlp1_up")
    x4, ymlp1 = _matmul_resid(u1, wts["mlp_w2_1"], x3, g2b, name="mlp1_down")

    loss, dx4, dfw = _loss_kernel(x4, row(small["final_w"]), target, name="loss")
    gs["final_w"] = dfw.reshape(-1)

    def mlp_bwd(i, dx_out, x_in, h2, a, u, ymlp, n2w, sc2_, g2_):
        dz, dm, dg2 = _gate_matmul_nt(dx_out, g2_, ymlp, wts[f"mlp_w2_{i}"], a, name=f"mlp{i}_down_bwd")
        dw(f"mlp_w2_{i}", u, dm[None], f"mlp{i}_dw2")
        dw(f"mlp_w1_{i}", h2, dz[None], f"mlp{i}_dw1")
        dx_in, dsc, dsh, dnw = _matmul_nt_lnbwd(dz[None], wts[f"mlp_w1_{i}"], x_in, n2w, sc2_, dx_out,
                                                name=f"mlp{i}_up_bwd")
        dmod[i][3], dmod[i][4], dmod[i][5] = dsh, dsc, dg2
        return dx_in, dnw

    dx3, dn2w1 = mlp_bwd(1, dx4, x3, h2_1, a1, u1, ymlp1, n2w1, sc2b, g2b)
    dyp1, dm1, dg1b = _gate_matmul_nt(dx3, g1b, ymix1, wts["fox_w_out"], None, name="fox_out_bwd")
    dw("fox_w_out", ypre1, dm1[None], "fox_dw_out")
    doa = _fox_bwd_prep(dyp1, o1, proj1, q2, name="fox_bwd_prep")
    dqa, dka, dva, colsum = _fox_bwd(imax, q2, ka, va, doa, name="fox_bwd")
    colsum = jnp.pad(colsum[:, 0, :].T, ((0, 0), (0, LANES - nheads)))
    dproj1, dqw, dkw = _fox_bwd_post(dqa, dka, dva, proj1, dyp1, o1, qw2, kw2, name="fox_bwd_post")
    dproj1, dbf = _fox_dfz(colsum, nheads, proj1, bf_pad, dproj1, name="fox_dfz")
    dw("fox_w_in", h1_1, dproj1, "fox_dw_in")
    dx2, dsc, dsh, dn1w1 = _matmul_nt_lnbwd(dproj1, wts["fox_w_in"], x2, n1w1, sc1b, dx3, name="fox_in_bwd")
    dmod[1][0], dmod[1][1], dmod[1][2] = dsh, dsc, dg1b
    gs["fox_qn_w"] = dqw[0, :FOX_DH] + dqw[0, FOX_DH:]
    gs["fox_kn_w"] = dkw[0, :FOX_DH] + dkw[0, FOX_DH:]
    gs["fox_b_f"] = dbf[0, :nheads]

    dx1, dn2w0 = mlp_bwd(0, dx2, x1, h2_0, a0, u0, ymlp0, n2w0, sc2, g2)
    dyp0, dm0, dg1 = _gate_matmul_nt(dx1, g1, ymix0, wts["hg_w_out"], None, name="hg_out_bwd")
    dw("hg_w_out", ypre0, dm0[None], "hg_dw_out")
    part, ctx = reduce_early(gw, slab) if reduce_early is not None else (None, None)
    dproj0, dlb, dgn, *from_chips = _hg_bwd(proj0, small["hg_lb"], gn, o0, states, dyp0, part, name="hg_bwd")
    early = (ctx, from_chips[0]) if reduce_early is not None else None
    dw("hg_w_in", h1_0, dproj0, "hg_dw_in")
    part, ctx = reduce_late(gw) if reduce_late is not None else (None, None)
    dx0, dsc, dsh, dn1w0, *from_chips = _matmul_nt_lnbwd(dproj0, wts["hg_w_in"], x, n1w0, sc1, dx1, part, name="hg_in_bwd")
    late = (ctx, from_chips[0]) if reduce_late is not None else None
    dmod[0][0], dmod[0][1], dmod[0][2] = dsh, dsc, dg1
    gs["hg_lb"] = dlb
    gs["hg_gn_w"] = jnp.sum(dgn, axis=0)

    gs["norm1_w"] = jnp.concatenate([dn1w0, dn1w1], axis=0)
    gs["norm2_w"] = jnp.concatenate([dn2w0, dn2w1], axis=0)
    gs["dmod"] = jnp.stack([jnp.concatenate(dmod[i], axis=1)[0] for i in range(2)])
    return loss, dx0, gw, gs, early, late


def _pack_halves(layout):
    rh = -(-max(sum(a.shape[0] for _, a in half) for half in layout) // 16) * 16
    place, parts = {}, []
    for h, half in enumerate(layout):
        off = 0
        for n, a in half:
            place[n] = (h, off, a.shape[0])
            off += a.shape[0]
        parts.append(jnp.pad(jnp.concatenate([a.astype(BF) for _, a in half], axis=0), ((0, rh - off), (0, 0))))
    return jnp.concatenate(parts, axis=0), place, rh


SMALL_NAMES = ["norm1_w", "norm2_w", "hg_lb", "hg_gn_w", "fox_b_f", "fox_qn_w", "fox_kn_w", "final_w"]


def _pack_small(d, names):
    rows, offs, r0 = [], {}, 0
    for n in names:
        flat = d[n].reshape(-1)
        nr = -(-flat.shape[0] // LANES)
        rows.append(jnp.pad(flat, (0, nr * LANES - flat.shape[0])).reshape(nr, LANES))
        offs[n] = (r0, nr)
        r0 += nr
    return jnp.concatenate(rows, axis=0), offs


def _unpack_small(packed, offs, name, like):
    r0, nr = offs[name]
    return packed[r0:r0 + nr].reshape(-1)[:like.size].reshape(like.shape)


def kernel(x, c, w_mod, b_mod, norm1_w, norm2_w, hg_w_in, hg_w_out, hg_lb, hg_gn_w, fox_w_in, fox_b_f, fox_qn_w, fox_kn_w, fox_w_out, mlp_w1, mlp_w2, final_w, loss_target, m_w_mod, m_b_mod, m_norm1_w, m_norm2_w, m_hg_w_in, m_hg_w_out, m_hg_lb, m_hg_gn_w, m_fox_w_in, m_fox_b_f, m_fox_qn_w, m_fox_kn_w, m_fox_w_out, m_mlp_w1, m_mlp_w2, m_final_w, v_w_mod, v_b_mod, v_norm1_w, v_norm2_w, v_hg_w_in, v_hg_w_out, v_hg_lb, v_hg_gn_w, v_fox_w_in, v_fox_b_f, v_fox_qn_w, v_fox_kn_w, v_fox_w_out, v_mlp_w1, v_mlp_w2, v_final_w):
    S, D = x.shape[1], x.shape[2]
    nheads = D // FOX_DH
    ax, ay, ac = _mesh_pos()
    chip = 2 * ax + ay
    dev = 2 * chip + ac
    xs, tgt = x.reshape(S, D), loss_target.reshape(S, D)

    c_all = _allgather_small(_pad_rows(c.reshape(-1, LANES), 8), name="gather_c")
    c_all = c_all.reshape(N_DEV, -1)[:, :D]
    c16 = _pad_rows(c_all, 16)
    nmod = w_mod.shape[2]
    b_shard = lax.dynamic_slice_in_dim(b_mod, chip * nmod, nmod, axis=1)
    mod_shard = _mod_fwd(c16, w_mod, b_shard[:, None, :], name="mod_fwd")[:, :N_DEV]
    mod_all = _allgather_small(mod_shard.reshape(-1, LANES), name="gather_mod")
    mod_all = mod_all.reshape(N_CHIPS, 2, 2, N_DEV, nmod)[:, 0]
    mod = lax.dynamic_index_in_dim(mod_all, dev, axis=2, keepdims=False)
    mod = mod.transpose(1, 0, 2).reshape(2, N_CHIPS * nmod)

    fox_rows = fox_w_in.shape[2]
    col = lambda g: g.transpose(1, 0, 2).reshape(g.shape[1], -1)
    rowsh = lambda g: g.reshape(-1, g.shape[2])
    own = lambda g, s: lax.dynamic_update_index_in_dim(g, s, chip, 0)

    slab_in = hg_w_in[0].astype(BF)
    wts = {"hg_w_in": col(own(_allgather_chip_slabs(slab_in, name="gather_hg_w_in"), slab_in))}
    fox_flat, fox_cut = fox_w_in[0].reshape(fox_rows, D), fox_rows // 2
    slabs, unpacks = [], []
    for layout_w in ([[("mlp_w1_0", mlp_w1[0])], [("mlp_w2_0", mlp_w2[0])]],
                     [[("mlp_w1_1", mlp_w1[1]), ("hg_w_out", hg_w_out[0])], [("mlp_w2_1", mlp_w2[1]), ("fox_w_out", fox_w_out[0])]],
                     [[("fox_a", fox_flat[:fox_cut])], [("fox_b", fox_flat[fox_cut:])]]):
        slab_w, place_w, rh_w = _pack_halves(layout_w)

        def unpack(gathered, slab_w=slab_w, place_w=place_w, rh_w=rh_w):
            gathered = own(gathered, slab_w)
            out = {}
            for n, (h, off, rows) in place_w.items():
                g = gathered[:, h * rh_w + off:h * rh_w + off + rows, :]
                out[n] = col(g) if n.startswith("mlp_w1") else rowsh(g) if n.startswith(("mlp_w2", "hg_", "fox_w")) else g
            if "fox_a" in out:
                fox_in = col(jnp.concatenate([out.pop("fox_a"), out.pop("fox_b")], axis=1).reshape(N_CHIPS, D, fox_rows))
                out["fox_w_in"] = jnp.pad(fox_in, ((0, 0), (0, 5 * D - fox_in.shape[1])))
            return out

        slabs.append(slab_w)
        unpacks.append(unpack)

    small = {"norm1_w": norm1_w, "norm2_w": norm2_w, "hg_lb": hg_lb, "hg_gn_w": hg_gn_w, "fox_b_f": fox_b_f,
             "fox_qn_w": fox_qn_w, "fox_kn_w": fox_kn_w, "final_w": final_w}

    def uncol(g, n):
        return g.reshape(g.shape[0], N_CHIPS, n).transpose(1, 0, 2)

    pos = jnp.stack([chip, ac])

    def swap_and_add(g4, tag):
        to_sibling = lax.dynamic_index_in_dim(g4, 1 - ac, axis=1, keepdims=False).astype(BF)
        from_sibling = _swap_halves(to_sibling, name=f"rs_swap_{tag}")
        return from_sibling, _add_halves(g4, from_sibling, ac.reshape(1), name=f"rs_add_halves_{tag}")

    def finish(g4, from_sibling, from_chips, tag):
        my_half = _add_four(g4, from_sibling, from_chips, pos, name=f"rs_add_chips_{tag}")
        return _join_halves(my_half, name=f"rs_join_{tag}")

    layout = [[("mlp_w1", 2 * D), ("hg_w_out", D // 4), ("fox_w_out", D // 4)], [("mlp_w2", 2 * D), ("fox_w_in", fox_rows)]]
    place = {}
    for h, half in enumerate(layout):
        off = 0
        for n, rows in half:
            place[n] = (h, off, rows)
            off += rows

    rh = -(-max(sum(rows for _, rows in half) for half in layout) // 16) * 16
    where = {"hg_w_out": ("row",) + place["hg_w_out"][:2], "fox_w_out": ("row",) + place["fox_w_out"][:2]}
    for i in range(2):
        where[f"mlp_w1_{i}"] = ("col", place["mlp_w1"][0], place["mlp_w1"][1] + i * D)
        where[f"mlp_w2_{i}"] = ("row", place["mlp_w2"][0], place["mlp_w2"][1] + i * D)

    def reduce_early(gw, slab):
        gfox = uncol(gw["fox_w_in"][:, :4 * fox_rows], fox_rows).reshape(N_CHIPS, 1, fox_rows, D)
        h, off, _ = place["fox_w_in"]
        slab = lax.dynamic_update_slice(slab, gfox, (0, h, off, 0))
        for h, half in enumerate(layout):
            used = sum(rows for _, rows in half)
            if used < rh:
                slab = lax.dynamic_update_slice(slab, jnp.zeros((N_CHIPS, 1, rh - used, D), F32), (0, h, used, 0))
        from_sibling, part = swap_and_add(slab, "early")
        return part, (slab, from_sibling)

    def reduce_late(gw):
        g4 = uncol(gw["hg_w_in"], D).reshape(N_CHIPS, 2, D // 2, D)
        from_sibling, part = swap_and_add(g4, "late")
        return part, (g4, from_sibling)

    loss_part, grad_x, gw, gs, (early, from_chips_early), (late, from_chips_late) = _local_step(
        xs, tgt, mod, wts, small, slabs, unpacks, reduce_early, (lax.empty((N_CHIPS, 2, rh, D), F32), where), reduce_late)
    gshard = finish(*early, from_chips_early, "early")
    g_hg_w_in = finish(*late, from_chips_late, "late").reshape(D, D)

    names = ["dmod", "loss"] + SMALL_NAMES
    packed, offs = _pack_small({**gs, "loss": loss_part[0, :1]}, names)
    packed = _pad_rows(packed, 8)
    rp = packed.shape[0]
    parts = _allgather_small(packed, name="gather_small").reshape(N_DEV, rp, LANES)
    total = _sum_parts(parts, name="sum_small")
    r0, nr = offs["dmod"]
    dmod_all = parts[:, r0:r0 + nr].reshape(N_DEV, 2, N_CHIPS * nmod)
    dmod_shard = lax.dynamic_slice_in_dim(dmod_all, chip * nmod, nmod, axis=2).transpose(1, 0, 2)
    g_w_mod = _mod_bwd(c16, jnp.pad(dmod_shard, ((0, 0), (0, 16 - N_DEV), (0, 0))), name="mod_bwd")

    loss = _unpack_small(total, offs, "loss", loss_part[0, :1]).reshape(())
    grads = {"w_mod": g_w_mod, "b_mod": _unpack_small(total, offs, "dmod", b_mod)}
    for n in SMALL_NAMES:
        grads[n] = _unpack_small(total, offs, n, small[n])

    given = dict(w_mod=(w_mod, m_w_mod, v_w_mod), b_mod=(b_mod, m_b_mod, v_b_mod), norm1_w=(norm1_w, m_norm1_w, v_norm1_w),
                 norm2_w=(norm2_w, m_norm2_w, v_norm2_w), hg_w_in=(hg_w_in, m_hg_w_in, v_hg_w_in),
                 hg_w_out=(hg_w_out, m_hg_w_out, v_hg_w_out), hg_lb=(hg_lb, m_hg_lb, v_hg_lb),
                 hg_gn_w=(hg_gn_w, m_hg_gn_w, v_hg_gn_w), fox_w_in=(fox_w_in, m_fox_w_in, v_fox_w_in),
                 fox_b_f=(fox_b_f, m_fox_b_f, v_fox_b_f), fox_qn_w=(fox_qn_w, m_fox_qn_w, v_fox_qn_w),
                 fox_kn_w=(fox_kn_w, m_fox_kn_w, v_fox_kn_w), fox_w_out=(fox_w_out, m_fox_w_out, v_fox_w_out),
                 mlp_w1=(mlp_w1, m_mlp_w1, v_mlp_w1), mlp_w2=(mlp_w2, m_mlp_w2, v_mlp_w2), final_w=(final_w, m_final_w, v_final_w))
    upd = {}

    for n, (h, off, rows) in place.items():
        w, m, v = given[n]
        flat = lambda a: a.reshape(rows, D)
        d, mn, vn = _adamw(flat(w), gshard, flat(m), flat(v), g_at=(h, off), name=f"adamw_{n}")
        grads[n] = gshard[h, off:off + rows].reshape(w.shape)
        upd[n] = tuple(a.reshape(w.shape) for a in (d, mn, vn))

    w, m, v = given["hg_w_in"]
    grads["hg_w_in"] = g_hg_w_in.reshape(w.shape)
    upd["hg_w_in"] = tuple(a.reshape(w.shape) for a in _adamw(w[0], g_hg_w_in, m[0], v[0], name="adamw_hg_w_in"))

    w, m, v = given["w_mod"]
    flat = lambda a: a.reshape(-1, nmod)
    upd["w_mod"] = tuple(a.reshape(w.shape) for a in _adamw(flat(w), flat(g_w_mod), flat(m), flat(v), name="adamw_w_mod"))

    snames = ["b_mod"] + SMALL_NAMES
    pw, soffs = _pack_small({n: given[n][0] for n in snames}, snames)
    pm, _ = _pack_small({n: given[n][1] for n in snames}, snames)
    pv, _ = _pack_small({n: given[n][2] for n in snames}, snames)
    pg, _ = _pack_small({n: grads[n] for n in snames}, snames)
    pw, pm, pv, pg = (_pad_rows(a, 8) for a in (pw, pm, pv, pg))
    sd, smn, svn = _adamw(pw, pg, pm, pv, name="adamw_small")
    for n in snames:
        like = given[n][0]
        upd[n] = tuple(_unpack_small(a, soffs, n, like) for a in (sd, smn, svn))

    order = ["w_mod", "b_mod", "norm1_w", "norm2_w", "hg_w_in", "hg_w_out", "hg_lb", "hg_gn_w", "fox_w_in", "fox_b_f",
             "fox_qn_w", "fox_kn_w", "fox_w_out", "mlp_w1", "mlp_w2", "final_w"]
    return (loss, grad_x.reshape(x.shape), *[grads[n] for n in order], *[upd[n][0] for n in order],
            *[upd[n][1] for n in order], *[upd[n][2] for n in order])
```

```python
import math

import jax
import jax.numpy as jnp
from jax import lax
from jax.experimental import pallas as pl
from jax.experimental.pallas import tpu as pltpu

EPS = 1e-6
ADAM_LR, ADAM_B1, ADAM_B2, ADAM_EPS, ADAM_WD, ADAM_STEP = 0.001, 0.9, 0.999, 1e-08, 0.01, 10

F32 = jnp.float32
BF = jnp.bfloat16
LANES = 128
HG_CHUNK = 64
HG_HEADS_PER_STEP = 8
HG_TOKENS_PER_STEP = 256
FOX_ROWS_PER_STEP = 2048
FOX_BWD_TILES = (8, 4, 2, 1)
LOG2E = 1.4426950408889634
FOX_DH = 64
N_CHIPS = 4
N_DEV = 8
VMEM_LIMIT = 56 * 1024 * 1024
MESH = pl.DeviceIdType.MESH

NT = (((1,), (1,)), ((), ()))
TN = (((0,), (0,)), ((), ()))


def _pick(n, pref, mult=LANES):
    if n <= pref:
        return n
    t = (pref // mult) * mult
    while t >= mult:
        if n % t == 0:
            return t
        t -= mult
    raise ValueError((n, pref, mult))


def _cp(*sem):
    return pltpu.CompilerParams(dimension_semantics=sem, vmem_limit_bytes=VMEM_LIMIT)


def _dot(a, b):
    return jnp.dot(a, b, preferred_element_type=F32)


def _dg(a, b, dims):
    return lax.dot_general(a, b, dims, preferred_element_type=F32)


def _split3(x):
    hi = x.astype(BF)
    r1 = x - hi.astype(F32)
    mid = r1.astype(BF)
    lo = (r1 - mid.astype(F32)).astype(BF)
    return hi, mid, lo


def _tri_dot(tri, x):
    hi, mid, lo = _split3(x)
    return _dot(tri, hi) + _dot(tri, mid) + _dot(tri, lo)


def _dg3(a, b, dims):
    ah, bh = a.astype(BF), b.astype(BF)
    al, bl = (a - ah.astype(F32)).astype(BF), (b - bh.astype(F32)).astype(BF)
    return _dg(ah, bh, dims) + _dg(ah, bl, dims) + _dg(al, bh, dims)


def _dg1(a, b, dims):
    return _dg(a.astype(BF), b.astype(BF), dims)


NN = (((1,), (0,)), ((), ()))


def _sigmoid(x):
    return jax.nn.sigmoid(x)


def _ln_matmul(x, nw, sc, sh, w, slab=None, *, relu2, name):
    S, D = x.shape
    N = w.shape[1]
    tm, tn = _pick(S, 512, 16), N
    fused = slab is not None

    def body(x_ref, nw_ref, sc_ref, sh_ref, w_ref, *rest):
        if fused:
            s_ref, *outs, out_ref, hs, send_sems, recv_sems = rest
            finish = _gather_behind(s_ref, out_ref, send_sems, recv_sems, pl.program_id(0), S // tm)
        else:
            outs, hs = rest[:-1], rest[-1]
        h_ref = outs[-1]

        @pl.when(pl.program_id(1) == 0)
        def _():
            xv = x_ref[...]
            r = lax.rsqrt(jnp.mean(xv * xv, axis=-1, keepdims=True) + EPS)
            hb = ((xv * r * nw_ref[...]) * (1.0 + sc_ref[...]) + sh_ref[...]).astype(BF)
            hs[...] = hb
            h_ref[...] = hb

        z = _dot(hs[...], w_ref[...])
        if relu2:
            outs[0][...] = jnp.maximum(z, 0.0).astype(BF)
        else:
            outs[0][...] = z
        if fused:
            finish()

    vec = pl.BlockSpec((1, D), lambda i, j: (0, 0))
    tile = pl.BlockSpec((tm, tn), lambda i, j: (i, j))
    if relu2:
        out_shape = [jax.ShapeDtypeStruct((S, N), BF)]
        out_specs = [tile]
    else:
        out_shape = [jax.ShapeDtypeStruct((S, N), F32)]
        out_specs = [tile]
    out_shape.append(jax.ShapeDtypeStruct((S, D), BF))
    out_specs.append(pl.BlockSpec((tm, D), lambda i, j: (i, 0)))
    in_specs = [pl.BlockSpec((tm, D), lambda i, j: (i, 0)), vec, vec, vec, pl.BlockSpec((D, tn), lambda i, j: (0, j))]
    scratch = [pltpu.VMEM((tm, D), BF)]
    args = [x, nw, sc, sh, w]
    if fused:
        in_specs.append(HBM)
        out_specs.append(HBM)
        out_shape.append(jax.ShapeDtypeStruct((N_CHIPS,) + slab.shape, slab.dtype))
        scratch += [pltpu.SemaphoreType.DMA((6,)), pltpu.SemaphoreType.DMA((6,))]
        args.append(slab)
    return pl.pallas_call(
        body, name=name, grid=(S // tm, N // tn), in_specs=in_specs, out_specs=out_specs, out_shape=out_shape,
        scratch_shapes=scratch, compiler_params=_cp("arbitrary", "arbitrary"),
    )(*args)


def _square(a):
    af = a.astype(F32)
    return (af * af).astype(BF)


def _matmul_resid(a, w, x, gate, *, square=False, name):
    S, K = a.shape
    D = w.shape[1]
    tm, tn = _pick(S, 1024 if K <= 1024 else 512, 16), D

    def body(a_ref, w_ref, x_ref, g_ref, o_ref, y_ref):
        y = _dot(_square(a_ref[...]) if square else a_ref[...], w_ref[...])
        y_ref[...] = y.astype(BF)
        o_ref[...] = x_ref[...] + g_ref[...] * y

    tile = pl.BlockSpec((tm, tn), lambda i, j: (i, j))
    return pl.pallas_call(
        body, name=name, grid=(S // tm, D // tn),
        in_specs=[pl.BlockSpec((tm, K), lambda i, j: (i, 0)), pl.BlockSpec((K, tn), lambda i, j: (0, j)),
                  tile, pl.BlockSpec((1, tn), lambda i, j: (0, j))],
        out_specs=[tile, tile],
        out_shape=[jax.ShapeDtypeStruct((S, D), F32), jax.ShapeDtypeStruct((S, D), BF)],
        compiler_params=_cp("parallel", "arbitrary"),
    )(a, w, x, gate)


def _gate_matmul_nt(dx, gate, y, w, act, *, name):
    S, D = dx.shape
    K = w.shape[0]
    tm, tn = _pick(S, 1024 if K <= 1024 else 512, 16), K
    fused = act is not None

    def body(dx_ref, g_ref, y_ref, w_ref, *rest):
        if fused:
            act_ref, da_ref, dm_ref, dg_ref, ms = rest
        else:
            da_ref, dm_ref, dg_ref, ms = rest
        i, j = pl.program_id(0), pl.program_id(1)

        @pl.when((i == 0) & (j == 0))
        def _():
            dg_ref[...] = jnp.zeros_like(dg_ref)

        @pl.when(j == 0)
        def _():
            dxv = dx_ref[...]
            dmb = (dxv * g_ref[...]).astype(BF)
            ms[...] = dmb
            dm_ref[...] = dmb
            dg_ref[...] += jnp.sum(dxv * y_ref[...].astype(F32), axis=0, keepdims=True)

        da = _dg(ms[...], w_ref[...], NT)
        if fused:
            da_ref[...] = (da * (2.0 * act_ref[...].astype(F32))).astype(BF)
        else:
            da_ref[...] = da

    row = pl.BlockSpec((tm, D), lambda i, j: (i, 0))
    vec = pl.BlockSpec((1, D), lambda i, j: (0, 0))
    tile = pl.BlockSpec((tm, tn), lambda i, j: (i, j))
    in_specs = [row, vec, row, pl.BlockSpec((tn, D), lambda i, j: (j, 0))]
    args = [dx, gate, y, w]
    if fused:
        in_specs.append(tile)
        args.append(act)
    return pl.pallas_call(
        body, name=name, grid=(S // tm, K // tn),
        in_specs=in_specs, out_specs=[tile, row, vec],
        out_shape=[jax.ShapeDtypeStruct((S, K), BF if fused else F32), jax.ShapeDtypeStruct((S, D), BF),
                   jax.ShapeDtypeStruct((1, D), F32)],
        scratch_shapes=[pltpu.VMEM((tm, D), BF)],
        compiler_params=_cp("arbitrary", "arbitrary"),
    )(*args)


def _matmul_tn(a, b, *, name, into=None, square=False):
    S, Ka = a.shape
    P, _, Db = b.shape
    tk, tn, ts = _pick(Ka, 1024), _pick(Db, 1024), _pick(S, 1024, 16)
    if into is not None:
        slab, kind, half, off = into
        C = tn = slab.shape[3]
        per_chip = Ka // N_CHIPS
        all_chips = kind == "row" and tk == Ka
        if kind == "row" and not all_chips:
            tk = min(tk, per_chip)
        assert tn == C and P * Db == (N_CHIPS * C if kind == "col" else C)
        if kind == "col":
            assert tk == Ka and off % tk == 0
        elif all_chips:
            assert off % per_chip == 0
        else:
            assert per_chip % tk == 0 and off % tk == 0
    npb = Db // tn

    def body(a_ref, b_ref, *rest):
        o_ref, acc = rest[-2:]
        s = pl.program_id(2)

        @pl.when(s == 0)
        def _():
            acc[...] = jnp.zeros_like(acc)

        acc[...] += _dg(_square(a_ref[...]) if square else a_ref[...], b_ref[...], TN)

        @pl.when(s == pl.num_programs(2) - 1)
        def _():
            o_ref[...] = acc[...].reshape(o_ref.shape)

    in_specs = [pl.BlockSpec((ts, tk), lambda i, j, s: (s, i)),
                pl.BlockSpec((None, ts, tn), lambda i, j, s: (j // npb, s, j % npb))]
    args = [a, b]
    if into is None:
        out_spec = pl.BlockSpec((tk, tn), lambda i, j, s: (i, j))
        out_shape = jax.ShapeDtypeStruct((Ka, P * Db), F32)
        aliases = {}
    else:
        per = per_chip // tk if kind == "row" and not all_chips else 1
        if kind == "col":
            out_spec = pl.BlockSpec((None, None, tk, tn), lambda i, j, s: (j, half, off // tk + i, 0))
        elif all_chips:
            out_spec = pl.BlockSpec((N_CHIPS, None, per_chip, tn), lambda i, j, s: (0, half, off // per_chip, 0))
        else:
            out_spec = pl.BlockSpec((None, None, tk, tn), lambda i, j, s: (i // per, half, off // tk + i % per, 0))
        out_shape = jax.ShapeDtypeStruct(slab.shape, F32)
        in_specs.append(pl.BlockSpec(memory_space=pl.ANY))
        args.append(slab)
        aliases = {2: 0}
    return pl.pallas_call(
        body, name=name, grid=(Ka // tk, P * npb, S // ts),
        in_specs=in_specs, out_specs=out_spec, out_shape=out_shape,
        scratch_shapes=[pltpu.VMEM((tk, tn), F32)], input_output_aliases=aliases,
        compiler_params=_cp("parallel", "parallel", "arbitrary"),
    )(*args)


def _matmul_nt_lnbwd(g, w, x, nw, sc, dx_out, part=None, *, name):
    P, S, Dg = g.shape
    D = x.shape[1]
    tm = _pick(S, 512, 16)
    fused = part is not None

    def body(g_ref, w_ref, x_ref, nw_ref, sc_ref, dxo_ref, *rest):
        if fused:
            p_ref, dx_ref, dsc_ref, dsh_ref, dnw_ref, recv_ref, send_sems, recv_sems = rest
            copies = _scatter_copies(p_ref, recv_ref, send_sems, recv_sems)
        else:
            dx_ref, dsc_ref, dsh_ref, dnw_ref = rest

        @pl.when(pl.program_id(0) == 0)
        def _():
            dsc_ref[...] = jnp.zeros_like(dsc_ref)
            dsh_ref[...] = jnp.zeros_like(dsh_ref)
            dnw_ref[...] = jnp.zeros_like(dnw_ref)
            if fused:
                for cp in copies:
                    cp.start()

        dh = _dg(g_ref[0], w_ref[:, 0:Dg], NT)
        for p in range(1, P):
            dh = dh + _dg(g_ref[p], w_ref[:, p * Dg:(p + 1) * Dg], NT)
        xv = x_ref[...]
        nwv = nw_ref[...]
        r = lax.rsqrt(jnp.mean(xv * xv, axis=-1, keepdims=True) + EPS)
        xr = xv * r
        dn = dh * (1.0 + sc_ref[...])
        dsc_ref[...] += jnp.sum(dh * (xr * nwv), axis=0, keepdims=True)
        dsh_ref[...] += jnp.sum(dh, axis=0, keepdims=True)
        dnw_ref[...] += jnp.sum(dn * xr, axis=0, keepdims=True)
        u = dn * nwv
        dx_ref[...] = dxo_ref[...] + r * (u - xr * jnp.mean(u * xr, axis=-1, keepdims=True))

        if fused:
            @pl.when(pl.program_id(0) == S // tm - 1)
            def _():
                for cp in copies:
                    cp.wait()

    row = pl.BlockSpec((tm, D), lambda i: (i, 0))
    vec = pl.BlockSpec((1, D), lambda i: (0, 0))
    in_specs = [pl.BlockSpec((P, tm, Dg), lambda i: (0, i, 0)), pl.BlockSpec((D, P * Dg), lambda i: (0, 0)), row, vec, vec, row]
    out_specs = [row, vec, vec, vec]
    out_shape = [jax.ShapeDtypeStruct((S, D), F32)] + [jax.ShapeDtypeStruct((1, D), F32)] * 3
    scratch, args = [], [g, w, x, nw, sc, dx_out]
    if fused:
        in_specs.append(HBM)
        out_specs.append(HBM)
        out_shape.append(jax.ShapeDtypeStruct((3,) + part.shape[1:], part.dtype))
        scratch = [pltpu.SemaphoreType.DMA((3,)), pltpu.SemaphoreType.DMA((3,))]
        args.append(part)
    return pl.pallas_call(
        body, name=name, grid=(S // tm,), in_specs=in_specs, out_specs=out_specs, out_shape=out_shape,
        scratch_shapes=scratch, compiler_params=_cp("arbitrary"),
    )(*args)


def _loss_kernel(x, fw, tgt, *, name):
    S, D = x.shape
    tm = _pick(S, 512, 8)

    def body(x_ref, fw_ref, t_ref, l_ref, dx_ref, dfw_ref):
        @pl.when(pl.program_id(0) == 0)
        def _():
            l_ref[...] = jnp.zeros_like(l_ref)
            dfw_ref[...] = jnp.zeros_like(dfw_ref)

        xv = x_ref[...]
        fwv = fw_ref[...]
        r = lax.rsqrt(jnp.mean(xv * xv, axis=-1, keepdims=True) + EPS)
        xr = xv * r
        err = xr * fwv - t_ref[...]
        per_tok = jnp.mean(err * err, axis=-1, keepdims=True)
        l_ref[...] += 0.5 * jnp.sum(per_tok, axis=0, keepdims=True)
        dy = err * (1.0 / D)
        dfw_ref[...] += jnp.sum(dy * xr, axis=0, keepdims=True)
        u = dy * fwv
        dx_ref[...] = r * (u - xr * jnp.mean(u * xr, axis=-1, keepdims=True))

    row = pl.BlockSpec((tm, D), lambda i: (i, 0))
    vec = pl.BlockSpec((1, D), lambda i: (0, 0))
    return pl.pallas_call(
        body, name=name, grid=(S // tm,),
        in_specs=[row, vec, row],
        out_specs=[pl.BlockSpec((1, LANES), lambda i: (0, 0)), row, vec],
        out_shape=[jax.ShapeDtypeStruct((1, LANES), F32), jax.ShapeDtypeStruct((S, D), F32),
                   jax.ShapeDtypeStruct((1, D), F32)],
        compiler_params=_cp("arbitrary"),
    )(x, fw, tgt)


def _hg_lower_bound(lb3):
    mx = jnp.max(lb3, axis=0, keepdims=True)
    e = jnp.exp(lb3 - mx)
    p = e / jnp.sum(e, axis=0, keepdims=True)
    return p[0:1, :], p


def _hg_chunk_common(qr, fz, lbv):
    sq = _sigmoid(qr)
    q = qr * sq
    sig = _sigmoid(fz)
    f = lbv + (1.0 - lbv) * sig
    k = (1.0 - lbv) * (1.0 - sig)
    return q, sq, sig, f, k, jnp.log(f)


def _row_of(x, rows, r):
    return jnp.sum(jnp.where(rows == r, x, 0.0), axis=0, keepdims=True)


def _hg_fwd(proj, hg_lb, gn, slab=None, *, name):
    S = proj.shape[0]
    D = proj.shape[1] // 4
    H = D // LANES
    HB = min(HG_HEADS_PER_STEP, H)
    W = HB * LANES
    C = HG_CHUNK
    T = _pick(S, HG_TOKENS_PER_STEP, C)
    nch, nb = T // C, S // T
    ng = H // HB
    fused = slab is not None

    def body(q_ref, fz_ref, v_ref, g_ref, lb_ref, gn_ref, *rest):
        if fused:
            s_ref, y_ref, o_ref, sts_ref, out_ref, st, send_sems, recv_sems = rest
            finish = _gather_behind(s_ref, out_ref, send_sems, recv_sems,
                                    pl.program_id(0) * nb + pl.program_id(1), ng * nb)
        else:
            y_ref, o_ref, sts_ref, st = rest

        @pl.when(pl.program_id(1) == 0)
        def _():
            st[...] = jnp.zeros_like(st)

        lb_all, _ = _hg_lower_bound(lb_ref[...])
        gnv = gn_ref[...]
        ri = lax.broadcasted_iota(jnp.int32, (C, C), 0)
        ci_ = lax.broadcasted_iota(jnp.int32, (C, C), 1)
        low = ri >= ci_
        tri = jnp.where(low, 1.0, 0.0).astype(BF)
        rows_w = lax.broadcasted_iota(jnp.int32, (C, W), 0)

        def chunk(ci, carry):
            sl = pl.ds(pl.multiple_of(ci * C, C), C)
            heads = [slice(hh * LANES, (hh + 1) * LANES) for hh in range(HB)]
            q, _, _, _, k, logf = _hg_chunk_common(q_ref[sl, :], fz_ref[sl, :], lb_all)
            vv, gg = v_ref[sl, :], g_ref[sl, :]
            G = _tri_dot(tri, logf)
            Gm = _row_of(G, rows_w, C // 2 - 1)
            Gl = _row_of(G, rows_w, C - 1)
            qt, kt = q * jnp.exp(G - Gm), k * jnp.exp(Gm - G)
            qe, kd, eGl = q * jnp.exp(G), k * jnp.exp(Gl - G), jnp.exp(Gl)
            A = [jnp.where(low, _dg1(qt[:, ls], kt[:, ls], NT), 0.0) for ls in heads]
            Sv = [st[hh] for hh in range(HB)]
            for hh in range(HB):
                sts_ref[hh, ci] = Sv[hh]
            o = [_dg1(A[hh], vv[:, ls], NN) + _dg1(qe[:, ls], Sv[hh], NT) for hh, ls in enumerate(heads)]
            for hh, ls in enumerate(heads):
                st[hh] = Sv[hh] * eGl[:, ls] + _dg1(vv[:, ls], kd[:, ls], TN)
            gate = gg * _sigmoid(gg)
            for hh, ls in enumerate(heads):
                r = lax.rsqrt(jnp.mean(o[hh] * o[hh], axis=-1, keepdims=True) + EPS)
                y_ref[sl, ls] = ((o[hh] * r * gnv) * gate[:, ls]).astype(BF)
                o_ref[sl, ls] = o[hh]
            return carry

        lax.fori_loop(0, nch, chunk, 0)

        if fused:
            finish()

    def part(p):
        return pl.BlockSpec((T, W), lambda h, n: (n, p * ng + h))

    blk = pl.BlockSpec((T, W), lambda h, n: (n, h))
    in_specs = [part(0), part(1), part(2), part(3),
                pl.BlockSpec((3, W), lambda h, n: (0, h)), pl.BlockSpec((1, LANES), lambda h, n: (0, 0))]
    out_specs = [blk, blk, pl.BlockSpec((HB, nch, LANES, LANES), lambda h, n: (h, n, 0, 0))]
    out_shape = [jax.ShapeDtypeStruct((S, D), BF), jax.ShapeDtypeStruct((S, D), F32),
                 jax.ShapeDtypeStruct((H, S // C, LANES, LANES), F32)]
    scratch = [pltpu.VMEM((HB, LANES, LANES), F32)]
    args = [proj, proj, proj, proj, hg_lb, gn]
    if fused:
        in_specs.append(HBM)
        out_specs.append(HBM)
        out_shape.append(jax.ShapeDtypeStruct((N_CHIPS,) + slab.shape, slab.dtype))
        scratch += [pltpu.SemaphoreType.DMA((6,)), pltpu.SemaphoreType.DMA((6,))]
        args.append(slab)
    return pl.pallas_call(
        body, name=name, grid=(ng, nb), in_specs=in_specs, out_specs=out_specs, out_shape=out_shape,
        scratch_shapes=scratch, compiler_params=_cp("arbitrary", "arbitrary"),
    )(*args)


def _hg_bwd(proj, hg_lb, gn, o_all, states, dy, part=None, *, name):
    S = proj.shape[0]
    D = proj.shape[1] // 4
    H = D // LANES
    HB = min(HG_HEADS_PER_STEP, H)
    W = HB * LANES
    C = HG_CHUNK
    T = _pick(S, HG_TOKENS_PER_STEP, C)
    nch, nb = T // C, S // T
    ng = H // HB
    fused = part is not None

    def body(q_ref, fz_ref, v_ref, g_ref, lb_ref, gn_ref, o_ref, sts_ref, dy_ref, *rest):
        if fused:
            p_ref, dp_ref, dlb_ref, dgn_ref, recv_ref, dst, dlb_acc, send_sems, recv_sems = rest
            copies = _scatter_copies(p_ref, recv_ref, send_sems, recv_sems)

            @pl.when((pl.program_id(0) == 0) & (pl.program_id(1) == 0))
            def _():
                for cp in copies:
                    cp.start()
        else:
            dp_ref, dlb_ref, dgn_ref, dst, dlb_acc = rest
        n = pl.program_id(1)

        @pl.when(n == 0)
        def _():
            dst[...] = jnp.zeros_like(dst)
            dlb_acc[...] = jnp.zeros_like(dlb_acc)
            dgn_ref[...] = jnp.zeros_like(dgn_ref)

        lb_all, p3 = _hg_lower_bound(lb_ref[...])
        gnv = gn_ref[...]
        ri = lax.broadcasted_iota(jnp.int32, (C, C), 0)
        ci_ = lax.broadcasted_iota(jnp.int32, (C, C), 1)
        low = ri >= ci_
        tri = jnp.where(low, 1.0, 0.0).astype(BF)
        triu = jnp.where(ri <= ci_, 1.0, 0.0).astype(BF)
        rows_w = lax.broadcasted_iota(jnp.int32, (C, W), 0)
        gnw = jnp.tile(gnv, (1, HB))

        def chunk(cj, carry):
            ci = nch - 1 - cj
            sl = pl.ds(pl.multiple_of(ci * C, C), C)
            heads = list(enumerate(slice(hh * LANES, (hh + 1) * LANES) for hh in range(HB)))
            wide = lambda parts: jnp.concatenate(parts, axis=1)
            qr, vv, gg = q_ref[sl, :], v_ref[sl, :], g_ref[sl, :]
            q, sq, sig, f, k, logf = _hg_chunk_common(qr, fz_ref[sl, :], lb_all)
            G = _tri_dot(tri, logf)
            Gm = _row_of(G, rows_w, C // 2 - 1)
            Gl = _row_of(G, rows_w, C - 1)
            eG, e_qm, e_km, e_lk, eGl = jnp.exp(G), jnp.exp(G - Gm), jnp.exp(Gm - G), jnp.exp(Gl - G), jnp.exp(Gl)
            qt, kt, kdec, qe = q * e_qm, k * e_km, k * e_lk, q * eG
            sg = _sigmoid(gg)
            d_onw = dy_ref[sl, :] * (gg * sg)
            u = d_onw * gnw
            o = o_ref[sl, :]
            on, do = [], []
            for hh, ls in heads:
                r = lax.rsqrt(jnp.mean(o[:, ls] * o[:, ls], axis=-1, keepdims=True) + EPS)
                on.append(o[:, ls] * r)
                dgn_ref[hh] += jnp.sum(d_onw[:, ls] * on[hh], axis=0, keepdims=True)
                do.append(r * (u[:, ls] - on[hh] * jnp.mean(u[:, ls] * on[hh], axis=-1, keepdims=True)))
            dgg = dy_ref[sl, :] * (wide(on) * gnw) * (sg * (1.0 + gg * (1.0 - sg)))
            Sv = [sts_ref[hh, ci] for hh, _ in heads]
            dSv = [dst[hh] for hh, _ in heads]
            A = [jnp.where(low, _dg1(qt[:, ls], kt[:, ls], NT), 0.0) for _, ls in heads]
            dA = [jnp.where(low, _dg3(do[hh], vv[:, ls], NT), 0.0) for hh, ls in heads]
            dv = wide([_dg1(A[hh], do[hh], TN) + _dg1(kdec[:, ls], dSv[hh], NT) for hh, ls in heads])
            dq = wide([_dg3(dA[hh], kt[:, ls], NN) for hh, ls in heads]) * e_qm \
                + eG * wide([_dg3(do[hh], Sv[hh], NN) for hh, _ in heads])
            dk = wide([_dg3(dA[hh], qt[:, ls], TN) for hh, ls in heads]) * e_km \
                + e_lk * wide([_dg3(vv[:, ls], dSv[hh], NN) for hh, ls in heads])
            s_end = [Sv[hh] * eGl[:, ls] + _dg3(vv[:, ls], kdec[:, ls], TN) for hh, ls in heads]
            dgl = wide([jnp.sum(dSv[hh] * s_end[hh], axis=0, keepdims=True) for hh, _ in heads])
            for hh, ls in heads:
                dst[hh] = dSv[hh] * eGl[:, ls] + _dg1(do[hh], qe[:, ls], TN)
            dG = q * dq - k * dk + jnp.where(rows_w == C - 1, dgl, 0.0)
            dlogf = _tri_dot(triu, dG) - f * dk
            dlf_f = dlogf / f
            dlb_acc[...] += jnp.sum(dlf_f * (1.0 - sig), axis=0, keepdims=True)
            dp_ref[0, sl, :] = (dq * (sq * (1.0 + qr * (1.0 - sq)))).astype(BF)
            dp_ref[1, sl, :] = (dlf_f * (1.0 - lb_all) * sig * (1.0 - sig)).astype(BF)
            dp_ref[2, sl, :] = dv.astype(BF)
            dp_ref[3, sl, :] = dgg.astype(BF)
            return carry

        lax.fori_loop(0, nch, chunk, 0)
        sel = jnp.where(lax.broadcasted_iota(jnp.int32, (3, W), 0) == 0, 1.0, 0.0)
        dlb_ref[...] = lb_all * (sel - p3) * dlb_acc[...]

        if fused:
            @pl.when((pl.program_id(0) == ng - 1) & (n == nb - 1))
            def _():
                for cp in copies:
                    cp.wait()

    def col(p):
        return pl.BlockSpec((T, W), lambda h, n: (nb - 1 - n, p * ng + h))

    blk = pl.BlockSpec((T, W), lambda h, n: (nb - 1 - n, h))
    in_specs = [col(0), col(1), col(2), col(3),
                pl.BlockSpec((3, W), lambda h, n: (0, h)), pl.BlockSpec((1, LANES), lambda h, n: (0, 0)),
                blk, pl.BlockSpec((HB, nch, LANES, LANES), lambda h, n: (h, nb - 1 - n, 0, 0)), blk]
    out_specs = [pl.BlockSpec((4, T, W), lambda h, n: (0, nb - 1 - n, h)),
                 pl.BlockSpec((3, W), lambda h, n: (0, h)),
                 pl.BlockSpec((HB, 1, LANES), lambda h, n: (h, 0, 0))]
    out_shape = [jax.ShapeDtypeStruct((4, S, D), BF), jax.ShapeDtypeStruct((3, D), F32),
                 jax.ShapeDtypeStruct((H, 1, LANES), F32)]
    scratch = [pltpu.VMEM((HB, LANES, LANES), F32), pltpu.VMEM((1, W), F32)]
    args = [proj, proj, proj, proj, hg_lb, gn, o_all, states, dy]
    if fused:
        in_specs.append(HBM)
        out_specs.append(HBM)
        out_shape.append(jax.ShapeDtypeStruct((3,) + part.shape[1:], part.dtype))
        scratch += [pltpu.SemaphoreType.DMA((3,)), pltpu.SemaphoreType.DMA((3,))]
        args.append(part)
    return pl.pallas_call(
        body, name=name, grid=(ng, nb), in_specs=in_specs, out_specs=out_specs, out_shape=out_shape,
        scratch_shapes=scratch, compiler_params=_cp("arbitrary", "arbitrary"),
    )(*args)


def _log_sigmoid(u):
    return jnp.minimum(u, 0.0) - jnp.log(1.0 + jnp.exp(-jnp.abs(u)))


def _lane_put(base, lane, first, pieces):
    for n, p in enumerate(pieces):
        base = jnp.where(lane == first + n, p, base)
    return base


def _fox_cumsum(proj, bf_pad, *, name):
    S = proj.shape[0]
    D = proj.shape[1] // 5
    T = _pick(S, 256, 8)

    def body(fz_ref, b_ref, f_ref, carry):
        @pl.when(pl.program_id(0) == 0)
        def _():
            carry[...] = jnp.zeros_like(carry)

        logf = _log_sigmoid(fz_ref[...] + b_ref[...])
        tri = jnp.where(lax.broadcasted_iota(jnp.int32, (T, T), 0) >= lax.broadcasted_iota(jnp.int32, (T, T), 1),
                        1.0, 0.0).astype(BF)
        fv = _tri_dot(tri, logf) + carry[...]
        f_ref[...] = fv
        carry[...] = _row_of(fv, lax.broadcasted_iota(jnp.int32, (T, LANES), 0), T - 1)

    return pl.pallas_call(
        body, name=name, grid=(S // T,),
        in_specs=[pl.BlockSpec((T, LANES), lambda i: (i, 4 * D // LANES)), pl.BlockSpec((1, LANES), lambda i: (0, 0))],
        out_specs=pl.BlockSpec((T, LANES), lambda i: (i, 0)),
        out_shape=jax.ShapeDtypeStruct((S, LANES), F32),
        scratch_shapes=[pltpu.VMEM((1, LANES), F32)],
        compiler_params=_cp("arbitrary"),
    )(proj, bf_pad)


def _pair_stats(sq, lo):
    del lo
    a = lax.broadcasted_iota(jnp.int32, (LANES, LANES), 0) < FOX_DH
    b = lax.broadcasted_iota(jnp.int32, (LANES, LANES), 1) < FOX_DH
    avg = jnp.where(a == b, 1.0 / FOX_DH, 0.0).astype(BF)
    hi, mid, low = _split3(sq)
    return _dot(hi, avg) + _dot(mid, avg) + _dot(low, avg)


def _fox_prep(proj, fcum, qw2, kw2, *, name):
    S = proj.shape[0]
    D = proj.shape[1] // 5
    HP = D // LANES
    T = _pick(S, FOX_ROWS_PER_STEP, 16)

    def body(q_ref, k_ref, v_ref, f_ref, qw_ref, kw_ref, qa_ref, ka_ref, va_ref, vt_ref):
        hp = pl.program_id(1)
        lane = lax.broadcasted_iota(jnp.int32, (T, LANES), 1)
        lo = lane < FOX_DH
        qv, kv, vv, fv = q_ref[...], k_ref[...], v_ref[...], f_ref[...]
        qn = qv * lax.rsqrt(_pair_stats(qv * qv, lo) + EPS) * qw_ref[...] * (0.125 * LOG2E)
        kn = kv * lax.rsqrt(_pair_stats(kv * kv, lo) + EPS) * kw_ref[...]
        ones_q = jnp.where((lane >= 67) & (lane <= 69), 1.0, 0.0)
        ones_k = jnp.where(((lane >= 64) & (lane <= 66)) | ((lane >= 70) & (lane <= 72)), 1.0, 0.0)
        ones_v = jnp.where((lane >= 64) & (lane <= 66), 1.0, 0.0)
        for hh in range(2):
            fh = jnp.sum(jnp.where(lane == 2 * hp + hh, fv, 0.0), axis=-1, keepdims=True) * LOG2E
            pieces = [p.astype(F32) for p in _split3(fh)]

            def half(x):
                return jnp.where(lo, x if hh == 0 else pltpu.roll(x, FOX_DH, 1), 0.0)

            qa_ref[hh] = _lane_put(half(qn) + ones_q, lane, 64, pieces).astype(BF)
            ka_ref[hh] = _lane_put(half(kn) + ones_k, lane, 67, [-p for p in pieces]).astype(BF)
            va = half(vv) + ones_v
            va_ref[hh] = va.astype(BF)
            vt_ref[hh] = va.T.astype(BF)

    def part(p):
        return pl.BlockSpec((T, LANES), lambda i, hp: (i, p * HP + hp))

    vec = pl.BlockSpec((1, LANES), lambda i, hp: (0, 0))
    aug = pl.BlockSpec((2, T, LANES), lambda i, hp: (hp, i, 0))
    return pl.pallas_call(
        body, name=name, grid=(S // T, HP),
        in_specs=[part(0), part(1), part(2), pl.BlockSpec((T, LANES), lambda i, hp: (i, 0)), vec, vec],
        out_specs=[aug, aug, aug, pl.BlockSpec((2, LANES, T), lambda i, hp: (hp, 0, i))],
        out_shape=[jax.ShapeDtypeStruct((2 * HP, S, LANES), BF)] * 3 + [jax.ShapeDtypeStruct((2 * HP, LANES, S), BF)],
        compiler_params=_cp("parallel", "arbitrary"),
    )(proj, proj, proj, fcum, qw2, kw2)


def _fox_block(S):
    return _pick(S, 256, 16)


def _fox_skip_bounds(fcum, qn_w, kn_w, nheads):
    S = fcum.shape[0]
    B = _fox_block(S)
    qk = 8.0 * LOG2E * 1.02 * jnp.max(jnp.abs(qn_w)) * jnp.max(jnp.abs(kn_w))
    thresh = -(2.0 * qk + 152.0)
    f2 = fcum[:, :nheads] * LOG2E
    first, last = f2[0::B], f2[B - 1::B]
    nb = S // B
    blk = jnp.arange(nb)
    dead = (first[0::2, None, :] - last[None, :, :]) < thresh
    jmin = jnp.sum(dead & (blk[None, :, None] < 2 * jnp.arange(nb // 2)[:, None, None]), axis=1)
    live = (first[:, None, :] - last[None, :, :]) >= thresh
    imax = blk[:, None] + jnp.sum(live & (blk[:, None, None] > blk[None, :, None]), axis=0)
    return jmin.T.astype(jnp.int32), imax.T.astype(jnp.int32)


def _fox_fwd(jmin, qa, ka, vat, proj, *, name):
    H, S, _ = qa.shape
    HP = H // 2
    D = HP * LANES
    B = _fox_block(S)
    BQ = 2 * B
    nq = S // BQ

    def body(jmin_ref, q_ref, k_ref, vt_ref, g_ref, y_ref, o_ref, q2_ref):
        hp, i = pl.program_id(0), pl.program_id(1)
        lane = lax.broadcasted_iota(jnp.int32, (BQ, LANES), 1)
        lo = lane < FOX_DH
        in_stat = (lane >= 70) & (lane <= 75)
        causal = lax.broadcasted_iota(jnp.int32, (BQ, BQ), 0) <= lax.broadcasted_iota(jnp.int32, (BQ, BQ), 1)
        row = lax.broadcasted_iota(jnp.int32, (LANES, BQ), 0)
        m0, acc0 = jnp.full((1, BQ), -jnp.inf, F32), jnp.zeros((LANES, BQ), F32)
        outs = []
        for hh in range(2):
            qb = q_ref[hh]

            def scores(j):
                sl = pl.ds(pl.multiple_of(j * BQ, BQ), BQ)
                return _dg(k_ref[hh, sl, :], qb, NT)

            def update(j, m, acc, st, masked=False):
                sl = pl.ds(pl.multiple_of(j * BQ, BQ), BQ)
                if masked:
                    st = jnp.where(causal, st, -jnp.inf)
                m_new = jnp.maximum(m, jnp.ceil(jnp.max(st, axis=0, keepdims=True)))
                p = jnp.exp2(st - m_new).astype(BF)
                return m_new, acc * jnp.exp2(m - m_new) + _dot(vt_ref[hh, :, sl], p)

            def step(j, carry):
                m, acc, st = carry
                st_next = scores(j + 1)
                return update(j, m, acc, st) + (st_next,)

            first = jmin_ref[2 * hp + hh, i] // 2
            m, acc, st = lax.fori_loop(first, i, step, (m0, acc0, scores(first)))
            m, acc = update(i, m, acc, st, masked=True)
            linv = 1.0 / jnp.sum(jnp.where(row == FOX_DH, acc, 0.0), axis=0, keepdims=True)
            tile = acc * linv
            for n, piece in enumerate(_split3(m) + _split3(linv)):
                tile = jnp.where(row == 70 + n, piece.astype(F32), tile)
            tile = tile.T
            outs.append(tile)
            q2_ref[hh] = jnp.where(in_stat, jnp.where(lane <= 72, -tile, tile), qb.astype(F32)).astype(BF)
        o = jnp.where(lo, outs[0], pltpu.roll(outs[1], FOX_DH, 1))
        o_ref[...] = o
        y_ref[...] = (o * _sigmoid(g_ref[...])).astype(BF)

    blk = pl.BlockSpec((BQ, LANES), lambda hp, i, jm: (i, hp))
    qblk = pl.BlockSpec((2, BQ, LANES), lambda hp, i, jm: (hp, i, 0))
    full = pl.BlockSpec((2, S, LANES), lambda hp, i, jm: (hp, 0, 0))
    full_t = pl.BlockSpec((2, LANES, S), lambda hp, i, jm: (hp, 0, 0))
    return pl.pallas_call(
        body, name=name,
        grid_spec=pltpu.PrefetchScalarGridSpec(
            num_scalar_prefetch=1, grid=(HP, nq),
            in_specs=[qblk, full, full_t, pl.BlockSpec((BQ, LANES), lambda hp, i, jm: (i, 3 * HP + hp))],
            out_specs=[blk, blk, qblk]),
        out_shape=[jax.ShapeDtypeStruct((S, D), BF), jax.ShapeDtypeStruct((S, D), F32),
                   jax.ShapeDtypeStruct((H, S, LANES), BF)],
        compiler_params=_cp("parallel", "arbitrary"),
    )(jmin, qa, ka, vat, proj)


def _fox_bwd_prep(dy, o, proj, q2, *, name):
    S, D = dy.shape
    HP = D // LANES
    T = _pick(S, FOX_ROWS_PER_STEP, 16)

    def body(dy_ref, o_ref, g_ref, q2_ref, da_ref):
        lane = lax.broadcasted_iota(jnp.int32, (T, LANES), 1)
        lo = lane < FOX_DH
        in_linv = (lane >= 73) & (lane <= 75)
        linv = [jnp.sum(jnp.where(in_linv, q2_ref[hh].astype(F32), 0.0), axis=-1, keepdims=True) for hh in range(2)]
        u = (dy_ref[...] * _sigmoid(g_ref[...]) * jnp.where(lo, linv[0], linv[1])).astype(BF).astype(F32)
        prod = u * o_ref[...]
        d_lo = jnp.sum(jnp.where(lo, prod, 0.0), axis=-1, keepdims=True)
        d_hi = jnp.sum(jnp.where(lo, 0.0, prod), axis=-1, keepdims=True)
        for hh, delta in enumerate((d_lo, d_hi)):
            base = jnp.where(lo, u if hh == 0 else pltpu.roll(u, FOX_DH, 1), 0.0)
            da_ref[hh] = _lane_put(base, lane, 64, [-(p.astype(F32)) for p in _split3(delta)]).astype(BF)

    blk = pl.BlockSpec((T, LANES), lambda i, hp: (i, hp))
    aug = pl.BlockSpec((2, T, LANES), lambda i, hp: (hp, i, 0))
    return pl.pallas_call(
        body, name=name, grid=(S // T, HP),
        in_specs=[blk, blk, pl.BlockSpec((T, LANES), lambda i, hp: (i, 3 * HP + hp)), aug],
        out_specs=aug,
        out_shape=jax.ShapeDtypeStruct((2 * HP, S, LANES), BF),
        compiler_params=_cp("parallel", "arbitrary"),
    )(dy, o, proj, q2)


def _fox_bwd(imax, q2, ka, va, doa, *, name):
    H, S, _ = q2.shape
    B = _fox_block(S)
    nb = S // B

    def body(imax_ref, q_ref, do_ref, k_ref, v_ref, dq_ref, dk_ref, dv_ref, cs_ref):
        j = pl.program_id(1)
        end = imax_ref[pl.program_id(0), j] + 1

        @pl.when(j == 0)
        def _():
            dq_ref[...] = jnp.zeros_like(dq_ref)

        kb, vb = k_ref[...], v_ref[...]

        def step(i, carry, nblk=1):
            dk_acc, dv_acc, cs_acc = carry
            rows = nblk * B
            sl = pl.ds(pl.multiple_of(i * B, B), rows)
            qb, dob = q_ref[sl, :], do_ref[sl, :]
            s = _dg(qb, kb, NT)
            ahead = lax.broadcasted_iota(jnp.int32, (rows, B), 0) - lax.broadcasted_iota(jnp.int32, (rows, B), 1)
            pb = jnp.exp2(jnp.where(ahead >= (j - i) * B, s, -jnp.inf)).astype(BF)
            ds = pb.astype(F32) * _dg(dob, vb, NT)
            dsb = ds.astype(BF)
            cs_acc = cs_acc + jnp.sum(ds.reshape(rows // 8, 8, B), axis=0)
            dv_acc = dv_acc + _dg(pb, dob, TN)
            dk_acc = dk_acc + _dg(dsb, qb, TN)
            dq_ref[sl, :] += _dot(dsb, kb)
            return dk_acc, dv_acc, cs_acc

        zero = jnp.zeros((B, LANES), F32)
        carry = (zero, zero, jnp.zeros((8, B), F32))
        pos = j
        for U in FOX_BWD_TILES:
            n = (end - pos) // U
            carry = lax.fori_loop(0, n, lambda ii, c, pos=pos, U=U: step(pos + U * ii, c, nblk=U), carry)
            pos = pos + U * n
        dk_acc, dv_acc, cs_acc = carry
        dk_ref[...] = dk_acc
        dv_ref[...] = dv_acc
        cs_ref[...] = jnp.sum(cs_acc, axis=0, keepdims=True)

    full = pl.BlockSpec((None, S, LANES), lambda h, j, im: (h, 0, 0))
    blk = pl.BlockSpec((None, B, LANES), lambda h, j, im: (h, j, 0))
    return pl.pallas_call(
        body, name=name,
        grid_spec=pltpu.PrefetchScalarGridSpec(
            num_scalar_prefetch=1, grid=(H, nb),
            in_specs=[full, full, blk, blk],
            out_specs=[full, blk, blk, pl.BlockSpec((None, 1, B), lambda h, j, im: (h, 0, j))]),
        out_shape=[jax.ShapeDtypeStruct((H, S, LANES), F32)] * 3 + [jax.ShapeDtypeStruct((H, 1, S), F32)],
        compiler_params=_cp("parallel", "arbitrary"),
    )(imax, q2, doa, ka, va)


def _fox_bwd_post(dqa, dka, dva, proj, dy, o, qw2, kw2, *, name):
    S, D = dy.shape
    HP = D // LANES
    T = _pick(S, FOX_ROWS_PER_STEP, 16)

    def body(dq_ref, dk_ref, dv_ref, q_ref, k_ref, g_ref, dy_ref, o_ref, qw_ref, kw_ref, dp_ref, dqw_ref, dkw_ref):
        @pl.when((pl.program_id(0) == 0) & (pl.program_id(1) == 0))
        def _():
            dqw_ref[...] = jnp.zeros_like(dqw_ref)
            dkw_ref[...] = jnp.zeros_like(dkw_ref)

        lane = lax.broadcasted_iota(jnp.int32, (T, LANES), 1)
        lo = lane < FOX_DH

        def pair(ref):
            return jnp.where(lo, ref[0], pltpu.roll(ref[1], FOX_DH, 1))

        def norm_bwd(xv, w, dyn, dw_ref):
            r = lax.rsqrt(_pair_stats(xv * xv, lo) + EPS)
            xr = xv * r
            dw_ref[...] += jnp.sum(dyn * xr, axis=0, keepdims=True)
            u = dyn * w
            return r * (u - xr * _pair_stats(u * xr, lo))

        dp_ref[0] = norm_bwd(q_ref[...], qw_ref[...], pair(dq_ref) * 0.125, dqw_ref).astype(BF)
        dp_ref[1] = norm_bwd(k_ref[...], kw_ref[...], pair(dk_ref) * (1.0 / LOG2E), dkw_ref).astype(BF)
        dp_ref[2] = pair(dv_ref).astype(BF)
        sg = _sigmoid(g_ref[...])
        dp_ref[3] = (dy_ref[...] * o_ref[...] * sg * (1.0 - sg)).astype(BF)

    def part(p):
        return pl.BlockSpec((T, LANES), lambda i, hp: (i, p * HP + hp))

    aug = pl.BlockSpec((2, T, LANES), lambda i, hp: (hp, i, 0))
    blk = pl.BlockSpec((T, LANES), lambda i, hp: (i, hp))
    vec = pl.BlockSpec((1, LANES), lambda i, hp: (0, 0))
    return pl.pallas_call(
        body, name=name, grid=(S // T, HP),
        in_specs=[aug, aug, aug, part(0), part(1), part(3), blk, blk, vec, vec],
        out_specs=[pl.BlockSpec((4, T, LANES), lambda i, hp: (0, i, hp)), vec, vec],
        out_shape=[jax.ShapeDtypeStruct((5, S, D), BF), jax.ShapeDtypeStruct((1, LANES), F32),
                   jax.ShapeDtypeStruct((1, LANES), F32)],
        compiler_params=_cp("arbitrary", "arbitrary"),
    )(dqa, dka, dva, proj, proj, proj, dy, o, qw2, kw2)


def _fox_dfz(colsum, nheads, proj, bf_pad, dproj, *, name):
    S = colsum.shape[0]
    H = nheads
    D = dproj.shape[2]
    T = _pick(S, 256, 16)
    nb = S // T

    def body(cs_ref, fz_ref, b_ref, _, dp_ref, db_ref, carry):
        @pl.when(pl.program_id(0) == 0)
        def _():
            carry[...] = jnp.zeros_like(carry)
            db_ref[...] = jnp.zeros_like(db_ref)

        lane = lax.broadcasted_iota(jnp.int32, (T, LANES), 1)
        df = -cs_ref[...]
        triu = jnp.where(lax.broadcasted_iota(jnp.int32, (T, T), 0) <= lax.broadcasted_iota(jnp.int32, (T, T), 1),
                         1.0, 0.0).astype(BF)
        dlogf = _tri_dot(triu, df) + carry[...]
        carry[...] = _row_of(dlogf, lax.broadcasted_iota(jnp.int32, (T, LANES), 0), 0)
        dfz = jnp.where(lane < H, dlogf * _sigmoid(-(fz_ref[...] + b_ref[...])), 0.0)
        db_ref[...] += jnp.sum(dfz, axis=0, keepdims=True)
        dp_ref[...] = jnp.zeros_like(dp_ref)
        dp_ref[:, 0:LANES] = dfz.astype(BF)

    return pl.pallas_call(
        body, name=name, grid=(nb,),
        in_specs=[pl.BlockSpec((T, LANES), lambda i: (nb - 1 - i, 0)),
                  pl.BlockSpec((T, LANES), lambda i: (nb - 1 - i, 4 * D // LANES)),
                  pl.BlockSpec((1, LANES), lambda i: (0, 0)),
                  pl.BlockSpec(memory_space=pl.ANY)],
        out_specs=[pl.BlockSpec((None, T, D), lambda i: (4, nb - 1 - i, 0)), pl.BlockSpec((1, LANES), lambda i: (0, 0))],
        out_shape=[jax.ShapeDtypeStruct(dproj.shape, BF), jax.ShapeDtypeStruct((1, LANES), F32)],
        scratch_shapes=[pltpu.VMEM((1, LANES), F32)],
        input_output_aliases={3: 0},
        compiler_params=_cp("arbitrary"),
    )(colsum, proj, bf_pad, dproj)


def _mod_fwd(c16, w, b, *, name):
    L, D, N = w.shape
    tn = _pick(N, 512)

    def body(c_ref, w_ref, b_ref, o_ref):
        cv = c_ref[...]
        ca = (cv * _sigmoid(cv)).astype(BF)
        o_ref[...] = _dot(ca, w_ref[...].astype(BF)) + b_ref[...]

    return pl.pallas_call(
        body, name=name, grid=(L, N // tn),
        in_specs=[pl.BlockSpec((16, D), lambda l, j: (0, 0)), pl.BlockSpec((None, D, tn), lambda l, j: (l, 0, j)),
                  pl.BlockSpec((None, 1, tn), lambda l, j: (l, 0, j))],
        out_specs=pl.BlockSpec((None, 16, tn), lambda l, j: (l, 0, j)),
        out_shape=jax.ShapeDtypeStruct((L, 16, N), F32),
        compiler_params=_cp("parallel", "arbitrary"),
    )(c16, w, b)


def _mod_bwd(c16, dmod, *, name):
    L, _, N = dmod.shape
    D = c16.shape[1]
    tn = _pick(N, 512)

    def body(c_ref, d_ref, o_ref):
        cv = c_ref[...]
        ca = (cv * _sigmoid(cv)).astype(BF)
        o_ref[...] = _dg(ca, d_ref[...].astype(BF), TN)

    return pl.pallas_call(
        body, name=name, grid=(L, N // tn),
        in_specs=[pl.BlockSpec((16, D), lambda l, j: (0, 0)), pl.BlockSpec((None, 16, tn), lambda l, j: (l, 0, j))],
        out_specs=pl.BlockSpec((None, D, tn), lambda l, j: (l, 0, j)),
        out_shape=jax.ShapeDtypeStruct((L, D, N), F32),
        compiler_params=_cp("parallel", "arbitrary"),
    )(c16, dmod)


def _adamw_math(w, g, m, v):
    m = ADAM_B1 * m + (1.0 - ADAM_B1) * g
    v = ADAM_B2 * v + (1.0 - ADAM_B2) * (g * g)
    m_hat = m / (1.0 - ADAM_B1 ** ADAM_STEP)
    v_hat = v / (1.0 - ADAM_B2 ** ADAM_STEP)
    return -ADAM_LR * (m_hat / (jnp.sqrt(v_hat) + ADAM_EPS) + ADAM_WD * w), m, v


def _adamw(w, g, m, v, *, g_at=None, name):
    R, C = w.shape
    row0 = 0 if g_at is None else g_at[1]
    tr = min(math.gcd(row0, 256) if row0 else 256, -(-R // 8) * 8)
    g0 = row0 // tr
    if g_at is None:
        g_spec = pl.BlockSpec((tr, C), lambda i: (i, 0))
    else:
        g_spec = pl.BlockSpec((None, tr, C), lambda i: (g_at[0], g0 + i, 0))

    def body(w_ref, g_ref, m_ref, v_ref, d_ref, mo_ref, vo_ref):
        d, mn, vn = _adamw_math(w_ref[...], g_ref[...], m_ref[...], v_ref[...])
        d_ref[...] = d
        mo_ref[...] = mn
        vo_ref[...] = vn

    blk = pl.BlockSpec((tr, C), lambda i: (i, 0))
    return pl.pallas_call(
        body, name=name, grid=(pl.cdiv(R, tr),),
        in_specs=[blk, g_spec, blk, blk],
        out_specs=[blk, blk, blk],
        out_shape=[jax.ShapeDtypeStruct((R, C), F32)] * 3,
        compiler_params=_cp("parallel"),
    )(w, g, m, v)


def _sum_parts(parts, *, name):
    P, R, C = parts.shape

    def body(p_ref, o_ref):
        acc = p_ref[0]
        for p in range(1, P):
            acc = acc + p_ref[p]
        o_ref[...] = acc

    return pl.pallas_call(
        body, name=name, grid=(1,),
        in_specs=[pl.BlockSpec((P, R, C), lambda i: (0, 0, 0))],
        out_specs=pl.BlockSpec((R, C), lambda i: (0, 0)),
        out_shape=jax.ShapeDtypeStruct((R, C), F32),
        compiler_params=_cp("arbitrary"),
    )(parts)


def _add_halves(g4, recv, c_idx, *, name):
    _, _, Rh, C = g4.shape
    tr = min(256, Rh)

    def body(c_ref, a_ref, b_ref, o_ref):
        o_ref[...] = (a_ref[...] + b_ref[...].astype(F32)).astype(BF)

    return pl.pallas_call(
        body, name=name,
        grid_spec=pltpu.PrefetchScalarGridSpec(
            num_scalar_prefetch=1, grid=(4, pl.cdiv(Rh, tr)),
            in_specs=[pl.BlockSpec((None, None, tr, C), lambda j, r, c: (j, c[0], r, 0)),
                      pl.BlockSpec((None, tr, C), lambda j, r, c: (j, r, 0))],
            out_specs=pl.BlockSpec((None, tr, C), lambda j, r, c: (j, r, 0))),
        out_shape=jax.ShapeDtypeStruct((4, Rh, C), BF),
        compiler_params=_cp("parallel", "arbitrary"),
    )(c_idx, g4, recv)


def _add_four(g4, from_sibling, from_chips, pos, *, name):
    _, _, Rh, C = g4.shape
    tr = min(256, Rh)

    def body(p_ref, a_ref, s_ref, b_ref, o_ref):
        own = a_ref[...] + s_ref[...].astype(F32)
        o_ref[...] = ((own + b_ref[0].astype(F32)) + b_ref[1].astype(F32)) + b_ref[2].astype(F32)

    return pl.pallas_call(
        body, name=name,
        grid_spec=pltpu.PrefetchScalarGridSpec(
            num_scalar_prefetch=1, grid=(pl.cdiv(Rh, tr),),
            in_specs=[pl.BlockSpec((None, None, tr, C), lambda r, p: (p[0], p[1], r, 0)),
                      pl.BlockSpec((None, tr, C), lambda r, p: (p[0], r, 0)),
                      pl.BlockSpec((3, tr, C), lambda r, p: (0, r, 0))],
            out_specs=pl.BlockSpec((None, tr, C), lambda r, p: (p[1], r, 0))),
        out_shape=jax.ShapeDtypeStruct((2, Rh, C), F32),
        compiler_params=_cp("arbitrary"),
    )(pos, g4, from_sibling, from_chips)


HBM = pl.BlockSpec(memory_space=pltpu.HBM)


def _mesh_pos():
    return lax.axis_index("x"), lax.axis_index("y"), lax.axis_index("c")


def _other_chips(x, y):
    return [(1 - x, y), (x, 1 - y), (1 - x, 1 - y)]


def _allgather_small(xs, *, name):
    m_per, n = xs.shape

    def body(x_ref, out_ref, send_sems, recv_sems, local_sem):
        x, y, c = _mesh_pos()
        me, sibling = (x, y, c), (x, y, 1 - c)
        chips = _other_chips(x, y)

        def rows(px, py, pc):
            return out_ref.at[pl.ds((4 * px + 2 * py + pc) * m_per, m_per), :]

        def copy(k, block, to, src=None):
            return pltpu.make_async_remote_copy(
                src_ref=rows(*block) if src is None else src, dst_ref=rows(*block),
                send_sem=send_sems.at[k], recv_sem=recv_sems.at[k], device_id=to, device_id_type=MESH)

        mine = pltpu.make_async_copy(x_ref, rows(*me), local_sem)
        mine.start()
        first = [copy(0, me, sibling, src=x_ref)]
        first += [copy(1 + j, me, (*chip, c), src=x_ref) for j, chip in enumerate(chips)]
        for cp in first:
            cp.start()
        passed = [copy(4 + j, (*chip, c), sibling) for j, chip in enumerate(chips)]
        for j, chip in enumerate(chips):
            copy(1 + j, (*chip, c), me).wait_recv()
            passed[j].start()
        copy(0, sibling, me).wait_recv()
        for j, chip in enumerate(chips):
            copy(4 + j, (*chip, 1 - c), me).wait_recv()
        for cp in first + passed:
            cp.wait_send()
        mine.wait()

    return pl.pallas_call(
        body, name=name,
        out_shape=jax.ShapeDtypeStruct((N_DEV * m_per, n), xs.dtype),
        in_specs=[pl.BlockSpec(memory_space=pltpu.VMEM)],
        out_specs=pl.BlockSpec(memory_space=pltpu.VMEM),
        scratch_shapes=[pltpu.SemaphoreType.DMA((7,)), pltpu.SemaphoreType.DMA((7,)), pltpu.SemaphoreType.DMA],
    )(xs)


def _chip_slab_copies(s_ref, out_ref, send_sems, recv_sems):
    R = s_ref.shape[0]
    Rh = R // 2
    x, y, c = _mesh_pos()
    me, sibling = (x, y, c), (x, y, 1 - c)
    chips = _other_chips(x, y)

    def half(px, py, pc):
        return out_ref.at[2 * px + py, pl.ds(pc * Rh, Rh), :]

    def copy(k, block, to, src=None):
        return pltpu.make_async_remote_copy(
            src_ref=half(*block) if src is None else src, dst_ref=half(*block),
            send_sem=send_sems.at[k], recv_sem=recv_sems.at[k], device_id=to, device_id_type=MESH)

    first = [copy(j, me, (*chip, c), src=s_ref.at[pl.ds(c * Rh, Rh), :]) for j, chip in enumerate(chips)]
    passed = [copy(3 + j, (*chip, c), sibling) for j, chip in enumerate(chips)]
    landed = [copy(j, (*chip, c), me) for j, chip in enumerate(chips)]
    from_sibling = [copy(3 + j, (*chip, 1 - c), me) for j, chip in enumerate(chips)]
    return first, passed, landed, from_sibling


def _gather_behind(s_ref, out_ref, send_sems, recv_sems, step, nsteps):
    first, passed, landed, from_sibling = _chip_slab_copies(s_ref, out_ref, send_sems, recv_sems)

    @pl.when(step == 0)
    def _():
        for cp in first:
            cp.start()

    @pl.when(step == (3 * nsteps) // 4)
    def _():
        for arrived, onward in zip(landed, passed):
            arrived.wait_recv()
            onward.start()

    def finish():
        @pl.when(step == nsteps - 1)
        def _():
            for cp in from_sibling:
                cp.wait_recv()
            for cp in first + passed:
                cp.wait_send()

    return finish


def _allgather_chip_slabs(slab, *, name):
    R, C = slab.shape

    def body(s_ref, out_ref, send_sems, recv_sems):
        first, passed, landed, from_sibling = _chip_slab_copies(s_ref, out_ref, send_sems, recv_sems)
        for cp in first:
            cp.start()
        for arrived, onward in zip(landed, passed):
            arrived.wait_recv()
            onward.start()
        for cp in from_sibling:
            cp.wait_recv()
        for cp in first + passed:
            cp.wait_send()

    return pl.pallas_call(
        body, name=name,
        out_shape=jax.ShapeDtypeStruct((N_CHIPS, R, C), slab.dtype),
        in_specs=[HBM], out_specs=HBM,
        scratch_shapes=[pltpu.SemaphoreType.DMA((6,)), pltpu.SemaphoreType.DMA((6,))],
    )(slab)


def _swap_halves(mine, *, name):
    def body(g_ref, out_ref, send_sems, recv_sems):
        x, y, c = _mesh_pos()
        copies = [pltpu.make_async_remote_copy(
            src_ref=g_ref.at[j], dst_ref=out_ref.at[j], send_sem=send_sems.at[j], recv_sem=recv_sems.at[j],
            device_id=(x, y, 1 - c), device_id_type=MESH) for j in range(N_CHIPS)]
        for cp in copies:
            cp.start()
        for cp in copies:
            cp.wait()

    return pl.pallas_call(
        body, name=name,
        out_shape=jax.ShapeDtypeStruct(mine.shape, mine.dtype),
        in_specs=[HBM], out_specs=HBM,
        scratch_shapes=[pltpu.SemaphoreType.DMA((N_CHIPS,)), pltpu.SemaphoreType.DMA((N_CHIPS,))],
    )(mine)


def _scatter_copies(p_ref, out_ref, send_sems, recv_sems):
    x, y, c = _mesh_pos()
    return [pltpu.make_async_remote_copy(
        src_ref=p_ref.at[2 * px + py], dst_ref=out_ref.at[j], send_sem=send_sems.at[j], recv_sem=recv_sems.at[j],
        device_id=(px, py, c), device_id_type=MESH) for j, (px, py) in enumerate(_other_chips(x, y))]


def _join_halves(buf, *, name):
    def body(b_ref, out_ref, send_sem, recv_sem):
        x, y, c = _mesh_pos()
        cp = pltpu.make_async_remote_copy(
            src_ref=b_ref.at[c], dst_ref=out_ref.at[c], send_sem=send_sem, recv_sem=recv_sem,
            device_id=(x, y, 1 - c), device_id_type=MESH)
        cp.start()
        cp.wait()

    return pl.pallas_call(
        body, name=name,
        out_shape=jax.ShapeDtypeStruct(buf.shape, buf.dtype),
        in_specs=[HBM], out_specs=HBM, input_output_aliases={0: 0},
        scratch_shapes=[pltpu.SemaphoreType.DMA, pltpu.SemaphoreType.DMA],
    )(buf)


def _pad_rows(a, mult):
    pad = (-a.shape[0]) % mult
    return a if pad == 0 else jnp.pad(a, ((0, pad),) + ((0, 0),) * (a.ndim - 1))


def _local_step(x, target, mod, wts, small, slabs=None, unpacks=None, reduce_early=None, grad_slab=None,
                reduce_late=None):
    S, D = x.shape
    HP = D // LANES
    row = lambda v: v.reshape(1, -1)
    msplit = [[row(mod[i, k * D:(k + 1) * D]) for k in range(6)] for i in range(2)]
    gw, gs = {}, {}
    dmod = [[None] * 6 for _ in range(2)]
    slab, where = grad_slab if grad_slab is not None else (None, {})

    def dw(key, a, b, name, square=False):
        nonlocal slab
        if key in where:
            slab = _matmul_tn(a, b, name=name, into=(slab,) + where[key], square=square)
        else:
            gw[key] = _matmul_tn(a, b, name=name, square=square)

    sh1, sc1, g1, sh2, sc2, g2 = msplit[0]
    n1w0, n2w0 = row(small["norm1_w"][0]), row(small["norm2_w"][0])
    slabs = slabs if slabs is not None else (None, None, None)
    proj0, h1_0, *gathered = _ln_matmul(x, n1w0, sc1, sh1, wts["hg_w_in"], slabs[0], relu2=False, name="hg_in_proj")
    if slabs[0] is not None:
        wts = {**wts, **unpacks[0](gathered[0])}
    gn = small["hg_gn_w"].reshape(1, LANES)
    ypre0, o0, states, *gathered = _hg_fwd(proj0, small["hg_lb"], gn, slabs[1], name="hg_fwd")
    if slabs[1] is not None:
        wts = {**wts, **unpacks[1](gathered[0])}
    x1, ymix0 = _matmul_resid(ypre0, wts["hg_w_out"], x, g1, name="hg_out_proj")
    a0, h2_0, *gathered = _ln_matmul(x1, n2w0, sc2, sh2, wts["mlp_w1_0"], slabs[2], relu2=True, name="mlp0_up")
    if slabs[2] is not None:
        wts = {**wts, **unpacks[2](gathered[0])}
    x2, ymlp0 = _matmul_resid(a0, wts["mlp_w2_0"], x1, g2, square=True, name="mlp0_down")

    sh1b, sc1b, g1b, sh2b, sc2b, g2b = msplit[1]
    n1w1, n2w1 = row(small["norm1_w"][1]), row(small["norm2_w"][1])
    proj1, h1_1 = _ln_matmul(x2, n1w1, sc1b, sh1b, wts["fox_w_in"], relu2=False, name="fox_in_proj")
    nheads = 2 * HP
    bf_pad = jnp.pad(small["fox_b_f"].reshape(1, nheads), ((0, 0), (0, LANES - nheads)))
    qw2 = jnp.tile(small["fox_qn_w"].reshape(1, FOX_DH), (1, 2))
    kw2 = jnp.tile(small["fox_kn_w"].reshape(1, FOX_DH), (1, 2))
    fcum = _fox_cumsum(proj1, bf_pad, name="fox_cumsum")
    qa, ka, va, vat = _fox_prep(proj1, fcum, qw2, kw2, name="fox_prep")
    jmin, imax = _fox_skip_bounds(fcum, small["fox_qn_w"], small["fox_kn_w"], nheads)
    ypre1, o1, q2 = _fox_fwd(jmin, qa, ka, vat, proj1, name="fox_fwd")
    x3, ymix1 = _matmul_resid(ypre1, wts["fox_w_out"], x2, g1b, name="fox_out_proj")
    a1, h2_1 = _ln_matmul(x3, n2w1, sc2b, sh2b, wts["mlp_w1_1"], relu2=True, name="mlp1_up")
    x4, ymlp1 = _matmul_resid(a1, wts["mlp_w2_1"], x3, g2b, square=True, name="mlp1_down")

    loss, dx4, dfw = _loss_kernel(x4, row(small["final_w"]), target, name="loss")
    gs["final_w"] = dfw.reshape(-1)

    def mlp_bwd(i, dx_out, x_in, h2, a, ymlp, n2w, sc2_, g2_):
        dz, dm, dg2 = _gate_matmul_nt(dx_out, g2_, ymlp, wts[f"mlp_w2_{i}"], a, name=f"mlp{i}_down_bwd")
        dw(f"mlp_w2_{i}", a, dm[None], f"mlp{i}_dw2", square=True)
        dw(f"mlp_w1_{i}", h2, dz[None], f"mlp{i}_dw1")
        dx_in, dsc, dsh, dnw = _matmul_nt_lnbwd(dz[None], wts[f"mlp_w1_{i}"], x_in, n2w, sc2_, dx_out,
                                                name=f"mlp{i}_up_bwd")
        dmod[i][3], dmod[i][4], dmod[i][5] = dsh, dsc, dg2
        return dx_in, dnw

    dx3, dn2w1 = mlp_bwd(1, dx4, x3, h2_1, a1, ymlp1, n2w1, sc2b, g2b)
    dyp1, dm1, dg1b = _gate_matmul_nt(dx3, g1b, ymix1, wts["fox_w_out"], None, name="fox_out_bwd")
    dw("fox_w_out", ypre1, dm1[None], "fox_dw_out")
    doa = _fox_bwd_prep(dyp1, o1, proj1, q2, name="fox_bwd_prep")
    dqa, dka, dva, colsum = _fox_bwd(imax, q2, ka, va, doa, name="fox_bwd")
    colsum = jnp.pad(colsum[:, 0, :].T, ((0, 0), (0, LANES - nheads)))
    dproj1, dqw, dkw = _fox_bwd_post(dqa, dka, dva, proj1, dyp1, o1, qw2, kw2, name="fox_bwd_post")
    dproj1, dbf = _fox_dfz(colsum, nheads, proj1, bf_pad, dproj1, name="fox_dfz")
    dw("fox_w_in", h1_1, dproj1, "fox_dw_in")
    dx2, dsc, dsh, dn1w1 = _matmul_nt_lnbwd(dproj1, wts["fox_w_in"], x2, n1w1, sc1b, dx3, name="fox_in_bwd")
    dmod[1][0], dmod[1][1], dmod[1][2] = dsh, dsc, dg1b
    gs["fox_qn_w"] = dqw[0, :FOX_DH] + dqw[0, FOX_DH:]
    gs["fox_kn_w"] = dkw[0, :FOX_DH] + dkw[0, FOX_DH:]
    gs["fox_b_f"] = dbf[0, :nheads]

    dx1, dn2w0 = mlp_bwd(0, dx2, x1, h2_0, a0, ymlp0, n2w0, sc2, g2)
    dyp0, dm0, dg1 = _gate_matmul_nt(dx1, g1, ymix0, wts["hg_w_out"], None, name="hg_out_bwd")
    dw("hg_w_out", ypre0, dm0[None], "hg_dw_out")
    part, ctx = reduce_early(gw, slab) if reduce_early is not None else (None, None)
    dproj0, dlb, dgn, *from_chips = _hg_bwd(proj0, small["hg_lb"], gn, o0, states, dyp0, part, name="hg_bwd")
    early = (ctx, from_chips[0]) if reduce_early is not None else None
    dw("hg_w_in", h1_0, dproj0, "hg_dw_in")
    part, ctx = reduce_late(gw) if reduce_late is not None else (None, None)
    dx0, dsc, dsh, dn1w0, *from_chips = _matmul_nt_lnbwd(dproj0, wts["hg_w_in"], x, n1w0, sc1, dx1, part, name="hg_in_bwd")
    late = (ctx, from_chips[0]) if reduce_late is not None else None
    dmod[0][0], dmod[0][1], dmod[0][2] = dsh, dsc, dg1
    gs["hg_lb"] = dlb
    gs["hg_gn_w"] = jnp.sum(dgn, axis=0)

    gs["norm1_w"] = jnp.concatenate([dn1w0, dn1w1], axis=0)
    gs["norm2_w"] = jnp.concatenate([dn2w0, dn2w1], axis=0)
    gs["dmod"] = jnp.stack([jnp.concatenate(dmod[i], axis=1)[0] for i in range(2)])
    return loss, dx0, gw, gs, early, late


def _pack_halves(layout):
    rh = -(-max(sum(a.shape[0] for _, a in half) for half in layout) // 16) * 16
    place, parts = {}, []
    for h, half in enumerate(layout):
        off = 0
        for n, a in half:
            place[n] = (h, off, a.shape[0])
            off += a.shape[0]
        parts.append(jnp.pad(jnp.concatenate([a.astype(BF) for _, a in half], axis=0), ((0, rh - off), (0, 0))))
    return jnp.concatenate(parts, axis=0), place, rh


SMALL_NAMES = ["norm1_w", "norm2_w", "hg_lb", "hg_gn_w", "fox_b_f", "fox_qn_w", "fox_kn_w", "final_w"]


def _pack_small(d, names):
    rows, offs, r0 = [], {}, 0
    for n in names:
        flat = d[n].reshape(-1)
        nr = -(-flat.shape[0] // LANES)
        rows.append(jnp.pad(flat, (0, nr * LANES - flat.shape[0])).reshape(nr, LANES))
        offs[n] = (r0, nr)
        r0 += nr
    return jnp.concatenate(rows, axis=0), offs


def _unpack_small(packed, offs, name, like):
    r0, nr = offs[name]
    return packed[r0:r0 + nr].reshape(-1)[:like.size].reshape(like.shape)


def kernel(x, c, w_mod, b_mod, norm1_w, norm2_w, hg_w_in, hg_w_out, hg_lb, hg_gn_w, fox_w_in, fox_b_f, fox_qn_w, fox_kn_w, fox_w_out, mlp_w1, mlp_w2, final_w, loss_target, m_w_mod, m_b_mod, m_norm1_w, m_norm2_w, m_hg_w_in, m_hg_w_out, m_hg_lb, m_hg_gn_w, m_fox_w_in, m_fox_b_f, m_fox_qn_w, m_fox_kn_w, m_fox_w_out, m_mlp_w1, m_mlp_w2, m_final_w, v_w_mod, v_b_mod, v_norm1_w, v_norm2_w, v_hg_w_in, v_hg_w_out, v_hg_lb, v_hg_gn_w, v_fox_w_in, v_fox_b_f, v_fox_qn_w, v_fox_kn_w, v_fox_w_out, v_mlp_w1, v_mlp_w2, v_final_w):
    S, D = x.shape[1], x.shape[2]
    nheads = D // FOX_DH
    ax, ay, ac = _mesh_pos()
    chip = 2 * ax + ay
    dev = 2 * chip + ac
    xs, tgt = x.reshape(S, D), loss_target.reshape(S, D)

    c_all = _allgather_small(_pad_rows(c.reshape(-1, LANES), 8), name="gather_c")
    c_all = c_all.reshape(N_DEV, -1)[:, :D]
    c16 = _pad_rows(c_all, 16)
    nmod = w_mod.shape[2]
    b_shard = lax.dynamic_slice_in_dim(b_mod, chip * nmod, nmod, axis=1)
    mod_shard = _mod_fwd(c16, w_mod, b_shard[:, None, :], name="mod_fwd")[:, :N_DEV]
    mod_all = _allgather_small(mod_shard.reshape(-1, LANES), name="gather_mod")
    mod_all = mod_all.reshape(N_CHIPS, 2, 2, N_DEV, nmod)[:, 0]
    mod = lax.dynamic_index_in_dim(mod_all, dev, axis=2, keepdims=False)
    mod = mod.transpose(1, 0, 2).reshape(2, N_CHIPS * nmod)

    fox_rows = fox_w_in.shape[2]
    col = lambda g: g.transpose(1, 0, 2).reshape(g.shape[1], -1)
    rowsh = lambda g: g.reshape(-1, g.shape[2])
    own = lambda g, s: lax.dynamic_update_index_in_dim(g, s, chip, 0)

    slab_in = hg_w_in[0].astype(BF)
    wts = {"hg_w_in": col(own(_allgather_chip_slabs(slab_in, name="gather_hg_w_in"), slab_in))}
    fox_flat, fox_cut = fox_w_in[0].reshape(fox_rows, D), fox_rows // 2
    slabs, unpacks = [], []
    for layout_w in ([[("mlp_w1_0", mlp_w1[0])], [("mlp_w2_0", mlp_w2[0])]],
                     [[("mlp_w1_1", mlp_w1[1]), ("hg_w_out", hg_w_out[0])], [("mlp_w2_1", mlp_w2[1]), ("fox_w_out", fox_w_out[0])]],
                     [[("fox_a", fox_flat[:fox_cut])], [("fox_b", fox_flat[fox_cut:])]]):
        slab_w, place_w, rh_w = _pack_halves(layout_w)

        def unpack(gathered, slab_w=slab_w, place_w=place_w, rh_w=rh_w):
            gathered = own(gathered, slab_w)
            out = {}
            for n, (h, off, rows) in place_w.items():
                g = gathered[:, h * rh_w + off:h * rh_w + off + rows, :]
                out[n] = col(g) if n.startswith("mlp_w1") else rowsh(g) if n.startswith(("mlp_w2", "hg_", "fox_w")) else g
            if "fox_a" in out:
                fox_in = col(jnp.concatenate([out.pop("fox_a"), out.pop("fox_b")], axis=1).reshape(N_CHIPS, D, fox_rows))
                out["fox_w_in"] = jnp.pad(fox_in, ((0, 0), (0, 5 * D - fox_in.shape[1])))
            return out

        slabs.append(slab_w)
        unpacks.append(unpack)

    small = {"norm1_w": norm1_w, "norm2_w": norm2_w, "hg_lb": hg_lb, "hg_gn_w": hg_gn_w, "fox_b_f": fox_b_f,
             "fox_qn_w": fox_qn_w, "fox_kn_w": fox_kn_w, "final_w": final_w}

    def uncol(g, n):
        return g.reshape(g.shape[0], N_CHIPS, n).transpose(1, 0, 2)

    pos = jnp.stack([chip, ac])

    def swap_and_add(g4, tag):
        to_sibling = lax.dynamic_index_in_dim(g4, 1 - ac, axis=1, keepdims=False).astype(BF)
        from_sibling = _swap_halves(to_sibling, name=f"rs_swap_{tag}")
        return from_sibling, _add_halves(g4, from_sibling, ac.reshape(1), name=f"rs_add_halves_{tag}")

    def finish(g4, from_sibling, from_chips, tag):
        my_half = _add_four(g4, from_sibling, from_chips, pos, name=f"rs_add_chips_{tag}")
        return _join_halves(my_half, name=f"rs_join_{tag}")

    layout = [[("mlp_w1", 2 * D), ("hg_w_out", D // 4), ("fox_w_out", D // 4)], [("mlp_w2", 2 * D), ("fox_w_in", fox_rows)]]
    place = {}
    for h, half in enumerate(layout):
        off = 0
        for n, rows in half:
            place[n] = (h, off, rows)
            off += rows

    rh = -(-max(sum(rows for _, rows in half) for half in layout) // 16) * 16
    where = {"hg_w_out": ("row",) + place["hg_w_out"][:2], "fox_w_out": ("row",) + place["fox_w_out"][:2]}
    for i in range(2):
        where[f"mlp_w1_{i}"] = ("col", place["mlp_w1"][0], place["mlp_w1"][1] + i * D)
        where[f"mlp_w2_{i}"] = ("row", place["mlp_w2"][0], place["mlp_w2"][1] + i * D)

    def reduce_early(gw, slab):
        gfox = uncol(gw["fox_w_in"][:, :4 * fox_rows], fox_rows).reshape(N_CHIPS, 1, fox_rows, D)
        h, off, _ = place["fox_w_in"]
        slab = lax.dynamic_update_slice(slab, gfox, (0, h, off, 0))
        for h, half in enumerate(layout):
            used = sum(rows for _, rows in half)
            if used < rh:
                slab = lax.dynamic_update_slice(slab, jnp.zeros((N_CHIPS, 1, rh - used, D), F32), (0, h, used, 0))
        from_sibling, part = swap_and_add(slab, "early")
        return part, (slab, from_sibling)

    def reduce_late(gw):
        g4 = uncol(gw["hg_w_in"], D).reshape(N_CHIPS, 2, D // 2, D)
        from_sibling, part = swap_and_add(g4, "late")
        return part, (g4, from_sibling)

    loss_part, grad_x, gw, gs, (early, from_chips_early), (late, from_chips_late) = _local_step(
        xs, tgt, mod, wts, small, slabs, unpacks, reduce_early, (lax.empty((N_CHIPS, 2, rh, D), F32), where), reduce_late)
    gshard = finish(*early, from_chips_early, "early")
    g_hg_w_in = finish(*late, from_chips_late, "late").reshape(D, D)

    names = ["dmod", "loss"] + SMALL_NAMES
    packed, offs = _pack_small({**gs, "loss": loss_part[0, :1]}, names)
    packed = _pad_rows(packed, 8)
    rp = packed.shape[0]
    parts = _allgather_small(packed, name="gather_small").reshape(N_DEV, rp, LANES)
    total = _sum_parts(parts, name="sum_small")
    r0, nr = offs["dmod"]
    dmod_all = parts[:, r0:r0 + nr].reshape(N_DEV, 2, N_CHIPS * nmod)
    dmod_shard = lax.dynamic_slice_in_dim(dmod_all, chip * nmod, nmod, axis=2).transpose(1, 0, 2)
    g_w_mod = _mod_bwd(c16, jnp.pad(dmod_shard, ((0, 0), (0, 16 - N_DEV), (0, 0))), name="mod_bwd")

    loss = _unpack_small(total, offs, "loss", loss_part[0, :1]).reshape(())
    grads = {"w_mod": g_w_mod, "b_mod": _unpack_small(total, offs, "dmod", b_mod)}
    for n in SMALL_NAMES:
        grads[n] = _unpack_small(total, offs, n, small[n])

    given = dict(w_mod=(w_mod, m_w_mod, v_w_mod), b_mod=(b_mod, m_b_mod, v_b_mod), norm1_w=(norm1_w, m_norm1_w, v_norm1_w),
                 norm2_w=(norm2_w, m_norm2_w, v_norm2_w), hg_w_in=(hg_w_in, m_hg_w_in, v_hg_w_in),
                 hg_w_out=(hg_w_out, m_hg_w_out, v_hg_w_out), hg_lb=(hg_lb, m_hg_lb, v_hg_lb),
                 hg_gn_w=(hg_gn_w, m_hg_gn_w, v_hg_gn_w), fox_w_in=(fox_w_in, m_fox_w_in, v_fox_w_in),
                 fox_b_f=(fox_b_f, m_fox_b_f, v_fox_b_f), fox_qn_w=(fox_qn_w, m_fox_qn_w, v_fox_qn_w),
                 fox_kn_w=(fox_kn_w, m_fox_kn_w, v_fox_kn_w), fox_w_out=(fox_w_out, m_fox_w_out, v_fox_w_out),
                 mlp_w1=(mlp_w1, m_mlp_w1, v_mlp_w1), mlp_w2=(mlp_w2, m_mlp_w2, v_mlp_w2), final_w=(final_w, m_final_w, v_final_w))
    upd = {}

    for n, (h, off, rows) in place.items():
        w, m, v = given[n]
        flat = lambda a: a.reshape(rows, D)
        d, mn, vn = _adamw(flat(w), gshard, flat(m), flat(v), g_at=(h, off), name=f"adamw_{n}")
        grads[n] = gshard[h, off:off + rows].reshape(w.shape)
        upd[n] = tuple(a.reshape(w.shape) for a in (d, mn, vn))

    w, m, v = given["hg_w_in"]
    grads["hg_w_in"] = g_hg_w_in.reshape(w.shape)
    upd["hg_w_in"] = tuple(a.reshape(w.shape) for a in _adamw(w[0], g_hg_w_in, m[0], v[0], name="adamw_hg_w_in"))

    w, m, v = given["w_mod"]
    flat = lambda a: a.reshape(-1, nmod)
    upd["w_mod"] = tuple(a.reshape(w.shape) for a in _adamw(flat(w), flat(g_w_mod), flat(m), flat(v), name="adamw_w_mod"))

    snames = ["b_mod"] + SMALL_NAMES
    pw, soffs = _pack_small({n: given[n][0] for n in snames}, snames)
    pm, _ = _pack_small({n: given[n][1] for n in snames}, snames)
    pv, _ = _pack_small({n: given[n][2] for n in snames}, snames)
    pg, _ = _pack_small({n: grads[n] for n in snames}, snames)
    pw, pm, pv, pg = (_pad_rows(a, 8) for a in (pw, pm, pv, pg))
    sd, smn, svn = _adamw(pw, pg, pm, pv, name="adamw_small")
    for n in snames:
        like = given[n][0]
        upd[n] = tuple(_unpack_small(a, soffs, n, like) for a in (sd, smn, svn))

    order = ["w_mod", "b_mod", "norm1_w", "norm2_w", "hg_w_in", "hg_w_out", "hg_lb", "hg_gn_w", "fox_w_in", "fox_b_f",
             "fox_qn_w", "fox_kn_w", "fox_w_out", "mlp_w1", "mlp_w2", "final_w"]
    return (loss, grad_x.reshape(x.shape), *[grads[n] for n in order], *[upd[n][0] for n in order],
            *[upd[n][1] for n in order], *[upd[n][2] for n in order])
```

```python
import math

import jax
import jax.numpy as jnp
from jax import lax
from jax.experimental import pallas as pl
from jax.experimental.pallas import tpu as pltpu

EPS = 1e-6
ADAM_LR, ADAM_B1, ADAM_B2, ADAM_EPS, ADAM_WD, ADAM_STEP = 0.001, 0.9, 0.999, 1e-08, 0.01, 10

F32 = jnp.float32
BF = jnp.bfloat16
LANES = 128
HG_CHUNK = 64
HG_HEADS_PER_STEP = 8
HG_TOKENS_PER_STEP = 256
FOX_ROWS_PER_STEP = 2048
FOX_BWD_TILES = (8, 4, 2, 1)
LOG2E = 1.4426950408889634
FOX_DH = 64
N_CHIPS = 4
N_DEV = 8
VMEM_LIMIT = 56 * 1024 * 1024
MESH = pl.DeviceIdType.MESH

NT = (((1,), (1,)), ((), ()))
TN = (((0,), (0,)), ((), ()))


def _pick(n, pref, mult=LANES):
    if n <= pref:
        return n
    t = (pref // mult) * mult
    while t >= mult:
        if n % t == 0:
            return t
        t -= mult
    raise ValueError((n, pref, mult))


def _cp(*sem):
    return pltpu.CompilerParams(dimension_semantics=sem, vmem_limit_bytes=VMEM_LIMIT)


def _dot(a, b):
    return jnp.dot(a, b, preferred_element_type=F32)


def _dg(a, b, dims):
    return lax.dot_general(a, b, dims, preferred_element_type=F32)


def _split3(x):
    hi = x.astype(BF)
    r1 = x - hi.astype(F32)
    mid = r1.astype(BF)
    lo = (r1 - mid.astype(F32)).astype(BF)
    return hi, mid, lo


def _tri_dot(tri, x):
    hi, mid, lo = _split3(x)
    return _dot(tri, hi) + _dot(tri, mid) + _dot(tri, lo)


def _dg3(a, b, dims):
    ah, bh = a.astype(BF), b.astype(BF)
    al, bl = (a - ah.astype(F32)).astype(BF), (b - bh.astype(F32)).astype(BF)
    return _dg(ah, bh, dims) + _dg(ah, bl, dims) + _dg(al, bh, dims)


def _dg1(a, b, dims):
    return _dg(a.astype(BF), b.astype(BF), dims)


NN = (((1,), (0,)), ((), ()))


def _sigmoid(x):
    return jax.nn.sigmoid(x)


def _ln_matmul(x, nw, sc, sh, w, slab=None, *, relu2, name):
    S, D = x.shape
    N = w.shape[1]
    tm, tn = _pick(S, 512, 16), N
    fused = slab is not None

    def body(x_ref, nw_ref, sc_ref, sh_ref, w_ref, *rest):
        if fused:
            s_ref, *outs, out_ref, hs, send_sems, recv_sems = rest
            finish = _gather_behind(s_ref, out_ref, send_sems, recv_sems, pl.program_id(0), S // tm)
        else:
            outs, hs = rest[:-1], rest[-1]
        h_ref = outs[-1]

        @pl.when(pl.program_id(1) == 0)
        def _():
            xv = x_ref[...]
            r = lax.rsqrt(jnp.mean(xv * xv, axis=-1, keepdims=True) + EPS)
            hb = ((xv * r * nw_ref[...]) * (1.0 + sc_ref[...]) + sh_ref[...]).astype(BF)
            hs[...] = hb
            h_ref[...] = hb

        z = _dot(hs[...], w_ref[...])
        if relu2:
            a = jnp.maximum(z, 0.0)
            outs[0][...] = a.astype(BF)
            outs[1][...] = (a * a).astype(BF)
        else:
            outs[0][...] = z
        if fused:
            finish()

    vec = pl.BlockSpec((1, D), lambda i, j: (0, 0))
    tile = pl.BlockSpec((tm, tn), lambda i, j: (i, j))
    if relu2:
        out_shape = [jax.ShapeDtypeStruct((S, N), BF), jax.ShapeDtypeStruct((S, N), BF)]
        out_specs = [tile, tile]
    else:
        out_shape = [jax.ShapeDtypeStruct((S, N), F32)]
        out_specs = [tile]
    out_shape.append(jax.ShapeDtypeStruct((S, D), BF))
    out_specs.append(pl.BlockSpec((tm, D), lambda i, j: (i, 0)))
    in_specs = [pl.BlockSpec((tm, D), lambda i, j: (i, 0)), vec, vec, vec, pl.BlockSpec((D, tn), lambda i, j: (0, j))]
    scratch = [pltpu.VMEM((tm, D), BF)]
    args = [x, nw, sc, sh, w]
    if fused:
        in_specs.append(HBM)
        out_specs.append(HBM)
        out_shape.append(jax.ShapeDtypeStruct((N_CHIPS,) + slab.shape, slab.dtype))
        scratch += [pltpu.SemaphoreType.DMA((6,)), pltpu.SemaphoreType.DMA((6,))]
        args.append(slab)
    return pl.pallas_call(
        body, name=name, grid=(S // tm, N // tn), in_specs=in_specs, out_specs=out_specs, out_shape=out_shape,
        scratch_shapes=scratch, compiler_params=_cp("arbitrary", "arbitrary"),
    )(*args)


def _matmul_resid(a, w, x, gate, *, name):
    S, K = a.shape
    D = w.shape[1]
    tm, tn = _pick(S, 1024 if K <= 1024 else 512, 16), D

    def body(a_ref, w_ref, x_ref, g_ref, o_ref, y_ref):
        y = _dot(a_ref[...], w_ref[...])
        y_ref[...] = y.astype(BF)
        o_ref[...] = x_ref[...] + g_ref[...] * y

    tile = pl.BlockSpec((tm, tn), lambda i, j: (i, j))
    return pl.pallas_call(
        body, name=name, grid=(S // tm, D // tn),
        in_specs=[pl.BlockSpec((tm, K), lambda i, j: (i, 0)), pl.BlockSpec((K, tn), lambda i, j: (0, j)),
                  tile, pl.BlockSpec((1, tn), lambda i, j: (0, j))],
        out_specs=[tile, tile],
        out_shape=[jax.ShapeDtypeStruct((S, D), F32), jax.ShapeDtypeStruct((S, D), BF)],
        compiler_params=_cp("parallel", "arbitrary"),
    )(a, w, x, gate)


def _gate_matmul_nt(dx, gate, y, w, act, *, name):
    S, D = dx.shape
    K = w.shape[0]
    tm, tn = _pick(S, 1024 if K <= 1024 else 512, 16), K
    fused = act is not None

    def body(dx_ref, g_ref, y_ref, w_ref, *rest):
        if fused:
            act_ref, da_ref, dm_ref, dg_ref, ms = rest
        else:
            da_ref, dm_ref, dg_ref, ms = rest
        i, j = pl.program_id(0), pl.program_id(1)

        @pl.when((i == 0) & (j == 0))
        def _():
            dg_ref[...] = jnp.zeros_like(dg_ref)

        @pl.when(j == 0)
        def _():
            dxv = dx_ref[...]
            dmb = (dxv * g_ref[...]).astype(BF)
            ms[...] = dmb
            dm_ref[...] = dmb
            dg_ref[...] += jnp.sum(dxv * y_ref[...].astype(F32), axis=0, keepdims=True)

        da = _dg(ms[...], w_ref[...], NT)
        if fused:
            da_ref[...] = (da * (2.0 * act_ref[...].astype(F32))).astype(BF)
        else:
            da_ref[...] = da

    row = pl.BlockSpec((tm, D), lambda i, j: (i, 0))
    vec = pl.BlockSpec((1, D), lambda i, j: (0, 0))
    tile = pl.BlockSpec((tm, tn), lambda i, j: (i, j))
    in_specs = [row, vec, row, pl.BlockSpec((tn, D), lambda i, j: (j, 0))]
    args = [dx, gate, y, w]
    if fused:
        in_specs.append(tile)
        args.append(act)
    return pl.pallas_call(
        body, name=name, grid=(S // tm, K // tn),
        in_specs=in_specs, out_specs=[tile, row, vec],
        out_shape=[jax.ShapeDtypeStruct((S, K), BF if fused else F32), jax.ShapeDtypeStruct((S, D), BF),
                   jax.ShapeDtypeStruct((1, D), F32)],
        scratch_shapes=[pltpu.VMEM((tm, D), BF)],
        compiler_params=_cp("arbitrary", "arbitrary"),
    )(*args)


def _matmul_tn(a, b, *, name, into=None):
    S, Ka = a.shape
    P, _, Db = b.shape
    tk, tn, ts = _pick(Ka, 1024), _pick(Db, 1024), _pick(S, 1024, 16)
    if into is not None:
        slab, kind, half, off = into
        C = tn = slab.shape[3]
        per_chip = Ka // N_CHIPS
        all_chips = kind == "row" and tk == Ka
        if kind == "row" and not all_chips:
            tk = min(tk, per_chip)
        assert tn == C and P * Db == (N_CHIPS * C if kind == "col" else C)
        if kind == "col":
            assert tk == Ka and off % tk == 0
        elif all_chips:
            assert off % per_chip == 0
        else:
            assert per_chip % tk == 0 and off % tk == 0
    npb = Db // tn

    if into is not None and kind == "col":
        def wide_body(a_ref, b_ref, slab_ref, o_ref):
            s, j = pl.program_id(0), pl.program_id(1)
            d = _dg(a_ref[...], b_ref[...], TN)

            @pl.when(s == 0)
            def _():
                o_ref[j] = d

            @pl.when(s > 0)
            def _():
                o_ref[j] += d

        return pl.pallas_call(
            wide_body, name=name, grid=(S // ts, P * npb),
            in_specs=[pl.BlockSpec((ts, Ka), lambda s, j: (s, 0)),
                      pl.BlockSpec((None, ts, tn), lambda s, j: (j // npb, s, j % npb)),
                      pl.BlockSpec(memory_space=pl.ANY)],
            out_specs=pl.BlockSpec((N_CHIPS, None, Ka, tn), lambda s, j: (0, half, off // Ka, 0)),
            out_shape=jax.ShapeDtypeStruct(slab.shape, F32), input_output_aliases={2: 0},
            compiler_params=_cp("arbitrary", "arbitrary"),
        )(a, b, slab)

    def body(a_ref, b_ref, *rest):
        o_ref, acc = rest[-2:]
        s = pl.program_id(2)

        @pl.when(s == 0)
        def _():
            acc[...] = jnp.zeros_like(acc)

        acc[...] += _dg(a_ref[...], b_ref[...], TN)

        @pl.when(s == pl.num_programs(2) - 1)
        def _():
            o_ref[...] = acc[...].reshape(o_ref.shape)

    in_specs = [pl.BlockSpec((ts, tk), lambda i, j, s: (s, i)),
                pl.BlockSpec((None, ts, tn), lambda i, j, s: (j // npb, s, j % npb))]
    args = [a, b]
    if into is None:
        out_spec = pl.BlockSpec((tk, tn), lambda i, j, s: (i, j))
        out_shape = jax.ShapeDtypeStruct((Ka, P * Db), F32)
        aliases = {}
    else:
        per = per_chip // tk if kind == "row" and not all_chips else 1
        if kind == "col":
            out_spec = pl.BlockSpec((None, None, tk, tn), lambda i, j, s: (j, half, off // tk + i, 0))
        elif all_chips:
            out_spec = pl.BlockSpec((N_CHIPS, None, per_chip, tn), lambda i, j, s: (0, half, off // per_chip, 0))
        else:
            out_spec = pl.BlockSpec((None, None, tk, tn), lambda i, j, s: (i // per, half, off // tk + i % per, 0))
        out_shape = jax.ShapeDtypeStruct(slab.shape, F32)
        in_specs.append(pl.BlockSpec(memory_space=pl.ANY))
        args.append(slab)
        aliases = {2: 0}
    return pl.pallas_call(
        body, name=name, grid=(Ka // tk, P * npb, S // ts),
        in_specs=in_specs, out_specs=out_spec, out_shape=out_shape,
        scratch_shapes=[pltpu.VMEM((tk, tn), F32)], input_output_aliases=aliases,
        compiler_params=_cp("parallel", "parallel", "arbitrary"),
    )(*args)


def _matmul_nt_lnbwd(g, w, x, nw, sc, dx_out, part=None, *, name):
    P, S, Dg = g.shape
    D = x.shape[1]
    tm = _pick(S, 512, 16)
    fused = part is not None

    def body(g_ref, w_ref, x_ref, nw_ref, sc_ref, dxo_ref, *rest):
        if fused:
            p_ref, dx_ref, dsc_ref, dsh_ref, dnw_ref, recv_ref, send_sems, recv_sems = rest
            copies = _scatter_copies(p_ref, recv_ref, send_sems, recv_sems)
        else:
            dx_ref, dsc_ref, dsh_ref, dnw_ref = rest

        @pl.when(pl.program_id(0) == 0)
        def _():
            dsc_ref[...] = jnp.zeros_like(dsc_ref)
            dsh_ref[...] = jnp.zeros_like(dsh_ref)
            dnw_ref[...] = jnp.zeros_like(dnw_ref)
            if fused:
                for cp in copies:
                    cp.start()

        dh = _dg(g_ref[0], w_ref[:, 0:Dg], NT)
        for p in range(1, P):
            dh = dh + _dg(g_ref[p], w_ref[:, p * Dg:(p + 1) * Dg], NT)
        xv = x_ref[...]
        nwv = nw_ref[...]
        r = lax.rsqrt(jnp.mean(xv * xv, axis=-1, keepdims=True) + EPS)
        xr = xv * r
        dn = dh * (1.0 + sc_ref[...])
        dsc_ref[...] += jnp.sum(dh * (xr * nwv), axis=0, keepdims=True)
        dsh_ref[...] += jnp.sum(dh, axis=0, keepdims=True)
        dnw_ref[...] += jnp.sum(dn * xr, axis=0, keepdims=True)
        u = dn * nwv
        dx_ref[...] = dxo_ref[...] + r * (u - xr * jnp.mean(u * xr, axis=-1, keepdims=True))

        if fused:
            @pl.when(pl.program_id(0) == S // tm - 1)
            def _():
                for cp in copies:
                    cp.wait()

    row = pl.BlockSpec((tm, D), lambda i: (i, 0))
    vec = pl.BlockSpec((1, D), lambda i: (0, 0))
    in_specs = [pl.BlockSpec((P, tm, Dg), lambda i: (0, i, 0)), pl.BlockSpec((D, P * Dg), lambda i: (0, 0)), row, vec, vec, row]
    out_specs = [row, vec, vec, vec]
    out_shape = [jax.ShapeDtypeStruct((S, D), F32)] + [jax.ShapeDtypeStruct((1, D), F32)] * 3
    scratch, args = [], [g, w, x, nw, sc, dx_out]
    if fused:
        in_specs.append(HBM)
        out_specs.append(HBM)
        out_shape.append(jax.ShapeDtypeStruct((3,) + part.shape[1:], part.dtype))
        scratch = [pltpu.SemaphoreType.DMA((3,)), pltpu.SemaphoreType.DMA((3,))]
        args.append(part)
    return pl.pallas_call(
        body, name=name, grid=(S // tm,), in_specs=in_specs, out_specs=out_specs, out_shape=out_shape,
        scratch_shapes=scratch, compiler_params=_cp("arbitrary"),
    )(*args)


def _loss_kernel(x, fw, tgt, *, name):
    S, D = x.shape
    tm = _pick(S, 512, 8)

    def body(x_ref, fw_ref, t_ref, l_ref, dx_ref, dfw_ref):
        @pl.when(pl.program_id(0) == 0)
        def _():
            l_ref[...] = jnp.zeros_like(l_ref)
            dfw_ref[...] = jnp.zeros_like(dfw_ref)

        xv = x_ref[...]
        fwv = fw_ref[...]
        r = lax.rsqrt(jnp.mean(xv * xv, axis=-1, keepdims=True) + EPS)
        xr = xv * r
        err = xr * fwv - t_ref[...]
        per_tok = jnp.mean(err * err, axis=-1, keepdims=True)
        l_ref[...] += 0.5 * jnp.sum(per_tok, axis=0, keepdims=True)
        dy = err * (1.0 / D)
        dfw_ref[...] += jnp.sum(dy * xr, axis=0, keepdims=True)
        u = dy * fwv
        dx_ref[...] = r * (u - xr * jnp.mean(u * xr, axis=-1, keepdims=True))

    row = pl.BlockSpec((tm, D), lambda i: (i, 0))
    vec = pl.BlockSpec((1, D), lambda i: (0, 0))
    return pl.pallas_call(
        body, name=name, grid=(S // tm,),
        in_specs=[row, vec, row],
        out_specs=[pl.BlockSpec((1, LANES), lambda i: (0, 0)), row, vec],
        out_shape=[jax.ShapeDtypeStruct((1, LANES), F32), jax.ShapeDtypeStruct((S, D), F32),
                   jax.ShapeDtypeStruct((1, D), F32)],
        compiler_params=_cp("arbitrary"),
    )(x, fw, tgt)


def _hg_lower_bound(lb3):
    mx = jnp.max(lb3, axis=0, keepdims=True)
    e = jnp.exp(lb3 - mx)
    p = e / jnp.sum(e, axis=0, keepdims=True)
    return p[0:1, :], p


def _hg_chunk_common(qr, fz, lbv):
    sq = _sigmoid(qr)
    q = qr * sq
    sig = _sigmoid(fz)
    f = lbv + (1.0 - lbv) * sig
    k = (1.0 - lbv) * (1.0 - sig)
    return q, sq, sig, f, k, jnp.log(f)


def _row_of(x, rows, r):
    return jnp.sum(jnp.where(rows == r, x, 0.0), axis=0, keepdims=True)


def _hg_fwd(proj, hg_lb, gn, slab=None, *, name):
    S = proj.shape[0]
    D = proj.shape[1] // 4
    H = D // LANES
    HB = min(HG_HEADS_PER_STEP, H)
    W = HB * LANES
    C = HG_CHUNK
    T = _pick(S, HG_TOKENS_PER_STEP, C)
    nch, nb = T // C, S // T
    ng = H // HB
    fused = slab is not None

    def body(q_ref, fz_ref, v_ref, g_ref, lb_ref, gn_ref, *rest):
        if fused:
            s_ref, y_ref, o_ref, sts_ref, out_ref, st, send_sems, recv_sems = rest
            finish = _gather_behind(s_ref, out_ref, send_sems, recv_sems,
                                    pl.program_id(0) * nb + pl.program_id(1), ng * nb)
        else:
            y_ref, o_ref, sts_ref, st = rest

        @pl.when(pl.program_id(1) == 0)
        def _():
            st[...] = jnp.zeros_like(st)

        lb_all, _ = _hg_lower_bound(lb_ref[...])
        gnv = gn_ref[...]
        ri = lax.broadcasted_iota(jnp.int32, (C, C), 0)
        ci_ = lax.broadcasted_iota(jnp.int32, (C, C), 1)
        low = ri >= ci_
        tri = jnp.where(low, 1.0, 0.0).astype(BF)
        rows_w = lax.broadcasted_iota(jnp.int32, (C, W), 0)

        def chunk(ci, carry):
            sl = pl.ds(pl.multiple_of(ci * C, C), C)
            heads = [slice(hh * LANES, (hh + 1) * LANES) for hh in range(HB)]
            q, _, _, _, k, logf = _hg_chunk_common(q_ref[sl, :], fz_ref[sl, :], lb_all)
            vv, gg = v_ref[sl, :], g_ref[sl, :]
            G = _tri_dot(tri, logf)
            Gm = _row_of(G, rows_w, C // 2 - 1)
            Gl = _row_of(G, rows_w, C - 1)
            qt, kt = q * jnp.exp(G - Gm), k * jnp.exp(Gm - G)
            qe, kd, eGl = q * jnp.exp(G), k * jnp.exp(Gl - G), jnp.exp(Gl)
            A = [jnp.where(low, _dg1(qt[:, ls], kt[:, ls], NT), 0.0) for ls in heads]
            Sv = [st[hh] for hh in range(HB)]
            for hh in range(HB):
                sts_ref[hh, ci] = Sv[hh]
            o = [_dg1(A[hh], vv[:, ls], NN) + _dg1(qe[:, ls], Sv[hh], NT) for hh, ls in enumerate(heads)]
            for hh, ls in enumerate(heads):
                st[hh] = Sv[hh] * eGl[:, ls] + _dg1(vv[:, ls], kd[:, ls], TN)
            gate = gg * _sigmoid(gg)
            for hh, ls in enumerate(heads):
                r = lax.rsqrt(jnp.mean(o[hh] * o[hh], axis=-1, keepdims=True) + EPS)
                y_ref[sl, ls] = ((o[hh] * r * gnv) * gate[:, ls]).astype(BF)
                o_ref[sl, ls] = o[hh]
            return carry

        lax.fori_loop(0, nch, chunk, 0)

        if fused:
            finish()

    def part(p):
        return pl.BlockSpec((T, W), lambda h, n: (n, p * ng + h))

    blk = pl.BlockSpec((T, W), lambda h, n: (n, h))
    in_specs = [part(0), part(1), part(2), part(3),
                pl.BlockSpec((3, W), lambda h, n: (0, h)), pl.BlockSpec((1, LANES), lambda h, n: (0, 0))]
    out_specs = [blk, blk, pl.BlockSpec((HB, nch, LANES, LANES), lambda h, n: (h, n, 0, 0))]
    out_shape = [jax.ShapeDtypeStruct((S, D), BF), jax.ShapeDtypeStruct((S, D), F32),
                 jax.ShapeDtypeStruct((H, S // C, LANES, LANES), F32)]
    scratch = [pltpu.VMEM((HB, LANES, LANES), F32)]
    args = [proj, proj, proj, proj, hg_lb, gn]
    if fused:
        in_specs.append(HBM)
        out_specs.append(HBM)
        out_shape.append(jax.ShapeDtypeStruct((N_CHIPS,) + slab.shape, slab.dtype))
        scratch += [pltpu.SemaphoreType.DMA((6,)), pltpu.SemaphoreType.DMA((6,))]
        args.append(slab)
    return pl.pallas_call(
        body, name=name, grid=(ng, nb), in_specs=in_specs, out_specs=out_specs, out_shape=out_shape,
        scratch_shapes=scratch, compiler_params=_cp("arbitrary", "arbitrary"),
    )(*args)


def _hg_bwd(proj, hg_lb, gn, o_all, states, dy, part=None, *, name):
    S = proj.shape[0]
    D = proj.shape[1] // 4
    H = D // LANES
    HB = min(HG_HEADS_PER_STEP, H)
    W = HB * LANES
    C = HG_CHUNK
    T = _pick(S, HG_TOKENS_PER_STEP, C)
    nch, nb = T // C, S // T
    ng = H // HB
    fused = part is not None

    def body(q_ref, fz_ref, v_ref, g_ref, lb_ref, gn_ref, o_ref, sts_ref, dy_ref, *rest):
        if fused:
            p_ref, dp_ref, dlb_ref, dgn_ref, recv_ref, dst, dlb_acc, send_sems, recv_sems = rest
            copies = _scatter_copies(p_ref, recv_ref, send_sems, recv_sems)

            @pl.when((pl.program_id(0) == 0) & (pl.program_id(1) == 0))
            def _():
                for cp in copies:
                    cp.start()
        else:
            dp_ref, dlb_ref, dgn_ref, dst, dlb_acc = rest
        n = pl.program_id(1)

        @pl.when(n == 0)
        def _():
            dst[...] = jnp.zeros_like(dst)
            dlb_acc[...] = jnp.zeros_like(dlb_acc)
            dgn_ref[...] = jnp.zeros_like(dgn_ref)

        lb_all, p3 = _hg_lower_bound(lb_ref[...])
        gnv = gn_ref[...]
        ri = lax.broadcasted_iota(jnp.int32, (C, C), 0)
        ci_ = lax.broadcasted_iota(jnp.int32, (C, C), 1)
        low = ri >= ci_
        tri = jnp.where(low, 1.0, 0.0).astype(BF)
        triu = jnp.where(ri <= ci_, 1.0, 0.0).astype(BF)
        rows_w = lax.broadcasted_iota(jnp.int32, (C, W), 0)
        gnw = jnp.tile(gnv, (1, HB))

        def chunk(cj, carry):
            ci = nch - 1 - cj
            sl = pl.ds(pl.multiple_of(ci * C, C), C)
            heads = list(enumerate(slice(hh * LANES, (hh + 1) * LANES) for hh in range(HB)))
            wide = lambda parts: jnp.concatenate(parts, axis=1)
            qr, vv, gg = q_ref[sl, :], v_ref[sl, :], g_ref[sl, :]
            q, sq, sig, f, k, logf = _hg_chunk_common(qr, fz_ref[sl, :], lb_all)
            G = _tri_dot(tri, logf)
            Gm = _row_of(G, rows_w, C // 2 - 1)
            Gl = _row_of(G, rows_w, C - 1)
            eG, e_qm, e_km, e_lk, eGl = jnp.exp(G), jnp.exp(G - Gm), jnp.exp(Gm - G), jnp.exp(Gl - G), jnp.exp(Gl)
            qt, kt, kdec, qe = q * e_qm, k * e_km, k * e_lk, q * eG
            sg = _sigmoid(gg)
            d_onw = dy_ref[sl, :] * (gg * sg)
            u = d_onw * gnw
            o = o_ref[sl, :]
            on, do = [], []
            for hh, ls in heads:
                r = lax.rsqrt(jnp.mean(o[:, ls] * o[:, ls], axis=-1, keepdims=True) + EPS)
                on.append(o[:, ls] * r)
                dgn_ref[hh] += jnp.sum(d_onw[:, ls] * on[hh], axis=0, keepdims=True)
                do.append(r * (u[:, ls] - on[hh] * jnp.mean(u[:, ls] * on[hh], axis=-1, keepdims=True)))
            dgg = dy_ref[sl, :] * (wide(on) * gnw) * (sg * (1.0 + gg * (1.0 - sg)))
            Sv = [sts_ref[hh, ci] for hh, _ in heads]
            dSv = [dst[hh] for hh, _ in heads]
            A = [jnp.where(low, _dg1(qt[:, ls], kt[:, ls], NT), 0.0) for _, ls in heads]
            dA = [jnp.where(low, _dg3(do[hh], vv[:, ls], NT), 0.0) for hh, ls in heads]
            dv = wide([_dg1(A[hh], do[hh], TN) + _dg1(kdec[:, ls], dSv[hh], NT) for hh, ls in heads])
            dq = wide([_dg3(dA[hh], kt[:, ls], NN) for hh, ls in heads]) * e_qm \
                + eG * wide([_dg3(do[hh], Sv[hh], NN) for hh, _ in heads])
            dk = wide([_dg3(dA[hh], qt[:, ls], TN) for hh, ls in heads]) * e_km \
                + e_lk * wide([_dg3(vv[:, ls], dSv[hh], NN) for hh, ls in heads])
            s_end = [Sv[hh] * eGl[:, ls] + _dg3(vv[:, ls], kdec[:, ls], TN) for hh, ls in heads]
            dgl = wide([jnp.sum(dSv[hh] * s_end[hh], axis=0, keepdims=True) for hh, _ in heads])
            for hh, ls in heads:
                dst[hh] = dSv[hh] * eGl[:, ls] + _dg1(do[hh], qe[:, ls], TN)
            dG = q * dq - k * dk + jnp.where(rows_w == C - 1, dgl, 0.0)
            dlogf = _tri_dot(triu, dG) - f * dk
            dlf_f = dlogf / f
            dlb_acc[...] += jnp.sum(dlf_f * (1.0 - sig), axis=0, keepdims=True)
            dp_ref[0, sl, :] = (dq * (sq * (1.0 + qr * (1.0 - sq)))).astype(BF)
            dp_ref[1, sl, :] = (dlf_f * (1.0 - lb_all) * sig * (1.0 - sig)).astype(BF)
            dp_ref[2, sl, :] = dv.astype(BF)
            dp_ref[3, sl, :] = dgg.astype(BF)
            return carry

        lax.fori_loop(0, nch, chunk, 0)
        sel = jnp.where(lax.broadcasted_iota(jnp.int32, (3, W), 0) == 0, 1.0, 0.0)
        dlb_ref[...] = lb_all * (sel - p3) * dlb_acc[...]

        if fused:
            @pl.when((pl.program_id(0) == ng - 1) & (n == nb - 1))
            def _():
                for cp in copies:
                    cp.wait()

    def col(p):
        return pl.BlockSpec((T, W), lambda h, n: (nb - 1 - n, p * ng + h))

    blk = pl.BlockSpec((T, W), lambda h, n: (nb - 1 - n, h))
    in_specs = [col(0), col(1), col(2), col(3),
                pl.BlockSpec((3, W), lambda h, n: (0, h)), pl.BlockSpec((1, LANES), lambda h, n: (0, 0)),
                blk, pl.BlockSpec((HB, nch, LANES, LANES), lambda h, n: (h, nb - 1 - n, 0, 0)), blk]
    out_specs = [pl.BlockSpec((4, T, W), lambda h, n: (0, nb - 1 - n, h)),
                 pl.BlockSpec((3, W), lambda h, n: (0, h)),
                 pl.BlockSpec((HB, 1, LANES), lambda h, n: (h, 0, 0))]
    out_shape = [jax.ShapeDtypeStruct((4, S, D), BF), jax.ShapeDtypeStruct((3, D), F32),
                 jax.ShapeDtypeStruct((H, 1, LANES), F32)]
    scratch = [pltpu.VMEM((HB, LANES, LANES), F32), pltpu.VMEM((1, W), F32)]
    args = [proj, proj, proj, proj, hg_lb, gn, o_all, states, dy]
    if fused:
        in_specs.append(HBM)
        out_specs.append(HBM)
        out_shape.append(jax.ShapeDtypeStruct((3,) + part.shape[1:], part.dtype))
        scratch += [pltpu.SemaphoreType.DMA((3,)), pltpu.SemaphoreType.DMA((3,))]
        args.append(part)
    return pl.pallas_call(
        body, name=name, grid=(ng, nb), in_specs=in_specs, out_specs=out_specs, out_shape=out_shape,
        scratch_shapes=scratch, compiler_params=_cp("arbitrary", "arbitrary"),
    )(*args)


def _log_sigmoid(u):
    return jnp.minimum(u, 0.0) - jnp.log(1.0 + jnp.exp(-jnp.abs(u)))


def _lane_put(base, lane, first, pieces):
    for n, p in enumerate(pieces):
        base = jnp.where(lane == first + n, p, base)
    return base


def _fox_cumsum(proj, bf_pad, *, name):
    S = proj.shape[0]
    D = proj.shape[1] // 5
    T = _pick(S, 256, 8)

    def body(fz_ref, b_ref, f_ref, carry):
        @pl.when(pl.program_id(0) == 0)
        def _():
            carry[...] = jnp.zeros_like(carry)

        logf = _log_sigmoid(fz_ref[...] + b_ref[...])
        tri = jnp.where(lax.broadcasted_iota(jnp.int32, (T, T), 0) >= lax.broadcasted_iota(jnp.int32, (T, T), 1),
                        1.0, 0.0).astype(BF)
        fv = _tri_dot(tri, logf) + carry[...]
        f_ref[...] = fv
        carry[...] = _row_of(fv, lax.broadcasted_iota(jnp.int32, (T, LANES), 0), T - 1)

    return pl.pallas_call(
        body, name=name, grid=(S // T,),
        in_specs=[pl.BlockSpec((T, LANES), lambda i: (i, 4 * D // LANES)), pl.BlockSpec((1, LANES), lambda i: (0, 0))],
        out_specs=pl.BlockSpec((T, LANES), lambda i: (i, 0)),
        out_shape=jax.ShapeDtypeStruct((S, LANES), F32),
        scratch_shapes=[pltpu.VMEM((1, LANES), F32)],
        compiler_params=_cp("arbitrary"),
    )(proj, bf_pad)


def _pair_stats(sq, lo):
    del lo
    a = lax.broadcasted_iota(jnp.int32, (LANES, LANES), 0) < FOX_DH
    b = lax.broadcasted_iota(jnp.int32, (LANES, LANES), 1) < FOX_DH
    avg = jnp.where(a == b, 1.0 / FOX_DH, 0.0).astype(BF)
    hi, mid, low = _split3(sq)
    return _dot(hi, avg) + _dot(mid, avg) + _dot(low, avg)


def _fox_prep(proj, fcum, qw2, kw2, *, name):
    S = proj.shape[0]
    D = proj.shape[1] // 5
    HP = D // LANES
    T = _pick(S, FOX_ROWS_PER_STEP, 16)

    def body(q_ref, k_ref, v_ref, f_ref, qw_ref, kw_ref, qa_ref, ka_ref, va_ref, vt_ref):
        hp = pl.program_id(1)
        lane = lax.broadcasted_iota(jnp.int32, (T, LANES), 1)
        lo = lane < FOX_DH
        qv, kv, vv, fv = q_ref[...], k_ref[...], v_ref[...], f_ref[...]
        qn = qv * lax.rsqrt(_pair_stats(qv * qv, lo) + EPS) * qw_ref[...] * (0.125 * LOG2E)
        kn = kv * lax.rsqrt(_pair_stats(kv * kv, lo) + EPS) * kw_ref[...]
        ones_q = jnp.where((lane >= 67) & (lane <= 69), 1.0, 0.0)
        ones_k = jnp.where(((lane >= 64) & (lane <= 66)) | ((lane >= 70) & (lane <= 72)), 1.0, 0.0)
        ones_v = jnp.where((lane >= 64) & (lane <= 66), 1.0, 0.0)
        for hh in range(2):
            fh = jnp.sum(jnp.where(lane == 2 * hp + hh, fv, 0.0), axis=-1, keepdims=True) * LOG2E
            pieces = [p.astype(F32) for p in _split3(fh)]

            def half(x):
                return jnp.where(lo, x if hh == 0 else pltpu.roll(x, FOX_DH, 1), 0.0)

            qa_ref[hh] = _lane_put(half(qn) + ones_q, lane, 64, pieces).astype(BF)
            ka_ref[hh] = _lane_put(half(kn) + ones_k, lane, 67, [-p for p in pieces]).astype(BF)
            va = half(vv) + ones_v
            va_ref[hh] = va.astype(BF)
            vt_ref[hh] = va.T.astype(BF)

    def part(p):
        return pl.BlockSpec((T, LANES), lambda i, hp: (i, p * HP + hp))

    vec = pl.BlockSpec((1, LANES), lambda i, hp: (0, 0))
    aug = pl.BlockSpec((2, T, LANES), lambda i, hp: (hp, i, 0))
    return pl.pallas_call(
        body, name=name, grid=(S // T, HP),
        in_specs=[part(0), part(1), part(2), pl.BlockSpec((T, LANES), lambda i, hp: (i, 0)), vec, vec],
        out_specs=[aug, aug, aug, pl.BlockSpec((2, LANES, T), lambda i, hp: (hp, 0, i))],
        out_shape=[jax.ShapeDtypeStruct((2 * HP, S, LANES), BF)] * 3 + [jax.ShapeDtypeStruct((2 * HP, LANES, S), BF)],
        compiler_params=_cp("parallel", "arbitrary"),
    )(proj, proj, proj, fcum, qw2, kw2)


def _fox_block(S):
    return _pick(S, 256, 16)


def _fox_skip_bounds(fcum, qn_w, kn_w, nheads):
    S = fcum.shape[0]
    B = _fox_block(S)
    qk = 8.0 * LOG2E * 1.02 * jnp.max(jnp.abs(qn_w)) * jnp.max(jnp.abs(kn_w))
    thresh = -(2.0 * qk + 152.0)
    f2 = fcum[:, :nheads] * LOG2E
    first, last = f2[0::B], f2[B - 1::B]
    nb = S // B
    blk = jnp.arange(nb)
    dead = (first[0::2, None, :] - last[None, :, :]) < thresh
    jmin = jnp.sum(dead & (blk[None, :, None] < 2 * jnp.arange(nb // 2)[:, None, None]), axis=1)
    live = (first[:, None, :] - last[None, :, :]) >= thresh
    imax = blk[:, None] + jnp.sum(live & (blk[:, None, None] > blk[None, :, None]), axis=0)
    return jmin.T.astype(jnp.int32), imax.T.astype(jnp.int32)


def _fox_fwd(jmin, qa, ka, vat, proj, *, name):
    H, S, _ = qa.shape
    HP = H // 2
    D = HP * LANES
    B = _fox_block(S)
    BQ = 2 * B
    nq = S // BQ

    def body(jmin_ref, q_ref, k_ref, vt_ref, g_ref, y_ref, o_ref, q2_ref):
        hp, i = pl.program_id(0), pl.program_id(1)
        lane = lax.broadcasted_iota(jnp.int32, (BQ, LANES), 1)
        lo = lane < FOX_DH
        in_stat = (lane >= 70) & (lane <= 75)
        causal = lax.broadcasted_iota(jnp.int32, (BQ, BQ), 0) <= lax.broadcasted_iota(jnp.int32, (BQ, BQ), 1)
        row = lax.broadcasted_iota(jnp.int32, (LANES, BQ), 0)
        m0, acc0 = jnp.full((1, BQ), -jnp.inf, F32), jnp.zeros((LANES, BQ), F32)
        outs = []
        for hh in range(2):
            qb = q_ref[hh]

            def scores(j):
                sl = pl.ds(pl.multiple_of(j * BQ, BQ), BQ)
                return _dg(k_ref[hh, sl, :], qb, NT)

            def update(j, m, acc, st, masked=False):
                sl = pl.ds(pl.multiple_of(j * BQ, BQ), BQ)
                if masked:
                    st = jnp.where(causal, st, -jnp.inf)
                m_new = jnp.maximum(m, jnp.ceil(jnp.max(st, axis=0, keepdims=True)))
                p = jnp.exp2(st - m_new).astype(BF)
                return m_new, acc * jnp.exp2(m - m_new) + _dot(vt_ref[hh, :, sl], p)

            def step(j, carry):
                m, acc, st = carry
                st_next = scores(j + 1)
                return update(j, m, acc, st) + (st_next,)

            first = jmin_ref[2 * hp + hh, i] // 2
            m, acc, st = lax.fori_loop(first, i, step, (m0, acc0, scores(first)))
            m, acc = update(i, m, acc, st, masked=True)
            linv = 1.0 / jnp.sum(jnp.where(row == FOX_DH, acc, 0.0), axis=0, keepdims=True)
            tile = acc * linv
            for n, piece in enumerate(_split3(m) + _split3(linv)):
                tile = jnp.where(row == 70 + n, piece.astype(F32), tile)
            tile = tile.T
            outs.append(tile)
            q2_ref[hh] = jnp.where(in_stat, jnp.where(lane <= 72, -tile, tile), qb.astype(F32)).astype(BF)
        o = jnp.where(lo, outs[0], pltpu.roll(outs[1], FOX_DH, 1))
        o_ref[...] = o
        y_ref[...] = (o * _sigmoid(g_ref[...])).astype(BF)

    blk = pl.BlockSpec((BQ, LANES), lambda hp, i, jm: (i, hp))
    qblk = pl.BlockSpec((2, BQ, LANES), lambda hp, i, jm: (hp, i, 0))
    full = pl.BlockSpec((2, S, LANES), lambda hp, i, jm: (hp, 0, 0))
    full_t = pl.BlockSpec((2, LANES, S), lambda hp, i, jm: (hp, 0, 0))
    return pl.pallas_call(
        body, name=name,
        grid_spec=pltpu.PrefetchScalarGridSpec(
            num_scalar_prefetch=1, grid=(HP, nq),
            in_specs=[qblk, full, full_t, pl.BlockSpec((BQ, LANES), lambda hp, i, jm: (i, 3 * HP + hp))],
            out_specs=[blk, blk, qblk]),
        out_shape=[jax.ShapeDtypeStruct((S, D), BF), jax.ShapeDtypeStruct((S, D), F32),
                   jax.ShapeDtypeStruct((H, S, LANES), BF)],
        compiler_params=_cp("parallel", "arbitrary"),
    )(jmin, qa, ka, vat, proj)


def _fox_bwd_prep(dy, o, proj, q2, *, name):
    S, D = dy.shape
    HP = D // LANES
    T = _pick(S, FOX_ROWS_PER_STEP, 16)

    def body(dy_ref, o_ref, g_ref, q2_ref, da_ref):
        lane = lax.broadcasted_iota(jnp.int32, (T, LANES), 1)
        lo = lane < FOX_DH
        in_linv = (lane >= 73) & (lane <= 75)
        linv = [jnp.sum(jnp.where(in_linv, q2_ref[hh].astype(F32), 0.0), axis=-1, keepdims=True) for hh in range(2)]
        u = (dy_ref[...] * _sigmoid(g_ref[...]) * jnp.where(lo, linv[0], linv[1])).astype(BF).astype(F32)
        prod = u * o_ref[...]
        d_lo = jnp.sum(jnp.where(lo, prod, 0.0), axis=-1, keepdims=True)
        d_hi = jnp.sum(jnp.where(lo, 0.0, prod), axis=-1, keepdims=True)
        for hh, delta in enumerate((d_lo, d_hi)):
            base = jnp.where(lo, u if hh == 0 else pltpu.roll(u, FOX_DH, 1), 0.0)
            da_ref[hh] = _lane_put(base, lane, 64, [-(p.astype(F32)) for p in _split3(delta)]).astype(BF)

    blk = pl.BlockSpec((T, LANES), lambda i, hp: (i, hp))
    aug = pl.BlockSpec((2, T, LANES), lambda i, hp: (hp, i, 0))
    return pl.pallas_call(
        body, name=name, grid=(S // T, HP),
        in_specs=[blk, blk, pl.BlockSpec((T, LANES), lambda i, hp: (i, 3 * HP + hp)), aug],
        out_specs=aug,
        out_shape=jax.ShapeDtypeStruct((2 * HP, S, LANES), BF),
        compiler_params=_cp("parallel", "arbitrary"),
    )(dy, o, proj, q2)


def _fox_bwd(imax, q2, ka, va, doa, *, name):
    H, S, _ = q2.shape
    B = _fox_block(S)
    nb = S // B

    def body(imax_ref, q_ref, do_ref, k_ref, v_ref, dq_ref, dk_ref, dv_ref, cs_ref):
        j = pl.program_id(1)
        end = imax_ref[pl.program_id(0), j] + 1

        @pl.when(j == 0)
        def _():
            dq_ref[...] = jnp.zeros_like(dq_ref)

        kb, vb = k_ref[...], v_ref[...]

        def step(i, carry, nblk=1):
            dk_acc, dv_acc, cs_acc = carry
            rows = nblk * B
            sl = pl.ds(pl.multiple_of(i * B, B), rows)
            qb, dob = q_ref[sl, :], do_ref[sl, :]
            s = _dg(qb, kb, NT)
            ahead = lax.broadcasted_iota(jnp.int32, (rows, B), 0) - lax.broadcasted_iota(jnp.int32, (rows, B), 1)
            pb = jnp.exp2(jnp.where(ahead >= (j - i) * B, s, -jnp.inf)).astype(BF)
            ds = pb.astype(F32) * _dg(dob, vb, NT)
            dsb = ds.astype(BF)
            cs_acc = cs_acc + jnp.sum(ds.reshape(rows // 8, 8, B), axis=0)
            dv_acc = dv_acc + _dg(pb, dob, TN)
            dk_acc = dk_acc + _dg(dsb, qb, TN)
            dq_ref[sl, :] += _dot(dsb, kb)
            return dk_acc, dv_acc, cs_acc

        zero = jnp.zeros((B, LANES), F32)
        carry = (zero, zero, jnp.zeros((8, B), F32))
        pos = j
        for U in FOX_BWD_TILES:
            n = (end - pos) // U
            carry = lax.fori_loop(0, n, lambda ii, c, pos=pos, U=U: step(pos + U * ii, c, nblk=U), carry)
            pos = pos + U * n
        dk_acc, dv_acc, cs_acc = carry
        dk_ref[...] = dk_acc
        dv_ref[...] = dv_acc
        cs_ref[...] = jnp.sum(cs_acc, axis=0, keepdims=True)

    full = pl.BlockSpec((None, S, LANES), lambda h, j, im: (h, 0, 0))
    blk = pl.BlockSpec((None, B, LANES), lambda h, j, im: (h, j, 0))
    return pl.pallas_call(
        body, name=name,
        grid_spec=pltpu.PrefetchScalarGridSpec(
            num_scalar_prefetch=1, grid=(H, nb),
            in_specs=[full, full, blk, blk],
            out_specs=[full, blk, blk, pl.BlockSpec((None, 1, B), lambda h, j, im: (h, 0, j))]),
        out_shape=[jax.ShapeDtypeStruct((H, S, LANES), F32)] * 3 + [jax.ShapeDtypeStruct((H, 1, S), F32)],
        compiler_params=_cp("parallel", "arbitrary"),
    )(imax, q2, doa, ka, va)


def _fox_bwd_post(dqa, dka, dva, proj, dy, o, qw2, kw2, *, name):
    S, D = dy.shape
    HP = D // LANES
    T = _pick(S, FOX_ROWS_PER_STEP, 16)

    def body(dq_ref, dk_ref, dv_ref, q_ref, k_ref, g_ref, dy_ref, o_ref, qw_ref, kw_ref, dp_ref, dqw_ref, dkw_ref):
        @pl.when((pl.program_id(0) == 0) & (pl.program_id(1) == 0))
        def _():
            dqw_ref[...] = jnp.zeros_like(dqw_ref)
            dkw_ref[...] = jnp.zeros_like(dkw_ref)

        lane = lax.broadcasted_iota(jnp.int32, (T, LANES), 1)
        lo = lane < FOX_DH

        def pair(ref):
            return jnp.where(lo, ref[0], pltpu.roll(ref[1], FOX_DH, 1))

        def norm_bwd(xv, w, dyn, dw_ref):
            r = lax.rsqrt(_pair_stats(xv * xv, lo) + EPS)
            xr = xv * r
            dw_ref[...] += jnp.sum(dyn * xr, axis=0, keepdims=True)
            u = dyn * w
            return r * (u - xr * _pair_stats(u * xr, lo))

        dp_ref[0] = norm_bwd(q_ref[...], qw_ref[...], pair(dq_ref) * 0.125, dqw_ref).astype(BF)
        dp_ref[1] = norm_bwd(k_ref[...], kw_ref[...], pair(dk_ref) * (1.0 / LOG2E), dkw_ref).astype(BF)
        dp_ref[2] = pair(dv_ref).astype(BF)
        sg = _sigmoid(g_ref[...])
        dp_ref[3] = (dy_ref[...] * o_ref[...] * sg * (1.0 - sg)).astype(BF)

    def part(p):
        return pl.BlockSpec((T, LANES), lambda i, hp: (i, p * HP + hp))

    aug = pl.BlockSpec((2, T, LANES), lambda i, hp: (hp, i, 0))
    blk = pl.BlockSpec((T, LANES), lambda i, hp: (i, hp))
    vec = pl.BlockSpec((1, LANES), lambda i, hp: (0, 0))
    return pl.pallas_call(
        body, name=name, grid=(S // T, HP),
        in_specs=[aug, aug, aug, part(0), part(1), part(3), blk, blk, vec, vec],
        out_specs=[pl.BlockSpec((4, T, LANES), lambda i, hp: (0, i, hp)), vec, vec],
        out_shape=[jax.ShapeDtypeStruct((5, S, D), BF), jax.ShapeDtypeStruct((1, LANES), F32),
                   jax.ShapeDtypeStruct((1, LANES), F32)],
        compiler_params=_cp("arbitrary", "arbitrary"),
    )(dqa, dka, dva, proj, proj, proj, dy, o, qw2, kw2)


def _fox_dfz(colsum, nheads, proj, bf_pad, dproj, *, name):
    S = colsum.shape[0]
    H = nheads
    D = dproj.shape[2]
    T = _pick(S, 256, 16)
    nb = S // T

    def body(cs_ref, fz_ref, b_ref, _, dp_ref, db_ref, carry):
        @pl.when(pl.program_id(0) == 0)
        def _():
            carry[...] = jnp.zeros_like(carry)
            db_ref[...] = jnp.zeros_like(db_ref)

        lane = lax.broadcasted_iota(jnp.int32, (T, LANES), 1)
        df = -cs_ref[...]
        triu = jnp.where(lax.broadcasted_iota(jnp.int32, (T, T), 0) <= lax.broadcasted_iota(jnp.int32, (T, T), 1),
                         1.0, 0.0).astype(BF)
        dlogf = _tri_dot(triu, df) + carry[...]
        carry[...] = _row_of(dlogf, lax.broadcasted_iota(jnp.int32, (T, LANES), 0), 0)
        dfz = jnp.where(lane < H, dlogf * _sigmoid(-(fz_ref[...] + b_ref[...])), 0.0)
        db_ref[...] += jnp.sum(dfz, axis=0, keepdims=True)
        dp_ref[...] = jnp.zeros_like(dp_ref)
        dp_ref[:, 0:LANES] = dfz.astype(BF)

    return pl.pallas_call(
        body, name=name, grid=(nb,),
        in_specs=[pl.BlockSpec((T, LANES), lambda i: (nb - 1 - i, 0)),
                  pl.BlockSpec((T, LANES), lambda i: (nb - 1 - i, 4 * D // LANES)),
                  pl.BlockSpec((1, LANES), lambda i: (0, 0)),
                  pl.BlockSpec(memory_space=pl.ANY)],
        out_specs=[pl.BlockSpec((None, T, D), lambda i: (4, nb - 1 - i, 0)), pl.BlockSpec((1, LANES), lambda i: (0, 0))],
        out_shape=[jax.ShapeDtypeStruct(dproj.shape, BF), jax.ShapeDtypeStruct((1, LANES), F32)],
        scratch_shapes=[pltpu.VMEM((1, LANES), F32)],
        input_output_aliases={3: 0},
        compiler_params=_cp("arbitrary"),
    )(colsum, proj, bf_pad, dproj)


def _mod_fwd(c16, w, b, *, name):
    L, D, N = w.shape
    tn = _pick(N, 512)

    def body(c_ref, w_ref, b_ref, o_ref):
        cv = c_ref[...]
        ca = (cv * _sigmoid(cv)).astype(BF)
        o_ref[...] = _dot(ca, w_ref[...].astype(BF)) + b_ref[...]

    return pl.pallas_call(
        body, name=name, grid=(L, N // tn),
        in_specs=[pl.BlockSpec((16, D), lambda l, j: (0, 0)), pl.BlockSpec((None, D, tn), lambda l, j: (l, 0, j)),
                  pl.BlockSpec((None, 1, tn), lambda l, j: (l, 0, j))],
        out_specs=pl.BlockSpec((None, 16, tn), lambda l, j: (l, 0, j)),
        out_shape=jax.ShapeDtypeStruct((L, 16, N), F32),
        compiler_params=_cp("parallel", "arbitrary"),
    )(c16, w, b)


def _mod_bwd(c16, dmod, *, name):
    L, _, N = dmod.shape
    D = c16.shape[1]
    tn = _pick(N, 512)

    def body(c_ref, d_ref, o_ref):
        cv = c_ref[...]
        ca = (cv * _sigmoid(cv)).astype(BF)
        o_ref[...] = _dg(ca, d_ref[...].astype(BF), TN)

    return pl.pallas_call(
        body, name=name, grid=(L, N // tn),
        in_specs=[pl.BlockSpec((16, D), lambda l, j: (0, 0)), pl.BlockSpec((None, 16, tn), lambda l, j: (l, 0, j))],
        out_specs=pl.BlockSpec((None, D, tn), lambda l, j: (l, 0, j)),
        out_shape=jax.ShapeDtypeStruct((L, D, N), F32),
        compiler_params=_cp("parallel", "arbitrary"),
    )(c16, dmod)


def _adamw_math(w, g, m, v):
    m = ADAM_B1 * m + (1.0 - ADAM_B1) * g
    v = ADAM_B2 * v + (1.0 - ADAM_B2) * (g * g)
    m_hat = m / (1.0 - ADAM_B1 ** ADAM_STEP)
    v_hat = v / (1.0 - ADAM_B2 ** ADAM_STEP)
    return -ADAM_LR * (m_hat / (jnp.sqrt(v_hat) + ADAM_EPS) + ADAM_WD * w), m, v


def _adamw(w, g, m, v, *, g_at=None, name):
    R, C = w.shape
    row0 = 0 if g_at is None else g_at[1]
    tr = min(math.gcd(row0, 256) if row0 else 256, -(-R // 8) * 8)
    g0 = row0 // tr
    if g_at is None:
        g_spec = pl.BlockSpec((tr, C), lambda i: (i, 0))
    else:
        g_spec = pl.BlockSpec((None, tr, C), lambda i: (g_at[0], g0 + i, 0))

    def body(w_ref, g_ref, m_ref, v_ref, d_ref, mo_ref, vo_ref):
        d, mn, vn = _adamw_math(w_ref[...], g_ref[...], m_ref[...], v_ref[...])
        d_ref[...] = d
        mo_ref[...] = mn
        vo_ref[...] = vn

    blk = pl.BlockSpec((tr, C), lambda i: (i, 0))
    return pl.pallas_call(
        body, name=name, grid=(pl.cdiv(R, tr),),
        in_specs=[blk, g_spec, blk, blk],
        out_specs=[blk, blk, blk],
        out_shape=[jax.ShapeDtypeStruct((R, C), F32)] * 3,
        compiler_params=_cp("parallel"),
    )(w, g, m, v)


def _sum_parts(parts, *, name):
    P, R, C = parts.shape

    def body(p_ref, o_ref):
        acc = p_ref[0]
        for p in range(1, P):
            acc = acc + p_ref[p]
        o_ref[...] = acc

    return pl.pallas_call(
        body, name=name, grid=(1,),
        in_specs=[pl.BlockSpec((P, R, C), lambda i: (0, 0, 0))],
        out_specs=pl.BlockSpec((R, C), lambda i: (0, 0)),
        out_shape=jax.ShapeDtypeStruct((R, C), F32),
        compiler_params=_cp("arbitrary"),
    )(parts)


def _add_halves(g4, recv, c_idx, *, name):
    _, _, Rh, C = g4.shape
    tr = min(256, Rh)

    def body(c_ref, a_ref, b_ref, o_ref):
        o_ref[...] = (a_ref[...] + b_ref[...].astype(F32)).astype(BF)

    return pl.pallas_call(
        body, name=name,
        grid_spec=pltpu.PrefetchScalarGridSpec(
            num_scalar_prefetch=1, grid=(4, pl.cdiv(Rh, tr)),
            in_specs=[pl.BlockSpec((None, None, tr, C), lambda j, r, c: (j, c[0], r, 0)),
                      pl.BlockSpec((None, tr, C), lambda j, r, c: (j, r, 0))],
            out_specs=pl.BlockSpec((None, tr, C), lambda j, r, c: (j, r, 0))),
        out_shape=jax.ShapeDtypeStruct((4, Rh, C), BF),
        compiler_params=_cp("parallel", "arbitrary"),
    )(c_idx, g4, recv)


def _add_four(g4, from_sibling, from_chips, pos, *, name):
    _, _, Rh, C = g4.shape
    tr = min(256, Rh)

    def body(p_ref, a_ref, s_ref, b_ref, o_ref):
        own = a_ref[...] + s_ref[...].astype(F32)
        o_ref[...] = ((own + b_ref[0].astype(F32)) + b_ref[1].astype(F32)) + b_ref[2].astype(F32)

    return pl.pallas_call(
        body, name=name,
        grid_spec=pltpu.PrefetchScalarGridSpec(
            num_scalar_prefetch=1, grid=(pl.cdiv(Rh, tr),),
            in_specs=[pl.BlockSpec((None, None, tr, C), lambda r, p: (p[0], p[1], r, 0)),
                      pl.BlockSpec((None, tr, C), lambda r, p: (p[0], r, 0)),
                      pl.BlockSpec((3, tr, C), lambda r, p: (0, r, 0))],
            out_specs=pl.BlockSpec((None, tr, C), lambda r, p: (p[1], r, 0))),
        out_shape=jax.ShapeDtypeStruct((2, Rh, C), F32),
        compiler_params=_cp("arbitrary"),
    )(pos, g4, from_sibling, from_chips)


HBM = pl.BlockSpec(memory_space=pltpu.HBM)


def _mesh_pos():
    return lax.axis_index("x"), lax.axis_index("y"), lax.axis_index("c")


def _other_chips(x, y):
    return [(1 - x, y), (x, 1 - y), (1 - x, 1 - y)]


def _allgather_small(xs, *, name):
    m_per, n = xs.shape

    def body(x_ref, out_ref, send_sems, recv_sems, local_sem):
        x, y, c = _mesh_pos()
        me, sibling = (x, y, c), (x, y, 1 - c)
        chips = _other_chips(x, y)

        def rows(px, py, pc):
            return out_ref.at[pl.ds((4 * px + 2 * py + pc) * m_per, m_per), :]

        def copy(k, block, to, src=None):
            return pltpu.make_async_remote_copy(
                src_ref=rows(*block) if src is None else src, dst_ref=rows(*block),
                send_sem=send_sems.at[k], recv_sem=recv_sems.at[k], device_id=to, device_id_type=MESH)

        mine = pltpu.make_async_copy(x_ref, rows(*me), local_sem)
        mine.start()
        first = [copy(0, me, sibling, src=x_ref)]
        first += [copy(1 + j, me, (*chip, c), src=x_ref) for j, chip in enumerate(chips)]
        for cp in first:
            cp.start()
        passed = [copy(4 + j, (*chip, c), sibling) for j, chip in enumerate(chips)]
        for j, chip in enumerate(chips):
            copy(1 + j, (*chip, c), me).wait_recv()
            passed[j].start()
        copy(0, sibling, me).wait_recv()
        for j, chip in enumerate(chips):
            copy(4 + j, (*chip, 1 - c), me).wait_recv()
        for cp in first + passed:
            cp.wait_send()
        mine.wait()

    return pl.pallas_call(
        body, name=name,
        out_shape=jax.ShapeDtypeStruct((N_DEV * m_per, n), xs.dtype),
        in_specs=[pl.BlockSpec(memory_space=pltpu.VMEM)],
        out_specs=pl.BlockSpec(memory_space=pltpu.VMEM),
        scratch_shapes=[pltpu.SemaphoreType.DMA((7,)), pltpu.SemaphoreType.DMA((7,)), pltpu.SemaphoreType.DMA],
    )(xs)


def _chip_slab_copies(s_ref, out_ref, send_sems, recv_sems):
    R = s_ref.shape[0]
    Rh = R // 2
    x, y, c = _mesh_pos()
    me, sibling = (x, y, c), (x, y, 1 - c)
    chips = _other_chips(x, y)

    def half(px, py, pc):
        return out_ref.at[2 * px + py, pl.ds(pc * Rh, Rh), :]

    def copy(k, block, to, src=None):
        return pltpu.make_async_remote_copy(
            src_ref=half(*block) if src is None else src, dst_ref=half(*block),
            send_sem=send_sems.at[k], recv_sem=recv_sems.at[k], device_id=to, device_id_type=MESH)

    first = [copy(j, me, (*chip, c), src=s_ref.at[pl.ds(c * Rh, Rh), :]) for j, chip in enumerate(chips)]
    passed = [copy(3 + j, (*chip, c), sibling) for j, chip in enumerate(chips)]
    landed = [copy(j, (*chip, c), me) for j, chip in enumerate(chips)]
    from_sibling = [copy(3 + j, (*chip, 1 - c), me) for j, chip in enumerate(chips)]
    return first, passed, landed, from_sibling


def _gather_behind(s_ref, out_ref, send_sems, recv_sems, step, nsteps):
    first, passed, landed, from_sibling = _chip_slab_copies(s_ref, out_ref, send_sems, recv_sems)

    @pl.when(step == 0)
    def _():
        for cp in first:
            cp.start()

    @pl.when(step == (3 * nsteps) // 4)
    def _():
        for arrived, onward in zip(landed, passed):
            arrived.wait_recv()
            onward.start()

    def finish():
        @pl.when(step == nsteps - 1)
        def _():
            for cp in from_sibling:
                cp.wait_recv()
            for cp in first + passed:
                cp.wait_send()

    return finish


def _allgather_chip_slabs(slab, *, name):
    R, C = slab.shape

    def body(s_ref, out_ref, send_sems, recv_sems):
        first, passed, landed, from_sibling = _chip_slab_copies(s_ref, out_ref, send_sems, recv_sems)
        for cp in first:
            cp.start()
        for arrived, onward in zip(landed, passed):
            arrived.wait_recv()
            onward.start()
        for cp in from_sibling:
            cp.wait_recv()
        for cp in first + passed:
            cp.wait_send()

    return pl.pallas_call(
        body, name=name,
        out_shape=jax.ShapeDtypeStruct((N_CHIPS, R, C), slab.dtype),
        in_specs=[HBM], out_specs=HBM,
        scratch_shapes=[pltpu.SemaphoreType.DMA((6,)), pltpu.SemaphoreType.DMA((6,))],
    )(slab)


def _swap_halves(mine, *, name):
    def body(g_ref, out_ref, send_sems, recv_sems):
        x, y, c = _mesh_pos()
        copies = [pltpu.make_async_remote_copy(
            src_ref=g_ref.at[j], dst_ref=out_ref.at[j], send_sem=send_sems.at[j], recv_sem=recv_sems.at[j],
            device_id=(x, y, 1 - c), device_id_type=MESH) for j in range(N_CHIPS)]
        for cp in copies:
            cp.start()
        for cp in copies:
            cp.wait()

    return pl.pallas_call(
        body, name=name,
        out_shape=jax.ShapeDtypeStruct(mine.shape, mine.dtype),
        in_specs=[HBM], out_specs=HBM,
        scratch_shapes=[pltpu.SemaphoreType.DMA((N_CHIPS,)), pltpu.SemaphoreType.DMA((N_CHIPS,))],
    )(mine)


def _scatter_copies(p_ref, out_ref, send_sems, recv_sems):
    x, y, c = _mesh_pos()
    return [pltpu.make_async_remote_copy(
        src_ref=p_ref.at[2 * px + py], dst_ref=out_ref.at[j], send_sem=send_sems.at[j], recv_sem=recv_sems.at[j],
        device_id=(px, py, c), device_id_type=MESH) for j, (px, py) in enumerate(_other_chips(x, y))]


def _join_halves(buf, *, name):
    def body(b_ref, out_ref, send_sem, recv_sem):
        x, y, c = _mesh_pos()
        cp = pltpu.make_async_remote_copy(
            src_ref=b_ref.at[c], dst_ref=out_ref.at[c], send_sem=send_sem, recv_sem=recv_sem,
            device_id=(x, y, 1 - c), device_id_type=MESH)
        cp.start()
        cp.wait()

    return pl.pallas_call(
        body, name=name,
        out_shape=jax.ShapeDtypeStruct(buf.shape, buf.dtype),
        in_specs=[HBM], out_specs=HBM, input_output_aliases={0: 0},
        scratch_shapes=[pltpu.SemaphoreType.DMA, pltpu.SemaphoreType.DMA],
    )(buf)


def _pad_rows(a, mult):
    pad = (-a.shape[0]) % mult
    return a if pad == 0 else jnp.pad(a, ((0, pad),) + ((0, 0),) * (a.ndim - 1))


def _local_step(x, target, mod, wts, small, slabs=None, unpacks=None, reduce_early=None, grad_slab=None,
                reduce_late=None):
    S, D = x.shape
    HP = D // LANES
    row = lambda v: v.reshape(1, -1)
    msplit = [[row(mod[i, k * D:(k + 1) * D]) for k in range(6)] for i in range(2)]
    gw, gs = {}, {}
    dmod = [[None] * 6 for _ in range(2)]
    slab, where = grad_slab if grad_slab is not None else (None, {})

    def dw(key, a, b, name):
        nonlocal slab
        if key in where:
            slab = _matmul_tn(a, b, name=name, into=(slab,) + where[key])
        else:
            gw[key] = _matmul_tn(a, b, name=name)

    sh1, sc1, g1, sh2, sc2, g2 = msplit[0]
    n1w0, n2w0 = row(small["norm1_w"][0]), row(small["norm2_w"][0])
    slabs = slabs if slabs is not None else (None, None, None)
    proj0, h1_0, *gathered = _ln_matmul(x, n1w0, sc1, sh1, wts["hg_w_in"], slabs[0], relu2=False, name="hg_in_proj")
    if slabs[0] is not None:
        wts = {**wts, **unpacks[0](gathered[0])}
    gn = small["hg_gn_w"].reshape(1, LANES)
    ypre0, o0, states, *gathered = _hg_fwd(proj0, small["hg_lb"], gn, slabs[1], name="hg_fwd")
    if slabs[1] is not None:
        wts = {**wts, **unpacks[1](gathered[0])}
    x1, ymix0 = _matmul_resid(ypre0, wts["hg_w_out"], x, g1, name="hg_out_proj")
    a0, u0, h2_0, *gathered = _ln_matmul(x1, n2w0, sc2, sh2, wts["mlp_w1_0"], slabs[2], relu2=True, name="mlp0_up")
    if slabs[2] is not None:
        wts = {**wts, **unpacks[2](gathered[0])}
    x2, ymlp0 = _matmul_resid(u0, wts["mlp_w2_0"], x1, g2, name="mlp0_down")

    sh1b, sc1b, g1b, sh2b, sc2b, g2b = msplit[1]
    n1w1, n2w1 = row(small["norm1_w"][1]), row(small["norm2_w"][1])
    proj1, h1_1 = _ln_matmul(x2, n1w1, sc1b, sh1b, wts["fox_w_in"], relu2=False, name="fox_in_proj")
    nheads = 2 * HP
    bf_pad = jnp.pad(small["fox_b_f"].reshape(1, nheads), ((0, 0), (0, LANES - nheads)))
    qw2 = jnp.tile(small["fox_qn_w"].reshape(1, FOX_DH), (1, 2))
    kw2 = jnp.tile(small["fox_kn_w"].reshape(1, FOX_DH), (1, 2))
    fcum = _fox_cumsum(proj1, bf_pad, name="fox_cumsum")
    qa, ka, va, vat = _fox_prep(proj1, fcum, qw2, kw2, name="fox_prep")
    jmin, imax = _fox_skip_bounds(fcum, small["fox_qn_w"], small["fox_kn_w"], nheads)
    ypre1, o1, q2 = _fox_fwd(jmin, qa, ka, vat, proj1, name="fox_fwd")
    x3, ymix1 = _matmul_resid(ypre1, wts["fox_w_out"], x2, g1b, name="fox_out_proj")
    a1, u1, h2_1 = _ln_matmul(x3, n2w1, sc2b, sh2b, wts["mlp_w1_1"], relu2=True, name="mlp1_up")
    x4, ymlp1 = _matmul_resid(u1, wts["mlp_w2_1"], x3, g2b, name="mlp1_down")

    loss, dx4, dfw = _loss_kernel(x4, row(small["final_w"]), target, name="loss")
    gs["final_w"] = dfw.reshape(-1)

    def mlp_bwd(i, dx_out, x_in, h2, a, u, ymlp, n2w, sc2_, g2_):
        dz, dm, dg2 = _gate_matmul_nt(dx_out, g2_, ymlp, wts[f"mlp_w2_{i}"], a, name=f"mlp{i}_down_bwd")
        dw(f"mlp_w2_{i}", u, dm[None], f"mlp{i}_dw2")
        dw(f"mlp_w1_{i}", h2, dz[None], f"mlp{i}_dw1")
        dx_in, dsc, dsh, dnw = _matmul_nt_lnbwd(dz[None], wts[f"mlp_w1_{i}"], x_in, n2w, sc2_, dx_out,
                                                name=f"mlp{i}_up_bwd")
        dmod[i][3], dmod[i][4], dmod[i][5] = dsh, dsc, dg2
        return dx_in, dnw

    dx3, dn2w1 = mlp_bwd(1, dx4, x3, h2_1, a1, u1, ymlp1, n2w1, sc2b, g2b)
    dyp1, dm1, dg1b = _gate_matmul_nt(dx3, g1b, ymix1, wts["fox_w_out"], None, name="fox_out_bwd")
    dw("fox_w_out", ypre1, dm1[None], "fox_dw_out")
    doa = _fox_bwd_prep(dyp1, o1, proj1, q2, name="fox_bwd_prep")
    dqa, dka, dva, colsum = _fox_bwd(imax, q2, ka, va, doa, name="fox_bwd")
    colsum = jnp.pad(colsum[:, 0, :].T, ((0, 0), (0, LANES - nheads)))
    dproj1, dqw, dkw = _fox_bwd_post(dqa, dka, dva, proj1, dyp1, o1, qw2, kw2, name="fox_bwd_post")
    dproj1, dbf = _fox_dfz(colsum, nheads, proj1, bf_pad, dproj1, name="fox_dfz")
    dw("fox_w_in", h1_1, dproj1, "fox_dw_in")
    dx2, dsc, dsh, dn1w1 = _matmul_nt_lnbwd(dproj1, wts["fox_w_in"], x2, n1w1, sc1b, dx3, name="fox_in_bwd")
    dmod[1][0], dmod[1][1], dmod[1][2] = dsh, dsc, dg1b
    gs["fox_qn_w"] = dqw[0, :FOX_DH] + dqw[0, FOX_DH:]
    gs["fox_kn_w"] = dkw[0, :FOX_DH] + dkw[0, FOX_DH:]
    gs["fox_b_f"] = dbf[0, :nheads]

    dx1, dn2w0 = mlp_bwd(0, dx2, x1, h2_0, a0, u0, ymlp0, n2w0, sc2, g2)
    dyp0, dm0, dg1 = _gate_matmul_nt(dx1, g1, ymix0, wts["hg_w_out"], None, name="hg_out_bwd")
    dw("hg_w_out", ypre0, dm0[None], "hg_dw_out")
    part, ctx = reduce_early(gw, slab) if reduce_early is not None else (None, None)
    dproj0, dlb, dgn, *from_chips = _hg_bwd(proj0, small["hg_lb"], gn, o0, states, dyp0, part, name="hg_bwd")
    early = (ctx, from_chips[0]) if reduce_early is not None else None
    dw("hg_w_in", h1_0, dproj0, "hg_dw_in")
    part, ctx = reduce_late(gw) if reduce_late is not None else (None, None)
    dx0, dsc, dsh, dn1w0, *from_chips = _matmul_nt_lnbwd(dproj0, wts["hg_w_in"], x, n1w0, sc1, dx1, part, name="hg_in_bwd")
    late = (ctx, from_chips[0]) if reduce_late is not None else None
    dmod[0][0], dmod[0][1], dmod[0][2] = dsh, dsc, dg1
    gs["hg_lb"] = dlb
    gs["hg_gn_w"] = jnp.sum(dgn, axis=0)

    gs["norm1_w"] = jnp.concatenate([dn1w0, dn1w1], axis=0)
    gs["norm2_w"] = jnp.concatenate([dn2w0, dn2w1], axis=0)
    gs["dmod"] = jnp.stack([jnp.concatenate(dmod[i], axis=1)[0] for i in range(2)])
    return loss, dx0, gw, gs, early, late


def _pack_halves(layout):
    rh = -(-max(sum(a.shape[0] for _, a in half) for half in layout) // 16) * 16
    place, parts = {}, []
    for h, half in enumerate(layout):
        off = 0
        for n, a in half:
            place[n] = (h, off, a.shape[0])
            off += a.shape[0]
        parts.append(jnp.pad(jnp.concatenate([a.astype(BF) for _, a in half], axis=0), ((0, rh - off), (0, 0))))
    return jnp.concatenate(parts, axis=0), place, rh


SMALL_NAMES = ["norm1_w", "norm2_w", "hg_lb", "hg_gn_w", "fox_b_f", "fox_qn_w", "fox_kn_w", "final_w"]


def _pack_small(d, names):
    rows, offs, r0 = [], {}, 0
    for n in names:
        flat = d[n].reshape(-1)
        nr = -(-flat.shape[0] // LANES)
        rows.append(jnp.pad(flat, (0, nr * LANES - flat.shape[0])).reshape(nr, LANES))
        offs[n] = (r0, nr)
        r0 += nr
    return jnp.concatenate(rows, axis=0), offs


def _unpack_small(packed, offs, name, like):
    r0, nr = offs[name]
    return packed[r0:r0 + nr].reshape(-1)[:like.size].reshape(like.shape)


def kernel(x, c, w_mod, b_mod, norm1_w, norm2_w, hg_w_in, hg_w_out, hg_lb, hg_gn_w, fox_w_in, fox_b_f, fox_qn_w, fox_kn_w, fox_w_out, mlp_w1, mlp_w2, final_w, loss_target, m_w_mod, m_b_mod, m_norm1_w, m_norm2_w, m_hg_w_in, m_hg_w_out, m_hg_lb, m_hg_gn_w, m_fox_w_in, m_fox_b_f, m_fox_qn_w, m_fox_kn_w, m_fox_w_out, m_mlp_w1, m_mlp_w2, m_final_w, v_w_mod, v_b_mod, v_norm1_w, v_norm2_w, v_hg_w_in, v_hg_w_out, v_hg_lb, v_hg_gn_w, v_fox_w_in, v_fox_b_f, v_fox_qn_w, v_fox_kn_w, v_fox_w_out, v_mlp_w1, v_mlp_w2, v_final_w):
    S, D = x.shape[1], x.shape[2]
    nheads = D // FOX_DH
    ax, ay, ac = _mesh_pos()
    chip = 2 * ax + ay
    dev = 2 * chip + ac
    xs, tgt = x.reshape(S, D), loss_target.reshape(S, D)

    c_all = _allgather_small(_pad_rows(c.reshape(-1, LANES), 8), name="gather_c")
    c_all = c_all.reshape(N_DEV, -1)[:, :D]
    c16 = _pad_rows(c_all, 16)
    nmod = w_mod.shape[2]
    b_shard = lax.dynamic_slice_in_dim(b_mod, chip * nmod, nmod, axis=1)
    mod_shard = _mod_fwd(c16, w_mod, b_shard[:, None, :], name="mod_fwd")[:, :N_DEV]
    mod_all = _allgather_small(mod_shard.reshape(-1, LANES), name="gather_mod")
    mod_all = mod_all.reshape(N_CHIPS, 2, 2, N_DEV, nmod)[:, 0]
    mod = lax.dynamic_index_in_dim(mod_all, dev, axis=2, keepdims=False)
    mod = mod.transpose(1, 0, 2).reshape(2, N_CHIPS * nmod)

    fox_rows = fox_w_in.shape[2]
    col = lambda g: g.transpose(1, 0, 2).reshape(g.shape[1], -1)
    rowsh = lambda g: g.reshape(-1, g.shape[2])
    own = lambda g, s: lax.dynamic_update_index_in_dim(g, s, chip, 0)

    slab_in = hg_w_in[0].astype(BF)
    wts = {"hg_w_in": col(own(_allgather_chip_slabs(slab_in, name="gather_hg_w_in"), slab_in))}
    fox_flat, fox_cut = fox_w_in[0].reshape(fox_rows, D), fox_rows // 2
    slabs, unpacks = [], []
    for layout_w in ([[("mlp_w1_0", mlp_w1[0])], [("mlp_w2_0", mlp_w2[0])]],
                     [[("mlp_w1_1", mlp_w1[1]), ("hg_w_out", hg_w_out[0])], [("mlp_w2_1", mlp_w2[1]), ("fox_w_out", fox_w_out[0])]],
                     [[("fox_a", fox_flat[:fox_cut])], [("fox_b", fox_flat[fox_cut:])]]):
        slab_w, place_w, rh_w = _pack_halves(layout_w)

        def unpack(gathered, slab_w=slab_w, place_w=place_w, rh_w=rh_w):
            gathered = own(gathered, slab_w)
            out = {}
            for n, (h, off, rows) in place_w.items():
                g = gathered[:, h * rh_w + off:h * rh_w + off + rows, :]
                out[n] = col(g) if n.startswith("mlp_w1") else rowsh(g) if n.startswith(("mlp_w2", "hg_", "fox_w")) else g
            if "fox_a" in out:
                fox_in = col(jnp.concatenate([out.pop("fox_a"), out.pop("fox_b")], axis=1).reshape(N_CHIPS, D, fox_rows))
                out["fox_w_in"] = jnp.pad(fox_in, ((0, 0), (0, 5 * D - fox_in.shape[1])))
            return out

        slabs.append(slab_w)
        unpacks.append(unpack)

    small = {"norm1_w": norm1_w, "norm2_w": norm2_w, "hg_lb": hg_lb, "hg_gn_w": hg_gn_w, "fox_b_f": fox_b_f,
             "fox_qn_w": fox_qn_w, "fox_kn_w": fox_kn_w, "final_w": final_w}

    def uncol(g, n):
        return g.reshape(g.shape[0], N_CHIPS, n).transpose(1, 0, 2)

    pos = jnp.stack([chip, ac])

    def swap_and_add(g4, tag):
        to_sibling = lax.dynamic_index_in_dim(g4, 1 - ac, axis=1, keepdims=False).astype(BF)
        from_sibling = _swap_halves(to_sibling, name=f"rs_swap_{tag}")
        return from_sibling, _add_halves(g4, from_sibling, ac.reshape(1), name=f"rs_add_halves_{tag}")

    def finish(g4, from_sibling, from_chips, tag):
        my_half = _add_four(g4, from_sibling, from_chips, pos, name=f"rs_add_chips_{tag}")
        return _join_halves(my_half, name=f"rs_join_{tag}")

    layout = [[("mlp_w1", 2 * D), ("hg_w_out", D // 4), ("fox_w_out", D // 4)], [("mlp_w2", 2 * D), ("fox_w_in", fox_rows)]]
    place = {}
    for h, half in enumerate(layout):
        off = 0
        for n, rows in half:
            place[n] = (h, off, rows)
            off += rows

    rh = -(-max(sum(rows for _, rows in half) for half in layout) // 16) * 16
    where = {"hg_w_out": ("row",) + place["hg_w_out"][:2], "fox_w_out": ("row",) + place["fox_w_out"][:2]}
    for i in range(2):
        where[f"mlp_w1_{i}"] = ("col", place["mlp_w1"][0], place["mlp_w1"][1] + i * D)
        where[f"mlp_w2_{i}"] = ("row", place["mlp_w2"][0], place["mlp_w2"][1] + i * D)

    def reduce_early(gw, slab):
        gfox = uncol(gw["fox_w_in"][:, :4 * fox_rows], fox_rows).reshape(N_CHIPS, 1, fox_rows, D)
        h, off, _ = place["fox_w_in"]
        slab = lax.dynamic_update_slice(slab, gfox, (0, h, off, 0))
        for h, half in enumerate(layout):
            used = sum(rows for _, rows in half)
            if used < rh:
                slab = lax.dynamic_update_slice(slab, jnp.zeros((N_CHIPS, 1, rh - used, D), F32), (0, h, used, 0))
        from_sibling, part = swap_and_add(slab, "early")
        return part, (slab, from_sibling)

    def reduce_late(gw):
        g4 = uncol(gw["hg_w_in"], D).reshape(N_CHIPS, 2, D // 2, D)
        from_sibling, part = swap_and_add(g4, "late")
        return part, (g4, from_sibling)

    loss_part, grad_x, gw, gs, (early, from_chips_early), (late, from_chips_late) = _local_step(
        xs, tgt, mod, wts, small, slabs, unpacks, reduce_early, (lax.empty((N_CHIPS, 2, rh, D), F32), where), reduce_late)
    gshard = finish(*early, from_chips_early, "early")
    g_hg_w_in = finish(*late, from_chips_late, "late").reshape(D, D)

    names = ["dmod", "loss"] + SMALL_NAMES
    packed, offs = _pack_small({**gs, "loss": loss_part[0, :1]}, names)
    packed = _pad_rows(packed, 8)
    rp = packed.shape[0]
    parts = _allgather_small(packed, name="gather_small").reshape(N_DEV, rp, LANES)
    total = _sum_parts(parts, name="sum_small")
    r0, nr = offs["dmod"]
    dmod_all = parts[:, r0:r0 + nr].reshape(N_DEV, 2, N_CHIPS * nmod)
    dmod_shard = lax.dynamic_slice_in_dim(dmod_all, chip * nmod, nmod, axis=2).transpose(1, 0, 2)
    g_w_mod = _mod_bwd(c16, jnp.pad(dmod_shard, ((0, 0), (0, 16 - N_DEV), (0, 0))), name="mod_bwd")

    loss = _unpack_small(total, offs, "loss", loss_part[0, :1]).reshape(())
    grads = {"w_mod": g_w_mod, "b_mod": _unpack_small(total, offs, "dmod", b_mod)}
    for n in SMALL_NAMES:
        grads[n] = _unpack_small(total, offs, n, small[n])

    given = dict(w_mod=(w_mod, m_w_mod, v_w_mod), b_mod=(b_mod, m_b_mod, v_b_mod), norm1_w=(norm1_w, m_norm1_w, v_norm1_w),
                 norm2_w=(norm2_w, m_norm2_w, v_norm2_w), hg_w_in=(hg_w_in, m_hg_w_in, v_hg_w_in),
                 hg_w_out=(hg_w_out, m_hg_w_out, v_hg_w_out), hg_lb=(hg_lb, m_hg_lb, v_hg_lb),
                 hg_gn_w=(hg_gn_w, m_hg_gn_w, v_hg_gn_w), fox_w_in=(fox_w_in, m_fox_w_in, v_fox_w_in),
                 fox_b_f=(fox_b_f, m_fox_b_f, v_fox_b_f), fox_qn_w=(fox_qn_w, m_fox_qn_w, v_fox_qn_w),
                 fox_kn_w=(fox_kn_w, m_fox_kn_w, v_fox_kn_w), fox_w_out=(fox_w_out, m_fox_w_out, v_fox_w_out),
                 mlp_w1=(mlp_w1, m_mlp_w1, v_mlp_w1), mlp_w2=(mlp_w2, m_mlp_w2, v_mlp_w2), final_w=(final_w, m_final_w, v_final_w))
    upd = {}

    for n, (h, off, rows) in place.items():
        w, m, v = given[n]
        flat = lambda a: a.reshape(rows, D)
        d, mn, vn = _adamw(flat(w), gshard, flat(m), flat(v), g_at=(h, off), name=f"adamw_{n}")
        grads[n] = gshard[h, off:off + rows].reshape(w.shape)
        upd[n] = tuple(a.reshape(w.shape) for a in (d, mn, vn))

    w, m, v = given["hg_w_in"]
    grads["hg_w_in"] = g_hg_w_in.reshape(w.shape)
    upd["hg_w_in"] = tuple(a.reshape(w.shape) for a in _adamw(w[0], g_hg_w_in, m[0], v[0], name="adamw_hg_w_in"))

    w, m, v = given["w_mod"]
    flat = lambda a: a.reshape(-1, nmod)
    upd["w_mod"] = tuple(a.reshape(w.shape) for a in _adamw(flat(w), flat(g_w_mod), flat(m), flat(v), name="adamw_w_mod"))

    snames = ["b_mod"] + SMALL_NAMES
    pw, soffs = _pack_small({n: given[n][0] for n in snames}, snames)
    pm, _ = _pack_small({n: given[n][1] for n in snames}, snames)
    pv, _ = _pack_small({n: given[n][2] for n in snames}, snames)
    pg, _ = _pack_small({n: grads[n] for n in snames}, snames)
    pw, pm, pv, pg = (_pad_rows(a, 8) for a in (pw, pm, pv, pg))
    sd, smn, svn = _adamw(pw, pg, pm, pv, name="adamw_small")
    for n in snames:
        like = given[n][0]
        upd[n] = tuple(_unpack_small(a, soffs, n, like) for a in (sd, smn, svn))

    order = ["w_mod", "b_mod", "norm1_w", "norm2_w", "hg_w_in", "hg_w_out", "hg_lb", "hg_gn_w", "fox_w_in", "fox_b_f",
             "fox_qn_w", "fox_kn_w", "fox_w_out", "mlp_w1", "mlp_w2", "final_w"]
    return (loss, grad_x.reshape(x.shape), *[grads[n] for n in order], *[upd[n][0] for n in order],
            *[upd[n][1] for n in order], *[upd[n][2] for n in order])
```
